```python
import math
import jax, jax.numpy as jnp
from jax import lax
import numpy as np

D_MODEL = 1024
BATCH = 16
SEQ = 2048
DEPTH = 2

SSD_HEAD_DIM = 64
SSD_HEADS = D_MODEL // SSD_HEAD_DIM
D_SSD = SSD_HEADS * SSD_HEAD_DIM
SSD_GROUPS = 2
SSD_HEADS_PER_GROUP = SSD_HEADS // SSD_GROUPS
D_STATE = 128
SSD_CONV = 4
SSD_CHUNK = 128
CONV_DIM = D_SSD + 2 * SSD_GROUPS * D_STATE
GM_HEAD_DIM = 128
GM_HEADS = D_MODEL // GM_HEAD_DIM
D_GM = GM_HEADS * GM_HEAD_DIM
GM_CHUNK = 128
D_MIX = D_SSD + D_GM
N_IN = D_SSD + CONV_DIM + SSD_HEADS + 2 * D_GM
D_FF = ((8 * D_MODEL // 3 + 255) // 256) * 256
FF_CONV = 3
N_MOD = 6
EPS = 1e-6

kernel_name = "hybrid_ssd_gmlp_convffn_adaln"


def rmsnorm(x, g):
    xf = x.astype(jnp.float32)
    y = xf * lax.rsqrt(jnp.mean(xf * xf, axis=-1, keepdims=True) + EPS)
    return (y * g.astype(jnp.float32)).astype(x.dtype)


def causal_dwconv(x, w, b):
    k, ch = w.shape
    y = lax.conv_general_dilated(x, w[:, None, :].astype(x.dtype), window_strides=(1,), padding=[(k - 1, 0)], dimension_numbers=("NWC", "WIO", "NWC"), feature_group_count=ch)
    return y + b.astype(x.dtype)


def ssd_chunked(xh, dt, a, bm, cm):
    bsz, s = xh.shape[:2]
    nc = s // SSD_CHUNK
    chunk = lambda t: t.reshape((bsz, nc, SSD_CHUNK) + t.shape[2:])
    xc = chunk(xh * dt[..., None])
    bc, cc = chunk(bm), chunk(cm)
    a_cs = jnp.cumsum(jnp.moveaxis(chunk(dt * a), 2, -1), axis=-1)
    causal = jnp.tril(jnp.ones((SSD_CHUNK, SSD_CHUNK), dtype=bool))
    seg = jnp.where(causal, a_cs[..., :, None] - a_cs[..., None, :], -jnp.inf)
    cb = jnp.einsum("bclgn,bcsgn->bcgls", cc, bc)
    att = cb[:, :, :, None] * jnp.exp(seg)
    y_diag = jnp.einsum("bcgrls,bcsgrp->bclgrp", att, xc)
    decay_st = jnp.moveaxis(jnp.exp(a_cs[..., -1:] - a_cs), -1, 2)
    states = jnp.einsum("bclgn,bclgrp->bcgrpn", bc, xc * decay_st[..., None])
    chunk_decay = jnp.exp(a_cs[..., -1])

    def step(h, inp):
        st, dec = inp
        return h * dec[..., None, None] + st, h

    h0 = jnp.zeros_like(states[:, 0])
    _, prev = lax.scan(step, h0, (jnp.moveaxis(states, 1, 0), jnp.moveaxis(chunk_decay, 1, 0)))
    prev = jnp.moveaxis(prev, 0, 1)
    decay_out = jnp.moveaxis(jnp.exp(a_cs), -1, 2)
    y_off = jnp.einsum("bclgn,bcgrpn->bclgrp", cc, prev) * decay_out[..., None]
    return (y_diag + y_off).reshape(xh.shape)


def hybrid_mixer(h, w_in, conv_w, conv_b, dt_bias, a_log, d_skip, ssd_norm_g, v_norm_g, ws, bs, gm_out_g, w_out):
    bsz, s, _ = h.shape
    f32 = jnp.float32
    proj = h @ w_in
    z, xbc, dt_raw, gm = jnp.split(proj, [D_SSD, D_SSD + CONV_DIM, D_SSD + CONV_DIM + SSD_HEADS], axis=-1)
    xbc = jax.nn.silu(causal_dwconv(xbc, conv_w, conv_b))
    xs, bm, cm = jnp.split(xbc, [D_SSD, D_SSD + SSD_GROUPS * D_STATE], axis=-1)
    xs = xs.astype(f32).reshape(bsz, s, SSD_GROUPS, SSD_HEADS_PER_GROUP, SSD_HEAD_DIM)
    bm = bm.astype(f32).reshape(bsz, s, SSD_GROUPS, D_STATE)
    cm = cm.astype(f32).reshape(bsz, s, SSD_GROUPS, D_STATE)
    dt = jax.nn.softplus(dt_raw.astype(f32) + dt_bias.astype(f32)).reshape(bsz, s, SSD_GROUPS, SSD_HEADS_PER_GROUP)
    a = -jnp.exp(a_log.astype(f32)).reshape(SSD_GROUPS, SSD_HEADS_PER_GROUP)
    y = ssd_chunked(xs, dt, a, bm, cm) + d_skip.astype(f32).reshape(SSD_GROUPS, SSD_HEADS_PER_GROUP)[..., None] * xs
    gw = SSD_HEADS_PER_GROUP * SSD_HEAD_DIM
    y = y.reshape(bsz, s, SSD_GROUPS, gw) * jax.nn.silu(z.astype(f32)).reshape(bsz, s, SSD_GROUPS, gw)
    y = rmsnorm(y, ssd_norm_g.reshape(SSD_GROUPS, gw)).reshape(bsz, s, D_SSD).astype(h.dtype)
    u, v = jnp.split(jax.nn.gelu(gm, approximate=False), 2, axis=-1)
    v = rmsnorm(v, v_norm_g).reshape(bsz, s // GM_CHUNK, GM_CHUNK, GM_HEADS, GM_HEAD_DIM)
    ws_c = jnp.where(jnp.tril(jnp.ones((GM_CHUNK, GM_CHUNK), dtype=bool)), ws, jnp.zeros_like(ws))
    sv = jnp.einsum("hts,bcshd->bcthd", ws_c, v) + bs.T[:, :, None]
    g_out = rmsnorm(u * sv.reshape(bsz, s, D_GM), gm_out_g)
    return jnp.concatenate([y, g_out], axis=-1) @ w_out


def conv_gated_ffn(h, w_up, conv_w, conv_b, w_down):
    gate, val = jnp.split(h @ w_up, 2, axis=-1)
    gate = causal_dwconv(gate, conv_w, conv_b)
    return (jax.nn.silu(gate) * val) @ w_down


def _fwd_setup_inputs(seed: int = 0) -> dict:
    key = jax.random.key(seed)
    ks = jax.random.split(key, 24)
    f32 = jnp.float32
    L = DEPTH
    nrm = lambda k, shape, sc: jax.random.normal(k, shape, f32) * sc
    gain = lambda k, shape: 1.0 + 0.1 * jax.random.normal(k, shape, f32)
    dt0 = jnp.exp(jax.random.uniform(ks[9], (L, SSD_HEADS), f32, math.log(1e-3), math.log(1e-1)))
    return {
        "x": nrm(ks[0], (BATCH, SEQ, D_MODEL), 1.0),
        "c": nrm(ks[1], (BATCH, D_MODEL), 1.0),
        "ada_w": nrm(ks[2], (L, D_MODEL, N_MOD * D_MODEL), 0.5 * D_MODEL ** -0.5),
        "ada_b": nrm(ks[3], (L, N_MOD * D_MODEL), 0.01),
        "norm1_g": gain(ks[4], (L, D_MODEL)),
        "norm2_g": gain(ks[5], (L, D_MODEL)),
        "w_in": nrm(ks[6], (L, D_MODEL, N_IN), D_MODEL ** -0.5),
        "ssd_conv_w": nrm(ks[7], (L, SSD_CONV, CONV_DIM), SSD_CONV ** -0.5),
        "ssd_conv_b": nrm(ks[8], (L, CONV_DIM), 0.01),
        "ssd_dt_bias": dt0 + jnp.log(-jnp.expm1(-dt0)),
        "ssd_a_log": jnp.log(jax.random.uniform(ks[10], (L, SSD_HEADS), f32, 1.0, 16.0)),
        "ssd_d": gain(ks[11], (L, SSD_HEADS)),
        "ssd_norm_g": gain(ks[12], (L, D_SSD)),
        "gm_vnorm_g": gain(ks[13], (L, D_GM)),
        "gm_ws": nrm(ks[14], (L, GM_HEADS, GM_CHUNK, GM_CHUNK), GM_CHUNK ** -0.5),
        "gm_bs": gain(ks[15], (L, GM_HEADS, GM_CHUNK)),
        "gm_out_g": gain(ks[16], (L, D_GM)),
        "w_out": nrm(ks[17], (L, D_MIX, D_MODEL), D_MIX ** -0.5),
        "ff_up": nrm(ks[18], (L, D_MODEL, 2 * D_FF), D_MODEL ** -0.5),
        "ff_conv_w": nrm(ks[19], (L, FF_CONV, D_FF), FF_CONV ** -0.5),
        "ff_conv_b": nrm(ks[20], (L, D_FF), 0.01),
        "ff_down": nrm(ks[21], (L, D_FF, D_MODEL), D_FF ** -0.5),
        "final_g": gain(ks[22], (D_MODEL,)),
    }


def _fwd_reference(x, c, ada_w, ada_b, norm1_g, norm2_g, w_in, ssd_conv_w, ssd_conv_b, ssd_dt_bias, ssd_a_log, ssd_d, ssd_norm_g, gm_vnorm_g, gm_ws, gm_bs, gm_out_g, w_out, ff_up, ff_conv_w, ff_conv_b, ff_down, final_g):
    c_act = jax.nn.silu(c)
    for l in range(DEPTH):
        mod = (c_act @ ada_w[l] + ada_b[l])[:, None, :]
        sh1, sc1, g1, sh2, sc2, g2 = jnp.split(mod, N_MOD, axis=-1)
        h = rmsnorm(x, norm1_g[l]) * (1 + sc1) + sh1
        x = x + g1 * hybrid_mixer(h, w_in[l], ssd_conv_w[l], ssd_conv_b[l], ssd_dt_bias[l], ssd_a_log[l], ssd_d[l], ssd_norm_g[l], gm_vnorm_g[l], gm_ws[l], gm_bs[l], gm_out_g[l], w_out[l])
        h = rmsnorm(x, norm2_g[l]) * (1 + sc2) + sh2
        x = x + g2 * conv_gated_ffn(h, ff_up[l], ff_conv_w[l], ff_conv_b[l], ff_down[l])
    return rmsnorm(x, final_g)


import jax as _jax
import jax.numpy as _jnp

TWIN_FORMAT = 'train_step'
FWD_PARAMS = ['x', 'c', 'ada_w', 'ada_b', 'norm1_g', 'norm2_g', 'w_in', 'ssd_conv_w', 'ssd_conv_b', 'ssd_dt_bias', 'ssd_a_log', 'ssd_d', 'ssd_norm_g', 'gm_vnorm_g', 'gm_ws', 'gm_bs', 'gm_out_g', 'w_out', 'ff_up', 'ff_conv_w', 'ff_conv_b', 'ff_down', 'final_g']
TWIN_WEIGHTS = ['ada_w', 'ada_b', 'norm1_g', 'norm2_g', 'w_in', 'ssd_conv_w', 'ssd_conv_b', 'ssd_dt_bias', 'ssd_a_log', 'ssd_d', 'ssd_norm_g', 'gm_vnorm_g', 'gm_ws', 'gm_bs', 'gm_out_g', 'w_out', 'ff_up', 'ff_conv_w', 'ff_conv_b', 'ff_down', 'final_g']
TWIN_DIFF_INPUT = 'x'
TWIN_INPUTS = ['x', 'c', 'ada_w', 'ada_b', 'norm1_g', 'norm2_g', 'w_in', 'ssd_conv_w', 'ssd_conv_b', 'ssd_dt_bias', 'ssd_a_log', 'ssd_d', 'ssd_norm_g', 'gm_vnorm_g', 'gm_ws', 'gm_bs', 'gm_out_g', 'w_out', 'ff_up', 'ff_conv_w', 'ff_conv_b', 'ff_down', 'final_g', 'loss_target', 'm_ada_w', 'm_ada_b', 'm_norm1_g', 'm_norm2_g', 'm_w_in', 'm_ssd_conv_w', 'm_ssd_conv_b', 'm_ssd_dt_bias', 'm_ssd_a_log', 'm_ssd_d', 'm_ssd_norm_g', 'm_gm_vnorm_g', 'm_gm_ws', 'm_gm_bs', 'm_gm_out_g', 'm_w_out', 'm_ff_up', 'm_ff_conv_w', 'm_ff_conv_b', 'm_ff_down', 'm_final_g', 'v_ada_w', 'v_ada_b', 'v_norm1_g', 'v_norm2_g', 'v_w_in', 'v_ssd_conv_w', 'v_ssd_conv_b', 'v_ssd_dt_bias', 'v_ssd_a_log', 'v_ssd_d', 'v_ssd_norm_g', 'v_gm_vnorm_g', 'v_gm_ws', 'v_gm_bs', 'v_gm_out_g', 'v_w_out', 'v_ff_up', 'v_ff_conv_w', 'v_ff_conv_b', 'v_ff_down', 'v_final_g']
TWIN_OUTPUTS = ['loss', 'grad_x', 'grad_ada_w', 'grad_ada_b', 'grad_norm1_g', 'grad_norm2_g', 'grad_w_in', 'grad_ssd_conv_w', 'grad_ssd_conv_b', 'grad_ssd_dt_bias', 'grad_ssd_a_log', 'grad_ssd_d', 'grad_ssd_norm_g', 'grad_gm_vnorm_g', 'grad_gm_ws', 'grad_gm_bs', 'grad_gm_out_g', 'grad_w_out', 'grad_ff_up', 'grad_ff_conv_w', 'grad_ff_conv_b', 'grad_ff_down', 'grad_final_g', 'delta_ada_w', 'delta_ada_b', 'delta_norm1_g', 'delta_norm2_g', 'delta_w_in', 'delta_ssd_conv_w', 'delta_ssd_conv_b', 'delta_ssd_dt_bias', 'delta_ssd_a_log', 'delta_ssd_d', 'delta_ssd_norm_g', 'delta_gm_vnorm_g', 'delta_gm_ws', 'delta_gm_bs', 'delta_gm_out_g', 'delta_w_out', 'delta_ff_up', 'delta_ff_conv_w', 'delta_ff_conv_b', 'delta_ff_down', 'delta_final_g', 'new_m_ada_w', 'new_m_ada_b', 'new_m_norm1_g', 'new_m_norm2_g', 'new_m_w_in', 'new_m_ssd_conv_w', 'new_m_ssd_conv_b', 'new_m_ssd_dt_bias', 'new_m_ssd_a_log', 'new_m_ssd_d', 'new_m_ssd_norm_g', 'new_m_gm_vnorm_g', 'new_m_gm_ws', 'new_m_gm_bs', 'new_m_gm_out_g', 'new_m_w_out', 'new_m_ff_up', 'new_m_ff_conv_w', 'new_m_ff_conv_b', 'new_m_ff_down', 'new_m_final_g', 'new_v_ada_w', 'new_v_ada_b', 'new_v_norm1_g', 'new_v_norm2_g', 'new_v_w_in', 'new_v_ssd_conv_w', 'new_v_ssd_conv_b', 'new_v_ssd_dt_bias', 'new_v_ssd_a_log', 'new_v_ssd_d', 'new_v_ssd_norm_g', 'new_v_gm_vnorm_g', 'new_v_gm_ws', 'new_v_gm_bs', 'new_v_gm_out_g', 'new_v_w_out', 'new_v_ff_up', 'new_v_ff_conv_w', 'new_v_ff_conv_b', 'new_v_ff_down', 'new_v_final_g']
TWIN_LEAF_KINDS = {'loss': 'loss', 'grad_x': 'grad_x', 'grad_ada_w': 'grad_w', 'grad_ada_b': 'grad_w', 'grad_norm1_g': 'grad_w', 'grad_norm2_g': 'grad_w', 'grad_w_in': 'grad_w', 'grad_ssd_conv_w': 'grad_w', 'grad_ssd_conv_b': 'grad_w', 'grad_ssd_dt_bias': 'grad_w', 'grad_ssd_a_log': 'grad_w', 'grad_ssd_d': 'grad_w', 'grad_ssd_norm_g': 'grad_w', 'grad_gm_vnorm_g': 'grad_w', 'grad_gm_ws': 'grad_w', 'grad_gm_bs': 'grad_w', 'grad_gm_out_g': 'grad_w', 'grad_w_out': 'grad_w', 'grad_ff_up': 'grad_w', 'grad_ff_conv_w': 'grad_w', 'grad_ff_conv_b': 'grad_w', 'grad_ff_down': 'grad_w', 'grad_final_g': 'grad_w', 'delta_ada_w': 'delta_w', 'delta_ada_b': 'delta_w', 'delta_norm1_g': 'delta_w', 'delta_norm2_g': 'delta_w', 'delta_w_in': 'delta_w', 'delta_ssd_conv_w': 'delta_w', 'delta_ssd_conv_b': 'delta_w', 'delta_ssd_dt_bias': 'delta_w', 'delta_ssd_a_log': 'delta_w', 'delta_ssd_d': 'delta_w', 'delta_ssd_norm_g': 'delta_w', 'delta_gm_vnorm_g': 'delta_w', 'delta_gm_ws': 'delta_w', 'delta_gm_bs': 'delta_w', 'delta_gm_out_g': 'delta_w', 'delta_w_out': 'delta_w', 'delta_ff_up': 'delta_w', 'delta_ff_conv_w': 'delta_w', 'delta_ff_conv_b': 'delta_w', 'delta_ff_down': 'delta_w', 'delta_final_g': 'delta_w', 'new_m_ada_w': 'new_m', 'new_m_ada_b': 'new_m', 'new_m_norm1_g': 'new_m', 'new_m_norm2_g': 'new_m', 'new_m_w_in': 'new_m', 'new_m_ssd_conv_w': 'new_m', 'new_m_ssd_conv_b': 'new_m', 'new_m_ssd_dt_bias': 'new_m', 'new_m_ssd_a_log': 'new_m', 'new_m_ssd_d': 'new_m', 'new_m_ssd_norm_g': 'new_m', 'new_m_gm_vnorm_g': 'new_m', 'new_m_gm_ws': 'new_m', 'new_m_gm_bs': 'new_m', 'new_m_gm_out_g': 'new_m', 'new_m_w_out': 'new_m', 'new_m_ff_up': 'new_m', 'new_m_ff_conv_w': 'new_m', 'new_m_ff_conv_b': 'new_m', 'new_m_ff_down': 'new_m', 'new_m_final_g': 'new_m', 'new_v_ada_w': 'new_v', 'new_v_ada_b': 'new_v', 'new_v_norm1_g': 'new_v', 'new_v_norm2_g': 'new_v', 'new_v_w_in': 'new_v', 'new_v_ssd_conv_w': 'new_v', 'new_v_ssd_conv_b': 'new_v', 'new_v_ssd_dt_bias': 'new_v', 'new_v_ssd_a_log': 'new_v', 'new_v_ssd_d': 'new_v', 'new_v_ssd_norm_g': 'new_v', 'new_v_gm_vnorm_g': 'new_v', 'new_v_gm_ws': 'new_v', 'new_v_gm_bs': 'new_v', 'new_v_gm_out_g': 'new_v', 'new_v_w_out': 'new_v', 'new_v_ff_up': 'new_v', 'new_v_ff_conv_w': 'new_v', 'new_v_ff_conv_b': 'new_v', 'new_v_ff_down': 'new_v', 'new_v_final_g': 'new_v'}


def _forward(args):
    return _fwd_reference(*[args[k] for k in FWD_PARAMS])


def _output_shape():
    out = _jax.eval_shape(lambda: _forward(_fwd_setup_inputs(0)))
    return out.shape, out.dtype

N_MICROBATCH = 1
ADAM_LR = 0.001
ADAM_B1 = 0.9
ADAM_B2 = 0.999
ADAM_EPS = 1e-08
ADAM_WD = 0.01
ADAM_STEP = 10
PER_EXAMPLE_BATCH_AXIS = {'x': 0, 'c': 0, 'loss_target': 0}
SHARED_INPUTS = []
_WEIGHT_DTYPES = {'ada_w': _jnp.float32, 'ada_b': _jnp.float32, 'norm1_g': _jnp.float32, 'norm2_g': _jnp.float32, 'w_in': _jnp.float32, 'ssd_conv_w': _jnp.float32, 'ssd_conv_b': _jnp.float32, 'ssd_dt_bias': _jnp.float32, 'ssd_a_log': _jnp.float32, 'ssd_d': _jnp.float32, 'ssd_norm_g': _jnp.float32, 'gm_vnorm_g': _jnp.float32, 'gm_ws': _jnp.float32, 'gm_bs': _jnp.float32, 'gm_out_g': _jnp.float32, 'w_out': _jnp.float32, 'ff_up': _jnp.float32, 'ff_conv_w': _jnp.float32, 'ff_conv_b': _jnp.float32, 'ff_down': _jnp.float32, 'final_g': _jnp.float32}
MOMENT_SCALE = {'ada_w': 1.736470e-01, 'ada_b': 3.374976e-01, 'norm1_g': 7.128166e-02, 'norm2_g': 5.348369e-02, 'w_in': 3.609665e-02, 'ssd_conv_w': 3.623535e-02, 'ssd_conv_b': 5.619306e-02, 'ssd_dt_bias': 1.482710e-01, 'ssd_a_log': 1.584078e-01, 'ssd_d': 1.348463e-01, 'ssd_norm_g': 4.463898e-02, 'gm_vnorm_g': 2.245909e-02, 'gm_ws': 2.181186e-02, 'gm_bs': 3.088831e-02, 'gm_out_g': 5.340174e-02, 'w_out': 7.019945e-02, 'ff_up': 2.402546e-02, 'ff_conv_w': 2.470573e-02, 'ff_conv_b': 2.294735e-02, 'ff_down': 3.989059e-02, 'final_g': 3.230051e+01}


def _to_microbatches(a, axis):
    t = _jnp.moveaxis(a, axis, 0)
    t = t.reshape((N_MICROBATCH, t.shape[0] // N_MICROBATCH) + t.shape[1:])
    return _jnp.moveaxis(t, 1, axis + 1)


def setup_inputs(seed: int = 0) -> dict:
    inp = _fwd_setup_inputs(seed)
    key = _jax.random.fold_in(_jax.random.key(seed), 7919)
    shape, _ = _output_shape()
    out = dict(inp)
    out["loss_target"] = _jax.random.normal(_jax.random.fold_in(key, 0), shape, _jnp.float32)
    for i, name in enumerate(TWIN_WEIGHTS):
        w = inp[name].astype(_jnp.float32)
        if MOMENT_SCALE is None:
            s = _jnp.sqrt(_jnp.mean(_jnp.square(w)) + 1e-30)
        else:
            s = MOMENT_SCALE[name]
        km, kv = _jax.random.split(_jax.random.fold_in(key, i + 1))
        out[name] = w
        out["m_" + name] = s * _jax.random.normal(km, w.shape, _jnp.float32)
        out["v_" + name] = (s * s) * _jax.random.uniform(kv, w.shape, _jnp.float32, 0.5, 1.5)
    if N_MICROBATCH > 1:
        for name, axis in PER_EXAMPLE_BATCH_AXIS.items():
            out[name] = _to_microbatches(out[name], axis)
    return {'x': out['x'], 'c': out['c'], 'ada_w': out['ada_w'], 'ada_b': out['ada_b'], 'norm1_g': out['norm1_g'], 'norm2_g': out['norm2_g'], 'w_in': out['w_in'], 'ssd_conv_w': out['ssd_conv_w'], 'ssd_conv_b': out['ssd_conv_b'], 'ssd_dt_bias': out['ssd_dt_bias'], 'ssd_a_log': out['ssd_a_log'], 'ssd_d': out['ssd_d'], 'ssd_norm_g': out['ssd_norm_g'], 'gm_vnorm_g': out['gm_vnorm_g'], 'gm_ws': out['gm_ws'], 'gm_bs': out['gm_bs'], 'gm_out_g': out['gm_out_g'], 'w_out': out['w_out'], 'ff_up': out['ff_up'], 'ff_conv_w': out['ff_conv_w'], 'ff_conv_b': out['ff_conv_b'], 'ff_down': out['ff_down'], 'final_g': out['final_g'], 'loss_target': out['loss_target'], 'm_ada_w': out['m_ada_w'], 'm_ada_b': out['m_ada_b'], 'm_norm1_g': out['m_norm1_g'], 'm_norm2_g': out['m_norm2_g'], 'm_w_in': out['m_w_in'], 'm_ssd_conv_w': out['m_ssd_conv_w'], 'm_ssd_conv_b': out['m_ssd_conv_b'], 'm_ssd_dt_bias': out['m_ssd_dt_bias'], 'm_ssd_a_log': out['m_ssd_a_log'], 'm_ssd_d': out['m_ssd_d'], 'm_ssd_norm_g': out['m_ssd_norm_g'], 'm_gm_vnorm_g': out['m_gm_vnorm_g'], 'm_gm_ws': out['m_gm_ws'], 'm_gm_bs': out['m_gm_bs'], 'm_gm_out_g': out['m_gm_out_g'], 'm_w_out': out['m_w_out'], 'm_ff_up': out['m_ff_up'], 'm_ff_conv_w': out['m_ff_conv_w'], 'm_ff_conv_b': out['m_ff_conv_b'], 'm_ff_down': out['m_ff_down'], 'm_final_g': out['m_final_g'], 'v_ada_w': out['v_ada_w'], 'v_ada_b': out['v_ada_b'], 'v_norm1_g': out['v_norm1_g'], 'v_norm2_g': out['v_norm2_g'], 'v_w_in': out['v_w_in'], 'v_ssd_conv_w': out['v_ssd_conv_w'], 'v_ssd_conv_b': out['v_ssd_conv_b'], 'v_ssd_dt_bias': out['v_ssd_dt_bias'], 'v_ssd_a_log': out['v_ssd_a_log'], 'v_ssd_d': out['v_ssd_d'], 'v_ssd_norm_g': out['v_ssd_norm_g'], 'v_gm_vnorm_g': out['v_gm_vnorm_g'], 'v_gm_ws': out['v_gm_ws'], 'v_gm_bs': out['v_gm_bs'], 'v_gm_out_g': out['v_gm_out_g'], 'v_w_out': out['v_w_out'], 'v_ff_up': out['v_ff_up'], 'v_ff_conv_w': out['v_ff_conv_w'], 'v_ff_conv_b': out['v_ff_conv_b'], 'v_ff_down': out['v_ff_down'], 'v_final_g': out['v_final_g']}


def _loss(weights, diff, rest, loss_target):
    with _jax.named_scope("forward"):
        args = {**rest, TWIN_DIFF_INPUT: diff, **{k: w.astype(_WEIGHT_DTYPES[k]) for k, w in weights.items()}}
        y = _forward(args)
    with _jax.named_scope("loss_head"):
        err = _jnp.square(y.astype(_jnp.float32) - loss_target)
        return 0.5 * _jnp.sum(_jnp.mean(err, axis=-1)) if err.ndim else 0.5 * err


def _adamw(w, g, m, v):
    m = ADAM_B1 * m + (1.0 - ADAM_B1) * g
    v = ADAM_B2 * v + (1.0 - ADAM_B2) * _jnp.square(g)
    m_hat = m / (1.0 - ADAM_B1 ** ADAM_STEP)
    v_hat = v / (1.0 - ADAM_B2 ** ADAM_STEP)
    delta = -ADAM_LR * (m_hat / (_jnp.sqrt(v_hat) + ADAM_EPS) + ADAM_WD * w)
    return delta, m, v


def reference(x, c, ada_w, ada_b, norm1_g, norm2_g, w_in, ssd_conv_w, ssd_conv_b, ssd_dt_bias, ssd_a_log, ssd_d, ssd_norm_g, gm_vnorm_g, gm_ws, gm_bs, gm_out_g, w_out, ff_up, ff_conv_w, ff_conv_b, ff_down, final_g, loss_target, m_ada_w, m_ada_b, m_norm1_g, m_norm2_g, m_w_in, m_ssd_conv_w, m_ssd_conv_b, m_ssd_dt_bias, m_ssd_a_log, m_ssd_d, m_ssd_norm_g, m_gm_vnorm_g, m_gm_ws, m_gm_bs, m_gm_out_g, m_w_out, m_ff_up, m_ff_conv_w, m_ff_conv_b, m_ff_down, m_final_g, v_ada_w, v_ada_b, v_norm1_g, v_norm2_g, v_w_in, v_ssd_conv_w, v_ssd_conv_b, v_ssd_dt_bias, v_ssd_a_log, v_ssd_d, v_ssd_norm_g, v_gm_vnorm_g, v_gm_ws, v_gm_bs, v_gm_out_g, v_w_out, v_ff_up, v_ff_conv_w, v_ff_conv_b, v_ff_down, v_final_g):
    given = dict(x=x, c=c, ada_w=ada_w, ada_b=ada_b, norm1_g=norm1_g, norm2_g=norm2_g, w_in=w_in, ssd_conv_w=ssd_conv_w, ssd_conv_b=ssd_conv_b, ssd_dt_bias=ssd_dt_bias, ssd_a_log=ssd_a_log, ssd_d=ssd_d, ssd_norm_g=ssd_norm_g, gm_vnorm_g=gm_vnorm_g, gm_ws=gm_ws, gm_bs=gm_bs, gm_out_g=gm_out_g, w_out=w_out, ff_up=ff_up, ff_conv_w=ff_conv_w, ff_conv_b=ff_conv_b, ff_down=ff_down, final_g=final_g, loss_target=loss_target, m_ada_w=m_ada_w, m_ada_b=m_ada_b, m_norm1_g=m_norm1_g, m_norm2_g=m_norm2_g, m_w_in=m_w_in, m_ssd_conv_w=m_ssd_conv_w, m_ssd_conv_b=m_ssd_conv_b, m_ssd_dt_bias=m_ssd_dt_bias, m_ssd_a_log=m_ssd_a_log, m_ssd_d=m_ssd_d, m_ssd_norm_g=m_ssd_norm_g, m_gm_vnorm_g=m_gm_vnorm_g, m_gm_ws=m_gm_ws, m_gm_bs=m_gm_bs, m_gm_out_g=m_gm_out_g, m_w_out=m_w_out, m_ff_up=m_ff_up, m_ff_conv_w=m_ff_conv_w, m_ff_conv_b=m_ff_conv_b, m_ff_down=m_ff_down, m_final_g=m_final_g, v_ada_w=v_ada_w, v_ada_b=v_ada_b, v_norm1_g=v_norm1_g, v_norm2_g=v_norm2_g, v_w_in=v_w_in, v_ssd_conv_w=v_ssd_conv_w, v_ssd_conv_b=v_ssd_conv_b, v_ssd_dt_bias=v_ssd_dt_bias, v_ssd_a_log=v_ssd_a_log, v_ssd_d=v_ssd_d, v_ssd_norm_g=v_ssd_norm_g, v_gm_vnorm_g=v_gm_vnorm_g, v_gm_ws=v_gm_ws, v_gm_bs=v_gm_bs, v_gm_out_g=v_gm_out_g, v_w_out=v_w_out, v_ff_up=v_ff_up, v_ff_conv_w=v_ff_conv_w, v_ff_conv_b=v_ff_conv_b, v_ff_down=v_ff_down, v_final_g=v_final_g)
    weights = {n: given[n] for n in TWIN_WEIGHTS}
    shared = {n: given[n] for n in SHARED_INPUTS}
    per_example = {n: given[n] for n in ['x', 'c']}
    grad_fn = _jax.value_and_grad(_loss, argnums=(0, 1))

    def one_microbatch(ex, loss_target):
        ex = dict(ex)
        diff = ex.pop(TWIN_DIFF_INPUT)
        return grad_fn(weights, diff, {**shared, **ex}, loss_target)

    if N_MICROBATCH == 1:
        loss, (grad_w, grad_x) = one_microbatch(per_example, given["loss_target"])
    else:
        def body(carry, xs):
            loss_sum, grad_sum = carry
            l_k, (gw_k, gx_k) = one_microbatch(xs[0], xs[1])
            with _jax.named_scope("update"):
                return (loss_sum + l_k, _jax.tree.map(_jnp.add, grad_sum, gw_k)), gx_k

        init = (_jnp.zeros((), _jnp.float32), _jax.tree.map(_jnp.zeros_like, weights))
        (loss, grad_w), grad_x = _jax.lax.scan(body, init, (per_example, given["loss_target"]))
    with _jax.named_scope("update"):
        delta_w, new_m, new_v = {}, {}, {}
        for n in TWIN_WEIGHTS:
            delta_w[n], new_m[n], new_v[n] = _adamw(weights[n], grad_w[n], given["m_" + n], given["v_" + n])
    return (loss, grad_x, *[grad_w[n] for n in TWIN_WEIGHTS], *[delta_w[n] for n in TWIN_WEIGHTS],
            *[new_m[n] for n in TWIN_WEIGHTS], *[new_v[n] for n in TWIN_WEIGHTS])
```

```python
import functools
import math

import jax
import jax.numpy as jnp
from jax import lax
from jax.experimental import pallas as pl
from jax.experimental.pallas import tpu as pltpu

F32 = jnp.float32
BF16 = jnp.bfloat16

N_DEV = 8
D_MODEL = 1024
DEPTH = 2
CHUNK = 128
SSD_HEADS = 16
SSD_HEAD_DIM = 64
SSD_GROUPS = 2
HEADS_PER_GROUP = SSD_HEADS // SSD_GROUPS
GROUP_WIDTH = HEADS_PER_GROUP * SSD_HEAD_DIM
D_STATE = 128
D_SSD = 1024
CONV_DIM = 1536
SSD_CONV = 4
GM_HEADS = 8
GM_HEAD_DIM = 128
D_GM = 1024
D_FF = 2816
FF_CONV = 3
N_IN = 4624
N_MOD = 6
EPS = 1e-6

N_INP = 5120
COL_U, COL_V, COL_Z, COL_XBC, COL_DT = 0, 1024, 2048, 3072, 4608
DT_BLOCK = 512

ADAM_LR = 0.001
ADAM_B1 = 0.9
ADAM_B2 = 0.999
ADAM_EPS = 1e-08
ADAM_WD = 0.01
ADAM_STEP = 10

VMEM_LIMIT = 56 * 1024 * 1024
MESH = pl.DeviceIdType.MESH
ANY = pl.BlockSpec(memory_space=pl.ANY)


def _cp(*sem):
    return pltpu.CompilerParams(dimension_semantics=sem, vmem_limit_bytes=VMEM_LIMIT)


def _tile(n, pref):
    if n <= pref or n % 128:
        return n
    best = 128
    for t in range(128, pref + 1, 128):
        if n % t == 0:
            best = t
    return best


def _silu(x):
    return x * jax.nn.sigmoid(x)


def _gelu(x):
    return 0.5 * x * (1.0 + lax.erf(x * (1.0 / math.sqrt(2.0))))


def _softplus(x):
    return jnp.maximum(x, 0.0) + jnp.log1p(jnp.exp(-jnp.abs(x)))


def _rms(x, g, width):
    return x * lax.rsqrt(jnp.sum(x * x, axis=-1, keepdims=True) / width + EPS) * g


def _b(x):
    return x.astype(BF16)


_NN = (((1,), (0,)), ((), ()))
_NT = (((1,), (1,)), ((), ()))
_TN = (((0,), (0,)), ((), ()))


def _dg(a, b, dn):
    return lax.dot_general(_b(a), _b(b), dn, preferred_element_type=F32)


@jax.custom_vjp
def _bdot(a, b):
    return _dg(a, b, _NN)


def _bdot_fwd(a, b):
    return _dg(a, b, _NN), (a, b)


def _bdot_bwd(res, ct):
    a, b = res
    return _dg(ct, b, _NT), _dg(a, ct, _TN)


_bdot.defvjp(_bdot_fwd, _bdot_bwd)


@jax.custom_vjp
def _bdot_nt(a, b):
    return _dg(a, b, _NT)


def _bdot_nt_fwd(a, b):
    return _dg(a, b, _NT), (a, b)


def _bdot_nt_bwd(res, ct):
    a, b = res
    return _dg(ct, b, _NN), _dg(ct, a, _TN)


_bdot_nt.defvjp(_bdot_nt_fwd, _bdot_nt_bwd)


@jax.custom_vjp
def _bdot_tn(a, b):
    return _dg(a, b, _TN)


def _bdot_tn_fwd(a, b):
    return _dg(a, b, _TN), (a, b)


def _bdot_tn_bwd(res, ct):
    a, b = res
    return _dg(b, ct, _NT), _dg(a, ct, _NN)


_bdot_tn.defvjp(_bdot_tn_fwd, _bdot_tn_bwd)


def _tri(n, lower):
    r = lax.broadcasted_iota(jnp.int32, (n, n), 0)
    c = lax.broadcasted_iota(jnp.int32, (n, n), 1)
    return ((r >= c) if lower else (r <= c)).astype(F32)


def _eye(n):
    r = lax.broadcasted_iota(jnp.int32, (n, n), 0)
    c = lax.broadcasted_iota(jnp.int32, (n, n), 1)
    return (r == c).astype(F32)


def _hdot(a, b, dn):
    return lax.dot_general(a, b, dn, precision=lax.Precision.HIGHEST, preferred_element_type=F32)


@jax.custom_vjp
def _cumsum_rows(x):
    return _hdot(_tri(x.shape[0], True), x, _NN)


def _cumsum_rows_fwd(x):
    return _cumsum_rows(x), None


def _cumsum_rows_bwd(_, ct):
    return (_hdot(_tri(ct.shape[0], False), ct, _NN),)


_cumsum_rows.defvjp(_cumsum_rows_fwd, _cumsum_rows_bwd)


@jax.custom_vjp
def _transpose(x):
    return _hdot(_eye(x.shape[1]), x, _NT)


def _transpose_fwd(x):
    return _transpose(x), None


def _transpose_bwd(_, ct):
    return (_hdot(_eye(ct.shape[1]), ct, _NT),)


_transpose.defvjp(_transpose_fwd, _transpose_bwd)


def _matmul(a, b, *, ta=False, tb=False, out_dtype=F32, name):
    if ta:
        k_dim, m_dim = a.shape
    else:
        m_dim, k_dim = a.shape
    if tb:
        n_dim, kb = b.shape
    else:
        kb, n_dim = b.shape
    assert kb == k_dim, (a.shape, b.shape, ta, tb)
    tm, tn, tk = _tile(m_dim, 512), _tile(n_dim, 1536), _tile(k_dim, 1536)
    nk = k_dim // tk
    dn = (((0 if ta else 1,), (1 if tb else 0,)), ((), ()))

    def body(a_ref, b_ref, o_ref, acc_ref):
        k = pl.program_id(2)
        p = lax.dot_general(a_ref[...], b_ref[...], dn, preferred_element_type=F32)

        @pl.when(k == 0)
        def _():
            acc_ref[...] = p

        @pl.when(k > 0)
        def _():
            acc_ref[...] += p

        @pl.when(k == nk - 1)
        def _():
            o_ref[...] = acc_ref[...].astype(o_ref.dtype)

    a_spec = pl.BlockSpec((tk, tm), lambda i, j, k: (k, i)) if ta else pl.BlockSpec((tm, tk), lambda i, j, k: (i, k))
    b_spec = pl.BlockSpec((tn, tk), lambda i, j, k: (j, k)) if tb else pl.BlockSpec((tk, tn), lambda i, j, k: (k, j))
    return pl.pallas_call(
        body, name=name,
        grid=(m_dim // tm, n_dim // tn, nk),
        in_specs=[a_spec, b_spec],
        out_specs=pl.BlockSpec((tm, tn), lambda i, j, k: (i, j)),
        out_shape=jax.ShapeDtypeStruct((m_dim, n_dim), out_dtype),
        scratch_shapes=[pltpu.VMEM((tm, tn), F32)],
        compiler_params=_cp("parallel", "parallel", "arbitrary"),
    )(a, b)


def _ada_fwd(c_all, ada_w, ada_b_shard):
    depth, d, n = ada_w.shape
    nb = c_all.shape[0]

    def body(c_ref, w_ref, b_ref, o_ref, ca_ref):
        ca = _silu(c_ref[...])
        ca_ref[...] = _b(ca)
        o_ref[0] = _dg(ca, w_ref[0], _NN) + b_ref[0]

    return pl.pallas_call(
        body, name="ada_fwd",
        grid=(depth,),
        in_specs=[pl.BlockSpec((nb, d), lambda l: (0, 0)),
                  pl.BlockSpec((1, d, n), lambda l: (l, 0, 0)),
                  pl.BlockSpec((1, 1, n), lambda l: (l, 0, 0))],
        out_specs=[pl.BlockSpec((1, nb, n), lambda l: (l, 0, 0)),
                   pl.BlockSpec((nb, d), lambda l: (0, 0))],
        out_shape=[jax.ShapeDtypeStruct((depth, nb, n), F32), jax.ShapeDtypeStruct((nb, d), BF16)],
        compiler_params=_cp("arbitrary"),
    )(c_all, ada_w, ada_b_shard)


def _normmod_f(x, g, sc, sh):
    return _rms(x, g, D_MODEL) * (1.0 + sc) + sh


def _row_tile(seq):
    return min(seq, 256)


def _normmod_fwd(xin, delta, gate, g, sc, sh, *, nseq, name):
    t, d = xin.shape
    seq = t // nseq
    tr = _row_tile(seq)
    nt = seq // tr
    has_delta = delta is not None
    row = pl.BlockSpec((tr, d), lambda s, i: (s * nt + i, 0))
    per_seq = pl.BlockSpec((1, 1, d), lambda s, i: (s, 0, 0))
    vec = pl.BlockSpec((1, d), lambda s, i: (0, 0))

    if has_delta:
        def body(xin_ref, delta_ref, gate_ref, g_ref, sc_ref, sh_ref, x_ref, h_ref):
            x = xin_ref[...] + gate_ref[0] * delta_ref[...]
            x_ref[...] = x
            h_ref[...] = _b(_normmod_f(x, g_ref[...], sc_ref[0], sh_ref[0]))

        return pl.pallas_call(
            body, name=name, grid=(nseq, nt),
            in_specs=[row, row, per_seq, vec, per_seq, per_seq],
            out_specs=[row, row],
            out_shape=[jax.ShapeDtypeStruct((t, d), F32), jax.ShapeDtypeStruct((t, d), BF16)],
            compiler_params=_cp("parallel", "parallel"),
        )(xin, delta, gate, g, sc, sh)

    def body0(xin_ref, g_ref, sc_ref, sh_ref, h_ref):
        h_ref[...] = _b(_normmod_f(xin_ref[...], g_ref[...], sc_ref[0], sh_ref[0]))

    h = pl.pallas_call(
        body0, name=name, grid=(nseq, nt),
        in_specs=[row, vec, per_seq, per_seq],
        out_specs=row,
        out_shape=jax.ShapeDtypeStruct((t, d), BF16),
        compiler_params=_cp("parallel", "parallel"),
    )(xin, g, sc, sh)
    return xin, h


def _normmod_bwd(dh, dxo, x, delta, gate, g, sc, *, nseq, name):
    t, d = x.shape
    seq = t // nseq
    tr = _row_tile(seq)
    nt = seq // tr
    has_delta = delta is not None
    row = pl.BlockSpec((tr, d), lambda s, i: (s * nt + i, 0))
    per_seq = pl.BlockSpec((1, 1, d), lambda s, i: (s, 0, 0))
    vec = pl.BlockSpec((1, d), lambda s, i: (0, 0))

    def core(dh_ref, dxo_ref, x_ref, g_ref, sc_ref, dx_ref, dg_ref, dsc_ref, dsh_ref):
        s, i = pl.program_id(0), pl.program_id(1)
        dh_v = dh_ref[...]
        _, vjp = jax.vjp(lambda xx, gg, ss: _normmod_f(xx, gg, ss, 0.0), x_ref[...], g_ref[...], sc_ref[0])
        dxn, dg_t, dsc_t = vjp(dh_v)
        dx = dxo_ref[...] + dxn
        dx_ref[...] = dx
        dsh_t = jnp.sum(dh_v, axis=0, keepdims=True)

        @pl.when((s == 0) & (i == 0))
        def _():
            dg_ref[...] = jnp.zeros_like(dg_ref)

        @pl.when(i == 0)
        def _():
            dsc_ref[...] = jnp.zeros_like(dsc_ref)
            dsh_ref[...] = jnp.zeros_like(dsh_ref)

        dg_ref[...] += dg_t
        dsc_ref[0] += dsc_t
        dsh_ref[0] += dsh_t
        return dx

    if has_delta:
        def body(dh_ref, dxo_ref, x_ref, delta_ref, gate_ref, g_ref, sc_ref,
                 dx_ref, dd_ref, dgate_ref, dg_ref, dsc_ref, dsh_ref):
            dx = core(dh_ref, dxo_ref, x_ref, g_ref, sc_ref, dx_ref, dg_ref, dsc_ref, dsh_ref)
            dd_ref[...] = _b(dx * gate_ref[0])

            @pl.when(pl.program_id(1) == 0)
            def _():
                dgate_ref[...] = jnp.zeros_like(dgate_ref)

            dgate_ref[0] += jnp.sum(dx * delta_ref[...], axis=0, keepdims=True)

        return pl.pallas_call(
            body, name=name, grid=(nseq, nt),
            in_specs=[row, row, row, row, per_seq, vec, per_seq],
            out_specs=[row, row, per_seq, vec, per_seq, per_seq],
            out_shape=[jax.ShapeDtypeStruct((t, d), F32), jax.ShapeDtypeStruct((t, d), BF16),
                       jax.ShapeDtypeStruct((nseq, 1, d), F32), jax.ShapeDtypeStruct((1, d), F32),
                       jax.ShapeDtypeStruct((nseq, 1, d), F32), jax.ShapeDtypeStruct((nseq, 1, d), F32)],
            compiler_params=_cp("arbitrary", "arbitrary"),
        )(dh, dxo, x, delta, gate, g, sc)

    def body0(dh_ref, dxo_ref, x_ref, g_ref, sc_ref, dx_ref, dg_ref, dsc_ref, dsh_ref):
        core(dh_ref, dxo_ref, x_ref, g_ref, sc_ref, dx_ref, dg_ref, dsc_ref, dsh_ref)

    dx, dg, dsc, dsh = pl.pallas_call(
        body0, name=name, grid=(nseq, nt),
        in_specs=[row, row, row, vec, per_seq],
        out_specs=[row, vec, per_seq, per_seq],
        out_shape=[jax.ShapeDtypeStruct((t, d), F32), jax.ShapeDtypeStruct((1, d), F32),
                   jax.ShapeDtypeStruct((nseq, 1, d), F32), jax.ShapeDtypeStruct((nseq, 1, d), F32)],
        compiler_params=_cp("arbitrary", "arbitrary"),
    )(dh, dxo, x, g, sc)
    return dx, None, None, dg, dsc, dsh


def _final_loss(xin, delta, gate, fg, target, *, nseq):
    t, d = xin.shape
    seq = t // nseq
    tr = _row_tile(seq)
    nt = seq // tr
    row = pl.BlockSpec((tr, d), lambda s, i: (s * nt + i, 0))
    per_seq = pl.BlockSpec((1, 1, d), lambda s, i: (s, 0, 0))
    vec = pl.BlockSpec((1, d), lambda s, i: (0, 0))

    def body(xin_ref, delta_ref, gate_ref, fg_ref, tgt_ref, loss_ref, dx_ref, dd_ref, dgate_ref, dfg_ref):
        s, i = pl.program_id(0), pl.program_id(1)
        dl = delta_ref[...]
        x = xin_ref[...] + gate_ref[0] * dl
        y, vjp = jax.vjp(lambda xx, gg: _rms(xx, gg, D_MODEL), x, fg_ref[...])
        err = y - tgt_ref[...]
        dx, dfg_t = vjp(err * (1.0 / d))
        dx_ref[...] = dx
        dd_ref[...] = _b(dx * gate_ref[0])

        @pl.when((s == 0) & (i == 0))
        def _():
            loss_ref[...] = jnp.zeros_like(loss_ref)
            dfg_ref[...] = jnp.zeros_like(dfg_ref)

        @pl.when(i == 0)
        def _():
            dgate_ref[...] = jnp.zeros_like(dgate_ref)

        loss_ref[...] += jnp.sum(err * err) * (0.5 / d)
        dfg_ref[...] += dfg_t
        dgate_ref[0] += jnp.sum(dx * dl, axis=0, keepdims=True)

    return pl.pallas_call(
        body, name="final_loss", grid=(nseq, nt),
        in_specs=[row, row, per_seq, vec, row],
        out_specs=[pl.BlockSpec((1, 128), lambda s, i: (0, 0)), row, row, per_seq, vec],
        out_shape=[jax.ShapeDtypeStruct((1, 128), F32), jax.ShapeDtypeStruct((t, d), F32),
                   jax.ShapeDtypeStruct((t, d), BF16), jax.ShapeDtypeStruct((nseq, 1, d), F32),
                   jax.ShapeDtypeStruct((1, d), F32)],
        compiler_params=_cp("arbitrary", "arbitrary"),
    )(xin, delta, gate, fg, target)


def _shift_down(x, j):
    if j == 0:
        return x
    rows = lax.broadcasted_iota(jnp.int32, x.shape, 0)
    return jnp.where(rows >= j, pltpu.roll(x, j, 0), 0.0)


def _shift_up(x, j):
    if j == 0:
        return x
    n = x.shape[0]
    rows = lax.broadcasted_iota(jnp.int32, x.shape, 0)
    return jnp.where(rows < n - j, pltpu.roll(x, n - j, 0), 0.0)


def _conv(x, w_ref, b_ref):
    kw = w_ref.shape[0]
    y = b_ref[...] + w_ref[kw - 1:kw, :] * x
    for j in range(1, kw):
        y = y + w_ref[kw - 1 - j:kw - j, :] * _shift_down(x, j)
    return y


def _conv_bwd(dy, x, w_ref, dw_ref, db_ref):
    kw = w_ref.shape[0]
    dx = w_ref[kw - 1:kw, :] * dy
    dw_ref[kw - 1:kw, :] += jnp.sum(dy * x, axis=0, keepdims=True)
    for j in range(1, kw):
        dx = dx + w_ref[kw - 1 - j:kw - j, :] * _shift_up(dy, j)
        dw_ref[kw - 1 - j:kw - j, :] += jnp.sum(dy * _shift_down(x, j), axis=0, keepdims=True)
    db_ref[...] += jnp.sum(dy, axis=0, keepdims=True)
    return dx


CONV_TC = 256


def _ssd_conv_fwd(proj, w, b, *, nseq):
    t = proj.shape[0]
    seq = t // nseq
    nb = CONV_DIM // CONV_TC
    off = COL_XBC // CONV_TC

    def body(x_ref, w_ref, b_ref, o_ref):
        o_ref[...] = _silu(_conv(x_ref[...], w_ref, b_ref))

    return pl.pallas_call(
        body, name="ssd_conv_fwd", grid=(nb, nseq),
        in_specs=[pl.BlockSpec((seq, CONV_TC), lambda j, s: (s, off + j)),
                  pl.BlockSpec((SSD_CONV, CONV_TC), lambda j, s: (0, j)),
                  pl.BlockSpec((1, CONV_TC), lambda j, s: (0, j))],
        out_specs=pl.BlockSpec((seq, CONV_TC), lambda j, s: (s, j)),
        out_shape=jax.ShapeDtypeStruct((t, CONV_DIM), F32),
        compiler_params=_cp("parallel", "parallel"),
    )(proj, w, b)


def _ssd_conv_bwd(dact, proj, w, b, *, nseq):
    t = proj.shape[0]
    seq = t // nseq
    nb = CONV_DIM // CONV_TC
    off = COL_XBC // CONV_TC

    def body(da_ref, x_ref, w_ref, b_ref, dx_ref, dw_ref, db_ref):
        @pl.when(pl.program_id(1) == 0)
        def _():
            dw_ref[...] = jnp.zeros_like(dw_ref)
            db_ref[...] = jnp.zeros_like(db_ref)

        x = x_ref[...]
        pre = _conv(x, w_ref, b_ref)
        sg = jax.nn.sigmoid(pre)
        dpre = da_ref[...] * (sg * (1.0 + pre * (1.0 - sg)))
        dx_ref[...] = _b(_conv_bwd(dpre, x, w_ref, dw_ref, db_ref))

    return pl.pallas_call(
        body, name="ssd_conv_bwd", grid=(nb, nseq),
        in_specs=[pl.BlockSpec((seq, CONV_TC), lambda j, s: (s, j)),
                  pl.BlockSpec((seq, CONV_TC), lambda j, s: (s, off + j)),
                  pl.BlockSpec((SSD_CONV, CONV_TC), lambda j, s: (0, j)),
                  pl.BlockSpec((1, CONV_TC), lambda j, s: (0, j))],
        out_specs=[pl.BlockSpec((seq, CONV_TC), lambda j, s: (s, j)),
                   pl.BlockSpec((SSD_CONV, CONV_TC), lambda j, s: (0, j)),
                   pl.BlockSpec((1, CONV_TC), lambda j, s: (0, j))],
        out_shape=[jax.ShapeDtypeStruct((t, CONV_DIM), BF16), jax.ShapeDtypeStruct((SSD_CONV, CONV_DIM), F32),
                   jax.ShapeDtypeStruct((1, CONV_DIM), F32)],
        compiler_params=_cp("parallel", "arbitrary"),
    )(dact, proj, w, b)


def _ffn_act_fwd(up, w, b, *, nseq):
    t = up.shape[0]
    seq = t // nseq
    nb = D_FF // CONV_TC

    def body(g_ref, v_ref, w_ref, b_ref, o_ref):
        o_ref[...] = _b(_silu(_conv(g_ref[...], w_ref, b_ref)) * v_ref[...])

    return pl.pallas_call(
        body, name="ffn_act_fwd", grid=(nb, nseq),
        in_specs=[pl.BlockSpec((seq, CONV_TC), lambda j, s: (s, j)),
                  pl.BlockSpec((seq, CONV_TC), lambda j, s: (s, nb + j)),
                  pl.BlockSpec((FF_CONV, CONV_TC), lambda j, s: (0, j)),
                  pl.BlockSpec((1, CONV_TC), lambda j, s: (0, j))],
        out_specs=pl.BlockSpec((seq, CONV_TC), lambda j, s: (s, j)),
        out_shape=jax.ShapeDtypeStruct((t, D_FF), BF16),
        compiler_params=_cp("parallel", "parallel"),
    )(up, up, w, b)


def _ffn_act_bwd(dact, up, w, b, *, nseq):
    t = up.shape[0]
    seq = t // nseq
    nb = D_FF // CONV_TC

    def body(da_ref, g_ref, v_ref, w_ref, b_ref, dup_ref, dw_ref, db_ref):
        j = pl.program_id(0)

        @pl.when(pl.program_id(1) == 0)
        def _():
            dw_ref[...] = jnp.zeros_like(dw_ref)
            db_ref[...] = jnp.zeros_like(db_ref)

        gate = g_ref[...]
        pre = _conv(gate, w_ref, b_ref)
        sg = jax.nn.sigmoid(pre)

        @pl.when(j < nb)
        def _():
            dpre = da_ref[...] * v_ref[...] * (sg * (1.0 + pre * (1.0 - sg)))
            dup_ref[...] = _b(_conv_bwd(dpre, gate, w_ref, dw_ref, db_ref))

        @pl.when(j >= nb)
        def _():
            dup_ref[...] = _b(da_ref[...] * (pre * sg))

    return pl.pallas_call(
        body, name="ffn_act_bwd", grid=(2 * nb, nseq),
        in_specs=[pl.BlockSpec((seq, CONV_TC), lambda j, s: (s, j % nb)),
                  pl.BlockSpec((seq, CONV_TC), lambda j, s: (s, j % nb)),
                  pl.BlockSpec((seq, CONV_TC), lambda j, s: (s, nb + j % nb)),
                  pl.BlockSpec((FF_CONV, CONV_TC), lambda j, s: (0, j % nb)),
                  pl.BlockSpec((1, CONV_TC), lambda j, s: (0, j % nb))],
        out_specs=[pl.BlockSpec((seq, CONV_TC), lambda j, s: (s, j)),
                   pl.BlockSpec((FF_CONV, CONV_TC), lambda j, s: (0, j)),
                   pl.BlockSpec((1, CONV_TC), lambda j, s: (0, j))],
        out_shape=[jax.ShapeDtypeStruct((t, 2 * D_FF), BF16), jax.ShapeDtypeStruct((FF_CONV, 2 * D_FF), F32),
                   jax.ShapeDtypeStruct((1, 2 * D_FF), F32)],
        compiler_params=_cp("parallel", "arbitrary"),
    )(dact, up, up, w, b)


def _ssd_chunk(xs, bg, cg, dtr, z, hp, dtb, alog, dskip, ng):
    n = dtr.shape[0]
    dt = _softplus(dtr + dtb)
    cs = _cumsum_rows(dt * (-jnp.exp(alog)))
    cs_t = _transpose(cs)
    lane = lax.broadcasted_iota(jnp.int32, (1, SSD_HEADS), 1)
    sub = lax.broadcasted_iota(jnp.int32, (SSD_HEADS, 1), 0)
    row = lax.broadcasted_iota(jnp.int32, (n, 1), 0)
    r2 = lax.broadcasted_iota(jnp.int32, (n, n), 0)
    c2 = lax.broadcasted_iota(jnp.int32, (n, n), 1)
    causal = r2 >= c2
    cb = [_bdot_nt(cg[g], bg[g]) for g in range(SSD_GROUPS)]
    ys, hn = [], []
    for h in range(SSD_HEADS):
        g = h // HEADS_PER_GROUP
        oh = (lane == h).astype(F32)
        oh_t = (sub == h).astype(F32)
        dt_h = jnp.sum(dt * oh, axis=1, keepdims=True)
        cs_h = jnp.sum(cs * oh, axis=1, keepdims=True)
        d_h = jnp.sum(dskip * oh, axis=1, keepdims=True)
        cs_row = jnp.sum(cs_t * oh_t, axis=0, keepdims=True)
        cs_last = jnp.sum(jnp.where(row == n - 1, cs_h, 0.0), axis=0, keepdims=True)
        decay = jnp.where(causal, jnp.exp(jnp.where(causal, cs_h - cs_row, 0.0)), 0.0)
        xc = xs[h] * dt_h
        y = _bdot(cb[g] * decay, xc)
        y = y + _bdot_nt(cg[g], hp[h]) * jnp.exp(cs_h)
        y = y + d_h * xs[h]
        hn.append(jnp.exp(cs_last) * hp[h] + _bdot_tn(xc * jnp.exp(cs_last - cs_h), bg[g]))
        ys.append(y * _silu(z[h]))
    outs = []
    for g in range(SSD_GROUPS):
        hs = range(g * HEADS_PER_GROUP, (g + 1) * HEADS_PER_GROUP)
        ms = sum(jnp.sum(ys[h] * ys[h], axis=1, keepdims=True) for h in hs) * (1.0 / GROUP_WIDTH)
        r = lax.rsqrt(ms + EPS)
        outs += [ys[h] * r * ng[h] for h in hs]
    return outs, hn


def _hslices(ref, width, count, base=0):
    return [ref[:, base + k * width: base + (k + 1) * width] for k in range(count)]


def _ssd_load(xbc_ref, z_ref, dt_ref, ng_ref):
    xs = _hslices(xbc_ref, SSD_HEAD_DIM, SSD_HEADS)
    bg = _hslices(xbc_ref, D_STATE, SSD_GROUPS, D_SSD)
    cg = _hslices(xbc_ref, D_STATE, SSD_GROUPS, D_SSD + SSD_GROUPS * D_STATE)
    z = _hslices(z_ref, SSD_HEAD_DIM, SSD_HEADS)
    ng = _hslices(ng_ref, SSD_HEAD_DIM, SSD_HEADS)
    return xs, bg, cg, dt_ref[:, 0:SSD_HEADS], z, ng


def _ssd_specs(nch):
    rowi = lambda s, c: s * nch + c
    return [pl.BlockSpec((CHUNK, CONV_DIM), lambda s, c: (rowi(s, c), 0)),
            pl.BlockSpec((CHUNK, D_SSD), lambda s, c: (rowi(s, c), COL_Z // D_SSD)),
            pl.BlockSpec((CHUNK, 128), lambda s, c: (rowi(s, c), COL_DT // 128)),
            pl.BlockSpec((1, SSD_HEADS), lambda s, c: (0, 0)),
            pl.BlockSpec((1, SSD_HEADS), lambda s, c: (0, 0)),
            pl.BlockSpec((1, SSD_HEADS), lambda s, c: (0, 0)),
            pl.BlockSpec((1, D_SSD), lambda s, c: (0, 0))]


def _ssd_fwd(xbc, proj, dtb, alog, dskip, ng, *, nseq):
    t = proj.shape[0]
    nch = t // nseq // CHUNK
    hd = SSD_HEAD_DIM

    def body(xbc_ref, z_ref, dt_ref, dtb_ref, alog_ref, dsk_ref, ng_ref, y_ref, hp_ref, h_ref):
        @pl.when(pl.program_id(1) == 0)
        def _():
            h_ref[...] = jnp.zeros_like(h_ref)

        xs, bg, cg, dtr, z, ngs = _ssd_load(xbc_ref, z_ref, dt_ref, ng_ref)
        hp_ref[0] = h_ref[...]
        hp = [h_ref[h * hd:(h + 1) * hd, :] for h in range(SSD_HEADS)]
        outs, hn = _ssd_chunk(xs, bg, cg, dtr, z, hp, dtb_ref[...], alog_ref[...], dsk_ref[...], ngs)
        for h in range(SSD_HEADS):
            y_ref[:, h * hd:(h + 1) * hd] = _b(outs[h])
            h_ref[h * hd:(h + 1) * hd, :] = hn[h]

    return pl.pallas_call(
        body, name="ssd_fwd", grid=(nseq, nch),
        in_specs=_ssd_specs(nch),
        out_specs=[pl.BlockSpec((CHUNK, D_SSD), lambda s, c: (s * nch + c, 0)),
                   pl.BlockSpec((1, SSD_HEADS * hd, D_STATE), lambda s, c: (s * nch + c, 0, 0))],
        out_shape=[jax.ShapeDtypeStruct((t, D_SSD), BF16),
                   jax.ShapeDtypeStruct((t // CHUNK, SSD_HEADS * hd, D_STATE), F32)],
        scratch_shapes=[pltpu.VMEM((SSD_HEADS * hd, D_STATE), F32)],
        compiler_params=_cp("arbitrary", "arbitrary"),
    )(xbc, proj, proj, dtb, alog, dskip, ng)


def _ssd_bwd(dy, xbc, proj, hprev, dtb, alog, dskip, ng, *, nseq):
    t = proj.shape[0]
    nch = t // nseq // CHUNK
    hd = SSD_HEAD_DIM
    rev = lambda s, c: s * nch + (nch - 1 - c)

    def body(dy_ref, xbc_ref, z_ref, dt_ref, hp_ref, dtb_ref, alog_ref, dsk_ref, ng_ref,
             dxbc_ref, dz_ref, ddt_ref, ddtb_ref, dalog_ref, ddsk_ref, dng_ref, dh_ref):
        first = (pl.program_id(0) == 0) & (pl.program_id(1) == 0)

        @pl.when(pl.program_id(1) == 0)
        def _():
            dh_ref[...] = jnp.zeros_like(dh_ref)

        @pl.when(first)
        def _():
            ddtb_ref[...] = jnp.zeros_like(ddtb_ref)
            dalog_ref[...] = jnp.zeros_like(dalog_ref)
            ddsk_ref[...] = jnp.zeros_like(ddsk_ref)
            dng_ref[...] = jnp.zeros_like(dng_ref)

        xs, bg, cg, dtr, z, ngs = _ssd_load(xbc_ref, z_ref, dt_ref, ng_ref)
        hp = [hp_ref[0, h * hd:(h + 1) * hd, :] for h in range(SSD_HEADS)]
        _, vjp = jax.vjp(_ssd_chunk, xs, bg, cg, dtr, z, hp, dtb_ref[...], alog_ref[...], dsk_ref[...], ngs)
        douts = [dy_ref[:, h * hd:(h + 1) * hd] for h in range(SSD_HEADS)]
        dhn = [dh_ref[h * hd:(h + 1) * hd, :] for h in range(SSD_HEADS)]
        dxs, dbg, dcg, ddtr, dz, dhp, ddtb, dalog, ddsk, dngs = vjp((douts, dhn))
        for h in range(SSD_HEADS):
            dxbc_ref[:, h * hd:(h + 1) * hd] = dxs[h]
            dz_ref[:, h * hd:(h + 1) * hd] = _b(dz[h])
            dh_ref[h * hd:(h + 1) * hd, :] = dhp[h]
            dng_ref[:, h * hd:(h + 1) * hd] += dngs[h]
        for g in range(SSD_GROUPS):
            dxbc_ref[:, D_SSD + g * D_STATE: D_SSD + (g + 1) * D_STATE] = dbg[g]
            dxbc_ref[:, D_SSD + (SSD_GROUPS + g) * D_STATE: D_SSD + (SSD_GROUPS + g + 1) * D_STATE] = dcg[g]
        ddt_ref[...] = jnp.zeros_like(ddt_ref)
        ddt_ref[:, 0:SSD_HEADS] = _b(ddtr)
        ddtb_ref[...] += ddtb
        dalog_ref[...] += dalog
        ddsk_ref[...] += ddsk

    small = pl.BlockSpec((1, SSD_HEADS), lambda s, c: (0, 0))
    return pl.pallas_call(
        body, name="ssd_bwd", grid=(nseq, nch),
        in_specs=[pl.BlockSpec((CHUNK, D_SSD), lambda s, c: (rev(s, c), 0)),
                  pl.BlockSpec((CHUNK, CONV_DIM), lambda s, c: (rev(s, c), 0)),
                  pl.BlockSpec((CHUNK, D_SSD), lambda s, c: (rev(s, c), COL_Z // D_SSD)),
                  pl.BlockSpec((CHUNK, 128), lambda s, c: (rev(s, c), COL_DT // 128)),
                  pl.BlockSpec((1, SSD_HEADS * hd, D_STATE), lambda s, c: (rev(s, c), 0, 0)),
                  small, small, small,
                  pl.BlockSpec((1, D_SSD), lambda s, c: (0, 0))],
        out_specs=[pl.BlockSpec((CHUNK, CONV_DIM), lambda s, c: (rev(s, c), 0)),
                   pl.BlockSpec((CHUNK, D_SSD), lambda s, c: (rev(s, c), 0)),
                   pl.BlockSpec((CHUNK, DT_BLOCK), lambda s, c: (rev(s, c), 0)),
                   small, small, small,
                   pl.BlockSpec((1, D_SSD), lambda s, c: (0, 0))],
        out_shape=[jax.ShapeDtypeStruct((t, CONV_DIM), F32), jax.ShapeDtypeStruct((t, D_SSD), BF16),
                   jax.ShapeDtypeStruct((t, DT_BLOCK), BF16),
                   jax.ShapeDtypeStruct((1, SSD_HEADS), F32), jax.ShapeDtypeStruct((1, SSD_HEADS), F32),
                   jax.ShapeDtypeStruct((1, SSD_HEADS), F32), jax.ShapeDtypeStruct((1, D_SSD), F32)],
        scratch_shapes=[pltpu.VMEM((SSD_HEADS * hd, D_STATE), F32)],
        compiler_params=_cp("arbitrary", "arbitrary"),
    )(dy, xbc, proj, proj, hprev, dtb, alog, dskip, ng)


def _gmlp_chunk(gu, gv, ws, bs_cols, vg, og):
    n = gu[0].shape[0]
    mask = _tri(n, True)
    au = [_gelu(t) for t in gu]
    av = [_gelu(t) for t in gv]
    r = lax.rsqrt(sum(jnp.sum(t * t, axis=1, keepdims=True) for t in av) * (1.0 / D_GM) + EPS)
    p = []
    for h in range(GM_HEADS):
        sv = _bdot(ws[h] * mask, av[h] * r * vg[h]) + bs_cols[h]
        p.append(au[h] * sv)
    r2 = lax.rsqrt(sum(jnp.sum(t * t, axis=1, keepdims=True) for t in p) * (1.0 / D_GM) + EPS)
    return [p[h] * r2 * og[h] for h in range(GM_HEADS)]


def _gmlp_load(u_ref, v_ref, ws_ref, bst_ref, vg_ref, og_ref):
    gu = _hslices(u_ref, GM_HEAD_DIM, GM_HEADS)
    gv = _hslices(v_ref, GM_HEAD_DIM, GM_HEADS)
    ws = [ws_ref[h] for h in range(GM_HEADS)]
    bs_cols = [bst_ref[:, h:h + 1] for h in range(GM_HEADS)]
    return gu, gv, ws, bs_cols, _hslices(vg_ref, GM_HEAD_DIM, GM_HEADS), _hslices(og_ref, GM_HEAD_DIM, GM_HEADS)


def _gmlp_specs():
    return [pl.BlockSpec((CHUNK, D_GM), lambda i: (i, COL_U // D_GM)),
            pl.BlockSpec((CHUNK, D_GM), lambda i: (i, COL_V // D_GM)),
            pl.BlockSpec((GM_HEADS, CHUNK, CHUNK), lambda i: (0, 0, 0)),
            pl.BlockSpec((CHUNK, GM_HEADS), lambda i: (0, 0)),
            pl.BlockSpec((1, D_GM), lambda i: (0, 0)),
            pl.BlockSpec((1, D_GM), lambda i: (0, 0))]


def _gmlp_fwd(proj, ws, bst, vg, og):
    t = proj.shape[0]

    def body(u_ref, v_ref, ws_ref, bst_ref, vg_ref, og_ref, o_ref):
        outs = _gmlp_chunk(*_gmlp_load(u_ref, v_ref, ws_ref, bst_ref, vg_ref, og_ref))
        for h in range(GM_HEADS):
            o_ref[:, h * GM_HEAD_DIM:(h + 1) * GM_HEAD_DIM] = _b(outs[h])

    return pl.pallas_call(
        body, name="gmlp_fwd", grid=(t // CHUNK,),
        in_specs=_gmlp_specs(),
        out_specs=pl.BlockSpec((CHUNK, D_GM), lambda i: (i, 0)),
        out_shape=jax.ShapeDtypeStruct((t, D_GM), BF16),
        compiler_params=_cp("parallel"),
    )(proj, proj, ws, bst, vg, og)


def _gmlp_bwd(dy, proj, ws, bst, vg, og):
    t = proj.shape[0]
    w = GM_HEAD_DIM

    def body(dy_ref, u_ref, v_ref, ws_ref, bst_ref, vg_ref, og_ref, dgm_ref, dws_ref, dbst_ref, dvg_ref, dog_ref):
        @pl.when(pl.program_id(0) == 0)
        def _():
            dws_ref[...] = jnp.zeros_like(dws_ref)
            dbst_ref[...] = jnp.zeros_like(dbst_ref)
            dvg_ref[...] = jnp.zeros_like(dvg_ref)
            dog_ref[...] = jnp.zeros_like(dog_ref)

        _, vjp = jax.vjp(_gmlp_chunk, *_gmlp_load(u_ref, v_ref, ws_ref, bst_ref, vg_ref, og_ref))
        dgu, dgv, dws, dbs, dvg, dog = vjp(_hslices(dy_ref, w, GM_HEADS))
        for h in range(GM_HEADS):
            dgm_ref[:, h * w:(h + 1) * w] = _b(dgu[h])
            dgm_ref[:, D_GM + h * w: D_GM + (h + 1) * w] = _b(dgv[h])
            dws_ref[h] += dws[h]
            dbst_ref[:, h:h + 1] += dbs[h]
            dvg_ref[:, h * w:(h + 1) * w] += dvg[h]
            dog_ref[:, h * w:(h + 1) * w] += dog[h]

    return pl.pallas_call(
        body, name="gmlp_bwd", grid=(t // CHUNK,),
        in_specs=[pl.BlockSpec((CHUNK, D_GM), lambda i: (i, 1))] + _gmlp_specs(),
        out_specs=[pl.BlockSpec((CHUNK, 2 * D_GM), lambda i: (i, 0)),
                   pl.BlockSpec((GM_HEADS, CHUNK, CHUNK), lambda i: (0, 0, 0)),
                   pl.BlockSpec((CHUNK, GM_HEADS), lambda i: (0, 0)),
                   pl.BlockSpec((1, D_GM), lambda i: (0, 0)),
                   pl.BlockSpec((1, D_GM), lambda i: (0, 0))],
        out_shape=[jax.ShapeDtypeStruct((t, 2 * D_GM), BF16), jax.ShapeDtypeStruct((GM_HEADS, CHUNK, CHUNK), F32),
                   jax.ShapeDtypeStruct((CHUNK, GM_HEADS), F32), jax.ShapeDtypeStruct((1, D_GM), F32),
                   jax.ShapeDtypeStruct((1, D_GM), F32)],
        compiler_params=_cp("arbitrary"),
    )(dy, proj, proj, ws, bst, vg, og)


def _local_step(x, target, mods, lw, final_g, *, nseq):
    saved = []
    xin, delta, gate = x, None, None
    for l in range(DEPTH):
        w = lw[l]
        sh1, sc1, g1, sh2, sc2, g2 = mods[l]
        x0, h1 = _normmod_fwd(xin, delta, gate, w["norm1_g"], sc1, sh1, nseq=nseq, name=f"norm1_fwd_{l}")
        proj = _matmul(h1, w["w_in"], name=f"mm_in_{l}")
        xbc = _ssd_conv_fwd(proj, w["ssd_conv_w"], w["ssd_conv_b"], nseq=nseq)
        y, hprev = _ssd_fwd(xbc, proj, w["ssd_dt_bias"], w["ssd_a_log"], w["ssd_d"], w["ssd_norm_g"], nseq=nseq)
        gout = _gmlp_fwd(proj, w["gm_ws"], w["gm_bst"], w["gm_vnorm_g"], w["gm_out_g"])
        ycat = jnp.concatenate([y, gout], axis=1)
        mix = _matmul(ycat, w["w_out"], name=f"mm_out_{l}")
        x1, h2 = _normmod_fwd(x0, mix, g1, w["norm2_g"], sc2, sh2, nseq=nseq, name=f"norm2_fwd_{l}")
        up = _matmul(h2, w["ff_up"], name=f"mm_up_{l}")
        act = _ffn_act_fwd(up, w["ff_conv_w"], w["ff_conv_b"], nseq=nseq)
        dn = _matmul(act, w["ff_down"], name=f"mm_down_{l}")
        saved.append(dict(x0=x0, xin_delta=delta, xin_gate=gate, h1=h1, proj=proj, xbc=xbc, hprev=hprev, ycat=ycat,
                          mix=mix, x1=x1, h2=h2, up=up, act=act, dn=dn))
        xin, delta, gate = x1, dn, g2

    loss, dx, ddelta, dgate, dfg = _final_loss(xin, delta, gate, final_g, target, nseq=nseq)

    big, small, dmods = [None] * DEPTH, [None] * DEPTH, [None] * DEPTH
    for l in reversed(range(DEPTH)):
        w, sv = lw[l], saved[l]
        sh1, sc1, g1, sh2, sc2, g2 = mods[l]
        dg2 = dgate
        dact = _matmul(ddelta, w["ff_down"], tb=True, name=f"mm_down_dx_{l}")
        g_ff_down = _matmul(sv["act"], ddelta, ta=True, name=f"mm_down_dw_{l}")
        dup, dfcw, dfcb = _ffn_act_bwd(dact, sv["up"], w["ff_conv_w"], w["ff_conv_b"], nseq=nseq)
        dh2 = _matmul(dup, w["ff_up"], tb=True, name=f"mm_up_dx_{l}")
        g_ff_up = _matmul(sv["h2"], dup, ta=True, name=f"mm_up_dw_{l}")
        dx, dmix, dg1, dn2g, dsc2, dsh2 = _normmod_bwd(dh2, dx, sv["x1"], sv["mix"], g1, w["norm2_g"], sc2,
                                                       nseq=nseq, name=f"norm2_bwd_{l}")
        dycat = _matmul(dmix, w["w_out"], tb=True, name=f"mm_out_dx_{l}")
        g_w_out = _matmul(sv["ycat"], dmix, ta=True, name=f"mm_out_dw_{l}")
        dgm, dws, dbst, dvg, dog = _gmlp_bwd(dycat, sv["proj"], w["gm_ws"], w["gm_bst"], w["gm_vnorm_g"], w["gm_out_g"])
        dxbc_act, dz, ddt, ddtb, dalog, ddsk, dng = _ssd_bwd(dycat, sv["xbc"], sv["proj"], sv["hprev"], w["ssd_dt_bias"],
                                                            w["ssd_a_log"], w["ssd_d"], w["ssd_norm_g"], nseq=nseq)
        dxbc, dscw, dscb = _ssd_conv_bwd(dxbc_act, sv["proj"], w["ssd_conv_w"], w["ssd_conv_b"], nseq=nseq)
        dproj = jnp.concatenate([dgm, dz, dxbc, ddt], axis=1)
        dh1 = _matmul(dproj, w["w_in"], tb=True, name=f"mm_in_dx_{l}")
        g_w_in = _matmul(sv["h1"], dproj, ta=True, name=f"mm_in_dw_{l}")
        dx, ddelta, dgate, dn1g, dsc1, dsh1 = _normmod_bwd(dh1, dx, sv["x0"], sv["xin_delta"], sv["xin_gate"],
                                                           w["norm1_g"], sc1, nseq=nseq, name=f"norm1_bwd_{l}")
        big[l] = dict(w_in=g_w_in, w_out=g_w_out, ff_up=g_ff_up, ff_down=g_ff_down)
        small[l] = dict(norm1_g=dn1g, norm2_g=dn2g, ssd_norm_g=dng, gm_vnorm_g=dvg, gm_out_g=dog,
                        ssd_conv_w=dscw, ssd_conv_b=dscb, ff_conv_w=dfcw[:, :D_FF], ff_conv_b=dfcb[:, :D_FF],
                        ssd_dt_bias=ddtb, ssd_a_log=dalog, ssd_d=ddsk, gm_ws=dws, gm_bs=dbst.T)
        dmods[l] = jnp.concatenate([dsh1, dsc1, dg1, dsh2, dsc2, dg2], axis=-1)[:, 0, :]
    return loss, dx, big, small, dmods, dfg


def _all_gather(arrs, name):
    n = len(arrs)

    def body(*refs):
        ins, outs = refs[:n], refs[n:2 * n]
        send_sems, recv_sems, local_sems = refs[2 * n:]
        x, y, c = lax.axis_index("x"), lax.axis_index("y"), lax.axis_index("c")
        me, sibling = (x, y, c), (x, y, 1 - c)
        chips = [(1 - x, y), (x, 1 - y), (1 - x, 1 - y)]

        def copy(i, k, block, to, src=None):
            px, py, pc = block
            dst = outs[i].at[4 * px + 2 * py + pc]
            return pltpu.make_async_remote_copy(
                src_ref=dst if src is None else src, dst_ref=dst,
                send_sem=send_sems.at[7 * i + k], recv_sem=recv_sems.at[7 * i + k],
                device_id=to, device_id_type=MESH)

        mine = [pltpu.make_async_copy(ins[i], outs[i].at[4 * x + 2 * y + c], local_sems.at[i]) for i in range(n)]
        for cp in mine:
            cp.start()
        first = []
        for i in range(n):
            first.append(copy(i, 0, me, sibling, src=ins[i]))
            first += [copy(i, 1 + j, me, (*chip, c), src=ins[i]) for j, chip in enumerate(chips)]
        for cp in first:
            cp.start()
        passed = []
        for j, chip in enumerate(chips):
            for i in range(n):
                copy(i, 1 + j, (*chip, c), me).wait_recv()
                fwd = copy(i, 4 + j, (*chip, c), sibling)
                fwd.start()
                passed.append(fwd)
        for i in range(n):
            copy(i, 0, sibling, me).wait_recv()
            for j, chip in enumerate(chips):
                copy(i, 4 + j, (*chip, 1 - c), me).wait_recv()
        for cp in first + passed:
            cp.wait_send()
        for cp in mine:
            cp.wait()

    return pl.pallas_call(
        body, name=name,
        in_specs=[ANY] * n, out_specs=[ANY] * n,
        out_shape=[jax.ShapeDtypeStruct((N_DEV,) + a.shape, a.dtype) for a in arrs],
        scratch_shapes=[pltpu.SemaphoreType.DMA((7 * n,)), pltpu.SemaphoreType.DMA((7 * n,)),
                        pltpu.SemaphoreType.DMA((n,))],
    )(*arrs)


def _exchange_sibling(arrs, name):
    n = len(arrs)

    def body(*refs):
        ins, outs = refs[:n], refs[n:2 * n]
        send_sems, recv_sems = refs[2 * n:]
        x, y, c = lax.axis_index("x"), lax.axis_index("y"), lax.axis_index("c")
        copies = []
        for i in range(n):
            for k in range(4):
                copies.append(pltpu.make_async_remote_copy(
                    src_ref=ins[i].at[2 * k + (1 - c)], dst_ref=outs[i].at[k],
                    send_sem=send_sems.at[4 * i + k], recv_sem=recv_sems.at[4 * i + k],
                    device_id=(x, y, 1 - c), device_id_type=MESH))
        for cp in copies:
            cp.start()
        for cp in copies:
            cp.wait_recv()
        for cp in copies:
            cp.wait_send()

    return pl.pallas_call(
        body, name=name,
        in_specs=[ANY] * n, out_specs=[ANY] * n,
        out_shape=[jax.ShapeDtypeStruct((4,) + a.shape[1:], a.dtype) for a in arrs],
        scratch_shapes=[pltpu.SemaphoreType.DMA((4 * n,)), pltpu.SemaphoreType.DMA((4 * n,))],
    )(*arrs)


def _exchange_chips(arrs, name):
    n = len(arrs)

    def body(*refs):
        ins, outs = refs[:n], refs[n:2 * n]
        send_sems, recv_sems = refs[2 * n:]
        x, y, c = lax.axis_index("x"), lax.axis_index("y"), lax.axis_index("c")
        chips = [(1 - x, y), (x, 1 - y), (1 - x, 1 - y)]
        copies = []
        for i in range(n):
            for j, (cx, cy) in enumerate(chips):
                copies.append(pltpu.make_async_remote_copy(
                    src_ref=ins[i].at[2 * cx + cy], dst_ref=outs[i].at[j],
                    send_sem=send_sems.at[3 * i + j], recv_sem=recv_sems.at[3 * i + j],
                    device_id=(cx, cy, c), device_id_type=MESH))
        for cp in copies:
            cp.start()
        for cp in copies:
            cp.wait_recv()
        for cp in copies:
            cp.wait_send()

    return pl.pallas_call(
        body, name=name,
        in_specs=[ANY] * n, out_specs=[ANY] * n,
        out_shape=[jax.ShapeDtypeStruct((3,) + a.shape[1:], a.dtype) for a in arrs],
        scratch_shapes=[pltpu.SemaphoreType.DMA((3 * n,)), pltpu.SemaphoreType.DMA((3 * n,))],
    )(*arrs)


def _add_sibling(a, r, pos, name):
    _, depth, rows, cols = a.shape
    tr = _tile(rows, 256) if rows % 8 == 0 else rows
    a3 = a.reshape(N_DEV * depth, rows, cols)
    r3 = r.reshape(4 * depth, rows, cols)

    def body(pos_ref, a_ref, r_ref, o_ref):
        o_ref[...] = a_ref[...] + r_ref[...]

    out = pl.pallas_call(
        body, name=name,
        grid_spec=pltpu.PrefetchScalarGridSpec(
            num_scalar_prefetch=1, grid=(4 * depth, rows // tr),
            in_specs=[pl.BlockSpec((1, tr, cols), lambda q, i, p: ((2 * (q // depth) + p[0]) * depth + q % depth, i, 0)),
                      pl.BlockSpec((1, tr, cols), lambda q, i, p: (q, i, 0))],
            out_specs=pl.BlockSpec((1, tr, cols), lambda q, i, p: (q, i, 0))),
        out_shape=jax.ShapeDtypeStruct((4 * depth, rows, cols), F32),
        compiler_params=_cp("parallel", "parallel"),
    )(pos, a3, r3)
    return out.reshape(4, depth, rows, cols)


def _adamw_math(w, g, m, v):
    m = ADAM_B1 * m + (1.0 - ADAM_B1) * g
    v = ADAM_B2 * v + (1.0 - ADAM_B2) * (g * g)
    m_hat = m / (1.0 - ADAM_B1 ** ADAM_STEP)
    v_hat = v / (1.0 - ADAM_B2 ** ADAM_STEP)
    delta = -ADAM_LR * (m_hat / (jnp.sqrt(v_hat) + ADAM_EPS) + ADAM_WD * w)
    return delta, m, v


def _adamw_sharded(parts, w, m, v, pos, name):
    depth, rows, cols = w.shape
    tr = _tile(rows, 256) if rows % 8 == 0 else rows
    npart = len(parts)

    def body(pos_ref, *refs):
        prefs = refs[:npart]
        w_ref, m_ref, v_ref, g_out, d_out, m_out, v_out = refs[npart:]
        g = prefs[0][...]
        for pr in prefs[1:]:
            g = g + pr[...]
        delta, mn, vn = _adamw_math(w_ref[...], g, m_ref[...], v_ref[...])
        g_out[...] = g
        d_out[...] = delta
        m_out[...] = mn
        v_out[...] = vn

    def part_spec(fn):
        return pl.BlockSpec((1, tr, cols), lambda l, i, p: (fn(p) * depth + l, i, 0))

    blk = pl.BlockSpec((1, tr, cols), lambda l, i, p: (l, i, 0))
    shp = jax.ShapeDtypeStruct((depth, rows, cols), F32)
    return pl.pallas_call(
        body, name=name,
        grid_spec=pltpu.PrefetchScalarGridSpec(
            num_scalar_prefetch=1, grid=(depth, rows // tr),
            in_specs=[part_spec(fn) for _, fn in parts] + [blk, blk, blk],
            out_specs=[blk, blk, blk, blk]),
        out_shape=[shp, shp, shp, shp],
        compiler_params=_cp("parallel", "parallel"),
    )(pos, *[a for a, _ in parts], w, m, v)


_P1024 = ["norm1_g", "norm2_g", "ssd_norm_g", "gm_vnorm_g", "gm_out_g"]
_P16 = ["ssd_dt_bias", "ssd_a_log", "ssd_d"]


def _adamw_small(gath, wmv):
    names = list(wmv.keys())
    classes = list(gath.keys())
    flat_in = [gath[k] for k in classes]
    for nme in names:
        flat_in += list(wmv[nme])
    out_shapes = []
    for nme in names:
        out_shapes += [jax.ShapeDtypeStruct(wmv[nme][0].shape, F32)] * 4
    out_shapes += [jax.ShapeDtypeStruct((DEPTH, SSD_CONV, CONV_DIM), F32), jax.ShapeDtypeStruct((DEPTH, FF_CONV, D_FF), F32)]
    scratch = [pltpu.VMEM(gath[k].shape[1:], F32) for k in classes]
    ncls = len(classes)

    def body(*refs):
        g_refs = dict(zip(classes, refs[:ncls]))
        pos = ncls
        w_refs = {}
        for nme in names:
            w_refs[nme] = refs[pos:pos + 3]
            pos += 3
        o_refs = {}
        for nme in names:
            o_refs[nme] = refs[pos:pos + 4]
            pos += 4
        scw_out, fcw_out = refs[pos], refs[pos + 1]
        s_refs = dict(zip(classes, refs[pos + 2:]))
        for k in classes:
            acc = g_refs[k][0]
            for dev in range(1, N_DEV):
                acc = acc + g_refs[k][dev]
            s_refs[k][...] = acc

        def apply(nme, grad_of):
            w_ref, m_ref, v_ref = w_refs[nme]
            g_out, d_out, m_out, v_out = o_refs[nme]
            shape = w_ref.shape
            if len(shape) == 2:
                idxs = [(slice(l, l + 1),) for l in range(shape[0])]
            elif len(shape) == 3:
                idxs = [(l,) for l in range(shape[0])]
            else:
                idxs = [(l, h) for l in range(shape[0]) for h in range(shape[1])]
            for n_i, ix in enumerate(idxs):
                g = grad_of(n_i)
                delta, mn, vn = _adamw_math(w_ref[ix], g, m_ref[ix], v_ref[ix])
                g_out[ix] = g
                d_out[ix] = delta
                m_out[ix] = mn
                v_out[ix] = vn

        s1024, s1536, s2816, s16, s128, s6144 = (s_refs[k] for k in classes)
        for n_i, nme in enumerate(_P1024):
            apply(nme, lambda l, b=2 * n_i: s1024[b + l:b + l + 1, :])
        apply("final_g", lambda l: s1024[10:11, :])
        apply("ssd_conv_b", lambda l: s1536[8 + l:9 + l, :])
        apply("ff_conv_b", lambda l: s2816[6 + l:7 + l, :])
        for n_i, nme in enumerate(_P16):
            apply(nme, lambda l, b=2 * n_i: s16[b + l:b + l + 1, :])
        apply("gm_ws", lambda q: s128[q * CHUNK:(q + 1) * CHUNK, :])
        apply("gm_bs", lambda l: s128[2048 + 8 * l:2048 + 8 * (l + 1), :])
        apply("ada_b", lambda l: s6144[2 * l:2 * l + 1, :] + s6144[2 * l + 1:2 * l + 2, :])
        for l in range(DEPTH):
            scw_out[l] = s1536[SSD_CONV * l:SSD_CONV * (l + 1), :]
            fcw_out[l] = s2816[FF_CONV * l:FF_CONV * (l + 1), :]

    outs = pl.pallas_call(
        body, name="adamw_small",
        out_shape=out_shapes,
        scratch_shapes=scratch,
        compiler_params=pltpu.CompilerParams(vmem_limit_bytes=VMEM_LIMIT),
    )(*flat_in)
    res = {nme: tuple(outs[4 * i:4 * i + 4]) for i, nme in enumerate(names)}
    return res, outs[-2], outs[-1]


_WEIGHTS = ['ada_w', 'ada_b', 'norm1_g', 'norm2_g', 'w_in', 'ssd_conv_w', 'ssd_conv_b', 'ssd_dt_bias', 'ssd_a_log',
            'ssd_d', 'ssd_norm_g', 'gm_vnorm_g', 'gm_ws', 'gm_bs', 'gm_out_g', 'w_out', 'ff_up', 'ff_conv_w',
            'ff_conv_b', 'ff_down', 'final_g']


def _unshard_cols(g):
    _, depth, k, n = g.shape
    return jnp.transpose(g, (1, 2, 0, 3)).reshape(depth, k, N_DEV * n)


def _shard_cols(full):
    depth, k, n8 = full.shape
    return jnp.transpose(full.reshape(depth, k, N_DEV, n8 // N_DEV), (2, 0, 1, 3))


def _shard_rows(full):
    depth, r8, n = full.shape
    return jnp.transpose(full.reshape(depth, N_DEV, r8 // N_DEV, n), (1, 0, 2, 3))


def kernel(x, c, ada_w, ada_b, norm1_g, norm2_g, w_in, ssd_conv_w, ssd_conv_b, ssd_dt_bias, ssd_a_log, ssd_d, ssd_norm_g, gm_vnorm_g, gm_ws, gm_bs, gm_out_g, w_out, ff_up, ff_conv_w, ff_conv_b, ff_down, final_g, loss_target, m_ada_w, m_ada_b, m_norm1_g, m_norm2_g, m_w_in, m_ssd_conv_w, m_ssd_conv_b, m_ssd_dt_bias, m_ssd_a_log, m_ssd_d, m_ssd_norm_g, m_gm_vnorm_g, m_gm_ws, m_gm_bs, m_gm_out_g, m_w_out, m_ff_up, m_ff_conv_w, m_ff_conv_b, m_ff_down, m_final_g, v_ada_w, v_ada_b, v_norm1_g, v_norm2_g, v_w_in, v_ssd_conv_w, v_ssd_conv_b, v_ssd_dt_bias, v_ssd_a_log, v_ssd_d, v_ssd_norm_g, v_gm_vnorm_g, v_gm_ws, v_gm_bs, v_gm_out_g, v_w_out, v_ff_up, v_ff_conv_w, v_ff_conv_b, v_ff_down, v_final_g):
    given = dict(locals())
    wts = {n: given[n] for n in _WEIGHTS}
    mom = {n: given["m_" + n] for n in _WEIGHTS}
    var = {n: given["v_" + n] for n in _WEIGHTS}
    nseq, seq, d = x.shape
    ix, iy, ic = lax.axis_index("x"), lax.axis_index("y"), lax.axis_index("c")
    me = 4 * ix + 2 * iy + ic
    chip = (2 * ix + iy).astype(jnp.int32).reshape(1)
    core = ic.astype(jnp.int32).reshape(1)

    g_win, g_wout, g_ffup, g_ffdown, g_scw, g_fcw, c_all = _all_gather(
        [_b(w_in), _b(w_out), _b(ff_up), _b(ff_down), ssd_conv_w, ff_conv_w, c], "gather_weights")
    w_in_f = _unshard_cols(g_win)
    zpad = jnp.zeros((DEPTH, d, N_INP - N_IN), BF16)
    o_xbc, o_dt, o_gm = D_SSD, D_SSD + CONV_DIM, D_SSD + CONV_DIM + SSD_HEADS
    w_in_p = jnp.concatenate([w_in_f[:, :, o_gm:], w_in_f[:, :, :o_xbc], w_in_f[:, :, o_xbc:o_dt],
                              w_in_f[:, :, o_dt:o_gm], zpad], axis=2)
    w_out_f = jnp.transpose(g_wout, (1, 0, 2, 3)).reshape(DEPTH, 2 * D_SSD, d)
    ff_up_f = _unshard_cols(g_ffup)
    ff_down_f = jnp.transpose(g_ffdown, (1, 0, 2, 3)).reshape(DEPTH, D_FF, d)
    scw_f = _unshard_cols(g_scw)
    fcw_f = _unshard_cols(g_fcw)
    c_all = c_all.reshape(N_DEV * nseq, d)

    n_ada = ada_w.shape[2]
    ada_b_shard = lax.dynamic_slice_in_dim(ada_b, me * n_ada, n_ada, axis=1).reshape(DEPTH, 1, n_ada)
    mod_part, c_act = _ada_fwd(c_all, ada_w, ada_b_shard)
    (mod_g,) = _all_gather([mod_part], "gather_mod")
    mod_all = jnp.transpose(mod_g, (1, 2, 0, 3)).reshape(DEPTH, N_DEV * nseq, N_MOD * d)
    mod_mine = lax.dynamic_slice_in_dim(mod_all, me * nseq, nseq, axis=1)
    mods = [[mod_mine[l, :, k * d:(k + 1) * d].reshape(nseq, 1, d) for k in range(N_MOD)] for l in range(DEPTH)]

    lw = []
    for l in range(DEPTH):
        lw.append(dict(
            norm1_g=norm1_g[l:l + 1], norm2_g=norm2_g[l:l + 1], w_in=w_in_p[l], ssd_conv_w=scw_f[l],
            ssd_conv_b=ssd_conv_b[l:l + 1], ssd_dt_bias=ssd_dt_bias[l:l + 1], ssd_a_log=ssd_a_log[l:l + 1],
            ssd_d=ssd_d[l:l + 1], ssd_norm_g=ssd_norm_g[l:l + 1], gm_vnorm_g=gm_vnorm_g[l:l + 1], gm_ws=gm_ws[l],
            gm_bst=gm_bs[l].T, gm_out_g=gm_out_g[l:l + 1], w_out=w_out_f[l], ff_up=ff_up_f[l], ff_conv_w=fcw_f[l],
            ff_conv_b=ff_conv_b[l:l + 1], ff_down=ff_down_f[l]))

    loss_p, grad_x, big, small, dmods, dfg = _local_step(
        x.reshape(nseq * seq, d), loss_target.reshape(nseq * seq, d), mods, lw, final_g.reshape(1, d), nseq=nseq)
    loss = lax.psum(loss_p[0, 0], ("x", "y", "c"))

    gw_in_p = jnp.stack([big[l]["w_in"] for l in range(DEPTH)])
    gw_in = jnp.concatenate([gw_in_p[:, :, COL_Z:COL_XBC], gw_in_p[:, :, COL_XBC:COL_DT],
                             gw_in_p[:, :, COL_DT:COL_DT + SSD_HEADS], gw_in_p[:, :, :COL_Z]], axis=2)
    full = [_shard_cols(gw_in),
            _shard_rows(jnp.stack([big[l]["w_out"] for l in range(DEPTH)])),
            _shard_cols(jnp.stack([big[l]["ff_up"] for l in range(DEPTH)])),
            _shard_rows(jnp.stack([big[l]["ff_down"] for l in range(DEPTH)]))]
    from_sib = _exchange_sibling(full, "rs_sibling")
    chip_sums = [_add_sibling(a, r, core, f"rs_add_{i}") for i, (a, r) in enumerate(zip(full, from_sib))]
    from_chips = _exchange_chips(chip_sums, "rs_chips")

    outs = {}
    for i, nme in enumerate(["w_in", "w_out", "ff_up", "ff_down"]):
        s1 = chip_sums[i]
        r3 = from_chips[i]
        depth_rows = s1.shape[2:]
        s1f = s1.reshape((4 * DEPTH,) + depth_rows)
        r3f = r3.reshape((3 * DEPTH,) + depth_rows)
        parts = [(s1f, lambda p: p[0]), (r3f, lambda p: 0), (r3f, lambda p: 1), (r3f, lambda p: 2)]
        outs[nme] = _adamw_sharded(parts, wts[nme], mom[nme], var[nme], chip, f"adamw_{nme}")

    def rows(name):
        return [small[l][name] for l in range(DEPTH)]

    p1024 = jnp.concatenate(sum([rows(n) for n in _P1024], []) + [dfg], axis=0)
    p1536 = jnp.concatenate(rows("ssd_conv_w") + rows("ssd_conv_b"), axis=0)
    p2816 = jnp.concatenate(rows("ff_conv_w") + rows("ff_conv_b"), axis=0)
    p16 = jnp.concatenate(sum([rows(n) for n in _P16], []), axis=0)
    p128 = jnp.concatenate([small[l]["gm_ws"].reshape(GM_HEADS * CHUNK, CHUNK) for l in range(DEPTH)] + rows("gm_bs"), axis=0)
    p6144 = jnp.concatenate(dmods, axis=0)
    gathered = _all_gather([p1024, p1536, p2816, p16, p128, p6144], "gather_small")
    gath = dict(zip(["p1024", "p1536", "p2816", "p16", "p128", "p6144"], gathered))

    dmod_all = jnp.transpose(gath["p6144"].reshape(N_DEV, DEPTH, nseq, N_MOD * d), (1, 0, 2, 3)).reshape(
        DEPTH, N_DEV * nseq, N_MOD * d)
    small_names = _P1024 + ["final_g", "ssd_conv_b", "ff_conv_b"] + _P16 + ["gm_ws", "gm_bs", "ada_b"]
    wmv = {}
    for nme in small_names:
        if nme == "final_g":
            wmv[nme] = tuple(a.reshape(1, d) for a in (wts[nme], mom[nme], var[nme]))
        else:
            wmv[nme] = (wts[nme], mom[nme], var[nme])
    small_out, scw_full, fcw_full = _adamw_small(gath, wmv)
    for nme in small_names:
        outs[nme] = small_out[nme]
    outs["final_g"] = tuple(a.reshape(d) for a in outs["final_g"])

    n_scw, n_fcw = ssd_conv_w.shape[2], ff_conv_w.shape[2]
    g_scw_mine = lax.dynamic_slice_in_dim(scw_full, me * n_scw, n_scw, axis=2)
    g_fcw_mine = lax.dynamic_slice_in_dim(fcw_full, me * n_fcw, n_fcw, axis=2)
    outs["ssd_conv_w"] = _adamw_sharded([(g_scw_mine, lambda p: 0)], ssd_conv_w, m_ssd_conv_w, v_ssd_conv_w, chip, "adamw_ssd_conv_w")
    outs["ff_conv_w"] = _adamw_sharded([(g_fcw_mine, lambda p: 0)], ff_conv_w, m_ff_conv_w, v_ff_conv_w, chip, "adamw_ff_conv_w")

    dmod_cols = _b(lax.dynamic_slice_in_dim(dmod_all, me * n_ada, n_ada, axis=2))
    g_ada = jnp.stack([_matmul(c_act, dmod_cols[l], ta=True, name=f"mm_ada_dw_{l}") for l in range(DEPTH)])
    outs["ada_w"] = _adamw_sharded([(g_ada, lambda p: 0)], ada_w, m_ada_w, v_ada_w, chip, "adamw_ada_w")

    result = [loss, grad_x.reshape(nseq, seq, d)]
    for k in range(4):
        result += [outs[n][k] for n in _WEIGHTS]
    return tuple(result)
```

```python
import functools
import math

import jax
import jax.numpy as jnp
from jax import lax
from jax.experimental import pallas as pl
from jax.experimental.pallas import tpu as pltpu

F32 = jnp.float32
BF16 = jnp.bfloat16

N_DEV = 8
D_MODEL = 1024
DEPTH = 2
CHUNK = 128
SSD_HEADS = 16
SSD_HEAD_DIM = 64
SSD_GROUPS = 2
HEADS_PER_GROUP = SSD_HEADS // SSD_GROUPS
GROUP_WIDTH = HEADS_PER_GROUP * SSD_HEAD_DIM
D_STATE = 128
D_SSD = 1024
CONV_DIM = 1536
SSD_CONV = 4
GM_HEADS = 8
GM_HEAD_DIM = 128
D_GM = 1024
D_FF = 2816
FF_CONV = 3
N_IN = 4624
N_MOD = 6
EPS = 1e-6

N_INP = 5120
COL_U, COL_V, COL_Z, COL_XBC, COL_DT = 0, 1024, 2048, 3072, 4608
DT_BLOCK = 512

ADAM_LR = 0.001
ADAM_B1 = 0.9
ADAM_B2 = 0.999
ADAM_EPS = 1e-08
ADAM_WD = 0.01
ADAM_STEP = 10

VMEM_LIMIT = 56 * 1024 * 1024
MESH = pl.DeviceIdType.MESH
ANY = pl.BlockSpec(memory_space=pl.ANY)


def _cp(*sem):
    return pltpu.CompilerParams(dimension_semantics=sem, vmem_limit_bytes=VMEM_LIMIT)


def _tile(n, pref):
    if n <= pref or n % 128:
        return n
    best = 128
    for t in range(128, pref + 1, 128):
        if n % t == 0:
            best = t
    return best


def _silu(x):
    return x * jax.nn.sigmoid(x)


def _gelu(x):
    return 0.5 * x * (1.0 + lax.erf(x * (1.0 / math.sqrt(2.0))))


def _softplus(x):
    return jnp.maximum(x, 0.0) + jnp.log1p(jnp.exp(-jnp.abs(x)))


def _rms(x, g, width):
    return x * lax.rsqrt(jnp.sum(x * x, axis=-1, keepdims=True) / width + EPS) * g


def _b(x):
    return x.astype(BF16)


_NN = (((1,), (0,)), ((), ()))
_NT = (((1,), (1,)), ((), ()))
_TN = (((0,), (0,)), ((), ()))


def _dg(a, b, dn):
    return lax.dot_general(_b(a), _b(b), dn, preferred_element_type=F32)


@jax.custom_vjp
def _bdot(a, b):
    return _dg(a, b, _NN)


def _bdot_fwd(a, b):
    return _dg(a, b, _NN), (a, b)


def _bdot_bwd(res, ct):
    a, b = res
    return _dg(ct, b, _NT), _dg(a, ct, _TN)


_bdot.defvjp(_bdot_fwd, _bdot_bwd)


@jax.custom_vjp
def _bdot_nt(a, b):
    return _dg(a, b, _NT)


def _bdot_nt_fwd(a, b):
    return _dg(a, b, _NT), (a, b)


def _bdot_nt_bwd(res, ct):
    a, b = res
    return _dg(ct, b, _NN), _dg(ct, a, _TN)


_bdot_nt.defvjp(_bdot_nt_fwd, _bdot_nt_bwd)


@jax.custom_vjp
def _bdot_tn(a, b):
    return _dg(a, b, _TN)


def _bdot_tn_fwd(a, b):
    return _dg(a, b, _TN), (a, b)


def _bdot_tn_bwd(res, ct):
    a, b = res
    return _dg(b, ct, _NT), _dg(a, ct, _NN)


_bdot_tn.defvjp(_bdot_tn_fwd, _bdot_tn_bwd)


def _tri(n, lower):
    r = lax.broadcasted_iota(jnp.int32, (n, n), 0)
    c = lax.broadcasted_iota(jnp.int32, (n, n), 1)
    return ((r >= c) if lower else (r <= c)).astype(F32)


def _eye(n):
    r = lax.broadcasted_iota(jnp.int32, (n, n), 0)
    c = lax.broadcasted_iota(jnp.int32, (n, n), 1)
    return (r == c).astype(F32)


def _hdot(a, b, dn):
    return lax.dot_general(a, b, dn, precision=lax.Precision.HIGHEST, preferred_element_type=F32)


@jax.custom_vjp
def _cumsum_rows(x):
    return _hdot(_tri(x.shape[0], True), x, _NN)


def _cumsum_rows_fwd(x):
    return _cumsum_rows(x), None


def _cumsum_rows_bwd(_, ct):
    return (_hdot(_tri(ct.shape[0], False), ct, _NN),)


_cumsum_rows.defvjp(_cumsum_rows_fwd, _cumsum_rows_bwd)


@jax.custom_vjp
def _transpose(x):
    return _hdot(_eye(x.shape[1]), x, _NT)


def _transpose_fwd(x):
    return _transpose(x), None


def _transpose_bwd(_, ct):
    return (_hdot(_eye(ct.shape[1]), ct, _NT),)


_transpose.defvjp(_transpose_fwd, _transpose_bwd)


def _matmul(a, b, *, ta=False, tb=False, out_dtype=F32, name):
    if ta:
        k_dim, m_dim = a.shape
    else:
        m_dim, k_dim = a.shape
    if tb:
        n_dim, kb = b.shape
    else:
        kb, n_dim = b.shape
    assert kb == k_dim, (a.shape, b.shape, ta, tb)
    assert out_dtype == F32
    tm, tn, tk = _tile(m_dim, 1024), _tile(n_dim, 1536), _tile(k_dim, 1536)
    ni, nj, nk = m_dim // tm, n_dim // tn, k_dim // tk
    dn = (((0 if ta else 1,), (1 if tb else 0,)), ((), ()))

    def body(a_ref, b_ref, o_ref):
        k = pl.program_id(2)
        p = lax.dot_general(a_ref[...], b_ref[...], dn, preferred_element_type=F32)
        if nk == 1:
            o_ref[...] = p
        else:
            @pl.when(k == 0)
            def _():
                o_ref[...] = p

            @pl.when(k > 0)
            def _():
                o_ref[...] += p

    a_bytes, b_bytes = m_dim * k_dim, k_dim * n_dim
    m_outer = nk > 1 or a_bytes + b_bytes * ni <= b_bytes + a_bytes * nj
    if m_outer:
        ij = lambda o, n, k: (o, n)
        grid = (ni, nj, nk)
    else:
        ij = lambda o, n, k: (n, o)
        grid = (nj, ni, nk)

    def a_map(o, n, k):
        i, _ = ij(o, n, k)
        return (k, i) if ta else (i, k)

    def b_map(o, n, k):
        _, j = ij(o, n, k)
        return (j, k) if tb else (k, j)

    return pl.pallas_call(
        body, name=name,
        grid=grid,
        in_specs=[pl.BlockSpec((tk, tm) if ta else (tm, tk), a_map),
                  pl.BlockSpec((tn, tk) if tb else (tk, tn), b_map)],
        out_specs=pl.BlockSpec((tm, tn), lambda o, n, k: ij(o, n, k)),
        out_shape=jax.ShapeDtypeStruct((m_dim, n_dim), out_dtype),
        compiler_params=_cp("parallel", "parallel", "arbitrary"),
    )(a, b)


def _ada_fwd(c_all, ada_w, ada_b_shard):
    depth, d, n = ada_w.shape
    nb = c_all.shape[0]

    def body(c_ref, w_ref, b_ref, o_ref, ca_ref):
        ca = _silu(c_ref[...])
        ca_ref[...] = _b(ca)
        o_ref[0] = _dg(ca, w_ref[0], _NN) + b_ref[0]

    return pl.pallas_call(
        body, name="ada_fwd",
        grid=(depth,),
        in_specs=[pl.BlockSpec((nb, d), lambda l: (0, 0)),
                  pl.BlockSpec((1, d, n), lambda l: (l, 0, 0)),
                  pl.BlockSpec((1, 1, n), lambda l: (l, 0, 0))],
        out_specs=[pl.BlockSpec((1, nb, n), lambda l: (l, 0, 0)),
                   pl.BlockSpec((nb, d), lambda l: (0, 0))],
        out_shape=[jax.ShapeDtypeStruct((depth, nb, n), F32), jax.ShapeDtypeStruct((nb, d), BF16)],
        compiler_params=_cp("arbitrary"),
    )(c_all, ada_w, ada_b_shard)


def _normmod_f(x, g, sc, sh):
    return _rms(x, g, D_MODEL) * (1.0 + sc) + sh


def _row_tile(seq):
    return min(seq, 256)


def _normmod_fwd(xin, delta, gate, g, sc, sh, *, nseq, name):
    t, d = xin.shape
    seq = t // nseq
    tr = _row_tile(seq)
    nt = seq // tr
    has_delta = delta is not None
    row = pl.BlockSpec((tr, d), lambda s, i: (s * nt + i, 0))
    per_seq = pl.BlockSpec((1, 1, d), lambda s, i: (s, 0, 0))
    vec = pl.BlockSpec((1, d), lambda s, i: (0, 0))

    if has_delta:
        def body(xin_ref, delta_ref, gate_ref, g_ref, sc_ref, sh_ref, x_ref, h_ref):
            x = xin_ref[...] + gate_ref[0] * delta_ref[...]
            x_ref[...] = x
            h_ref[...] = _b(_normmod_f(x, g_ref[...], sc_ref[0], sh_ref[0]))

        return pl.pallas_call(
            body, name=name, grid=(nseq, nt),
            in_specs=[row, row, per_seq, vec, per_seq, per_seq],
            out_specs=[row, row],
            out_shape=[jax.ShapeDtypeStruct((t, d), F32), jax.ShapeDtypeStruct((t, d), BF16)],
            compiler_params=_cp("parallel", "parallel"),
        )(xin, delta, gate, g, sc, sh)

    def body0(xin_ref, g_ref, sc_ref, sh_ref, h_ref):
        h_ref[...] = _b(_normmod_f(xin_ref[...], g_ref[...], sc_ref[0], sh_ref[0]))

    h = pl.pallas_call(
        body0, name=name, grid=(nseq, nt),
        in_specs=[row, vec, per_seq, per_seq],
        out_specs=row,
        out_shape=jax.ShapeDtypeStruct((t, d), BF16),
        compiler_params=_cp("parallel", "parallel"),
    )(xin, g, sc, sh)
    return xin, h


def _normmod_bwd(dh, dxo, x, delta, gate, g, sc, *, nseq, name):
    t, d = x.shape
    seq = t // nseq
    tr = _row_tile(seq)
    nt = seq // tr
    has_delta = delta is not None
    row = pl.BlockSpec((tr, d), lambda s, i: (s * nt + i, 0))
    per_seq = pl.BlockSpec((1, 1, d), lambda s, i: (s, 0, 0))
    vec = pl.BlockSpec((1, d), lambda s, i: (0, 0))

    def core(dh_ref, dxo_ref, x_ref, g_ref, sc_ref, dx_ref, dg_ref, dsc_ref, dsh_ref):
        s, i = pl.program_id(0), pl.program_id(1)
        dh_v = dh_ref[...]
        _, vjp = jax.vjp(lambda xx, gg, ss: _normmod_f(xx, gg, ss, 0.0), x_ref[...], g_ref[...], sc_ref[0])
        dxn, dg_t, dsc_t = vjp(dh_v)
        dx = dxo_ref[...] + dxn
        dx_ref[...] = dx
        dsh_t = jnp.sum(dh_v, axis=0, keepdims=True)

        @pl.when((s == 0) & (i == 0))
        def _():
            dg_ref[...] = jnp.zeros_like(dg_ref)

        @pl.when(i == 0)
        def _():
            dsc_ref[...] = jnp.zeros_like(dsc_ref)
            dsh_ref[...] = jnp.zeros_like(dsh_ref)

        dg_ref[...] += dg_t
        dsc_ref[0] += dsc_t
        dsh_ref[0] += dsh_t
        return dx

    if has_delta:
        def body(dh_ref, dxo_ref, x_ref, delta_ref, gate_ref, g_ref, sc_ref,
                 dx_ref, dd_ref, dgate_ref, dg_ref, dsc_ref, dsh_ref):
            dx = core(dh_ref, dxo_ref, x_ref, g_ref, sc_ref, dx_ref, dg_ref, dsc_ref, dsh_ref)
            dd_ref[...] = _b(dx * gate_ref[0])

            @pl.when(pl.program_id(1) == 0)
            def _():
                dgate_ref[...] = jnp.zeros_like(dgate_ref)

            dgate_ref[0] += jnp.sum(dx * delta_ref[...], axis=0, keepdims=True)

        return pl.pallas_call(
            body, name=name, grid=(nseq, nt),
            in_specs=[row, row, row, row, per_seq, vec, per_seq],
            out_specs=[row, row, per_seq, vec, per_seq, per_seq],
            out_shape=[jax.ShapeDtypeStruct((t, d), F32), jax.ShapeDtypeStruct((t, d), BF16),
                       jax.ShapeDtypeStruct((nseq, 1, d), F32), jax.ShapeDtypeStruct((1, d), F32),
                       jax.ShapeDtypeStruct((nseq, 1, d), F32), jax.ShapeDtypeStruct((nseq, 1, d), F32)],
            compiler_params=_cp("arbitrary", "arbitrary"),
        )(dh, dxo, x, delta, gate, g, sc)

    def body0(dh_ref, dxo_ref, x_ref, g_ref, sc_ref, dx_ref, dg_ref, dsc_ref, dsh_ref):
        core(dh_ref, dxo_ref, x_ref, g_ref, sc_ref, dx_ref, dg_ref, dsc_ref, dsh_ref)

    dx, dg, dsc, dsh = pl.pallas_call(
        body0, name=name, grid=(nseq, nt),
        in_specs=[row, row, row, vec, per_seq],
        out_specs=[row, vec, per_seq, per_seq],
        out_shape=[jax.ShapeDtypeStruct((t, d), F32), jax.ShapeDtypeStruct((1, d), F32),
                   jax.ShapeDtypeStruct((nseq, 1, d), F32), jax.ShapeDtypeStruct((nseq, 1, d), F32)],
        compiler_params=_cp("arbitrary", "arbitrary"),
    )(dh, dxo, x, g, sc)
    return dx, None, None, dg, dsc, dsh


def _final_loss(xin, delta, gate, fg, target, *, nseq):
    t, d = xin.shape
    seq = t // nseq
    tr = _row_tile(seq)
    nt = seq // tr
    row = pl.BlockSpec((tr, d), lambda s, i: (s * nt + i, 0))
    per_seq = pl.BlockSpec((1, 1, d), lambda s, i: (s, 0, 0))
    vec = pl.BlockSpec((1, d), lambda s, i: (0, 0))

    def body(xin_ref, delta_ref, gate_ref, fg_ref, tgt_ref, loss_ref, dx_ref, dd_ref, dgate_ref, dfg_ref):
        s, i = pl.program_id(0), pl.program_id(1)
        dl = delta_ref[...]
        x = xin_ref[...] + gate_ref[0] * dl
        y, vjp = jax.vjp(lambda xx, gg: _rms(xx, gg, D_MODEL), x, fg_ref[...])
        err = y - tgt_ref[...]
        dx, dfg_t = vjp(err * (1.0 / d))
        dx_ref[...] = dx
        dd_ref[...] = _b(dx * gate_ref[0])

        @pl.when((s == 0) & (i == 0))
        def _():
            loss_ref[...] = jnp.zeros_like(loss_ref)
            dfg_ref[...] = jnp.zeros_like(dfg_ref)

        @pl.when(i == 0)
        def _():
            dgate_ref[...] = jnp.zeros_like(dgate_ref)

        loss_ref[...] += jnp.sum(err * err) * (0.5 / d)
        dfg_ref[...] += dfg_t
        dgate_ref[0] += jnp.sum(dx * dl, axis=0, keepdims=True)

    return pl.pallas_call(
        body, name="final_loss", grid=(nseq, nt),
        in_specs=[row, row, per_seq, vec, row],
        out_specs=[pl.BlockSpec((1, 128), lambda s, i: (0, 0)), row, row, per_seq, vec],
        out_shape=[jax.ShapeDtypeStruct((1, 128), F32), jax.ShapeDtypeStruct((t, d), F32),
                   jax.ShapeDtypeStruct((t, d), BF16), jax.ShapeDtypeStruct((nseq, 1, d), F32),
                   jax.ShapeDtypeStruct((1, d), F32)],
        compiler_params=_cp("arbitrary", "arbitrary"),
    )(xin, delta, gate, fg, target)


def _shift_down(x, j):
    if j == 0:
        return x
    rows = lax.broadcasted_iota(jnp.int32, x.shape, 0)
    return jnp.where(rows >= j, pltpu.roll(x, j, 0), 0.0)


def _shift_up(x, j):
    if j == 0:
        return x
    n = x.shape[0]
    rows = lax.broadcasted_iota(jnp.int32, x.shape, 0)
    return jnp.where(rows < n - j, pltpu.roll(x, n - j, 0), 0.0)


def _conv(x, w_ref, b_ref):
    kw = w_ref.shape[0]
    y = b_ref[...] + w_ref[kw - 1:kw, :] * x
    for j in range(1, kw):
        y = y + w_ref[kw - 1 - j:kw - j, :] * _shift_down(x, j)
    return y


def _conv_bwd(dy, x, w_ref, dw_ref, db_ref):
    kw = w_ref.shape[0]
    dx = w_ref[kw - 1:kw, :] * dy
    dw_ref[kw - 1:kw, :] += jnp.sum(dy * x, axis=0, keepdims=True)
    for j in range(1, kw):
        dx = dx + w_ref[kw - 1 - j:kw - j, :] * _shift_up(dy, j)
        dw_ref[kw - 1 - j:kw - j, :] += jnp.sum(dy * _shift_down(x, j), axis=0, keepdims=True)
    db_ref[...] += jnp.sum(dy, axis=0, keepdims=True)
    return dx


CONV_TC = 256


def _ssd_conv_fwd(proj, w, b, *, nseq):
    t = proj.shape[0]
    seq = t // nseq
    nb = CONV_DIM // CONV_TC
    off = COL_XBC // CONV_TC

    def body(x_ref, w_ref, b_ref, o_ref):
        o_ref[...] = _silu(_conv(x_ref[...], w_ref, b_ref))

    return pl.pallas_call(
        body, name="ssd_conv_fwd", grid=(nb, nseq),
        in_specs=[pl.BlockSpec((seq, CONV_TC), lambda j, s: (s, off + j)),
                  pl.BlockSpec((SSD_CONV, CONV_TC), lambda j, s: (0, j)),
                  pl.BlockSpec((1, CONV_TC), lambda j, s: (0, j))],
        out_specs=pl.BlockSpec((seq, CONV_TC), lambda j, s: (s, j)),
        out_shape=jax.ShapeDtypeStruct((t, CONV_DIM), F32),
        compiler_params=_cp("parallel", "parallel"),
    )(proj, w, b)


def _ssd_conv_bwd(dact, proj, w, b, dproj, *, nseq):
    t = proj.shape[0]
    seq = t // nseq
    nb = CONV_DIM // CONV_TC
    off = COL_XBC // CONV_TC

    def body(da_ref, x_ref, w_ref, b_ref, dproj_ref, dx_ref, dw_ref, db_ref):
        del dproj_ref

        @pl.when(pl.program_id(1) == 0)
        def _():
            dw_ref[...] = jnp.zeros_like(dw_ref)
            db_ref[...] = jnp.zeros_like(db_ref)

        x = x_ref[...]
        pre = _conv(x, w_ref, b_ref)
        sg = jax.nn.sigmoid(pre)
        dpre = da_ref[...] * (sg * (1.0 + pre * (1.0 - sg)))
        dx_ref[...] = _b(_conv_bwd(dpre, x, w_ref, dw_ref, db_ref))

    return pl.pallas_call(
        body, name="ssd_conv_bwd", grid=(nb, nseq),
        in_specs=[pl.BlockSpec((seq, CONV_TC), lambda j, s: (s, j)),
                  pl.BlockSpec((seq, CONV_TC), lambda j, s: (s, off + j)),
                  pl.BlockSpec((SSD_CONV, CONV_TC), lambda j, s: (0, j)),
                  pl.BlockSpec((1, CONV_TC), lambda j, s: (0, j)),
                  ANY],
        out_specs=[pl.BlockSpec((seq, CONV_TC), lambda j, s: (s, off + j)),
                   pl.BlockSpec((SSD_CONV, CONV_TC), lambda j, s: (0, j)),
                   pl.BlockSpec((1, CONV_TC), lambda j, s: (0, j))],
        out_shape=[jax.ShapeDtypeStruct(dproj.shape, dproj.dtype), jax.ShapeDtypeStruct((SSD_CONV, CONV_DIM), F32),
                   jax.ShapeDtypeStruct((1, CONV_DIM), F32)],
        input_output_aliases={4: 0},
        compiler_params=_cp("parallel", "arbitrary"),
    )(dact, proj, w, b, dproj)


def _ffn_act_fwd(up, w, b, *, nseq):
    t = up.shape[0]
    seq = t // nseq
    nb = D_FF // CONV_TC

    def body(up_ref, w_ref, b_ref, o_ref):
        o_ref[...] = _b(_silu(_conv(up_ref[:, :CONV_TC], w_ref, b_ref)) * up_ref[:, CONV_TC:])

    return pl.pallas_call(
        body, name="ffn_act_fwd", grid=(nb, nseq),
        in_specs=[pl.BlockSpec((seq, 2 * CONV_TC), lambda j, s: (s, j)),
                  pl.BlockSpec((FF_CONV, CONV_TC), lambda j, s: (0, j)),
                  pl.BlockSpec((1, CONV_TC), lambda j, s: (0, j))],
        out_specs=pl.BlockSpec((seq, CONV_TC), lambda j, s: (s, j)),
        out_shape=jax.ShapeDtypeStruct((t, D_FF), BF16),
        compiler_params=_cp("parallel", "parallel"),
    )(up, w, b)


def _ffn_act_bwd(dact, up, w, b, *, nseq):
    t = up.shape[0]
    seq = t // nseq
    nb = D_FF // CONV_TC

    def body(da_ref, up_ref, w_ref, b_ref, dup_ref, dw_ref, db_ref):
        @pl.when(pl.program_id(1) == 0)
        def _():
            dw_ref[...] = jnp.zeros_like(dw_ref)
            db_ref[...] = jnp.zeros_like(db_ref)

        gate = up_ref[:, :CONV_TC]
        pre = _conv(gate, w_ref, b_ref)
        sg = jax.nn.sigmoid(pre)
        da = da_ref[...]
        dup_ref[:, CONV_TC:] = _b(da * (pre * sg))
        dpre = da * up_ref[:, CONV_TC:] * (sg * (1.0 + pre * (1.0 - sg)))
        dup_ref[:, :CONV_TC] = _b(_conv_bwd(dpre, gate, w_ref, dw_ref, db_ref))

    return pl.pallas_call(
        body, name="ffn_act_bwd", grid=(nb, nseq),
        in_specs=[pl.BlockSpec((seq, CONV_TC), lambda j, s: (s, j)),
                  pl.BlockSpec((seq, 2 * CONV_TC), lambda j, s: (s, j)),
                  pl.BlockSpec((FF_CONV, CONV_TC), lambda j, s: (0, j)),
                  pl.BlockSpec((1, CONV_TC), lambda j, s: (0, j))],
        out_specs=[pl.BlockSpec((seq, 2 * CONV_TC), lambda j, s: (s, j)),
                   pl.BlockSpec((FF_CONV, CONV_TC), lambda j, s: (0, j)),
                   pl.BlockSpec((1, CONV_TC), lambda j, s: (0, j))],
        out_shape=[jax.ShapeDtypeStruct((t, 2 * D_FF), BF16), jax.ShapeDtypeStruct((FF_CONV, D_FF), F32),
                   jax.ShapeDtypeStruct((1, D_FF), F32)],
        compiler_params=_cp("parallel", "arbitrary"),
    )(dact, up, w, b)


def _ssd_chunk(xs, bg, cg, dtr, z, hp, dtb, alog, dskip, ng):
    n = dtr.shape[0]
    dt = _softplus(dtr + dtb)
    cs = _cumsum_rows(dt * (-jnp.exp(alog)))
    cs_t = _transpose(cs)
    lane = lax.broadcasted_iota(jnp.int32, (1, SSD_HEADS), 1)
    sub = lax.broadcasted_iota(jnp.int32, (SSD_HEADS, 1), 0)
    row = lax.broadcasted_iota(jnp.int32, (n, 1), 0)
    r2 = lax.broadcasted_iota(jnp.int32, (n, n), 0)
    c2 = lax.broadcasted_iota(jnp.int32, (n, n), 1)
    causal = r2 >= c2
    cb = [_bdot_nt(cg[g], bg[g]) for g in range(SSD_GROUPS)]
    ys, hn = [], []
    for h in range(SSD_HEADS):
        g = h // HEADS_PER_GROUP
        oh = (lane == h).astype(F32)
        oh_t = (sub == h).astype(F32)
        dt_h = jnp.sum(dt * oh, axis=1, keepdims=True)
        cs_h = jnp.sum(cs * oh, axis=1, keepdims=True)
        d_h = jnp.sum(dskip * oh, axis=1, keepdims=True)
        cs_row = jnp.sum(cs_t * oh_t, axis=0, keepdims=True)
        cs_last = jnp.sum(jnp.where(row == n - 1, cs_h, 0.0), axis=0, keepdims=True)
        decay = jnp.where(causal, jnp.exp(jnp.where(causal, cs_h - cs_row, 0.0)), 0.0)
        xc = xs[h] * dt_h
        y = _bdot(cb[g] * decay, xc)
        y = y + _bdot_nt(cg[g], hp[h]) * jnp.exp(cs_h)
        y = y + d_h * xs[h]
        hn.append(jnp.exp(cs_last) * hp[h] + _bdot_tn(xc * jnp.exp(cs_last - cs_h), bg[g]))
        ys.append(y * _silu(z[h]))
    outs = []
    for g in range(SSD_GROUPS):
        hs = range(g * HEADS_PER_GROUP, (g + 1) * HEADS_PER_GROUP)
        ms = sum(jnp.sum(ys[h] * ys[h], axis=1, keepdims=True) for h in hs) * (1.0 / GROUP_WIDTH)
        r = lax.rsqrt(ms + EPS)
        outs += [ys[h] * r * ng[h] for h in hs]
    return outs, hn


def _hslices(ref, width, count, base=0):
    return [ref[:, base + k * width: base + (k + 1) * width] for k in range(count)]


def _ssd_load(xbc_ref, z_ref, dt_ref, ng_ref):
    xs = _hslices(xbc_ref, SSD_HEAD_DIM, SSD_HEADS)
    bg = _hslices(xbc_ref, D_STATE, SSD_GROUPS, D_SSD)
    cg = _hslices(xbc_ref, D_STATE, SSD_GROUPS, D_SSD + SSD_GROUPS * D_STATE)
    z = _hslices(z_ref, SSD_HEAD_DIM, SSD_HEADS)
    ng = _hslices(ng_ref, SSD_HEAD_DIM, SSD_HEADS)
    return xs, bg, cg, dt_ref[:, 0:SSD_HEADS], z, ng


def _ssd_specs(nch):
    rowi = lambda s, c: s * nch + c
    return [pl.BlockSpec((CHUNK, CONV_DIM), lambda s, c: (rowi(s, c), 0)),
            pl.BlockSpec((CHUNK, D_SSD), lambda s, c: (rowi(s, c), COL_Z // D_SSD)),
            pl.BlockSpec((CHUNK, 128), lambda s, c: (rowi(s, c), COL_DT // 128)),
            pl.BlockSpec((1, SSD_HEADS), lambda s, c: (0, 0)),
            pl.BlockSpec((1, SSD_HEADS), lambda s, c: (0, 0)),
            pl.BlockSpec((1, SSD_HEADS), lambda s, c: (0, 0)),
            pl.BlockSpec((1, D_SSD), lambda s, c: (0, 0))]


def _ssd_fwd(xbc, proj, dtb, alog, dskip, ng, *, nseq):
    t = proj.shape[0]
    nch = t // nseq // CHUNK
    hd = SSD_HEAD_DIM

    def body(xbc_ref, z_ref, dt_ref, dtb_ref, alog_ref, dsk_ref, ng_ref, y_ref, hp_ref, h_ref):
        @pl.when(pl.program_id(1) == 0)
        def _():
            h_ref[...] = jnp.zeros_like(h_ref)

        xs, bg, cg, dtr, z, ngs = _ssd_load(xbc_ref, z_ref, dt_ref, ng_ref)
        hp_ref[0] = h_ref[...]
        hp = [h_ref[h * hd:(h + 1) * hd, :] for h in range(SSD_HEADS)]
        outs, hn = _ssd_chunk(xs, bg, cg, dtr, z, hp, dtb_ref[...], alog_ref[...], dsk_ref[...], ngs)
        for h in range(SSD_HEADS):
            y_ref[:, h * hd:(h + 1) * hd] = _b(outs[h])
            h_ref[h * hd:(h + 1) * hd, :] = hn[h]

    return pl.pallas_call(
        body, name="ssd_fwd", grid=(nseq, nch),
        in_specs=_ssd_specs(nch),
        out_specs=[pl.BlockSpec((CHUNK, D_SSD), lambda s, c: (s * nch + c, 0)),
                   pl.BlockSpec((1, SSD_HEADS * hd, D_STATE), lambda s, c: (s * nch + c, 0, 0))],
        out_shape=[jax.ShapeDtypeStruct((t, D_SSD + D_GM), BF16),
                   jax.ShapeDtypeStruct((t // CHUNK, SSD_HEADS * hd, D_STATE), F32)],
        scratch_shapes=[pltpu.VMEM((SSD_HEADS * hd, D_STATE), F32)],
        compiler_params=_cp("arbitrary", "arbitrary"),
    )(xbc, proj, proj, dtb, alog, dskip, ng)


def _ssd_bwd(dy, xbc, proj, hprev, dtb, alog, dskip, ng, *, nseq):
    t = proj.shape[0]
    nch = t // nseq // CHUNK
    hd = SSD_HEAD_DIM
    rev = lambda s, c: s * nch + (nch - 1 - c)

    def body(dy_ref, xbc_ref, z_ref, dt_ref, hp_ref, dtb_ref, alog_ref, dsk_ref, ng_ref,
             dxbc_ref, dproj_ref, ddtb_ref, dalog_ref, ddsk_ref, dng_ref, dh_ref):
        first = (pl.program_id(0) == 0) & (pl.program_id(1) == 0)

        @pl.when(pl.program_id(1) == 0)
        def _():
            dh_ref[...] = jnp.zeros_like(dh_ref)

        @pl.when(first)
        def _():
            ddtb_ref[...] = jnp.zeros_like(ddtb_ref)
            dalog_ref[...] = jnp.zeros_like(dalog_ref)
            ddsk_ref[...] = jnp.zeros_like(ddsk_ref)
            dng_ref[...] = jnp.zeros_like(dng_ref)

        xs, bg, cg, dtr, z, ngs = _ssd_load(xbc_ref, z_ref, dt_ref, ng_ref)
        hp = [hp_ref[0, h * hd:(h + 1) * hd, :] for h in range(SSD_HEADS)]
        _, vjp = jax.vjp(_ssd_chunk, xs, bg, cg, dtr, z, hp, dtb_ref[...], alog_ref[...], dsk_ref[...], ngs)
        douts = [dy_ref[:, h * hd:(h + 1) * hd] for h in range(SSD_HEADS)]
        dhn = [dh_ref[h * hd:(h + 1) * hd, :] for h in range(SSD_HEADS)]
        dxs, dbg, dcg, ddtr, dz, dhp, ddtb, dalog, ddsk, dngs = vjp((douts, dhn))
        dproj_ref[:, :COL_Z] = jnp.zeros((CHUNK, COL_Z), BF16)
        dproj_ref[:, COL_XBC:] = jnp.zeros((CHUNK, N_INP - COL_XBC), BF16)
        for h in range(SSD_HEADS):
            dxbc_ref[:, h * hd:(h + 1) * hd] = dxs[h]
            dproj_ref[:, COL_Z + h * hd: COL_Z + (h + 1) * hd] = _b(dz[h])
            dh_ref[h * hd:(h + 1) * hd, :] = dhp[h]
            dng_ref[:, h * hd:(h + 1) * hd] += dngs[h]
        for g in range(SSD_GROUPS):
            dxbc_ref[:, D_SSD + g * D_STATE: D_SSD + (g + 1) * D_STATE] = dbg[g]
            dxbc_ref[:, D_SSD + (SSD_GROUPS + g) * D_STATE: D_SSD + (SSD_GROUPS + g + 1) * D_STATE] = dcg[g]
        dproj_ref[:, COL_DT:COL_DT + SSD_HEADS] = _b(ddtr)
        ddtb_ref[...] += ddtb
        dalog_ref[...] += dalog
        ddsk_ref[...] += ddsk

    small = pl.BlockSpec((1, SSD_HEADS), lambda s, c: (0, 0))
    return pl.pallas_call(
        body, name="ssd_bwd", grid=(nseq, nch),
        in_specs=[pl.BlockSpec((CHUNK, D_SSD), lambda s, c: (rev(s, c), 0)),
                  pl.BlockSpec((CHUNK, CONV_DIM), lambda s, c: (rev(s, c), 0)),
                  pl.BlockSpec((CHUNK, D_SSD), lambda s, c: (rev(s, c), COL_Z // D_SSD)),
                  pl.BlockSpec((CHUNK, 128), lambda s, c: (rev(s, c), COL_DT // 128)),
                  pl.BlockSpec((1, SSD_HEADS * hd, D_STATE), lambda s, c: (rev(s, c), 0, 0)),
                  small, small, small,
                  pl.BlockSpec((1, D_SSD), lambda s, c: (0, 0))],
        out_specs=[pl.BlockSpec((CHUNK, CONV_DIM), lambda s, c: (rev(s, c), 0)),
                   pl.BlockSpec((CHUNK, N_INP), lambda s, c: (rev(s, c), 0)),
                   small, small, small,
                   pl.BlockSpec((1, D_SSD), lambda s, c: (0, 0))],
        out_shape=[jax.ShapeDtypeStruct((t, CONV_DIM), F32), jax.ShapeDtypeStruct((t, N_INP), BF16),
                   jax.ShapeDtypeStruct((1, SSD_HEADS), F32), jax.ShapeDtypeStruct((1, SSD_HEADS), F32),
                   jax.ShapeDtypeStruct((1, SSD_HEADS), F32), jax.ShapeDtypeStruct((1, D_SSD), F32)],
        scratch_shapes=[pltpu.VMEM((SSD_HEADS * hd, D_STATE), F32)],
        compiler_params=_cp("arbitrary", "arbitrary"),
    )(dy, xbc, proj, proj, hprev, dtb, alog, dskip, ng)


def _gmlp_chunk(gu, gv, ws, bs_cols, vg, og):
    n = gu[0].shape[0]
    mask = _tri(n, True)
    au = [_gelu(t) for t in gu]
    av = [_gelu(t) for t in gv]
    r = lax.rsqrt(sum(jnp.sum(t * t, axis=1, keepdims=True) for t in av) * (1.0 / D_GM) + EPS)
    p = []
    for h in range(GM_HEADS):
        sv = _bdot(ws[h] * mask, av[h] * r * vg[h]) + bs_cols[h]
        p.append(au[h] * sv)
    r2 = lax.rsqrt(sum(jnp.sum(t * t, axis=1, keepdims=True) for t in p) * (1.0 / D_GM) + EPS)
    return [p[h] * r2 * og[h] for h in range(GM_HEADS)]


def _gmlp_load(u_ref, v_ref, ws_ref, bst_ref, vg_ref, og_ref):
    gu = _hslices(u_ref, GM_HEAD_DIM, GM_HEADS)
    gv = _hslices(v_ref, GM_HEAD_DIM, GM_HEADS)
    ws = [ws_ref[h] for h in range(GM_HEADS)]
    bs_cols = [bst_ref[:, h:h + 1] for h in range(GM_HEADS)]
    return gu, gv, ws, bs_cols, _hslices(vg_ref, GM_HEAD_DIM, GM_HEADS), _hslices(og_ref, GM_HEAD_DIM, GM_HEADS)


def _gmlp_specs():
    return [pl.BlockSpec((CHUNK, D_GM), lambda i: (i, COL_U // D_GM)),
            pl.BlockSpec((CHUNK, D_GM), lambda i: (i, COL_V // D_GM)),
            pl.BlockSpec((GM_HEADS, CHUNK, CHUNK), lambda i: (0, 0, 0)),
            pl.BlockSpec((CHUNK, GM_HEADS), lambda i: (0, 0)),
            pl.BlockSpec((1, D_GM), lambda i: (0, 0)),
            pl.BlockSpec((1, D_GM), lambda i: (0, 0))]


def _gmlp_fwd(proj, ycat, ws, bst, vg, og):
    t = proj.shape[0]

    def body(u_ref, v_ref, ws_ref, bst_ref, vg_ref, og_ref, ycat_ref, o_ref):
        del ycat_ref
        outs = _gmlp_chunk(*_gmlp_load(u_ref, v_ref, ws_ref, bst_ref, vg_ref, og_ref))
        for h in range(GM_HEADS):
            o_ref[:, h * GM_HEAD_DIM:(h + 1) * GM_HEAD_DIM] = _b(outs[h])

    return pl.pallas_call(
        body, name="gmlp_fwd", grid=(t // CHUNK,),
        in_specs=_gmlp_specs() + [ANY],
        out_specs=pl.BlockSpec((CHUNK, D_GM), lambda i: (i, D_SSD // D_GM)),
        out_shape=jax.ShapeDtypeStruct(ycat.shape, ycat.dtype),
        input_output_aliases={6: 0},
        compiler_params=_cp("parallel"),
    )(proj, proj, ws, bst, vg, og, ycat)


def _gmlp_bwd(dy, proj, ws, bst, vg, og, dproj):
    t = proj.shape[0]
    w = GM_HEAD_DIM

    def body(dy_ref, u_ref, v_ref, ws_ref, bst_ref, vg_ref, og_ref, dproj_ref,
             dgm_ref, dws_ref, dbst_ref, dvg_ref, dog_ref):
        del dproj_ref

        @pl.when(pl.program_id(0) == 0)
        def _():
            dws_ref[...] = jnp.zeros_like(dws_ref)
            dbst_ref[...] = jnp.zeros_like(dbst_ref)
            dvg_ref[...] = jnp.zeros_like(dvg_ref)
            dog_ref[...] = jnp.zeros_like(dog_ref)

        _, vjp = jax.vjp(_gmlp_chunk, *_gmlp_load(u_ref, v_ref, ws_ref, bst_ref, vg_ref, og_ref))
        dgu, dgv, dws, dbs, dvg, dog = vjp(_hslices(dy_ref, w, GM_HEADS))
        for h in range(GM_HEADS):
            dgm_ref[:, h * w:(h + 1) * w] = _b(dgu[h])
            dgm_ref[:, D_GM + h * w: D_GM + (h + 1) * w] = _b(dgv[h])
            dws_ref[h] += dws[h]
            dbst_ref[:, h:h + 1] += dbs[h]
            dvg_ref[:, h * w:(h + 1) * w] += dvg[h]
            dog_ref[:, h * w:(h + 1) * w] += dog[h]

    return pl.pallas_call(
        body, name="gmlp_bwd", grid=(t // CHUNK,),
        in_specs=[pl.BlockSpec((CHUNK, D_GM), lambda i: (i, 1))] + _gmlp_specs() + [ANY],
        out_specs=[pl.BlockSpec((CHUNK, 2 * D_GM), lambda i: (i, COL_U // (2 * D_GM))),
                   pl.BlockSpec((GM_HEADS, CHUNK, CHUNK), lambda i: (0, 0, 0)),
                   pl.BlockSpec((CHUNK, GM_HEADS), lambda i: (0, 0)),
                   pl.BlockSpec((1, D_GM), lambda i: (0, 0)),
                   pl.BlockSpec((1, D_GM), lambda i: (0, 0))],
        out_shape=[jax.ShapeDtypeStruct(dproj.shape, dproj.dtype), jax.ShapeDtypeStruct((GM_HEADS, CHUNK, CHUNK), F32),
                   jax.ShapeDtypeStruct((CHUNK, GM_HEADS), F32), jax.ShapeDtypeStruct((1, D_GM), F32),
                   jax.ShapeDtypeStruct((1, D_GM), F32)],
        input_output_aliases={7: 0},
        compiler_params=_cp("arbitrary"),
    )(dy, proj, proj, ws, bst, vg, og, dproj)


def _local_step(x, target, mods, lw, final_g, *, nseq):
    saved = []
    xin, delta, gate = x, None, None
    for l in range(DEPTH):
        w = lw[l]
        sh1, sc1, g1, sh2, sc2, g2 = mods[l]
        x0, h1 = _normmod_fwd(xin, delta, gate, w["norm1_g"], sc1, sh1, nseq=nseq, name=f"norm1_fwd_{l}")
        proj = _matmul(h1, w["w_in"], name=f"mm_in_{l}")
        xbc = _ssd_conv_fwd(proj, w["ssd_conv_w"], w["ssd_conv_b"], nseq=nseq)
        ycat, hprev = _ssd_fwd(xbc, proj, w["ssd_dt_bias"], w["ssd_a_log"], w["ssd_d"], w["ssd_norm_g"], nseq=nseq)
        ycat = _gmlp_fwd(proj, ycat, w["gm_ws"], w["gm_bst"], w["gm_vnorm_g"], w["gm_out_g"])
        mix = _matmul(ycat, w["w_out"], name=f"mm_out_{l}")
        x1, h2 = _normmod_fwd(x0, mix, g1, w["norm2_g"], sc2, sh2, nseq=nseq, name=f"norm2_fwd_{l}")
        up = _matmul(h2, w["ff_up"], name=f"mm_up_{l}")
        act = _ffn_act_fwd(up, w["ff_conv_w"], w["ff_conv_b"], nseq=nseq)
        dn = _matmul(act, w["ff_down"], name=f"mm_down_{l}")
        saved.append(dict(x0=x0, xin_delta=delta, xin_gate=gate, h1=h1, proj=proj, xbc=xbc, hprev=hprev, ycat=ycat,
                          mix=mix, x1=x1, h2=h2, up=up, act=act, dn=dn))
        xin, delta, gate = x1, dn, g2

    loss, dx, ddelta, dgate, dfg = _final_loss(xin, delta, gate, final_g, target, nseq=nseq)

    big, small, dmods = [None] * DEPTH, [None] * DEPTH, [None] * DEPTH
    for l in reversed(range(DEPTH)):
        w, sv = lw[l], saved[l]
        sh1, sc1, g1, sh2, sc2, g2 = mods[l]
        dg2 = dgate
        dact = _matmul(ddelta, w["ff_down"], tb=True, name=f"mm_down_dx_{l}")
        g_ff_down = _matmul(sv["act"], ddelta, ta=True, name=f"mm_down_dw_{l}")
        dup, dfcw, dfcb = _ffn_act_bwd(dact, sv["up"], w["ff_conv_w"], w["ff_conv_b"], nseq=nseq)
        dh2 = _matmul(dup, w["ff_up"], tb=True, name=f"mm_up_dx_{l}")
        g_ff_up = _matmul(sv["h2"], dup, ta=True, name=f"mm_up_dw_{l}")
        dx, dmix, dg1, dn2g, dsc2, dsh2 = _normmod_bwd(dh2, dx, sv["x1"], sv["mix"], g1, w["norm2_g"], sc2,
                                                       nseq=nseq, name=f"norm2_bwd_{l}")
        dycat = _matmul(dmix, w["w_out"], tb=True, name=f"mm_out_dx_{l}")
        g_w_out = _matmul(sv["ycat"], dmix, ta=True, name=f"mm_out_dw_{l}")
        dxbc_act, dproj, ddtb, dalog, ddsk, dng = _ssd_bwd(dycat, sv["xbc"], sv["proj"], sv["hprev"], w["ssd_dt_bias"],
                                                          w["ssd_a_log"], w["ssd_d"], w["ssd_norm_g"], nseq=nseq)
        dproj, dscw, dscb = _ssd_conv_bwd(dxbc_act, sv["proj"], w["ssd_conv_w"], w["ssd_conv_b"], dproj, nseq=nseq)
        dproj, dws, dbst, dvg, dog = _gmlp_bwd(dycat, sv["proj"], w["gm_ws"], w["gm_bst"], w["gm_vnorm_g"], w["gm_out_g"], dproj)
        dh1 = _matmul(dproj, w["w_in"], tb=True, name=f"mm_in_dx_{l}")
        g_w_in = _matmul(sv["h1"], dproj, ta=True, name=f"mm_in_dw_{l}")
        dx, ddelta, dgate, dn1g, dsc1, dsh1 = _normmod_bwd(dh1, dx, sv["x0"], sv["xin_delta"], sv["xin_gate"],
                                                           w["norm1_g"], sc1, nseq=nseq, name=f"norm1_bwd_{l}")
        big[l] = dict(w_in=g_w_in, w_out=g_w_out, ff_up=g_ff_up, ff_down=g_ff_down)
        small[l] = dict(norm1_g=dn1g, norm2_g=dn2g, ssd_norm_g=dng, gm_vnorm_g=dvg, gm_out_g=dog,
                        ssd_conv_w=dscw, ssd_conv_b=dscb, ff_conv_w=dfcw, ff_conv_b=dfcb,
                        ssd_dt_bias=ddtb, ssd_a_log=dalog, ssd_d=ddsk, gm_ws=dws, gm_bs=dbst.T)
        dmods[l] = jnp.concatenate([dsh1, dsc1, dg1, dsh2, dsc2, dg2], axis=-1)[:, 0, :]
    return loss, dx, big, small, dmods, dfg


def _all_gather(arrs, name):
    n = len(arrs)

    def body(*refs):
        ins, outs = refs[:n], refs[n:2 * n]
        send_sems, recv_sems, local_sems = refs[2 * n:]
        x, y, c = lax.axis_index("x"), lax.axis_index("y"), lax.axis_index("c")
        me, sibling = (x, y, c), (x, y, 1 - c)
        chips = [(1 - x, y), (x, 1 - y), (1 - x, 1 - y)]

        def copy(i, k, block, to, src=None):
            px, py, pc = block
            dst = outs[i].at[4 * px + 2 * py + pc]
            return pltpu.make_async_remote_copy(
                src_ref=dst if src is None else src, dst_ref=dst,
                send_sem=send_sems.at[7 * i + k], recv_sem=recv_sems.at[7 * i + k],
                device_id=to, device_id_type=MESH)

        mine = [pltpu.make_async_copy(ins[i], outs[i].at[4 * x + 2 * y + c], local_sems.at[i]) for i in range(n)]
        for cp in mine:
            cp.start()
        first = []
        for i in range(n):
            first.append(copy(i, 0, me, sibling, src=ins[i]))
            first += [copy(i, 1 + j, me, (*chip, c), src=ins[i]) for j, chip in enumerate(chips)]
        for cp in first:
            cp.start()
        passed = []
        for j, chip in enumerate(chips):
            for i in range(n):
                copy(i, 1 + j, (*chip, c), me).wait_recv()
                fwd = copy(i, 4 + j, (*chip, c), sibling)
                fwd.start()
                passed.append(fwd)
        for i in range(n):
            copy(i, 0, sibling, me).wait_recv()
            for j, chip in enumerate(chips):
                copy(i, 4 + j, (*chip, 1 - c), me).wait_recv()
        for cp in first + passed:
            cp.wait_send()
        for cp in mine:
            cp.wait()

    return pl.pallas_call(
        body, name=name,
        in_specs=[ANY] * n, out_specs=[ANY] * n,
        out_shape=[jax.ShapeDtypeStruct((N_DEV,) + a.shape, a.dtype) for a in arrs],
        scratch_shapes=[pltpu.SemaphoreType.DMA((7 * n,)), pltpu.SemaphoreType.DMA((7 * n,)),
                        pltpu.SemaphoreType.DMA((n,))],
    )(*arrs)


def _exchange_sibling(arrs, name):
    n = len(arrs)

    def body(*refs):
        ins, outs = refs[:n], refs[n:2 * n]
        send_sems, recv_sems = refs[2 * n:]
        x, y, c = lax.axis_index("x"), lax.axis_index("y"), lax.axis_index("c")
        copies = []
        for i in range(n):
            for k in range(4):
                copies.append(pltpu.make_async_remote_copy(
                    src_ref=ins[i].at[2 * k + (1 - c)], dst_ref=outs[i].at[k],
                    send_sem=send_sems.at[4 * i + k], recv_sem=recv_sems.at[4 * i + k],
                    device_id=(x, y, 1 - c), device_id_type=MESH))
        for cp in copies:
            cp.start()
        for cp in copies:
            cp.wait_recv()
        for cp in copies:
            cp.wait_send()

    return pl.pallas_call(
        body, name=name,
        in_specs=[ANY] * n, out_specs=[ANY] * n,
        out_shape=[jax.ShapeDtypeStruct((4,) + a.shape[1:], a.dtype) for a in arrs],
        scratch_shapes=[pltpu.SemaphoreType.DMA((4 * n,)), pltpu.SemaphoreType.DMA((4 * n,))],
    )(*arrs)


def _exchange_chips(arrs, name):
    n = len(arrs)

    def body(*refs):
        ins, outs = refs[:n], refs[n:2 * n]
        send_sems, recv_sems = refs[2 * n:]
        x, y, c = lax.axis_index("x"), lax.axis_index("y"), lax.axis_index("c")
        chips = [(1 - x, y), (x, 1 - y), (1 - x, 1 - y)]
        copies = []
        for i in range(n):
            for j, (cx, cy) in enumerate(chips):
                copies.append(pltpu.make_async_remote_copy(
                    src_ref=ins[i].at[2 * cx + cy], dst_ref=outs[i].at[j],
                    send_sem=send_sems.at[3 * i + j], recv_sem=recv_sems.at[3 * i + j],
                    device_id=(cx, cy, c), device_id_type=MESH))
        for cp in copies:
            cp.start()
        for cp in copies:
            cp.wait_recv()
        for cp in copies:
            cp.wait_send()

    return pl.pallas_call(
        body, name=name,
        in_specs=[ANY] * n, out_specs=[ANY] * n,
        out_shape=[jax.ShapeDtypeStruct((3,) + a.shape[1:], a.dtype) for a in arrs],
        scratch_shapes=[pltpu.SemaphoreType.DMA((3 * n,)), pltpu.SemaphoreType.DMA((3 * n,))],
    )(*arrs)


def _add_sibling(a, r, pos, name):
    _, depth, rows, cols = a.shape
    tr = _tile(rows, 256) if rows % 8 == 0 else rows
    a3 = a.reshape(N_DEV * depth, rows, cols)
    r3 = r.reshape(4 * depth, rows, cols)

    def body(pos_ref, a_ref, r_ref, o_ref):
        o_ref[...] = a_ref[...] + r_ref[...]

    out = pl.pallas_call(
        body, name=name,
        grid_spec=pltpu.PrefetchScalarGridSpec(
            num_scalar_prefetch=1, grid=(4 * depth, rows // tr),
            in_specs=[pl.BlockSpec((1, tr, cols), lambda q, i, p: ((2 * (q // depth) + p[0]) * depth + q % depth, i, 0)),
                      pl.BlockSpec((1, tr, cols), lambda q, i, p: (q, i, 0))],
            out_specs=pl.BlockSpec((1, tr, cols), lambda q, i, p: (q, i, 0))),
        out_shape=jax.ShapeDtypeStruct((4 * depth, rows, cols), F32),
        compiler_params=_cp("parallel", "parallel"),
    )(pos, a3, r3)
    return out.reshape(4, depth, rows, cols)


def _adamw_math(w, g, m, v):
    m = ADAM_B1 * m + (1.0 - ADAM_B1) * g
    v = ADAM_B2 * v + (1.0 - ADAM_B2) * (g * g)
    m_hat = m / (1.0 - ADAM_B1 ** ADAM_STEP)
    v_hat = v / (1.0 - ADAM_B2 ** ADAM_STEP)
    delta = -ADAM_LR * (m_hat / (jnp.sqrt(v_hat) + ADAM_EPS) + ADAM_WD * w)
    return delta, m, v


def _adamw_sharded(parts, w, m, v, pos, name):
    depth, rows, cols = w.shape
    tr = _tile(rows, 256) if rows % 8 == 0 else rows
    npart = len(parts)

    def body(pos_ref, *refs):
        prefs = refs[:npart]
        w_ref, m_ref, v_ref, g_out, d_out, m_out, v_out = refs[npart:]
        g = prefs[0][...]
        for pr in prefs[1:]:
            g = g + pr[...]
        delta, mn, vn = _adamw_math(w_ref[...], g, m_ref[...], v_ref[...])
        g_out[...] = g
        d_out[...] = delta
        m_out[...] = mn
        v_out[...] = vn

    def part_spec(fn):
        return pl.BlockSpec((1, tr, cols), lambda l, i, p: (fn(p) * depth + l, i, 0))

    blk = pl.BlockSpec((1, tr, cols), lambda l, i, p: (l, i, 0))
    shp = jax.ShapeDtypeStruct((depth, rows, cols), F32)
    return pl.pallas_call(
        body, name=name,
        grid_spec=pltpu.PrefetchScalarGridSpec(
            num_scalar_prefetch=1, grid=(depth, rows // tr),
            in_specs=[part_spec(fn) for _, fn in parts] + [blk, blk, blk],
            out_specs=[blk, blk, blk, blk]),
        out_shape=[shp, shp, shp, shp],
        compiler_params=_cp("parallel", "parallel"),
    )(pos, *[a for a, _ in parts], w, m, v)


_P1024 = ["norm1_g", "norm2_g", "ssd_norm_g", "gm_vnorm_g", "gm_out_g"]
_P16 = ["ssd_dt_bias", "ssd_a_log", "ssd_d"]


def _adamw_small(gath, wmv):
    names = list(wmv.keys())
    classes = list(gath.keys())
    flat_in = [gath[k] for k in classes]
    for nme in names:
        flat_in += list(wmv[nme])
    out_shapes = []
    for nme in names:
        out_shapes += [jax.ShapeDtypeStruct(wmv[nme][0].shape, F32)] * 4
    out_shapes += [jax.ShapeDtypeStruct((DEPTH, SSD_CONV, CONV_DIM), F32), jax.ShapeDtypeStruct((DEPTH, FF_CONV, D_FF), F32)]
    scratch = [pltpu.VMEM(gath[k].shape[1:], F32) for k in classes]
    ncls = len(classes)

    def body(*refs):
        g_refs = dict(zip(classes, refs[:ncls]))
        pos = ncls
        w_refs = {}
        for nme in names:
            w_refs[nme] = refs[pos:pos + 3]
            pos += 3
        o_refs = {}
        for nme in names:
            o_refs[nme] = refs[pos:pos + 4]
            pos += 4
        scw_out, fcw_out = refs[pos], refs[pos + 1]
        s_refs = dict(zip(classes, refs[pos + 2:]))
        for k in classes:
            acc = g_refs[k][0]
            for dev in range(1, N_DEV):
                acc = acc + g_refs[k][dev]
            s_refs[k][...] = acc

        def apply(nme, grad_of):
            w_ref, m_ref, v_ref = w_refs[nme]
            g_out, d_out, m_out, v_out = o_refs[nme]
            shape = w_ref.shape
            if len(shape) == 2:
                idxs = [(slice(l, l + 1),) for l in range(shape[0])]
            elif len(shape) == 3:
                idxs = [(l,) for l in range(shape[0])]
            else:
                idxs = [(l, h) for l in range(shape[0]) for h in range(shape[1])]
            for n_i, ix in enumerate(idxs):
                g = grad_of(n_i)
                delta, mn, vn = _adamw_math(w_ref[ix], g, m_ref[ix], v_ref[ix])
                g_out[ix] = g
                d_out[ix] = delta
                m_out[ix] = mn
                v_out[ix] = vn

        s1024, s1536, s2816, s16, s128, s6144 = (s_refs[k] for k in classes)
        for n_i, nme in enumerate(_P1024):
            apply(nme, lambda l, b=2 * n_i: s1024[b + l:b + l + 1, :])
        apply("final_g", lambda l: s1024[10:11, :])
        apply("ssd_conv_b", lambda l: s1536[8 + l:9 + l, :])
        apply("ff_conv_b", lambda l: s2816[6 + l:7 + l, :])
        for n_i, nme in enumerate(_P16):
            apply(nme, lambda l, b=2 * n_i: s16[b + l:b + l + 1, :])
        apply("gm_ws", lambda q: s128[q * CHUNK:(q + 1) * CHUNK, :])
        apply("gm_bs", lambda l: s128[2048 + 8 * l:2048 + 8 * (l + 1), :])
        apply("ada_b", lambda l: s6144[2 * l:2 * l + 1, :] + s6144[2 * l + 1:2 * l + 2, :])
        for l in range(DEPTH):
            scw_out[l] = s1536[SSD_CONV * l:SSD_CONV * (l + 1), :]
            fcw_out[l] = s2816[FF_CONV * l:FF_CONV * (l + 1), :]

    outs = pl.pallas_call(
        body, name="adamw_small",
        out_shape=out_shapes,
        scratch_shapes=scratch,
        compiler_params=pltpu.CompilerParams(vmem_limit_bytes=VMEM_LIMIT),
    )(*flat_in)
    res = {nme: tuple(outs[4 * i:4 * i + 4]) for i, nme in enumerate(names)}
    return res, outs[-2], outs[-1]


_WEIGHTS = ['ada_w', 'ada_b', 'norm1_g', 'norm2_g', 'w_in', 'ssd_conv_w', 'ssd_conv_b', 'ssd_dt_bias', 'ssd_a_log',
            'ssd_d', 'ssd_norm_g', 'gm_vnorm_g', 'gm_ws', 'gm_bs', 'gm_out_g', 'w_out', 'ff_up', 'ff_conv_w',
            'ff_conv_b', 'ff_down', 'final_g']


def _unshard_cols(g):
    _, depth, k, n = g.shape
    return jnp.transpose(g, (1, 2, 0, 3)).reshape(depth, k, N_DEV * n)


def _shard_cols(full):
    depth, k, n8 = full.shape
    return jnp.transpose(full.reshape(depth, k, N_DEV, n8 // N_DEV), (2, 0, 1, 3))


def _interleave_ff(w):
    lead = w.shape[:-1]
    return jnp.swapaxes(w.reshape(lead + (2, D_FF // CONV_TC, CONV_TC)), -3, -2).reshape(lead + (2 * D_FF,))


def _deinterleave_ff(w):
    lead = w.shape[:-1]
    return jnp.swapaxes(w.reshape(lead + (D_FF // CONV_TC, 2, CONV_TC)), -3, -2).reshape(lead + (2 * D_FF,))


def _shard_rows(full):
    depth, r8, n = full.shape
    return jnp.transpose(full.reshape(depth, N_DEV, r8 // N_DEV, n), (1, 0, 2, 3))


def kernel(x, c, ada_w, ada_b, norm1_g, norm2_g, w_in, ssd_conv_w, ssd_conv_b, ssd_dt_bias, ssd_a_log, ssd_d, ssd_norm_g, gm_vnorm_g, gm_ws, gm_bs, gm_out_g, w_out, ff_up, ff_conv_w, ff_conv_b, ff_down, final_g, loss_target, m_ada_w, m_ada_b, m_norm1_g, m_norm2_g, m_w_in, m_ssd_conv_w, m_ssd_conv_b, m_ssd_dt_bias, m_ssd_a_log, m_ssd_d, m_ssd_norm_g, m_gm_vnorm_g, m_gm_ws, m_gm_bs, m_gm_out_g, m_w_out, m_ff_up, m_ff_conv_w, m_ff_conv_b, m_ff_down, m_final_g, v_ada_w, v_ada_b, v_norm1_g, v_norm2_g, v_w_in, v_ssd_conv_w, v_ssd_conv_b, v_ssd_dt_bias, v_ssd_a_log, v_ssd_d, v_ssd_norm_g, v_gm_vnorm_g, v_gm_ws, v_gm_bs, v_gm_out_g, v_w_out, v_ff_up, v_ff_conv_w, v_ff_conv_b, v_ff_down, v_final_g):
    given = dict(locals())
    wts = {n: given[n] for n in _WEIGHTS}
    mom = {n: given["m_" + n] for n in _WEIGHTS}
    var = {n: given["v_" + n] for n in _WEIGHTS}
    nseq, seq, d = x.shape
    ix, iy, ic = lax.axis_index("x"), lax.axis_index("y"), lax.axis_index("c")
    me = 4 * ix + 2 * iy + ic
    chip = (2 * ix + iy).astype(jnp.int32).reshape(1)
    core = ic.astype(jnp.int32).reshape(1)

    g_win, g_wout, g_ffup, g_ffdown, g_scw, g_fcw, c_all = _all_gather(
        [_b(w_in), _b(w_out), _b(ff_up), _b(ff_down), ssd_conv_w, ff_conv_w, c], "gather_weights")
    w_in_f = _unshard_cols(g_win)
    zpad = jnp.zeros((DEPTH, d, N_INP - N_IN), BF16)
    o_xbc, o_dt, o_gm = D_SSD, D_SSD + CONV_DIM, D_SSD + CONV_DIM + SSD_HEADS
    w_in_p = jnp.concatenate([w_in_f[:, :, o_gm:], w_in_f[:, :, :o_xbc], w_in_f[:, :, o_xbc:o_dt],
                              w_in_f[:, :, o_dt:o_gm], zpad], axis=2)
    w_out_f = jnp.transpose(g_wout, (1, 0, 2, 3)).reshape(DEPTH, 2 * D_SSD, d)
    ff_up_f = _interleave_ff(_unshard_cols(g_ffup))
    ff_down_f = jnp.transpose(g_ffdown, (1, 0, 2, 3)).reshape(DEPTH, D_FF, d)
    scw_f = _unshard_cols(g_scw)
    fcw_f = _unshard_cols(g_fcw)
    c_all = c_all.reshape(N_DEV * nseq, d)

    n_ada = ada_w.shape[2]
    ada_b_shard = lax.dynamic_slice_in_dim(ada_b, me * n_ada, n_ada, axis=1).reshape(DEPTH, 1, n_ada)
    mod_part, c_act = _ada_fwd(c_all, ada_w, ada_b_shard)
    (mod_g,) = _all_gather([mod_part], "gather_mod")
    mod_all = jnp.transpose(mod_g, (1, 2, 0, 3)).reshape(DEPTH, N_DEV * nseq, N_MOD * d)
    mod_mine = lax.dynamic_slice_in_dim(mod_all, me * nseq, nseq, axis=1)
    mods = [[mod_mine[l, :, k * d:(k + 1) * d].reshape(nseq, 1, d) for k in range(N_MOD)] for l in range(DEPTH)]

    lw = []
    for l in range(DEPTH):
        lw.append(dict(
            norm1_g=norm1_g[l:l + 1], norm2_g=norm2_g[l:l + 1], w_in=w_in_p[l], ssd_conv_w=scw_f[l],
            ssd_conv_b=ssd_conv_b[l:l + 1], ssd_dt_bias=ssd_dt_bias[l:l + 1], ssd_a_log=ssd_a_log[l:l + 1],
            ssd_d=ssd_d[l:l + 1], ssd_norm_g=ssd_norm_g[l:l + 1], gm_vnorm_g=gm_vnorm_g[l:l + 1], gm_ws=gm_ws[l],
            gm_bst=gm_bs[l].T, gm_out_g=gm_out_g[l:l + 1], w_out=w_out_f[l], ff_up=ff_up_f[l], ff_conv_w=fcw_f[l],
            ff_conv_b=ff_conv_b[l:l + 1], ff_down=ff_down_f[l]))

    loss_p, grad_x, big, small, dmods, dfg = _local_step(
        x.reshape(nseq * seq, d), loss_target.reshape(nseq * seq, d), mods, lw, final_g.reshape(1, d), nseq=nseq)
    loss = lax.psum(loss_p[0, 0], ("x", "y", "c"))

    gw_in_p = jnp.stack([big[l]["w_in"] for l in range(DEPTH)])
    gw_in = jnp.concatenate([gw_in_p[:, :, COL_Z:COL_XBC], gw_in_p[:, :, COL_XBC:COL_DT],
                             gw_in_p[:, :, COL_DT:COL_DT + SSD_HEADS], gw_in_p[:, :, :COL_Z]], axis=2)
    full = [_shard_cols(gw_in),
            _shard_rows(jnp.stack([big[l]["w_out"] for l in range(DEPTH)])),
            _shard_cols(_deinterleave_ff(jnp.stack([big[l]["ff_up"] for l in range(DEPTH)]))),
            _shard_rows(jnp.stack([big[l]["ff_down"] for l in range(DEPTH)]))]
    from_sib = _exchange_sibling(full, "rs_sibling")
    chip_sums = [_add_sibling(a, r, core, f"rs_add_{i}") for i, (a, r) in enumerate(zip(full, from_sib))]
    from_chips = _exchange_chips(chip_sums, "rs_chips")

    outs = {}
    for i, nme in enumerate(["w_in", "w_out", "ff_up", "ff_down"]):
        s1 = chip_sums[i]
        r3 = from_chips[i]
        depth_rows = s1.shape[2:]
        s1f = s1.reshape((4 * DEPTH,) + depth_rows)
        r3f = r3.reshape((3 * DEPTH,) + depth_rows)
        parts = [(s1f, lambda p: p[0]), (r3f, lambda p: 0), (r3f, lambda p: 1), (r3f, lambda p: 2)]
        outs[nme] = _adamw_sharded(parts, wts[nme], mom[nme], var[nme], chip, f"adamw_{nme}")

    def rows(name):
        return [small[l][name] for l in range(DEPTH)]

    p1024 = jnp.concatenate(sum([rows(n) for n in _P1024], []) + [dfg], axis=0)
    p1536 = jnp.concatenate(rows("ssd_conv_w") + rows("ssd_conv_b"), axis=0)
    p2816 = jnp.concatenate(rows("ff_conv_w") + rows("ff_conv_b"), axis=0)
    p16 = jnp.concatenate(sum([rows(n) for n in _P16], []), axis=0)
    p128 = jnp.concatenate([small[l]["gm_ws"].reshape(GM_HEADS * CHUNK, CHUNK) for l in range(DEPTH)] + rows("gm_bs"), axis=0)
    p6144 = jnp.concatenate(dmods, axis=0)
    gathered = _all_gather([p1024, p1536, p2816, p16, p128, p6144], "gather_small")
    gath = dict(zip(["p1024", "p1536", "p2816", "p16", "p128", "p6144"], gathered))

    dmod_all = jnp.transpose(gath["p6144"].reshape(N_DEV, DEPTH, nseq, N_MOD * d), (1, 0, 2, 3)).reshape(
        DEPTH, N_DEV * nseq, N_MOD * d)
    small_names = _P1024 + ["final_g", "ssd_conv_b", "ff_conv_b"] + _P16 + ["gm_ws", "gm_bs", "ada_b"]
    wmv = {}
    for nme in small_names:
        if nme == "final_g":
            wmv[nme] = tuple(a.reshape(1, d) for a in (wts[nme], mom[nme], var[nme]))
        else:
            wmv[nme] = (wts[nme], mom[nme], var[nme])
    small_out, scw_full, fcw_full = _adamw_small(gath, wmv)
    for nme in small_names:
        outs[nme] = small_out[nme]
    outs["final_g"] = tuple(a.reshape(d) for a in outs["final_g"])

    n_scw, n_fcw = ssd_conv_w.shape[2], ff_conv_w.shape[2]
    g_scw_mine = lax.dynamic_slice_in_dim(scw_full, me * n_scw, n_scw, axis=2)
    g_fcw_mine = lax.dynamic_slice_in_dim(fcw_full, me * n_fcw, n_fcw, axis=2)
    outs["ssd_conv_w"] = _adamw_sharded([(g_scw_mine, lambda p: 0)], ssd_conv_w, m_ssd_conv_w, v_ssd_conv_w, chip, "adamw_ssd_conv_w")
    outs["ff_conv_w"] = _adamw_sharded([(g_fcw_mine, lambda p: 0)], ff_conv_w, m_ff_conv_w, v_ff_conv_w, chip, "adamw_ff_conv_w")

    dmod_cols = _b(lax.dynamic_slice_in_dim(dmod_all, me * n_ada, n_ada, axis=2))
    g_ada = jnp.stack([_matmul(c_act, dmod_cols[l], ta=True, name=f"mm_ada_dw_{l}") for l in range(DEPTH)])
    outs["ada_w"] = _adamw_sharded([(g_ada, lambda p: 0)], ada_w, m_ada_w, v_ada_w, chip, "adamw_ada_w")

    result = [loss, grad_x.reshape(nseq, seq, d)]
    for k in range(4):
        result += [outs[n][k] for n in _WEIGHTS]
    return tuple(result)
```

```python
import functools
import math

import jax
import jax.numpy as jnp
from jax import lax
from jax.experimental import pallas as pl
from jax.experimental.pallas import tpu as pltpu

F32 = jnp.float32
BF16 = jnp.bfloat16

N_DEV = 8
D_MODEL = 1024
DEPTH = 2
CHUNK = 128
SSD_HEADS = 16
SSD_HEAD_DIM = 64
SSD_GROUPS = 2
HEADS_PER_GROUP = SSD_HEADS // SSD_GROUPS
GROUP_WIDTH = HEADS_PER_GROUP * SSD_HEAD_DIM
D_STATE = 128
D_SSD = 1024
CONV_DIM = 1536
SSD_CONV = 4
GM_HEADS = 8
GM_HEAD_DIM = 128
D_GM = 1024
D_FF = 2816
FF_CONV = 3
N_IN = 4624
N_MOD = 6
EPS = 1e-6

N_INP = 5120
COL_U, COL_V, COL_Z, COL_XBC, COL_DT = 0, 1024, 2048, 3072, 4608
DT_BLOCK = 512

ADAM_LR = 0.001
ADAM_B1 = 0.9
ADAM_B2 = 0.999
ADAM_EPS = 1e-08
ADAM_WD = 0.01
ADAM_STEP = 10

VMEM_LIMIT = 56 * 1024 * 1024
MESH = pl.DeviceIdType.MESH
ANY = pl.BlockSpec(memory_space=pl.ANY)


def _cp(*sem):
    return pltpu.CompilerParams(dimension_semantics=sem, vmem_limit_bytes=VMEM_LIMIT)


def _tile(n, pref):
    if n <= pref or n % 128:
        return n
    best = 128
    for t in range(128, pref + 1, 128):
        if n % t == 0:
            best = t
    return best


def _silu(x):
    return x * jax.nn.sigmoid(x)


def _gelu(x):
    return 0.5 * x * (1.0 + lax.erf(x * (1.0 / math.sqrt(2.0))))


def _softplus(x):
    return jnp.maximum(x, 0.0) + jnp.log1p(jnp.exp(-jnp.abs(x)))


def _rms(x, g, width):
    return x * lax.rsqrt(jnp.sum(x * x, axis=-1, keepdims=True) / width + EPS) * g


def _b(x):
    return x.astype(BF16)


_NN = (((1,), (0,)), ((), ()))
_NT = (((1,), (1,)), ((), ()))
_TN = (((0,), (0,)), ((), ()))


def _dg(a, b, dn):
    return lax.dot_general(_b(a), _b(b), dn, preferred_element_type=F32)


@jax.custom_vjp
def _bdot(a, b):
    return _dg(a, b, _NN)


def _bdot_fwd(a, b):
    return _dg(a, b, _NN), (a, b)


def _bdot_bwd(res, ct):
    a, b = res
    return _dg(ct, b, _NT), _dg(a, ct, _TN)


_bdot.defvjp(_bdot_fwd, _bdot_bwd)


@jax.custom_vjp
def _bdot_nt(a, b):
    return _dg(a, b, _NT)


def _bdot_nt_fwd(a, b):
    return _dg(a, b, _NT), (a, b)


def _bdot_nt_bwd(res, ct):
    a, b = res
    return _dg(ct, b, _NN), _dg(ct, a, _TN)


_bdot_nt.defvjp(_bdot_nt_fwd, _bdot_nt_bwd)


@jax.custom_vjp
def _bdot_tn(a, b):
    return _dg(a, b, _TN)


def _bdot_tn_fwd(a, b):
    return _dg(a, b, _TN), (a, b)


def _bdot_tn_bwd(res, ct):
    a, b = res
    return _dg(b, ct, _NT), _dg(a, ct, _NN)


_bdot_tn.defvjp(_bdot_tn_fwd, _bdot_tn_bwd)


def _tri(n, lower):
    r = lax.broadcasted_iota(jnp.int32, (n, n), 0)
    c = lax.broadcasted_iota(jnp.int32, (n, n), 1)
    return ((r >= c) if lower else (r <= c)).astype(F32)


def _eye(n):
    r = lax.broadcasted_iota(jnp.int32, (n, n), 0)
    c = lax.broadcasted_iota(jnp.int32, (n, n), 1)
    return (r == c).astype(F32)


def _hdot(a, b, dn):
    return lax.dot_general(a, b, dn, precision=lax.Precision.HIGHEST, preferred_element_type=F32)


@jax.custom_vjp
def _cumsum_rows(x):
    return _hdot(_tri(x.shape[0], True), x, _NN)


def _cumsum_rows_fwd(x):
    return _cumsum_rows(x), None


def _cumsum_rows_bwd(_, ct):
    return (_hdot(_tri(ct.shape[0], False), ct, _NN),)


_cumsum_rows.defvjp(_cumsum_rows_fwd, _cumsum_rows_bwd)


@jax.custom_vjp
def _transpose(x):
    return _hdot(_eye(x.shape[1]), x, _NT)


def _transpose_fwd(x):
    return _transpose(x), None


def _transpose_bwd(_, ct):
    return (_hdot(_eye(ct.shape[1]), ct, _NT),)


_transpose.defvjp(_transpose_fwd, _transpose_bwd)


def _matmul(a, b, *, ta=False, tb=False, out_dtype=F32, name):
    if ta:
        k_dim, m_dim = a.shape
    else:
        m_dim, k_dim = a.shape
    if tb:
        n_dim, kb = b.shape
    else:
        kb, n_dim = b.shape
    assert kb == k_dim, (a.shape, b.shape, ta, tb)
    assert out_dtype == F32
    tm, tn, tk = _tile(m_dim, 1024), _tile(n_dim, 1536), _tile(k_dim, 1536)
    ni, nj, nk = m_dim // tm, n_dim // tn, k_dim // tk
    dn = (((0 if ta else 1,), (1 if tb else 0,)), ((), ()))

    def body(a_ref, b_ref, o_ref):
        k = pl.program_id(2)
        p = lax.dot_general(a_ref[...], b_ref[...], dn, preferred_element_type=F32)
        if nk == 1:
            o_ref[...] = p
        else:
            @pl.when(k == 0)
            def _():
                o_ref[...] = p

            @pl.when(k > 0)
            def _():
                o_ref[...] += p

    a_bytes, b_bytes = m_dim * k_dim, k_dim * n_dim
    m_outer = nk > 1 or a_bytes + b_bytes * ni <= b_bytes + a_bytes * nj
    if m_outer:
        ij = lambda o, n, k: (o, n)
        grid = (ni, nj, nk)
    else:
        ij = lambda o, n, k: (n, o)
        grid = (nj, ni, nk)

    def a_map(o, n, k):
        i, _ = ij(o, n, k)
        return (k, i) if ta else (i, k)

    def b_map(o, n, k):
        _, j = ij(o, n, k)
        return (j, k) if tb else (k, j)

    return pl.pallas_call(
        body, name=name,
        grid=grid,
        in_specs=[pl.BlockSpec((tk, tm) if ta else (tm, tk), a_map),
                  pl.BlockSpec((tn, tk) if tb else (tk, tn), b_map)],
        out_specs=pl.BlockSpec((tm, tn), lambda o, n, k: ij(o, n, k)),
        out_shape=jax.ShapeDtypeStruct((m_dim, n_dim), out_dtype),
        compiler_params=_cp("parallel", "parallel", "arbitrary"),
    )(a, b)


def _ada_fwd(c_all, ada_w, ada_b_shard):
    depth, d, n = ada_w.shape
    nb = c_all.shape[0]

    def body(c_ref, w_ref, b_ref, o_ref, ca_ref):
        ca = _silu(c_ref[...])
        ca_ref[...] = _b(ca)
        o_ref[0] = _dg(ca, w_ref[0], _NN) + b_ref[0]

    return pl.pallas_call(
        body, name="ada_fwd",
        grid=(depth,),
        in_specs=[pl.BlockSpec((nb, d), lambda l: (0, 0)),
                  pl.BlockSpec((1, d, n), lambda l: (l, 0, 0)),
                  pl.BlockSpec((1, 1, n), lambda l: (l, 0, 0))],
        out_specs=[pl.BlockSpec((1, nb, n), lambda l: (l, 0, 0)),
                   pl.BlockSpec((nb, d), lambda l: (0, 0))],
        out_shape=[jax.ShapeDtypeStruct((depth, nb, n), F32), jax.ShapeDtypeStruct((nb, d), BF16)],
        compiler_params=_cp("arbitrary"),
    )(c_all, ada_w, ada_b_shard)


def _normmod_f(x, g, sc, sh):
    return _rms(x, g, D_MODEL) * (1.0 + sc) + sh


def _row_tile(seq):
    return min(seq, 256)


def _normmod_fwd(xin, delta, gate, g, sc, sh, *, nseq, name):
    t, d = xin.shape
    seq = t // nseq
    tr = _row_tile(seq)
    nt = seq // tr
    has_delta = delta is not None
    row = pl.BlockSpec((tr, d), lambda s, i: (s * nt + i, 0))
    per_seq = pl.BlockSpec((1, 1, d), lambda s, i: (s, 0, 0))
    vec = pl.BlockSpec((1, d), lambda s, i: (0, 0))

    if has_delta:
        def body(xin_ref, delta_ref, gate_ref, g_ref, sc_ref, sh_ref, x_ref, h_ref):
            x = xin_ref[...] + gate_ref[0] * delta_ref[...]
            x_ref[...] = x
            h_ref[...] = _b(_normmod_f(x, g_ref[...], sc_ref[0], sh_ref[0]))

        return pl.pallas_call(
            body, name=name, grid=(nseq, nt),
            in_specs=[row, row, per_seq, vec, per_seq, per_seq],
            out_specs=[row, row],
            out_shape=[jax.ShapeDtypeStruct((t, d), F32), jax.ShapeDtypeStruct((t, d), BF16)],
            compiler_params=_cp("parallel", "parallel"),
        )(xin, delta, gate, g, sc, sh)

    def body0(xin_ref, g_ref, sc_ref, sh_ref, h_ref):
        h_ref[...] = _b(_normmod_f(xin_ref[...], g_ref[...], sc_ref[0], sh_ref[0]))

    h = pl.pallas_call(
        body0, name=name, grid=(nseq, nt),
        in_specs=[row, vec, per_seq, per_seq],
        out_specs=row,
        out_shape=jax.ShapeDtypeStruct((t, d), BF16),
        compiler_params=_cp("parallel", "parallel"),
    )(xin, g, sc, sh)
    return xin, h


def _normmod_bwd(dh, dxo, x, delta, gate, g, sc, *, nseq, name):
    t, d = x.shape
    seq = t // nseq
    tr = _row_tile(seq)
    nt = seq // tr
    has_delta = delta is not None
    row = pl.BlockSpec((tr, d), lambda s, i: (s * nt + i, 0))
    per_seq = pl.BlockSpec((1, 1, d), lambda s, i: (s, 0, 0))
    vec = pl.BlockSpec((1, d), lambda s, i: (0, 0))

    def core(dh_ref, dxo_ref, x_ref, g_ref, sc_ref, dx_ref, dg_ref, dsc_ref, dsh_ref):
        s, i = pl.program_id(0), pl.program_id(1)
        dh_v = dh_ref[...]
        _, vjp = jax.vjp(lambda xx, gg, ss: _normmod_f(xx, gg, ss, 0.0), x_ref[...], g_ref[...], sc_ref[0])
        dxn, dg_t, dsc_t = vjp(dh_v)
        dx = dxo_ref[...] + dxn
        dx_ref[...] = dx
        dsh_t = jnp.sum(dh_v, axis=0, keepdims=True)

        @pl.when((s == 0) & (i == 0))
        def _():
            dg_ref[...] = jnp.zeros_like(dg_ref)

        @pl.when(i == 0)
        def _():
            dsc_ref[...] = jnp.zeros_like(dsc_ref)
            dsh_ref[...] = jnp.zeros_like(dsh_ref)

        dg_ref[...] += dg_t
        dsc_ref[0] += dsc_t
        dsh_ref[0] += dsh_t
        return dx

    if has_delta:
        def body(dh_ref, dxo_ref, x_ref, delta_ref, gate_ref, g_ref, sc_ref,
                 dx_ref, dd_ref, dgate_ref, dg_ref, dsc_ref, dsh_ref):
            dx = core(dh_ref, dxo_ref, x_ref, g_ref, sc_ref, dx_ref, dg_ref, dsc_ref, dsh_ref)
            dd_ref[...] = _b(dx * gate_ref[0])

            @pl.when(pl.program_id(1) == 0)
            def _():
                dgate_ref[...] = jnp.zeros_like(dgate_ref)

            dgate_ref[0] += jnp.sum(dx * delta_ref[...], axis=0, keepdims=True)

        return pl.pallas_call(
            body, name=name, grid=(nseq, nt),
            in_specs=[row, row, row, row, per_seq, vec, per_seq],
            out_specs=[row, row, per_seq, vec, per_seq, per_seq],
            out_shape=[jax.ShapeDtypeStruct((t, d), F32), jax.ShapeDtypeStruct((t, d), BF16),
                       jax.ShapeDtypeStruct((nseq, 1, d), F32), jax.ShapeDtypeStruct((1, d), F32),
                       jax.ShapeDtypeStruct((nseq, 1, d), F32), jax.ShapeDtypeStruct((nseq, 1, d), F32)],
            compiler_params=_cp("arbitrary", "arbitrary"),
        )(dh, dxo, x, delta, gate, g, sc)

    def body0(dh_ref, dxo_ref, x_ref, g_ref, sc_ref, dx_ref, dg_ref, dsc_ref, dsh_ref):
        core(dh_ref, dxo_ref, x_ref, g_ref, sc_ref, dx_ref, dg_ref, dsc_ref, dsh_ref)

    dx, dg, dsc, dsh = pl.pallas_call(
        body0, name=name, grid=(nseq, nt),
        in_specs=[row, row, row, vec, per_seq],
        out_specs=[row, vec, per_seq, per_seq],
        out_shape=[jax.ShapeDtypeStruct((t, d), F32), jax.ShapeDtypeStruct((1, d), F32),
                   jax.ShapeDtypeStruct((nseq, 1, d), F32), jax.ShapeDtypeStruct((nseq, 1, d), F32)],
        compiler_params=_cp("arbitrary", "arbitrary"),
    )(dh, dxo, x, g, sc)
    return dx, None, None, dg, dsc, dsh


def _final_loss(xin, delta, gate, fg, target, *, nseq):
    t, d = xin.shape
    seq = t // nseq
    tr = _row_tile(seq)
    nt = seq // tr
    row = pl.BlockSpec((tr, d), lambda s, i: (s * nt + i, 0))
    per_seq = pl.BlockSpec((1, 1, d), lambda s, i: (s, 0, 0))
    vec = pl.BlockSpec((1, d), lambda s, i: (0, 0))

    def body(xin_ref, delta_ref, gate_ref, fg_ref, tgt_ref, loss_ref, dx_ref, dd_ref, dgate_ref, dfg_ref):
        s, i = pl.program_id(0), pl.program_id(1)
        dl = delta_ref[...]
        x = xin_ref[...] + gate_ref[0] * dl
        y, vjp = jax.vjp(lambda xx, gg: _rms(xx, gg, D_MODEL), x, fg_ref[...])
        err = y - tgt_ref[...]
        dx, dfg_t = vjp(err * (1.0 / d))
        dx_ref[...] = dx
        dd_ref[...] = _b(dx * gate_ref[0])

        @pl.when((s == 0) & (i == 0))
        def _():
            loss_ref[...] = jnp.zeros_like(loss_ref)
            dfg_ref[...] = jnp.zeros_like(dfg_ref)

        @pl.when(i == 0)
        def _():
            dgate_ref[...] = jnp.zeros_like(dgate_ref)

        loss_ref[...] += jnp.sum(err * err) * (0.5 / d)
        dfg_ref[...] += dfg_t
        dgate_ref[0] += jnp.sum(dx * dl, axis=0, keepdims=True)

    return pl.pallas_call(
        body, name="final_loss", grid=(nseq, nt),
        in_specs=[row, row, per_seq, vec, row],
        out_specs=[pl.BlockSpec((1, 128), lambda s, i: (0, 0)), row, row, per_seq, vec],
        out_shape=[jax.ShapeDtypeStruct((1, 128), F32), jax.ShapeDtypeStruct((t, d), F32),
                   jax.ShapeDtypeStruct((t, d), BF16), jax.ShapeDtypeStruct((nseq, 1, d), F32),
                   jax.ShapeDtypeStruct((1, d), F32)],
        compiler_params=_cp("arbitrary", "arbitrary"),
    )(xin, delta, gate, fg, target)


def _shift_down(x, j):
    if j == 0:
        return x
    rows = lax.broadcasted_iota(jnp.int32, x.shape, 0)
    return jnp.where(rows >= j, pltpu.roll(x, j, 0), 0.0)


def _shift_up(x, j):
    if j == 0:
        return x
    n = x.shape[0]
    rows = lax.broadcasted_iota(jnp.int32, x.shape, 0)
    return jnp.where(rows < n - j, pltpu.roll(x, n - j, 0), 0.0)


def _conv(x, w_ref, b_ref):
    kw = w_ref.shape[0]
    y = b_ref[...] + w_ref[kw - 1:kw, :] * x
    for j in range(1, kw):
        y = y + w_ref[kw - 1 - j:kw - j, :] * _shift_down(x, j)
    return y


def _conv_bwd(dy, x, w_ref, dw_ref, db_ref):
    kw = w_ref.shape[0]
    dx = w_ref[kw - 1:kw, :] * dy
    dw_ref[kw - 1:kw, :] += jnp.sum(dy * x, axis=0, keepdims=True)
    for j in range(1, kw):
        dx = dx + w_ref[kw - 1 - j:kw - j, :] * _shift_up(dy, j)
        dw_ref[kw - 1 - j:kw - j, :] += jnp.sum(dy * _shift_down(x, j), axis=0, keepdims=True)
    db_ref[...] += jnp.sum(dy, axis=0, keepdims=True)
    return dx


CONV_TC = 256


def _ssd_conv_fwd(proj, w, b, *, nseq):
    t = proj.shape[0]
    seq = t // nseq
    nb = CONV_DIM // CONV_TC
    off = COL_XBC // CONV_TC

    def body(x_ref, w_ref, b_ref, o_ref):
        o_ref[...] = _silu(_conv(x_ref[...], w_ref, b_ref))

    return pl.pallas_call(
        body, name="ssd_conv_fwd", grid=(nb, nseq),
        in_specs=[pl.BlockSpec((seq, CONV_TC), lambda j, s: (s, off + j)),
                  pl.BlockSpec((SSD_CONV, CONV_TC), lambda j, s: (0, j)),
                  pl.BlockSpec((1, CONV_TC), lambda j, s: (0, j))],
        out_specs=pl.BlockSpec((seq, CONV_TC), lambda j, s: (s, j)),
        out_shape=jax.ShapeDtypeStruct((t, CONV_DIM), F32),
        compiler_params=_cp("parallel", "parallel"),
    )(proj, w, b)


def _ssd_conv_bwd(dact, proj, w, b, dproj, *, nseq):
    t = proj.shape[0]
    seq = t // nseq
    nb = CONV_DIM // CONV_TC
    off = COL_XBC // CONV_TC

    def body(da_ref, x_ref, w_ref, b_ref, dproj_ref, dx_ref, dw_ref, db_ref):
        del dproj_ref

        @pl.when(pl.program_id(1) == 0)
        def _():
            dw_ref[...] = jnp.zeros_like(dw_ref)
            db_ref[...] = jnp.zeros_like(db_ref)

        x = x_ref[...]
        pre = _conv(x, w_ref, b_ref)
        sg = jax.nn.sigmoid(pre)
        dpre = da_ref[...] * (sg * (1.0 + pre * (1.0 - sg)))
        dx_ref[...] = _b(_conv_bwd(dpre, x, w_ref, dw_ref, db_ref))

    return pl.pallas_call(
        body, name="ssd_conv_bwd", grid=(nb, nseq),
        in_specs=[pl.BlockSpec((seq, CONV_TC), lambda j, s: (s, j)),
                  pl.BlockSpec((seq, CONV_TC), lambda j, s: (s, off + j)),
                  pl.BlockSpec((SSD_CONV, CONV_TC), lambda j, s: (0, j)),
                  pl.BlockSpec((1, CONV_TC), lambda j, s: (0, j)),
                  ANY],
        out_specs=[pl.BlockSpec((seq, CONV_TC), lambda j, s: (s, off + j)),
                   pl.BlockSpec((SSD_CONV, CONV_TC), lambda j, s: (0, j)),
                   pl.BlockSpec((1, CONV_TC), lambda j, s: (0, j))],
        out_shape=[jax.ShapeDtypeStruct(dproj.shape, dproj.dtype), jax.ShapeDtypeStruct((SSD_CONV, CONV_DIM), F32),
                   jax.ShapeDtypeStruct((1, CONV_DIM), F32)],
        input_output_aliases={4: 0},
        compiler_params=_cp("parallel", "arbitrary"),
    )(dact, proj, w, b, dproj)


def _ffn_act_fwd(up, w, b, *, nseq):
    t = up.shape[0]
    seq = t // nseq
    nb = D_FF // CONV_TC

    def body(up_ref, w_ref, b_ref, o_ref):
        o_ref[...] = _b(_silu(_conv(up_ref[:, :CONV_TC], w_ref, b_ref)) * up_ref[:, CONV_TC:])

    return pl.pallas_call(
        body, name="ffn_act_fwd", grid=(nb, nseq),
        in_specs=[pl.BlockSpec((seq, 2 * CONV_TC), lambda j, s: (s, j)),
                  pl.BlockSpec((FF_CONV, CONV_TC), lambda j, s: (0, j)),
                  pl.BlockSpec((1, CONV_TC), lambda j, s: (0, j))],
        out_specs=pl.BlockSpec((seq, CONV_TC), lambda j, s: (s, j)),
        out_shape=jax.ShapeDtypeStruct((t, D_FF), BF16),
        compiler_params=_cp("parallel", "parallel"),
    )(up, w, b)


def _ffn_act_bwd(dact, up, w, b, *, nseq):
    t = up.shape[0]
    seq = t // nseq
    nb = D_FF // CONV_TC

    def body(da_ref, up_ref, w_ref, b_ref, dup_ref, dw_ref, db_ref):
        @pl.when(pl.program_id(1) == 0)
        def _():
            dw_ref[...] = jnp.zeros_like(dw_ref)
            db_ref[...] = jnp.zeros_like(db_ref)

        gate = up_ref[:, :CONV_TC]
        pre = _conv(gate, w_ref, b_ref)
        sg = jax.nn.sigmoid(pre)
        da = da_ref[...]
        dup_ref[:, CONV_TC:] = _b(da * (pre * sg))
        dpre = da * up_ref[:, CONV_TC:] * (sg * (1.0 + pre * (1.0 - sg)))
        dup_ref[:, :CONV_TC] = _b(_conv_bwd(dpre, gate, w_ref, dw_ref, db_ref))

    return pl.pallas_call(
        body, name="ffn_act_bwd", grid=(nb, nseq),
        in_specs=[pl.BlockSpec((seq, CONV_TC), lambda j, s: (s, j)),
                  pl.BlockSpec((seq, 2 * CONV_TC), lambda j, s: (s, j)),
                  pl.BlockSpec((FF_CONV, CONV_TC), lambda j, s: (0, j)),
                  pl.BlockSpec((1, CONV_TC), lambda j, s: (0, j))],
        out_specs=[pl.BlockSpec((seq, 2 * CONV_TC), lambda j, s: (s, j)),
                   pl.BlockSpec((FF_CONV, CONV_TC), lambda j, s: (0, j)),
                   pl.BlockSpec((1, CONV_TC), lambda j, s: (0, j))],
        out_shape=[jax.ShapeDtypeStruct((t, 2 * D_FF), BF16), jax.ShapeDtypeStruct((FF_CONV, D_FF), F32),
                   jax.ShapeDtypeStruct((1, D_FF), F32)],
        compiler_params=_cp("parallel", "arbitrary"),
    )(dact, up, w, b)


def _ssd_chunk(xs, bg, cg, dtr, z, hp, dtb, alog, dskip, ng):
    n = dtr.shape[0]
    dt = _softplus(dtr + dtb)
    cs = _cumsum_rows(dt * (-jnp.exp(alog)))
    cs_t = _transpose(cs)
    lane = lax.broadcasted_iota(jnp.int32, (1, SSD_HEADS), 1)
    sub = lax.broadcasted_iota(jnp.int32, (SSD_HEADS, 1), 0)
    row = lax.broadcasted_iota(jnp.int32, (n, 1), 0)
    r2 = lax.broadcasted_iota(jnp.int32, (n, n), 0)
    c2 = lax.broadcasted_iota(jnp.int32, (n, n), 1)
    causal = r2 >= c2
    cb = [_bdot_nt(cg[g], bg[g]) for g in range(SSD_GROUPS)]
    ys, hn = [], []
    for h in range(SSD_HEADS):
        g = h // HEADS_PER_GROUP
        oh = (lane == h).astype(F32)
        oh_t = (sub == h).astype(F32)
        dt_h = jnp.sum(dt * oh, axis=1, keepdims=True)
        cs_h = jnp.sum(cs * oh, axis=1, keepdims=True)
        d_h = jnp.sum(dskip * oh, axis=1, keepdims=True)
        cs_row = jnp.sum(cs_t * oh_t, axis=0, keepdims=True)
        cs_last = jnp.sum(jnp.where(row == n - 1, cs_h, 0.0), axis=0, keepdims=True)
        decay = jnp.where(causal, jnp.exp(jnp.where(causal, cs_h - cs_row, 0.0)), 0.0)
        xc = xs[h] * dt_h
        y = _bdot(cb[g] * decay, xc)
        y = y + _bdot_nt(cg[g], hp[h]) * jnp.exp(cs_h)
        y = y + d_h * xs[h]
        hn.append(jnp.exp(cs_last) * hp[h] + _bdot_tn(xc * jnp.exp(cs_last - cs_h), bg[g]))
        ys.append(y * _silu(z[h]))
    outs = []
    for g in range(SSD_GROUPS):
        hs = range(g * HEADS_PER_GROUP, (g + 1) * HEADS_PER_GROUP)
        ms = sum(jnp.sum(ys[h] * ys[h], axis=1, keepdims=True) for h in hs) * (1.0 / GROUP_WIDTH)
        r = lax.rsqrt(ms + EPS)
        outs += [ys[h] * r * ng[h] for h in hs]
    return outs, hn


def _hslices(ref, width, count, base=0):
    return [ref[:, base + k * width: base + (k + 1) * width] for k in range(count)]


def _ssd_load(xbc_ref, z_ref, dt_ref, ng_ref):
    xs = _hslices(xbc_ref, SSD_HEAD_DIM, SSD_HEADS)
    bg = _hslices(xbc_ref, D_STATE, SSD_GROUPS, D_SSD)
    cg = _hslices(xbc_ref, D_STATE, SSD_GROUPS, D_SSD + SSD_GROUPS * D_STATE)
    z = _hslices(z_ref, SSD_HEAD_DIM, SSD_HEADS)
    ng = _hslices(ng_ref, SSD_HEAD_DIM, SSD_HEADS)
    return xs, bg, cg, dt_ref[:, 0:SSD_HEADS], z, ng


def _ssd_specs(nch):
    rowi = lambda s, c: s * nch + c
    return [pl.BlockSpec((CHUNK, CONV_DIM), lambda s, c: (rowi(s, c), 0)),
            pl.BlockSpec((CHUNK, D_SSD), lambda s, c: (rowi(s, c), COL_Z // D_SSD)),
            pl.BlockSpec((CHUNK, 128), lambda s, c: (rowi(s, c), COL_DT // 128)),
            pl.BlockSpec((1, SSD_HEADS), lambda s, c: (0, 0)),
            pl.BlockSpec((1, SSD_HEADS), lambda s, c: (0, 0)),
            pl.BlockSpec((1, SSD_HEADS), lambda s, c: (0, 0)),
            pl.BlockSpec((1, D_SSD), lambda s, c: (0, 0))]


def _ssd_fwd(xbc, proj, dtb, alog, dskip, ng, *, nseq):
    t = proj.shape[0]
    nch = t // nseq // CHUNK
    hd = SSD_HEAD_DIM

    def body(xbc_ref, z_ref, dt_ref, dtb_ref, alog_ref, dsk_ref, ng_ref, y_ref, hp_ref, h_ref):
        @pl.when(pl.program_id(1) == 0)
        def _():
            h_ref[...] = jnp.zeros_like(h_ref)

        xs, bg, cg, dtr, z, ngs = _ssd_load(xbc_ref, z_ref, dt_ref, ng_ref)
        hp_ref[0] = h_ref[...]
        hp = [h_ref[h * hd:(h + 1) * hd, :] for h in range(SSD_HEADS)]
        outs, hn = _ssd_chunk(xs, bg, cg, dtr, z, hp, dtb_ref[...], alog_ref[...], dsk_ref[...], ngs)
        for h in range(SSD_HEADS):
            y_ref[:, h * hd:(h + 1) * hd] = _b(outs[h])
            h_ref[h * hd:(h + 1) * hd, :] = hn[h]

    return pl.pallas_call(
        body, name="ssd_fwd", grid=(nseq, nch),
        in_specs=_ssd_specs(nch),
        out_specs=[pl.BlockSpec((CHUNK, D_SSD), lambda s, c: (s * nch + c, 0)),
                   pl.BlockSpec((1, SSD_HEADS * hd, D_STATE), lambda s, c: (s * nch + c, 0, 0))],
        out_shape=[jax.ShapeDtypeStruct((t, D_SSD + D_GM), BF16),
                   jax.ShapeDtypeStruct((t // CHUNK, SSD_HEADS * hd, D_STATE), F32)],
        scratch_shapes=[pltpu.VMEM((SSD_HEADS * hd, D_STATE), F32)],
        compiler_params=_cp("arbitrary", "arbitrary"),
    )(xbc, proj, proj, dtb, alog, dskip, ng)


def _ssd_bwd(dy, xbc, proj, hprev, dtb, alog, dskip, ng, *, nseq):
    t = proj.shape[0]
    nch = t // nseq // CHUNK
    hd = SSD_HEAD_DIM
    rev = lambda s, c: s * nch + (nch - 1 - c)

    def body(dy_ref, xbc_ref, z_ref, dt_ref, hp_ref, dtb_ref, alog_ref, dsk_ref, ng_ref,
             dxbc_ref, dproj_ref, ddtb_ref, dalog_ref, ddsk_ref, dng_ref, dh_ref):
        first = (pl.program_id(0) == 0) & (pl.program_id(1) == 0)

        @pl.when(pl.program_id(1) == 0)
        def _():
            dh_ref[...] = jnp.zeros_like(dh_ref)

        @pl.when(first)
        def _():
            ddtb_ref[...] = jnp.zeros_like(ddtb_ref)
            dalog_ref[...] = jnp.zeros_like(dalog_ref)
            ddsk_ref[...] = jnp.zeros_like(ddsk_ref)
            dng_ref[...] = jnp.zeros_like(dng_ref)

        xs, bg, cg, dtr, z, ngs = _ssd_load(xbc_ref, z_ref, dt_ref, ng_ref)
        hp = [hp_ref[0, h * hd:(h + 1) * hd, :] for h in range(SSD_HEADS)]
        _, vjp = jax.vjp(_ssd_chunk, xs, bg, cg, dtr, z, hp, dtb_ref[...], alog_ref[...], dsk_ref[...], ngs)
        douts = [dy_ref[:, h * hd:(h + 1) * hd] for h in range(SSD_HEADS)]
        dhn = [dh_ref[h * hd:(h + 1) * hd, :] for h in range(SSD_HEADS)]
        dxs, dbg, dcg, ddtr, dz, dhp, ddtb, dalog, ddsk, dngs = vjp((douts, dhn))
        dproj_ref[:, :COL_Z] = jnp.zeros((CHUNK, COL_Z), BF16)
        dproj_ref[:, COL_XBC:] = jnp.zeros((CHUNK, N_INP - COL_XBC), BF16)
        for h in range(SSD_HEADS):
            dxbc_ref[:, h * hd:(h + 1) * hd] = dxs[h]
            dproj_ref[:, COL_Z + h * hd: COL_Z + (h + 1) * hd] = _b(dz[h])
            dh_ref[h * hd:(h + 1) * hd, :] = dhp[h]
            dng_ref[:, h * hd:(h + 1) * hd] += dngs[h]
        for g in range(SSD_GROUPS):
            dxbc_ref[:, D_SSD + g * D_STATE: D_SSD + (g + 1) * D_STATE] = dbg[g]
            dxbc_ref[:, D_SSD + (SSD_GROUPS + g) * D_STATE: D_SSD + (SSD_GROUPS + g + 1) * D_STATE] = dcg[g]
        dproj_ref[:, COL_DT:COL_DT + SSD_HEADS] = _b(ddtr)
        ddtb_ref[...] += ddtb
        dalog_ref[...] += dalog
        ddsk_ref[...] += ddsk

    small = pl.BlockSpec((1, SSD_HEADS), lambda s, c: (0, 0))
    return pl.pallas_call(
        body, name="ssd_bwd", grid=(nseq, nch),
        in_specs=[pl.BlockSpec((CHUNK, D_SSD), lambda s, c: (rev(s, c), 0)),
                  pl.BlockSpec((CHUNK, CONV_DIM), lambda s, c: (rev(s, c), 0)),
                  pl.BlockSpec((CHUNK, D_SSD), lambda s, c: (rev(s, c), COL_Z // D_SSD)),
                  pl.BlockSpec((CHUNK, 128), lambda s, c: (rev(s, c), COL_DT // 128)),
                  pl.BlockSpec((1, SSD_HEADS * hd, D_STATE), lambda s, c: (rev(s, c), 0, 0)),
                  small, small, small,
                  pl.BlockSpec((1, D_SSD), lambda s, c: (0, 0))],
        out_specs=[pl.BlockSpec((CHUNK, CONV_DIM), lambda s, c: (rev(s, c), 0)),
                   pl.BlockSpec((CHUNK, N_INP), lambda s, c: (rev(s, c), 0)),
                   small, small, small,
                   pl.BlockSpec((1, D_SSD), lambda s, c: (0, 0))],
        out_shape=[jax.ShapeDtypeStruct((t, CONV_DIM), F32), jax.ShapeDtypeStruct((t, N_INP), BF16),
                   jax.ShapeDtypeStruct((1, SSD_HEADS), F32), jax.ShapeDtypeStruct((1, SSD_HEADS), F32),
                   jax.ShapeDtypeStruct((1, SSD_HEADS), F32), jax.ShapeDtypeStruct((1, D_SSD), F32)],
        scratch_shapes=[pltpu.VMEM((SSD_HEADS * hd, D_STATE), F32)],
        compiler_params=_cp("arbitrary", "arbitrary"),
    )(dy, xbc, proj, proj, hprev, dtb, alog, dskip, ng)


def _gmlp_chunk(gu, gv, ws, bs_cols, vg, og):
    n = gu[0].shape[0]
    mask = _tri(n, True)
    au = [_gelu(t) for t in gu]
    av = [_gelu(t) for t in gv]
    r = lax.rsqrt(sum(jnp.sum(t * t, axis=1, keepdims=True) for t in av) * (1.0 / D_GM) + EPS)
    p = []
    for h in range(GM_HEADS):
        sv = _bdot(ws[h] * mask, av[h] * r * vg[h]) + bs_cols[h]
        p.append(au[h] * sv)
    r2 = lax.rsqrt(sum(jnp.sum(t * t, axis=1, keepdims=True) for t in p) * (1.0 / D_GM) + EPS)
    return [p[h] * r2 * og[h] for h in range(GM_HEADS)]


def _gmlp_load(u_ref, v_ref, ws_ref, bst_ref, vg_ref, og_ref):
    gu = _hslices(u_ref, GM_HEAD_DIM, GM_HEADS)
    gv = _hslices(v_ref, GM_HEAD_DIM, GM_HEADS)
    ws = [ws_ref[h] for h in range(GM_HEADS)]
    bs_cols = [bst_ref[:, h:h + 1] for h in range(GM_HEADS)]
    return gu, gv, ws, bs_cols, _hslices(vg_ref, GM_HEAD_DIM, GM_HEADS), _hslices(og_ref, GM_HEAD_DIM, GM_HEADS)


def _gmlp_specs():
    return [pl.BlockSpec((CHUNK, D_GM), lambda i: (i, COL_U // D_GM)),
            pl.BlockSpec((CHUNK, D_GM), lambda i: (i, COL_V // D_GM)),
            pl.BlockSpec((GM_HEADS, CHUNK, CHUNK), lambda i: (0, 0, 0)),
            pl.BlockSpec((CHUNK, GM_HEADS), lambda i: (0, 0)),
            pl.BlockSpec((1, D_GM), lambda i: (0, 0)),
            pl.BlockSpec((1, D_GM), lambda i: (0, 0))]


def _gmlp_fwd(proj, ycat, ws, bst, vg, og):
    t = proj.shape[0]

    def body(u_ref, v_ref, ws_ref, bst_ref, vg_ref, og_ref, ycat_ref, o_ref):
        del ycat_ref
        outs = _gmlp_chunk(*_gmlp_load(u_ref, v_ref, ws_ref, bst_ref, vg_ref, og_ref))
        for h in range(GM_HEADS):
            o_ref[:, h * GM_HEAD_DIM:(h + 1) * GM_HEAD_DIM] = _b(outs[h])

    return pl.pallas_call(
        body, name="gmlp_fwd", grid=(t // CHUNK,),
        in_specs=_gmlp_specs() + [ANY],
        out_specs=pl.BlockSpec((CHUNK, D_GM), lambda i: (i, D_SSD // D_GM)),
        out_shape=jax.ShapeDtypeStruct(ycat.shape, ycat.dtype),
        input_output_aliases={6: 0},
        compiler_params=_cp("parallel"),
    )(proj, proj, ws, bst, vg, og, ycat)


def _gmlp_bwd(dy, proj, ws, bst, vg, og, dproj):
    t = proj.shape[0]
    w = GM_HEAD_DIM

    def body(dy_ref, u_ref, v_ref, ws_ref, bst_ref, vg_ref, og_ref, dproj_ref,
             dgm_ref, dws_ref, dbst_ref, dvg_ref, dog_ref):
        del dproj_ref

        @pl.when(pl.program_id(0) == 0)
        def _():
            dws_ref[...] = jnp.zeros_like(dws_ref)
            dbst_ref[...] = jnp.zeros_like(dbst_ref)
            dvg_ref[...] = jnp.zeros_like(dvg_ref)
            dog_ref[...] = jnp.zeros_like(dog_ref)

        _, vjp = jax.vjp(_gmlp_chunk, *_gmlp_load(u_ref, v_ref, ws_ref, bst_ref, vg_ref, og_ref))
        dgu, dgv, dws, dbs, dvg, dog = vjp(_hslices(dy_ref, w, GM_HEADS))
        for h in range(GM_HEADS):
            dgm_ref[:, h * w:(h + 1) * w] = _b(dgu[h])
            dgm_ref[:, D_GM + h * w: D_GM + (h + 1) * w] = _b(dgv[h])
            dws_ref[h] += dws[h]
            dbst_ref[:, h:h + 1] += dbs[h]
            dvg_ref[:, h * w:(h + 1) * w] += dvg[h]
            dog_ref[:, h * w:(h + 1) * w] += dog[h]

    return pl.pallas_call(
        body, name="gmlp_bwd", grid=(t // CHUNK,),
        in_specs=[pl.BlockSpec((CHUNK, D_GM), lambda i: (i, 1))] + _gmlp_specs() + [ANY],
        out_specs=[pl.BlockSpec((CHUNK, 2 * D_GM), lambda i: (i, COL_U // (2 * D_GM))),
                   pl.BlockSpec((GM_HEADS, CHUNK, CHUNK), lambda i: (0, 0, 0)),
                   pl.BlockSpec((CHUNK, GM_HEADS), lambda i: (0, 0)),
                   pl.BlockSpec((1, D_GM), lambda i: (0, 0)),
                   pl.BlockSpec((1, D_GM), lambda i: (0, 0))],
        out_shape=[jax.ShapeDtypeStruct(dproj.shape, dproj.dtype), jax.ShapeDtypeStruct((GM_HEADS, CHUNK, CHUNK), F32),
                   jax.ShapeDtypeStruct((CHUNK, GM_HEADS), F32), jax.ShapeDtypeStruct((1, D_GM), F32),
                   jax.ShapeDtypeStruct((1, D_GM), F32)],
        input_output_aliases={7: 0},
        compiler_params=_cp("arbitrary"),
    )(dy, proj, proj, ws, bst, vg, og, dproj)


def _local_step(x, target, mods, lw, final_g, *, nseq, big_w, grad_sink):
    saved = []
    xin, delta, gate = x, None, None
    for l in range(DEPTH):
        w = lw[l]
        sh1, sc1, g1, sh2, sc2, g2 = mods[l]
        w_in = big_w(l, "w_in", xin)
        x0, h1 = _normmod_fwd(xin, delta, gate, w["norm1_g"], sc1, sh1, nseq=nseq, name=f"norm1_fwd_{l}")
        proj = _matmul(h1, w_in, name=f"mm_in_{l}")
        xbc = _ssd_conv_fwd(proj, w["ssd_conv_w"], w["ssd_conv_b"], nseq=nseq)
        ycat, hprev = _ssd_fwd(xbc, proj, w["ssd_dt_bias"], w["ssd_a_log"], w["ssd_d"], w["ssd_norm_g"], nseq=nseq)
        ycat = _gmlp_fwd(proj, ycat, w["gm_ws"], w["gm_bst"], w["gm_vnorm_g"], w["gm_out_g"])
        w_out = big_w(l, "w_out", ycat)
        mix = _matmul(ycat, w_out, name=f"mm_out_{l}")
        x1, h2 = _normmod_fwd(x0, mix, g1, w["norm2_g"], sc2, sh2, nseq=nseq, name=f"norm2_fwd_{l}")
        ff_up = big_w(l, "ff_up", h2)
        up = _matmul(h2, ff_up, name=f"mm_up_{l}")
        act = _ffn_act_fwd(up, w["ff_conv_w"], w["ff_conv_b"], nseq=nseq)
        ff_down = big_w(l, "ff_down", act)
        dn = _matmul(act, ff_down, name=f"mm_down_{l}")
        saved.append(dict(x0=x0, xin_delta=delta, xin_gate=gate, h1=h1, proj=proj, xbc=xbc, hprev=hprev, ycat=ycat,
                          mix=mix, x1=x1, h2=h2, up=up, act=act, dn=dn,
                          w_in=w_in, w_out=w_out, ff_up=ff_up, ff_down=ff_down))
        xin, delta, gate = x1, dn, g2

    loss, dx, ddelta, dgate, dfg = _final_loss(xin, delta, gate, final_g, target, nseq=nseq)

    small, dmods = [None] * DEPTH, [None] * DEPTH
    for l in reversed(range(DEPTH)):
        w, sv = lw[l], saved[l]
        sh1, sc1, g1, sh2, sc2, g2 = mods[l]
        dg2 = dgate
        dact = _matmul(ddelta, sv["ff_down"], tb=True, name=f"mm_down_dx_{l}")
        g_ff_down = _matmul(sv["act"], ddelta, ta=True, name=f"mm_down_dw_{l}")
        dup, dfcw, dfcb = _ffn_act_bwd(dact, sv["up"], w["ff_conv_w"], w["ff_conv_b"], nseq=nseq)
        dh2 = _matmul(dup, sv["ff_up"], tb=True, name=f"mm_up_dx_{l}")
        g_ff_up = _matmul(sv["h2"], dup, ta=True, name=f"mm_up_dw_{l}")
        zero = grad_sink(l, "ffn", dict(ff_down=g_ff_down, ff_up=g_ff_up), dh2)
        dx, dmix, dg1, dn2g, dsc2, dsh2 = _normmod_bwd(dh2, dx, sv["x1"], sv["mix"], g1, w["norm2_g"] + zero, sc2,
                                                       nseq=nseq, name=f"norm2_bwd_{l}")
        dycat = _matmul(dmix, sv["w_out"], tb=True, name=f"mm_out_dx_{l}")
        g_w_out = _matmul(sv["ycat"], dmix, ta=True, name=f"mm_out_dw_{l}")
        dxbc_act, dproj, ddtb, dalog, ddsk, dng = _ssd_bwd(dycat, sv["xbc"], sv["proj"], sv["hprev"], w["ssd_dt_bias"],
                                                          w["ssd_a_log"], w["ssd_d"], w["ssd_norm_g"], nseq=nseq)
        dproj, dscw, dscb = _ssd_conv_bwd(dxbc_act, sv["proj"], w["ssd_conv_w"], w["ssd_conv_b"], dproj, nseq=nseq)
        dproj, dws, dbst, dvg, dog = _gmlp_bwd(dycat, sv["proj"], w["gm_ws"], w["gm_bst"], w["gm_vnorm_g"], w["gm_out_g"], dproj)
        dh1 = _matmul(dproj, sv["w_in"], tb=True, name=f"mm_in_dx_{l}")
        g_w_in = _matmul(sv["h1"], dproj, ta=True, name=f"mm_in_dw_{l}")
        zero = grad_sink(l, "mix", dict(w_out=g_w_out, w_in=g_w_in), dh1)
        dx, ddelta, dgate, dn1g, dsc1, dsh1 = _normmod_bwd(dh1, dx, sv["x0"], sv["xin_delta"], sv["xin_gate"],
                                                           w["norm1_g"] + zero, sc1, nseq=nseq, name=f"norm1_bwd_{l}")
        small[l] = dict(norm1_g=dn1g, norm2_g=dn2g, ssd_norm_g=dng, gm_vnorm_g=dvg, gm_out_g=dog,
                        ssd_conv_w=dscw, ssd_conv_b=dscb, ff_conv_w=dfcw, ff_conv_b=dfcb,
                        ssd_dt_bias=ddtb, ssd_a_log=dalog, ssd_d=ddsk, gm_ws=dws, gm_bs=dbst.T)
        dmods[l] = jnp.concatenate([dsh1, dsc1, dg1, dsh2, dsc2, dg2], axis=-1)[:, 0, :]
    return loss, dx, small, dmods, dfg


def _all_gather(arrs, name):
    n = len(arrs)

    def body(*refs):
        ins, outs = refs[:n], refs[n:2 * n]
        send_sems, recv_sems, local_sems = refs[2 * n:]
        x, y, c = lax.axis_index("x"), lax.axis_index("y"), lax.axis_index("c")
        me, sibling = (x, y, c), (x, y, 1 - c)
        chips = [(1 - x, y), (x, 1 - y), (1 - x, 1 - y)]

        def copy(i, k, block, to, src=None):
            px, py, pc = block
            dst = outs[i].at[4 * px + 2 * py + pc]
            return pltpu.make_async_remote_copy(
                src_ref=dst if src is None else src, dst_ref=dst,
                send_sem=send_sems.at[7 * i + k], recv_sem=recv_sems.at[7 * i + k],
                device_id=to, device_id_type=MESH)

        mine = [pltpu.make_async_copy(ins[i], outs[i].at[4 * x + 2 * y + c], local_sems.at[i]) for i in range(n)]
        for cp in mine:
            cp.start()
        first = []
        for i in range(n):
            first.append(copy(i, 0, me, sibling, src=ins[i]))
            first += [copy(i, 1 + j, me, (*chip, c), src=ins[i]) for j, chip in enumerate(chips)]
        for cp in first:
            cp.start()
        passed = []
        for j, chip in enumerate(chips):
            for i in range(n):
                copy(i, 1 + j, (*chip, c), me).wait_recv()
                fwd = copy(i, 4 + j, (*chip, c), sibling)
                fwd.start()
                passed.append(fwd)
        for i in range(n):
            copy(i, 0, sibling, me).wait_recv()
            for j, chip in enumerate(chips):
                copy(i, 4 + j, (*chip, 1 - c), me).wait_recv()
        for cp in first + passed:
            cp.wait_send()
        for cp in mine:
            cp.wait()

    return pl.pallas_call(
        body, name=name,
        in_specs=[ANY] * n, out_specs=[ANY] * n,
        out_shape=[jax.ShapeDtypeStruct((N_DEV,) + a.shape, a.dtype) for a in arrs],
        scratch_shapes=[pltpu.SemaphoreType.DMA((7 * n,)), pltpu.SemaphoreType.DMA((7 * n,)),
                        pltpu.SemaphoreType.DMA((n,))],
    )(*arrs)


def _exchange_sibling(arrs, name):
    n = len(arrs)

    def body(*refs):
        ins, outs = refs[:n], refs[n:2 * n]
        send_sems, recv_sems = refs[2 * n:]
        x, y, c = lax.axis_index("x"), lax.axis_index("y"), lax.axis_index("c")
        copies = []
        for i in range(n):
            for k in range(4):
                copies.append(pltpu.make_async_remote_copy(
                    src_ref=ins[i].at[2 * k + (1 - c)], dst_ref=outs[i].at[k],
                    send_sem=send_sems.at[4 * i + k], recv_sem=recv_sems.at[4 * i + k],
                    device_id=(x, y, 1 - c), device_id_type=MESH))
        for cp in copies:
            cp.start()
        for cp in copies:
            cp.wait_recv()
        for cp in copies:
            cp.wait_send()

    return pl.pallas_call(
        body, name=name,
        in_specs=[ANY] * n, out_specs=[ANY] * n,
        out_shape=[jax.ShapeDtypeStruct((4,) + a.shape[1:], a.dtype) for a in arrs],
        scratch_shapes=[pltpu.SemaphoreType.DMA((4 * n,)), pltpu.SemaphoreType.DMA((4 * n,))],
    )(*arrs)


def _exchange_chips(arrs, name):
    n = len(arrs)

    def body(*refs):
        ins, outs = refs[:n], refs[n:2 * n]
        send_sems, recv_sems = refs[2 * n:]
        x, y, c = lax.axis_index("x"), lax.axis_index("y"), lax.axis_index("c")
        chips = [(1 - x, y), (x, 1 - y), (1 - x, 1 - y)]
        copies = []
        for i in range(n):
            for j, (cx, cy) in enumerate(chips):
                copies.append(pltpu.make_async_remote_copy(
                    src_ref=ins[i].at[2 * cx + cy], dst_ref=outs[i].at[j],
                    send_sem=send_sems.at[3 * i + j], recv_sem=recv_sems.at[3 * i + j],
                    device_id=(cx, cy, c), device_id_type=MESH))
        for cp in copies:
            cp.start()
        for cp in copies:
            cp.wait_recv()
        for cp in copies:
            cp.wait_send()

    return pl.pallas_call(
        body, name=name,
        in_specs=[ANY] * n, out_specs=[ANY] * n,
        out_shape=[jax.ShapeDtypeStruct((3,) + a.shape[1:], a.dtype) for a in arrs],
        scratch_shapes=[pltpu.SemaphoreType.DMA((3 * n,)), pltpu.SemaphoreType.DMA((3 * n,))],
    )(*arrs)


def _add_sibling(a, r, pos, name):
    _, depth, rows, cols = a.shape
    tr = _tile(rows, 256) if rows % 8 == 0 else rows
    a3 = a.reshape(N_DEV * depth, rows, cols)
    r3 = r.reshape(4 * depth, rows, cols)

    def body(pos_ref, a_ref, r_ref, o_ref):
        o_ref[...] = a_ref[...] + r_ref[...]

    out = pl.pallas_call(
        body, name=name,
        grid_spec=pltpu.PrefetchScalarGridSpec(
            num_scalar_prefetch=1, grid=(4 * depth, rows // tr),
            in_specs=[pl.BlockSpec((1, tr, cols), lambda q, i, p: ((2 * (q // depth) + p[0]) * depth + q % depth, i, 0)),
                      pl.BlockSpec((1, tr, cols), lambda q, i, p: (q, i, 0))],
            out_specs=pl.BlockSpec((1, tr, cols), lambda q, i, p: (q, i, 0))),
        out_shape=jax.ShapeDtypeStruct((4 * depth, rows, cols), F32),
        compiler_params=_cp("parallel", "parallel"),
    )(pos, a3, r3)
    return out.reshape(4, depth, rows, cols)


HBM = pl.BlockSpec(memory_space=pltpu.HBM)
SEM = pl.BlockSpec(memory_space=pltpu.SEMAPHORE)
EFFECT = pltpu.SideEffectType.DATAFLOW_SIDE_EFFECTING


def _peer(k):
    x, y, c = lax.axis_index("x"), lax.axis_index("y"), lax.axis_index("c")
    return (1 - x if k & 4 else x, 1 - y if k & 2 else y, 1 - c if k & 1 else c)


def _xc_copies(scatter, srcs, lands, send_sems, recv_sems):
    x, y, c = lax.axis_index("x"), lax.axis_index("y"), lax.axis_index("c")
    copies = []
    for i in range(len(srcs)):
        for k in range(1, N_DEV):
            px, py, pc = _peer(k)
            src = srcs[i].at[4 * px + 2 * py + pc] if scatter else srcs[i]
            dst = lands[i].at[k - 1] if scatter else lands[i].at[4 * x + 2 * y + c]
            copies.append(pltpu.make_async_remote_copy(
                src_ref=src, dst_ref=dst, send_sem=send_sems[i].at[k - 1], recv_sem=recv_sems[i].at[k - 1],
                device_id=(px, py, pc), device_id_type=MESH))
    return copies


def _xc_start(scatter, arrs, after, name):
    n = len(arrs)
    lands = [lax.empty((N_DEV - 1,) + a.shape[1:] if scatter else (N_DEV,) + a.shape, a.dtype) for a in arrs]

    def body(*refs):
        srcs, lnd = refs[:n], refs[n:2 * n]
        send_sems, recv_sems = refs[2 * n + 1:3 * n + 1], refs[3 * n + 1:4 * n + 1]
        token = refs[6 * n + 1]
        for cp in _xc_copies(scatter, srcs, lnd, send_sems, recv_sems):
            cp.start()
        token[...] = jnp.zeros_like(token)

    outs = pl.pallas_call(
        body, name=name,
        out_shape=[pltpu.SemaphoreType.DMA((N_DEV - 1,))] * (2 * n)
        + [pltpu.HBM(a.shape, a.dtype) for a in arrs] + [pltpu.HBM(a.shape, a.dtype) for a in lands]
        + [jax.ShapeDtypeStruct((8, 128), F32)],
        in_specs=[HBM] * (2 * n) + [ANY],
        out_specs=[SEM] * (2 * n) + [HBM] * (2 * n) + [pl.BlockSpec(memory_space=pltpu.VMEM)],
        input_output_aliases={i: 2 * n + i for i in range(2 * n)},
        compiler_params=pltpu.CompilerParams(has_side_effects=EFFECT),
    )(*[pltpu.with_memory_space_constraint(a, pltpu.HBM) for a in list(arrs) + lands], after)
    return outs[:n], outs[n:2 * n], outs[2 * n:3 * n], outs[3 * n:4 * n], outs[4 * n][0, 0]


def _xc_wait(scatter, send_sems, recv_sems, srcs, lands, after, name):
    n = len(srcs)

    def body(*refs):
        s_refs, l_refs = refs[:n], refs[n:2 * n]
        ss, rs = refs[2 * n:3 * n], refs[3 * n:4 * n]
        for cp in _xc_copies(scatter, s_refs, l_refs, ss, rs):
            cp.wait_send()
            cp.wait_recv()

    outs = pl.pallas_call(
        body, name=name,
        out_shape=[pltpu.HBM(a.shape, a.dtype) for a in list(srcs) + list(lands)],
        in_specs=[HBM] * (2 * n) + [SEM] * (2 * n) + [ANY],
        out_specs=[HBM] * (2 * n),
        input_output_aliases={i: i for i in range(2 * n)},
        compiler_params=pltpu.CompilerParams(has_side_effects=EFFECT),
    )(*srcs, *lands, *send_sems, *recv_sems, after)
    return outs[:n], outs[n:]


def _adamw_math(w, g, m, v):
    m = ADAM_B1 * m + (1.0 - ADAM_B1) * g
    v = ADAM_B2 * v + (1.0 - ADAM_B2) * (g * g)
    m_hat = m / (1.0 - ADAM_B1 ** ADAM_STEP)
    v_hat = v / (1.0 - ADAM_B2 ** ADAM_STEP)
    delta = -ADAM_LR * (m_hat / (jnp.sqrt(v_hat) + ADAM_EPS) + ADAM_WD * w)
    return delta, m, v


def _adamw_sharded(parts, w, m, v, pos, name):
    depth, rows, cols = w.shape
    tr = _tile(rows, 256) if rows % 8 == 0 else rows
    npart = len(parts)

    def body(pos_ref, *refs):
        prefs = refs[:npart]
        w_ref, m_ref, v_ref, g_out, d_out, m_out, v_out = refs[npart:]
        g = prefs[0][...]
        for pr in prefs[1:]:
            g = g + pr[...]
        delta, mn, vn = _adamw_math(w_ref[...], g, m_ref[...], v_ref[...])
        g_out[...] = g
        d_out[...] = delta
        m_out[...] = mn
        v_out[...] = vn

    def part_spec(fn):
        return pl.BlockSpec((1, tr, cols), lambda l, i, p: (fn(p) * depth + l, i, 0))

    blk = pl.BlockSpec((1, tr, cols), lambda l, i, p: (l, i, 0))
    shp = jax.ShapeDtypeStruct((depth, rows, cols), F32)
    return pl.pallas_call(
        body, name=name,
        grid_spec=pltpu.PrefetchScalarGridSpec(
            num_scalar_prefetch=1, grid=(depth, rows // tr),
            in_specs=[part_spec(fn) for _, fn in parts] + [blk, blk, blk],
            out_specs=[blk, blk, blk, blk]),
        out_shape=[shp, shp, shp, shp],
        compiler_params=_cp("parallel", "parallel"),
    )(pos, *[a for a, _ in parts], w, m, v)


def _adamw_layer(parts, w, m, v, pos, layer, prev, name):
    depth, rows, cols = w.shape
    tr = _tile(rows, 256) if rows % 8 == 0 else rows
    npart = len(parts)
    nprev = 0 if prev is None else 4

    def body(pos_ref, *refs):
        prefs = refs[:npart]
        w_ref, m_ref, v_ref = refs[npart:npart + 3]
        g_out, d_out, m_out, v_out = refs[npart + 3 + nprev:]
        g = prefs[0][...]
        for pr in prefs[1:]:
            g = g + pr[...]
        delta, mn, vn = _adamw_math(w_ref[...], g, m_ref[...], v_ref[...])
        g_out[...] = g
        d_out[...] = delta
        m_out[...] = mn
        v_out[...] = vn

    def part_spec(fn):
        return pl.BlockSpec((1, tr, cols), lambda i, p: (fn(p), i, 0))

    blk = pl.BlockSpec((1, tr, cols), lambda i, p: (layer, i, 0))
    shp = jax.ShapeDtypeStruct((depth, rows, cols), F32)
    first_prev = 1 + npart + 3
    return pl.pallas_call(
        body, name=name,
        grid_spec=pltpu.PrefetchScalarGridSpec(
            num_scalar_prefetch=1, grid=(rows // tr,),
            in_specs=[part_spec(fn) for _, fn in parts] + [blk, blk, blk] + [ANY] * nprev,
            out_specs=[blk, blk, blk, blk]),
        out_shape=[shp, shp, shp, shp],
        input_output_aliases={first_prev + j: j for j in range(nprev)},
        compiler_params=_cp("parallel"),
    )(pos, *[a for a, _ in parts], w, m, v, *(prev or ()))


_P1024 = ["norm1_g", "norm2_g", "ssd_norm_g", "gm_vnorm_g", "gm_out_g"]
_P16 = ["ssd_dt_bias", "ssd_a_log", "ssd_d"]


def _adamw_small(gath, wmv):
    names = list(wmv.keys())
    classes = list(gath.keys())
    flat_in = [gath[k] for k in classes]
    for nme in names:
        flat_in += list(wmv[nme])
    out_shapes = []
    for nme in names:
        out_shapes += [jax.ShapeDtypeStruct(wmv[nme][0].shape, F32)] * 4
    out_shapes += [jax.ShapeDtypeStruct((DEPTH, SSD_CONV, CONV_DIM), F32), jax.ShapeDtypeStruct((DEPTH, FF_CONV, D_FF), F32)]
    scratch = [pltpu.VMEM(gath[k].shape[1:], F32) for k in classes]
    ncls = len(classes)

    def body(*refs):
        g_refs = dict(zip(classes, refs[:ncls]))
        pos = ncls
        w_refs = {}
        for nme in names:
            w_refs[nme] = refs[pos:pos + 3]
            pos += 3
        o_refs = {}
        for nme in names:
            o_refs[nme] = refs[pos:pos + 4]
            pos += 4
        scw_out, fcw_out = refs[pos], refs[pos + 1]
        s_refs = dict(zip(classes, refs[pos + 2:]))
        for k in classes:
            acc = g_refs[k][0]
            for dev in range(1, N_DEV):
                acc = acc + g_refs[k][dev]
            s_refs[k][...] = acc

        def apply(nme, grad_of):
            w_ref, m_ref, v_ref = w_refs[nme]
            g_out, d_out, m_out, v_out = o_refs[nme]
            shape = w_ref.shape
            if len(shape) == 2:
                idxs = [(slice(l, l + 1),) for l in range(shape[0])]
            elif len(shape) == 3:
                idxs = [(l,) for l in range(shape[0])]
            else:
                idxs = [(l, h) for l in range(shape[0]) for h in range(shape[1])]
            for n_i, ix in enumerate(idxs):
                g = grad_of(n_i)
                delta, mn, vn = _adamw_math(w_ref[ix], g, m_ref[ix], v_ref[ix])
                g_out[ix] = g
                d_out[ix] = delta
                m_out[ix] = mn
                v_out[ix] = vn

        s1024, s1536, s2816, s16, s128, s6144 = (s_refs[k] for k in classes)
        for n_i, nme in enumerate(_P1024):
            apply(nme, lambda l, b=2 * n_i: s1024[b + l:b + l + 1, :])
        apply("final_g", lambda l: s1024[10:11, :])
        apply("ssd_conv_b", lambda l: s1536[8 + l:9 + l, :])
        apply("ff_conv_b", lambda l: s2816[6 + l:7 + l, :])
        for n_i, nme in enumerate(_P16):
            apply(nme, lambda l, b=2 * n_i: s16[b + l:b + l + 1, :])
        apply("gm_ws", lambda q: s128[q * CHUNK:(q + 1) * CHUNK, :])
        apply("gm_bs", lambda l: s128[2048 + 8 * l:2048 + 8 * (l + 1), :])
        apply("ada_b", lambda l: s6144[2 * l:2 * l + 1, :] + s6144[2 * l + 1:2 * l + 2, :])
        for l in range(DEPTH):
            scw_out[l] = s1536[SSD_CONV * l:SSD_CONV * (l + 1), :]
            fcw_out[l] = s2816[FF_CONV * l:FF_CONV * (l + 1), :]

    outs = pl.pallas_call(
        body, name="adamw_small",
        out_shape=out_shapes,
        scratch_shapes=scratch,
        compiler_params=pltpu.CompilerParams(vmem_limit_bytes=VMEM_LIMIT),
    )(*flat_in)
    res = {nme: tuple(outs[4 * i:4 * i + 4]) for i, nme in enumerate(names)}
    return res, outs[-2], outs[-1]


_WEIGHTS = ['ada_w', 'ada_b', 'norm1_g', 'norm2_g', 'w_in', 'ssd_conv_w', 'ssd_conv_b', 'ssd_dt_bias', 'ssd_a_log',
            'ssd_d', 'ssd_norm_g', 'gm_vnorm_g', 'gm_ws', 'gm_bs', 'gm_out_g', 'w_out', 'ff_up', 'ff_conv_w',
            'ff_conv_b', 'ff_down', 'final_g']


_O_XBC, _O_DT, _O_GM = D_SSD, D_SSD + CONV_DIM, D_SSD + CONV_DIM + SSD_HEADS


def _full_weight(name, g):
    if name in ("w_out", "ff_down"):
        return g.reshape(g.shape[0] * g.shape[1], g.shape[2])
    k = g.shape[1]
    full = jnp.transpose(g, (1, 0, 2)).reshape(k, -1)
    if name == "ff_up":
        return _interleave_ff(full)
    zpad = jnp.zeros((k, N_INP - N_IN), full.dtype)
    return jnp.concatenate([full[:, _O_GM:], full[:, :_O_XBC], full[:, _O_XBC:_O_DT], full[:, _O_DT:_O_GM], zpad], axis=1)


def _by_owner(name, grad):
    if name in ("w_out", "ff_down"):
        return grad.reshape(N_DEV, grad.shape[0] // N_DEV, grad.shape[1])
    if name == "ff_up":
        grad = _deinterleave_ff(grad)
    else:
        grad = jnp.concatenate([grad[:, COL_Z:COL_XBC], grad[:, COL_XBC:COL_DT], grad[:, COL_DT:COL_DT + SSD_HEADS],
                                grad[:, :COL_Z]], axis=1)
    k, n8 = grad.shape
    return jnp.transpose(grad.reshape(k, N_DEV, n8 // N_DEV), (1, 0, 2))


def _interleave_ff(w):
    lead = w.shape[:-1]
    return jnp.swapaxes(w.reshape(lead + (2, D_FF // CONV_TC, CONV_TC)), -3, -2).reshape(lead + (2 * D_FF,))


def _deinterleave_ff(w):
    lead = w.shape[:-1]
    return jnp.swapaxes(w.reshape(lead + (D_FF // CONV_TC, 2, CONV_TC)), -3, -2).reshape(lead + (2 * D_FF,))


def kernel(x, c, ada_w, ada_b, norm1_g, norm2_g, w_in, ssd_conv_w, ssd_conv_b, ssd_dt_bias, ssd_a_log, ssd_d, ssd_norm_g, gm_vnorm_g, gm_ws, gm_bs, gm_out_g, w_out, ff_up, ff_conv_w, ff_conv_b, ff_down, final_g, loss_target, m_ada_w, m_ada_b, m_norm1_g, m_norm2_g, m_w_in, m_ssd_conv_w, m_ssd_conv_b, m_ssd_dt_bias, m_ssd_a_log, m_ssd_d, m_ssd_norm_g, m_gm_vnorm_g, m_gm_ws, m_gm_bs, m_gm_out_g, m_w_out, m_ff_up, m_ff_conv_w, m_ff_conv_b, m_ff_down, m_final_g, v_ada_w, v_ada_b, v_norm1_g, v_norm2_g, v_w_in, v_ssd_conv_w, v_ssd_conv_b, v_ssd_dt_bias, v_ssd_a_log, v_ssd_d, v_ssd_norm_g, v_gm_vnorm_g, v_gm_ws, v_gm_bs, v_gm_out_g, v_w_out, v_ff_up, v_ff_conv_w, v_ff_conv_b, v_ff_down, v_final_g):
    given = dict(locals())
    wts = {n: given[n] for n in _WEIGHTS}
    mom = {n: given["m_" + n] for n in _WEIGHTS}
    var = {n: given["v_" + n] for n in _WEIGHTS}
    nseq, seq, d = x.shape
    ix, iy, ic = lax.axis_index("x"), lax.axis_index("y"), lax.axis_index("c")
    me = 4 * ix + 2 * iy + ic
    me_arr = me.astype(jnp.int32).reshape(1)

    g_win0, g_scw, g_fcw, c_all = _all_gather([_b(w_in[0]), ssd_conv_w, ff_conv_w, c], "gather_first")
    later = [(0, "w_out"), (0, "ff_up"), (0, "ff_down"), (1, "w_in"), (1, "w_out"), (1, "ff_up"), (1, "ff_down")]
    ag_ssem, ag_rsem, ag_src, ag_land, ag_zero = _xc_start(False, [_b(wts[n][l]) for l, n in later], g_win0, "ag_start")
    ag_groups = {(0, "w_out"): [0], (0, "ff_up"): [1, 2], (1, "w_in"): [3, 4, 5, 6]}
    big_cache = {(0, "w_in"): _full_weight("w_in", g_win0)}

    def big_w(l, name, after):
        if (l, name) not in big_cache:
            idx = ag_groups[(l, name)]
            pick = lambda seq_: [seq_[i] for i in idx]
            srcs, lands = _xc_wait(False, pick(ag_ssem), pick(ag_rsem), pick(ag_src), pick(ag_land), after,
                                   f"ag_wait_{l}_{name}")
            for i, src, land in zip(idx, srcs, lands):
                big_cache[later[i]] = _full_weight(later[i][1], lax.dynamic_update_index_in_dim(land, src, me, 0))
        return big_cache[(l, name)]

    scw_f = jnp.transpose(g_scw, (1, 2, 0, 3)).reshape(DEPTH, SSD_CONV, CONV_DIM)
    fcw_f = jnp.transpose(g_fcw, (1, 2, 0, 3)).reshape(DEPTH, FF_CONV, D_FF)
    c_all = c_all.reshape(N_DEV * nseq, d)

    n_ada = ada_w.shape[2]
    ada_b_shard = lax.dynamic_slice_in_dim(ada_b, me * n_ada, n_ada, axis=1).reshape(DEPTH, 1, n_ada)
    mod_part, c_act = _ada_fwd(c_all, ada_w, ada_b_shard)
    (mod_g,) = _all_gather([mod_part], "gather_mod")
    mod_all = jnp.transpose(mod_g, (1, 2, 0, 3)).reshape(DEPTH, N_DEV * nseq, N_MOD * d)
    mod_mine = lax.dynamic_slice_in_dim(mod_all, me * nseq, nseq, axis=1)
    mods = [[mod_mine[l, :, k * d:(k + 1) * d].reshape(nseq, 1, d) for k in range(N_MOD)] for l in range(DEPTH)]

    lw = []
    for l in range(DEPTH):
        lw.append(dict(
            norm1_g=norm1_g[l:l + 1] + (ag_zero if l == 0 else 0.0), norm2_g=norm2_g[l:l + 1], ssd_conv_w=scw_f[l],
            ssd_conv_b=ssd_conv_b[l:l + 1], ssd_dt_bias=ssd_dt_bias[l:l + 1], ssd_a_log=ssd_a_log[l:l + 1],
            ssd_d=ssd_d[l:l + 1], ssd_norm_g=ssd_norm_g[l:l + 1], gm_vnorm_g=gm_vnorm_g[l:l + 1], gm_ws=gm_ws[l],
            gm_bst=gm_bs[l].T, gm_out_g=gm_out_g[l:l + 1], ff_conv_w=fcw_f[l], ff_conv_b=ff_conv_b[l:l + 1]))

    outs = {}
    pending = {}
    stash = {}

    def rs_start(tag, names, grads, after):
        ssem, rsem, srcs, lands, zero = _xc_start(True, [_by_owner(n, grads[n]) for n in names], after, f"rs_start_{tag}")
        pending[tag] = (names, ssem, rsem, srcs, lands)
        return zero

    def rs_finish(tag, layer, after):
        names, ssem, rsem, srcs, lands = pending.pop(tag)
        srcs, lands = _xc_wait(True, ssem, rsem, srcs, lands, after, f"rs_wait_{tag}")
        for nme, own, land in zip(names, srcs, lands):
            parts = [(own, lambda p: p[0])] + [(land, lambda p, k=k: k) for k in range(N_DEV - 1)]
            outs[nme] = _adamw_layer(parts, wts[nme], mom[nme], var[nme], me_arr, layer, outs.get(nme),
                                     f"adamw_{nme}_{layer}")
        return outs[names[-1]][0]

    def grad_sink(l, group, grads, after):
        if l == 1 and group == "ffn":
            stash.update(grads)
            return 0.0
        if l == 1:
            return rs_start("l1", ["ff_down", "ff_up", "w_out", "w_in"], {**stash, **grads}, after)
        if group == "ffn":
            return rs_start("l0_ffn", ["ff_down", "ff_up"], grads, after)
        zero = rs_start("l0_mix", ["w_out", "w_in"], grads, after)
        rs_finish("l1", 1, after)
        return zero

    loss_p, grad_x, small, dmods, dfg = _local_step(
        x.reshape(nseq * seq, d), loss_target.reshape(nseq * seq, d), mods, lw, final_g.reshape(1, d), nseq=nseq,
        big_w=big_w, grad_sink=grad_sink)
    loss = lax.psum(loss_p[0, 0], ("x", "y", "c"))

    def rows(name):
        return [small[l][name] for l in range(DEPTH)]

    p1024 = jnp.concatenate(sum([rows(n) for n in _P1024], []) + [dfg], axis=0)
    p1536 = jnp.concatenate(rows("ssd_conv_w") + rows("ssd_conv_b"), axis=0)
    p2816 = jnp.concatenate(rows("ff_conv_w") + rows("ff_conv_b"), axis=0)
    p16 = jnp.concatenate(sum([rows(n) for n in _P16], []), axis=0)
    p128 = jnp.concatenate([small[l]["gm_ws"].reshape(GM_HEADS * CHUNK, CHUNK) for l in range(DEPTH)] + rows("gm_bs"), axis=0)
    p6144 = jnp.concatenate(dmods, axis=0)
    gathered = _all_gather([p1024, p1536, p2816, p16, p128, p6144], "gather_small")
    gath = dict(zip(["p1024", "p1536", "p2816", "p16", "p128", "p6144"], gathered))

    dmod_all = jnp.transpose(gath["p6144"].reshape(N_DEV, DEPTH, nseq, N_MOD * d), (1, 0, 2, 3)).reshape(
        DEPTH, N_DEV * nseq, N_MOD * d)
    small_names = _P1024 + ["final_g", "ssd_conv_b", "ff_conv_b"] + _P16 + ["gm_ws", "gm_bs", "ada_b"]
    wmv = {}
    for nme in small_names:
        if nme == "final_g":
            wmv[nme] = tuple(a.reshape(1, d) for a in (wts[nme], mom[nme], var[nme]))
        else:
            wmv[nme] = (wts[nme], mom[nme], var[nme])
    small_out, scw_full, fcw_full = _adamw_small(gath, wmv)
    done = rs_finish("l0_ffn", 0, scw_full)
    rs_finish("l0_mix", 0, done)
    for nme in small_names:
        outs[nme] = small_out[nme]
    outs["final_g"] = tuple(a.reshape(d) for a in outs["final_g"])

    n_scw, n_fcw = ssd_conv_w.shape[2], ff_conv_w.shape[2]
    g_scw_mine = lax.dynamic_slice_in_dim(scw_full, me * n_scw, n_scw, axis=2)
    g_fcw_mine = lax.dynamic_slice_in_dim(fcw_full, me * n_fcw, n_fcw, axis=2)
    outs["ssd_conv_w"] = _adamw_sharded([(g_scw_mine, lambda p: 0)], ssd_conv_w, m_ssd_conv_w, v_ssd_conv_w, me_arr, "adamw_ssd_conv_w")
    outs["ff_conv_w"] = _adamw_sharded([(g_fcw_mine, lambda p: 0)], ff_conv_w, m_ff_conv_w, v_ff_conv_w, me_arr, "adamw_ff_conv_w")

    dmod_cols = _b(lax.dynamic_slice_in_dim(dmod_all, me * n_ada, n_ada, axis=2))
    g_ada = jnp.stack([_matmul(c_act, dmod_cols[l], ta=True, name=f"mm_ada_dw_{l}") for l in range(DEPTH)])
    outs["ada_w"] = _adamw_sharded([(g_ada, lambda p: 0)], ada_w, m_ada_w, v_ada_w, me_arr, "adamw_ada_w")

    result = [loss, grad_x.reshape(nseq, seq, d)]
    for k in range(4):
        result += [outs[n][k] for n in _WEIGHTS]
    return tuple(result)
```

```python
import functools
import math

import jax
import jax.numpy as jnp
from jax import lax
from jax.experimental import pallas as pl
from jax.experimental.pallas import tpu as pltpu

F32 = jnp.float32
BF16 = jnp.bfloat16

N_DEV = 8
D_MODEL = 1024
DEPTH = 2
CHUNK = 128
SSD_HEADS = 16
SSD_HEAD_DIM = 64
SSD_GROUPS = 2
HEADS_PER_GROUP = SSD_HEADS // SSD_GROUPS
GROUP_WIDTH = HEADS_PER_GROUP * SSD_HEAD_DIM
D_STATE = 128
D_SSD = 1024
CONV_DIM = 1536
SSD_CONV = 4
GM_HEADS = 8
GM_HEAD_DIM = 128
D_GM = 1024
D_FF = 2816
FF_CONV = 3
N_IN = 4624
N_MOD = 6
EPS = 1e-6

N_INP = 5120
COL_U, COL_V, COL_Z, COL_XBC, COL_DT = 0, 1024, 2048, 3072, 4608
DT_BLOCK = 512

ADAM_LR = 0.001
ADAM_B1 = 0.9
ADAM_B2 = 0.999
ADAM_EPS = 1e-08
ADAM_WD = 0.01
ADAM_STEP = 10

VMEM_LIMIT = 56 * 1024 * 1024
MESH = pl.DeviceIdType.MESH
ANY = pl.BlockSpec(memory_space=pl.ANY)


def _cp(*sem):
    return pltpu.CompilerParams(dimension_semantics=sem, vmem_limit_bytes=VMEM_LIMIT)


def _tile(n, pref):
    if n <= pref or n % 128:
        return n
    best = 128
    for t in range(128, pref + 1, 128):
        if n % t == 0:
            best = t
    return best


def _silu(x):
    return x * jax.nn.sigmoid(x)


def _gelu(x):
    return 0.5 * x * (1.0 + lax.erf(x * (1.0 / math.sqrt(2.0))))


def _softplus(x):
    return jnp.maximum(x, 0.0) + jnp.log1p(jnp.exp(-jnp.abs(x)))


def _rms(x, g, width):
    return x * lax.rsqrt(jnp.sum(x * x, axis=-1, keepdims=True) / width + EPS) * g


def _b(x):
    return x.astype(BF16)


_NN = (((1,), (0,)), ((), ()))
_NT = (((1,), (1,)), ((), ()))
_TN = (((0,), (0,)), ((), ()))


def _dg(a, b, dn):
    return lax.dot_general(_b(a), _b(b), dn, preferred_element_type=F32)


@jax.custom_vjp
def _bdot(a, b):
    return _dg(a, b, _NN)


def _bdot_fwd(a, b):
    return _dg(a, b, _NN), (a, b)


def _bdot_bwd(res, ct):
    a, b = res
    return _dg(ct, b, _NT), _dg(a, ct, _TN)


_bdot.defvjp(_bdot_fwd, _bdot_bwd)


@jax.custom_vjp
def _bdot_nt(a, b):
    return _dg(a, b, _NT)


def _bdot_nt_fwd(a, b):
    return _dg(a, b, _NT), (a, b)


def _bdot_nt_bwd(res, ct):
    a, b = res
    return _dg(ct, b, _NN), _dg(ct, a, _TN)


_bdot_nt.defvjp(_bdot_nt_fwd, _bdot_nt_bwd)


@jax.custom_vjp
def _bdot_tn(a, b):
    return _dg(a, b, _TN)


def _bdot_tn_fwd(a, b):
    return _dg(a, b, _TN), (a, b)


def _bdot_tn_bwd(res, ct):
    a, b = res
    return _dg(b, ct, _NT), _dg(a, ct, _NN)


_bdot_tn.defvjp(_bdot_tn_fwd, _bdot_tn_bwd)


def _tri(n, lower):
    r = lax.broadcasted_iota(jnp.int32, (n, n), 0)
    c = lax.broadcasted_iota(jnp.int32, (n, n), 1)
    return ((r >= c) if lower else (r <= c)).astype(F32)


def _eye(n):
    r = lax.broadcasted_iota(jnp.int32, (n, n), 0)
    c = lax.broadcasted_iota(jnp.int32, (n, n), 1)
    return (r == c).astype(F32)


def _hdot(a, b, dn):
    return lax.dot_general(a, b, dn, precision=lax.Precision.HIGHEST, preferred_element_type=F32)


@jax.custom_vjp
def _cumsum_rows(x):
    return _hdot(_tri(x.shape[0], True), x, _NN)


def _cumsum_rows_fwd(x):
    return _cumsum_rows(x), None


def _cumsum_rows_bwd(_, ct):
    return (_hdot(_tri(ct.shape[0], False), ct, _NN),)


_cumsum_rows.defvjp(_cumsum_rows_fwd, _cumsum_rows_bwd)


@jax.custom_vjp
def _transpose(x):
    return _hdot(_eye(x.shape[1]), x, _NT)


def _transpose_fwd(x):
    return _transpose(x), None


def _transpose_bwd(_, ct):
    return (_hdot(_eye(ct.shape[1]), ct, _NT),)


_transpose.defvjp(_transpose_fwd, _transpose_bwd)


def _matmul(a, b, *, ta=False, tb=False, name, dep=None):
    pieces = list(a) if isinstance(a, (list, tuple)) else [a]
    npc = len(pieces)
    rows, width = pieces[0].shape
    assert all(p.shape == (rows, width) for p in pieces)
    if ta:
        k_dim, m_dim = rows, width * npc
    else:
        m_dim, k_dim = rows, width * npc
    if tb:
        n_dim, kb = b.shape
    else:
        kb, n_dim = b.shape
    assert kb == k_dim, (pieces[0].shape, npc, b.shape, ta, tb)
    tm, tn, tk = _tile(m_dim, 1024), _tile(n_dim, 1536), _tile(k_dim, 1536)
    if npc > 1:
        if ta:
            tm = _tile(width, 1024)
        else:
            tk = _tile(width, 1536)
    ni, nj, nk = m_dim // tm, n_dim // tn, k_dim // tk
    per = width // (tm if ta else tk)
    dn = (((0 if ta else 1,), (1 if tb else 0,)), ((), ()))

    a_bytes, b_bytes = m_dim * k_dim, k_dim * n_dim
    m_outer = nk > 1 or a_bytes + b_bytes * ni <= b_bytes + a_bytes * nj
    if m_outer:
        ij = lambda o, n, k: (o, n)
        grid = (ni, nj, nk)
    else:
        ij = lambda o, n, k: (n, o)
        grid = (nj, ni, nk)

    def body(*refs):
        a_refs, b_ref, o_ref = refs[:npc], refs[npc], refs[-1]
        k = pl.program_id(2)
        i = pl.program_id(0 if m_outer else 1)
        along = i if ta else k

        def step(a_ref):
            p = lax.dot_general(a_ref[...], b_ref[...], dn, preferred_element_type=F32)
            if nk == 1:
                o_ref[...] = p
            else:
                @pl.when(k == 0)
                def _():
                    o_ref[...] = p

                @pl.when(k > 0)
                def _():
                    o_ref[...] += p

        if npc == 1:
            step(a_refs[0])
        else:
            for pc in range(npc):
                pl.when((along >= pc * per) & (along < (pc + 1) * per))(functools.partial(step, a_refs[pc]))

    def a_map(pc, o, n, k):
        i, _ = ij(o, n, k)
        along = i if ta else k
        if npc > 1:
            along = jnp.clip(along - pc * per, 0, per - 1)
        return (k, along) if ta else (i, along)

    def b_map(o, n, k):
        _, j = ij(o, n, k)
        return (j, k) if tb else (k, j)

    extra = [] if dep is None else [dep]
    return pl.pallas_call(
        body, name=name,
        grid=grid,
        in_specs=[pl.BlockSpec((tk, tm) if ta else (tm, tk), functools.partial(a_map, pc)) for pc in range(npc)]
        + [pl.BlockSpec((tn, tk) if tb else (tk, tn), b_map)] + [ANY] * len(extra),
        out_specs=pl.BlockSpec((tm, tn), lambda o, n, k: ij(o, n, k)),
        out_shape=jax.ShapeDtypeStruct((m_dim, n_dim), F32),
        compiler_params=_cp("parallel", "parallel", "arbitrary"),
    )(*pieces, b, *extra)


def _ada_fwd(c_all, ada_w, ada_b_shard):
    depth, d, n = ada_w.shape
    nb = c_all.shape[0]

    def body(c_ref, w_ref, b_ref, o_ref, ca_ref):
        ca = _silu(c_ref[...])
        ca_ref[...] = _b(ca)
        o_ref[0] = _dg(ca, w_ref[0], _NN) + b_ref[0]

    return pl.pallas_call(
        body, name="ada_fwd",
        grid=(depth,),
        in_specs=[pl.BlockSpec((nb, d), lambda l: (0, 0)),
                  pl.BlockSpec((1, d, n), lambda l: (l, 0, 0)),
                  pl.BlockSpec((1, 1, n), lambda l: (l, 0, 0))],
        out_specs=[pl.BlockSpec((1, nb, n), lambda l: (l, 0, 0)),
                   pl.BlockSpec((nb, d), lambda l: (0, 0))],
        out_shape=[jax.ShapeDtypeStruct((depth, nb, n), F32), jax.ShapeDtypeStruct((nb, d), BF16)],
        compiler_params=_cp("arbitrary"),
    )(c_all, ada_w, ada_b_shard)


def _normmod_f(x, g, sc, sh):
    return _rms(x, g, D_MODEL) * (1.0 + sc) + sh


def _row_tile(seq):
    return min(seq, 256)


def _normmod_fwd(xin, delta, gate, g, sc, sh, *, nseq, name):
    t, d = xin.shape
    seq = t // nseq
    tr = _row_tile(seq)
    nt = seq // tr
    has_delta = delta is not None
    row = pl.BlockSpec((tr, d), lambda s, i: (s * nt + i, 0))
    per_seq = pl.BlockSpec((1, 1, d), lambda s, i: (s, 0, 0))
    vec = pl.BlockSpec((1, d), lambda s, i: (0, 0))

    if has_delta:
        def body(xin_ref, delta_ref, gate_ref, g_ref, sc_ref, sh_ref, x_ref, h_ref):
            x = xin_ref[...] + gate_ref[0] * delta_ref[...]
            x_ref[...] = x
            h_ref[...] = _b(_normmod_f(x, g_ref[...], sc_ref[0], sh_ref[0]))

        return pl.pallas_call(
            body, name=name, grid=(nseq, nt),
            in_specs=[row, row, per_seq, vec, per_seq, per_seq],
            out_specs=[row, row],
            out_shape=[jax.ShapeDtypeStruct((t, d), F32), jax.ShapeDtypeStruct((t, d), BF16)],
            compiler_params=_cp("parallel", "parallel"),
        )(xin, delta, gate, g, sc, sh)

    def body0(xin_ref, g_ref, sc_ref, sh_ref, h_ref):
        h_ref[...] = _b(_normmod_f(xin_ref[...], g_ref[...], sc_ref[0], sh_ref[0]))

    h = pl.pallas_call(
        body0, name=name, grid=(nseq, nt),
        in_specs=[row, vec, per_seq, per_seq],
        out_specs=row,
        out_shape=jax.ShapeDtypeStruct((t, d), BF16),
        compiler_params=_cp("parallel", "parallel"),
    )(xin, g, sc, sh)
    return xin, h


def _normmod_bwd(dh, dxo, x, delta, gate, g, sc, *, nseq, name):
    t, d = x.shape
    seq = t // nseq
    tr = _row_tile(seq)
    nt = seq // tr
    has_delta = delta is not None
    row = pl.BlockSpec((tr, d), lambda s, i: (s * nt + i, 0))
    per_seq = pl.BlockSpec((1, 1, d), lambda s, i: (s, 0, 0))
    vec = pl.BlockSpec((1, d), lambda s, i: (0, 0))

    def core(dh_ref, dxo_ref, x_ref, g_ref, sc_ref, dx_ref, dg_ref, dsc_ref, dsh_ref):
        s, i = pl.program_id(0), pl.program_id(1)
        dh_v = dh_ref[...]
        _, vjp = jax.vjp(lambda xx, gg, ss: _normmod_f(xx, gg, ss, 0.0), x_ref[...], g_ref[...], sc_ref[0])
        dxn, dg_t, dsc_t = vjp(dh_v)
        dx = dxo_ref[...] + dxn
        dx_ref[...] = dx
        dsh_t = jnp.sum(dh_v, axis=0, keepdims=True)

        @pl.when((s == 0) & (i == 0))
        def _():
            dg_ref[...] = jnp.zeros_like(dg_ref)

        @pl.when(i == 0)
        def _():
            dsc_ref[...] = jnp.zeros_like(dsc_ref)
            dsh_ref[...] = jnp.zeros_like(dsh_ref)

        dg_ref[...] += dg_t
        dsc_ref[0] += dsc_t
        dsh_ref[0] += dsh_t
        return dx

    if has_delta:
        def body(dh_ref, dxo_ref, x_ref, delta_ref, gate_ref, g_ref, sc_ref,
                 dx_ref, dd_ref, dgate_ref, dg_ref, dsc_ref, dsh_ref):
            dx = core(dh_ref, dxo_ref, x_ref, g_ref, sc_ref, dx_ref, dg_ref, dsc_ref, dsh_ref)
            dd_ref[...] = _b(dx * gate_ref[0])

            @pl.when(pl.program_id(1) == 0)
            def _():
                dgate_ref[...] = jnp.zeros_like(dgate_ref)

            dgate_ref[0] += jnp.sum(dx * delta_ref[...], axis=0, keepdims=True)

        return pl.pallas_call(
            body, name=name, grid=(nseq, nt),
            in_specs=[row, row, row, row, per_seq, vec, per_seq],
            out_specs=[row, row, per_seq, vec, per_seq, per_seq],
            out_shape=[jax.ShapeDtypeStruct((t, d), F32), jax.ShapeDtypeStruct((t, d), BF16),
                       jax.ShapeDtypeStruct((nseq, 1, d), F32), jax.ShapeDtypeStruct((1, d), F32),
                       jax.ShapeDtypeStruct((nseq, 1, d), F32), jax.ShapeDtypeStruct((nseq, 1, d), F32)],
            compiler_params=_cp("arbitrary", "arbitrary"),
        )(dh, dxo, x, delta, gate, g, sc)

    def body0(dh_ref, dxo_ref, x_ref, g_ref, sc_ref, dx_ref, dg_ref, dsc_ref, dsh_ref):
        core(dh_ref, dxo_ref, x_ref, g_ref, sc_ref, dx_ref, dg_ref, dsc_ref, dsh_ref)

    dx, dg, dsc, dsh = pl.pallas_call(
        body0, name=name, grid=(nseq, nt),
        in_specs=[row, row, row, vec, per_seq],
        out_specs=[row, vec, per_seq, per_seq],
        out_shape=[jax.ShapeDtypeStruct((t, d), F32), jax.ShapeDtypeStruct((1, d), F32),
                   jax.ShapeDtypeStruct((nseq, 1, d), F32), jax.ShapeDtypeStruct((nseq, 1, d), F32)],
        compiler_params=_cp("arbitrary", "arbitrary"),
    )(dh, dxo, x, g, sc)
    return dx, None, None, dg, dsc, dsh


def _final_loss(xin, delta, gate, fg, target, *, nseq):
    t, d = xin.shape
    seq = t // nseq
    tr = _row_tile(seq)
    nt = seq // tr
    row = pl.BlockSpec((tr, d), lambda s, i: (s * nt + i, 0))
    per_seq = pl.BlockSpec((1, 1, d), lambda s, i: (s, 0, 0))
    vec = pl.BlockSpec((1, d), lambda s, i: (0, 0))

    def body(xin_ref, delta_ref, gate_ref, fg_ref, tgt_ref, loss_ref, dx_ref, dd_ref, dgate_ref, dfg_ref):
        s, i = pl.program_id(0), pl.program_id(1)
        dl = delta_ref[...]
        x = xin_ref[...] + gate_ref[0] * dl
        y, vjp = jax.vjp(lambda xx, gg: _rms(xx, gg, D_MODEL), x, fg_ref[...])
        err = y - tgt_ref[...]
        dx, dfg_t = vjp(err * (1.0 / d))
        dx_ref[...] = dx
        dd_ref[...] = _b(dx * gate_ref[0])

        @pl.when((s == 0) & (i == 0))
        def _():
            loss_ref[...] = jnp.zeros_like(loss_ref)
            dfg_ref[...] = jnp.zeros_like(dfg_ref)

        @pl.when(i == 0)
        def _():
            dgate_ref[...] = jnp.zeros_like(dgate_ref)

        loss_ref[...] += jnp.sum(err * err) * (0.5 / d)
        dfg_ref[...] += dfg_t
        dgate_ref[0] += jnp.sum(dx * dl, axis=0, keepdims=True)

    return pl.pallas_call(
        body, name="final_loss", grid=(nseq, nt),
        in_specs=[row, row, per_seq, vec, row],
        out_specs=[pl.BlockSpec((1, 128), lambda s, i: (0, 0)), row, row, per_seq, vec],
        out_shape=[jax.ShapeDtypeStruct((1, 128), F32), jax.ShapeDtypeStruct((t, d), F32),
                   jax.ShapeDtypeStruct((t, d), BF16), jax.ShapeDtypeStruct((nseq, 1, d), F32),
                   jax.ShapeDtypeStruct((1, d), F32)],
        compiler_params=_cp("arbitrary", "arbitrary"),
    )(xin, delta, gate, fg, target)


def _shift_down(x, j):
    if j == 0:
        return x
    rows = lax.broadcasted_iota(jnp.int32, x.shape, 0)
    return jnp.where(rows >= j, pltpu.roll(x, j, 0), 0.0)


def _shift_up(x, j):
    if j == 0:
        return x
    n = x.shape[0]
    rows = lax.broadcasted_iota(jnp.int32, x.shape, 0)
    return jnp.where(rows < n - j, pltpu.roll(x, n - j, 0), 0.0)


def _conv(x, w_ref, b_ref):
    kw = w_ref.shape[0]
    y = b_ref[...] + w_ref[kw - 1:kw, :] * x
    for j in range(1, kw):
        y = y + w_ref[kw - 1 - j:kw - j, :] * _shift_down(x, j)
    return y


def _conv_bwd(dy, x, w_ref, dw_ref, db_ref):
    kw = w_ref.shape[0]
    dx = w_ref[kw - 1:kw, :] * dy
    dw_ref[kw - 1:kw, :] += jnp.sum(dy * x, axis=0, keepdims=True)
    for j in range(1, kw):
        dx = dx + w_ref[kw - 1 - j:kw - j, :] * _shift_up(dy, j)
        dw_ref[kw - 1 - j:kw - j, :] += jnp.sum(dy * _shift_down(x, j), axis=0, keepdims=True)
    db_ref[...] += jnp.sum(dy, axis=0, keepdims=True)
    return dx


CONV_TC = 256


def _ssd_conv_fwd(proj, w, b, *, nseq):
    t = proj.shape[0]
    seq = t // nseq
    nb = CONV_DIM // CONV_TC
    off = COL_XBC // CONV_TC

    def body(x_ref, w_ref, b_ref, o_ref):
        o_ref[...] = _silu(_conv(x_ref[...], w_ref, b_ref))

    return pl.pallas_call(
        body, name="ssd_conv_fwd", grid=(nb, nseq),
        in_specs=[pl.BlockSpec((seq, CONV_TC), lambda j, s: (s, off + j)),
                  pl.BlockSpec((SSD_CONV, CONV_TC), lambda j, s: (0, j)),
                  pl.BlockSpec((1, CONV_TC), lambda j, s: (0, j))],
        out_specs=pl.BlockSpec((seq, CONV_TC), lambda j, s: (s, j)),
        out_shape=jax.ShapeDtypeStruct((t, CONV_DIM), F32),
        compiler_params=_cp("parallel", "parallel"),
    )(proj, w, b)


def _ssd_conv_bwd(dact, proj, w, b, dproj, *, nseq):
    t = proj.shape[0]
    seq = t // nseq
    nb = CONV_DIM // CONV_TC
    off = COL_XBC // CONV_TC

    def body(da_ref, x_ref, w_ref, b_ref, dproj_ref, dx_ref, dw_ref, db_ref):
        del dproj_ref

        @pl.when(pl.program_id(1) == 0)
        def _():
            dw_ref[...] = jnp.zeros_like(dw_ref)
            db_ref[...] = jnp.zeros_like(db_ref)

        x = x_ref[...]
        pre = _conv(x, w_ref, b_ref)
        sg = jax.nn.sigmoid(pre)
        dpre = da_ref[...] * (sg * (1.0 + pre * (1.0 - sg)))
        dx_ref[...] = _b(_conv_bwd(dpre, x, w_ref, dw_ref, db_ref))

    return pl.pallas_call(
        body, name="ssd_conv_bwd", grid=(nb, nseq),
        in_specs=[pl.BlockSpec((seq, CONV_TC), lambda j, s: (s, j)),
                  pl.BlockSpec((seq, CONV_TC), lambda j, s: (s, off + j)),
                  pl.BlockSpec((SSD_CONV, CONV_TC), lambda j, s: (0, j)),
                  pl.BlockSpec((1, CONV_TC), lambda j, s: (0, j)),
                  ANY],
        out_specs=[pl.BlockSpec((seq, CONV_TC), lambda j, s: (s, off + j)),
                   pl.BlockSpec((SSD_CONV, CONV_TC), lambda j, s: (0, j)),
                   pl.BlockSpec((1, CONV_TC), lambda j, s: (0, j))],
        out_shape=[jax.ShapeDtypeStruct(dproj.shape, dproj.dtype), jax.ShapeDtypeStruct((SSD_CONV, CONV_DIM), F32),
                   jax.ShapeDtypeStruct((1, CONV_DIM), F32)],
        input_output_aliases={4: 0},
        compiler_params=_cp("parallel", "arbitrary"),
    )(dact, proj, w, b, dproj)


def _ffn_act_fwd(up, w, b, *, nseq):
    t = up.shape[0]
    seq = t // nseq
    nb = D_FF // CONV_TC

    def body(g_ref, v_ref, w_ref, b_ref, o_ref):
        o_ref[...] = _b(_silu(_conv(g_ref[...], w_ref, b_ref)) * v_ref[...])

    return pl.pallas_call(
        body, name="ffn_act_fwd", grid=(nb, nseq),
        in_specs=[pl.BlockSpec((seq, CONV_TC), lambda j, s: (s, j)),
                  pl.BlockSpec((seq, CONV_TC), lambda j, s: (s, nb + j)),
                  pl.BlockSpec((FF_CONV, CONV_TC), lambda j, s: (0, j)),
                  pl.BlockSpec((1, CONV_TC), lambda j, s: (0, j))],
        out_specs=pl.BlockSpec((seq, CONV_TC), lambda j, s: (s, j)),
        out_shape=jax.ShapeDtypeStruct((t, D_FF), BF16),
        compiler_params=_cp("parallel", "parallel"),
    )(up, up, w, b)


def _ffn_act_bwd(dact, up, w, b, *, nseq):
    t = up.shape[0]
    seq = t // nseq
    nb = D_FF // CONV_TC

    def body(da_ref, g_ref, v_ref, w_ref, b_ref, dg_ref, dv_ref, dw_ref, db_ref):
        @pl.when(pl.program_id(1) == 0)
        def _():
            dw_ref[...] = jnp.zeros_like(dw_ref)
            db_ref[...] = jnp.zeros_like(db_ref)

        gate = g_ref[...]
        pre = _conv(gate, w_ref, b_ref)
        sg = jax.nn.sigmoid(pre)
        da = da_ref[...]
        dv_ref[...] = _b(da * (pre * sg))
        dpre = da * v_ref[...] * (sg * (1.0 + pre * (1.0 - sg)))
        dg_ref[...] = _b(_conv_bwd(dpre, gate, w_ref, dw_ref, db_ref))

    col = pl.BlockSpec((seq, CONV_TC), lambda j, s: (s, j))
    return pl.pallas_call(
        body, name="ffn_act_bwd", grid=(nb, nseq),
        in_specs=[col, col,
                  pl.BlockSpec((seq, CONV_TC), lambda j, s: (s, nb + j)),
                  pl.BlockSpec((FF_CONV, CONV_TC), lambda j, s: (0, j)),
                  pl.BlockSpec((1, CONV_TC), lambda j, s: (0, j))],
        out_specs=[col, col,
                   pl.BlockSpec((FF_CONV, CONV_TC), lambda j, s: (0, j)),
                   pl.BlockSpec((1, CONV_TC), lambda j, s: (0, j))],
        out_shape=[jax.ShapeDtypeStruct((t, D_FF), BF16), jax.ShapeDtypeStruct((t, D_FF), BF16),
                   jax.ShapeDtypeStruct((FF_CONV, D_FF), F32), jax.ShapeDtypeStruct((1, D_FF), F32)],
        compiler_params=_cp("parallel", "arbitrary"),
    )(dact, up, up, w, b)


def _ssd_chunk(xs, bg, cg, dtr, z, hp, dtb, alog, dskip, ng):
    n = dtr.shape[0]
    dt = _softplus(dtr + dtb)
    cs = _cumsum_rows(dt * (-jnp.exp(alog)))
    cs_t = _transpose(cs)
    lane = lax.broadcasted_iota(jnp.int32, (1, SSD_HEADS), 1)
    sub = lax.broadcasted_iota(jnp.int32, (SSD_HEADS, 1), 0)
    row = lax.broadcasted_iota(jnp.int32, (n, 1), 0)
    r2 = lax.broadcasted_iota(jnp.int32, (n, n), 0)
    c2 = lax.broadcasted_iota(jnp.int32, (n, n), 1)
    causal = r2 >= c2
    cb = [_bdot_nt(cg[g], bg[g]) for g in range(SSD_GROUPS)]
    ys, hn = [], []
    for h in range(SSD_HEADS):
        g = h // HEADS_PER_GROUP
        oh = (lane == h).astype(F32)
        oh_t = (sub == h).astype(F32)
        dt_h = jnp.sum(dt * oh, axis=1, keepdims=True)
        cs_h = jnp.sum(cs * oh, axis=1, keepdims=True)
        d_h = jnp.sum(dskip * oh, axis=1, keepdims=True)
        cs_row = jnp.sum(cs_t * oh_t, axis=0, keepdims=True)
        cs_last = jnp.sum(jnp.where(row == n - 1, cs_h, 0.0), axis=0, keepdims=True)
        decay = jnp.where(causal, jnp.exp(jnp.where(causal, cs_h - cs_row, 0.0)), 0.0)
        xc = xs[h] * dt_h
        y = _bdot(cb[g] * decay, xc)
        y = y + _bdot_nt(cg[g], hp[h]) * jnp.exp(cs_h)
        y = y + d_h * xs[h]
        hn.append(jnp.exp(cs_last) * hp[h] + _bdot_tn(xc * jnp.exp(cs_last - cs_h), bg[g]))
        ys.append(y * _silu(z[h]))
    outs = []
    for g in range(SSD_GROUPS):
        hs = range(g * HEADS_PER_GROUP, (g + 1) * HEADS_PER_GROUP)
        ms = sum(jnp.sum(ys[h] * ys[h], axis=1, keepdims=True) for h in hs) * (1.0 / GROUP_WIDTH)
        r = lax.rsqrt(ms + EPS)
        outs += [ys[h] * r * ng[h] for h in hs]
    return outs, hn


def _hslices(ref, width, count, base=0):
    return [ref[:, base + k * width: base + (k + 1) * width] for k in range(count)]


def _ssd_load(xbc_ref, z_ref, dt_ref, ng_ref):
    xs = _hslices(xbc_ref, SSD_HEAD_DIM, SSD_HEADS)
    bg = _hslices(xbc_ref, D_STATE, SSD_GROUPS, D_SSD)
    cg = _hslices(xbc_ref, D_STATE, SSD_GROUPS, D_SSD + SSD_GROUPS * D_STATE)
    z = _hslices(z_ref, SSD_HEAD_DIM, SSD_HEADS)
    ng = _hslices(ng_ref, SSD_HEAD_DIM, SSD_HEADS)
    return xs, bg, cg, dt_ref[:, 0:SSD_HEADS], z, ng


def _ssd_specs(nch):
    rowi = lambda s, c: s * nch + c
    return [pl.BlockSpec((CHUNK, CONV_DIM), lambda s, c: (rowi(s, c), 0)),
            pl.BlockSpec((CHUNK, D_SSD), lambda s, c: (rowi(s, c), COL_Z // D_SSD)),
            pl.BlockSpec((CHUNK, 128), lambda s, c: (rowi(s, c), COL_DT // 128)),
            pl.BlockSpec((1, SSD_HEADS), lambda s, c: (0, 0)),
            pl.BlockSpec((1, SSD_HEADS), lambda s, c: (0, 0)),
            pl.BlockSpec((1, SSD_HEADS), lambda s, c: (0, 0)),
            pl.BlockSpec((1, D_SSD), lambda s, c: (0, 0))]


def _ssd_fwd(xbc, proj, dtb, alog, dskip, ng, *, nseq):
    t = proj.shape[0]
    nch = t // nseq // CHUNK
    hd = SSD_HEAD_DIM

    def body(xbc_ref, z_ref, dt_ref, dtb_ref, alog_ref, dsk_ref, ng_ref, y_ref, hp_ref, h_ref):
        @pl.when(pl.program_id(1) == 0)
        def _():
            h_ref[...] = jnp.zeros_like(h_ref)

        xs, bg, cg, dtr, z, ngs = _ssd_load(xbc_ref, z_ref, dt_ref, ng_ref)
        hp_ref[0] = h_ref[...]
        hp = [h_ref[h * hd:(h + 1) * hd, :] for h in range(SSD_HEADS)]
        outs, hn = _ssd_chunk(xs, bg, cg, dtr, z, hp, dtb_ref[...], alog_ref[...], dsk_ref[...], ngs)
        for h in range(SSD_HEADS):
            y_ref[:, h * hd:(h + 1) * hd] = _b(outs[h])
            h_ref[h * hd:(h + 1) * hd, :] = hn[h]

    return pl.pallas_call(
        body, name="ssd_fwd", grid=(nseq, nch),
        in_specs=_ssd_specs(nch),
        out_specs=[pl.BlockSpec((CHUNK, D_SSD), lambda s, c: (s * nch + c, 0)),
                   pl.BlockSpec((1, SSD_HEADS * hd, D_STATE), lambda s, c: (s * nch + c, 0, 0))],
        out_shape=[jax.ShapeDtypeStruct((t, D_SSD + D_GM), BF16),
                   jax.ShapeDtypeStruct((t // CHUNK, SSD_HEADS * hd, D_STATE), F32)],
        scratch_shapes=[pltpu.VMEM((SSD_HEADS * hd, D_STATE), F32)],
        compiler_params=_cp("arbitrary", "arbitrary"),
    )(xbc, proj, proj, dtb, alog, dskip, ng)


def _ssd_bwd(dy, xbc, proj, hprev, dtb, alog, dskip, ng, *, nseq):
    t = proj.shape[0]
    nch = t // nseq // CHUNK
    hd = SSD_HEAD_DIM
    rev = lambda s, c: s * nch + (nch - 1 - c)

    def body(dy_ref, xbc_ref, z_ref, dt_ref, hp_ref, dtb_ref, alog_ref, dsk_ref, ng_ref,
             dxbc_ref, dproj_ref, ddtb_ref, dalog_ref, ddsk_ref, dng_ref, dh_ref):
        first = (pl.program_id(0) == 0) & (pl.program_id(1) == 0)

        @pl.when(pl.program_id(1) == 0)
        def _():
            dh_ref[...] = jnp.zeros_like(dh_ref)

        @pl.when(first)
        def _():
            ddtb_ref[...] = jnp.zeros_like(ddtb_ref)
            dalog_ref[...] = jnp.zeros_like(dalog_ref)
            ddsk_ref[...] = jnp.zeros_like(ddsk_ref)
            dng_ref[...] = jnp.zeros_like(dng_ref)

        xs, bg, cg, dtr, z, ngs = _ssd_load(xbc_ref, z_ref, dt_ref, ng_ref)
        hp = [hp_ref[0, h * hd:(h + 1) * hd, :] for h in range(SSD_HEADS)]
        _, vjp = jax.vjp(_ssd_chunk, xs, bg, cg, dtr, z, hp, dtb_ref[...], alog_ref[...], dsk_ref[...], ngs)
        douts = [dy_ref[:, h * hd:(h + 1) * hd] for h in range(SSD_HEADS)]
        dhn = [dh_ref[h * hd:(h + 1) * hd, :] for h in range(SSD_HEADS)]
        dxs, dbg, dcg, ddtr, dz, dhp, ddtb, dalog, ddsk, dngs = vjp((douts, dhn))
        dproj_ref[:, :COL_Z] = jnp.zeros((CHUNK, COL_Z), BF16)
        dproj_ref[:, COL_XBC:] = jnp.zeros((CHUNK, N_INP - COL_XBC), BF16)
        for h in range(SSD_HEADS):
            dxbc_ref[:, h * hd:(h + 1) * hd] = dxs[h]
            dproj_ref[:, COL_Z + h * hd: COL_Z + (h + 1) * hd] = _b(dz[h])
            dh_ref[h * hd:(h + 1) * hd, :] = dhp[h]
            dng_ref[:, h * hd:(h + 1) * hd] += dngs[h]
        for g in range(SSD_GROUPS):
            dxbc_ref[:, D_SSD + g * D_STATE: D_SSD + (g + 1) * D_STATE] = dbg[g]
            dxbc_ref[:, D_SSD + (SSD_GROUPS + g) * D_STATE: D_SSD + (SSD_GROUPS + g + 1) * D_STATE] = dcg[g]
        dproj_ref[:, COL_DT:COL_DT + SSD_HEADS] = _b(ddtr)
        ddtb_ref[...] += ddtb
        dalog_ref[...] += dalog
        ddsk_ref[...] += ddsk

    small = pl.BlockSpec((1, SSD_HEADS), lambda s, c: (0, 0))
    return pl.pallas_call(
        body, name="ssd_bwd", grid=(nseq, nch),
        in_specs=[pl.BlockSpec((CHUNK, D_SSD), lambda s, c: (rev(s, c), 0)),
                  pl.BlockSpec((CHUNK, CONV_DIM), lambda s, c: (rev(s, c), 0)),
                  pl.BlockSpec((CHUNK, D_SSD), lambda s, c: (rev(s, c), COL_Z // D_SSD)),
                  pl.BlockSpec((CHUNK, 128), lambda s, c: (rev(s, c), COL_DT // 128)),
                  pl.BlockSpec((1, SSD_HEADS * hd, D_STATE), lambda s, c: (rev(s, c), 0, 0)),
                  small, small, small,
                  pl.BlockSpec((1, D_SSD), lambda s, c: (0, 0))],
        out_specs=[pl.BlockSpec((CHUNK, CONV_DIM), lambda s, c: (rev(s, c), 0)),
                   pl.BlockSpec((CHUNK, N_INP), lambda s, c: (rev(s, c), 0)),
                   small, small, small,
                   pl.BlockSpec((1, D_SSD), lambda s, c: (0, 0))],
        out_shape=[jax.ShapeDtypeStruct((t, CONV_DIM), F32), jax.ShapeDtypeStruct((t, N_INP), BF16),
                   jax.ShapeDtypeStruct((1, SSD_HEADS), F32), jax.ShapeDtypeStruct((1, SSD_HEADS), F32),
                   jax.ShapeDtypeStruct((1, SSD_HEADS), F32), jax.ShapeDtypeStruct((1, D_SSD), F32)],
        scratch_shapes=[pltpu.VMEM((SSD_HEADS * hd, D_STATE), F32)],
        compiler_params=_cp("arbitrary", "arbitrary"),
    )(dy, xbc, proj, proj, hprev, dtb, alog, dskip, ng)


def _gmlp_chunk(gu, gv, ws, bs_cols, vg, og):
    n = gu[0].shape[0]
    mask = _tri(n, True)
    au = [_gelu(t) for t in gu]
    av = [_gelu(t) for t in gv]
    r = lax.rsqrt(sum(jnp.sum(t * t, axis=1, keepdims=True) for t in av) * (1.0 / D_GM) + EPS)
    p = []
    for h in range(GM_HEADS):
        sv = _bdot(ws[h] * mask, av[h] * r * vg[h]) + bs_cols[h]
        p.append(au[h] * sv)
    r2 = lax.rsqrt(sum(jnp.sum(t * t, axis=1, keepdims=True) for t in p) * (1.0 / D_GM) + EPS)
    return [p[h] * r2 * og[h] for h in range(GM_HEADS)]


def _gmlp_load(u_ref, v_ref, ws_ref, bst_ref, vg_ref, og_ref):
    gu = _hslices(u_ref, GM_HEAD_DIM, GM_HEADS)
    gv = _hslices(v_ref, GM_HEAD_DIM, GM_HEADS)
    ws = [ws_ref[h] for h in range(GM_HEADS)]
    bs_cols = [bst_ref[:, h:h + 1] for h in range(GM_HEADS)]
    return gu, gv, ws, bs_cols, _hslices(vg_ref, GM_HEAD_DIM, GM_HEADS), _hslices(og_ref, GM_HEAD_DIM, GM_HEADS)


def _gmlp_specs():
    return [pl.BlockSpec((CHUNK, D_GM), lambda i: (i, COL_U // D_GM)),
            pl.BlockSpec((CHUNK, D_GM), lambda i: (i, COL_V // D_GM)),
            pl.BlockSpec((GM_HEADS, CHUNK, CHUNK), lambda i: (0, 0, 0)),
            pl.BlockSpec((CHUNK, GM_HEADS), lambda i: (0, 0)),
            pl.BlockSpec((1, D_GM), lambda i: (0, 0)),
            pl.BlockSpec((1, D_GM), lambda i: (0, 0))]


def _gmlp_fwd(proj, ycat, ws, bst, vg, og):
    t = proj.shape[0]

    def body(u_ref, v_ref, ws_ref, bst_ref, vg_ref, og_ref, ycat_ref, o_ref):
        del ycat_ref
        outs = _gmlp_chunk(*_gmlp_load(u_ref, v_ref, ws_ref, bst_ref, vg_ref, og_ref))
        for h in range(GM_HEADS):
            o_ref[:, h * GM_HEAD_DIM:(h + 1) * GM_HEAD_DIM] = _b(outs[h])

    return pl.pallas_call(
        body, name="gmlp_fwd", grid=(t // CHUNK,),
        in_specs=_gmlp_specs() + [ANY],
        out_specs=pl.BlockSpec((CHUNK, D_GM), lambda i: (i, D_SSD // D_GM)),
        out_shape=jax.ShapeDtypeStruct(ycat.shape, ycat.dtype),
        input_output_aliases={6: 0},
        compiler_params=_cp("parallel"),
    )(proj, proj, ws, bst, vg, og, ycat)


def _gmlp_bwd(dy, proj, ws, bst, vg, og, dproj):
    t = proj.shape[0]
    w = GM_HEAD_DIM

    def body(dy_ref, u_ref, v_ref, ws_ref, bst_ref, vg_ref, og_ref, dproj_ref,
             dgm_ref, dws_ref, dbst_ref, dvg_ref, dog_ref):
        del dproj_ref

        @pl.when(pl.program_id(0) == 0)
        def _():
            dws_ref[...] = jnp.zeros_like(dws_ref)
            dbst_ref[...] = jnp.zeros_like(dbst_ref)
            dvg_ref[...] = jnp.zeros_like(dvg_ref)
            dog_ref[...] = jnp.zeros_like(dog_ref)

        _, vjp = jax.vjp(_gmlp_chunk, *_gmlp_load(u_ref, v_ref, ws_ref, bst_ref, vg_ref, og_ref))
        dgu, dgv, dws, dbs, dvg, dog = vjp(_hslices(dy_ref, w, GM_HEADS))
        for h in range(GM_HEADS):
            dgm_ref[:, h * w:(h + 1) * w] = _b(dgu[h])
            dgm_ref[:, D_GM + h * w: D_GM + (h + 1) * w] = _b(dgv[h])
            dws_ref[h] += dws[h]
            dbst_ref[:, h:h + 1] += dbs[h]
            dvg_ref[:, h * w:(h + 1) * w] += dvg[h]
            dog_ref[:, h * w:(h + 1) * w] += dog[h]

    return pl.pallas_call(
        body, name="gmlp_bwd", grid=(t // CHUNK,),
        in_specs=[pl.BlockSpec((CHUNK, D_GM), lambda i: (i, 1))] + _gmlp_specs() + [ANY],
        out_specs=[pl.BlockSpec((CHUNK, 2 * D_GM), lambda i: (i, COL_U // (2 * D_GM))),
                   pl.BlockSpec((GM_HEADS, CHUNK, CHUNK), lambda i: (0, 0, 0)),
                   pl.BlockSpec((CHUNK, GM_HEADS), lambda i: (0, 0)),
                   pl.BlockSpec((1, D_GM), lambda i: (0, 0)),
                   pl.BlockSpec((1, D_GM), lambda i: (0, 0))],
        out_shape=[jax.ShapeDtypeStruct(dproj.shape, dproj.dtype), jax.ShapeDtypeStruct((GM_HEADS, CHUNK, CHUNK), F32),
                   jax.ShapeDtypeStruct((CHUNK, GM_HEADS), F32), jax.ShapeDtypeStruct((1, D_GM), F32),
                   jax.ShapeDtypeStruct((1, D_GM), F32)],
        input_output_aliases={7: 0},
        compiler_params=_cp("arbitrary"),
    )(dy, proj, proj, ws, bst, vg, og, dproj)


def _local_step(x, target, mods, lw, final_g, *, nseq, big_w, grad_sink):
    saved = []
    xin, delta, gate = x, None, None
    for l in range(DEPTH):
        w = lw[l]
        sh1, sc1, g1, sh2, sc2, g2 = mods[l]
        w_in = big_w(l, "w_in", xin if delta is None else delta)
        x0, h1 = _normmod_fwd(xin, delta, gate, w["norm1_g"], sc1, sh1, nseq=nseq, name=f"norm1_fwd_{l}")
        proj = _matmul(h1, w_in, tb=True, name=f"mm_in_{l}")
        xbc = _ssd_conv_fwd(proj, w["ssd_conv_w"], w["ssd_conv_b"], nseq=nseq)
        ycat, hprev = _ssd_fwd(xbc, proj, w["ssd_dt_bias"], w["ssd_a_log"], w["ssd_d"], w["ssd_norm_g"], nseq=nseq)
        ycat = _gmlp_fwd(proj, ycat, w["gm_ws"], w["gm_bst"], w["gm_vnorm_g"], w["gm_out_g"])
        w_out = big_w(l, "w_out", ycat)
        mix = _matmul(ycat, w_out, name=f"mm_out_{l}")
        x1, h2 = _normmod_fwd(x0, mix, g1, w["norm2_g"], sc2, sh2, nseq=nseq, name=f"norm2_fwd_{l}")
        ff_up = big_w(l, "ff_up", h2)
        up = _matmul(h2, ff_up, tb=True, name=f"mm_up_{l}")
        act = _ffn_act_fwd(up, w["ff_conv_w"], w["ff_conv_b"], nseq=nseq)
        ff_down = big_w(l, "ff_down", act)
        dn = _matmul(act, ff_down, name=f"mm_down_{l}")
        saved.append(dict(x0=x0, xin_delta=delta, xin_gate=gate, h1=h1, proj=proj, xbc=xbc, hprev=hprev, ycat=ycat,
                          mix=mix, x1=x1, h2=h2, up=up, act=act, dn=dn,
                          w_in=w_in, w_out=w_out, ff_up=ff_up, ff_down=ff_down))
        xin, delta, gate = x1, dn, g2

    loss, dx, ddelta, dgate, dfg = _final_loss(xin, delta, gate, final_g, target, nseq=nseq)

    small, dmods = [None] * DEPTH, [None] * DEPTH
    for l in reversed(range(DEPTH)):
        w, sv = lw[l], saved[l]
        sh1, sc1, g1, sh2, sc2, g2 = mods[l]
        dg2 = dgate
        g_ff_down = _matmul(sv["act"], ddelta, ta=True, name=f"mm_down_dw_{l}")
        dact = _matmul(ddelta, sv["ff_down"], tb=True, name=f"mm_down_dx_{l}")
        dgate_ff, dval_ff, dfcw, dfcb = _ffn_act_bwd(dact, sv["up"], w["ff_conv_w"], w["ff_conv_b"], nseq=nseq)
        g_ff_up = _matmul([dgate_ff, dval_ff], sv["h2"], ta=True, name=f"mm_up_dw_{l}")
        dep = grad_sink(l, "ffn", dict(ff_down=g_ff_down, ff_up=g_ff_up), dval_ff)
        dh2 = _matmul([dgate_ff, dval_ff], sv["ff_up"], name=f"mm_up_dx_{l}", dep=dep)
        dx, dmix, dg1, dn2g, dsc2, dsh2 = _normmod_bwd(dh2, dx, sv["x1"], sv["mix"], g1, w["norm2_g"], sc2,
                                                       nseq=nseq, name=f"norm2_bwd_{l}")
        g_w_out = _matmul(sv["ycat"], dmix, ta=True, name=f"mm_out_dw_{l}")
        dep = grad_sink(l, "w_out", dict(w_out=g_w_out), dmix)
        dycat = _matmul(dmix, sv["w_out"], tb=True, name=f"mm_out_dx_{l}", dep=dep)
        dxbc_act, dproj, ddtb, dalog, ddsk, dng = _ssd_bwd(dycat, sv["xbc"], sv["proj"], sv["hprev"], w["ssd_dt_bias"],
                                                          w["ssd_a_log"], w["ssd_d"], w["ssd_norm_g"], nseq=nseq)
        dproj, dscw, dscb = _ssd_conv_bwd(dxbc_act, sv["proj"], w["ssd_conv_w"], w["ssd_conv_b"], dproj, nseq=nseq)
        dproj, dws, dbst, dvg, dog = _gmlp_bwd(dycat, sv["proj"], w["gm_ws"], w["gm_bst"], w["gm_vnorm_g"], w["gm_out_g"], dproj)
        g_w_in = _matmul(dproj, sv["h1"], ta=True, name=f"mm_in_dw_{l}")
        dep = grad_sink(l, "w_in", dict(w_in=g_w_in), dproj)
        dh1 = _matmul(dproj, sv["w_in"], name=f"mm_in_dx_{l}", dep=dep)
        dx, ddelta, dgate, dn1g, dsc1, dsh1 = _normmod_bwd(dh1, dx, sv["x0"], sv["xin_delta"], sv["xin_gate"],
                                                           w["norm1_g"], sc1, nseq=nseq, name=f"norm1_bwd_{l}")
        small[l] = dict(norm1_g=dn1g, norm2_g=dn2g, ssd_norm_g=dng, gm_vnorm_g=dvg, gm_out_g=dog,
                        ssd_conv_w=dscw, ssd_conv_b=dscb, ff_conv_w=dfcw, ff_conv_b=dfcb,
                        ssd_dt_bias=ddtb, ssd_a_log=dalog, ssd_d=ddsk, gm_ws=dws, gm_bs=dbst.T)
        dmods[l] = jnp.concatenate([dsh1, dsc1, dg1, dsh2, dsc2, dg2], axis=-1)[:, 0, :]
    return loss, dx, small, dmods, dfg


def _all_gather(arrs, name):
    n = len(arrs)

    def body(*refs):
        ins, outs = refs[:n], refs[n:2 * n]
        send_sems, recv_sems, local_sems = refs[2 * n:]
        x, y, c = lax.axis_index("x"), lax.axis_index("y"), lax.axis_index("c")
        me, sibling = (x, y, c), (x, y, 1 - c)
        chips = [(1 - x, y), (x, 1 - y), (1 - x, 1 - y)]

        def copy(i, k, block, to, src=None):
            px, py, pc = block
            dst = outs[i].at[4 * px + 2 * py + pc]
            return pltpu.make_async_remote_copy(
                src_ref=dst if src is None else src, dst_ref=dst,
                send_sem=send_sems.at[7 * i + k], recv_sem=recv_sems.at[7 * i + k],
                device_id=to, device_id_type=MESH)

        mine = [pltpu.make_async_copy(ins[i], outs[i].at[4 * x + 2 * y + c], local_sems.at[i]) for i in range(n)]
        for cp in mine:
            cp.start()
        first = []
        for i in range(n):
            first.append(copy(i, 0, me, sibling, src=ins[i]))
            first += [copy(i, 1 + j, me, (*chip, c), src=ins[i]) for j, chip in enumerate(chips)]
        for cp in first:
            cp.start()
        passed = []
        for j, chip in enumerate(chips):
            for i in range(n):
                copy(i, 1 + j, (*chip, c), me).wait_recv()
                fwd = copy(i, 4 + j, (*chip, c), sibling)
                fwd.start()
                passed.append(fwd)
        for i in range(n):
            copy(i, 0, sibling, me).wait_recv()
            for j, chip in enumerate(chips):
                copy(i, 4 + j, (*chip, 1 - c), me).wait_recv()
        for cp in first + passed:
            cp.wait_send()
        for cp in mine:
            cp.wait()

    return pl.pallas_call(
        body, name=name,
        in_specs=[ANY] * n, out_specs=[ANY] * n,
        out_shape=[jax.ShapeDtypeStruct((N_DEV,) + a.shape, a.dtype) for a in arrs],
        scratch_shapes=[pltpu.SemaphoreType.DMA((7 * n,)), pltpu.SemaphoreType.DMA((7 * n,)),
                        pltpu.SemaphoreType.DMA((n,))],
    )(*arrs)


def _exchange_sibling(arrs, name):
    n = len(arrs)

    def body(*refs):
        ins, outs = refs[:n], refs[n:2 * n]
        send_sems, recv_sems = refs[2 * n:]
        x, y, c = lax.axis_index("x"), lax.axis_index("y"), lax.axis_index("c")
        copies = []
        for i in range(n):
            for k in range(4):
                copies.append(pltpu.make_async_remote_copy(
                    src_ref=ins[i].at[2 * k + (1 - c)], dst_ref=outs[i].at[k],
                    send_sem=send_sems.at[4 * i + k], recv_sem=recv_sems.at[4 * i + k],
                    device_id=(x, y, 1 - c), device_id_type=MESH))
        for cp in copies:
            cp.start()
        for cp in copies:
            cp.wait_recv()
        for cp in copies:
            cp.wait_send()

    return pl.pallas_call(
        body, name=name,
        in_specs=[ANY] * n, out_specs=[ANY] * n,
        out_shape=[jax.ShapeDtypeStruct((4,) + a.shape[1:], a.dtype) for a in arrs],
        scratch_shapes=[pltpu.SemaphoreType.DMA((4 * n,)), pltpu.SemaphoreType.DMA((4 * n,))],
    )(*arrs)


def _exchange_chips(arrs, name):
    n = len(arrs)

    def body(*refs):
        ins, outs = refs[:n], refs[n:2 * n]
        send_sems, recv_sems = refs[2 * n:]
        x, y, c = lax.axis_index("x"), lax.axis_index("y"), lax.axis_index("c")
        chips = [(1 - x, y), (x, 1 - y), (1 - x, 1 - y)]
        copies = []
        for i in range(n):
            for j, (cx, cy) in enumerate(chips):
                copies.append(pltpu.make_async_remote_copy(
                    src_ref=ins[i].at[2 * cx + cy], dst_ref=outs[i].at[j],
                    send_sem=send_sems.at[3 * i + j], recv_sem=recv_sems.at[3 * i + j],
                    device_id=(cx, cy, c), device_id_type=MESH))
        for cp in copies:
            cp.start()
        for cp in copies:
            cp.wait_recv()
        for cp in copies:
            cp.wait_send()

    return pl.pallas_call(
        body, name=name,
        in_specs=[ANY] * n, out_specs=[ANY] * n,
        out_shape=[jax.ShapeDtypeStruct((3,) + a.shape[1:], a.dtype) for a in arrs],
        scratch_shapes=[pltpu.SemaphoreType.DMA((3 * n,)), pltpu.SemaphoreType.DMA((3 * n,))],
    )(*arrs)


def _add_sibling(a, r, pos, name):
    _, depth, rows, cols = a.shape
    tr = _tile(rows, 256) if rows % 8 == 0 else rows
    a3 = a.reshape(N_DEV * depth, rows, cols)
    r3 = r.reshape(4 * depth, rows, cols)

    def body(pos_ref, a_ref, r_ref, o_ref):
        o_ref[...] = a_ref[...] + r_ref[...]

    out = pl.pallas_call(
        body, name=name,
        grid_spec=pltpu.PrefetchScalarGridSpec(
            num_scalar_prefetch=1, grid=(4 * depth, rows // tr),
            in_specs=[pl.BlockSpec((1, tr, cols), lambda q, i, p: ((2 * (q // depth) + p[0]) * depth + q % depth, i, 0)),
                      pl.BlockSpec((1, tr, cols), lambda q, i, p: (q, i, 0))],
            out_specs=pl.BlockSpec((1, tr, cols), lambda q, i, p: (q, i, 0))),
        out_shape=jax.ShapeDtypeStruct((4 * depth, rows, cols), F32),
        compiler_params=_cp("parallel", "parallel"),
    )(pos, a3, r3)
    return out.reshape(4, depth, rows, cols)


HBM = pl.BlockSpec(memory_space=pltpu.HBM)
SEM = pl.BlockSpec(memory_space=pltpu.SEMAPHORE)
EFFECT = pltpu.SideEffectType.DATAFLOW_SIDE_EFFECTING


def _peer(k):
    x, y, c = lax.axis_index("x"), lax.axis_index("y"), lax.axis_index("c")
    return (1 - x if k & 4 else x, 1 - y if k & 2 else y, 1 - c if k & 1 else c)


def _xc_copies(scatter, srcs, lands, send_sems, recv_sems):
    x, y, c = lax.axis_index("x"), lax.axis_index("y"), lax.axis_index("c")
    copies = []
    for i in range(len(srcs)):
        for k in range(1, N_DEV):
            px, py, pc = _peer(k)
            src = srcs[i].at[4 * px + 2 * py + pc] if scatter else srcs[i]
            dst = lands[i].at[k - 1] if scatter else lands[i].at[4 * x + 2 * y + c]
            copies.append(pltpu.make_async_remote_copy(
                src_ref=src, dst_ref=dst, send_sem=send_sems[i].at[k - 1], recv_sem=recv_sems[i].at[k - 1],
                device_id=(px, py, pc), device_id_type=MESH))
    return copies


def _xc_start(scatter, arrs, after, name):
    n = len(arrs)
    lands = [lax.empty((N_DEV - 1,) + a.shape[1:] if scatter else (N_DEV,) + a.shape, a.dtype) for a in arrs]

    def body(*refs):
        srcs, lnd = refs[:n], refs[n:2 * n]
        send_sems, recv_sems = refs[2 * n + 1:3 * n + 1], refs[3 * n + 1:4 * n + 1]
        token = refs[6 * n + 1]
        for cp in _xc_copies(scatter, srcs, lnd, send_sems, recv_sems):
            cp.start()
        token[...] = jnp.zeros_like(token)

    outs = pl.pallas_call(
        body, name=name,
        out_shape=[pltpu.SemaphoreType.DMA((N_DEV - 1,))] * (2 * n)
        + [pltpu.HBM(a.shape, a.dtype) for a in arrs] + [pltpu.HBM(a.shape, a.dtype) for a in lands]
        + [jax.ShapeDtypeStruct((8, 128), F32)],
        in_specs=[HBM] * (2 * n) + [ANY],
        out_specs=[SEM] * (2 * n) + [HBM] * (2 * n) + [pl.BlockSpec(memory_space=pltpu.VMEM)],
        input_output_aliases={i: 2 * n + i for i in range(2 * n)},
        compiler_params=pltpu.CompilerParams(has_side_effects=EFFECT),
    )(*[pltpu.with_memory_space_constraint(a, pltpu.HBM) for a in list(arrs) + lands], after)
    return outs[:n], outs[n:2 * n], outs[2 * n:3 * n], outs[3 * n:4 * n], outs[4 * n][0, 0]


def _xc_wait(scatter, send_sems, recv_sems, srcs, lands, after, name):
    n = len(srcs)

    def body(*refs):
        s_refs, l_refs = refs[:n], refs[n:2 * n]
        ss, rs = refs[2 * n:3 * n], refs[3 * n:4 * n]
        for cp in _xc_copies(scatter, s_refs, l_refs, ss, rs):
            cp.wait_send()
            cp.wait_recv()

    outs = pl.pallas_call(
        body, name=name,
        out_shape=[pltpu.HBM(a.shape, a.dtype) for a in list(srcs) + list(lands)],
        in_specs=[HBM] * (2 * n) + [SEM] * (2 * n) + [ANY],
        out_specs=[HBM] * (2 * n),
        input_output_aliases={i: i for i in range(2 * n)},
        compiler_params=pltpu.CompilerParams(has_side_effects=EFFECT),
    )(*srcs, *lands, *send_sems, *recv_sems, after)
    return outs[:n], outs[n:]


def _adamw_math(w, g, m, v):
    m = ADAM_B1 * m + (1.0 - ADAM_B1) * g
    v = ADAM_B2 * v + (1.0 - ADAM_B2) * (g * g)
    m_hat = m / (1.0 - ADAM_B1 ** ADAM_STEP)
    v_hat = v / (1.0 - ADAM_B2 ** ADAM_STEP)
    delta = -ADAM_LR * (m_hat / (jnp.sqrt(v_hat) + ADAM_EPS) + ADAM_WD * w)
    return delta, m, v


def _adamw_sharded(parts, w, m, v, pos, name):
    depth, rows, cols = w.shape
    tr = _tile(rows, 256) if rows % 8 == 0 else rows
    npart = len(parts)

    def body(pos_ref, *refs):
        prefs = refs[:npart]
        w_ref, m_ref, v_ref, g_out, d_out, m_out, v_out = refs[npart:]
        g = prefs[0][...]
        for pr in prefs[1:]:
            g = g + pr[...]
        delta, mn, vn = _adamw_math(w_ref[...], g, m_ref[...], v_ref[...])
        g_out[...] = g
        d_out[...] = delta
        m_out[...] = mn
        v_out[...] = vn

    def part_spec(fn):
        return pl.BlockSpec((1, tr, cols), lambda l, i, p: (fn(p) * depth + l, i, 0))

    blk = pl.BlockSpec((1, tr, cols), lambda l, i, p: (l, i, 0))
    shp = jax.ShapeDtypeStruct((depth, rows, cols), F32)
    return pl.pallas_call(
        body, name=name,
        grid_spec=pltpu.PrefetchScalarGridSpec(
            num_scalar_prefetch=1, grid=(depth, rows // tr),
            in_specs=[part_spec(fn) for _, fn in parts] + [blk, blk, blk],
            out_specs=[blk, blk, blk, blk]),
        out_shape=[shp, shp, shp, shp],
        compiler_params=_cp("parallel", "parallel"),
    )(pos, *[a for a, _ in parts], w, m, v)


def _sum_parts(parts, pos, name):
    rows, cols = parts[0][0].shape[1:]
    tc = _tile(cols, 256)

    def body(pos_ref, *refs):
        g = refs[0][0]
        for r in refs[1:-1]:
            g = g + r[0]
        refs[-1][...] = g

    return pl.pallas_call(
        body, name=name,
        grid_spec=pltpu.PrefetchScalarGridSpec(
            num_scalar_prefetch=1, grid=(cols // tc,),
            in_specs=[pl.BlockSpec((1, rows, tc), functools.partial(lambda fn, j, p: (fn(p), 0, j), fn)) for _, fn in parts],
            out_specs=pl.BlockSpec((rows, tc), lambda j, p: (0, j))),
        out_shape=jax.ShapeDtypeStruct((rows, cols), F32),
        compiler_params=_cp("parallel"),
    )(pos, *[a for a, _ in parts])


def _adamw_layer(parts, w, m, v, pos, layer, prev, name):
    depth, rows, cols = w.shape
    tr = _tile(rows, 256) if rows % 8 == 0 else rows
    npart = len(parts)
    nprev = 0 if prev is None else 4

    def body(pos_ref, *refs):
        prefs = refs[:npart]
        w_ref, m_ref, v_ref = refs[npart:npart + 3]
        g_out, d_out, m_out, v_out = refs[npart + 3 + nprev:]
        g = prefs[0][...]
        for pr in prefs[1:]:
            g = g + pr[...]
        delta, mn, vn = _adamw_math(w_ref[...], g, m_ref[...], v_ref[...])
        g_out[...] = g
        d_out[...] = delta
        m_out[...] = mn
        v_out[...] = vn

    def part_spec(fn):
        return pl.BlockSpec((1, tr, cols), lambda i, p: (fn(p), i, 0))

    blk = pl.BlockSpec((1, tr, cols), lambda i, p: (layer, i, 0))
    shp = jax.ShapeDtypeStruct((depth, rows, cols), F32)
    first_prev = 1 + npart + 3
    return pl.pallas_call(
        body, name=name,
        grid_spec=pltpu.PrefetchScalarGridSpec(
            num_scalar_prefetch=1, grid=(rows // tr,),
            in_specs=[part_spec(fn) for _, fn in parts] + [blk, blk, blk] + [ANY] * nprev,
            out_specs=[blk, blk, blk, blk]),
        out_shape=[shp, shp, shp, shp],
        input_output_aliases={first_prev + j: j for j in range(nprev)},
        compiler_params=_cp("parallel"),
    )(pos, *[a for a, _ in parts], w, m, v, *(prev or ()))


_P1024 = ["norm1_g", "norm2_g", "ssd_norm_g", "gm_vnorm_g", "gm_out_g"]
_P16 = ["ssd_dt_bias", "ssd_a_log", "ssd_d"]


def _adamw_small(gath, wmv):
    names = list(wmv.keys())
    classes = list(gath.keys())
    flat_in = [gath[k] for k in classes]
    for nme in names:
        flat_in += list(wmv[nme])
    out_shapes = []
    for nme in names:
        out_shapes += [jax.ShapeDtypeStruct(wmv[nme][0].shape, F32)] * 4
    out_shapes += [jax.ShapeDtypeStruct((DEPTH, SSD_CONV, CONV_DIM), F32), jax.ShapeDtypeStruct((DEPTH, FF_CONV, D_FF), F32),
                   jax.ShapeDtypeStruct((1, SSD_HEADS), F32)]
    scratch = [pltpu.VMEM(gath[k].shape[1:], F32) for k in classes]
    ncls = len(classes)

    def body(*refs):
        g_refs = dict(zip(classes, refs[:ncls]))
        pos = ncls
        w_refs = {}
        for nme in names:
            w_refs[nme] = refs[pos:pos + 3]
            pos += 3
        o_refs = {}
        for nme in names:
            o_refs[nme] = refs[pos:pos + 4]
            pos += 4
        scw_out, fcw_out, loss_out = refs[pos], refs[pos + 1], refs[pos + 2]
        s_refs = dict(zip(classes, refs[pos + 3:]))
        for k in classes:
            acc = g_refs[k][0]
            for dev in range(1, N_DEV):
                acc = acc + g_refs[k][dev]
            s_refs[k][...] = acc

        def apply(nme, grad_of):
            w_ref, m_ref, v_ref = w_refs[nme]
            g_out, d_out, m_out, v_out = o_refs[nme]
            shape = w_ref.shape
            if len(shape) == 2:
                idxs = [(slice(l, l + 1),) for l in range(shape[0])]
            elif len(shape) == 3:
                idxs = [(l,) for l in range(shape[0])]
            else:
                idxs = [(l, h) for l in range(shape[0]) for h in range(shape[1])]
            for n_i, ix in enumerate(idxs):
                g = grad_of(n_i)
                delta, mn, vn = _adamw_math(w_ref[ix], g, m_ref[ix], v_ref[ix])
                g_out[ix] = g
                d_out[ix] = delta
                m_out[ix] = mn
                v_out[ix] = vn

        s1024, s1536, s2816, s16, s128, s6144 = (s_refs[k] for k in classes)
        for n_i, nme in enumerate(_P1024):
            apply(nme, lambda l, b=2 * n_i: s1024[b + l:b + l + 1, :])
        apply("final_g", lambda l: s1024[10:11, :])
        apply("ssd_conv_b", lambda l: s1536[8 + l:9 + l, :])
        apply("ff_conv_b", lambda l: s2816[6 + l:7 + l, :])
        for n_i, nme in enumerate(_P16):
            apply(nme, lambda l, b=2 * n_i: s16[b + l:b + l + 1, :])
        apply("gm_ws", lambda q: s128[q * CHUNK:(q + 1) * CHUNK, :])
        apply("gm_bs", lambda l: s128[2048 + 8 * l:2048 + 8 * (l + 1), :])
        apply("ada_b", lambda l: s6144[2 * l:2 * l + 1, :] + s6144[2 * l + 1:2 * l + 2, :])
        for l in range(DEPTH):
            scw_out[l] = s1536[SSD_CONV * l:SSD_CONV * (l + 1), :]
            fcw_out[l] = s2816[FF_CONV * l:FF_CONV * (l + 1), :]
        loss_out[...] = s16[2 * len(_P16):2 * len(_P16) + 1, :]

    outs = pl.pallas_call(
        body, name="adamw_small",
        out_shape=out_shapes,
        scratch_shapes=scratch,
        compiler_params=pltpu.CompilerParams(vmem_limit_bytes=VMEM_LIMIT),
    )(*flat_in)
    res = {nme: tuple(outs[4 * i:4 * i + 4]) for i, nme in enumerate(names)}
    return res, outs[-3], outs[-2], outs[-1]


_WEIGHTS = ['ada_w', 'ada_b', 'norm1_g', 'norm2_g', 'w_in', 'ssd_conv_w', 'ssd_conv_b', 'ssd_dt_bias', 'ssd_a_log',
            'ssd_d', 'ssd_norm_g', 'gm_vnorm_g', 'gm_ws', 'gm_bs', 'gm_out_g', 'w_out', 'ff_up', 'ff_conv_w',
            'ff_conv_b', 'ff_down', 'final_g']


_O_XBC, _O_DT, _O_GM = D_SSD, D_SSD + CONV_DIM, D_SSD + CONV_DIM + SSD_HEADS


_TRANSPOSED = ("w_in", "ff_up")


def _full_weight(name, g):
    full = g.reshape(g.shape[0] * g.shape[1], g.shape[2])
    if name != "w_in":
        return full
    zpad = jnp.zeros((N_INP - N_IN, full.shape[1]), full.dtype)
    return jnp.concatenate([full[_O_GM:], full[:_O_XBC], full[_O_XBC:_O_DT], full[_O_DT:_O_GM], zpad], axis=0)


def _by_owner(name, grad):
    if name == "w_in":
        grad = jnp.concatenate([grad[COL_Z:COL_XBC], grad[COL_XBC:COL_DT], grad[COL_DT:COL_DT + SSD_HEADS], grad[:COL_Z]], axis=0)
    return grad.reshape(N_DEV, grad.shape[0] // N_DEV, grad.shape[1])


def kernel(x, c, ada_w, ada_b, norm1_g, norm2_g, w_in, ssd_conv_w, ssd_conv_b, ssd_dt_bias, ssd_a_log, ssd_d, ssd_norm_g, gm_vnorm_g, gm_ws, gm_bs, gm_out_g, w_out, ff_up, ff_conv_w, ff_conv_b, ff_down, final_g, loss_target, m_ada_w, m_ada_b, m_norm1_g, m_norm2_g, m_w_in, m_ssd_conv_w, m_ssd_conv_b, m_ssd_dt_bias, m_ssd_a_log, m_ssd_d, m_ssd_norm_g, m_gm_vnorm_g, m_gm_ws, m_gm_bs, m_gm_out_g, m_w_out, m_ff_up, m_ff_conv_w, m_ff_conv_b, m_ff_down, m_final_g, v_ada_w, v_ada_b, v_norm1_g, v_norm2_g, v_w_in, v_ssd_conv_w, v_ssd_conv_b, v_ssd_dt_bias, v_ssd_a_log, v_ssd_d, v_ssd_norm_g, v_gm_vnorm_g, v_gm_ws, v_gm_bs, v_gm_out_g, v_w_out, v_ff_up, v_ff_conv_w, v_ff_conv_b, v_ff_down, v_final_g):
    given = dict(locals())
    wts = {n: given[n] for n in _WEIGHTS}
    mom = {n: given["m_" + n] for n in _WEIGHTS}
    var = {n: given["v_" + n] for n in _WEIGHTS}
    nseq, seq, d = x.shape
    ix, iy, ic = lax.axis_index("x"), lax.axis_index("y"), lax.axis_index("c")
    me = 4 * ix + 2 * iy + ic
    me_arr = me.astype(jnp.int32).reshape(1)

    def shard(l, name):
        s = _b(wts[name][l])
        return s.T if name in _TRANSPOSED else s

    g_win0, g_scw, g_fcw, c_all = _all_gather([shard(0, "w_in"), ssd_conv_w, ff_conv_w, c], "gather_first")
    scw_f = jnp.transpose(g_scw, (1, 2, 0, 3)).reshape(DEPTH, SSD_CONV, CONV_DIM)
    fcw_f = jnp.transpose(g_fcw, (1, 2, 0, 3)).reshape(DEPTH, FF_CONV, D_FF)
    c_all = c_all.reshape(N_DEV * nseq, d)

    n_ada = ada_w.shape[2]
    ada_b_shard = lax.dynamic_slice_in_dim(ada_b, me * n_ada, n_ada, axis=1).reshape(DEPTH, 1, n_ada)
    mod_part, c_act = _ada_fwd(c_all, ada_w, ada_b_shard)
    (mod_g,) = _all_gather([mod_part], "gather_mod")
    mod_all = jnp.transpose(mod_g, (1, 2, 0, 3)).reshape(DEPTH, N_DEV * nseq, N_MOD * d)
    mod_mine = lax.dynamic_slice_in_dim(mod_all, me * nseq, nseq, axis=1)
    mods = [[mod_mine[l, :, k * d:(k + 1) * d].reshape(nseq, 1, d) for k in range(N_MOD)] for l in range(DEPTH)]

    later = [(0, "w_out"), (0, "ff_up"), (0, "ff_down"), (1, "w_in"), (1, "w_out"), (1, "ff_up"), (1, "ff_down")]
    ag_ssem, ag_rsem, ag_src, ag_land, ag_zero = _xc_start(False, [shard(l, n) for l, n in later], mod_g, "ag_start")
    ag_groups = {(0, "w_out"): [0], (0, "ff_up"): [1, 2], (1, "w_in"): [3, 4, 5, 6]}
    big_cache = {(0, "w_in"): _full_weight("w_in", g_win0)}

    def big_w(l, name, after):
        if (l, name) not in big_cache:
            idx = ag_groups[(l, name)]
            pick = lambda seq_: [seq_[i] for i in idx]
            srcs, lands = _xc_wait(False, pick(ag_ssem), pick(ag_rsem), pick(ag_src), pick(ag_land), after,
                                   f"ag_wait_{l}_{name}")
            for i, src, land in zip(idx, srcs, lands):
                big_cache[later[i]] = _full_weight(later[i][1], lax.dynamic_update_index_in_dim(land, src, me, 0))
        return big_cache[(l, name)]

    lw = []
    for l in range(DEPTH):
        lw.append(dict(
            norm1_g=norm1_g[l:l + 1] + (ag_zero if l == 0 else 0.0), norm2_g=norm2_g[l:l + 1], ssd_conv_w=scw_f[l],
            ssd_conv_b=ssd_conv_b[l:l + 1], ssd_dt_bias=ssd_dt_bias[l:l + 1], ssd_a_log=ssd_a_log[l:l + 1],
            ssd_d=ssd_d[l:l + 1], ssd_norm_g=ssd_norm_g[l:l + 1], gm_vnorm_g=gm_vnorm_g[l:l + 1], gm_ws=gm_ws[l],
            gm_bst=gm_bs[l].T, gm_out_g=gm_out_g[l:l + 1], ff_conv_w=fcw_f[l], ff_conv_b=ff_conv_b[l:l + 1]))

    outs = {}
    pending = {}

    def rs_finish(l, group, after):
        names, ssem, rsem, srcs, lands = pending.pop((l, group))
        srcs, lands = _xc_wait(True, ssem, rsem, srcs, lands, after, f"rs_wait_{l}_{group}")
        for nme, own, land in zip(names, srcs, lands):
            parts = [(own, lambda p: p[0])] + [(land, lambda p, k=k: k) for k in range(N_DEV - 1)]
            if nme in _TRANSPOSED:
                g_t = _sum_parts(parts, me_arr, f"rs_sum_{nme}_{l}")
                parts = [(g_t.T[None], lambda p: 0)]
            outs[nme] = _adamw_layer(parts, wts[nme], mom[nme], var[nme], me_arr, l, outs.get(nme), f"adamw_{nme}_{l}")
        return outs[names[-1]][0]

    def grad_sink(l, group, grads, after):
        names = list(grads)
        ssem, rsem, srcs, lands, zero = _xc_start(True, [_by_owner(n, grads[n]) for n in names], after, f"rs_start_{l}_{group}")
        pending[(l, group)] = (names, ssem, rsem, srcs, lands)
        if (l, group) == (0, "w_out"):
            for grp in ("ffn", "w_out", "w_in"):
                after = rs_finish(1, grp, after)
        return zero.reshape(1, 1)

    loss_p, grad_x, small, dmods, dfg = _local_step(
        x.reshape(nseq * seq, d), loss_target.reshape(nseq * seq, d), mods, lw, final_g.reshape(1, d), nseq=nseq,
        big_w=big_w, grad_sink=grad_sink)

    def rows(name):
        return [small[l][name] for l in range(DEPTH)]

    p1024 = jnp.concatenate(sum([rows(n) for n in _P1024], []) + [dfg], axis=0)
    p1536 = jnp.concatenate(rows("ssd_conv_w") + rows("ssd_conv_b"), axis=0)
    p2816 = jnp.concatenate(rows("ff_conv_w") + rows("ff_conv_b"), axis=0)
    p16 = jnp.concatenate(sum([rows(n) for n in _P16], []) + [loss_p[:, :SSD_HEADS]], axis=0)
    p128 = jnp.concatenate([small[l]["gm_ws"].reshape(GM_HEADS * CHUNK, CHUNK) for l in range(DEPTH)] + rows("gm_bs"), axis=0)
    p6144 = jnp.concatenate(dmods, axis=0)
    gathered = _all_gather([p1024, p1536, p2816, p16, p128, p6144], "gather_small")
    gath = dict(zip(["p1024", "p1536", "p2816", "p16", "p128", "p6144"], gathered))

    dmod_all = jnp.transpose(gath["p6144"].reshape(N_DEV, DEPTH, nseq, N_MOD * d), (1, 0, 2, 3)).reshape(
        DEPTH, N_DEV * nseq, N_MOD * d)
    small_names = _P1024 + ["final_g", "ssd_conv_b", "ff_conv_b"] + _P16 + ["gm_ws", "gm_bs", "ada_b"]
    wmv = {}
    for nme in small_names:
        if nme == "final_g":
            wmv[nme] = tuple(a.reshape(1, d) for a in (wts[nme], mom[nme], var[nme]))
        else:
            wmv[nme] = (wts[nme], mom[nme], var[nme])
    small_out, scw_full, fcw_full, loss_sum = _adamw_small(gath, wmv)
    loss = loss_sum[0, 0]
    done = scw_full
    for grp in ("ffn", "w_out", "w_in"):
        done = rs_finish(0, grp, done)
    for nme in small_names:
        outs[nme] = small_out[nme]
    outs["final_g"] = tuple(a.reshape(d) for a in outs["final_g"])

    n_scw, n_fcw = ssd_conv_w.shape[2], ff_conv_w.shape[2]
    g_scw_mine = lax.dynamic_slice_in_dim(scw_full, me * n_scw, n_scw, axis=2)
    g_fcw_mine = lax.dynamic_slice_in_dim(fcw_full, me * n_fcw, n_fcw, axis=2)
    outs["ssd_conv_w"] = _adamw_sharded([(g_scw_mine, lambda p: 0)], ssd_conv_w, m_ssd_conv_w, v_ssd_conv_w, me_arr, "adamw_ssd_conv_w")
    outs["ff_conv_w"] = _adamw_sharded([(g_fcw_mine, lambda p: 0)], ff_conv_w, m_ff_conv_w, v_ff_conv_w, me_arr, "adamw_ff_conv_w")

    dmod_cols = _b(lax.dynamic_slice_in_dim(dmod_all, me * n_ada, n_ada, axis=2))
    g_ada = jnp.stack([_matmul(c_act, dmod_cols[l], ta=True, name=f"mm_ada_dw_{l}") for l in range(DEPTH)])
    outs["ada_w"] = _adamw_sharded([(g_ada, lambda p: 0)], ada_w, m_ada_w, v_ada_w, me_arr, "adamw_ada_w")

    result = [loss, grad_x.reshape(nseq, seq, d)]
    for k in range(4):
        result += [outs[n][k] for n in _WEIGHTS]
    return tuple(result)
```

```python
import functools
import math

import jax
import jax.numpy as jnp
from jax import lax
from jax.experimental import pallas as pl
from jax.experimental.pallas import tpu as pltpu

F32 = jnp.float32
BF16 = jnp.bfloat16

N_DEV = 8
D_MODEL = 1024
DEPTH = 2
CHUNK = 128
SSD_HEADS = 16
SSD_HEAD_DIM = 64
SSD_GROUPS = 2
HEADS_PER_GROUP = SSD_HEADS // SSD_GROUPS
GROUP_WIDTH = HEADS_PER_GROUP * SSD_HEAD_DIM
D_STATE = 128
D_SSD = 1024
CONV_DIM = 1536
SSD_CONV = 4
GM_HEADS = 8
GM_HEAD_DIM = 128
D_GM = 1024
D_FF = 2816
FF_CONV = 3
N_IN = 4624
N_MOD = 6
EPS = 1e-6

N_INP = 5120
COL_U, COL_V, COL_Z, COL_XBC, COL_DT = 0, 1024, 2048, 3072, 4608
DT_BLOCK = 512

ADAM_LR = 0.001
ADAM_B1 = 0.9
ADAM_B2 = 0.999
ADAM_EPS = 1e-08
ADAM_WD = 0.01
ADAM_STEP = 10

VMEM_LIMIT = 56 * 1024 * 1024
MESH = pl.DeviceIdType.MESH
ANY = pl.BlockSpec(memory_space=pl.ANY)


def _cp(*sem):
    return pltpu.CompilerParams(dimension_semantics=sem, vmem_limit_bytes=VMEM_LIMIT)


def _tile(n, pref):
    if n <= pref or n % 128:
        return n
    best = 128
    for t in range(128, pref + 1, 128):
        if n % t == 0:
            best = t
    return best


def _silu(x):
    return x * jax.nn.sigmoid(x)


def _gelu(x):
    return 0.5 * x * (1.0 + lax.erf(x * (1.0 / math.sqrt(2.0))))


def _softplus(x):
    return jnp.maximum(x, 0.0) + jnp.log1p(jnp.exp(-jnp.abs(x)))


def _rms(x, g, width):
    return x * lax.rsqrt(jnp.sum(x * x, axis=-1, keepdims=True) / width + EPS) * g


def _b(x):
    return x.astype(BF16)


_NN = (((1,), (0,)), ((), ()))
_NT = (((1,), (1,)), ((), ()))
_TN = (((0,), (0,)), ((), ()))


def _dg(a, b, dn):
    return lax.dot_general(_b(a), _b(b), dn, preferred_element_type=F32)


@jax.custom_vjp
def _bdot(a, b):
    return _dg(a, b, _NN)


def _bdot_fwd(a, b):
    return _dg(a, b, _NN), (a, b)


def _bdot_bwd(res, ct):
    a, b = res
    return _dg(ct, b, _NT), _dg(a, ct, _TN)


_bdot.defvjp(_bdot_fwd, _bdot_bwd)


@jax.custom_vjp
def _bdot_nt(a, b):
    return _dg(a, b, _NT)


def _bdot_nt_fwd(a, b):
    return _dg(a, b, _NT), (a, b)


def _bdot_nt_bwd(res, ct):
    a, b = res
    return _dg(ct, b, _NN), _dg(ct, a, _TN)


_bdot_nt.defvjp(_bdot_nt_fwd, _bdot_nt_bwd)


@jax.custom_vjp
def _bdot_tn(a, b):
    return _dg(a, b, _TN)


def _bdot_tn_fwd(a, b):
    return _dg(a, b, _TN), (a, b)


def _bdot_tn_bwd(res, ct):
    a, b = res
    return _dg(b, ct, _NT), _dg(a, ct, _NN)


_bdot_tn.defvjp(_bdot_tn_fwd, _bdot_tn_bwd)


def _tri(n, lower):
    r = lax.broadcasted_iota(jnp.int32, (n, n), 0)
    c = lax.broadcasted_iota(jnp.int32, (n, n), 1)
    return ((r >= c) if lower else (r <= c)).astype(F32)


def _eye(n):
    r = lax.broadcasted_iota(jnp.int32, (n, n), 0)
    c = lax.broadcasted_iota(jnp.int32, (n, n), 1)
    return (r == c).astype(F32)


def _hdot(a, b, dn):
    return lax.dot_general(a, b, dn, precision=lax.Precision.HIGHEST, preferred_element_type=F32)


@jax.custom_vjp
def _cumsum_rows(x):
    return _hdot(_tri(x.shape[0], True), x, _NN)


def _cumsum_rows_fwd(x):
    return _cumsum_rows(x), None


def _cumsum_rows_bwd(_, ct):
    return (_hdot(_tri(ct.shape[0], False), ct, _NN),)


_cumsum_rows.defvjp(_cumsum_rows_fwd, _cumsum_rows_bwd)


@jax.custom_vjp
def _transpose(x):
    return _hdot(_eye(x.shape[1]), x, _NT)


def _transpose_fwd(x):
    return _transpose(x), None


def _transpose_bwd(_, ct):
    return (_hdot(_eye(ct.shape[1]), ct, _NT),)


_transpose.defvjp(_transpose_fwd, _transpose_bwd)


def _matmul(a, b, *, ta=False, tb=False, name, dep=None):
    pieces = list(a) if isinstance(a, (list, tuple)) else [a]
    npc = len(pieces)
    rows, width = pieces[0].shape
    assert all(p.shape == (rows, width) for p in pieces)
    if ta:
        k_dim, m_dim = rows, width * npc
    else:
        m_dim, k_dim = rows, width * npc
    if tb:
        n_dim, kb = b.shape
    else:
        kb, n_dim = b.shape
    assert kb == k_dim, (pieces[0].shape, npc, b.shape, ta, tb)
    tm, tn, tk = _tile(m_dim, 1024), _tile(n_dim, 1536), _tile(k_dim, 1536)
    if npc > 1:
        if ta:
            tm = _tile(width, 1536)
        else:
            tk = _tile(width, 1536)
    ni, nj, nk = m_dim // tm, n_dim // tn, k_dim // tk
    per = width // (tm if ta else tk)
    dn = (((0 if ta else 1,), (1 if tb else 0,)), ((), ()))

    a_bytes, b_bytes = m_dim * k_dim, k_dim * n_dim
    m_outer = nk > 1 or a_bytes + b_bytes * ni <= b_bytes + a_bytes * nj
    if m_outer:
        ij = lambda o, n, k: (o, n)
        grid = (ni, nj, nk)
    else:
        ij = lambda o, n, k: (n, o)
        grid = (nj, ni, nk)

    def body(*refs):
        a_refs, b_ref, o_ref = refs[:npc], refs[npc], refs[-1]
        k = pl.program_id(2)
        i = pl.program_id(0 if m_outer else 1)
        along = i if ta else k

        def step(a_ref):
            p = lax.dot_general(a_ref[...], b_ref[...], dn, preferred_element_type=F32)
            if nk == 1:
                o_ref[...] = p
            else:
                @pl.when(k == 0)
                def _():
                    o_ref[...] = p

                @pl.when(k > 0)
                def _():
                    o_ref[...] += p

        if npc == 1:
            step(a_refs[0])
        else:
            for pc in range(npc):
                pl.when((along >= pc * per) & (along < (pc + 1) * per))(functools.partial(step, a_refs[pc]))

    def a_map(pc, o, n, k):
        i, _ = ij(o, n, k)
        along = i if ta else k
        if npc > 1:
            along = jnp.clip(along - pc * per, 0, per - 1)
        return (k, along) if ta else (i, along)

    def b_map(o, n, k):
        _, j = ij(o, n, k)
        return (j, k) if tb else (k, j)

    extra = [] if dep is None else [dep]
    return pl.pallas_call(
        body, name=name,
        grid=grid,
        in_specs=[pl.BlockSpec((tk, tm) if ta else (tm, tk), functools.partial(a_map, pc)) for pc in range(npc)]
        + [pl.BlockSpec((tn, tk) if tb else (tk, tn), b_map)] + [ANY] * len(extra),
        out_specs=pl.BlockSpec((tm, tn), lambda o, n, k: ij(o, n, k)),
        out_shape=jax.ShapeDtypeStruct((m_dim, n_dim), F32),
        compiler_params=_cp("parallel", "parallel", "arbitrary"),
    )(*pieces, b, *extra)


def _ada_fwd(c_all, ada_w, ada_b_shard):
    depth, d, n = ada_w.shape
    nb = c_all.shape[0]

    def body(c_ref, w_ref, b_ref, o_ref, ca_ref):
        ca = _silu(c_ref[...])
        ca_ref[...] = _b(ca)
        o_ref[0] = _dg(ca, w_ref[0], _NN) + b_ref[0]

    return pl.pallas_call(
        body, name="ada_fwd",
        grid=(depth,),
        in_specs=[pl.BlockSpec((nb, d), lambda l: (0, 0)),
                  pl.BlockSpec((1, d, n), lambda l: (l, 0, 0)),
                  pl.BlockSpec((1, 1, n), lambda l: (l, 0, 0))],
        out_specs=[pl.BlockSpec((1, nb, n), lambda l: (l, 0, 0)),
                   pl.BlockSpec((nb, d), lambda l: (0, 0))],
        out_shape=[jax.ShapeDtypeStruct((depth, nb, n), F32), jax.ShapeDtypeStruct((nb, d), BF16)],
        compiler_params=_cp("arbitrary"),
    )(c_all, ada_w, ada_b_shard)


def _normmod_f(x, g, sc, sh):
    return _rms(x, g, D_MODEL) * (1.0 + sc) + sh


def _row_tile(seq):
    return min(seq, 256)


def _normmod_fwd(xin, delta, gate, g, sc, sh, *, nseq, name):
    t, d = xin.shape
    seq = t // nseq
    tr = _row_tile(seq)
    nt = seq // tr
    has_delta = delta is not None
    row = pl.BlockSpec((tr, d), lambda s, i: (s * nt + i, 0))
    per_seq = pl.BlockSpec((1, 1, d), lambda s, i: (s, 0, 0))
    vec = pl.BlockSpec((1, d), lambda s, i: (0, 0))

    if has_delta:
        def body(xin_ref, delta_ref, gate_ref, g_ref, sc_ref, sh_ref, x_ref, h_ref):
            x = xin_ref[...] + gate_ref[0] * delta_ref[...]
            x_ref[...] = x
            h_ref[...] = _b(_normmod_f(x, g_ref[...], sc_ref[0], sh_ref[0]))

        return pl.pallas_call(
            body, name=name, grid=(nseq, nt),
            in_specs=[row, row, per_seq, vec, per_seq, per_seq],
            out_specs=[row, row],
            out_shape=[jax.ShapeDtypeStruct((t, d), F32), jax.ShapeDtypeStruct((t, d), BF16)],
            compiler_params=_cp("parallel", "parallel"),
        )(xin, delta, gate, g, sc, sh)

    def body0(xin_ref, g_ref, sc_ref, sh_ref, h_ref):
        h_ref[...] = _b(_normmod_f(xin_ref[...], g_ref[...], sc_ref[0], sh_ref[0]))

    h = pl.pallas_call(
        body0, name=name, grid=(nseq, nt),
        in_specs=[row, vec, per_seq, per_seq],
        out_specs=row,
        out_shape=jax.ShapeDtypeStruct((t, d), BF16),
        compiler_params=_cp("parallel", "parallel"),
    )(xin, g, sc, sh)
    return xin, h


def _normmod_bwd(dh, dxo, x, delta, gate, g, sc, *, nseq, name):
    t, d = x.shape
    seq = t // nseq
    tr = _row_tile(seq)
    nt = seq // tr
    has_delta = delta is not None
    row = pl.BlockSpec((tr, d), lambda s, i: (s * nt + i, 0))
    per_seq = pl.BlockSpec((1, 1, d), lambda s, i: (s, 0, 0))
    vec = pl.BlockSpec((1, d), lambda s, i: (0, 0))

    def core(dh_ref, dxo_ref, x_ref, g_ref, sc_ref, dx_ref, dg_ref, dsc_ref, dsh_ref):
        s, i = pl.program_id(0), pl.program_id(1)
        dh_v = dh_ref[...]
        _, vjp = jax.vjp(lambda xx, gg, ss: _normmod_f(xx, gg, ss, 0.0), x_ref[...], g_ref[...], sc_ref[0])
        dxn, dg_t, dsc_t = vjp(dh_v)
        dx = dxo_ref[...] + dxn
        dx_ref[...] = dx
        dsh_t = jnp.sum(dh_v, axis=0, keepdims=True)

        @pl.when((s == 0) & (i == 0))
        def _():
            dg_ref[...] = jnp.zeros_like(dg_ref)

        @pl.when(i == 0)
        def _():
            dsc_ref[...] = jnp.zeros_like(dsc_ref)
            dsh_ref[...] = jnp.zeros_like(dsh_ref)

        dg_ref[...] += dg_t
        dsc_ref[0] += dsc_t
        dsh_ref[0] += dsh_t
        return dx

    if has_delta:
        def body(dh_ref, dxo_ref, x_ref, delta_ref, gate_ref, g_ref, sc_ref,
                 dx_ref, dd_ref, dgate_ref, dg_ref, dsc_ref, dsh_ref):
            dx = core(dh_ref, dxo_ref, x_ref, g_ref, sc_ref, dx_ref, dg_ref, dsc_ref, dsh_ref)
            dd_ref[...] = _b(dx * gate_ref[0])

            @pl.when(pl.program_id(1) == 0)
            def _():
                dgate_ref[...] = jnp.zeros_like(dgate_ref)

            dgate_ref[0] += jnp.sum(dx * delta_ref[...], axis=0, keepdims=True)

        return pl.pallas_call(
            body, name=name, grid=(nseq, nt),
            in_specs=[row, row, row, row, per_seq, vec, per_seq],
            out_specs=[row, row, per_seq, vec, per_seq, per_seq],
            out_shape=[jax.ShapeDtypeStruct((t, d), F32), jax.ShapeDtypeStruct((t, d), BF16),
                       jax.ShapeDtypeStruct((nseq, 1, d), F32), jax.ShapeDtypeStruct((1, d), F32),
                       jax.ShapeDtypeStruct((nseq, 1, d), F32), jax.ShapeDtypeStruct((nseq, 1, d), F32)],
            compiler_params=_cp("arbitrary", "arbitrary"),
        )(dh, dxo, x, delta, gate, g, sc)

    def body0(dh_ref, dxo_ref, x_ref, g_ref, sc_ref, dx_ref, dg_ref, dsc_ref, dsh_ref):
        core(dh_ref, dxo_ref, x_ref, g_ref, sc_ref, dx_ref, dg_ref, dsc_ref, dsh_ref)

    dx, dg, dsc, dsh = pl.pallas_call(
        body0, name=name, grid=(nseq, nt),
        in_specs=[row, row, row, vec, per_seq],
        out_specs=[row, vec, per_seq, per_seq],
        out_shape=[jax.ShapeDtypeStruct((t, d), F32), jax.ShapeDtypeStruct((1, d), F32),
                   jax.ShapeDtypeStruct((nseq, 1, d), F32), jax.ShapeDtypeStruct((nseq, 1, d), F32)],
        compiler_params=_cp("arbitrary", "arbitrary"),
    )(dh, dxo, x, g, sc)
    return dx, None, None, dg, dsc, dsh


def _final_loss(xin, delta, gate, fg, target, *, nseq):
    t, d = xin.shape
    seq = t // nseq
    tr = _row_tile(seq)
    nt = seq // tr
    row = pl.BlockSpec((tr, d), lambda s, i: (s * nt + i, 0))
    per_seq = pl.BlockSpec((1, 1, d), lambda s, i: (s, 0, 0))
    vec = pl.BlockSpec((1, d), lambda s, i: (0, 0))

    def body(xin_ref, delta_ref, gate_ref, fg_ref, tgt_ref, loss_ref, dx_ref, dd_ref, dgate_ref, dfg_ref):
        s, i = pl.program_id(0), pl.program_id(1)
        dl = delta_ref[...]
        x = xin_ref[...] + gate_ref[0] * dl
        y, vjp = jax.vjp(lambda xx, gg: _rms(xx, gg, D_MODEL), x, fg_ref[...])
        err = y - tgt_ref[...]
        dx, dfg_t = vjp(err * (1.0 / d))
        dx_ref[...] = dx
        dd_ref[...] = _b(dx * gate_ref[0])

        @pl.when((s == 0) & (i == 0))
        def _():
            loss_ref[...] = jnp.zeros_like(loss_ref)
            dfg_ref[...] = jnp.zeros_like(dfg_ref)

        @pl.when(i == 0)
        def _():
            dgate_ref[...] = jnp.zeros_like(dgate_ref)

        loss_ref[...] += jnp.sum(err * err) * (0.5 / d)
        dfg_ref[...] += dfg_t
        dgate_ref[0] += jnp.sum(dx * dl, axis=0, keepdims=True)

    return pl.pallas_call(
        body, name="final_loss", grid=(nseq, nt),
        in_specs=[row, row, per_seq, vec, row],
        out_specs=[pl.BlockSpec((1, 128), lambda s, i: (0, 0)), row, row, per_seq, vec],
        out_shape=[jax.ShapeDtypeStruct((1, 128), F32), jax.ShapeDtypeStruct((t, d), F32),
                   jax.ShapeDtypeStruct((t, d), BF16), jax.ShapeDtypeStruct((nseq, 1, d), F32),
                   jax.ShapeDtypeStruct((1, d), F32)],
        compiler_params=_cp("arbitrary", "arbitrary"),
    )(xin, delta, gate, fg, target)


def _shift_down(x, j):
    if j == 0:
        return x
    rows = lax.broadcasted_iota(jnp.int32, x.shape, 0)
    return jnp.where(rows >= j, pltpu.roll(x, j, 0), 0.0)


def _shift_up(x, j):
    if j == 0:
        return x
    n = x.shape[0]
    rows = lax.broadcasted_iota(jnp.int32, x.shape, 0)
    return jnp.where(rows < n - j, pltpu.roll(x, n - j, 0), 0.0)


def _conv(x, w_ref, b_ref):
    kw = w_ref.shape[0]
    y = b_ref[...] + w_ref[kw - 1:kw, :] * x
    for j in range(1, kw):
        y = y + w_ref[kw - 1 - j:kw - j, :] * _shift_down(x, j)
    return y


def _conv_bwd(dy, x, w_ref, dw_ref, db_ref):
    kw = w_ref.shape[0]
    dx = w_ref[kw - 1:kw, :] * dy
    dw_ref[kw - 1:kw, :] += jnp.sum(dy * x, axis=0, keepdims=True)
    for j in range(1, kw):
        dx = dx + w_ref[kw - 1 - j:kw - j, :] * _shift_up(dy, j)
        dw_ref[kw - 1 - j:kw - j, :] += jnp.sum(dy * _shift_down(x, j), axis=0, keepdims=True)
    db_ref[...] += jnp.sum(dy, axis=0, keepdims=True)
    return dx


CONV_TC = 256


def _ssd_conv_fwd(proj, w, b, *, nseq):
    t = proj.shape[0]
    seq = t // nseq
    nb = CONV_DIM // CONV_TC
    off = COL_XBC // CONV_TC

    def body(x_ref, w_ref, b_ref, o_ref):
        o_ref[...] = _silu(_conv(x_ref[...], w_ref, b_ref))

    return pl.pallas_call(
        body, name="ssd_conv_fwd", grid=(nb, nseq),
        in_specs=[pl.BlockSpec((seq, CONV_TC), lambda j, s: (s, off + j)),
                  pl.BlockSpec((SSD_CONV, CONV_TC), lambda j, s: (0, j)),
                  pl.BlockSpec((1, CONV_TC), lambda j, s: (0, j))],
        out_specs=pl.BlockSpec((seq, CONV_TC), lambda j, s: (s, j)),
        out_shape=jax.ShapeDtypeStruct((t, CONV_DIM), F32),
        compiler_params=_cp("parallel", "parallel"),
    )(proj, w, b)


def _ssd_conv_bwd(dact, proj, w, b, dproj, *, nseq):
    t = proj.shape[0]
    seq = t // nseq
    nb = CONV_DIM // CONV_TC
    off = COL_XBC // CONV_TC

    def body(da_ref, x_ref, w_ref, b_ref, dproj_ref, dx_ref, dw_ref, db_ref):
        del dproj_ref

        @pl.when(pl.program_id(1) == 0)
        def _():
            dw_ref[...] = jnp.zeros_like(dw_ref)
            db_ref[...] = jnp.zeros_like(db_ref)

        x = x_ref[...]
        pre = _conv(x, w_ref, b_ref)
        sg = jax.nn.sigmoid(pre)
        dpre = da_ref[...] * (sg * (1.0 + pre * (1.0 - sg)))
        dx_ref[...] = _b(_conv_bwd(dpre, x, w_ref, dw_ref, db_ref))

    return pl.pallas_call(
        body, name="ssd_conv_bwd", grid=(nb, nseq),
        in_specs=[pl.BlockSpec((seq, CONV_TC), lambda j, s: (s, j)),
                  pl.BlockSpec((seq, CONV_TC), lambda j, s: (s, off + j)),
                  pl.BlockSpec((SSD_CONV, CONV_TC), lambda j, s: (0, j)),
                  pl.BlockSpec((1, CONV_TC), lambda j, s: (0, j)),
                  ANY],
        out_specs=[pl.BlockSpec((seq, CONV_TC), lambda j, s: (s, off + j)),
                   pl.BlockSpec((SSD_CONV, CONV_TC), lambda j, s: (0, j)),
                   pl.BlockSpec((1, CONV_TC), lambda j, s: (0, j))],
        out_shape=[jax.ShapeDtypeStruct(dproj.shape, dproj.dtype), jax.ShapeDtypeStruct((SSD_CONV, CONV_DIM), F32),
                   jax.ShapeDtypeStruct((1, CONV_DIM), F32)],
        input_output_aliases={4: 0},
        compiler_params=_cp("parallel", "arbitrary"),
    )(dact, proj, w, b, dproj)


def _ffn_act_fwd(up, w, b, *, nseq):
    t = up.shape[0]
    seq = t // nseq
    nb = D_FF // CONV_TC

    def body(g_ref, v_ref, w_ref, b_ref, o_ref):
        o_ref[...] = _b(_silu(_conv(g_ref[...], w_ref, b_ref)) * v_ref[...])

    return pl.pallas_call(
        body, name="ffn_act_fwd", grid=(nb, nseq),
        in_specs=[pl.BlockSpec((seq, CONV_TC), lambda j, s: (s, j)),
                  pl.BlockSpec((seq, CONV_TC), lambda j, s: (s, nb + j)),
                  pl.BlockSpec((FF_CONV, CONV_TC), lambda j, s: (0, j)),
                  pl.BlockSpec((1, CONV_TC), lambda j, s: (0, j))],
        out_specs=pl.BlockSpec((seq, CONV_TC), lambda j, s: (s, j)),
        out_shape=jax.ShapeDtypeStruct((t, D_FF), BF16),
        compiler_params=_cp("parallel", "parallel"),
    )(up, up, w, b)


def _ffn_act_bwd(dact, up, w, b, *, nseq):
    t = up.shape[0]
    seq = t // nseq
    nb = D_FF // CONV_TC

    def body(da_ref, g_ref, v_ref, w_ref, b_ref, dg_ref, dv_ref, dw_ref, db_ref):
        @pl.when(pl.program_id(1) == 0)
        def _():
            dw_ref[...] = jnp.zeros_like(dw_ref)
            db_ref[...] = jnp.zeros_like(db_ref)

        gate = g_ref[...]
        pre = _conv(gate, w_ref, b_ref)
        sg = jax.nn.sigmoid(pre)
        da = da_ref[...]
        dv_ref[...] = _b(da * (pre * sg))
        dpre = da * v_ref[...] * (sg * (1.0 + pre * (1.0 - sg)))
        dg_ref[...] = _b(_conv_bwd(dpre, gate, w_ref, dw_ref, db_ref))

    col = pl.BlockSpec((seq, CONV_TC), lambda j, s: (s, j))
    return pl.pallas_call(
        body, name="ffn_act_bwd", grid=(nb, nseq),
        in_specs=[col, col,
                  pl.BlockSpec((seq, CONV_TC), lambda j, s: (s, nb + j)),
                  pl.BlockSpec((FF_CONV, CONV_TC), lambda j, s: (0, j)),
                  pl.BlockSpec((1, CONV_TC), lambda j, s: (0, j))],
        out_specs=[col, col,
                   pl.BlockSpec((FF_CONV, CONV_TC), lambda j, s: (0, j)),
                   pl.BlockSpec((1, CONV_TC), lambda j, s: (0, j))],
        out_shape=[jax.ShapeDtypeStruct((t, D_FF), BF16), jax.ShapeDtypeStruct((t, D_FF), BF16),
                   jax.ShapeDtypeStruct((FF_CONV, D_FF), F32), jax.ShapeDtypeStruct((1, D_FF), F32)],
        compiler_params=_cp("parallel", "arbitrary"),
    )(dact, up, up, w, b)


SSD_PAIRS = SSD_HEADS // 2
PAIR_W = 2 * SSD_HEAD_DIM
PAIRS_PER_GROUP = SSD_PAIRS // SSD_GROUPS


def _ssd_chunk(xs, bg, cg, dtr, z, hp, dtb, alog, dskip, ng):
    n = dtr.shape[0]
    dt = _softplus(dtr + dtb)
    cs = _cumsum_rows(dt * (-jnp.exp(alog)))
    cs_t = _transpose(cs)
    lane = lax.broadcasted_iota(jnp.int32, (1, SSD_HEADS), 1)
    sub = lax.broadcasted_iota(jnp.int32, (SSD_HEADS, 1), 0)
    row = lax.broadcasted_iota(jnp.int32, (n, 1), 0)
    causal = lax.broadcasted_iota(jnp.int32, (n, n), 0) >= lax.broadcasted_iota(jnp.int32, (n, n), 1)
    first = lax.broadcasted_iota(jnp.int32, (1, PAIR_W), 1) < SSD_HEAD_DIM
    first_rows = lax.broadcasted_iota(jnp.int32, (PAIR_W, 1), 0) < SSD_HEAD_DIM
    first_f = first.astype(F32)
    cb = [_bdot_nt(cg[g], bg[g]) for g in range(SSD_GROUPS)]
    ys, hn = [], []
    for p in range(SSD_PAIRS):
        g = p // PAIRS_PER_GROUP
        col, decay, last = [], [], []
        for h in (2 * p, 2 * p + 1):
            oh = (lane == h).astype(F32)
            cs_h = jnp.sum(cs * oh, axis=1, keepdims=True)
            cs_row = jnp.sum(cs_t * (sub == h).astype(F32), axis=0, keepdims=True)
            col.append((jnp.sum(dt * oh, axis=1, keepdims=True), cs_h, jnp.sum(dskip * oh, axis=1, keepdims=True)))
            last.append(jnp.sum(jnp.where(row == n - 1, cs_h, 0.0), axis=0, keepdims=True))
            decay.append(jnp.where(causal, jnp.exp(jnp.where(causal, cs_h - cs_row, 0.0)), 0.0))
        pair = lambda a, b: jnp.where(first, a, b)
        dt_p = pair(col[0][0], col[1][0])
        cs_p = pair(col[0][1], col[1][1])
        last_p = pair(last[0], last[1])
        xc = xs[p] * dt_p
        y = _bdot(cb[g] * decay[0], xc * first_f) + _bdot(cb[g] * decay[1], xc * (1.0 - first_f))
        y = y + _bdot_nt(cg[g], hp[p]) * jnp.exp(cs_p)
        y = y + pair(col[0][2], col[1][2]) * xs[p]
        keep = jnp.where(first_rows, jnp.exp(last[0]), jnp.exp(last[1]))
        hn.append(keep * hp[p] + _bdot_tn(xc * jnp.exp(last_p - cs_p), bg[g]))
        ys.append(y * _silu(z[p]))
    outs = []
    for g in range(SSD_GROUPS):
        ps = range(g * PAIRS_PER_GROUP, (g + 1) * PAIRS_PER_GROUP)
        ms = sum(jnp.sum(ys[p] * ys[p], axis=1, keepdims=True) for p in ps) * (1.0 / GROUP_WIDTH)
        r = lax.rsqrt(ms + EPS)
        outs += [ys[p] * r * ng[p] for p in ps]
    return outs, hn


def _hslices(ref, width, count, base=0):
    return [ref[:, base + k * width: base + (k + 1) * width] for k in range(count)]


def _ssd_load(xbc_ref, z_ref, dt_ref, ng_ref):
    xs = _hslices(xbc_ref, PAIR_W, SSD_PAIRS)
    bg = _hslices(xbc_ref, D_STATE, SSD_GROUPS, D_SSD)
    cg = _hslices(xbc_ref, D_STATE, SSD_GROUPS, D_SSD + SSD_GROUPS * D_STATE)
    z = _hslices(z_ref, PAIR_W, SSD_PAIRS)
    ng = _hslices(ng_ref, PAIR_W, SSD_PAIRS)
    return xs, bg, cg, dt_ref[:, 0:SSD_HEADS], z, ng


def _ssd_specs(nch):
    rowi = lambda s, c: s * nch + c
    return [pl.BlockSpec((CHUNK, CONV_DIM), lambda s, c: (rowi(s, c), 0)),
            pl.BlockSpec((CHUNK, D_SSD), lambda s, c: (rowi(s, c), COL_Z // D_SSD)),
            pl.BlockSpec((CHUNK, 128), lambda s, c: (rowi(s, c), COL_DT // 128)),
            pl.BlockSpec((1, SSD_HEADS), lambda s, c: (0, 0)),
            pl.BlockSpec((1, SSD_HEADS), lambda s, c: (0, 0)),
            pl.BlockSpec((1, SSD_HEADS), lambda s, c: (0, 0)),
            pl.BlockSpec((1, D_SSD), lambda s, c: (0, 0))]


def _ssd_fwd(xbc, proj, dtb, alog, dskip, ng, *, nseq):
    t = proj.shape[0]
    nch = t // nseq // CHUNK
    hd = PAIR_W

    def body(xbc_ref, z_ref, dt_ref, dtb_ref, alog_ref, dsk_ref, ng_ref, y_ref, hp_ref, h_ref):
        @pl.when(pl.program_id(1) == 0)
        def _():
            h_ref[...] = jnp.zeros_like(h_ref)

        xs, bg, cg, dtr, z, ngs = _ssd_load(xbc_ref, z_ref, dt_ref, ng_ref)
        hp_ref[0] = h_ref[...]
        hp = [h_ref[h * hd:(h + 1) * hd, :] for h in range(SSD_PAIRS)]
        outs, hn = _ssd_chunk(xs, bg, cg, dtr, z, hp, dtb_ref[...], alog_ref[...], dsk_ref[...], ngs)
        for h in range(SSD_PAIRS):
            y_ref[:, h * hd:(h + 1) * hd] = _b(outs[h])
            h_ref[h * hd:(h + 1) * hd, :] = hn[h]

    return pl.pallas_call(
        body, name="ssd_fwd", grid=(nseq, nch),
        in_specs=_ssd_specs(nch),
        out_specs=[pl.BlockSpec((CHUNK, D_SSD), lambda s, c: (s * nch + c, 0)),
                   pl.BlockSpec((1, D_SSD, D_STATE), lambda s, c: (s * nch + c, 0, 0))],
        out_shape=[jax.ShapeDtypeStruct((t, D_SSD + D_GM), BF16),
                   jax.ShapeDtypeStruct((t // CHUNK, D_SSD, D_STATE), F32)],
        scratch_shapes=[pltpu.VMEM((D_SSD, D_STATE), F32)],
        compiler_params=_cp("arbitrary", "arbitrary"),
    )(xbc, proj, proj, dtb, alog, dskip, ng)


def _ssd_bwd(dy, xbc, proj, hprev, dtb, alog, dskip, ng, *, nseq):
    t = proj.shape[0]
    nch = t // nseq // CHUNK
    hd = PAIR_W
    rev = lambda s, c: s * nch + (nch - 1 - c)

    def body(dy_ref, xbc_ref, z_ref, dt_ref, hp_ref, dtb_ref, alog_ref, dsk_ref, ng_ref,
             dxbc_ref, dproj_ref, ddtb_ref, dalog_ref, ddsk_ref, dng_ref, dh_ref):
        first = (pl.program_id(0) == 0) & (pl.program_id(1) == 0)

        @pl.when(pl.program_id(1) == 0)
        def _():
            dh_ref[...] = jnp.zeros_like(dh_ref)

        @pl.when(first)
        def _():
            ddtb_ref[...] = jnp.zeros_like(ddtb_ref)
            dalog_ref[...] = jnp.zeros_like(dalog_ref)
            ddsk_ref[...] = jnp.zeros_like(ddsk_ref)
            dng_ref[...] = jnp.zeros_like(dng_ref)

        xs, bg, cg, dtr, z, ngs = _ssd_load(xbc_ref, z_ref, dt_ref, ng_ref)
        hp = [hp_ref[0, h * hd:(h + 1) * hd, :] for h in range(SSD_PAIRS)]
        _, vjp = jax.vjp(_ssd_chunk, xs, bg, cg, dtr, z, hp, dtb_ref[...], alog_ref[...], dsk_ref[...], ngs)
        douts = [dy_ref[:, h * hd:(h + 1) * hd] for h in range(SSD_PAIRS)]
        dhn = [dh_ref[h * hd:(h + 1) * hd, :] for h in range(SSD_PAIRS)]
        dxs, dbg, dcg, ddtr, dz, dhp, ddtb, dalog, ddsk, dngs = vjp((douts, dhn))
        dproj_ref[:, :COL_Z] = jnp.zeros((CHUNK, COL_Z), BF16)
        dproj_ref[:, COL_XBC:] = jnp.zeros((CHUNK, N_INP - COL_XBC), BF16)
        for h in range(SSD_PAIRS):
            dxbc_ref[:, h * hd:(h + 1) * hd] = dxs[h]
            dproj_ref[:, COL_Z + h * hd: COL_Z + (h + 1) * hd] = _b(dz[h])
            dh_ref[h * hd:(h + 1) * hd, :] = dhp[h]
            dng_ref[:, h * hd:(h + 1) * hd] += dngs[h]
        for g in range(SSD_GROUPS):
            dxbc_ref[:, D_SSD + g * D_STATE: D_SSD + (g + 1) * D_STATE] = dbg[g]
            dxbc_ref[:, D_SSD + (SSD_GROUPS + g) * D_STATE: D_SSD + (SSD_GROUPS + g + 1) * D_STATE] = dcg[g]
        dproj_ref[:, COL_DT:COL_DT + SSD_HEADS] = _b(ddtr)
        ddtb_ref[...] += ddtb
        dalog_ref[...] += dalog
        ddsk_ref[...] += ddsk

    small = pl.BlockSpec((1, SSD_HEADS), lambda s, c: (0, 0))
    return pl.pallas_call(
        body, name="ssd_bwd", grid=(nseq, nch),
        in_specs=[pl.BlockSpec((CHUNK, D_SSD), lambda s, c: (rev(s, c), 0)),
                  pl.BlockSpec((CHUNK, CONV_DIM), lambda s, c: (rev(s, c), 0)),
                  pl.BlockSpec((CHUNK, D_SSD), lambda s, c: (rev(s, c), COL_Z // D_SSD)),
                  pl.BlockSpec((CHUNK, 128), lambda s, c: (rev(s, c), COL_DT // 128)),
                  pl.BlockSpec((1, D_SSD, D_STATE), lambda s, c: (rev(s, c), 0, 0)),
                  small, small, small,
                  pl.BlockSpec((1, D_SSD), lambda s, c: (0, 0))],
        out_specs=[pl.BlockSpec((CHUNK, CONV_DIM), lambda s, c: (rev(s, c), 0)),
                   pl.BlockSpec((CHUNK, N_INP), lambda s, c: (rev(s, c), 0)),
                   small, small, small,
                   pl.BlockSpec((1, D_SSD), lambda s, c: (0, 0))],
        out_shape=[jax.ShapeDtypeStruct((t, CONV_DIM), F32), jax.ShapeDtypeStruct((t, N_INP), BF16),
                   jax.ShapeDtypeStruct((1, SSD_HEADS), F32), jax.ShapeDtypeStruct((1, SSD_HEADS), F32),
                   jax.ShapeDtypeStruct((1, SSD_HEADS), F32), jax.ShapeDtypeStruct((1, D_SSD), F32)],
        scratch_shapes=[pltpu.VMEM((D_SSD, D_STATE), F32)],
        compiler_params=_cp("arbitrary", "arbitrary"),
    )(dy, xbc, proj, proj, hprev, dtb, alog, dskip, ng)


def _gmlp_chunk(gu, gv, ws, bs_cols, vg, og):
    n = gu[0].shape[0]
    mask = _tri(n, True)
    au = [_gelu(t) for t in gu]
    av = [_gelu(t) for t in gv]
    r = lax.rsqrt(sum(jnp.sum(t * t, axis=1, keepdims=True) for t in av) * (1.0 / D_GM) + EPS)
    p = []
    for h in range(GM_HEADS):
        sv = _bdot(ws[h] * mask, av[h] * r * vg[h]) + bs_cols[h]
        p.append(au[h] * sv)
    r2 = lax.rsqrt(sum(jnp.sum(t * t, axis=1, keepdims=True) for t in p) * (1.0 / D_GM) + EPS)
    return [p[h] * r2 * og[h] for h in range(GM_HEADS)]


def _gmlp_load(u_ref, v_ref, ws_ref, bst_ref, vg_ref, og_ref):
    gu = _hslices(u_ref, GM_HEAD_DIM, GM_HEADS)
    gv = _hslices(v_ref, GM_HEAD_DIM, GM_HEADS)
    ws = [ws_ref[h] for h in range(GM_HEADS)]
    bs_cols = [bst_ref[:, h:h + 1] for h in range(GM_HEADS)]
    return gu, gv, ws, bs_cols, _hslices(vg_ref, GM_HEAD_DIM, GM_HEADS), _hslices(og_ref, GM_HEAD_DIM, GM_HEADS)


def _gmlp_specs():
    return [pl.BlockSpec((CHUNK, D_GM), lambda i: (i, COL_U // D_GM)),
            pl.BlockSpec((CHUNK, D_GM), lambda i: (i, COL_V // D_GM)),
            pl.BlockSpec((GM_HEADS, CHUNK, CHUNK), lambda i: (0, 0, 0)),
            pl.BlockSpec((CHUNK, GM_HEADS), lambda i: (0, 0)),
            pl.BlockSpec((1, D_GM), lambda i: (0, 0)),
            pl.BlockSpec((1, D_GM), lambda i: (0, 0))]


def _gmlp_fwd(proj, ycat, ws, bst, vg, og):
    t = proj.shape[0]

    def body(u_ref, v_ref, ws_ref, bst_ref, vg_ref, og_ref, ycat_ref, o_ref):
        del ycat_ref
        outs = _gmlp_chunk(*_gmlp_load(u_ref, v_ref, ws_ref, bst_ref, vg_ref, og_ref))
        for h in range(GM_HEADS):
            o_ref[:, h * GM_HEAD_DIM:(h + 1) * GM_HEAD_DIM] = _b(outs[h])

    return pl.pallas_call(
        body, name="gmlp_fwd", grid=(t // CHUNK,),
        in_specs=_gmlp_specs() + [ANY],
        out_specs=pl.BlockSpec((CHUNK, D_GM), lambda i: (i, D_SSD // D_GM)),
        out_shape=jax.ShapeDtypeStruct(ycat.shape, ycat.dtype),
        input_output_aliases={6: 0},
        compiler_params=_cp("parallel"),
    )(proj, proj, ws, bst, vg, og, ycat)


def _gmlp_bwd(dy, proj, ws, bst, vg, og, dproj):
    t = proj.shape[0]
    w = GM_HEAD_DIM

    def body(dy_ref, u_ref, v_ref, ws_ref, bst_ref, vg_ref, og_ref, dproj_ref,
             dgm_ref, dws_ref, dbst_ref, dvg_ref, dog_ref):
        del dproj_ref

        @pl.when(pl.program_id(0) == 0)
        def _():
            dws_ref[...] = jnp.zeros_like(dws_ref)
            dbst_ref[...] = jnp.zeros_like(dbst_ref)
            dvg_ref[...] = jnp.zeros_like(dvg_ref)
            dog_ref[...] = jnp.zeros_like(dog_ref)

        _, vjp = jax.vjp(_gmlp_chunk, *_gmlp_load(u_ref, v_ref, ws_ref, bst_ref, vg_ref, og_ref))
        dgu, dgv, dws, dbs, dvg, dog = vjp(_hslices(dy_ref, w, GM_HEADS))
        for h in range(GM_HEADS):
            dgm_ref[:, h * w:(h + 1) * w] = _b(dgu[h])
            dgm_ref[:, D_GM + h * w: D_GM + (h + 1) * w] = _b(dgv[h])
            dws_ref[h] += dws[h]
            dbst_ref[:, h:h + 1] += dbs[h]
            dvg_ref[:, h * w:(h + 1) * w] += dvg[h]
            dog_ref[:, h * w:(h + 1) * w] += dog[h]

    return pl.pallas_call(
        body, name="gmlp_bwd", grid=(t // CHUNK,),
        in_specs=[pl.BlockSpec((CHUNK, D_GM), lambda i: (i, 1))] + _gmlp_specs() + [ANY],
        out_specs=[pl.BlockSpec((CHUNK, 2 * D_GM), lambda i: (i, COL_U // (2 * D_GM))),
                   pl.BlockSpec((GM_HEADS, CHUNK, CHUNK), lambda i: (0, 0, 0)),
                   pl.BlockSpec((CHUNK, GM_HEADS), lambda i: (0, 0)),
                   pl.BlockSpec((1, D_GM), lambda i: (0, 0)),
                   pl.BlockSpec((1, D_GM), lambda i: (0, 0))],
        out_shape=[jax.ShapeDtypeStruct(dproj.shape, dproj.dtype), jax.ShapeDtypeStruct((GM_HEADS, CHUNK, CHUNK), F32),
                   jax.ShapeDtypeStruct((CHUNK, GM_HEADS), F32), jax.ShapeDtypeStruct((1, D_GM), F32),
                   jax.ShapeDtypeStruct((1, D_GM), F32)],
        input_output_aliases={7: 0},
        compiler_params=_cp("arbitrary"),
    )(dy, proj, proj, ws, bst, vg, og, dproj)


def _local_step(x, target, mods, lw, final_g, *, nseq, big_w, grad_sink):
    saved = []
    xin, delta, gate = x, None, None
    for l in range(DEPTH):
        w = lw[l]
        sh1, sc1, g1, sh2, sc2, g2 = mods[l]
        w_in = big_w(l, "w_in", xin if delta is None else delta)
        x0, h1 = _normmod_fwd(xin, delta, gate, w["norm1_g"], sc1, sh1, nseq=nseq, name=f"norm1_fwd_{l}")
        proj = _matmul(h1, w_in, tb=True, name=f"mm_in_{l}")
        xbc = _ssd_conv_fwd(proj, w["ssd_conv_w"], w["ssd_conv_b"], nseq=nseq)
        ycat, hprev = _ssd_fwd(xbc, proj, w["ssd_dt_bias"], w["ssd_a_log"], w["ssd_d"], w["ssd_norm_g"], nseq=nseq)
        ycat = _gmlp_fwd(proj, ycat, w["gm_ws"], w["gm_bst"], w["gm_vnorm_g"], w["gm_out_g"])
        w_out = big_w(l, "w_out", ycat)
        mix = _matmul(ycat, w_out, name=f"mm_out_{l}")
        x1, h2 = _normmod_fwd(x0, mix, g1, w["norm2_g"], sc2, sh2, nseq=nseq, name=f"norm2_fwd_{l}")
        ff_up = big_w(l, "ff_up", h2)
        up = _matmul(h2, ff_up, tb=True, name=f"mm_up_{l}")
        act = _ffn_act_fwd(up, w["ff_conv_w"], w["ff_conv_b"], nseq=nseq)
        ff_down = big_w(l, "ff_down", act)
        dn = _matmul(act, ff_down, name=f"mm_down_{l}")
        saved.append(dict(x0=x0, xin_delta=delta, xin_gate=gate, h1=h1, proj=proj, xbc=xbc, hprev=hprev, ycat=ycat,
                          mix=mix, x1=x1, h2=h2, up=up, act=act, dn=dn,
                          w_in=w_in, w_out=w_out, ff_up=ff_up, ff_down=ff_down))
        xin, delta, gate = x1, dn, g2

    loss, dx, ddelta, dgate, dfg = _final_loss(xin, delta, gate, final_g, target, nseq=nseq)

    small, dmods = [None] * DEPTH, [None] * DEPTH
    for l in reversed(range(DEPTH)):
        w, sv = lw[l], saved[l]
        sh1, sc1, g1, sh2, sc2, g2 = mods[l]
        dg2 = dgate
        g_ff_down = _matmul(sv["act"], ddelta, ta=True, name=f"mm_down_dw_{l}")
        dact = _matmul(ddelta, sv["ff_down"], tb=True, name=f"mm_down_dx_{l}")
        dgate_ff, dval_ff, dfcw, dfcb = _ffn_act_bwd(dact, sv["up"], w["ff_conv_w"], w["ff_conv_b"], nseq=nseq)
        g_ff_up = _matmul([dgate_ff, dval_ff], sv["h2"], ta=True, name=f"mm_up_dw_{l}")
        dep = grad_sink(l, "ffn", dict(ff_down=g_ff_down, ff_up=g_ff_up), dval_ff)
        dh2 = _matmul([dgate_ff, dval_ff], sv["ff_up"], name=f"mm_up_dx_{l}", dep=dep)
        dx, dmix, dg1, dn2g, dsc2, dsh2 = _normmod_bwd(dh2, dx, sv["x1"], sv["mix"], g1, w["norm2_g"], sc2,
                                                       nseq=nseq, name=f"norm2_bwd_{l}")
        g_w_out = _matmul(sv["ycat"], dmix, ta=True, name=f"mm_out_dw_{l}")
        dep = grad_sink(l, "w_out", dict(w_out=g_w_out), dmix)
        dycat = _matmul(dmix, sv["w_out"], tb=True, name=f"mm_out_dx_{l}", dep=dep)
        dxbc_act, dproj, ddtb, dalog, ddsk, dng = _ssd_bwd(dycat, sv["xbc"], sv["proj"], sv["hprev"], w["ssd_dt_bias"],
                                                          w["ssd_a_log"], w["ssd_d"], w["ssd_norm_g"], nseq=nseq)
        dproj, dscw, dscb = _ssd_conv_bwd(dxbc_act, sv["proj"], w["ssd_conv_w"], w["ssd_conv_b"], dproj, nseq=nseq)
        dproj, dws, dbst, dvg, dog = _gmlp_bwd(dycat, sv["proj"], w["gm_ws"], w["gm_bst"], w["gm_vnorm_g"], w["gm_out_g"], dproj)
        g_w_in = _matmul(dproj, sv["h1"], ta=True, name=f"mm_in_dw_{l}")
        dep = grad_sink(l, "w_in", dict(w_in=g_w_in), dproj)
        dh1 = _matmul(dproj, sv["w_in"], name=f"mm_in_dx_{l}", dep=dep)
        dx, ddelta, dgate, dn1g, dsc1, dsh1 = _normmod_bwd(dh1, dx, sv["x0"], sv["xin_delta"], sv["xin_gate"],
                                                           w["norm1_g"], sc1, nseq=nseq, name=f"norm1_bwd_{l}")
        small[l] = dict(norm1_g=dn1g, norm2_g=dn2g, ssd_norm_g=dng, gm_vnorm_g=dvg, gm_out_g=dog,
                        ssd_conv_w=dscw, ssd_conv_b=dscb, ff_conv_w=dfcw, ff_conv_b=dfcb,
                        ssd_dt_bias=ddtb, ssd_a_log=dalog, ssd_d=ddsk, gm_ws=dws, gm_bs=dbst.T)
        dmods[l] = jnp.concatenate([dsh1, dsc1, dg1, dsh2, dsc2, dg2], axis=-1)[:, 0, :]
    return loss, dx, small, dmods, dfg


def _all_gather(arrs, name, dep=None):
    n = len(arrs)
    extra = [] if dep is None else [dep]

    def body(*refs):
        ins, outs = refs[:n], refs[n + len(extra):2 * n + len(extra)]
        send_sems, recv_sems, local_sems = refs[2 * n + len(extra):]
        x, y, c = lax.axis_index("x"), lax.axis_index("y"), lax.axis_index("c")
        me, sibling = (x, y, c), (x, y, 1 - c)
        chips = [(1 - x, y), (x, 1 - y), (1 - x, 1 - y)]

        def copy(i, k, block, to, src=None):
            px, py, pc = block
            dst = outs[i].at[4 * px + 2 * py + pc]
            return pltpu.make_async_remote_copy(
                src_ref=dst if src is None else src, dst_ref=dst,
                send_sem=send_sems.at[7 * i + k], recv_sem=recv_sems.at[7 * i + k],
                device_id=to, device_id_type=MESH)

        mine = [pltpu.make_async_copy(ins[i], outs[i].at[4 * x + 2 * y + c], local_sems.at[i]) for i in range(n)]
        for cp in mine:
            cp.start()
        first = []
        for i in range(n):
            first.append(copy(i, 0, me, sibling, src=ins[i]))
            first += [copy(i, 1 + j, me, (*chip, c), src=ins[i]) for j, chip in enumerate(chips)]
        for cp in first:
            cp.start()
        passed = []
        for j, chip in enumerate(chips):
            for i in range(n):
                copy(i, 1 + j, (*chip, c), me).wait_recv()
                fwd = copy(i, 4 + j, (*chip, c), sibling)
                fwd.start()
                passed.append(fwd)
        for i in range(n):
            copy(i, 0, sibling, me).wait_recv()
            for j, chip in enumerate(chips):
                copy(i, 4 + j, (*chip, 1 - c), me).wait_recv()
        for cp in first + passed:
            cp.wait_send()
        for cp in mine:
            cp.wait()

    return pl.pallas_call(
        body, name=name,
        in_specs=[ANY] * (n + len(extra)), out_specs=[ANY] * n,
        out_shape=[jax.ShapeDtypeStruct((N_DEV,) + a.shape, a.dtype) for a in arrs],
        scratch_shapes=[pltpu.SemaphoreType.DMA((7 * n,)), pltpu.SemaphoreType.DMA((7 * n,)),
                        pltpu.SemaphoreType.DMA((n,))],
    )(*arrs, *extra)


def _exchange_sibling(arrs, name):
    n = len(arrs)

    def body(*refs):
        ins, outs = refs[:n], refs[n:2 * n]
        send_sems, recv_sems = refs[2 * n:]
        x, y, c = lax.axis_index("x"), lax.axis_index("y"), lax.axis_index("c")
        copies = []
        for i in range(n):
            for k in range(4):
                copies.append(pltpu.make_async_remote_copy(
                    src_ref=ins[i].at[2 * k + (1 - c)], dst_ref=outs[i].at[k],
                    send_sem=send_sems.at[4 * i + k], recv_sem=recv_sems.at[4 * i + k],
                    device_id=(x, y, 1 - c), device_id_type=MESH))
        for cp in copies:
            cp.start()
        for cp in copies:
            cp.wait_recv()
        for cp in copies:
            cp.wait_send()

    return pl.pallas_call(
        body, name=name,
        in_specs=[ANY] * n, out_specs=[ANY] * n,
        out_shape=[jax.ShapeDtypeStruct((4,) + a.shape[1:], a.dtype) for a in arrs],
        scratch_shapes=[pltpu.SemaphoreType.DMA((4 * n,)), pltpu.SemaphoreType.DMA((4 * n,))],
    )(*arrs)


def _exchange_chips(arrs, name):
    n = len(arrs)

    def body(*refs):
        ins, outs = refs[:n], refs[n:2 * n]
        send_sems, recv_sems = refs[2 * n:]
        x, y, c = lax.axis_index("x"), lax.axis_index("y"), lax.axis_index("c")
        chips = [(1 - x, y), (x, 1 - y), (1 - x, 1 - y)]
        copies = []
        for i in range(n):
            for j, (cx, cy) in enumerate(chips):
                copies.append(pltpu.make_async_remote_copy(
                    src_ref=ins[i].at[2 * cx + cy], dst_ref=outs[i].at[j],
                    send_sem=send_sems.at[3 * i + j], recv_sem=recv_sems.at[3 * i + j],
                    device_id=(cx, cy, c), device_id_type=MESH))
        for cp in copies:
            cp.start()
        for cp in copies:
            cp.wait_recv()
        for cp in copies:
            cp.wait_send()

    return pl.pallas_call(
        body, name=name,
        in_specs=[ANY] * n, out_specs=[ANY] * n,
        out_shape=[jax.ShapeDtypeStruct((3,) + a.shape[1:], a.dtype) for a in arrs],
        scratch_shapes=[pltpu.SemaphoreType.DMA((3 * n,)), pltpu.SemaphoreType.DMA((3 * n,))],
    )(*arrs)


def _add_sibling(a, r, pos, name):
    _, depth, rows, cols = a.shape
    tr = _tile(rows, 256) if rows % 8 == 0 else rows
    a3 = a.reshape(N_DEV * depth, rows, cols)
    r3 = r.reshape(4 * depth, rows, cols)

    def body(pos_ref, a_ref, r_ref, o_ref):
        o_ref[...] = a_ref[...] + r_ref[...]

    out = pl.pallas_call(
        body, name=name,
        grid_spec=pltpu.PrefetchScalarGridSpec(
            num_scalar_prefetch=1, grid=(4 * depth, rows // tr),
            in_specs=[pl.BlockSpec((1, tr, cols), lambda q, i, p: ((2 * (q // depth) + p[0]) * depth + q % depth, i, 0)),
                      pl.BlockSpec((1, tr, cols), lambda q, i, p: (q, i, 0))],
            out_specs=pl.BlockSpec((1, tr, cols), lambda q, i, p: (q, i, 0))),
        out_shape=jax.ShapeDtypeStruct((4 * depth, rows, cols), F32),
        compiler_params=_cp("parallel", "parallel"),
    )(pos, a3, r3)
    return out.reshape(4, depth, rows, cols)


HBM = pl.BlockSpec(memory_space=pltpu.HBM)
SEM = pl.BlockSpec(memory_space=pltpu.SEMAPHORE)
EFFECT = pltpu.SideEffectType.DATAFLOW_SIDE_EFFECTING


def _peer(k):
    x, y, c = lax.axis_index("x"), lax.axis_index("y"), lax.axis_index("c")
    return (1 - x if k & 4 else x, 1 - y if k & 2 else y, 1 - c if k & 1 else c)


def _xc_copies(scatter, srcs, lands, send_sems, recv_sems):
    x, y, c = lax.axis_index("x"), lax.axis_index("y"), lax.axis_index("c")
    copies = []
    for i in range(len(srcs)):
        for k in range(1, N_DEV):
            px, py, pc = _peer(k)
            src = srcs[i].at[4 * px + 2 * py + pc] if scatter else srcs[i]
            dst = lands[i].at[k - 1] if scatter else lands[i].at[4 * x + 2 * y + c]
            copies.append(pltpu.make_async_remote_copy(
                src_ref=src, dst_ref=dst, send_sem=send_sems[i].at[k - 1], recv_sem=recv_sems[i].at[k - 1],
                device_id=(px, py, pc), device_id_type=MESH))
    return copies


def _xc_start(scatter, arrs, after, name):
    n = len(arrs)
    lands = [lax.empty((N_DEV - 1,) + a.shape[1:] if scatter else (N_DEV,) + a.shape, a.dtype) for a in arrs]

    def body(*refs):
        srcs, lnd = refs[:n], refs[n:2 * n]
        send_sems, recv_sems = refs[2 * n + 1:3 * n + 1], refs[3 * n + 1:4 * n + 1]
        token = refs[6 * n + 1]
        for cp in _xc_copies(scatter, srcs, lnd, send_sems, recv_sems):
            cp.start()
        token[...] = jnp.zeros_like(token)

    outs = pl.pallas_call(
        body, name=name,
        out_shape=[pltpu.SemaphoreType.DMA((N_DEV - 1,))] * (2 * n)
        + [pltpu.HBM(a.shape, a.dtype) for a in arrs] + [pltpu.HBM(a.shape, a.dtype) for a in lands]
        + [jax.ShapeDtypeStruct((8, 128), F32)],
        in_specs=[HBM] * (2 * n) + [ANY],
        out_specs=[SEM] * (2 * n) + [HBM] * (2 * n) + [pl.BlockSpec(memory_space=pltpu.VMEM)],
        input_output_aliases={i: 2 * n + i for i in range(2 * n)},
        compiler_params=pltpu.CompilerParams(has_side_effects=EFFECT),
    )(*[pltpu.with_memory_space_constraint(a, pltpu.HBM) for a in list(arrs) + lands], after)
    return outs[:n], outs[n:2 * n], outs[2 * n:3 * n], outs[3 * n:4 * n], outs[4 * n][0, 0]


def _xc_wait(scatter, send_sems, recv_sems, srcs, lands, after, name):
    n = len(srcs)

    def body(*refs):
        s_refs, l_refs = refs[:n], refs[n:2 * n]
        ss, rs = refs[2 * n:3 * n], refs[3 * n:4 * n]
        for cp in _xc_copies(scatter, s_refs, l_refs, ss, rs):
            cp.wait_send()
            cp.wait_recv()

    outs = pl.pallas_call(
        body, name=name,
        out_shape=[pltpu.HBM(a.shape, a.dtype) for a in list(srcs) + list(lands)],
        in_specs=[HBM] * (2 * n) + [SEM] * (2 * n) + [ANY],
        out_specs=[HBM] * (2 * n),
        input_output_aliases={i: i for i in range(2 * n)},
        compiler_params=pltpu.CompilerParams(has_side_effects=EFFECT),
    )(*srcs, *lands, *send_sems, *recv_sems, after)
    return outs[:n], outs[n:]


def _adamw_math(w, g, m, v):
    m = ADAM_B1 * m + (1.0 - ADAM_B1) * g
    v = ADAM_B2 * v + (1.0 - ADAM_B2) * (g * g)
    m_hat = m / (1.0 - ADAM_B1 ** ADAM_STEP)
    v_hat = v / (1.0 - ADAM_B2 ** ADAM_STEP)
    delta = -ADAM_LR * (m_hat / (jnp.sqrt(v_hat) + ADAM_EPS) + ADAM_WD * w)
    return delta, m, v


def _adamw_sharded(parts, w, m, v, pos, name):
    depth, rows, cols = w.shape
    tr = _tile(rows, 256) if rows % 8 == 0 else rows
    npart = len(parts)

    def body(pos_ref, *refs):
        prefs = refs[:npart]
        w_ref, m_ref, v_ref, g_out, d_out, m_out, v_out = refs[npart:]
        g = prefs[0][...]
        for pr in prefs[1:]:
            g = g + pr[...]
        delta, mn, vn = _adamw_math(w_ref[...], g, m_ref[...], v_ref[...])
        g_out[...] = g
        d_out[...] = delta
        m_out[...] = mn
        v_out[...] = vn

    def part_spec(fn):
        return pl.BlockSpec((1, tr, cols), lambda l, i, p: (fn(p) * depth + l, i, 0))

    blk = pl.BlockSpec((1, tr, cols), lambda l, i, p: (l, i, 0))
    shp = jax.ShapeDtypeStruct((depth, rows, cols), F32)
    return pl.pallas_call(
        body, name=name,
        grid_spec=pltpu.PrefetchScalarGridSpec(
            num_scalar_prefetch=1, grid=(depth, rows // tr),
            in_specs=[part_spec(fn) for _, fn in parts] + [blk, blk, blk],
            out_specs=[blk, blk, blk, blk]),
        out_shape=[shp, shp, shp, shp],
        compiler_params=_cp("parallel", "parallel"),
    )(pos, *[a for a, _ in parts], w, m, v)


def _sum_parts(parts, pos, name):
    rows, cols = parts[0][0].shape[1:]
    tc = _tile(cols, 256)

    def body(pos_ref, *refs):
        g = refs[0][0]
        for r in refs[1:-1]:
            g = g + r[0]
        refs[-1][...] = g

    return pl.pallas_call(
        body, name=name,
        grid_spec=pltpu.PrefetchScalarGridSpec(
            num_scalar_prefetch=1, grid=(cols // tc,),
            in_specs=[pl.BlockSpec((1, rows, tc), functools.partial(lambda fn, j, p: (fn(p), 0, j), fn)) for _, fn in parts],
            out_specs=pl.BlockSpec((rows, tc), lambda j, p: (0, j))),
        out_shape=jax.ShapeDtypeStruct((rows, cols), F32),
        compiler_params=_cp("parallel"),
    )(pos, *[a for a, _ in parts])


def _adamw_layer(parts, w, m, v, pos, layer, prev, name):
    depth, rows, cols = w.shape
    tr = _tile(rows, 256) if rows % 8 == 0 else rows
    npart = len(parts)
    nprev = 0 if prev is None else 4

    def body(pos_ref, *refs):
        prefs = refs[:npart]
        w_ref, m_ref, v_ref = refs[npart:npart + 3]
        g_out, d_out, m_out, v_out = refs[npart + 3 + nprev:]
        g = prefs[0][...]
        for pr in prefs[1:]:
            g = g + pr[...]
        delta, mn, vn = _adamw_math(w_ref[...], g, m_ref[...], v_ref[...])
        g_out[...] = g
        d_out[...] = delta
        m_out[...] = mn
        v_out[...] = vn

    def part_spec(fn):
        return pl.BlockSpec((1, tr, cols), lambda i, p: (fn(p), i, 0))

    blk = pl.BlockSpec((1, tr, cols), lambda i, p: (layer, i, 0))
    shp = jax.ShapeDtypeStruct((depth, rows, cols), F32)
    first_prev = 1 + npart + 3
    return pl.pallas_call(
        body, name=name,
        grid_spec=pltpu.PrefetchScalarGridSpec(
            num_scalar_prefetch=1, grid=(rows // tr,),
            in_specs=[part_spec(fn) for _, fn in parts] + [blk, blk, blk] + [ANY] * nprev,
            out_specs=[blk, blk, blk, blk]),
        out_shape=[shp, shp, shp, shp],
        input_output_aliases={first_prev + j: j for j in range(nprev)},
        compiler_params=_cp("parallel"),
    )(pos, *[a for a, _ in parts], w, m, v, *(prev or ()))


_P1024 = ["norm1_g", "norm2_g", "ssd_norm_g", "gm_vnorm_g", "gm_out_g"]
_P16 = ["ssd_dt_bias", "ssd_a_log", "ssd_d"]


def _adamw_small(gath, wmv):
    names = list(wmv.keys())
    classes = list(gath.keys())
    flat_in = [gath[k] for k in classes]
    for nme in names:
        flat_in += list(wmv[nme])
    out_shapes = []
    for nme in names:
        out_shapes += [jax.ShapeDtypeStruct(wmv[nme][0].shape, F32)] * 4
    out_shapes += [jax.ShapeDtypeStruct((DEPTH, SSD_CONV, CONV_DIM), F32), jax.ShapeDtypeStruct((DEPTH, FF_CONV, D_FF), F32),
                   jax.ShapeDtypeStruct((1, SSD_HEADS), F32)]
    scratch = [pltpu.VMEM(gath[k].shape[1:], F32) for k in classes]
    ncls = len(classes)

    def body(*refs):
        g_refs = dict(zip(classes, refs[:ncls]))
        pos = ncls
        w_refs = {}
        for nme in names:
            w_refs[nme] = refs[pos:pos + 3]
            pos += 3
        o_refs = {}
        for nme in names:
            o_refs[nme] = refs[pos:pos + 4]
            pos += 4
        scw_out, fcw_out, loss_out = refs[pos], refs[pos + 1], refs[pos + 2]
        s_refs = dict(zip(classes, refs[pos + 3:]))
        for k in classes:
            acc = g_refs[k][0]
            for dev in range(1, N_DEV):
                acc = acc + g_refs[k][dev]
            s_refs[k][...] = acc

        def apply(nme, grad_of):
            w_ref, m_ref, v_ref = w_refs[nme]
            g_out, d_out, m_out, v_out = o_refs[nme]
            shape = w_ref.shape
            if len(shape) == 2:
                idxs = [(slice(l, l + 1),) for l in range(shape[0])]
            elif len(shape) == 3:
                idxs = [(l,) for l in range(shape[0])]
            else:
                idxs = [(l, h) for l in range(shape[0]) for h in range(shape[1])]
            for n_i, ix in enumerate(idxs):
                g = grad_of(n_i)
                delta, mn, vn = _adamw_math(w_ref[ix], g, m_ref[ix], v_ref[ix])
                g_out[ix] = g
                d_out[ix] = delta
                m_out[ix] = mn
                v_out[ix] = vn

        s1024, s1536, s2816, s16, s128, s6144 = (s_refs[k] for k in classes)
        for n_i, nme in enumerate(_P1024):
            apply(nme, lambda l, b=2 * n_i: s1024[b + l:b + l + 1, :])
        apply("final_g", lambda l: s1024[10:11, :])
        apply("ssd_conv_b", lambda l: s1536[8 + l:9 + l, :])
        apply("ff_conv_b", lambda l: s2816[6 + l:7 + l, :])
        for n_i, nme in enumerate(_P16):
            apply(nme, lambda l, b=2 * n_i: s16[b + l:b + l + 1, :])
        apply("gm_ws", lambda q: s128[q * CHUNK:(q + 1) * CHUNK, :])
        apply("gm_bs", lambda l: s128[2048 + 8 * l:2048 + 8 * (l + 1), :])
        apply("ada_b", lambda l: s6144[2 * l:2 * l + 1, :] + s6144[2 * l + 1:2 * l + 2, :])
        for l in range(DEPTH):
            scw_out[l] = s1536[SSD_CONV * l:SSD_CONV * (l + 1), :]
            fcw_out[l] = s2816[FF_CONV * l:FF_CONV * (l + 1), :]
        loss_out[...] = s16[2 * len(_P16):2 * len(_P16) + 1, :]

    outs = pl.pallas_call(
        body, name="adamw_small",
        out_shape=out_shapes,
        scratch_shapes=scratch,
        compiler_params=pltpu.CompilerParams(vmem_limit_bytes=VMEM_LIMIT),
    )(*flat_in)
    res = {nme: tuple(outs[4 * i:4 * i + 4]) for i, nme in enumerate(names)}
    return res, outs[-3], outs[-2], outs[-1]


_WEIGHTS = ['ada_w', 'ada_b', 'norm1_g', 'norm2_g', 'w_in', 'ssd_conv_w', 'ssd_conv_b', 'ssd_dt_bias', 'ssd_a_log',
            'ssd_d', 'ssd_norm_g', 'gm_vnorm_g', 'gm_ws', 'gm_bs', 'gm_out_g', 'w_out', 'ff_up', 'ff_conv_w',
            'ff_conv_b', 'ff_down', 'final_g']


_O_XBC, _O_DT, _O_GM = D_SSD, D_SSD + CONV_DIM, D_SSD + CONV_DIM + SSD_HEADS


_TRANSPOSED = ("w_in", "ff_up")


def _full_weight(name, g):
    full = g.reshape(g.shape[0] * g.shape[1], g.shape[2])
    if name != "w_in":
        return full
    zpad = jnp.zeros((N_INP - N_IN, full.shape[1]), full.dtype)
    return jnp.concatenate([full[_O_GM:], full[:_O_XBC], full[_O_XBC:_O_DT], full[_O_DT:_O_GM], zpad], axis=0)


def _by_owner(name, grad):
    if name == "w_in":
        grad = jnp.concatenate([grad[COL_Z:COL_XBC], grad[COL_XBC:COL_DT], grad[COL_DT:COL_DT + SSD_HEADS], grad[:COL_Z]], axis=0)
    return grad.reshape(N_DEV, grad.shape[0] // N_DEV, grad.shape[1])


def kernel(x, c, ada_w, ada_b, norm1_g, norm2_g, w_in, ssd_conv_w, ssd_conv_b, ssd_dt_bias, ssd_a_log, ssd_d, ssd_norm_g, gm_vnorm_g, gm_ws, gm_bs, gm_out_g, w_out, ff_up, ff_conv_w, ff_conv_b, ff_down, final_g, loss_target, m_ada_w, m_ada_b, m_norm1_g, m_norm2_g, m_w_in, m_ssd_conv_w, m_ssd_conv_b, m_ssd_dt_bias, m_ssd_a_log, m_ssd_d, m_ssd_norm_g, m_gm_vnorm_g, m_gm_ws, m_gm_bs, m_gm_out_g, m_w_out, m_ff_up, m_ff_conv_w, m_ff_conv_b, m_ff_down, m_final_g, v_ada_w, v_ada_b, v_norm1_g, v_norm2_g, v_w_in, v_ssd_conv_w, v_ssd_conv_b, v_ssd_dt_bias, v_ssd_a_log, v_ssd_d, v_ssd_norm_g, v_gm_vnorm_g, v_gm_ws, v_gm_bs, v_gm_out_g, v_w_out, v_ff_up, v_ff_conv_w, v_ff_conv_b, v_ff_down, v_final_g):
    given = dict(locals())
    wts = {n: given[n] for n in _WEIGHTS}
    mom = {n: given["m_" + n] for n in _WEIGHTS}
    var = {n: given["v_" + n] for n in _WEIGHTS}
    nseq, seq, d = x.shape
    ix, iy, ic = lax.axis_index("x"), lax.axis_index("y"), lax.axis_index("c")
    me = 4 * ix + 2 * iy + ic
    me_arr = me.astype(jnp.int32).reshape(1)

    def shard(l, name):
        s = _b(wts[name][l])
        return s.T if name in _TRANSPOSED else s

    g_win0, g_scw, g_fcw, c_all = _all_gather([shard(0, "w_in"), ssd_conv_w, ff_conv_w, c], "gather_first")
    scw_f = jnp.transpose(g_scw, (1, 2, 0, 3)).reshape(DEPTH, SSD_CONV, CONV_DIM)
    fcw_f = jnp.transpose(g_fcw, (1, 2, 0, 3)).reshape(DEPTH, FF_CONV, D_FF)
    c_all = c_all.reshape(N_DEV * nseq, d)

    n_ada = ada_w.shape[2]
    ada_b_shard = lax.dynamic_slice_in_dim(ada_b, me * n_ada, n_ada, axis=1).reshape(DEPTH, 1, n_ada)
    mod_part, c_act = _ada_fwd(c_all, ada_w, ada_b_shard)
    (mod_g,) = _all_gather([mod_part], "gather_mod")
    mod_all = jnp.transpose(mod_g, (1, 2, 0, 3)).reshape(DEPTH, N_DEV * nseq, N_MOD * d)
    mod_mine = lax.dynamic_slice_in_dim(mod_all, me * nseq, nseq, axis=1)
    mods = [[mod_mine[l, :, k * d:(k + 1) * d].reshape(nseq, 1, d) for k in range(N_MOD)] for l in range(DEPTH)]

    later = [(0, "w_out"), (0, "ff_up"), (0, "ff_down"), (1, "w_in"), (1, "w_out"), (1, "ff_up"), (1, "ff_down")]
    ag_ssem, ag_rsem, ag_src, ag_land, ag_zero = _xc_start(False, [shard(l, n) for l, n in later], mod_g, "ag_start")
    ag_groups = {(0, "w_out"): [0], (0, "ff_up"): [1, 2], (1, "w_in"): [3, 4], (1, "ff_up"): [5, 6]}
    big_cache = {(0, "w_in"): _full_weight("w_in", g_win0)}

    def big_w(l, name, after):
        if (l, name) not in big_cache:
            idx = ag_groups[(l, name)]
            pick = lambda seq_: [seq_[i] for i in idx]
            srcs, lands = _xc_wait(False, pick(ag_ssem), pick(ag_rsem), pick(ag_src), pick(ag_land), after,
                                   f"ag_wait_{l}_{name}")
            for i, src, land in zip(idx, srcs, lands):
                big_cache[later[i]] = _full_weight(later[i][1], lax.dynamic_update_index_in_dim(land, src, me, 0))
        return big_cache[(l, name)]

    lw = []
    for l in range(DEPTH):
        lw.append(dict(
            norm1_g=norm1_g[l:l + 1] + (ag_zero if l == 0 else 0.0), norm2_g=norm2_g[l:l + 1], ssd_conv_w=scw_f[l],
            ssd_conv_b=ssd_conv_b[l:l + 1], ssd_dt_bias=ssd_dt_bias[l:l + 1], ssd_a_log=ssd_a_log[l:l + 1],
            ssd_d=ssd_d[l:l + 1], ssd_norm_g=ssd_norm_g[l:l + 1], gm_vnorm_g=gm_vnorm_g[l:l + 1], gm_ws=gm_ws[l],
            gm_bst=gm_bs[l].T, gm_out_g=gm_out_g[l:l + 1], ff_conv_w=fcw_f[l], ff_conv_b=ff_conv_b[l:l + 1]))

    outs = {}
    pending = {}

    def rs_finish(l, group, after):
        names, ssem, rsem, srcs, lands = pending.pop((l, group))
        srcs, lands = _xc_wait(True, ssem, rsem, srcs, lands, after, f"rs_wait_{l}_{group}")
        for nme, own, land in zip(names, srcs, lands):
            parts = [(own, lambda p: p[0])] + [(land, lambda p, k=k: k) for k in range(N_DEV - 1)]
            if nme in _TRANSPOSED:
                g_t = _sum_parts(parts, me_arr, f"rs_sum_{nme}_{l}")
                parts = [(g_t.T[None], lambda p: 0)]
            outs[nme] = _adamw_layer(parts, wts[nme], mom[nme], var[nme], me_arr, l, outs.get(nme), f"adamw_{nme}_{l}")
        return outs[names[-1]][0]

    def grad_sink(l, group, grads, after):
        names = list(grads)
        ssem, rsem, srcs, lands, zero = _xc_start(True, [_by_owner(n, grads[n]) for n in names], after, f"rs_start_{l}_{group}")
        pending[(l, group)] = (names, ssem, rsem, srcs, lands)
        return zero.reshape(1, 1)

    loss_p, grad_x, small, dmods, dfg = _local_step(
        x.reshape(nseq * seq, d), loss_target.reshape(nseq * seq, d), mods, lw, final_g.reshape(1, d), nseq=nseq,
        big_w=big_w, grad_sink=grad_sink)

    def rows(name):
        return [small[l][name] for l in range(DEPTH)]

    p1024 = jnp.concatenate(sum([rows(n) for n in _P1024], []) + [dfg], axis=0)
    p1536 = jnp.concatenate(rows("ssd_conv_w") + rows("ssd_conv_b"), axis=0)
    p2816 = jnp.concatenate(rows("ff_conv_w") + rows("ff_conv_b"), axis=0)
    p16 = jnp.concatenate(sum([rows(n) for n in _P16], []) + [loss_p[:, :SSD_HEADS]], axis=0)
    p128 = jnp.concatenate([small[l]["gm_ws"].reshape(GM_HEADS * CHUNK, CHUNK) for l in range(DEPTH)] + rows("gm_bs"), axis=0)
    p6144 = jnp.concatenate(dmods, axis=0)
    done = grad_x
    for l, grp in ((1, "ffn"), (1, "w_out"), (1, "w_in"), (0, "ffn"), (0, "w_out")):
        done = rs_finish(l, grp, done)
    gathered = _all_gather([p1024, p1536, p2816, p16, p128, p6144], "gather_small", dep=done)
    gath = dict(zip(["p1024", "p1536", "p2816", "p16", "p128", "p6144"], gathered))

    dmod_all = jnp.transpose(gath["p6144"].reshape(N_DEV, DEPTH, nseq, N_MOD * d), (1, 0, 2, 3)).reshape(
        DEPTH, N_DEV * nseq, N_MOD * d)
    small_names = _P1024 + ["final_g", "ssd_conv_b", "ff_conv_b"] + _P16 + ["gm_ws", "gm_bs", "ada_b"]
    wmv = {}
    for nme in small_names:
        if nme == "final_g":
            wmv[nme] = tuple(a.reshape(1, d) for a in (wts[nme], mom[nme], var[nme]))
        else:
            wmv[nme] = (wts[nme], mom[nme], var[nme])
    small_out, scw_full, fcw_full, loss_sum = _adamw_small(gath, wmv)
    loss = loss_sum[0, 0]
    rs_finish(0, "w_in", scw_full)
    for nme in small_names:
        outs[nme] = small_out[nme]
    outs["final_g"] = tuple(a.reshape(d) for a in outs["final_g"])

    n_scw, n_fcw = ssd_conv_w.shape[2], ff_conv_w.shape[2]
    g_scw_mine = lax.dynamic_slice_in_dim(scw_full, me * n_scw, n_scw, axis=2)
    g_fcw_mine = lax.dynamic_slice_in_dim(fcw_full, me * n_fcw, n_fcw, axis=2)
    outs["ssd_conv_w"] = _adamw_sharded([(g_scw_mine, lambda p: 0)], ssd_conv_w, m_ssd_conv_w, v_ssd_conv_w, me_arr, "adamw_ssd_conv_w")
    outs["ff_conv_w"] = _adamw_sharded([(g_fcw_mine, lambda p: 0)], ff_conv_w, m_ff_conv_w, v_ff_conv_w, me_arr, "adamw_ff_conv_w")

    dmod_cols = _b(lax.dynamic_slice_in_dim(dmod_all, me * n_ada, n_ada, axis=2))
    g_ada = jnp.stack([_matmul(c_act, dmod_cols[l], ta=True, name=f"mm_ada_dw_{l}") for l in range(DEPTH)])
    outs["ada_w"] = _adamw_sharded([(g_ada, lambda p: 0)], ada_w, m_ada_w, v_ada_w, me_arr, "adamw_ada_w")

    result = [loss, grad_x.reshape(nseq, seq, d)]
    for k in range(4):
        result += [outs[n][k] for n in _WEIGHTS]
    return tuple(result)
```

```python
import functools
import math

import jax
import jax.numpy as jnp
from jax import lax
from jax.experimental import pallas as pl
from jax.experimental.pallas import tpu as pltpu

F32 = jnp.float32
BF16 = jnp.bfloat16

N_DEV = 8
D_MODEL = 1024
DEPTH = 2
CHUNK = 128
SSD_HEADS = 16
SSD_HEAD_DIM = 64
SSD_GROUPS = 2
HEADS_PER_GROUP = SSD_HEADS // SSD_GROUPS
GROUP_WIDTH = HEADS_PER_GROUP * SSD_HEAD_DIM
D_STATE = 128
D_SSD = 1024
CONV_DIM = 1536
SSD_CONV = 4
GM_HEADS = 8
GM_HEAD_DIM = 128
D_GM = 1024
D_FF = 2816
FF_CONV = 3
N_IN = 4624
N_MOD = 6
EPS = 1e-6

N_INP = 5120
COL_U, COL_V, COL_Z, COL_XBC, COL_DT = 0, 1024, 2048, 3072, 4608
DT_BLOCK = 512

ADAM_LR = 0.001
ADAM_B1 = 0.9
ADAM_B2 = 0.999
ADAM_EPS = 1e-08
ADAM_WD = 0.01
ADAM_STEP = 10

VMEM_LIMIT = 56 * 1024 * 1024
MESH = pl.DeviceIdType.MESH
ANY = pl.BlockSpec(memory_space=pl.ANY)


def _cp(*sem):
    return pltpu.CompilerParams(dimension_semantics=sem, vmem_limit_bytes=VMEM_LIMIT)


def _tile(n, pref):
    if n <= pref or n % 128:
        return n
    best = 128
    for t in range(128, pref + 1, 128):
        if n % t == 0:
            best = t
    return best


def _silu(x):
    return x * jax.nn.sigmoid(x)


def _gelu(x):
    return 0.5 * x * (1.0 + lax.erf(x * (1.0 / math.sqrt(2.0))))


def _softplus(x):
    return jnp.maximum(x, 0.0) + jnp.log1p(jnp.exp(-jnp.abs(x)))


def _rms(x, g, width):
    return x * lax.rsqrt(jnp.sum(x * x, axis=-1, keepdims=True) / width + EPS) * g


def _b(x):
    return x.astype(BF16)


_NN = (((1,), (0,)), ((), ()))
_NT = (((1,), (1,)), ((), ()))
_TN = (((0,), (0,)), ((), ()))


def _dg(a, b, dn):
    return lax.dot_general(_b(a), _b(b), dn, preferred_element_type=F32)


@jax.custom_vjp
def _bdot(a, b):
    return _dg(a, b, _NN)


def _bdot_fwd(a, b):
    return _dg(a, b, _NN), (a, b)


def _bdot_bwd(res, ct):
    a, b = res
    return _dg(ct, b, _NT), _dg(a, ct, _TN)


_bdot.defvjp(_bdot_fwd, _bdot_bwd)


@jax.custom_vjp
def _bdot_nt(a, b):
    return _dg(a, b, _NT)


def _bdot_nt_fwd(a, b):
    return _dg(a, b, _NT), (a, b)


def _bdot_nt_bwd(res, ct):
    a, b = res
    return _dg(ct, b, _NN), _dg(ct, a, _TN)


_bdot_nt.defvjp(_bdot_nt_fwd, _bdot_nt_bwd)


@jax.custom_vjp
def _bdot_tn(a, b):
    return _dg(a, b, _TN)


def _bdot_tn_fwd(a, b):
    return _dg(a, b, _TN), (a, b)


def _bdot_tn_bwd(res, ct):
    a, b = res
    return _dg(b, ct, _NT), _dg(a, ct, _NN)


_bdot_tn.defvjp(_bdot_tn_fwd, _bdot_tn_bwd)


def _tri(n, lower):
    r = lax.broadcasted_iota(jnp.int32, (n, n), 0)
    c = lax.broadcasted_iota(jnp.int32, (n, n), 1)
    return ((r >= c) if lower else (r <= c)).astype(F32)


def _eye(n):
    r = lax.broadcasted_iota(jnp.int32, (n, n), 0)
    c = lax.broadcasted_iota(jnp.int32, (n, n), 1)
    return (r == c).astype(F32)


def _hdot(a, b, dn):
    return lax.dot_general(a, b, dn, precision=lax.Precision.HIGHEST, preferred_element_type=F32)


@jax.custom_vjp
def _cumsum_rows(x):
    return _hdot(_tri(x.shape[0], True), x, _NN)


def _cumsum_rows_fwd(x):
    return _cumsum_rows(x), None


def _cumsum_rows_bwd(_, ct):
    return (_hdot(_tri(ct.shape[0], False), ct, _NN),)


_cumsum_rows.defvjp(_cumsum_rows_fwd, _cumsum_rows_bwd)


@jax.custom_vjp
def _transpose(x):
    return _hdot(_eye(x.shape[1]), x, _NT)


def _transpose_fwd(x):
    return _transpose(x), None


def _transpose_bwd(_, ct):
    return (_hdot(_eye(ct.shape[1]), ct, _NT),)


_transpose.defvjp(_transpose_fwd, _transpose_bwd)


MXU_WIDTH = 256
MATMUL_TILE_CAP = 2816
MATMUL_VMEM = 44 * 1024 * 1024


def _mxu_tiles(n):
    if n <= MATMUL_TILE_CAP or n % 128:
        return [n]
    for unit in (MXU_WIDTH, 128):
        opts = [t for t in range(unit, MATMUL_TILE_CAP + 1, unit) if n % t == 0]
        if opts:
            return opts
    return [n]


def _matmul(a, b, *, ta=False, tb=False, name, dep=None):
    pieces = list(a) if isinstance(a, (list, tuple)) else [a]
    npc = len(pieces)
    rows, width = pieces[0].shape
    assert all(p.shape == (rows, width) for p in pieces)
    if ta:
        k_dim, m_dim = rows, width * npc
    else:
        m_dim, k_dim = rows, width * npc
    if tb:
        n_dim, kb = b.shape
    else:
        kb, n_dim = b.shape
    assert kb == k_dim, (pieces[0].shape, npc, b.shape, ta, tb)
    m_unit = width if npc > 1 and ta else m_dim
    k_unit = width if npc > 1 and not ta else k_dim
    tm = _tile(m_unit, 1536)
    tn_opts, tk_opts = _mxu_tiles(n_dim), _mxu_tiles(k_unit)
    tn, tk = tn_opts.pop(), tk_opts.pop()
    while 4 * (tm * tk + tk * tn) + 8 * tm * tn > MATMUL_VMEM:
        if tn >= tk and tn_opts:
            tn = tn_opts.pop()
        else:
            tk = tk_opts.pop()
    ni, nj, nk = m_dim // tm, n_dim // tn, k_dim // tk
    per = width // (tm if ta else tk)
    dn = (((0 if ta else 1,), (1 if tb else 0,)), ((), ()))

    a_bytes, b_bytes = m_dim * k_dim, k_dim * n_dim
    m_outer = nk > 1 or a_bytes + b_bytes * ni <= b_bytes + a_bytes * nj
    if m_outer:
        ij = lambda o, n, k: (o, n)
        grid = (ni, nj, nk)
    else:
        ij = lambda o, n, k: (n, o)
        grid = (nj, ni, nk)

    def body(*refs):
        a_refs, b_ref, o_ref = refs[:npc], refs[npc], refs[-1]
        k = pl.program_id(2)
        i = pl.program_id(0 if m_outer else 1)
        along = i if ta else k

        def step(a_ref):
            p = lax.dot_general(a_ref[...], b_ref[...], dn, preferred_element_type=F32)
            if nk == 1:
                o_ref[...] = p
            else:
                @pl.when(k == 0)
                def _():
                    o_ref[...] = p

                @pl.when(k > 0)
                def _():
                    o_ref[...] += p

        if npc == 1:
            step(a_refs[0])
        else:
            for pc in range(npc):
                pl.when((along >= pc * per) & (along < (pc + 1) * per))(functools.partial(step, a_refs[pc]))

    def a_map(pc, o, n, k):
        i, _ = ij(o, n, k)
        along = i if ta else k
        if npc > 1:
            along = jnp.clip(along - pc * per, 0, per - 1)
        return (k, along) if ta else (i, along)

    def b_map(o, n, k):
        _, j = ij(o, n, k)
        return (j, k) if tb else (k, j)

    extra = [] if dep is None else [dep]
    return pl.pallas_call(
        body, name=name,
        grid=grid,
        in_specs=[pl.BlockSpec((tk, tm) if ta else (tm, tk), functools.partial(a_map, pc)) for pc in range(npc)]
        + [pl.BlockSpec((tn, tk) if tb else (tk, tn), b_map)] + [ANY] * len(extra),
        out_specs=pl.BlockSpec((tm, tn), lambda o, n, k: ij(o, n, k)),
        out_shape=jax.ShapeDtypeStruct((m_dim, n_dim), F32),
        compiler_params=_cp("parallel", "parallel", "arbitrary"),
    )(*pieces, b, *extra)


def _ada_fwd(c_all, ada_w, ada_b_shard):
    depth, d, n = ada_w.shape
    nb = c_all.shape[0]

    def body(c_ref, w_ref, b_ref, o_ref, ca_ref):
        ca = _silu(c_ref[...])
        ca_ref[...] = _b(ca)
        o_ref[0] = _dg(ca, w_ref[0], _NN) + b_ref[0]

    return pl.pallas_call(
        body, name="ada_fwd",
        grid=(depth,),
        in_specs=[pl.BlockSpec((nb, d), lambda l: (0, 0)),
                  pl.BlockSpec((1, d, n), lambda l: (l, 0, 0)),
                  pl.BlockSpec((1, 1, n), lambda l: (l, 0, 0))],
        out_specs=[pl.BlockSpec((1, nb, n), lambda l: (l, 0, 0)),
                   pl.BlockSpec((nb, d), lambda l: (0, 0))],
        out_shape=[jax.ShapeDtypeStruct((depth, nb, n), F32), jax.ShapeDtypeStruct((nb, d), BF16)],
        compiler_params=_cp("arbitrary"),
    )(c_all, ada_w, ada_b_shard)


def _normmod_f(x, g, sc, sh):
    return _rms(x, g, D_MODEL) * (1.0 + sc) + sh


def _row_tile(seq):
    return min(seq, 256)


def _normmod_fwd(xin, delta, gate, g, sc, sh, *, nseq, name):
    t, d = xin.shape
    seq = t // nseq
    tr = _row_tile(seq)
    nt = seq // tr
    has_delta = delta is not None
    row = pl.BlockSpec((tr, d), lambda s, i: (s * nt + i, 0))
    per_seq = pl.BlockSpec((1, 1, d), lambda s, i: (s, 0, 0))
    vec = pl.BlockSpec((1, d), lambda s, i: (0, 0))

    if has_delta:
        def body(xin_ref, delta_ref, gate_ref, g_ref, sc_ref, sh_ref, x_ref, h_ref):
            x = xin_ref[...] + gate_ref[0] * delta_ref[...]
            x_ref[...] = x
            h_ref[...] = _b(_normmod_f(x, g_ref[...], sc_ref[0], sh_ref[0]))

        return pl.pallas_call(
            body, name=name, grid=(nseq, nt),
            in_specs=[row, row, per_seq, vec, per_seq, per_seq],
            out_specs=[row, row],
            out_shape=[jax.ShapeDtypeStruct((t, d), F32), jax.ShapeDtypeStruct((t, d), BF16)],
            compiler_params=_cp("parallel", "parallel"),
        )(xin, delta, gate, g, sc, sh)

    def body0(xin_ref, g_ref, sc_ref, sh_ref, h_ref):
        h_ref[...] = _b(_normmod_f(xin_ref[...], g_ref[...], sc_ref[0], sh_ref[0]))

    h = pl.pallas_call(
        body0, name=name, grid=(nseq, nt),
        in_specs=[row, vec, per_seq, per_seq],
        out_specs=row,
        out_shape=jax.ShapeDtypeStruct((t, d), BF16),
        compiler_params=_cp("parallel", "parallel"),
    )(xin, g, sc, sh)
    return xin, h


def _normmod_bwd(dh, dxo, x, delta, gate, g, sc, *, nseq, name):
    t, d = x.shape
    seq = t // nseq
    tr = _row_tile(seq)
    nt = seq // tr
    has_delta = delta is not None
    row = pl.BlockSpec((tr, d), lambda s, i: (s * nt + i, 0))
    per_seq = pl.BlockSpec((1, 1, d), lambda s, i: (s, 0, 0))
    vec = pl.BlockSpec((1, d), lambda s, i: (0, 0))

    def core(dh_ref, dxo_ref, x_ref, g_ref, sc_ref, dx_ref, dg_ref, dsc_ref, dsh_ref):
        s, i = pl.program_id(0), pl.program_id(1)
        dh_v = dh_ref[...]
        _, vjp = jax.vjp(lambda xx, gg, ss: _normmod_f(xx, gg, ss, 0.0), x_ref[...], g_ref[...], sc_ref[0])
        dxn, dg_t, dsc_t = vjp(dh_v)
        dx = dxo_ref[...] + dxn
        dx_ref[...] = dx
        dsh_t = jnp.sum(dh_v, axis=0, keepdims=True)

        @pl.when((s == 0) & (i == 0))
        def _():
            dg_ref[...] = jnp.zeros_like(dg_ref)

        @pl.when(i == 0)
        def _():
            dsc_ref[...] = jnp.zeros_like(dsc_ref)
            dsh_ref[...] = jnp.zeros_like(dsh_ref)

        dg_ref[...] += dg_t
        dsc_ref[0] += dsc_t
        dsh_ref[0] += dsh_t
        return dx

    if has_delta:
        def body(dh_ref, dxo_ref, x_ref, delta_ref, gate_ref, g_ref, sc_ref,
                 dx_ref, dd_ref, dgate_ref, dg_ref, dsc_ref, dsh_ref):
            dx = core(dh_ref, dxo_ref, x_ref, g_ref, sc_ref, dx_ref, dg_ref, dsc_ref, dsh_ref)
            dd_ref[...] = _b(dx * gate_ref[0])

            @pl.when(pl.program_id(1) == 0)
            def _():
                dgate_ref[...] = jnp.zeros_like(dgate_ref)

            dgate_ref[0] += jnp.sum(dx * delta_ref[...], axis=0, keepdims=True)

        return pl.pallas_call(
            body, name=name, grid=(nseq, nt),
            in_specs=[row, row, row, row, per_seq, vec, per_seq],
            out_specs=[row, row, per_seq, vec, per_seq, per_seq],
            out_shape=[jax.ShapeDtypeStruct((t, d), F32), jax.ShapeDtypeStruct((t, d), BF16),
                       jax.ShapeDtypeStruct((nseq, 1, d), F32), jax.ShapeDtypeStruct((1, d), F32),
                       jax.ShapeDtypeStruct((nseq, 1, d), F32), jax.ShapeDtypeStruct((nseq, 1, d), F32)],
            compiler_params=_cp("arbitrary", "arbitrary"),
        )(dh, dxo, x, delta, gate, g, sc)

    def body0(dh_ref, dxo_ref, x_ref, g_ref, sc_ref, dx_ref, dg_ref, dsc_ref, dsh_ref):
        core(dh_ref, dxo_ref, x_ref, g_ref, sc_ref, dx_ref, dg_ref, dsc_ref, dsh_ref)

    dx, dg, dsc, dsh = pl.pallas_call(
        body0, name=name, grid=(nseq, nt),
        in_specs=[row, row, row, vec, per_seq],
        out_specs=[row, vec, per_seq, per_seq],
        out_shape=[jax.ShapeDtypeStruct((t, d), F32), jax.ShapeDtypeStruct((1, d), F32),
                   jax.ShapeDtypeStruct((nseq, 1, d), F32), jax.ShapeDtypeStruct((nseq, 1, d), F32)],
        compiler_params=_cp("arbitrary", "arbitrary"),
    )(dh, dxo, x, g, sc)
    return dx, None, None, dg, dsc, dsh


def _final_loss(xin, delta, gate, fg, target, *, nseq):
    t, d = xin.shape
    seq = t // nseq
    tr = _row_tile(seq)
    nt = seq // tr
    row = pl.BlockSpec((tr, d), lambda s, i: (s * nt + i, 0))
    per_seq = pl.BlockSpec((1, 1, d), lambda s, i: (s, 0, 0))
    vec = pl.BlockSpec((1, d), lambda s, i: (0, 0))

    def body(xin_ref, delta_ref, gate_ref, fg_ref, tgt_ref, loss_ref, dx_ref, dd_ref, dgate_ref, dfg_ref):
        s, i = pl.program_id(0), pl.program_id(1)
        dl = delta_ref[...]
        x = xin_ref[...] + gate_ref[0] * dl
        y, vjp = jax.vjp(lambda xx, gg: _rms(xx, gg, D_MODEL), x, fg_ref[...])
        err = y - tgt_ref[...]
        dx, dfg_t = vjp(err * (1.0 / d))
        dx_ref[...] = dx
        dd_ref[...] = _b(dx * gate_ref[0])

        @pl.when((s == 0) & (i == 0))
        def _():
            loss_ref[...] = jnp.zeros_like(loss_ref)
            dfg_ref[...] = jnp.zeros_like(dfg_ref)

        @pl.when(i == 0)
        def _():
            dgate_ref[...] = jnp.zeros_like(dgate_ref)

        loss_ref[...] += jnp.sum(err * err) * (0.5 / d)
        dfg_ref[...] += dfg_t
        dgate_ref[0] += jnp.sum(dx * dl, axis=0, keepdims=True)

    return pl.pallas_call(
        body, name="final_loss", grid=(nseq, nt),
        in_specs=[row, row, per_seq, vec, row],
        out_specs=[pl.BlockSpec((1, 128), lambda s, i: (0, 0)), row, row, per_seq, vec],
        out_shape=[jax.ShapeDtypeStruct((1, 128), F32), jax.ShapeDtypeStruct((t, d), F32),
                   jax.ShapeDtypeStruct((t, d), BF16), jax.ShapeDtypeStruct((nseq, 1, d), F32),
                   jax.ShapeDtypeStruct((1, d), F32)],
        compiler_params=_cp("arbitrary", "arbitrary"),
    )(xin, delta, gate, fg, target)


def _shift_down(x, j):
    if j == 0:
        return x
    rows = lax.broadcasted_iota(jnp.int32, x.shape, 0)
    return jnp.where(rows >= j, pltpu.roll(x, j, 0), 0.0)


def _shift_up(x, j):
    if j == 0:
        return x
    n = x.shape[0]
    rows = lax.broadcasted_iota(jnp.int32, x.shape, 0)
    return jnp.where(rows < n - j, pltpu.roll(x, n - j, 0), 0.0)


def _conv(x, w_ref, b_ref):
    kw = w_ref.shape[0]
    y = b_ref[...] + w_ref[kw - 1:kw, :] * x
    for j in range(1, kw):
        y = y + w_ref[kw - 1 - j:kw - j, :] * _shift_down(x, j)
    return y


def _conv_bwd(dy, x, w_ref, dw_ref, db_ref):
    kw = w_ref.shape[0]
    dx = w_ref[kw - 1:kw, :] * dy
    dw_ref[kw - 1:kw, :] += jnp.sum(dy * x, axis=0, keepdims=True)
    for j in range(1, kw):
        dx = dx + w_ref[kw - 1 - j:kw - j, :] * _shift_up(dy, j)
        dw_ref[kw - 1 - j:kw - j, :] += jnp.sum(dy * _shift_down(x, j), axis=0, keepdims=True)
    db_ref[...] += jnp.sum(dy, axis=0, keepdims=True)
    return dx


CONV_TC = 256


def _ssd_conv_fwd(proj, w, b, *, nseq):
    t = proj.shape[0]
    seq = t // nseq
    nb = CONV_DIM // CONV_TC
    off = COL_XBC // CONV_TC

    def body(x_ref, w_ref, b_ref, o_ref):
        o_ref[...] = _silu(_conv(x_ref[...], w_ref, b_ref))

    return pl.pallas_call(
        body, name="ssd_conv_fwd", grid=(nb, nseq),
        in_specs=[pl.BlockSpec((seq, CONV_TC), lambda j, s: (s, off + j)),
                  pl.BlockSpec((SSD_CONV, CONV_TC), lambda j, s: (0, j)),
                  pl.BlockSpec((1, CONV_TC), lambda j, s: (0, j))],
        out_specs=pl.BlockSpec((seq, CONV_TC), lambda j, s: (s, j)),
        out_shape=jax.ShapeDtypeStruct((t, CONV_DIM), F32),
        compiler_params=_cp("parallel", "parallel"),
    )(proj, w, b)


def _ssd_conv_bwd(dact, proj, w, b, dproj, *, nseq):
    t = proj.shape[0]
    seq = t // nseq
    nb = CONV_DIM // CONV_TC
    off = COL_XBC // CONV_TC

    def body(da_ref, x_ref, w_ref, b_ref, dproj_ref, dx_ref, dw_ref, db_ref):
        del dproj_ref

        @pl.when(pl.program_id(1) == 0)
        def _():
            dw_ref[...] = jnp.zeros_like(dw_ref)
            db_ref[...] = jnp.zeros_like(db_ref)

        x = x_ref[...]
        pre = _conv(x, w_ref, b_ref)
        sg = jax.nn.sigmoid(pre)
        dpre = da_ref[...] * (sg * (1.0 + pre * (1.0 - sg)))
        dx_ref[...] = _b(_conv_bwd(dpre, x, w_ref, dw_ref, db_ref))

    return pl.pallas_call(
        body, name="ssd_conv_bwd", grid=(nb, nseq),
        in_specs=[pl.BlockSpec((seq, CONV_TC), lambda j, s: (s, j)),
                  pl.BlockSpec((seq, CONV_TC), lambda j, s: (s, off + j)),
                  pl.BlockSpec((SSD_CONV, CONV_TC), lambda j, s: (0, j)),
                  pl.BlockSpec((1, CONV_TC), lambda j, s: (0, j)),
                  ANY],
        out_specs=[pl.BlockSpec((seq, CONV_TC), lambda j, s: (s, off + j)),
                   pl.BlockSpec((SSD_CONV, CONV_TC), lambda j, s: (0, j)),
                   pl.BlockSpec((1, CONV_TC), lambda j, s: (0, j))],
        out_shape=[jax.ShapeDtypeStruct(dproj.shape, dproj.dtype), jax.ShapeDtypeStruct((SSD_CONV, CONV_DIM), F32),
                   jax.ShapeDtypeStruct((1, CONV_DIM), F32)],
        input_output_aliases={4: 0},
        compiler_params=_cp("parallel", "arbitrary"),
    )(dact, proj, w, b, dproj)


def _ffn_act_fwd(up, w, b, *, nseq):
    t = up.shape[0]
    seq = t // nseq
    nb = D_FF // CONV_TC

    def body(g_ref, v_ref, w_ref, b_ref, o_ref):
        o_ref[...] = _b(_silu(_conv(g_ref[...], w_ref, b_ref)) * v_ref[...])

    return pl.pallas_call(
        body, name="ffn_act_fwd", grid=(nb, nseq),
        in_specs=[pl.BlockSpec((seq, CONV_TC), lambda j, s: (s, j)),
                  pl.BlockSpec((seq, CONV_TC), lambda j, s: (s, nb + j)),
                  pl.BlockSpec((FF_CONV, CONV_TC), lambda j, s: (0, j)),
                  pl.BlockSpec((1, CONV_TC), lambda j, s: (0, j))],
        out_specs=pl.BlockSpec((seq, CONV_TC), lambda j, s: (s, j)),
        out_shape=jax.ShapeDtypeStruct((t, D_FF), BF16),
        compiler_params=_cp("parallel", "parallel"),
    )(up, up, w, b)


def _ffn_act_bwd(dact, up, w, b, *, nseq):
    t = up.shape[0]
    seq = t // nseq
    nb = D_FF // CONV_TC

    def body(da_ref, g_ref, v_ref, w_ref, b_ref, dg_ref, dv_ref, dw_ref, db_ref):
        @pl.when(pl.program_id(1) == 0)
        def _():
            dw_ref[...] = jnp.zeros_like(dw_ref)
            db_ref[...] = jnp.zeros_like(db_ref)

        gate = g_ref[...]
        pre = _conv(gate, w_ref, b_ref)
        sg = jax.nn.sigmoid(pre)
        da = da_ref[...]
        dv_ref[...] = _b(da * (pre * sg))
        dpre = da * v_ref[...] * (sg * (1.0 + pre * (1.0 - sg)))
        dg_ref[...] = _b(_conv_bwd(dpre, gate, w_ref, dw_ref, db_ref))

    col = pl.BlockSpec((seq, CONV_TC), lambda j, s: (s, j))
    return pl.pallas_call(
        body, name="ffn_act_bwd", grid=(nb, nseq),
        in_specs=[col, col,
                  pl.BlockSpec((seq, CONV_TC), lambda j, s: (s, nb + j)),
                  pl.BlockSpec((FF_CONV, CONV_TC), lambda j, s: (0, j)),
                  pl.BlockSpec((1, CONV_TC), lambda j, s: (0, j))],
        out_specs=[col, col,
                   pl.BlockSpec((FF_CONV, CONV_TC), lambda j, s: (0, j)),
                   pl.BlockSpec((1, CONV_TC), lambda j, s: (0, j))],
        out_shape=[jax.ShapeDtypeStruct((t, D_FF), BF16), jax.ShapeDtypeStruct((t, D_FF), BF16),
                   jax.ShapeDtypeStruct((FF_CONV, D_FF), F32), jax.ShapeDtypeStruct((1, D_FF), F32)],
        compiler_params=_cp("parallel", "arbitrary"),
    )(dact, up, up, w, b)


SSD_PAIRS = SSD_HEADS // 2
PAIR_W = 2 * SSD_HEAD_DIM
PAIRS_PER_GROUP = SSD_PAIRS // SSD_GROUPS


def _ssd_chunk(xs, bg, cg, dtr, z, hp, dtb, alog, dskip, ng):
    n = dtr.shape[0]
    dt = _softplus(dtr + dtb)
    cs = _cumsum_rows(dt * (-jnp.exp(alog)))
    cs_t = _transpose(cs)
    lane = lax.broadcasted_iota(jnp.int32, (1, SSD_HEADS), 1)
    sub = lax.broadcasted_iota(jnp.int32, (SSD_HEADS, 1), 0)
    row = lax.broadcasted_iota(jnp.int32, (n, 1), 0)
    causal = lax.broadcasted_iota(jnp.int32, (n, n), 0) >= lax.broadcasted_iota(jnp.int32, (n, n), 1)
    first = lax.broadcasted_iota(jnp.int32, (1, PAIR_W), 1) < SSD_HEAD_DIM
    first_rows = lax.broadcasted_iota(jnp.int32, (PAIR_W, 1), 0) < SSD_HEAD_DIM
    first_f = first.astype(F32)
    cb = [_bdot_nt(cg[g], bg[g]) for g in range(SSD_GROUPS)]
    ys, hn = [], []
    for p in range(SSD_PAIRS):
        g = p // PAIRS_PER_GROUP
        col, decay, last = [], [], []
        for h in (2 * p, 2 * p + 1):
            oh = (lane == h).astype(F32)
            cs_h = jnp.sum(cs * oh, axis=1, keepdims=True)
            cs_row = jnp.sum(cs_t * (sub == h).astype(F32), axis=0, keepdims=True)
            col.append((jnp.sum(dt * oh, axis=1, keepdims=True), cs_h, jnp.sum(dskip * oh, axis=1, keepdims=True)))
            last.append(jnp.sum(jnp.where(row == n - 1, cs_h, 0.0), axis=0, keepdims=True))
            decay.append(jnp.where(causal, jnp.exp(jnp.where(causal, cs_h - cs_row, 0.0)), 0.0))
        pair = lambda a, b: jnp.where(first, a, b)
        dt_p = pair(col[0][0], col[1][0])
        cs_p = pair(col[0][1], col[1][1])
        last_p = pair(last[0], last[1])
        xc = xs[p] * dt_p
        y = _bdot(cb[g] * decay[0], xc * first_f) + _bdot(cb[g] * decay[1], xc * (1.0 - first_f))
        y = y + _bdot_nt(cg[g], hp[p]) * jnp.exp(cs_p)
        y = y + pair(col[0][2], col[1][2]) * xs[p]
        keep = jnp.where(first_rows, jnp.exp(last[0]), jnp.exp(last[1]))
        hn.append(keep * hp[p] + _bdot_tn(xc * jnp.exp(last_p - cs_p), bg[g]))
        ys.append(y * _silu(z[p]))
    outs = []
    for g in range(SSD_GROUPS):
        ps = range(g * PAIRS_PER_GROUP, (g + 1) * PAIRS_PER_GROUP)
        ms = sum(jnp.sum(ys[p] * ys[p], axis=1, keepdims=True) for p in ps) * (1.0 / GROUP_WIDTH)
        r = lax.rsqrt(ms + EPS)
        outs += [ys[p] * r * ng[p] for p in ps]
    return outs, hn


def _hslices(ref, width, count, base=0):
    return [ref[:, base + k * width: base + (k + 1) * width] for k in range(count)]


def _ssd_load(xbc_ref, z_ref, dt_ref, ng_ref):
    xs = _hslices(xbc_ref, PAIR_W, SSD_PAIRS)
    bg = _hslices(xbc_ref, D_STATE, SSD_GROUPS, D_SSD)
    cg = _hslices(xbc_ref, D_STATE, SSD_GROUPS, D_SSD + SSD_GROUPS * D_STATE)
    z = _hslices(z_ref, PAIR_W, SSD_PAIRS)
    ng = _hslices(ng_ref, PAIR_W, SSD_PAIRS)
    return xs, bg, cg, dt_ref[:, 0:SSD_HEADS], z, ng


def _ssd_specs(nch):
    rowi = lambda s, c: s * nch + c
    return [pl.BlockSpec((CHUNK, CONV_DIM), lambda s, c: (rowi(s, c), 0)),
            pl.BlockSpec((CHUNK, D_SSD), lambda s, c: (rowi(s, c), COL_Z // D_SSD)),
            pl.BlockSpec((CHUNK, 128), lambda s, c: (rowi(s, c), COL_DT // 128)),
            pl.BlockSpec((1, SSD_HEADS), lambda s, c: (0, 0)),
            pl.BlockSpec((1, SSD_HEADS), lambda s, c: (0, 0)),
            pl.BlockSpec((1, SSD_HEADS), lambda s, c: (0, 0)),
            pl.BlockSpec((1, D_SSD), lambda s, c: (0, 0))]


def _ssd_fwd(xbc, proj, dtb, alog, dskip, ng, *, nseq):
    t = proj.shape[0]
    nch = t // nseq // CHUNK
    hd = PAIR_W

    def body(xbc_ref, z_ref, dt_ref, dtb_ref, alog_ref, dsk_ref, ng_ref, y_ref, hp_ref, h_ref):
        @pl.when(pl.program_id(1) == 0)
        def _():
            h_ref[...] = jnp.zeros_like(h_ref)

        xs, bg, cg, dtr, z, ngs = _ssd_load(xbc_ref, z_ref, dt_ref, ng_ref)
        hp_ref[0] = h_ref[...]
        hp = [h_ref[h * hd:(h + 1) * hd, :] for h in range(SSD_PAIRS)]
        outs, hn = _ssd_chunk(xs, bg, cg, dtr, z, hp, dtb_ref[...], alog_ref[...], dsk_ref[...], ngs)
        for h in range(SSD_PAIRS):
            y_ref[:, h * hd:(h + 1) * hd] = _b(outs[h])
            h_ref[h * hd:(h + 1) * hd, :] = hn[h]

    return pl.pallas_call(
        body, name="ssd_fwd", grid=(nseq, nch),
        in_specs=_ssd_specs(nch),
        out_specs=[pl.BlockSpec((CHUNK, D_SSD), lambda s, c: (s * nch + c, 0)),
                   pl.BlockSpec((1, D_SSD, D_STATE), lambda s, c: (s * nch + c, 0, 0))],
        out_shape=[jax.ShapeDtypeStruct((t, D_SSD + D_GM), BF16),
                   jax.ShapeDtypeStruct((t // CHUNK, D_SSD, D_STATE), F32)],
        scratch_shapes=[pltpu.VMEM((D_SSD, D_STATE), F32)],
        compiler_params=_cp("arbitrary", "arbitrary"),
    )(xbc, proj, proj, dtb, alog, dskip, ng)


def _ssd_bwd(dy, xbc, proj, hprev, dtb, alog, dskip, ng, *, nseq):
    t = proj.shape[0]
    nch = t // nseq // CHUNK
    hd = PAIR_W
    rev = lambda s, c: s * nch + (nch - 1 - c)

    def body(dy_ref, xbc_ref, z_ref, dt_ref, hp_ref, dtb_ref, alog_ref, dsk_ref, ng_ref,
             dxbc_ref, dproj_ref, ddtb_ref, dalog_ref, ddsk_ref, dng_ref, dh_ref):
        first = (pl.program_id(0) == 0) & (pl.program_id(1) == 0)

        @pl.when(pl.program_id(1) == 0)
        def _():
            dh_ref[...] = jnp.zeros_like(dh_ref)

        @pl.when(first)
        def _():
            ddtb_ref[...] = jnp.zeros_like(ddtb_ref)
            dalog_ref[...] = jnp.zeros_like(dalog_ref)
            ddsk_ref[...] = jnp.zeros_like(ddsk_ref)
            dng_ref[...] = jnp.zeros_like(dng_ref)

        xs, bg, cg, dtr, z, ngs = _ssd_load(xbc_ref, z_ref, dt_ref, ng_ref)
        hp = [hp_ref[0, h * hd:(h + 1) * hd, :] for h in range(SSD_PAIRS)]
        _, vjp = jax.vjp(_ssd_chunk, xs, bg, cg, dtr, z, hp, dtb_ref[...], alog_ref[...], dsk_ref[...], ngs)
        douts = [dy_ref[:, h * hd:(h + 1) * hd] for h in range(SSD_PAIRS)]
        dhn = [dh_ref[h * hd:(h + 1) * hd, :] for h in range(SSD_PAIRS)]
        dxs, dbg, dcg, ddtr, dz, dhp, ddtb, dalog, ddsk, dngs = vjp((douts, dhn))
        dproj_ref[:, :COL_Z] = jnp.zeros((CHUNK, COL_Z), BF16)
        dproj_ref[:, COL_XBC:] = jnp.zeros((CHUNK, N_INP - COL_XBC), BF16)
        for h in range(SSD_PAIRS):
            dxbc_ref[:, h * hd:(h + 1) * hd] = dxs[h]
            dproj_ref[:, COL_Z + h * hd: COL_Z + (h + 1) * hd] = _b(dz[h])
            dh_ref[h * hd:(h + 1) * hd, :] = dhp[h]
            dng_ref[:, h * hd:(h + 1) * hd] += dngs[h]
        for g in range(SSD_GROUPS):
            dxbc_ref[:, D_SSD + g * D_STATE: D_SSD + (g + 1) * D_STATE] = dbg[g]
            dxbc_ref[:, D_SSD + (SSD_GROUPS + g) * D_STATE: D_SSD + (SSD_GROUPS + g + 1) * D_STATE] = dcg[g]
        dproj_ref[:, COL_DT:COL_DT + SSD_HEADS] = _b(ddtr)
        ddtb_ref[...] += ddtb
        dalog_ref[...] += dalog
        ddsk_ref[...] += ddsk

    small = pl.BlockSpec((1, SSD_HEADS), lambda s, c: (0, 0))
    return pl.pallas_call(
        body, name="ssd_bwd", grid=(nseq, nch),
        in_specs=[pl.BlockSpec((CHUNK, D_SSD), lambda s, c: (rev(s, c), 0)),
                  pl.BlockSpec((CHUNK, CONV_DIM), lambda s, c: (rev(s, c), 0)),
                  pl.BlockSpec((CHUNK, D_SSD), lambda s, c: (rev(s, c), COL_Z // D_SSD)),
                  pl.BlockSpec((CHUNK, 128), lambda s, c: (rev(s, c), COL_DT // 128)),
                  pl.BlockSpec((1, D_SSD, D_STATE), lambda s, c: (rev(s, c), 0, 0)),
                  small, small, small,
                  pl.BlockSpec((1, D_SSD), lambda s, c: (0, 0))],
        out_specs=[pl.BlockSpec((CHUNK, CONV_DIM), lambda s, c: (rev(s, c), 0)),
                   pl.BlockSpec((CHUNK, N_INP), lambda s, c: (rev(s, c), 0)),
                   small, small, small,
                   pl.BlockSpec((1, D_SSD), lambda s, c: (0, 0))],
        out_shape=[jax.ShapeDtypeStruct((t, CONV_DIM), F32), jax.ShapeDtypeStruct((t, N_INP), BF16),
                   jax.ShapeDtypeStruct((1, SSD_HEADS), F32), jax.ShapeDtypeStruct((1, SSD_HEADS), F32),
                   jax.ShapeDtypeStruct((1, SSD_HEADS), F32), jax.ShapeDtypeStruct((1, D_SSD), F32)],
        scratch_shapes=[pltpu.VMEM((D_SSD, D_STATE), F32)],
        compiler_params=_cp("arbitrary", "arbitrary"),
    )(dy, xbc, proj, proj, hprev, dtb, alog, dskip, ng)


def _gmlp_chunk(gu, gv, ws, bs_cols, vg, og):
    n = gu[0].shape[0]
    mask = _tri(n, True)
    au = [_gelu(t) for t in gu]
    av = [_gelu(t) for t in gv]
    r = lax.rsqrt(sum(jnp.sum(t * t, axis=1, keepdims=True) for t in av) * (1.0 / D_GM) + EPS)
    p = []
    for h in range(GM_HEADS):
        sv = _bdot(ws[h] * mask, av[h] * r * vg[h]) + bs_cols[h]
        p.append(au[h] * sv)
    r2 = lax.rsqrt(sum(jnp.sum(t * t, axis=1, keepdims=True) for t in p) * (1.0 / D_GM) + EPS)
    return [p[h] * r2 * og[h] for h in range(GM_HEADS)]


def _gmlp_load(u_ref, v_ref, ws_ref, bst_ref, vg_ref, og_ref):
    gu = _hslices(u_ref, GM_HEAD_DIM, GM_HEADS)
    gv = _hslices(v_ref, GM_HEAD_DIM, GM_HEADS)
    ws = [ws_ref[h] for h in range(GM_HEADS)]
    bs_cols = [bst_ref[:, h:h + 1] for h in range(GM_HEADS)]
    return gu, gv, ws, bs_cols, _hslices(vg_ref, GM_HEAD_DIM, GM_HEADS), _hslices(og_ref, GM_HEAD_DIM, GM_HEADS)


def _gmlp_specs():
    return [pl.BlockSpec((CHUNK, D_GM), lambda i: (i, COL_U // D_GM)),
            pl.BlockSpec((CHUNK, D_GM), lambda i: (i, COL_V // D_GM)),
            pl.BlockSpec((GM_HEADS, CHUNK, CHUNK), lambda i: (0, 0, 0)),
            pl.BlockSpec((CHUNK, GM_HEADS), lambda i: (0, 0)),
            pl.BlockSpec((1, D_GM), lambda i: (0, 0)),
            pl.BlockSpec((1, D_GM), lambda i: (0, 0))]


def _gmlp_fwd(proj, ycat, ws, bst, vg, og):
    t = proj.shape[0]

    def body(u_ref, v_ref, ws_ref, bst_ref, vg_ref, og_ref, ycat_ref, o_ref):
        del ycat_ref
        outs = _gmlp_chunk(*_gmlp_load(u_ref, v_ref, ws_ref, bst_ref, vg_ref, og_ref))
        for h in range(GM_HEADS):
            o_ref[:, h * GM_HEAD_DIM:(h + 1) * GM_HEAD_DIM] = _b(outs[h])

    return pl.pallas_call(
        body, name="gmlp_fwd", grid=(t // CHUNK,),
        in_specs=_gmlp_specs() + [ANY],
        out_specs=pl.BlockSpec((CHUNK, D_GM), lambda i: (i, D_SSD // D_GM)),
        out_shape=jax.ShapeDtypeStruct(ycat.shape, ycat.dtype),
        input_output_aliases={6: 0},
        compiler_params=_cp("parallel"),
    )(proj, proj, ws, bst, vg, og, ycat)


def _gmlp_bwd(dy, proj, ws, bst, vg, og, dproj):
    t = proj.shape[0]
    w = GM_HEAD_DIM

    def body(dy_ref, u_ref, v_ref, ws_ref, bst_ref, vg_ref, og_ref, dproj_ref,
             dgm_ref, dws_ref, dbst_ref, dvg_ref, dog_ref):
        del dproj_ref

        @pl.when(pl.program_id(0) == 0)
        def _():
            dws_ref[...] = jnp.zeros_like(dws_ref)
            dbst_ref[...] = jnp.zeros_like(dbst_ref)
            dvg_ref[...] = jnp.zeros_like(dvg_ref)
            dog_ref[...] = jnp.zeros_like(dog_ref)

        _, vjp = jax.vjp(_gmlp_chunk, *_gmlp_load(u_ref, v_ref, ws_ref, bst_ref, vg_ref, og_ref))
        dgu, dgv, dws, dbs, dvg, dog = vjp(_hslices(dy_ref, w, GM_HEADS))
        for h in range(GM_HEADS):
            dgm_ref[:, h * w:(h + 1) * w] = _b(dgu[h])
            dgm_ref[:, D_GM + h * w: D_GM + (h + 1) * w] = _b(dgv[h])
            dws_ref[h] += dws[h]
            dbst_ref[:, h:h + 1] += dbs[h]
            dvg_ref[:, h * w:(h + 1) * w] += dvg[h]
            dog_ref[:, h * w:(h + 1) * w] += dog[h]

    return pl.pallas_call(
        body, name="gmlp_bwd", grid=(t // CHUNK,),
        in_specs=[pl.BlockSpec((CHUNK, D_GM), lambda i: (i, 1))] + _gmlp_specs() + [ANY],
        out_specs=[pl.BlockSpec((CHUNK, 2 * D_GM), lambda i: (i, COL_U // (2 * D_GM))),
                   pl.BlockSpec((GM_HEADS, CHUNK, CHUNK), lambda i: (0, 0, 0)),
                   pl.BlockSpec((CHUNK, GM_HEADS), lambda i: (0, 0)),
                   pl.BlockSpec((1, D_GM), lambda i: (0, 0)),
                   pl.BlockSpec((1, D_GM), lambda i: (0, 0))],
        out_shape=[jax.ShapeDtypeStruct(dproj.shape, dproj.dtype), jax.ShapeDtypeStruct((GM_HEADS, CHUNK, CHUNK), F32),
                   jax.ShapeDtypeStruct((CHUNK, GM_HEADS), F32), jax.ShapeDtypeStruct((1, D_GM), F32),
                   jax.ShapeDtypeStruct((1, D_GM), F32)],
        input_output_aliases={7: 0},
        compiler_params=_cp("arbitrary"),
    )(dy, proj, proj, ws, bst, vg, og, dproj)


def _local_step(x, target, mods, lw, final_g, *, nseq, big_w, grad_sink):
    saved = []
    xin, delta, gate = x, None, None
    for l in range(DEPTH):
        w = lw[l]
        sh1, sc1, g1, sh2, sc2, g2 = mods[l]
        w_in = big_w(l, "w_in", xin if delta is None else delta)
        x0, h1 = _normmod_fwd(xin, delta, gate, w["norm1_g"], sc1, sh1, nseq=nseq, name=f"norm1_fwd_{l}")
        proj = _matmul(h1, w_in, tb=True, name=f"mm_in_{l}")
        xbc = _ssd_conv_fwd(proj, w["ssd_conv_w"], w["ssd_conv_b"], nseq=nseq)
        ycat, hprev = _ssd_fwd(xbc, proj, w["ssd_dt_bias"], w["ssd_a_log"], w["ssd_d"], w["ssd_norm_g"], nseq=nseq)
        ycat = _gmlp_fwd(proj, ycat, w["gm_ws"], w["gm_bst"], w["gm_vnorm_g"], w["gm_out_g"])
        w_out = big_w(l, "w_out", ycat)
        mix = _matmul(ycat, w_out, name=f"mm_out_{l}")
        x1, h2 = _normmod_fwd(x0, mix, g1, w["norm2_g"], sc2, sh2, nseq=nseq, name=f"norm2_fwd_{l}")
        ff_up = big_w(l, "ff_up", h2)
        up = _matmul(h2, ff_up, tb=True, name=f"mm_up_{l}")
        act = _ffn_act_fwd(up, w["ff_conv_w"], w["ff_conv_b"], nseq=nseq)
        ff_down = big_w(l, "ff_down", act)
        dn = _matmul(act, ff_down, name=f"mm_down_{l}")
        saved.append(dict(x0=x0, xin_delta=delta, xin_gate=gate, h1=h1, proj=proj, xbc=xbc, hprev=hprev, ycat=ycat,
                          mix=mix, x1=x1, h2=h2, up=up, act=act, dn=dn,
                          w_in=w_in, w_out=w_out, ff_up=ff_up, ff_down=ff_down))
        xin, delta, gate = x1, dn, g2

    loss, dx, ddelta, dgate, dfg = _final_loss(xin, delta, gate, final_g, target, nseq=nseq)

    small, dmods = [None] * DEPTH, [None] * DEPTH
    for l in reversed(range(DEPTH)):
        w, sv = lw[l], saved[l]
        sh1, sc1, g1, sh2, sc2, g2 = mods[l]
        dg2 = dgate
        g_ff_down = _matmul(sv["act"], ddelta, ta=True, name=f"mm_down_dw_{l}")
        dact = _matmul(ddelta, sv["ff_down"], tb=True, name=f"mm_down_dx_{l}")
        dgate_ff, dval_ff, dfcw, dfcb = _ffn_act_bwd(dact, sv["up"], w["ff_conv_w"], w["ff_conv_b"], nseq=nseq)
        g_ff_up = _matmul([dgate_ff, dval_ff], sv["h2"], ta=True, name=f"mm_up_dw_{l}")
        dep = grad_sink(l, "ffn", dict(ff_down=g_ff_down, ff_up=g_ff_up), dval_ff)
        dh2 = _matmul([dgate_ff, dval_ff], sv["ff_up"], name=f"mm_up_dx_{l}", dep=dep)
        dx, dmix, dg1, dn2g, dsc2, dsh2 = _normmod_bwd(dh2, dx, sv["x1"], sv["mix"], g1, w["norm2_g"], sc2,
                                                       nseq=nseq, name=f"norm2_bwd_{l}")
        g_w_out = _matmul(sv["ycat"], dmix, ta=True, name=f"mm_out_dw_{l}")
        dep = grad_sink(l, "w_out", dict(w_out=g_w_out), dmix)
        dycat = _matmul(dmix, sv["w_out"], tb=True, name=f"mm_out_dx_{l}", dep=dep)
        dxbc_act, dproj, ddtb, dalog, ddsk, dng = _ssd_bwd(dycat, sv["xbc"], sv["proj"], sv["hprev"], w["ssd_dt_bias"],
                                                          w["ssd_a_log"], w["ssd_d"], w["ssd_norm_g"], nseq=nseq)
        dproj, dscw, dscb = _ssd_conv_bwd(dxbc_act, sv["proj"], w["ssd_conv_w"], w["ssd_conv_b"], dproj, nseq=nseq)
        dproj, dws, dbst, dvg, dog = _gmlp_bwd(dycat, sv["proj"], w["gm_ws"], w["gm_bst"], w["gm_vnorm_g"], w["gm_out_g"], dproj)
        g_w_in = _matmul(dproj, sv["h1"], ta=True, name=f"mm_in_dw_{l}")
        dep = grad_sink(l, "w_in", dict(w_in=g_w_in), dproj)
        dh1 = _matmul(dproj, sv["w_in"], name=f"mm_in_dx_{l}", dep=dep)
        dx, ddelta, dgate, dn1g, dsc1, dsh1 = _normmod_bwd(dh1, dx, sv["x0"], sv["xin_delta"], sv["xin_gate"],
                                                           w["norm1_g"], sc1, nseq=nseq, name=f"norm1_bwd_{l}")
        small[l] = dict(norm1_g=dn1g, norm2_g=dn2g, ssd_norm_g=dng, gm_vnorm_g=dvg, gm_out_g=dog,
                        ssd_conv_w=dscw, ssd_conv_b=dscb, ff_conv_w=dfcw, ff_conv_b=dfcb,
                        ssd_dt_bias=ddtb, ssd_a_log=dalog, ssd_d=ddsk, gm_ws=dws, gm_bs=dbst.T)
        dmods[l] = jnp.concatenate([dsh1, dsc1, dg1, dsh2, dsc2, dg2], axis=-1)[:, 0, :]
    return loss, dx, small, dmods, dfg


def _all_gather(arrs, name, dep=None):
    n = len(arrs)
    extra = [] if dep is None else [dep]

    def body(*refs):
        ins, outs = refs[:n], refs[n + len(extra):2 * n + len(extra)]
        send_sems, recv_sems, local_sems = refs[2 * n + len(extra):]
        x, y, c = lax.axis_index("x"), lax.axis_index("y"), lax.axis_index("c")
        me, sibling = (x, y, c), (x, y, 1 - c)
        chips = [(1 - x, y), (x, 1 - y), (1 - x, 1 - y)]

        def copy(i, k, block, to, src=None):
            px, py, pc = block
            dst = outs[i].at[4 * px + 2 * py + pc]
            return pltpu.make_async_remote_copy(
                src_ref=dst if src is None else src, dst_ref=dst,
                send_sem=send_sems.at[7 * i + k], recv_sem=recv_sems.at[7 * i + k],
                device_id=to, device_id_type=MESH)

        mine = [pltpu.make_async_copy(ins[i], outs[i].at[4 * x + 2 * y + c], local_sems.at[i]) for i in range(n)]
        for cp in mine:
            cp.start()
        first = []
        for i in range(n):
            first.append(copy(i, 0, me, sibling, src=ins[i]))
            first += [copy(i, 1 + j, me, (*chip, c), src=ins[i]) for j, chip in enumerate(chips)]
        for cp in first:
            cp.start()
        passed = []
        for j, chip in enumerate(chips):
            for i in range(n):
                copy(i, 1 + j, (*chip, c), me).wait_recv()
                fwd = copy(i, 4 + j, (*chip, c), sibling)
                fwd.start()
                passed.append(fwd)
        for i in range(n):
            copy(i, 0, sibling, me).wait_recv()
            for j, chip in enumerate(chips):
                copy(i, 4 + j, (*chip, 1 - c), me).wait_recv()
        for cp in first + passed:
            cp.wait_send()
        for cp in mine:
            cp.wait()

    return pl.pallas_call(
        body, name=name,
        in_specs=[ANY] * (n + len(extra)), out_specs=[ANY] * n,
        out_shape=[jax.ShapeDtypeStruct((N_DEV,) + a.shape, a.dtype) for a in arrs],
        scratch_shapes=[pltpu.SemaphoreType.DMA((7 * n,)), pltpu.SemaphoreType.DMA((7 * n,)),
                        pltpu.SemaphoreType.DMA((n,))],
    )(*arrs, *extra)


def _exchange_sibling(arrs, name):
    n = len(arrs)

    def body(*refs):
        ins, outs = refs[:n], refs[n:2 * n]
        send_sems, recv_sems = refs[2 * n:]
        x, y, c = lax.axis_index("x"), lax.axis_index("y"), lax.axis_index("c")
        copies = []
        for i in range(n):
            for k in range(4):
                copies.append(pltpu.make_async_remote_copy(
                    src_ref=ins[i].at[2 * k + (1 - c)], dst_ref=outs[i].at[k],
                    send_sem=send_sems.at[4 * i + k], recv_sem=recv_sems.at[4 * i + k],
                    device_id=(x, y, 1 - c), device_id_type=MESH))
        for cp in copies:
            cp.start()
        for cp in copies:
            cp.wait_recv()
        for cp in copies:
            cp.wait_send()

    return pl.pallas_call(
        body, name=name,
        in_specs=[ANY] * n, out_specs=[ANY] * n,
        out_shape=[jax.ShapeDtypeStruct((4,) + a.shape[1:], a.dtype) for a in arrs],
        scratch_shapes=[pltpu.SemaphoreType.DMA((4 * n,)), pltpu.SemaphoreType.DMA((4 * n,))],
    )(*arrs)


def _exchange_chips(arrs, name):
    n = len(arrs)

    def body(*refs):
        ins, outs = refs[:n], refs[n:2 * n]
        send_sems, recv_sems = refs[2 * n:]
        x, y, c = lax.axis_index("x"), lax.axis_index("y"), lax.axis_index("c")
        chips = [(1 - x, y), (x, 1 - y), (1 - x, 1 - y)]
        copies = []
        for i in range(n):
            for j, (cx, cy) in enumerate(chips):
                copies.append(pltpu.make_async_remote_copy(
                    src_ref=ins[i].at[2 * cx + cy], dst_ref=outs[i].at[j],
                    send_sem=send_sems.at[3 * i + j], recv_sem=recv_sems.at[3 * i + j],
                    device_id=(cx, cy, c), device_id_type=MESH))
        for cp in copies:
            cp.start()
        for cp in copies:
            cp.wait_recv()
        for cp in copies:
            cp.wait_send()

    return pl.pallas_call(
        body, name=name,
        in_specs=[ANY] * n, out_specs=[ANY] * n,
        out_shape=[jax.ShapeDtypeStruct((3,) + a.shape[1:], a.dtype) for a in arrs],
        scratch_shapes=[pltpu.SemaphoreType.DMA((3 * n,)), pltpu.SemaphoreType.DMA((3 * n,))],
    )(*arrs)


def _add_sibling(a, r, pos, name):
    _, depth, rows, cols = a.shape
    tr = _tile(rows, 256) if rows % 8 == 0 else rows
    a3 = a.reshape(N_DEV * depth, rows, cols)
    r3 = r.reshape(4 * depth, rows, cols)

    def body(pos_ref, a_ref, r_ref, o_ref):
        o_ref[...] = a_ref[...] + r_ref[...]

    out = pl.pallas_call(
        body, name=name,
        grid_spec=pltpu.PrefetchScalarGridSpec(
            num_scalar_prefetch=1, grid=(4 * depth, rows // tr),
            in_specs=[pl.BlockSpec((1, tr, cols), lambda q, i, p: ((2 * (q // depth) + p[0]) * depth + q % depth, i, 0)),
                      pl.BlockSpec((1, tr, cols), lambda q, i, p: (q, i, 0))],
            out_specs=pl.BlockSpec((1, tr, cols), lambda q, i, p: (q, i, 0))),
        out_shape=jax.ShapeDtypeStruct((4 * depth, rows, cols), F32),
        compiler_params=_cp("parallel", "parallel"),
    )(pos, a3, r3)
    return out.reshape(4, depth, rows, cols)


HBM = pl.BlockSpec(memory_space=pltpu.HBM)
SEM = pl.BlockSpec(memory_space=pltpu.SEMAPHORE)
EFFECT = pltpu.SideEffectType.DATAFLOW_SIDE_EFFECTING


def _peer(k):
    x, y, c = lax.axis_index("x"), lax.axis_index("y"), lax.axis_index("c")
    return (1 - x if k & 4 else x, 1 - y if k & 2 else y, 1 - c if k & 1 else c)


def _xc_copies(scatter, srcs, lands, send_sems, recv_sems):
    x, y, c = lax.axis_index("x"), lax.axis_index("y"), lax.axis_index("c")
    copies = []
    for i in range(len(srcs)):
        for k in range(1, N_DEV):
            px, py, pc = _peer(k)
            src = srcs[i].at[4 * px + 2 * py + pc] if scatter else srcs[i]
            dst = lands[i].at[k - 1] if scatter else lands[i].at[4 * x + 2 * y + c]
            copies.append(pltpu.make_async_remote_copy(
                src_ref=src, dst_ref=dst, send_sem=send_sems[i].at[k - 1], recv_sem=recv_sems[i].at[k - 1],
                device_id=(px, py, pc), device_id_type=MESH))
    return copies


def _xc_start(scatter, arrs, after, name):
    n = len(arrs)
    lands = [lax.empty((N_DEV - 1,) + a.shape[1:] if scatter else (N_DEV,) + a.shape, a.dtype) for a in arrs]

    def body(*refs):
        srcs, lnd = refs[:n], refs[n:2 * n]
        send_sems, recv_sems = refs[2 * n + 1:3 * n + 1], refs[3 * n + 1:4 * n + 1]
        token = refs[6 * n + 1]
        for cp in _xc_copies(scatter, srcs, lnd, send_sems, recv_sems):
            cp.start()
        token[...] = jnp.zeros_like(token)

    outs = pl.pallas_call(
        body, name=name,
        out_shape=[pltpu.SemaphoreType.DMA((N_DEV - 1,))] * (2 * n)
        + [pltpu.HBM(a.shape, a.dtype) for a in arrs] + [pltpu.HBM(a.shape, a.dtype) for a in lands]
        + [jax.ShapeDtypeStruct((8, 128), F32)],
        in_specs=[HBM] * (2 * n) + [ANY],
        out_specs=[SEM] * (2 * n) + [HBM] * (2 * n) + [pl.BlockSpec(memory_space=pltpu.VMEM)],
        input_output_aliases={i: 2 * n + i for i in range(2 * n)},
        compiler_params=pltpu.CompilerParams(has_side_effects=EFFECT),
    )(*[pltpu.with_memory_space_constraint(a, pltpu.HBM) for a in list(arrs) + lands], after)
    return outs[:n], outs[n:2 * n], outs[2 * n:3 * n], outs[3 * n:4 * n], outs[4 * n][0, 0]


def _xc_wait(scatter, send_sems, recv_sems, srcs, lands, after, name):
    n = len(srcs)

    def body(*refs):
        s_refs, l_refs = refs[:n], refs[n:2 * n]
        ss, rs = refs[2 * n:3 * n], refs[3 * n:4 * n]
        for cp in _xc_copies(scatter, s_refs, l_refs, ss, rs):
            cp.wait_send()
            cp.wait_recv()

    outs = pl.pallas_call(
        body, name=name,
        out_shape=[pltpu.HBM(a.shape, a.dtype) for a in list(srcs) + list(lands)],
        in_specs=[HBM] * (2 * n) + [SEM] * (2 * n) + [ANY],
        out_specs=[HBM] * (2 * n),
        input_output_aliases={i: i for i in range(2 * n)},
        compiler_params=pltpu.CompilerParams(has_side_effects=EFFECT),
    )(*srcs, *lands, *send_sems, *recv_sems, after)
    return outs[:n], outs[n:]


def _adamw_math(w, g, m, v):
    m = ADAM_B1 * m + (1.0 - ADAM_B1) * g
    v = ADAM_B2 * v + (1.0 - ADAM_B2) * (g * g)
    m_hat = m / (1.0 - ADAM_B1 ** ADAM_STEP)
    v_hat = v / (1.0 - ADAM_B2 ** ADAM_STEP)
    delta = -ADAM_LR * (m_hat / (jnp.sqrt(v_hat) + ADAM_EPS) + ADAM_WD * w)
    return delta, m, v


def _adamw_sharded(parts, w, m, v, pos, name):
    depth, rows, cols = w.shape
    tr = _tile(rows, 256) if rows % 8 == 0 else rows
    npart = len(parts)

    def body(pos_ref, *refs):
        prefs = refs[:npart]
        w_ref, m_ref, v_ref, g_out, d_out, m_out, v_out = refs[npart:]
        g = prefs[0][...]
        for pr in prefs[1:]:
            g = g + pr[...]
        delta, mn, vn = _adamw_math(w_ref[...], g, m_ref[...], v_ref[...])
        g_out[...] = g
        d_out[...] = delta
        m_out[...] = mn
        v_out[...] = vn

    def part_spec(fn):
        return pl.BlockSpec((1, tr, cols), lambda l, i, p: (fn(p) * depth + l, i, 0))

    blk = pl.BlockSpec((1, tr, cols), lambda l, i, p: (l, i, 0))
    shp = jax.ShapeDtypeStruct((depth, rows, cols), F32)
    return pl.pallas_call(
        body, name=name,
        grid_spec=pltpu.PrefetchScalarGridSpec(
            num_scalar_prefetch=1, grid=(depth, rows // tr),
            in_specs=[part_spec(fn) for _, fn in parts] + [blk, blk, blk],
            out_specs=[blk, blk, blk, blk]),
        out_shape=[shp, shp, shp, shp],
        compiler_params=_cp("parallel", "parallel"),
    )(pos, *[a for a, _ in parts], w, m, v)


def _adamw_layer(parts, w, m, v, pos, layer, prev, name):
    depth, rows, cols = w.shape
    npart = len(parts)
    nprev = 0 if prev is None else 4
    if rows % 8 == 0:
        tr, tc = max(t for t in range(8, 257, 8) if rows % t == 0), cols
    else:
        tr, tc = rows, _tile(cols, 256)
    pick = (lambda i: (i, 0)) if rows % 8 == 0 else (lambda i: (0, i))

    def body(pos_ref, *refs):
        prefs = refs[:npart]
        w_ref, m_ref, v_ref = refs[npart:npart + 3]
        g_out, d_out, m_out, v_out = refs[npart + 3 + nprev:]
        g = prefs[0][...]
        for pr in prefs[1:]:
            g = g + pr[...]
        delta, mn, vn = _adamw_math(w_ref[...], g, m_ref[...], v_ref[...])
        g_out[...] = g
        d_out[...] = delta
        m_out[...] = mn
        v_out[...] = vn

    def part_spec(fn):
        return pl.BlockSpec((1, tr, tc), lambda i, p: (fn(p), *pick(i)))

    blk = pl.BlockSpec((1, tr, tc), lambda i, p: (layer, *pick(i)))
    shp = jax.ShapeDtypeStruct((depth, rows, cols), F32)
    first_prev = 1 + npart + 3
    return pl.pallas_call(
        body, name=name,
        grid_spec=pltpu.PrefetchScalarGridSpec(
            num_scalar_prefetch=1, grid=(rows // tr * (cols // tc),),
            in_specs=[part_spec(fn) for _, fn in parts] + [blk, blk, blk] + [ANY] * nprev,
            out_specs=[blk, blk, blk, blk]),
        out_shape=[shp, shp, shp, shp],
        input_output_aliases={first_prev + j: j for j in range(nprev)},
        compiler_params=_cp("parallel"),
    )(pos, *[a for a, _ in parts], w, m, v, *(prev or ()))


_P1024 = ["norm1_g", "norm2_g", "ssd_norm_g", "gm_vnorm_g", "gm_out_g"]
_P16 = ["ssd_dt_bias", "ssd_a_log", "ssd_d"]


def _adamw_small(gath, wmv):
    names = list(wmv.keys())
    classes = list(gath.keys())
    flat_in = [gath[k] for k in classes]
    for nme in names:
        flat_in += list(wmv[nme])
    out_shapes = []
    for nme in names:
        out_shapes += [jax.ShapeDtypeStruct(wmv[nme][0].shape, F32)] * 4
    out_shapes += [jax.ShapeDtypeStruct((DEPTH, SSD_CONV, CONV_DIM), F32), jax.ShapeDtypeStruct((DEPTH, FF_CONV, D_FF), F32),
                   jax.ShapeDtypeStruct((1, SSD_HEADS), F32)]
    scratch = [pltpu.VMEM(gath[k].shape[1:], F32) for k in classes]
    ncls = len(classes)

    def body(*refs):
        g_refs = dict(zip(classes, refs[:ncls]))
        pos = ncls
        w_refs = {}
        for nme in names:
            w_refs[nme] = refs[pos:pos + 3]
            pos += 3
        o_refs = {}
        for nme in names:
            o_refs[nme] = refs[pos:pos + 4]
            pos += 4
        scw_out, fcw_out, loss_out = refs[pos], refs[pos + 1], refs[pos + 2]
        s_refs = dict(zip(classes, refs[pos + 3:]))
        for k in classes:
            acc = g_refs[k][0]
            for dev in range(1, N_DEV):
                acc = acc + g_refs[k][dev]
            s_refs[k][...] = acc

        def apply(nme, grad_of):
            w_ref, m_ref, v_ref = w_refs[nme]
            g_out, d_out, m_out, v_out = o_refs[nme]
            shape = w_ref.shape
            if len(shape) == 2:
                idxs = [(slice(l, l + 1),) for l in range(shape[0])]
            elif len(shape) == 3:
                idxs = [(l,) for l in range(shape[0])]
            else:
                idxs = [(l, h) for l in range(shape[0]) for h in range(shape[1])]
            for n_i, ix in enumerate(idxs):
                g = grad_of(n_i)
                delta, mn, vn = _adamw_math(w_ref[ix], g, m_ref[ix], v_ref[ix])
                g_out[ix] = g
                d_out[ix] = delta
                m_out[ix] = mn
                v_out[ix] = vn

        s1024, s1536, s2816, s16, s128, s6144 = (s_refs[k] for k in classes)
        for n_i, nme in enumerate(_P1024):
            apply(nme, lambda l, b=2 * n_i: s1024[b + l:b + l + 1, :])
        apply("final_g", lambda l: s1024[10:11, :])
        apply("ssd_conv_b", lambda l: s1536[8 + l:9 + l, :])
        apply("ff_conv_b", lambda l: s2816[6 + l:7 + l, :])
        for n_i, nme in enumerate(_P16):
            apply(nme, lambda l, b=2 * n_i: s16[b + l:b + l + 1, :])
        apply("gm_ws", lambda q: s128[q * CHUNK:(q + 1) * CHUNK, :])
        apply("gm_bs", lambda l: s128[2048 + 8 * l:2048 + 8 * (l + 1), :])
        apply("ada_b", lambda l: s6144[2 * l:2 * l + 1, :] + s6144[2 * l + 1:2 * l + 2, :])
        for l in range(DEPTH):
            scw_out[l] = s1536[SSD_CONV * l:SSD_CONV * (l + 1), :]
            fcw_out[l] = s2816[FF_CONV * l:FF_CONV * (l + 1), :]
        loss_out[...] = s16[2 * len(_P16):2 * len(_P16) + 1, :]

    outs = pl.pallas_call(
        body, name="adamw_small",
        out_shape=out_shapes,
        scratch_shapes=scratch,
        compiler_params=pltpu.CompilerParams(vmem_limit_bytes=VMEM_LIMIT),
    )(*flat_in)
    res = {nme: tuple(outs[4 * i:4 * i + 4]) for i, nme in enumerate(names)}
    return res, outs[-3], outs[-2], outs[-1]


_WEIGHTS = ['ada_w', 'ada_b', 'norm1_g', 'norm2_g', 'w_in', 'ssd_conv_w', 'ssd_conv_b', 'ssd_dt_bias', 'ssd_a_log',
            'ssd_d', 'ssd_norm_g', 'gm_vnorm_g', 'gm_ws', 'gm_bs', 'gm_out_g', 'w_out', 'ff_up', 'ff_conv_w',
            'ff_conv_b', 'ff_down', 'final_g']


_O_XBC, _O_DT, _O_GM = D_SSD, D_SSD + CONV_DIM, D_SSD + CONV_DIM + SSD_HEADS


_TRANSPOSED = ("w_in", "ff_up")


def _full_weight(name, g):
    full = g.reshape(g.shape[0] * g.shape[1], g.shape[2])
    if name != "w_in":
        return full
    zpad = jnp.zeros((N_INP - N_IN, full.shape[1]), full.dtype)
    return jnp.concatenate([full[_O_GM:], full[:_O_XBC], full[_O_XBC:_O_DT], full[_O_DT:_O_GM], zpad], axis=0)


def _by_owner(name, grad):
    if name == "w_in":
        grad = jnp.concatenate([grad[COL_Z:COL_XBC], grad[COL_XBC:COL_DT], grad[COL_DT:COL_DT + SSD_HEADS], grad[:COL_Z]], axis=0)
    return grad.reshape(N_DEV, grad.shape[0] // N_DEV, grad.shape[1])


def kernel(x, c, ada_w, ada_b, norm1_g, norm2_g, w_in, ssd_conv_w, ssd_conv_b, ssd_dt_bias, ssd_a_log, ssd_d, ssd_norm_g, gm_vnorm_g, gm_ws, gm_bs, gm_out_g, w_out, ff_up, ff_conv_w, ff_conv_b, ff_down, final_g, loss_target, m_ada_w, m_ada_b, m_norm1_g, m_norm2_g, m_w_in, m_ssd_conv_w, m_ssd_conv_b, m_ssd_dt_bias, m_ssd_a_log, m_ssd_d, m_ssd_norm_g, m_gm_vnorm_g, m_gm_ws, m_gm_bs, m_gm_out_g, m_w_out, m_ff_up, m_ff_conv_w, m_ff_conv_b, m_ff_down, m_final_g, v_ada_w, v_ada_b, v_norm1_g, v_norm2_g, v_w_in, v_ssd_conv_w, v_ssd_conv_b, v_ssd_dt_bias, v_ssd_a_log, v_ssd_d, v_ssd_norm_g, v_gm_vnorm_g, v_gm_ws, v_gm_bs, v_gm_out_g, v_w_out, v_ff_up, v_ff_conv_w, v_ff_conv_b, v_ff_down, v_final_g):
    given = dict(locals())
    wts = {n: given[n] for n in _WEIGHTS}
    mom = {n: given["m_" + n] for n in _WEIGHTS}
    var = {n: given["v_" + n] for n in _WEIGHTS}
    nseq, seq, d = x.shape
    ix, iy, ic = lax.axis_index("x"), lax.axis_index("y"), lax.axis_index("c")
    me = 4 * ix + 2 * iy + ic
    me_arr = me.astype(jnp.int32).reshape(1)

    for nme in _TRANSPOSED:
        wts[nme], mom[nme], var[nme] = (jnp.transpose(a, (0, 2, 1)) for a in (wts[nme], mom[nme], var[nme]))

    def shard(l, name):
        return _b(wts[name][l])

    g_win0, g_scw, g_fcw, c_all = _all_gather([shard(0, "w_in"), ssd_conv_w, ff_conv_w, c], "gather_first")
    scw_f = jnp.transpose(g_scw, (1, 2, 0, 3)).reshape(DEPTH, SSD_CONV, CONV_DIM)
    fcw_f = jnp.transpose(g_fcw, (1, 2, 0, 3)).reshape(DEPTH, FF_CONV, D_FF)
    c_all = c_all.reshape(N_DEV * nseq, d)

    n_ada = ada_w.shape[2]
    ada_b_shard = lax.dynamic_slice_in_dim(ada_b, me * n_ada, n_ada, axis=1).reshape(DEPTH, 1, n_ada)
    mod_part, c_act = _ada_fwd(c_all, ada_w, ada_b_shard)
    (mod_g,) = _all_gather([mod_part], "gather_mod")
    mod_all = jnp.transpose(mod_g, (1, 2, 0, 3)).reshape(DEPTH, N_DEV * nseq, N_MOD * d)
    mod_mine = lax.dynamic_slice_in_dim(mod_all, me * nseq, nseq, axis=1)
    mods = [[mod_mine[l, :, k * d:(k + 1) * d].reshape(nseq, 1, d) for k in range(N_MOD)] for l in range(DEPTH)]

    later = [(0, "w_out"), (0, "ff_up"), (0, "ff_down"), (1, "w_in"), (1, "w_out"), (1, "ff_up"), (1, "ff_down")]
    ag_ssem, ag_rsem, ag_src, ag_land, ag_zero = _xc_start(False, [shard(l, n) for l, n in later], mod_g, "ag_start")
    ag_groups = {(0, "w_out"): [0], (0, "ff_up"): [1, 2], (1, "w_in"): [3, 4], (1, "ff_up"): [5, 6]}
    big_cache = {(0, "w_in"): _full_weight("w_in", g_win0)}

    def big_w(l, name, after):
        if (l, name) not in big_cache:
            idx = ag_groups[(l, name)]
            pick = lambda seq_: [seq_[i] for i in idx]
            srcs, lands = _xc_wait(False, pick(ag_ssem), pick(ag_rsem), pick(ag_src), pick(ag_land), after,
                                   f"ag_wait_{l}_{name}")
            for i, src, land in zip(idx, srcs, lands):
                big_cache[later[i]] = _full_weight(later[i][1], lax.dynamic_update_index_in_dim(land, src, me, 0))
        return big_cache[(l, name)]

    lw = []
    for l in range(DEPTH):
        lw.append(dict(
            norm1_g=norm1_g[l:l + 1] + (ag_zero if l == 0 else 0.0), norm2_g=norm2_g[l:l + 1], ssd_conv_w=scw_f[l],
            ssd_conv_b=ssd_conv_b[l:l + 1], ssd_dt_bias=ssd_dt_bias[l:l + 1], ssd_a_log=ssd_a_log[l:l + 1],
            ssd_d=ssd_d[l:l + 1], ssd_norm_g=ssd_norm_g[l:l + 1], gm_vnorm_g=gm_vnorm_g[l:l + 1], gm_ws=gm_ws[l],
            gm_bst=gm_bs[l].T, gm_out_g=gm_out_g[l:l + 1], ff_conv_w=fcw_f[l], ff_conv_b=ff_conv_b[l:l + 1]))

    outs = {}
    pending = {}

    def rs_finish(l, group, after):
        names, ssem, rsem, srcs, lands = pending.pop((l, group))
        srcs, lands = _xc_wait(True, ssem, rsem, srcs, lands, after, f"rs_wait_{l}_{group}")
        for nme, own, land in zip(names, srcs, lands):
            parts = [(own, lambda p: p[0])] + [(land, lambda p, k=k: k) for k in range(N_DEV - 1)]
            outs[nme] = _adamw_layer(parts, wts[nme], mom[nme], var[nme], me_arr, l, outs.get(nme), f"adamw_{nme}_{l}")
        return outs[names[-1]][0]

    def grad_sink(l, group, grads, after):
        names = list(grads)
        ssem, rsem, srcs, lands, zero = _xc_start(True, [_by_owner(n, grads[n]) for n in names], after, f"rs_start_{l}_{group}")
        pending[(l, group)] = (names, ssem, rsem, srcs, lands)
        return zero.reshape(1, 1)

    loss_p, grad_x, small, dmods, dfg = _local_step(
        x.reshape(nseq * seq, d), loss_target.reshape(nseq * seq, d), mods, lw, final_g.reshape(1, d), nseq=nseq,
        big_w=big_w, grad_sink=grad_sink)

    def rows(name):
        return [small[l][name] for l in range(DEPTH)]

    p1024 = jnp.concatenate(sum([rows(n) for n in _P1024], []) + [dfg], axis=0)
    p1536 = jnp.concatenate(rows("ssd_conv_w") + rows("ssd_conv_b"), axis=0)
    p2816 = jnp.concatenate(rows("ff_conv_w") + rows("ff_conv_b"), axis=0)
    p16 = jnp.concatenate(sum([rows(n) for n in _P16], []) + [loss_p[:, :SSD_HEADS]], axis=0)
    p128 = jnp.concatenate([small[l]["gm_ws"].reshape(GM_HEADS * CHUNK, CHUNK) for l in range(DEPTH)] + rows("gm_bs"), axis=0)
    p6144 = jnp.concatenate(dmods, axis=0)
    done = grad_x
    for l, grp in ((1, "ffn"), (1, "w_out"), (1, "w_in"), (0, "ffn"), (0, "w_out")):
        done = rs_finish(l, grp, done)
    gathered = _all_gather([p1024, p1536, p2816, p16, p128, p6144], "gather_small", dep=done)
    gath = dict(zip(["p1024", "p1536", "p2816", "p16", "p128", "p6144"], gathered))

    dmod_all = jnp.transpose(gath["p6144"].reshape(N_DEV, DEPTH, nseq, N_MOD * d), (1, 0, 2, 3)).reshape(
        DEPTH, N_DEV * nseq, N_MOD * d)
    small_names = _P1024 + ["final_g", "ssd_conv_b", "ff_conv_b"] + _P16 + ["gm_ws", "gm_bs", "ada_b"]
    wmv = {}
    for nme in small_names:
        if nme == "final_g":
            wmv[nme] = tuple(a.reshape(1, d) for a in (wts[nme], mom[nme], var[nme]))
        else:
            wmv[nme] = (wts[nme], mom[nme], var[nme])
    small_out, scw_full, fcw_full, loss_sum = _adamw_small(gath, wmv)
    loss = loss_sum[0, 0]
    rs_finish(0, "w_in", scw_full)
    for nme in small_names:
        outs[nme] = small_out[nme]
    outs["final_g"] = tuple(a.reshape(d) for a in outs["final_g"])

    n_scw, n_fcw = ssd_conv_w.shape[2], ff_conv_w.shape[2]
    g_scw_mine = lax.dynamic_slice_in_dim(scw_full, me * n_scw, n_scw, axis=2)
    g_fcw_mine = lax.dynamic_slice_in_dim(fcw_full, me * n_fcw, n_fcw, axis=2)
    outs["ssd_conv_w"] = _adamw_sharded([(g_scw_mine, lambda p: 0)], ssd_conv_w, m_ssd_conv_w, v_ssd_conv_w, me_arr, "adamw_ssd_conv_w")
    outs["ff_conv_w"] = _adamw_sharded([(g_fcw_mine, lambda p: 0)], ff_conv_w, m_ff_conv_w, v_ff_conv_w, me_arr, "adamw_ff_conv_w")

    dmod_cols = _b(lax.dynamic_slice_in_dim(dmod_all, me * n_ada, n_ada, axis=2))
    g_ada = jnp.stack([_matmul(c_act, dmod_cols[l], ta=True, name=f"mm_ada_dw_{l}") for l in range(DEPTH)])
    outs["ada_w"] = _adamw_sharded([(g_ada, lambda p: 0)], ada_w, m_ada_w, v_ada_w, me_arr, "adamw_ada_w")

    for nme in _TRANSPOSED:
        outs[nme] = tuple(jnp.transpose(a, (0, 2, 1)) for a in outs[nme])
    result = [loss, grad_x.reshape(nseq, seq, d)]
    for k in range(4):
        result += [outs[n][k] for n in _WEIGHTS]
    return tuple(result)
```

```python
import functools
import math

import jax
import jax.numpy as jnp
from jax import lax
from jax.experimental import pallas as pl
from jax.experimental.pallas import tpu as pltpu

F32 = jnp.float32
BF16 = jnp.bfloat16

N_DEV = 8
D_MODEL = 1024
DEPTH = 2
CHUNK = 128
SSD_HEADS = 16
SSD_HEAD_DIM = 64
SSD_GROUPS = 2
HEADS_PER_GROUP = SSD_HEADS // SSD_GROUPS
GROUP_WIDTH = HEADS_PER_GROUP * SSD_HEAD_DIM
D_STATE = 128
D_SSD = 1024
CONV_DIM = 1536
SSD_CONV = 4
GM_HEADS = 8
GM_HEAD_DIM = 128
D_GM = 1024
D_FF = 2816
FF_CONV = 3
N_IN = 4624
N_MOD = 6
EPS = 1e-6

N_INP = 5120
COL_U, COL_V, COL_Z, COL_XBC, COL_DT = 0, 1024, 2048, 3072, 4608
DT_BLOCK = 512

ADAM_LR = 0.001
ADAM_B1 = 0.9
ADAM_B2 = 0.999
ADAM_EPS = 1e-08
ADAM_WD = 0.01
ADAM_STEP = 10

VMEM_LIMIT = 56 * 1024 * 1024
MESH = pl.DeviceIdType.MESH
ANY = pl.BlockSpec(memory_space=pl.ANY)


def _cp(*sem):
    return pltpu.CompilerParams(dimension_semantics=sem, vmem_limit_bytes=VMEM_LIMIT)


def _tile(n, pref):
    if n <= pref or n % 128:
        return n
    best = 128
    for t in range(128, pref + 1, 128):
        if n % t == 0:
            best = t
    return best


def _silu(x):
    return x * jax.nn.sigmoid(x)


def _gelu(x):
    return 0.5 * x * (1.0 + lax.erf(x * (1.0 / math.sqrt(2.0))))


def _softplus(x):
    return jnp.maximum(x, 0.0) + jnp.log1p(jnp.exp(-jnp.abs(x)))


def _rms(x, g, width):
    return x * lax.rsqrt(jnp.sum(x * x, axis=-1, keepdims=True) / width + EPS) * g


def _b(x):
    return x.astype(BF16)


_NN = (((1,), (0,)), ((), ()))
_NT = (((1,), (1,)), ((), ()))
_TN = (((0,), (0,)), ((), ()))


def _dg(a, b, dn):
    return lax.dot_general(_b(a), _b(b), dn, preferred_element_type=F32)


@jax.custom_vjp
def _bdot(a, b):
    return _dg(a, b, _NN)


def _bdot_fwd(a, b):
    return _dg(a, b, _NN), (a, b)


def _bdot_bwd(res, ct):
    a, b = res
    return _dg(ct, b, _NT), _dg(a, ct, _TN)


_bdot.defvjp(_bdot_fwd, _bdot_bwd)


@jax.custom_vjp
def _bdot_nt(a, b):
    return _dg(a, b, _NT)


def _bdot_nt_fwd(a, b):
    return _dg(a, b, _NT), (a, b)


def _bdot_nt_bwd(res, ct):
    a, b = res
    return _dg(ct, b, _NN), _dg(ct, a, _TN)


_bdot_nt.defvjp(_bdot_nt_fwd, _bdot_nt_bwd)


@jax.custom_vjp
def _bdot_tn(a, b):
    return _dg(a, b, _TN)


def _bdot_tn_fwd(a, b):
    return _dg(a, b, _TN), (a, b)


def _bdot_tn_bwd(res, ct):
    a, b = res
    return _dg(b, ct, _NT), _dg(a, ct, _NN)


_bdot_tn.defvjp(_bdot_tn_fwd, _bdot_tn_bwd)


def _tri(n, lower):
    r = lax.broadcasted_iota(jnp.int32, (n, n), 0)
    c = lax.broadcasted_iota(jnp.int32, (n, n), 1)
    return ((r >= c) if lower else (r <= c)).astype(F32)


def _eye(n):
    r = lax.broadcasted_iota(jnp.int32, (n, n), 0)
    c = lax.broadcasted_iota(jnp.int32, (n, n), 1)
    return (r == c).astype(F32)


def _hdot(a, b, dn):
    return lax.dot_general(a, b, dn, precision=lax.Precision.HIGHEST, preferred_element_type=F32)


@jax.custom_vjp
def _cumsum_rows(x):
    return _hdot(_tri(x.shape[0], True), x, _NN)


def _cumsum_rows_fwd(x):
    return _cumsum_rows(x), None


def _cumsum_rows_bwd(_, ct):
    return (_hdot(_tri(ct.shape[0], False), ct, _NN),)


_cumsum_rows.defvjp(_cumsum_rows_fwd, _cumsum_rows_bwd)


@jax.custom_vjp
def _transpose(x):
    return _hdot(_eye(x.shape[1]), x, _NT)


def _transpose_fwd(x):
    return _transpose(x), None


def _transpose_bwd(_, ct):
    return (_hdot(_eye(ct.shape[1]), ct, _NT),)


_transpose.defvjp(_transpose_fwd, _transpose_bwd)


MXU_WIDTH = 256
MATMUL_TILE_CAP = 2816
MATMUL_VMEM = 44 * 1024 * 1024


def _mxu_tiles(n):
    if n <= MATMUL_TILE_CAP or n % 128:
        return [n]
    for unit in (MXU_WIDTH, 128):
        opts = [t for t in range(unit, MATMUL_TILE_CAP + 1, unit) if n % t == 0]
        if opts:
            return opts
    return [n]


def _matmul(a, b, *, ta=False, tb=False, name, dep=None, out_dtype=F32):
    pieces = list(a) if isinstance(a, (list, tuple)) else [a]
    npc = len(pieces)
    rows, width = pieces[0].shape
    assert all(p.shape == (rows, width) for p in pieces)
    if ta:
        k_dim, m_dim = rows, width * npc
    else:
        m_dim, k_dim = rows, width * npc
    if tb:
        n_dim, kb = b.shape
    else:
        kb, n_dim = b.shape
    assert kb == k_dim, (pieces[0].shape, npc, b.shape, ta, tb)
    m_unit = width if npc > 1 and ta else m_dim
    k_unit = width if npc > 1 and not ta else k_dim
    tm = _tile(m_unit, 1536)
    tn_opts, tk_opts = _mxu_tiles(n_dim), _mxu_tiles(k_unit)
    tn, tk = tn_opts.pop(), tk_opts.pop()
    while 4 * (tm * tk + tk * tn) + 8 * tm * tn > MATMUL_VMEM:
        if tn >= tk and tn_opts:
            tn = tn_opts.pop()
        else:
            tk = tk_opts.pop()
    ni, nj, nk = m_dim // tm, n_dim // tn, k_dim // tk
    per = width // (tm if ta else tk)
    dn = (((0 if ta else 1,), (1 if tb else 0,)), ((), ()))

    a_bytes, b_bytes = m_dim * k_dim, k_dim * n_dim
    m_outer = nk > 1 or a_bytes + b_bytes * ni <= b_bytes + a_bytes * nj
    if m_outer:
        ij = lambda o, n, k: (o, n)
        grid = (ni, nj, nk)
    else:
        ij = lambda o, n, k: (n, o)
        grid = (nj, ni, nk)

    use_acc = nk > 1 and out_dtype != F32

    def body(*refs):
        a_refs, b_ref = refs[:npc], refs[npc]
        o_ref = refs[-2] if use_acc else refs[-1]
        acc_ref = refs[-1]
        k = pl.program_id(2)
        i = pl.program_id(0 if m_outer else 1)
        along = i if ta else k

        def step(a_ref):
            p = lax.dot_general(a_ref[...], b_ref[...], dn, preferred_element_type=F32)
            if nk == 1:
                o_ref[...] = p.astype(out_dtype)
            else:
                @pl.when(k == 0)
                def _():
                    acc_ref[...] = p

                @pl.when((k > 0) & (k < nk - 1 if use_acc else True))
                def _():
                    acc_ref[...] += p

                if use_acc:
                    @pl.when(k == nk - 1)
                    def _():
                        o_ref[...] = (acc_ref[...] + p).astype(out_dtype)

        if npc == 1:
            step(a_refs[0])
        else:
            for pc in range(npc):
                pl.when((along >= pc * per) & (along < (pc + 1) * per))(functools.partial(step, a_refs[pc]))

    def a_map(pc, o, n, k):
        i, _ = ij(o, n, k)
        along = i if ta else k
        if npc > 1:
            along = jnp.clip(along - pc * per, 0, per - 1)
        return (k, along) if ta else (i, along)

    def b_map(o, n, k):
        _, j = ij(o, n, k)
        return (j, k) if tb else (k, j)

    extra = [] if dep is None else [dep]
    return pl.pallas_call(
        body, name=name,
        grid=grid,
        in_specs=[pl.BlockSpec((tk, tm) if ta else (tm, tk), functools.partial(a_map, pc)) for pc in range(npc)]
        + [pl.BlockSpec((tn, tk) if tb else (tk, tn), b_map)] + [ANY] * len(extra),
        out_specs=pl.BlockSpec((tm, tn), lambda o, n, k: ij(o, n, k)),
        out_shape=jax.ShapeDtypeStruct((m_dim, n_dim), out_dtype),
        scratch_shapes=[pltpu.VMEM((tm, tn), F32)] if use_acc else [],
        compiler_params=_cp("parallel", "parallel", "arbitrary"),
    )(*pieces, b, *extra)


def _ada_fwd(c_all, ada_w, ada_b_shard):
    depth, d, n = ada_w.shape
    nb = c_all.shape[0]

    def body(c_ref, w_ref, b_ref, o_ref, ca_ref):
        ca = _silu(c_ref[...])
        ca_ref[...] = _b(ca)
        o_ref[0] = _dg(ca, w_ref[0], _NN) + b_ref[0]

    return pl.pallas_call(
        body, name="ada_fwd",
        grid=(depth,),
        in_specs=[pl.BlockSpec((nb, d), lambda l: (0, 0)),
                  pl.BlockSpec((1, d, n), lambda l: (l, 0, 0)),
                  pl.BlockSpec((1, 1, n), lambda l: (l, 0, 0))],
        out_specs=[pl.BlockSpec((1, nb, n), lambda l: (l, 0, 0)),
                   pl.BlockSpec((nb, d), lambda l: (0, 0))],
        out_shape=[jax.ShapeDtypeStruct((depth, nb, n), F32), jax.ShapeDtypeStruct((nb, d), BF16)],
        compiler_params=_cp("arbitrary"),
    )(c_all, ada_w, ada_b_shard)


def _normmod_f(x, g, sc, sh):
    return _rms(x, g, D_MODEL) * (1.0 + sc) + sh


def _row_tile(seq):
    return min(seq, 256)


def _normmod_fwd(xin, delta, gate, g, sc, sh, *, nseq, name):
    t, d = xin.shape
    seq = t // nseq
    tr = _row_tile(seq)
    nt = seq // tr
    has_delta = delta is not None
    row = pl.BlockSpec((tr, d), lambda s, i: (s * nt + i, 0))
    per_seq = pl.BlockSpec((1, 1, d), lambda s, i: (s, 0, 0))
    vec = pl.BlockSpec((1, d), lambda s, i: (0, 0))

    if has_delta:
        def body(xin_ref, delta_ref, gate_ref, g_ref, sc_ref, sh_ref, x_ref, h_ref):
            x = xin_ref[...] + gate_ref[0] * delta_ref[...]
            x_ref[...] = x
            h_ref[...] = _b(_normmod_f(x, g_ref[...], sc_ref[0], sh_ref[0]))

        return pl.pallas_call(
            body, name=name, grid=(nseq, nt),
            in_specs=[row, row, per_seq, vec, per_seq, per_seq],
            out_specs=[row, row],
            out_shape=[jax.ShapeDtypeStruct((t, d), F32), jax.ShapeDtypeStruct((t, d), BF16)],
            compiler_params=_cp("parallel", "parallel"),
        )(xin, delta, gate, g, sc, sh)

    def body0(xin_ref, g_ref, sc_ref, sh_ref, h_ref):
        h_ref[...] = _b(_normmod_f(xin_ref[...], g_ref[...], sc_ref[0], sh_ref[0]))

    h = pl.pallas_call(
        body0, name=name, grid=(nseq, nt),
        in_specs=[row, vec, per_seq, per_seq],
        out_specs=row,
        out_shape=jax.ShapeDtypeStruct((t, d), BF16),
        compiler_params=_cp("parallel", "parallel"),
    )(xin, g, sc, sh)
    return xin, h


def _normmod_bwd(dh, dxo, x, delta, gate, g, sc, *, nseq, name):
    t, d = x.shape
    seq = t // nseq
    tr = _row_tile(seq)
    nt = seq // tr
    has_delta = delta is not None
    row = pl.BlockSpec((tr, d), lambda s, i: (s * nt + i, 0))
    per_seq = pl.BlockSpec((1, 1, d), lambda s, i: (s, 0, 0))
    vec = pl.BlockSpec((1, d), lambda s, i: (0, 0))

    def core(dh_ref, dxo_ref, x_ref, g_ref, sc_ref, dx_ref, dg_ref, dsc_ref, dsh_ref):
        s, i = pl.program_id(0), pl.program_id(1)
        dh_v = dh_ref[...]
        _, vjp = jax.vjp(lambda xx, gg, ss: _normmod_f(xx, gg, ss, 0.0), x_ref[...], g_ref[...], sc_ref[0])
        dxn, dg_t, dsc_t = vjp(dh_v)
        dx = dxo_ref[...] + dxn
        dx_ref[...] = dx
        dsh_t = jnp.sum(dh_v, axis=0, keepdims=True)

        @pl.when((s == 0) & (i == 0))
        def _():
            dg_ref[...] = jnp.zeros_like(dg_ref)

        @pl.when(i == 0)
        def _():
            dsc_ref[...] = jnp.zeros_like(dsc_ref)
            dsh_ref[...] = jnp.zeros_like(dsh_ref)

        dg_ref[...] += dg_t
        dsc_ref[0] += dsc_t
        dsh_ref[0] += dsh_t
        return dx

    if has_delta:
        def body(dh_ref, dxo_ref, x_ref, delta_ref, gate_ref, g_ref, sc_ref,
                 dx_ref, dd_ref, dgate_ref, dg_ref, dsc_ref, dsh_ref):
            dx = core(dh_ref, dxo_ref, x_ref, g_ref, sc_ref, dx_ref, dg_ref, dsc_ref, dsh_ref)
            dd_ref[...] = _b(dx * gate_ref[0])

            @pl.when(pl.program_id(1) == 0)
            def _():
                dgate_ref[...] = jnp.zeros_like(dgate_ref)

            dgate_ref[0] += jnp.sum(dx * delta_ref[...], axis=0, keepdims=True)

        return pl.pallas_call(
            body, name=name, grid=(nseq, nt),
            in_specs=[row, row, row, row, per_seq, vec, per_seq],
            out_specs=[row, row, per_seq, vec, per_seq, per_seq],
            out_shape=[jax.ShapeDtypeStruct((t, d), F32), jax.ShapeDtypeStruct((t, d), BF16),
                       jax.ShapeDtypeStruct((nseq, 1, d), F32), jax.ShapeDtypeStruct((1, d), F32),
                       jax.ShapeDtypeStruct((nseq, 1, d), F32), jax.ShapeDtypeStruct((nseq, 1, d), F32)],
            compiler_params=_cp("arbitrary", "arbitrary"),
        )(dh, dxo, x, delta, gate, g, sc)

    def body0(dh_ref, dxo_ref, x_ref, g_ref, sc_ref, dx_ref, dg_ref, dsc_ref, dsh_ref):
        core(dh_ref, dxo_ref, x_ref, g_ref, sc_ref, dx_ref, dg_ref, dsc_ref, dsh_ref)

    dx, dg, dsc, dsh = pl.pallas_call(
        body0, name=name, grid=(nseq, nt),
        in_specs=[row, row, row, vec, per_seq],
        out_specs=[row, vec, per_seq, per_seq],
        out_shape=[jax.ShapeDtypeStruct((t, d), F32), jax.ShapeDtypeStruct((1, d), F32),
                   jax.ShapeDtypeStruct((nseq, 1, d), F32), jax.ShapeDtypeStruct((nseq, 1, d), F32)],
        compiler_params=_cp("arbitrary", "arbitrary"),
    )(dh, dxo, x, g, sc)
    return dx, None, None, dg, dsc, dsh


def _final_loss(xin, delta, gate, fg, target, *, nseq):
    t, d = xin.shape
    seq = t // nseq
    tr = _row_tile(seq)
    nt = seq // tr
    row = pl.BlockSpec((tr, d), lambda s, i: (s * nt + i, 0))
    per_seq = pl.BlockSpec((1, 1, d), lambda s, i: (s, 0, 0))
    vec = pl.BlockSpec((1, d), lambda s, i: (0, 0))

    def body(xin_ref, delta_ref, gate_ref, fg_ref, tgt_ref, loss_ref, dx_ref, dd_ref, dgate_ref, dfg_ref):
        s, i = pl.program_id(0), pl.program_id(1)
        dl = delta_ref[...]
        x = xin_ref[...] + gate_ref[0] * dl
        y, vjp = jax.vjp(lambda xx, gg: _rms(xx, gg, D_MODEL), x, fg_ref[...])
        err = y - tgt_ref[...]
        dx, dfg_t = vjp(err * (1.0 / d))
        dx_ref[...] = dx
        dd_ref[...] = _b(dx * gate_ref[0])

        @pl.when((s == 0) & (i == 0))
        def _():
            loss_ref[...] = jnp.zeros_like(loss_ref)
            dfg_ref[...] = jnp.zeros_like(dfg_ref)

        @pl.when(i == 0)
        def _():
            dgate_ref[...] = jnp.zeros_like(dgate_ref)

        loss_ref[...] += jnp.sum(err * err) * (0.5 / d)
        dfg_ref[...] += dfg_t
        dgate_ref[0] += jnp.sum(dx * dl, axis=0, keepdims=True)

    return pl.pallas_call(
        body, name="final_loss", grid=(nseq, nt),
        in_specs=[row, row, per_seq, vec, row],
        out_specs=[pl.BlockSpec((1, 128), lambda s, i: (0, 0)), row, row, per_seq, vec],
        out_shape=[jax.ShapeDtypeStruct((1, 128), F32), jax.ShapeDtypeStruct((t, d), F32),
                   jax.ShapeDtypeStruct((t, d), BF16), jax.ShapeDtypeStruct((nseq, 1, d), F32),
                   jax.ShapeDtypeStruct((1, d), F32)],
        compiler_params=_cp("arbitrary", "arbitrary"),
    )(xin, delta, gate, fg, target)


def _shift_down(x, j):
    if j == 0:
        return x
    rows = lax.broadcasted_iota(jnp.int32, x.shape, 0)
    return jnp.where(rows >= j, pltpu.roll(x, j, 0), 0.0)


def _shift_up(x, j):
    if j == 0:
        return x
    n = x.shape[0]
    rows = lax.broadcasted_iota(jnp.int32, x.shape, 0)
    return jnp.where(rows < n - j, pltpu.roll(x, n - j, 0), 0.0)


def _conv(x, w_ref, b_ref):
    kw = w_ref.shape[0]
    y = b_ref[...] + w_ref[kw - 1:kw, :] * x
    for j in range(1, kw):
        y = y + w_ref[kw - 1 - j:kw - j, :] * _shift_down(x, j)
    return y


def _conv_bwd(dy, x, w_ref, dw_ref, db_ref):
    kw = w_ref.shape[0]
    dx = w_ref[kw - 1:kw, :] * dy
    dw_ref[kw - 1:kw, :] += jnp.sum(dy * x, axis=0, keepdims=True)
    for j in range(1, kw):
        dy_j = _shift_up(dy, j)
        dx = dx + w_ref[kw - 1 - j:kw - j, :] * dy_j
        dw_ref[kw - 1 - j:kw - j, :] += jnp.sum(dy_j * x, axis=0, keepdims=True)
    db_ref[...] += jnp.sum(dy, axis=0, keepdims=True)
    return dx


CONV_TC = 256


def _ssd_conv_fwd(proj, w, b, *, nseq):
    t = proj.shape[0]
    seq = t // nseq
    nb = CONV_DIM // CONV_TC
    off = COL_XBC // CONV_TC

    def body(x_ref, w_ref, b_ref, o_ref):
        o_ref[...] = _silu(_conv(x_ref[...], w_ref, b_ref))

    return pl.pallas_call(
        body, name="ssd_conv_fwd", grid=(nb, nseq),
        in_specs=[pl.BlockSpec((seq, CONV_TC), lambda j, s: (s, off + j)),
                  pl.BlockSpec((SSD_CONV, CONV_TC), lambda j, s: (0, j)),
                  pl.BlockSpec((1, CONV_TC), lambda j, s: (0, j))],
        out_specs=pl.BlockSpec((seq, CONV_TC), lambda j, s: (s, j)),
        out_shape=jax.ShapeDtypeStruct((t, CONV_DIM), F32),
        compiler_params=_cp("parallel", "parallel"),
    )(proj, w, b)


def _ssd_conv_bwd(dact, proj, w, b, dproj, *, nseq):
    t = proj.shape[0]
    seq = t // nseq
    nb = CONV_DIM // CONV_TC
    off = COL_XBC // CONV_TC

    def body(da_ref, x_ref, w_ref, b_ref, dproj_ref, dx_ref, dw_ref, db_ref):
        del dproj_ref

        @pl.when(pl.program_id(1) == 0)
        def _():
            dw_ref[...] = jnp.zeros_like(dw_ref)
            db_ref[...] = jnp.zeros_like(db_ref)

        x = x_ref[...]
        pre = _conv(x, w_ref, b_ref)
        sg = jax.nn.sigmoid(pre)
        dpre = da_ref[...] * (sg * (1.0 + pre * (1.0 - sg)))
        dx_ref[...] = _b(_conv_bwd(dpre, x, w_ref, dw_ref, db_ref))

    return pl.pallas_call(
        body, name="ssd_conv_bwd", grid=(nb, nseq),
        in_specs=[pl.BlockSpec((seq, CONV_TC), lambda j, s: (s, j)),
                  pl.BlockSpec((seq, CONV_TC), lambda j, s: (s, off + j)),
                  pl.BlockSpec((SSD_CONV, CONV_TC), lambda j, s: (0, j)),
                  pl.BlockSpec((1, CONV_TC), lambda j, s: (0, j)),
                  ANY],
        out_specs=[pl.BlockSpec((seq, CONV_TC), lambda j, s: (s, off + j)),
                   pl.BlockSpec((SSD_CONV, CONV_TC), lambda j, s: (0, j)),
                   pl.BlockSpec((1, CONV_TC), lambda j, s: (0, j))],
        out_shape=[jax.ShapeDtypeStruct(dproj.shape, dproj.dtype), jax.ShapeDtypeStruct((SSD_CONV, CONV_DIM), F32),
                   jax.ShapeDtypeStruct((1, CONV_DIM), F32)],
        input_output_aliases={4: 0},
        compiler_params=_cp("parallel", "arbitrary"),
    )(dact, proj, w, b, dproj)


def _ffn_act_fwd(up, w, b, *, nseq):
    t = up.shape[0]
    seq = t // nseq
    nb = D_FF // CONV_TC

    def body(g_ref, v_ref, w_ref, b_ref, o_ref):
        o_ref[...] = _b(_silu(_conv(g_ref[...], w_ref, b_ref)) * v_ref[...])

    return pl.pallas_call(
        body, name="ffn_act_fwd", grid=(nb, nseq),
        in_specs=[pl.BlockSpec((seq, CONV_TC), lambda j, s: (s, j)),
                  pl.BlockSpec((seq, CONV_TC), lambda j, s: (s, nb + j)),
                  pl.BlockSpec((FF_CONV, CONV_TC), lambda j, s: (0, j)),
                  pl.BlockSpec((1, CONV_TC), lambda j, s: (0, j))],
        out_specs=pl.BlockSpec((seq, CONV_TC), lambda j, s: (s, j)),
        out_shape=jax.ShapeDtypeStruct((t, D_FF), BF16),
        compiler_params=_cp("parallel", "parallel"),
    )(up, up, w, b)


def _ffn_act_bwd(dact, up, w, b, *, nseq):
    t = up.shape[0]
    seq = t // nseq
    nb = D_FF // CONV_TC

    def body(da_ref, g_ref, v_ref, w_ref, b_ref, dg_ref, dv_ref, dw_ref, db_ref):
        @pl.when(pl.program_id(1) == 0)
        def _():
            dw_ref[...] = jnp.zeros_like(dw_ref)
            db_ref[...] = jnp.zeros_like(db_ref)

        gate = g_ref[...]
        pre = _conv(gate, w_ref, b_ref)
        sg = jax.nn.sigmoid(pre)
        da = da_ref[...]
        dv_ref[...] = _b(da * (pre * sg))
        dpre = da * v_ref[...] * (sg * (1.0 + pre * (1.0 - sg)))
        dg_ref[...] = _b(_conv_bwd(dpre, gate, w_ref, dw_ref, db_ref))

    col = pl.BlockSpec((seq, CONV_TC), lambda j, s: (s, j))
    return pl.pallas_call(
        body, name="ffn_act_bwd", grid=(nb, nseq),
        in_specs=[col, col,
                  pl.BlockSpec((seq, CONV_TC), lambda j, s: (s, nb + j)),
                  pl.BlockSpec((FF_CONV, CONV_TC), lambda j, s: (0, j)),
                  pl.BlockSpec((1, CONV_TC), lambda j, s: (0, j))],
        out_specs=[col, col,
                   pl.BlockSpec((FF_CONV, CONV_TC), lambda j, s: (0, j)),
                   pl.BlockSpec((1, CONV_TC), lambda j, s: (0, j))],
        out_shape=[jax.ShapeDtypeStruct((t, D_FF), BF16), jax.ShapeDtypeStruct((t, D_FF), BF16),
                   jax.ShapeDtypeStruct((FF_CONV, D_FF), F32), jax.ShapeDtypeStruct((1, D_FF), F32)],
        compiler_params=_cp("parallel", "arbitrary"),
    )(dact, up, up, w, b)


SSD_PAIRS = SSD_HEADS // 2
PAIR_W = 2 * SSD_HEAD_DIM
PAIRS_PER_GROUP = SSD_PAIRS // SSD_GROUPS


def _ssd_chunk(xs, bg, cg, dtr, z, hp, dtb, alog, dskip, ng):
    n = dtr.shape[0]
    dt = _softplus(dtr + dtb)
    cs = _cumsum_rows(dt * (-jnp.exp(alog)))
    cs_t = _transpose(cs)
    lane = lax.broadcasted_iota(jnp.int32, (1, SSD_HEADS), 1)
    sub = lax.broadcasted_iota(jnp.int32, (SSD_HEADS, 1), 0)
    row = lax.broadcasted_iota(jnp.int32, (n, 1), 0)
    causal = lax.broadcasted_iota(jnp.int32, (n, n), 0) >= lax.broadcasted_iota(jnp.int32, (n, n), 1)
    first = lax.broadcasted_iota(jnp.int32, (1, PAIR_W), 1) < SSD_HEAD_DIM
    first_rows = lax.broadcasted_iota(jnp.int32, (PAIR_W, 1), 0) < SSD_HEAD_DIM
    first_f = first.astype(F32)
    cb = [_bdot_nt(cg[g], bg[g]) for g in range(SSD_GROUPS)]
    ys, hn = [], []
    for p in range(SSD_PAIRS):
        g = p // PAIRS_PER_GROUP
        col, decay, last = [], [], []
        for h in (2 * p, 2 * p + 1):
            oh = (lane == h).astype(F32)
            cs_h = jnp.sum(cs * oh, axis=1, keepdims=True)
            cs_row = jnp.sum(cs_t * (sub == h).astype(F32), axis=0, keepdims=True)
            col.append((jnp.sum(dt * oh, axis=1, keepdims=True), cs_h, jnp.sum(dskip * oh, axis=1, keepdims=True)))
            last.append(jnp.sum(jnp.where(row == n - 1, cs_h, 0.0), axis=0, keepdims=True))
            decay.append(jnp.where(causal, jnp.exp(jnp.where(causal, cs_h - cs_row, 0.0)), 0.0))
        pair = lambda a, b: jnp.where(first, a, b)
        dt_p = pair(col[0][0], col[1][0])
        cs_p = pair(col[0][1], col[1][1])
        last_p = pair(last[0], last[1])
        xc = xs[p] * dt_p
        y = _bdot(cb[g] * decay[0], xc * first_f) + _bdot(cb[g] * decay[1], xc * (1.0 - first_f))
        y = y + _bdot_nt(cg[g], hp[p]) * jnp.exp(cs_p)
        y = y + pair(col[0][2], col[1][2]) * xs[p]
        keep = jnp.where(first_rows, jnp.exp(last[0]), jnp.exp(last[1]))
        hn.append(keep * hp[p] + _bdot_tn(xc * jnp.exp(last_p - cs_p), bg[g]))
        ys.append(y * _silu(z[p]))
    outs = []
    for g in range(SSD_GROUPS):
        ps = range(g * PAIRS_PER_GROUP, (g + 1) * PAIRS_PER_GROUP)
        ms = sum(jnp.sum(ys[p] * ys[p], axis=1, keepdims=True) for p in ps) * (1.0 / GROUP_WIDTH)
        r = lax.rsqrt(ms + EPS)
        outs += [ys[p] * r * ng[p] for p in ps]
    return outs, hn


def _hslices(ref, width, count, base=0):
    return [ref[:, base + k * width: base + (k + 1) * width] for k in range(count)]


def _ssd_load(xbc_ref, z_ref, dt_ref, ng_ref):
    xs = _hslices(xbc_ref, PAIR_W, SSD_PAIRS)
    bg = _hslices(xbc_ref, D_STATE, SSD_GROUPS, D_SSD)
    cg = _hslices(xbc_ref, D_STATE, SSD_GROUPS, D_SSD + SSD_GROUPS * D_STATE)
    z = _hslices(z_ref, PAIR_W, SSD_PAIRS)
    ng = _hslices(ng_ref, PAIR_W, SSD_PAIRS)
    return xs, bg, cg, dt_ref[:, 0:SSD_HEADS], z, ng


def _ssd_specs(nch):
    rowi = lambda s, c: s * nch + c
    return [pl.BlockSpec((CHUNK, CONV_DIM), lambda s, c: (rowi(s, c), 0)),
            pl.BlockSpec((CHUNK, D_SSD), lambda s, c: (rowi(s, c), COL_Z // D_SSD)),
            pl.BlockSpec((CHUNK, 128), lambda s, c: (rowi(s, c), COL_DT // 128)),
            pl.BlockSpec((1, SSD_HEADS), lambda s, c: (0, 0)),
            pl.BlockSpec((1, SSD_HEADS), lambda s, c: (0, 0)),
            pl.BlockSpec((1, SSD_HEADS), lambda s, c: (0, 0)),
            pl.BlockSpec((1, D_SSD), lambda s, c: (0, 0))]


def _ssd_fwd(xbc, proj, dtb, alog, dskip, ng, *, nseq):
    t = proj.shape[0]
    nch = t // nseq // CHUNK
    hd = PAIR_W

    def body(xbc_ref, z_ref, dt_ref, dtb_ref, alog_ref, dsk_ref, ng_ref, y_ref, hp_ref, h_ref):
        @pl.when(pl.program_id(1) == 0)
        def _():
            h_ref[...] = jnp.zeros_like(h_ref)

        xs, bg, cg, dtr, z, ngs = _ssd_load(xbc_ref, z_ref, dt_ref, ng_ref)
        hp_ref[0] = h_ref[...]
        hp = [h_ref[h * hd:(h + 1) * hd, :] for h in range(SSD_PAIRS)]
        outs, hn = _ssd_chunk(xs, bg, cg, dtr, z, hp, dtb_ref[...], alog_ref[...], dsk_ref[...], ngs)
        for h in range(SSD_PAIRS):
            y_ref[:, h * hd:(h + 1) * hd] = _b(outs[h])
            h_ref[h * hd:(h + 1) * hd, :] = hn[h]

    return pl.pallas_call(
        body, name="ssd_fwd", grid=(nseq, nch),
        in_specs=_ssd_specs(nch),
        out_specs=[pl.BlockSpec((CHUNK, D_SSD), lambda s, c: (s * nch + c, 0)),
                   pl.BlockSpec((1, D_SSD, D_STATE), lambda s, c: (s * nch + c, 0, 0))],
        out_shape=[jax.ShapeDtypeStruct((t, D_SSD + D_GM), BF16),
                   jax.ShapeDtypeStruct((t // CHUNK, D_SSD, D_STATE), F32)],
        scratch_shapes=[pltpu.VMEM((D_SSD, D_STATE), F32)],
        compiler_params=_cp("arbitrary", "arbitrary"),
    )(xbc, proj, proj, dtb, alog, dskip, ng)


def _ssd_bwd(dy, xbc, proj, hprev, dtb, alog, dskip, ng, *, nseq):
    t = proj.shape[0]
    nch = t // nseq // CHUNK
    hd = PAIR_W
    rev = lambda s, c: s * nch + (nch - 1 - c)

    def body(dy_ref, xbc_ref, z_ref, dt_ref, hp_ref, dtb_ref, alog_ref, dsk_ref, ng_ref,
             dxbc_ref, dproj_ref, ddtb_ref, dalog_ref, ddsk_ref, dng_ref, dh_ref):
        first = (pl.program_id(0) == 0) & (pl.program_id(1) == 0)

        @pl.when(pl.program_id(1) == 0)
        def _():
            dh_ref[...] = jnp.zeros_like(dh_ref)

        @pl.when(first)
        def _():
            ddtb_ref[...] = jnp.zeros_like(ddtb_ref)
            dalog_ref[...] = jnp.zeros_like(dalog_ref)
            ddsk_ref[...] = jnp.zeros_like(ddsk_ref)
            dng_ref[...] = jnp.zeros_like(dng_ref)

        xs, bg, cg, dtr, z, ngs = _ssd_load(xbc_ref, z_ref, dt_ref, ng_ref)
        hp = [hp_ref[0, h * hd:(h + 1) * hd, :] for h in range(SSD_PAIRS)]
        _, vjp = jax.vjp(_ssd_chunk, xs, bg, cg, dtr, z, hp, dtb_ref[...], alog_ref[...], dsk_ref[...], ngs)
        douts = [dy_ref[:, h * hd:(h + 1) * hd] for h in range(SSD_PAIRS)]
        dhn = [dh_ref[h * hd:(h + 1) * hd, :] for h in range(SSD_PAIRS)]
        dxs, dbg, dcg, ddtr, dz, dhp, ddtb, dalog, ddsk, dngs = vjp((douts, dhn))
        dproj_ref[:, :COL_Z] = jnp.zeros((CHUNK, COL_Z), BF16)
        dproj_ref[:, COL_XBC:] = jnp.zeros((CHUNK, N_INP - COL_XBC), BF16)
        for h in range(SSD_PAIRS):
            dxbc_ref[:, h * hd:(h + 1) * hd] = dxs[h]
            dproj_ref[:, COL_Z + h * hd: COL_Z + (h + 1) * hd] = _b(dz[h])
            dh_ref[h * hd:(h + 1) * hd, :] = dhp[h]
            dng_ref[:, h * hd:(h + 1) * hd] += dngs[h]
        for g in range(SSD_GROUPS):
            dxbc_ref[:, D_SSD + g * D_STATE: D_SSD + (g + 1) * D_STATE] = dbg[g]
            dxbc_ref[:, D_SSD + (SSD_GROUPS + g) * D_STATE: D_SSD + (SSD_GROUPS + g + 1) * D_STATE] = dcg[g]
        dproj_ref[:, COL_DT:COL_DT + SSD_HEADS] = _b(ddtr)
        ddtb_ref[...] += ddtb
        dalog_ref[...] += dalog
        ddsk_ref[...] += ddsk

    small = pl.BlockSpec((1, SSD_HEADS), lambda s, c: (0, 0))
    return pl.pallas_call(
        body, name="ssd_bwd", grid=(nseq, nch),
        in_specs=[pl.BlockSpec((CHUNK, D_SSD), lambda s, c: (rev(s, c), 0)),
                  pl.BlockSpec((CHUNK, CONV_DIM), lambda s, c: (rev(s, c), 0)),
                  pl.BlockSpec((CHUNK, D_SSD), lambda s, c: (rev(s, c), COL_Z // D_SSD)),
                  pl.BlockSpec((CHUNK, 128), lambda s, c: (rev(s, c), COL_DT // 128)),
                  pl.BlockSpec((1, D_SSD, D_STATE), lambda s, c: (rev(s, c), 0, 0)),
                  small, small, small,
                  pl.BlockSpec((1, D_SSD), lambda s, c: (0, 0))],
        out_specs=[pl.BlockSpec((CHUNK, CONV_DIM), lambda s, c: (rev(s, c), 0)),
                   pl.BlockSpec((CHUNK, N_INP), lambda s, c: (rev(s, c), 0)),
                   small, small, small,
                   pl.BlockSpec((1, D_SSD), lambda s, c: (0, 0))],
        out_shape=[jax.ShapeDtypeStruct((t, CONV_DIM), F32), jax.ShapeDtypeStruct((t, N_INP), BF16),
                   jax.ShapeDtypeStruct((1, SSD_HEADS), F32), jax.ShapeDtypeStruct((1, SSD_HEADS), F32),
                   jax.ShapeDtypeStruct((1, SSD_HEADS), F32), jax.ShapeDtypeStruct((1, D_SSD), F32)],
        scratch_shapes=[pltpu.VMEM((D_SSD, D_STATE), F32)],
        compiler_params=_cp("arbitrary", "arbitrary"),
    )(dy, xbc, proj, proj, hprev, dtb, alog, dskip, ng)


def _gmlp_chunk(gu, gv, ws, bs_cols, vg, og):
    n = gu[0].shape[0]
    mask = _tri(n, True)
    au = [_gelu(t) for t in gu]
    av = [_gelu(t) for t in gv]
    r = lax.rsqrt(sum(jnp.sum(t * t, axis=1, keepdims=True) for t in av) * (1.0 / D_GM) + EPS)
    p = []
    for h in range(GM_HEADS):
        sv = _bdot(ws[h] * mask, av[h] * r * vg[h]) + bs_cols[h]
        p.append(au[h] * sv)
    r2 = lax.rsqrt(sum(jnp.sum(t * t, axis=1, keepdims=True) for t in p) * (1.0 / D_GM) + EPS)
    return [p[h] * r2 * og[h] for h in range(GM_HEADS)]


def _gmlp_load(u_ref, v_ref, ws_ref, bst_ref, vg_ref, og_ref):
    gu = _hslices(u_ref, GM_HEAD_DIM, GM_HEADS)
    gv = _hslices(v_ref, GM_HEAD_DIM, GM_HEADS)
    ws = [ws_ref[h] for h in range(GM_HEADS)]
    bs_cols = [bst_ref[:, h:h + 1] for h in range(GM_HEADS)]
    return gu, gv, ws, bs_cols, _hslices(vg_ref, GM_HEAD_DIM, GM_HEADS), _hslices(og_ref, GM_HEAD_DIM, GM_HEADS)


def _gmlp_specs():
    return [pl.BlockSpec((CHUNK, D_GM), lambda i: (i, COL_U // D_GM)),
            pl.BlockSpec((CHUNK, D_GM), lambda i: (i, COL_V // D_GM)),
            pl.BlockSpec((GM_HEADS, CHUNK, CHUNK), lambda i: (0, 0, 0)),
            pl.BlockSpec((CHUNK, GM_HEADS), lambda i: (0, 0)),
            pl.BlockSpec((1, D_GM), lambda i: (0, 0)),
            pl.BlockSpec((1, D_GM), lambda i: (0, 0))]


def _gmlp_fwd(proj, ycat, ws, bst, vg, og):
    t = proj.shape[0]

    def body(u_ref, v_ref, ws_ref, bst_ref, vg_ref, og_ref, ycat_ref, o_ref):
        del ycat_ref
        outs = _gmlp_chunk(*_gmlp_load(u_ref, v_ref, ws_ref, bst_ref, vg_ref, og_ref))
        for h in range(GM_HEADS):
            o_ref[:, h * GM_HEAD_DIM:(h + 1) * GM_HEAD_DIM] = _b(outs[h])

    return pl.pallas_call(
        body, name="gmlp_fwd", grid=(t // CHUNK,),
        in_specs=_gmlp_specs() + [ANY],
        out_specs=pl.BlockSpec((CHUNK, D_GM), lambda i: (i, D_SSD // D_GM)),
        out_shape=jax.ShapeDtypeStruct(ycat.shape, ycat.dtype),
        input_output_aliases={6: 0},
        compiler_params=_cp("parallel"),
    )(proj, proj, ws, bst, vg, og, ycat)


def _gmlp_bwd(dy, proj, ws, bst, vg, og, dproj):
    t = proj.shape[0]
    w = GM_HEAD_DIM

    def body(dy_ref, u_ref, v_ref, ws_ref, bst_ref, vg_ref, og_ref, dproj_ref,
             dgm_ref, dws_ref, dbst_ref, dvg_ref, dog_ref):
        del dproj_ref

        @pl.when(pl.program_id(0) == 0)
        def _():
            dws_ref[...] = jnp.zeros_like(dws_ref)
            dbst_ref[...] = jnp.zeros_like(dbst_ref)
            dvg_ref[...] = jnp.zeros_like(dvg_ref)
            dog_ref[...] = jnp.zeros_like(dog_ref)

        _, vjp = jax.vjp(_gmlp_chunk, *_gmlp_load(u_ref, v_ref, ws_ref, bst_ref, vg_ref, og_ref))
        dgu, dgv, dws, dbs, dvg, dog = vjp(_hslices(dy_ref, w, GM_HEADS))
        for h in range(GM_HEADS):
            dgm_ref[:, h * w:(h + 1) * w] = _b(dgu[h])
            dgm_ref[:, D_GM + h * w: D_GM + (h + 1) * w] = _b(dgv[h])
            dws_ref[h] += dws[h]
            dbst_ref[:, h:h + 1] += dbs[h]
            dvg_ref[:, h * w:(h + 1) * w] += dvg[h]
            dog_ref[:, h * w:(h + 1) * w] += dog[h]

    return pl.pallas_call(
        body, name="gmlp_bwd", grid=(t // CHUNK,),
        in_specs=[pl.BlockSpec((CHUNK, D_GM), lambda i: (i, 1))] + _gmlp_specs() + [ANY],
        out_specs=[pl.BlockSpec((CHUNK, 2 * D_GM), lambda i: (i, COL_U // (2 * D_GM))),
                   pl.BlockSpec((GM_HEADS, CHUNK, CHUNK), lambda i: (0, 0, 0)),
                   pl.BlockSpec((CHUNK, GM_HEADS), lambda i: (0, 0)),
                   pl.BlockSpec((1, D_GM), lambda i: (0, 0)),
                   pl.BlockSpec((1, D_GM), lambda i: (0, 0))],
        out_shape=[jax.ShapeDtypeStruct(dproj.shape, dproj.dtype), jax.ShapeDtypeStruct((GM_HEADS, CHUNK, CHUNK), F32),
                   jax.ShapeDtypeStruct((CHUNK, GM_HEADS), F32), jax.ShapeDtypeStruct((1, D_GM), F32),
                   jax.ShapeDtypeStruct((1, D_GM), F32)],
        input_output_aliases={7: 0},
        compiler_params=_cp("arbitrary"),
    )(dy, proj, proj, ws, bst, vg, og, dproj)


def _local_step(x, target, mods, lw, final_g, *, nseq, big_w, grad_sink):
    saved = []
    xin, delta, gate = x, None, None
    for l in range(DEPTH):
        w = lw[l]
        sh1, sc1, g1, sh2, sc2, g2 = mods[l]
        x0, h1 = _normmod_fwd(xin, delta, gate, w["norm1_g"], sc1, sh1, nseq=nseq, name=f"norm1_fwd_{l}")
        w_in = big_w(l, "w_in", h1)
        proj = _matmul(h1, w_in, tb=True, name=f"mm_in_{l}")
        xbc = _ssd_conv_fwd(proj, w["ssd_conv_w"], w["ssd_conv_b"], nseq=nseq)
        ycat, hprev = _ssd_fwd(xbc, proj, w["ssd_dt_bias"], w["ssd_a_log"], w["ssd_d"], w["ssd_norm_g"], nseq=nseq)
        ycat = _gmlp_fwd(proj, ycat, w["gm_ws"], w["gm_bst"], w["gm_vnorm_g"], w["gm_out_g"])
        w_out = big_w(l, "w_out", ycat)
        mix = _matmul(ycat, w_out, name=f"mm_out_{l}")
        x1, h2 = _normmod_fwd(x0, mix, g1, w["norm2_g"], sc2, sh2, nseq=nseq, name=f"norm2_fwd_{l}")
        ff_up = big_w(l, "ff_up", h2)
        up = _matmul(h2, ff_up, tb=True, name=f"mm_up_{l}")
        act = _ffn_act_fwd(up, w["ff_conv_w"], w["ff_conv_b"], nseq=nseq)
        ff_down = big_w(l, "ff_down", act)
        dn = _matmul(act, ff_down, name=f"mm_down_{l}")
        saved.append(dict(x0=x0, xin_delta=delta, xin_gate=gate, h1=h1, proj=proj, xbc=xbc, hprev=hprev, ycat=ycat,
                          mix=mix, x1=x1, h2=h2, up=up, act=act, dn=dn,
                          w_in=w_in, w_out=w_out, ff_up=ff_up, ff_down=ff_down))
        xin, delta, gate = x1, dn, g2

    loss, dx, ddelta, dgate, dfg = _final_loss(xin, delta, gate, final_g, target, nseq=nseq)

    small, dmods = [None] * DEPTH, [None] * DEPTH
    for l in reversed(range(DEPTH)):
        w, sv = lw[l], saved[l]
        sh1, sc1, g1, sh2, sc2, g2 = mods[l]
        dg2 = dgate
        g_ff_down = _matmul(sv["act"], ddelta, ta=True, name=f"mm_down_dw_{l}", out_dtype=BF16)
        dact = _matmul(ddelta, sv["ff_down"], tb=True, name=f"mm_down_dx_{l}")
        dgate_ff, dval_ff, dfcw, dfcb = _ffn_act_bwd(dact, sv["up"], w["ff_conv_w"], w["ff_conv_b"], nseq=nseq)
        g_ff_up = _matmul([dgate_ff, dval_ff], sv["h2"], ta=True, name=f"mm_up_dw_{l}", out_dtype=BF16)
        dep = grad_sink(l, "ffn", dict(ff_down=g_ff_down, ff_up=g_ff_up), dval_ff)
        dh2 = _matmul([dgate_ff, dval_ff], sv["ff_up"], name=f"mm_up_dx_{l}", dep=dep)
        dx, dmix, dg1, dn2g, dsc2, dsh2 = _normmod_bwd(dh2, dx, sv["x1"], sv["mix"], g1, w["norm2_g"], sc2,
                                                       nseq=nseq, name=f"norm2_bwd_{l}")
        g_w_out = _matmul(sv["ycat"], dmix, ta=True, name=f"mm_out_dw_{l}", out_dtype=BF16)
        dep = grad_sink(l, "w_out", dict(w_out=g_w_out), dmix)
        dycat = _matmul(dmix, sv["w_out"], tb=True, name=f"mm_out_dx_{l}", dep=dep)
        dxbc_act, dproj, ddtb, dalog, ddsk, dng = _ssd_bwd(dycat, sv["xbc"], sv["proj"], sv["hprev"], w["ssd_dt_bias"],
                                                          w["ssd_a_log"], w["ssd_d"], w["ssd_norm_g"], nseq=nseq)
        dproj, dscw, dscb = _ssd_conv_bwd(dxbc_act, sv["proj"], w["ssd_conv_w"], w["ssd_conv_b"], dproj, nseq=nseq)
        dproj, dws, dbst, dvg, dog = _gmlp_bwd(dycat, sv["proj"], w["gm_ws"], w["gm_bst"], w["gm_vnorm_g"], w["gm_out_g"], dproj)
        g_w_in = _matmul(dproj, sv["h1"], ta=True, name=f"mm_in_dw_{l}", out_dtype=BF16)
        dep = grad_sink(l, "w_in", dict(w_in=g_w_in), dproj)
        dh1 = _matmul(dproj, sv["w_in"], name=f"mm_in_dx_{l}", dep=dep)
        dx, ddelta, dgate, dn1g, dsc1, dsh1 = _normmod_bwd(dh1, dx, sv["x0"], sv["xin_delta"], sv["xin_gate"],
                                                           w["norm1_g"], sc1, nseq=nseq, name=f"norm1_bwd_{l}")
        small[l] = dict(norm1_g=dn1g, norm2_g=dn2g, ssd_norm_g=dng, gm_vnorm_g=dvg, gm_out_g=dog,
                        ssd_conv_w=dscw, ssd_conv_b=dscb, ff_conv_w=dfcw, ff_conv_b=dfcb,
                        ssd_dt_bias=ddtb, ssd_a_log=dalog, ssd_d=ddsk, gm_ws=dws, gm_bs=dbst.T)
        dmods[l] = jnp.concatenate([dsh1, dsc1, dg1, dsh2, dsc2, dg2], axis=-1)[:, 0, :]
    return loss, dx, small, dmods, dfg


def _all_gather(arrs, name, dep=None):
    n = len(arrs)
    extra = [] if dep is None else [dep]

    def body(*refs):
        ins, outs = refs[:n], refs[n + len(extra):2 * n + len(extra)]
        send_sems, recv_sems, local_sems = refs[2 * n + len(extra):]
        x, y, c = lax.axis_index("x"), lax.axis_index("y"), lax.axis_index("c")
        me, sibling = (x, y, c), (x, y, 1 - c)
        chips = [(1 - x, y), (x, 1 - y), (1 - x, 1 - y)]

        def copy(i, k, block, to, src=None):
            px, py, pc = block
            dst = outs[i].at[4 * px + 2 * py + pc]
            return pltpu.make_async_remote_copy(
                src_ref=dst if src is None else src, dst_ref=dst,
                send_sem=send_sems.at[7 * i + k], recv_sem=recv_sems.at[7 * i + k],
                device_id=to, device_id_type=MESH)

        mine = [pltpu.make_async_copy(ins[i], outs[i].at[4 * x + 2 * y + c], local_sems.at[i]) for i in range(n)]
        for cp in mine:
            cp.start()
        first = []
        for i in range(n):
            first.append(copy(i, 0, me, sibling, src=ins[i]))
            first += [copy(i, 1 + j, me, (*chip, c), src=ins[i]) for j, chip in enumerate(chips)]
        for cp in first:
            cp.start()
        passed = []
        for j, chip in enumerate(chips):
            for i in range(n):
                copy(i, 1 + j, (*chip, c), me).wait_recv()
                fwd = copy(i, 4 + j, (*chip, c), sibling)
                fwd.start()
                passed.append(fwd)
        for i in range(n):
            copy(i, 0, sibling, me).wait_recv()
            for j, chip in enumerate(chips):
                copy(i, 4 + j, (*chip, 1 - c), me).wait_recv()
        for cp in first + passed:
            cp.wait_send()
        for cp in mine:
            cp.wait()

    return pl.pallas_call(
        body, name=name,
        in_specs=[ANY] * (n + len(extra)), out_specs=[ANY] * n,
        out_shape=[jax.ShapeDtypeStruct((N_DEV,) + a.shape, a.dtype) for a in arrs],
        scratch_shapes=[pltpu.SemaphoreType.DMA((7 * n,)), pltpu.SemaphoreType.DMA((7 * n,)),
                        pltpu.SemaphoreType.DMA((n,))],
    )(*arrs, *extra)


def _exchange_sibling(arrs, name):
    n = len(arrs)

    def body(*refs):
        ins, outs = refs[:n], refs[n:2 * n]
        send_sems, recv_sems = refs[2 * n:]
        x, y, c = lax.axis_index("x"), lax.axis_index("y"), lax.axis_index("c")
        copies = []
        for i in range(n):
            for k in range(4):
                copies.append(pltpu.make_async_remote_copy(
                    src_ref=ins[i].at[2 * k + (1 - c)], dst_ref=outs[i].at[k],
                    send_sem=send_sems.at[4 * i + k], recv_sem=recv_sems.at[4 * i + k],
                    device_id=(x, y, 1 - c), device_id_type=MESH))
        for cp in copies:
            cp.start()
        for cp in copies:
            cp.wait_recv()
        for cp in copies:
            cp.wait_send()

    return pl.pallas_call(
        body, name=name,
        in_specs=[ANY] * n, out_specs=[ANY] * n,
        out_shape=[jax.ShapeDtypeStruct((4,) + a.shape[1:], a.dtype) for a in arrs],
        scratch_shapes=[pltpu.SemaphoreType.DMA((4 * n,)), pltpu.SemaphoreType.DMA((4 * n,))],
    )(*arrs)


def _exchange_chips(arrs, name):
    n = len(arrs)

    def body(*refs):
        ins, outs = refs[:n], refs[n:2 * n]
        send_sems, recv_sems = refs[2 * n:]
        x, y, c = lax.axis_index("x"), lax.axis_index("y"), lax.axis_index("c")
        chips = [(1 - x, y), (x, 1 - y), (1 - x, 1 - y)]
        copies = []
        for i in range(n):
            for j, (cx, cy) in enumerate(chips):
                copies.append(pltpu.make_async_remote_copy(
                    src_ref=ins[i].at[2 * cx + cy], dst_ref=outs[i].at[j],
                    send_sem=send_sems.at[3 * i + j], recv_sem=recv_sems.at[3 * i + j],
                    device_id=(cx, cy, c), device_id_type=MESH))
        for cp in copies:
            cp.start()
        for cp in copies:
            cp.wait_recv()
        for cp in copies:
            cp.wait_send()

    return pl.pallas_call(
        body, name=name,
        in_specs=[ANY] * n, out_specs=[ANY] * n,
        out_shape=[jax.ShapeDtypeStruct((3,) + a.shape[1:], a.dtype) for a in arrs],
        scratch_shapes=[pltpu.SemaphoreType.DMA((3 * n,)), pltpu.SemaphoreType.DMA((3 * n,))],
    )(*arrs)


def _add_sibling(a, r, pos, name):
    _, depth, rows, cols = a.shape
    tr = _tile(rows, 256) if rows % 8 == 0 else rows
    a3 = a.reshape(N_DEV * depth, rows, cols)
    r3 = r.reshape(4 * depth, rows, cols)

    def body(pos_ref, a_ref, r_ref, o_ref):
        o_ref[...] = a_ref[...] + r_ref[...]

    out = pl.pallas_call(
        body, name=name,
        grid_spec=pltpu.PrefetchScalarGridSpec(
            num_scalar_prefetch=1, grid=(4 * depth, rows // tr),
            in_specs=[pl.BlockSpec((1, tr, cols), lambda q, i, p: ((2 * (q // depth) + p[0]) * depth + q % depth, i, 0)),
                      pl.BlockSpec((1, tr, cols), lambda q, i, p: (q, i, 0))],
            out_specs=pl.BlockSpec((1, tr, cols), lambda q, i, p: (q, i, 0))),
        out_shape=jax.ShapeDtypeStruct((4 * depth, rows, cols), F32),
        compiler_params=_cp("parallel", "parallel"),
    )(pos, a3, r3)
    return out.reshape(4, depth, rows, cols)


HBM = pl.BlockSpec(memory_space=pltpu.HBM)
SEM = pl.BlockSpec(memory_space=pltpu.SEMAPHORE)
EFFECT = pltpu.SideEffectType.DATAFLOW_SIDE_EFFECTING


def _peer(k):
    x, y, c = lax.axis_index("x"), lax.axis_index("y"), lax.axis_index("c")
    return (1 - x if k & 4 else x, 1 - y if k & 2 else y, 1 - c if k & 1 else c)


def _xc_copies(scatter, srcs, lands, send_sems, recv_sems):
    x, y, c = lax.axis_index("x"), lax.axis_index("y"), lax.axis_index("c")
    copies = []
    for i in range(len(srcs)):
        for k in range(1, N_DEV):
            px, py, pc = _peer(k)
            src = srcs[i].at[4 * px + 2 * py + pc] if scatter else srcs[i]
            dst = lands[i].at[k - 1] if scatter else lands[i].at[4 * x + 2 * y + c]
            copies.append(pltpu.make_async_remote_copy(
                src_ref=src, dst_ref=dst, send_sem=send_sems[i].at[k - 1], recv_sem=recv_sems[i].at[k - 1],
                device_id=(px, py, pc), device_id_type=MESH))
    return copies


def _xc_start(scatter, arrs, after, name):
    n = len(arrs)
    lands = [lax.empty((N_DEV - 1,) + a.shape[1:] if scatter else (N_DEV,) + a.shape, a.dtype) for a in arrs]

    def body(*refs):
        srcs, lnd = refs[:n], refs[n:2 * n]
        send_sems, recv_sems = refs[2 * n + 1:3 * n + 1], refs[3 * n + 1:4 * n + 1]
        token = refs[6 * n + 1]
        for cp in _xc_copies(scatter, srcs, lnd, send_sems, recv_sems):
            cp.start()
        token[...] = jnp.zeros_like(token)

    outs = pl.pallas_call(
        body, name=name,
        out_shape=[pltpu.SemaphoreType.DMA((N_DEV - 1,))] * (2 * n)
        + [pltpu.HBM(a.shape, a.dtype) for a in arrs] + [pltpu.HBM(a.shape, a.dtype) for a in lands]
        + [jax.ShapeDtypeStruct((8, 128), F32)],
        in_specs=[HBM] * (2 * n) + [ANY],
        out_specs=[SEM] * (2 * n) + [HBM] * (2 * n) + [pl.BlockSpec(memory_space=pltpu.VMEM)],
        input_output_aliases={i: 2 * n + i for i in range(2 * n)},
        compiler_params=pltpu.CompilerParams(has_side_effects=EFFECT),
    )(*[pltpu.with_memory_space_constraint(a, pltpu.HBM) for a in list(arrs) + lands], after)
    return outs[:n], outs[n:2 * n], outs[2 * n:3 * n], outs[3 * n:4 * n], outs[4 * n][0, 0]


def _xc_wait(scatter, send_sems, recv_sems, srcs, lands, after, name):
    n = len(srcs)

    def body(*refs):
        s_refs, l_refs = refs[:n], refs[n:2 * n]
        ss, rs = refs[2 * n:3 * n], refs[3 * n:4 * n]
        for cp in _xc_copies(scatter, s_refs, l_refs, ss, rs):
            cp.wait_send()
            cp.wait_recv()

    outs = pl.pallas_call(
        body, name=name,
        out_shape=[pltpu.HBM(a.shape, a.dtype) for a in list(srcs) + list(lands)],
        in_specs=[HBM] * (2 * n) + [SEM] * (2 * n) + [ANY],
        out_specs=[HBM] * (2 * n),
        input_output_aliases={i: i for i in range(2 * n)},
        compiler_params=pltpu.CompilerParams(has_side_effects=EFFECT),
    )(*srcs, *lands, *send_sems, *recv_sems, after)
    return outs[:n], outs[n:]


def _adamw_math(w, g, m, v):
    m = ADAM_B1 * m + (1.0 - ADAM_B1) * g
    v = ADAM_B2 * v + (1.0 - ADAM_B2) * (g * g)
    m_hat = m / (1.0 - ADAM_B1 ** ADAM_STEP)
    v_hat = v / (1.0 - ADAM_B2 ** ADAM_STEP)
    delta = -ADAM_LR * (m_hat / (jnp.sqrt(v_hat) + ADAM_EPS) + ADAM_WD * w)
    return delta, m, v


def _adamw_sharded(parts, w, m, v, pos, name):
    depth, rows, cols = w.shape
    tr = _tile(rows, 256) if rows % 8 == 0 else rows
    npart = len(parts)

    def body(pos_ref, *refs):
        prefs = refs[:npart]
        w_ref, m_ref, v_ref, g_out, d_out, m_out, v_out = refs[npart:]
        g = prefs[0][...]
        for pr in prefs[1:]:
            g = g + pr[...]
        delta, mn, vn = _adamw_math(w_ref[...], g, m_ref[...], v_ref[...])
        g_out[...] = g
        d_out[...] = delta
        m_out[...] = mn
        v_out[...] = vn

    def part_spec(fn):
        return pl.BlockSpec((1, tr, cols), lambda l, i, p: (fn(p) * depth + l, i, 0))

    blk = pl.BlockSpec((1, tr, cols), lambda l, i, p: (l, i, 0))
    shp = jax.ShapeDtypeStruct((depth, rows, cols), F32)
    return pl.pallas_call(
        body, name=name,
        grid_spec=pltpu.PrefetchScalarGridSpec(
            num_scalar_prefetch=1, grid=(depth, rows // tr),
            in_specs=[part_spec(fn) for _, fn in parts] + [blk, blk, blk],
            out_specs=[blk, blk, blk, blk]),
        out_shape=[shp, shp, shp, shp],
        compiler_params=_cp("parallel", "parallel"),
    )(pos, *[a for a, _ in parts], w, m, v)


def _adamw_layer(parts, w, m, v, pos, layer, prev, name):
    depth, rows, cols = w.shape
    npart = len(parts)
    nprev = 0 if prev is None else 4
    if rows % 16 == 0:
        tr, tc = max(t for t in range(16, 257, 16) if rows % t == 0), cols
    else:
        tr, tc = rows, _tile(cols, 256)
    pick = (lambda i: (i, 0)) if rows % 16 == 0 else (lambda i: (0, i))

    def body(pos_ref, *refs):
        prefs = refs[:npart]
        w_ref, m_ref, v_ref = refs[npart:npart + 3]
        g_out, d_out, m_out, v_out = refs[npart + 3 + nprev:]
        g = prefs[0][...].astype(F32)
        for pr in prefs[1:]:
            g = g + pr[...].astype(F32)
        delta, mn, vn = _adamw_math(w_ref[...], g, m_ref[...], v_ref[...])
        g_out[...] = g
        d_out[...] = delta
        m_out[...] = mn
        v_out[...] = vn

    def part_spec(fn):
        return pl.BlockSpec((1, tr, tc), lambda i, p: (fn(p), *pick(i)))

    blk = pl.BlockSpec((1, tr, tc), lambda i, p: (layer, *pick(i)))
    shp = jax.ShapeDtypeStruct((depth, rows, cols), F32)
    first_prev = 1 + npart + 3
    return pl.pallas_call(
        body, name=name,
        grid_spec=pltpu.PrefetchScalarGridSpec(
            num_scalar_prefetch=1, grid=(rows // tr * (cols // tc),),
            in_specs=[part_spec(fn) for _, fn in parts] + [blk, blk, blk] + [ANY] * nprev,
            out_specs=[blk, blk, blk, blk]),
        out_shape=[shp, shp, shp, shp],
        input_output_aliases={first_prev + j: j for j in range(nprev)},
        compiler_params=_cp("parallel"),
    )(pos, *[a for a, _ in parts], w, m, v, *(prev or ()))


_P1024 = ["norm1_g", "norm2_g", "ssd_norm_g", "gm_vnorm_g", "gm_out_g"]
_P16 = ["ssd_dt_bias", "ssd_a_log", "ssd_d"]


def _adamw_small(gath, wmv):
    names = list(wmv.keys())
    classes = list(gath.keys())
    flat_in = [gath[k] for k in classes]
    for nme in names:
        flat_in += list(wmv[nme])
    out_shapes = []
    for nme in names:
        out_shapes += [jax.ShapeDtypeStruct(wmv[nme][0].shape, F32)] * 4
    out_shapes += [jax.ShapeDtypeStruct((DEPTH, SSD_CONV, CONV_DIM), F32), jax.ShapeDtypeStruct((DEPTH, FF_CONV, D_FF), F32),
                   jax.ShapeDtypeStruct((1, SSD_HEADS), F32)]
    scratch = [pltpu.VMEM(gath[k].shape[1:], F32) for k in classes]
    ncls = len(classes)

    def body(*refs):
        g_refs = dict(zip(classes, refs[:ncls]))
        pos = ncls
        w_refs = {}
        for nme in names:
            w_refs[nme] = refs[pos:pos + 3]
            pos += 3
        o_refs = {}
        for nme in names:
            o_refs[nme] = refs[pos:pos + 4]
            pos += 4
        scw_out, fcw_out, loss_out = refs[pos], refs[pos + 1], refs[pos + 2]
        s_refs = dict(zip(classes, refs[pos + 3:]))
        for k in classes:
            acc = g_refs[k][0]
            for dev in range(1, N_DEV):
                acc = acc + g_refs[k][dev]
            s_refs[k][...] = acc

        def apply(nme, grad_of):
            w_ref, m_ref, v_ref = w_refs[nme]
            g_out, d_out, m_out, v_out = o_refs[nme]
            shape = w_ref.shape
            if len(shape) == 2:
                idxs = [(slice(l, l + 1),) for l in range(shape[0])]
            elif len(shape) == 3:
                idxs = [(l,) for l in range(shape[0])]
            else:
                idxs = [(l, h) for l in range(shape[0]) for h in range(shape[1])]
            for n_i, ix in enumerate(idxs):
                g = grad_of(n_i)
                delta, mn, vn = _adamw_math(w_ref[ix], g, m_ref[ix], v_ref[ix])
                g_out[ix] = g
                d_out[ix] = delta
                m_out[ix] = mn
                v_out[ix] = vn

        s1024, s1536, s2816, s16, s128, s6144 = (s_refs[k] for k in classes)
        for n_i, nme in enumerate(_P1024):
            apply(nme, lambda l, b=2 * n_i: s1024[b + l:b + l + 1, :])
        apply("final_g", lambda l: s1024[10:11, :])
        apply("ssd_conv_b", lambda l: s1536[8 + l:9 + l, :])
        apply("ff_conv_b", lambda l: s2816[6 + l:7 + l, :])
        for n_i, nme in enumerate(_P16):
            apply(nme, lambda l, b=2 * n_i: s16[b + l:b + l + 1, :])
        apply("gm_ws", lambda q: s128[q * CHUNK:(q + 1) * CHUNK, :])
        apply("gm_bs", lambda l: s128[2048 + 8 * l:2048 + 8 * (l + 1), :])
        apply("ada_b", lambda l: s6144[2 * l:2 * l + 1, :] + s6144[2 * l + 1:2 * l + 2, :])
        for l in range(DEPTH):
            scw_out[l] = s1536[SSD_CONV * l:SSD_CONV * (l + 1), :]
            fcw_out[l] = s2816[FF_CONV * l:FF_CONV * (l + 1), :]
        loss_out[...] = s16[2 * len(_P16):2 * len(_P16) + 1, :]

    outs = pl.pallas_call(
        body, name="adamw_small",
        out_shape=out_shapes,
        scratch_shapes=scratch,
        compiler_params=pltpu.CompilerParams(vmem_limit_bytes=VMEM_LIMIT),
    )(*flat_in)
    res = {nme: tuple(outs[4 * i:4 * i + 4]) for i, nme in enumerate(names)}
    return res, outs[-3], outs[-2], outs[-1]


_WEIGHTS = ['ada_w', 'ada_b', 'norm1_g', 'norm2_g', 'w_in', 'ssd_conv_w', 'ssd_conv_b', 'ssd_dt_bias', 'ssd_a_log',
            'ssd_d', 'ssd_norm_g', 'gm_vnorm_g', 'gm_ws', 'gm_bs', 'gm_out_g', 'w_out', 'ff_up', 'ff_conv_w',
            'ff_conv_b', 'ff_down', 'final_g']


_O_XBC, _O_DT, _O_GM = D_SSD, D_SSD + CONV_DIM, D_SSD + CONV_DIM + SSD_HEADS


_TRANSPOSED = ("w_in", "ff_up")


def _full_weight(name, g):
    full = g.reshape(g.shape[0] * g.shape[1], g.shape[2])
    if name != "w_in":
        return full
    zpad = jnp.zeros((N_INP - N_IN, full.shape[1]), full.dtype)
    return jnp.concatenate([full[_O_GM:], full[:_O_XBC], full[_O_XBC:_O_DT], full[_O_DT:_O_GM], zpad], axis=0)


def _by_owner(name, grad):
    if name == "w_in":
        grad = jnp.concatenate([grad[COL_Z:COL_XBC], grad[COL_XBC:COL_DT], grad[COL_DT:COL_DT + SSD_HEADS], grad[:COL_Z]], axis=0)
    return grad.reshape(N_DEV, grad.shape[0] // N_DEV, grad.shape[1])


def kernel(x, c, ada_w, ada_b, norm1_g, norm2_g, w_in, ssd_conv_w, ssd_conv_b, ssd_dt_bias, ssd_a_log, ssd_d, ssd_norm_g, gm_vnorm_g, gm_ws, gm_bs, gm_out_g, w_out, ff_up, ff_conv_w, ff_conv_b, ff_down, final_g, loss_target, m_ada_w, m_ada_b, m_norm1_g, m_norm2_g, m_w_in, m_ssd_conv_w, m_ssd_conv_b, m_ssd_dt_bias, m_ssd_a_log, m_ssd_d, m_ssd_norm_g, m_gm_vnorm_g, m_gm_ws, m_gm_bs, m_gm_out_g, m_w_out, m_ff_up, m_ff_conv_w, m_ff_conv_b, m_ff_down, m_final_g, v_ada_w, v_ada_b, v_norm1_g, v_norm2_g, v_w_in, v_ssd_conv_w, v_ssd_conv_b, v_ssd_dt_bias, v_ssd_a_log, v_ssd_d, v_ssd_norm_g, v_gm_vnorm_g, v_gm_ws, v_gm_bs, v_gm_out_g, v_w_out, v_ff_up, v_ff_conv_w, v_ff_conv_b, v_ff_down, v_final_g):
    given = dict(locals())
    wts = {n: given[n] for n in _WEIGHTS}
    mom = {n: given["m_" + n] for n in _WEIGHTS}
    var = {n: given["v_" + n] for n in _WEIGHTS}
    nseq, seq, d = x.shape
    ix, iy, ic = lax.axis_index("x"), lax.axis_index("y"), lax.axis_index("c")
    me = 4 * ix + 2 * iy + ic
    me_arr = me.astype(jnp.int32).reshape(1)

    for nme in _TRANSPOSED:
        wts[nme], mom[nme], var[nme] = (jnp.transpose(a, (0, 2, 1)) for a in (wts[nme], mom[nme], var[nme]))

    def shard(l, name):
        return _b(wts[name][l])

    g_scw, g_fcw, c_all = _all_gather([ssd_conv_w, ff_conv_w, c], "gather_first")
    later = [(0, "w_in"), (0, "w_out"), (0, "ff_up"), (0, "ff_down"), (1, "w_in"), (1, "w_out"), (1, "ff_up"), (1, "ff_down")]
    ag_ssem, ag_rsem, ag_src, ag_land, ag_zero = _xc_start(False, [shard(l, n) for l, n in later], c_all, "ag_start")
    ag_groups = {(0, "w_in"): [0], (0, "w_out"): [1], (0, "ff_up"): [2, 3], (1, "w_in"): [4, 5], (1, "ff_up"): [6, 7]}
    scw_f = jnp.transpose(g_scw, (1, 2, 0, 3)).reshape(DEPTH, SSD_CONV, CONV_DIM)
    fcw_f = jnp.transpose(g_fcw, (1, 2, 0, 3)).reshape(DEPTH, FF_CONV, D_FF)
    c_all = c_all.reshape(N_DEV * nseq, d)

    n_ada = ada_w.shape[2]
    ada_b_shard = lax.dynamic_slice_in_dim(ada_b, me * n_ada, n_ada, axis=1).reshape(DEPTH, 1, n_ada)
    mod_part, c_act = _ada_fwd(c_all, ada_w, ada_b_shard)
    (mod_g,) = _all_gather([mod_part], "gather_mod")
    mod_all = jnp.transpose(mod_g, (1, 2, 0, 3)).reshape(DEPTH, N_DEV * nseq, N_MOD * d)
    mod_mine = lax.dynamic_slice_in_dim(mod_all, me * nseq, nseq, axis=1)
    mods = [[mod_mine[l, :, k * d:(k + 1) * d].reshape(nseq, 1, d) for k in range(N_MOD)] for l in range(DEPTH)]

    big_cache = {}

    def big_w(l, name, after):
        if (l, name) not in big_cache:
            idx = ag_groups[(l, name)]
            pick = lambda seq_: [seq_[i] for i in idx]
            srcs, lands = _xc_wait(False, pick(ag_ssem), pick(ag_rsem), pick(ag_src), pick(ag_land), after,
                                   f"ag_wait_{l}_{name}")
            for i, src, land in zip(idx, srcs, lands):
                big_cache[later[i]] = _full_weight(later[i][1], lax.dynamic_update_index_in_dim(land, src, me, 0))
        return big_cache[(l, name)]

    lw = []
    for l in range(DEPTH):
        lw.append(dict(
            norm1_g=norm1_g[l:l + 1] + (ag_zero if l == 0 else 0.0), norm2_g=norm2_g[l:l + 1], ssd_conv_w=scw_f[l],
            ssd_conv_b=ssd_conv_b[l:l + 1], ssd_dt_bias=ssd_dt_bias[l:l + 1], ssd_a_log=ssd_a_log[l:l + 1],
            ssd_d=ssd_d[l:l + 1], ssd_norm_g=ssd_norm_g[l:l + 1], gm_vnorm_g=gm_vnorm_g[l:l + 1], gm_ws=gm_ws[l],
            gm_bst=gm_bs[l].T, gm_out_g=gm_out_g[l:l + 1], ff_conv_w=fcw_f[l], ff_conv_b=ff_conv_b[l:l + 1]))

    outs = {}
    pending = {}

    def rs_finish(l, group, after):
        names, ssem, rsem, srcs, lands = pending.pop((l, group))
        srcs, lands = _xc_wait(True, ssem, rsem, srcs, lands, after, f"rs_wait_{l}_{group}")
        for nme, own, land in zip(names, srcs, lands):
            parts = [(own, lambda p: p[0])] + [(land, lambda p, k=k: k) for k in range(N_DEV - 1)]
            outs[nme] = _adamw_layer(parts, wts[nme], mom[nme], var[nme], me_arr, l, outs.get(nme), f"adamw_{nme}_{l}")
        return outs[names[-1]][0]

    def grad_sink(l, group, grads, after):
        names = list(grads)
        ssem, rsem, srcs, lands, zero = _xc_start(True, [_by_owner(n, grads[n]) for n in names], after, f"rs_start_{l}_{group}")
        pending[(l, group)] = (names, ssem, rsem, srcs, lands)
        return zero.reshape(1, 1)

    loss_p, grad_x, small, dmods, dfg = _local_step(
        x.reshape(nseq * seq, d), loss_target.reshape(nseq * seq, d), mods, lw, final_g.reshape(1, d), nseq=nseq,
        big_w=big_w, grad_sink=grad_sink)

    def rows(name):
        return [small[l][name] for l in range(DEPTH)]

    p1024 = jnp.concatenate(sum([rows(n) for n in _P1024], []) + [dfg], axis=0)
    p1536 = jnp.concatenate(rows("ssd_conv_w") + rows("ssd_conv_b"), axis=0)
    p2816 = jnp.concatenate(rows("ff_conv_w") + rows("ff_conv_b"), axis=0)
    p16 = jnp.concatenate(sum([rows(n) for n in _P16], []) + [loss_p[:, :SSD_HEADS]], axis=0)
    p128 = jnp.concatenate([small[l]["gm_ws"].reshape(GM_HEADS * CHUNK, CHUNK) for l in range(DEPTH)] + rows("gm_bs"), axis=0)
    p6144 = jnp.concatenate(dmods, axis=0)
    done = grad_x
    for l, grp in ((1, "ffn"), (1, "w_out"), (1, "w_in"), (0, "ffn"), (0, "w_out")):
        done = rs_finish(l, grp, done)
    gathered = _all_gather([p1024, p1536, p2816, p16, p128, p6144], "gather_small", dep=done)
    gath = dict(zip(["p1024", "p1536", "p2816", "p16", "p128", "p6144"], gathered))

    dmod_all = jnp.transpose(gath["p6144"].reshape(N_DEV, DEPTH, nseq, N_MOD * d), (1, 0, 2, 3)).reshape(
        DEPTH, N_DEV * nseq, N_MOD * d)
    small_names = _P1024 + ["final_g", "ssd_conv_b", "ff_conv_b"] + _P16 + ["gm_ws", "gm_bs", "ada_b"]
    wmv = {}
    for nme in small_names:
        if nme == "final_g":
            wmv[nme] = tuple(a.reshape(1, d) for a in (wts[nme], mom[nme], var[nme]))
        else:
            wmv[nme] = (wts[nme], mom[nme], var[nme])
    small_out, scw_full, fcw_full, loss_sum = _adamw_small(gath, wmv)
    loss = loss_sum[0, 0]
    rs_finish(0, "w_in", scw_full)
    for nme in small_names:
        outs[nme] = small_out[nme]
    outs["final_g"] = tuple(a.reshape(d) for a in outs["final_g"])

    n_scw, n_fcw = ssd_conv_w.shape[2], ff_conv_w.shape[2]
    g_scw_mine = lax.dynamic_slice_in_dim(scw_full, me * n_scw, n_scw, axis=2)
    g_fcw_mine = lax.dynamic_slice_in_dim(fcw_full, me * n_fcw, n_fcw, axis=2)
    outs["ssd_conv_w"] = _adamw_sharded([(g_scw_mine, lambda p: 0)], ssd_conv_w, m_ssd_conv_w, v_ssd_conv_w, me_arr, "adamw_ssd_conv_w")
    outs["ff_conv_w"] = _adamw_sharded([(g_fcw_mine, lambda p: 0)], ff_conv_w, m_ff_conv_w, v_ff_conv_w, me_arr, "adamw_ff_conv_w")

    dmod_cols = _b(lax.dynamic_slice_in_dim(dmod_all, me * n_ada, n_ada, axis=2))
    g_ada = jnp.stack([_matmul(c_act, dmod_cols[l], ta=True, name=f"mm_ada_dw_{l}") for l in range(DEPTH)])
    outs["ada_w"] = _adamw_sharded([(g_ada, lambda p: 0)], ada_w, m_ada_w, v_ada_w, me_arr, "adamw_ada_w")

    for nme in _TRANSPOSED:
        outs[nme] = tuple(jnp.transpose(a, (0, 2, 1)) for a in outs[nme])
    result = [loss, grad_x.reshape(nseq, seq, d)]
    for k in range(4):
        result += [outs[n][k] for n in _WEIGHTS]
    return tuple(result)
```

```python
import functools
import math

import jax
import jax.numpy as jnp
from jax import lax
from jax.experimental import pallas as pl
from jax.experimental.pallas import tpu as pltpu

F32 = jnp.float32
BF16 = jnp.bfloat16

N_DEV = 8
D_MODEL = 1024
DEPTH = 2
CHUNK = 128
SSD_HEADS = 16
SSD_HEAD_DIM = 64
SSD_GROUPS = 2
HEADS_PER_GROUP = SSD_HEADS // SSD_GROUPS
GROUP_WIDTH = HEADS_PER_GROUP * SSD_HEAD_DIM
D_STATE = 128
D_SSD = 1024
CONV_DIM = 1536
SSD_CONV = 4
GM_HEADS = 8
GM_HEAD_DIM = 128
D_GM = 1024
D_FF = 2816
FF_CONV = 3
N_IN = 4624
N_MOD = 6
EPS = 1e-6

N_INP = 5120
COL_U, COL_V, COL_Z, COL_XBC, COL_DT = 0, 1024, 2048, 3072, 4608
DT_BLOCK = 512

ADAM_LR = 0.001
ADAM_B1 = 0.9
ADAM_B2 = 0.999
ADAM_EPS = 1e-08
ADAM_WD = 0.01
ADAM_STEP = 10

VMEM_LIMIT = 56 * 1024 * 1024
MESH = pl.DeviceIdType.MESH
ANY = pl.BlockSpec(memory_space=pl.ANY)


def _cp(*sem):
    return pltpu.CompilerParams(dimension_semantics=sem, vmem_limit_bytes=VMEM_LIMIT)


def _tile(n, pref):
    if n <= pref or n % 128:
        return n
    best = 128
    for t in range(128, pref + 1, 128):
        if n % t == 0:
            best = t
    return best


def _silu(x):
    return x * jax.nn.sigmoid(x)


def _gelu(x):
    return 0.5 * x * (1.0 + lax.erf(x * (1.0 / math.sqrt(2.0))))


def _softplus(x):
    return jnp.maximum(x, 0.0) + jnp.log1p(jnp.exp(-jnp.abs(x)))


def _rms(x, g, width):
    return x * lax.rsqrt(jnp.sum(x * x, axis=-1, keepdims=True) / width + EPS) * g


def _b(x):
    return x.astype(BF16)


_NN = (((1,), (0,)), ((), ()))
_NT = (((1,), (1,)), ((), ()))
_TN = (((0,), (0,)), ((), ()))


def _dg(a, b, dn):
    return lax.dot_general(_b(a), _b(b), dn, preferred_element_type=F32)


@jax.custom_vjp
def _bdot(a, b):
    return _dg(a, b, _NN)


def _bdot_fwd(a, b):
    return _dg(a, b, _NN), (a, b)


def _bdot_bwd(res, ct):
    a, b = res
    return _dg(ct, b, _NT), _dg(a, ct, _TN)


_bdot.defvjp(_bdot_fwd, _bdot_bwd)


@jax.custom_vjp
def _bdot_nt(a, b):
    return _dg(a, b, _NT)


def _bdot_nt_fwd(a, b):
    return _dg(a, b, _NT), (a, b)


def _bdot_nt_bwd(res, ct):
    a, b = res
    return _dg(ct, b, _NN), _dg(ct, a, _TN)


_bdot_nt.defvjp(_bdot_nt_fwd, _bdot_nt_bwd)


@jax.custom_vjp
def _bdot_tn(a, b):
    return _dg(a, b, _TN)


def _bdot_tn_fwd(a, b):
    return _dg(a, b, _TN), (a, b)


def _bdot_tn_bwd(res, ct):
    a, b = res
    return _dg(b, ct, _NT), _dg(a, ct, _NN)


_bdot_tn.defvjp(_bdot_tn_fwd, _bdot_tn_bwd)


def _tri(n, lower):
    r = lax.broadcasted_iota(jnp.int32, (n, n), 0)
    c = lax.broadcasted_iota(jnp.int32, (n, n), 1)
    return ((r >= c) if lower else (r <= c)).astype(F32)


def _eye(n):
    r = lax.broadcasted_iota(jnp.int32, (n, n), 0)
    c = lax.broadcasted_iota(jnp.int32, (n, n), 1)
    return (r == c).astype(F32)


def _hdot(a, b, dn):
    return lax.dot_general(a, b, dn, precision=lax.Precision.HIGHEST, preferred_element_type=F32)


@jax.custom_vjp
def _cumsum_rows(x):
    return _hdot(_tri(x.shape[0], True), x, _NN)


def _cumsum_rows_fwd(x):
    return _cumsum_rows(x), None


def _cumsum_rows_bwd(_, ct):
    return (_hdot(_tri(ct.shape[0], False), ct, _NN),)


_cumsum_rows.defvjp(_cumsum_rows_fwd, _cumsum_rows_bwd)


@jax.custom_vjp
def _transpose(x):
    return _hdot(_eye(x.shape[1]), x, _NT)


def _transpose_fwd(x):
    return _transpose(x), None


def _transpose_bwd(_, ct):
    return (_hdot(_eye(ct.shape[1]), ct, _NT),)


_transpose.defvjp(_transpose_fwd, _transpose_bwd)


MXU_WIDTH = 256
MATMUL_TILE_CAP = 2816
MATMUL_VMEM = 44 * 1024 * 1024


def _mxu_tiles(n):
    if n <= MATMUL_TILE_CAP or n % 128:
        return [n]
    for unit in (MXU_WIDTH, 128):
        opts = [t for t in range(unit, MATMUL_TILE_CAP + 1, unit) if n % t == 0]
        if opts:
            return opts
    return [n]


def _matmul(a, b, *, ta=False, tb=False, name, dep=None, out_dtype=F32):
    pieces = list(a) if isinstance(a, (list, tuple)) else [a]
    npc = len(pieces)
    rows, width = pieces[0].shape
    assert all(p.shape == (rows, width) for p in pieces)
    if ta:
        k_dim, m_dim = rows, width * npc
    else:
        m_dim, k_dim = rows, width * npc
    if tb:
        n_dim, kb = b.shape
    else:
        kb, n_dim = b.shape
    assert kb == k_dim, (pieces[0].shape, npc, b.shape, ta, tb)
    m_unit = width if npc > 1 and ta else m_dim
    k_unit = width if npc > 1 and not ta else k_dim
    tm = _tile(m_unit, 1536)
    tn_opts, tk_opts = _mxu_tiles(n_dim), _mxu_tiles(k_unit)
    tn, tk = tn_opts.pop(), tk_opts.pop()
    while 4 * (tm * tk + tk * tn) + 8 * tm * tn > MATMUL_VMEM:
        if tn >= tk and tn_opts:
            tn = tn_opts.pop()
        else:
            tk = tk_opts.pop()
    ni, nj, nk = m_dim // tm, n_dim // tn, k_dim // tk
    per = width // (tm if ta else tk)
    dn = (((0 if ta else 1,), (1 if tb else 0,)), ((), ()))

    a_bytes, b_bytes = m_dim * k_dim, k_dim * n_dim
    m_outer = nk > 1 or a_bytes + b_bytes * ni <= b_bytes + a_bytes * nj
    if m_outer:
        ij = lambda o, n, k: (o, n)
        grid = (ni, nj, nk)
    else:
        ij = lambda o, n, k: (n, o)
        grid = (nj, ni, nk)

    use_acc = nk > 1 and out_dtype != F32

    def body(*refs):
        a_refs, b_ref = refs[:npc], refs[npc]
        o_ref = refs[-2] if use_acc else refs[-1]
        acc_ref = refs[-1]
        k = pl.program_id(2)
        i = pl.program_id(0 if m_outer else 1)
        along = i if ta else k

        def step(a_ref):
            p = lax.dot_general(a_ref[...], b_ref[...], dn, preferred_element_type=F32)
            if nk == 1:
                o_ref[...] = p.astype(out_dtype)
            else:
                @pl.when(k == 0)
                def _():
                    acc_ref[...] = p

                @pl.when((k > 0) & (k < nk - 1 if use_acc else True))
                def _():
                    acc_ref[...] += p

                if use_acc:
                    @pl.when(k == nk - 1)
                    def _():
                        o_ref[...] = (acc_ref[...] + p).astype(out_dtype)

        if npc == 1:
            step(a_refs[0])
        else:
            for pc in range(npc):
                pl.when((along >= pc * per) & (along < (pc + 1) * per))(functools.partial(step, a_refs[pc]))

    def a_map(pc, o, n, k):
        i, _ = ij(o, n, k)
        along = i if ta else k
        if npc > 1:
            along = jnp.clip(along - pc * per, 0, per - 1)
        return (k, along) if ta else (i, along)

    def b_map(o, n, k):
        _, j = ij(o, n, k)
        return (j, k) if tb else (k, j)

    extra = [] if dep is None else [dep]
    return pl.pallas_call(
        body, name=name,
        grid=grid,
        in_specs=[pl.BlockSpec((tk, tm) if ta else (tm, tk), functools.partial(a_map, pc)) for pc in range(npc)]
        + [pl.BlockSpec((tn, tk) if tb else (tk, tn), b_map)] + [ANY] * len(extra),
        out_specs=pl.BlockSpec((tm, tn), lambda o, n, k: ij(o, n, k)),
        out_shape=jax.ShapeDtypeStruct((m_dim, n_dim), out_dtype),
        scratch_shapes=[pltpu.VMEM((tm, tn), F32)] if use_acc else [],
        compiler_params=_cp("parallel", "parallel", "arbitrary"),
    )(*pieces, b, *extra)


def _ada_fwd(c_all, ada_w, ada_b_shard):
    depth, d, n = ada_w.shape
    nb = c_all.shape[0]

    def body(c_ref, w_ref, b_ref, o_ref, ca_ref):
        ca = _silu(c_ref[...])
        ca_ref[...] = _b(ca)
        o_ref[0] = _dg(ca, w_ref[0], _NN) + b_ref[0]

    return pl.pallas_call(
        body, name="ada_fwd",
        grid=(depth,),
        in_specs=[pl.BlockSpec((nb, d), lambda l: (0, 0)),
                  pl.BlockSpec((1, d, n), lambda l: (l, 0, 0)),
                  pl.BlockSpec((1, 1, n), lambda l: (l, 0, 0))],
        out_specs=[pl.BlockSpec((1, nb, n), lambda l: (l, 0, 0)),
                   pl.BlockSpec((nb, d), lambda l: (0, 0))],
        out_shape=[jax.ShapeDtypeStruct((depth, nb, n), F32), jax.ShapeDtypeStruct((nb, d), BF16)],
        compiler_params=_cp("arbitrary"),
    )(c_all, ada_w, ada_b_shard)


def _normmod_f(x, g, sc, sh):
    return _rms(x, g, D_MODEL) * (1.0 + sc) + sh


def _row_tile(seq):
    return min(seq, 256)


def _normmod_fwd(xin, delta, gate, g, sc, sh, *, nseq, name):
    t, d = xin.shape
    seq = t // nseq
    tr = _row_tile(seq)
    nt = seq // tr
    has_delta = delta is not None
    row = pl.BlockSpec((tr, d), lambda s, i: (s * nt + i, 0))
    per_seq = pl.BlockSpec((1, 1, d), lambda s, i: (s, 0, 0))
    vec = pl.BlockSpec((1, d), lambda s, i: (0, 0))

    if has_delta:
        def body(xin_ref, delta_ref, gate_ref, g_ref, sc_ref, sh_ref, x_ref, h_ref):
            x = xin_ref[...] + gate_ref[0] * delta_ref[...]
            x_ref[...] = x
            h_ref[...] = _b(_normmod_f(x, g_ref[...], sc_ref[0], sh_ref[0]))

        return pl.pallas_call(
            body, name=name, grid=(nseq, nt),
            in_specs=[row, row, per_seq, vec, per_seq, per_seq],
            out_specs=[row, row],
            out_shape=[jax.ShapeDtypeStruct((t, d), F32), jax.ShapeDtypeStruct((t, d), BF16)],
            compiler_params=_cp("parallel", "parallel"),
        )(xin, delta, gate, g, sc, sh)

    def body0(xin_ref, g_ref, sc_ref, sh_ref, h_ref):
        h_ref[...] = _b(_normmod_f(xin_ref[...], g_ref[...], sc_ref[0], sh_ref[0]))

    h = pl.pallas_call(
        body0, name=name, grid=(nseq, nt),
        in_specs=[row, vec, per_seq, per_seq],
        out_specs=row,
        out_shape=jax.ShapeDtypeStruct((t, d), BF16),
        compiler_params=_cp("parallel", "parallel"),
    )(xin, g, sc, sh)
    return xin, h


def _normmod_bwd(dh, dxo, x, delta, gate, g, sc, *, nseq, name):
    t, d = x.shape
    seq = t // nseq
    tr = _row_tile(seq)
    nt = seq // tr
    has_delta = delta is not None
    row = pl.BlockSpec((tr, d), lambda s, i: (s * nt + i, 0))
    per_seq = pl.BlockSpec((1, 1, d), lambda s, i: (s, 0, 0))
    vec = pl.BlockSpec((1, d), lambda s, i: (0, 0))

    def core(dh_ref, dxo_ref, x_ref, g_ref, sc_ref, dx_ref, dg_ref, dsc_ref, dsh_ref):
        s, i = pl.program_id(0), pl.program_id(1)
        dh_v = dh_ref[...]
        _, vjp = jax.vjp(lambda xx, gg, ss: _normmod_f(xx, gg, ss, 0.0), x_ref[...], g_ref[...], sc_ref[0])
        dxn, dg_t, dsc_t = vjp(dh_v)
        dx = dxo_ref[...] + dxn
        dx_ref[...] = dx
        dsh_t = jnp.sum(dh_v, axis=0, keepdims=True)

        @pl.when((s == 0) & (i == 0))
        def _():
            dg_ref[...] = jnp.zeros_like(dg_ref)

        @pl.when(i == 0)
        def _():
            dsc_ref[...] = jnp.zeros_like(dsc_ref)
            dsh_ref[...] = jnp.zeros_like(dsh_ref)

        dg_ref[...] += dg_t
        dsc_ref[0] += dsc_t
        dsh_ref[0] += dsh_t
        return dx

    if has_delta:
        def body(dh_ref, dxo_ref, x_ref, delta_ref, gate_ref, g_ref, sc_ref,
                 dx_ref, dd_ref, dgate_ref, dg_ref, dsc_ref, dsh_ref):
            dx = core(dh_ref, dxo_ref, x_ref, g_ref, sc_ref, dx_ref, dg_ref, dsc_ref, dsh_ref)
            dd_ref[...] = _b(dx * gate_ref[0])

            @pl.when(pl.program_id(1) == 0)
            def _():
                dgate_ref[...] = jnp.zeros_like(dgate_ref)

            dgate_ref[0] += jnp.sum(dx * delta_ref[...], axis=0, keepdims=True)

        return pl.pallas_call(
            body, name=name, grid=(nseq, nt),
            in_specs=[row, row, row, row, per_seq, vec, per_seq],
            out_specs=[row, row, per_seq, vec, per_seq, per_seq],
            out_shape=[jax.ShapeDtypeStruct((t, d), F32), jax.ShapeDtypeStruct((t, d), BF16),
                       jax.ShapeDtypeStruct((nseq, 1, d), F32), jax.ShapeDtypeStruct((1, d), F32),
                       jax.ShapeDtypeStruct((nseq, 1, d), F32), jax.ShapeDtypeStruct((nseq, 1, d), F32)],
            compiler_params=_cp("arbitrary", "arbitrary"),
        )(dh, dxo, x, delta, gate, g, sc)

    def body0(dh_ref, dxo_ref, x_ref, g_ref, sc_ref, dx_ref, dg_ref, dsc_ref, dsh_ref):
        core(dh_ref, dxo_ref, x_ref, g_ref, sc_ref, dx_ref, dg_ref, dsc_ref, dsh_ref)

    dx, dg, dsc, dsh = pl.pallas_call(
        body0, name=name, grid=(nseq, nt),
        in_specs=[row, row, row, vec, per_seq],
        out_specs=[row, vec, per_seq, per_seq],
        out_shape=[jax.ShapeDtypeStruct((t, d), F32), jax.ShapeDtypeStruct((1, d), F32),
                   jax.ShapeDtypeStruct((nseq, 1, d), F32), jax.ShapeDtypeStruct((nseq, 1, d), F32)],
        compiler_params=_cp("arbitrary", "arbitrary"),
    )(dh, dxo, x, g, sc)
    return dx, None, None, dg, dsc, dsh


def _final_loss(xin, delta, gate, fg, target, *, nseq):
    t, d = xin.shape
    seq = t // nseq
    tr = _row_tile(seq)
    nt = seq // tr
    row = pl.BlockSpec((tr, d), lambda s, i: (s * nt + i, 0))
    per_seq = pl.BlockSpec((1, 1, d), lambda s, i: (s, 0, 0))
    vec = pl.BlockSpec((1, d), lambda s, i: (0, 0))

    def body(xin_ref, delta_ref, gate_ref, fg_ref, tgt_ref, loss_ref, dx_ref, dd_ref, dgate_ref, dfg_ref):
        s, i = pl.program_id(0), pl.program_id(1)
        dl = delta_ref[...]
        x = xin_ref[...] + gate_ref[0] * dl
        y, vjp = jax.vjp(lambda xx, gg: _rms(xx, gg, D_MODEL), x, fg_ref[...])
        err = y - tgt_ref[...]
        dx, dfg_t = vjp(err * (1.0 / d))
        dx_ref[...] = dx
        dd_ref[...] = _b(dx * gate_ref[0])

        @pl.when((s == 0) & (i == 0))
        def _():
            loss_ref[...] = jnp.zeros_like(loss_ref)
            dfg_ref[...] = jnp.zeros_like(dfg_ref)

        @pl.when(i == 0)
        def _():
            dgate_ref[...] = jnp.zeros_like(dgate_ref)

        loss_ref[...] += jnp.sum(err * err) * (0.5 / d)
        dfg_ref[...] += dfg_t
        dgate_ref[0] += jnp.sum(dx * dl, axis=0, keepdims=True)

    return pl.pallas_call(
        body, name="final_loss", grid=(nseq, nt),
        in_specs=[row, row, per_seq, vec, row],
        out_specs=[pl.BlockSpec((1, 128), lambda s, i: (0, 0)), row, row, per_seq, vec],
        out_shape=[jax.ShapeDtypeStruct((1, 128), F32), jax.ShapeDtypeStruct((t, d), F32),
                   jax.ShapeDtypeStruct((t, d), BF16), jax.ShapeDtypeStruct((nseq, 1, d), F32),
                   jax.ShapeDtypeStruct((1, d), F32)],
        compiler_params=_cp("arbitrary", "arbitrary"),
    )(xin, delta, gate, fg, target)


def _shift_down(x, j):
    if j == 0:
        return x
    rows = lax.broadcasted_iota(jnp.int32, x.shape, 0)
    return jnp.where(rows >= j, pltpu.roll(x, j, 0), 0.0)


def _shift_up(x, j):
    if j == 0:
        return x
    n = x.shape[0]
    rows = lax.broadcasted_iota(jnp.int32, x.shape, 0)
    return jnp.where(rows < n - j, pltpu.roll(x, n - j, 0), 0.0)


def _conv(x, w_ref, b_ref):
    kw = w_ref.shape[0]
    y = b_ref[...] + w_ref[kw - 1:kw, :] * x
    for j in range(1, kw):
        y = y + w_ref[kw - 1 - j:kw - j, :] * _shift_down(x, j)
    return y


def _conv_bwd(dy, x, w_ref, dw_ref, db_ref):
    kw = w_ref.shape[0]
    dx = w_ref[kw - 1:kw, :] * dy
    dw_ref[kw - 1:kw, :] += jnp.sum(dy * x, axis=0, keepdims=True)
    for j in range(1, kw):
        dy_j = _shift_up(dy, j)
        dx = dx + w_ref[kw - 1 - j:kw - j, :] * dy_j
        dw_ref[kw - 1 - j:kw - j, :] += jnp.sum(dy_j * x, axis=0, keepdims=True)
    db_ref[...] += jnp.sum(dy, axis=0, keepdims=True)
    return dx


CONV_TC = 256


def _ssd_conv_fwd(proj, w, b, *, nseq):
    t = proj.shape[0]
    seq = t // nseq
    nb = CONV_DIM // CONV_TC
    off = COL_XBC // CONV_TC

    def body(x_ref, w_ref, b_ref, o_ref):
        o_ref[...] = _silu(_conv(x_ref[...], w_ref, b_ref))

    return pl.pallas_call(
        body, name="ssd_conv_fwd", grid=(nb, nseq),
        in_specs=[pl.BlockSpec((seq, CONV_TC), lambda j, s: (s, off + j)),
                  pl.BlockSpec((SSD_CONV, CONV_TC), lambda j, s: (0, j)),
                  pl.BlockSpec((1, CONV_TC), lambda j, s: (0, j))],
        out_specs=pl.BlockSpec((seq, CONV_TC), lambda j, s: (s, j)),
        out_shape=jax.ShapeDtypeStruct((t, CONV_DIM), F32),
        compiler_params=_cp("parallel", "parallel"),
    )(proj, w, b)


def _ssd_conv_bwd(dact, proj, w, b, dproj, *, nseq):
    t = proj.shape[0]
    seq = t // nseq
    nb = CONV_DIM // CONV_TC
    off = COL_XBC // CONV_TC

    def body(da_ref, x_ref, w_ref, b_ref, dproj_ref, dx_ref, dw_ref, db_ref):
        del dproj_ref

        @pl.when(pl.program_id(1) == 0)
        def _():
            dw_ref[...] = jnp.zeros_like(dw_ref)
            db_ref[...] = jnp.zeros_like(db_ref)

        x = x_ref[...]
        pre = _conv(x, w_ref, b_ref)
        sg = jax.nn.sigmoid(pre)
        dpre = da_ref[...] * (sg * (1.0 + pre * (1.0 - sg)))
        dx_ref[...] = _b(_conv_bwd(dpre, x, w_ref, dw_ref, db_ref))

    return pl.pallas_call(
        body, name="ssd_conv_bwd", grid=(nb, nseq),
        in_specs=[pl.BlockSpec((seq, CONV_TC), lambda j, s: (s, j)),
                  pl.BlockSpec((seq, CONV_TC), lambda j, s: (s, off + j)),
                  pl.BlockSpec((SSD_CONV, CONV_TC), lambda j, s: (0, j)),
                  pl.BlockSpec((1, CONV_TC), lambda j, s: (0, j)),
                  ANY],
        out_specs=[pl.BlockSpec((seq, CONV_TC), lambda j, s: (s, off + j)),
                   pl.BlockSpec((SSD_CONV, CONV_TC), lambda j, s: (0, j)),
                   pl.BlockSpec((1, CONV_TC), lambda j, s: (0, j))],
        out_shape=[jax.ShapeDtypeStruct(dproj.shape, dproj.dtype), jax.ShapeDtypeStruct((SSD_CONV, CONV_DIM), F32),
                   jax.ShapeDtypeStruct((1, CONV_DIM), F32)],
        input_output_aliases={4: 0},
        compiler_params=_cp("parallel", "arbitrary"),
    )(dact, proj, w, b, dproj)


def _ffn_act_fwd(up, w, b, *, nseq):
    t = up.shape[0]
    seq = t // nseq
    nb = D_FF // CONV_TC

    def body(g_ref, v_ref, w_ref, b_ref, o_ref):
        o_ref[...] = _b(_silu(_conv(g_ref[...], w_ref, b_ref)) * v_ref[...])

    return pl.pallas_call(
        body, name="ffn_act_fwd", grid=(nb, nseq),
        in_specs=[pl.BlockSpec((seq, CONV_TC), lambda j, s: (s, j)),
                  pl.BlockSpec((seq, CONV_TC), lambda j, s: (s, nb + j)),
                  pl.BlockSpec((FF_CONV, CONV_TC), lambda j, s: (0, j)),
                  pl.BlockSpec((1, CONV_TC), lambda j, s: (0, j))],
        out_specs=pl.BlockSpec((seq, CONV_TC), lambda j, s: (s, j)),
        out_shape=jax.ShapeDtypeStruct((t, D_FF), BF16),
        compiler_params=_cp("parallel", "parallel"),
    )(up, up, w, b)


def _ffn_act_bwd(dact, up, w, b, *, nseq):
    t = up.shape[0]
    seq = t // nseq
    nb = D_FF // CONV_TC

    def body(da_ref, g_ref, v_ref, w_ref, b_ref, dg_ref, dv_ref, dw_ref, db_ref):
        @pl.when(pl.program_id(1) == 0)
        def _():
            dw_ref[...] = jnp.zeros_like(dw_ref)
            db_ref[...] = jnp.zeros_like(db_ref)

        gate = g_ref[...]
        pre = _conv(gate, w_ref, b_ref)
        sg = jax.nn.sigmoid(pre)
        da = da_ref[...]
        dv_ref[...] = _b(da * (pre * sg))
        dpre = da * v_ref[...] * (sg * (1.0 + pre * (1.0 - sg)))
        dg_ref[...] = _b(_conv_bwd(dpre, gate, w_ref, dw_ref, db_ref))

    col = pl.BlockSpec((seq, CONV_TC), lambda j, s: (s, j))
    return pl.pallas_call(
        body, name="ffn_act_bwd", grid=(nb, nseq),
        in_specs=[col, col,
                  pl.BlockSpec((seq, CONV_TC), lambda j, s: (s, nb + j)),
                  pl.BlockSpec((FF_CONV, CONV_TC), lambda j, s: (0, j)),
                  pl.BlockSpec((1, CONV_TC), lambda j, s: (0, j))],
        out_specs=[col, col,
                   pl.BlockSpec((FF_CONV, CONV_TC), lambda j, s: (0, j)),
                   pl.BlockSpec((1, CONV_TC), lambda j, s: (0, j))],
        out_shape=[jax.ShapeDtypeStruct((t, D_FF), BF16), jax.ShapeDtypeStruct((t, D_FF), BF16),
                   jax.ShapeDtypeStruct((FF_CONV, D_FF), F32), jax.ShapeDtypeStruct((1, D_FF), F32)],
        compiler_params=_cp("parallel", "arbitrary"),
    )(dact, up, up, w, b)


SSD_PAIRS = SSD_HEADS // 2
PAIR_W = 2 * SSD_HEAD_DIM
PAIRS_PER_GROUP = SSD_PAIRS // SSD_GROUPS


def _ssd_chunk(xs, bg, cg, dtr, z, hp, dtb, alog, dskip, ng):
    n = dtr.shape[0]
    dt = _softplus(dtr + dtb)
    cs = _cumsum_rows(dt * (-jnp.exp(alog)))
    cs_t = _transpose(cs)
    lane = lax.broadcasted_iota(jnp.int32, (1, SSD_HEADS), 1)
    sub = lax.broadcasted_iota(jnp.int32, (SSD_HEADS, 1), 0)
    row = lax.broadcasted_iota(jnp.int32, (n, 1), 0)
    causal = lax.broadcasted_iota(jnp.int32, (n, n), 0) >= lax.broadcasted_iota(jnp.int32, (n, n), 1)
    first = lax.broadcasted_iota(jnp.int32, (1, PAIR_W), 1) < SSD_HEAD_DIM
    first_rows = lax.broadcasted_iota(jnp.int32, (PAIR_W, 1), 0) < SSD_HEAD_DIM
    first_f = first.astype(F32)
    cb = [_bdot_nt(cg[g], bg[g]) for g in range(SSD_GROUPS)]
    ys, hn = [], []
    for p in range(SSD_PAIRS):
        g = p // PAIRS_PER_GROUP
        col, decay, last = [], [], []
        for h in (2 * p, 2 * p + 1):
            oh = (lane == h).astype(F32)
            cs_h = jnp.sum(cs * oh, axis=1, keepdims=True)
            cs_row = jnp.sum(cs_t * (sub == h).astype(F32), axis=0, keepdims=True)
            col.append((jnp.sum(dt * oh, axis=1, keepdims=True), cs_h, jnp.sum(dskip * oh, axis=1, keepdims=True)))
            last.append(jnp.sum(jnp.where(row == n - 1, cs_h, 0.0), axis=0, keepdims=True))
            decay.append(jnp.where(causal, jnp.exp(jnp.where(causal, cs_h - cs_row, 0.0)), 0.0))
        pair = lambda a, b: jnp.where(first, a, b)
        dt_p = pair(col[0][0], col[1][0])
        cs_p = pair(col[0][1], col[1][1])
        last_p = pair(last[0], last[1])
        xc = xs[p] * dt_p
        y = _bdot(cb[g] * decay[0], xc * first_f) + _bdot(cb[g] * decay[1], xc * (1.0 - first_f))
        y = y + _bdot_nt(cg[g], hp[p]) * jnp.exp(cs_p)
        y = y + pair(col[0][2], col[1][2]) * xs[p]
        keep = jnp.where(first_rows, jnp.exp(last[0]), jnp.exp(last[1]))
        hn.append(keep * hp[p] + _bdot_tn(xc * jnp.exp(last_p - cs_p), bg[g]))
        ys.append(y * _silu(z[p]))
    outs = []
    for g in range(SSD_GROUPS):
        ps = range(g * PAIRS_PER_GROUP, (g + 1) * PAIRS_PER_GROUP)
        ms = sum(jnp.sum(ys[p] * ys[p], axis=1, keepdims=True) for p in ps) * (1.0 / GROUP_WIDTH)
        r = lax.rsqrt(ms + EPS)
        outs += [ys[p] * r * ng[p] for p in ps]
    return outs, hn


def _hslices(ref, width, count, base=0):
    return [ref[:, base + k * width: base + (k + 1) * width] for k in range(count)]


def _ssd_load(xbc_ref, z_ref, dt_ref, ng_ref):
    xs = _hslices(xbc_ref, PAIR_W, SSD_PAIRS)
    bg = _hslices(xbc_ref, D_STATE, SSD_GROUPS, D_SSD)
    cg = _hslices(xbc_ref, D_STATE, SSD_GROUPS, D_SSD + SSD_GROUPS * D_STATE)
    z = _hslices(z_ref, PAIR_W, SSD_PAIRS)
    ng = _hslices(ng_ref, PAIR_W, SSD_PAIRS)
    return xs, bg, cg, dt_ref[:, 0:SSD_HEADS], z, ng


def _ssd_specs(nch):
    rowi = lambda s, c: s * nch + c
    return [pl.BlockSpec((CHUNK, CONV_DIM), lambda s, c: (rowi(s, c), 0)),
            pl.BlockSpec((CHUNK, D_SSD), lambda s, c: (rowi(s, c), COL_Z // D_SSD)),
            pl.BlockSpec((CHUNK, 128), lambda s, c: (rowi(s, c), COL_DT // 128)),
            pl.BlockSpec((1, SSD_HEADS), lambda s, c: (0, 0)),
            pl.BlockSpec((1, SSD_HEADS), lambda s, c: (0, 0)),
            pl.BlockSpec((1, SSD_HEADS), lambda s, c: (0, 0)),
            pl.BlockSpec((1, D_SSD), lambda s, c: (0, 0))]


def _ssd_fwd(xbc, proj, dtb, alog, dskip, ng, *, nseq):
    t = proj.shape[0]
    nch = t // nseq // CHUNK
    hd = PAIR_W

    def body(xbc_ref, z_ref, dt_ref, dtb_ref, alog_ref, dsk_ref, ng_ref, y_ref, hp_ref, h_ref):
        @pl.when(pl.program_id(1) == 0)
        def _():
            h_ref[...] = jnp.zeros_like(h_ref)

        xs, bg, cg, dtr, z, ngs = _ssd_load(xbc_ref, z_ref, dt_ref, ng_ref)
        hp_ref[0] = h_ref[...]
        hp = [h_ref[h * hd:(h + 1) * hd, :] for h in range(SSD_PAIRS)]
        outs, hn = _ssd_chunk(xs, bg, cg, dtr, z, hp, dtb_ref[...], alog_ref[...], dsk_ref[...], ngs)
        for h in range(SSD_PAIRS):
            y_ref[:, h * hd:(h + 1) * hd] = _b(outs[h])
            h_ref[h * hd:(h + 1) * hd, :] = hn[h]

    return pl.pallas_call(
        body, name="ssd_fwd", grid=(nseq, nch),
        in_specs=_ssd_specs(nch),
        out_specs=[pl.BlockSpec((CHUNK, D_SSD), lambda s, c: (s * nch + c, 0)),
                   pl.BlockSpec((1, D_SSD, D_STATE), lambda s, c: (s * nch + c, 0, 0))],
        out_shape=[jax.ShapeDtypeStruct((t, D_SSD + D_GM), BF16),
                   jax.ShapeDtypeStruct((t // CHUNK, D_SSD, D_STATE), F32)],
        scratch_shapes=[pltpu.VMEM((D_SSD, D_STATE), F32)],
        compiler_params=_cp("arbitrary", "arbitrary"),
    )(xbc, proj, proj, dtb, alog, dskip, ng)


def _ssd_bwd(dy, xbc, proj, hprev, dtb, alog, dskip, ng, *, nseq):
    t = proj.shape[0]
    nch = t // nseq // CHUNK
    hd = PAIR_W
    rev = lambda s, c: s * nch + (nch - 1 - c)

    def body(dy_ref, xbc_ref, z_ref, dt_ref, hp_ref, dtb_ref, alog_ref, dsk_ref, ng_ref,
             dxbc_ref, dproj_ref, ddtb_ref, dalog_ref, ddsk_ref, dng_ref, dh_ref):
        first = (pl.program_id(0) == 0) & (pl.program_id(1) == 0)

        @pl.when(pl.program_id(1) == 0)
        def _():
            dh_ref[...] = jnp.zeros_like(dh_ref)

        @pl.when(first)
        def _():
            ddtb_ref[...] = jnp.zeros_like(ddtb_ref)
            dalog_ref[...] = jnp.zeros_like(dalog_ref)
            ddsk_ref[...] = jnp.zeros_like(ddsk_ref)
            dng_ref[...] = jnp.zeros_like(dng_ref)

        xs, bg, cg, dtr, z, ngs = _ssd_load(xbc_ref, z_ref, dt_ref, ng_ref)
        hp = [hp_ref[0, h * hd:(h + 1) * hd, :] for h in range(SSD_PAIRS)]
        _, vjp = jax.vjp(_ssd_chunk, xs, bg, cg, dtr, z, hp, dtb_ref[...], alog_ref[...], dsk_ref[...], ngs)
        douts = [dy_ref[:, h * hd:(h + 1) * hd] for h in range(SSD_PAIRS)]
        dhn = [dh_ref[h * hd:(h + 1) * hd, :] for h in range(SSD_PAIRS)]
        dxs, dbg, dcg, ddtr, dz, dhp, ddtb, dalog, ddsk, dngs = vjp((douts, dhn))
        dproj_ref[:, :COL_Z] = jnp.zeros((CHUNK, COL_Z), BF16)
        dproj_ref[:, COL_XBC:] = jnp.zeros((CHUNK, N_INP - COL_XBC), BF16)
        for h in range(SSD_PAIRS):
            dxbc_ref[:, h * hd:(h + 1) * hd] = dxs[h]
            dproj_ref[:, COL_Z + h * hd: COL_Z + (h + 1) * hd] = _b(dz[h])
            dh_ref[h * hd:(h + 1) * hd, :] = dhp[h]
            dng_ref[:, h * hd:(h + 1) * hd] += dngs[h]
        for g in range(SSD_GROUPS):
            dxbc_ref[:, D_SSD + g * D_STATE: D_SSD + (g + 1) * D_STATE] = dbg[g]
            dxbc_ref[:, D_SSD + (SSD_GROUPS + g) * D_STATE: D_SSD + (SSD_GROUPS + g + 1) * D_STATE] = dcg[g]
        dproj_ref[:, COL_DT:COL_DT + SSD_HEADS] = _b(ddtr)
        ddtb_ref[...] += ddtb
        dalog_ref[...] += dalog
        ddsk_ref[...] += ddsk

    small = pl.BlockSpec((1, SSD_HEADS), lambda s, c: (0, 0))
    return pl.pallas_call(
        body, name="ssd_bwd", grid=(nseq, nch),
        in_specs=[pl.BlockSpec((CHUNK, D_SSD), lambda s, c: (rev(s, c), 0)),
                  pl.BlockSpec((CHUNK, CONV_DIM), lambda s, c: (rev(s, c), 0)),
                  pl.BlockSpec((CHUNK, D_SSD), lambda s, c: (rev(s, c), COL_Z // D_SSD)),
                  pl.BlockSpec((CHUNK, 128), lambda s, c: (rev(s, c), COL_DT // 128)),
                  pl.BlockSpec((1, D_SSD, D_STATE), lambda s, c: (rev(s, c), 0, 0)),
                  small, small, small,
                  pl.BlockSpec((1, D_SSD), lambda s, c: (0, 0))],
        out_specs=[pl.BlockSpec((CHUNK, CONV_DIM), lambda s, c: (rev(s, c), 0)),
                   pl.BlockSpec((CHUNK, N_INP), lambda s, c: (rev(s, c), 0)),
                   small, small, small,
                   pl.BlockSpec((1, D_SSD), lambda s, c: (0, 0))],
        out_shape=[jax.ShapeDtypeStruct((t, CONV_DIM), F32), jax.ShapeDtypeStruct((t, N_INP), BF16),
                   jax.ShapeDtypeStruct((1, SSD_HEADS), F32), jax.ShapeDtypeStruct((1, SSD_HEADS), F32),
                   jax.ShapeDtypeStruct((1, SSD_HEADS), F32), jax.ShapeDtypeStruct((1, D_SSD), F32)],
        scratch_shapes=[pltpu.VMEM((D_SSD, D_STATE), F32)],
        compiler_params=_cp("arbitrary", "arbitrary"),
    )(dy, xbc, proj, proj, hprev, dtb, alog, dskip, ng)


def _gmlp_chunk(gu, gv, ws, bs_cols, vg, og):
    n = gu[0].shape[0]
    mask = _tri(n, True)
    au = [_gelu(t) for t in gu]
    av = [_gelu(t) for t in gv]
    r = lax.rsqrt(sum(jnp.sum(t * t, axis=1, keepdims=True) for t in av) * (1.0 / D_GM) + EPS)
    p = []
    for h in range(GM_HEADS):
        sv = _bdot(ws[h] * mask, av[h] * r * vg[h]) + bs_cols[h]
        p.append(au[h] * sv)
    r2 = lax.rsqrt(sum(jnp.sum(t * t, axis=1, keepdims=True) for t in p) * (1.0 / D_GM) + EPS)
    return [p[h] * r2 * og[h] for h in range(GM_HEADS)]


def _gmlp_load(u_ref, v_ref, ws_ref, bst_ref, vg_ref, og_ref):
    gu = _hslices(u_ref, GM_HEAD_DIM, GM_HEADS)
    gv = _hslices(v_ref, GM_HEAD_DIM, GM_HEADS)
    ws = [ws_ref[h] for h in range(GM_HEADS)]
    bs_cols = [bst_ref[:, h:h + 1] for h in range(GM_HEADS)]
    return gu, gv, ws, bs_cols, _hslices(vg_ref, GM_HEAD_DIM, GM_HEADS), _hslices(og_ref, GM_HEAD_DIM, GM_HEADS)


def _gmlp_specs():
    return [pl.BlockSpec((CHUNK, D_GM), lambda i: (i, COL_U // D_GM)),
            pl.BlockSpec((CHUNK, D_GM), lambda i: (i, COL_V // D_GM)),
            pl.BlockSpec((GM_HEADS, CHUNK, CHUNK), lambda i: (0, 0, 0)),
            pl.BlockSpec((CHUNK, GM_HEADS), lambda i: (0, 0)),
            pl.BlockSpec((1, D_GM), lambda i: (0, 0)),
            pl.BlockSpec((1, D_GM), lambda i: (0, 0))]


def _gmlp_fwd(proj, ycat, ws, bst, vg, og):
    t = proj.shape[0]

    def body(u_ref, v_ref, ws_ref, bst_ref, vg_ref, og_ref, ycat_ref, o_ref):
        del ycat_ref
        outs = _gmlp_chunk(*_gmlp_load(u_ref, v_ref, ws_ref, bst_ref, vg_ref, og_ref))
        for h in range(GM_HEADS):
            o_ref[:, h * GM_HEAD_DIM:(h + 1) * GM_HEAD_DIM] = _b(outs[h])

    return pl.pallas_call(
        body, name="gmlp_fwd", grid=(t // CHUNK,),
        in_specs=_gmlp_specs() + [ANY],
        out_specs=pl.BlockSpec((CHUNK, D_GM), lambda i: (i, D_SSD // D_GM)),
        out_shape=jax.ShapeDtypeStruct(ycat.shape, ycat.dtype),
        input_output_aliases={6: 0},
        compiler_params=_cp("parallel"),
    )(proj, proj, ws, bst, vg, og, ycat)


def _gmlp_bwd(dy, proj, ws, bst, vg, og, dproj):
    t = proj.shape[0]
    w = GM_HEAD_DIM

    def body(dy_ref, u_ref, v_ref, ws_ref, bst_ref, vg_ref, og_ref, dproj_ref,
             dgm_ref, dws_ref, dbst_ref, dvg_ref, dog_ref):
        del dproj_ref

        @pl.when(pl.program_id(0) == 0)
        def _():
            dws_ref[...] = jnp.zeros_like(dws_ref)
            dbst_ref[...] = jnp.zeros_like(dbst_ref)
            dvg_ref[...] = jnp.zeros_like(dvg_ref)
            dog_ref[...] = jnp.zeros_like(dog_ref)

        _, vjp = jax.vjp(_gmlp_chunk, *_gmlp_load(u_ref, v_ref, ws_ref, bst_ref, vg_ref, og_ref))
        dgu, dgv, dws, dbs, dvg, dog = vjp(_hslices(dy_ref, w, GM_HEADS))
        for h in range(GM_HEADS):
            dgm_ref[:, h * w:(h + 1) * w] = _b(dgu[h])
            dgm_ref[:, D_GM + h * w: D_GM + (h + 1) * w] = _b(dgv[h])
            dws_ref[h] += dws[h]
            dbst_ref[:, h:h + 1] += dbs[h]
            dvg_ref[:, h * w:(h + 1) * w] += dvg[h]
            dog_ref[:, h * w:(h + 1) * w] += dog[h]

    return pl.pallas_call(
        body, name="gmlp_bwd", grid=(t // CHUNK,),
        in_specs=[pl.BlockSpec((CHUNK, D_GM), lambda i: (i, 1))] + _gmlp_specs() + [ANY],
        out_specs=[pl.BlockSpec((CHUNK, 2 * D_GM), lambda i: (i, COL_U // (2 * D_GM))),
                   pl.BlockSpec((GM_HEADS, CHUNK, CHUNK), lambda i: (0, 0, 0)),
                   pl.BlockSpec((CHUNK, GM_HEADS), lambda i: (0, 0)),
                   pl.BlockSpec((1, D_GM), lambda i: (0, 0)),
                   pl.BlockSpec((1, D_GM), lambda i: (0, 0))],
        out_shape=[jax.ShapeDtypeStruct(dproj.shape, dproj.dtype), jax.ShapeDtypeStruct((GM_HEADS, CHUNK, CHUNK), F32),
                   jax.ShapeDtypeStruct((CHUNK, GM_HEADS), F32), jax.ShapeDtypeStruct((1, D_GM), F32),
                   jax.ShapeDtypeStruct((1, D_GM), F32)],
        input_output_aliases={7: 0},
        compiler_params=_cp("arbitrary"),
    )(dy, proj, proj, ws, bst, vg, og, dproj)


def _local_step(x, target, mods, lw, final_g, *, nseq, big_w, grad_sink, small_sink):
    saved = []
    xin, delta, gate = x, None, None
    for l in range(DEPTH):
        w = lw[l]
        sh1, sc1, g1, sh2, sc2, g2 = mods[l]
        x0, h1 = _normmod_fwd(xin, delta, gate, w["norm1_g"], sc1, sh1, nseq=nseq, name=f"norm1_fwd_{l}")
        w_in = big_w(l, "w_in", h1)
        proj = _matmul(h1, w_in, tb=True, name=f"mm_in_{l}")
        xbc = _ssd_conv_fwd(proj, w["ssd_conv_w"], w["ssd_conv_b"], nseq=nseq)
        ycat, hprev = _ssd_fwd(xbc, proj, w["ssd_dt_bias"], w["ssd_a_log"], w["ssd_d"], w["ssd_norm_g"], nseq=nseq)
        ycat = _gmlp_fwd(proj, ycat, w["gm_ws"], w["gm_bst"], w["gm_vnorm_g"], w["gm_out_g"])
        w_out = big_w(l, "w_out", ycat)
        mix = _matmul(ycat, w_out, name=f"mm_out_{l}")
        x1, h2 = _normmod_fwd(x0, mix, g1, w["norm2_g"], sc2, sh2, nseq=nseq, name=f"norm2_fwd_{l}")
        ff_up = big_w(l, "ff_up", h2)
        up = _matmul(h2, ff_up, tb=True, name=f"mm_up_{l}")
        act = _ffn_act_fwd(up, w["ff_conv_w"], w["ff_conv_b"], nseq=nseq)
        ff_down = big_w(l, "ff_down", act)
        dn = _matmul(act, ff_down, name=f"mm_down_{l}")
        saved.append(dict(x0=x0, xin_delta=delta, xin_gate=gate, h1=h1, proj=proj, xbc=xbc, hprev=hprev, ycat=ycat,
                          mix=mix, x1=x1, h2=h2, up=up, act=act, dn=dn,
                          w_in=w_in, w_out=w_out, ff_up=ff_up, ff_down=ff_down))
        xin, delta, gate = x1, dn, g2

    loss, dx, ddelta, dgate, dfg = _final_loss(xin, delta, gate, final_g, target, nseq=nseq)

    small, dmods = [None] * DEPTH, [None] * DEPTH
    for l in reversed(range(DEPTH)):
        w, sv = lw[l], saved[l]
        sh1, sc1, g1, sh2, sc2, g2 = mods[l]
        dg2 = dgate
        g_ff_down = _matmul(sv["act"], ddelta, ta=True, name=f"mm_down_dw_{l}", out_dtype=BF16)
        dact = _matmul(ddelta, sv["ff_down"], tb=True, name=f"mm_down_dx_{l}")
        dgate_ff, dval_ff, dfcw, dfcb = _ffn_act_bwd(dact, sv["up"], w["ff_conv_w"], w["ff_conv_b"], nseq=nseq)
        g_ff_up = _matmul([dgate_ff, dval_ff], sv["h2"], ta=True, name=f"mm_up_dw_{l}", out_dtype=BF16)
        dep = grad_sink(l, "ffn", dict(ff_down=g_ff_down, ff_up=g_ff_up), dval_ff)
        dh2 = _matmul([dgate_ff, dval_ff], sv["ff_up"], name=f"mm_up_dx_{l}", dep=dep)
        dx, dmix, dg1, dn2g, dsc2, dsh2 = _normmod_bwd(dh2, dx, sv["x1"], sv["mix"], g1, w["norm2_g"], sc2,
                                                       nseq=nseq, name=f"norm2_bwd_{l}")
        g_w_out = _matmul(sv["ycat"], dmix, ta=True, name=f"mm_out_dw_{l}", out_dtype=BF16)
        dep = grad_sink(l, "w_out", dict(w_out=g_w_out), dmix)
        dycat = _matmul(dmix, sv["w_out"], tb=True, name=f"mm_out_dx_{l}", dep=dep)
        dxbc_act, dproj, ddtb, dalog, ddsk, dng = _ssd_bwd(dycat, sv["xbc"], sv["proj"], sv["hprev"], w["ssd_dt_bias"],
                                                          w["ssd_a_log"], w["ssd_d"], w["ssd_norm_g"], nseq=nseq)
        dproj, dscw, dscb = _ssd_conv_bwd(dxbc_act, sv["proj"], w["ssd_conv_w"], w["ssd_conv_b"], dproj, nseq=nseq)
        dproj, dws, dbst, dvg, dog = _gmlp_bwd(dycat, sv["proj"], w["gm_ws"], w["gm_bst"], w["gm_vnorm_g"], w["gm_out_g"], dproj)
        early = dict(norm2_g=dn2g, ssd_norm_g=dng, gm_vnorm_g=dvg, gm_out_g=dog,
                     ssd_conv_w=dscw, ssd_conv_b=dscb, ff_conv_w=dfcw, ff_conv_b=dfcb,
                     ssd_dt_bias=ddtb, ssd_a_log=dalog, ssd_d=ddsk, gm_ws=dws, gm_bs=dbst.T)
        dep = small_sink(l, early, small, dmods, dfg, loss)
        g_w_in = _matmul(dproj, sv["h1"], ta=True, name=f"mm_in_dw_{l}", out_dtype=BF16, dep=dep)
        dep = grad_sink(l, "w_in", dict(w_in=g_w_in), dproj)
        dh1 = _matmul(dproj, sv["w_in"], name=f"mm_in_dx_{l}", dep=dep)
        dx, ddelta, dgate, dn1g, dsc1, dsh1 = _normmod_bwd(dh1, dx, sv["x0"], sv["xin_delta"], sv["xin_gate"],
                                                           w["norm1_g"], sc1, nseq=nseq, name=f"norm1_bwd_{l}")
        small[l] = dict(early, norm1_g=dn1g)
        dmods[l] = jnp.concatenate([dsh1, dsc1, dg1, dsh2, dsc2, dg2], axis=-1)[:, 0, :]
    return dx, small, dmods


def _all_gather(arrs, name, dep=None):
    n = len(arrs)
    extra = [] if dep is None else [dep]

    def body(*refs):
        ins, outs = refs[:n], refs[n + len(extra):2 * n + len(extra)]
        send_sems, recv_sems, local_sems = refs[2 * n + len(extra):]
        x, y, c = lax.axis_index("x"), lax.axis_index("y"), lax.axis_index("c")
        me, sibling = (x, y, c), (x, y, 1 - c)
        chips = [(1 - x, y), (x, 1 - y), (1 - x, 1 - y)]

        def copy(i, k, block, to, src=None):
            px, py, pc = block
            dst = outs[i].at[4 * px + 2 * py + pc]
            return pltpu.make_async_remote_copy(
                src_ref=dst if src is None else src, dst_ref=dst,
                send_sem=send_sems.at[7 * i + k], recv_sem=recv_sems.at[7 * i + k],
                device_id=to, device_id_type=MESH)

        mine = [pltpu.make_async_copy(ins[i], outs[i].at[4 * x + 2 * y + c], local_sems.at[i]) for i in range(n)]
        for cp in mine:
            cp.start()
        first = []
        for i in range(n):
            first.append(copy(i, 0, me, sibling, src=ins[i]))
            first += [copy(i, 1 + j, me, (*chip, c), src=ins[i]) for j, chip in enumerate(chips)]
        for cp in first:
            cp.start()
        passed = []
        for j, chip in enumerate(chips):
            for i in range(n):
                copy(i, 1 + j, (*chip, c), me).wait_recv()
                fwd = copy(i, 4 + j, (*chip, c), sibling)
                fwd.start()
                passed.append(fwd)
        for i in range(n):
            copy(i, 0, sibling, me).wait_recv()
            for j, chip in enumerate(chips):
                copy(i, 4 + j, (*chip, 1 - c), me).wait_recv()
        for cp in first + passed:
            cp.wait_send()
        for cp in mine:
            cp.wait()

    return pl.pallas_call(
        body, name=name,
        in_specs=[ANY] * (n + len(extra)), out_specs=[ANY] * n,
        out_shape=[jax.ShapeDtypeStruct((N_DEV,) + a.shape, a.dtype) for a in arrs],
        scratch_shapes=[pltpu.SemaphoreType.DMA((7 * n,)), pltpu.SemaphoreType.DMA((7 * n,)),
                        pltpu.SemaphoreType.DMA((n,))],
    )(*arrs, *extra)


def _exchange_sibling(arrs, name):
    n = len(arrs)

    def body(*refs):
        ins, outs = refs[:n], refs[n:2 * n]
        send_sems, recv_sems = refs[2 * n:]
        x, y, c = lax.axis_index("x"), lax.axis_index("y"), lax.axis_index("c")
        copies = []
        for i in range(n):
            for k in range(4):
                copies.append(pltpu.make_async_remote_copy(
                    src_ref=ins[i].at[2 * k + (1 - c)], dst_ref=outs[i].at[k],
                    send_sem=send_sems.at[4 * i + k], recv_sem=recv_sems.at[4 * i + k],
                    device_id=(x, y, 1 - c), device_id_type=MESH))
        for cp in copies:
            cp.start()
        for cp in copies:
            cp.wait_recv()
        for cp in copies:
            cp.wait_send()

    return pl.pallas_call(
        body, name=name,
        in_specs=[ANY] * n, out_specs=[ANY] * n,
        out_shape=[jax.ShapeDtypeStruct((4,) + a.shape[1:], a.dtype) for a in arrs],
        scratch_shapes=[pltpu.SemaphoreType.DMA((4 * n,)), pltpu.SemaphoreType.DMA((4 * n,))],
    )(*arrs)


def _exchange_chips(arrs, name):
    n = len(arrs)

    def body(*refs):
        ins, outs = refs[:n], refs[n:2 * n]
        send_sems, recv_sems = refs[2 * n:]
        x, y, c = lax.axis_index("x"), lax.axis_index("y"), lax.axis_index("c")
        chips = [(1 - x, y), (x, 1 - y), (1 - x, 1 - y)]
        copies = []
        for i in range(n):
            for j, (cx, cy) in enumerate(chips):
                copies.append(pltpu.make_async_remote_copy(
                    src_ref=ins[i].at[2 * cx + cy], dst_ref=outs[i].at[j],
                    send_sem=send_sems.at[3 * i + j], recv_sem=recv_sems.at[3 * i + j],
                    device_id=(cx, cy, c), device_id_type=MESH))
        for cp in copies:
            cp.start()
        for cp in copies:
            cp.wait_recv()
        for cp in copies:
            cp.wait_send()

    return pl.pallas_call(
        body, name=name,
        in_specs=[ANY] * n, out_specs=[ANY] * n,
        out_shape=[jax.ShapeDtypeStruct((3,) + a.shape[1:], a.dtype) for a in arrs],
        scratch_shapes=[pltpu.SemaphoreType.DMA((3 * n,)), pltpu.SemaphoreType.DMA((3 * n,))],
    )(*arrs)


def _add_sibling(a, r, pos, name):
    _, depth, rows, cols = a.shape
    tr = _tile(rows, 256) if rows % 8 == 0 else rows
    a3 = a.reshape(N_DEV * depth, rows, cols)
    r3 = r.reshape(4 * depth, rows, cols)

    def body(pos_ref, a_ref, r_ref, o_ref):
        o_ref[...] = a_ref[...] + r_ref[...]

    out = pl.pallas_call(
        body, name=name,
        grid_spec=pltpu.PrefetchScalarGridSpec(
            num_scalar_prefetch=1, grid=(4 * depth, rows // tr),
            in_specs=[pl.BlockSpec((1, tr, cols), lambda q, i, p: ((2 * (q // depth) + p[0]) * depth + q % depth, i, 0)),
                      pl.BlockSpec((1, tr, cols), lambda q, i, p: (q, i, 0))],
            out_specs=pl.BlockSpec((1, tr, cols), lambda q, i, p: (q, i, 0))),
        out_shape=jax.ShapeDtypeStruct((4 * depth, rows, cols), F32),
        compiler_params=_cp("parallel", "parallel"),
    )(pos, a3, r3)
    return out.reshape(4, depth, rows, cols)


HBM = pl.BlockSpec(memory_space=pltpu.HBM)
SEM = pl.BlockSpec(memory_space=pltpu.SEMAPHORE)
EFFECT = pltpu.SideEffectType.DATAFLOW_SIDE_EFFECTING


def _peer(k):
    x, y, c = lax.axis_index("x"), lax.axis_index("y"), lax.axis_index("c")
    return (1 - x if k & 4 else x, 1 - y if k & 2 else y, 1 - c if k & 1 else c)


ALL_PEERS = tuple(range(1, N_DEV))
OTHER_CHIPS = (2, 4, 6)


def _xc_copies(scatter, srcs, lands, send_sems, recv_sems, peers=ALL_PEERS):
    x, y, c = lax.axis_index("x"), lax.axis_index("y"), lax.axis_index("c")
    copies = []
    for i in range(len(srcs)):
        for k in peers:
            px, py, pc = _peer(k)
            src = srcs[i].at[4 * px + 2 * py + pc] if scatter else srcs[i]
            dst = lands[i].at[k - 1] if scatter else lands[i].at[4 * x + 2 * y + c]
            copies.append(pltpu.make_async_remote_copy(
                src_ref=src, dst_ref=dst, send_sem=send_sems[i].at[k - 1], recv_sem=recv_sems[i].at[k - 1],
                device_id=(px, py, pc), device_id_type=MESH))
    return copies


def _xc_start(scatter, arrs, after, name, peers=ALL_PEERS):
    n = len(arrs)
    lands = [lax.empty((N_DEV - 1,) + a.shape[1:] if scatter else (N_DEV,) + a.shape, a.dtype) for a in arrs]

    def body(*refs):
        srcs, lnd = refs[:n], refs[n:2 * n]
        send_sems, recv_sems = refs[2 * n + 1:3 * n + 1], refs[3 * n + 1:4 * n + 1]
        token = refs[6 * n + 1]
        for cp in _xc_copies(scatter, srcs, lnd, send_sems, recv_sems, peers):
            cp.start()
        token[...] = jnp.zeros_like(token)

    outs = pl.pallas_call(
        body, name=name,
        out_shape=[pltpu.SemaphoreType.DMA((N_DEV - 1,))] * (2 * n)
        + [pltpu.HBM(a.shape, a.dtype) for a in arrs] + [pltpu.HBM(a.shape, a.dtype) for a in lands]
        + [jax.ShapeDtypeStruct((8, 128), F32)],
        in_specs=[HBM] * (2 * n) + [ANY],
        out_specs=[SEM] * (2 * n) + [HBM] * (2 * n) + [pl.BlockSpec(memory_space=pltpu.VMEM)],
        input_output_aliases={i: 2 * n + i for i in range(2 * n)},
        compiler_params=pltpu.CompilerParams(has_side_effects=EFFECT),
    )(*[pltpu.with_memory_space_constraint(a, pltpu.HBM) for a in list(arrs) + lands], after)
    return outs[:n], outs[n:2 * n], outs[2 * n:3 * n], outs[3 * n:4 * n], outs[4 * n][0, 0]


def _xc_wait(scatter, send_sems, recv_sems, srcs, lands, after, name, peers=ALL_PEERS):
    n = len(srcs)

    def body(*refs):
        s_refs, l_refs = refs[:n], refs[n:2 * n]
        ss, rs = refs[2 * n:3 * n], refs[3 * n:4 * n]
        for cp in _xc_copies(scatter, s_refs, l_refs, ss, rs, peers):
            cp.wait_send()
            cp.wait_recv()

    outs = pl.pallas_call(
        body, name=name,
        out_shape=[pltpu.HBM(a.shape, a.dtype) for a in list(srcs) + list(lands)],
        in_specs=[HBM] * (2 * n) + [SEM] * (2 * n) + [ANY],
        out_specs=[HBM] * (2 * n),
        input_output_aliases={i: i for i in range(2 * n)},
        compiler_params=pltpu.CompilerParams(has_side_effects=EFFECT),
    )(*srcs, *lands, *send_sems, *recv_sems, after)
    return outs[:n], outs[n:]


def _sib_copies(zones, send_sems, recv_sems):
    x, y, c = lax.axis_index("x"), lax.axis_index("y"), lax.axis_index("c")
    copies = []
    for i in range(len(zones)):
        for q in range(N_DEV // 2):
            slot = zones[i].at[2 * q + c]
            copies.append(pltpu.make_async_remote_copy(
                src_ref=slot, dst_ref=slot, send_sem=send_sems[i].at[q], recv_sem=recv_sems[i].at[q],
                device_id=(x, y, 1 - c), device_id_type=MESH))
    return copies


def _sib_start(zones, name):
    n = len(zones)

    def body(*refs):
        for cp in _sib_copies(refs[:n], refs[n:2 * n], refs[2 * n:3 * n]):
            cp.start()

    outs = pl.pallas_call(
        body, name=name,
        out_shape=[pltpu.SemaphoreType.DMA((N_DEV // 2,))] * (2 * n) + [pltpu.HBM(a.shape, a.dtype) for a in zones],
        in_specs=[HBM] * n,
        out_specs=[SEM] * (2 * n) + [HBM] * n,
        input_output_aliases={i: 2 * n + i for i in range(n)},
        compiler_params=pltpu.CompilerParams(has_side_effects=EFFECT),
    )(*[pltpu.with_memory_space_constraint(a, pltpu.HBM) for a in zones])
    return outs[:n], outs[n:2 * n], outs[2 * n:]


def _sib_wait(send_sems, recv_sems, zones, name):
    n = len(zones)

    def body(*refs):
        for cp in _sib_copies(refs[:n], refs[n:2 * n], refs[2 * n:3 * n]):
            cp.wait_send()
            cp.wait_recv()

    return pl.pallas_call(
        body, name=name,
        out_shape=[pltpu.HBM(a.shape, a.dtype) for a in zones],
        in_specs=[HBM] * n + [SEM] * (2 * n),
        out_specs=[HBM] * n,
        input_output_aliases={i: i for i in range(n)},
        compiler_params=pltpu.CompilerParams(has_side_effects=EFFECT),
    )(*zones, *send_sems, *recv_sems)


def _adamw_math(w, g, m, v):
    m = ADAM_B1 * m + (1.0 - ADAM_B1) * g
    v = ADAM_B2 * v + (1.0 - ADAM_B2) * (g * g)
    m_hat = m / (1.0 - ADAM_B1 ** ADAM_STEP)
    v_hat = v / (1.0 - ADAM_B2 ** ADAM_STEP)
    delta = -ADAM_LR * (m_hat / (jnp.sqrt(v_hat) + ADAM_EPS) + ADAM_WD * w)
    return delta, m, v


def _adamw_sharded(parts, w, m, v, pos, name):
    depth, rows, cols = w.shape
    tr = _tile(rows, 256) if rows % 8 == 0 else rows
    npart = len(parts)

    def body(pos_ref, *refs):
        prefs = refs[:npart]
        w_ref, m_ref, v_ref, g_out, d_out, m_out, v_out = refs[npart:]
        g = prefs[0][...]
        for pr in prefs[1:]:
            g = g + pr[...]
        delta, mn, vn = _adamw_math(w_ref[...], g, m_ref[...], v_ref[...])
        g_out[...] = g
        d_out[...] = delta
        m_out[...] = mn
        v_out[...] = vn

    def part_spec(fn):
        return pl.BlockSpec((1, tr, cols), lambda l, i, p: (fn(p) * depth + l, i, 0))

    blk = pl.BlockSpec((1, tr, cols), lambda l, i, p: (l, i, 0))
    shp = jax.ShapeDtypeStruct((depth, rows, cols), F32)
    return pl.pallas_call(
        body, name=name,
        grid_spec=pltpu.PrefetchScalarGridSpec(
            num_scalar_prefetch=1, grid=(depth, rows // tr),
            in_specs=[part_spec(fn) for _, fn in parts] + [blk, blk, blk],
            out_specs=[blk, blk, blk, blk]),
        out_shape=[shp, shp, shp, shp],
        compiler_params=_cp("parallel", "parallel"),
    )(pos, *[a for a, _ in parts], w, m, v)


def _adamw_layer(parts, w, m, v, pos, layer, prev, name):
    depth, rows, cols = w.shape
    npart = len(parts)
    nprev = 0 if prev is None else 4
    if rows % 16 == 0:
        tr, tc = max(t for t in range(16, 257, 16) if rows % t == 0), cols
    else:
        tr, tc = rows, _tile(cols, 256)
    pick = (lambda i: (i, 0)) if rows % 16 == 0 else (lambda i: (0, i))

    def body(pos_ref, *refs):
        prefs = refs[:npart]
        w_ref, m_ref, v_ref = refs[npart:npart + 3]
        g_out, d_out, m_out, v_out = refs[npart + 3 + nprev:]
        g = prefs[0][...].astype(F32)
        for pr in prefs[1:]:
            g = g + pr[...].astype(F32)
        delta, mn, vn = _adamw_math(w_ref[...], g, m_ref[...], v_ref[...])
        g_out[...] = g
        d_out[...] = delta
        m_out[...] = mn
        v_out[...] = vn

    def part_spec(fn):
        return pl.BlockSpec((1, tr, tc), lambda i, p: (fn(p), *pick(i)))

    blk = pl.BlockSpec((1, tr, tc), lambda i, p: (layer, *pick(i)))
    shp = jax.ShapeDtypeStruct((depth, rows, cols), F32)
    first_prev = 1 + npart + 3
    return pl.pallas_call(
        body, name=name,
        grid_spec=pltpu.PrefetchScalarGridSpec(
            num_scalar_prefetch=1, grid=(rows // tr * (cols // tc),),
            in_specs=[part_spec(fn) for _, fn in parts] + [blk, blk, blk] + [ANY] * nprev,
            out_specs=[blk, blk, blk, blk]),
        out_shape=[shp, shp, shp, shp],
        input_output_aliases={first_prev + j: j for j in range(nprev)},
        compiler_params=_cp("parallel"),
    )(pos, *[a for a, _ in parts], w, m, v, *(prev or ()))


_P1024 = ["norm1_g", "norm2_g", "ssd_norm_g", "gm_vnorm_g", "gm_out_g"]
_P16 = ["ssd_dt_bias", "ssd_a_log", "ssd_d"]


def _adamw_small(gath, wmv):
    names = list(wmv.keys())
    classes = list(gath.keys())
    flat_in = [gath[k] for k in classes]
    for nme in names:
        flat_in += list(wmv[nme])
    out_shapes = []
    for nme in names:
        out_shapes += [jax.ShapeDtypeStruct(wmv[nme][0].shape, F32)] * 4
    out_shapes += [jax.ShapeDtypeStruct((DEPTH, SSD_CONV, CONV_DIM), F32), jax.ShapeDtypeStruct((DEPTH, FF_CONV, D_FF), F32),
                   jax.ShapeDtypeStruct((1, SSD_HEADS), F32)]
    scratch = [pltpu.VMEM(gath[k].shape[1:], F32) for k in classes]
    ncls = len(classes)

    def body(*refs):
        g_refs = dict(zip(classes, refs[:ncls]))
        pos = ncls
        w_refs = {}
        for nme in names:
            w_refs[nme] = refs[pos:pos + 3]
            pos += 3
        o_refs = {}
        for nme in names:
            o_refs[nme] = refs[pos:pos + 4]
            pos += 4
        scw_out, fcw_out, loss_out = refs[pos], refs[pos + 1], refs[pos + 2]
        s_refs = dict(zip(classes, refs[pos + 3:]))
        for k in classes:
            acc = g_refs[k][0]
            for dev in range(1, N_DEV):
                acc = acc + g_refs[k][dev]
            s_refs[k][...] = acc

        def apply(nme, grad_of):
            w_ref, m_ref, v_ref = w_refs[nme]
            g_out, d_out, m_out, v_out = o_refs[nme]
            shape = w_ref.shape
            if len(shape) == 2:
                idxs = [(slice(l, l + 1),) for l in range(shape[0])]
            elif len(shape) == 3:
                idxs = [(l,) for l in range(shape[0])]
            else:
                idxs = [(l, h) for l in range(shape[0]) for h in range(shape[1])]
            for n_i, ix in enumerate(idxs):
                g = grad_of(n_i)
                delta, mn, vn = _adamw_math(w_ref[ix], g, m_ref[ix], v_ref[ix])
                g_out[ix] = g
                d_out[ix] = delta
                m_out[ix] = mn
                v_out[ix] = vn

        s1024, s1536, s2816, s16, s128, s6144, late1024, late6144 = (s_refs[k] for k in classes)
        s1024[0:1, :] += late1024[...]
        s6144[0:late6144.shape[0], :] += late6144[...]
        for n_i, nme in enumerate(_P1024):
            apply(nme, lambda l, b=2 * n_i: s1024[b + l:b + l + 1, :])
        apply("final_g", lambda l: s1024[10:11, :])
        apply("ssd_conv_b", lambda l: s1536[8 + l:9 + l, :])
        apply("ff_conv_b", lambda l: s2816[6 + l:7 + l, :])
        for n_i, nme in enumerate(_P16):
            apply(nme, lambda l, b=2 * n_i: s16[b + l:b + l + 1, :])
        apply("gm_ws", lambda q: s128[q * CHUNK:(q + 1) * CHUNK, :])
        apply("gm_bs", lambda l: s128[2048 + 8 * l:2048 + 8 * (l + 1), :])
        apply("ada_b", lambda l: s6144[2 * l:2 * l + 1, :] + s6144[2 * l + 1:2 * l + 2, :])
        for l in range(DEPTH):
            scw_out[l] = s1536[SSD_CONV * l:SSD_CONV * (l + 1), :]
            fcw_out[l] = s2816[FF_CONV * l:FF_CONV * (l + 1), :]
        loss_out[...] = s16[2 * len(_P16):2 * len(_P16) + 1, :]

    outs = pl.pallas_call(
        body, name="adamw_small",
        out_shape=out_shapes,
        scratch_shapes=scratch,
        compiler_params=pltpu.CompilerParams(vmem_limit_bytes=VMEM_LIMIT),
    )(*flat_in)
    res = {nme: tuple(outs[4 * i:4 * i + 4]) for i, nme in enumerate(names)}
    return res, outs[-3], outs[-2], outs[-1]


_WEIGHTS = ['ada_w', 'ada_b', 'norm1_g', 'norm2_g', 'w_in', 'ssd_conv_w', 'ssd_conv_b', 'ssd_dt_bias', 'ssd_a_log',
            'ssd_d', 'ssd_norm_g', 'gm_vnorm_g', 'gm_ws', 'gm_bs', 'gm_out_g', 'w_out', 'ff_up', 'ff_conv_w',
            'ff_conv_b', 'ff_down', 'final_g']


_O_XBC, _O_DT, _O_GM = D_SSD, D_SSD + CONV_DIM, D_SSD + CONV_DIM + SSD_HEADS


_TRANSPOSED = ("w_in", "ff_up")


def _full_weight(name, g):
    full = g.reshape(g.shape[0] * g.shape[1], g.shape[2])
    if name != "w_in":
        return full
    zpad = jnp.zeros((N_INP - N_IN, full.shape[1]), full.dtype)
    return jnp.concatenate([full[_O_GM:], full[:_O_XBC], full[_O_XBC:_O_DT], full[_O_DT:_O_GM], zpad], axis=0)


def _by_owner(name, grad):
    if name == "w_in":
        grad = jnp.concatenate([grad[COL_Z:COL_XBC], grad[COL_XBC:COL_DT], grad[COL_DT:COL_DT + SSD_HEADS], grad[:COL_Z]], axis=0)
    return grad.reshape(N_DEV, grad.shape[0] // N_DEV, grad.shape[1])


def kernel(x, c, ada_w, ada_b, norm1_g, norm2_g, w_in, ssd_conv_w, ssd_conv_b, ssd_dt_bias, ssd_a_log, ssd_d, ssd_norm_g, gm_vnorm_g, gm_ws, gm_bs, gm_out_g, w_out, ff_up, ff_conv_w, ff_conv_b, ff_down, final_g, loss_target, m_ada_w, m_ada_b, m_norm1_g, m_norm2_g, m_w_in, m_ssd_conv_w, m_ssd_conv_b, m_ssd_dt_bias, m_ssd_a_log, m_ssd_d, m_ssd_norm_g, m_gm_vnorm_g, m_gm_ws, m_gm_bs, m_gm_out_g, m_w_out, m_ff_up, m_ff_conv_w, m_ff_conv_b, m_ff_down, m_final_g, v_ada_w, v_ada_b, v_norm1_g, v_norm2_g, v_w_in, v_ssd_conv_w, v_ssd_conv_b, v_ssd_dt_bias, v_ssd_a_log, v_ssd_d, v_ssd_norm_g, v_gm_vnorm_g, v_gm_ws, v_gm_bs, v_gm_out_g, v_w_out, v_ff_up, v_ff_conv_w, v_ff_conv_b, v_ff_down, v_final_g):
    given = dict(locals())
    wts = {n: given[n] for n in _WEIGHTS}
    mom = {n: given["m_" + n] for n in _WEIGHTS}
    var = {n: given["v_" + n] for n in _WEIGHTS}
    nseq, seq, d = x.shape
    ix, iy, ic = lax.axis_index("x"), lax.axis_index("y"), lax.axis_index("c")
    me = 4 * ix + 2 * iy + ic
    me_arr = me.astype(jnp.int32).reshape(1)

    for nme in _TRANSPOSED:
        wts[nme], mom[nme], var[nme] = (jnp.transpose(a, (0, 2, 1)) for a in (wts[nme], mom[nme], var[nme]))

    def shard(l, name):
        return _b(wts[name][l])

    g_scw, g_fcw, c_all = _all_gather([ssd_conv_w, ff_conv_w, c], "gather_first")
    first_ssem, first_rsem, first_src, first_land, first_zero = _xc_start(
        False, [shard(0, "w_in")], c_all, "ag_first_start", peers=OTHER_CHIPS)
    later = [(0, "w_out"), (0, "ff_up"), (0, "ff_down"), (1, "w_in"), (1, "w_out"), (1, "ff_up"), (1, "ff_down")]
    ag_ssem, ag_rsem, ag_src, ag_land, ag_zero = _xc_start(
        False, [shard(l, n) for l, n in later], first_zero.reshape(1, 1), "ag_start")
    ag_groups = {(0, "w_out"): [0], (0, "ff_up"): [1, 2], (1, "w_in"): [3, 4], (1, "ff_up"): [5, 6]}
    scw_f = jnp.transpose(g_scw, (1, 2, 0, 3)).reshape(DEPTH, SSD_CONV, CONV_DIM)
    fcw_f = jnp.transpose(g_fcw, (1, 2, 0, 3)).reshape(DEPTH, FF_CONV, D_FF)
    c_all = c_all.reshape(N_DEV * nseq, d)

    n_ada = ada_w.shape[2]
    ada_b_shard = lax.dynamic_slice_in_dim(ada_b, me * n_ada, n_ada, axis=1).reshape(DEPTH, 1, n_ada)
    mod_part, c_act = _ada_fwd(c_all, ada_w, ada_b_shard)
    (mod_g,) = _all_gather([mod_part], "gather_mod")
    mod_all = jnp.transpose(mod_g, (1, 2, 0, 3)).reshape(DEPTH, N_DEV * nseq, N_MOD * d)
    mod_mine = lax.dynamic_slice_in_dim(mod_all, me * nseq, nseq, axis=1)
    mods = [[mod_mine[l, :, k * d:(k + 1) * d].reshape(nseq, 1, d) for k in range(N_MOD)] for l in range(DEPTH)]

    big_cache = {}

    def big_w(l, name, after):
        if (l, name) == (0, "w_in") and (l, name) not in big_cache:
            srcs, lands = _xc_wait(False, first_ssem, first_rsem, first_src, first_land, after, "ag_first_wait",
                                   peers=OTHER_CHIPS)
            zone = lax.dynamic_update_index_in_dim(lands[0], srcs[0], me, 0)
            (zone,) = _sib_wait(*_sib_start([zone], "ag_first_sib_start"), "ag_first_sib_wait")
            big_cache[(l, name)] = _full_weight(name, zone)
        if (l, name) not in big_cache:
            idx = ag_groups[(l, name)]
            pick = lambda seq_: [seq_[i] for i in idx]
            srcs, lands = _xc_wait(False, pick(ag_ssem), pick(ag_rsem), pick(ag_src), pick(ag_land), after,
                                   f"ag_wait_{l}_{name}")
            for i, src, land in zip(idx, srcs, lands):
                big_cache[later[i]] = _full_weight(later[i][1], lax.dynamic_update_index_in_dim(land, src, me, 0))
        return big_cache[(l, name)]

    lw = []
    for l in range(DEPTH):
        lw.append(dict(
            norm1_g=norm1_g[l:l + 1] + (ag_zero if l == 0 else 0.0), norm2_g=norm2_g[l:l + 1], ssd_conv_w=scw_f[l],
            ssd_conv_b=ssd_conv_b[l:l + 1], ssd_dt_bias=ssd_dt_bias[l:l + 1], ssd_a_log=ssd_a_log[l:l + 1],
            ssd_d=ssd_d[l:l + 1], ssd_norm_g=ssd_norm_g[l:l + 1], gm_vnorm_g=gm_vnorm_g[l:l + 1], gm_ws=gm_ws[l],
            gm_bst=gm_bs[l].T, gm_out_g=gm_out_g[l:l + 1], ff_conv_w=fcw_f[l], ff_conv_b=ff_conv_b[l:l + 1]))

    outs = {}
    pending = {}

    def rs_finish(l, group, after):
        names, ssem, rsem, srcs, lands = pending.pop((l, group))
        srcs, lands = _xc_wait(True, ssem, rsem, srcs, lands, after, f"rs_wait_{l}_{group}")
        for nme, own, land in zip(names, srcs, lands):
            parts = [(own, lambda p: p[0])] + [(land, lambda p, k=k: k) for k in range(N_DEV - 1)]
            outs[nme] = _adamw_layer(parts, wts[nme], mom[nme], var[nme], me_arr, l, outs.get(nme), f"adamw_{nme}_{l}")
        return outs[names[-1]][0]

    def grad_sink(l, group, grads, after):
        names = list(grads)
        ssem, rsem, srcs, lands, zero = _xc_start(True, [_by_owner(n, grads[n]) for n in names], after, f"rs_start_{l}_{group}")
        pending[(l, group)] = (names, ssem, rsem, srcs, lands)
        return zero.reshape(1, 1)

    early_gather = {}

    def small_sink(l, early, small, dmods, dfg, loss_p):
        if l > 0:
            return None
        layers = [dict(early, norm1_g=jnp.zeros((1, d), F32))] + small[1:]
        rows = lambda name: [layers[k][name] for k in range(DEPTH)]
        packed = [
            jnp.concatenate(sum([rows(n) for n in _P1024], []) + [dfg], axis=0),
            jnp.concatenate(rows("ssd_conv_w") + rows("ssd_conv_b"), axis=0),
            jnp.concatenate(rows("ff_conv_w") + rows("ff_conv_b"), axis=0),
            jnp.concatenate(sum([rows(n) for n in _P16], []) + [loss_p[:, :SSD_HEADS]], axis=0),
            jnp.concatenate([layers[k]["gm_ws"].reshape(GM_HEADS * CHUNK, CHUNK) for k in range(DEPTH)] + rows("gm_bs"), axis=0),
            jnp.concatenate([jnp.zeros((nseq, N_MOD * d), F32)] + dmods[1:], axis=0)]
        ssem, rsem, srcs, lands, zero = _xc_start(False, packed, packed[0], "small_start")
        early_gather.update(ssem=ssem, rsem=rsem, srcs=srcs, lands=lands)
        return zero.reshape(1, 1)

    grad_x, small, dmods = _local_step(
        x.reshape(nseq * seq, d), loss_target.reshape(nseq * seq, d), mods, lw, final_g.reshape(1, d), nseq=nseq,
        big_w=big_w, grad_sink=grad_sink, small_sink=small_sink)

    done = grad_x
    for l, grp in ((1, "ffn"), (1, "w_out"), (1, "w_in"), (0, "ffn"), (0, "w_out")):
        done = rs_finish(l, grp, done)
    srcs, lands = _xc_wait(False, early_gather["ssem"], early_gather["rsem"], early_gather["srcs"],
                           early_gather["lands"], done, "small_wait")
    gathered = [lax.dynamic_update_index_in_dim(land, src, me, 0) for src, land in zip(srcs, lands)]
    gathered += _all_gather([small[0]["norm1_g"], dmods[0]], "gather_late", dep=gathered[0])
    gath = dict(zip(["p1024", "p1536", "p2816", "p16", "p128", "p6144", "late1024", "late6144"], gathered))

    dmod_all = jnp.concatenate([gath["late6144"].reshape(1, N_DEV * nseq, N_MOD * d),
                                jnp.transpose(gath["p6144"].reshape(N_DEV, DEPTH, nseq, N_MOD * d)[:, 1:], (1, 0, 2, 3)).reshape(
                                    DEPTH - 1, N_DEV * nseq, N_MOD * d)], axis=0)
    small_names = _P1024 + ["final_g", "ssd_conv_b", "ff_conv_b"] + _P16 + ["gm_ws", "gm_bs", "ada_b"]
    wmv = {}
    for nme in small_names:
        if nme == "final_g":
            wmv[nme] = tuple(a.reshape(1, d) for a in (wts[nme], mom[nme], var[nme]))
        else:
            wmv[nme] = (wts[nme], mom[nme], var[nme])
    small_out, scw_full, fcw_full, loss_sum = _adamw_small(gath, wmv)
    loss = loss_sum[0, 0]
    rs_finish(0, "w_in", scw_full)
    for nme in small_names:
        outs[nme] = small_out[nme]
    outs["final_g"] = tuple(a.reshape(d) for a in outs["final_g"])

    n_scw, n_fcw = ssd_conv_w.shape[2], ff_conv_w.shape[2]
    g_scw_mine = lax.dynamic_slice_in_dim(scw_full, me * n_scw, n_scw, axis=2)
    g_fcw_mine = lax.dynamic_slice_in_dim(fcw_full, me * n_fcw, n_fcw, axis=2)
    outs["ssd_conv_w"] = _adamw_sharded([(g_scw_mine, lambda p: 0)], ssd_conv_w, m_ssd_conv_w, v_ssd_conv_w, me_arr, "adamw_ssd_conv_w")
    outs["ff_conv_w"] = _adamw_sharded([(g_fcw_mine, lambda p: 0)], ff_conv_w, m_ff_conv_w, v_ff_conv_w, me_arr, "adamw_ff_conv_w")

    dmod_cols = _b(lax.dynamic_slice_in_dim(dmod_all, me * n_ada, n_ada, axis=2))
    g_ada = jnp.stack([_matmul(c_act, dmod_cols[l], ta=True, name=f"mm_ada_dw_{l}") for l in range(DEPTH)])
    outs["ada_w"] = _adamw_sharded([(g_ada, lambda p: 0)], ada_w, m_ada_w, v_ada_w, me_arr, "adamw_ada_w")

    for nme in _TRANSPOSED:
        outs[nme] = tuple(jnp.transpose(a, (0, 2, 1)) for a in outs[nme])
    result = [loss, grad_x.reshape(nseq, seq, d)]
    for k in range(4):
        result += [outs[n][k] for n in _WEIGHTS]
    return tuple(result)
```

```python
import functools
import math

import jax
import jax.numpy as jnp
from jax import lax
from jax.experimental import pallas as pl
from jax.experimental.pallas import tpu as pltpu

F32 = jnp.float32
BF16 = jnp.bfloat16

N_DEV = 8
D_MODEL = 1024
DEPTH = 2
CHUNK = 128
SSD_HEADS = 16
SSD_HEAD_DIM = 64
SSD_GROUPS = 2
HEADS_PER_GROUP = SSD_HEADS // SSD_GROUPS
GROUP_WIDTH = HEADS_PER_GROUP * SSD_HEAD_DIM
D_STATE = 128
D_SSD = 1024
CONV_DIM = 1536
SSD_CONV = 4
GM_HEADS = 8
GM_HEAD_DIM = 128
D_GM = 1024
D_FF = 2816
FF_CONV = 3
N_IN = 4624
N_MOD = 6
EPS = 1e-6

N_INP = 5120
COL_U, COL_V, COL_Z, COL_XBC, COL_DT = 0, 1024, 2048, 3072, 4608
DT_BLOCK = 512

ADAM_LR = 0.001
ADAM_B1 = 0.9
ADAM_B2 = 0.999
ADAM_EPS = 1e-08
ADAM_WD = 0.01
ADAM_STEP = 10

VMEM_LIMIT = 56 * 1024 * 1024
MESH = pl.DeviceIdType.MESH
ANY = pl.BlockSpec(memory_space=pl.ANY)


def _cp(*sem):
    return pltpu.CompilerParams(dimension_semantics=sem, vmem_limit_bytes=VMEM_LIMIT)


def _tile(n, pref):
    if n <= pref or n % 128:
        return n
    best = 128
    for t in range(128, pref + 1, 128):
        if n % t == 0:
            best = t
    return best


def _silu(x):
    return x * jax.nn.sigmoid(x)


def _gelu(x):
    return 0.5 * x * (1.0 + lax.erf(x * (1.0 / math.sqrt(2.0))))


def _softplus(x):
    return jnp.maximum(x, 0.0) + jnp.log1p(jnp.exp(-jnp.abs(x)))


def _rms(x, g, width):
    return x * lax.rsqrt(jnp.sum(x * x, axis=-1, keepdims=True) / width + EPS) * g


def _b(x):
    return x.astype(BF16)


_NN = (((1,), (0,)), ((), ()))
_NT = (((1,), (1,)), ((), ()))
_TN = (((0,), (0,)), ((), ()))


def _dg(a, b, dn):
    return lax.dot_general(_b(a), _b(b), dn, preferred_element_type=F32)


@jax.custom_vjp
def _bdot(a, b):
    return _dg(a, b, _NN)


def _bdot_fwd(a, b):
    return _dg(a, b, _NN), (a, b)


def _bdot_bwd(res, ct):
    a, b = res
    return _dg(ct, b, _NT), _dg(a, ct, _TN)


_bdot.defvjp(_bdot_fwd, _bdot_bwd)


@jax.custom_vjp
def _bdot_nt(a, b):
    return _dg(a, b, _NT)


def _bdot_nt_fwd(a, b):
    return _dg(a, b, _NT), (a, b)


def _bdot_nt_bwd(res, ct):
    a, b = res
    return _dg(ct, b, _NN), _dg(ct, a, _TN)


_bdot_nt.defvjp(_bdot_nt_fwd, _bdot_nt_bwd)


@jax.custom_vjp
def _bdot_tn(a, b):
    return _dg(a, b, _TN)


def _bdot_tn_fwd(a, b):
    return _dg(a, b, _TN), (a, b)


def _bdot_tn_bwd(res, ct):
    a, b = res
    return _dg(b, ct, _NT), _dg(a, ct, _NN)


_bdot_tn.defvjp(_bdot_tn_fwd, _bdot_tn_bwd)


def _tri(n, lower):
    r = lax.broadcasted_iota(jnp.int32, (n, n), 0)
    c = lax.broadcasted_iota(jnp.int32, (n, n), 1)
    return ((r >= c) if lower else (r <= c)).astype(F32)


def _eye(n):
    r = lax.broadcasted_iota(jnp.int32, (n, n), 0)
    c = lax.broadcasted_iota(jnp.int32, (n, n), 1)
    return (r == c).astype(F32)


def _hdot(a, b, dn):
    return lax.dot_general(a, b, dn, precision=lax.Precision.HIGHEST, preferred_element_type=F32)


@jax.custom_vjp
def _cumsum_rows(x):
    return _hdot(_tri(x.shape[0], True), x, _NN)


def _cumsum_rows_fwd(x):
    return _cumsum_rows(x), None


def _cumsum_rows_bwd(_, ct):
    return (_hdot(_tri(ct.shape[0], False), ct, _NN),)


_cumsum_rows.defvjp(_cumsum_rows_fwd, _cumsum_rows_bwd)


@jax.custom_vjp
def _transpose(x):
    return _hdot(_eye(x.shape[1]), x, _NT)


def _transpose_fwd(x):
    return _transpose(x), None


def _transpose_bwd(_, ct):
    return (_hdot(_eye(ct.shape[1]), ct, _NT),)


_transpose.defvjp(_transpose_fwd, _transpose_bwd)


MXU_WIDTH = 256
MATMUL_TILE_CAP = 2816
MATMUL_VMEM = 44 * 1024 * 1024


def _mxu_tiles(n):
    if n <= MATMUL_TILE_CAP or n % 128:
        return [n]
    for unit in (MXU_WIDTH, 128):
        opts = [t for t in range(unit, MATMUL_TILE_CAP + 1, unit) if n % t == 0]
        if opts:
            return opts
    return [n]


def _matmul(a, b, *, ta=False, tb=False, name, dep=None, out_dtype=F32):
    pieces = list(a) if isinstance(a, (list, tuple)) else [a]
    npc = len(pieces)
    rows, width = pieces[0].shape
    assert all(p.shape == (rows, width) for p in pieces)
    if ta:
        k_dim, m_dim = rows, width * npc
    else:
        m_dim, k_dim = rows, width * npc
    if tb:
        n_dim, kb = b.shape
    else:
        kb, n_dim = b.shape
    assert kb == k_dim, (pieces[0].shape, npc, b.shape, ta, tb)
    m_unit = width if npc > 1 and ta else m_dim
    k_unit = width if npc > 1 and not ta else k_dim
    tm = _tile(m_unit, 1536)
    tn_opts, tk_opts = _mxu_tiles(n_dim), _mxu_tiles(k_unit)
    tn, tk = tn_opts.pop(), tk_opts.pop()
    while 4 * (tm * tk + tk * tn) + 8 * tm * tn > MATMUL_VMEM:
        if tn >= tk and tn_opts:
            tn = tn_opts.pop()
        else:
            tk = tk_opts.pop()
    ni, nj, nk = m_dim // tm, n_dim // tn, k_dim // tk
    per = width // (tm if ta else tk)
    dn = (((0 if ta else 1,), (1 if tb else 0,)), ((), ()))

    a_bytes, b_bytes = m_dim * k_dim, k_dim * n_dim
    m_outer = nk > 1 or a_bytes + b_bytes * ni <= b_bytes + a_bytes * nj
    if m_outer:
        ij = lambda o, n, k: (o, n)
        grid = (ni, nj, nk)
    else:
        ij = lambda o, n, k: (n, o)
        grid = (nj, ni, nk)

    use_acc = nk > 1 and out_dtype != F32

    def body(*refs):
        a_refs, b_ref = refs[:npc], refs[npc]
        o_ref = refs[-2] if use_acc else refs[-1]
        acc_ref = refs[-1]
        k = pl.program_id(2)
        i = pl.program_id(0 if m_outer else 1)
        along = i if ta else k

        def step(a_ref):
            p = lax.dot_general(a_ref[...], b_ref[...], dn, preferred_element_type=F32)
            if nk == 1:
                o_ref[...] = p.astype(out_dtype)
            else:
                @pl.when(k == 0)
                def _():
                    acc_ref[...] = p

                @pl.when((k > 0) & (k < nk - 1 if use_acc else True))
                def _():
                    acc_ref[...] += p

                if use_acc:
                    @pl.when(k == nk - 1)
                    def _():
                        o_ref[...] = (acc_ref[...] + p).astype(out_dtype)

        if npc == 1:
            step(a_refs[0])
        else:
            for pc in range(npc):
                pl.when((along >= pc * per) & (along < (pc + 1) * per))(functools.partial(step, a_refs[pc]))

    def a_map(pc, o, n, k):
        i, _ = ij(o, n, k)
        along = i if ta else k
        if npc > 1:
            along = jnp.clip(along - pc * per, 0, per - 1)
        return (k, along) if ta else (i, along)

    def b_map(o, n, k):
        _, j = ij(o, n, k)
        return (j, k) if tb else (k, j)

    extra = [] if dep is None else [dep]
    return pl.pallas_call(
        body, name=name,
        grid=grid,
        in_specs=[pl.BlockSpec((tk, tm) if ta else (tm, tk), functools.partial(a_map, pc)) for pc in range(npc)]
        + [pl.BlockSpec((tn, tk) if tb else (tk, tn), b_map)] + [ANY] * len(extra),
        out_specs=pl.BlockSpec((tm, tn), lambda o, n, k: ij(o, n, k)),
        out_shape=jax.ShapeDtypeStruct((m_dim, n_dim), out_dtype),
        scratch_shapes=[pltpu.VMEM((tm, tn), F32)] if use_acc else [],
        compiler_params=_cp("parallel", "parallel", "arbitrary"),
    )(*pieces, b, *extra)


def _ada_fwd(c_all, ada_w, ada_b_shard):
    depth, d, n = ada_w.shape
    nb = c_all.shape[0]

    def body(c_ref, w_ref, b_ref, o_ref, ca_ref):
        ca = _silu(c_ref[...])
        ca_ref[...] = _b(ca)
        o_ref[0] = _dg(ca, w_ref[0], _NN) + b_ref[0]

    return pl.pallas_call(
        body, name="ada_fwd",
        grid=(depth,),
        in_specs=[pl.BlockSpec((nb, d), lambda l: (0, 0)),
                  pl.BlockSpec((1, d, n), lambda l: (l, 0, 0)),
                  pl.BlockSpec((1, 1, n), lambda l: (l, 0, 0))],
        out_specs=[pl.BlockSpec((1, nb, n), lambda l: (l, 0, 0)),
                   pl.BlockSpec((nb, d), lambda l: (0, 0))],
        out_shape=[jax.ShapeDtypeStruct((depth, nb, n), F32), jax.ShapeDtypeStruct((nb, d), BF16)],
        compiler_params=_cp("arbitrary"),
    )(c_all, ada_w, ada_b_shard)


def _normmod_f(x, g, sc, sh):
    return _rms(x, g, D_MODEL) * (1.0 + sc) + sh


def _row_tile(seq):
    return min(seq, 256)


def _normmod_fwd(xin, delta, gate, g, sc, sh, *, nseq, name):
    t, d = xin.shape
    seq = t // nseq
    tr = _row_tile(seq)
    nt = seq // tr
    has_delta = delta is not None
    row = pl.BlockSpec((tr, d), lambda s, i: (s * nt + i, 0))
    per_seq = pl.BlockSpec((1, 1, d), lambda s, i: (s, 0, 0))
    vec = pl.BlockSpec((1, d), lambda s, i: (0, 0))

    if has_delta:
        def body(xin_ref, delta_ref, gate_ref, g_ref, sc_ref, sh_ref, x_ref, h_ref):
            x = xin_ref[...] + gate_ref[0] * delta_ref[...]
            x_ref[...] = x
            h_ref[...] = _b(_normmod_f(x, g_ref[...], sc_ref[0], sh_ref[0]))

        return pl.pallas_call(
            body, name=name, grid=(nseq, nt),
            in_specs=[row, row, per_seq, vec, per_seq, per_seq],
            out_specs=[row, row],
            out_shape=[jax.ShapeDtypeStruct((t, d), F32), jax.ShapeDtypeStruct((t, d), BF16)],
            compiler_params=_cp("parallel", "parallel"),
        )(xin, delta, gate, g, sc, sh)

    def body0(xin_ref, g_ref, sc_ref, sh_ref, h_ref):
        h_ref[...] = _b(_normmod_f(xin_ref[...], g_ref[...], sc_ref[0], sh_ref[0]))

    h = pl.pallas_call(
        body0, name=name, grid=(nseq, nt),
        in_specs=[row, vec, per_seq, per_seq],
        out_specs=row,
        out_shape=jax.ShapeDtypeStruct((t, d), BF16),
        compiler_params=_cp("parallel", "parallel"),
    )(xin, g, sc, sh)
    return xin, h


def _normmod_bwd(dh, dxo, x, delta, gate, g, sc, *, nseq, name):
    t, d = x.shape
    seq = t // nseq
    tr = _row_tile(seq)
    nt = seq // tr
    has_delta = delta is not None
    row = pl.BlockSpec((tr, d), lambda s, i: (s * nt + i, 0))
    per_seq = pl.BlockSpec((1, 1, d), lambda s, i: (s, 0, 0))
    vec = pl.BlockSpec((1, d), lambda s, i: (0, 0))

    def core(dh_ref, dxo_ref, x_ref, g_ref, sc_ref, dx_ref, dg_ref, dsc_ref, dsh_ref):
        s, i = pl.program_id(0), pl.program_id(1)
        dh_v = dh_ref[...]
        _, vjp = jax.vjp(lambda xx, gg, ss: _normmod_f(xx, gg, ss, 0.0), x_ref[...], g_ref[...], sc_ref[0])
        dxn, dg_t, dsc_t = vjp(dh_v)
        dx = dxo_ref[...] + dxn
        dx_ref[...] = dx
        dsh_t = jnp.sum(dh_v, axis=0, keepdims=True)

        @pl.when((s == 0) & (i == 0))
        def _():
            dg_ref[...] = jnp.zeros_like(dg_ref)

        @pl.when(i == 0)
        def _():
            dsc_ref[...] = jnp.zeros_like(dsc_ref)
            dsh_ref[...] = jnp.zeros_like(dsh_ref)

        dg_ref[...] += dg_t
        dsc_ref[0] += dsc_t
        dsh_ref[0] += dsh_t
        return dx

    if has_delta:
        def body(dh_ref, dxo_ref, x_ref, delta_ref, gate_ref, g_ref, sc_ref,
                 dx_ref, dd_ref, dgate_ref, dg_ref, dsc_ref, dsh_ref):
            dx = core(dh_ref, dxo_ref, x_ref, g_ref, sc_ref, dx_ref, dg_ref, dsc_ref, dsh_ref)
            dd_ref[...] = _b(dx * gate_ref[0])

            @pl.when(pl.program_id(1) == 0)
            def _():
                dgate_ref[...] = jnp.zeros_like(dgate_ref)

            dgate_ref[0] += jnp.sum(dx * delta_ref[...], axis=0, keepdims=True)

        return pl.pallas_call(
            body, name=name, grid=(nseq, nt),
            in_specs=[row, row, row, row, per_seq, vec, per_seq],
            out_specs=[row, row, per_seq, vec, per_seq, per_seq],
            out_shape=[jax.ShapeDtypeStruct((t, d), F32), jax.ShapeDtypeStruct((t, d), BF16),
                       jax.ShapeDtypeStruct((nseq, 1, d), F32), jax.ShapeDtypeStruct((1, d), F32),
                       jax.ShapeDtypeStruct((nseq, 1, d), F32), jax.ShapeDtypeStruct((nseq, 1, d), F32)],
            compiler_params=_cp("arbitrary", "arbitrary"),
        )(dh, dxo, x, delta, gate, g, sc)

    def body0(dh_ref, dxo_ref, x_ref, g_ref, sc_ref, dx_ref, dg_ref, dsc_ref, dsh_ref):
        core(dh_ref, dxo_ref, x_ref, g_ref, sc_ref, dx_ref, dg_ref, dsc_ref, dsh_ref)

    dx, dg, dsc, dsh = pl.pallas_call(
        body0, name=name, grid=(nseq, nt),
        in_specs=[row, row, row, vec, per_seq],
        out_specs=[row, vec, per_seq, per_seq],
        out_shape=[jax.ShapeDtypeStruct((t, d), F32), jax.ShapeDtypeStruct((1, d), F32),
                   jax.ShapeDtypeStruct((nseq, 1, d), F32), jax.ShapeDtypeStruct((nseq, 1, d), F32)],
        compiler_params=_cp("arbitrary", "arbitrary"),
    )(dh, dxo, x, g, sc)
    return dx, None, None, dg, dsc, dsh


def _final_loss(xin, delta, gate, fg, target, *, nseq):
    t, d = xin.shape
    seq = t // nseq
    tr = _row_tile(seq)
    nt = seq // tr
    row = pl.BlockSpec((tr, d), lambda s, i: (s * nt + i, 0))
    per_seq = pl.BlockSpec((1, 1, d), lambda s, i: (s, 0, 0))
    vec = pl.BlockSpec((1, d), lambda s, i: (0, 0))

    def body(xin_ref, delta_ref, gate_ref, fg_ref, tgt_ref, loss_ref, dx_ref, dd_ref, dgate_ref, dfg_ref):
        s, i = pl.program_id(0), pl.program_id(1)
        dl = delta_ref[...]
        x = xin_ref[...] + gate_ref[0] * dl
        y, vjp = jax.vjp(lambda xx, gg: _rms(xx, gg, D_MODEL), x, fg_ref[...])
        err = y - tgt_ref[...]
        dx, dfg_t = vjp(err * (1.0 / d))
        dx_ref[...] = dx
        dd_ref[...] = _b(dx * gate_ref[0])

        @pl.when((s == 0) & (i == 0))
        def _():
            loss_ref[...] = jnp.zeros_like(loss_ref)
            dfg_ref[...] = jnp.zeros_like(dfg_ref)

        @pl.when(i == 0)
        def _():
            dgate_ref[...] = jnp.zeros_like(dgate_ref)

        loss_ref[...] += jnp.sum(err * err) * (0.5 / d)
        dfg_ref[...] += dfg_t
        dgate_ref[0] += jnp.sum(dx * dl, axis=0, keepdims=True)

    return pl.pallas_call(
        body, name="final_loss", grid=(nseq, nt),
        in_specs=[row, row, per_seq, vec, row],
        out_specs=[pl.BlockSpec((1, 128), lambda s, i: (0, 0)), row, row, per_seq, vec],
        out_shape=[jax.ShapeDtypeStruct((1, 128), F32), jax.ShapeDtypeStruct((t, d), F32),
                   jax.ShapeDtypeStruct((t, d), BF16), jax.ShapeDtypeStruct((nseq, 1, d), F32),
                   jax.ShapeDtypeStruct((1, d), F32)],
        compiler_params=_cp("arbitrary", "arbitrary"),
    )(xin, delta, gate, fg, target)


CONV_TC = 256
CONV_LANES = 128
CONV_ROWS = 64
CONV_HALO = 8


def _conv_slabs(seq, fn):
    def step(i, carry):
        r0 = pl.multiple_of(i * CONV_ROWS, CONV_ROWS)
        for h in range(CONV_TC // CONV_LANES):
            fn(r0, slice(h * CONV_LANES, (h + 1) * CONV_LANES))
        return carry

    lax.fori_loop(0, seq // CONV_ROWS, step, 0)


def _slab(ref, r0, cols, seq, before=0, after=0):
    parts = []
    if before:
        top = ref[pl.ds(pl.multiple_of(jnp.maximum(r0 - before, 0), CONV_HALO), before), cols]
        parts.append(jnp.where(r0 > 0, top, 0.0))
    parts.append(ref[pl.ds(r0, CONV_ROWS), cols])
    if after:
        bot = ref[pl.ds(pl.multiple_of(jnp.minimum(r0 + CONV_ROWS, seq - after), CONV_HALO), after), cols]
        parts.append(jnp.where(r0 + CONV_ROWS < seq, bot, 0.0))
    return jnp.concatenate(parts, axis=0) if len(parts) > 1 else parts[0]


def _conv_block(x, w_ref, b_ref):
    kw = w_ref.shape[0]
    rows = lax.broadcasted_iota(jnp.int32, x.shape, 0)
    y = b_ref[...] + w_ref[kw - 1:kw, :] * x
    for j in range(1, kw):
        y = y + w_ref[kw - 1 - j:kw - j, :] * jnp.where(rows >= j, pltpu.roll(x, j, 0), 0.0)
    return y


def _conv_bwd(dy_ext, x, w_ref, dw_ref, db_ref, cols):
    kw = w_ref.shape[0]
    n = dy_ext.shape[0]
    dy = dy_ext[:CONV_ROWS]
    dx = w_ref[kw - 1:kw, cols] * dy
    dw_ref[kw - 1:kw, cols] += jnp.sum(dy * x, axis=0, keepdims=True)
    for j in range(1, kw):
        dy_j = pltpu.roll(dy_ext, n - j, 0)[:CONV_ROWS]
        dx = dx + w_ref[kw - 1 - j:kw - j, cols] * dy_j
        dw_ref[kw - 1 - j:kw - j, cols] += jnp.sum(dy_j * x, axis=0, keepdims=True)
    db_ref[:, cols] += jnp.sum(dy, axis=0, keepdims=True)
    return dx


def _dsilu(pre):
    sg = jax.nn.sigmoid(pre)
    return pre * sg, sg * (1.0 + pre * (1.0 - sg))


def _ssd_conv_fwd(proj, w, b, *, nseq):
    t = proj.shape[0]
    seq = t // nseq
    nb = CONV_DIM // CONV_TC
    off = COL_XBC // CONV_TC

    def body(x_ref, w_ref, b_ref, o_ref, pre_ref):
        pre = _conv_block(x_ref[...], w_ref, b_ref)
        pre_ref[...] = pre
        o_ref[...] = _silu(pre)

    col = pl.BlockSpec((seq, CONV_TC), lambda j, s: (s, j))
    return pl.pallas_call(
        body, name="ssd_conv_fwd", grid=(nb, nseq),
        in_specs=[pl.BlockSpec((seq, CONV_TC), lambda j, s: (s, off + j)),
                  pl.BlockSpec((SSD_CONV, CONV_TC), lambda j, s: (0, j)),
                  pl.BlockSpec((1, CONV_TC), lambda j, s: (0, j))],
        out_specs=[col, col],
        out_shape=[jax.ShapeDtypeStruct((t, CONV_DIM), F32)] * 2,
        compiler_params=_cp("parallel", "parallel"),
    )(proj, w, b)


def _ssd_conv_bwd(dact, pre, proj, w, dproj, *, nseq):
    t = proj.shape[0]
    seq = t // nseq
    nb = CONV_DIM // CONV_TC
    off = COL_XBC // CONV_TC

    def body(da_ref, pre_ref, x_ref, w_ref, dproj_ref, dx_ref, dw_ref, db_ref):
        del dproj_ref

        @pl.when(pl.program_id(1) == 0)
        def _():
            dw_ref[...] = jnp.zeros_like(dw_ref)
            db_ref[...] = jnp.zeros_like(db_ref)

        def slab(r0, cols):
            _, dsilu = _dsilu(_slab(pre_ref, r0, cols, seq, after=CONV_HALO))
            dpre_ext = _slab(da_ref, r0, cols, seq, after=CONV_HALO) * dsilu
            x = x_ref[pl.ds(r0, CONV_ROWS), cols]
            dx_ref[pl.ds(r0, CONV_ROWS), cols] = _b(_conv_bwd(dpre_ext, x, w_ref, dw_ref, db_ref, cols))

        _conv_slabs(seq, slab)

    return pl.pallas_call(
        body, name="ssd_conv_bwd", grid=(nb, nseq),
        in_specs=[pl.BlockSpec((seq, CONV_TC), lambda j, s: (s, j)),
                  pl.BlockSpec((seq, CONV_TC), lambda j, s: (s, j)),
                  pl.BlockSpec((seq, CONV_TC), lambda j, s: (s, off + j)),
                  pl.BlockSpec((SSD_CONV, CONV_TC), lambda j, s: (0, j)),
                  ANY],
        out_specs=[pl.BlockSpec((seq, CONV_TC), lambda j, s: (s, off + j)),
                   pl.BlockSpec((SSD_CONV, CONV_TC), lambda j, s: (0, j)),
                   pl.BlockSpec((1, CONV_TC), lambda j, s: (0, j))],
        out_shape=[jax.ShapeDtypeStruct(dproj.shape, dproj.dtype), jax.ShapeDtypeStruct((SSD_CONV, CONV_DIM), F32),
                   jax.ShapeDtypeStruct((1, CONV_DIM), F32)],
        input_output_aliases={4: 0},
        compiler_params=_cp("parallel", "arbitrary"),
    )(dact, pre, proj, w, dproj)


def _ffn_act_fwd(up, w, b, *, nseq):
    t = up.shape[0]
    seq = t // nseq
    nb = D_FF // CONV_TC

    def body(g_ref, v_ref, w_ref, b_ref, o_ref, pre_ref):
        pre = _conv_block(g_ref[...], w_ref, b_ref)
        pre_ref[...] = pre
        o_ref[...] = _b(_silu(pre) * v_ref[...])

    col = pl.BlockSpec((seq, CONV_TC), lambda j, s: (s, j))
    return pl.pallas_call(
        body, name="ffn_act_fwd", grid=(nb, nseq),
        in_specs=[col,
                  pl.BlockSpec((seq, CONV_TC), lambda j, s: (s, nb + j)),
                  pl.BlockSpec((FF_CONV, CONV_TC), lambda j, s: (0, j)),
                  pl.BlockSpec((1, CONV_TC), lambda j, s: (0, j))],
        out_specs=[col, col],
        out_shape=[jax.ShapeDtypeStruct((t, D_FF), BF16), jax.ShapeDtypeStruct((t, D_FF), F32)],
        compiler_params=_cp("parallel", "parallel"),
    )(up, up, w, b)


def _ffn_act_bwd(dact, pre, up, w, *, nseq):
    t = up.shape[0]
    seq = t // nseq
    nb = D_FF // CONV_TC

    def body(da_ref, pre_ref, g_ref, v_ref, w_ref, dg_ref, dv_ref, dw_ref, db_ref):
        @pl.when(pl.program_id(1) == 0)
        def _():
            dw_ref[...] = jnp.zeros_like(dw_ref)
            db_ref[...] = jnp.zeros_like(db_ref)

        def slab(r0, cols):
            silu, dsilu = _dsilu(_slab(pre_ref, r0, cols, seq, after=CONV_HALO))
            da_ext = _slab(da_ref, r0, cols, seq, after=CONV_HALO)
            dv_ref[pl.ds(r0, CONV_ROWS), cols] = _b(da_ext[:CONV_ROWS] * silu[:CONV_ROWS])
            dpre_ext = da_ext * _slab(v_ref, r0, cols, seq, after=CONV_HALO) * dsilu
            gate = g_ref[pl.ds(r0, CONV_ROWS), cols]
            dg_ref[pl.ds(r0, CONV_ROWS), cols] = _b(_conv_bwd(dpre_ext, gate, w_ref, dw_ref, db_ref, cols))

        _conv_slabs(seq, slab)

    col = pl.BlockSpec((seq, CONV_TC), lambda j, s: (s, j))
    return pl.pallas_call(
        body, name="ffn_act_bwd", grid=(nb, nseq),
        in_specs=[col, col, col,
                  pl.BlockSpec((seq, CONV_TC), lambda j, s: (s, nb + j)),
                  pl.BlockSpec((FF_CONV, CONV_TC), lambda j, s: (0, j))],
        out_specs=[col, col,
                   pl.BlockSpec((FF_CONV, CONV_TC), lambda j, s: (0, j)),
                   pl.BlockSpec((1, CONV_TC), lambda j, s: (0, j))],
        out_shape=[jax.ShapeDtypeStruct((t, D_FF), BF16), jax.ShapeDtypeStruct((t, D_FF), BF16),
                   jax.ShapeDtypeStruct((FF_CONV, D_FF), F32), jax.ShapeDtypeStruct((1, D_FF), F32)],
        compiler_params=_cp("parallel", "arbitrary"),
    )(dact, pre, up, up, w)


SSD_PAIRS = SSD_HEADS // 2
PAIR_W = 2 * SSD_HEAD_DIM
PAIRS_PER_GROUP = SSD_PAIRS // SSD_GROUPS


def _ssd_chunk(xs, bg, cg, dtr, z, hp, dtb, alog, dskip, ng):
    n = dtr.shape[0]
    dt = _softplus(dtr + dtb)
    cs = _cumsum_rows(dt * (-jnp.exp(alog)))
    cs_t = _transpose(cs)
    lane = lax.broadcasted_iota(jnp.int32, (1, SSD_HEADS), 1)
    sub = lax.broadcasted_iota(jnp.int32, (SSD_HEADS, 1), 0)
    row = lax.broadcasted_iota(jnp.int32, (n, 1), 0)
    causal = lax.broadcasted_iota(jnp.int32, (n, n), 0) >= lax.broadcasted_iota(jnp.int32, (n, n), 1)
    first = lax.broadcasted_iota(jnp.int32, (1, PAIR_W), 1) < SSD_HEAD_DIM
    first_rows = lax.broadcasted_iota(jnp.int32, (PAIR_W, 1), 0) < SSD_HEAD_DIM
    first_f = first.astype(F32)
    cb = [_bdot_nt(cg[g], bg[g]) for g in range(SSD_GROUPS)]
    ys, hn = [], []
    for p in range(SSD_PAIRS):
        g = p // PAIRS_PER_GROUP
        col, decay, last = [], [], []
        for h in (2 * p, 2 * p + 1):
            oh = (lane == h).astype(F32)
            cs_h = jnp.sum(cs * oh, axis=1, keepdims=True)
            cs_row = jnp.sum(cs_t * (sub == h).astype(F32), axis=0, keepdims=True)
            col.append((jnp.sum(dt * oh, axis=1, keepdims=True), cs_h, jnp.sum(dskip * oh, axis=1, keepdims=True)))
            last.append(jnp.sum(jnp.where(row == n - 1, cs_h, 0.0), axis=0, keepdims=True))
            decay.append(jnp.where(causal, jnp.exp(jnp.where(causal, cs_h - cs_row, 0.0)), 0.0))
        pair = lambda a, b: jnp.where(first, a, b)
        dt_p = pair(col[0][0], col[1][0])
        cs_p = pair(col[0][1], col[1][1])
        last_p = pair(last[0], last[1])
        xc = xs[p] * dt_p
        y = _bdot(cb[g] * decay[0], xc * first_f) + _bdot(cb[g] * decay[1], xc * (1.0 - first_f))
        y = y + _bdot_nt(cg[g], hp[p]) * jnp.exp(cs_p)
        y = y + pair(col[0][2], col[1][2]) * xs[p]
        keep = jnp.where(first_rows, jnp.exp(last[0]), jnp.exp(last[1]))
        hn.append(keep * hp[p] + _bdot_tn(xc * jnp.exp(last_p - cs_p), bg[g]))
        ys.append(y * _silu(z[p]))
    outs = []
    for g in range(SSD_GROUPS):
        ps = range(g * PAIRS_PER_GROUP, (g + 1) * PAIRS_PER_GROUP)
        ms = sum(jnp.sum(ys[p] * ys[p], axis=1, keepdims=True) for p in ps) * (1.0 / GROUP_WIDTH)
        r = lax.rsqrt(ms + EPS)
        outs += [ys[p] * r * ng[p] for p in ps]
    return outs, hn


def _hslices(ref, width, count, base=0):
    return [ref[:, base + k * width: base + (k + 1) * width] for k in range(count)]


def _ssd_load(xbc_ref, z_ref, dt_ref, ng_ref):
    xs = _hslices(xbc_ref, PAIR_W, SSD_PAIRS)
    bg = _hslices(xbc_ref, D_STATE, SSD_GROUPS, D_SSD)
    cg = _hslices(xbc_ref, D_STATE, SSD_GROUPS, D_SSD + SSD_GROUPS * D_STATE)
    z = _hslices(z_ref, PAIR_W, SSD_PAIRS)
    ng = _hslices(ng_ref, PAIR_W, SSD_PAIRS)
    return xs, bg, cg, dt_ref[:, 0:SSD_HEADS], z, ng


def _ssd_specs(nch):
    rowi = lambda s, c: s * nch + c
    return [pl.BlockSpec((CHUNK, CONV_DIM), lambda s, c: (rowi(s, c), 0)),
            pl.BlockSpec((CHUNK, D_SSD), lambda s, c: (rowi(s, c), COL_Z // D_SSD)),
            pl.BlockSpec((CHUNK, 128), lambda s, c: (rowi(s, c), COL_DT // 128)),
            pl.BlockSpec((1, SSD_HEADS), lambda s, c: (0, 0)),
            pl.BlockSpec((1, SSD_HEADS), lambda s, c: (0, 0)),
            pl.BlockSpec((1, SSD_HEADS), lambda s, c: (0, 0)),
            pl.BlockSpec((1, D_SSD), lambda s, c: (0, 0))]


def _ssd_fwd(xbc, proj, dtb, alog, dskip, ng, *, nseq):
    t = proj.shape[0]
    nch = t // nseq // CHUNK
    hd = PAIR_W

    def body(xbc_ref, z_ref, dt_ref, dtb_ref, alog_ref, dsk_ref, ng_ref, y_ref, hp_ref, h_ref):
        @pl.when(pl.program_id(1) == 0)
        def _():
            h_ref[...] = jnp.zeros_like(h_ref)

        xs, bg, cg, dtr, z, ngs = _ssd_load(xbc_ref, z_ref, dt_ref, ng_ref)
        hp_ref[0] = h_ref[...]
        hp = [h_ref[h * hd:(h + 1) * hd, :] for h in range(SSD_PAIRS)]
        outs, hn = _ssd_chunk(xs, bg, cg, dtr, z, hp, dtb_ref[...], alog_ref[...], dsk_ref[...], ngs)
        for h in range(SSD_PAIRS):
            y_ref[:, h * hd:(h + 1) * hd] = _b(outs[h])
            h_ref[h * hd:(h + 1) * hd, :] = hn[h]

    return pl.pallas_call(
        body, name="ssd_fwd", grid=(nseq, nch),
        in_specs=_ssd_specs(nch),
        out_specs=[pl.BlockSpec((CHUNK, D_SSD), lambda s, c: (s * nch + c, 0)),
                   pl.BlockSpec((1, D_SSD, D_STATE), lambda s, c: (s * nch + c, 0, 0))],
        out_shape=[jax.ShapeDtypeStruct((t, D_SSD + D_GM), BF16),
                   jax.ShapeDtypeStruct((t // CHUNK, D_SSD, D_STATE), F32)],
        scratch_shapes=[pltpu.VMEM((D_SSD, D_STATE), F32)],
        compiler_params=_cp("arbitrary", "arbitrary"),
    )(xbc, proj, proj, dtb, alog, dskip, ng)


def _ssd_bwd(dy, xbc, proj, hprev, dtb, alog, dskip, ng, *, nseq):
    t = proj.shape[0]
    nch = t // nseq // CHUNK
    hd = PAIR_W
    rev = lambda s, c: s * nch + (nch - 1 - c)

    def body(dy_ref, xbc_ref, z_ref, dt_ref, hp_ref, dtb_ref, alog_ref, dsk_ref, ng_ref,
             dxbc_ref, dproj_ref, ddtb_ref, dalog_ref, ddsk_ref, dng_ref, dh_ref):
        first = (pl.program_id(0) == 0) & (pl.program_id(1) == 0)

        @pl.when(pl.program_id(1) == 0)
        def _():
            dh_ref[...] = jnp.zeros_like(dh_ref)

        @pl.when(first)
        def _():
            ddtb_ref[...] = jnp.zeros_like(ddtb_ref)
            dalog_ref[...] = jnp.zeros_like(dalog_ref)
            ddsk_ref[...] = jnp.zeros_like(ddsk_ref)
            dng_ref[...] = jnp.zeros_like(dng_ref)

        xs, bg, cg, dtr, z, ngs = _ssd_load(xbc_ref, z_ref, dt_ref, ng_ref)
        hp = [hp_ref[0, h * hd:(h + 1) * hd, :] for h in range(SSD_PAIRS)]
        _, vjp = jax.vjp(_ssd_chunk, xs, bg, cg, dtr, z, hp, dtb_ref[...], alog_ref[...], dsk_ref[...], ngs)
        douts = [dy_ref[:, h * hd:(h + 1) * hd] for h in range(SSD_PAIRS)]
        dhn = [dh_ref[h * hd:(h + 1) * hd, :] for h in range(SSD_PAIRS)]
        dxs, dbg, dcg, ddtr, dz, dhp, ddtb, dalog, ddsk, dngs = vjp((douts, dhn))
        dproj_ref[:, :COL_Z] = jnp.zeros((CHUNK, COL_Z), BF16)
        dproj_ref[:, COL_XBC:] = jnp.zeros((CHUNK, N_INP - COL_XBC), BF16)
        for h in range(SSD_PAIRS):
            dxbc_ref[:, h * hd:(h + 1) * hd] = dxs[h]
            dproj_ref[:, COL_Z + h * hd: COL_Z + (h + 1) * hd] = _b(dz[h])
            dh_ref[h * hd:(h + 1) * hd, :] = dhp[h]
            dng_ref[:, h * hd:(h + 1) * hd] += dngs[h]
        for g in range(SSD_GROUPS):
            dxbc_ref[:, D_SSD + g * D_STATE: D_SSD + (g + 1) * D_STATE] = dbg[g]
            dxbc_ref[:, D_SSD + (SSD_GROUPS + g) * D_STATE: D_SSD + (SSD_GROUPS + g + 1) * D_STATE] = dcg[g]
        dproj_ref[:, COL_DT:COL_DT + SSD_HEADS] = _b(ddtr)
        ddtb_ref[...] += ddtb
        dalog_ref[...] += dalog
        ddsk_ref[...] += ddsk

    small = pl.BlockSpec((1, SSD_HEADS), lambda s, c: (0, 0))
    return pl.pallas_call(
        body, name="ssd_bwd", grid=(nseq, nch),
        in_specs=[pl.BlockSpec((CHUNK, D_SSD), lambda s, c: (rev(s, c), 0)),
                  pl.BlockSpec((CHUNK, CONV_DIM), lambda s, c: (rev(s, c), 0)),
                  pl.BlockSpec((CHUNK, D_SSD), lambda s, c: (rev(s, c), COL_Z // D_SSD)),
                  pl.BlockSpec((CHUNK, 128), lambda s, c: (rev(s, c), COL_DT // 128)),
                  pl.BlockSpec((1, D_SSD, D_STATE), lambda s, c: (rev(s, c), 0, 0)),
                  small, small, small,
                  pl.BlockSpec((1, D_SSD), lambda s, c: (0, 0))],
        out_specs=[pl.BlockSpec((CHUNK, CONV_DIM), lambda s, c: (rev(s, c), 0)),
                   pl.BlockSpec((CHUNK, N_INP), lambda s, c: (rev(s, c), 0)),
                   small, small, small,
                   pl.BlockSpec((1, D_SSD), lambda s, c: (0, 0))],
        out_shape=[jax.ShapeDtypeStruct((t, CONV_DIM), F32), jax.ShapeDtypeStruct((t, N_INP), BF16),
                   jax.ShapeDtypeStruct((1, SSD_HEADS), F32), jax.ShapeDtypeStruct((1, SSD_HEADS), F32),
                   jax.ShapeDtypeStruct((1, SSD_HEADS), F32), jax.ShapeDtypeStruct((1, D_SSD), F32)],
        scratch_shapes=[pltpu.VMEM((D_SSD, D_STATE), F32)],
        compiler_params=_cp("arbitrary", "arbitrary"),
    )(dy, xbc, proj, proj, hprev, dtb, alog, dskip, ng)


def _gmlp_chunk(gu, gv, ws, bs_cols, vg, og):
    n = gu[0].shape[0]
    mask = _tri(n, True)
    au = [_gelu(t) for t in gu]
    av = [_gelu(t) for t in gv]
    r = lax.rsqrt(sum(jnp.sum(t * t, axis=1, keepdims=True) for t in av) * (1.0 / D_GM) + EPS)
    p = []
    for h in range(GM_HEADS):
        sv = _bdot(ws[h] * mask, av[h] * r * vg[h]) + bs_cols[h]
        p.append(au[h] * sv)
    r2 = lax.rsqrt(sum(jnp.sum(t * t, axis=1, keepdims=True) for t in p) * (1.0 / D_GM) + EPS)
    return [p[h] * r2 * og[h] for h in range(GM_HEADS)]


def _gmlp_load(u_ref, v_ref, ws_ref, bst_ref, vg_ref, og_ref):
    gu = _hslices(u_ref, GM_HEAD_DIM, GM_HEADS)
    gv = _hslices(v_ref, GM_HEAD_DIM, GM_HEADS)
    ws = [ws_ref[h] for h in range(GM_HEADS)]
    bs_cols = [bst_ref[:, h:h + 1] for h in range(GM_HEADS)]
    return gu, gv, ws, bs_cols, _hslices(vg_ref, GM_HEAD_DIM, GM_HEADS), _hslices(og_ref, GM_HEAD_DIM, GM_HEADS)


def _gmlp_specs():
    return [pl.BlockSpec((CHUNK, D_GM), lambda i: (i, COL_U // D_GM)),
            pl.BlockSpec((CHUNK, D_GM), lambda i: (i, COL_V // D_GM)),
            pl.BlockSpec((GM_HEADS, CHUNK, CHUNK), lambda i: (0, 0, 0)),
            pl.BlockSpec((CHUNK, GM_HEADS), lambda i: (0, 0)),
            pl.BlockSpec((1, D_GM), lambda i: (0, 0)),
            pl.BlockSpec((1, D_GM), lambda i: (0, 0))]


def _gmlp_fwd(proj, ycat, ws, bst, vg, og):
    t = proj.shape[0]

    def body(u_ref, v_ref, ws_ref, bst_ref, vg_ref, og_ref, ycat_ref, o_ref):
        del ycat_ref
        outs = _gmlp_chunk(*_gmlp_load(u_ref, v_ref, ws_ref, bst_ref, vg_ref, og_ref))
        for h in range(GM_HEADS):
            o_ref[:, h * GM_HEAD_DIM:(h + 1) * GM_HEAD_DIM] = _b(outs[h])

    return pl.pallas_call(
        body, name="gmlp_fwd", grid=(t // CHUNK,),
        in_specs=_gmlp_specs() + [ANY],
        out_specs=pl.BlockSpec((CHUNK, D_GM), lambda i: (i, D_SSD // D_GM)),
        out_shape=jax.ShapeDtypeStruct(ycat.shape, ycat.dtype),
        input_output_aliases={6: 0},
        compiler_params=_cp("parallel"),
    )(proj, proj, ws, bst, vg, og, ycat)


def _gmlp_bwd(dy, proj, ws, bst, vg, og, dproj):
    t = proj.shape[0]
    w = GM_HEAD_DIM

    def body(dy_ref, u_ref, v_ref, ws_ref, bst_ref, vg_ref, og_ref, dproj_ref,
             dgm_ref, dws_ref, dbst_ref, dvg_ref, dog_ref):
        del dproj_ref

        @pl.when(pl.program_id(0) == 0)
        def _():
            dws_ref[...] = jnp.zeros_like(dws_ref)
            dbst_ref[...] = jnp.zeros_like(dbst_ref)
            dvg_ref[...] = jnp.zeros_like(dvg_ref)
            dog_ref[...] = jnp.zeros_like(dog_ref)

        _, vjp = jax.vjp(_gmlp_chunk, *_gmlp_load(u_ref, v_ref, ws_ref, bst_ref, vg_ref, og_ref))
        dgu, dgv, dws, dbs, dvg, dog = vjp(_hslices(dy_ref, w, GM_HEADS))
        for h in range(GM_HEADS):
            dgm_ref[:, h * w:(h + 1) * w] = _b(dgu[h])
            dgm_ref[:, D_GM + h * w: D_GM + (h + 1) * w] = _b(dgv[h])
            dws_ref[h] += dws[h]
            dbst_ref[:, h:h + 1] += dbs[h]
            dvg_ref[:, h * w:(h + 1) * w] += dvg[h]
            dog_ref[:, h * w:(h + 1) * w] += dog[h]

    return pl.pallas_call(
        body, name="gmlp_bwd", grid=(t // CHUNK,),
        in_specs=[pl.BlockSpec((CHUNK, D_GM), lambda i: (i, 1))] + _gmlp_specs() + [ANY],
        out_specs=[pl.BlockSpec((CHUNK, 2 * D_GM), lambda i: (i, COL_U // (2 * D_GM))),
                   pl.BlockSpec((GM_HEADS, CHUNK, CHUNK), lambda i: (0, 0, 0)),
                   pl.BlockSpec((CHUNK, GM_HEADS), lambda i: (0, 0)),
                   pl.BlockSpec((1, D_GM), lambda i: (0, 0)),
                   pl.BlockSpec((1, D_GM), lambda i: (0, 0))],
        out_shape=[jax.ShapeDtypeStruct(dproj.shape, dproj.dtype), jax.ShapeDtypeStruct((GM_HEADS, CHUNK, CHUNK), F32),
                   jax.ShapeDtypeStruct((CHUNK, GM_HEADS), F32), jax.ShapeDtypeStruct((1, D_GM), F32),
                   jax.ShapeDtypeStruct((1, D_GM), F32)],
        input_output_aliases={7: 0},
        compiler_params=_cp("arbitrary"),
    )(dy, proj, proj, ws, bst, vg, og, dproj)


def _local_step(x, target, mods, lw, final_g, *, nseq, big_w, grad_sink, small_sink):
    saved = []
    xin, delta, gate = x, None, None
    for l in range(DEPTH):
        w = lw[l]
        sh1, sc1, g1, sh2, sc2, g2 = mods[l]
        x0, h1 = _normmod_fwd(xin, delta, gate, w["norm1_g"], sc1, sh1, nseq=nseq, name=f"norm1_fwd_{l}")
        w_in = big_w(l, "w_in", h1)
        proj = _matmul(h1, w_in, tb=True, name=f"mm_in_{l}")
        xbc, xbc_pre = _ssd_conv_fwd(proj, w["ssd_conv_w"], w["ssd_conv_b"], nseq=nseq)
        ycat, hprev = _ssd_fwd(xbc, proj, w["ssd_dt_bias"], w["ssd_a_log"], w["ssd_d"], w["ssd_norm_g"], nseq=nseq)
        ycat = _gmlp_fwd(proj, ycat, w["gm_ws"], w["gm_bst"], w["gm_vnorm_g"], w["gm_out_g"])
        w_out = big_w(l, "w_out", ycat)
        mix = _matmul(ycat, w_out, name=f"mm_out_{l}")
        x1, h2 = _normmod_fwd(x0, mix, g1, w["norm2_g"], sc2, sh2, nseq=nseq, name=f"norm2_fwd_{l}")
        ff_up = big_w(l, "ff_up", h2)
        up = _matmul(h2, ff_up, tb=True, name=f"mm_up_{l}")
        act, ff_pre = _ffn_act_fwd(up, w["ff_conv_w"], w["ff_conv_b"], nseq=nseq)
        ff_down = big_w(l, "ff_down", act)
        dn = _matmul(act, ff_down, name=f"mm_down_{l}")
        saved.append(dict(x0=x0, xin_delta=delta, xin_gate=gate, h1=h1, proj=proj, xbc=xbc, xbc_pre=xbc_pre, hprev=hprev,
                          ycat=ycat, mix=mix, x1=x1, h2=h2, up=up, ff_pre=ff_pre, act=act, dn=dn,
                          w_in=w_in, w_out=w_out, ff_up=ff_up, ff_down=ff_down))
        xin, delta, gate = x1, dn, g2

    loss, dx, ddelta, dgate, dfg = _final_loss(xin, delta, gate, final_g, target, nseq=nseq)

    small, dmods = [None] * DEPTH, [None] * DEPTH
    for l in reversed(range(DEPTH)):
        w, sv = lw[l], saved[l]
        sh1, sc1, g1, sh2, sc2, g2 = mods[l]
        dg2 = dgate
        g_ff_down = _matmul(sv["act"], ddelta, ta=True, name=f"mm_down_dw_{l}", out_dtype=BF16)
        dact = _matmul(ddelta, sv["ff_down"], tb=True, name=f"mm_down_dx_{l}")
        dgate_ff, dval_ff, dfcw, dfcb = _ffn_act_bwd(dact, sv["ff_pre"], sv["up"], w["ff_conv_w"], nseq=nseq)
        g_ff_up = _matmul([dgate_ff, dval_ff], sv["h2"], ta=True, name=f"mm_up_dw_{l}", out_dtype=BF16)
        dep = grad_sink(l, "ffn", dict(ff_down=g_ff_down, ff_up=g_ff_up), dval_ff)
        dh2 = _matmul([dgate_ff, dval_ff], sv["ff_up"], name=f"mm_up_dx_{l}", dep=dep)
        dx, dmix, dg1, dn2g, dsc2, dsh2 = _normmod_bwd(dh2, dx, sv["x1"], sv["mix"], g1, w["norm2_g"], sc2,
                                                       nseq=nseq, name=f"norm2_bwd_{l}")
        g_w_out = _matmul(sv["ycat"], dmix, ta=True, name=f"mm_out_dw_{l}", out_dtype=BF16)
        dep = grad_sink(l, "w_out", dict(w_out=g_w_out), dmix)
        dycat = _matmul(dmix, sv["w_out"], tb=True, name=f"mm_out_dx_{l}", dep=dep)
        dxbc_act, dproj, ddtb, dalog, ddsk, dng = _ssd_bwd(dycat, sv["xbc"], sv["proj"], sv["hprev"], w["ssd_dt_bias"],
                                                          w["ssd_a_log"], w["ssd_d"], w["ssd_norm_g"], nseq=nseq)
        dproj, dscw, dscb = _ssd_conv_bwd(dxbc_act, sv["xbc_pre"], sv["proj"], w["ssd_conv_w"], dproj, nseq=nseq)
        dproj, dws, dbst, dvg, dog = _gmlp_bwd(dycat, sv["proj"], w["gm_ws"], w["gm_bst"], w["gm_vnorm_g"], w["gm_out_g"], dproj)
        early = dict(norm2_g=dn2g, ssd_norm_g=dng, gm_vnorm_g=dvg, gm_out_g=dog,
                     ssd_conv_w=dscw, ssd_conv_b=dscb, ff_conv_w=dfcw, ff_conv_b=dfcb,
                     ssd_dt_bias=ddtb, ssd_a_log=dalog, ssd_d=ddsk, gm_ws=dws, gm_bs=dbst.T)
        dep = small_sink(l, early, small, dmods, dfg, loss)
        g_w_in = _matmul(dproj, sv["h1"], ta=True, name=f"mm_in_dw_{l}", out_dtype=BF16, dep=dep)
        dep = grad_sink(l, "w_in", dict(w_in=g_w_in), dproj)
        dh1 = _matmul(dproj, sv["w_in"], name=f"mm_in_dx_{l}", dep=dep)
        dx, ddelta, dgate, dn1g, dsc1, dsh1 = _normmod_bwd(dh1, dx, sv["x0"], sv["xin_delta"], sv["xin_gate"],
                                                           w["norm1_g"], sc1, nseq=nseq, name=f"norm1_bwd_{l}")
        small[l] = dict(early, norm1_g=dn1g)
        dmods[l] = jnp.concatenate([dsh1, dsc1, dg1, dsh2, dsc2, dg2], axis=-1)[:, 0, :]
    return dx, small, dmods


def _all_gather(arrs, name, dep=None):
    n = len(arrs)
    extra = [] if dep is None else [dep]

    def body(*refs):
        ins, outs = refs[:n], refs[n + len(extra):2 * n + len(extra)]
        send_sems, recv_sems, local_sems = refs[2 * n + len(extra):]
        x, y, c = lax.axis_index("x"), lax.axis_index("y"), lax.axis_index("c")
        me, sibling = (x, y, c), (x, y, 1 - c)
        chips = [(1 - x, y), (x, 1 - y), (1 - x, 1 - y)]

        def copy(i, k, block, to, src=None):
            px, py, pc = block
            dst = outs[i].at[4 * px + 2 * py + pc]
            return pltpu.make_async_remote_copy(
                src_ref=dst if src is None else src, dst_ref=dst,
                send_sem=send_sems.at[7 * i + k], recv_sem=recv_sems.at[7 * i + k],
                device_id=to, device_id_type=MESH)

        mine = [pltpu.make_async_copy(ins[i], outs[i].at[4 * x + 2 * y + c], local_sems.at[i]) for i in range(n)]
        for cp in mine:
            cp.start()
        first = []
        for i in range(n):
            first.append(copy(i, 0, me, sibling, src=ins[i]))
            first += [copy(i, 1 + j, me, (*chip, c), src=ins[i]) for j, chip in enumerate(chips)]
        for cp in first:
            cp.start()
        passed = []
        for j, chip in enumerate(chips):
            for i in range(n):
                copy(i, 1 + j, (*chip, c), me).wait_recv()
                fwd = copy(i, 4 + j, (*chip, c), sibling)
                fwd.start()
                passed.append(fwd)
        for i in range(n):
            copy(i, 0, sibling, me).wait_recv()
            for j, chip in enumerate(chips):
                copy(i, 4 + j, (*chip, 1 - c), me).wait_recv()
        for cp in first + passed:
            cp.wait_send()
        for cp in mine:
            cp.wait()

    return pl.pallas_call(
        body, name=name,
        in_specs=[ANY] * (n + len(extra)), out_specs=[ANY] * n,
        out_shape=[jax.ShapeDtypeStruct((N_DEV,) + a.shape, a.dtype) for a in arrs],
        scratch_shapes=[pltpu.SemaphoreType.DMA((7 * n,)), pltpu.SemaphoreType.DMA((7 * n,)),
                        pltpu.SemaphoreType.DMA((n,))],
    )(*arrs, *extra)


def _exchange_sibling(arrs, name):
    n = len(arrs)

    def body(*refs):
        ins, outs = refs[:n], refs[n:2 * n]
        send_sems, recv_sems = refs[2 * n:]
        x, y, c = lax.axis_index("x"), lax.axis_index("y"), lax.axis_index("c")
        copies = []
        for i in range(n):
            for k in range(4):
                copies.append(pltpu.make_async_remote_copy(
                    src_ref=ins[i].at[2 * k + (1 - c)], dst_ref=outs[i].at[k],
                    send_sem=send_sems.at[4 * i + k], recv_sem=recv_sems.at[4 * i + k],
                    device_id=(x, y, 1 - c), device_id_type=MESH))
        for cp in copies:
            cp.start()
        for cp in copies:
            cp.wait_recv()
        for cp in copies:
            cp.wait_send()

    return pl.pallas_call(
        body, name=name,
        in_specs=[ANY] * n, out_specs=[ANY] * n,
        out_shape=[jax.ShapeDtypeStruct((4,) + a.shape[1:], a.dtype) for a in arrs],
        scratch_shapes=[pltpu.SemaphoreType.DMA((4 * n,)), pltpu.SemaphoreType.DMA((4 * n,))],
    )(*arrs)


def _exchange_chips(arrs, name):
    n = len(arrs)

    def body(*refs):
        ins, outs = refs[:n], refs[n:2 * n]
        send_sems, recv_sems = refs[2 * n:]
        x, y, c = lax.axis_index("x"), lax.axis_index("y"), lax.axis_index("c")
        chips = [(1 - x, y), (x, 1 - y), (1 - x, 1 - y)]
        copies = []
        for i in range(n):
            for j, (cx, cy) in enumerate(chips):
                copies.append(pltpu.make_async_remote_copy(
                    src_ref=ins[i].at[2 * cx + cy], dst_ref=outs[i].at[j],
                    send_sem=send_sems.at[3 * i + j], recv_sem=recv_sems.at[3 * i + j],
                    device_id=(cx, cy, c), device_id_type=MESH))
        for cp in copies:
            cp.start()
        for cp in copies:
            cp.wait_recv()
        for cp in copies:
            cp.wait_send()

    return pl.pallas_call(
        body, name=name,
        in_specs=[ANY] * n, out_specs=[ANY] * n,
        out_shape=[jax.ShapeDtypeStruct((3,) + a.shape[1:], a.dtype) for a in arrs],
        scratch_shapes=[pltpu.SemaphoreType.DMA((3 * n,)), pltpu.SemaphoreType.DMA((3 * n,))],
    )(*arrs)


def _add_sibling(a, r, pos, name):
    _, depth, rows, cols = a.shape
    tr = _tile(rows, 256) if rows % 8 == 0 else rows
    a3 = a.reshape(N_DEV * depth, rows, cols)
    r3 = r.reshape(4 * depth, rows, cols)

    def body(pos_ref, a_ref, r_ref, o_ref):
        o_ref[...] = a_ref[...] + r_ref[...]

    out = pl.pallas_call(
        body, name=name,
        grid_spec=pltpu.PrefetchScalarGridSpec(
            num_scalar_prefetch=1, grid=(4 * depth, rows // tr),
            in_specs=[pl.BlockSpec((1, tr, cols), lambda q, i, p: ((2 * (q // depth) + p[0]) * depth + q % depth, i, 0)),
                      pl.BlockSpec((1, tr, cols), lambda q, i, p: (q, i, 0))],
            out_specs=pl.BlockSpec((1, tr, cols), lambda q, i, p: (q, i, 0))),
        out_shape=jax.ShapeDtypeStruct((4 * depth, rows, cols), F32),
        compiler_params=_cp("parallel", "parallel"),
    )(pos, a3, r3)
    return out.reshape(4, depth, rows, cols)


HBM = pl.BlockSpec(memory_space=pltpu.HBM)
SEM = pl.BlockSpec(memory_space=pltpu.SEMAPHORE)
EFFECT = pltpu.SideEffectType.DATAFLOW_SIDE_EFFECTING


def _peer(k):
    x, y, c = lax.axis_index("x"), lax.axis_index("y"), lax.axis_index("c")
    return (1 - x if k & 4 else x, 1 - y if k & 2 else y, 1 - c if k & 1 else c)


ALL_PEERS = tuple(range(1, N_DEV))
OTHER_CHIPS = (2, 4, 6)


def _xc_copies(scatter, srcs, lands, send_sems, recv_sems, peers=ALL_PEERS):
    x, y, c = lax.axis_index("x"), lax.axis_index("y"), lax.axis_index("c")
    copies = []
    for i in range(len(srcs)):
        for k in peers:
            px, py, pc = _peer(k)
            src = srcs[i].at[4 * px + 2 * py + pc] if scatter else srcs[i]
            dst = lands[i].at[k - 1] if scatter else lands[i].at[4 * x + 2 * y + c]
            copies.append(pltpu.make_async_remote_copy(
                src_ref=src, dst_ref=dst, send_sem=send_sems[i].at[k - 1], recv_sem=recv_sems[i].at[k - 1],
                device_id=(px, py, pc), device_id_type=MESH))
    return copies


def _xc_start(scatter, arrs, after, name, peers=ALL_PEERS):
    n = len(arrs)
    lands = [lax.empty((N_DEV - 1,) + a.shape[1:] if scatter else (N_DEV,) + a.shape, a.dtype) for a in arrs]

    def body(*refs):
        srcs, lnd = refs[:n], refs[n:2 * n]
        send_sems, recv_sems = refs[2 * n + 1:3 * n + 1], refs[3 * n + 1:4 * n + 1]
        token = refs[6 * n + 1]
        for cp in _xc_copies(scatter, srcs, lnd, send_sems, recv_sems, peers):
            cp.start()
        token[...] = jnp.zeros_like(token)

    outs = pl.pallas_call(
        body, name=name,
        out_shape=[pltpu.SemaphoreType.DMA((N_DEV - 1,))] * (2 * n)
        + [pltpu.HBM(a.shape, a.dtype) for a in arrs] + [pltpu.HBM(a.shape, a.dtype) for a in lands]
        + [jax.ShapeDtypeStruct((8, 128), F32)],
        in_specs=[HBM] * (2 * n) + [ANY],
        out_specs=[SEM] * (2 * n) + [HBM] * (2 * n) + [pl.BlockSpec(memory_space=pltpu.VMEM)],
        input_output_aliases={i: 2 * n + i for i in range(2 * n)},
        compiler_params=pltpu.CompilerParams(has_side_effects=EFFECT),
    )(*[pltpu.with_memory_space_constraint(a, pltpu.HBM) for a in list(arrs) + lands], after)
    return outs[:n], outs[n:2 * n], outs[2 * n:3 * n], outs[3 * n:4 * n], outs[4 * n][0, 0]


def _xc_wait(scatter, send_sems, recv_sems, srcs, lands, after, name, peers=ALL_PEERS):
    n = len(srcs)

    def body(*refs):
        s_refs, l_refs = refs[:n], refs[n:2 * n]
        ss, rs = refs[2 * n:3 * n], refs[3 * n:4 * n]
        for cp in _xc_copies(scatter, s_refs, l_refs, ss, rs, peers):
            cp.wait_send()
            cp.wait_recv()

    outs = pl.pallas_call(
        body, name=name,
        out_shape=[pltpu.HBM(a.shape, a.dtype) for a in list(srcs) + list(lands)],
        in_specs=[HBM] * (2 * n) + [SEM] * (2 * n) + [ANY],
        out_specs=[HBM] * (2 * n),
        input_output_aliases={i: i for i in range(2 * n)},
        compiler_params=pltpu.CompilerParams(has_side_effects=EFFECT),
    )(*srcs, *lands, *send_sems, *recv_sems, after)
    return outs[:n], outs[n:]


def _sib_copies(zones, send_sems, recv_sems):
    x, y, c = lax.axis_index("x"), lax.axis_index("y"), lax.axis_index("c")
    copies = []
    for i in range(len(zones)):
        for q in range(N_DEV // 2):
            slot = zones[i].at[2 * q + c]
            copies.append(pltpu.make_async_remote_copy(
                src_ref=slot, dst_ref=slot, send_sem=send_sems[i].at[q], recv_sem=recv_sems[i].at[q],
                device_id=(x, y, 1 - c), device_id_type=MESH))
    return copies


def _sib_start(zones, name):
    n = len(zones)

    def body(*refs):
        for cp in _sib_copies(refs[:n], refs[n:2 * n], refs[2 * n:3 * n]):
            cp.start()

    outs = pl.pallas_call(
        body, name=name,
        out_shape=[pltpu.SemaphoreType.DMA((N_DEV // 2,))] * (2 * n) + [pltpu.HBM(a.shape, a.dtype) for a in zones],
        in_specs=[HBM] * n,
        out_specs=[SEM] * (2 * n) + [HBM] * n,
        input_output_aliases={i: 2 * n + i for i in range(n)},
        compiler_params=pltpu.CompilerParams(has_side_effects=EFFECT),
    )(*[pltpu.with_memory_space_constraint(a, pltpu.HBM) for a in zones])
    return outs[:n], outs[n:2 * n], outs[2 * n:]


def _sib_wait(send_sems, recv_sems, zones, name):
    n = len(zones)

    def body(*refs):
        for cp in _sib_copies(refs[:n], refs[n:2 * n], refs[2 * n:3 * n]):
            cp.wait_send()
            cp.wait_recv()

    return pl.pallas_call(
        body, name=name,
        out_shape=[pltpu.HBM(a.shape, a.dtype) for a in zones],
        in_specs=[HBM] * n + [SEM] * (2 * n),
        out_specs=[HBM] * n,
        input_output_aliases={i: i for i in range(n)},
        compiler_params=pltpu.CompilerParams(has_side_effects=EFFECT),
    )(*zones, *send_sems, *recv_sems)


def _adamw_math(w, g, m, v):
    m = ADAM_B1 * m + (1.0 - ADAM_B1) * g
    v = ADAM_B2 * v + (1.0 - ADAM_B2) * (g * g)
    m_hat = m / (1.0 - ADAM_B1 ** ADAM_STEP)
    v_hat = v / (1.0 - ADAM_B2 ** ADAM_STEP)
    delta = -ADAM_LR * (m_hat / (jnp.sqrt(v_hat) + ADAM_EPS) + ADAM_WD * w)
    return delta, m, v


def _adamw_sharded(parts, w, m, v, pos, name):
    depth, rows, cols = w.shape
    tr = _tile(rows, 256) if rows % 8 == 0 else rows
    npart = len(parts)

    def body(pos_ref, *refs):
        prefs = refs[:npart]
        w_ref, m_ref, v_ref, g_out, d_out, m_out, v_out = refs[npart:]
        g = prefs[0][...]
        for pr in prefs[1:]:
            g = g + pr[...]
        delta, mn, vn = _adamw_math(w_ref[...], g, m_ref[...], v_ref[...])
        g_out[...] = g
        d_out[...] = delta
        m_out[...] = mn
        v_out[...] = vn

    def part_spec(fn):
        return pl.BlockSpec((1, tr, cols), lambda l, i, p: (fn(p) * depth + l, i, 0))

    blk = pl.BlockSpec((1, tr, cols), lambda l, i, p: (l, i, 0))
    shp = jax.ShapeDtypeStruct((depth, rows, cols), F32)
    return pl.pallas_call(
        body, name=name,
        grid_spec=pltpu.PrefetchScalarGridSpec(
            num_scalar_prefetch=1, grid=(depth, rows // tr),
            in_specs=[part_spec(fn) for _, fn in parts] + [blk, blk, blk],
            out_specs=[blk, blk, blk, blk]),
        out_shape=[shp, shp, shp, shp],
        compiler_params=_cp("parallel", "parallel"),
    )(pos, *[a for a, _ in parts], w, m, v)


def _adamw_layer(parts, w, m, v, pos, layer, prev, name):
    depth, rows, cols = w.shape
    npart = len(parts)
    nprev = 0 if prev is None else 4
    if rows % 16 == 0:
        tr, tc = max(t for t in range(16, 257, 16) if rows % t == 0), cols
    else:
        tr, tc = rows, _tile(cols, 256)
    pick = (lambda i: (i, 0)) if rows % 16 == 0 else (lambda i: (0, i))

    def body(pos_ref, *refs):
        prefs = refs[:npart]
        w_ref, m_ref, v_ref = refs[npart:npart + 3]
        g_out, d_out, m_out, v_out = refs[npart + 3 + nprev:]
        g = prefs[0][...].astype(F32)
        for pr in prefs[1:]:
            g = g + pr[...].astype(F32)
        delta, mn, vn = _adamw_math(w_ref[...], g, m_ref[...], v_ref[...])
        g_out[...] = g
        d_out[...] = delta
        m_out[...] = mn
        v_out[...] = vn

    def part_spec(fn):
        return pl.BlockSpec((1, tr, tc), lambda i, p: (fn(p), *pick(i)))

    blk = pl.BlockSpec((1, tr, tc), lambda i, p: (layer, *pick(i)))
    shp = jax.ShapeDtypeStruct((depth, rows, cols), F32)
    first_prev = 1 + npart + 3
    return pl.pallas_call(
        body, name=name,
        grid_spec=pltpu.PrefetchScalarGridSpec(
            num_scalar_prefetch=1, grid=(rows // tr * (cols // tc),),
            in_specs=[part_spec(fn) for _, fn in parts] + [blk, blk, blk] + [ANY] * nprev,
            out_specs=[blk, blk, blk, blk]),
        out_shape=[shp, shp, shp, shp],
        input_output_aliases={first_prev + j: j for j in range(nprev)},
        compiler_params=_cp("parallel"),
    )(pos, *[a for a, _ in parts], w, m, v, *(prev or ()))


_P1024 = ["norm1_g", "norm2_g", "ssd_norm_g", "gm_vnorm_g", "gm_out_g"]
_P16 = ["ssd_dt_bias", "ssd_a_log", "ssd_d"]


def _adamw_small(gath, wmv):
    names = list(wmv.keys())
    classes = list(gath.keys())
    flat_in = [gath[k] for k in classes]
    for nme in names:
        flat_in += list(wmv[nme])
    out_shapes = []
    for nme in names:
        out_shapes += [jax.ShapeDtypeStruct(wmv[nme][0].shape, F32)] * 4
    out_shapes += [jax.ShapeDtypeStruct((DEPTH, SSD_CONV, CONV_DIM), F32), jax.ShapeDtypeStruct((DEPTH, FF_CONV, D_FF), F32),
                   jax.ShapeDtypeStruct((1, SSD_HEADS), F32)]
    scratch = [pltpu.VMEM(gath[k].shape[1:], F32) for k in classes]
    ncls = len(classes)

    def body(*refs):
        g_refs = dict(zip(classes, refs[:ncls]))
        pos = ncls
        w_refs = {}
        for nme in names:
            w_refs[nme] = refs[pos:pos + 3]
            pos += 3
        o_refs = {}
        for nme in names:
            o_refs[nme] = refs[pos:pos + 4]
            pos += 4
        scw_out, fcw_out, loss_out = refs[pos], refs[pos + 1], refs[pos + 2]
        s_refs = dict(zip(classes, refs[pos + 3:]))
        for k in classes:
            acc = g_refs[k][0]
            for dev in range(1, N_DEV):
                acc = acc + g_refs[k][dev]
            s_refs[k][...] = acc

        def apply(nme, grad_of):
            w_ref, m_ref, v_ref = w_refs[nme]
            g_out, d_out, m_out, v_out = o_refs[nme]
            shape = w_ref.shape
            if len(shape) == 2:
                idxs = [(slice(l, l + 1),) for l in range(shape[0])]
            elif len(shape) == 3:
                idxs = [(l,) for l in range(shape[0])]
            else:
                idxs = [(l, h) for l in range(shape[0]) for h in range(shape[1])]
            for n_i, ix in enumerate(idxs):
                g = grad_of(n_i)
                delta, mn, vn = _adamw_math(w_ref[ix], g, m_ref[ix], v_ref[ix])
                g_out[ix] = g
                d_out[ix] = delta
                m_out[ix] = mn
                v_out[ix] = vn

        s1024, s1536, s2816, s16, s128, s6144, late1024, late6144 = (s_refs[k] for k in classes)
        s1024[0:1, :] += late1024[...]
        s6144[0:late6144.shape[0], :] += late6144[...]
        for n_i, nme in enumerate(_P1024):
            apply(nme, lambda l, b=2 * n_i: s1024[b + l:b + l + 1, :])
        apply("final_g", lambda l: s1024[10:11, :])
        apply("ssd_conv_b", lambda l: s1536[8 + l:9 + l, :])
        apply("ff_conv_b", lambda l: s2816[6 + l:7 + l, :])
        for n_i, nme in enumerate(_P16):
            apply(nme, lambda l, b=2 * n_i: s16[b + l:b + l + 1, :])
        apply("gm_ws", lambda q: s128[q * CHUNK:(q + 1) * CHUNK, :])
        apply("gm_bs", lambda l: s128[2048 + 8 * l:2048 + 8 * (l + 1), :])
        apply("ada_b", lambda l: s6144[2 * l:2 * l + 1, :] + s6144[2 * l + 1:2 * l + 2, :])
        for l in range(DEPTH):
            scw_out[l] = s1536[SSD_CONV * l:SSD_CONV * (l + 1), :]
            fcw_out[l] = s2816[FF_CONV * l:FF_CONV * (l + 1), :]
        loss_out[...] = s16[2 * len(_P16):2 * len(_P16) + 1, :]

    outs = pl.pallas_call(
        body, name="adamw_small",
        out_shape=out_shapes,
        scratch_shapes=scratch,
        compiler_params=pltpu.CompilerParams(vmem_limit_bytes=VMEM_LIMIT),
    )(*flat_in)
    res = {nme: tuple(outs[4 * i:4 * i + 4]) for i, nme in enumerate(names)}
    return res, outs[-3], outs[-2], outs[-1]


_WEIGHTS = ['ada_w', 'ada_b', 'norm1_g', 'norm2_g', 'w_in', 'ssd_conv_w', 'ssd_conv_b', 'ssd_dt_bias', 'ssd_a_log',
            'ssd_d', 'ssd_norm_g', 'gm_vnorm_g', 'gm_ws', 'gm_bs', 'gm_out_g', 'w_out', 'ff_up', 'ff_conv_w',
            'ff_conv_b', 'ff_down', 'final_g']


_O_XBC, _O_DT, _O_GM = D_SSD, D_SSD + CONV_DIM, D_SSD + CONV_DIM + SSD_HEADS


_TRANSPOSED = ("w_in", "ff_up")


def _full_weight(name, g):
    full = g.reshape(g.shape[0] * g.shape[1], g.shape[2])
    if name != "w_in":
        return full
    zpad = jnp.zeros((N_INP - N_IN, full.shape[1]), full.dtype)
    return jnp.concatenate([full[_O_GM:], full[:_O_XBC], full[_O_XBC:_O_DT], full[_O_DT:_O_GM], zpad], axis=0)


def _by_owner(name, grad):
    if name == "w_in":
        grad = jnp.concatenate([grad[COL_Z:COL_XBC], grad[COL_XBC:COL_DT], grad[COL_DT:COL_DT + SSD_HEADS], grad[:COL_Z]], axis=0)
    return grad.reshape(N_DEV, grad.shape[0] // N_DEV, grad.shape[1])


def kernel(x, c, ada_w, ada_b, norm1_g, norm2_g, w_in, ssd_conv_w, ssd_conv_b, ssd_dt_bias, ssd_a_log, ssd_d, ssd_norm_g, gm_vnorm_g, gm_ws, gm_bs, gm_out_g, w_out, ff_up, ff_conv_w, ff_conv_b, ff_down, final_g, loss_target, m_ada_w, m_ada_b, m_norm1_g, m_norm2_g, m_w_in, m_ssd_conv_w, m_ssd_conv_b, m_ssd_dt_bias, m_ssd_a_log, m_ssd_d, m_ssd_norm_g, m_gm_vnorm_g, m_gm_ws, m_gm_bs, m_gm_out_g, m_w_out, m_ff_up, m_ff_conv_w, m_ff_conv_b, m_ff_down, m_final_g, v_ada_w, v_ada_b, v_norm1_g, v_norm2_g, v_w_in, v_ssd_conv_w, v_ssd_conv_b, v_ssd_dt_bias, v_ssd_a_log, v_ssd_d, v_ssd_norm_g, v_gm_vnorm_g, v_gm_ws, v_gm_bs, v_gm_out_g, v_w_out, v_ff_up, v_ff_conv_w, v_ff_conv_b, v_ff_down, v_final_g):
    given = dict(locals())
    wts = {n: given[n] for n in _WEIGHTS}
    mom = {n: given["m_" + n] for n in _WEIGHTS}
    var = {n: given["v_" + n] for n in _WEIGHTS}
    nseq, seq, d = x.shape
    ix, iy, ic = lax.axis_index("x"), lax.axis_index("y"), lax.axis_index("c")
    me = 4 * ix + 2 * iy + ic
    me_arr = me.astype(jnp.int32).reshape(1)

    for nme in _TRANSPOSED:
        wts[nme], mom[nme], var[nme] = (jnp.transpose(a, (0, 2, 1)) for a in (wts[nme], mom[nme], var[nme]))

    def shard(l, name):
        return _b(wts[name][l])

    g_scw, g_fcw, c_all = _all_gather([ssd_conv_w, ff_conv_w, c], "gather_first")
    first_ssem, first_rsem, first_src, first_land, first_zero = _xc_start(
        False, [shard(0, "w_in")], c_all, "ag_first_start", peers=OTHER_CHIPS)
    later = [(0, "w_out"), (0, "ff_up"), (0, "ff_down"), (1, "w_in"), (1, "w_out"), (1, "ff_up"), (1, "ff_down")]
    ag_ssem, ag_rsem, ag_src, ag_land, ag_zero = _xc_start(
        False, [shard(l, n) for l, n in later], first_zero.reshape(1, 1), "ag_start")
    ag_groups = {(0, "w_out"): [0], (0, "ff_up"): [1, 2], (1, "w_in"): [3, 4], (1, "ff_up"): [5, 6]}
    scw_f = jnp.transpose(g_scw, (1, 2, 0, 3)).reshape(DEPTH, SSD_CONV, CONV_DIM)
    fcw_f = jnp.transpose(g_fcw, (1, 2, 0, 3)).reshape(DEPTH, FF_CONV, D_FF)
    c_all = c_all.reshape(N_DEV * nseq, d)

    n_ada = ada_w.shape[2]
    ada_b_shard = lax.dynamic_slice_in_dim(ada_b, me * n_ada, n_ada, axis=1).reshape(DEPTH, 1, n_ada)
    mod_part, c_act = _ada_fwd(c_all + first_zero, ada_w, ada_b_shard)
    (mod_g,) = _all_gather([mod_part], "gather_mod")
    mod_all = jnp.transpose(mod_g, (1, 2, 0, 3)).reshape(DEPTH, N_DEV * nseq, N_MOD * d)
    mod_mine = lax.dynamic_slice_in_dim(mod_all, me * nseq, nseq, axis=1)
    mods = [[mod_mine[l, :, k * d:(k + 1) * d].reshape(nseq, 1, d) for k in range(N_MOD)] for l in range(DEPTH)]

    big_cache = {}

    def big_w(l, name, after):
        if (l, name) == (0, "w_in") and (l, name) not in big_cache:
            srcs, lands = _xc_wait(False, first_ssem, first_rsem, first_src, first_land, after, "ag_first_wait",
                                   peers=OTHER_CHIPS)
            zone = lax.dynamic_update_index_in_dim(lands[0], srcs[0], me, 0)
            (zone,) = _sib_wait(*_sib_start([zone], "ag_first_sib_start"), "ag_first_sib_wait")
            big_cache[(l, name)] = _full_weight(name, zone)
        if (l, name) not in big_cache:
            idx = ag_groups[(l, name)]
            pick = lambda seq_: [seq_[i] for i in idx]
            srcs, lands = _xc_wait(False, pick(ag_ssem), pick(ag_rsem), pick(ag_src), pick(ag_land), after,
                                   f"ag_wait_{l}_{name}")
            for i, src, land in zip(idx, srcs, lands):
                big_cache[later[i]] = _full_weight(later[i][1], lax.dynamic_update_index_in_dim(land, src, me, 0))
        return big_cache[(l, name)]

    lw = []
    for l in range(DEPTH):
        lw.append(dict(
            norm1_g=norm1_g[l:l + 1] + (ag_zero if l == 0 else 0.0), norm2_g=norm2_g[l:l + 1], ssd_conv_w=scw_f[l],
            ssd_conv_b=ssd_conv_b[l:l + 1], ssd_dt_bias=ssd_dt_bias[l:l + 1], ssd_a_log=ssd_a_log[l:l + 1],
            ssd_d=ssd_d[l:l + 1], ssd_norm_g=ssd_norm_g[l:l + 1], gm_vnorm_g=gm_vnorm_g[l:l + 1], gm_ws=gm_ws[l],
            gm_bst=gm_bs[l].T, gm_out_g=gm_out_g[l:l + 1], ff_conv_w=fcw_f[l], ff_conv_b=ff_conv_b[l:l + 1]))

    outs = {}
    pending = {}

    def rs_finish(l, group, after):
        names, ssem, rsem, srcs, lands = pending.pop((l, group))
        srcs, lands = _xc_wait(True, ssem, rsem, srcs, lands, after, f"rs_wait_{l}_{group}")
        for nme, own, land in zip(names, srcs, lands):
            parts = [(own, lambda p: p[0])] + [(land, lambda p, k=k: k) for k in range(N_DEV - 1)]
            outs[nme] = _adamw_layer(parts, wts[nme], mom[nme], var[nme], me_arr, l, outs.get(nme), f"adamw_{nme}_{l}")
        return outs[names[-1]][0]

    def grad_sink(l, group, grads, after):
        names = list(grads)
        ssem, rsem, srcs, lands, zero = _xc_start(True, [_by_owner(n, grads[n]) for n in names], after, f"rs_start_{l}_{group}")
        pending[(l, group)] = (names, ssem, rsem, srcs, lands)
        return zero.reshape(1, 1)

    early_gather = {}

    def small_sink(l, early, small, dmods, dfg, loss_p):
        if l > 0:
            return None
        layers = [dict(early, norm1_g=jnp.zeros((1, d), F32))] + small[1:]
        rows = lambda name: [layers[k][name] for k in range(DEPTH)]
        packed = [
            jnp.concatenate(sum([rows(n) for n in _P1024], []) + [dfg], axis=0),
            jnp.concatenate(rows("ssd_conv_w") + rows("ssd_conv_b"), axis=0),
            jnp.concatenate(rows("ff_conv_w") + rows("ff_conv_b"), axis=0),
            jnp.concatenate(sum([rows(n) for n in _P16], []) + [loss_p[:, :SSD_HEADS]], axis=0),
            jnp.concatenate([layers[k]["gm_ws"].reshape(GM_HEADS * CHUNK, CHUNK) for k in range(DEPTH)] + rows("gm_bs"), axis=0),
            jnp.concatenate([jnp.zeros((nseq, N_MOD * d), F32)] + dmods[1:], axis=0)]
        ssem, rsem, srcs, lands, zero = _xc_start(False, packed, packed[0], "small_start")
        early_gather.update(ssem=ssem, rsem=rsem, srcs=srcs, lands=lands)
        return zero.reshape(1, 1)

    grad_x, small, dmods = _local_step(
        x.reshape(nseq * seq, d), loss_target.reshape(nseq * seq, d), mods, lw, final_g.reshape(1, d), nseq=nseq,
        big_w=big_w, grad_sink=grad_sink, small_sink=small_sink)

    done = grad_x
    for l, grp in ((1, "ffn"), (1, "w_out"), (1, "w_in"), (0, "ffn"), (0, "w_out")):
        done = rs_finish(l, grp, done)
    srcs, lands = _xc_wait(False, early_gather["ssem"], early_gather["rsem"], early_gather["srcs"],
                           early_gather["lands"], done, "small_wait")
    gathered = [lax.dynamic_update_index_in_dim(land, src, me, 0) for src, land in zip(srcs, lands)]
    gathered += _all_gather([small[0]["norm1_g"], dmods[0]], "gather_late", dep=gathered[0])
    gath = dict(zip(["p1024", "p1536", "p2816", "p16", "p128", "p6144", "late1024", "late6144"], gathered))

    dmod_all = jnp.concatenate([gath["late6144"].reshape(1, N_DEV * nseq, N_MOD * d),
                                jnp.transpose(gath["p6144"].reshape(N_DEV, DEPTH, nseq, N_MOD * d)[:, 1:], (1, 0, 2, 3)).reshape(
                                    DEPTH - 1, N_DEV * nseq, N_MOD * d)], axis=0)
    small_names = _P1024 + ["final_g", "ssd_conv_b", "ff_conv_b"] + _P16 + ["gm_ws", "gm_bs", "ada_b"]
    wmv = {}
    for nme in small_names:
        if nme == "final_g":
            wmv[nme] = tuple(a.reshape(1, d) for a in (wts[nme], mom[nme], var[nme]))
        else:
            wmv[nme] = (wts[nme], mom[nme], var[nme])
    small_out, scw_full, fcw_full, loss_sum = _adamw_small(gath, wmv)
    loss = loss_sum[0, 0]
    rs_finish(0, "w_in", scw_full)
    for nme in small_names:
        outs[nme] = small_out[nme]
    outs["final_g"] = tuple(a.reshape(d) for a in outs["final_g"])

    n_scw, n_fcw = ssd_conv_w.shape[2], ff_conv_w.shape[2]
    g_scw_mine = lax.dynamic_slice_in_dim(scw_full, me * n_scw, n_scw, axis=2)
    g_fcw_mine = lax.dynamic_slice_in_dim(fcw_full, me * n_fcw, n_fcw, axis=2)
    outs["ssd_conv_w"] = _adamw_sharded([(g_scw_mine, lambda p: 0)], ssd_conv_w, m_ssd_conv_w, v_ssd_conv_w, me_arr, "adamw_ssd_conv_w")
    outs["ff_conv_w"] = _adamw_sharded([(g_fcw_mine, lambda p: 0)], ff_conv_w, m_ff_conv_w, v_ff_conv_w, me_arr, "adamw_ff_conv_w")

    dmod_cols = _b(lax.dynamic_slice_in_dim(dmod_all, me * n_ada, n_ada, axis=2))
    g_ada = jnp.stack([_matmul(c_act, dmod_cols[l], ta=True, name=f"mm_ada_dw_{l}") for l in range(DEPTH)])
    outs["ada_w"] = _adamw_sharded([(g_ada, lambda p: 0)], ada_w, m_ada_w, v_ada_w, me_arr, "adamw_ada_w")

    for nme in _TRANSPOSED:
        outs[nme] = tuple(jnp.transpose(a, (0, 2, 1)) for a in outs[nme])
    result = [loss, grad_x.reshape(nseq, seq, d)]
    for k in range(4):
        result += [outs[n][k] for n in _WEIGHTS]
    return tuple(result)
```

```python
import functools
import math

import jax
import jax.numpy as jnp
from jax import lax
from jax.experimental import pallas as pl
from jax.experimental.pallas import tpu as pltpu

F32 = jnp.float32
BF16 = jnp.bfloat16

N_DEV = 8
D_MODEL = 1024
DEPTH = 2
CHUNK = 128
SSD_HEADS = 16
SSD_HEAD_DIM = 64
SSD_GROUPS = 2
HEADS_PER_GROUP = SSD_HEADS // SSD_GROUPS
GROUP_WIDTH = HEADS_PER_GROUP * SSD_HEAD_DIM
D_STATE = 128
D_SSD = 1024
CONV_DIM = 1536
SSD_CONV = 4
GM_HEADS = 8
GM_HEAD_DIM = 128
D_GM = 1024
D_FF = 2816
FF_CONV = 3
N_IN = 4624
N_MOD = 6
EPS = 1e-6

N_INP = 5120
COL_U, COL_V, COL_Z, COL_XBC, COL_DT = 0, 1024, 2048, 3072, 4608
DT_BLOCK = 512

ADAM_LR = 0.001
ADAM_B1 = 0.9
ADAM_B2 = 0.999
ADAM_EPS = 1e-08
ADAM_WD = 0.01
ADAM_STEP = 10

VMEM_LIMIT = 56 * 1024 * 1024
MESH = pl.DeviceIdType.MESH
ANY = pl.BlockSpec(memory_space=pl.ANY)


def _cp(*sem):
    return pltpu.CompilerParams(dimension_semantics=sem, vmem_limit_bytes=VMEM_LIMIT)


def _tile(n, pref):
    if n <= pref or n % 128:
        return n
    best = 128
    for t in range(128, pref + 1, 128):
        if n % t == 0:
            best = t
    return best


def _silu(x):
    return x * jax.nn.sigmoid(x)


def _gelu(x):
    return 0.5 * x * (1.0 + lax.erf(x * (1.0 / math.sqrt(2.0))))


def _softplus(x):
    return jnp.maximum(x, 0.0) + jnp.log1p(jnp.exp(-jnp.abs(x)))


def _rms(x, g, width):
    return x * lax.rsqrt(jnp.sum(x * x, axis=-1, keepdims=True) / width + EPS) * g


def _b(x):
    return x.astype(BF16)


_NN = (((1,), (0,)), ((), ()))
_NT = (((1,), (1,)), ((), ()))
_TN = (((0,), (0,)), ((), ()))


def _dg(a, b, dn):
    return lax.dot_general(_b(a), _b(b), dn, preferred_element_type=F32)


@jax.custom_vjp
def _bdot(a, b):
    return _dg(a, b, _NN)


def _bdot_fwd(a, b):
    return _dg(a, b, _NN), (a, b)


def _bdot_bwd(res, ct):
    a, b = res
    return _dg(ct, b, _NT), _dg(a, ct, _TN)


_bdot.defvjp(_bdot_fwd, _bdot_bwd)


@jax.custom_vjp
def _bdot_nt(a, b):
    return _dg(a, b, _NT)


def _bdot_nt_fwd(a, b):
    return _dg(a, b, _NT), (a, b)


def _bdot_nt_bwd(res, ct):
    a, b = res
    return _dg(ct, b, _NN), _dg(ct, a, _TN)


_bdot_nt.defvjp(_bdot_nt_fwd, _bdot_nt_bwd)


@jax.custom_vjp
def _bdot_tn(a, b):
    return _dg(a, b, _TN)


def _bdot_tn_fwd(a, b):
    return _dg(a, b, _TN), (a, b)


def _bdot_tn_bwd(res, ct):
    a, b = res
    return _dg(b, ct, _NT), _dg(a, ct, _NN)


_bdot_tn.defvjp(_bdot_tn_fwd, _bdot_tn_bwd)


def _tri(n, lower):
    r = lax.broadcasted_iota(jnp.int32, (n, n), 0)
    c = lax.broadcasted_iota(jnp.int32, (n, n), 1)
    return ((r >= c) if lower else (r <= c)).astype(F32)


def _eye(n):
    r = lax.broadcasted_iota(jnp.int32, (n, n), 0)
    c = lax.broadcasted_iota(jnp.int32, (n, n), 1)
    return (r == c).astype(F32)


def _hdot(a, b, dn):
    return lax.dot_general(a, b, dn, precision=lax.Precision.HIGHEST, preferred_element_type=F32)


@jax.custom_vjp
def _cumsum_rows(x):
    return _hdot(_tri(x.shape[0], True), x, _NN)


def _cumsum_rows_fwd(x):
    return _cumsum_rows(x), None


def _cumsum_rows_bwd(_, ct):
    return (_hdot(_tri(ct.shape[0], False), ct, _NN),)


_cumsum_rows.defvjp(_cumsum_rows_fwd, _cumsum_rows_bwd)


@jax.custom_vjp
def _transpose(x):
    return _hdot(_eye(x.shape[1]), x, _NT)


def _transpose_fwd(x):
    return _transpose(x), None


def _transpose_bwd(_, ct):
    return (_hdot(_eye(ct.shape[1]), ct, _NT),)


_transpose.defvjp(_transpose_fwd, _transpose_bwd)


MXU_WIDTH = 256
MATMUL_TILE_CAP = 2816
MATMUL_VMEM = 44 * 1024 * 1024


def _mxu_tiles(n):
    if n <= MATMUL_TILE_CAP or n % 128:
        return [n]
    for unit in (MXU_WIDTH, 128):
        opts = [t for t in range(unit, MATMUL_TILE_CAP + 1, unit) if n % t == 0]
        if opts:
            return opts
    return [n]


def _matmul(a, b, *, ta=False, tb=False, name, dep=None, out_dtype=F32):
    pieces = list(a) if isinstance(a, (list, tuple)) else [a]
    npc = len(pieces)
    rows, width = pieces[0].shape
    assert all(p.shape == (rows, width) for p in pieces)
    if ta:
        k_dim, m_dim = rows, width * npc
    else:
        m_dim, k_dim = rows, width * npc
    if tb:
        n_dim, kb = b.shape
    else:
        kb, n_dim = b.shape
    assert kb == k_dim, (pieces[0].shape, npc, b.shape, ta, tb)
    m_unit = width if npc > 1 and ta else m_dim
    k_unit = width if npc > 1 and not ta else k_dim
    tm = _tile(m_unit, 1536)
    tn_opts, tk_opts = _mxu_tiles(n_dim), _mxu_tiles(k_unit)
    tn, tk = tn_opts.pop(), tk_opts.pop()
    while 4 * (tm * tk + tk * tn) + 8 * tm * tn > MATMUL_VMEM:
        if tn >= tk and tn_opts:
            tn = tn_opts.pop()
        else:
            tk = tk_opts.pop()
    ni, nj, nk = m_dim // tm, n_dim // tn, k_dim // tk
    per = width // (tm if ta else tk)
    dn = (((0 if ta else 1,), (1 if tb else 0,)), ((), ()))

    a_bytes, b_bytes = m_dim * k_dim, k_dim * n_dim
    m_outer = nk > 1 or a_bytes + b_bytes * ni <= b_bytes + a_bytes * nj
    if m_outer:
        ij = lambda o, n, k: (o, n)
        grid = (ni, nj, nk)
    else:
        ij = lambda o, n, k: (n, o)
        grid = (nj, ni, nk)

    use_acc = nk > 1 and out_dtype != F32

    def body(*refs):
        a_refs, b_ref = refs[:npc], refs[npc]
        o_ref = refs[-2] if use_acc else refs[-1]
        acc_ref = refs[-1]
        k = pl.program_id(2)
        i = pl.program_id(0 if m_outer else 1)
        along = i if ta else k

        def step(a_ref):
            p = lax.dot_general(a_ref[...], b_ref[...], dn, preferred_element_type=F32)
            if nk == 1:
                o_ref[...] = p.astype(out_dtype)
            else:
                @pl.when(k == 0)
                def _():
                    acc_ref[...] = p

                @pl.when((k > 0) & (k < nk - 1 if use_acc else True))
                def _():
                    acc_ref[...] += p

                if use_acc:
                    @pl.when(k == nk - 1)
                    def _():
                        o_ref[...] = (acc_ref[...] + p).astype(out_dtype)

        if npc == 1:
            step(a_refs[0])
        else:
            for pc in range(npc):
                pl.when((along >= pc * per) & (along < (pc + 1) * per))(functools.partial(step, a_refs[pc]))

    def a_map(pc, o, n, k):
        i, _ = ij(o, n, k)
        along = i if ta else k
        if npc > 1:
            along = jnp.clip(along - pc * per, 0, per - 1)
        return (k, along) if ta else (i, along)

    def b_map(o, n, k):
        _, j = ij(o, n, k)
        return (j, k) if tb else (k, j)

    extra = [] if dep is None else [dep]
    return pl.pallas_call(
        body, name=name,
        grid=grid,
        in_specs=[pl.BlockSpec((tk, tm) if ta else (tm, tk), functools.partial(a_map, pc)) for pc in range(npc)]
        + [pl.BlockSpec((tn, tk) if tb else (tk, tn), b_map)] + [ANY] * len(extra),
        out_specs=pl.BlockSpec((tm, tn), lambda o, n, k: ij(o, n, k)),
        out_shape=jax.ShapeDtypeStruct((m_dim, n_dim), out_dtype),
        scratch_shapes=[pltpu.VMEM((tm, tn), F32)] if use_acc else [],
        compiler_params=_cp("parallel", "parallel", "arbitrary"),
    )(*pieces, b, *extra)


def _ada_fwd(c_all, ada_w, ada_b_shard):
    depth, d, n = ada_w.shape
    nb = c_all.shape[0]

    def body(c_ref, w_ref, b_ref, o_ref, ca_ref):
        ca = _silu(c_ref[...])
        ca_ref[...] = _b(ca)
        o_ref[0] = _dg(ca, w_ref[0], _NN) + b_ref[0]

    return pl.pallas_call(
        body, name="ada_fwd",
        grid=(depth,),
        in_specs=[pl.BlockSpec((nb, d), lambda l: (0, 0)),
                  pl.BlockSpec((1, d, n), lambda l: (l, 0, 0)),
                  pl.BlockSpec((1, 1, n), lambda l: (l, 0, 0))],
        out_specs=[pl.BlockSpec((1, nb, n), lambda l: (l, 0, 0)),
                   pl.BlockSpec((nb, d), lambda l: (0, 0))],
        out_shape=[jax.ShapeDtypeStruct((depth, nb, n), F32), jax.ShapeDtypeStruct((nb, d), BF16)],
        compiler_params=_cp("arbitrary"),
    )(c_all, ada_w, ada_b_shard)


def _normmod_f(x, g, sc, sh):
    return _rms(x, g, D_MODEL) * (1.0 + sc) + sh


def _row_tile(seq):
    return min(seq, 256)


def _normmod_fwd(xin, delta, gate, g, sc, sh, *, nseq, name):
    t, d = xin.shape
    seq = t // nseq
    tr = _row_tile(seq)
    nt = seq // tr
    has_delta = delta is not None
    row = pl.BlockSpec((tr, d), lambda s, i: (s * nt + i, 0))
    per_seq = pl.BlockSpec((1, 1, d), lambda s, i: (s, 0, 0))
    vec = pl.BlockSpec((1, d), lambda s, i: (0, 0))

    if has_delta:
        def body(xin_ref, delta_ref, gate_ref, g_ref, sc_ref, sh_ref, x_ref, h_ref):
            x = xin_ref[...] + gate_ref[0] * delta_ref[...]
            x_ref[...] = x
            h_ref[...] = _b(_normmod_f(x, g_ref[...], sc_ref[0], sh_ref[0]))

        return pl.pallas_call(
            body, name=name, grid=(nseq, nt),
            in_specs=[row, row, per_seq, vec, per_seq, per_seq],
            out_specs=[row, row],
            out_shape=[jax.ShapeDtypeStruct((t, d), F32), jax.ShapeDtypeStruct((t, d), BF16)],
            compiler_params=_cp("parallel", "parallel"),
        )(xin, delta, gate, g, sc, sh)

    def body0(xin_ref, g_ref, sc_ref, sh_ref, h_ref):
        h_ref[...] = _b(_normmod_f(xin_ref[...], g_ref[...], sc_ref[0], sh_ref[0]))

    h = pl.pallas_call(
        body0, name=name, grid=(nseq, nt),
        in_specs=[row, vec, per_seq, per_seq],
        out_specs=row,
        out_shape=jax.ShapeDtypeStruct((t, d), BF16),
        compiler_params=_cp("parallel", "parallel"),
    )(xin, g, sc, sh)
    return xin, h


def _normmod_bwd(dh, dxo, x, delta, gate, g, sc, *, nseq, name):
    t, d = x.shape
    seq = t // nseq
    tr = _row_tile(seq)
    nt = seq // tr
    has_delta = delta is not None
    row = pl.BlockSpec((tr, d), lambda s, i: (s * nt + i, 0))
    per_seq = pl.BlockSpec((1, 1, d), lambda s, i: (s, 0, 0))
    vec = pl.BlockSpec((1, d), lambda s, i: (0, 0))

    def core(dh_ref, dxo_ref, x_ref, g_ref, sc_ref, dx_ref, dg_ref, dsc_ref, dsh_ref):
        s, i = pl.program_id(0), pl.program_id(1)
        dh_v = dh_ref[...]
        _, vjp = jax.vjp(lambda xx, gg, ss: _normmod_f(xx, gg, ss, 0.0), x_ref[...], g_ref[...], sc_ref[0])
        dxn, dg_t, dsc_t = vjp(dh_v)
        dx = dxo_ref[...] + dxn
        dx_ref[...] = dx
        dsh_t = jnp.sum(dh_v, axis=0, keepdims=True)

        @pl.when((s == 0) & (i == 0))
        def _():
            dg_ref[...] = jnp.zeros_like(dg_ref)

        @pl.when(i == 0)
        def _():
            dsc_ref[...] = jnp.zeros_like(dsc_ref)
            dsh_ref[...] = jnp.zeros_like(dsh_ref)

        dg_ref[...] += dg_t
        dsc_ref[0] += dsc_t
        dsh_ref[0] += dsh_t
        return dx

    if has_delta:
        def body(dh_ref, dxo_ref, x_ref, delta_ref, gate_ref, g_ref, sc_ref,
                 dx_ref, dd_ref, dgate_ref, dg_ref, dsc_ref, dsh_ref):
            dx = core(dh_ref, dxo_ref, x_ref, g_ref, sc_ref, dx_ref, dg_ref, dsc_ref, dsh_ref)
            dd_ref[...] = _b(dx * gate_ref[0])

            @pl.when(pl.program_id(1) == 0)
            def _():
                dgate_ref[...] = jnp.zeros_like(dgate_ref)

            dgate_ref[0] += jnp.sum(dx * delta_ref[...], axis=0, keepdims=True)

        return pl.pallas_call(
            body, name=name, grid=(nseq, nt),
            in_specs=[row, row, row, row, per_seq, vec, per_seq],
            out_specs=[row, row, per_seq, vec, per_seq, per_seq],
            out_shape=[jax.ShapeDtypeStruct((t, d), F32), jax.ShapeDtypeStruct((t, d), BF16),
                       jax.ShapeDtypeStruct((nseq, 1, d), F32), jax.ShapeDtypeStruct((1, d), F32),
                       jax.ShapeDtypeStruct((nseq, 1, d), F32), jax.ShapeDtypeStruct((nseq, 1, d), F32)],
            compiler_params=_cp("arbitrary", "arbitrary"),
        )(dh, dxo, x, delta, gate, g, sc)

    def body0(dh_ref, dxo_ref, x_ref, g_ref, sc_ref, dx_ref, dg_ref, dsc_ref, dsh_ref):
        core(dh_ref, dxo_ref, x_ref, g_ref, sc_ref, dx_ref, dg_ref, dsc_ref, dsh_ref)

    dx, dg, dsc, dsh = pl.pallas_call(
        body0, name=name, grid=(nseq, nt),
        in_specs=[row, row, row, vec, per_seq],
        out_specs=[row, vec, per_seq, per_seq],
        out_shape=[jax.ShapeDtypeStruct((t, d), F32), jax.ShapeDtypeStruct((1, d), F32),
                   jax.ShapeDtypeStruct((nseq, 1, d), F32), jax.ShapeDtypeStruct((nseq, 1, d), F32)],
        compiler_params=_cp("arbitrary", "arbitrary"),
    )(dh, dxo, x, g, sc)
    return dx, None, None, dg, dsc, dsh


def _final_loss(xin, delta, gate, fg, target, *, nseq):
    t, d = xin.shape
    seq = t // nseq
    tr = _row_tile(seq)
    nt = seq // tr
    row = pl.BlockSpec((tr, d), lambda s, i: (s * nt + i, 0))
    per_seq = pl.BlockSpec((1, 1, d), lambda s, i: (s, 0, 0))
    vec = pl.BlockSpec((1, d), lambda s, i: (0, 0))

    def body(xin_ref, delta_ref, gate_ref, fg_ref, tgt_ref, loss_ref, dx_ref, dd_ref, dgate_ref, dfg_ref):
        s, i = pl.program_id(0), pl.program_id(1)
        dl = delta_ref[...]
        x = xin_ref[...] + gate_ref[0] * dl
        y, vjp = jax.vjp(lambda xx, gg: _rms(xx, gg, D_MODEL), x, fg_ref[...])
        err = y - tgt_ref[...]
        dx, dfg_t = vjp(err * (1.0 / d))
        dx_ref[...] = dx
        dd_ref[...] = _b(dx * gate_ref[0])

        @pl.when((s == 0) & (i == 0))
        def _():
            loss_ref[...] = jnp.zeros_like(loss_ref)
            dfg_ref[...] = jnp.zeros_like(dfg_ref)

        @pl.when(i == 0)
        def _():
            dgate_ref[...] = jnp.zeros_like(dgate_ref)

        loss_ref[...] += jnp.sum(err * err) * (0.5 / d)
        dfg_ref[...] += dfg_t
        dgate_ref[0] += jnp.sum(dx * dl, axis=0, keepdims=True)

    return pl.pallas_call(
        body, name="final_loss", grid=(nseq, nt),
        in_specs=[row, row, per_seq, vec, row],
        out_specs=[pl.BlockSpec((1, 128), lambda s, i: (0, 0)), row, row, per_seq, vec],
        out_shape=[jax.ShapeDtypeStruct((1, 128), F32), jax.ShapeDtypeStruct((t, d), F32),
                   jax.ShapeDtypeStruct((t, d), BF16), jax.ShapeDtypeStruct((nseq, 1, d), F32),
                   jax.ShapeDtypeStruct((1, d), F32)],
        compiler_params=_cp("arbitrary", "arbitrary"),
    )(xin, delta, gate, fg, target)


CONV_TC = 256
CONV_LANES = 128
CONV_ROWS = 64
CONV_HALO = 8


def _conv_slabs(seq, fn):
    def step(i, carry):
        r0 = pl.multiple_of(i * CONV_ROWS, CONV_ROWS)
        for h in range(CONV_TC // CONV_LANES):
            fn(r0, slice(h * CONV_LANES, (h + 1) * CONV_LANES))
        return carry

    lax.fori_loop(0, seq // CONV_ROWS, step, 0)


def _slab(ref, r0, cols, seq, before=0, after=0):
    parts = []
    if before:
        top = ref[pl.ds(pl.multiple_of(jnp.maximum(r0 - before, 0), CONV_HALO), before), cols]
        parts.append(jnp.where(r0 > 0, top, 0.0))
    parts.append(ref[pl.ds(r0, CONV_ROWS), cols])
    if after:
        bot = ref[pl.ds(pl.multiple_of(jnp.minimum(r0 + CONV_ROWS, seq - after), CONV_HALO), after), cols]
        parts.append(jnp.where(r0 + CONV_ROWS < seq, bot, 0.0))
    return jnp.concatenate(parts, axis=0) if len(parts) > 1 else parts[0]


def _conv_block(x, w_ref, b_ref):
    kw = w_ref.shape[0]
    rows = lax.broadcasted_iota(jnp.int32, x.shape, 0)
    y = b_ref[...] + w_ref[kw - 1:kw, :] * x
    for j in range(1, kw):
        y = y + w_ref[kw - 1 - j:kw - j, :] * jnp.where(rows >= j, pltpu.roll(x, j, 0), 0.0)
    return y


def _conv_block_bwd(dy, x, w_ref, dw_ref, db_ref):
    kw = w_ref.shape[0]
    n = x.shape[0]
    rows = lax.broadcasted_iota(jnp.int32, x.shape, 0)
    dx = w_ref[kw - 1:kw, :] * dy
    dw_ref[kw - 1:kw, :] += jnp.sum(dy * x, axis=0, keepdims=True)
    for j in range(1, kw):
        dy_j = jnp.where(rows < n - j, pltpu.roll(dy, n - j, 0), 0.0)
        dx = dx + w_ref[kw - 1 - j:kw - j, :] * dy_j
        dw_ref[kw - 1 - j:kw - j, :] += jnp.sum(dy_j * x, axis=0, keepdims=True)
    db_ref[...] += jnp.sum(dy, axis=0, keepdims=True)
    return dx


def _conv_bwd(dy_ext, x, w_ref, dw_ref, db_ref, cols):
    kw = w_ref.shape[0]
    n = dy_ext.shape[0]
    dy = dy_ext[:CONV_ROWS]
    dx = w_ref[kw - 1:kw, cols] * dy
    dw_ref[kw - 1:kw, cols] += jnp.sum(dy * x, axis=0, keepdims=True)
    for j in range(1, kw):
        dy_j = pltpu.roll(dy_ext, n - j, 0)[:CONV_ROWS]
        dx = dx + w_ref[kw - 1 - j:kw - j, cols] * dy_j
        dw_ref[kw - 1 - j:kw - j, cols] += jnp.sum(dy_j * x, axis=0, keepdims=True)
    db_ref[:, cols] += jnp.sum(dy, axis=0, keepdims=True)
    return dx


def _dsilu(pre):
    sg = jax.nn.sigmoid(pre)
    return pre * sg, sg * (1.0 + pre * (1.0 - sg))


def _ssd_conv_fwd(proj, w, b, *, nseq):
    t = proj.shape[0]
    seq = t // nseq
    nb = CONV_DIM // CONV_TC
    off = COL_XBC // CONV_TC

    def body(x_ref, w_ref, b_ref, o_ref, pre_ref):
        pre = _conv_block(x_ref[...], w_ref, b_ref)
        pre_ref[...] = pre
        o_ref[...] = _silu(pre)

    col = pl.BlockSpec((seq, CONV_TC), lambda j, s: (s, j))
    return pl.pallas_call(
        body, name="ssd_conv_fwd", grid=(nb, nseq),
        in_specs=[pl.BlockSpec((seq, CONV_TC), lambda j, s: (s, off + j)),
                  pl.BlockSpec((SSD_CONV, CONV_TC), lambda j, s: (0, j)),
                  pl.BlockSpec((1, CONV_TC), lambda j, s: (0, j))],
        out_specs=[col, col],
        out_shape=[jax.ShapeDtypeStruct((t, CONV_DIM), F32)] * 2,
        compiler_params=_cp("parallel", "parallel"),
    )(proj, w, b)


def _ssd_conv_bwd(dact, pre, proj, w, dproj, *, nseq):
    t = proj.shape[0]
    seq = t // nseq
    nb = CONV_DIM // CONV_TC
    off = COL_XBC // CONV_TC

    def body(da_ref, pre_ref, x_ref, w_ref, dproj_ref, dx_ref, dw_ref, db_ref):
        del dproj_ref

        @pl.when(pl.program_id(1) == 0)
        def _():
            dw_ref[...] = jnp.zeros_like(dw_ref)
            db_ref[...] = jnp.zeros_like(db_ref)

        def slab(r0, cols):
            _, dsilu = _dsilu(_slab(pre_ref, r0, cols, seq, after=CONV_HALO))
            dpre_ext = _slab(da_ref, r0, cols, seq, after=CONV_HALO) * dsilu
            x = x_ref[pl.ds(r0, CONV_ROWS), cols]
            dx_ref[pl.ds(r0, CONV_ROWS), cols] = _b(_conv_bwd(dpre_ext, x, w_ref, dw_ref, db_ref, cols))

        _conv_slabs(seq, slab)

    return pl.pallas_call(
        body, name="ssd_conv_bwd", grid=(nb, nseq),
        in_specs=[pl.BlockSpec((seq, CONV_TC), lambda j, s: (s, j)),
                  pl.BlockSpec((seq, CONV_TC), lambda j, s: (s, j)),
                  pl.BlockSpec((seq, CONV_TC), lambda j, s: (s, off + j)),
                  pl.BlockSpec((SSD_CONV, CONV_TC), lambda j, s: (0, j)),
                  ANY],
        out_specs=[pl.BlockSpec((seq, CONV_TC), lambda j, s: (s, off + j)),
                   pl.BlockSpec((SSD_CONV, CONV_TC), lambda j, s: (0, j)),
                   pl.BlockSpec((1, CONV_TC), lambda j, s: (0, j))],
        out_shape=[jax.ShapeDtypeStruct(dproj.shape, dproj.dtype), jax.ShapeDtypeStruct((SSD_CONV, CONV_DIM), F32),
                   jax.ShapeDtypeStruct((1, CONV_DIM), F32)],
        input_output_aliases={4: 0},
        compiler_params=_cp("parallel", "arbitrary"),
    )(dact, pre, proj, w, dproj)


def _ffn_act_fwd(up, w, b, *, nseq):
    t = up.shape[0]
    seq = t // nseq
    nb = D_FF // CONV_TC

    def body(g_ref, v_ref, w_ref, b_ref, o_ref):
        o_ref[...] = _b(_silu(_conv_block(g_ref[...], w_ref, b_ref)) * v_ref[...])

    col = pl.BlockSpec((seq, CONV_TC), lambda j, s: (s, j))
    return pl.pallas_call(
        body, name="ffn_act_fwd", grid=(nb, nseq),
        in_specs=[col,
                  pl.BlockSpec((seq, CONV_TC), lambda j, s: (s, nb + j)),
                  pl.BlockSpec((FF_CONV, CONV_TC), lambda j, s: (0, j)),
                  pl.BlockSpec((1, CONV_TC), lambda j, s: (0, j))],
        out_specs=col,
        out_shape=jax.ShapeDtypeStruct((t, D_FF), BF16),
        compiler_params=_cp("parallel", "parallel"),
    )(up, up, w, b)


def _ffn_act_bwd(dact, up, w, b, *, nseq):
    t = up.shape[0]
    seq = t // nseq
    nb = D_FF // CONV_TC

    def body(da_ref, g_ref, v_ref, w_ref, b_ref, dg_ref, dv_ref, dw_ref, db_ref):
        @pl.when(pl.program_id(1) == 0)
        def _():
            dw_ref[...] = jnp.zeros_like(dw_ref)
            db_ref[...] = jnp.zeros_like(db_ref)

        gate = g_ref[...]
        silu, dsilu = _dsilu(_conv_block(gate, w_ref, b_ref))
        da = da_ref[...]
        dv_ref[...] = _b(da * silu)
        dg_ref[...] = _b(_conv_block_bwd(da * v_ref[...] * dsilu, gate, w_ref, dw_ref, db_ref))

    col = pl.BlockSpec((seq, CONV_TC), lambda j, s: (s, j))
    return pl.pallas_call(
        body, name="ffn_act_bwd", grid=(nb, nseq),
        in_specs=[col, col,
                  pl.BlockSpec((seq, CONV_TC), lambda j, s: (s, nb + j)),
                  pl.BlockSpec((FF_CONV, CONV_TC), lambda j, s: (0, j)),
                  pl.BlockSpec((1, CONV_TC), lambda j, s: (0, j))],
        out_specs=[col, col,
                   pl.BlockSpec((FF_CONV, CONV_TC), lambda j, s: (0, j)),
                   pl.BlockSpec((1, CONV_TC), lambda j, s: (0, j))],
        out_shape=[jax.ShapeDtypeStruct((t, D_FF), BF16), jax.ShapeDtypeStruct((t, D_FF), BF16),
                   jax.ShapeDtypeStruct((FF_CONV, D_FF), F32), jax.ShapeDtypeStruct((1, D_FF), F32)],
        compiler_params=_cp("parallel", "arbitrary"),
    )(dact, up, up, w, b)


SSD_PAIRS = SSD_HEADS // 2
PAIR_W = 2 * SSD_HEAD_DIM
PAIRS_PER_GROUP = SSD_PAIRS // SSD_GROUPS


def _ssd_chunk(xs, bg, cg, dtr, z, hp, dtb, alog, dskip, ng):
    n = dtr.shape[0]
    dt = _softplus(dtr + dtb)
    cs = _cumsum_rows(dt * (-jnp.exp(alog)))
    cs_t = _transpose(cs)
    lane = lax.broadcasted_iota(jnp.int32, (1, SSD_HEADS), 1)
    sub = lax.broadcasted_iota(jnp.int32, (SSD_HEADS, 1), 0)
    row = lax.broadcasted_iota(jnp.int32, (n, 1), 0)
    causal = lax.broadcasted_iota(jnp.int32, (n, n), 0) >= lax.broadcasted_iota(jnp.int32, (n, n), 1)
    first = lax.broadcasted_iota(jnp.int32, (1, PAIR_W), 1) < SSD_HEAD_DIM
    first_rows = lax.broadcasted_iota(jnp.int32, (PAIR_W, 1), 0) < SSD_HEAD_DIM
    first_f = first.astype(F32)
    cb = [_bdot_nt(cg[g], bg[g]) for g in range(SSD_GROUPS)]
    ys, hn = [], []
    for p in range(SSD_PAIRS):
        g = p // PAIRS_PER_GROUP
        col, decay, last = [], [], []
        for h in (2 * p, 2 * p + 1):
            oh = (lane == h).astype(F32)
            cs_h = jnp.sum(cs * oh, axis=1, keepdims=True)
            cs_row = jnp.sum(cs_t * (sub == h).astype(F32), axis=0, keepdims=True)
            col.append((jnp.sum(dt * oh, axis=1, keepdims=True), cs_h, jnp.sum(dskip * oh, axis=1, keepdims=True)))
            last.append(jnp.sum(jnp.where(row == n - 1, cs_h, 0.0), axis=0, keepdims=True))
            decay.append(jnp.where(causal, jnp.exp(jnp.where(causal, cs_h - cs_row, 0.0)), 0.0))
        pair = lambda a, b: jnp.where(first, a, b)
        dt_p = pair(col[0][0], col[1][0])
        cs_p = pair(col[0][1], col[1][1])
        last_p = pair(last[0], last[1])
        xc = xs[p] * dt_p
        y = _bdot(cb[g] * decay[0], xc * first_f) + _bdot(cb[g] * decay[1], xc * (1.0 - first_f))
        y = y + _bdot_nt(cg[g], hp[p]) * jnp.exp(cs_p)
        y = y + pair(col[0][2], col[1][2]) * xs[p]
        keep = jnp.where(first_rows, jnp.exp(last[0]), jnp.exp(last[1]))
        hn.append(keep * hp[p] + _bdot_tn(xc * jnp.exp(last_p - cs_p), bg[g]))
        ys.append(y * _silu(z[p]))
    outs = []
    for g in range(SSD_GROUPS):
        ps = range(g * PAIRS_PER_GROUP, (g + 1) * PAIRS_PER_GROUP)
        ms = sum(jnp.sum(ys[p] * ys[p], axis=1, keepdims=True) for p in ps) * (1.0 / GROUP_WIDTH)
        r = lax.rsqrt(ms + EPS)
        outs += [ys[p] * r * ng[p] for p in ps]
    return outs, hn


def _hslices(ref, width, count, base=0):
    return [ref[:, base + k * width: base + (k + 1) * width] for k in range(count)]


def _ssd_load(xbc_ref, z_ref, dt_ref, ng_ref):
    xs = _hslices(xbc_ref, PAIR_W, SSD_PAIRS)
    bg = _hslices(xbc_ref, D_STATE, SSD_GROUPS, D_SSD)
    cg = _hslices(xbc_ref, D_STATE, SSD_GROUPS, D_SSD + SSD_GROUPS * D_STATE)
    z = _hslices(z_ref, PAIR_W, SSD_PAIRS)
    ng = _hslices(ng_ref, PAIR_W, SSD_PAIRS)
    return xs, bg, cg, dt_ref[:, 0:SSD_HEADS], z, ng


def _ssd_specs(nch):
    rowi = lambda s, c: s * nch + c
    return [pl.BlockSpec((CHUNK, CONV_DIM), lambda s, c: (rowi(s, c), 0)),
            pl.BlockSpec((CHUNK, D_SSD), lambda s, c: (rowi(s, c), COL_Z // D_SSD)),
            pl.BlockSpec((CHUNK, 128), lambda s, c: (rowi(s, c), COL_DT // 128)),
            pl.BlockSpec((1, SSD_HEADS), lambda s, c: (0, 0)),
            pl.BlockSpec((1, SSD_HEADS), lambda s, c: (0, 0)),
            pl.BlockSpec((1, SSD_HEADS), lambda s, c: (0, 0)),
            pl.BlockSpec((1, D_SSD), lambda s, c: (0, 0))]


def _ssd_fwd(xbc, proj, dtb, alog, dskip, ng, *, nseq):
    t = proj.shape[0]
    nch = t // nseq // CHUNK
    hd = PAIR_W

    def body(xbc_ref, z_ref, dt_ref, dtb_ref, alog_ref, dsk_ref, ng_ref, y_ref, hp_ref, h_ref):
        @pl.when(pl.program_id(1) == 0)
        def _():
            h_ref[...] = jnp.zeros_like(h_ref)

        xs, bg, cg, dtr, z, ngs = _ssd_load(xbc_ref, z_ref, dt_ref, ng_ref)
        hp_ref[0] = h_ref[...]
        hp = [h_ref[h * hd:(h + 1) * hd, :] for h in range(SSD_PAIRS)]
        outs, hn = _ssd_chunk(xs, bg, cg, dtr, z, hp, dtb_ref[...], alog_ref[...], dsk_ref[...], ngs)
        for h in range(SSD_PAIRS):
            y_ref[:, h * hd:(h + 1) * hd] = _b(outs[h])
            h_ref[h * hd:(h + 1) * hd, :] = hn[h]

    return pl.pallas_call(
        body, name="ssd_fwd", grid=(nseq, nch),
        in_specs=_ssd_specs(nch),
        out_specs=[pl.BlockSpec((CHUNK, D_SSD), lambda s, c: (s * nch + c, 0)),
                   pl.BlockSpec((1, D_SSD, D_STATE), lambda s, c: (s * nch + c, 0, 0))],
        out_shape=[jax.ShapeDtypeStruct((t, D_SSD + D_GM), BF16),
                   jax.ShapeDtypeStruct((t // CHUNK, D_SSD, D_STATE), F32)],
        scratch_shapes=[pltpu.VMEM((D_SSD, D_STATE), F32)],
        compiler_params=_cp("arbitrary", "arbitrary"),
    )(xbc, proj, proj, dtb, alog, dskip, ng)


def _ssd_bwd(dy, xbc, proj, hprev, dtb, alog, dskip, ng, *, nseq):
    t = proj.shape[0]
    nch = t // nseq // CHUNK
    hd = PAIR_W
    rev = lambda s, c: s * nch + (nch - 1 - c)

    def body(dy_ref, xbc_ref, z_ref, dt_ref, hp_ref, dtb_ref, alog_ref, dsk_ref, ng_ref,
             dxbc_ref, dproj_ref, ddtb_ref, dalog_ref, ddsk_ref, dng_ref, dh_ref):
        first = (pl.program_id(0) == 0) & (pl.program_id(1) == 0)

        @pl.when(pl.program_id(1) == 0)
        def _():
            dh_ref[...] = jnp.zeros_like(dh_ref)

        @pl.when(first)
        def _():
            ddtb_ref[...] = jnp.zeros_like(ddtb_ref)
            dalog_ref[...] = jnp.zeros_like(dalog_ref)
            ddsk_ref[...] = jnp.zeros_like(ddsk_ref)
            dng_ref[...] = jnp.zeros_like(dng_ref)

        xs, bg, cg, dtr, z, ngs = _ssd_load(xbc_ref, z_ref, dt_ref, ng_ref)
        hp = [hp_ref[0, h * hd:(h + 1) * hd, :] for h in range(SSD_PAIRS)]
        _, vjp = jax.vjp(_ssd_chunk, xs, bg, cg, dtr, z, hp, dtb_ref[...], alog_ref[...], dsk_ref[...], ngs)
        douts = [dy_ref[:, h * hd:(h + 1) * hd] for h in range(SSD_PAIRS)]
        dhn = [dh_ref[h * hd:(h + 1) * hd, :] for h in range(SSD_PAIRS)]
        dxs, dbg, dcg, ddtr, dz, dhp, ddtb, dalog, ddsk, dngs = vjp((douts, dhn))
        dproj_ref[:, :COL_Z] = jnp.zeros((CHUNK, COL_Z), BF16)
        dproj_ref[:, COL_XBC:] = jnp.zeros((CHUNK, N_INP - COL_XBC), BF16)
        for h in range(SSD_PAIRS):
            dxbc_ref[:, h * hd:(h + 1) * hd] = dxs[h]
            dproj_ref[:, COL_Z + h * hd: COL_Z + (h + 1) * hd] = _b(dz[h])
            dh_ref[h * hd:(h + 1) * hd, :] = dhp[h]
            dng_ref[:, h * hd:(h + 1) * hd] += dngs[h]
        for g in range(SSD_GROUPS):
            dxbc_ref[:, D_SSD + g * D_STATE: D_SSD + (g + 1) * D_STATE] = dbg[g]
            dxbc_ref[:, D_SSD + (SSD_GROUPS + g) * D_STATE: D_SSD + (SSD_GROUPS + g + 1) * D_STATE] = dcg[g]
        dproj_ref[:, COL_DT:COL_DT + SSD_HEADS] = _b(ddtr)
        ddtb_ref[...] += ddtb
        dalog_ref[...] += dalog
        ddsk_ref[...] += ddsk

    small = pl.BlockSpec((1, SSD_HEADS), lambda s, c: (0, 0))
    return pl.pallas_call(
        body, name="ssd_bwd", grid=(nseq, nch),
        in_specs=[pl.BlockSpec((CHUNK, D_SSD), lambda s, c: (rev(s, c), 0)),
                  pl.BlockSpec((CHUNK, CONV_DIM), lambda s, c: (rev(s, c), 0)),
                  pl.BlockSpec((CHUNK, D_SSD), lambda s, c: (rev(s, c), COL_Z // D_SSD)),
                  pl.BlockSpec((CHUNK, 128), lambda s, c: (rev(s, c), COL_DT // 128)),
                  pl.BlockSpec((1, D_SSD, D_STATE), lambda s, c: (rev(s, c), 0, 0)),
                  small, small, small,
                  pl.BlockSpec((1, D_SSD), lambda s, c: (0, 0))],
        out_specs=[pl.BlockSpec((CHUNK, CONV_DIM), lambda s, c: (rev(s, c), 0)),
                   pl.BlockSpec((CHUNK, N_INP), lambda s, c: (rev(s, c), 0)),
                   small, small, small,
                   pl.BlockSpec((1, D_SSD), lambda s, c: (0, 0))],
        out_shape=[jax.ShapeDtypeStruct((t, CONV_DIM), F32), jax.ShapeDtypeStruct((t, N_INP), BF16),
                   jax.ShapeDtypeStruct((1, SSD_HEADS), F32), jax.ShapeDtypeStruct((1, SSD_HEADS), F32),
                   jax.ShapeDtypeStruct((1, SSD_HEADS), F32), jax.ShapeDtypeStruct((1, D_SSD), F32)],
        scratch_shapes=[pltpu.VMEM((D_SSD, D_STATE), F32)],
        compiler_params=_cp("arbitrary", "arbitrary"),
    )(dy, xbc, proj, proj, hprev, dtb, alog, dskip, ng)


def _gmlp_chunk(gu, gv, ws, bs_cols, vg, og):
    n = gu[0].shape[0]
    mask = _tri(n, True)
    au = [_gelu(t) for t in gu]
    av = [_gelu(t) for t in gv]
    r = lax.rsqrt(sum(jnp.sum(t * t, axis=1, keepdims=True) for t in av) * (1.0 / D_GM) + EPS)
    p = []
    for h in range(GM_HEADS):
        sv = _bdot(ws[h] * mask, av[h] * r * vg[h]) + bs_cols[h]
        p.append(au[h] * sv)
    r2 = lax.rsqrt(sum(jnp.sum(t * t, axis=1, keepdims=True) for t in p) * (1.0 / D_GM) + EPS)
    return [p[h] * r2 * og[h] for h in range(GM_HEADS)]


def _gmlp_load(u_ref, v_ref, ws_ref, bst_ref, vg_ref, og_ref):
    gu = _hslices(u_ref, GM_HEAD_DIM, GM_HEADS)
    gv = _hslices(v_ref, GM_HEAD_DIM, GM_HEADS)
    ws = [ws_ref[h] for h in range(GM_HEADS)]
    bs_cols = [bst_ref[:, h:h + 1] for h in range(GM_HEADS)]
    return gu, gv, ws, bs_cols, _hslices(vg_ref, GM_HEAD_DIM, GM_HEADS), _hslices(og_ref, GM_HEAD_DIM, GM_HEADS)


def _gmlp_specs():
    return [pl.BlockSpec((CHUNK, D_GM), lambda i: (i, COL_U // D_GM)),
            pl.BlockSpec((CHUNK, D_GM), lambda i: (i, COL_V // D_GM)),
            pl.BlockSpec((GM_HEADS, CHUNK, CHUNK), lambda i: (0, 0, 0)),
            pl.BlockSpec((CHUNK, GM_HEADS), lambda i: (0, 0)),
            pl.BlockSpec((1, D_GM), lambda i: (0, 0)),
            pl.BlockSpec((1, D_GM), lambda i: (0, 0))]


def _gmlp_fwd(proj, ycat, ws, bst, vg, og):
    t = proj.shape[0]

    def body(u_ref, v_ref, ws_ref, bst_ref, vg_ref, og_ref, ycat_ref, o_ref):
        del ycat_ref
        outs = _gmlp_chunk(*_gmlp_load(u_ref, v_ref, ws_ref, bst_ref, vg_ref, og_ref))
        for h in range(GM_HEADS):
            o_ref[:, h * GM_HEAD_DIM:(h + 1) * GM_HEAD_DIM] = _b(outs[h])

    return pl.pallas_call(
        body, name="gmlp_fwd", grid=(t // CHUNK,),
        in_specs=_gmlp_specs() + [ANY],
        out_specs=pl.BlockSpec((CHUNK, D_GM), lambda i: (i, D_SSD // D_GM)),
        out_shape=jax.ShapeDtypeStruct(ycat.shape, ycat.dtype),
        input_output_aliases={6: 0},
        compiler_params=_cp("parallel"),
    )(proj, proj, ws, bst, vg, og, ycat)


def _gmlp_bwd(dy, proj, ws, bst, vg, og, dproj):
    t = proj.shape[0]
    w = GM_HEAD_DIM

    def body(dy_ref, u_ref, v_ref, ws_ref, bst_ref, vg_ref, og_ref, dproj_ref,
             dgm_ref, dws_ref, dbst_ref, dvg_ref, dog_ref):
        del dproj_ref

        @pl.when(pl.program_id(0) == 0)
        def _():
            dws_ref[...] = jnp.zeros_like(dws_ref)
            dbst_ref[...] = jnp.zeros_like(dbst_ref)
            dvg_ref[...] = jnp.zeros_like(dvg_ref)
            dog_ref[...] = jnp.zeros_like(dog_ref)

        _, vjp = jax.vjp(_gmlp_chunk, *_gmlp_load(u_ref, v_ref, ws_ref, bst_ref, vg_ref, og_ref))
        dgu, dgv, dws, dbs, dvg, dog = vjp(_hslices(dy_ref, w, GM_HEADS))
        for h in range(GM_HEADS):
            dgm_ref[:, h * w:(h + 1) * w] = _b(dgu[h])
            dgm_ref[:, D_GM + h * w: D_GM + (h + 1) * w] = _b(dgv[h])
            dws_ref[h] += dws[h]
            dbst_ref[:, h:h + 1] += dbs[h]
            dvg_ref[:, h * w:(h + 1) * w] += dvg[h]
            dog_ref[:, h * w:(h + 1) * w] += dog[h]

    return pl.pallas_call(
        body, name="gmlp_bwd", grid=(t // CHUNK,),
        in_specs=[pl.BlockSpec((CHUNK, D_GM), lambda i: (i, 1))] + _gmlp_specs() + [ANY],
        out_specs=[pl.BlockSpec((CHUNK, 2 * D_GM), lambda i: (i, COL_U // (2 * D_GM))),
                   pl.BlockSpec((GM_HEADS, CHUNK, CHUNK), lambda i: (0, 0, 0)),
                   pl.BlockSpec((CHUNK, GM_HEADS), lambda i: (0, 0)),
                   pl.BlockSpec((1, D_GM), lambda i: (0, 0)),
                   pl.BlockSpec((1, D_GM), lambda i: (0, 0))],
        out_shape=[jax.ShapeDtypeStruct(dproj.shape, dproj.dtype), jax.ShapeDtypeStruct((GM_HEADS, CHUNK, CHUNK), F32),
                   jax.ShapeDtypeStruct((CHUNK, GM_HEADS), F32), jax.ShapeDtypeStruct((1, D_GM), F32),
                   jax.ShapeDtypeStruct((1, D_GM), F32)],
        input_output_aliases={7: 0},
        compiler_params=_cp("arbitrary"),
    )(dy, proj, proj, ws, bst, vg, og, dproj)


def _local_step(x, target, mods, lw, final_g, *, nseq, big_w, grad_sink, small_sink):
    saved = []
    xin, delta, gate = x, None, None
    for l in range(DEPTH):
        w = lw[l]
        sh1, sc1, g1, sh2, sc2, g2 = mods[l]
        x0, h1 = _normmod_fwd(xin, delta, gate, w["norm1_g"], sc1, sh1, nseq=nseq, name=f"norm1_fwd_{l}")
        w_in = big_w(l, "w_in", h1)
        proj = _matmul(h1, w_in, tb=True, name=f"mm_in_{l}")
        xbc, xbc_pre = _ssd_conv_fwd(proj, w["ssd_conv_w"], w["ssd_conv_b"], nseq=nseq)
        ycat, hprev = _ssd_fwd(xbc, proj, w["ssd_dt_bias"], w["ssd_a_log"], w["ssd_d"], w["ssd_norm_g"], nseq=nseq)
        ycat = _gmlp_fwd(proj, ycat, w["gm_ws"], w["gm_bst"], w["gm_vnorm_g"], w["gm_out_g"])
        w_out = big_w(l, "w_out", ycat)
        mix = _matmul(ycat, w_out, name=f"mm_out_{l}")
        x1, h2 = _normmod_fwd(x0, mix, g1, w["norm2_g"], sc2, sh2, nseq=nseq, name=f"norm2_fwd_{l}")
        ff_up = big_w(l, "ff_up", h2)
        up = _matmul(h2, ff_up, tb=True, name=f"mm_up_{l}")
        act = _ffn_act_fwd(up, w["ff_conv_w"], w["ff_conv_b"], nseq=nseq)
        ff_down = big_w(l, "ff_down", act)
        dn = _matmul(act, ff_down, name=f"mm_down_{l}")
        saved.append(dict(x0=x0, xin_delta=delta, xin_gate=gate, h1=h1, proj=proj, xbc=xbc, xbc_pre=xbc_pre, hprev=hprev,
                          ycat=ycat, mix=mix, x1=x1, h2=h2, up=up, act=act, dn=dn,
                          w_in=w_in, w_out=w_out, ff_up=ff_up, ff_down=ff_down))
        xin, delta, gate = x1, dn, g2

    loss, dx, ddelta, dgate, dfg = _final_loss(xin, delta, gate, final_g, target, nseq=nseq)

    small, dmods = [None] * DEPTH, [None] * DEPTH
    for l in reversed(range(DEPTH)):
        w, sv = lw[l], saved[l]
        sh1, sc1, g1, sh2, sc2, g2 = mods[l]
        dg2 = dgate
        g_ff_down = _matmul(sv["act"], ddelta, ta=True, name=f"mm_down_dw_{l}", out_dtype=BF16)
        dact = _matmul(ddelta, sv["ff_down"], tb=True, name=f"mm_down_dx_{l}")
        dgate_ff, dval_ff, dfcw, dfcb = _ffn_act_bwd(dact, sv["up"], w["ff_conv_w"], w["ff_conv_b"], nseq=nseq)
        g_ff_up = _matmul([dgate_ff, dval_ff], sv["h2"], ta=True, name=f"mm_up_dw_{l}", out_dtype=BF16)
        dep = grad_sink(l, "ffn", dict(ff_down=g_ff_down, ff_up=g_ff_up), dval_ff)
        dh2 = _matmul([dgate_ff, dval_ff], sv["ff_up"], name=f"mm_up_dx_{l}", dep=dep)
        dx, dmix, dg1, dn2g, dsc2, dsh2 = _normmod_bwd(dh2, dx, sv["x1"], sv["mix"], g1, w["norm2_g"], sc2,
                                                       nseq=nseq, name=f"norm2_bwd_{l}")
        g_w_out = _matmul(sv["ycat"], dmix, ta=True, name=f"mm_out_dw_{l}", out_dtype=BF16)
        dep = grad_sink(l, "w_out", dict(w_out=g_w_out), dmix)
        dycat = _matmul(dmix, sv["w_out"], tb=True, name=f"mm_out_dx_{l}", dep=dep)
        dxbc_act, dproj, ddtb, dalog, ddsk, dng = _ssd_bwd(dycat, sv["xbc"], sv["proj"], sv["hprev"], w["ssd_dt_bias"],
                                                          w["ssd_a_log"], w["ssd_d"], w["ssd_norm_g"], nseq=nseq)
        dproj, dscw, dscb = _ssd_conv_bwd(dxbc_act, sv["xbc_pre"], sv["proj"], w["ssd_conv_w"], dproj, nseq=nseq)
        dproj, dws, dbst, dvg, dog = _gmlp_bwd(dycat, sv["proj"], w["gm_ws"], w["gm_bst"], w["gm_vnorm_g"], w["gm_out_g"], dproj)
        early = dict(norm2_g=dn2g, ssd_norm_g=dng, gm_vnorm_g=dvg, gm_out_g=dog,
                     ssd_conv_w=dscw, ssd_conv_b=dscb, ff_conv_w=dfcw, ff_conv_b=dfcb,
                     ssd_dt_bias=ddtb, ssd_a_log=dalog, ssd_d=ddsk, gm_ws=dws, gm_bs=dbst.T)
        dep = small_sink(l, early, small, dmods, dfg, loss)
        g_w_in = _matmul(dproj, sv["h1"], ta=True, name=f"mm_in_dw_{l}", out_dtype=BF16, dep=dep)
        dep = grad_sink(l, "w_in", dict(w_in=g_w_in), dproj)
        dh1 = _matmul(dproj, sv["w_in"], name=f"mm_in_dx_{l}", dep=dep)
        dx, ddelta, dgate, dn1g, dsc1, dsh1 = _normmod_bwd(dh1, dx, sv["x0"], sv["xin_delta"], sv["xin_gate"],
                                                           w["norm1_g"], sc1, nseq=nseq, name=f"norm1_bwd_{l}")
        small[l] = dict(early, norm1_g=dn1g)
        dmods[l] = jnp.concatenate([dsh1, dsc1, dg1, dsh2, dsc2, dg2], axis=-1)[:, 0, :]
    return dx, small, dmods


def _all_gather(arrs, name, dep=None):
    n = len(arrs)
    extra = [] if dep is None else [dep]

    def body(*refs):
        ins, outs = refs[:n], refs[n + len(extra):2 * n + len(extra)]
        send_sems, recv_sems, local_sems = refs[2 * n + len(extra):]
        x, y, c = lax.axis_index("x"), lax.axis_index("y"), lax.axis_index("c")
        me, sibling = (x, y, c), (x, y, 1 - c)
        chips = [(1 - x, y), (x, 1 - y), (1 - x, 1 - y)]

        def copy(i, k, block, to, src=None):
            px, py, pc = block
            dst = outs[i].at[4 * px + 2 * py + pc]
            return pltpu.make_async_remote_copy(
                src_ref=dst if src is None else src, dst_ref=dst,
                send_sem=send_sems.at[7 * i + k], recv_sem=recv_sems.at[7 * i + k],
                device_id=to, device_id_type=MESH)

        mine = [pltpu.make_async_copy(ins[i], outs[i].at[4 * x + 2 * y + c], local_sems.at[i]) for i in range(n)]
        for cp in mine:
            cp.start()
        first = []
        for i in range(n):
            first.append(copy(i, 0, me, sibling, src=ins[i]))
            first += [copy(i, 1 + j, me, (*chip, c), src=ins[i]) for j, chip in enumerate(chips)]
        for cp in first:
            cp.start()
        passed = []
        for j, chip in enumerate(chips):
            for i in range(n):
                copy(i, 1 + j, (*chip, c), me).wait_recv()
                fwd = copy(i, 4 + j, (*chip, c), sibling)
                fwd.start()
                passed.append(fwd)
        for i in range(n):
            copy(i, 0, sibling, me).wait_recv()
            for j, chip in enumerate(chips):
                copy(i, 4 + j, (*chip, 1 - c), me).wait_recv()
        for cp in first + passed:
            cp.wait_send()
        for cp in mine:
            cp.wait()

    return pl.pallas_call(
        body, name=name,
        in_specs=[ANY] * (n + len(extra)), out_specs=[ANY] * n,
        out_shape=[jax.ShapeDtypeStruct((N_DEV,) + a.shape, a.dtype) for a in arrs],
        scratch_shapes=[pltpu.SemaphoreType.DMA((7 * n,)), pltpu.SemaphoreType.DMA((7 * n,)),
                        pltpu.SemaphoreType.DMA((n,))],
    )(*arrs, *extra)


def _exchange_sibling(arrs, name):
    n = len(arrs)

    def body(*refs):
        ins, outs = refs[:n], refs[n:2 * n]
        send_sems, recv_sems = refs[2 * n:]
        x, y, c = lax.axis_index("x"), lax.axis_index("y"), lax.axis_index("c")
        copies = []
        for i in range(n):
            for k in range(4):
                copies.append(pltpu.make_async_remote_copy(
                    src_ref=ins[i].at[2 * k + (1 - c)], dst_ref=outs[i].at[k],
                    send_sem=send_sems.at[4 * i + k], recv_sem=recv_sems.at[4 * i + k],
                    device_id=(x, y, 1 - c), device_id_type=MESH))
        for cp in copies:
            cp.start()
        for cp in copies:
            cp.wait_recv()
        for cp in copies:
            cp.wait_send()

    return pl.pallas_call(
        body, name=name,
        in_specs=[ANY] * n, out_specs=[ANY] * n,
        out_shape=[jax.ShapeDtypeStruct((4,) + a.shape[1:], a.dtype) for a in arrs],
        scratch_shapes=[pltpu.SemaphoreType.DMA((4 * n,)), pltpu.SemaphoreType.DMA((4 * n,))],
    )(*arrs)


def _exchange_chips(arrs, name):
    n = len(arrs)

    def body(*refs):
        ins, outs = refs[:n], refs[n:2 * n]
        send_sems, recv_sems = refs[2 * n:]
        x, y, c = lax.axis_index("x"), lax.axis_index("y"), lax.axis_index("c")
        chips = [(1 - x, y), (x, 1 - y), (1 - x, 1 - y)]
        copies = []
        for i in range(n):
            for j, (cx, cy) in enumerate(chips):
                copies.append(pltpu.make_async_remote_copy(
                    src_ref=ins[i].at[2 * cx + cy], dst_ref=outs[i].at[j],
                    send_sem=send_sems.at[3 * i + j], recv_sem=recv_sems.at[3 * i + j],
                    device_id=(cx, cy, c), device_id_type=MESH))
        for cp in copies:
            cp.start()
        for cp in copies:
            cp.wait_recv()
        for cp in copies:
            cp.wait_send()

    return pl.pallas_call(
        body, name=name,
        in_specs=[ANY] * n, out_specs=[ANY] * n,
        out_shape=[jax.ShapeDtypeStruct((3,) + a.shape[1:], a.dtype) for a in arrs],
        scratch_shapes=[pltpu.SemaphoreType.DMA((3 * n,)), pltpu.SemaphoreType.DMA((3 * n,))],
    )(*arrs)


def _add_sibling(a, r, pos, name):
    _, depth, rows, cols = a.shape
    tr = _tile(rows, 256) if rows % 8 == 0 else rows
    a3 = a.reshape(N_DEV * depth, rows, cols)
    r3 = r.reshape(4 * depth, rows, cols)

    def body(pos_ref, a_ref, r_ref, o_ref):
        o_ref[...] = a_ref[...] + r_ref[...]

    out = pl.pallas_call(
        body, name=name,
        grid_spec=pltpu.PrefetchScalarGridSpec(
            num_scalar_prefetch=1, grid=(4 * depth, rows // tr),
            in_specs=[pl.BlockSpec((1, tr, cols), lambda q, i, p: ((2 * (q // depth) + p[0]) * depth + q % depth, i, 0)),
                      pl.BlockSpec((1, tr, cols), lambda q, i, p: (q, i, 0))],
            out_specs=pl.BlockSpec((1, tr, cols), lambda q, i, p: (q, i, 0))),
        out_shape=jax.ShapeDtypeStruct((4 * depth, rows, cols), F32),
        compiler_params=_cp("parallel", "parallel"),
    )(pos, a3, r3)
    return out.reshape(4, depth, rows, cols)


HBM = pl.BlockSpec(memory_space=pltpu.HBM)
SEM = pl.BlockSpec(memory_space=pltpu.SEMAPHORE)
EFFECT = pltpu.SideEffectType.DATAFLOW_SIDE_EFFECTING


def _peer(k):
    x, y, c = lax.axis_index("x"), lax.axis_index("y"), lax.axis_index("c")
    return (1 - x if k & 4 else x, 1 - y if k & 2 else y, 1 - c if k & 1 else c)


ALL_PEERS = tuple(range(1, N_DEV))
OTHER_CHIPS = (2, 4, 6)


def _xc_copies(scatter, srcs, lands, send_sems, recv_sems, peers=ALL_PEERS):
    x, y, c = lax.axis_index("x"), lax.axis_index("y"), lax.axis_index("c")
    copies = []
    for i in range(len(srcs)):
        for k in peers:
            px, py, pc = _peer(k)
            src = srcs[i].at[4 * px + 2 * py + pc] if scatter else srcs[i]
            dst = lands[i].at[k - 1] if scatter else lands[i].at[4 * x + 2 * y + c]
            copies.append(pltpu.make_async_remote_copy(
                src_ref=src, dst_ref=dst, send_sem=send_sems[i].at[k - 1], recv_sem=recv_sems[i].at[k - 1],
                device_id=(px, py, pc), device_id_type=MESH))
    return copies


def _xc_start(scatter, arrs, after, name, peers=ALL_PEERS):
    n = len(arrs)
    lands = [lax.empty((N_DEV - 1,) + a.shape[1:] if scatter else (N_DEV,) + a.shape, a.dtype) for a in arrs]

    def body(*refs):
        srcs, lnd = refs[:n], refs[n:2 * n]
        send_sems, recv_sems = refs[2 * n + 1:3 * n + 1], refs[3 * n + 1:4 * n + 1]
        token = refs[6 * n + 1]
        for cp in _xc_copies(scatter, srcs, lnd, send_sems, recv_sems, peers):
            cp.start()
        token[...] = jnp.zeros_like(token)

    outs = pl.pallas_call(
        body, name=name,
        out_shape=[pltpu.SemaphoreType.DMA((N_DEV - 1,))] * (2 * n)
        + [pltpu.HBM(a.shape, a.dtype) for a in arrs] + [pltpu.HBM(a.shape, a.dtype) for a in lands]
        + [jax.ShapeDtypeStruct((8, 128), F32)],
        in_specs=[HBM] * (2 * n) + [ANY],
        out_specs=[SEM] * (2 * n) + [HBM] * (2 * n) + [pl.BlockSpec(memory_space=pltpu.VMEM)],
        input_output_aliases={i: 2 * n + i for i in range(2 * n)},
        compiler_params=pltpu.CompilerParams(has_side_effects=EFFECT),
    )(*[pltpu.with_memory_space_constraint(a, pltpu.HBM) for a in list(arrs) + lands], after)
    return outs[:n], outs[n:2 * n], outs[2 * n:3 * n], outs[3 * n:4 * n], outs[4 * n][0, 0]


def _xc_wait(scatter, send_sems, recv_sems, srcs, lands, after, name, peers=ALL_PEERS):
    n = len(srcs)

    def body(*refs):
        s_refs, l_refs = refs[:n], refs[n:2 * n]
        ss, rs = refs[2 * n:3 * n], refs[3 * n:4 * n]
        for cp in _xc_copies(scatter, s_refs, l_refs, ss, rs, peers):
            cp.wait_send()
            cp.wait_recv()

    outs = pl.pallas_call(
        body, name=name,
        out_shape=[pltpu.HBM(a.shape, a.dtype) for a in list(srcs) + list(lands)],
        in_specs=[HBM] * (2 * n) + [SEM] * (2 * n) + [ANY],
        out_specs=[HBM] * (2 * n),
        input_output_aliases={i: i for i in range(2 * n)},
        compiler_params=pltpu.CompilerParams(has_side_effects=EFFECT),
    )(*srcs, *lands, *send_sems, *recv_sems, after)
    return outs[:n], outs[n:]


def _sib_copies(zones, send_sems, recv_sems):
    x, y, c = lax.axis_index("x"), lax.axis_index("y"), lax.axis_index("c")
    copies = []
    for i in range(len(zones)):
        for q in range(N_DEV // 2):
            slot = zones[i].at[2 * q + c]
            copies.append(pltpu.make_async_remote_copy(
                src_ref=slot, dst_ref=slot, send_sem=send_sems[i].at[q], recv_sem=recv_sems[i].at[q],
                device_id=(x, y, 1 - c), device_id_type=MESH))
    return copies


def _sib_start(zones, name):
    n = len(zones)

    def body(*refs):
        for cp in _sib_copies(refs[:n], refs[n:2 * n], refs[2 * n:3 * n]):
            cp.start()

    outs = pl.pallas_call(
        body, name=name,
        out_shape=[pltpu.SemaphoreType.DMA((N_DEV // 2,))] * (2 * n) + [pltpu.HBM(a.shape, a.dtype) for a in zones],
        in_specs=[HBM] * n,
        out_specs=[SEM] * (2 * n) + [HBM] * n,
        input_output_aliases={i: 2 * n + i for i in range(n)},
        compiler_params=pltpu.CompilerParams(has_side_effects=EFFECT),
    )(*[pltpu.with_memory_space_constraint(a, pltpu.HBM) for a in zones])
    return outs[:n], outs[n:2 * n], outs[2 * n:]


def _sib_wait(send_sems, recv_sems, zones, name):
    n = len(zones)

    def body(*refs):
        for cp in _sib_copies(refs[:n], refs[n:2 * n], refs[2 * n:3 * n]):
            cp.wait_send()
            cp.wait_recv()

    return pl.pallas_call(
        body, name=name,
        out_shape=[pltpu.HBM(a.shape, a.dtype) for a in zones],
        in_specs=[HBM] * n + [SEM] * (2 * n),
        out_specs=[HBM] * n,
        input_output_aliases={i: i for i in range(n)},
        compiler_params=pltpu.CompilerParams(has_side_effects=EFFECT),
    )(*zones, *send_sems, *recv_sems)


def _adamw_math(w, g, m, v):
    m = ADAM_B1 * m + (1.0 - ADAM_B1) * g
    v = ADAM_B2 * v + (1.0 - ADAM_B2) * (g * g)
    m_hat = m / (1.0 - ADAM_B1 ** ADAM_STEP)
    v_hat = v / (1.0 - ADAM_B2 ** ADAM_STEP)
    delta = -ADAM_LR * (m_hat / (jnp.sqrt(v_hat) + ADAM_EPS) + ADAM_WD * w)
    return delta, m, v


def _adamw_sharded(parts, w, m, v, pos, name):
    depth, rows, cols = w.shape
    tr = _tile(rows, 256) if rows % 8 == 0 else rows
    npart = len(parts)

    def body(pos_ref, *refs):
        prefs = refs[:npart]
        w_ref, m_ref, v_ref, g_out, d_out, m_out, v_out = refs[npart:]
        g = prefs[0][...]
        for pr in prefs[1:]:
            g = g + pr[...]
        delta, mn, vn = _adamw_math(w_ref[...], g, m_ref[...], v_ref[...])
        g_out[...] = g
        d_out[...] = delta
        m_out[...] = mn
        v_out[...] = vn

    def part_spec(fn):
        return pl.BlockSpec((1, tr, cols), lambda l, i, p: (fn(p) * depth + l, i, 0))

    blk = pl.BlockSpec((1, tr, cols), lambda l, i, p: (l, i, 0))
    shp = jax.ShapeDtypeStruct((depth, rows, cols), F32)
    return pl.pallas_call(
        body, name=name,
        grid_spec=pltpu.PrefetchScalarGridSpec(
            num_scalar_prefetch=1, grid=(depth, rows // tr),
            in_specs=[part_spec(fn) for _, fn in parts] + [blk, blk, blk],
            out_specs=[blk, blk, blk, blk]),
        out_shape=[shp, shp, shp, shp],
        compiler_params=_cp("parallel", "parallel"),
    )(pos, *[a for a, _ in parts], w, m, v)


def _adamw_layer(parts, w, m, v, pos, layer, prev, name):
    depth, rows, cols = w.shape
    npart = len(parts)
    nprev = 0 if prev is None else 4
    if rows % 16 == 0:
        tr, tc = max(t for t in range(16, 257, 16) if rows % t == 0), cols
    else:
        tr, tc = rows, _tile(cols, 256)
    pick = (lambda i: (i, 0)) if rows % 16 == 0 else (lambda i: (0, i))

    def body(pos_ref, *refs):
        prefs = refs[:npart]
        w_ref, m_ref, v_ref = refs[npart:npart + 3]
        g_out, d_out, m_out, v_out = refs[npart + 3 + nprev:]
        g = prefs[0][...].astype(F32)
        for pr in prefs[1:]:
            g = g + pr[...].astype(F32)
        delta, mn, vn = _adamw_math(w_ref[...], g, m_ref[...], v_ref[...])
        g_out[...] = g
        d_out[...] = delta
        m_out[...] = mn
        v_out[...] = vn

    def part_spec(fn):
        return pl.BlockSpec((1, tr, tc), lambda i, p: (fn(p), *pick(i)))

    blk = pl.BlockSpec((1, tr, tc), lambda i, p: (layer, *pick(i)))
    shp = jax.ShapeDtypeStruct((depth, rows, cols), F32)
    first_prev = 1 + npart + 3
    return pl.pallas_call(
        body, name=name,
        grid_spec=pltpu.PrefetchScalarGridSpec(
            num_scalar_prefetch=1, grid=(rows // tr * (cols // tc),),
            in_specs=[part_spec(fn) for _, fn in parts] + [blk, blk, blk] + [ANY] * nprev,
            out_specs=[blk, blk, blk, blk]),
        out_shape=[shp, shp, shp, shp],
        input_output_aliases={first_prev + j: j for j in range(nprev)},
        compiler_params=_cp("parallel"),
    )(pos, *[a for a, _ in parts], w, m, v, *(prev or ()))


_P1024 = ["norm1_g", "norm2_g", "ssd_norm_g", "gm_vnorm_g", "gm_out_g"]
_P16 = ["ssd_dt_bias", "ssd_a_log", "ssd_d"]


def _adamw_small(gath, wmv):
    names = list(wmv.keys())
    classes = list(gath.keys())
    flat_in = [gath[k] for k in classes]
    for nme in names:
        flat_in += list(wmv[nme])
    out_shapes = []
    for nme in names:
        out_shapes += [jax.ShapeDtypeStruct(wmv[nme][0].shape, F32)] * 4
    out_shapes += [jax.ShapeDtypeStruct((DEPTH, SSD_CONV, CONV_DIM), F32), jax.ShapeDtypeStruct((DEPTH, FF_CONV, D_FF), F32),
                   jax.ShapeDtypeStruct((1, SSD_HEADS), F32)]
    scratch = [pltpu.VMEM(gath[k].shape[1:], F32) for k in classes]
    ncls = len(classes)

    def body(*refs):
        g_refs = dict(zip(classes, refs[:ncls]))
        pos = ncls
        w_refs = {}
        for nme in names:
            w_refs[nme] = refs[pos:pos + 3]
            pos += 3
        o_refs = {}
        for nme in names:
            o_refs[nme] = refs[pos:pos + 4]
            pos += 4
        scw_out, fcw_out, loss_out = refs[pos], refs[pos + 1], refs[pos + 2]
        s_refs = dict(zip(classes, refs[pos + 3:]))
        for k in classes:
            acc = g_refs[k][0]
            for dev in range(1, N_DEV):
                acc = acc + g_refs[k][dev]
            s_refs[k][...] = acc

        def apply(nme, grad_of):
            w_ref, m_ref, v_ref = w_refs[nme]
            g_out, d_out, m_out, v_out = o_refs[nme]
            shape = w_ref.shape
            if len(shape) == 2:
                idxs = [(slice(l, l + 1),) for l in range(shape[0])]
            elif len(shape) == 3:
                idxs = [(l,) for l in range(shape[0])]
            else:
                idxs = [(l, h) for l in range(shape[0]) for h in range(shape[1])]
            for n_i, ix in enumerate(idxs):
                g = grad_of(n_i)
                delta, mn, vn = _adamw_math(w_ref[ix], g, m_ref[ix], v_ref[ix])
                g_out[ix] = g
                d_out[ix] = delta
                m_out[ix] = mn
                v_out[ix] = vn

        s1024, s1536, s2816, s16, s128, s6144, late1024, late6144 = (s_refs[k] for k in classes)
        s1024[0:1, :] += late1024[...]
        s6144[0:late6144.shape[0], :] += late6144[...]
        for n_i, nme in enumerate(_P1024):
            apply(nme, lambda l, b=2 * n_i: s1024[b + l:b + l + 1, :])
        apply("final_g", lambda l: s1024[10:11, :])
        apply("ssd_conv_b", lambda l: s1536[8 + l:9 + l, :])
        apply("ff_conv_b", lambda l: s2816[6 + l:7 + l, :])
        for n_i, nme in enumerate(_P16):
            apply(nme, lambda l, b=2 * n_i: s16[b + l:b + l + 1, :])
        apply("gm_ws", lambda q: s128[q * CHUNK:(q + 1) * CHUNK, :])
        apply("gm_bs", lambda l: s128[2048 + 8 * l:2048 + 8 * (l + 1), :])
        apply("ada_b", lambda l: s6144[2 * l:2 * l + 1, :] + s6144[2 * l + 1:2 * l + 2, :])
        for l in range(DEPTH):
            scw_out[l] = s1536[SSD_CONV * l:SSD_CONV * (l + 1), :]
            fcw_out[l] = s2816[FF_CONV * l:FF_CONV * (l + 1), :]
        loss_out[...] = s16[2 * len(_P16):2 * len(_P16) + 1, :]

    outs = pl.pallas_call(
        body, name="adamw_small",
        out_shape=out_shapes,
        scratch_shapes=scratch,
        compiler_params=pltpu.CompilerParams(vmem_limit_bytes=VMEM_LIMIT),
    )(*flat_in)
    res = {nme: tuple(outs[4 * i:4 * i + 4]) for i, nme in enumerate(names)}
    return res, outs[-3], outs[-2], outs[-1]


_WEIGHTS = ['ada_w', 'ada_b', 'norm1_g', 'norm2_g', 'w_in', 'ssd_conv_w', 'ssd_conv_b', 'ssd_dt_bias', 'ssd_a_log',
            'ssd_d', 'ssd_norm_g', 'gm_vnorm_g', 'gm_ws', 'gm_bs', 'gm_out_g', 'w_out', 'ff_up', 'ff_conv_w',
            'ff_conv_b', 'ff_down', 'final_g']


_O_XBC, _O_DT, _O_GM = D_SSD, D_SSD + CONV_DIM, D_SSD + CONV_DIM + SSD_HEADS


_TRANSPOSED = ("w_in", "ff_up")


def _full_weight(name, g):
    full = g.reshape(g.shape[0] * g.shape[1], g.shape[2])
    if name != "w_in":
        return full
    zpad = jnp.zeros((N_INP - N_IN, full.shape[1]), full.dtype)
    return jnp.concatenate([full[_O_GM:], full[:_O_XBC], full[_O_XBC:_O_DT], full[_O_DT:_O_GM], zpad], axis=0)


def _by_owner(name, grad):
    if name == "w_in":
        grad = jnp.concatenate([grad[COL_Z:COL_XBC], grad[COL_XBC:COL_DT], grad[COL_DT:COL_DT + SSD_HEADS], grad[:COL_Z]], axis=0)
    return grad.reshape(N_DEV, grad.shape[0] // N_DEV, grad.shape[1])


def kernel(x, c, ada_w, ada_b, norm1_g, norm2_g, w_in, ssd_conv_w, ssd_conv_b, ssd_dt_bias, ssd_a_log, ssd_d, ssd_norm_g, gm_vnorm_g, gm_ws, gm_bs, gm_out_g, w_out, ff_up, ff_conv_w, ff_conv_b, ff_down, final_g, loss_target, m_ada_w, m_ada_b, m_norm1_g, m_norm2_g, m_w_in, m_ssd_conv_w, m_ssd_conv_b, m_ssd_dt_bias, m_ssd_a_log, m_ssd_d, m_ssd_norm_g, m_gm_vnorm_g, m_gm_ws, m_gm_bs, m_gm_out_g, m_w_out, m_ff_up, m_ff_conv_w, m_ff_conv_b, m_ff_down, m_final_g, v_ada_w, v_ada_b, v_norm1_g, v_norm2_g, v_w_in, v_ssd_conv_w, v_ssd_conv_b, v_ssd_dt_bias, v_ssd_a_log, v_ssd_d, v_ssd_norm_g, v_gm_vnorm_g, v_gm_ws, v_gm_bs, v_gm_out_g, v_w_out, v_ff_up, v_ff_conv_w, v_ff_conv_b, v_ff_down, v_final_g):
    given = dict(locals())
    wts = {n: given[n] for n in _WEIGHTS}
    mom = {n: given["m_" + n] for n in _WEIGHTS}
    var = {n: given["v_" + n] for n in _WEIGHTS}
    nseq, seq, d = x.shape
    ix, iy, ic = lax.axis_index("x"), lax.axis_index("y"), lax.axis_index("c")
    me = 4 * ix + 2 * iy + ic
    me_arr = me.astype(jnp.int32).reshape(1)

    for nme in _TRANSPOSED:
        wts[nme], mom[nme], var[nme] = (jnp.transpose(a, (0, 2, 1)) for a in (wts[nme], mom[nme], var[nme]))

    def shard(l, name):
        return _b(wts[name][l])

    g_scw, g_fcw, c_all = _all_gather([ssd_conv_w, ff_conv_w, c], "gather_first")
    scw_f = jnp.transpose(g_scw, (1, 2, 0, 3)).reshape(DEPTH, SSD_CONV, CONV_DIM)
    fcw_f = jnp.transpose(g_fcw, (1, 2, 0, 3)).reshape(DEPTH, FF_CONV, D_FF)
    c_all = c_all.reshape(N_DEV * nseq, d)

    n_ada = ada_w.shape[2]
    ada_b_shard = lax.dynamic_slice_in_dim(ada_b, me * n_ada, n_ada, axis=1).reshape(DEPTH, 1, n_ada)
    mod_part, c_act = _ada_fwd(c_all, ada_w, ada_b_shard)
    (mod_g,) = _all_gather([mod_part], "gather_mod")
    mod_all = jnp.transpose(mod_g, (1, 2, 0, 3)).reshape(DEPTH, N_DEV * nseq, N_MOD * d)
    mod_mine = lax.dynamic_slice_in_dim(mod_all, me * nseq, nseq, axis=1)
    mods = [[mod_mine[l, :, k * d:(k + 1) * d].reshape(nseq, 1, d) for k in range(N_MOD)] for l in range(DEPTH)]

    first_ssem, first_rsem, first_src, first_land, first_zero = _xc_start(
        False, [shard(0, "w_in")], mod_g, "ag_first_start", peers=OTHER_CHIPS)
    later = [(0, "w_out"), (0, "ff_up"), (0, "ff_down"), (1, "w_in"), (1, "w_out"), (1, "ff_up"), (1, "ff_down")]
    ag_ssem, ag_rsem, ag_src, ag_land, ag_zero = _xc_start(
        False, [shard(l, n) for l, n in later], first_zero.reshape(1, 1), "ag_start")
    ag_groups = {(0, "w_out"): [0], (0, "ff_up"): [1, 2], (1, "w_in"): [3, 4], (1, "ff_up"): [5, 6]}
    big_cache = {}

    def big_w(l, name, after):
        if (l, name) == (0, "w_in") and (l, name) not in big_cache:
            srcs, lands = _xc_wait(False, first_ssem, first_rsem, first_src, first_land, after, "ag_first_wait",
                                   peers=OTHER_CHIPS)
            zone = lax.dynamic_update_index_in_dim(lands[0], srcs[0], me, 0)
            (zone,) = _sib_wait(*_sib_start([zone], "ag_first_sib_start"), "ag_first_sib_wait")
            big_cache[(l, name)] = _full_weight(name, zone)
        if (l, name) not in big_cache:
            idx = ag_groups[(l, name)]
            pick = lambda seq_: [seq_[i] for i in idx]
            srcs, lands = _xc_wait(False, pick(ag_ssem), pick(ag_rsem), pick(ag_src), pick(ag_land), after,
                                   f"ag_wait_{l}_{name}")
            for i, src, land in zip(idx, srcs, lands):
                big_cache[later[i]] = _full_weight(later[i][1], lax.dynamic_update_index_in_dim(land, src, me, 0))
        return big_cache[(l, name)]

    lw = []
    for l in range(DEPTH):
        lw.append(dict(
            norm1_g=norm1_g[l:l + 1] + (ag_zero if l == 0 else 0.0), norm2_g=norm2_g[l:l + 1], ssd_conv_w=scw_f[l],
            ssd_conv_b=ssd_conv_b[l:l + 1], ssd_dt_bias=ssd_dt_bias[l:l + 1], ssd_a_log=ssd_a_log[l:l + 1],
            ssd_d=ssd_d[l:l + 1], ssd_norm_g=ssd_norm_g[l:l + 1], gm_vnorm_g=gm_vnorm_g[l:l + 1], gm_ws=gm_ws[l],
            gm_bst=gm_bs[l].T, gm_out_g=gm_out_g[l:l + 1], ff_conv_w=fcw_f[l], ff_conv_b=ff_conv_b[l:l + 1]))

    outs = {}
    pending = {}

    def rs_finish(l, group, after):
        names, ssem, rsem, srcs, lands = pending.pop((l, group))
        srcs, lands = _xc_wait(True, ssem, rsem, srcs, lands, after, f"rs_wait_{l}_{group}")
        for nme, own, land in zip(names, srcs, lands):
            parts = [(own, lambda p: p[0])] + [(land, lambda p, k=k: k) for k in range(N_DEV - 1)]
            outs[nme] = _adamw_layer(parts, wts[nme], mom[nme], var[nme], me_arr, l, outs.get(nme), f"adamw_{nme}_{l}")
        return outs[names[-1]][0]

    def grad_sink(l, group, grads, after):
        names = list(grads)
        ssem, rsem, srcs, lands, zero = _xc_start(True, [_by_owner(n, grads[n]) for n in names], after, f"rs_start_{l}_{group}")
        pending[(l, group)] = (names, ssem, rsem, srcs, lands)
        return zero.reshape(1, 1)

    early_gather = {}

    def small_sink(l, early, small, dmods, dfg, loss_p):
        if l > 0:
            return None
        layers = [dict(early, norm1_g=jnp.zeros((1, d), F32))] + small[1:]
        rows = lambda name: [layers[k][name] for k in range(DEPTH)]
        packed = [
            jnp.concatenate(sum([rows(n) for n in _P1024], []) + [dfg], axis=0),
            jnp.concatenate(rows("ssd_conv_w") + rows("ssd_conv_b"), axis=0),
            jnp.concatenate(rows("ff_conv_w") + rows("ff_conv_b"), axis=0),
            jnp.concatenate(sum([rows(n) for n in _P16], []) + [loss_p[:, :SSD_HEADS]], axis=0),
            jnp.concatenate([layers[k]["gm_ws"].reshape(GM_HEADS * CHUNK, CHUNK) for k in range(DEPTH)] + rows("gm_bs"), axis=0),
            jnp.concatenate([jnp.zeros((nseq, N_MOD * d), F32)] + dmods[1:], axis=0)]
        ssem, rsem, srcs, lands, zero = _xc_start(False, packed, packed[0], "small_start")
        early_gather.update(ssem=ssem, rsem=rsem, srcs=srcs, lands=lands)
        return zero.reshape(1, 1)

    grad_x, small, dmods = _local_step(
        x.reshape(nseq * seq, d), loss_target.reshape(nseq * seq, d), mods, lw, final_g.reshape(1, d), nseq=nseq,
        big_w=big_w, grad_sink=grad_sink, small_sink=small_sink)

    done = grad_x
    for l, grp in ((1, "ffn"), (1, "w_out"), (1, "w_in"), (0, "ffn"), (0, "w_out")):
        done = rs_finish(l, grp, done)
    srcs, lands = _xc_wait(False, early_gather["ssem"], early_gather["rsem"], early_gather["srcs"],
                           early_gather["lands"], done, "small_wait")
    gathered = [lax.dynamic_update_index_in_dim(land, src, me, 0) for src, land in zip(srcs, lands)]
    gathered += _all_gather([small[0]["norm1_g"], dmods[0]], "gather_late", dep=gathered[0])
    gath = dict(zip(["p1024", "p1536", "p2816", "p16", "p128", "p6144", "late1024", "late6144"], gathered))

    dmod_all = jnp.concatenate([gath["late6144"].reshape(1, N_DEV * nseq, N_MOD * d),
                                jnp.transpose(gath["p6144"].reshape(N_DEV, DEPTH, nseq, N_MOD * d)[:, 1:], (1, 0, 2, 3)).reshape(
                                    DEPTH - 1, N_DEV * nseq, N_MOD * d)], axis=0)
    small_names = _P1024 + ["final_g", "ssd_conv_b", "ff_conv_b"] + _P16 + ["gm_ws", "gm_bs", "ada_b"]
    wmv = {}
    for nme in small_names:
        if nme == "final_g":
            wmv[nme] = tuple(a.reshape(1, d) for a in (wts[nme], mom[nme], var[nme]))
        else:
            wmv[nme] = (wts[nme], mom[nme], var[nme])
    small_out, scw_full, fcw_full, loss_sum = _adamw_small(gath, wmv)
    loss = loss_sum[0, 0]
    rs_finish(0, "w_in", scw_full)
    for nme in small_names:
        outs[nme] = small_out[nme]
    outs["final_g"] = tuple(a.reshape(d) for a in outs["final_g"])

    n_scw, n_fcw = ssd_conv_w.shape[2], ff_conv_w.shape[2]
    g_scw_mine = lax.dynamic_slice_in_dim(scw_full, me * n_scw, n_scw, axis=2)
    g_fcw_mine = lax.dynamic_slice_in_dim(fcw_full, me * n_fcw, n_fcw, axis=2)
    outs["ssd_conv_w"] = _adamw_sharded([(g_scw_mine, lambda p: 0)], ssd_conv_w, m_ssd_conv_w, v_ssd_conv_w, me_arr, "adamw_ssd_conv_w")
    outs["ff_conv_w"] = _adamw_sharded([(g_fcw_mine, lambda p: 0)], ff_conv_w, m_ff_conv_w, v_ff_conv_w, me_arr, "adamw_ff_conv_w")

    dmod_cols = _b(lax.dynamic_slice_in_dim(dmod_all, me * n_ada, n_ada, axis=2))
    g_ada = jnp.stack([_matmul(c_act, dmod_cols[l], ta=True, name=f"mm_ada_dw_{l}") for l in range(DEPTH)])
    outs["ada_w"] = _adamw_sharded([(g_ada, lambda p: 0)], ada_w, m_ada_w, v_ada_w, me_arr, "adamw_ada_w")

    for nme in _TRANSPOSED:
        outs[nme] = tuple(jnp.transpose(a, (0, 2, 1)) for a in outs[nme])
    result = [loss, grad_x.reshape(nseq, seq, d)]
    for k in range(4):
        result += [outs[n][k] for n in _WEIGHTS]
    return tuple(result)
```

```python
import functools
import math

import jax
import jax.numpy as jnp
from jax import lax
from jax.experimental import pallas as pl
from jax.experimental.pallas import tpu as pltpu

F32 = jnp.float32
BF16 = jnp.bfloat16

N_DEV = 8
D_MODEL = 1024
DEPTH = 2
CHUNK = 128
SSD_HEADS = 16
SSD_HEAD_DIM = 64
SSD_GROUPS = 2
HEADS_PER_GROUP = SSD_HEADS // SSD_GROUPS
GROUP_WIDTH = HEADS_PER_GROUP * SSD_HEAD_DIM
D_STATE = 128
D_SSD = 1024
CONV_DIM = 1536
SSD_CONV = 4
GM_HEADS = 8
GM_HEAD_DIM = 128
D_GM = 1024
D_FF = 2816
FF_CONV = 3
N_IN = 4624
N_MOD = 6
EPS = 1e-6

N_INP = 5120
COL_U, COL_V, COL_Z, COL_XBC, COL_DT = 0, 1024, 2048, 3072, 4608
DT_BLOCK = 512

ADAM_LR = 0.001
ADAM_B1 = 0.9
ADAM_B2 = 0.999
ADAM_EPS = 1e-08
ADAM_WD = 0.01
ADAM_STEP = 10

VMEM_LIMIT = 56 * 1024 * 1024
MESH = pl.DeviceIdType.MESH
ANY = pl.BlockSpec(memory_space=pl.ANY)


def _cp(*sem):
    return pltpu.CompilerParams(dimension_semantics=sem, vmem_limit_bytes=VMEM_LIMIT)


def _tile(n, pref):
    if n <= pref or n % 128:
        return n
    best = 128
    for t in range(128, pref + 1, 128):
        if n % t == 0:
            best = t
    return best


def _silu(x):
    return x * jax.nn.sigmoid(x)


def _gelu(x):
    return 0.5 * x * (1.0 + lax.erf(x * (1.0 / math.sqrt(2.0))))


def _softplus(x):
    return jnp.maximum(x, 0.0) + jnp.log1p(jnp.exp(-jnp.abs(x)))


def _rms(x, g, width):
    return x * lax.rsqrt(jnp.sum(x * x, axis=-1, keepdims=True) / width + EPS) * g


def _b(x):
    return x.astype(BF16)


_NN = (((1,), (0,)), ((), ()))
_NT = (((1,), (1,)), ((), ()))
_TN = (((0,), (0,)), ((), ()))


def _dg(a, b, dn):
    return lax.dot_general(_b(a), _b(b), dn, preferred_element_type=F32)


@jax.custom_vjp
def _bdot(a, b):
    return _dg(a, b, _NN)


def _bdot_fwd(a, b):
    return _dg(a, b, _NN), (a, b)


def _bdot_bwd(res, ct):
    a, b = res
    return _dg(ct, b, _NT), _dg(a, ct, _TN)


_bdot.defvjp(_bdot_fwd, _bdot_bwd)


@jax.custom_vjp
def _bdot_nt(a, b):
    return _dg(a, b, _NT)


def _bdot_nt_fwd(a, b):
    return _dg(a, b, _NT), (a, b)


def _bdot_nt_bwd(res, ct):
    a, b = res
    return _dg(ct, b, _NN), _dg(ct, a, _TN)


_bdot_nt.defvjp(_bdot_nt_fwd, _bdot_nt_bwd)


@jax.custom_vjp
def _bdot_tn(a, b):
    return _dg(a, b, _TN)


def _bdot_tn_fwd(a, b):
    return _dg(a, b, _TN), (a, b)


def _bdot_tn_bwd(res, ct):
    a, b = res
    return _dg(b, ct, _NT), _dg(a, ct, _NN)


_bdot_tn.defvjp(_bdot_tn_fwd, _bdot_tn_bwd)


def _tri(n, lower):
    r = lax.broadcasted_iota(jnp.int32, (n, n), 0)
    c = lax.broadcasted_iota(jnp.int32, (n, n), 1)
    return ((r >= c) if lower else (r <= c)).astype(F32)


def _eye(n):
    r = lax.broadcasted_iota(jnp.int32, (n, n), 0)
    c = lax.broadcasted_iota(jnp.int32, (n, n), 1)
    return (r == c).astype(F32)


def _hdot(a, b, dn):
    return lax.dot_general(a, b, dn, precision=lax.Precision.HIGHEST, preferred_element_type=F32)


@jax.custom_vjp
def _cumsum_rows(x):
    return _hdot(_tri(x.shape[0], True), x, _NN)


def _cumsum_rows_fwd(x):
    return _cumsum_rows(x), None


def _cumsum_rows_bwd(_, ct):
    return (_hdot(_tri(ct.shape[0], False), ct, _NN),)


_cumsum_rows.defvjp(_cumsum_rows_fwd, _cumsum_rows_bwd)


@jax.custom_vjp
def _transpose(x):
    return _hdot(_eye(x.shape[1]), x, _NT)


def _transpose_fwd(x):
    return _transpose(x), None


def _transpose_bwd(_, ct):
    return (_hdot(_eye(ct.shape[1]), ct, _NT),)


_transpose.defvjp(_transpose_fwd, _transpose_bwd)


MXU_WIDTH = 256
MATMUL_TILE_CAP = 2816
MATMUL_VMEM = 44 * 1024 * 1024


def _mxu_tiles(n):
    if n <= MATMUL_TILE_CAP or n % 128:
        return [n]
    for unit in (MXU_WIDTH, 128):
        opts = [t for t in range(unit, MATMUL_TILE_CAP + 1, unit) if n % t == 0]
        if opts:
            return opts
    return [n]


def _matmul(a, b, *, ta=False, tb=False, name, dep=None, out_dtype=F32):
    pieces = list(a) if isinstance(a, (list, tuple)) else [a]
    npc = len(pieces)
    rows, width = pieces[0].shape
    assert all(p.shape == (rows, width) for p in pieces)
    if ta:
        k_dim, m_dim = rows, width * npc
    else:
        m_dim, k_dim = rows, width * npc
    if tb:
        n_dim, kb = b.shape
    else:
        kb, n_dim = b.shape
    assert kb == k_dim, (pieces[0].shape, npc, b.shape, ta, tb)
    m_unit = width if npc > 1 and ta else m_dim
    k_unit = width if npc > 1 and not ta else k_dim
    tm = _tile(m_unit, 1536)
    tn_opts, tk_opts = _mxu_tiles(n_dim), _mxu_tiles(k_unit)
    tn, tk = tn_opts.pop(), tk_opts.pop()
    while 4 * (tm * tk + tk * tn) + 8 * tm * tn > MATMUL_VMEM:
        if tn >= tk and tn_opts:
            tn = tn_opts.pop()
        else:
            tk = tk_opts.pop()
    ni, nj, nk = m_dim // tm, n_dim // tn, k_dim // tk
    per = width // (tm if ta else tk)
    dn = (((0 if ta else 1,), (1 if tb else 0,)), ((), ()))

    a_bytes, b_bytes = m_dim * k_dim, k_dim * n_dim
    m_outer = nk > 1 or a_bytes + b_bytes * ni <= b_bytes + a_bytes * nj
    if m_outer:
        ij = lambda o, n, k: (o, n)
        grid = (ni, nj, nk)
    else:
        ij = lambda o, n, k: (n, o)
        grid = (nj, ni, nk)

    use_acc = nk > 1 and out_dtype != F32

    def body(*refs):
        a_refs, b_ref = refs[:npc], refs[npc]
        o_ref = refs[-2] if use_acc else refs[-1]
        acc_ref = refs[-1]
        k = pl.program_id(2)
        i = pl.program_id(0 if m_outer else 1)
        along = i if ta else k

        def step(a_ref):
            p = lax.dot_general(a_ref[...], b_ref[...], dn, preferred_element_type=F32)
            if nk == 1:
                o_ref[...] = p.astype(out_dtype)
            else:
                @pl.when(k == 0)
                def _():
                    acc_ref[...] = p

                @pl.when((k > 0) & (k < nk - 1 if use_acc else True))
                def _():
                    acc_ref[...] += p

                if use_acc:
                    @pl.when(k == nk - 1)
                    def _():
                        o_ref[...] = (acc_ref[...] + p).astype(out_dtype)

        if npc == 1:
            step(a_refs[0])
        else:
            for pc in range(npc):
                pl.when((along >= pc * per) & (along < (pc + 1) * per))(functools.partial(step, a_refs[pc]))

    def a_map(pc, o, n, k):
        i, _ = ij(o, n, k)
        along = i if ta else k
        if npc > 1:
            along = jnp.clip(along - pc * per, 0, per - 1)
        return (k, along) if ta else (i, along)

    def b_map(o, n, k):
        _, j = ij(o, n, k)
        return (j, k) if tb else (k, j)

    extra = [] if dep is None else [dep]
    return pl.pallas_call(
        body, name=name,
        grid=grid,
        in_specs=[pl.BlockSpec((tk, tm) if ta else (tm, tk), functools.partial(a_map, pc)) for pc in range(npc)]
        + [pl.BlockSpec((tn, tk) if tb else (tk, tn), b_map)] + [ANY] * len(extra),
        out_specs=pl.BlockSpec((tm, tn), lambda o, n, k: ij(o, n, k)),
        out_shape=jax.ShapeDtypeStruct((m_dim, n_dim), out_dtype),
        scratch_shapes=[pltpu.VMEM((tm, tn), F32)] if use_acc else [],
        compiler_params=_cp("parallel", "parallel", "arbitrary"),
    )(*pieces, b, *extra)


def _ada_fwd(c_all, ada_w, ada_b_shard):
    depth, d, n = ada_w.shape
    nb = c_all.shape[0]

    def body(c_ref, w_ref, b_ref, o_ref, ca_ref):
        ca = _silu(c_ref[...])
        ca_ref[...] = _b(ca)
        o_ref[0] = _dg(ca, w_ref[0], _NN) + b_ref[0]

    return pl.pallas_call(
        body, name="ada_fwd",
        grid=(depth,),
        in_specs=[pl.BlockSpec((nb, d), lambda l: (0, 0)),
                  pl.BlockSpec((1, d, n), lambda l: (l, 0, 0)),
                  pl.BlockSpec((1, 1, n), lambda l: (l, 0, 0))],
        out_specs=[pl.BlockSpec((1, nb, n), lambda l: (l, 0, 0)),
                   pl.BlockSpec((nb, d), lambda l: (0, 0))],
        out_shape=[jax.ShapeDtypeStruct((depth, nb, n), F32), jax.ShapeDtypeStruct((nb, d), BF16)],
        compiler_params=_cp("arbitrary"),
    )(c_all, ada_w, ada_b_shard)


def _fold(acc):
    return jnp.sum(acc, axis=0, keepdims=True)


def _rinv(x):
    return lax.rsqrt(jnp.sum(x * x, axis=-1, keepdims=True) * (1.0 / D_MODEL) + EPS)


def _rms_bwd(a, xhat, rinv):
    return rinv * (a - xhat * (jnp.sum(a * xhat, axis=-1, keepdims=True) * (1.0 / D_MODEL)))


def _row_tile(seq):
    return min(seq, 256)


def _normmod_fwd(xin, delta, gate, g, sc, sh, *, nseq, name):
    t, d = xin.shape
    seq = t // nseq
    tr = _row_tile(seq)
    nt = seq // tr
    has_delta = delta is not None
    row = pl.BlockSpec((tr, d), lambda s, i: (s * nt + i, 0))
    per_seq = pl.BlockSpec((1, 1, d), lambda s, i: (s, 0, 0))
    vec = pl.BlockSpec((1, d), lambda s, i: (0, 0))

    if has_delta:
        def body(xin_ref, delta_ref, gate_ref, g_ref, sc_ref, sh_ref, x_ref, h_ref):
            x = xin_ref[...] + gate_ref[0] * delta_ref[...]
            x_ref[...] = x
            h_ref[...] = _b(x * _rinv(x) * (g_ref[...] * (1.0 + sc_ref[0])) + sh_ref[0])

        return pl.pallas_call(
            body, name=name, grid=(nseq, nt),
            in_specs=[row, row, per_seq, vec, per_seq, per_seq],
            out_specs=[row, row],
            out_shape=[jax.ShapeDtypeStruct((t, d), F32), jax.ShapeDtypeStruct((t, d), BF16)],
            compiler_params=_cp("parallel", "parallel"),
        )(xin, delta, gate, g, sc, sh)

    def body0(xin_ref, g_ref, sc_ref, sh_ref, h_ref):
        x = xin_ref[...]
        h_ref[...] = _b(x * _rinv(x) * (g_ref[...] * (1.0 + sc_ref[0])) + sh_ref[0])

    h = pl.pallas_call(
        body0, name=name, grid=(nseq, nt),
        in_specs=[row, vec, per_seq, per_seq],
        out_specs=row,
        out_shape=jax.ShapeDtypeStruct((t, d), BF16),
        compiler_params=_cp("parallel", "parallel"),
    )(xin, g, sc, sh)
    return xin, h


def _normmod_bwd(dh, dxo, x, delta, gate, g, sc, *, nseq, name):
    t, d = x.shape
    seq = t // nseq
    tr = _row_tile(seq)
    nt = seq // tr
    has_delta = delta is not None
    row = pl.BlockSpec((tr, d), lambda s, i: (s * nt + i, 0))
    per_seq = pl.BlockSpec((1, 1, d), lambda s, i: (s, 0, 0))
    vec = pl.BlockSpec((1, d), lambda s, i: (0, 0))

    def core(dh_ref, dxo_ref, x_ref, g_ref, sc_ref, dx_ref, dg_ref, dsc_ref, dsh_ref, residual=None):
        s, i = pl.program_id(0), pl.program_id(1)
        g_v, one_sc = g_ref[...], 1.0 + sc_ref[0]
        x, dh_v = x_ref[...], dh_ref[...]
        rinv = _rinv(x)
        xhat = x * rinv
        dx = dxo_ref[...] + _rms_bwd(dh_v * (g_v * one_sc), xhat, rinv)
        dx_ref[...] = dx
        if residual is not None:
            delta_ref, gate_ref, dd_ref, _ = residual
            dd_ref[...] = _b(dx * gate_ref[0])
            acc_g = dx * delta_ref[...]
        acc_t, acc_h = dh_v * xhat, dh_v

        @pl.when((s == 0) & (i == 0))
        def _():
            dg_ref[...] = jnp.zeros_like(dg_ref)

        @pl.when(i == 0)
        def _():
            dsc_ref[...] = jnp.zeros_like(dsc_ref)
            dsh_ref[...] = jnp.zeros_like(dsh_ref)
            if residual is not None:
                residual[3][...] = jnp.zeros_like(residual[3])

        t_sum = _fold(acc_t)
        dg_ref[...] += t_sum * one_sc
        dsc_ref[0] += t_sum * g_v
        dsh_ref[0] += _fold(acc_h)
        if residual is not None:
            residual[3][0] += _fold(acc_g)

    if has_delta:
        def body(dh_ref, dxo_ref, x_ref, delta_ref, gate_ref, g_ref, sc_ref,
                 dx_ref, dd_ref, dgate_ref, dg_ref, dsc_ref, dsh_ref):
            core(dh_ref, dxo_ref, x_ref, g_ref, sc_ref, dx_ref, dg_ref, dsc_ref, dsh_ref,
                 residual=(delta_ref, gate_ref, dd_ref, dgate_ref))

        return pl.pallas_call(
            body, name=name, grid=(nseq, nt),
            in_specs=[row, row, row, row, per_seq, vec, per_seq],
            out_specs=[row, row, per_seq, vec, per_seq, per_seq],
            out_shape=[jax.ShapeDtypeStruct((t, d), F32), jax.ShapeDtypeStruct((t, d), BF16),
                       jax.ShapeDtypeStruct((nseq, 1, d), F32), jax.ShapeDtypeStruct((1, d), F32),
                       jax.ShapeDtypeStruct((nseq, 1, d), F32), jax.ShapeDtypeStruct((nseq, 1, d), F32)],
            compiler_params=_cp("arbitrary", "arbitrary"),
        )(dh, dxo, x, delta, gate, g, sc)

    def body0(dh_ref, dxo_ref, x_ref, g_ref, sc_ref, dx_ref, dg_ref, dsc_ref, dsh_ref):
        core(dh_ref, dxo_ref, x_ref, g_ref, sc_ref, dx_ref, dg_ref, dsc_ref, dsh_ref)

    dx, dg, dsc, dsh = pl.pallas_call(
        body0, name=name, grid=(nseq, nt),
        in_specs=[row, row, row, vec, per_seq],
        out_specs=[row, vec, per_seq, per_seq],
        out_shape=[jax.ShapeDtypeStruct((t, d), F32), jax.ShapeDtypeStruct((1, d), F32),
                   jax.ShapeDtypeStruct((nseq, 1, d), F32), jax.ShapeDtypeStruct((nseq, 1, d), F32)],
        compiler_params=_cp("arbitrary", "arbitrary"),
    )(dh, dxo, x, g, sc)
    return dx, None, None, dg, dsc, dsh


def _final_loss(xin, delta, gate, fg, target, *, nseq):
    t, d = xin.shape
    seq = t // nseq
    tr = _row_tile(seq)
    nt = seq // tr
    row = pl.BlockSpec((tr, d), lambda s, i: (s * nt + i, 0))
    per_seq = pl.BlockSpec((1, 1, d), lambda s, i: (s, 0, 0))
    vec = pl.BlockSpec((1, d), lambda s, i: (0, 0))

    def body(xin_ref, delta_ref, gate_ref, fg_ref, tgt_ref, loss_ref, dx_ref, dd_ref, dgate_ref, dfg_ref):
        s, i = pl.program_id(0), pl.program_id(1)
        fg_v, gate_v = fg_ref[...], gate_ref[0]
        dl = delta_ref[...]
        x = xin_ref[...] + gate_v * dl
        rinv = _rinv(x)
        xhat = x * rinv
        err = xhat * fg_v - tgt_ref[...]
        dx = _rms_bwd(err * fg_v * (1.0 / d), xhat, rinv)
        dx_ref[...] = dx
        dd_ref[...] = _b(dx * gate_v)
        acc_l, acc_f, acc_g = err * err, err * xhat, dx * dl

        @pl.when((s == 0) & (i == 0))
        def _():
            loss_ref[...] = jnp.zeros_like(loss_ref)
            dfg_ref[...] = jnp.zeros_like(dfg_ref)

        @pl.when(i == 0)
        def _():
            dgate_ref[...] = jnp.zeros_like(dgate_ref)

        loss_ref[...] += jnp.sum(acc_l) * (0.5 / d)
        dfg_ref[...] += _fold(acc_f) * (1.0 / d)
        dgate_ref[0] += _fold(acc_g)

    return pl.pallas_call(
        body, name="final_loss", grid=(nseq, nt),
        in_specs=[row, row, per_seq, vec, row],
        out_specs=[pl.BlockSpec((1, 128), lambda s, i: (0, 0)), row, row, per_seq, vec],
        out_shape=[jax.ShapeDtypeStruct((1, 128), F32), jax.ShapeDtypeStruct((t, d), F32),
                   jax.ShapeDtypeStruct((t, d), BF16), jax.ShapeDtypeStruct((nseq, 1, d), F32),
                   jax.ShapeDtypeStruct((1, d), F32)],
        compiler_params=_cp("arbitrary", "arbitrary"),
    )(xin, delta, gate, fg, target)


CONV_TC = 256
CONV_LANES = 128
CONV_ROWS = 64
CONV_HALO = 8


def _conv_slabs(seq, fn):
    def step(i, carry):
        r0 = pl.multiple_of(i * CONV_ROWS, CONV_ROWS)
        for h in range(CONV_TC // CONV_LANES):
            fn(r0, slice(h * CONV_LANES, (h + 1) * CONV_LANES))
        return carry

    lax.fori_loop(0, seq // CONV_ROWS, step, 0)


def _slab(ref, r0, cols, seq, before=0, after=0):
    parts = []
    if before:
        top = ref[pl.ds(pl.multiple_of(jnp.maximum(r0 - before, 0), CONV_HALO), before), cols]
        parts.append(jnp.where(r0 > 0, top, 0.0))
    parts.append(ref[pl.ds(r0, CONV_ROWS), cols])
    if after:
        bot = ref[pl.ds(pl.multiple_of(jnp.minimum(r0 + CONV_ROWS, seq - after), CONV_HALO), after), cols]
        parts.append(jnp.where(r0 + CONV_ROWS < seq, bot, 0.0))
    return jnp.concatenate(parts, axis=0) if len(parts) > 1 else parts[0]


def _conv_block(x, w_ref, b_ref):
    kw = w_ref.shape[0]
    rows = lax.broadcasted_iota(jnp.int32, x.shape, 0)
    y = b_ref[...] + w_ref[kw - 1:kw, :] * x
    for j in range(1, kw):
        y = y + w_ref[kw - 1 - j:kw - j, :] * jnp.where(rows >= j, pltpu.roll(x, j, 0), 0.0)
    return y


def _conv_block_bwd(dy, x, w_ref, dw_ref, db_ref):
    kw = w_ref.shape[0]
    n = x.shape[0]
    rows = lax.broadcasted_iota(jnp.int32, x.shape, 0)
    dx = w_ref[kw - 1:kw, :] * dy
    dw_ref[kw - 1:kw, :] += jnp.sum(dy * x, axis=0, keepdims=True)
    for j in range(1, kw):
        dy_j = jnp.where(rows < n - j, pltpu.roll(dy, n - j, 0), 0.0)
        dx = dx + w_ref[kw - 1 - j:kw - j, :] * dy_j
        dw_ref[kw - 1 - j:kw - j, :] += jnp.sum(dy_j * x, axis=0, keepdims=True)
    db_ref[...] += jnp.sum(dy, axis=0, keepdims=True)
    return dx


def _conv_bwd(dy_ext, x, w_ref, dw_ref, db_ref, cols):
    kw = w_ref.shape[0]
    n = dy_ext.shape[0]
    dy = dy_ext[:CONV_ROWS]
    dx = w_ref[kw - 1:kw, cols] * dy
    dw_ref[kw - 1:kw, cols] += jnp.sum(dy * x, axis=0, keepdims=True)
    for j in range(1, kw):
        dy_j = pltpu.roll(dy_ext, n - j, 0)[:CONV_ROWS]
        dx = dx + w_ref[kw - 1 - j:kw - j, cols] * dy_j
        dw_ref[kw - 1 - j:kw - j, cols] += jnp.sum(dy_j * x, axis=0, keepdims=True)
    db_ref[:, cols] += jnp.sum(dy, axis=0, keepdims=True)
    return dx


def _dsilu(pre):
    sg = jax.nn.sigmoid(pre)
    return pre * sg, sg * (1.0 + pre * (1.0 - sg))


def _ssd_conv_fwd(proj, w, b, *, nseq):
    t = proj.shape[0]
    seq = t // nseq
    nb = CONV_DIM // CONV_TC
    off = COL_XBC // CONV_TC

    def body(x_ref, w_ref, b_ref, o_ref, pre_ref):
        pre = _conv_block(x_ref[...], w_ref, b_ref)
        pre_ref[...] = pre
        o_ref[...] = _silu(pre)

    col = pl.BlockSpec((seq, CONV_TC), lambda j, s: (s, j))
    return pl.pallas_call(
        body, name="ssd_conv_fwd", grid=(nb, nseq),
        in_specs=[pl.BlockSpec((seq, CONV_TC), lambda j, s: (s, off + j)),
                  pl.BlockSpec((SSD_CONV, CONV_TC), lambda j, s: (0, j)),
                  pl.BlockSpec((1, CONV_TC), lambda j, s: (0, j))],
        out_specs=[col, col],
        out_shape=[jax.ShapeDtypeStruct((t, CONV_DIM), F32)] * 2,
        compiler_params=_cp("parallel", "parallel"),
    )(proj, w, b)


def _ssd_conv_bwd(dact, pre, proj, w, dproj, *, nseq):
    t = proj.shape[0]
    seq = t // nseq
    nb = CONV_DIM // CONV_TC
    off = COL_XBC // CONV_TC

    def body(da_ref, pre_ref, x_ref, w_ref, dproj_ref, dx_ref, dw_ref, db_ref):
        del dproj_ref

        @pl.when(pl.program_id(1) == 0)
        def _():
            dw_ref[...] = jnp.zeros_like(dw_ref)
            db_ref[...] = jnp.zeros_like(db_ref)

        def slab(r0, cols):
            _, dsilu = _dsilu(_slab(pre_ref, r0, cols, seq, after=CONV_HALO))
            dpre_ext = _slab(da_ref, r0, cols, seq, after=CONV_HALO) * dsilu
            x = x_ref[pl.ds(r0, CONV_ROWS), cols]
            dx_ref[pl.ds(r0, CONV_ROWS), cols] = _b(_conv_bwd(dpre_ext, x, w_ref, dw_ref, db_ref, cols))

        _conv_slabs(seq, slab)

    return pl.pallas_call(
        body, name="ssd_conv_bwd", grid=(nb, nseq),
        in_specs=[pl.BlockSpec((seq, CONV_TC), lambda j, s: (s, j)),
                  pl.BlockSpec((seq, CONV_TC), lambda j, s: (s, j)),
                  pl.BlockSpec((seq, CONV_TC), lambda j, s: (s, off + j)),
                  pl.BlockSpec((SSD_CONV, CONV_TC), lambda j, s: (0, j)),
                  ANY],
        out_specs=[pl.BlockSpec((seq, CONV_TC), lambda j, s: (s, off + j)),
                   pl.BlockSpec((SSD_CONV, CONV_TC), lambda j, s: (0, j)),
                   pl.BlockSpec((1, CONV_TC), lambda j, s: (0, j))],
        out_shape=[jax.ShapeDtypeStruct(dproj.shape, dproj.dtype), jax.ShapeDtypeStruct((SSD_CONV, CONV_DIM), F32),
                   jax.ShapeDtypeStruct((1, CONV_DIM), F32)],
        input_output_aliases={4: 0},
        compiler_params=_cp("parallel", "arbitrary"),
    )(dact, pre, proj, w, dproj)


def _ffn_act_fwd(up, w, b, *, nseq):
    t = up.shape[0]
    seq = t // nseq
    nb = D_FF // CONV_TC

    def body(g_ref, v_ref, w_ref, b_ref, o_ref):
        o_ref[...] = _b(_silu(_conv_block(g_ref[...].astype(F32), w_ref, b_ref)) * v_ref[...].astype(F32))

    col = pl.BlockSpec((seq, CONV_TC), lambda j, s: (s, j))
    return pl.pallas_call(
        body, name="ffn_act_fwd", grid=(nb, nseq),
        in_specs=[col,
                  pl.BlockSpec((seq, CONV_TC), lambda j, s: (s, nb + j)),
                  pl.BlockSpec((FF_CONV, CONV_TC), lambda j, s: (0, j)),
                  pl.BlockSpec((1, CONV_TC), lambda j, s: (0, j))],
        out_specs=col,
        out_shape=jax.ShapeDtypeStruct((t, D_FF), BF16),
        compiler_params=_cp("parallel", "parallel"),
    )(up, up, w, b)


def _ffn_act_bwd(dact, up, w, b, *, nseq):
    t = up.shape[0]
    seq = t // nseq
    nb = D_FF // CONV_TC

    def body(da_ref, g_ref, v_ref, w_ref, b_ref, dg_ref, dv_ref, dw_ref, db_ref):
        @pl.when(pl.program_id(1) == 0)
        def _():
            dw_ref[...] = jnp.zeros_like(dw_ref)
            db_ref[...] = jnp.zeros_like(db_ref)

        gate = g_ref[...].astype(F32)
        silu, dsilu = _dsilu(_conv_block(gate, w_ref, b_ref))
        da = da_ref[...].astype(F32)
        dv_ref[...] = _b(da * silu)
        dg_ref[...] = _b(_conv_block_bwd(da * v_ref[...].astype(F32) * dsilu, gate, w_ref, dw_ref, db_ref))

    col = pl.BlockSpec((seq, CONV_TC), lambda j, s: (s, j))
    return pl.pallas_call(
        body, name="ffn_act_bwd", grid=(nb, nseq),
        in_specs=[col, col,
                  pl.BlockSpec((seq, CONV_TC), lambda j, s: (s, nb + j)),
                  pl.BlockSpec((FF_CONV, CONV_TC), lambda j, s: (0, j)),
                  pl.BlockSpec((1, CONV_TC), lambda j, s: (0, j))],
        out_specs=[col, col,
                   pl.BlockSpec((FF_CONV, CONV_TC), lambda j, s: (0, j)),
                   pl.BlockSpec((1, CONV_TC), lambda j, s: (0, j))],
        out_shape=[jax.ShapeDtypeStruct((t, D_FF), BF16), jax.ShapeDtypeStruct((t, D_FF), BF16),
                   jax.ShapeDtypeStruct((FF_CONV, D_FF), F32), jax.ShapeDtypeStruct((1, D_FF), F32)],
        compiler_params=_cp("parallel", "arbitrary"),
    )(dact, up, up, w, b)


SSD_PAIRS = SSD_HEADS // 2
PAIR_W = 2 * SSD_HEAD_DIM
PAIRS_PER_GROUP = SSD_PAIRS // SSD_GROUPS


def _ssd_chunk(xs, bg, cg, dtr, z, hp, dtb, alog, dskip, ng):
    n = dtr.shape[0]
    dt = _softplus(dtr + dtb)
    cs = _cumsum_rows(dt * (-jnp.exp(alog)))
    cs_t = _transpose(cs)
    lane = lax.broadcasted_iota(jnp.int32, (1, SSD_HEADS), 1)
    sub = lax.broadcasted_iota(jnp.int32, (SSD_HEADS, 1), 0)
    row = lax.broadcasted_iota(jnp.int32, (n, 1), 0)
    causal = lax.broadcasted_iota(jnp.int32, (n, n), 0) >= lax.broadcasted_iota(jnp.int32, (n, n), 1)
    future = jnp.where(causal, 0.0, -1e30)
    first = lax.broadcasted_iota(jnp.int32, (1, PAIR_W), 1) < SSD_HEAD_DIM
    first_rows = lax.broadcasted_iota(jnp.int32, (PAIR_W, 1), 0) < SSD_HEAD_DIM
    first_f = first.astype(F32)
    cb = [_bdot_nt(cg[g], bg[g]) for g in range(SSD_GROUPS)]
    ys, hn = [], []
    for p in range(SSD_PAIRS):
        g = p // PAIRS_PER_GROUP
        col, decay, last = [], [], []
        for h in (2 * p, 2 * p + 1):
            oh = (lane == h).astype(F32)
            cs_h = jnp.sum(cs * oh, axis=1, keepdims=True)
            cs_row = jnp.sum(cs_t * (sub == h).astype(F32), axis=0, keepdims=True)
            col.append((jnp.sum(dt * oh, axis=1, keepdims=True), cs_h, jnp.sum(dskip * oh, axis=1, keepdims=True)))
            last.append(jnp.sum(jnp.where(row == n - 1, cs_h, 0.0), axis=0, keepdims=True))
            decay.append(jnp.exp(cs_h - cs_row + future))
        pair = lambda a, b: jnp.where(first, a, b)
        dt_p = pair(col[0][0], col[1][0])
        cs_p = pair(col[0][1], col[1][1])
        last_p = pair(last[0], last[1])
        xc = xs[p] * dt_p
        y = _bdot(cb[g] * decay[0], xc * first_f) + _bdot(cb[g] * decay[1], xc * (1.0 - first_f))
        y = y + _bdot_nt(cg[g], hp[p]) * jnp.exp(cs_p)
        y = y + pair(col[0][2], col[1][2]) * xs[p]
        keep = jnp.where(first_rows, jnp.exp(last[0]), jnp.exp(last[1]))
        hn.append(keep * hp[p] + _bdot_tn(xc * jnp.exp(last_p - cs_p), bg[g]))
        ys.append(y * _silu(z[p]))
    outs = []
    for g in range(SSD_GROUPS):
        ps = range(g * PAIRS_PER_GROUP, (g + 1) * PAIRS_PER_GROUP)
        ms = sum(jnp.sum(ys[p] * ys[p], axis=1, keepdims=True) for p in ps) * (1.0 / GROUP_WIDTH)
        r = lax.rsqrt(ms + EPS)
        outs += [ys[p] * r * ng[p] for p in ps]
    return outs, hn


def _hslices(ref, width, count, base=0):
    return [ref[:, base + k * width: base + (k + 1) * width] for k in range(count)]


def _ssd_load(xbc_ref, z_ref, dt_ref, ng_ref):
    xs = _hslices(xbc_ref, PAIR_W, SSD_PAIRS)
    bg = _hslices(xbc_ref, D_STATE, SSD_GROUPS, D_SSD)
    cg = _hslices(xbc_ref, D_STATE, SSD_GROUPS, D_SSD + SSD_GROUPS * D_STATE)
    z = _hslices(z_ref, PAIR_W, SSD_PAIRS)
    ng = _hslices(ng_ref, PAIR_W, SSD_PAIRS)
    return xs, bg, cg, dt_ref[:, 0:SSD_HEADS], z, ng


def _ssd_specs(nch):
    rowi = lambda s, c: s * nch + c
    return [pl.BlockSpec((CHUNK, CONV_DIM), lambda s, c: (rowi(s, c), 0)),
            pl.BlockSpec((CHUNK, D_SSD), lambda s, c: (rowi(s, c), COL_Z // D_SSD)),
            pl.BlockSpec((CHUNK, 128), lambda s, c: (rowi(s, c), COL_DT // 128)),
            pl.BlockSpec((1, SSD_HEADS), lambda s, c: (0, 0)),
            pl.BlockSpec((1, SSD_HEADS), lambda s, c: (0, 0)),
            pl.BlockSpec((1, SSD_HEADS), lambda s, c: (0, 0)),
            pl.BlockSpec((1, D_SSD), lambda s, c: (0, 0))]


def _ssd_fwd(xbc, proj, dtb, alog, dskip, ng, *, nseq):
    t = proj.shape[0]
    nch = t // nseq // CHUNK
    hd = PAIR_W

    def body(xbc_ref, z_ref, dt_ref, dtb_ref, alog_ref, dsk_ref, ng_ref, y_ref, hp_ref, h_ref):
        @pl.when(pl.program_id(1) == 0)
        def _():
            h_ref[...] = jnp.zeros_like(h_ref)

        xs, bg, cg, dtr, z, ngs = _ssd_load(xbc_ref, z_ref, dt_ref, ng_ref)
        hp_ref[0] = h_ref[...]
        hp = [h_ref[h * hd:(h + 1) * hd, :] for h in range(SSD_PAIRS)]
        outs, hn = _ssd_chunk(xs, bg, cg, dtr, z, hp, dtb_ref[...], alog_ref[...], dsk_ref[...], ngs)
        for h in range(SSD_PAIRS):
            y_ref[:, h * hd:(h + 1) * hd] = _b(outs[h])
            h_ref[h * hd:(h + 1) * hd, :] = hn[h]

    return pl.pallas_call(
        body, name="ssd_fwd", grid=(nseq, nch),
        in_specs=_ssd_specs(nch),
        out_specs=[pl.BlockSpec((CHUNK, D_SSD), lambda s, c: (s * nch + c, 0)),
                   pl.BlockSpec((1, D_SSD, D_STATE), lambda s, c: (s * nch + c, 0, 0))],
        out_shape=[jax.ShapeDtypeStruct((t, D_SSD + D_GM), BF16),
                   jax.ShapeDtypeStruct((t // CHUNK, D_SSD, D_STATE), F32)],
        scratch_shapes=[pltpu.VMEM((D_SSD, D_STATE), F32)],
        compiler_params=_cp("arbitrary", "arbitrary"),
    )(xbc, proj, proj, dtb, alog, dskip, ng)


def _ssd_bwd(dy, xbc, proj, hprev, dtb, alog, dskip, ng, *, nseq):
    t = proj.shape[0]
    nch = t // nseq // CHUNK
    hd = PAIR_W
    rev = lambda s, c: s * nch + (nch - 1 - c)

    def body(dy_ref, xbc_ref, z_ref, dt_ref, hp_ref, dtb_ref, alog_ref, dsk_ref, ng_ref,
             dxbc_ref, dproj_ref, ddtb_ref, dalog_ref, ddsk_ref, dng_ref, dh_ref):
        first = (pl.program_id(0) == 0) & (pl.program_id(1) == 0)

        @pl.when(pl.program_id(1) == 0)
        def _():
            dh_ref[...] = jnp.zeros_like(dh_ref)

        @pl.when(first)
        def _():
            ddtb_ref[...] = jnp.zeros_like(ddtb_ref)
            dalog_ref[...] = jnp.zeros_like(dalog_ref)
            ddsk_ref[...] = jnp.zeros_like(ddsk_ref)
            dng_ref[...] = jnp.zeros_like(dng_ref)

        xs, bg, cg, dtr, z, ngs = _ssd_load(xbc_ref, z_ref, dt_ref, ng_ref)
        hp = [hp_ref[0, h * hd:(h + 1) * hd, :] for h in range(SSD_PAIRS)]
        _, vjp = jax.vjp(_ssd_chunk, xs, bg, cg, dtr, z, hp, dtb_ref[...], alog_ref[...], dsk_ref[...], ngs)
        douts = [dy_ref[:, h * hd:(h + 1) * hd] for h in range(SSD_PAIRS)]
        dhn = [dh_ref[h * hd:(h + 1) * hd, :] for h in range(SSD_PAIRS)]
        dxs, dbg, dcg, ddtr, dz, dhp, ddtb, dalog, ddsk, dngs = vjp((douts, dhn))
        dproj_ref[:, :COL_Z] = jnp.zeros((CHUNK, COL_Z), BF16)
        dproj_ref[:, COL_XBC:] = jnp.zeros((CHUNK, N_INP - COL_XBC), BF16)
        for h in range(SSD_PAIRS):
            dxbc_ref[:, h * hd:(h + 1) * hd] = dxs[h]
            dproj_ref[:, COL_Z + h * hd: COL_Z + (h + 1) * hd] = _b(dz[h])
            dh_ref[h * hd:(h + 1) * hd, :] = dhp[h]
            dng_ref[:, h * hd:(h + 1) * hd] += dngs[h]
        for g in range(SSD_GROUPS):
            dxbc_ref[:, D_SSD + g * D_STATE: D_SSD + (g + 1) * D_STATE] = dbg[g]
            dxbc_ref[:, D_SSD + (SSD_GROUPS + g) * D_STATE: D_SSD + (SSD_GROUPS + g + 1) * D_STATE] = dcg[g]
        dproj_ref[:, COL_DT:COL_DT + SSD_HEADS] = _b(ddtr)
        ddtb_ref[...] += ddtb
        dalog_ref[...] += dalog
        ddsk_ref[...] += ddsk

    small = pl.BlockSpec((1, SSD_HEADS), lambda s, c: (0, 0))
    return pl.pallas_call(
        body, name="ssd_bwd", grid=(nseq, nch),
        in_specs=[pl.BlockSpec((CHUNK, D_SSD), lambda s, c: (rev(s, c), 0)),
                  pl.BlockSpec((CHUNK, CONV_DIM), lambda s, c: (rev(s, c), 0)),
                  pl.BlockSpec((CHUNK, D_SSD), lambda s, c: (rev(s, c), COL_Z // D_SSD)),
                  pl.BlockSpec((CHUNK, 128), lambda s, c: (rev(s, c), COL_DT // 128)),
                  pl.BlockSpec((1, D_SSD, D_STATE), lambda s, c: (rev(s, c), 0, 0)),
                  small, small, small,
                  pl.BlockSpec((1, D_SSD), lambda s, c: (0, 0))],
        out_specs=[pl.BlockSpec((CHUNK, CONV_DIM), lambda s, c: (rev(s, c), 0)),
                   pl.BlockSpec((CHUNK, N_INP), lambda s, c: (rev(s, c), 0)),
                   small, small, small,
                   pl.BlockSpec((1, D_SSD), lambda s, c: (0, 0))],
        out_shape=[jax.ShapeDtypeStruct((t, CONV_DIM), F32), jax.ShapeDtypeStruct((t, N_INP), BF16),
                   jax.ShapeDtypeStruct((1, SSD_HEADS), F32), jax.ShapeDtypeStruct((1, SSD_HEADS), F32),
                   jax.ShapeDtypeStruct((1, SSD_HEADS), F32), jax.ShapeDtypeStruct((1, D_SSD), F32)],
        scratch_shapes=[pltpu.VMEM((D_SSD, D_STATE), F32)],
        compiler_params=_cp("arbitrary", "arbitrary"),
    )(dy, xbc, proj, proj, hprev, dtb, alog, dskip, ng)


def _gmlp_chunk(gu, gv, ws, bs_cols, vg, og):
    n = gu[0].shape[0]
    mask = _tri(n, True)
    au = [_gelu(t) for t in gu]
    av = [_gelu(t) for t in gv]
    r = lax.rsqrt(sum(jnp.sum(t * t, axis=1, keepdims=True) for t in av) * (1.0 / D_GM) + EPS)
    p = []
    for h in range(GM_HEADS):
        sv = _bdot(ws[h] * mask, av[h] * r * vg[h]) + bs_cols[h]
        p.append(au[h] * sv)
    r2 = lax.rsqrt(sum(jnp.sum(t * t, axis=1, keepdims=True) for t in p) * (1.0 / D_GM) + EPS)
    return [p[h] * r2 * og[h] for h in range(GM_HEADS)]


def _gmlp_load(u_ref, v_ref, ws_ref, bst_ref, vg_ref, og_ref):
    gu = _hslices(u_ref, GM_HEAD_DIM, GM_HEADS)
    gv = _hslices(v_ref, GM_HEAD_DIM, GM_HEADS)
    ws = [ws_ref[h] for h in range(GM_HEADS)]
    bs_cols = [bst_ref[:, h:h + 1] for h in range(GM_HEADS)]
    return gu, gv, ws, bs_cols, _hslices(vg_ref, GM_HEAD_DIM, GM_HEADS), _hslices(og_ref, GM_HEAD_DIM, GM_HEADS)


def _gmlp_specs():
    return [pl.BlockSpec((CHUNK, D_GM), lambda i: (i, COL_U // D_GM)),
            pl.BlockSpec((CHUNK, D_GM), lambda i: (i, COL_V // D_GM)),
            pl.BlockSpec((GM_HEADS, CHUNK, CHUNK), lambda i: (0, 0, 0)),
            pl.BlockSpec((CHUNK, GM_HEADS), lambda i: (0, 0)),
            pl.BlockSpec((1, D_GM), lambda i: (0, 0)),
            pl.BlockSpec((1, D_GM), lambda i: (0, 0))]


def _gmlp_fwd(proj, ycat, ws, bst, vg, og):
    t = proj.shape[0]

    def body(u_ref, v_ref, ws_ref, bst_ref, vg_ref, og_ref, ycat_ref, o_ref):
        del ycat_ref
        outs = _gmlp_chunk(*_gmlp_load(u_ref, v_ref, ws_ref, bst_ref, vg_ref, og_ref))
        for h in range(GM_HEADS):
            o_ref[:, h * GM_HEAD_DIM:(h + 1) * GM_HEAD_DIM] = _b(outs[h])

    return pl.pallas_call(
        body, name="gmlp_fwd", grid=(t // CHUNK,),
        in_specs=_gmlp_specs() + [ANY],
        out_specs=pl.BlockSpec((CHUNK, D_GM), lambda i: (i, D_SSD // D_GM)),
        out_shape=jax.ShapeDtypeStruct(ycat.shape, ycat.dtype),
        input_output_aliases={6: 0},
        compiler_params=_cp("parallel"),
    )(proj, proj, ws, bst, vg, og, ycat)


def _gmlp_bwd(dy, proj, ws, bst, vg, og, dproj):
    t = proj.shape[0]
    w = GM_HEAD_DIM

    def body(dy_ref, u_ref, v_ref, ws_ref, bst_ref, vg_ref, og_ref, dproj_ref,
             dgm_ref, dws_ref, dbst_ref, dvg_ref, dog_ref):
        del dproj_ref

        @pl.when(pl.program_id(0) == 0)
        def _():
            dws_ref[...] = jnp.zeros_like(dws_ref)
            dbst_ref[...] = jnp.zeros_like(dbst_ref)
            dvg_ref[...] = jnp.zeros_like(dvg_ref)
            dog_ref[...] = jnp.zeros_like(dog_ref)

        _, vjp = jax.vjp(_gmlp_chunk, *_gmlp_load(u_ref, v_ref, ws_ref, bst_ref, vg_ref, og_ref))
        dgu, dgv, dws, dbs, dvg, dog = vjp(_hslices(dy_ref, w, GM_HEADS))
        for h in range(GM_HEADS):
            dgm_ref[:, h * w:(h + 1) * w] = _b(dgu[h])
            dgm_ref[:, D_GM + h * w: D_GM + (h + 1) * w] = _b(dgv[h])
            dws_ref[h] += dws[h]
            dbst_ref[:, h:h + 1] += dbs[h]
            dvg_ref[:, h * w:(h + 1) * w] += dvg[h]
            dog_ref[:, h * w:(h + 1) * w] += dog[h]

    return pl.pallas_call(
        body, name="gmlp_bwd", grid=(t // CHUNK,),
        in_specs=[pl.BlockSpec((CHUNK, D_GM), lambda i: (i, 1))] + _gmlp_specs() + [ANY],
        out_specs=[pl.BlockSpec((CHUNK, 2 * D_GM), lambda i: (i, COL_U // (2 * D_GM))),
                   pl.BlockSpec((GM_HEADS, CHUNK, CHUNK), lambda i: (0, 0, 0)),
                   pl.BlockSpec((CHUNK, GM_HEADS), lambda i: (0, 0)),
                   pl.BlockSpec((1, D_GM), lambda i: (0, 0)),
                   pl.BlockSpec((1, D_GM), lambda i: (0, 0))],
        out_shape=[jax.ShapeDtypeStruct(dproj.shape, dproj.dtype), jax.ShapeDtypeStruct((GM_HEADS, CHUNK, CHUNK), F32),
                   jax.ShapeDtypeStruct((CHUNK, GM_HEADS), F32), jax.ShapeDtypeStruct((1, D_GM), F32),
                   jax.ShapeDtypeStruct((1, D_GM), F32)],
        input_output_aliases={7: 0},
        compiler_params=_cp("arbitrary"),
    )(dy, proj, proj, ws, bst, vg, og, dproj)


def _local_step(x, target, mods, lw, final_g, *, nseq, big_w, grad_sink, small_sink):
    saved = []
    xin, delta, gate = x, None, None
    for l in range(DEPTH):
        w = lw[l]
        sh1, sc1, g1, sh2, sc2, g2 = mods[l]
        x0, h1 = _normmod_fwd(xin, delta, gate, w["norm1_g"], sc1, sh1, nseq=nseq, name=f"norm1_fwd_{l}")
        w_in = big_w(l, "w_in", h1)
        proj = _matmul(h1, w_in, tb=True, name=f"mm_in_{l}")
        xbc, xbc_pre = _ssd_conv_fwd(proj, w["ssd_conv_w"], w["ssd_conv_b"], nseq=nseq)
        ycat, hprev = _ssd_fwd(xbc, proj, w["ssd_dt_bias"], w["ssd_a_log"], w["ssd_d"], w["ssd_norm_g"], nseq=nseq)
        ycat = _gmlp_fwd(proj, ycat, w["gm_ws"], w["gm_bst"], w["gm_vnorm_g"], w["gm_out_g"])
        w_out = big_w(l, "w_out", ycat)
        mix = _matmul(ycat, w_out, name=f"mm_out_{l}")
        x1, h2 = _normmod_fwd(x0, mix, g1, w["norm2_g"], sc2, sh2, nseq=nseq, name=f"norm2_fwd_{l}")
        ff_up = big_w(l, "ff_up", h2)
        up = _matmul(h2, ff_up, tb=True, name=f"mm_up_{l}", out_dtype=BF16)
        act = _ffn_act_fwd(up, w["ff_conv_w"], w["ff_conv_b"], nseq=nseq)
        ff_down = big_w(l, "ff_down", act)
        dn = _matmul(act, ff_down, name=f"mm_down_{l}")
        saved.append(dict(x0=x0, xin_delta=delta, xin_gate=gate, h1=h1, proj=proj, xbc=xbc, xbc_pre=xbc_pre, hprev=hprev,
                          ycat=ycat, mix=mix, x1=x1, h2=h2, up=up, act=act, dn=dn,
                          w_in=w_in, w_out=w_out, ff_up=ff_up, ff_down=ff_down))
        xin, delta, gate = x1, dn, g2

    loss, dx, ddelta, dgate, dfg = _final_loss(xin, delta, gate, final_g, target, nseq=nseq)

    small, dmods = [None] * DEPTH, [None] * DEPTH
    for l in reversed(range(DEPTH)):
        w, sv = lw[l], saved[l]
        sh1, sc1, g1, sh2, sc2, g2 = mods[l]
        dg2 = dgate
        g_ff_down = _matmul(sv["act"], ddelta, ta=True, name=f"mm_down_dw_{l}", out_dtype=BF16)
        dact = _matmul(ddelta, sv["ff_down"], tb=True, name=f"mm_down_dx_{l}", out_dtype=BF16)
        dgate_ff, dval_ff, dfcw, dfcb = _ffn_act_bwd(dact, sv["up"], w["ff_conv_w"], w["ff_conv_b"], nseq=nseq)
        g_ff_up = _matmul([dgate_ff, dval_ff], sv["h2"], ta=True, name=f"mm_up_dw_{l}", out_dtype=BF16)
        dep = grad_sink(l, "ffn", dict(ff_down=g_ff_down, ff_up=g_ff_up), dval_ff)
        dh2 = _matmul([dgate_ff, dval_ff], sv["ff_up"], name=f"mm_up_dx_{l}", dep=dep)
        dx, dmix, dg1, dn2g, dsc2, dsh2 = _normmod_bwd(dh2, dx, sv["x1"], sv["mix"], g1, w["norm2_g"], sc2,
                                                       nseq=nseq, name=f"norm2_bwd_{l}")
        g_w_out = _matmul(sv["ycat"], dmix, ta=True, name=f"mm_out_dw_{l}", out_dtype=BF16)
        dep = grad_sink(l, "w_out", dict(w_out=g_w_out), dmix)
        dycat = _matmul(dmix, sv["w_out"], tb=True, name=f"mm_out_dx_{l}", dep=dep)
        dxbc_act, dproj, ddtb, dalog, ddsk, dng = _ssd_bwd(dycat, sv["xbc"], sv["proj"], sv["hprev"], w["ssd_dt_bias"],
                                                          w["ssd_a_log"], w["ssd_d"], w["ssd_norm_g"], nseq=nseq)
        dproj, dscw, dscb = _ssd_conv_bwd(dxbc_act, sv["xbc_pre"], sv["proj"], w["ssd_conv_w"], dproj, nseq=nseq)
        dproj, dws, dbst, dvg, dog = _gmlp_bwd(dycat, sv["proj"], w["gm_ws"], w["gm_bst"], w["gm_vnorm_g"], w["gm_out_g"], dproj)
        early = dict(norm2_g=dn2g, ssd_norm_g=dng, gm_vnorm_g=dvg, gm_out_g=dog,
                     ssd_conv_w=dscw, ssd_conv_b=dscb, ff_conv_w=dfcw, ff_conv_b=dfcb,
                     ssd_dt_bias=ddtb, ssd_a_log=dalog, ssd_d=ddsk, gm_ws=dws, gm_bs=dbst.T)
        dep = small_sink(l, early, small, dmods, dfg, loss)
        g_w_in = _matmul(dproj, sv["h1"], ta=True, name=f"mm_in_dw_{l}", out_dtype=BF16, dep=dep)
        dep = grad_sink(l, "w_in", dict(w_in=g_w_in), dproj)
        dh1 = _matmul(dproj, sv["w_in"], name=f"mm_in_dx_{l}", dep=dep)
        dx, ddelta, dgate, dn1g, dsc1, dsh1 = _normmod_bwd(dh1, dx, sv["x0"], sv["xin_delta"], sv["xin_gate"],
                                                           w["norm1_g"], sc1, nseq=nseq, name=f"norm1_bwd_{l}")
        small[l] = dict(early, norm1_g=dn1g)
        dmods[l] = jnp.concatenate([dsh1, dsc1, dg1, dsh2, dsc2, dg2], axis=-1)[:, 0, :]
    return dx, small, dmods


def _all_gather(arrs, name, dep=None):
    n = len(arrs)
    extra = [] if dep is None else [dep]

    def body(*refs):
        ins, outs = refs[:n], refs[n + len(extra):2 * n + len(extra)]
        send_sems, recv_sems, local_sems = refs[2 * n + len(extra):]
        x, y, c = lax.axis_index("x"), lax.axis_index("y"), lax.axis_index("c")
        me, sibling = (x, y, c), (x, y, 1 - c)
        chips = [(1 - x, y), (x, 1 - y), (1 - x, 1 - y)]

        def copy(i, k, block, to, src=None):
            px, py, pc = block
            dst = outs[i].at[4 * px + 2 * py + pc]
            return pltpu.make_async_remote_copy(
                src_ref=dst if src is None else src, dst_ref=dst,
                send_sem=send_sems.at[7 * i + k], recv_sem=recv_sems.at[7 * i + k],
                device_id=to, device_id_type=MESH)

        mine = [pltpu.make_async_copy(ins[i], outs[i].at[4 * x + 2 * y + c], local_sems.at[i]) for i in range(n)]
        for cp in mine:
            cp.start()
        first = []
        for i in range(n):
            first.append(copy(i, 0, me, sibling, src=ins[i]))
            first += [copy(i, 1 + j, me, (*chip, c), src=ins[i]) for j, chip in enumerate(chips)]
        for cp in first:
            cp.start()
        passed = []
        for j, chip in enumerate(chips):
            for i in range(n):
                copy(i, 1 + j, (*chip, c), me).wait_recv()
                fwd = copy(i, 4 + j, (*chip, c), sibling)
                fwd.start()
                passed.append(fwd)
        for i in range(n):
            copy(i, 0, sibling, me).wait_recv()
            for j, chip in enumerate(chips):
                copy(i, 4 + j, (*chip, 1 - c), me).wait_recv()
        for cp in first + passed:
            cp.wait_send()
        for cp in mine:
            cp.wait()

    return pl.pallas_call(
        body, name=name,
        in_specs=[ANY] * (n + len(extra)), out_specs=[ANY] * n,
        out_shape=[jax.ShapeDtypeStruct((N_DEV,) + a.shape, a.dtype) for a in arrs],
        scratch_shapes=[pltpu.SemaphoreType.DMA((7 * n,)), pltpu.SemaphoreType.DMA((7 * n,)),
                        pltpu.SemaphoreType.DMA((n,))],
    )(*arrs, *extra)


def _exchange_sibling(arrs, name):
    n = len(arrs)

    def body(*refs):
        ins, outs = refs[:n], refs[n:2 * n]
        send_sems, recv_sems = refs[2 * n:]
        x, y, c = lax.axis_index("x"), lax.axis_index("y"), lax.axis_index("c")
        copies = []
        for i in range(n):
            for k in range(4):
                copies.append(pltpu.make_async_remote_copy(
                    src_ref=ins[i].at[2 * k + (1 - c)], dst_ref=outs[i].at[k],
                    send_sem=send_sems.at[4 * i + k], recv_sem=recv_sems.at[4 * i + k],
                    device_id=(x, y, 1 - c), device_id_type=MESH))
        for cp in copies:
            cp.start()
        for cp in copies:
            cp.wait_recv()
        for cp in copies:
            cp.wait_send()

    return pl.pallas_call(
        body, name=name,
        in_specs=[ANY] * n, out_specs=[ANY] * n,
        out_shape=[jax.ShapeDtypeStruct((4,) + a.shape[1:], a.dtype) for a in arrs],
        scratch_shapes=[pltpu.SemaphoreType.DMA((4 * n,)), pltpu.SemaphoreType.DMA((4 * n,))],
    )(*arrs)


def _exchange_chips(arrs, name):
    n = len(arrs)

    def body(*refs):
        ins, outs = refs[:n], refs[n:2 * n]
        send_sems, recv_sems = refs[2 * n:]
        x, y, c = lax.axis_index("x"), lax.axis_index("y"), lax.axis_index("c")
        chips = [(1 - x, y), (x, 1 - y), (1 - x, 1 - y)]
        copies = []
        for i in range(n):
            for j, (cx, cy) in enumerate(chips):
                copies.append(pltpu.make_async_remote_copy(
                    src_ref=ins[i].at[2 * cx + cy], dst_ref=outs[i].at[j],
                    send_sem=send_sems.at[3 * i + j], recv_sem=recv_sems.at[3 * i + j],
                    device_id=(cx, cy, c), device_id_type=MESH))
        for cp in copies:
            cp.start()
        for cp in copies:
            cp.wait_recv()
        for cp in copies:
            cp.wait_send()

    return pl.pallas_call(
        body, name=name,
        in_specs=[ANY] * n, out_specs=[ANY] * n,
        out_shape=[jax.ShapeDtypeStruct((3,) + a.shape[1:], a.dtype) for a in arrs],
        scratch_shapes=[pltpu.SemaphoreType.DMA((3 * n,)), pltpu.SemaphoreType.DMA((3 * n,))],
    )(*arrs)


def _add_sibling(a, r, pos, name):
    _, depth, rows, cols = a.shape
    tr = _tile(rows, 256) if rows % 8 == 0 else rows
    a3 = a.reshape(N_DEV * depth, rows, cols)
    r3 = r.reshape(4 * depth, rows, cols)

    def body(pos_ref, a_ref, r_ref, o_ref):
        o_ref[...] = a_ref[...] + r_ref[...]

    out = pl.pallas_call(
        body, name=name,
        grid_spec=pltpu.PrefetchScalarGridSpec(
            num_scalar_prefetch=1, grid=(4 * depth, rows // tr),
            in_specs=[pl.BlockSpec((1, tr, cols), lambda q, i, p: ((2 * (q // depth) + p[0]) * depth + q % depth, i, 0)),
                      pl.BlockSpec((1, tr, cols), lambda q, i, p: (q, i, 0))],
            out_specs=pl.BlockSpec((1, tr, cols), lambda q, i, p: (q, i, 0))),
        out_shape=jax.ShapeDtypeStruct((4 * depth, rows, cols), F32),
        compiler_params=_cp("parallel", "parallel"),
    )(pos, a3, r3)
    return out.reshape(4, depth, rows, cols)


HBM = pl.BlockSpec(memory_space=pltpu.HBM)
SEM = pl.BlockSpec(memory_space=pltpu.SEMAPHORE)
EFFECT = pltpu.SideEffectType.DATAFLOW_SIDE_EFFECTING


def _peer(k):
    x, y, c = lax.axis_index("x"), lax.axis_index("y"), lax.axis_index("c")
    return (1 - x if k & 4 else x, 1 - y if k & 2 else y, 1 - c if k & 1 else c)


ALL_PEERS = tuple(range(1, N_DEV))
OTHER_CHIPS = (2, 4, 6)


def _xc_copies(scatter, srcs, lands, send_sems, recv_sems, peers=ALL_PEERS):
    x, y, c = lax.axis_index("x"), lax.axis_index("y"), lax.axis_index("c")
    copies = []
    for i in range(len(srcs)):
        for k in peers:
            px, py, pc = _peer(k)
            src = srcs[i].at[4 * px + 2 * py + pc] if scatter else srcs[i]
            dst = lands[i].at[k - 1] if scatter else lands[i].at[4 * x + 2 * y + c]
            copies.append(pltpu.make_async_remote_copy(
                src_ref=src, dst_ref=dst, send_sem=send_sems[i].at[k - 1], recv_sem=recv_sems[i].at[k - 1],
                device_id=(px, py, pc), device_id_type=MESH))
    return copies


def _xc_start(scatter, arrs, after, name, peers=ALL_PEERS):
    n = len(arrs)
    lands = [lax.empty((N_DEV - 1,) + a.shape[1:] if scatter else (N_DEV,) + a.shape, a.dtype) for a in arrs]

    def body(*refs):
        srcs, lnd = refs[:n], refs[n:2 * n]
        send_sems, recv_sems = refs[2 * n + 1:3 * n + 1], refs[3 * n + 1:4 * n + 1]
        token = refs[6 * n + 1]
        for cp in _xc_copies(scatter, srcs, lnd, send_sems, recv_sems, peers):
            cp.start()
        token[...] = jnp.zeros_like(token)

    outs = pl.pallas_call(
        body, name=name,
        out_shape=[pltpu.SemaphoreType.DMA((N_DEV - 1,))] * (2 * n)
        + [pltpu.HBM(a.shape, a.dtype) for a in arrs] + [pltpu.HBM(a.shape, a.dtype) for a in lands]
        + [jax.ShapeDtypeStruct((8, 128), F32)],
        in_specs=[HBM] * (2 * n) + [ANY],
        out_specs=[SEM] * (2 * n) + [HBM] * (2 * n) + [pl.BlockSpec(memory_space=pltpu.VMEM)],
        input_output_aliases={i: 2 * n + i for i in range(2 * n)},
        compiler_params=pltpu.CompilerParams(has_side_effects=EFFECT),
    )(*[pltpu.with_memory_space_constraint(a, pltpu.HBM) for a in list(arrs) + lands], after)
    return outs[:n], outs[n:2 * n], outs[2 * n:3 * n], outs[3 * n:4 * n], outs[4 * n][0, 0]


def _xc_wait(scatter, send_sems, recv_sems, srcs, lands, after, name, peers=ALL_PEERS):
    n = len(srcs)

    def body(*refs):
        s_refs, l_refs = refs[:n], refs[n:2 * n]
        ss, rs = refs[2 * n:3 * n], refs[3 * n:4 * n]
        for cp in _xc_copies(scatter, s_refs, l_refs, ss, rs, peers):
            cp.wait_send()
            cp.wait_recv()

    outs = pl.pallas_call(
        body, name=name,
        out_shape=[pltpu.HBM(a.shape, a.dtype) for a in list(srcs) + list(lands)],
        in_specs=[HBM] * (2 * n) + [SEM] * (2 * n) + [ANY],
        out_specs=[HBM] * (2 * n),
        input_output_aliases={i: i for i in range(2 * n)},
        compiler_params=pltpu.CompilerParams(has_side_effects=EFFECT),
    )(*srcs, *lands, *send_sems, *recv_sems, after)
    return outs[:n], outs[n:]


def _sib_copies(zones, send_sems, recv_sems):
    x, y, c = lax.axis_index("x"), lax.axis_index("y"), lax.axis_index("c")
    copies = []
    for i in range(len(zones)):
        for q in range(N_DEV // 2):
            slot = zones[i].at[2 * q + c]
            copies.append(pltpu.make_async_remote_copy(
                src_ref=slot, dst_ref=slot, send_sem=send_sems[i].at[q], recv_sem=recv_sems[i].at[q],
                device_id=(x, y, 1 - c), device_id_type=MESH))
    return copies


def _sib_start(zones, name):
    n = len(zones)

    def body(*refs):
        for cp in _sib_copies(refs[:n], refs[n:2 * n], refs[2 * n:3 * n]):
            cp.start()

    outs = pl.pallas_call(
        body, name=name,
        out_shape=[pltpu.SemaphoreType.DMA((N_DEV // 2,))] * (2 * n) + [pltpu.HBM(a.shape, a.dtype) for a in zones],
        in_specs=[HBM] * n,
        out_specs=[SEM] * (2 * n) + [HBM] * n,
        input_output_aliases={i: 2 * n + i for i in range(n)},
        compiler_params=pltpu.CompilerParams(has_side_effects=EFFECT),
    )(*[pltpu.with_memory_space_constraint(a, pltpu.HBM) for a in zones])
    return outs[:n], outs[n:2 * n], outs[2 * n:]


def _sib_wait(send_sems, recv_sems, zones, name):
    n = len(zones)

    def body(*refs):
        for cp in _sib_copies(refs[:n], refs[n:2 * n], refs[2 * n:3 * n]):
            cp.wait_send()
            cp.wait_recv()

    return pl.pallas_call(
        body, name=name,
        out_shape=[pltpu.HBM(a.shape, a.dtype) for a in zones],
        in_specs=[HBM] * n + [SEM] * (2 * n),
        out_specs=[HBM] * n,
        input_output_aliases={i: i for i in range(n)},
        compiler_params=pltpu.CompilerParams(has_side_effects=EFFECT),
    )(*zones, *send_sems, *recv_sems)


def _adamw_math(w, g, m, v):
    m = ADAM_B1 * m + (1.0 - ADAM_B1) * g
    v = ADAM_B2 * v + (1.0 - ADAM_B2) * (g * g)
    m_hat = m / (1.0 - ADAM_B1 ** ADAM_STEP)
    v_hat = v / (1.0 - ADAM_B2 ** ADAM_STEP)
    delta = -ADAM_LR * (m_hat / (jnp.sqrt(v_hat) + ADAM_EPS) + ADAM_WD * w)
    return delta, m, v


def _adamw_sharded(parts, w, m, v, pos, name):
    depth, rows, cols = w.shape
    tr = _tile(rows, 256) if rows % 8 == 0 else rows
    npart = len(parts)

    def body(pos_ref, *refs):
        prefs = refs[:npart]
        w_ref, m_ref, v_ref, g_out, d_out, m_out, v_out = refs[npart:]
        g = prefs[0][...]
        for pr in prefs[1:]:
            g = g + pr[...]
        delta, mn, vn = _adamw_math(w_ref[...], g, m_ref[...], v_ref[...])
        g_out[...] = g
        d_out[...] = delta
        m_out[...] = mn
        v_out[...] = vn

    def part_spec(fn):
        return pl.BlockSpec((1, tr, cols), lambda l, i, p: (fn(p) * depth + l, i, 0))

    blk = pl.BlockSpec((1, tr, cols), lambda l, i, p: (l, i, 0))
    shp = jax.ShapeDtypeStruct((depth, rows, cols), F32)
    return pl.pallas_call(
        body, name=name,
        grid_spec=pltpu.PrefetchScalarGridSpec(
            num_scalar_prefetch=1, grid=(depth, rows // tr),
            in_specs=[part_spec(fn) for _, fn in parts] + [blk, blk, blk],
            out_specs=[blk, blk, blk, blk]),
        out_shape=[shp, shp, shp, shp],
        compiler_params=_cp("parallel", "parallel"),
    )(pos, *[a for a, _ in parts], w, m, v)


def _adamw_layer(parts, w, m, v, pos, layer, prev, name):
    depth, rows, cols = w.shape
    npart = len(parts)
    nprev = 0 if prev is None else 4
    if rows % 16 == 0:
        tr, tc = max(t for t in range(16, 257, 16) if rows % t == 0), cols
    else:
        tr, tc = rows, _tile(cols, 256)
    pick = (lambda i: (i, 0)) if rows % 16 == 0 else (lambda i: (0, i))

    def body(pos_ref, *refs):
        prefs = refs[:npart]
        w_ref, m_ref, v_ref = refs[npart:npart + 3]
        g_out, d_out, m_out, v_out = refs[npart + 3 + nprev:]
        g = prefs[0][...].astype(F32)
        for pr in prefs[1:]:
            g = g + pr[...].astype(F32)
        delta, mn, vn = _adamw_math(w_ref[...], g, m_ref[...], v_ref[...])
        g_out[...] = g
        d_out[...] = delta
        m_out[...] = mn
        v_out[...] = vn

    def part_spec(fn):
        return pl.BlockSpec((1, tr, tc), lambda i, p: (fn(p), *pick(i)))

    blk = pl.BlockSpec((1, tr, tc), lambda i, p: (layer, *pick(i)))
    shp = jax.ShapeDtypeStruct((depth, rows, cols), F32)
    first_prev = 1 + npart + 3
    return pl.pallas_call(
        body, name=name,
        grid_spec=pltpu.PrefetchScalarGridSpec(
            num_scalar_prefetch=1, grid=(rows // tr * (cols // tc),),
            in_specs=[part_spec(fn) for _, fn in parts] + [blk, blk, blk] + [ANY] * nprev,
            out_specs=[blk, blk, blk, blk]),
        out_shape=[shp, shp, shp, shp],
        input_output_aliases={first_prev + j: j for j in range(nprev)},
        compiler_params=_cp("parallel"),
    )(pos, *[a for a, _ in parts], w, m, v, *(prev or ()))


_P1024 = ["norm1_g", "norm2_g", "ssd_norm_g", "gm_vnorm_g", "gm_out_g"]
_P16 = ["ssd_dt_bias", "ssd_a_log", "ssd_d"]


def _adamw_small(gath, wmv):
    names = list(wmv.keys())
    classes = list(gath.keys())
    flat_in = [gath[k] for k in classes]
    for nme in names:
        flat_in += list(wmv[nme])
    out_shapes = []
    for nme in names:
        out_shapes += [jax.ShapeDtypeStruct(wmv[nme][0].shape, F32)] * 4
    out_shapes += [jax.ShapeDtypeStruct((DEPTH, SSD_CONV, CONV_DIM), F32), jax.ShapeDtypeStruct((DEPTH, FF_CONV, D_FF), F32),
                   jax.ShapeDtypeStruct((1, SSD_HEADS), F32)]
    scratch = [pltpu.VMEM(gath[k].shape[1:], F32) for k in classes]
    ncls = len(classes)

    def body(*refs):
        g_refs = dict(zip(classes, refs[:ncls]))
        pos = ncls
        w_refs = {}
        for nme in names:
            w_refs[nme] = refs[pos:pos + 3]
            pos += 3
        o_refs = {}
        for nme in names:
            o_refs[nme] = refs[pos:pos + 4]
            pos += 4
        scw_out, fcw_out, loss_out = refs[pos], refs[pos + 1], refs[pos + 2]
        s_refs = dict(zip(classes, refs[pos + 3:]))
        for k in classes:
            acc = g_refs[k][0]
            for dev in range(1, N_DEV):
                acc = acc + g_refs[k][dev]
            s_refs[k][...] = acc

        def apply(nme, grad_of):
            w_ref, m_ref, v_ref = w_refs[nme]
            g_out, d_out, m_out, v_out = o_refs[nme]
            shape = w_ref.shape
            if len(shape) == 2:
                idxs = [(slice(l, l + 1),) for l in range(shape[0])]
            elif len(shape) == 3:
                idxs = [(l,) for l in range(shape[0])]
            else:
                idxs = [(l, h) for l in range(shape[0]) for h in range(shape[1])]
            for n_i, ix in enumerate(idxs):
                g = grad_of(n_i)
                delta, mn, vn = _adamw_math(w_ref[ix], g, m_ref[ix], v_ref[ix])
                g_out[ix] = g
                d_out[ix] = delta
                m_out[ix] = mn
                v_out[ix] = vn

        s1024, s1536, s2816, s16, s128, s6144, late1024, late6144 = (s_refs[k] for k in classes)
        s1024[0:1, :] += late1024[...]
        s6144[0:late6144.shape[0], :] += late6144[...]
        for n_i, nme in enumerate(_P1024):
            apply(nme, lambda l, b=2 * n_i: s1024[b + l:b + l + 1, :])
        apply("final_g", lambda l: s1024[10:11, :])
        apply("ssd_conv_b", lambda l: s1536[8 + l:9 + l, :])
        apply("ff_conv_b", lambda l: s2816[6 + l:7 + l, :])
        for n_i, nme in enumerate(_P16):
            apply(nme, lambda l, b=2 * n_i: s16[b + l:b + l + 1, :])
        apply("gm_ws", lambda q: s128[q * CHUNK:(q + 1) * CHUNK, :])
        apply("gm_bs", lambda l: s128[2048 + 8 * l:2048 + 8 * (l + 1), :])
        apply("ada_b", lambda l: s6144[2 * l:2 * l + 1, :] + s6144[2 * l + 1:2 * l + 2, :])
        for l in range(DEPTH):
            scw_out[l] = s1536[SSD_CONV * l:SSD_CONV * (l + 1), :]
            fcw_out[l] = s2816[FF_CONV * l:FF_CONV * (l + 1), :]
        loss_out[...] = s16[2 * len(_P16):2 * len(_P16) + 1, :]

    outs = pl.pallas_call(
        body, name="adamw_small",
        out_shape=out_shapes,
        scratch_shapes=scratch,
        compiler_params=pltpu.CompilerParams(vmem_limit_bytes=VMEM_LIMIT),
    )(*flat_in)
    res = {nme: tuple(outs[4 * i:4 * i + 4]) for i, nme in enumerate(names)}
    return res, outs[-3], outs[-2], outs[-1]


_WEIGHTS = ['ada_w', 'ada_b', 'norm1_g', 'norm2_g', 'w_in', 'ssd_conv_w', 'ssd_conv_b', 'ssd_dt_bias', 'ssd_a_log',
            'ssd_d', 'ssd_norm_g', 'gm_vnorm_g', 'gm_ws', 'gm_bs', 'gm_out_g', 'w_out', 'ff_up', 'ff_conv_w',
            'ff_conv_b', 'ff_down', 'final_g']


_O_XBC, _O_DT, _O_GM = D_SSD, D_SSD + CONV_DIM, D_SSD + CONV_DIM + SSD_HEADS


_TRANSPOSED = ("w_in", "ff_up")


def _full_weight(name, g):
    full = g.reshape(g.shape[0] * g.shape[1], g.shape[2])
    if name != "w_in":
        return full
    zpad = jnp.zeros((N_INP - N_IN, full.shape[1]), full.dtype)
    return jnp.concatenate([full[_O_GM:], full[:_O_XBC], full[_O_XBC:_O_DT], full[_O_DT:_O_GM], zpad], axis=0)


def _by_owner(name, grad):
    if name == "w_in":
        grad = jnp.concatenate([grad[COL_Z:COL_XBC], grad[COL_XBC:COL_DT], grad[COL_DT:COL_DT + SSD_HEADS], grad[:COL_Z]], axis=0)
    return grad.reshape(N_DEV, grad.shape[0] // N_DEV, grad.shape[1])


def kernel(x, c, ada_w, ada_b, norm1_g, norm2_g, w_in, ssd_conv_w, ssd_conv_b, ssd_dt_bias, ssd_a_log, ssd_d, ssd_norm_g, gm_vnorm_g, gm_ws, gm_bs, gm_out_g, w_out, ff_up, ff_conv_w, ff_conv_b, ff_down, final_g, loss_target, m_ada_w, m_ada_b, m_norm1_g, m_norm2_g, m_w_in, m_ssd_conv_w, m_ssd_conv_b, m_ssd_dt_bias, m_ssd_a_log, m_ssd_d, m_ssd_norm_g, m_gm_vnorm_g, m_gm_ws, m_gm_bs, m_gm_out_g, m_w_out, m_ff_up, m_ff_conv_w, m_ff_conv_b, m_ff_down, m_final_g, v_ada_w, v_ada_b, v_norm1_g, v_norm2_g, v_w_in, v_ssd_conv_w, v_ssd_conv_b, v_ssd_dt_bias, v_ssd_a_log, v_ssd_d, v_ssd_norm_g, v_gm_vnorm_g, v_gm_ws, v_gm_bs, v_gm_out_g, v_w_out, v_ff_up, v_ff_conv_w, v_ff_conv_b, v_ff_down, v_final_g):
    given = dict(locals())
    wts = {n: given[n] for n in _WEIGHTS}
    mom = {n: given["m_" + n] for n in _WEIGHTS}
    var = {n: given["v_" + n] for n in _WEIGHTS}
    nseq, seq, d = x.shape
    ix, iy, ic = lax.axis_index("x"), lax.axis_index("y"), lax.axis_index("c")
    me = 4 * ix + 2 * iy + ic
    me_arr = me.astype(jnp.int32).reshape(1)

    for nme in _TRANSPOSED:
        wts[nme], mom[nme], var[nme] = (jnp.transpose(a, (0, 2, 1)) for a in (wts[nme], mom[nme], var[nme]))

    def shard(l, name):
        return _b(wts[name][l])

    g_scw, g_fcw, c_all = _all_gather([ssd_conv_w, ff_conv_w, c], "gather_first")
    scw_f = jnp.transpose(g_scw, (1, 2, 0, 3)).reshape(DEPTH, SSD_CONV, CONV_DIM)
    fcw_f = jnp.transpose(g_fcw, (1, 2, 0, 3)).reshape(DEPTH, FF_CONV, D_FF)
    c_all = c_all.reshape(N_DEV * nseq, d)

    n_ada = ada_w.shape[2]
    ada_b_shard = lax.dynamic_slice_in_dim(ada_b, me * n_ada, n_ada, axis=1).reshape(DEPTH, 1, n_ada)
    mod_part, c_act = _ada_fwd(c_all, ada_w, ada_b_shard)
    (mod_g,) = _all_gather([mod_part], "gather_mod")
    mod_all = jnp.transpose(mod_g, (1, 2, 0, 3)).reshape(DEPTH, N_DEV * nseq, N_MOD * d)
    mod_mine = lax.dynamic_slice_in_dim(mod_all, me * nseq, nseq, axis=1)
    mods = [[mod_mine[l, :, k * d:(k + 1) * d].reshape(nseq, 1, d) for k in range(N_MOD)] for l in range(DEPTH)]

    first_ssem, first_rsem, first_src, first_land, first_zero = _xc_start(
        False, [shard(0, "w_in")], mod_g, "ag_first_start", peers=OTHER_CHIPS)
    later = [(0, "w_out"), (0, "ff_up"), (0, "ff_down"), (1, "w_in"), (1, "w_out"), (1, "ff_up"), (1, "ff_down")]
    ag_ssem, ag_rsem, ag_src, ag_land, ag_zero = _xc_start(
        False, [shard(l, n) for l, n in later], first_zero.reshape(1, 1), "ag_start")
    ag_groups = {(0, "w_out"): [0], (0, "ff_up"): [1, 2], (1, "w_in"): [3, 4], (1, "ff_up"): [5, 6]}
    big_cache = {}

    def big_w(l, name, after):
        if (l, name) == (0, "w_in") and (l, name) not in big_cache:
            srcs, lands = _xc_wait(False, first_ssem, first_rsem, first_src, first_land, after, "ag_first_wait",
                                   peers=OTHER_CHIPS)
            zone = lax.dynamic_update_index_in_dim(lands[0], srcs[0], me, 0)
            (zone,) = _sib_wait(*_sib_start([zone], "ag_first_sib_start"), "ag_first_sib_wait")
            big_cache[(l, name)] = _full_weight(name, zone)
        if (l, name) not in big_cache:
            idx = ag_groups[(l, name)]
            pick = lambda seq_: [seq_[i] for i in idx]
            srcs, lands = _xc_wait(False, pick(ag_ssem), pick(ag_rsem), pick(ag_src), pick(ag_land), after,
                                   f"ag_wait_{l}_{name}")
            for i, src, land in zip(idx, srcs, lands):
                big_cache[later[i]] = _full_weight(later[i][1], lax.dynamic_update_index_in_dim(land, src, me, 0))
        return big_cache[(l, name)]

    lw = []
    for l in range(DEPTH):
        lw.append(dict(
            norm1_g=norm1_g[l:l + 1] + (ag_zero if l == 0 else 0.0), norm2_g=norm2_g[l:l + 1], ssd_conv_w=scw_f[l],
            ssd_conv_b=ssd_conv_b[l:l + 1], ssd_dt_bias=ssd_dt_bias[l:l + 1], ssd_a_log=ssd_a_log[l:l + 1],
            ssd_d=ssd_d[l:l + 1], ssd_norm_g=ssd_norm_g[l:l + 1], gm_vnorm_g=gm_vnorm_g[l:l + 1], gm_ws=gm_ws[l],
            gm_bst=gm_bs[l].T, gm_out_g=gm_out_g[l:l + 1], ff_conv_w=fcw_f[l], ff_conv_b=ff_conv_b[l:l + 1]))

    outs = {}
    pending = {}

    def rs_finish(l, group, after):
        names, ssem, rsem, srcs, lands = pending.pop((l, group))
        srcs, lands = _xc_wait(True, ssem, rsem, srcs, lands, after, f"rs_wait_{l}_{group}")
        for nme, own, land in zip(names, srcs, lands):
            parts = [(own, lambda p: p[0])] + [(land, lambda p, k=k: k) for k in range(N_DEV - 1)]
            outs[nme] = _adamw_layer(parts, wts[nme], mom[nme], var[nme], me_arr, l, outs.get(nme), f"adamw_{nme}_{l}")
        return outs[names[-1]][0]

    def grad_sink(l, group, grads, after):
        names = list(grads)
        ssem, rsem, srcs, lands, zero = _xc_start(True, [_by_owner(n, grads[n]) for n in names], after, f"rs_start_{l}_{group}")
        pending[(l, group)] = (names, ssem, rsem, srcs, lands)
        return zero.reshape(1, 1)

    early_gather = {}

    def small_sink(l, early, small, dmods, dfg, loss_p):
        if l > 0:
            return None
        layers = [dict(early, norm1_g=jnp.zeros((1, d), F32))] + small[1:]
        rows = lambda name: [layers[k][name] for k in range(DEPTH)]
        packed = [
            jnp.concatenate(sum([rows(n) for n in _P1024], []) + [dfg], axis=0),
            jnp.concatenate(rows("ssd_conv_w") + rows("ssd_conv_b"), axis=0),
            jnp.concatenate(rows("ff_conv_w") + rows("ff_conv_b"), axis=0),
            jnp.concatenate(sum([rows(n) for n in _P16], []) + [loss_p[:, :SSD_HEADS]], axis=0),
            jnp.concatenate([layers[k]["gm_ws"].reshape(GM_HEADS * CHUNK, CHUNK) for k in range(DEPTH)] + rows("gm_bs"), axis=0),
            jnp.concatenate([jnp.zeros((nseq, N_MOD * d), F32)] + dmods[1:], axis=0)]
        ssem, rsem, srcs, lands, zero = _xc_start(False, packed, packed[0], "small_start")
        early_gather.update(ssem=ssem, rsem=rsem, srcs=srcs, lands=lands)
        return zero.reshape(1, 1)

    grad_x, small, dmods = _local_step(
        x.reshape(nseq * seq, d), loss_target.reshape(nseq * seq, d), mods, lw, final_g.reshape(1, d), nseq=nseq,
        big_w=big_w, grad_sink=grad_sink, small_sink=small_sink)

    done = grad_x
    for l, grp in ((1, "ffn"), (1, "w_out"), (1, "w_in"), (0, "ffn"), (0, "w_out")):
        done = rs_finish(l, grp, done)
    srcs, lands = _xc_wait(False, early_gather["ssem"], early_gather["rsem"], early_gather["srcs"],
                           early_gather["lands"], done, "small_wait")
    gathered = [lax.dynamic_update_index_in_dim(land, src, me, 0) for src, land in zip(srcs, lands)]
    gathered += _all_gather([small[0]["norm1_g"], dmods[0]], "gather_late", dep=gathered[0])
    gath = dict(zip(["p1024", "p1536", "p2816", "p16", "p128", "p6144", "late1024", "late6144"], gathered))

    dmod_all = jnp.concatenate([gath["late6144"].reshape(1, N_DEV * nseq, N_MOD * d),
                                jnp.transpose(gath["p6144"].reshape(N_DEV, DEPTH, nseq, N_MOD * d)[:, 1:], (1, 0, 2, 3)).reshape(
                                    DEPTH - 1, N_DEV * nseq, N_MOD * d)], axis=0)
    small_names = _P1024 + ["final_g", "ssd_conv_b", "ff_conv_b"] + _P16 + ["gm_ws", "gm_bs", "ada_b"]
    wmv = {}
    for nme in small_names:
        if nme == "final_g":
            wmv[nme] = tuple(a.reshape(1, d) for a in (wts[nme], mom[nme], var[nme]))
        else:
            wmv[nme] = (wts[nme], mom[nme], var[nme])
    small_out, scw_full, fcw_full, loss_sum = _adamw_small(gath, wmv)
    loss = loss_sum[0, 0]
    rs_finish(0, "w_in", scw_full)
    for nme in small_names:
        outs[nme] = small_out[nme]
    outs["final_g"] = tuple(a.reshape(d) for a in outs["final_g"])

    n_scw, n_fcw = ssd_conv_w.shape[2], ff_conv_w.shape[2]
    g_scw_mine = lax.dynamic_slice_in_dim(scw_full, me * n_scw, n_scw, axis=2)
    g_fcw_mine = lax.dynamic_slice_in_dim(fcw_full, me * n_fcw, n_fcw, axis=2)
    outs["ssd_conv_w"] = _adamw_sharded([(g_scw_mine, lambda p: 0)], ssd_conv_w, m_ssd_conv_w, v_ssd_conv_w, me_arr, "adamw_ssd_conv_w")
    outs["ff_conv_w"] = _adamw_sharded([(g_fcw_mine, lambda p: 0)], ff_conv_w, m_ff_conv_w, v_ff_conv_w, me_arr, "adamw_ff_conv_w")

    dmod_cols = _b(lax.dynamic_slice_in_dim(dmod_all, me * n_ada, n_ada, axis=2))
    g_ada = jnp.stack([_matmul(c_act, dmod_cols[l], ta=True, name=f"mm_ada_dw_{l}") for l in range(DEPTH)])
    outs["ada_w"] = _adamw_sharded([(g_ada, lambda p: 0)], ada_w, m_ada_w, v_ada_w, me_arr, "adamw_ada_w")

    for nme in _TRANSPOSED:
        outs[nme] = tuple(jnp.transpose(a, (0, 2, 1)) for a in outs[nme])
    result = [loss, grad_x.reshape(nseq, seq, d)]
    for k in range(4):
        result += [outs[n][k] for n in _WEIGHTS]
    return tuple(result)
```

```python
import functools
import math

import jax
import jax.numpy as jnp
from jax import lax
from jax.experimental import pallas as pl
from jax.experimental.pallas import tpu as pltpu

F32 = jnp.float32
BF16 = jnp.bfloat16

N_DEV = 8
D_MODEL = 1024
DEPTH = 2
CHUNK = 128
SSD_HEADS = 16
SSD_HEAD_DIM = 64
SSD_GROUPS = 2
HEADS_PER_GROUP = SSD_HEADS // SSD_GROUPS
GROUP_WIDTH = HEADS_PER_GROUP * SSD_HEAD_DIM
D_STATE = 128
D_SSD = 1024
CONV_DIM = 1536
SSD_CONV = 4
GM_HEADS = 8
GM_HEAD_DIM = 128
D_GM = 1024
D_FF = 2816
FF_CONV = 3
N_IN = 4624
N_MOD = 6
EPS = 1e-6

N_INP = 5120
COL_U, COL_V, COL_Z, COL_XBC, COL_DT = 0, 1024, 2048, 3072, 4608
DT_BLOCK = 512

ADAM_LR = 0.001
ADAM_B1 = 0.9
ADAM_B2 = 0.999
ADAM_EPS = 1e-08
ADAM_WD = 0.01
ADAM_STEP = 10

VMEM_LIMIT = 56 * 1024 * 1024
MESH = pl.DeviceIdType.MESH
ANY = pl.BlockSpec(memory_space=pl.ANY)


def _cp(*sem):
    return pltpu.CompilerParams(dimension_semantics=sem, vmem_limit_bytes=VMEM_LIMIT)


def _tile(n, pref):
    if n <= pref or n % 128:
        return n
    best = 128
    for t in range(128, pref + 1, 128):
        if n % t == 0:
            best = t
    return best


def _silu(x):
    return x * jax.nn.sigmoid(x)


def _gelu(x):
    return 0.5 * x * (1.0 + lax.erf(x * (1.0 / math.sqrt(2.0))))


def _softplus(x):
    return jnp.maximum(x, 0.0) + jnp.log1p(jnp.exp(-jnp.abs(x)))


def _rms(x, g, width):
    return x * lax.rsqrt(jnp.sum(x * x, axis=-1, keepdims=True) / width + EPS) * g


def _b(x):
    return x.astype(BF16)


_NN = (((1,), (0,)), ((), ()))
_NT = (((1,), (1,)), ((), ()))
_TN = (((0,), (0,)), ((), ()))


def _dg(a, b, dn):
    return lax.dot_general(_b(a), _b(b), dn, preferred_element_type=F32)


@jax.custom_vjp
def _bdot(a, b):
    return _dg(a, b, _NN)


def _bdot_fwd(a, b):
    return _dg(a, b, _NN), (a, b)


def _bdot_bwd(res, ct):
    a, b = res
    return _dg(ct, b, _NT), _dg(a, ct, _TN)


_bdot.defvjp(_bdot_fwd, _bdot_bwd)


@jax.custom_vjp
def _bdot_nt(a, b):
    return _dg(a, b, _NT)


def _bdot_nt_fwd(a, b):
    return _dg(a, b, _NT), (a, b)


def _bdot_nt_bwd(res, ct):
    a, b = res
    return _dg(ct, b, _NN), _dg(ct, a, _TN)


_bdot_nt.defvjp(_bdot_nt_fwd, _bdot_nt_bwd)


@jax.custom_vjp
def _bdot_tn(a, b):
    return _dg(a, b, _TN)


def _bdot_tn_fwd(a, b):
    return _dg(a, b, _TN), (a, b)


def _bdot_tn_bwd(res, ct):
    a, b = res
    return _dg(b, ct, _NT), _dg(a, ct, _NN)


_bdot_tn.defvjp(_bdot_tn_fwd, _bdot_tn_bwd)


def _tri(n, lower):
    r = lax.broadcasted_iota(jnp.int32, (n, n), 0)
    c = lax.broadcasted_iota(jnp.int32, (n, n), 1)
    return ((r >= c) if lower else (r <= c)).astype(F32)


def _eye(n):
    r = lax.broadcasted_iota(jnp.int32, (n, n), 0)
    c = lax.broadcasted_iota(jnp.int32, (n, n), 1)
    return (r == c).astype(F32)


def _hdot(a, b, dn):
    return lax.dot_general(a, b, dn, precision=lax.Precision.HIGHEST, preferred_element_type=F32)


@jax.custom_vjp
def _cumsum_rows(x):
    return _hdot(_tri(x.shape[0], True), x, _NN)


def _cumsum_rows_fwd(x):
    return _cumsum_rows(x), None


def _cumsum_rows_bwd(_, ct):
    return (_hdot(_tri(ct.shape[0], False), ct, _NN),)


_cumsum_rows.defvjp(_cumsum_rows_fwd, _cumsum_rows_bwd)


@jax.custom_vjp
def _transpose(x):
    return _hdot(_eye(x.shape[1]), x, _NT)


def _transpose_fwd(x):
    return _transpose(x), None


def _transpose_bwd(_, ct):
    return (_hdot(_eye(ct.shape[1]), ct, _NT),)


_transpose.defvjp(_transpose_fwd, _transpose_bwd)


MXU_WIDTH = 256
MATMUL_TILE_CAP = 2816
MATMUL_VMEM = 44 * 1024 * 1024


def _mxu_tiles(n):
    if n <= MATMUL_TILE_CAP or n % 128:
        return [n]
    for unit in (MXU_WIDTH, 128):
        opts = [t for t in range(unit, MATMUL_TILE_CAP + 1, unit) if n % t == 0]
        if opts:
            return opts
    return [n]


def _matmul(a, b, *, ta=False, tb=False, name, dep=None, out_dtype=F32):
    pieces = list(a) if isinstance(a, (list, tuple)) else [a]
    npc = len(pieces)
    rows, width = pieces[0].shape
    assert all(p.shape == (rows, width) for p in pieces)
    if ta:
        k_dim, m_dim = rows, width * npc
    else:
        m_dim, k_dim = rows, width * npc
    if tb:
        n_dim, kb = b.shape
    else:
        kb, n_dim = b.shape
    assert kb == k_dim, (pieces[0].shape, npc, b.shape, ta, tb)
    m_unit = width if npc > 1 and ta else m_dim
    k_unit = width if npc > 1 and not ta else k_dim
    tm = _tile(m_unit, 1536)
    tn_opts, tk_opts = _mxu_tiles(n_dim), _mxu_tiles(k_unit)
    tn, tk = tn_opts.pop(), tk_opts.pop()
    while 4 * (tm * tk + tk * tn) + 8 * tm * tn > MATMUL_VMEM:
        if tn >= tk and tn_opts:
            tn = tn_opts.pop()
        else:
            tk = tk_opts.pop()
    ni, nj, nk = m_dim // tm, n_dim // tn, k_dim // tk
    per = width // (tm if ta else tk)
    dn = (((0 if ta else 1,), (1 if tb else 0,)), ((), ()))

    a_bytes, b_bytes = m_dim * k_dim, k_dim * n_dim
    m_outer = nk > 1 or a_bytes + b_bytes * ni <= b_bytes + a_bytes * nj
    if m_outer:
        ij = lambda o, n, k: (o, n)
        grid = (ni, nj, nk)
    else:
        ij = lambda o, n, k: (n, o)
        grid = (nj, ni, nk)

    use_acc = nk > 1 and out_dtype != F32

    def body(*refs):
        a_refs, b_ref = refs[:npc], refs[npc]
        o_ref = refs[-2] if use_acc else refs[-1]
        acc_ref = refs[-1]
        k = pl.program_id(2)
        i = pl.program_id(0 if m_outer else 1)
        along = i if ta else k

        def step(a_ref):
            p = lax.dot_general(a_ref[...], b_ref[...], dn, preferred_element_type=F32)
            if nk == 1:
                o_ref[...] = p.astype(out_dtype)
            else:
                @pl.when(k == 0)
                def _():
                    acc_ref[...] = p

                @pl.when((k > 0) & (k < nk - 1 if use_acc else True))
                def _():
                    acc_ref[...] += p

                if use_acc:
                    @pl.when(k == nk - 1)
                    def _():
                        o_ref[...] = (acc_ref[...] + p).astype(out_dtype)

        if npc == 1:
            step(a_refs[0])
        else:
            for pc in range(npc):
                pl.when((along >= pc * per) & (along < (pc + 1) * per))(functools.partial(step, a_refs[pc]))

    def a_map(pc, o, n, k):
        i, _ = ij(o, n, k)
        along = i if ta else k
        if npc > 1:
            along = jnp.clip(along - pc * per, 0, per - 1)
        return (k, along) if ta else (i, along)

    def b_map(o, n, k):
        _, j = ij(o, n, k)
        return (j, k) if tb else (k, j)

    extra = [] if dep is None else [dep]
    return pl.pallas_call(
        body, name=name,
        grid=grid,
        in_specs=[pl.BlockSpec((tk, tm) if ta else (tm, tk), functools.partial(a_map, pc)) for pc in range(npc)]
        + [pl.BlockSpec((tn, tk) if tb else (tk, tn), b_map)] + [ANY] * len(extra),
        out_specs=pl.BlockSpec((tm, tn), lambda o, n, k: ij(o, n, k)),
        out_shape=jax.ShapeDtypeStruct((m_dim, n_dim), out_dtype),
        scratch_shapes=[pltpu.VMEM((tm, tn), F32)] if use_acc else [],
        compiler_params=_cp("parallel", "parallel", "arbitrary"),
    )(*pieces, b, *extra)


def _ada_fwd(c_all, ada_w, ada_b_shard):
    depth, d, n = ada_w.shape
    nb = c_all.shape[0]

    def body(c_ref, w_ref, b_ref, o_ref, ca_ref):
        ca = _silu(c_ref[...])
        ca_ref[...] = _b(ca)
        o_ref[0] = _dg(ca, w_ref[0], _NN) + b_ref[0]

    return pl.pallas_call(
        body, name="ada_fwd",
        grid=(depth,),
        in_specs=[pl.BlockSpec((nb, d), lambda l: (0, 0)),
                  pl.BlockSpec((1, d, n), lambda l: (l, 0, 0)),
                  pl.BlockSpec((1, 1, n), lambda l: (l, 0, 0))],
        out_specs=[pl.BlockSpec((1, nb, n), lambda l: (l, 0, 0)),
                   pl.BlockSpec((nb, d), lambda l: (0, 0))],
        out_shape=[jax.ShapeDtypeStruct((depth, nb, n), F32), jax.ShapeDtypeStruct((nb, d), BF16)],
        compiler_params=_cp("arbitrary"),
    )(c_all, ada_w, ada_b_shard)


def _fold(acc):
    return jnp.sum(acc, axis=0, keepdims=True)


def _rinv(x):
    return lax.rsqrt(jnp.sum(x * x, axis=-1, keepdims=True) * (1.0 / D_MODEL) + EPS)


def _rms_bwd(a, xhat, rinv):
    return rinv * (a - xhat * (jnp.sum(a * xhat, axis=-1, keepdims=True) * (1.0 / D_MODEL)))


def _row_tile(seq):
    return min(seq, 256)


def _normmod_fwd(x, g, sc, sh, *, nseq, name):
    t, d = x.shape
    seq = t // nseq
    tr = _row_tile(seq)
    nt = seq // tr
    row = pl.BlockSpec((tr, d), lambda s, i: (s * nt + i, 0))
    per_seq = pl.BlockSpec((1, 1, d), lambda s, i: (s, 0, 0))

    def body(x_ref, g_ref, sc_ref, sh_ref, h_ref):
        x_v = x_ref[...]
        h_ref[...] = _b(x_v * _rinv(x_v) * (g_ref[...] * (1.0 + sc_ref[0])) + sh_ref[0])

    return pl.pallas_call(
        body, name=name, grid=(nseq, nt),
        in_specs=[row, pl.BlockSpec((1, d), lambda s, i: (0, 0)), per_seq, per_seq],
        out_specs=row,
        out_shape=jax.ShapeDtypeStruct((t, d), BF16),
        compiler_params=_cp("parallel", "parallel"),
    )(x, g, sc, sh)


NORM_TM = 512


def _normfwd_matmul(xin, delta, gate, g, sc, sh, b, *, nseq, name, out_dtype=F32):
    t, d = xin.shape
    n_dim = b.shape[0]
    seq = t // nseq
    tm = min(NORM_TM, seq)
    per_seq_tiles = seq // tm
    has_delta = delta is not None
    out_bytes = jnp.dtype(out_dtype).itemsize
    tn_opts = _mxu_tiles(n_dim)
    tn = tn_opts.pop()
    while 2 * tm * d * (14 if has_delta else 6) + 4 * tn * d + 2 * tm * tn * out_bytes > MATMUL_VMEM:
        tn = tn_opts.pop()

    def body(*refs):
        if has_delta:
            xin_ref, delta_ref, gate_ref, g_ref, sc_ref, sh_ref, b_ref, x_ref, h_ref, o_ref = refs
        else:
            xin_ref, g_ref, sc_ref, sh_ref, b_ref, h_ref, o_ref = refs

        @pl.when(pl.program_id(1) == 0)
        def _():
            x = xin_ref[...]
            if has_delta:
                x = x + gate_ref[0] * delta_ref[...]
                x_ref[...] = x
            h_ref[...] = _b(x * _rinv(x) * (g_ref[...] * (1.0 + sc_ref[0])) + sh_ref[0])

        o_ref[...] = lax.dot_general(h_ref[...], b_ref[...], _NT, preferred_element_type=F32).astype(out_dtype)

    row = pl.BlockSpec((tm, d), lambda i, j: (i, 0))
    per_seq = pl.BlockSpec((1, 1, d), lambda i, j: (i // per_seq_tiles, 0, 0))
    vec = pl.BlockSpec((1, d), lambda i, j: (0, 0))
    b_spec = pl.BlockSpec((tn, d), lambda i, j: (j, 0))
    o_spec = pl.BlockSpec((tm, tn), lambda i, j: (i, j))
    shp = lambda *s, dt=F32: jax.ShapeDtypeStruct(s, dt)
    if has_delta:
        in_specs, operands = [row, row, per_seq, vec, per_seq, per_seq, b_spec], [xin, delta, gate, g, sc, sh, b]
        out_specs, out_shape = [row, row, o_spec], [shp(t, d), shp(t, d, dt=BF16), shp(t, n_dim, dt=out_dtype)]
    else:
        in_specs, operands = [row, vec, per_seq, per_seq, b_spec], [xin, g, sc, sh, b]
        out_specs, out_shape = [row, o_spec], [shp(t, d, dt=BF16), shp(t, n_dim, dt=out_dtype)]
    outs = pl.pallas_call(
        body, name=name, grid=(t // tm, n_dim // tn),
        in_specs=in_specs, out_specs=out_specs, out_shape=out_shape,
        compiler_params=_cp("parallel", "arbitrary"),
    )(*operands)
    return tuple(outs) if has_delta else (xin, *outs)


def _matmul_normbwd(a, b, dxo, x, delta, gate, g, sc, *, nseq, name, dep=None):
    pieces = list(a) if isinstance(a, (list, tuple)) else [a]
    npc = len(pieces)
    t, width = pieces[0].shape
    k_dim, d = width * npc, b.shape[1]
    assert b.shape[0] == k_dim and all(p.shape == (t, width) for p in pieces)
    seq = t // nseq
    tm = min(NORM_TM, seq)
    per_seq_tiles = seq // tm
    tk = _mxu_tiles(width if npc > 1 else k_dim).pop()
    nk, per = k_dim // tk, width // tk
    has_delta = delta is not None
    extra = [] if dep is None else [dep]

    def body(*refs):
        a_refs, b_ref = refs[:npc], refs[npc]
        dxo_ref, x_ref = refs[npc + 1], refs[npc + 2]
        pos = npc + 3
        if has_delta:
            delta_ref, gate_ref = refs[pos], refs[pos + 1]
            pos += 2
        g_ref, sc_ref = refs[pos], refs[pos + 1]
        pos += 2 + len(extra)
        if has_delta:
            dx_ref, dd_ref, dgate_ref, dg_ref, dsc_ref, dsh_ref = refs[pos:pos + 6]
        else:
            dx_ref, dg_ref, dsc_ref, dsh_ref = refs[pos:pos + 4]
        acc_ref = refs[-1]
        i, k = pl.program_id(0), pl.program_id(1)

        def norm_bwd(dh_v):
            g_v, one_sc = g_ref[...], 1.0 + sc_ref[0]
            x_v = x_ref[...]
            rinv = _rinv(x_v)
            xhat = x_v * rinv
            dx = dxo_ref[...] + _rms_bwd(dh_v * (g_v * one_sc), xhat, rinv)
            dx_ref[...] = dx

            @pl.when(i == 0)
            def _():
                dg_ref[...] = jnp.zeros_like(dg_ref)

            @pl.when(i % per_seq_tiles == 0)
            def _():
                dsc_ref[...] = jnp.zeros_like(dsc_ref)
                dsh_ref[...] = jnp.zeros_like(dsh_ref)
                if has_delta:
                    dgate_ref[...] = jnp.zeros_like(dgate_ref)

            t_sum = _fold(dh_v * xhat)
            dg_ref[...] += t_sum * one_sc
            dsc_ref[0] += t_sum * g_v
            dsh_ref[0] += _fold(dh_v)
            if has_delta:
                dd_ref[...] = _b(dx * gate_ref[0])
                dgate_ref[0] += _fold(dx * delta_ref[...])

        def step(a_ref):
            p = lax.dot_general(a_ref[...], b_ref[...], _NN, preferred_element_type=F32)
            if nk == 1:
                norm_bwd(p)
            else:
                @pl.when(k == 0)
                def _():
                    acc_ref[...] = p

                @pl.when((k > 0) & (k < nk - 1))
                def _():
                    acc_ref[...] += p

                @pl.when(k == nk - 1)
                def _():
                    norm_bwd(acc_ref[...] + p)

        if npc == 1:
            step(a_refs[0])
        else:
            for pc in range(npc):
                pl.when((k >= pc * per) & (k < (pc + 1) * per))(functools.partial(step, a_refs[pc]))

    def a_map(pc, i, k):
        return (i, jnp.clip(k - pc * per, 0, per - 1) if npc > 1 else k)

    row = pl.BlockSpec((tm, d), lambda i, k: (i, 0))
    per_seq = pl.BlockSpec((1, 1, d), lambda i, k: (i // per_seq_tiles, 0, 0))
    vec = pl.BlockSpec((1, d), lambda i, k: (0, 0))
    shp = lambda *s, dt=F32: jax.ShapeDtypeStruct(s, dt)
    in_specs = [pl.BlockSpec((tm, tk), functools.partial(a_map, pc)) for pc in range(npc)]
    in_specs += [pl.BlockSpec((tk, d), lambda i, k: (k, 0)), row, row]
    operands = [*pieces, b, dxo, x]
    if has_delta:
        in_specs += [row, per_seq]
        operands += [delta, gate]
    in_specs += [vec, per_seq] + [ANY] * len(extra)
    operands += [g, sc, *extra]
    if has_delta:
        out_specs = [row, row, per_seq, vec, per_seq, per_seq]
        out_shape = [shp(t, d), shp(t, d, dt=BF16), shp(nseq, 1, d), shp(1, d), shp(nseq, 1, d), shp(nseq, 1, d)]
    else:
        out_specs = [row, vec, per_seq, per_seq]
        out_shape = [shp(t, d), shp(1, d), shp(nseq, 1, d), shp(nseq, 1, d)]
    outs = pl.pallas_call(
        body, name=name, grid=(t // tm, nk),
        in_specs=in_specs, out_specs=out_specs, out_shape=out_shape,
        scratch_shapes=[pltpu.VMEM((tm, d), F32)],
        compiler_params=_cp("arbitrary", "arbitrary"),
    )(*operands)
    if has_delta:
        return tuple(outs)
    dx, dg, dsc, dsh = outs
    return dx, None, None, dg, dsc, dsh


def _final_loss(xin, delta, gate, fg, target, *, nseq):
    t, d = xin.shape
    seq = t // nseq
    tr = _row_tile(seq)
    nt = seq // tr
    row = pl.BlockSpec((tr, d), lambda s, i: (s * nt + i, 0))
    per_seq = pl.BlockSpec((1, 1, d), lambda s, i: (s, 0, 0))
    vec = pl.BlockSpec((1, d), lambda s, i: (0, 0))

    def body(xin_ref, delta_ref, gate_ref, fg_ref, tgt_ref, loss_ref, dx_ref, dd_ref, dgate_ref, dfg_ref):
        s, i = pl.program_id(0), pl.program_id(1)
        fg_v, gate_v = fg_ref[...], gate_ref[0]
        dl = delta_ref[...]
        x = xin_ref[...] + gate_v * dl
        rinv = _rinv(x)
        xhat = x * rinv
        err = xhat * fg_v - tgt_ref[...]
        dx = _rms_bwd(err * fg_v * (1.0 / d), xhat, rinv)
        dx_ref[...] = dx
        dd_ref[...] = _b(dx * gate_v)
        acc_l, acc_f, acc_g = err * err, err * xhat, dx * dl

        @pl.when((s == 0) & (i == 0))
        def _():
            loss_ref[...] = jnp.zeros_like(loss_ref)
            dfg_ref[...] = jnp.zeros_like(dfg_ref)

        @pl.when(i == 0)
        def _():
            dgate_ref[...] = jnp.zeros_like(dgate_ref)

        loss_ref[...] += jnp.sum(acc_l) * (0.5 / d)
        dfg_ref[...] += _fold(acc_f) * (1.0 / d)
        dgate_ref[0] += _fold(acc_g)

    return pl.pallas_call(
        body, name="final_loss", grid=(nseq, nt),
        in_specs=[row, row, per_seq, vec, row],
        out_specs=[pl.BlockSpec((1, 128), lambda s, i: (0, 0)), row, row, per_seq, vec],
        out_shape=[jax.ShapeDtypeStruct((1, 128), F32), jax.ShapeDtypeStruct((t, d), F32),
                   jax.ShapeDtypeStruct((t, d), BF16), jax.ShapeDtypeStruct((nseq, 1, d), F32),
                   jax.ShapeDtypeStruct((1, d), F32)],
        compiler_params=_cp("arbitrary", "arbitrary"),
    )(xin, delta, gate, fg, target)


CONV_TC = 256
CONV_LANES = 128
CONV_ROWS = 64
CONV_HALO = 8


def _conv_slabs(seq, fn):
    def step(i, carry):
        r0 = pl.multiple_of(i * CONV_ROWS, CONV_ROWS)
        for h in range(CONV_TC // CONV_LANES):
            fn(r0, slice(h * CONV_LANES, (h + 1) * CONV_LANES))
        return carry

    lax.fori_loop(0, seq // CONV_ROWS, step, 0)


def _slab(ref, r0, cols, seq, before=0, after=0):
    parts = []
    if before:
        top = ref[pl.ds(pl.multiple_of(jnp.maximum(r0 - before, 0), CONV_HALO), before), cols]
        parts.append(jnp.where(r0 > 0, top, 0.0))
    parts.append(ref[pl.ds(r0, CONV_ROWS), cols])
    if after:
        bot = ref[pl.ds(pl.multiple_of(jnp.minimum(r0 + CONV_ROWS, seq - after), CONV_HALO), after), cols]
        parts.append(jnp.where(r0 + CONV_ROWS < seq, bot, 0.0))
    return jnp.concatenate(parts, axis=0) if len(parts) > 1 else parts[0]


def _conv_block(x, w_ref, b_ref):
    kw = w_ref.shape[0]
    rows = lax.broadcasted_iota(jnp.int32, x.shape, 0)
    y = b_ref[...] + w_ref[kw - 1:kw, :] * x
    for j in range(1, kw):
        y = y + w_ref[kw - 1 - j:kw - j, :] * jnp.where(rows >= j, pltpu.roll(x, j, 0), 0.0)
    return y


def _conv_block_bwd(dy, x, w_ref, dw_ref, db_ref):
    kw = w_ref.shape[0]
    n = x.shape[0]
    rows = lax.broadcasted_iota(jnp.int32, x.shape, 0)
    dx = w_ref[kw - 1:kw, :] * dy
    dw_ref[kw - 1:kw, :] += jnp.sum(dy * x, axis=0, keepdims=True)
    for j in range(1, kw):
        dy_j = jnp.where(rows < n - j, pltpu.roll(dy, n - j, 0), 0.0)
        dx = dx + w_ref[kw - 1 - j:kw - j, :] * dy_j
        dw_ref[kw - 1 - j:kw - j, :] += jnp.sum(dy_j * x, axis=0, keepdims=True)
    db_ref[...] += jnp.sum(dy, axis=0, keepdims=True)
    return dx


def _conv_bwd(dy_ext, x, w_ref, dw_ref, db_ref, cols):
    kw = w_ref.shape[0]
    n = dy_ext.shape[0]
    dy = dy_ext[:CONV_ROWS]
    dx = w_ref[kw - 1:kw, cols] * dy
    dw_ref[kw - 1:kw, cols] += jnp.sum(dy * x, axis=0, keepdims=True)
    for j in range(1, kw):
        dy_j = pltpu.roll(dy_ext, n - j, 0)[:CONV_ROWS]
        dx = dx + w_ref[kw - 1 - j:kw - j, cols] * dy_j
        dw_ref[kw - 1 - j:kw - j, cols] += jnp.sum(dy_j * x, axis=0, keepdims=True)
    db_ref[:, cols] += jnp.sum(dy, axis=0, keepdims=True)
    return dx


def _dsilu(pre):
    sg = jax.nn.sigmoid(pre)
    return pre * sg, sg * (1.0 + pre * (1.0 - sg))


def _ssd_conv_fwd(proj, w, b, *, nseq):
    t = proj.shape[0]
    seq = t // nseq
    nb = CONV_DIM // CONV_TC
    off = COL_XBC // CONV_TC

    def body(x_ref, w_ref, b_ref, o_ref, pre_ref):
        pre = _conv_block(x_ref[...], w_ref, b_ref)
        pre_ref[...] = pre
        o_ref[...] = _silu(pre)

    col = pl.BlockSpec((seq, CONV_TC), lambda j, s: (s, j))
    return pl.pallas_call(
        body, name="ssd_conv_fwd", grid=(nb, nseq),
        in_specs=[pl.BlockSpec((seq, CONV_TC), lambda j, s: (s, off + j)),
                  pl.BlockSpec((SSD_CONV, CONV_TC), lambda j, s: (0, j)),
                  pl.BlockSpec((1, CONV_TC), lambda j, s: (0, j))],
        out_specs=[col, col],
        out_shape=[jax.ShapeDtypeStruct((t, CONV_DIM), F32)] * 2,
        compiler_params=_cp("parallel", "parallel"),
    )(proj, w, b)


def _ssd_conv_bwd(dact, pre, proj, w, dproj, *, nseq):
    t = proj.shape[0]
    seq = t // nseq
    nb = CONV_DIM // CONV_TC
    off = COL_XBC // CONV_TC

    def body(da_ref, pre_ref, x_ref, w_ref, dproj_ref, dx_ref, dw_ref, db_ref):
        del dproj_ref

        @pl.when(pl.program_id(1) == 0)
        def _():
            dw_ref[...] = jnp.zeros_like(dw_ref)
            db_ref[...] = jnp.zeros_like(db_ref)

        def slab(r0, cols):
            _, dsilu = _dsilu(_slab(pre_ref, r0, cols, seq, after=CONV_HALO))
            dpre_ext = _slab(da_ref, r0, cols, seq, after=CONV_HALO) * dsilu
            x = x_ref[pl.ds(r0, CONV_ROWS), cols]
            dx_ref[pl.ds(r0, CONV_ROWS), cols] = _b(_conv_bwd(dpre_ext, x, w_ref, dw_ref, db_ref, cols))

        _conv_slabs(seq, slab)

    return pl.pallas_call(
        body, name="ssd_conv_bwd", grid=(nb, nseq),
        in_specs=[pl.BlockSpec((seq, CONV_TC), lambda j, s: (s, j)),
                  pl.BlockSpec((seq, CONV_TC), lambda j, s: (s, j)),
                  pl.BlockSpec((seq, CONV_TC), lambda j, s: (s, off + j)),
                  pl.BlockSpec((SSD_CONV, CONV_TC), lambda j, s: (0, j)),
                  ANY],
        out_specs=[pl.BlockSpec((seq, CONV_TC), lambda j, s: (s, off + j)),
                   pl.BlockSpec((SSD_CONV, CONV_TC), lambda j, s: (0, j)),
                   pl.BlockSpec((1, CONV_TC), lambda j, s: (0, j))],
        out_shape=[jax.ShapeDtypeStruct(dproj.shape, dproj.dtype), jax.ShapeDtypeStruct((SSD_CONV, CONV_DIM), F32),
                   jax.ShapeDtypeStruct((1, CONV_DIM), F32)],
        input_output_aliases={4: 0},
        compiler_params=_cp("parallel", "arbitrary"),
    )(dact, pre, proj, w, dproj)


def _ffn_act_fwd(up, w, b, *, nseq):
    t = up.shape[0]
    seq = t // nseq
    nb = D_FF // CONV_TC

    def body(g_ref, v_ref, w_ref, b_ref, o_ref):
        o_ref[...] = _b(_silu(_conv_block(g_ref[...].astype(F32), w_ref, b_ref)) * v_ref[...].astype(F32))

    col = pl.BlockSpec((seq, CONV_TC), lambda j, s: (s, j))
    return pl.pallas_call(
        body, name="ffn_act_fwd", grid=(nb, nseq),
        in_specs=[col,
                  pl.BlockSpec((seq, CONV_TC), lambda j, s: (s, nb + j)),
                  pl.BlockSpec((FF_CONV, CONV_TC), lambda j, s: (0, j)),
                  pl.BlockSpec((1, CONV_TC), lambda j, s: (0, j))],
        out_specs=col,
        out_shape=jax.ShapeDtypeStruct((t, D_FF), BF16),
        compiler_params=_cp("parallel", "parallel"),
    )(up, up, w, b)


def _ffn_act_bwd(dact, up, w, b, *, nseq):
    t = up.shape[0]
    seq = t // nseq
    nb = D_FF // CONV_TC

    def body(da_ref, g_ref, v_ref, w_ref, b_ref, dg_ref, dv_ref, dw_ref, db_ref):
        @pl.when(pl.program_id(1) == 0)
        def _():
            dw_ref[...] = jnp.zeros_like(dw_ref)
            db_ref[...] = jnp.zeros_like(db_ref)

        gate = g_ref[...].astype(F32)
        silu, dsilu = _dsilu(_conv_block(gate, w_ref, b_ref))
        da = da_ref[...].astype(F32)
        dv_ref[...] = _b(da * silu)
        dg_ref[...] = _b(_conv_block_bwd(da * v_ref[...].astype(F32) * dsilu, gate, w_ref, dw_ref, db_ref))

    col = pl.BlockSpec((seq, CONV_TC), lambda j, s: (s, j))
    return pl.pallas_call(
        body, name="ffn_act_bwd", grid=(nb, nseq),
        in_specs=[col, col,
                  pl.BlockSpec((seq, CONV_TC), lambda j, s: (s, nb + j)),
                  pl.BlockSpec((FF_CONV, CONV_TC), lambda j, s: (0, j)),
                  pl.BlockSpec((1, CONV_TC), lambda j, s: (0, j))],
        out_specs=[col, col,
                   pl.BlockSpec((FF_CONV, CONV_TC), lambda j, s: (0, j)),
                   pl.BlockSpec((1, CONV_TC), lambda j, s: (0, j))],
        out_shape=[jax.ShapeDtypeStruct((t, D_FF), BF16), jax.ShapeDtypeStruct((t, D_FF), BF16),
                   jax.ShapeDtypeStruct((FF_CONV, D_FF), F32), jax.ShapeDtypeStruct((1, D_FF), F32)],
        compiler_params=_cp("parallel", "arbitrary"),
    )(dact, up, up, w, b)


SSD_PAIRS = SSD_HEADS // 2
PAIR_W = 2 * SSD_HEAD_DIM
PAIRS_PER_GROUP = SSD_PAIRS // SSD_GROUPS


def _ssd_chunk(xs, bg, cg, dtr, z, hp, dtb, alog, dskip, ng):
    n = dtr.shape[0]
    dt = _softplus(dtr + dtb)
    cs = _cumsum_rows(dt * (-jnp.exp(alog)))
    cs_t = _transpose(cs)
    lane = lax.broadcasted_iota(jnp.int32, (1, SSD_HEADS), 1)
    sub = lax.broadcasted_iota(jnp.int32, (SSD_HEADS, 1), 0)
    row = lax.broadcasted_iota(jnp.int32, (n, 1), 0)
    causal = lax.broadcasted_iota(jnp.int32, (n, n), 0) >= lax.broadcasted_iota(jnp.int32, (n, n), 1)
    future = jnp.where(causal, 0.0, -1e30)
    first = lax.broadcasted_iota(jnp.int32, (1, PAIR_W), 1) < SSD_HEAD_DIM
    first_rows = lax.broadcasted_iota(jnp.int32, (PAIR_W, 1), 0) < SSD_HEAD_DIM
    first_f = first.astype(F32)
    cb = [_bdot_nt(cg[g], bg[g]) for g in range(SSD_GROUPS)]
    ys, hn = [], []
    for p in range(SSD_PAIRS):
        g = p // PAIRS_PER_GROUP
        col, decay, last = [], [], []
        for h in (2 * p, 2 * p + 1):
            oh = (lane == h).astype(F32)
            cs_h = jnp.sum(cs * oh, axis=1, keepdims=True)
            cs_row = jnp.sum(cs_t * (sub == h).astype(F32), axis=0, keepdims=True)
            col.append((jnp.sum(dt * oh, axis=1, keepdims=True), cs_h, jnp.sum(dskip * oh, axis=1, keepdims=True)))
            last.append(jnp.sum(jnp.where(row == n - 1, cs_h, 0.0), axis=0, keepdims=True))
            decay.append(jnp.exp(cs_h - cs_row + future))
        pair = lambda a, b: jnp.where(first, a, b)
        dt_p = pair(col[0][0], col[1][0])
        cs_p = pair(col[0][1], col[1][1])
        last_p = pair(last[0], last[1])
        xc = xs[p] * dt_p
        y = _bdot(cb[g] * decay[0], xc * first_f) + _bdot(cb[g] * decay[1], xc * (1.0 - first_f))
        y = y + _bdot_nt(cg[g], hp[p]) * jnp.exp(cs_p)
        y = y + pair(col[0][2], col[1][2]) * xs[p]
        keep = jnp.where(first_rows, jnp.exp(last[0]), jnp.exp(last[1]))
        hn.append(keep * hp[p] + _bdot_tn(xc * jnp.exp(last_p - cs_p), bg[g]))
        ys.append(y * _silu(z[p]))
    outs = []
    for g in range(SSD_GROUPS):
        ps = range(g * PAIRS_PER_GROUP, (g + 1) * PAIRS_PER_GROUP)
        ms = sum(jnp.sum(ys[p] * ys[p], axis=1, keepdims=True) for p in ps) * (1.0 / GROUP_WIDTH)
        r = lax.rsqrt(ms + EPS)
        outs += [ys[p] * r * ng[p] for p in ps]
    return outs, hn


def _hslices(ref, width, count, base=0):
    return [ref[:, base + k * width: base + (k + 1) * width] for k in range(count)]


def _ssd_load(xbc_ref, z_ref, dt_ref, ng_ref):
    xs = _hslices(xbc_ref, PAIR_W, SSD_PAIRS)
    bg = _hslices(xbc_ref, D_STATE, SSD_GROUPS, D_SSD)
    cg = _hslices(xbc_ref, D_STATE, SSD_GROUPS, D_SSD + SSD_GROUPS * D_STATE)
    z = _hslices(z_ref, PAIR_W, SSD_PAIRS)
    ng = _hslices(ng_ref, PAIR_W, SSD_PAIRS)
    return xs, bg, cg, dt_ref[:, 0:SSD_HEADS], z, ng


def _ssd_specs(nch):
    rowi = lambda s, c: s * nch + c
    return [pl.BlockSpec((CHUNK, CONV_DIM), lambda s, c: (rowi(s, c), 0)),
            pl.BlockSpec((CHUNK, D_SSD), lambda s, c: (rowi(s, c), COL_Z // D_SSD)),
            pl.BlockSpec((CHUNK, 128), lambda s, c: (rowi(s, c), COL_DT // 128)),
            pl.BlockSpec((1, SSD_HEADS), lambda s, c: (0, 0)),
            pl.BlockSpec((1, SSD_HEADS), lambda s, c: (0, 0)),
            pl.BlockSpec((1, SSD_HEADS), lambda s, c: (0, 0)),
            pl.BlockSpec((1, D_SSD), lambda s, c: (0, 0))]


def _ssd_fwd(xbc, proj, dtb, alog, dskip, ng, *, nseq):
    t = proj.shape[0]
    nch = t // nseq // CHUNK
    hd = PAIR_W

    def body(xbc_ref, z_ref, dt_ref, dtb_ref, alog_ref, dsk_ref, ng_ref, y_ref, hp_ref, h_ref):
        @pl.when(pl.program_id(1) == 0)
        def _():
            h_ref[...] = jnp.zeros_like(h_ref)

        xs, bg, cg, dtr, z, ngs = _ssd_load(xbc_ref, z_ref, dt_ref, ng_ref)
        hp_ref[0] = h_ref[...]
        hp = [h_ref[h * hd:(h + 1) * hd, :] for h in range(SSD_PAIRS)]
        outs, hn = _ssd_chunk(xs, bg, cg, dtr, z, hp, dtb_ref[...], alog_ref[...], dsk_ref[...], ngs)
        for h in range(SSD_PAIRS):
            y_ref[:, h * hd:(h + 1) * hd] = _b(outs[h])
            h_ref[h * hd:(h + 1) * hd, :] = hn[h]

    return pl.pallas_call(
        body, name="ssd_fwd", grid=(nseq, nch),
        in_specs=_ssd_specs(nch),
        out_specs=[pl.BlockSpec((CHUNK, D_SSD), lambda s, c: (s * nch + c, 0)),
                   pl.BlockSpec((1, D_SSD, D_STATE), lambda s, c: (s * nch + c, 0, 0))],
        out_shape=[jax.ShapeDtypeStruct((t, D_SSD + D_GM), BF16),
                   jax.ShapeDtypeStruct((t // CHUNK, D_SSD, D_STATE), F32)],
        scratch_shapes=[pltpu.VMEM((D_SSD, D_STATE), F32)],
        compiler_params=_cp("arbitrary", "arbitrary"),
    )(xbc, proj, proj, dtb, alog, dskip, ng)


def _ssd_bwd(dy, xbc, proj, hprev, dtb, alog, dskip, ng, *, nseq):
    t = proj.shape[0]
    nch = t // nseq // CHUNK
    hd = PAIR_W
    rev = lambda s, c: s * nch + (nch - 1 - c)

    def body(dy_ref, xbc_ref, z_ref, dt_ref, hp_ref, dtb_ref, alog_ref, dsk_ref, ng_ref,
             dxbc_ref, dproj_ref, ddtb_ref, dalog_ref, ddsk_ref, dng_ref, dh_ref):
        first = (pl.program_id(0) == 0) & (pl.program_id(1) == 0)

        @pl.when(pl.program_id(1) == 0)
        def _():
            dh_ref[...] = jnp.zeros_like(dh_ref)

        @pl.when(first)
        def _():
            ddtb_ref[...] = jnp.zeros_like(ddtb_ref)
            dalog_ref[...] = jnp.zeros_like(dalog_ref)
            ddsk_ref[...] = jnp.zeros_like(ddsk_ref)
            dng_ref[...] = jnp.zeros_like(dng_ref)

        xs, bg, cg, dtr, z, ngs = _ssd_load(xbc_ref, z_ref, dt_ref, ng_ref)
        hp = [hp_ref[0, h * hd:(h + 1) * hd, :] for h in range(SSD_PAIRS)]
        _, vjp = jax.vjp(_ssd_chunk, xs, bg, cg, dtr, z, hp, dtb_ref[...], alog_ref[...], dsk_ref[...], ngs)
        douts = [dy_ref[:, h * hd:(h + 1) * hd] for h in range(SSD_PAIRS)]
        dhn = [dh_ref[h * hd:(h + 1) * hd, :] for h in range(SSD_PAIRS)]
        dxs, dbg, dcg, ddtr, dz, dhp, ddtb, dalog, ddsk, dngs = vjp((douts, dhn))
        dproj_ref[:, :COL_Z] = jnp.zeros((CHUNK, COL_Z), BF16)
        dproj_ref[:, COL_XBC:] = jnp.zeros((CHUNK, N_INP - COL_XBC), BF16)
        for h in range(SSD_PAIRS):
            dxbc_ref[:, h * hd:(h + 1) * hd] = dxs[h]
            dproj_ref[:, COL_Z + h * hd: COL_Z + (h + 1) * hd] = _b(dz[h])
            dh_ref[h * hd:(h + 1) * hd, :] = dhp[h]
            dng_ref[:, h * hd:(h + 1) * hd] += dngs[h]
        for g in range(SSD_GROUPS):
            dxbc_ref[:, D_SSD + g * D_STATE: D_SSD + (g + 1) * D_STATE] = dbg[g]
            dxbc_ref[:, D_SSD + (SSD_GROUPS + g) * D_STATE: D_SSD + (SSD_GROUPS + g + 1) * D_STATE] = dcg[g]
        dproj_ref[:, COL_DT:COL_DT + SSD_HEADS] = _b(ddtr)
        ddtb_ref[...] += ddtb
        dalog_ref[...] += dalog
        ddsk_ref[...] += ddsk

    small = pl.BlockSpec((1, SSD_HEADS), lambda s, c: (0, 0))
    return pl.pallas_call(
        body, name="ssd_bwd", grid=(nseq, nch),
        in_specs=[pl.BlockSpec((CHUNK, D_SSD), lambda s, c: (rev(s, c), 0)),
                  pl.BlockSpec((CHUNK, CONV_DIM), lambda s, c: (rev(s, c), 0)),
                  pl.BlockSpec((CHUNK, D_SSD), lambda s, c: (rev(s, c), COL_Z // D_SSD)),
                  pl.BlockSpec((CHUNK, 128), lambda s, c: (rev(s, c), COL_DT // 128)),
                  pl.BlockSpec((1, D_SSD, D_STATE), lambda s, c: (rev(s, c), 0, 0)),
                  small, small, small,
                  pl.BlockSpec((1, D_SSD), lambda s, c: (0, 0))],
        out_specs=[pl.BlockSpec((CHUNK, CONV_DIM), lambda s, c: (rev(s, c), 0)),
                   pl.BlockSpec((CHUNK, N_INP), lambda s, c: (rev(s, c), 0)),
                   small, small, small,
                   pl.BlockSpec((1, D_SSD), lambda s, c: (0, 0))],
        out_shape=[jax.ShapeDtypeStruct((t, CONV_DIM), F32), jax.ShapeDtypeStruct((t, N_INP), BF16),
                   jax.ShapeDtypeStruct((1, SSD_HEADS), F32), jax.ShapeDtypeStruct((1, SSD_HEADS), F32),
                   jax.ShapeDtypeStruct((1, SSD_HEADS), F32), jax.ShapeDtypeStruct((1, D_SSD), F32)],
        scratch_shapes=[pltpu.VMEM((D_SSD, D_STATE), F32)],
        compiler_params=_cp("arbitrary", "arbitrary"),
    )(dy, xbc, proj, proj, hprev, dtb, alog, dskip, ng)


def _gmlp_chunk(gu, gv, ws, bs_cols, vg, og):
    n = gu[0].shape[0]
    mask = _tri(n, True)
    au = [_gelu(t) for t in gu]
    av = [_gelu(t) for t in gv]
    r = lax.rsqrt(sum(jnp.sum(t * t, axis=1, keepdims=True) for t in av) * (1.0 / D_GM) + EPS)
    p = []
    for h in range(GM_HEADS):
        sv = _bdot(ws[h] * mask, av[h] * r * vg[h]) + bs_cols[h]
        p.append(au[h] * sv)
    r2 = lax.rsqrt(sum(jnp.sum(t * t, axis=1, keepdims=True) for t in p) * (1.0 / D_GM) + EPS)
    return [p[h] * r2 * og[h] for h in range(GM_HEADS)]


def _gmlp_load(u_ref, v_ref, ws_ref, bst_ref, vg_ref, og_ref):
    gu = _hslices(u_ref, GM_HEAD_DIM, GM_HEADS)
    gv = _hslices(v_ref, GM_HEAD_DIM, GM_HEADS)
    ws = [ws_ref[h] for h in range(GM_HEADS)]
    bs_cols = [bst_ref[:, h:h + 1] for h in range(GM_HEADS)]
    return gu, gv, ws, bs_cols, _hslices(vg_ref, GM_HEAD_DIM, GM_HEADS), _hslices(og_ref, GM_HEAD_DIM, GM_HEADS)


def _gmlp_specs():
    return [pl.BlockSpec((CHUNK, D_GM), lambda i: (i, COL_U // D_GM)),
            pl.BlockSpec((CHUNK, D_GM), lambda i: (i, COL_V // D_GM)),
            pl.BlockSpec((GM_HEADS, CHUNK, CHUNK), lambda i: (0, 0, 0)),
            pl.BlockSpec((CHUNK, GM_HEADS), lambda i: (0, 0)),
            pl.BlockSpec((1, D_GM), lambda i: (0, 0)),
            pl.BlockSpec((1, D_GM), lambda i: (0, 0))]


def _gmlp_fwd(proj, ycat, ws, bst, vg, og):
    t = proj.shape[0]

    def body(u_ref, v_ref, ws_ref, bst_ref, vg_ref, og_ref, ycat_ref, o_ref):
        del ycat_ref
        outs = _gmlp_chunk(*_gmlp_load(u_ref, v_ref, ws_ref, bst_ref, vg_ref, og_ref))
        for h in range(GM_HEADS):
            o_ref[:, h * GM_HEAD_DIM:(h + 1) * GM_HEAD_DIM] = _b(outs[h])

    return pl.pallas_call(
        body, name="gmlp_fwd", grid=(t // CHUNK,),
        in_specs=_gmlp_specs() + [ANY],
        out_specs=pl.BlockSpec((CHUNK, D_GM), lambda i: (i, D_SSD // D_GM)),
        out_shape=jax.ShapeDtypeStruct(ycat.shape, ycat.dtype),
        input_output_aliases={6: 0},
        compiler_params=_cp("parallel"),
    )(proj, proj, ws, bst, vg, og, ycat)


def _gmlp_bwd(dy, proj, ws, bst, vg, og, dproj):
    t = proj.shape[0]
    w = GM_HEAD_DIM

    def body(dy_ref, u_ref, v_ref, ws_ref, bst_ref, vg_ref, og_ref, dproj_ref,
             dgm_ref, dws_ref, dbst_ref, dvg_ref, dog_ref):
        del dproj_ref

        @pl.when(pl.program_id(0) == 0)
        def _():
            dws_ref[...] = jnp.zeros_like(dws_ref)
            dbst_ref[...] = jnp.zeros_like(dbst_ref)
            dvg_ref[...] = jnp.zeros_like(dvg_ref)
            dog_ref[...] = jnp.zeros_like(dog_ref)

        _, vjp = jax.vjp(_gmlp_chunk, *_gmlp_load(u_ref, v_ref, ws_ref, bst_ref, vg_ref, og_ref))
        dgu, dgv, dws, dbs, dvg, dog = vjp(_hslices(dy_ref, w, GM_HEADS))
        for h in range(GM_HEADS):
            dgm_ref[:, h * w:(h + 1) * w] = _b(dgu[h])
            dgm_ref[:, D_GM + h * w: D_GM + (h + 1) * w] = _b(dgv[h])
            dws_ref[h] += dws[h]
            dbst_ref[:, h:h + 1] += dbs[h]
            dvg_ref[:, h * w:(h + 1) * w] += dvg[h]
            dog_ref[:, h * w:(h + 1) * w] += dog[h]

    return pl.pallas_call(
        body, name="gmlp_bwd", grid=(t // CHUNK,),
        in_specs=[pl.BlockSpec((CHUNK, D_GM), lambda i: (i, 1))] + _gmlp_specs() + [ANY],
        out_specs=[pl.BlockSpec((CHUNK, 2 * D_GM), lambda i: (i, COL_U // (2 * D_GM))),
                   pl.BlockSpec((GM_HEADS, CHUNK, CHUNK), lambda i: (0, 0, 0)),
                   pl.BlockSpec((CHUNK, GM_HEADS), lambda i: (0, 0)),
                   pl.BlockSpec((1, D_GM), lambda i: (0, 0)),
                   pl.BlockSpec((1, D_GM), lambda i: (0, 0))],
        out_shape=[jax.ShapeDtypeStruct(dproj.shape, dproj.dtype), jax.ShapeDtypeStruct((GM_HEADS, CHUNK, CHUNK), F32),
                   jax.ShapeDtypeStruct((CHUNK, GM_HEADS), F32), jax.ShapeDtypeStruct((1, D_GM), F32),
                   jax.ShapeDtypeStruct((1, D_GM), F32)],
        input_output_aliases={7: 0},
        compiler_params=_cp("arbitrary"),
    )(dy, proj, proj, ws, bst, vg, og, dproj)


def _local_step(x, target, mods, lw, final_g, *, nseq, big_w, grad_sink, small_sink):
    saved = []
    xin, delta, gate = x, None, None
    for l in range(DEPTH):
        w = lw[l]
        sh1, sc1, g1, sh2, sc2, g2 = mods[l]
        if delta is None:
            x0, h1 = xin, _normmod_fwd(xin, w["norm1_g"], sc1, sh1, nseq=nseq, name=f"norm1_fwd_{l}")
            w_in = big_w(l, "w_in", h1)
            proj = _matmul(h1, w_in, tb=True, name=f"mm_in_{l}")
        else:
            w_in = big_w(l, "w_in", delta)
            x0, h1, proj = _normfwd_matmul(xin, delta, gate, w["norm1_g"], sc1, sh1, w_in, nseq=nseq, name=f"mm_in_{l}")
        xbc, xbc_pre = _ssd_conv_fwd(proj, w["ssd_conv_w"], w["ssd_conv_b"], nseq=nseq)
        ycat, hprev = _ssd_fwd(xbc, proj, w["ssd_dt_bias"], w["ssd_a_log"], w["ssd_d"], w["ssd_norm_g"], nseq=nseq)
        ycat = _gmlp_fwd(proj, ycat, w["gm_ws"], w["gm_bst"], w["gm_vnorm_g"], w["gm_out_g"])
        w_out = big_w(l, "w_out", ycat)
        mix = _matmul(ycat, w_out, name=f"mm_out_{l}")
        ff_up = big_w(l, "ff_up", mix)
        x1, h2, up = _normfwd_matmul(x0, mix, g1, w["norm2_g"], sc2, sh2, ff_up, nseq=nseq, name=f"mm_up_{l}",
                                     out_dtype=BF16)
        act = _ffn_act_fwd(up, w["ff_conv_w"], w["ff_conv_b"], nseq=nseq)
        ff_down = big_w(l, "ff_down", act)
        dn = _matmul(act, ff_down, name=f"mm_down_{l}")
        saved.append(dict(x0=x0, xin_delta=delta, xin_gate=gate, h1=h1, proj=proj, xbc=xbc, xbc_pre=xbc_pre, hprev=hprev,
                          ycat=ycat, mix=mix, x1=x1, h2=h2, up=up, act=act, dn=dn,
                          w_in=w_in, w_out=w_out, ff_up=ff_up, ff_down=ff_down))
        xin, delta, gate = x1, dn, g2

    loss, dx, ddelta, dgate, dfg = _final_loss(xin, delta, gate, final_g, target, nseq=nseq)

    small, dmods = [None] * DEPTH, [None] * DEPTH
    for l in reversed(range(DEPTH)):
        w, sv = lw[l], saved[l]
        sh1, sc1, g1, sh2, sc2, g2 = mods[l]
        dg2 = dgate
        g_ff_down = _matmul(sv["act"], ddelta, ta=True, name=f"mm_down_dw_{l}", out_dtype=BF16)
        dact = _matmul(ddelta, sv["ff_down"], tb=True, name=f"mm_down_dx_{l}", out_dtype=BF16)
        dgate_ff, dval_ff, dfcw, dfcb = _ffn_act_bwd(dact, sv["up"], w["ff_conv_w"], w["ff_conv_b"], nseq=nseq)
        g_ff_up = _matmul([dgate_ff, dval_ff], sv["h2"], ta=True, name=f"mm_up_dw_{l}", out_dtype=BF16)
        dep = grad_sink(l, "ffn", dict(ff_down=g_ff_down, ff_up=g_ff_up), dval_ff)
        dx, dmix, dg1, dn2g, dsc2, dsh2 = _matmul_normbwd([dgate_ff, dval_ff], sv["ff_up"], dx, sv["x1"], sv["mix"], g1,
                                                          w["norm2_g"], sc2, nseq=nseq, name=f"mm_up_dx_{l}", dep=dep)
        g_w_out = _matmul(sv["ycat"], dmix, ta=True, name=f"mm_out_dw_{l}", out_dtype=BF16)
        dep = grad_sink(l, "w_out", dict(w_out=g_w_out), dmix)
        dycat = _matmul(dmix, sv["w_out"], tb=True, name=f"mm_out_dx_{l}", dep=dep)
        dxbc_act, dproj, ddtb, dalog, ddsk, dng = _ssd_bwd(dycat, sv["xbc"], sv["proj"], sv["hprev"], w["ssd_dt_bias"],
                                                          w["ssd_a_log"], w["ssd_d"], w["ssd_norm_g"], nseq=nseq)
        dproj, dscw, dscb = _ssd_conv_bwd(dxbc_act, sv["xbc_pre"], sv["proj"], w["ssd_conv_w"], dproj, nseq=nseq)
        dproj, dws, dbst, dvg, dog = _gmlp_bwd(dycat, sv["proj"], w["gm_ws"], w["gm_bst"], w["gm_vnorm_g"], w["gm_out_g"], dproj)
        early = dict(norm2_g=dn2g, ssd_norm_g=dng, gm_vnorm_g=dvg, gm_out_g=dog,
                     ssd_conv_w=dscw, ssd_conv_b=dscb, ff_conv_w=dfcw, ff_conv_b=dfcb,
                     ssd_dt_bias=ddtb, ssd_a_log=dalog, ssd_d=ddsk, gm_ws=dws, gm_bs=dbst.T)
        dep = small_sink(l, early, small, dmods, dfg, loss)
        g_w_in = _matmul(dproj, sv["h1"], ta=True, name=f"mm_in_dw_{l}", out_dtype=BF16, dep=dep)
        dep = grad_sink(l, "w_in", dict(w_in=g_w_in), dproj)
        dx, ddelta, dgate, dn1g, dsc1, dsh1 = _matmul_normbwd(dproj, sv["w_in"], dx, sv["x0"], sv["xin_delta"],
                                                              sv["xin_gate"], w["norm1_g"], sc1, nseq=nseq,
                                                              name=f"mm_in_dx_{l}", dep=dep)
        small[l] = dict(early, norm1_g=dn1g)
        dmods[l] = jnp.concatenate([dsh1, dsc1, dg1, dsh2, dsc2, dg2], axis=-1)[:, 0, :]
    return dx, small, dmods


def _all_gather(arrs, name, dep=None):
    n = len(arrs)
    extra = [] if dep is None else [dep]

    def body(*refs):
        ins, outs = refs[:n], refs[n + len(extra):2 * n + len(extra)]
        send_sems, recv_sems, local_sems = refs[2 * n + len(extra):]
        x, y, c = lax.axis_index("x"), lax.axis_index("y"), lax.axis_index("c")
        me, sibling = (x, y, c), (x, y, 1 - c)
        chips = [(1 - x, y), (x, 1 - y), (1 - x, 1 - y)]

        def copy(i, k, block, to, src=None):
            px, py, pc = block
            dst = outs[i].at[4 * px + 2 * py + pc]
            return pltpu.make_async_remote_copy(
                src_ref=dst if src is None else src, dst_ref=dst,
                send_sem=send_sems.at[7 * i + k], recv_sem=recv_sems.at[7 * i + k],
                device_id=to, device_id_type=MESH)

        mine = [pltpu.make_async_copy(ins[i], outs[i].at[4 * x + 2 * y + c], local_sems.at[i]) for i in range(n)]
        for cp in mine:
            cp.start()
        first = []
        for i in range(n):
            first.append(copy(i, 0, me, sibling, src=ins[i]))
            first += [copy(i, 1 + j, me, (*chip, c), src=ins[i]) for j, chip in enumerate(chips)]
        for cp in first:
            cp.start()
        passed = []
        for j, chip in enumerate(chips):
            for i in range(n):
                copy(i, 1 + j, (*chip, c), me).wait_recv()
                fwd = copy(i, 4 + j, (*chip, c), sibling)
                fwd.start()
                passed.append(fwd)
        for i in range(n):
            copy(i, 0, sibling, me).wait_recv()
            for j, chip in enumerate(chips):
                copy(i, 4 + j, (*chip, 1 - c), me).wait_recv()
        for cp in first + passed:
            cp.wait_send()
        for cp in mine:
            cp.wait()

    return pl.pallas_call(
        body, name=name,
        in_specs=[ANY] * (n + len(extra)), out_specs=[ANY] * n,
        out_shape=[jax.ShapeDtypeStruct((N_DEV,) + a.shape, a.dtype) for a in arrs],
        scratch_shapes=[pltpu.SemaphoreType.DMA((7 * n,)), pltpu.SemaphoreType.DMA((7 * n,)),
                        pltpu.SemaphoreType.DMA((n,))],
    )(*arrs, *extra)


def _exchange_sibling(arrs, name):
    n = len(arrs)

    def body(*refs):
        ins, outs = refs[:n], refs[n:2 * n]
        send_sems, recv_sems = refs[2 * n:]
        x, y, c = lax.axis_index("x"), lax.axis_index("y"), lax.axis_index("c")
        copies = []
        for i in range(n):
            for k in range(4):
                copies.append(pltpu.make_async_remote_copy(
                    src_ref=ins[i].at[2 * k + (1 - c)], dst_ref=outs[i].at[k],
                    send_sem=send_sems.at[4 * i + k], recv_sem=recv_sems.at[4 * i + k],
                    device_id=(x, y, 1 - c), device_id_type=MESH))
        for cp in copies:
            cp.start()
        for cp in copies:
            cp.wait_recv()
        for cp in copies:
            cp.wait_send()

    return pl.pallas_call(
        body, name=name,
        in_specs=[ANY] * n, out_specs=[ANY] * n,
        out_shape=[jax.ShapeDtypeStruct((4,) + a.shape[1:], a.dtype) for a in arrs],
        scratch_shapes=[pltpu.SemaphoreType.DMA((4 * n,)), pltpu.SemaphoreType.DMA((4 * n,))],
    )(*arrs)


def _exchange_chips(arrs, name):
    n = len(arrs)

    def body(*refs):
        ins, outs = refs[:n], refs[n:2 * n]
        send_sems, recv_sems = refs[2 * n:]
        x, y, c = lax.axis_index("x"), lax.axis_index("y"), lax.axis_index("c")
        chips = [(1 - x, y), (x, 1 - y), (1 - x, 1 - y)]
        copies = []
        for i in range(n):
            for j, (cx, cy) in enumerate(chips):
                copies.append(pltpu.make_async_remote_copy(
                    src_ref=ins[i].at[2 * cx + cy], dst_ref=outs[i].at[j],
                    send_sem=send_sems.at[3 * i + j], recv_sem=recv_sems.at[3 * i + j],
                    device_id=(cx, cy, c), device_id_type=MESH))
        for cp in copies:
            cp.start()
        for cp in copies:
            cp.wait_recv()
        for cp in copies:
            cp.wait_send()

    return pl.pallas_call(
        body, name=name,
        in_specs=[ANY] * n, out_specs=[ANY] * n,
        out_shape=[jax.ShapeDtypeStruct((3,) + a.shape[1:], a.dtype) for a in arrs],
        scratch_shapes=[pltpu.SemaphoreType.DMA((3 * n,)), pltpu.SemaphoreType.DMA((3 * n,))],
    )(*arrs)


def _add_sibling(a, r, pos, name):
    _, depth, rows, cols = a.shape
    tr = _tile(rows, 256) if rows % 8 == 0 else rows
    a3 = a.reshape(N_DEV * depth, rows, cols)
    r3 = r.reshape(4 * depth, rows, cols)

    def body(pos_ref, a_ref, r_ref, o_ref):
        o_ref[...] = a_ref[...] + r_ref[...]

    out = pl.pallas_call(
        body, name=name,
        grid_spec=pltpu.PrefetchScalarGridSpec(
            num_scalar_prefetch=1, grid=(4 * depth, rows // tr),
            in_specs=[pl.BlockSpec((1, tr, cols), lambda q, i, p: ((2 * (q // depth) + p[0]) * depth + q % depth, i, 0)),
                      pl.BlockSpec((1, tr, cols), lambda q, i, p: (q, i, 0))],
            out_specs=pl.BlockSpec((1, tr, cols), lambda q, i, p: (q, i, 0))),
        out_shape=jax.ShapeDtypeStruct((4 * depth, rows, cols), F32),
        compiler_params=_cp("parallel", "parallel"),
    )(pos, a3, r3)
    return out.reshape(4, depth, rows, cols)


HBM = pl.BlockSpec(memory_space=pltpu.HBM)
SEM = pl.BlockSpec(memory_space=pltpu.SEMAPHORE)
EFFECT = pltpu.SideEffectType.DATAFLOW_SIDE_EFFECTING


def _peer(k):
    x, y, c = lax.axis_index("x"), lax.axis_index("y"), lax.axis_index("c")
    return (1 - x if k & 4 else x, 1 - y if k & 2 else y, 1 - c if k & 1 else c)


ALL_PEERS = tuple(range(1, N_DEV))
OTHER_CHIPS = (2, 4, 6)


def _xc_copies(scatter, srcs, lands, send_sems, recv_sems, peers=ALL_PEERS):
    x, y, c = lax.axis_index("x"), lax.axis_index("y"), lax.axis_index("c")
    copies = []
    for i in range(len(srcs)):
        for k in peers:
            px, py, pc = _peer(k)
            src = srcs[i].at[4 * px + 2 * py + pc] if scatter else srcs[i]
            dst = lands[i].at[k - 1] if scatter else lands[i].at[4 * x + 2 * y + c]
            copies.append(pltpu.make_async_remote_copy(
                src_ref=src, dst_ref=dst, send_sem=send_sems[i].at[k - 1], recv_sem=recv_sems[i].at[k - 1],
                device_id=(px, py, pc), device_id_type=MESH))
    return copies


def _xc_start(scatter, arrs, after, name, peers=ALL_PEERS):
    n = len(arrs)
    lands = [lax.empty((N_DEV - 1,) + a.shape[1:] if scatter else (N_DEV,) + a.shape, a.dtype) for a in arrs]

    def body(*refs):
        srcs, lnd = refs[:n], refs[n:2 * n]
        send_sems, recv_sems = refs[2 * n + 1:3 * n + 1], refs[3 * n + 1:4 * n + 1]
        token = refs[6 * n + 1]
        for cp in _xc_copies(scatter, srcs, lnd, send_sems, recv_sems, peers):
            cp.start()
        token[...] = jnp.zeros_like(token)

    outs = pl.pallas_call(
        body, name=name,
        out_shape=[pltpu.SemaphoreType.DMA((N_DEV - 1,))] * (2 * n)
        + [pltpu.HBM(a.shape, a.dtype) for a in arrs] + [pltpu.HBM(a.shape, a.dtype) for a in lands]
        + [jax.ShapeDtypeStruct((8, 128), F32)],
        in_specs=[HBM] * (2 * n) + [ANY],
        out_specs=[SEM] * (2 * n) + [HBM] * (2 * n) + [pl.BlockSpec(memory_space=pltpu.VMEM)],
        input_output_aliases={i: 2 * n + i for i in range(2 * n)},
        compiler_params=pltpu.CompilerParams(has_side_effects=EFFECT),
    )(*[pltpu.with_memory_space_constraint(a, pltpu.HBM) for a in list(arrs) + lands], after)
    return outs[:n], outs[n:2 * n], outs[2 * n:3 * n], outs[3 * n:4 * n], outs[4 * n][0, 0]


def _xc_wait(scatter, send_sems, recv_sems, srcs, lands, after, name, peers=ALL_PEERS):
    n = len(srcs)

    def body(*refs):
        s_refs, l_refs = refs[:n], refs[n:2 * n]
        ss, rs = refs[2 * n:3 * n], refs[3 * n:4 * n]
        for cp in _xc_copies(scatter, s_refs, l_refs, ss, rs, peers):
            cp.wait_send()
            cp.wait_recv()

    outs = pl.pallas_call(
        body, name=name,
        out_shape=[pltpu.HBM(a.shape, a.dtype) for a in list(srcs) + list(lands)],
        in_specs=[HBM] * (2 * n) + [SEM] * (2 * n) + [ANY],
        out_specs=[HBM] * (2 * n),
        input_output_aliases={i: i for i in range(2 * n)},
        compiler_params=pltpu.CompilerParams(has_side_effects=EFFECT),
    )(*srcs, *lands, *send_sems, *recv_sems, after)
    return outs[:n], outs[n:]


def _sib_copies(zones, send_sems, recv_sems):
    x, y, c = lax.axis_index("x"), lax.axis_index("y"), lax.axis_index("c")
    copies = []
    for i in range(len(zones)):
        for q in range(N_DEV // 2):
            slot = zones[i].at[2 * q + c]
            copies.append(pltpu.make_async_remote_copy(
                src_ref=slot, dst_ref=slot, send_sem=send_sems[i].at[q], recv_sem=recv_sems[i].at[q],
                device_id=(x, y, 1 - c), device_id_type=MESH))
    return copies


def _sib_start(zones, name):
    n = len(zones)

    def body(*refs):
        for cp in _sib_copies(refs[:n], refs[n:2 * n], refs[2 * n:3 * n]):
            cp.start()

    outs = pl.pallas_call(
        body, name=name,
        out_shape=[pltpu.SemaphoreType.DMA((N_DEV // 2,))] * (2 * n) + [pltpu.HBM(a.shape, a.dtype) for a in zones],
        in_specs=[HBM] * n,
        out_specs=[SEM] * (2 * n) + [HBM] * n,
        input_output_aliases={i: 2 * n + i for i in range(n)},
        compiler_params=pltpu.CompilerParams(has_side_effects=EFFECT),
    )(*[pltpu.with_memory_space_constraint(a, pltpu.HBM) for a in zones])
    return outs[:n], outs[n:2 * n], outs[2 * n:]


def _sib_wait(send_sems, recv_sems, zones, name):
    n = len(zones)

    def body(*refs):
        for cp in _sib_copies(refs[:n], refs[n:2 * n], refs[2 * n:3 * n]):
            cp.wait_send()
            cp.wait_recv()

    return pl.pallas_call(
        body, name=name,
        out_shape=[pltpu.HBM(a.shape, a.dtype) for a in zones],
        in_specs=[HBM] * n + [SEM] * (2 * n),
        out_specs=[HBM] * n,
        input_output_aliases={i: i for i in range(n)},
        compiler_params=pltpu.CompilerParams(has_side_effects=EFFECT),
    )(*zones, *send_sems, *recv_sems)


def _adamw_math(w, g, m, v):
    m = ADAM_B1 * m + (1.0 - ADAM_B1) * g
    v = ADAM_B2 * v + (1.0 - ADAM_B2) * (g * g)
    m_hat = m / (1.0 - ADAM_B1 ** ADAM_STEP)
    v_hat = v / (1.0 - ADAM_B2 ** ADAM_STEP)
    delta = -ADAM_LR * (m_hat / (jnp.sqrt(v_hat) + ADAM_EPS) + ADAM_WD * w)
    return delta, m, v


def _adamw_sharded(parts, w, m, v, pos, name):
    depth, rows, cols = w.shape
    tr = _tile(rows, 256) if rows % 8 == 0 else rows
    npart = len(parts)

    def body(pos_ref, *refs):
        prefs = refs[:npart]
        w_ref, m_ref, v_ref, g_out, d_out, m_out, v_out = refs[npart:]
        g = prefs[0][...]
        for pr in prefs[1:]:
            g = g + pr[...]
        delta, mn, vn = _adamw_math(w_ref[...], g, m_ref[...], v_ref[...])
        g_out[...] = g
        d_out[...] = delta
        m_out[...] = mn
        v_out[...] = vn

    def part_spec(fn):
        return pl.BlockSpec((1, tr, cols), lambda l, i, p: (fn(p) * depth + l, i, 0))

    blk = pl.BlockSpec((1, tr, cols), lambda l, i, p: (l, i, 0))
    shp = jax.ShapeDtypeStruct((depth, rows, cols), F32)
    return pl.pallas_call(
        body, name=name,
        grid_spec=pltpu.PrefetchScalarGridSpec(
            num_scalar_prefetch=1, grid=(depth, rows // tr),
            in_specs=[part_spec(fn) for _, fn in parts] + [blk, blk, blk],
            out_specs=[blk, blk, blk, blk]),
        out_shape=[shp, shp, shp, shp],
        compiler_params=_cp("parallel", "parallel"),
    )(pos, *[a for a, _ in parts], w, m, v)


def _adamw_layer(parts, w, m, v, pos, layer, prev, name):
    depth, rows, cols = w.shape
    npart = len(parts)
    nprev = 0 if prev is None else 4
    if rows % 16 == 0:
        tr, tc = max(t for t in range(16, 257, 16) if rows % t == 0), cols
    else:
        tr, tc = rows, _tile(cols, 256)
    pick = (lambda i: (i, 0)) if rows % 16 == 0 else (lambda i: (0, i))

    def body(pos_ref, *refs):
        prefs = refs[:npart]
        w_ref, m_ref, v_ref = refs[npart:npart + 3]
        g_out, d_out, m_out, v_out = refs[npart + 3 + nprev:]
        g = prefs[0][...].astype(F32)
        for pr in prefs[1:]:
            g = g + pr[...].astype(F32)
        delta, mn, vn = _adamw_math(w_ref[...], g, m_ref[...], v_ref[...])
        g_out[...] = g
        d_out[...] = delta
        m_out[...] = mn
        v_out[...] = vn

    def part_spec(fn):
        return pl.BlockSpec((1, tr, tc), lambda i, p: (fn(p), *pick(i)))

    blk = pl.BlockSpec((1, tr, tc), lambda i, p: (layer, *pick(i)))
    shp = jax.ShapeDtypeStruct((depth, rows, cols), F32)
    first_prev = 1 + npart + 3
    return pl.pallas_call(
        body, name=name,
        grid_spec=pltpu.PrefetchScalarGridSpec(
            num_scalar_prefetch=1, grid=(rows // tr * (cols // tc),),
            in_specs=[part_spec(fn) for _, fn in parts] + [blk, blk, blk] + [ANY] * nprev,
            out_specs=[blk, blk, blk, blk]),
        out_shape=[shp, shp, shp, shp],
        input_output_aliases={first_prev + j: j for j in range(nprev)},
        compiler_params=_cp("parallel"),
    )(pos, *[a for a, _ in parts], w, m, v, *(prev or ()))


_P1024 = ["norm1_g", "norm2_g", "ssd_norm_g", "gm_vnorm_g", "gm_out_g"]
_P16 = ["ssd_dt_bias", "ssd_a_log", "ssd_d"]


def _adamw_small(gath, wmv):
    names = list(wmv.keys())
    classes = list(gath.keys())
    flat_in = [gath[k] for k in classes]
    for nme in names:
        flat_in += list(wmv[nme])
    out_shapes = []
    for nme in names:
        out_shapes += [jax.ShapeDtypeStruct(wmv[nme][0].shape, F32)] * 4
    out_shapes += [jax.ShapeDtypeStruct((DEPTH, SSD_CONV, CONV_DIM), F32), jax.ShapeDtypeStruct((DEPTH, FF_CONV, D_FF), F32),
                   jax.ShapeDtypeStruct((1, SSD_HEADS), F32)]
    scratch = [pltpu.VMEM(gath[k].shape[1:], F32) for k in classes]
    ncls = len(classes)

    def body(*refs):
        g_refs = dict(zip(classes, refs[:ncls]))
        pos = ncls
        w_refs = {}
        for nme in names:
            w_refs[nme] = refs[pos:pos + 3]
            pos += 3
        o_refs = {}
        for nme in names:
            o_refs[nme] = refs[pos:pos + 4]
            pos += 4
        scw_out, fcw_out, loss_out = refs[pos], refs[pos + 1], refs[pos + 2]
        s_refs = dict(zip(classes, refs[pos + 3:]))
        for k in classes:
            acc = g_refs[k][0]
            for dev in range(1, N_DEV):
                acc = acc + g_refs[k][dev]
            s_refs[k][...] = acc

        def apply(nme, grad_of):
            w_ref, m_ref, v_ref = w_refs[nme]
            g_out, d_out, m_out, v_out = o_refs[nme]
            shape = w_ref.shape
            if len(shape) == 2:
                idxs = [(slice(l, l + 1),) for l in range(shape[0])]
            elif len(shape) == 3:
                idxs = [(l,) for l in range(shape[0])]
            else:
                idxs = [(l, h) for l in range(shape[0]) for h in range(shape[1])]
            for n_i, ix in enumerate(idxs):
                g = grad_of(n_i)
                delta, mn, vn = _adamw_math(w_ref[ix], g, m_ref[ix], v_ref[ix])
                g_out[ix] = g
                d_out[ix] = delta
                m_out[ix] = mn
                v_out[ix] = vn

        s1024, s1536, s2816, s16, s128, s6144, late1024, late6144 = (s_refs[k] for k in classes)
        s1024[0:1, :] += late1024[...]
        s6144[0:late6144.shape[0], :] += late6144[...]
        for n_i, nme in enumerate(_P1024):
            apply(nme, lambda l, b=2 * n_i: s1024[b + l:b + l + 1, :])
        apply("final_g", lambda l: s1024[10:11, :])
        apply("ssd_conv_b", lambda l: s1536[8 + l:9 + l, :])
        apply("ff_conv_b", lambda l: s2816[6 + l:7 + l, :])
        for n_i, nme in enumerate(_P16):
            apply(nme, lambda l, b=2 * n_i: s16[b + l:b + l + 1, :])
        apply("gm_ws", lambda q: s128[q * CHUNK:(q + 1) * CHUNK, :])
        apply("gm_bs", lambda l: s128[2048 + 8 * l:2048 + 8 * (l + 1), :])
        apply("ada_b", lambda l: s6144[2 * l:2 * l + 1, :] + s6144[2 * l + 1:2 * l + 2, :])
        for l in range(DEPTH):
            scw_out[l] = s1536[SSD_CONV * l:SSD_CONV * (l + 1), :]
            fcw_out[l] = s2816[FF_CONV * l:FF_CONV * (l + 1), :]
        loss_out[...] = s16[2 * len(_P16):2 * len(_P16) + 1, :]

    outs = pl.pallas_call(
        body, name="adamw_small",
        out_shape=out_shapes,
        scratch_shapes=scratch,
        compiler_params=pltpu.CompilerParams(vmem_limit_bytes=VMEM_LIMIT),
    )(*flat_in)
    res = {nme: tuple(outs[4 * i:4 * i + 4]) for i, nme in enumerate(names)}
    return res, outs[-3], outs[-2], outs[-1]


_WEIGHTS = ['ada_w', 'ada_b', 'norm1_g', 'norm2_g', 'w_in', 'ssd_conv_w', 'ssd_conv_b', 'ssd_dt_bias', 'ssd_a_log',
            'ssd_d', 'ssd_norm_g', 'gm_vnorm_g', 'gm_ws', 'gm_bs', 'gm_out_g', 'w_out', 'ff_up', 'ff_conv_w',
            'ff_conv_b', 'ff_down', 'final_g']


_O_XBC, _O_DT, _O_GM = D_SSD, D_SSD + CONV_DIM, D_SSD + CONV_DIM + SSD_HEADS


_TRANSPOSED = ("w_in", "ff_up")


def _full_weight(name, g):
    full = g.reshape(g.shape[0] * g.shape[1], g.shape[2])
    if name != "w_in":
        return full
    zpad = jnp.zeros((N_INP - N_IN, full.shape[1]), full.dtype)
    return jnp.concatenate([full[_O_GM:], full[:_O_XBC], full[_O_XBC:_O_DT], full[_O_DT:_O_GM], zpad], axis=0)


def _by_owner(name, grad):
    if name == "w_in":
        grad = jnp.concatenate([grad[COL_Z:COL_XBC], grad[COL_XBC:COL_DT], grad[COL_DT:COL_DT + SSD_HEADS], grad[:COL_Z]], axis=0)
    return grad.reshape(N_DEV, grad.shape[0] // N_DEV, grad.shape[1])


def kernel(x, c, ada_w, ada_b, norm1_g, norm2_g, w_in, ssd_conv_w, ssd_conv_b, ssd_dt_bias, ssd_a_log, ssd_d, ssd_norm_g, gm_vnorm_g, gm_ws, gm_bs, gm_out_g, w_out, ff_up, ff_conv_w, ff_conv_b, ff_down, final_g, loss_target, m_ada_w, m_ada_b, m_norm1_g, m_norm2_g, m_w_in, m_ssd_conv_w, m_ssd_conv_b, m_ssd_dt_bias, m_ssd_a_log, m_ssd_d, m_ssd_norm_g, m_gm_vnorm_g, m_gm_ws, m_gm_bs, m_gm_out_g, m_w_out, m_ff_up, m_ff_conv_w, m_ff_conv_b, m_ff_down, m_final_g, v_ada_w, v_ada_b, v_norm1_g, v_norm2_g, v_w_in, v_ssd_conv_w, v_ssd_conv_b, v_ssd_dt_bias, v_ssd_a_log, v_ssd_d, v_ssd_norm_g, v_gm_vnorm_g, v_gm_ws, v_gm_bs, v_gm_out_g, v_w_out, v_ff_up, v_ff_conv_w, v_ff_conv_b, v_ff_down, v_final_g):
    given = dict(locals())
    wts = {n: given[n] for n in _WEIGHTS}
    mom = {n: given["m_" + n] for n in _WEIGHTS}
    var = {n: given["v_" + n] for n in _WEIGHTS}
    nseq, seq, d = x.shape
    ix, iy, ic = lax.axis_index("x"), lax.axis_index("y"), lax.axis_index("c")
    me = 4 * ix + 2 * iy + ic
    me_arr = me.astype(jnp.int32).reshape(1)

    for nme in _TRANSPOSED:
        wts[nme], mom[nme], var[nme] = (jnp.transpose(a, (0, 2, 1)) for a in (wts[nme], mom[nme], var[nme]))

    def shard(l, name):
        return _b(wts[name][l])

    g_scw, g_fcw, c_all = _all_gather([ssd_conv_w, ff_conv_w, c], "gather_first")
    scw_f = jnp.transpose(g_scw, (1, 2, 0, 3)).reshape(DEPTH, SSD_CONV, CONV_DIM)
    fcw_f = jnp.transpose(g_fcw, (1, 2, 0, 3)).reshape(DEPTH, FF_CONV, D_FF)
    c_all = c_all.reshape(N_DEV * nseq, d)

    n_ada = ada_w.shape[2]
    ada_b_shard = lax.dynamic_slice_in_dim(ada_b, me * n_ada, n_ada, axis=1).reshape(DEPTH, 1, n_ada)
    mod_part, c_act = _ada_fwd(c_all, ada_w, ada_b_shard)
    (mod_g,) = _all_gather([mod_part], "gather_mod")
    mod_all = jnp.transpose(mod_g, (1, 2, 0, 3)).reshape(DEPTH, N_DEV * nseq, N_MOD * d)
    mod_mine = lax.dynamic_slice_in_dim(mod_all, me * nseq, nseq, axis=1)
    mods = [[mod_mine[l, :, k * d:(k + 1) * d].reshape(nseq, 1, d) for k in range(N_MOD)] for l in range(DEPTH)]

    first_ssem, first_rsem, first_src, first_land, first_zero = _xc_start(
        False, [shard(0, "w_in")], mod_g, "ag_first_start", peers=OTHER_CHIPS)
    later = [(0, "w_out"), (0, "ff_up"), (0, "ff_down"), (1, "w_in"), (1, "w_out"), (1, "ff_up"), (1, "ff_down")]
    ag_ssem, ag_rsem, ag_src, ag_land, ag_zero = _xc_start(
        False, [shard(l, n) for l, n in later], first_zero.reshape(1, 1), "ag_start")
    ag_groups = {(0, "w_out"): [0], (0, "ff_up"): [1, 2], (1, "w_in"): [3, 4], (1, "ff_up"): [5, 6]}
    big_cache = {}

    def big_w(l, name, after):
        if (l, name) == (0, "w_in") and (l, name) not in big_cache:
            srcs, lands = _xc_wait(False, first_ssem, first_rsem, first_src, first_land, after, "ag_first_wait",
                                   peers=OTHER_CHIPS)
            zone = lax.dynamic_update_index_in_dim(lands[0], srcs[0], me, 0)
            (zone,) = _sib_wait(*_sib_start([zone], "ag_first_sib_start"), "ag_first_sib_wait")
            big_cache[(l, name)] = _full_weight(name, zone)
        if (l, name) not in big_cache:
            idx = ag_groups[(l, name)]
            pick = lambda seq_: [seq_[i] for i in idx]
            srcs, lands = _xc_wait(False, pick(ag_ssem), pick(ag_rsem), pick(ag_src), pick(ag_land), after,
                                   f"ag_wait_{l}_{name}")
            for i, src, land in zip(idx, srcs, lands):
                big_cache[later[i]] = _full_weight(later[i][1], lax.dynamic_update_index_in_dim(land, src, me, 0))
        return big_cache[(l, name)]

    lw = []
    for l in range(DEPTH):
        lw.append(dict(
            norm1_g=norm1_g[l:l + 1] + (ag_zero if l == 0 else 0.0), norm2_g=norm2_g[l:l + 1], ssd_conv_w=scw_f[l],
            ssd_conv_b=ssd_conv_b[l:l + 1], ssd_dt_bias=ssd_dt_bias[l:l + 1], ssd_a_log=ssd_a_log[l:l + 1],
            ssd_d=ssd_d[l:l + 1], ssd_norm_g=ssd_norm_g[l:l + 1], gm_vnorm_g=gm_vnorm_g[l:l + 1], gm_ws=gm_ws[l],
            gm_bst=gm_bs[l].T, gm_out_g=gm_out_g[l:l + 1], ff_conv_w=fcw_f[l], ff_conv_b=ff_conv_b[l:l + 1]))

    outs = {}
    pending = {}

    def rs_finish(l, group, after):
        names, ssem, rsem, srcs, lands = pending.pop((l, group))
        srcs, lands = _xc_wait(True, ssem, rsem, srcs, lands, after, f"rs_wait_{l}_{group}")
        for nme, own, land in zip(names, srcs, lands):
            parts = [(own, lambda p: p[0])] + [(land, lambda p, k=k: k) for k in range(N_DEV - 1)]
            outs[nme] = _adamw_layer(parts, wts[nme], mom[nme], var[nme], me_arr, l, outs.get(nme), f"adamw_{nme}_{l}")
        return outs[names[-1]][0]

    def grad_sink(l, group, grads, after):
        names = list(grads)
        ssem, rsem, srcs, lands, zero = _xc_start(True, [_by_owner(n, grads[n]) for n in names], after, f"rs_start_{l}_{group}")
        pending[(l, group)] = (names, ssem, rsem, srcs, lands)
        return zero.reshape(1, 1)

    early_gather = {}

    def small_sink(l, early, small, dmods, dfg, loss_p):
        if l > 0:
            return None
        layers = [dict(early, norm1_g=jnp.zeros((1, d), F32))] + small[1:]
        rows = lambda name: [layers[k][name] for k in range(DEPTH)]
        packed = [
            jnp.concatenate(sum([rows(n) for n in _P1024], []) + [dfg], axis=0),
            jnp.concatenate(rows("ssd_conv_w") + rows("ssd_conv_b"), axis=0),
            jnp.concatenate(rows("ff_conv_w") + rows("ff_conv_b"), axis=0),
            jnp.concatenate(sum([rows(n) for n in _P16], []) + [loss_p[:, :SSD_HEADS]], axis=0),
            jnp.concatenate([layers[k]["gm_ws"].reshape(GM_HEADS * CHUNK, CHUNK) for k in range(DEPTH)] + rows("gm_bs"), axis=0),
            jnp.concatenate([jnp.zeros((nseq, N_MOD * d), F32)] + dmods[1:], axis=0)]
        ssem, rsem, srcs, lands, zero = _xc_start(False, packed, packed[0], "small_start")
        early_gather.update(ssem=ssem, rsem=rsem, srcs=srcs, lands=lands)
        return zero.reshape(1, 1)

    grad_x, small, dmods = _local_step(
        x.reshape(nseq * seq, d), loss_target.reshape(nseq * seq, d), mods, lw, final_g.reshape(1, d), nseq=nseq,
        big_w=big_w, grad_sink=grad_sink, small_sink=small_sink)

    done = grad_x
    for l, grp in ((1, "ffn"), (1, "w_out"), (1, "w_in"), (0, "ffn"), (0, "w_out")):
        done = rs_finish(l, grp, done)
    srcs, lands = _xc_wait(False, early_gather["ssem"], early_gather["rsem"], early_gather["srcs"],
                           early_gather["lands"], done, "small_wait")
    gathered = [lax.dynamic_update_index_in_dim(land, src, me, 0) for src, land in zip(srcs, lands)]
    gathered += _all_gather([small[0]["norm1_g"], dmods[0]], "gather_late", dep=gathered[0])
    gath = dict(zip(["p1024", "p1536", "p2816", "p16", "p128", "p6144", "late1024", "late6144"], gathered))

    dmod_all = jnp.concatenate([gath["late6144"].reshape(1, N_DEV * nseq, N_MOD * d),
                                jnp.transpose(gath["p6144"].reshape(N_DEV, DEPTH, nseq, N_MOD * d)[:, 1:], (1, 0, 2, 3)).reshape(
                                    DEPTH - 1, N_DEV * nseq, N_MOD * d)], axis=0)
    small_names = _P1024 + ["final_g", "ssd_conv_b", "ff_conv_b"] + _P16 + ["gm_ws", "gm_bs", "ada_b"]
    wmv = {}
    for nme in small_names:
        if nme == "final_g":
            wmv[nme] = tuple(a.reshape(1, d) for a in (wts[nme], mom[nme], var[nme]))
        else:
            wmv[nme] = (wts[nme], mom[nme], var[nme])
    small_out, scw_full, fcw_full, loss_sum = _adamw_small(gath, wmv)
    loss = loss_sum[0, 0]
    rs_finish(0, "w_in", scw_full)
    for nme in small_names:
        outs[nme] = small_out[nme]
    outs["final_g"] = tuple(a.reshape(d) for a in outs["final_g"])

    n_scw, n_fcw = ssd_conv_w.shape[2], ff_conv_w.shape[2]
    g_scw_mine = lax.dynamic_slice_in_dim(scw_full, me * n_scw, n_scw, axis=2)
    g_fcw_mine = lax.dynamic_slice_in_dim(fcw_full, me * n_fcw, n_fcw, axis=2)
    outs["ssd_conv_w"] = _adamw_sharded([(g_scw_mine, lambda p: 0)], ssd_conv_w, m_ssd_conv_w, v_ssd_conv_w, me_arr, "adamw_ssd_conv_w")
    outs["ff_conv_w"] = _adamw_sharded([(g_fcw_mine, lambda p: 0)], ff_conv_w, m_ff_conv_w, v_ff_conv_w, me_arr, "adamw_ff_conv_w")

    dmod_cols = _b(lax.dynamic_slice_in_dim(dmod_all, me * n_ada, n_ada, axis=2))
    g_ada = jnp.stack([_matmul(c_act, dmod_cols[l], ta=True, name=f"mm_ada_dw_{l}") for l in range(DEPTH)])
    outs["ada_w"] = _adamw_sharded([(g_ada, lambda p: 0)], ada_w, m_ada_w, v_ada_w, me_arr, "adamw_ada_w")

    for nme in _TRANSPOSED:
        outs[nme] = tuple(jnp.transpose(a, (0, 2, 1)) for a in outs[nme])
    result = [loss, grad_x.reshape(nseq, seq, d)]
    for k in range(4):
        result += [outs[n][k] for n in _WEIGHTS]
    return tuple(result)
```

```python
import functools
import math

import jax
import jax.numpy as jnp
from jax import lax
from jax.experimental import pallas as pl
from jax.experimental.pallas import tpu as pltpu

F32 = jnp.float32
BF16 = jnp.bfloat16

N_DEV = 8
D_MODEL = 1024
DEPTH = 2
CHUNK = 128
SSD_HEADS = 16
SSD_HEAD_DIM = 64
SSD_GROUPS = 2
HEADS_PER_GROUP = SSD_HEADS // SSD_GROUPS
GROUP_WIDTH = HEADS_PER_GROUP * SSD_HEAD_DIM
D_STATE = 128
D_SSD = 1024
CONV_DIM = 1536
SSD_CONV = 4
GM_HEADS = 8
GM_HEAD_DIM = 128
D_GM = 1024
D_FF = 2816
FF_CONV = 3
N_IN = 4624
N_MOD = 6
EPS = 1e-6

N_INP = 5120
COL_U, COL_V, COL_Z, COL_XBC, COL_DT = 0, 1024, 2048, 3072, 4608

ADAM_LR = 0.001
ADAM_B1 = 0.9
ADAM_B2 = 0.999
ADAM_EPS = 1e-08
ADAM_WD = 0.01
ADAM_STEP = 10

VMEM_LIMIT = 56 * 1024 * 1024
MESH = pl.DeviceIdType.MESH
ANY = pl.BlockSpec(memory_space=pl.ANY)


def _cp(*sem):
    return pltpu.CompilerParams(dimension_semantics=sem, vmem_limit_bytes=VMEM_LIMIT)


def _tile(n, pref):
    if n <= pref or n % 128:
        return n
    best = 128
    for t in range(128, pref + 1, 128):
        if n % t == 0:
            best = t
    return best


def _silu(x):
    return x * jax.nn.sigmoid(x)


def _gelu(x):
    return 0.5 * x * (1.0 + lax.erf(x * (1.0 / math.sqrt(2.0))))


def _softplus(x):
    return jnp.maximum(x, 0.0) + jnp.log1p(jnp.exp(-jnp.abs(x)))


def _b(x):
    return x.astype(BF16)


_NN = (((1,), (0,)), ((), ()))
_NT = (((1,), (1,)), ((), ()))
_TN = (((0,), (0,)), ((), ()))


def _dg(a, b, dn):
    return lax.dot_general(_b(a), _b(b), dn, preferred_element_type=F32)


@jax.custom_vjp
def _bdot(a, b):
    return _dg(a, b, _NN)


def _bdot_fwd(a, b):
    return _dg(a, b, _NN), (a, b)


def _bdot_bwd(res, ct):
    a, b = res
    return _dg(ct, b, _NT), _dg(a, ct, _TN)


_bdot.defvjp(_bdot_fwd, _bdot_bwd)


@jax.custom_vjp
def _bdot_nt(a, b):
    return _dg(a, b, _NT)


def _bdot_nt_fwd(a, b):
    return _dg(a, b, _NT), (a, b)


def _bdot_nt_bwd(res, ct):
    a, b = res
    return _dg(ct, b, _NN), _dg(ct, a, _TN)


_bdot_nt.defvjp(_bdot_nt_fwd, _bdot_nt_bwd)


@jax.custom_vjp
def _bdot_tn(a, b):
    return _dg(a, b, _TN)


def _bdot_tn_fwd(a, b):
    return _dg(a, b, _TN), (a, b)


def _bdot_tn_bwd(res, ct):
    a, b = res
    return _dg(b, ct, _NT), _dg(a, ct, _NN)


_bdot_tn.defvjp(_bdot_tn_fwd, _bdot_tn_bwd)


def _tri(n, lower):
    r = lax.broadcasted_iota(jnp.int32, (n, n), 0)
    c = lax.broadcasted_iota(jnp.int32, (n, n), 1)
    return ((r >= c) if lower else (r <= c)).astype(F32)


def _eye(n):
    r = lax.broadcasted_iota(jnp.int32, (n, n), 0)
    c = lax.broadcasted_iota(jnp.int32, (n, n), 1)
    return (r == c).astype(F32)


def _hdot(a, b, dn):
    return lax.dot_general(a, b, dn, precision=lax.Precision.HIGHEST, preferred_element_type=F32)


@jax.custom_vjp
def _cumsum_rows(x):
    return _hdot(_tri(x.shape[0], True), x, _NN)


def _cumsum_rows_fwd(x):
    return _cumsum_rows(x), None


def _cumsum_rows_bwd(_, ct):
    return (_hdot(_tri(ct.shape[0], False), ct, _NN),)


_cumsum_rows.defvjp(_cumsum_rows_fwd, _cumsum_rows_bwd)


@jax.custom_vjp
def _transpose(x):
    return _hdot(_eye(x.shape[1]), x, _NT)


def _transpose_fwd(x):
    return _transpose(x), None


def _transpose_bwd(_, ct):
    return (_hdot(_eye(ct.shape[1]), ct, _NT),)


_transpose.defvjp(_transpose_fwd, _transpose_bwd)


MXU_WIDTH = 256
MATMUL_TILE_CAP = 2816
MATMUL_VMEM = 44 * 1024 * 1024


def _mxu_tiles(n):
    if n <= MATMUL_TILE_CAP or n % 128:
        return [n]
    for unit in (MXU_WIDTH, 128):
        opts = [t for t in range(unit, MATMUL_TILE_CAP + 1, unit) if n % t == 0]
        if opts:
            return opts
    return [n]


def _matmul(a, b, *, ta=False, tb=False, name, dep=None, out_dtype=F32):
    pieces = list(a) if isinstance(a, (list, tuple)) else [a]
    npc = len(pieces)
    rows, width = pieces[0].shape
    assert all(p.shape == (rows, width) for p in pieces)
    if ta:
        k_dim, m_dim = rows, width * npc
    else:
        m_dim, k_dim = rows, width * npc
    if tb:
        n_dim, kb = b.shape
    else:
        kb, n_dim = b.shape
    assert kb == k_dim, (pieces[0].shape, npc, b.shape, ta, tb)
    m_unit = width if npc > 1 and ta else m_dim
    k_unit = width if npc > 1 and not ta else k_dim
    tm = _tile(m_unit, 1536)
    tn_opts, tk_opts = _mxu_tiles(n_dim), _mxu_tiles(k_unit)
    tn, tk = tn_opts.pop(), tk_opts.pop()
    while 4 * (tm * tk + tk * tn) + 8 * tm * tn > MATMUL_VMEM:
        if tn >= tk and tn_opts:
            tn = tn_opts.pop()
        else:
            tk = tk_opts.pop()
    ni, nj, nk = m_dim // tm, n_dim // tn, k_dim // tk
    per = width // (tm if ta else tk)
    dn = (((0 if ta else 1,), (1 if tb else 0,)), ((), ()))

    a_bytes, b_bytes = m_dim * k_dim, k_dim * n_dim
    m_outer = nk > 1 or a_bytes + b_bytes * ni <= b_bytes + a_bytes * nj
    if m_outer:
        ij = lambda o, n, k: (o, n)
        grid = (ni, nj, nk)
    else:
        ij = lambda o, n, k: (n, o)
        grid = (nj, ni, nk)

    use_acc = nk > 1 and out_dtype != F32

    def body(*refs):
        a_refs, b_ref = refs[:npc], refs[npc]
        o_ref = refs[-2] if use_acc else refs[-1]
        acc_ref = refs[-1]
        k = pl.program_id(2)
        i = pl.program_id(0 if m_outer else 1)
        along = i if ta else k

        def step(a_ref):
            p = lax.dot_general(a_ref[...], b_ref[...], dn, preferred_element_type=F32)
            if nk == 1:
                o_ref[...] = p.astype(out_dtype)
            else:
                @pl.when(k == 0)
                def _():
                    acc_ref[...] = p

                @pl.when((k > 0) & (k < nk - 1 if use_acc else True))
                def _():
                    acc_ref[...] += p

                if use_acc:
                    @pl.when(k == nk - 1)
                    def _():
                        o_ref[...] = (acc_ref[...] + p).astype(out_dtype)

        if npc == 1:
            step(a_refs[0])
        else:
            for pc in range(npc):
                pl.when((along >= pc * per) & (along < (pc + 1) * per))(functools.partial(step, a_refs[pc]))

    def a_map(pc, o, n, k):
        i, _ = ij(o, n, k)
        along = i if ta else k
        if npc > 1:
            along = jnp.clip(along - pc * per, 0, per - 1)
        return (k, along) if ta else (i, along)

    def b_map(o, n, k):
        _, j = ij(o, n, k)
        return (j, k) if tb else (k, j)

    extra = [] if dep is None else [dep]
    return pl.pallas_call(
        body, name=name,
        grid=grid,
        in_specs=[pl.BlockSpec((tk, tm) if ta else (tm, tk), functools.partial(a_map, pc)) for pc in range(npc)]
        + [pl.BlockSpec((tn, tk) if tb else (tk, tn), b_map)] + [ANY] * len(extra),
        out_specs=pl.BlockSpec((tm, tn), lambda o, n, k: ij(o, n, k)),
        out_shape=jax.ShapeDtypeStruct((m_dim, n_dim), out_dtype),
        scratch_shapes=[pltpu.VMEM((tm, tn), F32)] if use_acc else [],
        compiler_params=_cp("parallel", "parallel", "arbitrary"),
    )(*pieces, b, *extra)


def _ada_fwd(c_all, ada_w, ada_b_shard):
    depth, d, n = ada_w.shape
    nb = c_all.shape[0]

    def body(c_ref, w_ref, b_ref, o_ref, ca_ref):
        ca = _silu(c_ref[...])
        ca_ref[...] = _b(ca)
        o_ref[0] = _dg(ca, w_ref[0], _NN) + b_ref[0]

    return pl.pallas_call(
        body, name="ada_fwd",
        grid=(depth,),
        in_specs=[pl.BlockSpec((nb, d), lambda l: (0, 0)),
                  pl.BlockSpec((1, d, n), lambda l: (l, 0, 0)),
                  pl.BlockSpec((1, 1, n), lambda l: (l, 0, 0))],
        out_specs=[pl.BlockSpec((1, nb, n), lambda l: (l, 0, 0)),
                   pl.BlockSpec((nb, d), lambda l: (0, 0))],
        out_shape=[jax.ShapeDtypeStruct((depth, nb, n), F32), jax.ShapeDtypeStruct((nb, d), BF16)],
        compiler_params=_cp("arbitrary"),
    )(c_all, ada_w, ada_b_shard)


def _fold(acc):
    return jnp.sum(acc, axis=0, keepdims=True)


def _rinv(x):
    return lax.rsqrt(jnp.sum(x * x, axis=-1, keepdims=True) * (1.0 / D_MODEL) + EPS)


def _rms_bwd(a, xhat, rinv):
    return rinv * (a - xhat * (jnp.sum(a * xhat, axis=-1, keepdims=True) * (1.0 / D_MODEL)))


def _row_tile(seq):
    return min(seq, 256)


def _normmod_fwd(xin, delta, gate, g, sc, sh, *, nseq, name):
    t, d = xin.shape
    seq = t // nseq
    tr = _row_tile(seq)
    nt = seq // tr
    has_delta = delta is not None
    row = pl.BlockSpec((tr, d), lambda s, i: (s * nt + i, 0))
    per_seq = pl.BlockSpec((1, 1, d), lambda s, i: (s, 0, 0))
    vec = pl.BlockSpec((1, d), lambda s, i: (0, 0))

    def body(*refs):
        if has_delta:
            xin_ref, delta_ref, gate_ref, g_ref, sc_ref, sh_ref, x_ref, h_ref = refs
            x = xin_ref[...] + gate_ref[0] * delta_ref[...]
            x_ref[...] = x
        else:
            xin_ref, g_ref, sc_ref, sh_ref, h_ref = refs
            x = xin_ref[...]
        h_ref[...] = _b(x * _rinv(x) * (g_ref[...] * (1.0 + sc_ref[0])) + sh_ref[0])

    h_shape = jax.ShapeDtypeStruct((t, d), BF16)
    if has_delta:
        return pl.pallas_call(
            body, name=name, grid=(nseq, nt),
            in_specs=[row, row, per_seq, vec, per_seq, per_seq],
            out_specs=[row, row],
            out_shape=[jax.ShapeDtypeStruct((t, d), F32), h_shape],
            compiler_params=_cp("parallel", "parallel"),
        )(xin, delta, gate, g, sc, sh)
    h = pl.pallas_call(
        body, name=name, grid=(nseq, nt),
        in_specs=[row, vec, per_seq, per_seq],
        out_specs=row, out_shape=h_shape,
        compiler_params=_cp("parallel", "parallel"),
    )(xin, g, sc, sh)
    return xin, h


NORM_TM = 512


def _matmul_normbwd(a, b, dxo, x, delta, gate, g, sc, *, nseq, name, dep=None):
    pieces = list(a) if isinstance(a, (list, tuple)) else [a]
    npc = len(pieces)
    t, width = pieces[0].shape
    k_dim, d = width * npc, b.shape[1]
    assert b.shape[0] == k_dim and all(p.shape == (t, width) for p in pieces)
    seq = t // nseq
    tm = min(NORM_TM, seq)
    per_seq_tiles = seq // tm
    tk = _mxu_tiles(width if npc > 1 else k_dim).pop()
    nk, per = k_dim // tk, width // tk
    has_delta = delta is not None
    extra = [] if dep is None else [dep]

    def body(*refs):
        a_refs, b_ref = refs[:npc], refs[npc]
        dxo_ref, x_ref = refs[npc + 1], refs[npc + 2]
        pos = npc + 3
        if has_delta:
            delta_ref, gate_ref = refs[pos], refs[pos + 1]
            pos += 2
        g_ref, sc_ref = refs[pos], refs[pos + 1]
        pos += 2 + len(extra)
        if has_delta:
            dx_ref, dd_ref, dgate_ref, dg_ref, dsc_ref, dsh_ref = refs[pos:pos + 6]
        else:
            dx_ref, dg_ref, dsc_ref, dsh_ref = refs[pos:pos + 4]
        acc_ref = refs[-1]
        i, k = pl.program_id(0), pl.program_id(1)

        def norm_bwd(dh_v):
            g_v, one_sc = g_ref[...], 1.0 + sc_ref[0]
            x_v = x_ref[...]
            rinv = _rinv(x_v)
            xhat = x_v * rinv
            dx = dxo_ref[...] + _rms_bwd(dh_v * (g_v * one_sc), xhat, rinv)
            dx_ref[...] = dx

            @pl.when(i == 0)
            def _():
                dg_ref[...] = jnp.zeros_like(dg_ref)

            @pl.when(i % per_seq_tiles == 0)
            def _():
                dsc_ref[...] = jnp.zeros_like(dsc_ref)
                dsh_ref[...] = jnp.zeros_like(dsh_ref)
                if has_delta:
                    dgate_ref[...] = jnp.zeros_like(dgate_ref)

            t_sum = _fold(dh_v * xhat)
            dg_ref[...] += t_sum * one_sc
            dsc_ref[0] += t_sum * g_v
            dsh_ref[0] += _fold(dh_v)
            if has_delta:
                dd_ref[...] = _b(dx * gate_ref[0])
                dgate_ref[0] += _fold(dx * delta_ref[...])

        def step(a_ref):
            p = lax.dot_general(a_ref[...], b_ref[...], _NN, preferred_element_type=F32)
            if nk == 1:
                norm_bwd(p)
            else:
                @pl.when(k == 0)
                def _():
                    acc_ref[...] = p

                @pl.when((k > 0) & (k < nk - 1))
                def _():
                    acc_ref[...] += p

                @pl.when(k == nk - 1)
                def _():
                    norm_bwd(acc_ref[...] + p)

        if npc == 1:
            step(a_refs[0])
        else:
            for pc in range(npc):
                pl.when((k >= pc * per) & (k < (pc + 1) * per))(functools.partial(step, a_refs[pc]))

    def a_map(pc, i, k):
        return (i, jnp.clip(k - pc * per, 0, per - 1) if npc > 1 else k)

    row = pl.BlockSpec((tm, d), lambda i, k: (i, 0))
    per_seq = pl.BlockSpec((1, 1, d), lambda i, k: (i // per_seq_tiles, 0, 0))
    vec = pl.BlockSpec((1, d), lambda i, k: (0, 0))
    shp = lambda *s, dt=F32: jax.ShapeDtypeStruct(s, dt)
    in_specs = [pl.BlockSpec((tm, tk), functools.partial(a_map, pc)) for pc in range(npc)]
    in_specs += [pl.BlockSpec((tk, d), lambda i, k: (k, 0)), row, row]
    operands = [*pieces, b, dxo, x]
    if has_delta:
        in_specs += [row, per_seq]
        operands += [delta, gate]
    in_specs += [vec, per_seq] + [ANY] * len(extra)
    operands += [g, sc, *extra]
    if has_delta:
        out_specs = [row, row, per_seq, vec, per_seq, per_seq]
        out_shape = [shp(t, d), shp(t, d, dt=BF16), shp(nseq, 1, d), shp(1, d), shp(nseq, 1, d), shp(nseq, 1, d)]
    else:
        out_specs = [row, vec, per_seq, per_seq]
        out_shape = [shp(t, d), shp(1, d), shp(nseq, 1, d), shp(nseq, 1, d)]
    outs = pl.pallas_call(
        body, name=name, grid=(t // tm, nk),
        in_specs=in_specs, out_specs=out_specs, out_shape=out_shape,
        scratch_shapes=[pltpu.VMEM((tm, d), F32)],
        compiler_params=_cp("arbitrary", "arbitrary"),
    )(*operands)
    if has_delta:
        return tuple(outs)
    dx, dg, dsc, dsh = outs
    return dx, None, None, dg, dsc, dsh


def _final_loss(xin, delta, gate, fg, target, *, nseq):
    t, d = xin.shape
    seq = t // nseq
    tr = _row_tile(seq)
    nt = seq // tr
    row = pl.BlockSpec((tr, d), lambda s, i: (s * nt + i, 0))
    per_seq = pl.BlockSpec((1, 1, d), lambda s, i: (s, 0, 0))
    vec = pl.BlockSpec((1, d), lambda s, i: (0, 0))

    def body(xin_ref, delta_ref, gate_ref, fg_ref, tgt_ref, loss_ref, dx_ref, dd_ref, dgate_ref, dfg_ref):
        s, i = pl.program_id(0), pl.program_id(1)
        fg_v, gate_v = fg_ref[...], gate_ref[0]
        dl = delta_ref[...]
        x = xin_ref[...] + gate_v * dl
        rinv = _rinv(x)
        xhat = x * rinv
        err = xhat * fg_v - tgt_ref[...]
        dx = _rms_bwd(err * fg_v * (1.0 / d), xhat, rinv)
        dx_ref[...] = dx
        dd_ref[...] = _b(dx * gate_v)
        acc_l, acc_f, acc_g = err * err, err * xhat, dx * dl

        @pl.when((s == 0) & (i == 0))
        def _():
            loss_ref[...] = jnp.zeros_like(loss_ref)
            dfg_ref[...] = jnp.zeros_like(dfg_ref)

        @pl.when(i == 0)
        def _():
            dgate_ref[...] = jnp.zeros_like(dgate_ref)

        loss_ref[...] += jnp.sum(acc_l) * (0.5 / d)
        dfg_ref[...] += _fold(acc_f) * (1.0 / d)
        dgate_ref[0] += _fold(acc_g)

    return pl.pallas_call(
        body, name="final_loss", grid=(nseq, nt),
        in_specs=[row, row, per_seq, vec, row],
        out_specs=[pl.BlockSpec((1, 128), lambda s, i: (0, 0)), row, row, per_seq, vec],
        out_shape=[jax.ShapeDtypeStruct((1, 128), F32), jax.ShapeDtypeStruct((t, d), F32),
                   jax.ShapeDtypeStruct((t, d), BF16), jax.ShapeDtypeStruct((nseq, 1, d), F32),
                   jax.ShapeDtypeStruct((1, d), F32)],
        compiler_params=_cp("arbitrary", "arbitrary"),
    )(xin, delta, gate, fg, target)


CONV_TC = 256
CONV_LANES = 128
CONV_ROWS = 64
CONV_HALO = 8


def _conv_slabs(seq, fn):
    def step(i, carry):
        r0 = pl.multiple_of(i * CONV_ROWS, CONV_ROWS)
        for h in range(CONV_TC // CONV_LANES):
            fn(r0, slice(h * CONV_LANES, (h + 1) * CONV_LANES))
        return carry

    lax.fori_loop(0, seq // CONV_ROWS, step, 0)


def _slab(ref, r0, cols, seq):
    after = ref[pl.ds(pl.multiple_of(jnp.minimum(r0 + CONV_ROWS, seq - CONV_HALO), CONV_HALO), CONV_HALO), cols]
    return jnp.concatenate([ref[pl.ds(r0, CONV_ROWS), cols], jnp.where(r0 + CONV_ROWS < seq, after, 0.0)], axis=0)


def _conv_block(x, w_ref, b_ref):
    kw = w_ref.shape[0]
    rows = lax.broadcasted_iota(jnp.int32, x.shape, 0)
    y = b_ref[...] + w_ref[kw - 1:kw, :] * x
    for j in range(1, kw):
        y = y + w_ref[kw - 1 - j:kw - j, :] * jnp.where(rows >= j, pltpu.roll(x, j, 0), 0.0)
    return y


def _conv_block_bwd(dy, x, w_ref, dw_ref, db_ref):
    kw = w_ref.shape[0]
    n = x.shape[0]
    rows = lax.broadcasted_iota(jnp.int32, x.shape, 0)
    dx = w_ref[kw - 1:kw, :] * dy
    dw_ref[kw - 1:kw, :] += jnp.sum(dy * x, axis=0, keepdims=True)
    for j in range(1, kw):
        dy_j = jnp.where(rows < n - j, pltpu.roll(dy, n - j, 0), 0.0)
        dx = dx + w_ref[kw - 1 - j:kw - j, :] * dy_j
        dw_ref[kw - 1 - j:kw - j, :] += jnp.sum(dy_j * x, axis=0, keepdims=True)
    db_ref[...] += jnp.sum(dy, axis=0, keepdims=True)
    return dx


def _conv_bwd(dy_ext, x, w_ref, dw_ref, db_ref, cols):
    kw = w_ref.shape[0]
    n = dy_ext.shape[0]
    dy = dy_ext[:CONV_ROWS]
    dx = w_ref[kw - 1:kw, cols] * dy
    dw_ref[kw - 1:kw, cols] += jnp.sum(dy * x, axis=0, keepdims=True)
    for j in range(1, kw):
        dy_j = pltpu.roll(dy_ext, n - j, 0)[:CONV_ROWS]
        dx = dx + w_ref[kw - 1 - j:kw - j, cols] * dy_j
        dw_ref[kw - 1 - j:kw - j, cols] += jnp.sum(dy_j * x, axis=0, keepdims=True)
    db_ref[:, cols] += jnp.sum(dy, axis=0, keepdims=True)
    return dx


def _dsilu(pre):
    sg = jax.nn.sigmoid(pre)
    return pre * sg, sg * (1.0 + pre * (1.0 - sg))


def _ssd_conv_fwd(proj, w, b, *, nseq):
    t = proj.shape[0]
    seq = t // nseq
    nb = CONV_DIM // CONV_TC
    off = COL_XBC // CONV_TC

    def body(x_ref, w_ref, b_ref, o_ref, pre_ref):
        pre = _conv_block(x_ref[...], w_ref, b_ref)
        pre_ref[...] = pre
        o_ref[...] = _silu(pre)

    col = pl.BlockSpec((seq, CONV_TC), lambda j, s: (s, j))
    return pl.pallas_call(
        body, name="ssd_conv_fwd", grid=(nb, nseq),
        in_specs=[pl.BlockSpec((seq, CONV_TC), lambda j, s: (s, off + j)),
                  pl.BlockSpec((SSD_CONV, CONV_TC), lambda j, s: (0, j)),
                  pl.BlockSpec((1, CONV_TC), lambda j, s: (0, j))],
        out_specs=[col, col],
        out_shape=[jax.ShapeDtypeStruct((t, CONV_DIM), F32)] * 2,
        compiler_params=_cp("parallel", "parallel"),
    )(proj, w, b)


def _ssd_conv_bwd(dact, pre, proj, w, dproj, *, nseq):
    t = proj.shape[0]
    seq = t // nseq
    nb = CONV_DIM // CONV_TC
    off = COL_XBC // CONV_TC

    def body(da_ref, pre_ref, x_ref, w_ref, dproj_ref, dx_ref, dw_ref, db_ref):
        del dproj_ref

        @pl.when(pl.program_id(1) == 0)
        def _():
            dw_ref[...] = jnp.zeros_like(dw_ref)
            db_ref[...] = jnp.zeros_like(db_ref)

        def slab(r0, cols):
            _, dsilu = _dsilu(_slab(pre_ref, r0, cols, seq))
            dpre_ext = _slab(da_ref, r0, cols, seq) * dsilu
            x = x_ref[pl.ds(r0, CONV_ROWS), cols]
            dx_ref[pl.ds(r0, CONV_ROWS), cols] = _b(_conv_bwd(dpre_ext, x, w_ref, dw_ref, db_ref, cols))

        _conv_slabs(seq, slab)

    return pl.pallas_call(
        body, name="ssd_conv_bwd", grid=(nb, nseq),
        in_specs=[pl.BlockSpec((seq, CONV_TC), lambda j, s: (s, j)),
                  pl.BlockSpec((seq, CONV_TC), lambda j, s: (s, j)),
                  pl.BlockSpec((seq, CONV_TC), lambda j, s: (s, off + j)),
                  pl.BlockSpec((SSD_CONV, CONV_TC), lambda j, s: (0, j)),
                  ANY],
        out_specs=[pl.BlockSpec((seq, CONV_TC), lambda j, s: (s, off + j)),
                   pl.BlockSpec((SSD_CONV, CONV_TC), lambda j, s: (0, j)),
                   pl.BlockSpec((1, CONV_TC), lambda j, s: (0, j))],
        out_shape=[jax.ShapeDtypeStruct(dproj.shape, dproj.dtype), jax.ShapeDtypeStruct((SSD_CONV, CONV_DIM), F32),
                   jax.ShapeDtypeStruct((1, CONV_DIM), F32)],
        input_output_aliases={4: 0},
        compiler_params=_cp("parallel", "arbitrary"),
    )(dact, pre, proj, w, dproj)


def _ffn_act_fwd(up, w, b, *, nseq):
    t = up.shape[0]
    seq = t // nseq
    nb = D_FF // CONV_TC

    def body(g_ref, v_ref, w_ref, b_ref, o_ref):
        o_ref[...] = _b(_silu(_conv_block(g_ref[...].astype(F32), w_ref, b_ref)) * v_ref[...].astype(F32))

    col = pl.BlockSpec((seq, CONV_TC), lambda j, s: (s, j))
    return pl.pallas_call(
        body, name="ffn_act_fwd", grid=(nb, nseq),
        in_specs=[col,
                  pl.BlockSpec((seq, CONV_TC), lambda j, s: (s, nb + j)),
                  pl.BlockSpec((FF_CONV, CONV_TC), lambda j, s: (0, j)),
                  pl.BlockSpec((1, CONV_TC), lambda j, s: (0, j))],
        out_specs=col,
        out_shape=jax.ShapeDtypeStruct((t, D_FF), BF16),
        compiler_params=_cp("parallel", "parallel"),
    )(up, up, w, b)


def _ffn_act_bwd(dact, up, w, b, *, nseq):
    t = up.shape[0]
    seq = t // nseq
    nb = D_FF // CONV_TC

    def body(da_ref, g_ref, v_ref, w_ref, b_ref, dg_ref, dv_ref, dw_ref, db_ref):
        @pl.when(pl.program_id(1) == 0)
        def _():
            dw_ref[...] = jnp.zeros_like(dw_ref)
            db_ref[...] = jnp.zeros_like(db_ref)

        gate = g_ref[...].astype(F32)
        silu, dsilu = _dsilu(_conv_block(gate, w_ref, b_ref))
        da = da_ref[...].astype(F32)
        dv_ref[...] = _b(da * silu)
        dg_ref[...] = _b(_conv_block_bwd(da * v_ref[...].astype(F32) * dsilu, gate, w_ref, dw_ref, db_ref))

    col = pl.BlockSpec((seq, CONV_TC), lambda j, s: (s, j))
    return pl.pallas_call(
        body, name="ffn_act_bwd", grid=(nb, nseq),
        in_specs=[col, col,
                  pl.BlockSpec((seq, CONV_TC), lambda j, s: (s, nb + j)),
                  pl.BlockSpec((FF_CONV, CONV_TC), lambda j, s: (0, j)),
                  pl.BlockSpec((1, CONV_TC), lambda j, s: (0, j))],
        out_specs=[col, col,
                   pl.BlockSpec((FF_CONV, CONV_TC), lambda j, s: (0, j)),
                   pl.BlockSpec((1, CONV_TC), lambda j, s: (0, j))],
        out_shape=[jax.ShapeDtypeStruct((t, D_FF), BF16), jax.ShapeDtypeStruct((t, D_FF), BF16),
                   jax.ShapeDtypeStruct((FF_CONV, D_FF), F32), jax.ShapeDtypeStruct((1, D_FF), F32)],
        compiler_params=_cp("parallel", "arbitrary"),
    )(dact, up, up, w, b)


SSD_PAIRS = SSD_HEADS // 2
PAIR_W = 2 * SSD_HEAD_DIM
PAIRS_PER_GROUP = SSD_PAIRS // SSD_GROUPS


def _ssd_chunk(xs, bg, cg, dtr, z, hp, dtb, alog, dskip, ng):
    n = dtr.shape[0]
    dt = _softplus(dtr + dtb)
    cs = _cumsum_rows(dt * (-jnp.exp(alog)))
    cs_t = _transpose(cs)
    lane = lax.broadcasted_iota(jnp.int32, (1, SSD_HEADS), 1)
    sub = lax.broadcasted_iota(jnp.int32, (SSD_HEADS, 1), 0)
    row = lax.broadcasted_iota(jnp.int32, (n, 1), 0)
    causal = lax.broadcasted_iota(jnp.int32, (n, n), 0) >= lax.broadcasted_iota(jnp.int32, (n, n), 1)
    future = jnp.where(causal, 0.0, -1e30)
    first = lax.broadcasted_iota(jnp.int32, (1, PAIR_W), 1) < SSD_HEAD_DIM
    first_rows = lax.broadcasted_iota(jnp.int32, (PAIR_W, 1), 0) < SSD_HEAD_DIM
    first_f = first.astype(F32)
    cb = [_bdot_nt(cg[g], bg[g]) for g in range(SSD_GROUPS)]
    ys, hn = [], []
    for p in range(SSD_PAIRS):
        g = p // PAIRS_PER_GROUP
        col, decay, last = [], [], []
        for h in (2 * p, 2 * p + 1):
            oh = (lane == h).astype(F32)
            cs_h = jnp.sum(cs * oh, axis=1, keepdims=True)
            cs_row = jnp.sum(cs_t * (sub == h).astype(F32), axis=0, keepdims=True)
            col.append((jnp.sum(dt * oh, axis=1, keepdims=True), cs_h, jnp.sum(dskip * oh, axis=1, keepdims=True)))
            last.append(jnp.sum(jnp.where(row == n - 1, cs_h, 0.0), axis=0, keepdims=True))
            decay.append(jnp.exp(cs_h - cs_row + future))
        pair = lambda a, b: jnp.where(first, a, b)
        dt_p = pair(col[0][0], col[1][0])
        cs_p = pair(col[0][1], col[1][1])
        last_p = pair(last[0], last[1])
        xc = xs[p] * dt_p
        y = _bdot(cb[g] * decay[0], xc * first_f) + _bdot(cb[g] * decay[1], xc * (1.0 - first_f))
        y = y + _bdot_nt(cg[g], hp[p]) * jnp.exp(cs_p)
        y = y + pair(col[0][2], col[1][2]) * xs[p]
        keep = jnp.where(first_rows, jnp.exp(last[0]), jnp.exp(last[1]))
        hn.append(keep * hp[p] + _bdot_tn(xc * jnp.exp(last_p - cs_p), bg[g]))
        ys.append(y * _silu(z[p]))
    outs = []
    for g in range(SSD_GROUPS):
        ps = range(g * PAIRS_PER_GROUP, (g + 1) * PAIRS_PER_GROUP)
        ms = sum(jnp.sum(ys[p] * ys[p], axis=1, keepdims=True) for p in ps) * (1.0 / GROUP_WIDTH)
        r = lax.rsqrt(ms + EPS)
        outs += [ys[p] * r * ng[p] for p in ps]
    return outs, hn


def _hslices(ref, width, count, base=0):
    return [ref[:, base + k * width: base + (k + 1) * width] for k in range(count)]


def _ssd_load(xbc_ref, z_ref, dt_ref, ng_ref):
    xs = _hslices(xbc_ref, PAIR_W, SSD_PAIRS)
    bg = _hslices(xbc_ref, D_STATE, SSD_GROUPS, D_SSD)
    cg = _hslices(xbc_ref, D_STATE, SSD_GROUPS, D_SSD + SSD_GROUPS * D_STATE)
    z = _hslices(z_ref, PAIR_W, SSD_PAIRS)
    ng = _hslices(ng_ref, PAIR_W, SSD_PAIRS)
    return xs, bg, cg, dt_ref[:, 0:SSD_HEADS], z, ng


def _ssd_specs(nch):
    rowi = lambda s, c: s * nch + c
    return [pl.BlockSpec((CHUNK, CONV_DIM), lambda s, c: (rowi(s, c), 0)),
            pl.BlockSpec((CHUNK, D_SSD), lambda s, c: (rowi(s, c), COL_Z // D_SSD)),
            pl.BlockSpec((CHUNK, 128), lambda s, c: (rowi(s, c), COL_DT // 128)),
            pl.BlockSpec((1, SSD_HEADS), lambda s, c: (0, 0)),
            pl.BlockSpec((1, SSD_HEADS), lambda s, c: (0, 0)),
            pl.BlockSpec((1, SSD_HEADS), lambda s, c: (0, 0)),
            pl.BlockSpec((1, D_SSD), lambda s, c: (0, 0))]


def _ssd_fwd(xbc, proj, dtb, alog, dskip, ng, *, nseq):
    t = proj.shape[0]
    nch = t // nseq // CHUNK
    hd = PAIR_W

    def body(xbc_ref, z_ref, dt_ref, dtb_ref, alog_ref, dsk_ref, ng_ref, y_ref, hp_ref, h_ref):
        @pl.when(pl.program_id(1) == 0)
        def _():
            h_ref[...] = jnp.zeros_like(h_ref)

        xs, bg, cg, dtr, z, ngs = _ssd_load(xbc_ref, z_ref, dt_ref, ng_ref)
        hp_ref[0] = h_ref[...]
        hp = [h_ref[h * hd:(h + 1) * hd, :] for h in range(SSD_PAIRS)]
        outs, hn = _ssd_chunk(xs, bg, cg, dtr, z, hp, dtb_ref[...], alog_ref[...], dsk_ref[...], ngs)
        for h in range(SSD_PAIRS):
            y_ref[:, h * hd:(h + 1) * hd] = _b(outs[h])
            h_ref[h * hd:(h + 1) * hd, :] = hn[h]

    return pl.pallas_call(
        body, name="ssd_fwd", grid=(nseq, nch),
        in_specs=_ssd_specs(nch),
        out_specs=[pl.BlockSpec((CHUNK, D_SSD), lambda s, c: (s * nch + c, 0)),
                   pl.BlockSpec((1, D_SSD, D_STATE), lambda s, c: (s * nch + c, 0, 0))],
        out_shape=[jax.ShapeDtypeStruct((t, D_SSD + D_GM), BF16),
                   jax.ShapeDtypeStruct((t // CHUNK, D_SSD, D_STATE), F32)],
        scratch_shapes=[pltpu.VMEM((D_SSD, D_STATE), F32)],
        compiler_params=_cp("arbitrary", "arbitrary"),
    )(xbc, proj, proj, dtb, alog, dskip, ng)


def _ssd_bwd(dy, xbc, proj, hprev, dtb, alog, dskip, ng, *, nseq):
    t = proj.shape[0]
    nch = t // nseq // CHUNK
    hd = PAIR_W
    rev = lambda s, c: s * nch + (nch - 1 - c)

    def body(dy_ref, xbc_ref, z_ref, dt_ref, hp_ref, dtb_ref, alog_ref, dsk_ref, ng_ref,
             dxbc_ref, dproj_ref, ddtb_ref, dalog_ref, ddsk_ref, dng_ref, dh_ref):
        first = (pl.program_id(0) == 0) & (pl.program_id(1) == 0)

        @pl.when(pl.program_id(1) == 0)
        def _():
            dh_ref[...] = jnp.zeros_like(dh_ref)

        @pl.when(first)
        def _():
            ddtb_ref[...] = jnp.zeros_like(ddtb_ref)
            dalog_ref[...] = jnp.zeros_like(dalog_ref)
            ddsk_ref[...] = jnp.zeros_like(ddsk_ref)
            dng_ref[...] = jnp.zeros_like(dng_ref)

        xs, bg, cg, dtr, z, ngs = _ssd_load(xbc_ref, z_ref, dt_ref, ng_ref)
        hp = [hp_ref[0, h * hd:(h + 1) * hd, :] for h in range(SSD_PAIRS)]
        _, vjp = jax.vjp(_ssd_chunk, xs, bg, cg, dtr, z, hp, dtb_ref[...], alog_ref[...], dsk_ref[...], ngs)
        douts = [dy_ref[:, h * hd:(h + 1) * hd] for h in range(SSD_PAIRS)]
        dhn = [dh_ref[h * hd:(h + 1) * hd, :] for h in range(SSD_PAIRS)]
        dxs, dbg, dcg, ddtr, dz, dhp, ddtb, dalog, ddsk, dngs = vjp((douts, dhn))
        dproj_ref[:, :COL_Z] = jnp.zeros((CHUNK, COL_Z), BF16)
        dproj_ref[:, COL_XBC:] = jnp.zeros((CHUNK, N_INP - COL_XBC), BF16)
        for h in range(SSD_PAIRS):
            dxbc_ref[:, h * hd:(h + 1) * hd] = dxs[h]
            dproj_ref[:, COL_Z + h * hd: COL_Z + (h + 1) * hd] = _b(dz[h])
            dh_ref[h * hd:(h + 1) * hd, :] = dhp[h]
            dng_ref[:, h * hd:(h + 1) * hd] += dngs[h]
        for g in range(SSD_GROUPS):
            dxbc_ref[:, D_SSD + g * D_STATE: D_SSD + (g + 1) * D_STATE] = dbg[g]
            dxbc_ref[:, D_SSD + (SSD_GROUPS + g) * D_STATE: D_SSD + (SSD_GROUPS + g + 1) * D_STATE] = dcg[g]
        dproj_ref[:, COL_DT:COL_DT + SSD_HEADS] = _b(ddtr)
        ddtb_ref[...] += ddtb
        dalog_ref[...] += dalog
        ddsk_ref[...] += ddsk

    small = pl.BlockSpec((1, SSD_HEADS), lambda s, c: (0, 0))
    return pl.pallas_call(
        body, name="ssd_bwd", grid=(nseq, nch),
        in_specs=[pl.BlockSpec((CHUNK, D_SSD), lambda s, c: (rev(s, c), 0)),
                  pl.BlockSpec((CHUNK, CONV_DIM), lambda s, c: (rev(s, c), 0)),
                  pl.BlockSpec((CHUNK, D_SSD), lambda s, c: (rev(s, c), COL_Z // D_SSD)),
                  pl.BlockSpec((CHUNK, 128), lambda s, c: (rev(s, c), COL_DT // 128)),
                  pl.BlockSpec((1, D_SSD, D_STATE), lambda s, c: (rev(s, c), 0, 0)),
                  small, small, small,
                  pl.BlockSpec((1, D_SSD), lambda s, c: (0, 0))],
        out_specs=[pl.BlockSpec((CHUNK, CONV_DIM), lambda s, c: (rev(s, c), 0)),
                   pl.BlockSpec((CHUNK, N_INP), lambda s, c: (rev(s, c), 0)),
                   small, small, small,
                   pl.BlockSpec((1, D_SSD), lambda s, c: (0, 0))],
        out_shape=[jax.ShapeDtypeStruct((t, CONV_DIM), F32), jax.ShapeDtypeStruct((t, N_INP), BF16),
                   jax.ShapeDtypeStruct((1, SSD_HEADS), F32), jax.ShapeDtypeStruct((1, SSD_HEADS), F32),
                   jax.ShapeDtypeStruct((1, SSD_HEADS), F32), jax.ShapeDtypeStruct((1, D_SSD), F32)],
        scratch_shapes=[pltpu.VMEM((D_SSD, D_STATE), F32)],
        compiler_params=_cp("arbitrary", "arbitrary"),
    )(dy, xbc, proj, proj, hprev, dtb, alog, dskip, ng)


def _gmlp_chunk(gu, gv, ws, bs_cols, vg, og):
    n = gu[0].shape[0]
    mask = _tri(n, True)
    au = [_gelu(t) for t in gu]
    av = [_gelu(t) for t in gv]
    r = lax.rsqrt(sum(jnp.sum(t * t, axis=1, keepdims=True) for t in av) * (1.0 / D_GM) + EPS)
    p = []
    for h in range(GM_HEADS):
        sv = _bdot(ws[h] * mask, av[h] * r * vg[h]) + bs_cols[h]
        p.append(au[h] * sv)
    r2 = lax.rsqrt(sum(jnp.sum(t * t, axis=1, keepdims=True) for t in p) * (1.0 / D_GM) + EPS)
    return [p[h] * r2 * og[h] for h in range(GM_HEADS)]


def _gmlp_load(u_ref, v_ref, ws_ref, bst_ref, vg_ref, og_ref):
    gu = _hslices(u_ref, GM_HEAD_DIM, GM_HEADS)
    gv = _hslices(v_ref, GM_HEAD_DIM, GM_HEADS)
    ws = [ws_ref[h] for h in range(GM_HEADS)]
    bs_cols = [bst_ref[:, h:h + 1] for h in range(GM_HEADS)]
    return gu, gv, ws, bs_cols, _hslices(vg_ref, GM_HEAD_DIM, GM_HEADS), _hslices(og_ref, GM_HEAD_DIM, GM_HEADS)


def _gmlp_specs():
    return [pl.BlockSpec((CHUNK, D_GM), lambda i: (i, COL_U // D_GM)),
            pl.BlockSpec((CHUNK, D_GM), lambda i: (i, COL_V // D_GM)),
            pl.BlockSpec((GM_HEADS, CHUNK, CHUNK), lambda i: (0, 0, 0)),
            pl.BlockSpec((CHUNK, GM_HEADS), lambda i: (0, 0)),
            pl.BlockSpec((1, D_GM), lambda i: (0, 0)),
            pl.BlockSpec((1, D_GM), lambda i: (0, 0))]


def _gmlp_fwd(proj, ycat, ws, bst, vg, og):
    t = proj.shape[0]

    def body(u_ref, v_ref, ws_ref, bst_ref, vg_ref, og_ref, ycat_ref, o_ref):
        del ycat_ref
        outs = _gmlp_chunk(*_gmlp_load(u_ref, v_ref, ws_ref, bst_ref, vg_ref, og_ref))
        for h in range(GM_HEADS):
            o_ref[:, h * GM_HEAD_DIM:(h + 1) * GM_HEAD_DIM] = _b(outs[h])

    return pl.pallas_call(
        body, name="gmlp_fwd", grid=(t // CHUNK,),
        in_specs=_gmlp_specs() + [ANY],
        out_specs=pl.BlockSpec((CHUNK, D_GM), lambda i: (i, D_SSD // D_GM)),
        out_shape=jax.ShapeDtypeStruct(ycat.shape, ycat.dtype),
        input_output_aliases={6: 0},
        compiler_params=_cp("parallel"),
    )(proj, proj, ws, bst, vg, og, ycat)


def _gmlp_bwd(dy, proj, ws, bst, vg, og, dproj):
    t = proj.shape[0]
    w = GM_HEAD_DIM

    def body(dy_ref, u_ref, v_ref, ws_ref, bst_ref, vg_ref, og_ref, dproj_ref,
             dgm_ref, dws_ref, dbst_ref, dvg_ref, dog_ref):
        del dproj_ref

        @pl.when(pl.program_id(0) == 0)
        def _():
            dws_ref[...] = jnp.zeros_like(dws_ref)
            dbst_ref[...] = jnp.zeros_like(dbst_ref)
            dvg_ref[...] = jnp.zeros_like(dvg_ref)
            dog_ref[...] = jnp.zeros_like(dog_ref)

        _, vjp = jax.vjp(_gmlp_chunk, *_gmlp_load(u_ref, v_ref, ws_ref, bst_ref, vg_ref, og_ref))
        dgu, dgv, dws, dbs, dvg, dog = vjp(_hslices(dy_ref, w, GM_HEADS))
        for h in range(GM_HEADS):
            dgm_ref[:, h * w:(h + 1) * w] = _b(dgu[h])
            dgm_ref[:, D_GM + h * w: D_GM + (h + 1) * w] = _b(dgv[h])
            dws_ref[h] += dws[h]
            dbst_ref[:, h:h + 1] += dbs[h]
            dvg_ref[:, h * w:(h + 1) * w] += dvg[h]
            dog_ref[:, h * w:(h + 1) * w] += dog[h]

    return pl.pallas_call(
        body, name="gmlp_bwd", grid=(t // CHUNK,),
        in_specs=[pl.BlockSpec((CHUNK, D_GM), lambda i: (i, 1))] + _gmlp_specs() + [ANY],
        out_specs=[pl.BlockSpec((CHUNK, 2 * D_GM), lambda i: (i, COL_U // (2 * D_GM))),
                   pl.BlockSpec((GM_HEADS, CHUNK, CHUNK), lambda i: (0, 0, 0)),
                   pl.BlockSpec((CHUNK, GM_HEADS), lambda i: (0, 0)),
                   pl.BlockSpec((1, D_GM), lambda i: (0, 0)),
                   pl.BlockSpec((1, D_GM), lambda i: (0, 0))],
        out_shape=[jax.ShapeDtypeStruct(dproj.shape, dproj.dtype), jax.ShapeDtypeStruct((GM_HEADS, CHUNK, CHUNK), F32),
                   jax.ShapeDtypeStruct((CHUNK, GM_HEADS), F32), jax.ShapeDtypeStruct((1, D_GM), F32),
                   jax.ShapeDtypeStruct((1, D_GM), F32)],
        input_output_aliases={7: 0},
        compiler_params=_cp("arbitrary"),
    )(dy, proj, proj, ws, bst, vg, og, dproj)


def _local_step(x, target, mods, lw, final_g, *, nseq, big_w, grad_sink, small_sink):
    saved = []
    xin, delta, gate = x, None, None
    for l in range(DEPTH):
        w = lw[l]
        sh1, sc1, g1, sh2, sc2, g2 = mods[l]
        x0, h1 = _normmod_fwd(xin, delta, gate, w["norm1_g"], sc1, sh1, nseq=nseq, name=f"norm1_fwd_{l}")
        w_in = big_w(l, "w_in", h1)
        proj = _matmul(h1, w_in, tb=True, name=f"mm_in_{l}")
        xbc, xbc_pre = _ssd_conv_fwd(proj, w["ssd_conv_w"], w["ssd_conv_b"], nseq=nseq)
        ycat, hprev = _ssd_fwd(xbc, proj, w["ssd_dt_bias"], w["ssd_a_log"], w["ssd_d"], w["ssd_norm_g"], nseq=nseq)
        ycat = _gmlp_fwd(proj, ycat, w["gm_ws"], w["gm_bst"], w["gm_vnorm_g"], w["gm_out_g"])
        w_out = big_w(l, "w_out", ycat)
        mix = _matmul(ycat, w_out, name=f"mm_out_{l}")
        x1, h2 = _normmod_fwd(x0, mix, g1, w["norm2_g"], sc2, sh2, nseq=nseq, name=f"norm2_fwd_{l}")
        ff_up = big_w(l, "ff_up", h2)
        up = _matmul(h2, ff_up, tb=True, name=f"mm_up_{l}", out_dtype=BF16)
        act = _ffn_act_fwd(up, w["ff_conv_w"], w["ff_conv_b"], nseq=nseq)
        ff_down = big_w(l, "ff_down", act)
        dn = _matmul(act, ff_down, name=f"mm_down_{l}")
        saved.append(dict(x0=x0, xin_delta=delta, xin_gate=gate, h1=h1, proj=proj, xbc=xbc, xbc_pre=xbc_pre, hprev=hprev,
                          ycat=ycat, mix=mix, x1=x1, h2=h2, up=up, act=act, dn=dn,
                          w_in=w_in, w_out=w_out, ff_up=ff_up, ff_down=ff_down))
        xin, delta, gate = x1, dn, g2

    loss, dx, ddelta, dgate, dfg = _final_loss(xin, delta, gate, final_g, target, nseq=nseq)

    small, dmods = [None] * DEPTH, [None] * DEPTH
    for l in reversed(range(DEPTH)):
        w, sv = lw[l], saved[l]
        sh1, sc1, g1, sh2, sc2, g2 = mods[l]
        dg2 = dgate
        g_ff_down = _matmul(sv["act"], ddelta, ta=True, name=f"mm_down_dw_{l}", out_dtype=BF16)
        dact = _matmul(ddelta, sv["ff_down"], tb=True, name=f"mm_down_dx_{l}", out_dtype=BF16)
        dgate_ff, dval_ff, dfcw, dfcb = _ffn_act_bwd(dact, sv["up"], w["ff_conv_w"], w["ff_conv_b"], nseq=nseq)
        g_ff_up = _matmul([dgate_ff, dval_ff], sv["h2"], ta=True, name=f"mm_up_dw_{l}", out_dtype=BF16)
        dep = grad_sink(l, "ffn", dict(ff_down=g_ff_down, ff_up=g_ff_up), dval_ff)
        dx, dmix, dg1, dn2g, dsc2, dsh2 = _matmul_normbwd([dgate_ff, dval_ff], sv["ff_up"], dx, sv["x1"], sv["mix"], g1,
                                                          w["norm2_g"], sc2, nseq=nseq, name=f"mm_up_dx_{l}", dep=dep)
        g_w_out = _matmul(sv["ycat"], dmix, ta=True, name=f"mm_out_dw_{l}", out_dtype=BF16)
        dep = grad_sink(l, "w_out", dict(w_out=g_w_out), dmix)
        dycat = _matmul(dmix, sv["w_out"], tb=True, name=f"mm_out_dx_{l}", dep=dep)
        dxbc_act, dproj, ddtb, dalog, ddsk, dng = _ssd_bwd(dycat, sv["xbc"], sv["proj"], sv["hprev"], w["ssd_dt_bias"],
                                                          w["ssd_a_log"], w["ssd_d"], w["ssd_norm_g"], nseq=nseq)
        dproj, dscw, dscb = _ssd_conv_bwd(dxbc_act, sv["xbc_pre"], sv["proj"], w["ssd_conv_w"], dproj, nseq=nseq)
        dproj, dws, dbst, dvg, dog = _gmlp_bwd(dycat, sv["proj"], w["gm_ws"], w["gm_bst"], w["gm_vnorm_g"], w["gm_out_g"], dproj)
        early = dict(norm2_g=dn2g, ssd_norm_g=dng, gm_vnorm_g=dvg, gm_out_g=dog,
                     ssd_conv_w=dscw, ssd_conv_b=dscb, ff_conv_w=dfcw, ff_conv_b=dfcb,
                     ssd_dt_bias=ddtb, ssd_a_log=dalog, ssd_d=ddsk, gm_ws=dws, gm_bs=dbst.T)
        dep = small_sink(l, early, small, dmods, dfg, loss)
        g_w_in = _matmul(dproj, sv["h1"], ta=True, name=f"mm_in_dw_{l}", out_dtype=BF16, dep=dep)
        dep = grad_sink(l, "w_in", dict(w_in=g_w_in), dproj)
        dx, ddelta, dgate, dn1g, dsc1, dsh1 = _matmul_normbwd(dproj, sv["w_in"], dx, sv["x0"], sv["xin_delta"],
                                                              sv["xin_gate"], w["norm1_g"], sc1, nseq=nseq,
                                                              name=f"mm_in_dx_{l}", dep=dep)
        small[l] = dict(early, norm1_g=dn1g)
        dmods[l] = jnp.concatenate([dsh1, dsc1, dg1, dsh2, dsc2, dg2], axis=-1)[:, 0, :]
    return dx, small, dmods


def _all_gather(arrs, name, dep=None):
    n = len(arrs)
    extra = [] if dep is None else [dep]

    def body(*refs):
        ins, outs = refs[:n], refs[n + len(extra):2 * n + len(extra)]
        send_sems, recv_sems, local_sems = refs[2 * n + len(extra):]
        x, y, c = lax.axis_index("x"), lax.axis_index("y"), lax.axis_index("c")
        me, sibling = (x, y, c), (x, y, 1 - c)
        chips = [(1 - x, y), (x, 1 - y), (1 - x, 1 - y)]

        def copy(i, k, block, to, src=None):
            px, py, pc = block
            dst = outs[i].at[4 * px + 2 * py + pc]
            return pltpu.make_async_remote_copy(
                src_ref=dst if src is None else src, dst_ref=dst,
                send_sem=send_sems.at[7 * i + k], recv_sem=recv_sems.at[7 * i + k],
                device_id=to, device_id_type=MESH)

        mine = [pltpu.make_async_copy(ins[i], outs[i].at[4 * x + 2 * y + c], local_sems.at[i]) for i in range(n)]
        for cp in mine:
            cp.start()
        first = []
        for i in range(n):
            first.append(copy(i, 0, me, sibling, src=ins[i]))
            first += [copy(i, 1 + j, me, (*chip, c), src=ins[i]) for j, chip in enumerate(chips)]
        for cp in first:
            cp.start()
        passed = []
        for j, chip in enumerate(chips):
            for i in range(n):
                copy(i, 1 + j, (*chip, c), me).wait_recv()
                fwd = copy(i, 4 + j, (*chip, c), sibling)
                fwd.start()
                passed.append(fwd)
        for i in range(n):
            copy(i, 0, sibling, me).wait_recv()
            for j, chip in enumerate(chips):
                copy(i, 4 + j, (*chip, 1 - c), me).wait_recv()
        for cp in first + passed:
            cp.wait_send()
        for cp in mine:
            cp.wait()

    return pl.pallas_call(
        body, name=name,
        in_specs=[ANY] * (n + len(extra)), out_specs=[ANY] * n,
        out_shape=[jax.ShapeDtypeStruct((N_DEV,) + a.shape, a.dtype) for a in arrs],
        scratch_shapes=[pltpu.SemaphoreType.DMA((7 * n,)), pltpu.SemaphoreType.DMA((7 * n,)),
                        pltpu.SemaphoreType.DMA((n,))],
    )(*arrs, *extra)


HBM = pl.BlockSpec(memory_space=pltpu.HBM)
SEM = pl.BlockSpec(memory_space=pltpu.SEMAPHORE)
EFFECT = pltpu.SideEffectType.DATAFLOW_SIDE_EFFECTING


def _peer(k):
    x, y, c = lax.axis_index("x"), lax.axis_index("y"), lax.axis_index("c")
    return (1 - x if k & 4 else x, 1 - y if k & 2 else y, 1 - c if k & 1 else c)


ALL_PEERS = tuple(range(1, N_DEV))
OTHER_CHIPS = (2, 4, 6)


def _xc_copies(scatter, srcs, lands, send_sems, recv_sems, peers=ALL_PEERS):
    x, y, c = lax.axis_index("x"), lax.axis_index("y"), lax.axis_index("c")
    copies = []
    for i in range(len(srcs)):
        for k in peers:
            px, py, pc = _peer(k)
            src = srcs[i].at[4 * px + 2 * py + pc] if scatter else srcs[i]
            dst = lands[i].at[k - 1] if scatter else lands[i].at[4 * x + 2 * y + c]
            copies.append(pltpu.make_async_remote_copy(
                src_ref=src, dst_ref=dst, send_sem=send_sems[i].at[k - 1], recv_sem=recv_sems[i].at[k - 1],
                device_id=(px, py, pc), device_id_type=MESH))
    return copies


def _xc_start(scatter, arrs, after, name, peers=ALL_PEERS):
    n = len(arrs)
    lands = [lax.empty((N_DEV - 1,) + a.shape[1:] if scatter else (N_DEV,) + a.shape, a.dtype) for a in arrs]

    def body(*refs):
        srcs, lnd = refs[:n], refs[n:2 * n]
        send_sems, recv_sems = refs[2 * n + 1:3 * n + 1], refs[3 * n + 1:4 * n + 1]
        token = refs[6 * n + 1]
        for cp in _xc_copies(scatter, srcs, lnd, send_sems, recv_sems, peers):
            cp.start()
        token[...] = jnp.zeros_like(token)

    outs = pl.pallas_call(
        body, name=name,
        out_shape=[pltpu.SemaphoreType.DMA((N_DEV - 1,))] * (2 * n)
        + [pltpu.HBM(a.shape, a.dtype) for a in arrs] + [pltpu.HBM(a.shape, a.dtype) for a in lands]
        + [jax.ShapeDtypeStruct((8, 128), F32)],
        in_specs=[HBM] * (2 * n) + [ANY],
        out_specs=[SEM] * (2 * n) + [HBM] * (2 * n) + [pl.BlockSpec(memory_space=pltpu.VMEM)],
        input_output_aliases={i: 2 * n + i for i in range(2 * n)},
        compiler_params=pltpu.CompilerParams(has_side_effects=EFFECT),
    )(*[pltpu.with_memory_space_constraint(a, pltpu.HBM) for a in list(arrs) + lands], after)
    return outs[:n], outs[n:2 * n], outs[2 * n:3 * n], outs[3 * n:4 * n], outs[4 * n][0, 0]


def _xc_wait(scatter, send_sems, recv_sems, srcs, lands, after, name, peers=ALL_PEERS):
    n = len(srcs)

    def body(*refs):
        s_refs, l_refs = refs[:n], refs[n:2 * n]
        ss, rs = refs[2 * n:3 * n], refs[3 * n:4 * n]
        for cp in _xc_copies(scatter, s_refs, l_refs, ss, rs, peers):
            cp.wait_send()
            cp.wait_recv()

    outs = pl.pallas_call(
        body, name=name,
        out_shape=[pltpu.HBM(a.shape, a.dtype) for a in list(srcs) + list(lands)],
        in_specs=[HBM] * (2 * n) + [SEM] * (2 * n) + [ANY],
        out_specs=[HBM] * (2 * n),
        input_output_aliases={i: i for i in range(2 * n)},
        compiler_params=pltpu.CompilerParams(has_side_effects=EFFECT),
    )(*srcs, *lands, *send_sems, *recv_sems, after)
    return outs[:n], outs[n:]


def _sib_copies(zones, send_sems, recv_sems):
    x, y, c = lax.axis_index("x"), lax.axis_index("y"), lax.axis_index("c")
    copies = []
    for i in range(len(zones)):
        for q in range(N_DEV // 2):
            slot = zones[i].at[2 * q + c]
            copies.append(pltpu.make_async_remote_copy(
                src_ref=slot, dst_ref=slot, send_sem=send_sems[i].at[q], recv_sem=recv_sems[i].at[q],
                device_id=(x, y, 1 - c), device_id_type=MESH))
    return copies


def _sib_start(zones, name):
    n = len(zones)

    def body(*refs):
        for cp in _sib_copies(refs[:n], refs[n:2 * n], refs[2 * n:3 * n]):
            cp.start()

    outs = pl.pallas_call(
        body, name=name,
        out_shape=[pltpu.SemaphoreType.DMA((N_DEV // 2,))] * (2 * n) + [pltpu.HBM(a.shape, a.dtype) for a in zones],
        in_specs=[HBM] * n,
        out_specs=[SEM] * (2 * n) + [HBM] * n,
        input_output_aliases={i: 2 * n + i for i in range(n)},
        compiler_params=pltpu.CompilerParams(has_side_effects=EFFECT),
    )(*[pltpu.with_memory_space_constraint(a, pltpu.HBM) for a in zones])
    return outs[:n], outs[n:2 * n], outs[2 * n:]


def _sib_wait(send_sems, recv_sems, zones, name):
    n = len(zones)

    def body(*refs):
        for cp in _sib_copies(refs[:n], refs[n:2 * n], refs[2 * n:3 * n]):
            cp.wait_send()
            cp.wait_recv()

    return pl.pallas_call(
        body, name=name,
        out_shape=[pltpu.HBM(a.shape, a.dtype) for a in zones],
        in_specs=[HBM] * n + [SEM] * (2 * n),
        out_specs=[HBM] * n,
        input_output_aliases={i: i for i in range(n)},
        compiler_params=pltpu.CompilerParams(has_side_effects=EFFECT),
    )(*zones, *send_sems, *recv_sems)


def _adamw_math(w, g, m, v):
    m = ADAM_B1 * m + (1.0 - ADAM_B1) * g
    v = ADAM_B2 * v + (1.0 - ADAM_B2) * (g * g)
    m_hat = m / (1.0 - ADAM_B1 ** ADAM_STEP)
    v_hat = v / (1.0 - ADAM_B2 ** ADAM_STEP)
    delta = -ADAM_LR * (m_hat / (jnp.sqrt(v_hat) + ADAM_EPS) + ADAM_WD * w)
    return delta, m, v


def _adamw_sharded(parts, w, m, v, pos, name):
    depth, rows, cols = w.shape
    tr = _tile(rows, 256) if rows % 8 == 0 else rows
    npart = len(parts)

    def body(pos_ref, *refs):
        prefs = refs[:npart]
        w_ref, m_ref, v_ref, g_out, d_out, m_out, v_out = refs[npart:]
        g = prefs[0][...]
        for pr in prefs[1:]:
            g = g + pr[...]
        delta, mn, vn = _adamw_math(w_ref[...], g, m_ref[...], v_ref[...])
        g_out[...] = g
        d_out[...] = delta
        m_out[...] = mn
        v_out[...] = vn

    def part_spec(fn):
        return pl.BlockSpec((1, tr, cols), lambda l, i, p: (fn(p) * depth + l, i, 0))

    blk = pl.BlockSpec((1, tr, cols), lambda l, i, p: (l, i, 0))
    shp = jax.ShapeDtypeStruct((depth, rows, cols), F32)
    return pl.pallas_call(
        body, name=name,
        grid_spec=pltpu.PrefetchScalarGridSpec(
            num_scalar_prefetch=1, grid=(depth, rows // tr),
            in_specs=[part_spec(fn) for _, fn in parts] + [blk, blk, blk],
            out_specs=[blk, blk, blk, blk]),
        out_shape=[shp, shp, shp, shp],
        compiler_params=_cp("parallel", "parallel"),
    )(pos, *[a for a, _ in parts], w, m, v)


def _adamw_layer(parts, w, m, v, pos, layer, prev, name):
    depth, rows, cols = w.shape
    npart = len(parts)
    nprev = 0 if prev is None else 4
    if rows % 16 == 0:
        tr, tc = max(t for t in range(16, 257, 16) if rows % t == 0), cols
    else:
        tr, tc = rows, _tile(cols, 256)
    pick = (lambda i: (i, 0)) if rows % 16 == 0 else (lambda i: (0, i))

    def body(pos_ref, *refs):
        prefs = refs[:npart]
        w_ref, m_ref, v_ref = refs[npart:npart + 3]
        g_out, d_out, m_out, v_out = refs[npart + 3 + nprev:]
        g = prefs[0][...].astype(F32)
        for pr in prefs[1:]:
            g = g + pr[...].astype(F32)
        delta, mn, vn = _adamw_math(w_ref[...], g, m_ref[...], v_ref[...])
        g_out[...] = g
        d_out[...] = delta
        m_out[...] = mn
        v_out[...] = vn

    def part_spec(fn):
        return pl.BlockSpec((1, tr, tc), lambda i, p: (fn(p), *pick(i)))

    blk = pl.BlockSpec((1, tr, tc), lambda i, p: (layer, *pick(i)))
    shp = jax.ShapeDtypeStruct((depth, rows, cols), F32)
    first_prev = 1 + npart + 3
    return pl.pallas_call(
        body, name=name,
        grid_spec=pltpu.PrefetchScalarGridSpec(
            num_scalar_prefetch=1, grid=(rows // tr * (cols // tc),),
            in_specs=[part_spec(fn) for _, fn in parts] + [blk, blk, blk] + [ANY] * nprev,
            out_specs=[blk, blk, blk, blk]),
        out_shape=[shp, shp, shp, shp],
        input_output_aliases={first_prev + j: j for j in range(nprev)},
        compiler_params=_cp("parallel"),
    )(pos, *[a for a, _ in parts], w, m, v, *(prev or ()))


_P1024 = ["norm1_g", "norm2_g", "ssd_norm_g", "gm_vnorm_g", "gm_out_g"]
_P16 = ["ssd_dt_bias", "ssd_a_log", "ssd_d"]


def _adamw_small(gath, wmv):
    names = list(wmv.keys())
    classes = list(gath.keys())
    flat_in = [gath[k] for k in classes]
    for nme in names:
        flat_in += list(wmv[nme])
    out_shapes = []
    for nme in names:
        out_shapes += [jax.ShapeDtypeStruct(wmv[nme][0].shape, F32)] * 4
    out_shapes += [jax.ShapeDtypeStruct((DEPTH, SSD_CONV, CONV_DIM), F32), jax.ShapeDtypeStruct((DEPTH, FF_CONV, D_FF), F32),
                   jax.ShapeDtypeStruct((1, SSD_HEADS), F32)]
    scratch = [pltpu.VMEM(gath[k].shape[1:], F32) for k in classes]
    ncls = len(classes)

    def body(*refs):
        g_refs = dict(zip(classes, refs[:ncls]))
        pos = ncls
        w_refs = {}
        for nme in names:
            w_refs[nme] = refs[pos:pos + 3]
            pos += 3
        o_refs = {}
        for nme in names:
            o_refs[nme] = refs[pos:pos + 4]
            pos += 4
        scw_out, fcw_out, loss_out = refs[pos], refs[pos + 1], refs[pos + 2]
        s_refs = dict(zip(classes, refs[pos + 3:]))
        for k in classes:
            acc = g_refs[k][0]
            for dev in range(1, N_DEV):
                acc = acc + g_refs[k][dev]
            s_refs[k][...] = acc

        def apply(nme, grad_of):
            w_ref, m_ref, v_ref = w_refs[nme]
            g_out, d_out, m_out, v_out = o_refs[nme]
            shape = w_ref.shape
            if len(shape) == 2:
                idxs = [(slice(l, l + 1),) for l in range(shape[0])]
            elif len(shape) == 3:
                idxs = [(l,) for l in range(shape[0])]
            else:
                idxs = [(l, h) for l in range(shape[0]) for h in range(shape[1])]
            for n_i, ix in enumerate(idxs):
                g = grad_of(n_i)
                delta, mn, vn = _adamw_math(w_ref[ix], g, m_ref[ix], v_ref[ix])
                g_out[ix] = g
                d_out[ix] = delta
                m_out[ix] = mn
                v_out[ix] = vn

        s1024, s1536, s2816, s16, s128, s6144, late1024, late6144 = (s_refs[k] for k in classes)
        s1024[0:1, :] += late1024[...]
        s6144[0:late6144.shape[0], :] += late6144[...]
        for n_i, nme in enumerate(_P1024):
            apply(nme, lambda l, b=2 * n_i: s1024[b + l:b + l + 1, :])
        apply("final_g", lambda l: s1024[10:11, :])
        apply("ssd_conv_b", lambda l: s1536[8 + l:9 + l, :])
        apply("ff_conv_b", lambda l: s2816[6 + l:7 + l, :])
        for n_i, nme in enumerate(_P16):
            apply(nme, lambda l, b=2 * n_i: s16[b + l:b + l + 1, :])
        apply("gm_ws", lambda q: s128[q * CHUNK:(q + 1) * CHUNK, :])
        apply("gm_bs", lambda l: s128[2048 + 8 * l:2048 + 8 * (l + 1), :])
        apply("ada_b", lambda l: s6144[2 * l:2 * l + 1, :] + s6144[2 * l + 1:2 * l + 2, :])
        for l in range(DEPTH):
            scw_out[l] = s1536[SSD_CONV * l:SSD_CONV * (l + 1), :]
            fcw_out[l] = s2816[FF_CONV * l:FF_CONV * (l + 1), :]
        loss_out[...] = s16[2 * len(_P16):2 * len(_P16) + 1, :]

    outs = pl.pallas_call(
        body, name="adamw_small",
        out_shape=out_shapes,
        scratch_shapes=scratch,
        compiler_params=pltpu.CompilerParams(vmem_limit_bytes=VMEM_LIMIT),
    )(*flat_in)
    res = {nme: tuple(outs[4 * i:4 * i + 4]) for i, nme in enumerate(names)}
    return res, outs[-3], outs[-2], outs[-1]


_WEIGHTS = ['ada_w', 'ada_b', 'norm1_g', 'norm2_g', 'w_in', 'ssd_conv_w', 'ssd_conv_b', 'ssd_dt_bias', 'ssd_a_log',
            'ssd_d', 'ssd_norm_g', 'gm_vnorm_g', 'gm_ws', 'gm_bs', 'gm_out_g', 'w_out', 'ff_up', 'ff_conv_w',
            'ff_conv_b', 'ff_down', 'final_g']


_O_XBC, _O_DT, _O_GM = D_SSD, D_SSD + CONV_DIM, D_SSD + CONV_DIM + SSD_HEADS


_TRANSPOSED = ("w_in", "ff_up")


def _full_weight(name, g):
    full = g.reshape(g.shape[0] * g.shape[1], g.shape[2])
    if name != "w_in":
        return full
    zpad = jnp.zeros((N_INP - N_IN, full.shape[1]), full.dtype)
    return jnp.concatenate([full[_O_GM:], full[:_O_XBC], full[_O_XBC:_O_DT], full[_O_DT:_O_GM], zpad], axis=0)


def _by_owner(name, grad):
    if name == "w_in":
        grad = jnp.concatenate([grad[COL_Z:COL_XBC], grad[COL_XBC:COL_DT], grad[COL_DT:COL_DT + SSD_HEADS], grad[:COL_Z]], axis=0)
    return grad.reshape(N_DEV, grad.shape[0] // N_DEV, grad.shape[1])


def kernel(x, c, ada_w, ada_b, norm1_g, norm2_g, w_in, ssd_conv_w, ssd_conv_b, ssd_dt_bias, ssd_a_log, ssd_d, ssd_norm_g, gm_vnorm_g, gm_ws, gm_bs, gm_out_g, w_out, ff_up, ff_conv_w, ff_conv_b, ff_down, final_g, loss_target, m_ada_w, m_ada_b, m_norm1_g, m_norm2_g, m_w_in, m_ssd_conv_w, m_ssd_conv_b, m_ssd_dt_bias, m_ssd_a_log, m_ssd_d, m_ssd_norm_g, m_gm_vnorm_g, m_gm_ws, m_gm_bs, m_gm_out_g, m_w_out, m_ff_up, m_ff_conv_w, m_ff_conv_b, m_ff_down, m_final_g, v_ada_w, v_ada_b, v_norm1_g, v_norm2_g, v_w_in, v_ssd_conv_w, v_ssd_conv_b, v_ssd_dt_bias, v_ssd_a_log, v_ssd_d, v_ssd_norm_g, v_gm_vnorm_g, v_gm_ws, v_gm_bs, v_gm_out_g, v_w_out, v_ff_up, v_ff_conv_w, v_ff_conv_b, v_ff_down, v_final_g):
    given = dict(locals())
    wts = {n: given[n] for n in _WEIGHTS}
    mom = {n: given["m_" + n] for n in _WEIGHTS}
    var = {n: given["v_" + n] for n in _WEIGHTS}
    nseq, seq, d = x.shape
    ix, iy, ic = lax.axis_index("x"), lax.axis_index("y"), lax.axis_index("c")
    me = 4 * ix + 2 * iy + ic
    me_arr = me.astype(jnp.int32).reshape(1)

    for nme in _TRANSPOSED:
        wts[nme], mom[nme], var[nme] = (jnp.transpose(a, (0, 2, 1)) for a in (wts[nme], mom[nme], var[nme]))

    def shard(l, name):
        return _b(wts[name][l])

    g_scw, g_fcw, c_all = _all_gather([ssd_conv_w, ff_conv_w, c], "gather_first")
    scw_f = jnp.transpose(g_scw, (1, 2, 0, 3)).reshape(DEPTH, SSD_CONV, CONV_DIM)
    fcw_f = jnp.transpose(g_fcw, (1, 2, 0, 3)).reshape(DEPTH, FF_CONV, D_FF)
    c_all = c_all.reshape(N_DEV * nseq, d)

    n_ada = ada_w.shape[2]
    ada_b_shard = lax.dynamic_slice_in_dim(ada_b, me * n_ada, n_ada, axis=1).reshape(DEPTH, 1, n_ada)
    mod_part, c_act = _ada_fwd(c_all, ada_w, ada_b_shard)
    (mod_g,) = _all_gather([mod_part], "gather_mod")
    mod_all = jnp.transpose(mod_g, (1, 2, 0, 3)).reshape(DEPTH, N_DEV * nseq, N_MOD * d)
    mod_mine = lax.dynamic_slice_in_dim(mod_all, me * nseq, nseq, axis=1)
    mods = [[mod_mine[l, :, k * d:(k + 1) * d].reshape(nseq, 1, d) for k in range(N_MOD)] for l in range(DEPTH)]

    first_ssem, first_rsem, first_src, first_land, first_zero = _xc_start(
        False, [shard(0, "w_in")], mod_g, "ag_first_start", peers=OTHER_CHIPS)
    later = [(0, "w_out"), (0, "ff_up"), (0, "ff_down"), (1, "w_in"), (1, "w_out"), (1, "ff_up"), (1, "ff_down")]
    ag_ssem, ag_rsem, ag_src, ag_land, ag_zero = _xc_start(
        False, [shard(l, n) for l, n in later], first_zero.reshape(1, 1), "ag_start")
    ag_groups = {(0, "w_out"): [0], (0, "ff_up"): [1, 2], (1, "w_in"): [3, 4], (1, "ff_up"): [5, 6]}
    big_cache = {}

    def big_w(l, name, after):
        if (l, name) == (0, "w_in") and (l, name) not in big_cache:
            srcs, lands = _xc_wait(False, first_ssem, first_rsem, first_src, first_land, after, "ag_first_wait",
                                   peers=OTHER_CHIPS)
            zone = lax.dynamic_update_index_in_dim(lands[0], srcs[0], me, 0)
            (zone,) = _sib_wait(*_sib_start([zone], "ag_first_sib_start"), "ag_first_sib_wait")
            big_cache[(l, name)] = _full_weight(name, zone)
        if (l, name) not in big_cache:
            idx = ag_groups[(l, name)]
            pick = lambda seq_: [seq_[i] for i in idx]
            srcs, lands = _xc_wait(False, pick(ag_ssem), pick(ag_rsem), pick(ag_src), pick(ag_land), after,
                                   f"ag_wait_{l}_{name}")
            for i, src, land in zip(idx, srcs, lands):
                big_cache[later[i]] = _full_weight(later[i][1], lax.dynamic_update_index_in_dim(land, src, me, 0))
        return big_cache[(l, name)]

    lw = []
    for l in range(DEPTH):
        lw.append(dict(
            norm1_g=norm1_g[l:l + 1] + (ag_zero if l == 0 else 0.0), norm2_g=norm2_g[l:l + 1], ssd_conv_w=scw_f[l],
            ssd_conv_b=ssd_conv_b[l:l + 1], ssd_dt_bias=ssd_dt_bias[l:l + 1], ssd_a_log=ssd_a_log[l:l + 1],
            ssd_d=ssd_d[l:l + 1], ssd_norm_g=ssd_norm_g[l:l + 1], gm_vnorm_g=gm_vnorm_g[l:l + 1], gm_ws=gm_ws[l],
            gm_bst=gm_bs[l].T, gm_out_g=gm_out_g[l:l + 1], ff_conv_w=fcw_f[l], ff_conv_b=ff_conv_b[l:l + 1]))

    outs = {}
    pending = {}

    def rs_finish(l, group, after):
        names, ssem, rsem, srcs, lands = pending.pop((l, group))
        srcs, lands = _xc_wait(True, ssem, rsem, srcs, lands, after, f"rs_wait_{l}_{group}")
        for nme, own, land in zip(names, srcs, lands):
            parts = [(own, lambda p: p[0])] + [(land, lambda p, k=k: k) for k in range(N_DEV - 1)]
            outs[nme] = _adamw_layer(parts, wts[nme], mom[nme], var[nme], me_arr, l, outs.get(nme), f"adamw_{nme}_{l}")
        return outs[names[-1]][0]

    def grad_sink(l, group, grads, after):
        names = list(grads)
        ssem, rsem, srcs, lands, zero = _xc_start(True, [_by_owner(n, grads[n]) for n in names], after, f"rs_start_{l}_{group}")
        pending[(l, group)] = (names, ssem, rsem, srcs, lands)
        return zero.reshape(1, 1)

    early_gather = {}

    def small_sink(l, early, small, dmods, dfg, loss_p):
        if l > 0:
            return None
        layers = [dict(early, norm1_g=jnp.zeros((1, d), F32))] + small[1:]
        rows = lambda name: [layers[k][name] for k in range(DEPTH)]
        packed = [
            jnp.concatenate(sum([rows(n) for n in _P1024], []) + [dfg], axis=0),
            jnp.concatenate(rows("ssd_conv_w") + rows("ssd_conv_b"), axis=0),
            jnp.concatenate(rows("ff_conv_w") + rows("ff_conv_b"), axis=0),
            jnp.concatenate(sum([rows(n) for n in _P16], []) + [loss_p[:, :SSD_HEADS]], axis=0),
            jnp.concatenate([layers[k]["gm_ws"].reshape(GM_HEADS * CHUNK, CHUNK) for k in range(DEPTH)] + rows("gm_bs"), axis=0),
            jnp.concatenate([jnp.zeros((nseq, N_MOD * d), F32)] + dmods[1:], axis=0)]
        ssem, rsem, srcs, lands, zero = _xc_start(False, packed, packed[0], "small_start")
        early_gather.update(ssem=ssem, rsem=rsem, srcs=srcs, lands=lands)
        return zero.reshape(1, 1)

    grad_x, small, dmods = _local_step(
        x.reshape(nseq * seq, d), loss_target.reshape(nseq * seq, d), mods, lw, final_g.reshape(1, d), nseq=nseq,
        big_w=big_w, grad_sink=grad_sink, small_sink=small_sink)

    done = grad_x
    for l, grp in ((1, "ffn"), (1, "w_out"), (1, "w_in"), (0, "ffn"), (0, "w_out")):
        done = rs_finish(l, grp, done)
    srcs, lands = _xc_wait(False, early_gather["ssem"], early_gather["rsem"], early_gather["srcs"],
                           early_gather["lands"], done, "small_wait")
    gathered = [lax.dynamic_update_index_in_dim(land, src, me, 0) for src, land in zip(srcs, lands)]
    gathered += _all_gather([small[0]["norm1_g"], dmods[0]], "gather_late", dep=gathered[0])
    gath = dict(zip(["p1024", "p1536", "p2816", "p16", "p128", "p6144", "late1024", "late6144"], gathered))

    dmod_all = jnp.concatenate([gath["late6144"].reshape(1, N_DEV * nseq, N_MOD * d),
                                jnp.transpose(gath["p6144"].reshape(N_DEV, DEPTH, nseq, N_MOD * d)[:, 1:], (1, 0, 2, 3)).reshape(
                                    DEPTH - 1, N_DEV * nseq, N_MOD * d)], axis=0)
    small_names = _P1024 + ["final_g", "ssd_conv_b", "ff_conv_b"] + _P16 + ["gm_ws", "gm_bs", "ada_b"]
    wmv = {}
    for nme in small_names:
        if nme == "final_g":
            wmv[nme] = tuple(a.reshape(1, d) for a in (wts[nme], mom[nme], var[nme]))
        else:
            wmv[nme] = (wts[nme], mom[nme], var[nme])
    small_out, scw_full, fcw_full, loss_sum = _adamw_small(gath, wmv)
    loss = loss_sum[0, 0]
    rs_finish(0, "w_in", scw_full)
    for nme in small_names:
        outs[nme] = small_out[nme]
    outs["final_g"] = tuple(a.reshape(d) for a in outs["final_g"])

    n_scw, n_fcw = ssd_conv_w.shape[2], ff_conv_w.shape[2]
    g_scw_mine = lax.dynamic_slice_in_dim(scw_full, me * n_scw, n_scw, axis=2)
    g_fcw_mine = lax.dynamic_slice_in_dim(fcw_full, me * n_fcw, n_fcw, axis=2)
    outs["ssd_conv_w"] = _adamw_sharded([(g_scw_mine, lambda p: 0)], ssd_conv_w, m_ssd_conv_w, v_ssd_conv_w, me_arr, "adamw_ssd_conv_w")
    outs["ff_conv_w"] = _adamw_sharded([(g_fcw_mine, lambda p: 0)], ff_conv_w, m_ff_conv_w, v_ff_conv_w, me_arr, "adamw_ff_conv_w")

    dmod_cols = _b(lax.dynamic_slice_in_dim(dmod_all, me * n_ada, n_ada, axis=2))
    g_ada = jnp.stack([_matmul(c_act, dmod_cols[l], ta=True, name=f"mm_ada_dw_{l}") for l in range(DEPTH)])
    outs["ada_w"] = _adamw_sharded([(g_ada, lambda p: 0)], ada_w, m_ada_w, v_ada_w, me_arr, "adamw_ada_w")

    for nme in _TRANSPOSED:
        outs[nme] = tuple(jnp.transpose(a, (0, 2, 1)) for a in outs[nme])
    result = [loss, grad_x.reshape(nseq, seq, d)]
    for k in range(4):
        result += [outs[n][k] for n in _WEIGHTS]
    return tuple(result)
```

```python
import functools
import math

import jax
import jax.numpy as jnp
from jax import lax
from jax.experimental import pallas as pl
from jax.experimental.pallas import tpu as pltpu

F32 = jnp.float32
BF16 = jnp.bfloat16

N_DEV = 8
D_MODEL = 1024
DEPTH = 2
CHUNK = 128
SSD_HEADS = 16
SSD_HEAD_DIM = 64
SSD_GROUPS = 2
HEADS_PER_GROUP = SSD_HEADS // SSD_GROUPS
GROUP_WIDTH = HEADS_PER_GROUP * SSD_HEAD_DIM
D_STATE = 128
D_SSD = 1024
CONV_DIM = 1536
SSD_CONV = 4
GM_HEADS = 8
GM_HEAD_DIM = 128
D_GM = 1024
D_FF = 2816
FF_CONV = 3
N_IN = 4624
N_MOD = 6
EPS = 1e-6

N_INP = 5120
COL_U, COL_V, COL_Z, COL_XBC, COL_DT = 0, 1024, 2048, 3072, 4608

ADAM_LR = 0.001
ADAM_B1 = 0.9
ADAM_B2 = 0.999
ADAM_EPS = 1e-08
ADAM_WD = 0.01
ADAM_STEP = 10

VMEM_LIMIT = 56 * 1024 * 1024
MESH = pl.DeviceIdType.MESH
ANY = pl.BlockSpec(memory_space=pl.ANY)


def _cp(*sem):
    return pltpu.CompilerParams(dimension_semantics=sem, vmem_limit_bytes=VMEM_LIMIT)


def _tile(n, pref):
    if n <= pref or n % 128:
        return n
    best = 128
    for t in range(128, pref + 1, 128):
        if n % t == 0:
            best = t
    return best


def _silu(x):
    return x * jax.nn.sigmoid(x)


def _gelu(x):
    return 0.5 * x * (1.0 + lax.erf(x * (1.0 / math.sqrt(2.0))))


def _softplus(x):
    return jnp.maximum(x, 0.0) + jnp.log1p(jnp.exp(-jnp.abs(x)))


def _b(x):
    return x.astype(BF16)


_NN = (((1,), (0,)), ((), ()))
_NT = (((1,), (1,)), ((), ()))
_TN = (((0,), (0,)), ((), ()))


def _dg(a, b, dn):
    return lax.dot_general(_b(a), _b(b), dn, preferred_element_type=F32)


@jax.custom_vjp
def _bdot(a, b):
    return _dg(a, b, _NN)


def _bdot_fwd(a, b):
    return _dg(a, b, _NN), (a, b)


def _bdot_bwd(res, ct):
    a, b = res
    return _dg(ct, b, _NT), _dg(a, ct, _TN)


_bdot.defvjp(_bdot_fwd, _bdot_bwd)


@jax.custom_vjp
def _bdot_nt(a, b):
    return _dg(a, b, _NT)


def _bdot_nt_fwd(a, b):
    return _dg(a, b, _NT), (a, b)


def _bdot_nt_bwd(res, ct):
    a, b = res
    return _dg(ct, b, _NN), _dg(ct, a, _TN)


_bdot_nt.defvjp(_bdot_nt_fwd, _bdot_nt_bwd)


@jax.custom_vjp
def _bdot_tn(a, b):
    return _dg(a, b, _TN)


def _bdot_tn_fwd(a, b):
    return _dg(a, b, _TN), (a, b)


def _bdot_tn_bwd(res, ct):
    a, b = res
    return _dg(b, ct, _NT), _dg(a, ct, _NN)


_bdot_tn.defvjp(_bdot_tn_fwd, _bdot_tn_bwd)


def _tri(n, lower):
    r = lax.broadcasted_iota(jnp.int32, (n, n), 0)
    c = lax.broadcasted_iota(jnp.int32, (n, n), 1)
    return ((r >= c) if lower else (r <= c)).astype(F32)


def _eye(n):
    r = lax.broadcasted_iota(jnp.int32, (n, n), 0)
    c = lax.broadcasted_iota(jnp.int32, (n, n), 1)
    return (r == c).astype(F32)


def _hdot(a, b, dn):
    return lax.dot_general(a, b, dn, precision=lax.Precision.HIGHEST, preferred_element_type=F32)


@jax.custom_vjp
def _cumsum_rows(x):
    return _hdot(_tri(x.shape[0], True), x, _NN)


def _cumsum_rows_fwd(x):
    return _cumsum_rows(x), None


def _cumsum_rows_bwd(_, ct):
    return (_hdot(_tri(ct.shape[0], False), ct, _NN),)


_cumsum_rows.defvjp(_cumsum_rows_fwd, _cumsum_rows_bwd)


@jax.custom_vjp
def _transpose(x):
    return _hdot(_eye(x.shape[1]), x, _NT)


def _transpose_fwd(x):
    return _transpose(x), None


def _transpose_bwd(_, ct):
    return (_hdot(_eye(ct.shape[1]), ct, _NT),)


_transpose.defvjp(_transpose_fwd, _transpose_bwd)


MXU_WIDTH = 256
MATMUL_TILE_CAP = 2816
MATMUL_VMEM = 44 * 1024 * 1024


def _mxu_tiles(n):
    if n <= MATMUL_TILE_CAP or n % 128:
        return [n]
    for unit in (MXU_WIDTH, 128):
        opts = [t for t in range(unit, MATMUL_TILE_CAP + 1, unit) if n % t == 0]
        if opts:
            return opts
    return [n]


def _matmul(a, b, *, ta=False, tb=False, name, dep=None, out_dtype=F32):
    pieces = list(a) if isinstance(a, (list, tuple)) else [a]
    npc = len(pieces)
    rows, width = pieces[0].shape
    assert all(p.shape == (rows, width) for p in pieces)
    if ta:
        k_dim, m_dim = rows, width * npc
    else:
        m_dim, k_dim = rows, width * npc
    if tb:
        n_dim, kb = b.shape
    else:
        kb, n_dim = b.shape
    assert kb == k_dim, (pieces[0].shape, npc, b.shape, ta, tb)
    m_unit = width if npc > 1 and ta else m_dim
    k_unit = width if npc > 1 and not ta else k_dim
    tm = _tile(m_unit, 1536)
    tn_opts, tk_opts = _mxu_tiles(n_dim), _mxu_tiles(k_unit)
    tn, tk = tn_opts.pop(), tk_opts.pop()
    while 4 * (tm * tk + tk * tn) + 8 * tm * tn > MATMUL_VMEM:
        if tn >= tk and tn_opts:
            tn = tn_opts.pop()
        else:
            tk = tk_opts.pop()
    ni, nj, nk = m_dim // tm, n_dim // tn, k_dim // tk
    per = width // (tm if ta else tk)
    dn = (((0 if ta else 1,), (1 if tb else 0,)), ((), ()))

    a_bytes, b_bytes = m_dim * k_dim, k_dim * n_dim
    m_outer = nk > 1 or a_bytes + b_bytes * ni <= b_bytes + a_bytes * nj
    if m_outer:
        ij = lambda o, n, k: (o, n)
        grid = (ni, nj, nk)
    else:
        ij = lambda o, n, k: (n, o)
        grid = (nj, ni, nk)

    use_acc = nk > 1 and out_dtype != F32

    def body(*refs):
        a_refs, b_ref = refs[:npc], refs[npc]
        o_ref = refs[-2] if use_acc else refs[-1]
        acc_ref = refs[-1]
        k = pl.program_id(2)
        i = pl.program_id(0 if m_outer else 1)
        along = i if ta else k

        def step(a_ref):
            p = lax.dot_general(a_ref[...], b_ref[...], dn, preferred_element_type=F32)
            if nk == 1:
                o_ref[...] = p.astype(out_dtype)
            else:
                @pl.when(k == 0)
                def _():
                    acc_ref[...] = p

                @pl.when((k > 0) & (k < nk - 1 if use_acc else True))
                def _():
                    acc_ref[...] += p

                if use_acc:
                    @pl.when(k == nk - 1)
                    def _():
                        o_ref[...] = (acc_ref[...] + p).astype(out_dtype)

        if npc == 1:
            step(a_refs[0])
        else:
            for pc in range(npc):
                pl.when((along >= pc * per) & (along < (pc + 1) * per))(functools.partial(step, a_refs[pc]))

    def a_map(pc, o, n, k):
        i, _ = ij(o, n, k)
        along = i if ta else k
        if npc > 1:
            along = jnp.clip(along - pc * per, 0, per - 1)
        return (k, along) if ta else (i, along)

    def b_map(o, n, k):
        _, j = ij(o, n, k)
        return (j, k) if tb else (k, j)

    extra = [] if dep is None else [dep]
    return pl.pallas_call(
        body, name=name,
        grid=grid,
        in_specs=[pl.BlockSpec((tk, tm) if ta else (tm, tk), functools.partial(a_map, pc)) for pc in range(npc)]
        + [pl.BlockSpec((tn, tk) if tb else (tk, tn), b_map)] + [ANY] * len(extra),
        out_specs=pl.BlockSpec((tm, tn), lambda o, n, k: ij(o, n, k)),
        out_shape=jax.ShapeDtypeStruct((m_dim, n_dim), out_dtype),
        scratch_shapes=[pltpu.VMEM((tm, tn), F32)] if use_acc else [],
        compiler_params=_cp("parallel", "parallel", "arbitrary"),
    )(*pieces, b, *extra)


def _ada_fwd(c_all, ada_w, ada_b_shard):
    depth, d, n = ada_w.shape
    nb = c_all.shape[0]

    def body(c_ref, w_ref, b_ref, o_ref, ca_ref):
        ca = _silu(c_ref[...])
        ca_ref[...] = _b(ca)
        o_ref[0] = _dg(ca, w_ref[0], _NN) + b_ref[0]

    return pl.pallas_call(
        body, name="ada_fwd",
        grid=(depth,),
        in_specs=[pl.BlockSpec((nb, d), lambda l: (0, 0)),
                  pl.BlockSpec((1, d, n), lambda l: (l, 0, 0)),
                  pl.BlockSpec((1, 1, n), lambda l: (l, 0, 0))],
        out_specs=[pl.BlockSpec((1, nb, n), lambda l: (l, 0, 0)),
                   pl.BlockSpec((nb, d), lambda l: (0, 0))],
        out_shape=[jax.ShapeDtypeStruct((depth, nb, n), F32), jax.ShapeDtypeStruct((nb, d), BF16)],
        compiler_params=_cp("arbitrary"),
    )(c_all, ada_w, ada_b_shard)


def _fold(acc):
    return jnp.sum(acc, axis=0, keepdims=True)


def _rinv(x):
    return lax.rsqrt(jnp.sum(x * x, axis=-1, keepdims=True) * (1.0 / D_MODEL) + EPS)


def _rms_bwd(a, xhat, rinv):
    return rinv * (a - xhat * (jnp.sum(a * xhat, axis=-1, keepdims=True) * (1.0 / D_MODEL)))


def _row_tile(seq):
    return min(seq, 256)


def _normmod_fwd(x, g, sc, sh, *, nseq, name):
    t, d = x.shape
    seq = t // nseq
    tr = _row_tile(seq)
    nt = seq // tr
    row = pl.BlockSpec((tr, d), lambda s, i: (s * nt + i, 0))
    per_seq = pl.BlockSpec((1, 1, d), lambda s, i: (s, 0, 0))

    def body(x_ref, g_ref, sc_ref, sh_ref, h_ref):
        x_v = x_ref[...]
        h_ref[...] = _b(x_v * _rinv(x_v) * (g_ref[...] * (1.0 + sc_ref[0])) + sh_ref[0])

    return pl.pallas_call(
        body, name=name, grid=(nseq, nt),
        in_specs=[row, pl.BlockSpec((1, d), lambda s, i: (0, 0)), per_seq, per_seq],
        out_specs=row,
        out_shape=jax.ShapeDtypeStruct((t, d), BF16),
        compiler_params=_cp("parallel", "parallel"),
    )(x, g, sc, sh)


NORM_TM = 512


def _matmul_normbwd(a, b, dxo, x, delta, gate, g, sc, *, nseq, name, dep=None):
    pieces = list(a) if isinstance(a, (list, tuple)) else [a]
    npc = len(pieces)
    t, width = pieces[0].shape
    k_dim, d = width * npc, b.shape[1]
    assert b.shape[0] == k_dim and all(p.shape == (t, width) for p in pieces)
    seq = t // nseq
    tm = min(NORM_TM, seq)
    per_seq_tiles = seq // tm
    tk = _mxu_tiles(width if npc > 1 else k_dim).pop()
    nk, per = k_dim // tk, width // tk
    has_delta = delta is not None
    extra = [] if dep is None else [dep]

    def body(*refs):
        a_refs, b_ref = refs[:npc], refs[npc]
        dxo_ref, x_ref = refs[npc + 1], refs[npc + 2]
        pos = npc + 3
        if has_delta:
            delta_ref, gate_ref = refs[pos], refs[pos + 1]
            pos += 2
        g_ref, sc_ref = refs[pos], refs[pos + 1]
        pos += 2 + len(extra)
        if has_delta:
            dx_ref, dd_ref, dgate_ref, dg_ref, dsc_ref, dsh_ref = refs[pos:pos + 6]
        else:
            dx_ref, dg_ref, dsc_ref, dsh_ref = refs[pos:pos + 4]
        acc_ref = refs[-1]
        i, k = pl.program_id(0), pl.program_id(1)

        def norm_bwd(dh_v):
            g_v, one_sc = g_ref[...], 1.0 + sc_ref[0]
            x_v = x_ref[...]
            rinv = _rinv(x_v)
            xhat = x_v * rinv
            dx = dxo_ref[...] + _rms_bwd(dh_v * (g_v * one_sc), xhat, rinv)
            dx_ref[...] = dx

            @pl.when(i == 0)
            def _():
                dg_ref[...] = jnp.zeros_like(dg_ref)

            @pl.when(i % per_seq_tiles == 0)
            def _():
                dsc_ref[...] = jnp.zeros_like(dsc_ref)
                dsh_ref[...] = jnp.zeros_like(dsh_ref)
                if has_delta:
                    dgate_ref[...] = jnp.zeros_like(dgate_ref)

            t_sum = _fold(dh_v * xhat)
            dg_ref[...] += t_sum * one_sc
            dsc_ref[0] += t_sum * g_v
            dsh_ref[0] += _fold(dh_v)
            if has_delta:
                dd_ref[...] = _b(dx * gate_ref[0])
                dgate_ref[0] += _fold(dx * delta_ref[...])

        def step(a_ref):
            p = lax.dot_general(a_ref[...], b_ref[...], _NN, preferred_element_type=F32)
            if nk == 1:
                norm_bwd(p)
            else:
                @pl.when(k == 0)
                def _():
                    acc_ref[...] = p

                @pl.when((k > 0) & (k < nk - 1))
                def _():
                    acc_ref[...] += p

                @pl.when(k == nk - 1)
                def _():
                    norm_bwd(acc_ref[...] + p)

        if npc == 1:
            step(a_refs[0])
        else:
            for pc in range(npc):
                pl.when((k >= pc * per) & (k < (pc + 1) * per))(functools.partial(step, a_refs[pc]))

    def a_map(pc, i, k):
        return (i, jnp.clip(k - pc * per, 0, per - 1) if npc > 1 else k)

    row = pl.BlockSpec((tm, d), lambda i, k: (i, 0))
    per_seq = pl.BlockSpec((1, 1, d), lambda i, k: (i // per_seq_tiles, 0, 0))
    vec = pl.BlockSpec((1, d), lambda i, k: (0, 0))
    shp = lambda *s, dt=F32: jax.ShapeDtypeStruct(s, dt)
    in_specs = [pl.BlockSpec((tm, tk), functools.partial(a_map, pc)) for pc in range(npc)]
    in_specs += [pl.BlockSpec((tk, d), lambda i, k: (k, 0)), row, row]
    operands = [*pieces, b, dxo, x]
    if has_delta:
        in_specs += [row, per_seq]
        operands += [delta, gate]
    in_specs += [vec, per_seq] + [ANY] * len(extra)
    operands += [g, sc, *extra]
    if has_delta:
        out_specs = [row, row, per_seq, vec, per_seq, per_seq]
        out_shape = [shp(t, d), shp(t, d, dt=BF16), shp(nseq, 1, d), shp(1, d), shp(nseq, 1, d), shp(nseq, 1, d)]
    else:
        out_specs = [row, vec, per_seq, per_seq]
        out_shape = [shp(t, d), shp(1, d), shp(nseq, 1, d), shp(nseq, 1, d)]
    outs = pl.pallas_call(
        body, name=name, grid=(t // tm, nk),
        in_specs=in_specs, out_specs=out_specs, out_shape=out_shape,
        scratch_shapes=[pltpu.VMEM((tm, d), F32)],
        compiler_params=_cp("arbitrary", "arbitrary"),
    )(*operands)
    if has_delta:
        return tuple(outs)
    dx, dg, dsc, dsh = outs
    return dx, None, None, dg, dsc, dsh


def _matmul_normfwd(a, b, xin, gate, g, sc, sh, *, nseq, name):
    t, k_dim = a.shape
    d = b.shape[1]
    assert b.shape[0] == k_dim and k_dim <= MATMUL_TILE_CAP
    seq = t // nseq
    tm = min(NORM_TM, seq)
    per_seq_tiles = seq // tm

    def body(a_ref, b_ref, xin_ref, gate_ref, g_ref, sc_ref, sh_ref, dl_ref, x_ref, h_ref):
        dl = lax.dot_general(a_ref[...], b_ref[...], _NN, preferred_element_type=F32)
        dl_ref[...] = dl
        x = xin_ref[...] + gate_ref[0] * dl
        x_ref[...] = x
        h_ref[...] = _b(x * _rinv(x) * (g_ref[...] * (1.0 + sc_ref[0])) + sh_ref[0])

    row = pl.BlockSpec((tm, d), lambda i: (i, 0))
    per_seq = pl.BlockSpec((1, 1, d), lambda i: (i // per_seq_tiles, 0, 0))
    return pl.pallas_call(
        body, name=name, grid=(t // tm,),
        in_specs=[pl.BlockSpec((tm, k_dim), lambda i: (i, 0)), pl.BlockSpec((k_dim, d), lambda i: (0, 0)),
                  row, per_seq, pl.BlockSpec((1, d), lambda i: (0, 0)), per_seq, per_seq],
        out_specs=[row, row, row],
        out_shape=[jax.ShapeDtypeStruct((t, d), F32), jax.ShapeDtypeStruct((t, d), F32), jax.ShapeDtypeStruct((t, d), BF16)],
        compiler_params=_cp("parallel"),
    )(a, b, xin, gate, g, sc, sh)


def _matmul_loss(a, b, xin, gate, fg, target, *, nseq, name):
    t, k_dim = a.shape
    d = b.shape[1]
    assert b.shape[0] == k_dim and k_dim <= MATMUL_TILE_CAP
    seq = t // nseq
    tm = min(NORM_TM, seq)
    per_seq_tiles = seq // tm

    def body(a_ref, b_ref, xin_ref, gate_ref, fg_ref, tgt_ref, dl_ref, loss_ref, dx_ref, dd_ref, dgate_ref, dfg_ref):
        i = pl.program_id(0)
        fg_v, gate_v = fg_ref[...], gate_ref[0]
        dl = lax.dot_general(a_ref[...], b_ref[...], _NN, preferred_element_type=F32)
        dl_ref[...] = dl
        x = xin_ref[...] + gate_v * dl
        rinv = _rinv(x)
        xhat = x * rinv
        err = xhat * fg_v - tgt_ref[...]
        dx = _rms_bwd(err * fg_v * (1.0 / d), xhat, rinv)
        dx_ref[...] = dx
        dd_ref[...] = _b(dx * gate_v)

        @pl.when(i == 0)
        def _():
            loss_ref[...] = jnp.zeros_like(loss_ref)
            dfg_ref[...] = jnp.zeros_like(dfg_ref)

        @pl.when(i % per_seq_tiles == 0)
        def _():
            dgate_ref[...] = jnp.zeros_like(dgate_ref)

        loss_ref[...] += jnp.sum(err * err) * (0.5 / d)
        dfg_ref[...] += _fold(err * xhat) * (1.0 / d)
        dgate_ref[0] += _fold(dx * dl)

    row = pl.BlockSpec((tm, d), lambda i: (i, 0))
    per_seq = pl.BlockSpec((1, 1, d), lambda i: (i // per_seq_tiles, 0, 0))
    vec = pl.BlockSpec((1, d), lambda i: (0, 0))
    return pl.pallas_call(
        body, name=name, grid=(t // tm,),
        in_specs=[pl.BlockSpec((tm, k_dim), lambda i: (i, 0)), pl.BlockSpec((k_dim, d), lambda i: (0, 0)),
                  row, per_seq, vec, row],
        out_specs=[row, pl.BlockSpec((1, 128), lambda i: (0, 0)), row, row, per_seq, vec],
        out_shape=[jax.ShapeDtypeStruct((t, d), F32), jax.ShapeDtypeStruct((1, 128), F32), jax.ShapeDtypeStruct((t, d), F32),
                   jax.ShapeDtypeStruct((t, d), BF16), jax.ShapeDtypeStruct((nseq, 1, d), F32),
                   jax.ShapeDtypeStruct((1, d), F32)],
        compiler_params=_cp("arbitrary"),
    )(a, b, xin, gate, fg, target)


CONV_TC = 256
CONV_LANES = 128
CONV_ROWS = 64
CONV_HALO = 8


def _conv_slabs(seq, fn):
    def step(i, carry):
        r0 = pl.multiple_of(i * CONV_ROWS, CONV_ROWS)
        for h in range(CONV_TC // CONV_LANES):
            fn(r0, slice(h * CONV_LANES, (h + 1) * CONV_LANES))
        return carry

    lax.fori_loop(0, seq // CONV_ROWS, step, 0)


def _slab(ref, r0, cols, seq):
    after = ref[pl.ds(pl.multiple_of(jnp.minimum(r0 + CONV_ROWS, seq - CONV_HALO), CONV_HALO), CONV_HALO), cols]
    return jnp.concatenate([ref[pl.ds(r0, CONV_ROWS), cols], jnp.where(r0 + CONV_ROWS < seq, after, 0.0)], axis=0)


def _conv_block(x, w_ref, b_ref):
    kw = w_ref.shape[0]
    rows = lax.broadcasted_iota(jnp.int32, x.shape, 0)
    y = b_ref[...] + w_ref[kw - 1:kw, :] * x
    for j in range(1, kw):
        y = y + w_ref[kw - 1 - j:kw - j, :] * jnp.where(rows >= j, pltpu.roll(x, j, 0), 0.0)
    return y


def _conv_block_bwd(dy, x, w_ref, dw_ref, db_ref):
    kw = w_ref.shape[0]
    n = x.shape[0]
    rows = lax.broadcasted_iota(jnp.int32, x.shape, 0)
    dx = w_ref[kw - 1:kw, :] * dy
    dw_ref[kw - 1:kw, :] += jnp.sum(dy * x, axis=0, keepdims=True)
    for j in range(1, kw):
        dy_j = jnp.where(rows < n - j, pltpu.roll(dy, n - j, 0), 0.0)
        dx = dx + w_ref[kw - 1 - j:kw - j, :] * dy_j
        dw_ref[kw - 1 - j:kw - j, :] += jnp.sum(dy_j * x, axis=0, keepdims=True)
    db_ref[...] += jnp.sum(dy, axis=0, keepdims=True)
    return dx


def _conv_bwd(dy_ext, x, w_ref, dw_ref, db_ref, cols):
    kw = w_ref.shape[0]
    n = dy_ext.shape[0]
    dy = dy_ext[:CONV_ROWS]
    dx = w_ref[kw - 1:kw, cols] * dy
    dw_ref[kw - 1:kw, cols] += jnp.sum(dy * x, axis=0, keepdims=True)
    for j in range(1, kw):
        dy_j = pltpu.roll(dy_ext, n - j, 0)[:CONV_ROWS]
        dx = dx + w_ref[kw - 1 - j:kw - j, cols] * dy_j
        dw_ref[kw - 1 - j:kw - j, cols] += jnp.sum(dy_j * x, axis=0, keepdims=True)
    db_ref[:, cols] += jnp.sum(dy, axis=0, keepdims=True)
    return dx


def _dsilu(pre):
    sg = jax.nn.sigmoid(pre)
    return pre * sg, sg * (1.0 + pre * (1.0 - sg))


def _ssd_conv_fwd(proj, w, b, *, nseq):
    t = proj.shape[0]
    seq = t // nseq
    nb = CONV_DIM // CONV_TC
    off = COL_XBC // CONV_TC

    def body(x_ref, w_ref, b_ref, o_ref, pre_ref):
        pre = _conv_block(x_ref[...], w_ref, b_ref)
        pre_ref[...] = pre
        o_ref[...] = _silu(pre)

    col = pl.BlockSpec((seq, CONV_TC), lambda j, s: (s, j))
    return pl.pallas_call(
        body, name="ssd_conv_fwd", grid=(nb, nseq),
        in_specs=[pl.BlockSpec((seq, CONV_TC), lambda j, s: (s, off + j)),
                  pl.BlockSpec((SSD_CONV, CONV_TC), lambda j, s: (0, j)),
                  pl.BlockSpec((1, CONV_TC), lambda j, s: (0, j))],
        out_specs=[col, col],
        out_shape=[jax.ShapeDtypeStruct((t, CONV_DIM), F32)] * 2,
        compiler_params=_cp("parallel", "parallel"),
    )(proj, w, b)


def _ssd_conv_bwd(dact, pre, proj, w, dproj, *, nseq):
    t = proj.shape[0]
    seq = t // nseq
    nb = CONV_DIM // CONV_TC
    off = COL_XBC // CONV_TC

    def body(da_ref, pre_ref, x_ref, w_ref, dproj_ref, dx_ref, dw_ref, db_ref):
        del dproj_ref

        @pl.when(pl.program_id(1) == 0)
        def _():
            dw_ref[...] = jnp.zeros_like(dw_ref)
            db_ref[...] = jnp.zeros_like(db_ref)

        def slab(r0, cols):
            _, dsilu = _dsilu(_slab(pre_ref, r0, cols, seq))
            dpre_ext = _slab(da_ref, r0, cols, seq) * dsilu
            x = x_ref[pl.ds(r0, CONV_ROWS), cols]
            dx_ref[pl.ds(r0, CONV_ROWS), cols] = _b(_conv_bwd(dpre_ext, x, w_ref, dw_ref, db_ref, cols))

        _conv_slabs(seq, slab)

    return pl.pallas_call(
        body, name="ssd_conv_bwd", grid=(nb, nseq),
        in_specs=[pl.BlockSpec((seq, CONV_TC), lambda j, s: (s, j)),
                  pl.BlockSpec((seq, CONV_TC), lambda j, s: (s, j)),
                  pl.BlockSpec((seq, CONV_TC), lambda j, s: (s, off + j)),
                  pl.BlockSpec((SSD_CONV, CONV_TC), lambda j, s: (0, j)),
                  ANY],
        out_specs=[pl.BlockSpec((seq, CONV_TC), lambda j, s: (s, off + j)),
                   pl.BlockSpec((SSD_CONV, CONV_TC), lambda j, s: (0, j)),
                   pl.BlockSpec((1, CONV_TC), lambda j, s: (0, j))],
        out_shape=[jax.ShapeDtypeStruct(dproj.shape, dproj.dtype), jax.ShapeDtypeStruct((SSD_CONV, CONV_DIM), F32),
                   jax.ShapeDtypeStruct((1, CONV_DIM), F32)],
        input_output_aliases={4: 0},
        compiler_params=_cp("parallel", "arbitrary"),
    )(dact, pre, proj, w, dproj)


def _ffn_act_fwd(up, w, b, *, nseq):
    t = up.shape[0]
    seq = t // nseq
    nb = D_FF // CONV_TC

    def body(g_ref, v_ref, w_ref, b_ref, o_ref):
        o_ref[...] = _b(_silu(_conv_block(g_ref[...].astype(F32), w_ref, b_ref)) * v_ref[...].astype(F32))

    col = pl.BlockSpec((seq, CONV_TC), lambda j, s: (s, j))
    return pl.pallas_call(
        body, name="ffn_act_fwd", grid=(nb, nseq),
        in_specs=[col,
                  pl.BlockSpec((seq, CONV_TC), lambda j, s: (s, nb + j)),
                  pl.BlockSpec((FF_CONV, CONV_TC), lambda j, s: (0, j)),
                  pl.BlockSpec((1, CONV_TC), lambda j, s: (0, j))],
        out_specs=col,
        out_shape=jax.ShapeDtypeStruct((t, D_FF), BF16),
        compiler_params=_cp("parallel", "parallel"),
    )(up, up, w, b)


def _ffn_act_bwd(dact, up, w, b, *, nseq):
    t = up.shape[0]
    seq = t // nseq
    nb = D_FF // CONV_TC

    def body(da_ref, g_ref, v_ref, w_ref, b_ref, dg_ref, dv_ref, dw_ref, db_ref):
        @pl.when(pl.program_id(1) == 0)
        def _():
            dw_ref[...] = jnp.zeros_like(dw_ref)
            db_ref[...] = jnp.zeros_like(db_ref)

        gate = g_ref[...].astype(F32)
        silu, dsilu = _dsilu(_conv_block(gate, w_ref, b_ref))
        da = da_ref[...].astype(F32)
        dv_ref[...] = _b(da * silu)
        dg_ref[...] = _b(_conv_block_bwd(da * v_ref[...].astype(F32) * dsilu, gate, w_ref, dw_ref, db_ref))

    col = pl.BlockSpec((seq, CONV_TC), lambda j, s: (s, j))
    return pl.pallas_call(
        body, name="ffn_act_bwd", grid=(nb, nseq),
        in_specs=[col, col,
                  pl.BlockSpec((seq, CONV_TC), lambda j, s: (s, nb + j)),
                  pl.BlockSpec((FF_CONV, CONV_TC), lambda j, s: (0, j)),
                  pl.BlockSpec((1, CONV_TC), lambda j, s: (0, j))],
        out_specs=[col, col,
                   pl.BlockSpec((FF_CONV, CONV_TC), lambda j, s: (0, j)),
                   pl.BlockSpec((1, CONV_TC), lambda j, s: (0, j))],
        out_shape=[jax.ShapeDtypeStruct((t, D_FF), BF16), jax.ShapeDtypeStruct((t, D_FF), BF16),
                   jax.ShapeDtypeStruct((FF_CONV, D_FF), F32), jax.ShapeDtypeStruct((1, D_FF), F32)],
        compiler_params=_cp("parallel", "arbitrary"),
    )(dact, up, up, w, b)


SSD_PAIRS = SSD_HEADS // 2
PAIR_W = 2 * SSD_HEAD_DIM
PAIRS_PER_GROUP = SSD_PAIRS // SSD_GROUPS


def _ssd_chunk(xs, bg, cg, dtr, z, hp, dtb, alog, dskip, ng):
    n = dtr.shape[0]
    dt = _softplus(dtr + dtb)
    cs = _cumsum_rows(dt * (-jnp.exp(alog)))
    cs_t = _transpose(cs)
    lane = lax.broadcasted_iota(jnp.int32, (1, SSD_HEADS), 1)
    sub = lax.broadcasted_iota(jnp.int32, (SSD_HEADS, 1), 0)
    row = lax.broadcasted_iota(jnp.int32, (n, 1), 0)
    causal = lax.broadcasted_iota(jnp.int32, (n, n), 0) >= lax.broadcasted_iota(jnp.int32, (n, n), 1)
    future = jnp.where(causal, 0.0, -1e30)
    first = lax.broadcasted_iota(jnp.int32, (1, PAIR_W), 1) < SSD_HEAD_DIM
    first_rows = lax.broadcasted_iota(jnp.int32, (PAIR_W, 1), 0) < SSD_HEAD_DIM
    first_f = first.astype(F32)
    cb = [_bdot_nt(cg[g], bg[g]) for g in range(SSD_GROUPS)]
    ys, hn = [], []
    for p in range(SSD_PAIRS):
        g = p // PAIRS_PER_GROUP
        col, decay, last = [], [], []
        for h in (2 * p, 2 * p + 1):
            oh = (lane == h).astype(F32)
            cs_h = jnp.sum(cs * oh, axis=1, keepdims=True)
            cs_row = jnp.sum(cs_t * (sub == h).astype(F32), axis=0, keepdims=True)
            col.append((jnp.sum(dt * oh, axis=1, keepdims=True), cs_h, jnp.sum(dskip * oh, axis=1, keepdims=True)))
            last.append(jnp.sum(jnp.where(row == n - 1, cs_h, 0.0), axis=0, keepdims=True))
            decay.append(jnp.exp(cs_h - cs_row + future))
        pair = lambda a, b: jnp.where(first, a, b)
        dt_p = pair(col[0][0], col[1][0])
        cs_p = pair(col[0][1], col[1][1])
        last_p = pair(last[0], last[1])
        xc = xs[p] * dt_p
        y = _bdot(cb[g] * decay[0], xc * first_f) + _bdot(cb[g] * decay[1], xc * (1.0 - first_f))
        y = y + _bdot_nt(cg[g], hp[p]) * jnp.exp(cs_p)
        y = y + pair(col[0][2], col[1][2]) * xs[p]
        keep = jnp.where(first_rows, jnp.exp(last[0]), jnp.exp(last[1]))
        hn.append(keep * hp[p] + _bdot_tn(xc * jnp.exp(last_p - cs_p), bg[g]))
        ys.append(y * _silu(z[p]))
    outs = []
    for g in range(SSD_GROUPS):
        ps = range(g * PAIRS_PER_GROUP, (g + 1) * PAIRS_PER_GROUP)
        ms = sum(jnp.sum(ys[p] * ys[p], axis=1, keepdims=True) for p in ps) * (1.0 / GROUP_WIDTH)
        r = lax.rsqrt(ms + EPS)
        outs += [ys[p] * r * ng[p] for p in ps]
    return outs, hn


def _hslices(ref, width, count, base=0):
    return [ref[:, base + k * width: base + (k + 1) * width] for k in range(count)]


def _ssd_load(xbc_ref, z_ref, dt_ref, ng_ref):
    xs = _hslices(xbc_ref, PAIR_W, SSD_PAIRS)
    bg = _hslices(xbc_ref, D_STATE, SSD_GROUPS, D_SSD)
    cg = _hslices(xbc_ref, D_STATE, SSD_GROUPS, D_SSD + SSD_GROUPS * D_STATE)
    z = _hslices(z_ref, PAIR_W, SSD_PAIRS)
    ng = _hslices(ng_ref, PAIR_W, SSD_PAIRS)
    return xs, bg, cg, dt_ref[:, 0:SSD_HEADS], z, ng


def _ssd_specs(nch):
    rowi = lambda s, c: s * nch + c
    return [pl.BlockSpec((CHUNK, CONV_DIM), lambda s, c: (rowi(s, c), 0)),
            pl.BlockSpec((CHUNK, D_SSD), lambda s, c: (rowi(s, c), COL_Z // D_SSD)),
            pl.BlockSpec((CHUNK, 128), lambda s, c: (rowi(s, c), COL_DT // 128)),
            pl.BlockSpec((1, SSD_HEADS), lambda s, c: (0, 0)),
            pl.BlockSpec((1, SSD_HEADS), lambda s, c: (0, 0)),
            pl.BlockSpec((1, SSD_HEADS), lambda s, c: (0, 0)),
            pl.BlockSpec((1, D_SSD), lambda s, c: (0, 0))]


def _ssd_fwd(xbc, proj, dtb, alog, dskip, ng, *, nseq):
    t = proj.shape[0]
    nch = t // nseq // CHUNK
    hd = PAIR_W

    def body(xbc_ref, z_ref, dt_ref, dtb_ref, alog_ref, dsk_ref, ng_ref, y_ref, hp_ref, h_ref):
        @pl.when(pl.program_id(1) == 0)
        def _():
            h_ref[...] = jnp.zeros_like(h_ref)

        xs, bg, cg, dtr, z, ngs = _ssd_load(xbc_ref, z_ref, dt_ref, ng_ref)
        hp_ref[0] = h_ref[...]
        hp = [h_ref[h * hd:(h + 1) * hd, :] for h in range(SSD_PAIRS)]
        outs, hn = _ssd_chunk(xs, bg, cg, dtr, z, hp, dtb_ref[...], alog_ref[...], dsk_ref[...], ngs)
        for h in range(SSD_PAIRS):
            y_ref[:, h * hd:(h + 1) * hd] = _b(outs[h])
            h_ref[h * hd:(h + 1) * hd, :] = hn[h]

    return pl.pallas_call(
        body, name="ssd_fwd", grid=(nseq, nch),
        in_specs=_ssd_specs(nch),
        out_specs=[pl.BlockSpec((CHUNK, D_SSD), lambda s, c: (s * nch + c, 0)),
                   pl.BlockSpec((1, D_SSD, D_STATE), lambda s, c: (s * nch + c, 0, 0))],
        out_shape=[jax.ShapeDtypeStruct((t, D_SSD + D_GM), BF16),
                   jax.ShapeDtypeStruct((t // CHUNK, D_SSD, D_STATE), F32)],
        scratch_shapes=[pltpu.VMEM((D_SSD, D_STATE), F32)],
        compiler_params=_cp("arbitrary", "arbitrary"),
    )(xbc, proj, proj, dtb, alog, dskip, ng)


def _ssd_bwd(dy, xbc, proj, hprev, dtb, alog, dskip, ng, *, nseq):
    t = proj.shape[0]
    nch = t // nseq // CHUNK
    hd = PAIR_W
    rev = lambda s, c: s * nch + (nch - 1 - c)

    def body(dy_ref, xbc_ref, z_ref, dt_ref, hp_ref, dtb_ref, alog_ref, dsk_ref, ng_ref,
             dxbc_ref, dproj_ref, ddtb_ref, dalog_ref, ddsk_ref, dng_ref, dh_ref):
        first = (pl.program_id(0) == 0) & (pl.program_id(1) == 0)

        @pl.when(pl.program_id(1) == 0)
        def _():
            dh_ref[...] = jnp.zeros_like(dh_ref)

        @pl.when(first)
        def _():
            ddtb_ref[...] = jnp.zeros_like(ddtb_ref)
            dalog_ref[...] = jnp.zeros_like(dalog_ref)
            ddsk_ref[...] = jnp.zeros_like(ddsk_ref)
            dng_ref[...] = jnp.zeros_like(dng_ref)

        xs, bg, cg, dtr, z, ngs = _ssd_load(xbc_ref, z_ref, dt_ref, ng_ref)
        hp = [hp_ref[0, h * hd:(h + 1) * hd, :] for h in range(SSD_PAIRS)]
        _, vjp = jax.vjp(_ssd_chunk, xs, bg, cg, dtr, z, hp, dtb_ref[...], alog_ref[...], dsk_ref[...], ngs)
        douts = [dy_ref[:, h * hd:(h + 1) * hd] for h in range(SSD_PAIRS)]
        dhn = [dh_ref[h * hd:(h + 1) * hd, :] for h in range(SSD_PAIRS)]
        dxs, dbg, dcg, ddtr, dz, dhp, ddtb, dalog, ddsk, dngs = vjp((douts, dhn))
        dproj_ref[:, :COL_Z] = jnp.zeros((CHUNK, COL_Z), BF16)
        dproj_ref[:, COL_XBC:] = jnp.zeros((CHUNK, N_INP - COL_XBC), BF16)
        for h in range(SSD_PAIRS):
            dxbc_ref[:, h * hd:(h + 1) * hd] = dxs[h]
            dproj_ref[:, COL_Z + h * hd: COL_Z + (h + 1) * hd] = _b(dz[h])
            dh_ref[h * hd:(h + 1) * hd, :] = dhp[h]
            dng_ref[:, h * hd:(h + 1) * hd] += dngs[h]
        for g in range(SSD_GROUPS):
            dxbc_ref[:, D_SSD + g * D_STATE: D_SSD + (g + 1) * D_STATE] = dbg[g]
            dxbc_ref[:, D_SSD + (SSD_GROUPS + g) * D_STATE: D_SSD + (SSD_GROUPS + g + 1) * D_STATE] = dcg[g]
        dproj_ref[:, COL_DT:COL_DT + SSD_HEADS] = _b(ddtr)
        ddtb_ref[...] += ddtb
        dalog_ref[...] += dalog
        ddsk_ref[...] += ddsk

    small = pl.BlockSpec((1, SSD_HEADS), lambda s, c: (0, 0))
    return pl.pallas_call(
        body, name="ssd_bwd", grid=(nseq, nch),
        in_specs=[pl.BlockSpec((CHUNK, D_SSD), lambda s, c: (rev(s, c), 0)),
                  pl.BlockSpec((CHUNK, CONV_DIM), lambda s, c: (rev(s, c), 0)),
                  pl.BlockSpec((CHUNK, D_SSD), lambda s, c: (rev(s, c), COL_Z // D_SSD)),
                  pl.BlockSpec((CHUNK, 128), lambda s, c: (rev(s, c), COL_DT // 128)),
                  pl.BlockSpec((1, D_SSD, D_STATE), lambda s, c: (rev(s, c), 0, 0)),
                  small, small, small,
                  pl.BlockSpec((1, D_SSD), lambda s, c: (0, 0))],
        out_specs=[pl.BlockSpec((CHUNK, CONV_DIM), lambda s, c: (rev(s, c), 0)),
                   pl.BlockSpec((CHUNK, N_INP), lambda s, c: (rev(s, c), 0)),
                   small, small, small,
                   pl.BlockSpec((1, D_SSD), lambda s, c: (0, 0))],
        out_shape=[jax.ShapeDtypeStruct((t, CONV_DIM), F32), jax.ShapeDtypeStruct((t, N_INP), BF16),
                   jax.ShapeDtypeStruct((1, SSD_HEADS), F32), jax.ShapeDtypeStruct((1, SSD_HEADS), F32),
                   jax.ShapeDtypeStruct((1, SSD_HEADS), F32), jax.ShapeDtypeStruct((1, D_SSD), F32)],
        scratch_shapes=[pltpu.VMEM((D_SSD, D_STATE), F32)],
        compiler_params=_cp("arbitrary", "arbitrary"),
    )(dy, xbc, proj, proj, hprev, dtb, alog, dskip, ng)


def _gmlp_chunk(gu, gv, ws, bs_cols, vg, og):
    n = gu[0].shape[0]
    mask = _tri(n, True)
    au = [_gelu(t) for t in gu]
    av = [_gelu(t) for t in gv]
    r = lax.rsqrt(sum(jnp.sum(t * t, axis=1, keepdims=True) for t in av) * (1.0 / D_GM) + EPS)
    p = []
    for h in range(GM_HEADS):
        sv = _bdot(ws[h] * mask, av[h] * r * vg[h]) + bs_cols[h]
        p.append(au[h] * sv)
    r2 = lax.rsqrt(sum(jnp.sum(t * t, axis=1, keepdims=True) for t in p) * (1.0 / D_GM) + EPS)
    return [p[h] * r2 * og[h] for h in range(GM_HEADS)]


def _gmlp_load(u_ref, v_ref, ws_ref, bst_ref, vg_ref, og_ref):
    gu = _hslices(u_ref, GM_HEAD_DIM, GM_HEADS)
    gv = _hslices(v_ref, GM_HEAD_DIM, GM_HEADS)
    ws = [ws_ref[h] for h in range(GM_HEADS)]
    bs_cols = [bst_ref[:, h:h + 1] for h in range(GM_HEADS)]
    return gu, gv, ws, bs_cols, _hslices(vg_ref, GM_HEAD_DIM, GM_HEADS), _hslices(og_ref, GM_HEAD_DIM, GM_HEADS)


def _gmlp_specs():
    return [pl.BlockSpec((CHUNK, D_GM), lambda i: (i, COL_U // D_GM)),
            pl.BlockSpec((CHUNK, D_GM), lambda i: (i, COL_V // D_GM)),
            pl.BlockSpec((GM_HEADS, CHUNK, CHUNK), lambda i: (0, 0, 0)),
            pl.BlockSpec((CHUNK, GM_HEADS), lambda i: (0, 0)),
            pl.BlockSpec((1, D_GM), lambda i: (0, 0)),
            pl.BlockSpec((1, D_GM), lambda i: (0, 0))]


def _gmlp_fwd(proj, ycat, ws, bst, vg, og):
    t = proj.shape[0]

    def body(u_ref, v_ref, ws_ref, bst_ref, vg_ref, og_ref, ycat_ref, o_ref):
        del ycat_ref
        outs = _gmlp_chunk(*_gmlp_load(u_ref, v_ref, ws_ref, bst_ref, vg_ref, og_ref))
        for h in range(GM_HEADS):
            o_ref[:, h * GM_HEAD_DIM:(h + 1) * GM_HEAD_DIM] = _b(outs[h])

    return pl.pallas_call(
        body, name="gmlp_fwd", grid=(t // CHUNK,),
        in_specs=_gmlp_specs() + [ANY],
        out_specs=pl.BlockSpec((CHUNK, D_GM), lambda i: (i, D_SSD // D_GM)),
        out_shape=jax.ShapeDtypeStruct(ycat.shape, ycat.dtype),
        input_output_aliases={6: 0},
        compiler_params=_cp("parallel"),
    )(proj, proj, ws, bst, vg, og, ycat)


def _gmlp_bwd(dy, proj, ws, bst, vg, og, dproj):
    t = proj.shape[0]
    w = GM_HEAD_DIM

    def body(dy_ref, u_ref, v_ref, ws_ref, bst_ref, vg_ref, og_ref, dproj_ref,
             dgm_ref, dws_ref, dbst_ref, dvg_ref, dog_ref):
        del dproj_ref

        @pl.when(pl.program_id(0) == 0)
        def _():
            dws_ref[...] = jnp.zeros_like(dws_ref)
            dbst_ref[...] = jnp.zeros_like(dbst_ref)
            dvg_ref[...] = jnp.zeros_like(dvg_ref)
            dog_ref[...] = jnp.zeros_like(dog_ref)

        _, vjp = jax.vjp(_gmlp_chunk, *_gmlp_load(u_ref, v_ref, ws_ref, bst_ref, vg_ref, og_ref))
        dgu, dgv, dws, dbs, dvg, dog = vjp(_hslices(dy_ref, w, GM_HEADS))
        for h in range(GM_HEADS):
            dgm_ref[:, h * w:(h + 1) * w] = _b(dgu[h])
            dgm_ref[:, D_GM + h * w: D_GM + (h + 1) * w] = _b(dgv[h])
            dws_ref[h] += dws[h]
            dbst_ref[:, h:h + 1] += dbs[h]
            dvg_ref[:, h * w:(h + 1) * w] += dvg[h]
            dog_ref[:, h * w:(h + 1) * w] += dog[h]

    return pl.pallas_call(
        body, name="gmlp_bwd", grid=(t // CHUNK,),
        in_specs=[pl.BlockSpec((CHUNK, D_GM), lambda i: (i, 1))] + _gmlp_specs() + [ANY],
        out_specs=[pl.BlockSpec((CHUNK, 2 * D_GM), lambda i: (i, COL_U // (2 * D_GM))),
                   pl.BlockSpec((GM_HEADS, CHUNK, CHUNK), lambda i: (0, 0, 0)),
                   pl.BlockSpec((CHUNK, GM_HEADS), lambda i: (0, 0)),
                   pl.BlockSpec((1, D_GM), lambda i: (0, 0)),
                   pl.BlockSpec((1, D_GM), lambda i: (0, 0))],
        out_shape=[jax.ShapeDtypeStruct(dproj.shape, dproj.dtype), jax.ShapeDtypeStruct((GM_HEADS, CHUNK, CHUNK), F32),
                   jax.ShapeDtypeStruct((CHUNK, GM_HEADS), F32), jax.ShapeDtypeStruct((1, D_GM), F32),
                   jax.ShapeDtypeStruct((1, D_GM), F32)],
        input_output_aliases={7: 0},
        compiler_params=_cp("arbitrary"),
    )(dy, proj, proj, ws, bst, vg, og, dproj)


def _local_step(x, target, mods, lw, final_g, *, nseq, big_w, grad_sink, small_sink):
    saved = []
    x0, delta, gate = x, None, None
    h1 = _normmod_fwd(x, lw[0]["norm1_g"], mods[0][1], mods[0][0], nseq=nseq, name="norm1_fwd_0")
    for l in range(DEPTH):
        w = lw[l]
        sh1, sc1, g1, sh2, sc2, g2 = mods[l]
        w_in = big_w(l, "w_in", h1)
        proj = _matmul(h1, w_in, tb=True, name=f"mm_in_{l}")
        xbc, xbc_pre = _ssd_conv_fwd(proj, w["ssd_conv_w"], w["ssd_conv_b"], nseq=nseq)
        ycat, hprev = _ssd_fwd(xbc, proj, w["ssd_dt_bias"], w["ssd_a_log"], w["ssd_d"], w["ssd_norm_g"], nseq=nseq)
        ycat = _gmlp_fwd(proj, ycat, w["gm_ws"], w["gm_bst"], w["gm_vnorm_g"], w["gm_out_g"])
        w_out = big_w(l, "w_out", ycat)
        mix, x1, h2 = _matmul_normfwd(ycat, w_out, x0, g1, w["norm2_g"], sc2, sh2, nseq=nseq, name=f"mm_out_{l}")
        ff_up = big_w(l, "ff_up", h2)
        up = _matmul(h2, ff_up, tb=True, name=f"mm_up_{l}", out_dtype=BF16)
        act = _ffn_act_fwd(up, w["ff_conv_w"], w["ff_conv_b"], nseq=nseq)
        ff_down = big_w(l, "ff_down", act)
        sv = dict(x0=x0, xin_delta=delta, xin_gate=gate, h1=h1, proj=proj, xbc=xbc, xbc_pre=xbc_pre, hprev=hprev,
                  ycat=ycat, mix=mix, x1=x1, h2=h2, up=up, act=act,
                  w_in=w_in, w_out=w_out, ff_up=ff_up, ff_down=ff_down)
        if l + 1 < DEPTH:
            nsh1, nsc1 = mods[l + 1][0], mods[l + 1][1]
            dn, x0, h1 = _matmul_normfwd(act, ff_down, x1, g2, lw[l + 1]["norm1_g"], nsc1, nsh1, nseq=nseq,
                                         name=f"mm_down_{l}")
        else:
            dn, loss, dx, ddelta, dgate, dfg = _matmul_loss(act, ff_down, x1, g2, final_g, target, nseq=nseq,
                                                            name=f"mm_down_{l}")
        saved.append(dict(sv, dn=dn))
        delta, gate = dn, g2

    small, dmods = [None] * DEPTH, [None] * DEPTH
    for l in reversed(range(DEPTH)):
        w, sv = lw[l], saved[l]
        sh1, sc1, g1, sh2, sc2, g2 = mods[l]
        dg2 = dgate
        g_ff_down = _matmul(sv["act"], ddelta, ta=True, name=f"mm_down_dw_{l}", out_dtype=BF16)
        dact = _matmul(ddelta, sv["ff_down"], tb=True, name=f"mm_down_dx_{l}", out_dtype=BF16)
        dgate_ff, dval_ff, dfcw, dfcb = _ffn_act_bwd(dact, sv["up"], w["ff_conv_w"], w["ff_conv_b"], nseq=nseq)
        g_ff_up = _matmul([dgate_ff, dval_ff], sv["h2"], ta=True, name=f"mm_up_dw_{l}", out_dtype=BF16)
        dep = grad_sink(l, "ffn", dict(ff_down=g_ff_down, ff_up=g_ff_up), dval_ff)
        dx, dmix, dg1, dn2g, dsc2, dsh2 = _matmul_normbwd([dgate_ff, dval_ff], sv["ff_up"], dx, sv["x1"], sv["mix"], g1,
                                                          w["norm2_g"], sc2, nseq=nseq, name=f"mm_up_dx_{l}", dep=dep)
        g_w_out = _matmul(sv["ycat"], dmix, ta=True, name=f"mm_out_dw_{l}", out_dtype=BF16)
        dep = grad_sink(l, "w_out", dict(w_out=g_w_out), dmix)
        dycat = _matmul(dmix, sv["w_out"], tb=True, name=f"mm_out_dx_{l}", dep=dep)
        dxbc_act, dproj, ddtb, dalog, ddsk, dng = _ssd_bwd(dycat, sv["xbc"], sv["proj"], sv["hprev"], w["ssd_dt_bias"],
                                                          w["ssd_a_log"], w["ssd_d"], w["ssd_norm_g"], nseq=nseq)
        dproj, dscw, dscb = _ssd_conv_bwd(dxbc_act, sv["xbc_pre"], sv["proj"], w["ssd_conv_w"], dproj, nseq=nseq)
        dproj, dws, dbst, dvg, dog = _gmlp_bwd(dycat, sv["proj"], w["gm_ws"], w["gm_bst"], w["gm_vnorm_g"], w["gm_out_g"], dproj)
        early = dict(norm2_g=dn2g, ssd_norm_g=dng, gm_vnorm_g=dvg, gm_out_g=dog,
                     ssd_conv_w=dscw, ssd_conv_b=dscb, ff_conv_w=dfcw, ff_conv_b=dfcb,
                     ssd_dt_bias=ddtb, ssd_a_log=dalog, ssd_d=ddsk, gm_ws=dws, gm_bs=dbst.T)
        dep = small_sink(l, early, small, dmods, dfg, loss)
        g_w_in = _matmul(dproj, sv["h1"], ta=True, name=f"mm_in_dw_{l}", out_dtype=BF16, dep=dep)
        dep = grad_sink(l, "w_in", dict(w_in=g_w_in), dproj)
        dx, ddelta, dgate, dn1g, dsc1, dsh1 = _matmul_normbwd(dproj, sv["w_in"], dx, sv["x0"], sv["xin_delta"],
                                                              sv["xin_gate"], w["norm1_g"], sc1, nseq=nseq,
                                                              name=f"mm_in_dx_{l}", dep=dep)
        small[l] = dict(early, norm1_g=dn1g)
        dmods[l] = jnp.concatenate([dsh1, dsc1, dg1, dsh2, dsc2, dg2], axis=-1)[:, 0, :]
    return dx, small, dmods


def _all_gather(arrs, name, dep=None):
    n = len(arrs)
    extra = [] if dep is None else [dep]

    def body(*refs):
        ins, outs = refs[:n], refs[n + len(extra):2 * n + len(extra)]
        send_sems, recv_sems, local_sems = refs[2 * n + len(extra):]
        x, y, c = lax.axis_index("x"), lax.axis_index("y"), lax.axis_index("c")
        me, sibling = (x, y, c), (x, y, 1 - c)
        chips = [(1 - x, y), (x, 1 - y), (1 - x, 1 - y)]

        def copy(i, k, block, to, src=None):
            px, py, pc = block
            dst = outs[i].at[4 * px + 2 * py + pc]
            return pltpu.make_async_remote_copy(
                src_ref=dst if src is None else src, dst_ref=dst,
                send_sem=send_sems.at[7 * i + k], recv_sem=recv_sems.at[7 * i + k],
                device_id=to, device_id_type=MESH)

        mine = [pltpu.make_async_copy(ins[i], outs[i].at[4 * x + 2 * y + c], local_sems.at[i]) for i in range(n)]
        for cp in mine:
            cp.start()
        first = []
        for i in range(n):
            first.append(copy(i, 0, me, sibling, src=ins[i]))
            first += [copy(i, 1 + j, me, (*chip, c), src=ins[i]) for j, chip in enumerate(chips)]
        for cp in first:
            cp.start()
        passed = []
        for j, chip in enumerate(chips):
            for i in range(n):
                copy(i, 1 + j, (*chip, c), me).wait_recv()
                fwd = copy(i, 4 + j, (*chip, c), sibling)
                fwd.start()
                passed.append(fwd)
        for i in range(n):
            copy(i, 0, sibling, me).wait_recv()
            for j, chip in enumerate(chips):
                copy(i, 4 + j, (*chip, 1 - c), me).wait_recv()
        for cp in first + passed:
            cp.wait_send()
        for cp in mine:
            cp.wait()

    return pl.pallas_call(
        body, name=name,
        in_specs=[ANY] * (n + len(extra)), out_specs=[ANY] * n,
        out_shape=[jax.ShapeDtypeStruct((N_DEV,) + a.shape, a.dtype) for a in arrs],
        scratch_shapes=[pltpu.SemaphoreType.DMA((7 * n,)), pltpu.SemaphoreType.DMA((7 * n,)),
                        pltpu.SemaphoreType.DMA((n,))],
    )(*arrs, *extra)


HBM = pl.BlockSpec(memory_space=pltpu.HBM)
SEM = pl.BlockSpec(memory_space=pltpu.SEMAPHORE)
EFFECT = pltpu.SideEffectType.DATAFLOW_SIDE_EFFECTING


def _peer(k):
    x, y, c = lax.axis_index("x"), lax.axis_index("y"), lax.axis_index("c")
    return (1 - x if k & 4 else x, 1 - y if k & 2 else y, 1 - c if k & 1 else c)


ALL_PEERS = tuple(range(1, N_DEV))
OTHER_CHIPS = (2, 4, 6)


def _xc_copies(scatter, srcs, lands, send_sems, recv_sems, peers=ALL_PEERS):
    x, y, c = lax.axis_index("x"), lax.axis_index("y"), lax.axis_index("c")
    copies = []
    for i in range(len(srcs)):
        for k in peers:
            px, py, pc = _peer(k)
            src = srcs[i].at[4 * px + 2 * py + pc] if scatter else srcs[i]
            dst = lands[i].at[k - 1] if scatter else lands[i].at[4 * x + 2 * y + c]
            copies.append(pltpu.make_async_remote_copy(
                src_ref=src, dst_ref=dst, send_sem=send_sems[i].at[k - 1], recv_sem=recv_sems[i].at[k - 1],
                device_id=(px, py, pc), device_id_type=MESH))
    return copies


def _xc_start(scatter, arrs, after, name, peers=ALL_PEERS):
    n = len(arrs)
    lands = [lax.empty((N_DEV - 1,) + a.shape[1:] if scatter else (N_DEV,) + a.shape, a.dtype) for a in arrs]

    def body(*refs):
        srcs, lnd = refs[:n], refs[n:2 * n]
        send_sems, recv_sems = refs[2 * n + 1:3 * n + 1], refs[3 * n + 1:4 * n + 1]
        token = refs[6 * n + 1]
        for cp in _xc_copies(scatter, srcs, lnd, send_sems, recv_sems, peers):
            cp.start()
        token[...] = jnp.zeros_like(token)

    outs = pl.pallas_call(
        body, name=name,
        out_shape=[pltpu.SemaphoreType.DMA((N_DEV - 1,))] * (2 * n)
        + [pltpu.HBM(a.shape, a.dtype) for a in arrs] + [pltpu.HBM(a.shape, a.dtype) for a in lands]
        + [jax.ShapeDtypeStruct((8, 128), F32)],
        in_specs=[HBM] * (2 * n) + [ANY],
        out_specs=[SEM] * (2 * n) + [HBM] * (2 * n) + [pl.BlockSpec(memory_space=pltpu.VMEM)],
        input_output_aliases={i: 2 * n + i for i in range(2 * n)},
        compiler_params=pltpu.CompilerParams(has_side_effects=EFFECT),
    )(*[pltpu.with_memory_space_constraint(a, pltpu.HBM) for a in list(arrs) + lands], after)
    return outs[:n], outs[n:2 * n], outs[2 * n:3 * n], outs[3 * n:4 * n], outs[4 * n][0, 0]


def _xc_wait(scatter, send_sems, recv_sems, srcs, lands, after, name, peers=ALL_PEERS):
    n = len(srcs)

    def body(*refs):
        s_refs, l_refs = refs[:n], refs[n:2 * n]
        ss, rs = refs[2 * n:3 * n], refs[3 * n:4 * n]
        for cp in _xc_copies(scatter, s_refs, l_refs, ss, rs, peers):
            cp.wait_send()
            cp.wait_recv()

    outs = pl.pallas_call(
        body, name=name,
        out_shape=[pltpu.HBM(a.shape, a.dtype) for a in list(srcs) + list(lands)],
        in_specs=[HBM] * (2 * n) + [SEM] * (2 * n) + [ANY],
        out_specs=[HBM] * (2 * n),
        input_output_aliases={i: i for i in range(2 * n)},
        compiler_params=pltpu.CompilerParams(has_side_effects=EFFECT),
    )(*srcs, *lands, *send_sems, *recv_sems, after)
    return outs[:n], outs[n:]


def _sib_copies(zones, send_sems, recv_sems):
    x, y, c = lax.axis_index("x"), lax.axis_index("y"), lax.axis_index("c")
    copies = []
    for i in range(len(zones)):
        for q in range(N_DEV // 2):
            slot = zones[i].at[2 * q + c]
            copies.append(pltpu.make_async_remote_copy(
                src_ref=slot, dst_ref=slot, send_sem=send_sems[i].at[q], recv_sem=recv_sems[i].at[q],
                device_id=(x, y, 1 - c), device_id_type=MESH))
    return copies


def _sib_start(zones, name):
    n = len(zones)

    def body(*refs):
        for cp in _sib_copies(refs[:n], refs[n:2 * n], refs[2 * n:3 * n]):
            cp.start()

    outs = pl.pallas_call(
        body, name=name,
        out_shape=[pltpu.SemaphoreType.DMA((N_DEV // 2,))] * (2 * n) + [pltpu.HBM(a.shape, a.dtype) for a in zones],
        in_specs=[HBM] * n,
        out_specs=[SEM] * (2 * n) + [HBM] * n,
        input_output_aliases={i: 2 * n + i for i in range(n)},
        compiler_params=pltpu.CompilerParams(has_side_effects=EFFECT),
    )(*[pltpu.with_memory_space_constraint(a, pltpu.HBM) for a in zones])
    return outs[:n], outs[n:2 * n], outs[2 * n:]


def _sib_wait(send_sems, recv_sems, zones, name):
    n = len(zones)

    def body(*refs):
        for cp in _sib_copies(refs[:n], refs[n:2 * n], refs[2 * n:3 * n]):
            cp.wait_send()
            cp.wait_recv()

    return pl.pallas_call(
        body, name=name,
        out_shape=[pltpu.HBM(a.shape, a.dtype) for a in zones],
        in_specs=[HBM] * n + [SEM] * (2 * n),
        out_specs=[HBM] * n,
        input_output_aliases={i: i for i in range(n)},
        compiler_params=pltpu.CompilerParams(has_side_effects=EFFECT),
    )(*zones, *send_sems, *recv_sems)


def _adamw_math(w, g, m, v):
    m = ADAM_B1 * m + (1.0 - ADAM_B1) * g
    v = ADAM_B2 * v + (1.0 - ADAM_B2) * (g * g)
    m_hat = m / (1.0 - ADAM_B1 ** ADAM_STEP)
    v_hat = v / (1.0 - ADAM_B2 ** ADAM_STEP)
    delta = -ADAM_LR * (m_hat / (jnp.sqrt(v_hat) + ADAM_EPS) + ADAM_WD * w)
    return delta, m, v


def _adamw_sharded(parts, w, m, v, pos, name):
    depth, rows, cols = w.shape
    tr = _tile(rows, 256) if rows % 8 == 0 else rows
    npart = len(parts)

    def body(pos_ref, *refs):
        prefs = refs[:npart]
        w_ref, m_ref, v_ref, g_out, d_out, m_out, v_out = refs[npart:]
        g = prefs[0][...]
        for pr in prefs[1:]:
            g = g + pr[...]
        delta, mn, vn = _adamw_math(w_ref[...], g, m_ref[...], v_ref[...])
        g_out[...] = g
        d_out[...] = delta
        m_out[...] = mn
        v_out[...] = vn

    def part_spec(fn):
        return pl.BlockSpec((1, tr, cols), lambda l, i, p: (fn(p) * depth + l, i, 0))

    blk = pl.BlockSpec((1, tr, cols), lambda l, i, p: (l, i, 0))
    shp = jax.ShapeDtypeStruct((depth, rows, cols), F32)
    return pl.pallas_call(
        body, name=name,
        grid_spec=pltpu.PrefetchScalarGridSpec(
            num_scalar_prefetch=1, grid=(depth, rows // tr),
            in_specs=[part_spec(fn) for _, fn in parts] + [blk, blk, blk],
            out_specs=[blk, blk, blk, blk]),
        out_shape=[shp, shp, shp, shp],
        compiler_params=_cp("parallel", "parallel"),
    )(pos, *[a for a, _ in parts], w, m, v)


def _adamw_layer(parts, w, m, v, pos, layer, prev, name):
    depth, rows, cols = w.shape
    npart = len(parts)
    nprev = 0 if prev is None else 4
    if rows % 16 == 0:
        tr, tc = max(t for t in range(16, 257, 16) if rows % t == 0), cols
    else:
        tr, tc = rows, _tile(cols, 256)
    pick = (lambda i: (i, 0)) if rows % 16 == 0 else (lambda i: (0, i))

    def body(pos_ref, *refs):
        prefs = refs[:npart]
        w_ref, m_ref, v_ref = refs[npart:npart + 3]
        g_out, d_out, m_out, v_out = refs[npart + 3 + nprev:]
        g = prefs[0][...].astype(F32)
        for pr in prefs[1:]:
            g = g + pr[...].astype(F32)
        delta, mn, vn = _adamw_math(w_ref[...], g, m_ref[...], v_ref[...])
        g_out[...] = g
        d_out[...] = delta
        m_out[...] = mn
        v_out[...] = vn

    def part_spec(fn):
        return pl.BlockSpec((1, tr, tc), lambda i, p: (fn(p), *pick(i)))

    blk = pl.BlockSpec((1, tr, tc), lambda i, p: (layer, *pick(i)))
    shp = jax.ShapeDtypeStruct((depth, rows, cols), F32)
    first_prev = 1 + npart + 3
    return pl.pallas_call(
        body, name=name,
        grid_spec=pltpu.PrefetchScalarGridSpec(
            num_scalar_prefetch=1, grid=(rows // tr * (cols // tc),),
            in_specs=[part_spec(fn) for _, fn in parts] + [blk, blk, blk] + [ANY] * nprev,
            out_specs=[blk, blk, blk, blk]),
        out_shape=[shp, shp, shp, shp],
        input_output_aliases={first_prev + j: j for j in range(nprev)},
        compiler_params=_cp("parallel"),
    )(pos, *[a for a, _ in parts], w, m, v, *(prev or ()))


_P1024 = ["norm1_g", "norm2_g", "ssd_norm_g", "gm_vnorm_g", "gm_out_g"]
_P16 = ["ssd_dt_bias", "ssd_a_log", "ssd_d"]


def _adamw_small(gath, wmv):
    names = list(wmv.keys())
    classes = list(gath.keys())
    flat_in = [gath[k] for k in classes]
    for nme in names:
        flat_in += list(wmv[nme])
    out_shapes = []
    for nme in names:
        out_shapes += [jax.ShapeDtypeStruct(wmv[nme][0].shape, F32)] * 4
    out_shapes += [jax.ShapeDtypeStruct((DEPTH, SSD_CONV, CONV_DIM), F32), jax.ShapeDtypeStruct((DEPTH, FF_CONV, D_FF), F32),
                   jax.ShapeDtypeStruct((1, SSD_HEADS), F32)]
    scratch = [pltpu.VMEM(gath[k].shape[1:], F32) for k in classes]
    ncls = len(classes)

    def body(*refs):
        g_refs = dict(zip(classes, refs[:ncls]))
        pos = ncls
        w_refs = {}
        for nme in names:
            w_refs[nme] = refs[pos:pos + 3]
            pos += 3
        o_refs = {}
        for nme in names:
            o_refs[nme] = refs[pos:pos + 4]
            pos += 4
        scw_out, fcw_out, loss_out = refs[pos], refs[pos + 1], refs[pos + 2]
        s_refs = dict(zip(classes, refs[pos + 3:]))
        for k in classes:
            acc = g_refs[k][0]
            for dev in range(1, N_DEV):
                acc = acc + g_refs[k][dev]
            s_refs[k][...] = acc

        def apply(nme, grad_of):
            w_ref, m_ref, v_ref = w_refs[nme]
            g_out, d_out, m_out, v_out = o_refs[nme]
            shape = w_ref.shape
            if len(shape) == 2:
                idxs = [(slice(l, l + 1),) for l in range(shape[0])]
            elif len(shape) == 3:
                idxs = [(l,) for l in range(shape[0])]
            else:
                idxs = [(l, h) for l in range(shape[0]) for h in range(shape[1])]
            for n_i, ix in enumerate(idxs):
                g = grad_of(n_i)
                delta, mn, vn = _adamw_math(w_ref[ix], g, m_ref[ix], v_ref[ix])
                g_out[ix] = g
                d_out[ix] = delta
                m_out[ix] = mn
                v_out[ix] = vn

        s1024, s1536, s2816, s16, s128, s6144, late1024, late6144 = (s_refs[k] for k in classes)
        s1024[0:1, :] += late1024[...]
        s6144[0:late6144.shape[0], :] += late6144[...]
        for n_i, nme in enumerate(_P1024):
            apply(nme, lambda l, b=2 * n_i: s1024[b + l:b + l + 1, :])
        apply("final_g", lambda l: s1024[10:11, :])
        apply("ssd_conv_b", lambda l: s1536[8 + l:9 + l, :])
        apply("ff_conv_b", lambda l: s2816[6 + l:7 + l, :])
        for n_i, nme in enumerate(_P16):
            apply(nme, lambda l, b=2 * n_i: s16[b + l:b + l + 1, :])
        apply("gm_ws", lambda q: s128[q * CHUNK:(q + 1) * CHUNK, :])
        apply("gm_bs", lambda l: s128[2048 + 8 * l:2048 + 8 * (l + 1), :])
        apply("ada_b", lambda l: s6144[2 * l:2 * l + 1, :] + s6144[2 * l + 1:2 * l + 2, :])
        for l in range(DEPTH):
            scw_out[l] = s1536[SSD_CONV * l:SSD_CONV * (l + 1), :]
            fcw_out[l] = s2816[FF_CONV * l:FF_CONV * (l + 1), :]
        loss_out[...] = s16[2 * len(_P16):2 * len(_P16) + 1, :]

    outs = pl.pallas_call(
        body, name="adamw_small",
        out_shape=out_shapes,
        scratch_shapes=scratch,
        compiler_params=pltpu.CompilerParams(vmem_limit_bytes=VMEM_LIMIT),
    )(*flat_in)
    res = {nme: tuple(outs[4 * i:4 * i + 4]) for i, nme in enumerate(names)}
    return res, outs[-3], outs[-2], outs[-1]


_WEIGHTS = ['ada_w', 'ada_b', 'norm1_g', 'norm2_g', 'w_in', 'ssd_conv_w', 'ssd_conv_b', 'ssd_dt_bias', 'ssd_a_log',
            'ssd_d', 'ssd_norm_g', 'gm_vnorm_g', 'gm_ws', 'gm_bs', 'gm_out_g', 'w_out', 'ff_up', 'ff_conv_w',
            'ff_conv_b', 'ff_down', 'final_g']


_O_XBC, _O_DT, _O_GM = D_SSD, D_SSD + CONV_DIM, D_SSD + CONV_DIM + SSD_HEADS


_TRANSPOSED = ("w_in", "ff_up")


def _full_weight(name, g):
    full = g.reshape(g.shape[0] * g.shape[1], g.shape[2])
    if name != "w_in":
        return full
    zpad = jnp.zeros((N_INP - N_IN, full.shape[1]), full.dtype)
    return jnp.concatenate([full[_O_GM:], full[:_O_XBC], full[_O_XBC:_O_DT], full[_O_DT:_O_GM], zpad], axis=0)


def _by_owner(name, grad):
    if name == "w_in":
        grad = jnp.concatenate([grad[COL_Z:COL_XBC], grad[COL_XBC:COL_DT], grad[COL_DT:COL_DT + SSD_HEADS], grad[:COL_Z]], axis=0)
    return grad.reshape(N_DEV, grad.shape[0] // N_DEV, grad.shape[1])


def kernel(x, c, ada_w, ada_b, norm1_g, norm2_g, w_in, ssd_conv_w, ssd_conv_b, ssd_dt_bias, ssd_a_log, ssd_d, ssd_norm_g, gm_vnorm_g, gm_ws, gm_bs, gm_out_g, w_out, ff_up, ff_conv_w, ff_conv_b, ff_down, final_g, loss_target, m_ada_w, m_ada_b, m_norm1_g, m_norm2_g, m_w_in, m_ssd_conv_w, m_ssd_conv_b, m_ssd_dt_bias, m_ssd_a_log, m_ssd_d, m_ssd_norm_g, m_gm_vnorm_g, m_gm_ws, m_gm_bs, m_gm_out_g, m_w_out, m_ff_up, m_ff_conv_w, m_ff_conv_b, m_ff_down, m_final_g, v_ada_w, v_ada_b, v_norm1_g, v_norm2_g, v_w_in, v_ssd_conv_w, v_ssd_conv_b, v_ssd_dt_bias, v_ssd_a_log, v_ssd_d, v_ssd_norm_g, v_gm_vnorm_g, v_gm_ws, v_gm_bs, v_gm_out_g, v_w_out, v_ff_up, v_ff_conv_w, v_ff_conv_b, v_ff_down, v_final_g):
    given = dict(locals())
    wts = {n: given[n] for n in _WEIGHTS}
    mom = {n: given["m_" + n] for n in _WEIGHTS}
    var = {n: given["v_" + n] for n in _WEIGHTS}
    nseq, seq, d = x.shape
    ix, iy, ic = lax.axis_index("x"), lax.axis_index("y"), lax.axis_index("c")
    me = 4 * ix + 2 * iy + ic
    me_arr = me.astype(jnp.int32).reshape(1)

    for nme in _TRANSPOSED:
        wts[nme], mom[nme], var[nme] = (jnp.transpose(a, (0, 2, 1)) for a in (wts[nme], mom[nme], var[nme]))

    def shard(l, name):
        return _b(wts[name][l])

    g_scw, g_fcw, c_all = _all_gather([ssd_conv_w, ff_conv_w, c], "gather_first")
    scw_f = jnp.transpose(g_scw, (1, 2, 0, 3)).reshape(DEPTH, SSD_CONV, CONV_DIM)
    fcw_f = jnp.transpose(g_fcw, (1, 2, 0, 3)).reshape(DEPTH, FF_CONV, D_FF)
    c_all = c_all.reshape(N_DEV * nseq, d)

    n_ada = ada_w.shape[2]
    ada_b_shard = lax.dynamic_slice_in_dim(ada_b, me * n_ada, n_ada, axis=1).reshape(DEPTH, 1, n_ada)
    mod_part, c_act = _ada_fwd(c_all, ada_w, ada_b_shard)
    (mod_g,) = _all_gather([mod_part], "gather_mod")
    mod_all = jnp.transpose(mod_g, (1, 2, 0, 3)).reshape(DEPTH, N_DEV * nseq, N_MOD * d)
    mod_mine = lax.dynamic_slice_in_dim(mod_all, me * nseq, nseq, axis=1)
    mods = [[mod_mine[l, :, k * d:(k + 1) * d].reshape(nseq, 1, d) for k in range(N_MOD)] for l in range(DEPTH)]

    first_ssem, first_rsem, first_src, first_land, first_zero = _xc_start(
        False, [shard(0, "w_in")], mod_g, "ag_first_start", peers=OTHER_CHIPS)
    later = [(0, "w_out"), (0, "ff_up"), (0, "ff_down"), (1, "w_in"), (1, "w_out"), (1, "ff_up"), (1, "ff_down")]
    ag_ssem, ag_rsem, ag_src, ag_land, ag_zero = _xc_start(
        False, [shard(l, n) for l, n in later], first_zero.reshape(1, 1), "ag_start")
    ag_groups = {(0, "w_out"): [0], (0, "ff_up"): [1, 2], (1, "w_in"): [3, 4], (1, "ff_up"): [5, 6]}
    big_cache = {}

    def big_w(l, name, after):
        if (l, name) == (0, "w_in") and (l, name) not in big_cache:
            srcs, lands = _xc_wait(False, first_ssem, first_rsem, first_src, first_land, after, "ag_first_wait",
                                   peers=OTHER_CHIPS)
            zone = lax.dynamic_update_index_in_dim(lands[0], srcs[0], me, 0)
            (zone,) = _sib_wait(*_sib_start([zone], "ag_first_sib_start"), "ag_first_sib_wait")
            big_cache[(l, name)] = _full_weight(name, zone)
        if (l, name) not in big_cache:
            idx = ag_groups[(l, name)]
            pick = lambda seq_: [seq_[i] for i in idx]
            srcs, lands = _xc_wait(False, pick(ag_ssem), pick(ag_rsem), pick(ag_src), pick(ag_land), after,
                                   f"ag_wait_{l}_{name}")
            for i, src, land in zip(idx, srcs, lands):
                big_cache[later[i]] = _full_weight(later[i][1], lax.dynamic_update_index_in_dim(land, src, me, 0))
        return big_cache[(l, name)]

    lw = []
    for l in range(DEPTH):
        lw.append(dict(
            norm1_g=norm1_g[l:l + 1] + (ag_zero if l == 0 else 0.0), norm2_g=norm2_g[l:l + 1], ssd_conv_w=scw_f[l],
            ssd_conv_b=ssd_conv_b[l:l + 1], ssd_dt_bias=ssd_dt_bias[l:l + 1], ssd_a_log=ssd_a_log[l:l + 1],
            ssd_d=ssd_d[l:l + 1], ssd_norm_g=ssd_norm_g[l:l + 1], gm_vnorm_g=gm_vnorm_g[l:l + 1], gm_ws=gm_ws[l],
            gm_bst=gm_bs[l].T, gm_out_g=gm_out_g[l:l + 1], ff_conv_w=fcw_f[l], ff_conv_b=ff_conv_b[l:l + 1]))

    outs = {}
    pending = {}

    def rs_finish(l, group, after):
        names, ssem, rsem, srcs, lands = pending.pop((l, group))
        srcs, lands = _xc_wait(True, ssem, rsem, srcs, lands, after, f"rs_wait_{l}_{group}")
        for nme, own, land in zip(names, srcs, lands):
            parts = [(own, lambda p: p[0])] + [(land, lambda p, k=k: k) for k in range(N_DEV - 1)]
            outs[nme] = _adamw_layer(parts, wts[nme], mom[nme], var[nme], me_arr, l, outs.get(nme), f"adamw_{nme}_{l}")
        return outs[names[-1]][0]

    def grad_sink(l, group, grads, after):
        names = list(grads)
        ssem, rsem, srcs, lands, zero = _xc_start(True, [_by_owner(n, grads[n]) for n in names], after, f"rs_start_{l}_{group}")
        pending[(l, group)] = (names, ssem, rsem, srcs, lands)
        return zero.reshape(1, 1)

    early_gather = {}

    def small_sink(l, early, small, dmods, dfg, loss_p):
        if l > 0:
            return None
        layers = [dict(early, norm1_g=jnp.zeros((1, d), F32))] + small[1:]
        rows = lambda name: [layers[k][name] for k in range(DEPTH)]
        packed = [
            jnp.concatenate(sum([rows(n) for n in _P1024], []) + [dfg], axis=0),
            jnp.concatenate(rows("ssd_conv_w") + rows("ssd_conv_b"), axis=0),
            jnp.concatenate(rows("ff_conv_w") + rows("ff_conv_b"), axis=0),
            jnp.concatenate(sum([rows(n) for n in _P16], []) + [loss_p[:, :SSD_HEADS]], axis=0),
            jnp.concatenate([layers[k]["gm_ws"].reshape(GM_HEADS * CHUNK, CHUNK) for k in range(DEPTH)] + rows("gm_bs"), axis=0),
            jnp.concatenate([jnp.zeros((nseq, N_MOD * d), F32)] + dmods[1:], axis=0)]
        ssem, rsem, srcs, lands, zero = _xc_start(False, packed, packed[0], "small_start")
        early_gather.update(ssem=ssem, rsem=rsem, srcs=srcs, lands=lands)
        return zero.reshape(1, 1)

    grad_x, small, dmods = _local_step(
        x.reshape(nseq * seq, d), loss_target.reshape(nseq * seq, d), mods, lw, final_g.reshape(1, d), nseq=nseq,
        big_w=big_w, grad_sink=grad_sink, small_sink=small_sink)

    done = grad_x
    for l, grp in ((1, "ffn"), (1, "w_out"), (1, "w_in"), (0, "ffn"), (0, "w_out")):
        done = rs_finish(l, grp, done)
    srcs, lands = _xc_wait(False, early_gather["ssem"], early_gather["rsem"], early_gather["srcs"],
                           early_gather["lands"], done, "small_wait")
    gathered = [lax.dynamic_update_index_in_dim(land, src, me, 0) for src, land in zip(srcs, lands)]
    gathered += _all_gather([small[0]["norm1_g"], dmods[0]], "gather_late", dep=gathered[0])
    gath = dict(zip(["p1024", "p1536", "p2816", "p16", "p128", "p6144", "late1024", "late6144"], gathered))

    dmod_all = jnp.concatenate([gath["late6144"].reshape(1, N_DEV * nseq, N_MOD * d),
                                jnp.transpose(gath["p6144"].reshape(N_DEV, DEPTH, nseq, N_MOD * d)[:, 1:], (1, 0, 2, 3)).reshape(
                                    DEPTH - 1, N_DEV * nseq, N_MOD * d)], axis=0)
    small_names = _P1024 + ["final_g", "ssd_conv_b", "ff_conv_b"] + _P16 + ["gm_ws", "gm_bs", "ada_b"]
    wmv = {}
    for nme in small_names:
        if nme == "final_g":
            wmv[nme] = tuple(a.reshape(1, d) for a in (wts[nme], mom[nme], var[nme]))
        else:
            wmv[nme] = (wts[nme], mom[nme], var[nme])
    small_out, scw_full, fcw_full, loss_sum = _adamw_small(gath, wmv)
    loss = loss_sum[0, 0]
    rs_finish(0, "w_in", scw_full)
    for nme in small_names:
        outs[nme] = small_out[nme]
    outs["final_g"] = tuple(a.reshape(d) for a in outs["final_g"])

    n_scw, n_fcw = ssd_conv_w.shape[2], ff_conv_w.shape[2]
    g_scw_mine = lax.dynamic_slice_in_dim(scw_full, me * n_scw, n_scw, axis=2)
    g_fcw_mine = lax.dynamic_slice_in_dim(fcw_full, me * n_fcw, n_fcw, axis=2)
    outs["ssd_conv_w"] = _adamw_sharded([(g_scw_mine, lambda p: 0)], ssd_conv_w, m_ssd_conv_w, v_ssd_conv_w, me_arr, "adamw_ssd_conv_w")
    outs["ff_conv_w"] = _adamw_sharded([(g_fcw_mine, lambda p: 0)], ff_conv_w, m_ff_conv_w, v_ff_conv_w, me_arr, "adamw_ff_conv_w")

    dmod_cols = _b(lax.dynamic_slice_in_dim(dmod_all, me * n_ada, n_ada, axis=2))
    g_ada = jnp.stack([_matmul(c_act, dmod_cols[l], ta=True, name=f"mm_ada_dw_{l}") for l in range(DEPTH)])
    outs["ada_w"] = _adamw_sharded([(g_ada, lambda p: 0)], ada_w, m_ada_w, v_ada_w, me_arr, "adamw_ada_w")

    for nme in _TRANSPOSED:
        outs[nme] = tuple(jnp.transpose(a, (0, 2, 1)) for a in outs[nme])
    result = [loss, grad_x.reshape(nseq, seq, d)]
    for k in range(4):
        result += [outs[n][k] for n in _WEIGHTS]
    return tuple(result)
```

```python
import functools
import math

import jax
import jax.numpy as jnp
from jax import lax
from jax.experimental import pallas as pl
from jax.experimental.pallas import tpu as pltpu

F32 = jnp.float32
BF16 = jnp.bfloat16

N_DEV = 8
D_MODEL = 1024
DEPTH = 2
CHUNK = 128
SSD_HEADS = 16
SSD_HEAD_DIM = 64
SSD_GROUPS = 2
HEADS_PER_GROUP = SSD_HEADS // SSD_GROUPS
GROUP_WIDTH = HEADS_PER_GROUP * SSD_HEAD_DIM
D_STATE = 128
D_SSD = 1024
CONV_DIM = 1536
SSD_CONV = 4
GM_HEADS = 8
GM_HEAD_DIM = 128
D_GM = 1024
D_FF = 2816
FF_CONV = 3
N_IN = 4624
N_MOD = 6
EPS = 1e-6

N_INP = 5120
COL_U, COL_V, COL_Z, COL_XBC, COL_DT = 0, 1024, 2048, 3072, 4608

ADAM_LR = 0.001
ADAM_B1 = 0.9
ADAM_B2 = 0.999
ADAM_EPS = 1e-08
ADAM_WD = 0.01
ADAM_STEP = 10

VMEM_LIMIT = 56 * 1024 * 1024
MESH = pl.DeviceIdType.MESH
ANY = pl.BlockSpec(memory_space=pl.ANY)


def _cp(*sem):
    return pltpu.CompilerParams(dimension_semantics=sem, vmem_limit_bytes=VMEM_LIMIT)


def _tile(n, pref):
    if n <= pref or n % 128:
        return n
    best = 128
    for t in range(128, pref + 1, 128):
        if n % t == 0:
            best = t
    return best


def _silu(x):
    return x * jax.nn.sigmoid(x)


def _gelu(x):
    return 0.5 * x * (1.0 + lax.erf(x * (1.0 / math.sqrt(2.0))))


def _softplus(x):
    return jnp.maximum(x, 0.0) + jnp.log1p(jnp.exp(-jnp.abs(x)))


def _b(x):
    return x.astype(BF16)


_NN = (((1,), (0,)), ((), ()))
_NT = (((1,), (1,)), ((), ()))
_TN = (((0,), (0,)), ((), ()))


def _dg(a, b, dn):
    return lax.dot_general(_b(a), _b(b), dn, preferred_element_type=F32)


@jax.custom_vjp
def _bdot(a, b):
    return _dg(a, b, _NN)


def _bdot_fwd(a, b):
    return _dg(a, b, _NN), (a, b)


def _bdot_bwd(res, ct):
    a, b = res
    return _dg(ct, b, _NT), _dg(a, ct, _TN)


_bdot.defvjp(_bdot_fwd, _bdot_bwd)


@jax.custom_vjp
def _bdot_nt(a, b):
    return _dg(a, b, _NT)


def _bdot_nt_fwd(a, b):
    return _dg(a, b, _NT), (a, b)


def _bdot_nt_bwd(res, ct):
    a, b = res
    return _dg(ct, b, _NN), _dg(ct, a, _TN)


_bdot_nt.defvjp(_bdot_nt_fwd, _bdot_nt_bwd)


@jax.custom_vjp
def _bdot_tn(a, b):
    return _dg(a, b, _TN)


def _bdot_tn_fwd(a, b):
    return _dg(a, b, _TN), (a, b)


def _bdot_tn_bwd(res, ct):
    a, b = res
    return _dg(b, ct, _NT), _dg(a, ct, _NN)


_bdot_tn.defvjp(_bdot_tn_fwd, _bdot_tn_bwd)


def _tri(n, lower):
    r = lax.broadcasted_iota(jnp.int32, (n, n), 0)
    c = lax.broadcasted_iota(jnp.int32, (n, n), 1)
    return ((r >= c) if lower else (r <= c)).astype(F32)


def _eye(n):
    r = lax.broadcasted_iota(jnp.int32, (n, n), 0)
    c = lax.broadcasted_iota(jnp.int32, (n, n), 1)
    return (r == c).astype(F32)


def _hdot(a, b, dn):
    return lax.dot_general(a, b, dn, precision=lax.Precision.HIGHEST, preferred_element_type=F32)


@jax.custom_vjp
def _cumsum_rows(x):
    return _hdot(_tri(x.shape[0], True), x, _NN)


def _cumsum_rows_fwd(x):
    return _cumsum_rows(x), None


def _cumsum_rows_bwd(_, ct):
    return (_hdot(_tri(ct.shape[0], False), ct, _NN),)


_cumsum_rows.defvjp(_cumsum_rows_fwd, _cumsum_rows_bwd)


@jax.custom_vjp
def _transpose(x):
    return _hdot(_eye(x.shape[1]), x, _NT)


def _transpose_fwd(x):
    return _transpose(x), None


def _transpose_bwd(_, ct):
    return (_hdot(_eye(ct.shape[1]), ct, _NT),)


_transpose.defvjp(_transpose_fwd, _transpose_bwd)


MXU_WIDTH = 256
MATMUL_TILE_CAP = 2816
MATMUL_VMEM = 44 * 1024 * 1024


def _mxu_tiles(n):
    if n <= MATMUL_TILE_CAP or n % 128:
        return [n]
    for unit in (MXU_WIDTH, 128):
        opts = [t for t in range(unit, MATMUL_TILE_CAP + 1, unit) if n % t == 0]
        if opts:
            return opts
    return [n]


def _matmul(a, b, *, ta=False, tb=False, name, dep=None, out_dtype=F32):
    pieces = list(a) if isinstance(a, (list, tuple)) else [a]
    npc = len(pieces)
    rows, width = pieces[0].shape
    assert all(p.shape == (rows, width) for p in pieces)
    if ta:
        k_dim, m_dim = rows, width * npc
    else:
        m_dim, k_dim = rows, width * npc
    if tb:
        n_dim, kb = b.shape
    else:
        kb, n_dim = b.shape
    assert kb == k_dim, (pieces[0].shape, npc, b.shape, ta, tb)
    m_unit = width if npc > 1 and ta else m_dim
    k_unit = width if npc > 1 and not ta else k_dim
    tm = _tile(m_unit, 1536)
    tn_opts, tk_opts = _mxu_tiles(n_dim), _mxu_tiles(k_unit)
    tn, tk = tn_opts.pop(), tk_opts.pop()
    while 4 * (tm * tk + tk * tn) + 8 * tm * tn > MATMUL_VMEM:
        if tn >= tk and tn_opts:
            tn = tn_opts.pop()
        else:
            tk = tk_opts.pop()
    ni, nj, nk = m_dim // tm, n_dim // tn, k_dim // tk
    per = width // (tm if ta else tk)
    dn = (((0 if ta else 1,), (1 if tb else 0,)), ((), ()))

    a_bytes, b_bytes = m_dim * k_dim, k_dim * n_dim
    m_outer = nk > 1 or a_bytes + b_bytes * ni <= b_bytes + a_bytes * nj
    if m_outer:
        ij = lambda o, n, k: (o, n)
        grid = (ni, nj, nk)
    else:
        ij = lambda o, n, k: (n, o)
        grid = (nj, ni, nk)

    use_acc = nk > 1 and out_dtype != F32

    def body(*refs):
        a_refs, b_ref = refs[:npc], refs[npc]
        o_ref = refs[-2] if use_acc else refs[-1]
        acc_ref = refs[-1]
        k = pl.program_id(2)
        i = pl.program_id(0 if m_outer else 1)
        along = i if ta else k

        def step(a_ref):
            p = lax.dot_general(a_ref[...], b_ref[...], dn, preferred_element_type=F32)
            if nk == 1:
                o_ref[...] = p.astype(out_dtype)
            else:
                @pl.when(k == 0)
                def _():
                    acc_ref[...] = p

                @pl.when((k > 0) & (k < nk - 1 if use_acc else True))
                def _():
                    acc_ref[...] += p

                if use_acc:
                    @pl.when(k == nk - 1)
                    def _():
                        o_ref[...] = (acc_ref[...] + p).astype(out_dtype)

        if npc == 1:
            step(a_refs[0])
        else:
            for pc in range(npc):
                pl.when((along >= pc * per) & (along < (pc + 1) * per))(functools.partial(step, a_refs[pc]))

    def a_map(pc, o, n, k):
        i, _ = ij(o, n, k)
        along = i if ta else k
        if npc > 1:
            along = jnp.clip(along - pc * per, 0, per - 1)
        return (k, along) if ta else (i, along)

    def b_map(o, n, k):
        _, j = ij(o, n, k)
        return (j, k) if tb else (k, j)

    extra = [] if dep is None else [dep]
    return pl.pallas_call(
        body, name=name,
        grid=grid,
        in_specs=[pl.BlockSpec((tk, tm) if ta else (tm, tk), functools.partial(a_map, pc)) for pc in range(npc)]
        + [pl.BlockSpec((tn, tk) if tb else (tk, tn), b_map)] + [ANY] * len(extra),
        out_specs=pl.BlockSpec((tm, tn), lambda o, n, k: ij(o, n, k)),
        out_shape=jax.ShapeDtypeStruct((m_dim, n_dim), out_dtype),
        scratch_shapes=[pltpu.VMEM((tm, tn), F32)] if use_acc else [],
        compiler_params=_cp("parallel", "parallel", "arbitrary"),
    )(*pieces, b, *extra)


def _ada_fwd(c_all, ada_w, ada_b_shard):
    depth, d, n = ada_w.shape
    nb = c_all.shape[0]

    def body(c_ref, w_ref, b_ref, o_ref, ca_ref):
        ca = _silu(c_ref[...])
        ca_ref[...] = _b(ca)
        o_ref[0] = _dg(ca, w_ref[0], _NN) + b_ref[0]

    return pl.pallas_call(
        body, name="ada_fwd",
        grid=(depth,),
        in_specs=[pl.BlockSpec((nb, d), lambda l: (0, 0)),
                  pl.BlockSpec((1, d, n), lambda l: (l, 0, 0)),
                  pl.BlockSpec((1, 1, n), lambda l: (l, 0, 0))],
        out_specs=[pl.BlockSpec((1, nb, n), lambda l: (l, 0, 0)),
                   pl.BlockSpec((nb, d), lambda l: (0, 0))],
        out_shape=[jax.ShapeDtypeStruct((depth, nb, n), F32), jax.ShapeDtypeStruct((nb, d), BF16)],
        compiler_params=_cp("arbitrary"),
    )(c_all, ada_w, ada_b_shard)


def _fold(acc):
    return jnp.sum(acc, axis=0, keepdims=True)


def _rinv(x):
    return lax.rsqrt(jnp.sum(x * x, axis=-1, keepdims=True) * (1.0 / D_MODEL) + EPS)


def _rms_bwd(a, xhat, rinv):
    return rinv * (a - xhat * (jnp.sum(a * xhat, axis=-1, keepdims=True) * (1.0 / D_MODEL)))


def _row_tile(seq):
    return min(seq, 256)


def _normmod_fwd(x, g, sc, sh, *, nseq, name):
    t, d = x.shape
    seq = t // nseq
    tr = _row_tile(seq)
    nt = seq // tr
    row = pl.BlockSpec((tr, d), lambda s, i: (s * nt + i, 0))
    per_seq = pl.BlockSpec((1, 1, d), lambda s, i: (s, 0, 0))

    def body(x_ref, g_ref, sc_ref, sh_ref, h_ref):
        x_v = x_ref[...]
        h_ref[...] = _b(x_v * _rinv(x_v) * (g_ref[...] * (1.0 + sc_ref[0])) + sh_ref[0])

    return pl.pallas_call(
        body, name=name, grid=(nseq, nt),
        in_specs=[row, pl.BlockSpec((1, d), lambda s, i: (0, 0)), per_seq, per_seq],
        out_specs=row,
        out_shape=jax.ShapeDtypeStruct((t, d), BF16),
        compiler_params=_cp("parallel", "parallel"),
    )(x, g, sc, sh)


NORM_TM = 512


def _matmul_normbwd(a, b, dxo, x, delta, gate, g, sc, *, nseq, name, dep=None):
    pieces = list(a) if isinstance(a, (list, tuple)) else [a]
    npc = len(pieces)
    t, width = pieces[0].shape
    k_dim, d = width * npc, b.shape[1]
    assert b.shape[0] == k_dim and all(p.shape == (t, width) for p in pieces)
    seq = t // nseq
    tm = min(NORM_TM, seq)
    per_seq_tiles = seq // tm
    tk = _mxu_tiles(width if npc > 1 else k_dim).pop()
    nk, per = k_dim // tk, width // tk
    has_delta = delta is not None
    extra = [] if dep is None else [dep]

    def body(*refs):
        a_refs, b_ref = refs[:npc], refs[npc]
        dxo_ref, x_ref = refs[npc + 1], refs[npc + 2]
        pos = npc + 3
        if has_delta:
            delta_ref, gate_ref = refs[pos], refs[pos + 1]
            pos += 2
        g_ref, sc_ref = refs[pos], refs[pos + 1]
        pos += 2 + len(extra)
        if has_delta:
            dx_ref, dd_ref, dgate_ref, dg_ref, dsc_ref, dsh_ref = refs[pos:pos + 6]
        else:
            dx_ref, dg_ref, dsc_ref, dsh_ref = refs[pos:pos + 4]
        acc_ref = refs[-1]
        i, k = pl.program_id(0), pl.program_id(1)

        def norm_bwd(dh_v):
            g_v, one_sc = g_ref[...], 1.0 + sc_ref[0]
            x_v = x_ref[...]
            rinv = _rinv(x_v)
            xhat = x_v * rinv
            dx = dxo_ref[...] + _rms_bwd(dh_v * (g_v * one_sc), xhat, rinv)
            dx_ref[...] = dx

            @pl.when(i == 0)
            def _():
                dg_ref[...] = jnp.zeros_like(dg_ref)

            @pl.when(i % per_seq_tiles == 0)
            def _():
                dsc_ref[...] = jnp.zeros_like(dsc_ref)
                dsh_ref[...] = jnp.zeros_like(dsh_ref)
                if has_delta:
                    dgate_ref[...] = jnp.zeros_like(dgate_ref)

            t_sum = _fold(dh_v * xhat)
            dg_ref[...] += t_sum * one_sc
            dsc_ref[0] += t_sum * g_v
            dsh_ref[0] += _fold(dh_v)
            if has_delta:
                dd_ref[...] = _b(dx * gate_ref[0])
                dgate_ref[0] += _fold(dx * delta_ref[...])

        def step(a_ref):
            p = lax.dot_general(a_ref[...], b_ref[...], _NN, preferred_element_type=F32)
            if nk == 1:
                norm_bwd(p)
            else:
                @pl.when(k == 0)
                def _():
                    acc_ref[...] = p

                @pl.when((k > 0) & (k < nk - 1))
                def _():
                    acc_ref[...] += p

                @pl.when(k == nk - 1)
                def _():
                    norm_bwd(acc_ref[...] + p)

        if npc == 1:
            step(a_refs[0])
        else:
            for pc in range(npc):
                pl.when((k >= pc * per) & (k < (pc + 1) * per))(functools.partial(step, a_refs[pc]))

    def a_map(pc, i, k):
        return (i, jnp.clip(k - pc * per, 0, per - 1) if npc > 1 else k)

    row = pl.BlockSpec((tm, d), lambda i, k: (i, 0))
    per_seq = pl.BlockSpec((1, 1, d), lambda i, k: (i // per_seq_tiles, 0, 0))
    vec = pl.BlockSpec((1, d), lambda i, k: (0, 0))
    shp = lambda *s, dt=F32: jax.ShapeDtypeStruct(s, dt)
    in_specs = [pl.BlockSpec((tm, tk), functools.partial(a_map, pc)) for pc in range(npc)]
    in_specs += [pl.BlockSpec((tk, d), lambda i, k: (k, 0)), row, row]
    operands = [*pieces, b, dxo, x]
    if has_delta:
        in_specs += [row, per_seq]
        operands += [delta, gate]
    in_specs += [vec, per_seq] + [ANY] * len(extra)
    operands += [g, sc, *extra]
    if has_delta:
        out_specs = [row, row, per_seq, vec, per_seq, per_seq]
        out_shape = [shp(t, d), shp(t, d, dt=BF16), shp(nseq, 1, d), shp(1, d), shp(nseq, 1, d), shp(nseq, 1, d)]
    else:
        out_specs = [row, vec, per_seq, per_seq]
        out_shape = [shp(t, d), shp(1, d), shp(nseq, 1, d), shp(nseq, 1, d)]
    outs = pl.pallas_call(
        body, name=name, grid=(t // tm, nk),
        in_specs=in_specs, out_specs=out_specs, out_shape=out_shape,
        scratch_shapes=[pltpu.VMEM((tm, d), F32)],
        compiler_params=_cp("arbitrary", "arbitrary"),
    )(*operands)
    if has_delta:
        return tuple(outs)
    dx, dg, dsc, dsh = outs
    return dx, None, None, dg, dsc, dsh


def _matmul_normfwd(a, b, xin, gate, g, sc, sh, *, nseq, name):
    t, k_dim = a.shape
    d = b.shape[1]
    assert b.shape[0] == k_dim and k_dim <= MATMUL_TILE_CAP
    seq = t // nseq
    tm = min(NORM_TM, seq)
    per_seq_tiles = seq // tm

    def body(a_ref, b_ref, xin_ref, gate_ref, g_ref, sc_ref, sh_ref, dl_ref, x_ref, h_ref):
        dl = lax.dot_general(a_ref[...], b_ref[...], _NN, preferred_element_type=F32)
        dl_ref[...] = dl
        x = xin_ref[...] + gate_ref[0] * dl
        x_ref[...] = x
        h_ref[...] = _b(x * _rinv(x) * (g_ref[...] * (1.0 + sc_ref[0])) + sh_ref[0])

    row = pl.BlockSpec((tm, d), lambda i: (i, 0))
    per_seq = pl.BlockSpec((1, 1, d), lambda i: (i // per_seq_tiles, 0, 0))
    return pl.pallas_call(
        body, name=name, grid=(t // tm,),
        in_specs=[pl.BlockSpec((tm, k_dim), lambda i: (i, 0)), pl.BlockSpec((k_dim, d), lambda i: (0, 0)),
                  row, per_seq, pl.BlockSpec((1, d), lambda i: (0, 0)), per_seq, per_seq],
        out_specs=[row, row, row],
        out_shape=[jax.ShapeDtypeStruct((t, d), F32), jax.ShapeDtypeStruct((t, d), F32), jax.ShapeDtypeStruct((t, d), BF16)],
        compiler_params=_cp("parallel"),
    )(a, b, xin, gate, g, sc, sh)


def _matmul_loss(a, b, xin, gate, fg, target, *, nseq, name):
    t, k_dim = a.shape
    d = b.shape[1]
    assert b.shape[0] == k_dim and k_dim <= MATMUL_TILE_CAP
    seq = t // nseq
    tm = min(NORM_TM, seq)
    per_seq_tiles = seq // tm

    def body(a_ref, b_ref, xin_ref, gate_ref, fg_ref, tgt_ref, dl_ref, loss_ref, dx_ref, dd_ref, dgate_ref, dfg_ref):
        i = pl.program_id(0)
        fg_v, gate_v = fg_ref[...], gate_ref[0]
        dl = lax.dot_general(a_ref[...], b_ref[...], _NN, preferred_element_type=F32)
        dl_ref[...] = dl
        x = xin_ref[...] + gate_v * dl
        rinv = _rinv(x)
        xhat = x * rinv
        err = xhat * fg_v - tgt_ref[...]
        dx = _rms_bwd(err * fg_v * (1.0 / d), xhat, rinv)
        dx_ref[...] = dx
        dd_ref[...] = _b(dx * gate_v)

        @pl.when(i == 0)
        def _():
            loss_ref[...] = jnp.zeros_like(loss_ref)
            dfg_ref[...] = jnp.zeros_like(dfg_ref)

        @pl.when(i % per_seq_tiles == 0)
        def _():
            dgate_ref[...] = jnp.zeros_like(dgate_ref)

        loss_ref[...] += jnp.sum(err * err) * (0.5 / d)
        dfg_ref[...] += _fold(err * xhat) * (1.0 / d)
        dgate_ref[0] += _fold(dx * dl)

    row = pl.BlockSpec((tm, d), lambda i: (i, 0))
    per_seq = pl.BlockSpec((1, 1, d), lambda i: (i // per_seq_tiles, 0, 0))
    vec = pl.BlockSpec((1, d), lambda i: (0, 0))
    return pl.pallas_call(
        body, name=name, grid=(t // tm,),
        in_specs=[pl.BlockSpec((tm, k_dim), lambda i: (i, 0)), pl.BlockSpec((k_dim, d), lambda i: (0, 0)),
                  row, per_seq, vec, row],
        out_specs=[row, pl.BlockSpec((1, 128), lambda i: (0, 0)), row, row, per_seq, vec],
        out_shape=[jax.ShapeDtypeStruct((t, d), F32), jax.ShapeDtypeStruct((1, 128), F32), jax.ShapeDtypeStruct((t, d), F32),
                   jax.ShapeDtypeStruct((t, d), BF16), jax.ShapeDtypeStruct((nseq, 1, d), F32),
                   jax.ShapeDtypeStruct((1, d), F32)],
        compiler_params=_cp("arbitrary"),
    )(a, b, xin, gate, fg, target)


CONV_TC = 256
CONV_LANES = 128
CONV_ROWS = 64
CONV_HALO = 8


def _conv_slabs(seq, fn):
    def step(i, carry):
        r0 = pl.multiple_of(i * CONV_ROWS, CONV_ROWS)
        for h in range(CONV_TC // CONV_LANES):
            fn(r0, slice(h * CONV_LANES, (h + 1) * CONV_LANES))
        return carry

    lax.fori_loop(0, seq // CONV_ROWS, step, 0)


def _slab(ref, r0, cols, seq):
    after = ref[pl.ds(pl.multiple_of(jnp.minimum(r0 + CONV_ROWS, seq - CONV_HALO), CONV_HALO), CONV_HALO), cols]
    return jnp.concatenate([ref[pl.ds(r0, CONV_ROWS), cols], jnp.where(r0 + CONV_ROWS < seq, after, 0.0)], axis=0)


def _conv_block(x, w_ref, b_ref):
    kw = w_ref.shape[0]
    rows = lax.broadcasted_iota(jnp.int32, x.shape, 0)
    y = b_ref[...] + w_ref[kw - 1:kw, :] * x
    for j in range(1, kw):
        y = y + w_ref[kw - 1 - j:kw - j, :] * jnp.where(rows >= j, pltpu.roll(x, j, 0), 0.0)
    return y


def _conv_block_bwd(dy, x, w_ref, dw_ref, db_ref):
    kw = w_ref.shape[0]
    n = x.shape[0]
    rows = lax.broadcasted_iota(jnp.int32, x.shape, 0)
    dx = w_ref[kw - 1:kw, :] * dy
    dw_ref[kw - 1:kw, :] += jnp.sum(dy * x, axis=0, keepdims=True)
    for j in range(1, kw):
        dy_j = jnp.where(rows < n - j, pltpu.roll(dy, n - j, 0), 0.0)
        dx = dx + w_ref[kw - 1 - j:kw - j, :] * dy_j
        dw_ref[kw - 1 - j:kw - j, :] += jnp.sum(dy_j * x, axis=0, keepdims=True)
    db_ref[...] += jnp.sum(dy, axis=0, keepdims=True)
    return dx


def _conv_bwd(dy_ext, x, w_ref, dw_ref, db_ref, cols):
    kw = w_ref.shape[0]
    n = dy_ext.shape[0]
    dy = dy_ext[:CONV_ROWS]
    dx = w_ref[kw - 1:kw, cols] * dy
    dw_ref[kw - 1:kw, cols] += jnp.sum(dy * x, axis=0, keepdims=True)
    for j in range(1, kw):
        dy_j = pltpu.roll(dy_ext, n - j, 0)[:CONV_ROWS]
        dx = dx + w_ref[kw - 1 - j:kw - j, cols] * dy_j
        dw_ref[kw - 1 - j:kw - j, cols] += jnp.sum(dy_j * x, axis=0, keepdims=True)
    db_ref[:, cols] += jnp.sum(dy, axis=0, keepdims=True)
    return dx


def _dsilu(pre):
    sg = jax.nn.sigmoid(pre)
    return pre * sg, sg * (1.0 + pre * (1.0 - sg))


def _ssd_conv_fwd(proj, w, b, *, nseq):
    t = proj.shape[0]
    seq = t // nseq
    nb = CONV_DIM // CONV_TC
    off = COL_XBC // CONV_TC

    def body(x_ref, w_ref, b_ref, o_ref, pre_ref):
        pre = _conv_block(x_ref[...], w_ref, b_ref)
        pre_ref[...] = pre
        o_ref[...] = _silu(pre)

    col = pl.BlockSpec((seq, CONV_TC), lambda j, s: (s, j))
    return pl.pallas_call(
        body, name="ssd_conv_fwd", grid=(nb, nseq),
        in_specs=[pl.BlockSpec((seq, CONV_TC), lambda j, s: (s, off + j)),
                  pl.BlockSpec((SSD_CONV, CONV_TC), lambda j, s: (0, j)),
                  pl.BlockSpec((1, CONV_TC), lambda j, s: (0, j))],
        out_specs=[col, col],
        out_shape=[jax.ShapeDtypeStruct((t, CONV_DIM), F32)] * 2,
        compiler_params=_cp("parallel", "parallel"),
    )(proj, w, b)


def _ssd_conv_bwd(dact, pre, proj, w, dproj, *, nseq):
    t = proj.shape[0]
    seq = t // nseq
    nb = CONV_DIM // CONV_TC
    off = COL_XBC // CONV_TC

    def body(da_ref, pre_ref, x_ref, w_ref, dproj_ref, dx_ref, dw_ref, db_ref):
        del dproj_ref

        @pl.when(pl.program_id(1) == 0)
        def _():
            dw_ref[...] = jnp.zeros_like(dw_ref)
            db_ref[...] = jnp.zeros_like(db_ref)

        def slab(r0, cols):
            _, dsilu = _dsilu(_slab(pre_ref, r0, cols, seq))
            dpre_ext = _slab(da_ref, r0, cols, seq) * dsilu
            x = x_ref[pl.ds(r0, CONV_ROWS), cols]
            dx_ref[pl.ds(r0, CONV_ROWS), cols] = _b(_conv_bwd(dpre_ext, x, w_ref, dw_ref, db_ref, cols))

        _conv_slabs(seq, slab)

    return pl.pallas_call(
        body, name="ssd_conv_bwd", grid=(nb, nseq),
        in_specs=[pl.BlockSpec((seq, CONV_TC), lambda j, s: (s, j)),
                  pl.BlockSpec((seq, CONV_TC), lambda j, s: (s, j)),
                  pl.BlockSpec((seq, CONV_TC), lambda j, s: (s, off + j)),
                  pl.BlockSpec((SSD_CONV, CONV_TC), lambda j, s: (0, j)),
                  ANY],
        out_specs=[pl.BlockSpec((seq, CONV_TC), lambda j, s: (s, off + j)),
                   pl.BlockSpec((SSD_CONV, CONV_TC), lambda j, s: (0, j)),
                   pl.BlockSpec((1, CONV_TC), lambda j, s: (0, j))],
        out_shape=[jax.ShapeDtypeStruct(dproj.shape, dproj.dtype), jax.ShapeDtypeStruct((SSD_CONV, CONV_DIM), F32),
                   jax.ShapeDtypeStruct((1, CONV_DIM), F32)],
        input_output_aliases={4: 0},
        compiler_params=_cp("parallel", "arbitrary"),
    )(dact, pre, proj, w, dproj)


def _ffn_act_fwd(up, w, b, *, nseq):
    t = up.shape[0]
    seq = t // nseq
    nb = D_FF // CONV_TC

    def body(g_ref, v_ref, w_ref, b_ref, o_ref):
        o_ref[...] = _b(_silu(_conv_block(g_ref[...].astype(F32), w_ref, b_ref)) * v_ref[...].astype(F32))

    col = pl.BlockSpec((seq, CONV_TC), lambda j, s: (s, j))
    return pl.pallas_call(
        body, name="ffn_act_fwd", grid=(nb, nseq),
        in_specs=[col,
                  pl.BlockSpec((seq, CONV_TC), lambda j, s: (s, nb + j)),
                  pl.BlockSpec((FF_CONV, CONV_TC), lambda j, s: (0, j)),
                  pl.BlockSpec((1, CONV_TC), lambda j, s: (0, j))],
        out_specs=col,
        out_shape=jax.ShapeDtypeStruct((t, D_FF), BF16),
        compiler_params=_cp("parallel", "parallel"),
    )(up, up, w, b)


def _ffn_act_bwd(dact, up, w, b, *, nseq):
    t = up.shape[0]
    seq = t // nseq
    nb = D_FF // CONV_TC

    def body(da_ref, g_ref, v_ref, w_ref, b_ref, dg_ref, dv_ref, dw_ref, db_ref):
        @pl.when(pl.program_id(1) == 0)
        def _():
            dw_ref[...] = jnp.zeros_like(dw_ref)
            db_ref[...] = jnp.zeros_like(db_ref)

        gate = g_ref[...].astype(F32)
        silu, dsilu = _dsilu(_conv_block(gate, w_ref, b_ref))
        da = da_ref[...].astype(F32)
        dv_ref[...] = _b(da * silu)
        dg_ref[...] = _b(_conv_block_bwd(da * v_ref[...].astype(F32) * dsilu, gate, w_ref, dw_ref, db_ref))

    col = pl.BlockSpec((seq, CONV_TC), lambda j, s: (s, j))
    return pl.pallas_call(
        body, name="ffn_act_bwd", grid=(nb, nseq),
        in_specs=[col, col,
                  pl.BlockSpec((seq, CONV_TC), lambda j, s: (s, nb + j)),
                  pl.BlockSpec((FF_CONV, CONV_TC), lambda j, s: (0, j)),
                  pl.BlockSpec((1, CONV_TC), lambda j, s: (0, j))],
        out_specs=[col, col,
                   pl.BlockSpec((FF_CONV, CONV_TC), lambda j, s: (0, j)),
                   pl.BlockSpec((1, CONV_TC), lambda j, s: (0, j))],
        out_shape=[jax.ShapeDtypeStruct((t, D_FF), BF16), jax.ShapeDtypeStruct((t, D_FF), BF16),
                   jax.ShapeDtypeStruct((FF_CONV, D_FF), F32), jax.ShapeDtypeStruct((1, D_FF), F32)],
        compiler_params=_cp("parallel", "arbitrary"),
    )(dact, up, up, w, b)


SSD_PAIRS = SSD_HEADS // 2
PAIR_W = 2 * SSD_HEAD_DIM
PAIRS_PER_GROUP = SSD_PAIRS // SSD_GROUPS


def _ssd_chunk(xs, bg, cg, dtr, z, hp, dtb, alog, dskip, ng):
    n = dtr.shape[0]
    dt = _softplus(dtr + dtb)
    cs = _cumsum_rows(dt * (-jnp.exp(alog)))
    cs_t = _transpose(cs)
    lane = lax.broadcasted_iota(jnp.int32, (1, SSD_HEADS), 1)
    sub = lax.broadcasted_iota(jnp.int32, (SSD_HEADS, 1), 0)
    row = lax.broadcasted_iota(jnp.int32, (n, 1), 0)
    causal = lax.broadcasted_iota(jnp.int32, (n, n), 0) >= lax.broadcasted_iota(jnp.int32, (n, n), 1)
    future = jnp.where(causal, 0.0, -1e30)
    first = lax.broadcasted_iota(jnp.int32, (1, PAIR_W), 1) < SSD_HEAD_DIM
    first_rows = lax.broadcasted_iota(jnp.int32, (PAIR_W, 1), 0) < SSD_HEAD_DIM
    first_f = first.astype(F32)
    cb = [_bdot_nt(cg[g], bg[g]) for g in range(SSD_GROUPS)]
    ys, hn = [], []
    for p in range(SSD_PAIRS):
        g = p // PAIRS_PER_GROUP
        col, decay, last = [], [], []
        for h in (2 * p, 2 * p + 1):
            oh = (lane == h).astype(F32)
            cs_h = jnp.sum(cs * oh, axis=1, keepdims=True)
            cs_row = jnp.sum(cs_t * (sub == h).astype(F32), axis=0, keepdims=True)
            col.append((jnp.sum(dt * oh, axis=1, keepdims=True), cs_h, jnp.sum(dskip * oh, axis=1, keepdims=True)))
            last.append(jnp.sum(jnp.where(row == n - 1, cs_h, 0.0), axis=0, keepdims=True))
            decay.append(jnp.exp(cs_h - cs_row + future))
        pair = lambda a, b: jnp.where(first, a, b)
        dt_p = pair(col[0][0], col[1][0])
        cs_p = pair(col[0][1], col[1][1])
        last_p = pair(last[0], last[1])
        xc = xs[p] * dt_p
        y = _bdot(cb[g] * decay[0], xc * first_f) + _bdot(cb[g] * decay[1], xc * (1.0 - first_f))
        y = y + _bdot_nt(cg[g], hp[p]) * jnp.exp(cs_p)
        y = y + pair(col[0][2], col[1][2]) * xs[p]
        keep = jnp.where(first_rows, jnp.exp(last[0]), jnp.exp(last[1]))
        hn.append(keep * hp[p] + _bdot_tn(xc * jnp.exp(last_p - cs_p), bg[g]))
        ys.append(y * _silu(z[p]))
    outs = []
    for g in range(SSD_GROUPS):
        ps = range(g * PAIRS_PER_GROUP, (g + 1) * PAIRS_PER_GROUP)
        ms = sum(jnp.sum(ys[p] * ys[p], axis=1, keepdims=True) for p in ps) * (1.0 / GROUP_WIDTH)
        r = lax.rsqrt(ms + EPS)
        outs += [ys[p] * r * ng[p] for p in ps]
    return outs, hn


def _hslices(ref, width, count, base=0):
    return [ref[:, base + k * width: base + (k + 1) * width] for k in range(count)]


def _ssd_load(xbc_ref, z_ref, dt_ref, ng_ref):
    xs = _hslices(xbc_ref, PAIR_W, SSD_PAIRS)
    bg = _hslices(xbc_ref, D_STATE, SSD_GROUPS, D_SSD)
    cg = _hslices(xbc_ref, D_STATE, SSD_GROUPS, D_SSD + SSD_GROUPS * D_STATE)
    z = _hslices(z_ref, PAIR_W, SSD_PAIRS)
    ng = _hslices(ng_ref, PAIR_W, SSD_PAIRS)
    return xs, bg, cg, dt_ref[:, 0:SSD_HEADS], z, ng


def _ssd_specs(nch):
    rowi = lambda s, c: s * nch + c
    return [pl.BlockSpec((CHUNK, CONV_DIM), lambda s, c: (rowi(s, c), 0)),
            pl.BlockSpec((CHUNK, D_SSD), lambda s, c: (rowi(s, c), COL_Z // D_SSD)),
            pl.BlockSpec((CHUNK, 128), lambda s, c: (rowi(s, c), COL_DT // 128)),
            pl.BlockSpec((1, SSD_HEADS), lambda s, c: (0, 0)),
            pl.BlockSpec((1, SSD_HEADS), lambda s, c: (0, 0)),
            pl.BlockSpec((1, SSD_HEADS), lambda s, c: (0, 0)),
            pl.BlockSpec((1, D_SSD), lambda s, c: (0, 0))]


def _ssd_fwd(xbc, proj, dtb, alog, dskip, ng, *, nseq):
    t = proj.shape[0]
    nch = t // nseq // CHUNK
    hd = PAIR_W

    def body(xbc_ref, z_ref, dt_ref, dtb_ref, alog_ref, dsk_ref, ng_ref, y_ref, hp_ref, h_ref):
        @pl.when(pl.program_id(1) == 0)
        def _():
            h_ref[...] = jnp.zeros_like(h_ref)

        xs, bg, cg, dtr, z, ngs = _ssd_load(xbc_ref, z_ref, dt_ref, ng_ref)
        hp_ref[0] = h_ref[...]
        hp = [h_ref[h * hd:(h + 1) * hd, :] for h in range(SSD_PAIRS)]
        outs, hn = _ssd_chunk(xs, bg, cg, dtr, z, hp, dtb_ref[...], alog_ref[...], dsk_ref[...], ngs)
        for h in range(SSD_PAIRS):
            y_ref[:, h * hd:(h + 1) * hd] = _b(outs[h])
            h_ref[h * hd:(h + 1) * hd, :] = hn[h]

    return pl.pallas_call(
        body, name="ssd_fwd", grid=(nseq, nch),
        in_specs=_ssd_specs(nch),
        out_specs=[pl.BlockSpec((CHUNK, D_SSD), lambda s, c: (s * nch + c, 0)),
                   pl.BlockSpec((1, D_SSD, D_STATE), lambda s, c: (s * nch + c, 0, 0))],
        out_shape=[jax.ShapeDtypeStruct((t, D_SSD + D_GM), BF16),
                   jax.ShapeDtypeStruct((t // CHUNK, D_SSD, D_STATE), F32)],
        scratch_shapes=[pltpu.VMEM((D_SSD, D_STATE), F32)],
        compiler_params=_cp("arbitrary", "arbitrary"),
    )(xbc, proj, proj, dtb, alog, dskip, ng)


def _ssd_bwd(dy, xbc, proj, hprev, dtb, alog, dskip, ng, *, nseq):
    t = proj.shape[0]
    nch = t // nseq // CHUNK
    hd = PAIR_W
    rev = lambda s, c: s * nch + (nch - 1 - c)

    def body(dy_ref, xbc_ref, z_ref, dt_ref, hp_ref, dtb_ref, alog_ref, dsk_ref, ng_ref,
             dxbc_ref, dproj_ref, ddtb_ref, dalog_ref, ddsk_ref, dng_ref, dh_ref):
        first = (pl.program_id(0) == 0) & (pl.program_id(1) == 0)

        @pl.when(pl.program_id(1) == 0)
        def _():
            dh_ref[...] = jnp.zeros_like(dh_ref)

        @pl.when(first)
        def _():
            ddtb_ref[...] = jnp.zeros_like(ddtb_ref)
            dalog_ref[...] = jnp.zeros_like(dalog_ref)
            ddsk_ref[...] = jnp.zeros_like(ddsk_ref)
            dng_ref[...] = jnp.zeros_like(dng_ref)

        xs, bg, cg, dtr, z, ngs = _ssd_load(xbc_ref, z_ref, dt_ref, ng_ref)
        hp = [hp_ref[0, h * hd:(h + 1) * hd, :] for h in range(SSD_PAIRS)]
        _, vjp = jax.vjp(_ssd_chunk, xs, bg, cg, dtr, z, hp, dtb_ref[...], alog_ref[...], dsk_ref[...], ngs)
        douts = [dy_ref[:, h * hd:(h + 1) * hd] for h in range(SSD_PAIRS)]
        dhn = [dh_ref[h * hd:(h + 1) * hd, :] for h in range(SSD_PAIRS)]
        dxs, dbg, dcg, ddtr, dz, dhp, ddtb, dalog, ddsk, dngs = vjp((douts, dhn))
        dproj_ref[:, :COL_Z] = jnp.zeros((CHUNK, COL_Z), BF16)
        dproj_ref[:, COL_XBC:] = jnp.zeros((CHUNK, N_INP - COL_XBC), BF16)
        for h in range(SSD_PAIRS):
            dxbc_ref[:, h * hd:(h + 1) * hd] = dxs[h]
            dproj_ref[:, COL_Z + h * hd: COL_Z + (h + 1) * hd] = _b(dz[h])
            dh_ref[h * hd:(h + 1) * hd, :] = dhp[h]
            dng_ref[:, h * hd:(h + 1) * hd] += dngs[h]
        for g in range(SSD_GROUPS):
            dxbc_ref[:, D_SSD + g * D_STATE: D_SSD + (g + 1) * D_STATE] = dbg[g]
            dxbc_ref[:, D_SSD + (SSD_GROUPS + g) * D_STATE: D_SSD + (SSD_GROUPS + g + 1) * D_STATE] = dcg[g]
        dproj_ref[:, COL_DT:COL_DT + SSD_HEADS] = _b(ddtr)
        ddtb_ref[...] += ddtb
        dalog_ref[...] += dalog
        ddsk_ref[...] += ddsk

    small = pl.BlockSpec((1, SSD_HEADS), lambda s, c: (0, 0))
    return pl.pallas_call(
        body, name="ssd_bwd", grid=(nseq, nch),
        in_specs=[pl.BlockSpec((CHUNK, D_SSD), lambda s, c: (rev(s, c), 0)),
                  pl.BlockSpec((CHUNK, CONV_DIM), lambda s, c: (rev(s, c), 0)),
                  pl.BlockSpec((CHUNK, D_SSD), lambda s, c: (rev(s, c), COL_Z // D_SSD)),
                  pl.BlockSpec((CHUNK, 128), lambda s, c: (rev(s, c), COL_DT // 128)),
                  pl.BlockSpec((1, D_SSD, D_STATE), lambda s, c: (rev(s, c), 0, 0)),
                  small, small, small,
                  pl.BlockSpec((1, D_SSD), lambda s, c: (0, 0))],
        out_specs=[pl.BlockSpec((CHUNK, CONV_DIM), lambda s, c: (rev(s, c), 0)),
                   pl.BlockSpec((CHUNK, N_INP), lambda s, c: (rev(s, c), 0)),
                   small, small, small,
                   pl.BlockSpec((1, D_SSD), lambda s, c: (0, 0))],
        out_shape=[jax.ShapeDtypeStruct((t, CONV_DIM), F32), jax.ShapeDtypeStruct((t, N_INP), BF16),
                   jax.ShapeDtypeStruct((1, SSD_HEADS), F32), jax.ShapeDtypeStruct((1, SSD_HEADS), F32),
                   jax.ShapeDtypeStruct((1, SSD_HEADS), F32), jax.ShapeDtypeStruct((1, D_SSD), F32)],
        scratch_shapes=[pltpu.VMEM((D_SSD, D_STATE), F32)],
        compiler_params=_cp("arbitrary", "arbitrary"),
    )(dy, xbc, proj, proj, hprev, dtb, alog, dskip, ng)


def _gmlp_chunk(gu, gv, ws, bs_cols, vg, og):
    n = gu[0].shape[0]
    mask = _tri(n, True)
    au = [_gelu(t) for t in gu]
    av = [_gelu(t) for t in gv]
    r = lax.rsqrt(sum(jnp.sum(t * t, axis=1, keepdims=True) for t in av) * (1.0 / D_GM) + EPS)
    p = []
    for h in range(GM_HEADS):
        sv = _bdot(ws[h] * mask, av[h] * r * vg[h]) + bs_cols[h]
        p.append(au[h] * sv)
    r2 = lax.rsqrt(sum(jnp.sum(t * t, axis=1, keepdims=True) for t in p) * (1.0 / D_GM) + EPS)
    return [p[h] * r2 * og[h] for h in range(GM_HEADS)]


def _gmlp_load(u_ref, v_ref, ws_ref, bst_ref, vg_ref, og_ref):
    gu = _hslices(u_ref, GM_HEAD_DIM, GM_HEADS)
    gv = _hslices(v_ref, GM_HEAD_DIM, GM_HEADS)
    ws = [ws_ref[h] for h in range(GM_HEADS)]
    bs_cols = [bst_ref[:, h:h + 1] for h in range(GM_HEADS)]
    return gu, gv, ws, bs_cols, _hslices(vg_ref, GM_HEAD_DIM, GM_HEADS), _hslices(og_ref, GM_HEAD_DIM, GM_HEADS)


def _gmlp_specs():
    return [pl.BlockSpec((CHUNK, D_GM), lambda i: (i, COL_U // D_GM)),
            pl.BlockSpec((CHUNK, D_GM), lambda i: (i, COL_V // D_GM)),
            pl.BlockSpec((GM_HEADS, CHUNK, CHUNK), lambda i: (0, 0, 0)),
            pl.BlockSpec((CHUNK, GM_HEADS), lambda i: (0, 0)),
            pl.BlockSpec((1, D_GM), lambda i: (0, 0)),
            pl.BlockSpec((1, D_GM), lambda i: (0, 0))]


def _gmlp_fwd(proj, ycat, ws, bst, vg, og):
    t = proj.shape[0]

    def body(u_ref, v_ref, ws_ref, bst_ref, vg_ref, og_ref, ycat_ref, o_ref):
        del ycat_ref
        outs = _gmlp_chunk(*_gmlp_load(u_ref, v_ref, ws_ref, bst_ref, vg_ref, og_ref))
        for h in range(GM_HEADS):
            o_ref[:, h * GM_HEAD_DIM:(h + 1) * GM_HEAD_DIM] = _b(outs[h])

    return pl.pallas_call(
        body, name="gmlp_fwd", grid=(t // CHUNK,),
        in_specs=_gmlp_specs() + [ANY],
        out_specs=pl.BlockSpec((CHUNK, D_GM), lambda i: (i, D_SSD // D_GM)),
        out_shape=jax.ShapeDtypeStruct(ycat.shape, ycat.dtype),
        input_output_aliases={6: 0},
        compiler_params=_cp("parallel"),
    )(proj, proj, ws, bst, vg, og, ycat)


def _gmlp_bwd(dy, proj, ws, bst, vg, og, dproj):
    t = proj.shape[0]
    w = GM_HEAD_DIM

    def body(dy_ref, u_ref, v_ref, ws_ref, bst_ref, vg_ref, og_ref, dproj_ref,
             dgm_ref, dws_ref, dbst_ref, dvg_ref, dog_ref):
        del dproj_ref

        @pl.when(pl.program_id(0) == 0)
        def _():
            dws_ref[...] = jnp.zeros_like(dws_ref)
            dbst_ref[...] = jnp.zeros_like(dbst_ref)
            dvg_ref[...] = jnp.zeros_like(dvg_ref)
            dog_ref[...] = jnp.zeros_like(dog_ref)

        _, vjp = jax.vjp(_gmlp_chunk, *_gmlp_load(u_ref, v_ref, ws_ref, bst_ref, vg_ref, og_ref))
        dgu, dgv, dws, dbs, dvg, dog = vjp(_hslices(dy_ref, w, GM_HEADS))
        for h in range(GM_HEADS):
            dgm_ref[:, h * w:(h + 1) * w] = _b(dgu[h])
            dgm_ref[:, D_GM + h * w: D_GM + (h + 1) * w] = _b(dgv[h])
            dws_ref[h] += dws[h]
            dbst_ref[:, h:h + 1] += dbs[h]
            dvg_ref[:, h * w:(h + 1) * w] += dvg[h]
            dog_ref[:, h * w:(h + 1) * w] += dog[h]

    return pl.pallas_call(
        body, name="gmlp_bwd", grid=(t // CHUNK,),
        in_specs=[pl.BlockSpec((CHUNK, D_GM), lambda i: (i, 1))] + _gmlp_specs() + [ANY],
        out_specs=[pl.BlockSpec((CHUNK, 2 * D_GM), lambda i: (i, COL_U // (2 * D_GM))),
                   pl.BlockSpec((GM_HEADS, CHUNK, CHUNK), lambda i: (0, 0, 0)),
                   pl.BlockSpec((CHUNK, GM_HEADS), lambda i: (0, 0)),
                   pl.BlockSpec((1, D_GM), lambda i: (0, 0)),
                   pl.BlockSpec((1, D_GM), lambda i: (0, 0))],
        out_shape=[jax.ShapeDtypeStruct(dproj.shape, dproj.dtype), jax.ShapeDtypeStruct((GM_HEADS, CHUNK, CHUNK), F32),
                   jax.ShapeDtypeStruct((CHUNK, GM_HEADS), F32), jax.ShapeDtypeStruct((1, D_GM), F32),
                   jax.ShapeDtypeStruct((1, D_GM), F32)],
        input_output_aliases={7: 0},
        compiler_params=_cp("arbitrary"),
    )(dy, proj, proj, ws, bst, vg, og, dproj)


def _local_step(x, target, mods, lw, final_g, *, nseq, big_w, grad_sink, small_sink):
    saved = []
    x0, delta, gate = x, None, None
    h1 = _normmod_fwd(x, lw[0]["norm1_g"], mods[0][1], mods[0][0], nseq=nseq, name="norm1_fwd_0")
    for l in range(DEPTH):
        w = lw[l]
        sh1, sc1, g1, sh2, sc2, g2 = mods[l]
        w_in = big_w(l, "w_in", h1)
        proj = _matmul(h1, w_in, tb=True, name=f"mm_in_{l}")
        xbc, xbc_pre = _ssd_conv_fwd(proj, w["ssd_conv_w"], w["ssd_conv_b"], nseq=nseq)
        ycat, hprev = _ssd_fwd(xbc, proj, w["ssd_dt_bias"], w["ssd_a_log"], w["ssd_d"], w["ssd_norm_g"], nseq=nseq)
        ycat = _gmlp_fwd(proj, ycat, w["gm_ws"], w["gm_bst"], w["gm_vnorm_g"], w["gm_out_g"])
        w_out = big_w(l, "w_out", ycat)
        mix, x1, h2 = _matmul_normfwd(ycat, w_out, x0, g1, w["norm2_g"], sc2, sh2, nseq=nseq, name=f"mm_out_{l}")
        ff_up = big_w(l, "ff_up", h2)
        up = _matmul(h2, ff_up, tb=True, name=f"mm_up_{l}", out_dtype=BF16)
        act = _ffn_act_fwd(up, w["ff_conv_w"], w["ff_conv_b"], nseq=nseq)
        ff_down = big_w(l, "ff_down", act)
        sv = dict(x0=x0, xin_delta=delta, xin_gate=gate, h1=h1, proj=proj, xbc=xbc, xbc_pre=xbc_pre, hprev=hprev,
                  ycat=ycat, mix=mix, x1=x1, h2=h2, up=up, act=act,
                  w_in=w_in, w_out=w_out, ff_up=ff_up, ff_down=ff_down)
        if l + 1 < DEPTH:
            nsh1, nsc1 = mods[l + 1][0], mods[l + 1][1]
            dn, x0, h1 = _matmul_normfwd(act, ff_down, x1, g2, lw[l + 1]["norm1_g"], nsc1, nsh1, nseq=nseq,
                                         name=f"mm_down_{l}")
        else:
            dn, loss, dx, ddelta, dgate, dfg = _matmul_loss(act, ff_down, x1, g2, final_g, target, nseq=nseq,
                                                            name=f"mm_down_{l}")
        saved.append(dict(sv, dn=dn))
        delta, gate = dn, g2

    small, dmods = [None] * DEPTH, [None] * DEPTH
    for l in reversed(range(DEPTH)):
        w, sv = lw[l], saved[l]
        sh1, sc1, g1, sh2, sc2, g2 = mods[l]
        dg2 = dgate
        g_ff_down = _matmul(sv["act"], ddelta, ta=True, name=f"mm_down_dw_{l}", out_dtype=BF16)
        dact = _matmul(ddelta, sv["ff_down"], tb=True, name=f"mm_down_dx_{l}", out_dtype=BF16)
        dgate_ff, dval_ff, dfcw, dfcb = _ffn_act_bwd(dact, sv["up"], w["ff_conv_w"], w["ff_conv_b"], nseq=nseq)
        g_ff_up = _matmul([dgate_ff, dval_ff], sv["h2"], ta=True, name=f"mm_up_dw_{l}", out_dtype=BF16)
        dep = grad_sink(l, "ffn", dict(ff_down=g_ff_down, ff_up=g_ff_up), dval_ff)
        dx, dmix, dg1, dn2g, dsc2, dsh2 = _matmul_normbwd([dgate_ff, dval_ff], sv["ff_up"], dx, sv["x1"], sv["mix"], g1,
                                                          w["norm2_g"], sc2, nseq=nseq, name=f"mm_up_dx_{l}", dep=dep)
        g_w_out = _matmul(sv["ycat"], dmix, ta=True, name=f"mm_out_dw_{l}", out_dtype=BF16)
        dep = grad_sink(l, "w_out", dict(w_out=g_w_out), dmix)
        dycat = _matmul(dmix, sv["w_out"], tb=True, name=f"mm_out_dx_{l}", dep=dep)
        dxbc_act, dproj, ddtb, dalog, ddsk, dng = _ssd_bwd(dycat, sv["xbc"], sv["proj"], sv["hprev"], w["ssd_dt_bias"],
                                                          w["ssd_a_log"], w["ssd_d"], w["ssd_norm_g"], nseq=nseq)
        dproj, dscw, dscb = _ssd_conv_bwd(dxbc_act, sv["xbc_pre"], sv["proj"], w["ssd_conv_w"], dproj, nseq=nseq)
        dproj, dws, dbst, dvg, dog = _gmlp_bwd(dycat, sv["proj"], w["gm_ws"], w["gm_bst"], w["gm_vnorm_g"], w["gm_out_g"], dproj)
        early = dict(norm2_g=dn2g, ssd_norm_g=dng, gm_vnorm_g=dvg, gm_out_g=dog,
                     ssd_conv_w=dscw, ssd_conv_b=dscb, ff_conv_w=dfcw, ff_conv_b=dfcb,
                     ssd_dt_bias=ddtb, ssd_a_log=dalog, ssd_d=ddsk, gm_ws=dws, gm_bs=dbst.T)
        dep = small_sink(l, early, small, dmods, dfg, loss)
        g_w_in = _matmul(dproj, sv["h1"], ta=True, name=f"mm_in_dw_{l}", out_dtype=BF16, dep=dep)
        dep = grad_sink(l, "w_in", dict(w_in=g_w_in), dproj)
        dx, ddelta, dgate, dn1g, dsc1, dsh1 = _matmul_normbwd(dproj, sv["w_in"], dx, sv["x0"], sv["xin_delta"],
                                                              sv["xin_gate"], w["norm1_g"], sc1, nseq=nseq,
                                                              name=f"mm_in_dx_{l}", dep=dep)
        small[l] = dict(early, norm1_g=dn1g)
        dmods[l] = jnp.concatenate([dsh1, dsc1, dg1, dsh2, dsc2, dg2], axis=-1)[:, 0, :]
    return dx, small, dmods


def _all_gather(arrs, name, dep=None):
    n = len(arrs)
    extra = [] if dep is None else [dep]

    def body(*refs):
        ins, outs = refs[:n], refs[n + len(extra):2 * n + len(extra)]
        send_sems, recv_sems, local_sems = refs[2 * n + len(extra):]
        x, y, c = lax.axis_index("x"), lax.axis_index("y"), lax.axis_index("c")
        me, sibling = (x, y, c), (x, y, 1 - c)
        chips = [(1 - x, y), (x, 1 - y), (1 - x, 1 - y)]

        def copy(i, k, block, to, src=None):
            px, py, pc = block
            dst = outs[i].at[4 * px + 2 * py + pc]
            return pltpu.make_async_remote_copy(
                src_ref=dst if src is None else src, dst_ref=dst,
                send_sem=send_sems.at[7 * i + k], recv_sem=recv_sems.at[7 * i + k],
                device_id=to, device_id_type=MESH)

        mine = [pltpu.make_async_copy(ins[i], outs[i].at[4 * x + 2 * y + c], local_sems.at[i]) for i in range(n)]
        for cp in mine:
            cp.start()
        first = []
        for i in range(n):
            first.append(copy(i, 0, me, sibling, src=ins[i]))
            first += [copy(i, 1 + j, me, (*chip, c), src=ins[i]) for j, chip in enumerate(chips)]
        for cp in first:
            cp.start()
        passed = []
        for j, chip in enumerate(chips):
            for i in range(n):
                copy(i, 1 + j, (*chip, c), me).wait_recv()
                fwd = copy(i, 4 + j, (*chip, c), sibling)
                fwd.start()
                passed.append(fwd)
        for i in range(n):
            copy(i, 0, sibling, me).wait_recv()
            for j, chip in enumerate(chips):
                copy(i, 4 + j, (*chip, 1 - c), me).wait_recv()
        for cp in first + passed:
            cp.wait_send()
        for cp in mine:
            cp.wait()

    return pl.pallas_call(
        body, name=name,
        in_specs=[ANY] * (n + len(extra)), out_specs=[ANY] * n,
        out_shape=[jax.ShapeDtypeStruct((N_DEV,) + a.shape, a.dtype) for a in arrs],
        scratch_shapes=[pltpu.SemaphoreType.DMA((7 * n,)), pltpu.SemaphoreType.DMA((7 * n,)),
                        pltpu.SemaphoreType.DMA((n,))],
    )(*arrs, *extra)


HBM = pl.BlockSpec(memory_space=pltpu.HBM)
SEM = pl.BlockSpec(memory_space=pltpu.SEMAPHORE)
EFFECT = pltpu.SideEffectType.DATAFLOW_SIDE_EFFECTING


def _peer(k):
    x, y, c = lax.axis_index("x"), lax.axis_index("y"), lax.axis_index("c")
    return (1 - x if k & 4 else x, 1 - y if k & 2 else y, 1 - c if k & 1 else c)


ALL_PEERS = tuple(range(1, N_DEV))
OTHER_CHIPS = (2, 4, 6)


def _xc_copies(scatter, srcs, lands, send_sems, recv_sems, peers=ALL_PEERS):
    x, y, c = lax.axis_index("x"), lax.axis_index("y"), lax.axis_index("c")
    copies = []
    for i in range(len(srcs)):
        for k in peers:
            px, py, pc = _peer(k)
            src = srcs[i].at[4 * px + 2 * py + pc] if scatter else srcs[i]
            dst = lands[i].at[k - 1] if scatter else lands[i].at[4 * x + 2 * y + c]
            copies.append(pltpu.make_async_remote_copy(
                src_ref=src, dst_ref=dst, send_sem=send_sems[i].at[k - 1], recv_sem=recv_sems[i].at[k - 1],
                device_id=(px, py, pc), device_id_type=MESH))
    return copies


def _xc_start(scatter, arrs, after, name, peers=ALL_PEERS):
    n = len(arrs)
    lands = [lax.empty((N_DEV - 1,) + a.shape[1:] if scatter else (N_DEV,) + a.shape, a.dtype) for a in arrs]

    def body(*refs):
        srcs, lnd = refs[:n], refs[n:2 * n]
        send_sems, recv_sems = refs[2 * n + 1:3 * n + 1], refs[3 * n + 1:4 * n + 1]
        token = refs[6 * n + 1]
        for cp in _xc_copies(scatter, srcs, lnd, send_sems, recv_sems, peers):
            cp.start()
        token[...] = jnp.zeros_like(token)

    outs = pl.pallas_call(
        body, name=name,
        out_shape=[pltpu.SemaphoreType.DMA((N_DEV - 1,))] * (2 * n)
        + [pltpu.HBM(a.shape, a.dtype) for a in arrs] + [pltpu.HBM(a.shape, a.dtype) for a in lands]
        + [jax.ShapeDtypeStruct((8, 128), F32)],
        in_specs=[HBM] * (2 * n) + [ANY],
        out_specs=[SEM] * (2 * n) + [HBM] * (2 * n) + [pl.BlockSpec(memory_space=pltpu.VMEM)],
        input_output_aliases={i: 2 * n + i for i in range(2 * n)},
        compiler_params=pltpu.CompilerParams(has_side_effects=EFFECT),
    )(*[pltpu.with_memory_space_constraint(a, pltpu.HBM) for a in list(arrs) + lands], after)
    return outs[:n], outs[n:2 * n], outs[2 * n:3 * n], outs[3 * n:4 * n], outs[4 * n][0, 0]


def _xc_wait(scatter, send_sems, recv_sems, srcs, lands, after, name, peers=ALL_PEERS):
    n = len(srcs)

    def body(*refs):
        s_refs, l_refs = refs[:n], refs[n:2 * n]
        ss, rs = refs[2 * n:3 * n], refs[3 * n:4 * n]
        for cp in _xc_copies(scatter, s_refs, l_refs, ss, rs, peers):
            cp.wait_send()
            cp.wait_recv()

    outs = pl.pallas_call(
        body, name=name,
        out_shape=[pltpu.HBM(a.shape, a.dtype) for a in list(srcs) + list(lands)],
        in_specs=[HBM] * (2 * n) + [SEM] * (2 * n) + [ANY],
        out_specs=[HBM] * (2 * n),
        input_output_aliases={i: i for i in range(2 * n)},
        compiler_params=pltpu.CompilerParams(has_side_effects=EFFECT),
    )(*srcs, *lands, *send_sems, *recv_sems, after)
    return outs[:n], outs[n:]


def _sib_copies(zones, send_sems, recv_sems):
    x, y, c = lax.axis_index("x"), lax.axis_index("y"), lax.axis_index("c")
    copies = []
    for i in range(len(zones)):
        for q in range(N_DEV // 2):
            slot = zones[i].at[2 * q + c]
            copies.append(pltpu.make_async_remote_copy(
                src_ref=slot, dst_ref=slot, send_sem=send_sems[i].at[q], recv_sem=recv_sems[i].at[q],
                device_id=(x, y, 1 - c), device_id_type=MESH))
    return copies


def _sib_start(zones, name):
    n = len(zones)

    def body(*refs):
        for cp in _sib_copies(refs[:n], refs[n:2 * n], refs[2 * n:3 * n]):
            cp.start()

    outs = pl.pallas_call(
        body, name=name,
        out_shape=[pltpu.SemaphoreType.DMA((N_DEV // 2,))] * (2 * n) + [pltpu.HBM(a.shape, a.dtype) for a in zones],
        in_specs=[HBM] * n,
        out_specs=[SEM] * (2 * n) + [HBM] * n,
        input_output_aliases={i: 2 * n + i for i in range(n)},
        compiler_params=pltpu.CompilerParams(has_side_effects=EFFECT),
    )(*[pltpu.with_memory_space_constraint(a, pltpu.HBM) for a in zones])
    return outs[:n], outs[n:2 * n], outs[2 * n:]


def _sib_wait(send_sems, recv_sems, zones, name):
    n = len(zones)

    def body(*refs):
        for cp in _sib_copies(refs[:n], refs[n:2 * n], refs[2 * n:3 * n]):
            cp.wait_send()
            cp.wait_recv()

    return pl.pallas_call(
        body, name=name,
        out_shape=[pltpu.HBM(a.shape, a.dtype) for a in zones],
        in_specs=[HBM] * n + [SEM] * (2 * n),
        out_specs=[HBM] * n,
        input_output_aliases={i: i for i in range(n)},
        compiler_params=pltpu.CompilerParams(has_side_effects=EFFECT),
    )(*zones, *send_sems, *recv_sems)


def _adamw_math(w, g, m, v):
    m = ADAM_B1 * m + (1.0 - ADAM_B1) * g
    v = ADAM_B2 * v + (1.0 - ADAM_B2) * (g * g)
    m_hat = m / (1.0 - ADAM_B1 ** ADAM_STEP)
    v_hat = v / (1.0 - ADAM_B2 ** ADAM_STEP)
    delta = -ADAM_LR * (m_hat / (jnp.sqrt(v_hat) + ADAM_EPS) + ADAM_WD * w)
    return delta, m, v


def _adamw_sharded(parts, w, m, v, pos, name):
    depth, rows, cols = w.shape
    tr = _tile(rows, 256) if rows % 8 == 0 else rows
    npart = len(parts)

    def body(pos_ref, *refs):
        prefs = refs[:npart]
        w_ref, m_ref, v_ref, g_out, d_out, m_out, v_out = refs[npart:]
        g = prefs[0][...]
        for pr in prefs[1:]:
            g = g + pr[...]
        delta, mn, vn = _adamw_math(w_ref[...], g, m_ref[...], v_ref[...])
        g_out[...] = g
        d_out[...] = delta
        m_out[...] = mn
        v_out[...] = vn

    def part_spec(fn):
        return pl.BlockSpec((1, tr, cols), lambda l, i, p: (fn(p) * depth + l, i, 0))

    blk = pl.BlockSpec((1, tr, cols), lambda l, i, p: (l, i, 0))
    shp = jax.ShapeDtypeStruct((depth, rows, cols), F32)
    return pl.pallas_call(
        body, name=name,
        grid_spec=pltpu.PrefetchScalarGridSpec(
            num_scalar_prefetch=1, grid=(depth, rows // tr),
            in_specs=[part_spec(fn) for _, fn in parts] + [blk, blk, blk],
            out_specs=[blk, blk, blk, blk]),
        out_shape=[shp, shp, shp, shp],
        compiler_params=_cp("parallel", "parallel"),
    )(pos, *[a for a, _ in parts], w, m, v)


def _adamw_layer(parts, w, m, v, pos, layer, prev, name):
    depth, rows, cols = w.shape
    npart = len(parts)
    nprev = 0 if prev is None else 4
    if rows % 16 == 0:
        tr, tc = max(t for t in range(16, 257, 16) if rows % t == 0), cols
    else:
        tr, tc = rows, _tile(cols, 256)
    pick = (lambda i: (i, 0)) if rows % 16 == 0 else (lambda i: (0, i))

    def body(pos_ref, *refs):
        prefs = refs[:npart]
        w_ref, m_ref, v_ref = refs[npart:npart + 3]
        g_out, d_out, m_out, v_out = refs[npart + 3 + nprev:]
        g = prefs[0][...].astype(F32)
        for pr in prefs[1:]:
            g = g + pr[...].astype(F32)
        delta, mn, vn = _adamw_math(w_ref[...], g, m_ref[...], v_ref[...])
        g_out[...] = g
        d_out[...] = delta
        m_out[...] = mn
        v_out[...] = vn

    def part_spec(fn):
        return pl.BlockSpec((1, tr, tc), lambda i, p: (fn(p), *pick(i)))

    blk = pl.BlockSpec((1, tr, tc), lambda i, p: (layer, *pick(i)))
    shp = jax.ShapeDtypeStruct((depth, rows, cols), F32)
    first_prev = 1 + npart + 3
    return pl.pallas_call(
        body, name=name,
        grid_spec=pltpu.PrefetchScalarGridSpec(
            num_scalar_prefetch=1, grid=(rows // tr * (cols // tc),),
            in_specs=[part_spec(fn) for _, fn in parts] + [blk, blk, blk] + [ANY] * nprev,
            out_specs=[blk, blk, blk, blk]),
        out_shape=[shp, shp, shp, shp],
        input_output_aliases={first_prev + j: j for j in range(nprev)},
        compiler_params=_cp("parallel"),
    )(pos, *[a for a, _ in parts], w, m, v, *(prev or ()))


_P1024 = ["norm1_g", "norm2_g", "ssd_norm_g", "gm_vnorm_g", "gm_out_g"]
_P16 = ["ssd_dt_bias", "ssd_a_log", "ssd_d"]


def _adamw_small(gath, wmv):
    names = list(wmv.keys())
    classes = list(gath.keys())
    flat_in = [gath[k] for k in classes]
    for nme in names:
        flat_in += list(wmv[nme])
    out_shapes = []
    for nme in names:
        out_shapes += [jax.ShapeDtypeStruct(wmv[nme][0].shape, F32)] * 4
    out_shapes += [jax.ShapeDtypeStruct((DEPTH, SSD_CONV, CONV_DIM), F32), jax.ShapeDtypeStruct((DEPTH, FF_CONV, D_FF), F32),
                   jax.ShapeDtypeStruct((1, SSD_HEADS), F32)]
    scratch = [pltpu.VMEM(gath[k].shape[1:], F32) for k in classes]
    ncls = len(classes)

    def body(*refs):
        g_refs = dict(zip(classes, refs[:ncls]))
        pos = ncls
        w_refs = {}
        for nme in names:
            w_refs[nme] = refs[pos:pos + 3]
            pos += 3
        o_refs = {}
        for nme in names:
            o_refs[nme] = refs[pos:pos + 4]
            pos += 4
        scw_out, fcw_out, loss_out = refs[pos], refs[pos + 1], refs[pos + 2]
        s_refs = dict(zip(classes, refs[pos + 3:]))
        for k in classes:
            acc = g_refs[k][0]
            for dev in range(1, N_DEV):
                acc = acc + g_refs[k][dev]
            s_refs[k][...] = acc

        def apply(nme, grad_of):
            w_ref, m_ref, v_ref = w_refs[nme]
            g_out, d_out, m_out, v_out = o_refs[nme]
            shape = w_ref.shape
            if len(shape) == 2:
                idxs = [(slice(l, l + 1),) for l in range(shape[0])]
            elif len(shape) == 3:
                idxs = [(l,) for l in range(shape[0])]
            else:
                idxs = [(l, h) for l in range(shape[0]) for h in range(shape[1])]
            for n_i, ix in enumerate(idxs):
                g = grad_of(n_i)
                delta, mn, vn = _adamw_math(w_ref[ix], g, m_ref[ix], v_ref[ix])
                g_out[ix] = g
                d_out[ix] = delta
                m_out[ix] = mn
                v_out[ix] = vn

        s1024, s1536, s2816, s16, s128, s6144, late1024, late6144 = (s_refs[k] for k in classes)
        s1024[0:1, :] += late1024[...]
        s6144[0:late6144.shape[0], :] += late6144[...]
        for n_i, nme in enumerate(_P1024):
            apply(nme, lambda l, b=2 * n_i: s1024[b + l:b + l + 1, :])
        apply("final_g", lambda l: s1024[10:11, :])
        apply("ssd_conv_b", lambda l: s1536[8 + l:9 + l, :])
        apply("ff_conv_b", lambda l: s2816[6 + l:7 + l, :])
        for n_i, nme in enumerate(_P16):
            apply(nme, lambda l, b=2 * n_i: s16[b + l:b + l + 1, :])
        apply("gm_ws", lambda q: s128[q * CHUNK:(q + 1) * CHUNK, :])
        apply("gm_bs", lambda l: s128[2048 + 8 * l:2048 + 8 * (l + 1), :])
        apply("ada_b", lambda l: s6144[2 * l:2 * l + 1, :] + s6144[2 * l + 1:2 * l + 2, :])
        for l in range(DEPTH):
            scw_out[l] = s1536[SSD_CONV * l:SSD_CONV * (l + 1), :]
            fcw_out[l] = s2816[FF_CONV * l:FF_CONV * (l + 1), :]
        loss_out[...] = s16[2 * len(_P16):2 * len(_P16) + 1, :]

    outs = pl.pallas_call(
        body, name="adamw_small",
        out_shape=out_shapes,
        scratch_shapes=scratch,
        compiler_params=pltpu.CompilerParams(vmem_limit_bytes=VMEM_LIMIT),
    )(*flat_in)
    res = {nme: tuple(outs[4 * i:4 * i + 4]) for i, nme in enumerate(names)}
    return res, outs[-3], outs[-2], outs[-1]


_WEIGHTS = ['ada_w', 'ada_b', 'norm1_g', 'norm2_g', 'w_in', 'ssd_conv_w', 'ssd_conv_b', 'ssd_dt_bias', 'ssd_a_log',
            'ssd_d', 'ssd_norm_g', 'gm_vnorm_g', 'gm_ws', 'gm_bs', 'gm_out_g', 'w_out', 'ff_up', 'ff_conv_w',
            'ff_conv_b', 'ff_down', 'final_g']


_O_XBC, _O_DT, _O_GM = D_SSD, D_SSD + CONV_DIM, D_SSD + CONV_DIM + SSD_HEADS


_TRANSPOSED = ("w_in", "ff_up")


def _full_weight(name, g):
    full = g.reshape(g.shape[0] * g.shape[1], g.shape[2])
    if name != "w_in":
        return full
    zpad = jnp.zeros((N_INP - N_IN, full.shape[1]), full.dtype)
    return jnp.concatenate([full[_O_GM:], full[:_O_XBC], full[_O_XBC:_O_DT], full[_O_DT:_O_GM], zpad], axis=0)


def _by_owner(name, grad):
    if name == "w_in":
        grad = jnp.concatenate([grad[COL_Z:COL_XBC], grad[COL_XBC:COL_DT], grad[COL_DT:COL_DT + SSD_HEADS], grad[:COL_Z]], axis=0)
    return grad.reshape(N_DEV, grad.shape[0] // N_DEV, grad.shape[1])


def kernel(x, c, ada_w, ada_b, norm1_g, norm2_g, w_in, ssd_conv_w, ssd_conv_b, ssd_dt_bias, ssd_a_log, ssd_d, ssd_norm_g, gm_vnorm_g, gm_ws, gm_bs, gm_out_g, w_out, ff_up, ff_conv_w, ff_conv_b, ff_down, final_g, loss_target, m_ada_w, m_ada_b, m_norm1_g, m_norm2_g, m_w_in, m_ssd_conv_w, m_ssd_conv_b, m_ssd_dt_bias, m_ssd_a_log, m_ssd_d, m_ssd_norm_g, m_gm_vnorm_g, m_gm_ws, m_gm_bs, m_gm_out_g, m_w_out, m_ff_up, m_ff_conv_w, m_ff_conv_b, m_ff_down, m_final_g, v_ada_w, v_ada_b, v_norm1_g, v_norm2_g, v_w_in, v_ssd_conv_w, v_ssd_conv_b, v_ssd_dt_bias, v_ssd_a_log, v_ssd_d, v_ssd_norm_g, v_gm_vnorm_g, v_gm_ws, v_gm_bs, v_gm_out_g, v_w_out, v_ff_up, v_ff_conv_w, v_ff_conv_b, v_ff_down, v_final_g):
    given = dict(locals())
    wts = {n: given[n] for n in _WEIGHTS}
    mom = {n: given["m_" + n] for n in _WEIGHTS}
    var = {n: given["v_" + n] for n in _WEIGHTS}
    nseq, seq, d = x.shape
    ix, iy, ic = lax.axis_index("x"), lax.axis_index("y"), lax.axis_index("c")
    me = 4 * ix + 2 * iy + ic
    me_arr = me.astype(jnp.int32).reshape(1)

    for nme in _TRANSPOSED:
        wts[nme], mom[nme], var[nme] = (jnp.transpose(a, (0, 2, 1)) for a in (wts[nme], mom[nme], var[nme]))

    def shard(l, name):
        return _b(wts[name][l])

    g_scw, g_fcw, c_all = _all_gather([ssd_conv_w, ff_conv_w, c], "gather_first")
    scw_f = jnp.transpose(g_scw, (1, 2, 0, 3)).reshape(DEPTH, SSD_CONV, CONV_DIM)
    fcw_f = jnp.transpose(g_fcw, (1, 2, 0, 3)).reshape(DEPTH, FF_CONV, D_FF)
    c_all = c_all.reshape(N_DEV * nseq, d)

    n_ada = ada_w.shape[2]
    ada_b_shard = lax.dynamic_slice_in_dim(ada_b, me * n_ada, n_ada, axis=1).reshape(DEPTH, 1, n_ada)
    mod_part, c_act = _ada_fwd(c_all, ada_w, ada_b_shard)
    (mod_g,) = _all_gather([mod_part], "gather_mod")
    mod_all = jnp.transpose(mod_g, (1, 2, 0, 3)).reshape(DEPTH, N_DEV * nseq, N_MOD * d)
    mod_mine = lax.dynamic_slice_in_dim(mod_all, me * nseq, nseq, axis=1)
    mod_k = jnp.transpose(mod_mine.reshape(DEPTH, nseq, N_MOD, 1, d), (0, 2, 1, 3, 4))
    mods = [[mod_k[l, k] for k in range(N_MOD)] for l in range(DEPTH)]

    first_ssem, first_rsem, first_src, first_land, first_zero = _xc_start(
        False, [shard(0, "w_in")], mod_g, "ag_first_start", peers=OTHER_CHIPS)
    later = [(0, "w_out"), (0, "ff_up"), (0, "ff_down"), (1, "w_in"), (1, "w_out"), (1, "ff_up"), (1, "ff_down")]
    ag_groups = {(0, "w_out"): [0], (0, "ff_up"): [1, 2], (1, "w_in"): [3, 4], (1, "ff_up"): [5, 6]}
    big_cache, ag = {}, {}

    def big_w(l, name, after):
        if (l, name) == (0, "w_in") and (l, name) not in big_cache:
            ag["ssem"], ag["rsem"], ag["src"], ag["land"], started = _xc_start(
                False, [shard(l2, n2) for l2, n2 in later], after, "ag_start")
            srcs, lands = _xc_wait(False, first_ssem, first_rsem, first_src, first_land,
                                   jnp.full((8, 128), started, F32), "ag_first_wait", peers=OTHER_CHIPS)
            zone = lax.dynamic_update_index_in_dim(lands[0], srcs[0], me, 0)
            (zone,) = _sib_wait(*_sib_start([zone], "ag_first_sib_start"), "ag_first_sib_wait")
            big_cache[(l, name)] = _full_weight(name, zone)
        if (l, name) not in big_cache:
            idx = ag_groups[(l, name)]
            pick = lambda seq_: [seq_[i] for i in idx]
            srcs, lands = _xc_wait(False, pick(ag["ssem"]), pick(ag["rsem"]), pick(ag["src"]), pick(ag["land"]), after,
                                   f"ag_wait_{l}_{name}")
            for i, src, land in zip(idx, srcs, lands):
                big_cache[later[i]] = _full_weight(later[i][1], lax.dynamic_update_index_in_dim(land, src, me, 0))
        return big_cache[(l, name)]

    lw = []
    for l in range(DEPTH):
        lw.append(dict(
            norm1_g=norm1_g[l:l + 1] + (first_zero if l == 0 else 0.0), norm2_g=norm2_g[l:l + 1], ssd_conv_w=scw_f[l],
            ssd_conv_b=ssd_conv_b[l:l + 1], ssd_dt_bias=ssd_dt_bias[l:l + 1], ssd_a_log=ssd_a_log[l:l + 1],
            ssd_d=ssd_d[l:l + 1], ssd_norm_g=ssd_norm_g[l:l + 1], gm_vnorm_g=gm_vnorm_g[l:l + 1], gm_ws=gm_ws[l],
            gm_bst=gm_bs[l].T, gm_out_g=gm_out_g[l:l + 1], ff_conv_w=fcw_f[l], ff_conv_b=ff_conv_b[l:l + 1]))

    outs = {}
    pending = {}

    def rs_finish(l, group, after):
        names, ssem, rsem, srcs, lands = pending.pop((l, group))
        srcs, lands = _xc_wait(True, ssem, rsem, srcs, lands, after, f"rs_wait_{l}_{group}")
        for nme, own, land in zip(names, srcs, lands):
            parts = [(own, lambda p: p[0])] + [(land, lambda p, k=k: k) for k in range(N_DEV - 1)]
            outs[nme] = _adamw_layer(parts, wts[nme], mom[nme], var[nme], me_arr, l, outs.get(nme), f"adamw_{nme}_{l}")
        return outs[names[-1]][0]

    def grad_sink(l, group, grads, after):
        names = list(grads)
        ssem, rsem, srcs, lands, zero = _xc_start(True, [_by_owner(n, grads[n]) for n in names], after, f"rs_start_{l}_{group}")
        pending[(l, group)] = (names, ssem, rsem, srcs, lands)
        return zero.reshape(1, 1)

    early_gather = {}

    def small_sink(l, early, small, dmods, dfg, loss_p):
        if l > 0:
            return None
        layers = [dict(early, norm1_g=jnp.zeros((1, d), F32))] + small[1:]
        rows = lambda name: [layers[k][name] for k in range(DEPTH)]
        packed = [
            jnp.concatenate(sum([rows(n) for n in _P1024], []) + [dfg], axis=0),
            jnp.concatenate(rows("ssd_conv_w") + rows("ssd_conv_b"), axis=0),
            jnp.concatenate(rows("ff_conv_w") + rows("ff_conv_b"), axis=0),
            jnp.concatenate(sum([rows(n) for n in _P16], []) + [loss_p[:, :SSD_HEADS]], axis=0),
            jnp.concatenate([layers[k]["gm_ws"].reshape(GM_HEADS * CHUNK, CHUNK) for k in range(DEPTH)] + rows("gm_bs"), axis=0),
            jnp.concatenate([jnp.zeros((nseq, N_MOD * d), F32)] + dmods[1:], axis=0)]
        ssem, rsem, srcs, lands, zero = _xc_start(False, packed, packed[0], "small_start")
        early_gather.update(ssem=ssem, rsem=rsem, srcs=srcs, lands=lands)
        return zero.reshape(1, 1)

    grad_x, small, dmods = _local_step(
        x.reshape(nseq * seq, d), loss_target.reshape(nseq * seq, d), mods, lw, final_g.reshape(1, d), nseq=nseq,
        big_w=big_w, grad_sink=grad_sink, small_sink=small_sink)

    done = grad_x
    for l, grp in ((1, "ffn"), (1, "w_out"), (1, "w_in"), (0, "ffn"), (0, "w_out")):
        done = rs_finish(l, grp, done)
    srcs, lands = _xc_wait(False, early_gather["ssem"], early_gather["rsem"], early_gather["srcs"],
                           early_gather["lands"], done, "small_wait")
    gathered = [lax.dynamic_update_index_in_dim(land, src, me, 0) for src, land in zip(srcs, lands)]
    gathered += _all_gather([small[0]["norm1_g"], dmods[0]], "gather_late", dep=gathered[0])
    gath = dict(zip(["p1024", "p1536", "p2816", "p16", "p128", "p6144", "late1024", "late6144"], gathered))

    dmod_all = jnp.concatenate([gath["late6144"].reshape(1, N_DEV * nseq, N_MOD * d),
                                jnp.transpose(gath["p6144"].reshape(N_DEV, DEPTH, nseq, N_MOD * d)[:, 1:], (1, 0, 2, 3)).reshape(
                                    DEPTH - 1, N_DEV * nseq, N_MOD * d)], axis=0)
    small_names = _P1024 + ["final_g", "ssd_conv_b", "ff_conv_b"] + _P16 + ["gm_ws", "gm_bs", "ada_b"]
    wmv = {}
    for nme in small_names:
        if nme == "final_g":
            wmv[nme] = tuple(a.reshape(1, d) for a in (wts[nme], mom[nme], var[nme]))
        else:
            wmv[nme] = (wts[nme], mom[nme], var[nme])
    small_out, scw_full, fcw_full, loss_sum = _adamw_small(gath, wmv)
    loss = loss_sum[0, 0]
    rs_finish(0, "w_in", scw_full)
    for nme in small_names:
        outs[nme] = small_out[nme]
    outs["final_g"] = tuple(a.reshape(d) for a in outs["final_g"])

    n_scw, n_fcw = ssd_conv_w.shape[2], ff_conv_w.shape[2]
    g_scw_mine = lax.dynamic_slice_in_dim(scw_full, me * n_scw, n_scw, axis=2)
    g_fcw_mine = lax.dynamic_slice_in_dim(fcw_full, me * n_fcw, n_fcw, axis=2)
    outs["ssd_conv_w"] = _adamw_sharded([(g_scw_mine, lambda p: 0)], ssd_conv_w, m_ssd_conv_w, v_ssd_conv_w, me_arr, "adamw_ssd_conv_w")
    outs["ff_conv_w"] = _adamw_sharded([(g_fcw_mine, lambda p: 0)], ff_conv_w, m_ff_conv_w, v_ff_conv_w, me_arr, "adamw_ff_conv_w")

    dmod_cols = _b(lax.dynamic_slice_in_dim(dmod_all, me * n_ada, n_ada, axis=2))
    g_ada = jnp.stack([_matmul(c_act, dmod_cols[l], ta=True, name=f"mm_ada_dw_{l}") for l in range(DEPTH)])
    outs["ada_w"] = _adamw_sharded([(g_ada, lambda p: 0)], ada_w, m_ada_w, v_ada_w, me_arr, "adamw_ada_w")

    for nme in _TRANSPOSED:
        outs[nme] = tuple(jnp.transpose(a, (0, 2, 1)) for a in outs[nme])
    result = [loss, grad_x.reshape(nseq, seq, d)]
    for k in range(4):
        result += [outs[n][k] for n in _WEIGHTS]
    return tuple(result)
```

```python
import functools
import math

import jax
import jax.numpy as jnp
from jax import lax
from jax.experimental import pallas as pl
from jax.experimental.pallas import tpu as pltpu

F32 = jnp.float32
BF16 = jnp.bfloat16

N_DEV = 8
D_MODEL = 1024
DEPTH = 2
CHUNK = 128
SSD_HEADS = 16
SSD_HEAD_DIM = 64
SSD_GROUPS = 2
HEADS_PER_GROUP = SSD_HEADS // SSD_GROUPS
GROUP_WIDTH = HEADS_PER_GROUP * SSD_HEAD_DIM
D_STATE = 128
D_SSD = 1024
CONV_DIM = 1536
SSD_CONV = 4
GM_HEADS = 8
GM_HEAD_DIM = 128
D_GM = 1024
D_FF = 2816
FF_CONV = 3
N_IN = 4624
N_MOD = 6
EPS = 1e-6

N_INP = 5120
COL_U, COL_V, COL_Z, COL_XBC, COL_DT = 0, 1024, 2048, 3072, 4608

ADAM_LR = 0.001
ADAM_B1 = 0.9
ADAM_B2 = 0.999
ADAM_EPS = 1e-08
ADAM_WD = 0.01
ADAM_STEP = 10

VMEM_LIMIT = 56 * 1024 * 1024
MESH = pl.DeviceIdType.MESH
ANY = pl.BlockSpec(memory_space=pl.ANY)


def _cp(*sem):
    return pltpu.CompilerParams(dimension_semantics=sem, vmem_limit_bytes=VMEM_LIMIT)


def _tile(n, pref):
    if n <= pref or n % 128:
        return n
    best = 128
    for t in range(128, pref + 1, 128):
        if n % t == 0:
            best = t
    return best


def _silu(x):
    return x * jax.nn.sigmoid(x)


def _gelu(x):
    return 0.5 * x * (1.0 + lax.erf(x * (1.0 / math.sqrt(2.0))))


def _softplus(x):
    return jnp.maximum(x, 0.0) + jnp.log1p(jnp.exp(-jnp.abs(x)))


def _b(x):
    return x.astype(BF16)


_NN = (((1,), (0,)), ((), ()))
_NT = (((1,), (1,)), ((), ()))
_TN = (((0,), (0,)), ((), ()))


def _dg(a, b, dn):
    return lax.dot_general(_b(a), _b(b), dn, preferred_element_type=F32)


@jax.custom_vjp
def _bdot(a, b):
    return _dg(a, b, _NN)


def _bdot_fwd(a, b):
    return _dg(a, b, _NN), (a, b)


def _bdot_bwd(res, ct):
    a, b = res
    return _dg(ct, b, _NT), _dg(a, ct, _TN)


_bdot.defvjp(_bdot_fwd, _bdot_bwd)


@jax.custom_vjp
def _bdot_nt(a, b):
    return _dg(a, b, _NT)


def _bdot_nt_fwd(a, b):
    return _dg(a, b, _NT), (a, b)


def _bdot_nt_bwd(res, ct):
    a, b = res
    return _dg(ct, b, _NN), _dg(ct, a, _TN)


_bdot_nt.defvjp(_bdot_nt_fwd, _bdot_nt_bwd)


@jax.custom_vjp
def _bdot_tn(a, b):
    return _dg(a, b, _TN)


def _bdot_tn_fwd(a, b):
    return _dg(a, b, _TN), (a, b)


def _bdot_tn_bwd(res, ct):
    a, b = res
    return _dg(b, ct, _NT), _dg(a, ct, _NN)


_bdot_tn.defvjp(_bdot_tn_fwd, _bdot_tn_bwd)


def _tri(n, lower):
    r = lax.broadcasted_iota(jnp.int32, (n, n), 0)
    c = lax.broadcasted_iota(jnp.int32, (n, n), 1)
    return ((r >= c) if lower else (r <= c)).astype(F32)


def _eye(n):
    r = lax.broadcasted_iota(jnp.int32, (n, n), 0)
    c = lax.broadcasted_iota(jnp.int32, (n, n), 1)
    return (r == c).astype(F32)


def _hdot(a, b, dn):
    return lax.dot_general(a, b, dn, precision=lax.Precision.HIGHEST, preferred_element_type=F32)


@jax.custom_vjp
def _cumsum_rows(x):
    return _hdot(_tri(x.shape[0], True), x, _NN)


def _cumsum_rows_fwd(x):
    return _cumsum_rows(x), None


def _cumsum_rows_bwd(_, ct):
    return (_hdot(_tri(ct.shape[0], False), ct, _NN),)


_cumsum_rows.defvjp(_cumsum_rows_fwd, _cumsum_rows_bwd)


@jax.custom_vjp
def _transpose(x):
    return _hdot(_eye(x.shape[1]), x, _NT)


def _transpose_fwd(x):
    return _transpose(x), None


def _transpose_bwd(_, ct):
    return (_hdot(_eye(ct.shape[1]), ct, _NT),)


_transpose.defvjp(_transpose_fwd, _transpose_bwd)


MXU_WIDTH = 256
MATMUL_TILE_CAP = 2816
MATMUL_VMEM = 44 * 1024 * 1024


def _mxu_tiles(n):
    if n <= MATMUL_TILE_CAP or n % 128:
        return [n]
    for unit in (MXU_WIDTH, 128):
        opts = [t for t in range(unit, MATMUL_TILE_CAP + 1, unit) if n % t == 0]
        if opts:
            return opts
    return [n]


def _matmul(a, b, *, ta=False, tb=False, name, dep=None, out_dtype=F32):
    pieces = list(a) if isinstance(a, (list, tuple)) else [a]
    npc = len(pieces)
    rows, width = pieces[0].shape
    assert all(p.shape == (rows, width) for p in pieces)
    if ta:
        k_dim, m_dim = rows, width * npc
    else:
        m_dim, k_dim = rows, width * npc
    if tb:
        n_dim, kb = b.shape
    else:
        kb, n_dim = b.shape
    assert kb == k_dim, (pieces[0].shape, npc, b.shape, ta, tb)
    m_unit = width if npc > 1 and ta else m_dim
    k_unit = width if npc > 1 and not ta else k_dim
    tm = _tile(m_unit, 1536)
    tn_opts, tk_opts = _mxu_tiles(n_dim), _mxu_tiles(k_unit)
    tn, tk = tn_opts.pop(), tk_opts.pop()
    while 4 * (tm * tk + tk * tn) + 8 * tm * tn > MATMUL_VMEM:
        if tn >= tk and tn_opts:
            tn = tn_opts.pop()
        else:
            tk = tk_opts.pop()
    ni, nj, nk = m_dim // tm, n_dim // tn, k_dim // tk
    per = width // (tm if ta else tk)
    dn = (((0 if ta else 1,), (1 if tb else 0,)), ((), ()))

    a_bytes, b_bytes = m_dim * k_dim, k_dim * n_dim
    m_outer = nk > 1 or a_bytes + b_bytes * ni <= b_bytes + a_bytes * nj
    if m_outer:
        ij = lambda o, n, k: (o, n)
        grid = (ni, nj, nk)
    else:
        ij = lambda o, n, k: (n, o)
        grid = (nj, ni, nk)

    use_acc = nk > 1 and out_dtype != F32

    def body(*refs):
        a_refs, b_ref = refs[:npc], refs[npc]
        o_ref = refs[-2] if use_acc else refs[-1]
        acc_ref = refs[-1]
        k = pl.program_id(2)
        i = pl.program_id(0 if m_outer else 1)
        along = i if ta else k

        def step(a_ref):
            p = lax.dot_general(a_ref[...], b_ref[...], dn, preferred_element_type=F32)
            if nk == 1:
                o_ref[...] = p.astype(out_dtype)
            else:
                @pl.when(k == 0)
                def _():
                    acc_ref[...] = p

                @pl.when((k > 0) & (k < nk - 1 if use_acc else True))
                def _():
                    acc_ref[...] += p

                if use_acc:
                    @pl.when(k == nk - 1)
                    def _():
                        o_ref[...] = (acc_ref[...] + p).astype(out_dtype)

        if npc == 1:
            step(a_refs[0])
        else:
            for pc in range(npc):
                pl.when((along >= pc * per) & (along < (pc + 1) * per))(functools.partial(step, a_refs[pc]))

    def a_map(pc, o, n, k):
        i, _ = ij(o, n, k)
        along = i if ta else k
        if npc > 1:
            along = jnp.clip(along - pc * per, 0, per - 1)
        return (k, along) if ta else (i, along)

    def b_map(o, n, k):
        _, j = ij(o, n, k)
        return (j, k) if tb else (k, j)

    extra = [] if dep is None else [dep]
    return pl.pallas_call(
        body, name=name,
        grid=grid,
        in_specs=[pl.BlockSpec((tk, tm) if ta else (tm, tk), functools.partial(a_map, pc)) for pc in range(npc)]
        + [pl.BlockSpec((tn, tk) if tb else (tk, tn), b_map)] + [ANY] * len(extra),
        out_specs=pl.BlockSpec((tm, tn), lambda o, n, k: ij(o, n, k)),
        out_shape=jax.ShapeDtypeStruct((m_dim, n_dim), out_dtype),
        scratch_shapes=[pltpu.VMEM((tm, tn), F32)] if use_acc else [],
        compiler_params=_cp("parallel", "parallel", "arbitrary"),
    )(*pieces, b, *extra)


def _ada_fwd(c_all, ada_w, ada_b_shard):
    depth, d, n = ada_w.shape
    nb = c_all.shape[0]

    def body(c_ref, w_ref, b_ref, o_ref, ca_ref):
        ca = _silu(c_ref[...])
        ca_ref[...] = _b(ca)
        o_ref[0] = _dg(ca, w_ref[0], _NN) + b_ref[0]

    return pl.pallas_call(
        body, name="ada_fwd",
        grid=(depth,),
        in_specs=[pl.BlockSpec((nb, d), lambda l: (0, 0)),
                  pl.BlockSpec((1, d, n), lambda l: (l, 0, 0)),
                  pl.BlockSpec((1, 1, n), lambda l: (l, 0, 0))],
        out_specs=[pl.BlockSpec((1, nb, n), lambda l: (l, 0, 0)),
                   pl.BlockSpec((nb, d), lambda l: (0, 0))],
        out_shape=[jax.ShapeDtypeStruct((depth, nb, n), F32), jax.ShapeDtypeStruct((nb, d), BF16)],
        compiler_params=_cp("arbitrary"),
    )(c_all, ada_w, ada_b_shard)


def _fold(acc):
    return jnp.sum(acc, axis=0, keepdims=True)


def _rinv(x):
    return lax.rsqrt(jnp.sum(x * x, axis=-1, keepdims=True) * (1.0 / D_MODEL) + EPS)


def _rms_bwd(a, xhat, rinv):
    return rinv * (a - xhat * (jnp.sum(a * xhat, axis=-1, keepdims=True) * (1.0 / D_MODEL)))


def _row_tile(seq):
    return min(seq, 256)


def _normmod_fwd(x, g, sc, sh, *, nseq, name):
    t, d = x.shape
    seq = t // nseq
    tr = _row_tile(seq)
    nt = seq // tr
    row = pl.BlockSpec((tr, d), lambda s, i: (s * nt + i, 0))
    per_seq = pl.BlockSpec((1, 1, d), lambda s, i: (s, 0, 0))

    def body(x_ref, g_ref, sc_ref, sh_ref, h_ref):
        x_v = x_ref[...]
        h_ref[...] = _b(x_v * _rinv(x_v) * (g_ref[...] * (1.0 + sc_ref[0])) + sh_ref[0])

    return pl.pallas_call(
        body, name=name, grid=(nseq, nt),
        in_specs=[row, pl.BlockSpec((1, d), lambda s, i: (0, 0)), per_seq, per_seq],
        out_specs=row,
        out_shape=jax.ShapeDtypeStruct((t, d), BF16),
        compiler_params=_cp("parallel", "parallel"),
    )(x, g, sc, sh)


NORM_TM = 512


def _matmul_normbwd(a, b, dxo, x, delta, gate, g, sc, *, nseq, name, dep=None):
    pieces = list(a) if isinstance(a, (list, tuple)) else [a]
    npc = len(pieces)
    t, width = pieces[0].shape
    k_dim, d = width * npc, b.shape[1]
    assert b.shape[0] == k_dim and all(p.shape == (t, width) for p in pieces)
    seq = t // nseq
    tm = min(NORM_TM, seq)
    per_seq_tiles = seq // tm
    tk = _mxu_tiles(width if npc > 1 else k_dim).pop()
    nk, per = k_dim // tk, width // tk
    has_delta = delta is not None
    extra = [] if dep is None else [dep]

    def body(*refs):
        a_refs, b_ref = refs[:npc], refs[npc]
        dxo_ref, x_ref = refs[npc + 1], refs[npc + 2]
        pos = npc + 3
        if has_delta:
            delta_ref, gate_ref = refs[pos], refs[pos + 1]
            pos += 2
        g_ref, sc_ref = refs[pos], refs[pos + 1]
        pos += 2 + len(extra)
        if has_delta:
            dx_ref, dd_ref, dgate_ref, dg_ref, dsc_ref, dsh_ref = refs[pos:pos + 6]
        else:
            dx_ref, dg_ref, dsc_ref, dsh_ref = refs[pos:pos + 4]
        acc_ref = refs[-1]
        i, k = pl.program_id(0), pl.program_id(1)

        def norm_bwd(dh_v):
            g_v, one_sc = g_ref[...], 1.0 + sc_ref[0]
            x_v = x_ref[...]
            rinv = _rinv(x_v)
            xhat = x_v * rinv
            dx = dxo_ref[...] + _rms_bwd(dh_v * (g_v * one_sc), xhat, rinv)
            dx_ref[...] = dx

            @pl.when(i == 0)
            def _():
                dg_ref[...] = jnp.zeros_like(dg_ref)

            @pl.when(i % per_seq_tiles == 0)
            def _():
                dsc_ref[...] = jnp.zeros_like(dsc_ref)
                dsh_ref[...] = jnp.zeros_like(dsh_ref)
                if has_delta:
                    dgate_ref[...] = jnp.zeros_like(dgate_ref)

            t_sum = _fold(dh_v * xhat)
            dg_ref[...] += t_sum * one_sc
            dsc_ref[0] += t_sum * g_v
            dsh_ref[0] += _fold(dh_v)
            if has_delta:
                dd_ref[...] = _b(dx * gate_ref[0])
                dgate_ref[0] += _fold(dx * delta_ref[...])

        def step(a_ref):
            p = lax.dot_general(a_ref[...], b_ref[...], _NN, preferred_element_type=F32)
            if nk == 1:
                norm_bwd(p)
            else:
                @pl.when(k == 0)
                def _():
                    acc_ref[...] = p

                @pl.when((k > 0) & (k < nk - 1))
                def _():
                    acc_ref[...] += p

                @pl.when(k == nk - 1)
                def _():
                    norm_bwd(acc_ref[...] + p)

        if npc == 1:
            step(a_refs[0])
        else:
            for pc in range(npc):
                pl.when((k >= pc * per) & (k < (pc + 1) * per))(functools.partial(step, a_refs[pc]))

    def a_map(pc, i, k):
        return (i, jnp.clip(k - pc * per, 0, per - 1) if npc > 1 else k)

    row = pl.BlockSpec((tm, d), lambda i, k: (i, 0))
    per_seq = pl.BlockSpec((1, 1, d), lambda i, k: (i // per_seq_tiles, 0, 0))
    vec = pl.BlockSpec((1, d), lambda i, k: (0, 0))
    shp = lambda *s, dt=F32: jax.ShapeDtypeStruct(s, dt)
    in_specs = [pl.BlockSpec((tm, tk), functools.partial(a_map, pc)) for pc in range(npc)]
    in_specs += [pl.BlockSpec((tk, d), lambda i, k: (k, 0)), row, row]
    operands = [*pieces, b, dxo, x]
    if has_delta:
        in_specs += [row, per_seq]
        operands += [delta, gate]
    in_specs += [vec, per_seq] + [ANY] * len(extra)
    operands += [g, sc, *extra]
    if has_delta:
        out_specs = [row, row, per_seq, vec, per_seq, per_seq]
        out_shape = [shp(t, d), shp(t, d, dt=BF16), shp(nseq, 1, d), shp(1, d), shp(nseq, 1, d), shp(nseq, 1, d)]
    else:
        out_specs = [row, vec, per_seq, per_seq]
        out_shape = [shp(t, d), shp(1, d), shp(nseq, 1, d), shp(nseq, 1, d)]
    outs = pl.pallas_call(
        body, name=name, grid=(t // tm, nk),
        in_specs=in_specs, out_specs=out_specs, out_shape=out_shape,
        scratch_shapes=[pltpu.VMEM((tm, d), F32)],
        compiler_params=_cp("arbitrary", "arbitrary"),
    )(*operands)
    if has_delta:
        return tuple(outs)
    dx, dg, dsc, dsh = outs
    return dx, None, None, dg, dsc, dsh


def _matmul_normfwd(a, b, xin, gate, g, sc, sh, *, nseq, name):
    t, k_dim = a.shape
    d = b.shape[1]
    assert b.shape[0] == k_dim and k_dim <= MATMUL_TILE_CAP
    seq = t // nseq
    tm = min(NORM_TM, seq)
    per_seq_tiles = seq // tm

    def body(a_ref, b_ref, xin_ref, gate_ref, g_ref, sc_ref, sh_ref, dl_ref, x_ref, h_ref):
        dl = lax.dot_general(a_ref[...], b_ref[...], _NN, preferred_element_type=F32)
        dl_ref[...] = dl
        x = xin_ref[...] + gate_ref[0] * dl
        x_ref[...] = x
        h_ref[...] = _b(x * _rinv(x) * (g_ref[...] * (1.0 + sc_ref[0])) + sh_ref[0])

    row = pl.BlockSpec((tm, d), lambda i: (i, 0))
    per_seq = pl.BlockSpec((1, 1, d), lambda i: (i // per_seq_tiles, 0, 0))
    return pl.pallas_call(
        body, name=name, grid=(t // tm,),
        in_specs=[pl.BlockSpec((tm, k_dim), lambda i: (i, 0)), pl.BlockSpec((k_dim, d), lambda i: (0, 0)),
                  row, per_seq, pl.BlockSpec((1, d), lambda i: (0, 0)), per_seq, per_seq],
        out_specs=[row, row, row],
        out_shape=[jax.ShapeDtypeStruct((t, d), F32), jax.ShapeDtypeStruct((t, d), F32), jax.ShapeDtypeStruct((t, d), BF16)],
        compiler_params=_cp("parallel"),
    )(a, b, xin, gate, g, sc, sh)


def _matmul_loss(a, b, xin, gate, fg, target, *, nseq, name):
    t, k_dim = a.shape
    d = b.shape[1]
    assert b.shape[0] == k_dim and k_dim <= MATMUL_TILE_CAP
    seq = t // nseq
    tm = min(NORM_TM, seq)
    per_seq_tiles = seq // tm

    def body(a_ref, b_ref, xin_ref, gate_ref, fg_ref, tgt_ref, dl_ref, loss_ref, dx_ref, dd_ref, dgate_ref, dfg_ref):
        i = pl.program_id(0)
        fg_v, gate_v = fg_ref[...], gate_ref[0]
        dl = lax.dot_general(a_ref[...], b_ref[...], _NN, preferred_element_type=F32)
        dl_ref[...] = dl
        x = xin_ref[...] + gate_v * dl
        rinv = _rinv(x)
        xhat = x * rinv
        err = xhat * fg_v - tgt_ref[...]
        dx = _rms_bwd(err * fg_v * (1.0 / d), xhat, rinv)
        dx_ref[...] = dx
        dd_ref[...] = _b(dx * gate_v)

        @pl.when(i == 0)
        def _():
            loss_ref[...] = jnp.zeros_like(loss_ref)
            dfg_ref[...] = jnp.zeros_like(dfg_ref)

        @pl.when(i % per_seq_tiles == 0)
        def _():
            dgate_ref[...] = jnp.zeros_like(dgate_ref)

        loss_ref[...] += jnp.sum(err * err) * (0.5 / d)
        dfg_ref[...] += _fold(err * xhat) * (1.0 / d)
        dgate_ref[0] += _fold(dx * dl)

    row = pl.BlockSpec((tm, d), lambda i: (i, 0))
    per_seq = pl.BlockSpec((1, 1, d), lambda i: (i // per_seq_tiles, 0, 0))
    vec = pl.BlockSpec((1, d), lambda i: (0, 0))
    return pl.pallas_call(
        body, name=name, grid=(t // tm,),
        in_specs=[pl.BlockSpec((tm, k_dim), lambda i: (i, 0)), pl.BlockSpec((k_dim, d), lambda i: (0, 0)),
                  row, per_seq, vec, row],
        out_specs=[row, pl.BlockSpec((1, 128), lambda i: (0, 0)), row, row, per_seq, vec],
        out_shape=[jax.ShapeDtypeStruct((t, d), F32), jax.ShapeDtypeStruct((1, 128), F32), jax.ShapeDtypeStruct((t, d), F32),
                   jax.ShapeDtypeStruct((t, d), BF16), jax.ShapeDtypeStruct((nseq, 1, d), F32),
                   jax.ShapeDtypeStruct((1, d), F32)],
        compiler_params=_cp("arbitrary"),
    )(a, b, xin, gate, fg, target)


CONV_TC = 256
CONV_LANES = 128
CONV_ROWS = 64
CONV_HALO = 8


def _conv_slabs(seq, fn):
    def step(i, carry):
        r0 = pl.multiple_of(i * CONV_ROWS, CONV_ROWS)
        for h in range(CONV_TC // CONV_LANES):
            fn(r0, slice(h * CONV_LANES, (h + 1) * CONV_LANES))
        return carry

    lax.fori_loop(0, seq // CONV_ROWS, step, 0)


def _slab(ref, r0, cols, seq):
    after = ref[pl.ds(pl.multiple_of(jnp.minimum(r0 + CONV_ROWS, seq - CONV_HALO), CONV_HALO), CONV_HALO), cols]
    return jnp.concatenate([ref[pl.ds(r0, CONV_ROWS), cols], jnp.where(r0 + CONV_ROWS < seq, after, 0.0)], axis=0)


def _conv_block(x, w_ref, b_ref):
    kw = w_ref.shape[0]
    rows = lax.broadcasted_iota(jnp.int32, x.shape, 0)
    y = b_ref[...] + w_ref[kw - 1:kw, :] * x
    for j in range(1, kw):
        y = y + w_ref[kw - 1 - j:kw - j, :] * jnp.where(rows >= j, pltpu.roll(x, j, 0), 0.0)
    return y


def _conv_block_bwd(dy, x, w_ref, dw_ref, db_ref):
    kw = w_ref.shape[0]
    n = x.shape[0]
    rows = lax.broadcasted_iota(jnp.int32, x.shape, 0)
    dx = w_ref[kw - 1:kw, :] * dy
    dw_ref[kw - 1:kw, :] += jnp.sum(dy * x, axis=0, keepdims=True)
    for j in range(1, kw):
        dy_j = jnp.where(rows < n - j, pltpu.roll(dy, n - j, 0), 0.0)
        dx = dx + w_ref[kw - 1 - j:kw - j, :] * dy_j
        dw_ref[kw - 1 - j:kw - j, :] += jnp.sum(dy_j * x, axis=0, keepdims=True)
    db_ref[...] += jnp.sum(dy, axis=0, keepdims=True)
    return dx


def _conv_bwd(dy_ext, x, w_ref, dw_ref, db_ref, cols):
    kw = w_ref.shape[0]
    n = dy_ext.shape[0]
    dy = dy_ext[:CONV_ROWS]
    dx = w_ref[kw - 1:kw, cols] * dy
    dw_ref[kw - 1:kw, cols] += jnp.sum(dy * x, axis=0, keepdims=True)
    for j in range(1, kw):
        dy_j = pltpu.roll(dy_ext, n - j, 0)[:CONV_ROWS]
        dx = dx + w_ref[kw - 1 - j:kw - j, cols] * dy_j
        dw_ref[kw - 1 - j:kw - j, cols] += jnp.sum(dy_j * x, axis=0, keepdims=True)
    db_ref[:, cols] += jnp.sum(dy, axis=0, keepdims=True)
    return dx


def _dsilu(pre):
    sg = jax.nn.sigmoid(pre)
    return pre * sg, sg * (1.0 + pre * (1.0 - sg))


def _ssd_conv_fwd(proj, w, b, *, nseq):
    t = proj.shape[0]
    seq = t // nseq
    nb = CONV_DIM // CONV_TC
    off = COL_XBC // CONV_TC

    def body(x_ref, w_ref, b_ref, o_ref, pre_ref):
        pre = _conv_block(x_ref[...], w_ref, b_ref)
        pre_ref[...] = pre
        o_ref[...] = _silu(pre)

    col = pl.BlockSpec((seq, CONV_TC), lambda j, s: (s, j))
    return pl.pallas_call(
        body, name="ssd_conv_fwd", grid=(nb, nseq),
        in_specs=[pl.BlockSpec((seq, CONV_TC), lambda j, s: (s, off + j)),
                  pl.BlockSpec((SSD_CONV, CONV_TC), lambda j, s: (0, j)),
                  pl.BlockSpec((1, CONV_TC), lambda j, s: (0, j))],
        out_specs=[col, col],
        out_shape=[jax.ShapeDtypeStruct((t, CONV_DIM), F32)] * 2,
        compiler_params=_cp("parallel", "parallel"),
    )(proj, w, b)


def _ssd_conv_bwd(dact, pre, proj, w, dproj, *, nseq):
    t = proj.shape[0]
    seq = t // nseq
    nb = CONV_DIM // CONV_TC
    off = COL_XBC // CONV_TC

    def body(da_ref, pre_ref, x_ref, w_ref, dproj_ref, dx_ref, dw_ref, db_ref):
        del dproj_ref

        @pl.when(pl.program_id(1) == 0)
        def _():
            dw_ref[...] = jnp.zeros_like(dw_ref)
            db_ref[...] = jnp.zeros_like(db_ref)

        def slab(r0, cols):
            _, dsilu = _dsilu(_slab(pre_ref, r0, cols, seq))
            dpre_ext = _slab(da_ref, r0, cols, seq) * dsilu
            x = x_ref[pl.ds(r0, CONV_ROWS), cols]
            dx_ref[pl.ds(r0, CONV_ROWS), cols] = _b(_conv_bwd(dpre_ext, x, w_ref, dw_ref, db_ref, cols))

        _conv_slabs(seq, slab)

    return pl.pallas_call(
        body, name="ssd_conv_bwd", grid=(nb, nseq),
        in_specs=[pl.BlockSpec((seq, CONV_TC), lambda j, s: (s, j)),
                  pl.BlockSpec((seq, CONV_TC), lambda j, s: (s, j)),
                  pl.BlockSpec((seq, CONV_TC), lambda j, s: (s, off + j)),
                  pl.BlockSpec((SSD_CONV, CONV_TC), lambda j, s: (0, j)),
                  ANY],
        out_specs=[pl.BlockSpec((seq, CONV_TC), lambda j, s: (s, off + j)),
                   pl.BlockSpec((SSD_CONV, CONV_TC), lambda j, s: (0, j)),
                   pl.BlockSpec((1, CONV_TC), lambda j, s: (0, j))],
        out_shape=[jax.ShapeDtypeStruct(dproj.shape, dproj.dtype), jax.ShapeDtypeStruct((SSD_CONV, CONV_DIM), F32),
                   jax.ShapeDtypeStruct((1, CONV_DIM), F32)],
        input_output_aliases={4: 0},
        compiler_params=_cp("parallel", "arbitrary"),
    )(dact, pre, proj, w, dproj)


def _ffn_act_fwd(up, w, b, *, nseq):
    t = up.shape[0]
    seq = t // nseq
    nb = D_FF // CONV_TC

    def body(g_ref, v_ref, w_ref, b_ref, o_ref):
        o_ref[...] = _b(_silu(_conv_block(g_ref[...].astype(F32), w_ref, b_ref)) * v_ref[...].astype(F32))

    col = pl.BlockSpec((seq, CONV_TC), lambda j, s: (s, j))
    return pl.pallas_call(
        body, name="ffn_act_fwd", grid=(nb, nseq),
        in_specs=[col,
                  pl.BlockSpec((seq, CONV_TC), lambda j, s: (s, nb + j)),
                  pl.BlockSpec((FF_CONV, CONV_TC), lambda j, s: (0, j)),
                  pl.BlockSpec((1, CONV_TC), lambda j, s: (0, j))],
        out_specs=col,
        out_shape=jax.ShapeDtypeStruct((t, D_FF), BF16),
        compiler_params=_cp("parallel", "parallel"),
    )(up, up, w, b)


def _ffn_act_bwd(dact, up, w, b, *, nseq):
    t = up.shape[0]
    seq = t // nseq
    nb = D_FF // CONV_TC

    def body(da_ref, g_ref, v_ref, w_ref, b_ref, dg_ref, dv_ref, dw_ref, db_ref):
        @pl.when(pl.program_id(1) == 0)
        def _():
            dw_ref[...] = jnp.zeros_like(dw_ref)
            db_ref[...] = jnp.zeros_like(db_ref)

        gate = g_ref[...].astype(F32)
        silu, dsilu = _dsilu(_conv_block(gate, w_ref, b_ref))
        da = da_ref[...].astype(F32)
        dv_ref[...] = _b(da * silu)
        dg_ref[...] = _b(_conv_block_bwd(da * v_ref[...].astype(F32) * dsilu, gate, w_ref, dw_ref, db_ref))

    col = pl.BlockSpec((seq, CONV_TC), lambda j, s: (s, j))
    return pl.pallas_call(
        body, name="ffn_act_bwd", grid=(nb, nseq),
        in_specs=[col, col,
                  pl.BlockSpec((seq, CONV_TC), lambda j, s: (s, nb + j)),
                  pl.BlockSpec((FF_CONV, CONV_TC), lambda j, s: (0, j)),
                  pl.BlockSpec((1, CONV_TC), lambda j, s: (0, j))],
        out_specs=[col, col,
                   pl.BlockSpec((FF_CONV, CONV_TC), lambda j, s: (0, j)),
                   pl.BlockSpec((1, CONV_TC), lambda j, s: (0, j))],
        out_shape=[jax.ShapeDtypeStruct((t, D_FF), BF16), jax.ShapeDtypeStruct((t, D_FF), BF16),
                   jax.ShapeDtypeStruct((FF_CONV, D_FF), F32), jax.ShapeDtypeStruct((1, D_FF), F32)],
        compiler_params=_cp("parallel", "arbitrary"),
    )(dact, up, up, w, b)


SSD_PAIRS = SSD_HEADS // 2
PAIR_W = 2 * SSD_HEAD_DIM
PAIRS_PER_GROUP = SSD_PAIRS // SSD_GROUPS


def _ssd_chunk(xs, bg, cg, dtr, z, hp, dtb, alog, dskip, ng):
    n = dtr.shape[0]
    dt = _softplus(dtr + dtb)
    cs = _cumsum_rows(dt * (-jnp.exp(alog)))
    cs_t = _transpose(cs)
    lane = lax.broadcasted_iota(jnp.int32, (1, SSD_HEADS), 1)
    sub = lax.broadcasted_iota(jnp.int32, (SSD_HEADS, 1), 0)
    row = lax.broadcasted_iota(jnp.int32, (n, 1), 0)
    causal = lax.broadcasted_iota(jnp.int32, (n, n), 0) >= lax.broadcasted_iota(jnp.int32, (n, n), 1)
    future = jnp.where(causal, 0.0, -1e30)
    first = lax.broadcasted_iota(jnp.int32, (1, PAIR_W), 1) < SSD_HEAD_DIM
    first_rows = lax.broadcasted_iota(jnp.int32, (PAIR_W, 1), 0) < SSD_HEAD_DIM
    first_f = first.astype(F32)
    cb = [_bdot_nt(cg[g], bg[g]) for g in range(SSD_GROUPS)]
    ys, hn = [], []
    for p in range(SSD_PAIRS):
        g = p // PAIRS_PER_GROUP
        col, decay, last = [], [], []
        for h in (2 * p, 2 * p + 1):
            oh = (lane == h).astype(F32)
            cs_h = jnp.sum(cs * oh, axis=1, keepdims=True)
            cs_row = jnp.sum(cs_t * (sub == h).astype(F32), axis=0, keepdims=True)
            col.append((jnp.sum(dt * oh, axis=1, keepdims=True), cs_h, jnp.sum(dskip * oh, axis=1, keepdims=True)))
            last.append(jnp.sum(jnp.where(row == n - 1, cs_h, 0.0), axis=0, keepdims=True))
            decay.append(jnp.exp(cs_h - cs_row + future))
        pair = lambda a, b: jnp.where(first, a, b)
        dt_p = pair(col[0][0], col[1][0])
        cs_p = pair(col[0][1], col[1][1])
        last_p = pair(last[0], last[1])
        xc = xs[p] * dt_p
        y = _bdot(cb[g] * decay[0], xc * first_f) + _bdot(cb[g] * decay[1], xc * (1.0 - first_f))
        y = y + _bdot_nt(cg[g], hp[p]) * jnp.exp(cs_p)
        y = y + pair(col[0][2], col[1][2]) * xs[p]
        keep = jnp.where(first_rows, jnp.exp(last[0]), jnp.exp(last[1]))
        hn.append(keep * hp[p] + _bdot_tn(xc * jnp.exp(last_p - cs_p), bg[g]))
        ys.append(y * _silu(z[p]))
    outs = []
    for g in range(SSD_GROUPS):
        ps = range(g * PAIRS_PER_GROUP, (g + 1) * PAIRS_PER_GROUP)
        ms = sum(jnp.sum(ys[p] * ys[p], axis=1, keepdims=True) for p in ps) * (1.0 / GROUP_WIDTH)
        r = lax.rsqrt(ms + EPS)
        outs += [ys[p] * r * ng[p] for p in ps]
    return outs, hn


def _hslices(ref, width, count, base=0):
    return [ref[:, base + k * width: base + (k + 1) * width] for k in range(count)]


def _ssd_load(xbc_ref, z_ref, dt_ref, ng_ref):
    xs = _hslices(xbc_ref, PAIR_W, SSD_PAIRS)
    bg = _hslices(xbc_ref, D_STATE, SSD_GROUPS, D_SSD)
    cg = _hslices(xbc_ref, D_STATE, SSD_GROUPS, D_SSD + SSD_GROUPS * D_STATE)
    z = _hslices(z_ref, PAIR_W, SSD_PAIRS)
    ng = _hslices(ng_ref, PAIR_W, SSD_PAIRS)
    return xs, bg, cg, dt_ref[:, 0:SSD_HEADS], z, ng


def _ssd_specs(nch):
    rowi = lambda s, c: s * nch + c
    return [pl.BlockSpec((CHUNK, CONV_DIM), lambda s, c: (rowi(s, c), 0)),
            pl.BlockSpec((CHUNK, D_SSD), lambda s, c: (rowi(s, c), COL_Z // D_SSD)),
            pl.BlockSpec((CHUNK, 128), lambda s, c: (rowi(s, c), COL_DT // 128)),
            pl.BlockSpec((1, SSD_HEADS), lambda s, c: (0, 0)),
            pl.BlockSpec((1, SSD_HEADS), lambda s, c: (0, 0)),
            pl.BlockSpec((1, SSD_HEADS), lambda s, c: (0, 0)),
            pl.BlockSpec((1, D_SSD), lambda s, c: (0, 0))]


def _ssd_fwd(xbc, proj, dtb, alog, dskip, ng, *, nseq):
    t = proj.shape[0]
    nch = t // nseq // CHUNK
    hd = PAIR_W

    def body(xbc_ref, z_ref, dt_ref, dtb_ref, alog_ref, dsk_ref, ng_ref, y_ref, hp_ref, h_ref):
        @pl.when(pl.program_id(1) == 0)
        def _():
            h_ref[...] = jnp.zeros_like(h_ref)

        xs, bg, cg, dtr, z, ngs = _ssd_load(xbc_ref, z_ref, dt_ref, ng_ref)
        hp_ref[0] = h_ref[...]
        hp = [h_ref[h * hd:(h + 1) * hd, :] for h in range(SSD_PAIRS)]
        outs, hn = _ssd_chunk(xs, bg, cg, dtr, z, hp, dtb_ref[...], alog_ref[...], dsk_ref[...], ngs)
        for h in range(SSD_PAIRS):
            y_ref[:, h * hd:(h + 1) * hd] = _b(outs[h])
            h_ref[h * hd:(h + 1) * hd, :] = hn[h]

    return pl.pallas_call(
        body, name="ssd_fwd", grid=(nseq, nch),
        in_specs=_ssd_specs(nch),
        out_specs=[pl.BlockSpec((CHUNK, D_SSD), lambda s, c: (s * nch + c, 0)),
                   pl.BlockSpec((1, D_SSD, D_STATE), lambda s, c: (s * nch + c, 0, 0))],
        out_shape=[jax.ShapeDtypeStruct((t, D_SSD + D_GM), BF16),
                   jax.ShapeDtypeStruct((t // CHUNK, D_SSD, D_STATE), F32)],
        scratch_shapes=[pltpu.VMEM((D_SSD, D_STATE), F32)],
        compiler_params=_cp("arbitrary", "arbitrary"),
    )(xbc, proj, proj, dtb, alog, dskip, ng)


def _ssd_bwd(dy, xbc, proj, hprev, dtb, alog, dskip, ng, *, nseq):
    t = proj.shape[0]
    nch = t // nseq // CHUNK
    hd = PAIR_W
    rev = lambda s, c: s * nch + (nch - 1 - c)

    def body(dy_ref, xbc_ref, z_ref, dt_ref, hp_ref, dtb_ref, alog_ref, dsk_ref, ng_ref,
             dxbc_ref, dproj_ref, ddtb_ref, dalog_ref, ddsk_ref, dng_ref, dh_ref):
        first = (pl.program_id(0) == 0) & (pl.program_id(1) == 0)

        @pl.when(pl.program_id(1) == 0)
        def _():
            dh_ref[...] = jnp.zeros_like(dh_ref)

        @pl.when(first)
        def _():
            ddtb_ref[...] = jnp.zeros_like(ddtb_ref)
            dalog_ref[...] = jnp.zeros_like(dalog_ref)
            ddsk_ref[...] = jnp.zeros_like(ddsk_ref)
            dng_ref[...] = jnp.zeros_like(dng_ref)

        xs, bg, cg, dtr, z, ngs = _ssd_load(xbc_ref, z_ref, dt_ref, ng_ref)
        hp = [hp_ref[0, h * hd:(h + 1) * hd, :] for h in range(SSD_PAIRS)]
        _, vjp = jax.vjp(_ssd_chunk, xs, bg, cg, dtr, z, hp, dtb_ref[...], alog_ref[...], dsk_ref[...], ngs)
        douts = [dy_ref[:, h * hd:(h + 1) * hd] for h in range(SSD_PAIRS)]
        dhn = [dh_ref[h * hd:(h + 1) * hd, :] for h in range(SSD_PAIRS)]
        dxs, dbg, dcg, ddtr, dz, dhp, ddtb, dalog, ddsk, dngs = vjp((douts, dhn))
        dproj_ref[:, :COL_Z] = jnp.zeros((CHUNK, COL_Z), BF16)
        dproj_ref[:, COL_XBC:] = jnp.zeros((CHUNK, N_INP - COL_XBC), BF16)
        for h in range(SSD_PAIRS):
            dxbc_ref[:, h * hd:(h + 1) * hd] = dxs[h]
            dproj_ref[:, COL_Z + h * hd: COL_Z + (h + 1) * hd] = _b(dz[h])
            dh_ref[h * hd:(h + 1) * hd, :] = dhp[h]
            dng_ref[:, h * hd:(h + 1) * hd] += dngs[h]
        for g in range(SSD_GROUPS):
            dxbc_ref[:, D_SSD + g * D_STATE: D_SSD + (g + 1) * D_STATE] = dbg[g]
            dxbc_ref[:, D_SSD + (SSD_GROUPS + g) * D_STATE: D_SSD + (SSD_GROUPS + g + 1) * D_STATE] = dcg[g]
        dproj_ref[:, COL_DT:COL_DT + SSD_HEADS] = _b(ddtr)
        ddtb_ref[...] += ddtb
        dalog_ref[...] += dalog
        ddsk_ref[...] += ddsk

    small = pl.BlockSpec((1, SSD_HEADS), lambda s, c: (0, 0))
    return pl.pallas_call(
        body, name="ssd_bwd", grid=(nseq, nch),
        in_specs=[pl.BlockSpec((CHUNK, D_SSD), lambda s, c: (rev(s, c), 0)),
                  pl.BlockSpec((CHUNK, CONV_DIM), lambda s, c: (rev(s, c), 0)),
                  pl.BlockSpec((CHUNK, D_SSD), lambda s, c: (rev(s, c), COL_Z // D_SSD)),
                  pl.BlockSpec((CHUNK, 128), lambda s, c: (rev(s, c), COL_DT // 128)),
                  pl.BlockSpec((1, D_SSD, D_STATE), lambda s, c: (rev(s, c), 0, 0)),
                  small, small, small,
                  pl.BlockSpec((1, D_SSD), lambda s, c: (0, 0))],
        out_specs=[pl.BlockSpec((CHUNK, CONV_DIM), lambda s, c: (rev(s, c), 0)),
                   pl.BlockSpec((CHUNK, N_INP), lambda s, c: (rev(s, c), 0)),
                   small, small, small,
                   pl.BlockSpec((1, D_SSD), lambda s, c: (0, 0))],
        out_shape=[jax.ShapeDtypeStruct((t, CONV_DIM), F32), jax.ShapeDtypeStruct((t, N_INP), BF16),
                   jax.ShapeDtypeStruct((1, SSD_HEADS), F32), jax.ShapeDtypeStruct((1, SSD_HEADS), F32),
                   jax.ShapeDtypeStruct((1, SSD_HEADS), F32), jax.ShapeDtypeStruct((1, D_SSD), F32)],
        scratch_shapes=[pltpu.VMEM((D_SSD, D_STATE), F32)],
        compiler_params=_cp("arbitrary", "arbitrary"),
    )(dy, xbc, proj, proj, hprev, dtb, alog, dskip, ng)


def _gmlp_chunk(gu, gv, ws, bs_cols, vg, og):
    n = gu[0].shape[0]
    mask = _tri(n, True)
    au = [_gelu(t) for t in gu]
    av = [_gelu(t) for t in gv]
    r = lax.rsqrt(sum(jnp.sum(t * t, axis=1, keepdims=True) for t in av) * (1.0 / D_GM) + EPS)
    p = []
    for h in range(GM_HEADS):
        sv = _bdot(ws[h] * mask, av[h] * r * vg[h]) + bs_cols[h]
        p.append(au[h] * sv)
    r2 = lax.rsqrt(sum(jnp.sum(t * t, axis=1, keepdims=True) for t in p) * (1.0 / D_GM) + EPS)
    return [p[h] * r2 * og[h] for h in range(GM_HEADS)]


def _gmlp_load(u_ref, v_ref, ws_ref, bst_ref, vg_ref, og_ref):
    gu = _hslices(u_ref, GM_HEAD_DIM, GM_HEADS)
    gv = _hslices(v_ref, GM_HEAD_DIM, GM_HEADS)
    ws = [ws_ref[h] for h in range(GM_HEADS)]
    bs_cols = [bst_ref[:, h:h + 1] for h in range(GM_HEADS)]
    return gu, gv, ws, bs_cols, _hslices(vg_ref, GM_HEAD_DIM, GM_HEADS), _hslices(og_ref, GM_HEAD_DIM, GM_HEADS)


def _gmlp_specs():
    return [pl.BlockSpec((CHUNK, D_GM), lambda i: (i, COL_U // D_GM)),
            pl.BlockSpec((CHUNK, D_GM), lambda i: (i, COL_V // D_GM)),
            pl.BlockSpec((GM_HEADS, CHUNK, CHUNK), lambda i: (0, 0, 0)),
            pl.BlockSpec((CHUNK, GM_HEADS), lambda i: (0, 0)),
            pl.BlockSpec((1, D_GM), lambda i: (0, 0)),
            pl.BlockSpec((1, D_GM), lambda i: (0, 0))]


def _gmlp_fwd(proj, ycat, ws, bst, vg, og):
    t = proj.shape[0]

    def body(u_ref, v_ref, ws_ref, bst_ref, vg_ref, og_ref, ycat_ref, o_ref):
        del ycat_ref
        outs = _gmlp_chunk(*_gmlp_load(u_ref, v_ref, ws_ref, bst_ref, vg_ref, og_ref))
        for h in range(GM_HEADS):
            o_ref[:, h * GM_HEAD_DIM:(h + 1) * GM_HEAD_DIM] = _b(outs[h])

    return pl.pallas_call(
        body, name="gmlp_fwd", grid=(t // CHUNK,),
        in_specs=_gmlp_specs() + [ANY],
        out_specs=pl.BlockSpec((CHUNK, D_GM), lambda i: (i, D_SSD // D_GM)),
        out_shape=jax.ShapeDtypeStruct(ycat.shape, ycat.dtype),
        input_output_aliases={6: 0},
        compiler_params=_cp("parallel"),
    )(proj, proj, ws, bst, vg, og, ycat)


def _gmlp_bwd(dy, proj, ws, bst, vg, og, dproj):
    t = proj.shape[0]
    w = GM_HEAD_DIM

    def body(dy_ref, u_ref, v_ref, ws_ref, bst_ref, vg_ref, og_ref, dproj_ref,
             dgm_ref, dws_ref, dbst_ref, dvg_ref, dog_ref):
        del dproj_ref

        @pl.when(pl.program_id(0) == 0)
        def _():
            dws_ref[...] = jnp.zeros_like(dws_ref)
            dbst_ref[...] = jnp.zeros_like(dbst_ref)
            dvg_ref[...] = jnp.zeros_like(dvg_ref)
            dog_ref[...] = jnp.zeros_like(dog_ref)

        _, vjp = jax.vjp(_gmlp_chunk, *_gmlp_load(u_ref, v_ref, ws_ref, bst_ref, vg_ref, og_ref))
        dgu, dgv, dws, dbs, dvg, dog = vjp(_hslices(dy_ref, w, GM_HEADS))
        for h in range(GM_HEADS):
            dgm_ref[:, h * w:(h + 1) * w] = _b(dgu[h])
            dgm_ref[:, D_GM + h * w: D_GM + (h + 1) * w] = _b(dgv[h])
            dws_ref[h] += dws[h]
            dbst_ref[:, h:h + 1] += dbs[h]
            dvg_ref[:, h * w:(h + 1) * w] += dvg[h]
            dog_ref[:, h * w:(h + 1) * w] += dog[h]

    return pl.pallas_call(
        body, name="gmlp_bwd", grid=(t // CHUNK,),
        in_specs=[pl.BlockSpec((CHUNK, D_GM), lambda i: (i, 1))] + _gmlp_specs() + [ANY],
        out_specs=[pl.BlockSpec((CHUNK, 2 * D_GM), lambda i: (i, COL_U // (2 * D_GM))),
                   pl.BlockSpec((GM_HEADS, CHUNK, CHUNK), lambda i: (0, 0, 0)),
                   pl.BlockSpec((CHUNK, GM_HEADS), lambda i: (0, 0)),
                   pl.BlockSpec((1, D_GM), lambda i: (0, 0)),
                   pl.BlockSpec((1, D_GM), lambda i: (0, 0))],
        out_shape=[jax.ShapeDtypeStruct(dproj.shape, dproj.dtype), jax.ShapeDtypeStruct((GM_HEADS, CHUNK, CHUNK), F32),
                   jax.ShapeDtypeStruct((CHUNK, GM_HEADS), F32), jax.ShapeDtypeStruct((1, D_GM), F32),
                   jax.ShapeDtypeStruct((1, D_GM), F32)],
        input_output_aliases={7: 0},
        compiler_params=_cp("arbitrary"),
    )(dy, proj, proj, ws, bst, vg, og, dproj)


def _local_step(x, target, mods, lw, final_g, *, nseq, big_w, grad_sink, small_sink):
    saved = []
    x0, delta, gate = x, None, None
    h1 = _normmod_fwd(x, lw[0]["norm1_g"], mods[0][1], mods[0][0], nseq=nseq, name="norm1_fwd_0")
    for l in range(DEPTH):
        w = lw[l]
        sh1, sc1, g1, sh2, sc2, g2 = mods[l]
        w_in = big_w(l, "w_in", h1)
        proj = _matmul(h1, w_in, tb=True, name=f"mm_in_{l}")
        xbc, xbc_pre = _ssd_conv_fwd(proj, w["ssd_conv_w"], w["ssd_conv_b"], nseq=nseq)
        ycat, hprev = _ssd_fwd(xbc, proj, w["ssd_dt_bias"], w["ssd_a_log"], w["ssd_d"], w["ssd_norm_g"], nseq=nseq)
        ycat = _gmlp_fwd(proj, ycat, w["gm_ws"], w["gm_bst"], w["gm_vnorm_g"], w["gm_out_g"])
        w_out = big_w(l, "w_out", ycat)
        mix, x1, h2 = _matmul_normfwd(ycat, w_out, x0, g1, w["norm2_g"], sc2, sh2, nseq=nseq, name=f"mm_out_{l}")
        ff_up = big_w(l, "ff_up", h2)
        up = _matmul(h2, ff_up, tb=True, name=f"mm_up_{l}", out_dtype=BF16)
        act = _ffn_act_fwd(up, w["ff_conv_w"], w["ff_conv_b"], nseq=nseq)
        ff_down = big_w(l, "ff_down", act)
        sv = dict(x0=x0, xin_delta=delta, xin_gate=gate, h1=h1, proj=proj, xbc=xbc, xbc_pre=xbc_pre, hprev=hprev,
                  ycat=ycat, mix=mix, x1=x1, h2=h2, up=up, act=act,
                  w_in=w_in, w_out=w_out, ff_up=ff_up, ff_down=ff_down)
        if l + 1 < DEPTH:
            nsh1, nsc1 = mods[l + 1][0], mods[l + 1][1]
            dn, x0, h1 = _matmul_normfwd(act, ff_down, x1, g2, lw[l + 1]["norm1_g"], nsc1, nsh1, nseq=nseq,
                                         name=f"mm_down_{l}")
        else:
            dn, loss, dx, ddelta, dgate, dfg = _matmul_loss(act, ff_down, x1, g2, final_g, target, nseq=nseq,
                                                            name=f"mm_down_{l}")
        saved.append(dict(sv, dn=dn))
        delta, gate = dn, g2

    small, dmods = [None] * DEPTH, [None] * DEPTH
    for l in reversed(range(DEPTH)):
        w, sv = lw[l], saved[l]
        sh1, sc1, g1, sh2, sc2, g2 = mods[l]
        dg2 = dgate
        g_ff_down = _matmul(sv["act"], ddelta, ta=True, name=f"mm_down_dw_{l}", out_dtype=BF16)
        dact = _matmul(ddelta, sv["ff_down"], tb=True, name=f"mm_down_dx_{l}", out_dtype=BF16)
        dgate_ff, dval_ff, dfcw, dfcb = _ffn_act_bwd(dact, sv["up"], w["ff_conv_w"], w["ff_conv_b"], nseq=nseq)
        g_ff_up = _matmul([dgate_ff, dval_ff], sv["h2"], ta=True, name=f"mm_up_dw_{l}", out_dtype=BF16)
        dep = grad_sink(l, "ffn", dict(ff_down=g_ff_down, ff_up=g_ff_up), dval_ff)
        dx, dmix, dg1, dn2g, dsc2, dsh2 = _matmul_normbwd([dgate_ff, dval_ff], sv["ff_up"], dx, sv["x1"], sv["mix"], g1,
                                                          w["norm2_g"], sc2, nseq=nseq, name=f"mm_up_dx_{l}", dep=dep)
        g_w_out = _matmul(sv["ycat"], dmix, ta=True, name=f"mm_out_dw_{l}", out_dtype=BF16)
        dep = grad_sink(l, "w_out", dict(w_out=g_w_out), dmix)
        dycat = _matmul(dmix, sv["w_out"], tb=True, name=f"mm_out_dx_{l}", dep=dep)
        dxbc_act, dproj, ddtb, dalog, ddsk, dng = _ssd_bwd(dycat, sv["xbc"], sv["proj"], sv["hprev"], w["ssd_dt_bias"],
                                                          w["ssd_a_log"], w["ssd_d"], w["ssd_norm_g"], nseq=nseq)
        dproj, dscw, dscb = _ssd_conv_bwd(dxbc_act, sv["xbc_pre"], sv["proj"], w["ssd_conv_w"], dproj, nseq=nseq)
        dproj, dws, dbst, dvg, dog = _gmlp_bwd(dycat, sv["proj"], w["gm_ws"], w["gm_bst"], w["gm_vnorm_g"], w["gm_out_g"], dproj)
        early = dict(norm2_g=dn2g, ssd_norm_g=dng, gm_vnorm_g=dvg, gm_out_g=dog,
                     ssd_conv_w=dscw, ssd_conv_b=dscb, ff_conv_w=dfcw, ff_conv_b=dfcb,
                     ssd_dt_bias=ddtb, ssd_a_log=dalog, ssd_d=ddsk, gm_ws=dws, gm_bs=dbst.T)
        dep = small_sink(l, early, small, dmods, dfg, loss)
        g_w_in = _matmul(dproj, sv["h1"], ta=True, name=f"mm_in_dw_{l}", out_dtype=BF16, dep=dep)
        dep = grad_sink(l, "w_in", dict(w_in=g_w_in), dproj)
        dx, ddelta, dgate, dn1g, dsc1, dsh1 = _matmul_normbwd(dproj, sv["w_in"], dx, sv["x0"], sv["xin_delta"],
                                                              sv["xin_gate"], w["norm1_g"], sc1, nseq=nseq,
                                                              name=f"mm_in_dx_{l}", dep=dep)
        small[l] = dict(early, norm1_g=dn1g)
        dmods[l] = jnp.concatenate([dsh1, dsc1, dg1, dsh2, dsc2, dg2], axis=-1)[:, 0, :]
    return dx, small, dmods


def _all_gather(arrs, name, dep=None):
    n = len(arrs)
    extra = [] if dep is None else [dep]

    def body(*refs):
        ins, outs = refs[:n], refs[n + len(extra):2 * n + len(extra)]
        send_sems, recv_sems, local_sems = refs[2 * n + len(extra):]
        x, y, c = lax.axis_index("x"), lax.axis_index("y"), lax.axis_index("c")
        me, sibling = (x, y, c), (x, y, 1 - c)
        chips = [(1 - x, y), (x, 1 - y), (1 - x, 1 - y)]

        def copy(i, k, block, to, src=None):
            px, py, pc = block
            dst = outs[i].at[4 * px + 2 * py + pc]
            return pltpu.make_async_remote_copy(
                src_ref=dst if src is None else src, dst_ref=dst,
                send_sem=send_sems.at[7 * i + k], recv_sem=recv_sems.at[7 * i + k],
                device_id=to, device_id_type=MESH)

        mine = [pltpu.make_async_copy(ins[i], outs[i].at[4 * x + 2 * y + c], local_sems.at[i]) for i in range(n)]
        for cp in mine:
            cp.start()
        first = []
        for i in range(n):
            first.append(copy(i, 0, me, sibling, src=ins[i]))
            first += [copy(i, 1 + j, me, (*chip, c), src=ins[i]) for j, chip in enumerate(chips)]
        for cp in first:
            cp.start()
        passed = []
        for j, chip in enumerate(chips):
            for i in range(n):
                copy(i, 1 + j, (*chip, c), me).wait_recv()
                fwd = copy(i, 4 + j, (*chip, c), sibling)
                fwd.start()
                passed.append(fwd)
        for i in range(n):
            copy(i, 0, sibling, me).wait_recv()
            for j, chip in enumerate(chips):
                copy(i, 4 + j, (*chip, 1 - c), me).wait_recv()
        for cp in first + passed:
            cp.wait_send()
        for cp in mine:
            cp.wait()

    return pl.pallas_call(
        body, name=name,
        in_specs=[ANY] * (n + len(extra)), out_specs=[ANY] * n,
        out_shape=[jax.ShapeDtypeStruct((N_DEV,) + a.shape, a.dtype) for a in arrs],
        scratch_shapes=[pltpu.SemaphoreType.DMA((7 * n,)), pltpu.SemaphoreType.DMA((7 * n,)),
                        pltpu.SemaphoreType.DMA((n,))],
    )(*arrs, *extra)


HBM = pl.BlockSpec(memory_space=pltpu.HBM)
SEM = pl.BlockSpec(memory_space=pltpu.SEMAPHORE)
EFFECT = pltpu.SideEffectType.DATAFLOW_SIDE_EFFECTING


def _peer(k):
    x, y, c = lax.axis_index("x"), lax.axis_index("y"), lax.axis_index("c")
    return (1 - x if k & 4 else x, 1 - y if k & 2 else y, 1 - c if k & 1 else c)


ALL_PEERS = tuple(range(1, N_DEV))
OTHER_CHIPS = (2, 4, 6)


def _xc_copies(scatter, srcs, lands, send_sems, recv_sems, peers=ALL_PEERS):
    x, y, c = lax.axis_index("x"), lax.axis_index("y"), lax.axis_index("c")
    copies = []
    for i in range(len(srcs)):
        for k in (peers[i] if isinstance(peers[0], tuple) else peers):
            px, py, pc = _peer(k)
            src = srcs[i].at[4 * px + 2 * py + pc] if scatter else srcs[i]
            dst = lands[i].at[k - 1] if scatter else lands[i].at[4 * x + 2 * y + c]
            copies.append(pltpu.make_async_remote_copy(
                src_ref=src, dst_ref=dst, send_sem=send_sems[i].at[k - 1], recv_sem=recv_sems[i].at[k - 1],
                device_id=(px, py, pc), device_id_type=MESH))
    return copies


def _xc_start(scatter, arrs, after, name, peers=ALL_PEERS):
    n = len(arrs)
    lands = [lax.empty((N_DEV - 1,) + a.shape[1:] if scatter else (N_DEV,) + a.shape, a.dtype) for a in arrs]

    def body(*refs):
        srcs, lnd = refs[:n], refs[n:2 * n]
        send_sems, recv_sems = refs[2 * n + 1:3 * n + 1], refs[3 * n + 1:4 * n + 1]
        token = refs[6 * n + 1]
        for cp in _xc_copies(scatter, srcs, lnd, send_sems, recv_sems, peers):
            cp.start()
        token[...] = jnp.zeros_like(token)

    outs = pl.pallas_call(
        body, name=name,
        out_shape=[pltpu.SemaphoreType.DMA((N_DEV - 1,))] * (2 * n)
        + [pltpu.HBM(a.shape, a.dtype) for a in arrs] + [pltpu.HBM(a.shape, a.dtype) for a in lands]
        + [jax.ShapeDtypeStruct((8, 128), F32)],
        in_specs=[HBM] * (2 * n) + [ANY],
        out_specs=[SEM] * (2 * n) + [HBM] * (2 * n) + [pl.BlockSpec(memory_space=pltpu.VMEM)],
        input_output_aliases={i: 2 * n + i for i in range(2 * n)},
        compiler_params=pltpu.CompilerParams(has_side_effects=EFFECT),
    )(*[pltpu.with_memory_space_constraint(a, pltpu.HBM) for a in list(arrs) + lands], after)
    return outs[:n], outs[n:2 * n], outs[2 * n:3 * n], outs[3 * n:4 * n], outs[4 * n][0, 0]


def _xc_wait(scatter, send_sems, recv_sems, srcs, lands, after, name, peers=ALL_PEERS):
    n = len(srcs)

    def body(*refs):
        s_refs, l_refs = refs[:n], refs[n:2 * n]
        ss, rs = refs[2 * n:3 * n], refs[3 * n:4 * n]
        for cp in _xc_copies(scatter, s_refs, l_refs, ss, rs, peers):
            cp.wait_send()
            cp.wait_recv()

    outs = pl.pallas_call(
        body, name=name,
        out_shape=[pltpu.HBM(a.shape, a.dtype) for a in list(srcs) + list(lands)],
        in_specs=[HBM] * (2 * n) + [SEM] * (2 * n) + [ANY],
        out_specs=[HBM] * (2 * n),
        input_output_aliases={i: i for i in range(2 * n)},
        compiler_params=pltpu.CompilerParams(has_side_effects=EFFECT),
    )(*srcs, *lands, *send_sems, *recv_sems, after)
    return outs[:n], outs[n:]


def _sib_copies(zones, send_sems, recv_sems):
    x, y, c = lax.axis_index("x"), lax.axis_index("y"), lax.axis_index("c")
    copies = []
    for i in range(len(zones)):
        for q in range(N_DEV // 2):
            slot = zones[i].at[2 * q + c]
            copies.append(pltpu.make_async_remote_copy(
                src_ref=slot, dst_ref=slot, send_sem=send_sems[i].at[q], recv_sem=recv_sems[i].at[q],
                device_id=(x, y, 1 - c), device_id_type=MESH))
    return copies


def _sib_start(zones, name):
    n = len(zones)

    def body(*refs):
        for cp in _sib_copies(refs[:n], refs[n:2 * n], refs[2 * n:3 * n]):
            cp.start()

    outs = pl.pallas_call(
        body, name=name,
        out_shape=[pltpu.SemaphoreType.DMA((N_DEV // 2,))] * (2 * n) + [pltpu.HBM(a.shape, a.dtype) for a in zones],
        in_specs=[HBM] * n,
        out_specs=[SEM] * (2 * n) + [HBM] * n,
        input_output_aliases={i: 2 * n + i for i in range(n)},
        compiler_params=pltpu.CompilerParams(has_side_effects=EFFECT),
    )(*[pltpu.with_memory_space_constraint(a, pltpu.HBM) for a in zones])
    return outs[:n], outs[n:2 * n], outs[2 * n:]


def _sib_wait(send_sems, recv_sems, zones, name):
    n = len(zones)

    def body(*refs):
        for cp in _sib_copies(refs[:n], refs[n:2 * n], refs[2 * n:3 * n]):
            cp.wait_send()
            cp.wait_recv()

    return pl.pallas_call(
        body, name=name,
        out_shape=[pltpu.HBM(a.shape, a.dtype) for a in zones],
        in_specs=[HBM] * n + [SEM] * (2 * n),
        out_specs=[HBM] * n,
        input_output_aliases={i: i for i in range(n)},
        compiler_params=pltpu.CompilerParams(has_side_effects=EFFECT),
    )(*zones, *send_sems, *recv_sems)


def _adamw_math(w, g, m, v):
    m = ADAM_B1 * m + (1.0 - ADAM_B1) * g
    v = ADAM_B2 * v + (1.0 - ADAM_B2) * (g * g)
    m_hat = m / (1.0 - ADAM_B1 ** ADAM_STEP)
    v_hat = v / (1.0 - ADAM_B2 ** ADAM_STEP)
    delta = -ADAM_LR * (m_hat / (jnp.sqrt(v_hat) + ADAM_EPS) + ADAM_WD * w)
    return delta, m, v


def _adamw_sharded(parts, w, m, v, pos, name):
    depth, rows, cols = w.shape
    tr = _tile(rows, 256) if rows % 8 == 0 else rows
    npart = len(parts)

    def body(pos_ref, *refs):
        prefs = refs[:npart]
        w_ref, m_ref, v_ref, g_out, d_out, m_out, v_out = refs[npart:]
        g = prefs[0][...]
        for pr in prefs[1:]:
            g = g + pr[...]
        delta, mn, vn = _adamw_math(w_ref[...], g, m_ref[...], v_ref[...])
        g_out[...] = g
        d_out[...] = delta
        m_out[...] = mn
        v_out[...] = vn

    def part_spec(fn):
        return pl.BlockSpec((1, tr, cols), lambda l, i, p: (fn(p) * depth + l, i, 0))

    blk = pl.BlockSpec((1, tr, cols), lambda l, i, p: (l, i, 0))
    shp = jax.ShapeDtypeStruct((depth, rows, cols), F32)
    return pl.pallas_call(
        body, name=name,
        grid_spec=pltpu.PrefetchScalarGridSpec(
            num_scalar_prefetch=1, grid=(depth, rows // tr),
            in_specs=[part_spec(fn) for _, fn in parts] + [blk, blk, blk],
            out_specs=[blk, blk, blk, blk]),
        out_shape=[shp, shp, shp, shp],
        compiler_params=_cp("parallel", "parallel"),
    )(pos, *[a for a, _ in parts], w, m, v)


def _adamw_layer(parts, w, m, v, pos, layer, prev, name):
    depth, rows, cols = w.shape
    npart = len(parts)
    nprev = 0 if prev is None else 4
    if rows % 16 == 0:
        tr, tc = max(t for t in range(16, 257, 16) if rows % t == 0), cols
    else:
        tr, tc = rows, _tile(cols, 256)
    pick = (lambda i: (i, 0)) if rows % 16 == 0 else (lambda i: (0, i))

    def body(pos_ref, *refs):
        prefs = refs[:npart]
        w_ref, m_ref, v_ref = refs[npart:npart + 3]
        g_out, d_out, m_out, v_out = refs[npart + 3 + nprev:]
        g = prefs[0][...].astype(F32)
        for pr in prefs[1:]:
            g = g + pr[...].astype(F32)
        delta, mn, vn = _adamw_math(w_ref[...], g, m_ref[...], v_ref[...])
        g_out[...] = g
        d_out[...] = delta
        m_out[...] = mn
        v_out[...] = vn

    def part_spec(fn):
        return pl.BlockSpec((1, tr, tc), lambda i, p: (fn(p), *pick(i)))

    blk = pl.BlockSpec((1, tr, tc), lambda i, p: (layer, *pick(i)))
    shp = jax.ShapeDtypeStruct((depth, rows, cols), F32)
    first_prev = 1 + npart + 3
    return pl.pallas_call(
        body, name=name,
        grid_spec=pltpu.PrefetchScalarGridSpec(
            num_scalar_prefetch=1, grid=(rows // tr * (cols // tc),),
            in_specs=[part_spec(fn) for _, fn in parts] + [blk, blk, blk] + [ANY] * nprev,
            out_specs=[blk, blk, blk, blk]),
        out_shape=[shp, shp, shp, shp],
        input_output_aliases={first_prev + j: j for j in range(nprev)},
        compiler_params=_cp("parallel"),
    )(pos, *[a for a, _ in parts], w, m, v, *(prev or ()))


_P1024 = ["norm1_g", "norm2_g", "ssd_norm_g", "gm_vnorm_g", "gm_out_g"]
_P16 = ["ssd_dt_bias", "ssd_a_log", "ssd_d"]


def _adamw_small(gath, wmv):
    names = list(wmv.keys())
    classes = list(gath.keys())
    flat_in = [gath[k] for k in classes]
    for nme in names:
        flat_in += list(wmv[nme])
    out_shapes = []
    for nme in names:
        out_shapes += [jax.ShapeDtypeStruct(wmv[nme][0].shape, F32)] * 4
    out_shapes += [jax.ShapeDtypeStruct((DEPTH, SSD_CONV, CONV_DIM), F32), jax.ShapeDtypeStruct((DEPTH, FF_CONV, D_FF), F32),
                   jax.ShapeDtypeStruct((1, SSD_HEADS), F32)]
    scratch = [pltpu.VMEM(gath[k].shape[1:], F32) for k in classes]
    ncls = len(classes)

    def body(*refs):
        g_refs = dict(zip(classes, refs[:ncls]))
        pos = ncls
        w_refs = {}
        for nme in names:
            w_refs[nme] = refs[pos:pos + 3]
            pos += 3
        o_refs = {}
        for nme in names:
            o_refs[nme] = refs[pos:pos + 4]
            pos += 4
        scw_out, fcw_out, loss_out = refs[pos], refs[pos + 1], refs[pos + 2]
        s_refs = dict(zip(classes, refs[pos + 3:]))
        for k in classes:
            acc = g_refs[k][0]
            for dev in range(1, N_DEV):
                acc = acc + g_refs[k][dev]
            s_refs[k][...] = acc

        def apply(nme, grad_of):
            w_ref, m_ref, v_ref = w_refs[nme]
            g_out, d_out, m_out, v_out = o_refs[nme]
            shape = w_ref.shape
            if len(shape) == 2:
                idxs = [(slice(l, l + 1),) for l in range(shape[0])]
            elif len(shape) == 3:
                idxs = [(l,) for l in range(shape[0])]
            else:
                idxs = [(l, h) for l in range(shape[0]) for h in range(shape[1])]
            for n_i, ix in enumerate(idxs):
                g = grad_of(n_i)
                delta, mn, vn = _adamw_math(w_ref[ix], g, m_ref[ix], v_ref[ix])
                g_out[ix] = g
                d_out[ix] = delta
                m_out[ix] = mn
                v_out[ix] = vn

        s1024, s1536, s2816, s16, s128, s6144, late1024, late6144 = (s_refs[k] for k in classes)
        s1024[0:1, :] += late1024[...]
        s6144[0:late6144.shape[0], :] += late6144[...]
        for n_i, nme in enumerate(_P1024):
            apply(nme, lambda l, b=2 * n_i: s1024[b + l:b + l + 1, :])
        apply("final_g", lambda l: s1024[10:11, :])
        apply("ssd_conv_b", lambda l: s1536[8 + l:9 + l, :])
        apply("ff_conv_b", lambda l: s2816[6 + l:7 + l, :])
        for n_i, nme in enumerate(_P16):
            apply(nme, lambda l, b=2 * n_i: s16[b + l:b + l + 1, :])
        apply("gm_ws", lambda q: s128[q * CHUNK:(q + 1) * CHUNK, :])
        apply("gm_bs", lambda l: s128[2048 + 8 * l:2048 + 8 * (l + 1), :])
        apply("ada_b", lambda l: s6144[2 * l:2 * l + 1, :] + s6144[2 * l + 1:2 * l + 2, :])
        for l in range(DEPTH):
            scw_out[l] = s1536[SSD_CONV * l:SSD_CONV * (l + 1), :]
            fcw_out[l] = s2816[FF_CONV * l:FF_CONV * (l + 1), :]
        loss_out[...] = s16[2 * len(_P16):2 * len(_P16) + 1, :]

    outs = pl.pallas_call(
        body, name="adamw_small",
        out_shape=out_shapes,
        scratch_shapes=scratch,
        compiler_params=pltpu.CompilerParams(vmem_limit_bytes=VMEM_LIMIT),
    )(*flat_in)
    res = {nme: tuple(outs[4 * i:4 * i + 4]) for i, nme in enumerate(names)}
    return res, outs[-3], outs[-2], outs[-1]


_WEIGHTS = ['ada_w', 'ada_b', 'norm1_g', 'norm2_g', 'w_in', 'ssd_conv_w', 'ssd_conv_b', 'ssd_dt_bias', 'ssd_a_log',
            'ssd_d', 'ssd_norm_g', 'gm_vnorm_g', 'gm_ws', 'gm_bs', 'gm_out_g', 'w_out', 'ff_up', 'ff_conv_w',
            'ff_conv_b', 'ff_down', 'final_g']


_O_XBC, _O_DT, _O_GM = D_SSD, D_SSD + CONV_DIM, D_SSD + CONV_DIM + SSD_HEADS


_TRANSPOSED = ("w_in", "ff_up")


def _full_weight(name, g):
    full = g.reshape(g.shape[0] * g.shape[1], g.shape[2])
    if name != "w_in":
        return full
    zpad = jnp.zeros((N_INP - N_IN, full.shape[1]), full.dtype)
    return jnp.concatenate([full[_O_GM:], full[:_O_XBC], full[_O_XBC:_O_DT], full[_O_DT:_O_GM], zpad], axis=0)


def _by_owner(name, grad):
    if name == "w_in":
        grad = jnp.concatenate([grad[COL_Z:COL_XBC], grad[COL_XBC:COL_DT], grad[COL_DT:COL_DT + SSD_HEADS], grad[:COL_Z]], axis=0)
    return grad.reshape(N_DEV, grad.shape[0] // N_DEV, grad.shape[1])


def kernel(x, c, ada_w, ada_b, norm1_g, norm2_g, w_in, ssd_conv_w, ssd_conv_b, ssd_dt_bias, ssd_a_log, ssd_d, ssd_norm_g, gm_vnorm_g, gm_ws, gm_bs, gm_out_g, w_out, ff_up, ff_conv_w, ff_conv_b, ff_down, final_g, loss_target, m_ada_w, m_ada_b, m_norm1_g, m_norm2_g, m_w_in, m_ssd_conv_w, m_ssd_conv_b, m_ssd_dt_bias, m_ssd_a_log, m_ssd_d, m_ssd_norm_g, m_gm_vnorm_g, m_gm_ws, m_gm_bs, m_gm_out_g, m_w_out, m_ff_up, m_ff_conv_w, m_ff_conv_b, m_ff_down, m_final_g, v_ada_w, v_ada_b, v_norm1_g, v_norm2_g, v_w_in, v_ssd_conv_w, v_ssd_conv_b, v_ssd_dt_bias, v_ssd_a_log, v_ssd_d, v_ssd_norm_g, v_gm_vnorm_g, v_gm_ws, v_gm_bs, v_gm_out_g, v_w_out, v_ff_up, v_ff_conv_w, v_ff_conv_b, v_ff_down, v_final_g):
    given = dict(locals())
    wts = {n: given[n] for n in _WEIGHTS}
    mom = {n: given["m_" + n] for n in _WEIGHTS}
    var = {n: given["v_" + n] for n in _WEIGHTS}
    nseq, seq, d = x.shape
    ix, iy, ic = lax.axis_index("x"), lax.axis_index("y"), lax.axis_index("c")
    me = 4 * ix + 2 * iy + ic
    me_arr = me.astype(jnp.int32).reshape(1)

    for nme in _TRANSPOSED:
        wts[nme], mom[nme], var[nme] = (jnp.transpose(a, (0, 2, 1)) for a in (wts[nme], mom[nme], var[nme]))

    def shard(l, name):
        return _b(wts[name][l])

    g_scw, g_fcw, c_all = _all_gather([ssd_conv_w, ff_conv_w, c], "gather_first")
    scw_f = jnp.transpose(g_scw, (1, 2, 0, 3)).reshape(DEPTH, SSD_CONV, CONV_DIM)
    fcw_f = jnp.transpose(g_fcw, (1, 2, 0, 3)).reshape(DEPTH, FF_CONV, D_FF)
    c_all = c_all.reshape(N_DEV * nseq, d)

    n_ada = ada_w.shape[2]
    ada_b_shard = lax.dynamic_slice_in_dim(ada_b, me * n_ada, n_ada, axis=1).reshape(DEPTH, 1, n_ada)
    mod_part, c_act = _ada_fwd(c_all, ada_w, ada_b_shard)
    first_ssem, first_rsem, first_src, first_land, first_zero = _xc_start(
        False, [mod_part, shard(0, "w_in")], c_act, "ag_first_start", peers=[ALL_PEERS, OTHER_CHIPS])
    (mod_src,), (mod_land,) = _xc_wait(False, first_ssem[:1], first_rsem[:1], first_src[:1], first_land[:1], c_act,
                                       "mod_wait")
    mod_g = lax.dynamic_update_index_in_dim(mod_land, mod_src, me, 0)
    mod_all = jnp.transpose(mod_g, (1, 2, 0, 3)).reshape(DEPTH, N_DEV * nseq, N_MOD * d)
    mod_mine = lax.dynamic_slice_in_dim(mod_all, me * nseq, nseq, axis=1)
    mod_k = jnp.transpose(mod_mine.reshape(DEPTH, nseq, N_MOD, 1, d), (0, 2, 1, 3, 4))
    mods = [[mod_k[l, k] for k in range(N_MOD)] for l in range(DEPTH)]

    later =[(0, "w_out"), (0, "ff_up"), (0, "ff_down"), (1, "w_in"), (1, "w_out"), (1, "ff_up"), (1, "ff_down")]
    ag_groups = {(0, "w_out"): [0], (0, "ff_up"): [1, 2], (1, "w_in"): [3, 4], (1, "ff_up"): [5, 6]}
    big_cache, ag = {}, {}

    def big_w(l, name, after):
        if (l, name) == (0, "w_in") and (l, name) not in big_cache:
            ag["ssem"], ag["rsem"], ag["src"], ag["land"], started = _xc_start(
                False, [shard(l2, n2) for l2, n2 in later], after, "ag_start")
            srcs, lands = _xc_wait(False, first_ssem[1:], first_rsem[1:], first_src[1:], first_land[1:],
                                   jnp.full((8, 128), started, F32), "ag_first_wait", peers=OTHER_CHIPS)
            zone = lax.dynamic_update_index_in_dim(lands[0], srcs[0], me, 0)
            (zone,) = _sib_wait(*_sib_start([zone], "ag_first_sib_start"), "ag_first_sib_wait")
            big_cache[(l, name)] = _full_weight(name, zone)
        if (l, name) not in big_cache:
            idx = ag_groups[(l, name)]
            pick = lambda seq_: [seq_[i] for i in idx]
            srcs, lands = _xc_wait(False, pick(ag["ssem"]), pick(ag["rsem"]), pick(ag["src"]), pick(ag["land"]), after,
                                   f"ag_wait_{l}_{name}")
            for i, src, land in zip(idx, srcs, lands):
                big_cache[later[i]] = _full_weight(later[i][1], lax.dynamic_update_index_in_dim(land, src, me, 0))
        return big_cache[(l, name)]

    lw = []
    for l in range(DEPTH):
        lw.append(dict(
            norm1_g=norm1_g[l:l + 1] + (first_zero if l == 0 else 0.0), norm2_g=norm2_g[l:l + 1], ssd_conv_w=scw_f[l],
            ssd_conv_b=ssd_conv_b[l:l + 1], ssd_dt_bias=ssd_dt_bias[l:l + 1], ssd_a_log=ssd_a_log[l:l + 1],
            ssd_d=ssd_d[l:l + 1], ssd_norm_g=ssd_norm_g[l:l + 1], gm_vnorm_g=gm_vnorm_g[l:l + 1], gm_ws=gm_ws[l],
            gm_bst=gm_bs[l].T, gm_out_g=gm_out_g[l:l + 1], ff_conv_w=fcw_f[l], ff_conv_b=ff_conv_b[l:l + 1]))

    outs = {}
    pending = {}

    def rs_finish(l, group, after):
        names, ssem, rsem, srcs, lands = pending.pop((l, group))
        srcs, lands = _xc_wait(True, ssem, rsem, srcs, lands, after, f"rs_wait_{l}_{group}")
        for nme, own, land in zip(names, srcs, lands):
            parts = [(own, lambda p: p[0])] + [(land, lambda p, k=k: k) for k in range(N_DEV - 1)]
            outs[nme] = _adamw_layer(parts, wts[nme], mom[nme], var[nme], me_arr, l, outs.get(nme), f"adamw_{nme}_{l}")
        return outs[names[-1]][0]

    def grad_sink(l, group, grads, after):
        names = list(grads)
        ssem, rsem, srcs, lands, zero = _xc_start(True, [_by_owner(n, grads[n]) for n in names], after, f"rs_start_{l}_{group}")
        pending[(l, group)] = (names, ssem, rsem, srcs, lands)
        return zero.reshape(1, 1)

    early_gather = {}

    def small_sink(l, early, small, dmods, dfg, loss_p):
        if l > 0:
            return None
        layers = [dict(early, norm1_g=jnp.zeros((1, d), F32))] + small[1:]
        rows = lambda name: [layers[k][name] for k in range(DEPTH)]
        packed = [
            jnp.concatenate(sum([rows(n) for n in _P1024], []) + [dfg], axis=0),
            jnp.concatenate(rows("ssd_conv_w") + rows("ssd_conv_b"), axis=0),
            jnp.concatenate(rows("ff_conv_w") + rows("ff_conv_b"), axis=0),
            jnp.concatenate(sum([rows(n) for n in _P16], []) + [loss_p[:, :SSD_HEADS]], axis=0),
            jnp.concatenate([layers[k]["gm_ws"].reshape(GM_HEADS * CHUNK, CHUNK) for k in range(DEPTH)] + rows("gm_bs"), axis=0),
            jnp.concatenate([jnp.zeros((nseq, N_MOD * d), F32)] + dmods[1:], axis=0)]
        ssem, rsem, srcs, lands, zero = _xc_start(False, packed, packed[0], "small_start")
        early_gather.update(ssem=ssem, rsem=rsem, srcs=srcs, lands=lands)
        return zero.reshape(1, 1)

    grad_x, small, dmods = _local_step(
        x.reshape(nseq * seq, d), loss_target.reshape(nseq * seq, d), mods, lw, final_g.reshape(1, d), nseq=nseq,
        big_w=big_w, grad_sink=grad_sink, small_sink=small_sink)

    done = grad_x
    for l, grp in ((1, "ffn"), (1, "w_out"), (1, "w_in"), (0, "ffn"), (0, "w_out")):
        done = rs_finish(l, grp, done)
    srcs, lands = _xc_wait(False, early_gather["ssem"], early_gather["rsem"], early_gather["srcs"],
                           early_gather["lands"], done, "small_wait")
    gathered = [lax.dynamic_update_index_in_dim(land, src, me, 0) for src, land in zip(srcs, lands)]
    gathered += _all_gather([small[0]["norm1_g"], dmods[0]], "gather_late", dep=gathered[0])
    gath = dict(zip(["p1024", "p1536", "p2816", "p16", "p128", "p6144", "late1024", "late6144"], gathered))

    dmod_all = jnp.concatenate([gath["late6144"].reshape(1, N_DEV * nseq, N_MOD * d),
                                jnp.transpose(gath["p6144"].reshape(N_DEV, DEPTH, nseq, N_MOD * d)[:, 1:], (1, 0, 2, 3)).reshape(
                                    DEPTH - 1, N_DEV * nseq, N_MOD * d)], axis=0)
    small_names = _P1024 + ["final_g", "ssd_conv_b", "ff_conv_b"] + _P16 + ["gm_ws", "gm_bs", "ada_b"]
    wmv = {}
    for nme in small_names:
        if nme == "final_g":
            wmv[nme] = tuple(a.reshape(1, d) for a in (wts[nme], mom[nme], var[nme]))
        else:
            wmv[nme] = (wts[nme], mom[nme], var[nme])
    small_out, scw_full, fcw_full, loss_sum = _adamw_small(gath, wmv)
    loss = loss_sum[0, 0]
    rs_finish(0, "w_in", scw_full)
    for nme in small_names:
        outs[nme] = small_out[nme]
    outs["final_g"] = tuple(a.reshape(d) for a in outs["final_g"])

    n_scw, n_fcw = ssd_conv_w.shape[2], ff_conv_w.shape[2]
    g_scw_mine = lax.dynamic_slice_in_dim(scw_full, me * n_scw, n_scw, axis=2)
    g_fcw_mine = lax.dynamic_slice_in_dim(fcw_full, me * n_fcw, n_fcw, axis=2)
    outs["ssd_conv_w"] = _adamw_sharded([(g_scw_mine, lambda p: 0)], ssd_conv_w, m_ssd_conv_w, v_ssd_conv_w, me_arr, "adamw_ssd_conv_w")
    outs["ff_conv_w"] = _adamw_sharded([(g_fcw_mine, lambda p: 0)], ff_conv_w, m_ff_conv_w, v_ff_conv_w, me_arr, "adamw_ff_conv_w")

    dmod_cols = _b(lax.dynamic_slice_in_dim(dmod_all, me * n_ada, n_ada, axis=2))
    g_ada = jnp.stack([_matmul(c_act, dmod_cols[l], ta=True, name=f"mm_ada_dw_{l}") for l in range(DEPTH)])
    outs["ada_w"] = _adamw_sharded([(g_ada, lambda p: 0)], ada_w, m_ada_w, v_ada_w, me_arr, "adamw_ada_w")

    for nme in _TRANSPOSED:
        outs[nme] = tuple(jnp.transpose(a, (0, 2, 1)) for a in outs[nme])
    result = [loss, grad_x.reshape(nseq, seq, d)]
    for k in range(4):
        result += [outs[n][k] for n in _WEIGHTS]
    return tuple(result)
```

```python
import functools
import math

import jax
import jax.numpy as jnp
from jax import lax
from jax.experimental import pallas as pl
from jax.experimental.pallas import tpu as pltpu

F32 = jnp.float32
BF16 = jnp.bfloat16

N_DEV = 8
D_MODEL = 1024
DEPTH = 2
CHUNK = 128
SSD_HEADS = 16
SSD_HEAD_DIM = 64
SSD_GROUPS = 2
HEADS_PER_GROUP = SSD_HEADS // SSD_GROUPS
GROUP_WIDTH = HEADS_PER_GROUP * SSD_HEAD_DIM
D_STATE = 128
D_SSD = 1024
CONV_DIM = 1536
SSD_CONV = 4
GM_HEADS = 8
GM_HEAD_DIM = 128
D_GM = 1024
D_FF = 2816
FF_CONV = 3
N_IN = 4624
N_MOD = 6
EPS = 1e-6

N_INP = 5120
COL_U, COL_V, COL_Z, COL_XBC, COL_DT = 0, 1024, 2048, 3072, 4608

ADAM_LR = 0.001
ADAM_B1 = 0.9
ADAM_B2 = 0.999
ADAM_EPS = 1e-08
ADAM_WD = 0.01
ADAM_STEP = 10

VMEM_LIMIT = 56 * 1024 * 1024
MESH = pl.DeviceIdType.MESH
ANY = pl.BlockSpec(memory_space=pl.ANY)


def _cp(*sem):
    return pltpu.CompilerParams(dimension_semantics=sem, vmem_limit_bytes=VMEM_LIMIT)


def _tile(n, pref):
    if n <= pref or n % 128:
        return n
    best = 128
    for t in range(128, pref + 1, 128):
        if n % t == 0:
            best = t
    return best


def _silu(x):
    return x * jax.nn.sigmoid(x)


def _gelu(x):
    return 0.5 * x * (1.0 + lax.erf(x * (1.0 / math.sqrt(2.0))))


def _softplus(x):
    return jnp.maximum(x, 0.0) + jnp.log1p(jnp.exp(-jnp.abs(x)))


def _b(x):
    return x.astype(BF16)


_NN = (((1,), (0,)), ((), ()))
_NT = (((1,), (1,)), ((), ()))
_TN = (((0,), (0,)), ((), ()))


def _dg(a, b, dn):
    return lax.dot_general(_b(a), _b(b), dn, preferred_element_type=F32)


@jax.custom_vjp
def _bdot(a, b):
    return _dg(a, b, _NN)


def _bdot_fwd(a, b):
    return _dg(a, b, _NN), (a, b)


def _bdot_bwd(res, ct):
    a, b = res
    return _dg(ct, b, _NT), _dg(a, ct, _TN)


_bdot.defvjp(_bdot_fwd, _bdot_bwd)


@jax.custom_vjp
def _bdot_nt(a, b):
    return _dg(a, b, _NT)


def _bdot_nt_fwd(a, b):
    return _dg(a, b, _NT), (a, b)


def _bdot_nt_bwd(res, ct):
    a, b = res
    return _dg(ct, b, _NN), _dg(ct, a, _TN)


_bdot_nt.defvjp(_bdot_nt_fwd, _bdot_nt_bwd)


@jax.custom_vjp
def _bdot_tn(a, b):
    return _dg(a, b, _TN)


def _bdot_tn_fwd(a, b):
    return _dg(a, b, _TN), (a, b)


def _bdot_tn_bwd(res, ct):
    a, b = res
    return _dg(b, ct, _NT), _dg(a, ct, _NN)


_bdot_tn.defvjp(_bdot_tn_fwd, _bdot_tn_bwd)


def _tri(n, lower):
    r = lax.broadcasted_iota(jnp.int32, (n, n), 0)
    c = lax.broadcasted_iota(jnp.int32, (n, n), 1)
    return ((r >= c) if lower else (r <= c)).astype(F32)


def _eye(n):
    r = lax.broadcasted_iota(jnp.int32, (n, n), 0)
    c = lax.broadcasted_iota(jnp.int32, (n, n), 1)
    return (r == c).astype(F32)


def _hdot(a, b, dn):
    return lax.dot_general(a, b, dn, precision=lax.Precision.HIGHEST, preferred_element_type=F32)


@jax.custom_vjp
def _cumsum_rows(x):
    return _hdot(_tri(x.shape[0], True), x, _NN)


def _cumsum_rows_fwd(x):
    return _cumsum_rows(x), None


def _cumsum_rows_bwd(_, ct):
    return (_hdot(_tri(ct.shape[0], False), ct, _NN),)


_cumsum_rows.defvjp(_cumsum_rows_fwd, _cumsum_rows_bwd)


@jax.custom_vjp
def _transpose(x):
    return _hdot(_eye(x.shape[1]), x, _NT)


def _transpose_fwd(x):
    return _transpose(x), None


def _transpose_bwd(_, ct):
    return (_hdot(_eye(ct.shape[1]), ct, _NT),)


_transpose.defvjp(_transpose_fwd, _transpose_bwd)


MXU_WIDTH = 256
MATMUL_TILE_CAP = 2816
MATMUL_VMEM = 44 * 1024 * 1024


def _mxu_tiles(n):
    if n <= MATMUL_TILE_CAP or n % 128:
        return [n]
    for unit in (MXU_WIDTH, 128):
        opts = [t for t in range(unit, MATMUL_TILE_CAP + 1, unit) if n % t == 0]
        if opts:
            return opts
    return [n]


def _matmul(a, b, *, ta=False, tb=False, name, dep=None, out_dtype=F32):
    pieces = list(a) if isinstance(a, (list, tuple)) else [a]
    npc = len(pieces)
    rows, width = pieces[0].shape
    assert all(p.shape == (rows, width) for p in pieces)
    if ta:
        k_dim, m_dim = rows, width * npc
    else:
        m_dim, k_dim = rows, width * npc
    if tb:
        n_dim, kb = b.shape
    else:
        kb, n_dim = b.shape
    assert kb == k_dim, (pieces[0].shape, npc, b.shape, ta, tb)
    m_unit = width if npc > 1 and ta else m_dim
    k_unit = width if npc > 1 and not ta else k_dim
    tm = _tile(m_unit, 1536)
    tn_opts, tk_opts = _mxu_tiles(n_dim), _mxu_tiles(k_unit)
    tn, tk = tn_opts.pop(), tk_opts.pop()
    while 4 * (tm * tk + tk * tn) + 8 * tm * tn > MATMUL_VMEM:
        if tn >= tk and tn_opts:
            tn = tn_opts.pop()
        else:
            tk = tk_opts.pop()
    ni, nj, nk = m_dim // tm, n_dim // tn, k_dim // tk
    per = width // (tm if ta else tk)
    dn = (((0 if ta else 1,), (1 if tb else 0,)), ((), ()))

    a_bytes, b_bytes = m_dim * k_dim, k_dim * n_dim
    m_outer = nk > 1 or a_bytes + b_bytes * ni <= b_bytes + a_bytes * nj
    if m_outer:
        ij = lambda o, n, k: (o, n)
        grid = (ni, nj, nk)
    else:
        ij = lambda o, n, k: (n, o)
        grid = (nj, ni, nk)

    use_acc = nk > 1 and out_dtype != F32

    def body(*refs):
        a_refs, b_ref = refs[:npc], refs[npc]
        o_ref = refs[-2] if use_acc else refs[-1]
        acc_ref = refs[-1]
        k = pl.program_id(2)
        i = pl.program_id(0 if m_outer else 1)
        along = i if ta else k

        def step(a_ref):
            p = lax.dot_general(a_ref[...], b_ref[...], dn, preferred_element_type=F32)
            if nk == 1:
                o_ref[...] = p.astype(out_dtype)
            else:
                @pl.when(k == 0)
                def _():
                    acc_ref[...] = p

                @pl.when((k > 0) & (k < nk - 1 if use_acc else True))
                def _():
                    acc_ref[...] += p

                if use_acc:
                    @pl.when(k == nk - 1)
                    def _():
                        o_ref[...] = (acc_ref[...] + p).astype(out_dtype)

        if npc == 1:
            step(a_refs[0])
        else:
            for pc in range(npc):
                pl.when((along >= pc * per) & (along < (pc + 1) * per))(functools.partial(step, a_refs[pc]))

    def a_map(pc, o, n, k):
        i, _ = ij(o, n, k)
        along = i if ta else k
        if npc > 1:
            along = jnp.clip(along - pc * per, 0, per - 1)
        return (k, along) if ta else (i, along)

    def b_map(o, n, k):
        _, j = ij(o, n, k)
        return (j, k) if tb else (k, j)

    extra = [] if dep is None else [dep]
    return pl.pallas_call(
        body, name=name,
        grid=grid,
        in_specs=[pl.BlockSpec((tk, tm) if ta else (tm, tk), functools.partial(a_map, pc)) for pc in range(npc)]
        + [pl.BlockSpec((tn, tk) if tb else (tk, tn), b_map)] + [ANY] * len(extra),
        out_specs=pl.BlockSpec((tm, tn), lambda o, n, k: ij(o, n, k)),
        out_shape=jax.ShapeDtypeStruct((m_dim, n_dim), out_dtype),
        scratch_shapes=[pltpu.VMEM((tm, tn), F32)] if use_acc else [],
        compiler_params=_cp("parallel", "parallel", "arbitrary"),
    )(*pieces, b, *extra)


def _ada_fwd(c_all, ada_w, ada_b_shard):
    depth, d, n = ada_w.shape
    nb = c_all.shape[0]

    def body(c_ref, w_ref, b_ref, o_ref, ca_ref):
        ca = _silu(c_ref[...])
        ca_ref[...] = _b(ca)
        o_ref[0] = _dg(ca, w_ref[0], _NN) + b_ref[0]

    return pl.pallas_call(
        body, name="ada_fwd",
        grid=(depth,),
        in_specs=[pl.BlockSpec((nb, d), lambda l: (0, 0)),
                  pl.BlockSpec((1, d, n), lambda l: (l, 0, 0)),
                  pl.BlockSpec((1, 1, n), lambda l: (l, 0, 0))],
        out_specs=[pl.BlockSpec((1, nb, n), lambda l: (l, 0, 0)),
                   pl.BlockSpec((nb, d), lambda l: (0, 0))],
        out_shape=[jax.ShapeDtypeStruct((depth, nb, n), F32), jax.ShapeDtypeStruct((nb, d), BF16)],
        compiler_params=_cp("arbitrary"),
    )(c_all, ada_w, ada_b_shard)


def _fold(acc):
    return jnp.sum(acc, axis=0, keepdims=True)


def _rinv(x):
    return lax.rsqrt(jnp.sum(x * x, axis=-1, keepdims=True) * (1.0 / D_MODEL) + EPS)


def _rms_bwd(a, xhat, rinv):
    return rinv * (a - xhat * (jnp.sum(a * xhat, axis=-1, keepdims=True) * (1.0 / D_MODEL)))


def _row_tile(seq):
    return min(seq, 256)


def _normmod_fwd(x, g, sc, sh, *, nseq, name):
    t, d = x.shape
    seq = t // nseq
    tr = _row_tile(seq)
    nt = seq // tr
    row = pl.BlockSpec((tr, d), lambda s, i: (s * nt + i, 0))
    per_seq = pl.BlockSpec((1, 1, d), lambda s, i: (s, 0, 0))

    def body(x_ref, g_ref, sc_ref, sh_ref, h_ref):
        x_v = x_ref[...]
        h_ref[...] = _b(x_v * _rinv(x_v) * (g_ref[...] * (1.0 + sc_ref[0])) + sh_ref[0])

    return pl.pallas_call(
        body, name=name, grid=(nseq, nt),
        in_specs=[row, pl.BlockSpec((1, d), lambda s, i: (0, 0)), per_seq, per_seq],
        out_specs=row,
        out_shape=jax.ShapeDtypeStruct((t, d), BF16),
        compiler_params=_cp("parallel", "parallel"),
    )(x, g, sc, sh)


NORM_TM = 512


def _matmul_normbwd(a, b, dxo, x, delta, gate, g, sc, *, nseq, name, dep=None):
    pieces = list(a) if isinstance(a, (list, tuple)) else [a]
    npc = len(pieces)
    t, width = pieces[0].shape
    k_dim, d = width * npc, b.shape[1]
    assert b.shape[0] == k_dim and all(p.shape == (t, width) for p in pieces)
    seq = t // nseq
    tm = min(NORM_TM, seq)
    per_seq_tiles = seq // tm
    tk = _mxu_tiles(width if npc > 1 else k_dim).pop()
    nk, per = k_dim // tk, width // tk
    has_delta = delta is not None
    extra = [] if dep is None else [dep]

    def body(*refs):
        a_refs, b_ref = refs[:npc], refs[npc]
        dxo_ref, x_ref = refs[npc + 1], refs[npc + 2]
        pos = npc + 3
        if has_delta:
            delta_ref, gate_ref = refs[pos], refs[pos + 1]
            pos += 2
        g_ref, sc_ref = refs[pos], refs[pos + 1]
        pos += 2 + len(extra)
        if has_delta:
            dx_ref, dd_ref, dgate_ref, dg_ref, dsc_ref, dsh_ref = refs[pos:pos + 6]
        else:
            dx_ref, dg_ref, dsc_ref, dsh_ref = refs[pos:pos + 4]
        acc_ref = refs[-1]
        i, k = pl.program_id(0), pl.program_id(1)

        def norm_bwd(dh_v):
            g_v, one_sc = g_ref[...], 1.0 + sc_ref[0]
            x_v = x_ref[...]
            rinv = _rinv(x_v)
            xhat = x_v * rinv
            dx = dxo_ref[...] + _rms_bwd(dh_v * (g_v * one_sc), xhat, rinv)
            dx_ref[...] = dx

            @pl.when(i == 0)
            def _():
                dg_ref[...] = jnp.zeros_like(dg_ref)

            @pl.when(i % per_seq_tiles == 0)
            def _():
                dsc_ref[...] = jnp.zeros_like(dsc_ref)
                dsh_ref[...] = jnp.zeros_like(dsh_ref)
                if has_delta:
                    dgate_ref[...] = jnp.zeros_like(dgate_ref)

            t_sum = _fold(dh_v * xhat)
            dg_ref[...] += t_sum * one_sc
            dsc_ref[0] += t_sum * g_v
            dsh_ref[0] += _fold(dh_v)
            if has_delta:
                dd_ref[...] = _b(dx * gate_ref[0])
                dgate_ref[0] += _fold(dx * delta_ref[...])

        def step(a_ref):
            p = lax.dot_general(a_ref[...], b_ref[...], _NN, preferred_element_type=F32)
            if nk == 1:
                norm_bwd(p)
            else:
                @pl.when(k == 0)
                def _():
                    acc_ref[...] = p

                @pl.when((k > 0) & (k < nk - 1))
                def _():
                    acc_ref[...] += p

                @pl.when(k == nk - 1)
                def _():
                    norm_bwd(acc_ref[...] + p)

        if npc == 1:
            step(a_refs[0])
        else:
            for pc in range(npc):
                pl.when((k >= pc * per) & (k < (pc + 1) * per))(functools.partial(step, a_refs[pc]))

    def a_map(pc, i, k):
        return (i, jnp.clip(k - pc * per, 0, per - 1) if npc > 1 else k)

    row = pl.BlockSpec((tm, d), lambda i, k: (i, 0))
    per_seq = pl.BlockSpec((1, 1, d), lambda i, k: (i // per_seq_tiles, 0, 0))
    vec = pl.BlockSpec((1, d), lambda i, k: (0, 0))
    shp = lambda *s, dt=F32: jax.ShapeDtypeStruct(s, dt)
    in_specs = [pl.BlockSpec((tm, tk), functools.partial(a_map, pc)) for pc in range(npc)]
    in_specs += [pl.BlockSpec((tk, d), lambda i, k: (k, 0)), row, row]
    operands = [*pieces, b, dxo, x]
    if has_delta:
        in_specs += [row, per_seq]
        operands += [delta, gate]
    in_specs += [vec, per_seq] + [ANY] * len(extra)
    operands += [g, sc, *extra]
    if has_delta:
        out_specs = [row, row, per_seq, vec, per_seq, per_seq]
        out_shape = [shp(t, d), shp(t, d, dt=BF16), shp(nseq, 1, d), shp(1, d), shp(nseq, 1, d), shp(nseq, 1, d)]
    else:
        out_specs = [row, vec, per_seq, per_seq]
        out_shape = [shp(t, d), shp(1, d), shp(nseq, 1, d), shp(nseq, 1, d)]
    outs = pl.pallas_call(
        body, name=name, grid=(t // tm, nk),
        in_specs=in_specs, out_specs=out_specs, out_shape=out_shape,
        scratch_shapes=[pltpu.VMEM((tm, d), F32)],
        compiler_params=_cp("arbitrary", "arbitrary"),
    )(*operands)
    if has_delta:
        return tuple(outs)
    dx, dg, dsc, dsh = outs
    return dx, None, None, dg, dsc, dsh


def _matmul_normfwd(a, b, xin, gate, g, sc, sh, *, nseq, name):
    t, k_dim = a.shape
    d = b.shape[1]
    assert b.shape[0] == k_dim and k_dim <= MATMUL_TILE_CAP
    seq = t // nseq
    tm = min(NORM_TM, seq)
    per_seq_tiles = seq // tm

    def body(a_ref, b_ref, xin_ref, gate_ref, g_ref, sc_ref, sh_ref, dl_ref, x_ref, h_ref):
        dl = lax.dot_general(a_ref[...], b_ref[...], _NN, preferred_element_type=F32)
        dl_ref[...] = dl
        x = xin_ref[...] + gate_ref[0] * dl
        x_ref[...] = x
        h_ref[...] = _b(x * _rinv(x) * (g_ref[...] * (1.0 + sc_ref[0])) + sh_ref[0])

    row = pl.BlockSpec((tm, d), lambda i: (i, 0))
    per_seq = pl.BlockSpec((1, 1, d), lambda i: (i // per_seq_tiles, 0, 0))
    return pl.pallas_call(
        body, name=name, grid=(t // tm,),
        in_specs=[pl.BlockSpec((tm, k_dim), lambda i: (i, 0)), pl.BlockSpec((k_dim, d), lambda i: (0, 0)),
                  row, per_seq, pl.BlockSpec((1, d), lambda i: (0, 0)), per_seq, per_seq],
        out_specs=[row, row, row],
        out_shape=[jax.ShapeDtypeStruct((t, d), F32), jax.ShapeDtypeStruct((t, d), F32), jax.ShapeDtypeStruct((t, d), BF16)],
        compiler_params=_cp("parallel"),
    )(a, b, xin, gate, g, sc, sh)


def _matmul_loss(a, b, xin, gate, fg, target, *, nseq, name):
    t, k_dim = a.shape
    d = b.shape[1]
    assert b.shape[0] == k_dim and k_dim <= MATMUL_TILE_CAP
    seq = t // nseq
    tm = min(NORM_TM, seq)
    per_seq_tiles = seq // tm

    def body(a_ref, b_ref, xin_ref, gate_ref, fg_ref, tgt_ref, dl_ref, loss_ref, dx_ref, dd_ref, dgate_ref, dfg_ref):
        i = pl.program_id(0)
        fg_v, gate_v = fg_ref[...], gate_ref[0]
        dl = lax.dot_general(a_ref[...], b_ref[...], _NN, preferred_element_type=F32)
        dl_ref[...] = dl
        x = xin_ref[...] + gate_v * dl
        rinv = _rinv(x)
        xhat = x * rinv
        err = xhat * fg_v - tgt_ref[...]
        dx = _rms_bwd(err * fg_v * (1.0 / d), xhat, rinv)
        dx_ref[...] = dx
        dd_ref[...] = _b(dx * gate_v)

        @pl.when(i == 0)
        def _():
            loss_ref[...] = jnp.zeros_like(loss_ref)
            dfg_ref[...] = jnp.zeros_like(dfg_ref)

        @pl.when(i % per_seq_tiles == 0)
        def _():
            dgate_ref[...] = jnp.zeros_like(dgate_ref)

        loss_ref[...] += jnp.sum(err * err) * (0.5 / d)
        dfg_ref[...] += _fold(err * xhat) * (1.0 / d)
        dgate_ref[0] += _fold(dx * dl)

    row = pl.BlockSpec((tm, d), lambda i: (i, 0))
    per_seq = pl.BlockSpec((1, 1, d), lambda i: (i // per_seq_tiles, 0, 0))
    vec = pl.BlockSpec((1, d), lambda i: (0, 0))
    return pl.pallas_call(
        body, name=name, grid=(t // tm,),
        in_specs=[pl.BlockSpec((tm, k_dim), lambda i: (i, 0)), pl.BlockSpec((k_dim, d), lambda i: (0, 0)),
                  row, per_seq, vec, row],
        out_specs=[row, pl.BlockSpec((1, 128), lambda i: (0, 0)), row, row, per_seq, vec],
        out_shape=[jax.ShapeDtypeStruct((t, d), F32), jax.ShapeDtypeStruct((1, 128), F32), jax.ShapeDtypeStruct((t, d), F32),
                   jax.ShapeDtypeStruct((t, d), BF16), jax.ShapeDtypeStruct((nseq, 1, d), F32),
                   jax.ShapeDtypeStruct((1, d), F32)],
        compiler_params=_cp("arbitrary"),
    )(a, b, xin, gate, fg, target)


CONV_TC = 256
CONV_LANES = 128
CONV_ROWS = 64
CONV_HALO = 8


def _conv_slabs(seq, fn):
    def step(i, carry):
        r0 = pl.multiple_of(i * CONV_ROWS, CONV_ROWS)
        for h in range(CONV_TC // CONV_LANES):
            fn(r0, slice(h * CONV_LANES, (h + 1) * CONV_LANES))
        return carry

    lax.fori_loop(0, seq // CONV_ROWS, step, 0)


def _slab(ref, r0, cols, seq):
    after = ref[pl.ds(pl.multiple_of(jnp.minimum(r0 + CONV_ROWS, seq - CONV_HALO), CONV_HALO), CONV_HALO), cols]
    return jnp.concatenate([ref[pl.ds(r0, CONV_ROWS), cols], jnp.where(r0 + CONV_ROWS < seq, after, 0.0)], axis=0)


def _conv_block(x, w_ref, b_ref):
    kw = w_ref.shape[0]
    rows = lax.broadcasted_iota(jnp.int32, x.shape, 0)
    y = b_ref[...] + w_ref[kw - 1:kw, :] * x
    for j in range(1, kw):
        y = y + w_ref[kw - 1 - j:kw - j, :] * jnp.where(rows >= j, pltpu.roll(x, j, 0), 0.0)
    return y


def _conv_block_bwd(dy, x, w_ref, dw_ref, db_ref):
    kw = w_ref.shape[0]
    n = x.shape[0]
    rows = lax.broadcasted_iota(jnp.int32, x.shape, 0)
    dx = w_ref[kw - 1:kw, :] * dy
    dw_ref[kw - 1:kw, :] += jnp.sum(dy * x, axis=0, keepdims=True)
    for j in range(1, kw):
        dy_j = jnp.where(rows < n - j, pltpu.roll(dy, n - j, 0), 0.0)
        dx = dx + w_ref[kw - 1 - j:kw - j, :] * dy_j
        dw_ref[kw - 1 - j:kw - j, :] += jnp.sum(dy_j * x, axis=0, keepdims=True)
    db_ref[...] += jnp.sum(dy, axis=0, keepdims=True)
    return dx


def _conv_bwd(dy_ext, x, w_ref, dw_ref, db_ref, cols):
    kw = w_ref.shape[0]
    n = dy_ext.shape[0]
    dy = dy_ext[:CONV_ROWS]
    dx = w_ref[kw - 1:kw, cols] * dy
    dw_ref[kw - 1:kw, cols] += jnp.sum(dy * x, axis=0, keepdims=True)
    for j in range(1, kw):
        dy_j = pltpu.roll(dy_ext, n - j, 0)[:CONV_ROWS]
        dx = dx + w_ref[kw - 1 - j:kw - j, cols] * dy_j
        dw_ref[kw - 1 - j:kw - j, cols] += jnp.sum(dy_j * x, axis=0, keepdims=True)
    db_ref[:, cols] += jnp.sum(dy, axis=0, keepdims=True)
    return dx


def _dsilu(pre):
    sg = jax.nn.sigmoid(pre)
    return pre * sg, sg * (1.0 + pre * (1.0 - sg))


def _ssd_conv_fwd(proj, w, b, *, nseq):
    t = proj.shape[0]
    seq = t // nseq
    nb = CONV_DIM // CONV_TC
    off = COL_XBC // CONV_TC

    def body(x_ref, w_ref, b_ref, o_ref, pre_ref):
        pre = _conv_block(x_ref[...], w_ref, b_ref)
        pre_ref[...] = pre
        o_ref[...] = _silu(pre)

    col = pl.BlockSpec((seq, CONV_TC), lambda j, s: (s, j))
    return pl.pallas_call(
        body, name="ssd_conv_fwd", grid=(nb, nseq),
        in_specs=[pl.BlockSpec((seq, CONV_TC), lambda j, s: (s, off + j)),
                  pl.BlockSpec((SSD_CONV, CONV_TC), lambda j, s: (0, j)),
                  pl.BlockSpec((1, CONV_TC), lambda j, s: (0, j))],
        out_specs=[col, col],
        out_shape=[jax.ShapeDtypeStruct((t, CONV_DIM), F32)] * 2,
        compiler_params=_cp("parallel", "parallel"),
    )(proj, w, b)


def _ssd_conv_bwd(dact, pre, proj, w, dproj, *, nseq):
    t = proj.shape[0]
    seq = t // nseq
    nb = CONV_DIM // CONV_TC
    off = COL_XBC // CONV_TC

    def body(da_ref, pre_ref, x_ref, w_ref, dproj_ref, dx_ref, dw_ref, db_ref):
        del dproj_ref

        @pl.when(pl.program_id(1) == 0)
        def _():
            dw_ref[...] = jnp.zeros_like(dw_ref)
            db_ref[...] = jnp.zeros_like(db_ref)

        def slab(r0, cols):
            _, dsilu = _dsilu(_slab(pre_ref, r0, cols, seq))
            dpre_ext = _slab(da_ref, r0, cols, seq) * dsilu
            x = x_ref[pl.ds(r0, CONV_ROWS), cols]
            dx_ref[pl.ds(r0, CONV_ROWS), cols] = _b(_conv_bwd(dpre_ext, x, w_ref, dw_ref, db_ref, cols))

        _conv_slabs(seq, slab)

    return pl.pallas_call(
        body, name="ssd_conv_bwd", grid=(nb, nseq),
        in_specs=[pl.BlockSpec((seq, CONV_TC), lambda j, s: (s, j)),
                  pl.BlockSpec((seq, CONV_TC), lambda j, s: (s, j)),
                  pl.BlockSpec((seq, CONV_TC), lambda j, s: (s, off + j)),
                  pl.BlockSpec((SSD_CONV, CONV_TC), lambda j, s: (0, j)),
                  ANY],
        out_specs=[pl.BlockSpec((seq, CONV_TC), lambda j, s: (s, off + j)),
                   pl.BlockSpec((SSD_CONV, CONV_TC), lambda j, s: (0, j)),
                   pl.BlockSpec((1, CONV_TC), lambda j, s: (0, j))],
        out_shape=[jax.ShapeDtypeStruct(dproj.shape, dproj.dtype), jax.ShapeDtypeStruct((SSD_CONV, CONV_DIM), F32),
                   jax.ShapeDtypeStruct((1, CONV_DIM), F32)],
        input_output_aliases={4: 0},
        compiler_params=_cp("parallel", "arbitrary"),
    )(dact, pre, proj, w, dproj)


def _ffn_act_fwd(up, w, b, *, nseq):
    t = up.shape[0]
    seq = t // nseq
    nb = D_FF // CONV_TC

    def body(g_ref, v_ref, w_ref, b_ref, o_ref):
        o_ref[...] = _b(_silu(_conv_block(g_ref[...].astype(F32), w_ref, b_ref)) * v_ref[...].astype(F32))

    col = pl.BlockSpec((seq, CONV_TC), lambda j, s: (s, j))
    return pl.pallas_call(
        body, name="ffn_act_fwd", grid=(nb, nseq),
        in_specs=[col,
                  pl.BlockSpec((seq, CONV_TC), lambda j, s: (s, nb + j)),
                  pl.BlockSpec((FF_CONV, CONV_TC), lambda j, s: (0, j)),
                  pl.BlockSpec((1, CONV_TC), lambda j, s: (0, j))],
        out_specs=col,
        out_shape=jax.ShapeDtypeStruct((t, D_FF), BF16),
        compiler_params=_cp("parallel", "parallel"),
    )(up, up, w, b)


def _ffn_act_bwd(dact, up, w, b, *, nseq):
    t = up.shape[0]
    seq = t // nseq
    nb = D_FF // CONV_TC

    def body(da_ref, g_ref, v_ref, w_ref, b_ref, dg_ref, dv_ref, dw_ref, db_ref):
        @pl.when(pl.program_id(1) == 0)
        def _():
            dw_ref[...] = jnp.zeros_like(dw_ref)
            db_ref[...] = jnp.zeros_like(db_ref)

        gate = g_ref[...].astype(F32)
        silu, dsilu = _dsilu(_conv_block(gate, w_ref, b_ref))
        da = da_ref[...].astype(F32)
        dv_ref[...] = _b(da * silu)
        dg_ref[...] = _b(_conv_block_bwd(da * v_ref[...].astype(F32) * dsilu, gate, w_ref, dw_ref, db_ref))

    col = pl.BlockSpec((seq, CONV_TC), lambda j, s: (s, j))
    return pl.pallas_call(
        body, name="ffn_act_bwd", grid=(nb, nseq),
        in_specs=[col, col,
                  pl.BlockSpec((seq, CONV_TC), lambda j, s: (s, nb + j)),
                  pl.BlockSpec((FF_CONV, CONV_TC), lambda j, s: (0, j)),
                  pl.BlockSpec((1, CONV_TC), lambda j, s: (0, j))],
        out_specs=[col, col,
                   pl.BlockSpec((FF_CONV, CONV_TC), lambda j, s: (0, j)),
                   pl.BlockSpec((1, CONV_TC), lambda j, s: (0, j))],
        out_shape=[jax.ShapeDtypeStruct((t, D_FF), BF16), jax.ShapeDtypeStruct((t, D_FF), BF16),
                   jax.ShapeDtypeStruct((FF_CONV, D_FF), F32), jax.ShapeDtypeStruct((1, D_FF), F32)],
        compiler_params=_cp("parallel", "arbitrary"),
    )(dact, up, up, w, b)


SSD_PAIRS = SSD_HEADS // 2
PAIR_W = 2 * SSD_HEAD_DIM
PAIRS_PER_GROUP = SSD_PAIRS // SSD_GROUPS


def _ssd_chunk(xs, bg, cg, dtr, z, hp, dtb, alog, dskip, ng):
    n = dtr.shape[0]
    dt = _softplus(dtr + dtb)
    cs = _cumsum_rows(dt * (-jnp.exp(alog)))
    cs_t = _transpose(cs)
    lane = lax.broadcasted_iota(jnp.int32, (1, SSD_HEADS), 1)
    sub = lax.broadcasted_iota(jnp.int32, (SSD_HEADS, 1), 0)
    row = lax.broadcasted_iota(jnp.int32, (n, 1), 0)
    causal = lax.broadcasted_iota(jnp.int32, (n, n), 0) >= lax.broadcasted_iota(jnp.int32, (n, n), 1)
    future = jnp.where(causal, 0.0, -1e30)
    first = lax.broadcasted_iota(jnp.int32, (1, PAIR_W), 1) < SSD_HEAD_DIM
    first_rows = lax.broadcasted_iota(jnp.int32, (PAIR_W, 1), 0) < SSD_HEAD_DIM
    first_f = first.astype(F32)
    cb = [_bdot_nt(cg[g], bg[g]) for g in range(SSD_GROUPS)]
    ys, hn = [], []
    for p in range(SSD_PAIRS):
        g = p // PAIRS_PER_GROUP
        col, decay, last = [], [], []
        for h in (2 * p, 2 * p + 1):
            oh = (lane == h).astype(F32)
            cs_h = jnp.sum(cs * oh, axis=1, keepdims=True)
            cs_row = jnp.sum(cs_t * (sub == h).astype(F32), axis=0, keepdims=True)
            col.append((jnp.sum(dt * oh, axis=1, keepdims=True), cs_h, jnp.sum(dskip * oh, axis=1, keepdims=True)))
            last.append(jnp.sum(jnp.where(row == n - 1, cs_h, 0.0), axis=0, keepdims=True))
            decay.append(jnp.exp(cs_h - cs_row + future))
        pair = lambda a, b: jnp.where(first, a, b)
        dt_p = pair(col[0][0], col[1][0])
        cs_p = pair(col[0][1], col[1][1])
        last_p = pair(last[0], last[1])
        xc = xs[p] * dt_p
        y = _bdot(cb[g] * decay[0], xc * first_f) + _bdot(cb[g] * decay[1], xc * (1.0 - first_f))
        y = y + _bdot_nt(cg[g], hp[p]) * jnp.exp(cs_p)
        y = y + pair(col[0][2], col[1][2]) * xs[p]
        keep = jnp.where(first_rows, jnp.exp(last[0]), jnp.exp(last[1]))
        hn.append(keep * hp[p] + _bdot_tn(xc * jnp.exp(last_p - cs_p), bg[g]))
        ys.append(y * _silu(z[p]))
    outs = []
    for g in range(SSD_GROUPS):
        ps = range(g * PAIRS_PER_GROUP, (g + 1) * PAIRS_PER_GROUP)
        ms = sum(jnp.sum(ys[p] * ys[p], axis=1, keepdims=True) for p in ps) * (1.0 / GROUP_WIDTH)
        r = lax.rsqrt(ms + EPS)
        outs += [ys[p] * r * ng[p] for p in ps]
    return outs, hn


def _hslices(ref, width, count, base=0):
    return [ref[:, base + k * width: base + (k + 1) * width] for k in range(count)]


def _ssd_load(xbc_ref, z_ref, dt_ref, ng_ref):
    xs = _hslices(xbc_ref, PAIR_W, SSD_PAIRS)
    bg = _hslices(xbc_ref, D_STATE, SSD_GROUPS, D_SSD)
    cg = _hslices(xbc_ref, D_STATE, SSD_GROUPS, D_SSD + SSD_GROUPS * D_STATE)
    z = _hslices(z_ref, PAIR_W, SSD_PAIRS)
    ng = _hslices(ng_ref, PAIR_W, SSD_PAIRS)
    return xs, bg, cg, dt_ref[:, 0:SSD_HEADS], z, ng


def _ssd_specs(nch):
    rowi = lambda s, c: s * nch + c
    return [pl.BlockSpec((CHUNK, CONV_DIM), lambda s, c: (rowi(s, c), 0)),
            pl.BlockSpec((CHUNK, D_SSD), lambda s, c: (rowi(s, c), COL_Z // D_SSD)),
            pl.BlockSpec((CHUNK, 128), lambda s, c: (rowi(s, c), COL_DT // 128)),
            pl.BlockSpec((1, SSD_HEADS), lambda s, c: (0, 0)),
            pl.BlockSpec((1, SSD_HEADS), lambda s, c: (0, 0)),
            pl.BlockSpec((1, SSD_HEADS), lambda s, c: (0, 0)),
            pl.BlockSpec((1, D_SSD), lambda s, c: (0, 0))]


def _ssd_fwd(xbc, proj, dtb, alog, dskip, ng, *, nseq):
    t = proj.shape[0]
    nch = t // nseq // CHUNK
    hd = PAIR_W

    def body(xbc_ref, z_ref, dt_ref, dtb_ref, alog_ref, dsk_ref, ng_ref, y_ref, hp_ref, h_ref):
        @pl.when(pl.program_id(1) == 0)
        def _():
            h_ref[...] = jnp.zeros_like(h_ref)

        xs, bg, cg, dtr, z, ngs = _ssd_load(xbc_ref, z_ref, dt_ref, ng_ref)
        hp_ref[0] = h_ref[...]
        hp = [h_ref[h * hd:(h + 1) * hd, :] for h in range(SSD_PAIRS)]
        outs, hn = _ssd_chunk(xs, bg, cg, dtr, z, hp, dtb_ref[...], alog_ref[...], dsk_ref[...], ngs)
        for h in range(SSD_PAIRS):
            y_ref[:, h * hd:(h + 1) * hd] = _b(outs[h])
            h_ref[h * hd:(h + 1) * hd, :] = hn[h]

    return pl.pallas_call(
        body, name="ssd_fwd", grid=(nseq, nch),
        in_specs=_ssd_specs(nch),
        out_specs=[pl.BlockSpec((CHUNK, D_SSD), lambda s, c: (s * nch + c, 0)),
                   pl.BlockSpec((1, D_SSD, D_STATE), lambda s, c: (s * nch + c, 0, 0))],
        out_shape=[jax.ShapeDtypeStruct((t, D_SSD + D_GM), BF16),
                   jax.ShapeDtypeStruct((t // CHUNK, D_SSD, D_STATE), F32)],
        scratch_shapes=[pltpu.VMEM((D_SSD, D_STATE), F32)],
        compiler_params=_cp("arbitrary", "arbitrary"),
    )(xbc, proj, proj, dtb, alog, dskip, ng)


def _ssd_bwd(dy, xbc, proj, hprev, dtb, alog, dskip, ng, *, nseq):
    t = proj.shape[0]
    nch = t // nseq // CHUNK
    hd = PAIR_W
    rev = lambda s, c: s * nch + (nch - 1 - c)

    def body(dy_ref, xbc_ref, z_ref, dt_ref, hp_ref, dtb_ref, alog_ref, dsk_ref, ng_ref,
             dxbc_ref, dproj_ref, ddtb_ref, dalog_ref, ddsk_ref, dng_ref, dh_ref):
        first = (pl.program_id(0) == 0) & (pl.program_id(1) == 0)

        @pl.when(pl.program_id(1) == 0)
        def _():
            dh_ref[...] = jnp.zeros_like(dh_ref)

        @pl.when(first)
        def _():
            ddtb_ref[...] = jnp.zeros_like(ddtb_ref)
            dalog_ref[...] = jnp.zeros_like(dalog_ref)
            ddsk_ref[...] = jnp.zeros_like(ddsk_ref)
            dng_ref[...] = jnp.zeros_like(dng_ref)

        xs, bg, cg, dtr, z, ngs = _ssd_load(xbc_ref, z_ref, dt_ref, ng_ref)
        hp = [hp_ref[0, h * hd:(h + 1) * hd, :] for h in range(SSD_PAIRS)]
        _, vjp = jax.vjp(_ssd_chunk, xs, bg, cg, dtr, z, hp, dtb_ref[...], alog_ref[...], dsk_ref[...], ngs)
        douts = [dy_ref[:, h * hd:(h + 1) * hd] for h in range(SSD_PAIRS)]
        dhn = [dh_ref[h * hd:(h + 1) * hd, :] for h in range(SSD_PAIRS)]
        dxs, dbg, dcg, ddtr, dz, dhp, ddtb, dalog, ddsk, dngs = vjp((douts, dhn))
        dproj_ref[:, :COL_Z] = jnp.zeros((CHUNK, COL_Z), BF16)
        dproj_ref[:, COL_XBC:] = jnp.zeros((CHUNK, N_INP - COL_XBC), BF16)
        for h in range(SSD_PAIRS):
            dxbc_ref[:, h * hd:(h + 1) * hd] = dxs[h]
            dproj_ref[:, COL_Z + h * hd: COL_Z + (h + 1) * hd] = _b(dz[h])
            dh_ref[h * hd:(h + 1) * hd, :] = dhp[h]
            dng_ref[:, h * hd:(h + 1) * hd] += dngs[h]
        for g in range(SSD_GROUPS):
            dxbc_ref[:, D_SSD + g * D_STATE: D_SSD + (g + 1) * D_STATE] = dbg[g]
            dxbc_ref[:, D_SSD + (SSD_GROUPS + g) * D_STATE: D_SSD + (SSD_GROUPS + g + 1) * D_STATE] = dcg[g]
        dproj_ref[:, COL_DT:COL_DT + SSD_HEADS] = _b(ddtr)
        ddtb_ref[...] += ddtb
        dalog_ref[...] += dalog
        ddsk_ref[...] += ddsk

    small = pl.BlockSpec((1, SSD_HEADS), lambda s, c: (0, 0))
    return pl.pallas_call(
        body, name="ssd_bwd", grid=(nseq, nch),
        in_specs=[pl.BlockSpec((CHUNK, D_SSD), lambda s, c: (rev(s, c), 0)),
                  pl.BlockSpec((CHUNK, CONV_DIM), lambda s, c: (rev(s, c), 0)),
                  pl.BlockSpec((CHUNK, D_SSD), lambda s, c: (rev(s, c), COL_Z // D_SSD)),
                  pl.BlockSpec((CHUNK, 128), lambda s, c: (rev(s, c), COL_DT // 128)),
                  pl.BlockSpec((1, D_SSD, D_STATE), lambda s, c: (rev(s, c), 0, 0)),
                  small, small, small,
                  pl.BlockSpec((1, D_SSD), lambda s, c: (0, 0))],
        out_specs=[pl.BlockSpec((CHUNK, CONV_DIM), lambda s, c: (rev(s, c), 0)),
                   pl.BlockSpec((CHUNK, N_INP), lambda s, c: (rev(s, c), 0)),
                   small, small, small,
                   pl.BlockSpec((1, D_SSD), lambda s, c: (0, 0))],
        out_shape=[jax.ShapeDtypeStruct((t, CONV_DIM), F32), jax.ShapeDtypeStruct((t, N_INP), BF16),
                   jax.ShapeDtypeStruct((1, SSD_HEADS), F32), jax.ShapeDtypeStruct((1, SSD_HEADS), F32),
                   jax.ShapeDtypeStruct((1, SSD_HEADS), F32), jax.ShapeDtypeStruct((1, D_SSD), F32)],
        scratch_shapes=[pltpu.VMEM((D_SSD, D_STATE), F32)],
        compiler_params=_cp("arbitrary", "arbitrary"),
    )(dy, xbc, proj, proj, hprev, dtb, alog, dskip, ng)


def _gmlp_chunk(gu, gv, ws, bs_cols, vg, og):
    n = gu[0].shape[0]
    mask = _tri(n, True)
    au = [_gelu(t) for t in gu]
    av = [_gelu(t) for t in gv]
    r = lax.rsqrt(sum(jnp.sum(t * t, axis=1, keepdims=True) for t in av) * (1.0 / D_GM) + EPS)
    p = []
    for h in range(GM_HEADS):
        sv = _bdot(ws[h] * mask, av[h] * r * vg[h]) + bs_cols[h]
        p.append(au[h] * sv)
    r2 = lax.rsqrt(sum(jnp.sum(t * t, axis=1, keepdims=True) for t in p) * (1.0 / D_GM) + EPS)
    return [p[h] * r2 * og[h] for h in range(GM_HEADS)]


def _gmlp_load(u_ref, v_ref, ws_ref, bst_ref, vg_ref, og_ref):
    gu = _hslices(u_ref, GM_HEAD_DIM, GM_HEADS)
    gv = _hslices(v_ref, GM_HEAD_DIM, GM_HEADS)
    ws = [ws_ref[h] for h in range(GM_HEADS)]
    bs_cols = [bst_ref[:, h:h + 1] for h in range(GM_HEADS)]
    return gu, gv, ws, bs_cols, _hslices(vg_ref, GM_HEAD_DIM, GM_HEADS), _hslices(og_ref, GM_HEAD_DIM, GM_HEADS)


def _gmlp_specs():
    return [pl.BlockSpec((CHUNK, D_GM), lambda i: (i, COL_U // D_GM)),
            pl.BlockSpec((CHUNK, D_GM), lambda i: (i, COL_V // D_GM)),
            pl.BlockSpec((GM_HEADS, CHUNK, CHUNK), lambda i: (0, 0, 0)),
            pl.BlockSpec((CHUNK, GM_HEADS), lambda i: (0, 0)),
            pl.BlockSpec((1, D_GM), lambda i: (0, 0)),
            pl.BlockSpec((1, D_GM), lambda i: (0, 0))]


def _gmlp_fwd(proj, ycat, ws, bst, vg, og):
    t = proj.shape[0]

    def body(u_ref, v_ref, ws_ref, bst_ref, vg_ref, og_ref, ycat_ref, o_ref):
        del ycat_ref
        outs = _gmlp_chunk(*_gmlp_load(u_ref, v_ref, ws_ref, bst_ref, vg_ref, og_ref))
        for h in range(GM_HEADS):
            o_ref[:, h * GM_HEAD_DIM:(h + 1) * GM_HEAD_DIM] = _b(outs[h])

    return pl.pallas_call(
        body, name="gmlp_fwd", grid=(t // CHUNK,),
        in_specs=_gmlp_specs() + [ANY],
        out_specs=pl.BlockSpec((CHUNK, D_GM), lambda i: (i, D_SSD // D_GM)),
        out_shape=jax.ShapeDtypeStruct(ycat.shape, ycat.dtype),
        input_output_aliases={6: 0},
        compiler_params=_cp("parallel"),
    )(proj, proj, ws, bst, vg, og, ycat)


def _gmlp_bwd(dy, proj, ws, bst, vg, og, dproj):
    t = proj.shape[0]
    w = GM_HEAD_DIM

    def body(dy_ref, u_ref, v_ref, ws_ref, bst_ref, vg_ref, og_ref, dproj_ref,
             dgm_ref, dws_ref, dbst_ref, dvg_ref, dog_ref):
        del dproj_ref

        @pl.when(pl.program_id(0) == 0)
        def _():
            dws_ref[...] = jnp.zeros_like(dws_ref)
            dbst_ref[...] = jnp.zeros_like(dbst_ref)
            dvg_ref[...] = jnp.zeros_like(dvg_ref)
            dog_ref[...] = jnp.zeros_like(dog_ref)

        _, vjp = jax.vjp(_gmlp_chunk, *_gmlp_load(u_ref, v_ref, ws_ref, bst_ref, vg_ref, og_ref))
        dgu, dgv, dws, dbs, dvg, dog = vjp(_hslices(dy_ref, w, GM_HEADS))
        for h in range(GM_HEADS):
            dgm_ref[:, h * w:(h + 1) * w] = _b(dgu[h])
            dgm_ref[:, D_GM + h * w: D_GM + (h + 1) * w] = _b(dgv[h])
            dws_ref[h] += dws[h]
            dbst_ref[:, h:h + 1] += dbs[h]
            dvg_ref[:, h * w:(h + 1) * w] += dvg[h]
            dog_ref[:, h * w:(h + 1) * w] += dog[h]

    return pl.pallas_call(
        body, name="gmlp_bwd", grid=(t // CHUNK,),
        in_specs=[pl.BlockSpec((CHUNK, D_GM), lambda i: (i, 1))] + _gmlp_specs() + [ANY],
        out_specs=[pl.BlockSpec((CHUNK, 2 * D_GM), lambda i: (i, COL_U // (2 * D_GM))),
                   pl.BlockSpec((GM_HEADS, CHUNK, CHUNK), lambda i: (0, 0, 0)),
                   pl.BlockSpec((CHUNK, GM_HEADS), lambda i: (0, 0)),
                   pl.BlockSpec((1, D_GM), lambda i: (0, 0)),
                   pl.BlockSpec((1, D_GM), lambda i: (0, 0))],
        out_shape=[jax.ShapeDtypeStruct(dproj.shape, dproj.dtype), jax.ShapeDtypeStruct((GM_HEADS, CHUNK, CHUNK), F32),
                   jax.ShapeDtypeStruct((CHUNK, GM_HEADS), F32), jax.ShapeDtypeStruct((1, D_GM), F32),
                   jax.ShapeDtypeStruct((1, D_GM), F32)],
        input_output_aliases={7: 0},
        compiler_params=_cp("arbitrary"),
    )(dy, proj, proj, ws, bst, vg, og, dproj)


def _local_step(x, target, mods, lw, final_g, *, nseq, big_w, grad_sink, small_sink):
    saved = []
    x0, delta, gate = x, None, None
    h1 = _normmod_fwd(x, lw[0]["norm1_g"], mods[0][1], mods[0][0], nseq=nseq, name="norm1_fwd_0")
    for l in range(DEPTH):
        w = lw[l]
        sh1, sc1, g1, sh2, sc2, g2 = mods[l]
        w_in = big_w(l, "w_in", h1)
        proj = _matmul(h1, w_in, tb=True, name=f"mm_in_{l}")
        xbc, xbc_pre = _ssd_conv_fwd(proj, w["ssd_conv_w"], w["ssd_conv_b"], nseq=nseq)
        ycat, hprev = _ssd_fwd(xbc, proj, w["ssd_dt_bias"], w["ssd_a_log"], w["ssd_d"], w["ssd_norm_g"], nseq=nseq)
        ycat = _gmlp_fwd(proj, ycat, w["gm_ws"], w["gm_bst"], w["gm_vnorm_g"], w["gm_out_g"])
        w_out = big_w(l, "w_out", ycat)
        mix, x1, h2 = _matmul_normfwd(ycat, w_out, x0, g1, w["norm2_g"], sc2, sh2, nseq=nseq, name=f"mm_out_{l}")
        ff_up = big_w(l, "ff_up", h2)
        up = _matmul(h2, ff_up, tb=True, name=f"mm_up_{l}", out_dtype=BF16)
        act = _ffn_act_fwd(up, w["ff_conv_w"], w["ff_conv_b"], nseq=nseq)
        ff_down = big_w(l, "ff_down", act)
        sv = dict(x0=x0, xin_delta=delta, xin_gate=gate, h1=h1, proj=proj, xbc=xbc, xbc_pre=xbc_pre, hprev=hprev,
                  ycat=ycat, mix=mix, x1=x1, h2=h2, up=up, act=act,
                  w_in=w_in, w_out=w_out, ff_up=ff_up, ff_down=ff_down)
        if l + 1 < DEPTH:
            nsh1, nsc1 = mods[l + 1][0], mods[l + 1][1]
            dn, x0, h1 = _matmul_normfwd(act, ff_down, x1, g2, lw[l + 1]["norm1_g"], nsc1, nsh1, nseq=nseq,
                                         name=f"mm_down_{l}")
        else:
            dn, loss, dx, ddelta, dgate, dfg = _matmul_loss(act, ff_down, x1, g2, final_g, target, nseq=nseq,
                                                            name=f"mm_down_{l}")
        saved.append(dict(sv, dn=dn))
        delta, gate = dn, g2

    small, dmods = [None] * DEPTH, [None] * DEPTH
    for l in reversed(range(DEPTH)):
        w, sv = lw[l], saved[l]
        sh1, sc1, g1, sh2, sc2, g2 = mods[l]
        dg2 = dgate
        g_ff_down = _matmul(sv["act"], ddelta, ta=True, name=f"mm_down_dw_{l}", out_dtype=BF16)
        dact = _matmul(ddelta, sv["ff_down"], tb=True, name=f"mm_down_dx_{l}", out_dtype=BF16)
        dgate_ff, dval_ff, dfcw, dfcb = _ffn_act_bwd(dact, sv["up"], w["ff_conv_w"], w["ff_conv_b"], nseq=nseq)
        g_ff_up = _matmul([dgate_ff, dval_ff], sv["h2"], ta=True, name=f"mm_up_dw_{l}", out_dtype=BF16)
        dep = grad_sink(l, "ffn", dict(ff_down=g_ff_down, ff_up=g_ff_up), dval_ff)
        dx, dmix, dg1, dn2g, dsc2, dsh2 = _matmul_normbwd([dgate_ff, dval_ff], sv["ff_up"], dx, sv["x1"], sv["mix"], g1,
                                                          w["norm2_g"], sc2, nseq=nseq, name=f"mm_up_dx_{l}", dep=dep)
        g_w_out = _matmul(sv["ycat"], dmix, ta=True, name=f"mm_out_dw_{l}", out_dtype=BF16)
        dep = grad_sink(l, "w_out", dict(w_out=g_w_out), dmix)
        dycat = _matmul(dmix, sv["w_out"], tb=True, name=f"mm_out_dx_{l}", dep=dep)
        dxbc_act, dproj, ddtb, dalog, ddsk, dng = _ssd_bwd(dycat, sv["xbc"], sv["proj"], sv["hprev"], w["ssd_dt_bias"],
                                                          w["ssd_a_log"], w["ssd_d"], w["ssd_norm_g"], nseq=nseq)
        dproj, dscw, dscb = _ssd_conv_bwd(dxbc_act, sv["xbc_pre"], sv["proj"], w["ssd_conv_w"], dproj, nseq=nseq)
        dproj, dws, dbst, dvg, dog = _gmlp_bwd(dycat, sv["proj"], w["gm_ws"], w["gm_bst"], w["gm_vnorm_g"], w["gm_out_g"], dproj)
        early = dict(norm2_g=dn2g, ssd_norm_g=dng, gm_vnorm_g=dvg, gm_out_g=dog,
                     ssd_conv_w=dscw, ssd_conv_b=dscb, ff_conv_w=dfcw, ff_conv_b=dfcb,
                     ssd_dt_bias=ddtb, ssd_a_log=dalog, ssd_d=ddsk, gm_ws=dws, gm_bs=dbst.T)
        dep = small_sink(l, early, small, dmods, dfg, loss)
        g_w_in = _matmul(dproj, sv["h1"], ta=True, name=f"mm_in_dw_{l}", out_dtype=BF16, dep=dep)
        dep = grad_sink(l, "w_in", dict(w_in=g_w_in), dproj)
        dx, ddelta, dgate, dn1g, dsc1, dsh1 = _matmul_normbwd(dproj, sv["w_in"], dx, sv["x0"], sv["xin_delta"],
                                                              sv["xin_gate"], w["norm1_g"], sc1, nseq=nseq,
                                                              name=f"mm_in_dx_{l}", dep=dep)
        small[l] = dict(early, norm1_g=dn1g)
        dmods[l] = jnp.concatenate([dsh1, dsc1, dg1, dsh2, dsc2, dg2], axis=-1)[:, 0, :]
    return dx, small, dmods


def _all_gather(arrs, name, dep=None):
    n = len(arrs)
    extra = [] if dep is None else [dep]

    def body(*refs):
        ins, outs = refs[:n], refs[n + len(extra):2 * n + len(extra)]
        send_sems, recv_sems, local_sems = refs[2 * n + len(extra):]
        x, y, c = lax.axis_index("x"), lax.axis_index("y"), lax.axis_index("c")
        me, sibling = (x, y, c), (x, y, 1 - c)
        chips = [(1 - x, y), (x, 1 - y), (1 - x, 1 - y)]

        def copy(i, k, block, to, src=None):
            px, py, pc = block
            dst = outs[i].at[4 * px + 2 * py + pc]
            return pltpu.make_async_remote_copy(
                src_ref=dst if src is None else src, dst_ref=dst,
                send_sem=send_sems.at[7 * i + k], recv_sem=recv_sems.at[7 * i + k],
                device_id=to, device_id_type=MESH)

        mine = [pltpu.make_async_copy(ins[i], outs[i].at[4 * x + 2 * y + c], local_sems.at[i]) for i in range(n)]
        for cp in mine:
            cp.start()
        first = []
        for i in range(n):
            first.append(copy(i, 0, me, sibling, src=ins[i]))
            first += [copy(i, 1 + j, me, (*chip, c), src=ins[i]) for j, chip in enumerate(chips)]
        for cp in first:
            cp.start()
        passed = []
        for j, chip in enumerate(chips):
            for i in range(n):
                copy(i, 1 + j, (*chip, c), me).wait_recv()
                fwd = copy(i, 4 + j, (*chip, c), sibling)
                fwd.start()
                passed.append(fwd)
        for i in range(n):
            copy(i, 0, sibling, me).wait_recv()
            for j, chip in enumerate(chips):
                copy(i, 4 + j, (*chip, 1 - c), me).wait_recv()
        for cp in first + passed:
            cp.wait_send()
        for cp in mine:
            cp.wait()

    return pl.pallas_call(
        body, name=name,
        in_specs=[ANY] * (n + len(extra)), out_specs=[ANY] * n,
        out_shape=[jax.ShapeDtypeStruct((N_DEV,) + a.shape, a.dtype) for a in arrs],
        scratch_shapes=[pltpu.SemaphoreType.DMA((7 * n,)), pltpu.SemaphoreType.DMA((7 * n,)),
                        pltpu.SemaphoreType.DMA((n,))],
    )(*arrs, *extra)


HBM = pl.BlockSpec(memory_space=pltpu.HBM)
SEM = pl.BlockSpec(memory_space=pltpu.SEMAPHORE)
EFFECT = pltpu.SideEffectType.DATAFLOW_SIDE_EFFECTING


def _peer(k):
    x, y, c = lax.axis_index("x"), lax.axis_index("y"), lax.axis_index("c")
    return (1 - x if k & 4 else x, 1 - y if k & 2 else y, 1 - c if k & 1 else c)


ALL_PEERS = tuple(range(1, N_DEV))
OTHER_CHIPS = (2, 4, 6)


def _xc_copies(scatter, srcs, lands, send_sems, recv_sems, peers=ALL_PEERS):
    x, y, c = lax.axis_index("x"), lax.axis_index("y"), lax.axis_index("c")
    copies = []
    for i in range(len(srcs)):
        for k in (peers[i] if isinstance(peers[0], tuple) else peers):
            px, py, pc = _peer(k)
            src = srcs[i].at[4 * px + 2 * py + pc] if scatter else srcs[i]
            dst = lands[i].at[k - 1] if scatter else lands[i].at[4 * x + 2 * y + c]
            copies.append(pltpu.make_async_remote_copy(
                src_ref=src, dst_ref=dst, send_sem=send_sems[i].at[k - 1], recv_sem=recv_sems[i].at[k - 1],
                device_id=(px, py, pc), device_id_type=MESH))
    return copies


def _xc_own(scatter, srcs, lands, send_sems):
    if scatter:
        return []
    me = 4 * lax.axis_index("x") + 2 * lax.axis_index("y") + lax.axis_index("c")
    return [pltpu.make_async_copy(srcs[i], lands[i].at[me], send_sems[i].at[N_DEV - 1]) for i in range(len(srcs))]


def _xc_start(scatter, arrs, after, name, peers=ALL_PEERS):
    n = len(arrs)
    lands = [lax.empty((N_DEV - 1,) + a.shape[1:] if scatter else (N_DEV,) + a.shape, a.dtype) for a in arrs]

    def body(*refs):
        srcs, lnd = refs[:n], refs[n:2 * n]
        send_sems, recv_sems = refs[2 * n + 1:3 * n + 1], refs[3 * n + 1:4 * n + 1]
        token = refs[6 * n + 1]
        for cp in _xc_copies(scatter, srcs, lnd, send_sems, recv_sems, peers) + _xc_own(scatter, srcs, lnd, send_sems):
            cp.start()
        token[...] = jnp.zeros_like(token)

    outs = pl.pallas_call(
        body, name=name,
        out_shape=[pltpu.SemaphoreType.DMA((N_DEV,))] * (2 * n)
        + [pltpu.HBM(a.shape, a.dtype) for a in arrs] + [pltpu.HBM(a.shape, a.dtype) for a in lands]
        + [jax.ShapeDtypeStruct((8, 128), F32)],
        in_specs=[HBM] * (2 * n) + [ANY],
        out_specs=[SEM] * (2 * n) + [HBM] * (2 * n) + [pl.BlockSpec(memory_space=pltpu.VMEM)],
        input_output_aliases={i: 2 * n + i for i in range(2 * n)},
        compiler_params=pltpu.CompilerParams(has_side_effects=EFFECT),
    )(*[pltpu.with_memory_space_constraint(a, pltpu.HBM) for a in list(arrs) + lands], after)
    return outs[:n], outs[n:2 * n], outs[2 * n:3 * n], outs[3 * n:4 * n], outs[4 * n][0, 0]


def _xc_wait(scatter, send_sems, recv_sems, srcs, lands, after, name, peers=ALL_PEERS):
    n = len(srcs)

    def body(*refs):
        s_refs, l_refs = refs[:n], refs[n:2 * n]
        ss, rs = refs[2 * n:3 * n], refs[3 * n:4 * n]
        for cp in _xc_copies(scatter, s_refs, l_refs, ss, rs, peers):
            cp.wait_send()
            cp.wait_recv()
        for cp in _xc_own(scatter, s_refs, l_refs, ss):
            cp.wait()

    outs = pl.pallas_call(
        body, name=name,
        out_shape=[pltpu.HBM(a.shape, a.dtype) for a in list(srcs) + list(lands)],
        in_specs=[HBM] * (2 * n) + [SEM] * (2 * n) + [ANY],
        out_specs=[HBM] * (2 * n),
        input_output_aliases={i: i for i in range(2 * n)},
        compiler_params=pltpu.CompilerParams(has_side_effects=EFFECT),
    )(*srcs, *lands, *send_sems, *recv_sems, after)
    return outs[:n], outs[n:]


def _sib_copies(zones, send_sems, recv_sems):
    x, y, c = lax.axis_index("x"), lax.axis_index("y"), lax.axis_index("c")
    copies = []
    for i in range(len(zones)):
        for q in range(N_DEV // 2):
            slot = zones[i].at[2 * q + c]
            copies.append(pltpu.make_async_remote_copy(
                src_ref=slot, dst_ref=slot, send_sem=send_sems[i].at[q], recv_sem=recv_sems[i].at[q],
                device_id=(x, y, 1 - c), device_id_type=MESH))
    return copies


def _sib_start(zones, name):
    n = len(zones)

    def body(*refs):
        for cp in _sib_copies(refs[:n], refs[n:2 * n], refs[2 * n:3 * n]):
            cp.start()

    outs = pl.pallas_call(
        body, name=name,
        out_shape=[pltpu.SemaphoreType.DMA((N_DEV // 2,))] * (2 * n) + [pltpu.HBM(a.shape, a.dtype) for a in zones],
        in_specs=[HBM] * n,
        out_specs=[SEM] * (2 * n) + [HBM] * n,
        input_output_aliases={i: 2 * n + i for i in range(n)},
        compiler_params=pltpu.CompilerParams(has_side_effects=EFFECT),
    )(*[pltpu.with_memory_space_constraint(a, pltpu.HBM) for a in zones])
    return outs[:n], outs[n:2 * n], outs[2 * n:]


def _sib_wait(send_sems, recv_sems, zones, name):
    n = len(zones)

    def body(*refs):
        for cp in _sib_copies(refs[:n], refs[n:2 * n], refs[2 * n:3 * n]):
            cp.wait_send()
            cp.wait_recv()

    return pl.pallas_call(
        body, name=name,
        out_shape=[pltpu.HBM(a.shape, a.dtype) for a in zones],
        in_specs=[HBM] * n + [SEM] * (2 * n),
        out_specs=[HBM] * n,
        input_output_aliases={i: i for i in range(n)},
        compiler_params=pltpu.CompilerParams(has_side_effects=EFFECT),
    )(*zones, *send_sems, *recv_sems)


def _adamw_math(w, g, m, v):
    m = ADAM_B1 * m + (1.0 - ADAM_B1) * g
    v = ADAM_B2 * v + (1.0 - ADAM_B2) * (g * g)
    m_hat = m / (1.0 - ADAM_B1 ** ADAM_STEP)
    v_hat = v / (1.0 - ADAM_B2 ** ADAM_STEP)
    delta = -ADAM_LR * (m_hat / (jnp.sqrt(v_hat) + ADAM_EPS) + ADAM_WD * w)
    return delta, m, v


def _adamw_sharded(parts, w, m, v, pos, name):
    depth, rows, cols = w.shape
    tr = _tile(rows, 256) if rows % 8 == 0 else rows
    npart = len(parts)

    def body(pos_ref, *refs):
        prefs = refs[:npart]
        w_ref, m_ref, v_ref, g_out, d_out, m_out, v_out = refs[npart:]
        g = prefs[0][...]
        for pr in prefs[1:]:
            g = g + pr[...]
        delta, mn, vn = _adamw_math(w_ref[...], g, m_ref[...], v_ref[...])
        g_out[...] = g
        d_out[...] = delta
        m_out[...] = mn
        v_out[...] = vn

    def part_spec(fn):
        return pl.BlockSpec((1, tr, cols), lambda l, i, p: (fn(p) * depth + l, i, 0))

    blk = pl.BlockSpec((1, tr, cols), lambda l, i, p: (l, i, 0))
    shp = jax.ShapeDtypeStruct((depth, rows, cols), F32)
    return pl.pallas_call(
        body, name=name,
        grid_spec=pltpu.PrefetchScalarGridSpec(
            num_scalar_prefetch=1, grid=(depth, rows // tr),
            in_specs=[part_spec(fn) for _, fn in parts] + [blk, blk, blk],
            out_specs=[blk, blk, blk, blk]),
        out_shape=[shp, shp, shp, shp],
        compiler_params=_cp("parallel", "parallel"),
    )(pos, *[a for a, _ in parts], w, m, v)


def _adamw_layer(parts, w, m, v, pos, layer, prev, name):
    depth, rows, cols = w.shape
    npart = len(parts)
    nprev = 0 if prev is None else 4
    if rows % 16 == 0:
        tr, tc = max(t for t in range(16, 257, 16) if rows % t == 0), cols
    else:
        tr, tc = rows, _tile(cols, 256)
    pick = (lambda i: (i, 0)) if rows % 16 == 0 else (lambda i: (0, i))

    def body(pos_ref, *refs):
        prefs = refs[:npart]
        w_ref, m_ref, v_ref = refs[npart:npart + 3]
        g_out, d_out, m_out, v_out = refs[npart + 3 + nprev:]
        g = prefs[0][...].astype(F32)
        for pr in prefs[1:]:
            g = g + pr[...].astype(F32)
        delta, mn, vn = _adamw_math(w_ref[...], g, m_ref[...], v_ref[...])
        g_out[...] = g
        d_out[...] = delta
        m_out[...] = mn
        v_out[...] = vn

    def part_spec(fn):
        return pl.BlockSpec((1, tr, tc), lambda i, p: (fn(p), *pick(i)))

    blk = pl.BlockSpec((1, tr, tc), lambda i, p: (layer, *pick(i)))
    shp = jax.ShapeDtypeStruct((depth, rows, cols), F32)
    first_prev = 1 + npart + 3
    return pl.pallas_call(
        body, name=name,
        grid_spec=pltpu.PrefetchScalarGridSpec(
            num_scalar_prefetch=1, grid=(rows // tr * (cols // tc),),
            in_specs=[part_spec(fn) for _, fn in parts] + [blk, blk, blk] + [ANY] * nprev,
            out_specs=[blk, blk, blk, blk]),
        out_shape=[shp, shp, shp, shp],
        input_output_aliases={first_prev + j: j for j in range(nprev)},
        compiler_params=_cp("parallel"),
    )(pos, *[a for a, _ in parts], w, m, v, *(prev or ()))


_P1024 = ["norm1_g", "norm2_g", "ssd_norm_g", "gm_vnorm_g", "gm_out_g"]
_P16 = ["ssd_dt_bias", "ssd_a_log", "ssd_d"]


def _adamw_small(gath, wmv):
    names = list(wmv.keys())
    classes = list(gath.keys())
    flat_in = [gath[k] for k in classes]
    for nme in names:
        flat_in += list(wmv[nme])
    out_shapes = []
    for nme in names:
        out_shapes += [jax.ShapeDtypeStruct(wmv[nme][0].shape, F32)] * 4
    out_shapes += [jax.ShapeDtypeStruct((DEPTH, SSD_CONV, CONV_DIM), F32), jax.ShapeDtypeStruct((DEPTH, FF_CONV, D_FF), F32),
                   jax.ShapeDtypeStruct((1, SSD_HEADS), F32)]
    scratch = [pltpu.VMEM(gath[k].shape[1:], F32) for k in classes]
    ncls = len(classes)

    def body(*refs):
        g_refs = dict(zip(classes, refs[:ncls]))
        pos = ncls
        w_refs = {}
        for nme in names:
            w_refs[nme] = refs[pos:pos + 3]
            pos += 3
        o_refs = {}
        for nme in names:
            o_refs[nme] = refs[pos:pos + 4]
            pos += 4
        scw_out, fcw_out, loss_out = refs[pos], refs[pos + 1], refs[pos + 2]
        s_refs = dict(zip(classes, refs[pos + 3:]))
        for k in classes:
            acc = g_refs[k][0]
            for dev in range(1, N_DEV):
                acc = acc + g_refs[k][dev]
            s_refs[k][...] = acc

        def apply(nme, grad_of):
            w_ref, m_ref, v_ref = w_refs[nme]
            g_out, d_out, m_out, v_out = o_refs[nme]
            shape = w_ref.shape
            if len(shape) == 2:
                idxs = [(slice(l, l + 1),) for l in range(shape[0])]
            elif len(shape) == 3:
                idxs = [(l,) for l in range(shape[0])]
            else:
                idxs = [(l, h) for l in range(shape[0]) for h in range(shape[1])]
            for n_i, ix in enumerate(idxs):
                g = grad_of(n_i)
                delta, mn, vn = _adamw_math(w_ref[ix], g, m_ref[ix], v_ref[ix])
                g_out[ix] = g
                d_out[ix] = delta
                m_out[ix] = mn
                v_out[ix] = vn

        s1024, s1536, s2816, s16, s128, s6144, late1024, late6144 = (s_refs[k] for k in classes)
        s1024[0:1, :] += late1024[...]
        s6144[0:late6144.shape[0], :] += late6144[...]
        for n_i, nme in enumerate(_P1024):
            apply(nme, lambda l, b=2 * n_i: s1024[b + l:b + l + 1, :])
        apply("final_g", lambda l: s1024[10:11, :])
        apply("ssd_conv_b", lambda l: s1536[8 + l:9 + l, :])
        apply("ff_conv_b", lambda l: s2816[6 + l:7 + l, :])
        for n_i, nme in enumerate(_P16):
            apply(nme, lambda l, b=2 * n_i: s16[b + l:b + l + 1, :])
        apply("gm_ws", lambda q: s128[q * CHUNK:(q + 1) * CHUNK, :])
        apply("gm_bs", lambda l: s128[2048 + 8 * l:2048 + 8 * (l + 1), :])
        apply("ada_b", lambda l: s6144[2 * l:2 * l + 1, :] + s6144[2 * l + 1:2 * l + 2, :])
        for l in range(DEPTH):
            scw_out[l] = s1536[SSD_CONV * l:SSD_CONV * (l + 1), :]
            fcw_out[l] = s2816[FF_CONV * l:FF_CONV * (l + 1), :]
        loss_out[...] = s16[2 * len(_P16):2 * len(_P16) + 1, :]

    outs = pl.pallas_call(
        body, name="adamw_small",
        out_shape=out_shapes,
        scratch_shapes=scratch,
        compiler_params=pltpu.CompilerParams(vmem_limit_bytes=VMEM_LIMIT),
    )(*flat_in)
    res = {nme: tuple(outs[4 * i:4 * i + 4]) for i, nme in enumerate(names)}
    return res, outs[-3], outs[-2], outs[-1]


_WEIGHTS = ['ada_w', 'ada_b', 'norm1_g', 'norm2_g', 'w_in', 'ssd_conv_w', 'ssd_conv_b', 'ssd_dt_bias', 'ssd_a_log',
            'ssd_d', 'ssd_norm_g', 'gm_vnorm_g', 'gm_ws', 'gm_bs', 'gm_out_g', 'w_out', 'ff_up', 'ff_conv_w',
            'ff_conv_b', 'ff_down', 'final_g']


_O_XBC, _O_DT, _O_GM = D_SSD, D_SSD + CONV_DIM, D_SSD + CONV_DIM + SSD_HEADS


_TRANSPOSED = ("w_in", "ff_up")


def _full_weight(name, g):
    full = g.reshape(g.shape[0] * g.shape[1], g.shape[2])
    if name != "w_in":
        return full
    zpad = jnp.zeros((N_INP - N_IN, full.shape[1]), full.dtype)
    return jnp.concatenate([full[_O_GM:], full[:_O_XBC], full[_O_XBC:_O_DT], full[_O_DT:_O_GM], zpad], axis=0)


def _by_owner(name, grad):
    if name == "w_in":
        grad = jnp.concatenate([grad[COL_Z:COL_XBC], grad[COL_XBC:COL_DT], grad[COL_DT:COL_DT + SSD_HEADS], grad[:COL_Z]], axis=0)
    return grad.reshape(N_DEV, grad.shape[0] // N_DEV, grad.shape[1])


def kernel(x, c, ada_w, ada_b, norm1_g, norm2_g, w_in, ssd_conv_w, ssd_conv_b, ssd_dt_bias, ssd_a_log, ssd_d, ssd_norm_g, gm_vnorm_g, gm_ws, gm_bs, gm_out_g, w_out, ff_up, ff_conv_w, ff_conv_b, ff_down, final_g, loss_target, m_ada_w, m_ada_b, m_norm1_g, m_norm2_g, m_w_in, m_ssd_conv_w, m_ssd_conv_b, m_ssd_dt_bias, m_ssd_a_log, m_ssd_d, m_ssd_norm_g, m_gm_vnorm_g, m_gm_ws, m_gm_bs, m_gm_out_g, m_w_out, m_ff_up, m_ff_conv_w, m_ff_conv_b, m_ff_down, m_final_g, v_ada_w, v_ada_b, v_norm1_g, v_norm2_g, v_w_in, v_ssd_conv_w, v_ssd_conv_b, v_ssd_dt_bias, v_ssd_a_log, v_ssd_d, v_ssd_norm_g, v_gm_vnorm_g, v_gm_ws, v_gm_bs, v_gm_out_g, v_w_out, v_ff_up, v_ff_conv_w, v_ff_conv_b, v_ff_down, v_final_g):
    given = dict(locals())
    wts = {n: given[n] for n in _WEIGHTS}
    mom = {n: given["m_" + n] for n in _WEIGHTS}
    var = {n: given["v_" + n] for n in _WEIGHTS}
    nseq, seq, d = x.shape
    ix, iy, ic = lax.axis_index("x"), lax.axis_index("y"), lax.axis_index("c")
    me = 4 * ix + 2 * iy + ic
    me_arr = me.astype(jnp.int32).reshape(1)

    for nme in _TRANSPOSED:
        wts[nme], mom[nme], var[nme] = (jnp.transpose(a, (0, 2, 1)) for a in (wts[nme], mom[nme], var[nme]))

    def shard(l, name):
        return _b(wts[name][l])

    g_scw, g_fcw, c_all = _all_gather([ssd_conv_w, ff_conv_w, c], "gather_first")
    scw_f = jnp.transpose(g_scw, (1, 2, 0, 3)).reshape(DEPTH, SSD_CONV, CONV_DIM)
    fcw_f = jnp.transpose(g_fcw, (1, 2, 0, 3)).reshape(DEPTH, FF_CONV, D_FF)
    c_all = c_all.reshape(N_DEV * nseq, d)

    n_ada = ada_w.shape[2]
    ada_b_shard = lax.dynamic_slice_in_dim(ada_b, me * n_ada, n_ada, axis=1).reshape(DEPTH, 1, n_ada)
    mod_part, c_act = _ada_fwd(c_all, ada_w, ada_b_shard)
    first_ssem, first_rsem, first_src, first_land, first_zero = _xc_start(
        False, [mod_part, shard(0, "w_in")], c_act, "ag_first_start", peers=[ALL_PEERS, OTHER_CHIPS])
    _, (mod_g,) = _xc_wait(False, first_ssem[:1], first_rsem[:1], first_src[:1], first_land[:1], c_act,
                           "mod_wait")
    mod_all = jnp.transpose(mod_g, (1, 2, 0, 3)).reshape(DEPTH, N_DEV * nseq, N_MOD * d)
    mod_mine = lax.dynamic_slice_in_dim(mod_all, me * nseq, nseq, axis=1)
    mod_k = jnp.transpose(mod_mine.reshape(DEPTH, nseq, N_MOD, 1, d), (0, 2, 1, 3, 4))
    mods = [[mod_k[l, k] for k in range(N_MOD)] for l in range(DEPTH)]

    later =[(0, "w_out"), (0, "ff_up"), (0, "ff_down"), (1, "w_in"), (1, "w_out"), (1, "ff_up"), (1, "ff_down")]
    ag_groups = {(0, "w_out"): [0], (0, "ff_up"): [1, 2], (1, "w_in"): [3, 4], (1, "ff_up"): [5, 6]}
    big_cache, ag = {}, {}

    def big_w(l, name, after):
        if (l, name) == (0, "w_in") and (l, name) not in big_cache:
            ag["ssem"], ag["rsem"], ag["src"], ag["land"], started = _xc_start(
                False, [shard(l2, n2) for l2, n2 in later], after, "ag_start")
            _, zones = _xc_wait(False, first_ssem[1:], first_rsem[1:], first_src[1:], first_land[1:],
                                jnp.full((8, 128), started, F32), "ag_first_wait", peers=OTHER_CHIPS)
            (zone,) = _sib_wait(*_sib_start(zones, "ag_first_sib_start"), "ag_first_sib_wait")
            big_cache[(l, name)] = _full_weight(name, zone)
        if (l, name) not in big_cache:
            idx = ag_groups[(l, name)]
            pick = lambda seq_: [seq_[i] for i in idx]
            _, lands = _xc_wait(False, pick(ag["ssem"]), pick(ag["rsem"]), pick(ag["src"]), pick(ag["land"]), after,
                                f"ag_wait_{l}_{name}")
            for i, land in zip(idx, lands):
                big_cache[later[i]] = _full_weight(later[i][1], land)
        return big_cache[(l, name)]

    lw = []
    for l in range(DEPTH):
        lw.append(dict(
            norm1_g=norm1_g[l:l + 1] + (first_zero if l == 0 else 0.0), norm2_g=norm2_g[l:l + 1], ssd_conv_w=scw_f[l],
            ssd_conv_b=ssd_conv_b[l:l + 1], ssd_dt_bias=ssd_dt_bias[l:l + 1], ssd_a_log=ssd_a_log[l:l + 1],
            ssd_d=ssd_d[l:l + 1], ssd_norm_g=ssd_norm_g[l:l + 1], gm_vnorm_g=gm_vnorm_g[l:l + 1], gm_ws=gm_ws[l],
            gm_bst=gm_bs[l].T, gm_out_g=gm_out_g[l:l + 1], ff_conv_w=fcw_f[l], ff_conv_b=ff_conv_b[l:l + 1]))

    outs = {}
    pending = {}

    def rs_finish(l, group, after):
        names, ssem, rsem, srcs, lands = pending.pop((l, group))
        srcs, lands = _xc_wait(True, ssem, rsem, srcs, lands, after, f"rs_wait_{l}_{group}")
        for nme, own, land in zip(names, srcs, lands):
            parts = [(own, lambda p: p[0])] + [(land, lambda p, k=k: k) for k in range(N_DEV - 1)]
            outs[nme] = _adamw_layer(parts, wts[nme], mom[nme], var[nme], me_arr, l, outs.get(nme), f"adamw_{nme}_{l}")
        return outs[names[-1]][0]

    def grad_sink(l, group, grads, after):
        names = list(grads)
        ssem, rsem, srcs, lands, zero = _xc_start(True, [_by_owner(n, grads[n]) for n in names], after, f"rs_start_{l}_{group}")
        pending[(l, group)] = (names, ssem, rsem, srcs, lands)
        return zero.reshape(1, 1)

    early_gather = {}

    def small_sink(l, early, small, dmods, dfg, loss_p):
        if l > 0:
            return None
        layers = [dict(early, norm1_g=jnp.zeros((1, d), F32))] + small[1:]
        rows = lambda name: [layers[k][name] for k in range(DEPTH)]
        packed = [
            jnp.concatenate(sum([rows(n) for n in _P1024], []) + [dfg], axis=0),
            jnp.concatenate(rows("ssd_conv_w") + rows("ssd_conv_b"), axis=0),
            jnp.concatenate(rows("ff_conv_w") + rows("ff_conv_b"), axis=0),
            jnp.concatenate(sum([rows(n) for n in _P16], []) + [loss_p[:, :SSD_HEADS]], axis=0),
            jnp.concatenate([layers[k]["gm_ws"].reshape(GM_HEADS * CHUNK, CHUNK) for k in range(DEPTH)] + rows("gm_bs"), axis=0),
            jnp.concatenate([jnp.zeros((nseq, N_MOD * d), F32)] + dmods[1:], axis=0)]
        ssem, rsem, srcs, lands, zero = _xc_start(False, packed, packed[0], "small_start")
        early_gather.update(ssem=ssem, rsem=rsem, srcs=srcs, lands=lands)
        return zero.reshape(1, 1)

    grad_x, small, dmods = _local_step(
        x.reshape(nseq * seq, d), loss_target.reshape(nseq * seq, d), mods, lw, final_g.reshape(1, d), nseq=nseq,
        big_w=big_w, grad_sink=grad_sink, small_sink=small_sink)

    done = grad_x
    for l, grp in ((1, "ffn"), (1, "w_out"), (1, "w_in"), (0, "ffn"), (0, "w_out")):
        done = rs_finish(l, grp, done)
    _, gathered = _xc_wait(False, early_gather["ssem"], early_gather["rsem"], early_gather["srcs"],
                           early_gather["lands"], done, "small_wait")
    gathered = list(gathered)
    gathered += _all_gather([small[0]["norm1_g"], dmods[0]], "gather_late", dep=gathered[0])
    gath = dict(zip(["p1024", "p1536", "p2816", "p16", "p128", "p6144", "late1024", "late6144"], gathered))

    dmod_all = jnp.concatenate([gath["late6144"].reshape(1, N_DEV * nseq, N_MOD * d),
                                jnp.transpose(gath["p6144"].reshape(N_DEV, DEPTH, nseq, N_MOD * d)[:, 1:], (1, 0, 2, 3)).reshape(
                                    DEPTH - 1, N_DEV * nseq, N_MOD * d)], axis=0)
    small_names = _P1024 + ["final_g", "ssd_conv_b", "ff_conv_b"] + _P16 + ["gm_ws", "gm_bs", "ada_b"]
    wmv = {}
    for nme in small_names:
        if nme == "final_g":
            wmv[nme] = tuple(a.reshape(1, d) for a in (wts[nme], mom[nme], var[nme]))
        else:
            wmv[nme] = (wts[nme], mom[nme], var[nme])
    small_out, scw_full, fcw_full, loss_sum = _adamw_small(gath, wmv)
    loss = loss_sum[0, 0]
    rs_finish(0, "w_in", scw_full)
    for nme in small_names:
        outs[nme] = small_out[nme]
    outs["final_g"] = tuple(a.reshape(d) for a in outs["final_g"])

    n_scw, n_fcw = ssd_conv_w.shape[2], ff_conv_w.shape[2]
    g_scw_mine = lax.dynamic_slice_in_dim(scw_full, me * n_scw, n_scw, axis=2)
    g_fcw_mine = lax.dynamic_slice_in_dim(fcw_full, me * n_fcw, n_fcw, axis=2)
    outs["ssd_conv_w"] = _adamw_sharded([(g_scw_mine, lambda p: 0)], ssd_conv_w, m_ssd_conv_w, v_ssd_conv_w, me_arr, "adamw_ssd_conv_w")
    outs["ff_conv_w"] = _adamw_sharded([(g_fcw_mine, lambda p: 0)], ff_conv_w, m_ff_conv_w, v_ff_conv_w, me_arr, "adamw_ff_conv_w")

    dmod_cols = _b(lax.dynamic_slice_in_dim(dmod_all, me * n_ada, n_ada, axis=2))
    g_ada = jnp.stack([_matmul(c_act, dmod_cols[l], ta=True, name=f"mm_ada_dw_{l}") for l in range(DEPTH)])
    outs["ada_w"] = _adamw_sharded([(g_ada, lambda p: 0)], ada_w, m_ada_w, v_ada_w, me_arr, "adamw_ada_w")

    for nme in _TRANSPOSED:
        outs[nme] = tuple(jnp.transpose(a, (0, 2, 1)) for a in outs[nme])
    result = [loss, grad_x.reshape(nseq, seq, d)]
    for k in range(4):
        result += [outs[n][k] for n in _WEIGHTS]
    return tuple(result)
```

```python
import functools
import math

import jax
import jax.numpy as jnp
from jax import lax
from jax.experimental import pallas as pl
from jax.experimental.pallas import tpu as pltpu

F32 = jnp.float32
BF16 = jnp.bfloat16

N_DEV = 8
D_MODEL = 1024
DEPTH = 2
CHUNK = 128
SSD_HEADS = 16
SSD_HEAD_DIM = 64
SSD_GROUPS = 2
HEADS_PER_GROUP = SSD_HEADS // SSD_GROUPS
GROUP_WIDTH = HEADS_PER_GROUP * SSD_HEAD_DIM
D_STATE = 128
D_SSD = 1024
CONV_DIM = 1536
SSD_CONV = 4
GM_HEADS = 8
GM_HEAD_DIM = 128
D_GM = 1024
D_FF = 2816
FF_CONV = 3
N_IN = 4624
N_MOD = 6
EPS = 1e-6

N_INP = 5120
COL_U, COL_V, COL_Z, COL_XBC, COL_DT = 0, 1024, 2048, 3072, 4608

ADAM_LR = 0.001
ADAM_B1 = 0.9
ADAM_B2 = 0.999
ADAM_EPS = 1e-08
ADAM_WD = 0.01
ADAM_STEP = 10

VMEM_LIMIT = 56 * 1024 * 1024
MESH = pl.DeviceIdType.MESH
ANY = pl.BlockSpec(memory_space=pl.ANY)


def _cp(*sem):
    return pltpu.CompilerParams(dimension_semantics=sem, vmem_limit_bytes=VMEM_LIMIT)


def _tile(n, pref):
    if n <= pref or n % 128:
        return n
    best = 128
    for t in range(128, pref + 1, 128):
        if n % t == 0:
            best = t
    return best


def _silu(x):
    return x * jax.nn.sigmoid(x)


def _gelu(x):
    return 0.5 * x * (1.0 + lax.erf(x * (1.0 / math.sqrt(2.0))))


def _softplus(x):
    return jnp.maximum(x, 0.0) + jnp.log1p(jnp.exp(-jnp.abs(x)))


def _b(x):
    return x.astype(BF16)


_NN = (((1,), (0,)), ((), ()))
_NT = (((1,), (1,)), ((), ()))
_TN = (((0,), (0,)), ((), ()))


def _dg(a, b, dn):
    return lax.dot_general(_b(a), _b(b), dn, preferred_element_type=F32)


@jax.custom_vjp
def _bdot(a, b):
    return _dg(a, b, _NN)


def _bdot_fwd(a, b):
    return _dg(a, b, _NN), (a, b)


def _bdot_bwd(res, ct):
    a, b = res
    return _dg(ct, b, _NT), _dg(a, ct, _TN)


_bdot.defvjp(_bdot_fwd, _bdot_bwd)


@jax.custom_vjp
def _bdot_nt(a, b):
    return _dg(a, b, _NT)


def _bdot_nt_fwd(a, b):
    return _dg(a, b, _NT), (a, b)


def _bdot_nt_bwd(res, ct):
    a, b = res
    return _dg(ct, b, _NN), _dg(ct, a, _TN)


_bdot_nt.defvjp(_bdot_nt_fwd, _bdot_nt_bwd)


@jax.custom_vjp
def _bdot_tn(a, b):
    return _dg(a, b, _TN)


def _bdot_tn_fwd(a, b):
    return _dg(a, b, _TN), (a, b)


def _bdot_tn_bwd(res, ct):
    a, b = res
    return _dg(b, ct, _NT), _dg(a, ct, _NN)


_bdot_tn.defvjp(_bdot_tn_fwd, _bdot_tn_bwd)


def _tri(n, lower):
    r = lax.broadcasted_iota(jnp.int32, (n, n), 0)
    c = lax.broadcasted_iota(jnp.int32, (n, n), 1)
    return ((r >= c) if lower else (r <= c)).astype(F32)


def _eye(n):
    r = lax.broadcasted_iota(jnp.int32, (n, n), 0)
    c = lax.broadcasted_iota(jnp.int32, (n, n), 1)
    return (r == c).astype(F32)


def _hdot(a, b, dn):
    return lax.dot_general(a, b, dn, precision=lax.Precision.HIGHEST, preferred_element_type=F32)


@jax.custom_vjp
def _cumsum_rows(x):
    return _hdot(_tri(x.shape[0], True), x, _NN)


def _cumsum_rows_fwd(x):
    return _cumsum_rows(x), None


def _cumsum_rows_bwd(_, ct):
    return (_hdot(_tri(ct.shape[0], False), ct, _NN),)


_cumsum_rows.defvjp(_cumsum_rows_fwd, _cumsum_rows_bwd)


@jax.custom_vjp
def _transpose(x):
    return _hdot(_eye(x.shape[1]), x, _NT)


def _transpose_fwd(x):
    return _transpose(x), None


def _transpose_bwd(_, ct):
    return (_hdot(_eye(ct.shape[1]), ct, _NT),)


_transpose.defvjp(_transpose_fwd, _transpose_bwd)


MXU_WIDTH = 256
MATMUL_TILE_CAP = 2816
MATMUL_VMEM = 44 * 1024 * 1024


def _mxu_tiles(n):
    if n <= MATMUL_TILE_CAP or n % 128:
        return [n]
    for unit in (MXU_WIDTH, 128):
        opts = [t for t in range(unit, MATMUL_TILE_CAP + 1, unit) if n % t == 0]
        if opts:
            return opts
    return [n]


def _matmul(a, b, *, ta=False, tb=False, name, dep=None, out_dtype=F32):
    pieces = list(a) if isinstance(a, (list, tuple)) else [a]
    npc = len(pieces)
    rows, width = pieces[0].shape
    assert all(p.shape == (rows, width) for p in pieces)
    if ta:
        k_dim, m_dim = rows, width * npc
    else:
        m_dim, k_dim = rows, width * npc
    if tb:
        n_dim, kb = b.shape
    else:
        kb, n_dim = b.shape
    assert kb == k_dim, (pieces[0].shape, npc, b.shape, ta, tb)
    m_unit = width if npc > 1 and ta else m_dim
    k_unit = width if npc > 1 and not ta else k_dim
    tm = _tile(m_unit, 1536)
    tn_opts, tk_opts = _mxu_tiles(n_dim), _mxu_tiles(k_unit)
    tn, tk = tn_opts.pop(), tk_opts.pop()
    while 4 * (tm * tk + tk * tn) + 8 * tm * tn > MATMUL_VMEM:
        if tn >= tk and tn_opts:
            tn = tn_opts.pop()
        else:
            tk = tk_opts.pop()
    ni, nj, nk = m_dim // tm, n_dim // tn, k_dim // tk
    per = width // (tm if ta else tk)
    dn = (((0 if ta else 1,), (1 if tb else 0,)), ((), ()))

    a_bytes, b_bytes = m_dim * k_dim, k_dim * n_dim
    m_outer = nk > 1 or a_bytes + b_bytes * ni <= b_bytes + a_bytes * nj
    if m_outer:
        ij = lambda o, n, k: (o, n)
        grid = (ni, nj, nk)
    else:
        ij = lambda o, n, k: (n, o)
        grid = (nj, ni, nk)

    use_acc = nk > 1 and out_dtype != F32

    def body(*refs):
        a_refs, b_ref = refs[:npc], refs[npc]
        o_ref = refs[-2] if use_acc else refs[-1]
        acc_ref = refs[-1]
        k = pl.program_id(2)
        i = pl.program_id(0 if m_outer else 1)
        along = i if ta else k

        def step(a_ref):
            p = lax.dot_general(a_ref[...], b_ref[...], dn, preferred_element_type=F32)
            if nk == 1:
                o_ref[...] = p.astype(out_dtype)
            else:
                @pl.when(k == 0)
                def _():
                    acc_ref[...] = p

                @pl.when((k > 0) & (k < nk - 1 if use_acc else True))
                def _():
                    acc_ref[...] += p

                if use_acc:
                    @pl.when(k == nk - 1)
                    def _():
                        o_ref[...] = (acc_ref[...] + p).astype(out_dtype)

        if npc == 1:
            step(a_refs[0])
        else:
            for pc in range(npc):
                pl.when((along >= pc * per) & (along < (pc + 1) * per))(functools.partial(step, a_refs[pc]))

    def a_map(pc, o, n, k):
        i, _ = ij(o, n, k)
        along = i if ta else k
        if npc > 1:
            along = jnp.clip(along - pc * per, 0, per - 1)
        return (k, along) if ta else (i, along)

    def b_map(o, n, k):
        _, j = ij(o, n, k)
        return (j, k) if tb else (k, j)

    extra = [] if dep is None else [dep]
    return pl.pallas_call(
        body, name=name,
        grid=grid,
        in_specs=[pl.BlockSpec((tk, tm) if ta else (tm, tk), functools.partial(a_map, pc)) for pc in range(npc)]
        + [pl.BlockSpec((tn, tk) if tb else (tk, tn), b_map)] + [ANY] * len(extra),
        out_specs=pl.BlockSpec((tm, tn), lambda o, n, k: ij(o, n, k)),
        out_shape=jax.ShapeDtypeStruct((m_dim, n_dim), out_dtype),
        scratch_shapes=[pltpu.VMEM((tm, tn), F32)] if use_acc else [],
        compiler_params=_cp("parallel", "parallel", "arbitrary"),
    )(*pieces, b, *extra)


def _ada_fwd(c_all, ada_w, ada_b_shard):
    depth, d, n = ada_w.shape
    nb = c_all.shape[0]

    def body(c_ref, w_ref, b_ref, o_ref, ca_ref):
        ca = _silu(c_ref[...])
        ca_ref[...] = _b(ca)
        o_ref[0] = _dg(ca, w_ref[0], _NN) + b_ref[0]

    return pl.pallas_call(
        body, name="ada_fwd",
        grid=(depth,),
        in_specs=[pl.BlockSpec((nb, d), lambda l: (0, 0)),
                  pl.BlockSpec((1, d, n), lambda l: (l, 0, 0)),
                  pl.BlockSpec((1, 1, n), lambda l: (l, 0, 0))],
        out_specs=[pl.BlockSpec((1, nb, n), lambda l: (l, 0, 0)),
                   pl.BlockSpec((nb, d), lambda l: (0, 0))],
        out_shape=[jax.ShapeDtypeStruct((depth, nb, n), F32), jax.ShapeDtypeStruct((nb, d), BF16)],
        compiler_params=_cp("arbitrary"),
    )(c_all, ada_w, ada_b_shard)


def _fold(acc):
    return jnp.sum(acc, axis=0, keepdims=True)


def _rinv(x):
    return lax.rsqrt(jnp.sum(x * x, axis=-1, keepdims=True) * (1.0 / D_MODEL) + EPS)


def _rms_bwd(a, xhat, rinv):
    return rinv * (a - xhat * (jnp.sum(a * xhat, axis=-1, keepdims=True) * (1.0 / D_MODEL)))


def _row_tile(seq):
    return min(seq, 256)


def _normmod_fwd(x, g, sc, sh, *, nseq, name):
    t, d = x.shape
    seq = t // nseq
    tr = _row_tile(seq)
    nt = seq // tr
    row = pl.BlockSpec((tr, d), lambda s, i: (s * nt + i, 0))
    per_seq = pl.BlockSpec((1, 1, d), lambda s, i: (s, 0, 0))

    def body(x_ref, g_ref, sc_ref, sh_ref, h_ref):
        x_v = x_ref[...]
        h_ref[...] = _b(x_v * _rinv(x_v) * (g_ref[...] * (1.0 + sc_ref[0])) + sh_ref[0])

    return pl.pallas_call(
        body, name=name, grid=(nseq, nt),
        in_specs=[row, pl.BlockSpec((1, d), lambda s, i: (0, 0)), per_seq, per_seq],
        out_specs=row,
        out_shape=jax.ShapeDtypeStruct((t, d), BF16),
        compiler_params=_cp("parallel", "parallel"),
    )(x, g, sc, sh)


NORM_TM = 512


def _matmul_normbwd(a, b, dxo, x, delta, gate, g, sc, *, nseq, name, dep=None):
    pieces = list(a) if isinstance(a, (list, tuple)) else [a]
    npc = len(pieces)
    t, width = pieces[0].shape
    k_dim, d = width * npc, b.shape[1]
    assert b.shape[0] == k_dim and all(p.shape == (t, width) for p in pieces)
    seq = t // nseq
    tm = min(NORM_TM, seq)
    per_seq_tiles = seq // tm
    tk = _mxu_tiles(width if npc > 1 else k_dim).pop()
    nk, per = k_dim // tk, width // tk
    has_delta = delta is not None
    extra = [] if dep is None else [dep]

    def body(*refs):
        a_refs, b_ref = refs[:npc], refs[npc]
        dxo_ref, x_ref = refs[npc + 1], refs[npc + 2]
        pos = npc + 3
        if has_delta:
            delta_ref, gate_ref = refs[pos], refs[pos + 1]
            pos += 2
        g_ref, sc_ref = refs[pos], refs[pos + 1]
        pos += 2 + len(extra)
        if has_delta:
            dx_ref, dd_ref, dgate_ref, dg_ref, dsc_ref, dsh_ref = refs[pos:pos + 6]
        else:
            dx_ref, dg_ref, dsc_ref, dsh_ref = refs[pos:pos + 4]
        acc_ref = refs[-1]
        i, k = pl.program_id(0), pl.program_id(1)

        def norm_bwd(dh_v):
            g_v, one_sc = g_ref[...], 1.0 + sc_ref[0]
            x_v = x_ref[...]
            rinv = _rinv(x_v)
            xhat = x_v * rinv
            dx = dxo_ref[...] + _rms_bwd(dh_v * (g_v * one_sc), xhat, rinv)
            dx_ref[...] = dx

            @pl.when(i == 0)
            def _():
                dg_ref[...] = jnp.zeros_like(dg_ref)

            @pl.when(i % per_seq_tiles == 0)
            def _():
                dsc_ref[...] = jnp.zeros_like(dsc_ref)
                dsh_ref[...] = jnp.zeros_like(dsh_ref)
                if has_delta:
                    dgate_ref[...] = jnp.zeros_like(dgate_ref)

            t_sum = _fold(dh_v * xhat)
            dg_ref[...] += t_sum * one_sc
            dsc_ref[0] += t_sum * g_v
            dsh_ref[0] += _fold(dh_v)
            if has_delta:
                dd_ref[...] = _b(dx * gate_ref[0])
                dgate_ref[0] += _fold(dx * delta_ref[...])

        def step(a_ref):
            p = lax.dot_general(a_ref[...], b_ref[...], _NN, preferred_element_type=F32)
            if nk == 1:
                norm_bwd(p)
            else:
                @pl.when(k == 0)
                def _():
                    acc_ref[...] = p

                @pl.when((k > 0) & (k < nk - 1))
                def _():
                    acc_ref[...] += p

                @pl.when(k == nk - 1)
                def _():
                    norm_bwd(acc_ref[...] + p)

        if npc == 1:
            step(a_refs[0])
        else:
            for pc in range(npc):
                pl.when((k >= pc * per) & (k < (pc + 1) * per))(functools.partial(step, a_refs[pc]))

    def a_map(pc, i, k):
        return (i, jnp.clip(k - pc * per, 0, per - 1) if npc > 1 else k)

    row = pl.BlockSpec((tm, d), lambda i, k: (i, 0))
    per_seq = pl.BlockSpec((1, 1, d), lambda i, k: (i // per_seq_tiles, 0, 0))
    vec = pl.BlockSpec((1, d), lambda i, k: (0, 0))
    shp = lambda *s, dt=F32: jax.ShapeDtypeStruct(s, dt)
    in_specs = [pl.BlockSpec((tm, tk), functools.partial(a_map, pc)) for pc in range(npc)]
    in_specs += [pl.BlockSpec((tk, d), lambda i, k: (k, 0)), row, row]
    operands = [*pieces, b, dxo, x]
    if has_delta:
        in_specs += [row, per_seq]
        operands += [delta, gate]
    in_specs += [vec, per_seq] + [ANY] * len(extra)
    operands += [g, sc, *extra]
    if has_delta:
        out_specs = [row, row, per_seq, vec, per_seq, per_seq]
        out_shape = [shp(t, d), shp(t, d, dt=BF16), shp(nseq, 1, d), shp(1, d), shp(nseq, 1, d), shp(nseq, 1, d)]
    else:
        out_specs = [row, vec, per_seq, per_seq]
        out_shape = [shp(t, d), shp(1, d), shp(nseq, 1, d), shp(nseq, 1, d)]
    outs = pl.pallas_call(
        body, name=name, grid=(t // tm, nk),
        in_specs=in_specs, out_specs=out_specs, out_shape=out_shape,
        scratch_shapes=[pltpu.VMEM((tm, d), F32)],
        compiler_params=_cp("arbitrary", "arbitrary"),
    )(*operands)
    if has_delta:
        return tuple(outs)
    dx, dg, dsc, dsh = outs
    return dx, None, None, dg, dsc, dsh


def _matmul_normfwd(a, b, xin, gate, g, sc, sh, *, nseq, name):
    t, k_dim = a.shape
    d = b.shape[1]
    assert b.shape[0] == k_dim and k_dim <= MATMUL_TILE_CAP
    seq = t // nseq
    tm = min(NORM_TM, seq)
    per_seq_tiles = seq // tm

    def body(a_ref, b_ref, xin_ref, gate_ref, g_ref, sc_ref, sh_ref, dl_ref, x_ref, h_ref):
        dl = lax.dot_general(a_ref[...], b_ref[...], _NN, preferred_element_type=F32)
        dl_ref[...] = dl
        x = xin_ref[...] + gate_ref[0] * dl
        x_ref[...] = x
        h_ref[...] = _b(x * _rinv(x) * (g_ref[...] * (1.0 + sc_ref[0])) + sh_ref[0])

    row = pl.BlockSpec((tm, d), lambda i: (i, 0))
    per_seq = pl.BlockSpec((1, 1, d), lambda i: (i // per_seq_tiles, 0, 0))
    return pl.pallas_call(
        body, name=name, grid=(t // tm,),
        in_specs=[pl.BlockSpec((tm, k_dim), lambda i: (i, 0)), pl.BlockSpec((k_dim, d), lambda i: (0, 0)),
                  row, per_seq, pl.BlockSpec((1, d), lambda i: (0, 0)), per_seq, per_seq],
        out_specs=[row, row, row],
        out_shape=[jax.ShapeDtypeStruct((t, d), F32), jax.ShapeDtypeStruct((t, d), F32), jax.ShapeDtypeStruct((t, d), BF16)],
        compiler_params=_cp("parallel"),
    )(a, b, xin, gate, g, sc, sh)


def _matmul_loss(a, b, xin, gate, fg, target, *, nseq, name):
    t, k_dim = a.shape
    d = b.shape[1]
    assert b.shape[0] == k_dim and k_dim <= MATMUL_TILE_CAP
    seq = t // nseq
    tm = min(NORM_TM, seq)
    per_seq_tiles = seq // tm

    def body(a_ref, b_ref, xin_ref, gate_ref, fg_ref, tgt_ref, dl_ref, loss_ref, dx_ref, dd_ref, dgate_ref, dfg_ref):
        i = pl.program_id(0)
        fg_v, gate_v = fg_ref[...], gate_ref[0]
        dl = lax.dot_general(a_ref[...], b_ref[...], _NN, preferred_element_type=F32)
        dl_ref[...] = dl
        x = xin_ref[...] + gate_v * dl
        rinv = _rinv(x)
        xhat = x * rinv
        err = xhat * fg_v - tgt_ref[...]
        dx = _rms_bwd(err * fg_v * (1.0 / d), xhat, rinv)
        dx_ref[...] = dx
        dd_ref[...] = _b(dx * gate_v)

        @pl.when(i == 0)
        def _():
            loss_ref[...] = jnp.zeros_like(loss_ref)
            dfg_ref[...] = jnp.zeros_like(dfg_ref)

        @pl.when(i % per_seq_tiles == 0)
        def _():
            dgate_ref[...] = jnp.zeros_like(dgate_ref)

        loss_ref[...] += jnp.sum(err * err) * (0.5 / d)
        dfg_ref[...] += _fold(err * xhat) * (1.0 / d)
        dgate_ref[0] += _fold(dx * dl)

    row = pl.BlockSpec((tm, d), lambda i: (i, 0))
    per_seq = pl.BlockSpec((1, 1, d), lambda i: (i // per_seq_tiles, 0, 0))
    vec = pl.BlockSpec((1, d), lambda i: (0, 0))
    return pl.pallas_call(
        body, name=name, grid=(t // tm,),
        in_specs=[pl.BlockSpec((tm, k_dim), lambda i: (i, 0)), pl.BlockSpec((k_dim, d), lambda i: (0, 0)),
                  row, per_seq, vec, row],
        out_specs=[row, pl.BlockSpec((1, 128), lambda i: (0, 0)), row, row, per_seq, vec],
        out_shape=[jax.ShapeDtypeStruct((t, d), F32), jax.ShapeDtypeStruct((1, 128), F32), jax.ShapeDtypeStruct((t, d), F32),
                   jax.ShapeDtypeStruct((t, d), BF16), jax.ShapeDtypeStruct((nseq, 1, d), F32),
                   jax.ShapeDtypeStruct((1, d), F32)],
        compiler_params=_cp("arbitrary"),
    )(a, b, xin, gate, fg, target)


CONV_TC = 256
CONV_LANES = 128
CONV_ROWS = 64
CONV_HALO = 8


def _conv_slabs(seq, fn):
    def step(i, carry):
        r0 = pl.multiple_of(i * CONV_ROWS, CONV_ROWS)
        for h in range(CONV_TC // CONV_LANES):
            fn(r0, slice(h * CONV_LANES, (h + 1) * CONV_LANES))
        return carry

    lax.fori_loop(0, seq // CONV_ROWS, step, 0)


def _slab(ref, r0, cols, seq):
    after = ref[pl.ds(pl.multiple_of(jnp.minimum(r0 + CONV_ROWS, seq - CONV_HALO), CONV_HALO), CONV_HALO), cols]
    return jnp.concatenate([ref[pl.ds(r0, CONV_ROWS), cols], jnp.where(r0 + CONV_ROWS < seq, after, 0.0)], axis=0)


def _conv_block(x, w_ref, b_ref):
    kw = w_ref.shape[0]
    rows = lax.broadcasted_iota(jnp.int32, x.shape, 0)
    y = b_ref[...] + w_ref[kw - 1:kw, :] * x
    for j in range(1, kw):
        y = y + w_ref[kw - 1 - j:kw - j, :] * jnp.where(rows >= j, pltpu.roll(x, j, 0), 0.0)
    return y


def _conv_block_bwd(dy, x, w_ref, dw_ref, db_ref):
    kw = w_ref.shape[0]
    n = x.shape[0]
    rows = lax.broadcasted_iota(jnp.int32, x.shape, 0)
    dx = w_ref[kw - 1:kw, :] * dy
    dw_ref[kw - 1:kw, :] += jnp.sum(dy * x, axis=0, keepdims=True)
    for j in range(1, kw):
        dy_j = jnp.where(rows < n - j, pltpu.roll(dy, n - j, 0), 0.0)
        dx = dx + w_ref[kw - 1 - j:kw - j, :] * dy_j
        dw_ref[kw - 1 - j:kw - j, :] += jnp.sum(dy_j * x, axis=0, keepdims=True)
    db_ref[...] += jnp.sum(dy, axis=0, keepdims=True)
    return dx


def _conv_bwd(dy_ext, x, w_ref, dw_ref, db_ref, cols):
    kw = w_ref.shape[0]
    n = dy_ext.shape[0]
    dy = dy_ext[:CONV_ROWS]
    dx = w_ref[kw - 1:kw, cols] * dy
    dw_ref[kw - 1:kw, cols] += jnp.sum(dy * x, axis=0, keepdims=True)
    for j in range(1, kw):
        dy_j = pltpu.roll(dy_ext, n - j, 0)[:CONV_ROWS]
        dx = dx + w_ref[kw - 1 - j:kw - j, cols] * dy_j
        dw_ref[kw - 1 - j:kw - j, cols] += jnp.sum(dy_j * x, axis=0, keepdims=True)
    db_ref[:, cols] += jnp.sum(dy, axis=0, keepdims=True)
    return dx


def _dsilu(pre):
    sg = jax.nn.sigmoid(pre)
    return pre * sg, sg * (1.0 + pre * (1.0 - sg))


def _ssd_conv_fwd(proj, w, b, *, nseq):
    t = proj.shape[0]
    seq = t // nseq
    nb = CONV_DIM // CONV_TC
    off = COL_XBC // CONV_TC

    def body(x_ref, w_ref, b_ref, o_ref, pre_ref):
        pre = _conv_block(x_ref[...], w_ref, b_ref)
        pre_ref[...] = pre
        o_ref[...] = _silu(pre)

    col = pl.BlockSpec((seq, CONV_TC), lambda j, s: (s, j))
    return pl.pallas_call(
        body, name="ssd_conv_fwd", grid=(nb, nseq),
        in_specs=[pl.BlockSpec((seq, CONV_TC), lambda j, s: (s, off + j)),
                  pl.BlockSpec((SSD_CONV, CONV_TC), lambda j, s: (0, j)),
                  pl.BlockSpec((1, CONV_TC), lambda j, s: (0, j))],
        out_specs=[col, col],
        out_shape=[jax.ShapeDtypeStruct((t, CONV_DIM), F32)] * 2,
        compiler_params=_cp("parallel", "parallel"),
    )(proj, w, b)


def _ssd_conv_bwd(dact, pre, proj, w, dproj, *, nseq):
    t = proj.shape[0]
    seq = t // nseq
    nb = CONV_DIM // CONV_TC
    off = COL_XBC // CONV_TC

    def body(da_ref, pre_ref, x_ref, w_ref, dproj_ref, dx_ref, dw_ref, db_ref):
        del dproj_ref

        @pl.when(pl.program_id(1) == 0)
        def _():
            dw_ref[...] = jnp.zeros_like(dw_ref)
            db_ref[...] = jnp.zeros_like(db_ref)

        def slab(r0, cols):
            _, dsilu = _dsilu(_slab(pre_ref, r0, cols, seq))
            dpre_ext = _slab(da_ref, r0, cols, seq) * dsilu
            x = x_ref[pl.ds(r0, CONV_ROWS), cols]
            dx_ref[pl.ds(r0, CONV_ROWS), cols] = _b(_conv_bwd(dpre_ext, x, w_ref, dw_ref, db_ref, cols))

        _conv_slabs(seq, slab)

    return pl.pallas_call(
        body, name="ssd_conv_bwd", grid=(nb, nseq),
        in_specs=[pl.BlockSpec((seq, CONV_TC), lambda j, s: (s, j)),
                  pl.BlockSpec((seq, CONV_TC), lambda j, s: (s, j)),
                  pl.BlockSpec((seq, CONV_TC), lambda j, s: (s, off + j)),
                  pl.BlockSpec((SSD_CONV, CONV_TC), lambda j, s: (0, j)),
                  ANY],
        out_specs=[pl.BlockSpec((seq, CONV_TC), lambda j, s: (s, off + j)),
                   pl.BlockSpec((SSD_CONV, CONV_TC), lambda j, s: (0, j)),
                   pl.BlockSpec((1, CONV_TC), lambda j, s: (0, j))],
        out_shape=[jax.ShapeDtypeStruct(dproj.shape, dproj.dtype), jax.ShapeDtypeStruct((SSD_CONV, CONV_DIM), F32),
                   jax.ShapeDtypeStruct((1, CONV_DIM), F32)],
        input_output_aliases={4: 0},
        compiler_params=_cp("parallel", "arbitrary"),
    )(dact, pre, proj, w, dproj)


def _ffn_act_fwd(up, w, b, *, nseq):
    t = up.shape[0]
    seq = t // nseq
    nb = D_FF // CONV_TC

    def body(g_ref, v_ref, w_ref, b_ref, o_ref):
        o_ref[...] = _b(_silu(_conv_block(g_ref[...].astype(F32), w_ref, b_ref)) * v_ref[...].astype(F32))

    col = pl.BlockSpec((seq, CONV_TC), lambda j, s: (s, j))
    return pl.pallas_call(
        body, name="ffn_act_fwd", grid=(nb, nseq),
        in_specs=[col,
                  pl.BlockSpec((seq, CONV_TC), lambda j, s: (s, nb + j)),
                  pl.BlockSpec((FF_CONV, CONV_TC), lambda j, s: (0, j)),
                  pl.BlockSpec((1, CONV_TC), lambda j, s: (0, j))],
        out_specs=col,
        out_shape=jax.ShapeDtypeStruct((t, D_FF), BF16),
        compiler_params=_cp("parallel", "parallel"),
    )(up, up, w, b)


def _ffn_act_bwd(dact, up, w, b, *, nseq):
    t = up.shape[0]
    seq = t // nseq
    nb = D_FF // CONV_TC

    def body(da_ref, g_ref, v_ref, w_ref, b_ref, dg_ref, dv_ref, dw_ref, db_ref):
        @pl.when(pl.program_id(1) == 0)
        def _():
            dw_ref[...] = jnp.zeros_like(dw_ref)
            db_ref[...] = jnp.zeros_like(db_ref)

        gate = g_ref[...].astype(F32)
        silu, dsilu = _dsilu(_conv_block(gate, w_ref, b_ref))
        da = da_ref[...].astype(F32)
        dv_ref[...] = _b(da * silu)
        dg_ref[...] = _b(_conv_block_bwd(da * v_ref[...].astype(F32) * dsilu, gate, w_ref, dw_ref, db_ref))

    col = pl.BlockSpec((seq, CONV_TC), lambda j, s: (s, j))
    return pl.pallas_call(
        body, name="ffn_act_bwd", grid=(nb, nseq),
        in_specs=[col, col,
                  pl.BlockSpec((seq, CONV_TC), lambda j, s: (s, nb + j)),
                  pl.BlockSpec((FF_CONV, CONV_TC), lambda j, s: (0, j)),
                  pl.BlockSpec((1, CONV_TC), lambda j, s: (0, j))],
        out_specs=[col, col,
                   pl.BlockSpec((FF_CONV, CONV_TC), lambda j, s: (0, j)),
                   pl.BlockSpec((1, CONV_TC), lambda j, s: (0, j))],
        out_shape=[jax.ShapeDtypeStruct((t, D_FF), BF16), jax.ShapeDtypeStruct((t, D_FF), BF16),
                   jax.ShapeDtypeStruct((FF_CONV, D_FF), F32), jax.ShapeDtypeStruct((1, D_FF), F32)],
        compiler_params=_cp("parallel", "arbitrary"),
    )(dact, up, up, w, b)


SSD_PAIRS = SSD_HEADS // 2
PAIR_W = 2 * SSD_HEAD_DIM
PAIRS_PER_GROUP = SSD_PAIRS // SSD_GROUPS


def _ssd_chunk(xs, bg, cg, dtr, z, hp, dtb, alog, dskip, ng):
    n = dtr.shape[0]
    dt = _softplus(dtr + dtb)
    cs = _cumsum_rows(dt * (-jnp.exp(alog)))
    cs_t = _transpose(cs)
    lane = lax.broadcasted_iota(jnp.int32, (1, SSD_HEADS), 1)
    sub = lax.broadcasted_iota(jnp.int32, (SSD_HEADS, 1), 0)
    row = lax.broadcasted_iota(jnp.int32, (n, 1), 0)
    causal = lax.broadcasted_iota(jnp.int32, (n, n), 0) >= lax.broadcasted_iota(jnp.int32, (n, n), 1)
    future = jnp.where(causal, 0.0, -1e30)
    first = lax.broadcasted_iota(jnp.int32, (1, PAIR_W), 1) < SSD_HEAD_DIM
    first_rows = lax.broadcasted_iota(jnp.int32, (PAIR_W, 1), 0) < SSD_HEAD_DIM
    first_f = first.astype(F32)
    cb = [_bdot_nt(cg[g], bg[g]) for g in range(SSD_GROUPS)]
    ys, hn = [], []
    for p in range(SSD_PAIRS):
        g = p // PAIRS_PER_GROUP
        col, decay, last = [], [], []
        for h in (2 * p, 2 * p + 1):
            oh = (lane == h).astype(F32)
            cs_h = jnp.sum(cs * oh, axis=1, keepdims=True)
            cs_row = jnp.sum(cs_t * (sub == h).astype(F32), axis=0, keepdims=True)
            col.append((jnp.sum(dt * oh, axis=1, keepdims=True), cs_h, jnp.sum(dskip * oh, axis=1, keepdims=True)))
            last.append(jnp.sum(jnp.where(row == n - 1, cs_h, 0.0), axis=0, keepdims=True))
            decay.append(jnp.exp(cs_h - cs_row + future))
        pair = lambda a, b: jnp.where(first, a, b)
        dt_p = pair(col[0][0], col[1][0])
        cs_p = pair(col[0][1], col[1][1])
        last_p = pair(last[0], last[1])
        xc = xs[p] * dt_p
        y = _bdot(cb[g] * decay[0], xc * first_f) + _bdot(cb[g] * decay[1], xc * (1.0 - first_f))
        y = y + _bdot_nt(cg[g], hp[p]) * jnp.exp(cs_p)
        y = y + pair(col[0][2], col[1][2]) * xs[p]
        keep = jnp.where(first_rows, jnp.exp(last[0]), jnp.exp(last[1]))
        hn.append(keep * hp[p] + _bdot_tn(xc * jnp.exp(last_p - cs_p), bg[g]))
        ys.append(y * _silu(z[p]))
    outs = []
    for g in range(SSD_GROUPS):
        ps = range(g * PAIRS_PER_GROUP, (g + 1) * PAIRS_PER_GROUP)
        ms = sum(jnp.sum(ys[p] * ys[p], axis=1, keepdims=True) for p in ps) * (1.0 / GROUP_WIDTH)
        r = lax.rsqrt(ms + EPS)
        outs += [ys[p] * r * ng[p] for p in ps]
    return outs, hn


def _hslices(ref, width, count, base=0):
    return [ref[:, base + k * width: base + (k + 1) * width] for k in range(count)]


def _ssd_load(xbc_ref, z_ref, dt_ref, ng_ref):
    xs = _hslices(xbc_ref, PAIR_W, SSD_PAIRS)
    bg = _hslices(xbc_ref, D_STATE, SSD_GROUPS, D_SSD)
    cg = _hslices(xbc_ref, D_STATE, SSD_GROUPS, D_SSD + SSD_GROUPS * D_STATE)
    z = _hslices(z_ref, PAIR_W, SSD_PAIRS)
    ng = _hslices(ng_ref, PAIR_W, SSD_PAIRS)
    return xs, bg, cg, dt_ref[:, 0:SSD_HEADS], z, ng


def _ssd_specs(nch):
    rowi = lambda s, c: s * nch + c
    return [pl.BlockSpec((CHUNK, CONV_DIM), lambda s, c: (rowi(s, c), 0)),
            pl.BlockSpec((CHUNK, D_SSD), lambda s, c: (rowi(s, c), COL_Z // D_SSD)),
            pl.BlockSpec((CHUNK, 128), lambda s, c: (rowi(s, c), COL_DT // 128)),
            pl.BlockSpec((1, SSD_HEADS), lambda s, c: (0, 0)),
            pl.BlockSpec((1, SSD_HEADS), lambda s, c: (0, 0)),
            pl.BlockSpec((1, SSD_HEADS), lambda s, c: (0, 0)),
            pl.BlockSpec((1, D_SSD), lambda s, c: (0, 0))]


def _ssd_fwd(xbc, proj, dtb, alog, dskip, ng, *, nseq):
    t = proj.shape[0]
    nch = t // nseq // CHUNK
    hd = PAIR_W

    def body(xbc_ref, z_ref, dt_ref, dtb_ref, alog_ref, dsk_ref, ng_ref, y_ref, hp_ref, h_ref):
        @pl.when(pl.program_id(1) == 0)
        def _():
            h_ref[...] = jnp.zeros_like(h_ref)

        xs, bg, cg, dtr, z, ngs = _ssd_load(xbc_ref, z_ref, dt_ref, ng_ref)
        hp_ref[0] = h_ref[...]
        hp = [h_ref[h * hd:(h + 1) * hd, :] for h in range(SSD_PAIRS)]
        outs, hn = _ssd_chunk(xs, bg, cg, dtr, z, hp, dtb_ref[...], alog_ref[...], dsk_ref[...], ngs)
        for h in range(SSD_PAIRS):
            y_ref[:, h * hd:(h + 1) * hd] = _b(outs[h])
            h_ref[h * hd:(h + 1) * hd, :] = hn[h]

    return pl.pallas_call(
        body, name="ssd_fwd", grid=(nseq, nch),
        in_specs=_ssd_specs(nch),
        out_specs=[pl.BlockSpec((CHUNK, D_SSD), lambda s, c: (s * nch + c, 0)),
                   pl.BlockSpec((1, D_SSD, D_STATE), lambda s, c: (s * nch + c, 0, 0))],
        out_shape=[jax.ShapeDtypeStruct((t, D_SSD + D_GM), BF16),
                   jax.ShapeDtypeStruct((t // CHUNK, D_SSD, D_STATE), F32)],
        scratch_shapes=[pltpu.VMEM((D_SSD, D_STATE), F32)],
        compiler_params=_cp("arbitrary", "arbitrary"),
    )(xbc, proj, proj, dtb, alog, dskip, ng)


def _ssd_bwd(dy, xbc, proj, hprev, dtb, alog, dskip, ng, *, nseq):
    t = proj.shape[0]
    nch = t // nseq // CHUNK
    hd = PAIR_W
    rev = lambda s, c: s * nch + (nch - 1 - c)

    def body(dy_ref, xbc_ref, z_ref, dt_ref, hp_ref, dtb_ref, alog_ref, dsk_ref, ng_ref,
             dxbc_ref, dproj_ref, ddtb_ref, dalog_ref, ddsk_ref, dng_ref, dh_ref):
        first = (pl.program_id(0) == 0) & (pl.program_id(1) == 0)

        @pl.when(pl.program_id(1) == 0)
        def _():
            dh_ref[...] = jnp.zeros_like(dh_ref)

        @pl.when(first)
        def _():
            ddtb_ref[...] = jnp.zeros_like(ddtb_ref)
            dalog_ref[...] = jnp.zeros_like(dalog_ref)
            ddsk_ref[...] = jnp.zeros_like(ddsk_ref)
            dng_ref[...] = jnp.zeros_like(dng_ref)

        xs, bg, cg, dtr, z, ngs = _ssd_load(xbc_ref, z_ref, dt_ref, ng_ref)
        hp = [hp_ref[0, h * hd:(h + 1) * hd, :] for h in range(SSD_PAIRS)]
        _, vjp = jax.vjp(_ssd_chunk, xs, bg, cg, dtr, z, hp, dtb_ref[...], alog_ref[...], dsk_ref[...], ngs)
        douts = [dy_ref[:, h * hd:(h + 1) * hd] for h in range(SSD_PAIRS)]
        dhn = [dh_ref[h * hd:(h + 1) * hd, :] for h in range(SSD_PAIRS)]
        dxs, dbg, dcg, ddtr, dz, dhp, ddtb, dalog, ddsk, dngs = vjp((douts, dhn))
        dproj_ref[:, :COL_Z] = jnp.zeros((CHUNK, COL_Z), BF16)
        dproj_ref[:, COL_XBC:] = jnp.zeros((CHUNK, N_INP - COL_XBC), BF16)
        for h in range(SSD_PAIRS):
            dxbc_ref[:, h * hd:(h + 1) * hd] = dxs[h]
            dproj_ref[:, COL_Z + h * hd: COL_Z + (h + 1) * hd] = _b(dz[h])
            dh_ref[h * hd:(h + 1) * hd, :] = dhp[h]
            dng_ref[:, h * hd:(h + 1) * hd] += dngs[h]
        for g in range(SSD_GROUPS):
            dxbc_ref[:, D_SSD + g * D_STATE: D_SSD + (g + 1) * D_STATE] = dbg[g]
            dxbc_ref[:, D_SSD + (SSD_GROUPS + g) * D_STATE: D_SSD + (SSD_GROUPS + g + 1) * D_STATE] = dcg[g]
        dproj_ref[:, COL_DT:COL_DT + SSD_HEADS] = _b(ddtr)
        ddtb_ref[...] += ddtb
        dalog_ref[...] += dalog
        ddsk_ref[...] += ddsk

    small = pl.BlockSpec((1, SSD_HEADS), lambda s, c: (0, 0))
    return pl.pallas_call(
        body, name="ssd_bwd", grid=(nseq, nch),
        in_specs=[pl.BlockSpec((CHUNK, D_SSD), lambda s, c: (rev(s, c), 0)),
                  pl.BlockSpec((CHUNK, CONV_DIM), lambda s, c: (rev(s, c), 0)),
                  pl.BlockSpec((CHUNK, D_SSD), lambda s, c: (rev(s, c), COL_Z // D_SSD)),
                  pl.BlockSpec((CHUNK, 128), lambda s, c: (rev(s, c), COL_DT // 128)),
                  pl.BlockSpec((1, D_SSD, D_STATE), lambda s, c: (rev(s, c), 0, 0)),
                  small, small, small,
                  pl.BlockSpec((1, D_SSD), lambda s, c: (0, 0))],
        out_specs=[pl.BlockSpec((CHUNK, CONV_DIM), lambda s, c: (rev(s, c), 0)),
                   pl.BlockSpec((CHUNK, N_INP), lambda s, c: (rev(s, c), 0)),
                   small, small, small,
                   pl.BlockSpec((1, D_SSD), lambda s, c: (0, 0))],
        out_shape=[jax.ShapeDtypeStruct((t, CONV_DIM), F32), jax.ShapeDtypeStruct((t, N_INP), BF16),
                   jax.ShapeDtypeStruct((1, SSD_HEADS), F32), jax.ShapeDtypeStruct((1, SSD_HEADS), F32),
                   jax.ShapeDtypeStruct((1, SSD_HEADS), F32), jax.ShapeDtypeStruct((1, D_SSD), F32)],
        scratch_shapes=[pltpu.VMEM((D_SSD, D_STATE), F32)],
        compiler_params=_cp("arbitrary", "arbitrary"),
    )(dy, xbc, proj, proj, hprev, dtb, alog, dskip, ng)


def _gmlp_chunk(gu, gv, ws, bs_cols, vg, og):
    n = gu[0].shape[0]
    mask = _tri(n, True)
    au = [_gelu(t) for t in gu]
    av = [_gelu(t) for t in gv]
    r = lax.rsqrt(sum(jnp.sum(t * t, axis=1, keepdims=True) for t in av) * (1.0 / D_GM) + EPS)
    p = []
    for h in range(GM_HEADS):
        sv = _bdot(ws[h] * mask, av[h] * r * vg[h]) + bs_cols[h]
        p.append(au[h] * sv)
    r2 = lax.rsqrt(sum(jnp.sum(t * t, axis=1, keepdims=True) for t in p) * (1.0 / D_GM) + EPS)
    return [p[h] * r2 * og[h] for h in range(GM_HEADS)]


def _gmlp_load(u_ref, v_ref, ws_ref, bst_ref, vg_ref, og_ref):
    gu = _hslices(u_ref, GM_HEAD_DIM, GM_HEADS)
    gv = _hslices(v_ref, GM_HEAD_DIM, GM_HEADS)
    ws = [ws_ref[h] for h in range(GM_HEADS)]
    bs_cols = [bst_ref[:, h:h + 1] for h in range(GM_HEADS)]
    return gu, gv, ws, bs_cols, _hslices(vg_ref, GM_HEAD_DIM, GM_HEADS), _hslices(og_ref, GM_HEAD_DIM, GM_HEADS)


def _gmlp_specs():
    return [pl.BlockSpec((CHUNK, D_GM), lambda i: (i, COL_U // D_GM)),
            pl.BlockSpec((CHUNK, D_GM), lambda i: (i, COL_V // D_GM)),
            pl.BlockSpec((GM_HEADS, CHUNK, CHUNK), lambda i: (0, 0, 0)),
            pl.BlockSpec((CHUNK, GM_HEADS), lambda i: (0, 0)),
            pl.BlockSpec((1, D_GM), lambda i: (0, 0)),
            pl.BlockSpec((1, D_GM), lambda i: (0, 0))]


def _gmlp_fwd(proj, ycat, ws, bst, vg, og):
    t = proj.shape[0]

    def body(u_ref, v_ref, ws_ref, bst_ref, vg_ref, og_ref, ycat_ref, o_ref):
        del ycat_ref
        outs = _gmlp_chunk(*_gmlp_load(u_ref, v_ref, ws_ref, bst_ref, vg_ref, og_ref))
        for h in range(GM_HEADS):
            o_ref[:, h * GM_HEAD_DIM:(h + 1) * GM_HEAD_DIM] = _b(outs[h])

    return pl.pallas_call(
        body, name="gmlp_fwd", grid=(t // CHUNK,),
        in_specs=_gmlp_specs() + [ANY],
        out_specs=pl.BlockSpec((CHUNK, D_GM), lambda i: (i, D_SSD // D_GM)),
        out_shape=jax.ShapeDtypeStruct(ycat.shape, ycat.dtype),
        input_output_aliases={6: 0},
        compiler_params=_cp("parallel"),
    )(proj, proj, ws, bst, vg, og, ycat)


def _gmlp_bwd(dy, proj, ws, bst, vg, og, dproj):
    t = proj.shape[0]
    w = GM_HEAD_DIM

    def body(dy_ref, u_ref, v_ref, ws_ref, bst_ref, vg_ref, og_ref, dproj_ref,
             dgm_ref, dws_ref, dbst_ref, dvg_ref, dog_ref):
        del dproj_ref

        @pl.when(pl.program_id(0) == 0)
        def _():
            dws_ref[...] = jnp.zeros_like(dws_ref)
            dbst_ref[...] = jnp.zeros_like(dbst_ref)
            dvg_ref[...] = jnp.zeros_like(dvg_ref)
            dog_ref[...] = jnp.zeros_like(dog_ref)

        _, vjp = jax.vjp(_gmlp_chunk, *_gmlp_load(u_ref, v_ref, ws_ref, bst_ref, vg_ref, og_ref))
        dgu, dgv, dws, dbs, dvg, dog = vjp(_hslices(dy_ref, w, GM_HEADS))
        for h in range(GM_HEADS):
            dgm_ref[:, h * w:(h + 1) * w] = _b(dgu[h])
            dgm_ref[:, D_GM + h * w: D_GM + (h + 1) * w] = _b(dgv[h])
            dws_ref[h] += dws[h]
            dbst_ref[:, h:h + 1] += dbs[h]
            dvg_ref[:, h * w:(h + 1) * w] += dvg[h]
            dog_ref[:, h * w:(h + 1) * w] += dog[h]

    return pl.pallas_call(
        body, name="gmlp_bwd", grid=(t // CHUNK,),
        in_specs=[pl.BlockSpec((CHUNK, D_GM), lambda i: (i, 1))] + _gmlp_specs() + [ANY],
        out_specs=[pl.BlockSpec((CHUNK, 2 * D_GM), lambda i: (i, COL_U // (2 * D_GM))),
                   pl.BlockSpec((GM_HEADS, CHUNK, CHUNK), lambda i: (0, 0, 0)),
                   pl.BlockSpec((CHUNK, GM_HEADS), lambda i: (0, 0)),
                   pl.BlockSpec((1, D_GM), lambda i: (0, 0)),
                   pl.BlockSpec((1, D_GM), lambda i: (0, 0))],
        out_shape=[jax.ShapeDtypeStruct(dproj.shape, dproj.dtype), jax.ShapeDtypeStruct((GM_HEADS, CHUNK, CHUNK), F32),
                   jax.ShapeDtypeStruct((CHUNK, GM_HEADS), F32), jax.ShapeDtypeStruct((1, D_GM), F32),
                   jax.ShapeDtypeStruct((1, D_GM), F32)],
        input_output_aliases={7: 0},
        compiler_params=_cp("arbitrary"),
    )(dy, proj, proj, ws, bst, vg, og, dproj)


def _local_step(x, target, mods, lw, final_g, *, nseq, big_w, grad_sink, small_sink):
    saved = []
    x0, delta, gate = x, None, None
    h1 = _normmod_fwd(x, lw[0]["norm1_g"], mods[0][1], mods[0][0], nseq=nseq, name="norm1_fwd_0")
    for l in range(DEPTH):
        w = lw[l]
        sh1, sc1, g1, sh2, sc2, g2 = mods[l]
        w_in = big_w(l, "w_in", h1)
        proj = _matmul(h1, w_in, tb=True, name=f"mm_in_{l}")
        xbc, xbc_pre = _ssd_conv_fwd(proj, w["ssd_conv_w"], w["ssd_conv_b"], nseq=nseq)
        ycat, hprev = _ssd_fwd(xbc, proj, w["ssd_dt_bias"], w["ssd_a_log"], w["ssd_d"], w["ssd_norm_g"], nseq=nseq)
        ycat = _gmlp_fwd(proj, ycat, w["gm_ws"], w["gm_bst"], w["gm_vnorm_g"], w["gm_out_g"])
        w_out = big_w(l, "w_out", ycat)
        mix, x1, h2 = _matmul_normfwd(ycat, w_out, x0, g1, w["norm2_g"], sc2, sh2, nseq=nseq, name=f"mm_out_{l}")
        ff_up = big_w(l, "ff_up", h2)
        up = _matmul(h2, ff_up, tb=True, name=f"mm_up_{l}", out_dtype=BF16)
        act = _ffn_act_fwd(up, w["ff_conv_w"], w["ff_conv_b"], nseq=nseq)
        ff_down = big_w(l, "ff_down", act)
        sv = dict(x0=x0, xin_delta=delta, xin_gate=gate, h1=h1, proj=proj, xbc=xbc, xbc_pre=xbc_pre, hprev=hprev,
                  ycat=ycat, mix=mix, x1=x1, h2=h2, up=up, act=act,
                  w_in=w_in, w_out=w_out, ff_up=ff_up, ff_down=ff_down)
        if l + 1 < DEPTH:
            nsh1, nsc1 = mods[l + 1][0], mods[l + 1][1]
            dn, x0, h1 = _matmul_normfwd(act, ff_down, x1, g2, lw[l + 1]["norm1_g"], nsc1, nsh1, nseq=nseq,
                                         name=f"mm_down_{l}")
        else:
            dn, loss, dx, ddelta, dgate, dfg = _matmul_loss(act, ff_down, x1, g2, final_g, target, nseq=nseq,
                                                            name=f"mm_down_{l}")
        saved.append(dict(sv, dn=dn))
        delta, gate = dn, g2

    small, dmods = [None] * DEPTH, [None] * DEPTH
    for l in reversed(range(DEPTH)):
        w, sv = lw[l], saved[l]
        sh1, sc1, g1, sh2, sc2, g2 = mods[l]
        dg2 = dgate
        g_ff_down = _matmul(sv["act"], ddelta, ta=True, name=f"mm_down_dw_{l}", out_dtype=BF16)
        dact = _matmul(ddelta, sv["ff_down"], tb=True, name=f"mm_down_dx_{l}", out_dtype=BF16)
        dgate_ff, dval_ff, dfcw, dfcb = _ffn_act_bwd(dact, sv["up"], w["ff_conv_w"], w["ff_conv_b"], nseq=nseq)
        g_ff_up = _matmul([dgate_ff, dval_ff], sv["h2"], ta=True, name=f"mm_up_dw_{l}", out_dtype=BF16)
        dep = grad_sink(l, "ffn", dict(ff_down=g_ff_down, ff_up=g_ff_up), dval_ff)
        dx, dmix, dg1, dn2g, dsc2, dsh2 = _matmul_normbwd([dgate_ff, dval_ff], sv["ff_up"], dx, sv["x1"], sv["mix"], g1,
                                                          w["norm2_g"], sc2, nseq=nseq, name=f"mm_up_dx_{l}", dep=dep)
        g_w_out = _matmul(sv["ycat"], dmix, ta=True, name=f"mm_out_dw_{l}", out_dtype=BF16)
        dep = grad_sink(l, "w_out", dict(w_out=g_w_out), dmix)
        dycat = _matmul(dmix, sv["w_out"], tb=True, name=f"mm_out_dx_{l}", dep=dep)
        dxbc_act, dproj, ddtb, dalog, ddsk, dng = _ssd_bwd(dycat, sv["xbc"], sv["proj"], sv["hprev"], w["ssd_dt_bias"],
                                                          w["ssd_a_log"], w["ssd_d"], w["ssd_norm_g"], nseq=nseq)
        dproj, dscw, dscb = _ssd_conv_bwd(dxbc_act, sv["xbc_pre"], sv["proj"], w["ssd_conv_w"], dproj, nseq=nseq)
        dproj, dws, dbst, dvg, dog = _gmlp_bwd(dycat, sv["proj"], w["gm_ws"], w["gm_bst"], w["gm_vnorm_g"], w["gm_out_g"], dproj)
        early = dict(norm2_g=dn2g, ssd_norm_g=dng, gm_vnorm_g=dvg, gm_out_g=dog,
                     ssd_conv_w=dscw, ssd_conv_b=dscb, ff_conv_w=dfcw, ff_conv_b=dfcb,
                     ssd_dt_bias=ddtb, ssd_a_log=dalog, ssd_d=ddsk, gm_ws=dws, gm_bs=dbst.T)
        dep = small_sink(l, early, small, dmods, dfg, loss)
        g_w_in = _matmul(dproj, sv["h1"], ta=True, name=f"mm_in_dw_{l}", out_dtype=BF16, dep=dep)
        dep = grad_sink(l, "w_in", dict(w_in=g_w_in), dproj)
        dx, ddelta, dgate, dn1g, dsc1, dsh1 = _matmul_normbwd(dproj, sv["w_in"], dx, sv["x0"], sv["xin_delta"],
                                                              sv["xin_gate"], w["norm1_g"], sc1, nseq=nseq,
                                                              name=f"mm_in_dx_{l}", dep=dep)
        small[l] = dict(early, norm1_g=dn1g)
        dmods[l] = jnp.concatenate([dsh1, dsc1, dg1, dsh2, dsc2, dg2], axis=-1)[:, 0, :]
    return dx, small, dmods


def _all_gather(arrs, name, dep=None):
    n = len(arrs)
    extra = [] if dep is None else [dep]

    def body(*refs):
        ins, outs = refs[:n], refs[n + len(extra):2 * n + len(extra)]
        send_sems, recv_sems, local_sems = refs[2 * n + len(extra):]
        x, y, c = lax.axis_index("x"), lax.axis_index("y"), lax.axis_index("c")
        me, sibling = (x, y, c), (x, y, 1 - c)
        chips = [(1 - x, y), (x, 1 - y), (1 - x, 1 - y)]

        def copy(i, k, block, to, src=None):
            px, py, pc = block
            dst = outs[i].at[4 * px + 2 * py + pc]
            return pltpu.make_async_remote_copy(
                src_ref=dst if src is None else src, dst_ref=dst,
                send_sem=send_sems.at[7 * i + k], recv_sem=recv_sems.at[7 * i + k],
                device_id=to, device_id_type=MESH)

        mine = [pltpu.make_async_copy(ins[i], outs[i].at[4 * x + 2 * y + c], local_sems.at[i]) for i in range(n)]
        for cp in mine:
            cp.start()
        first = []
        for i in range(n):
            first.append(copy(i, 0, me, sibling, src=ins[i]))
            first += [copy(i, 1 + j, me, (*chip, c), src=ins[i]) for j, chip in enumerate(chips)]
        for cp in first:
            cp.start()
        passed = []
        for j, chip in enumerate(chips):
            for i in range(n):
                copy(i, 1 + j, (*chip, c), me).wait_recv()
                fwd = copy(i, 4 + j, (*chip, c), sibling)
                fwd.start()
                passed.append(fwd)
        for i in range(n):
            copy(i, 0, sibling, me).wait_recv()
            for j, chip in enumerate(chips):
                copy(i, 4 + j, (*chip, 1 - c), me).wait_recv()
        for cp in first + passed:
            cp.wait_send()
        for cp in mine:
            cp.wait()

    return pl.pallas_call(
        body, name=name,
        in_specs=[ANY] * (n + len(extra)), out_specs=[ANY] * n,
        out_shape=[jax.ShapeDtypeStruct((N_DEV,) + a.shape, a.dtype) for a in arrs],
        scratch_shapes=[pltpu.SemaphoreType.DMA((7 * n,)), pltpu.SemaphoreType.DMA((7 * n,)),
                        pltpu.SemaphoreType.DMA((n,))],
    )(*arrs, *extra)


HBM = pl.BlockSpec(memory_space=pltpu.HBM)
SEM = pl.BlockSpec(memory_space=pltpu.SEMAPHORE)
EFFECT = pltpu.SideEffectType.DATAFLOW_SIDE_EFFECTING


def _peer(k):
    x, y, c = lax.axis_index("x"), lax.axis_index("y"), lax.axis_index("c")
    return (1 - x if k & 4 else x, 1 - y if k & 2 else y, 1 - c if k & 1 else c)


ALL_PEERS = tuple(range(1, N_DEV))
OTHER_CHIPS = (2, 4, 6)


def _xc_copies(scatter, srcs, lands, send_sems, recv_sems, peers=ALL_PEERS):
    x, y, c = lax.axis_index("x"), lax.axis_index("y"), lax.axis_index("c")
    copies = []
    for i in range(len(srcs)):
        for k in (peers[i] if isinstance(peers[0], tuple) else peers):
            px, py, pc = _peer(k)
            src = srcs[i].at[4 * px + 2 * py + pc] if scatter else srcs[i]
            dst = lands[i].at[k - 1] if scatter else lands[i].at[4 * x + 2 * y + c]
            copies.append(pltpu.make_async_remote_copy(
                src_ref=src, dst_ref=dst, send_sem=send_sems[i].at[k - 1], recv_sem=recv_sems[i].at[k - 1],
                device_id=(px, py, pc), device_id_type=MESH))
    return copies


def _xc_own(scatter, srcs, lands, send_sems):
    if scatter:
        return []
    me = 4 * lax.axis_index("x") + 2 * lax.axis_index("y") + lax.axis_index("c")
    return [pltpu.make_async_copy(srcs[i], lands[i].at[me], send_sems[i].at[N_DEV - 1]) for i in range(len(srcs))]


def _xc_start(scatter, arrs, after, name, peers=ALL_PEERS):
    n = len(arrs)
    lands = [lax.empty((N_DEV - 1,) + a.shape[1:] if scatter else (N_DEV,) + a.shape, a.dtype) for a in arrs]

    def body(*refs):
        srcs, lnd = refs[:n], refs[n:2 * n]
        send_sems, recv_sems = refs[2 * n + 1:3 * n + 1], refs[3 * n + 1:4 * n + 1]
        token = refs[6 * n + 1]
        for cp in _xc_copies(scatter, srcs, lnd, send_sems, recv_sems, peers) + _xc_own(scatter, srcs, lnd, send_sems):
            cp.start()
        token[...] = jnp.zeros_like(token)

    outs = pl.pallas_call(
        body, name=name,
        out_shape=[pltpu.SemaphoreType.DMA((N_DEV,))] * (2 * n)
        + [pltpu.HBM(a.shape, a.dtype) for a in arrs] + [pltpu.HBM(a.shape, a.dtype) for a in lands]
        + [jax.ShapeDtypeStruct((8, 128), F32)],
        in_specs=[HBM] * (2 * n) + [ANY],
        out_specs=[SEM] * (2 * n) + [HBM] * (2 * n) + [pl.BlockSpec(memory_space=pltpu.VMEM)],
        input_output_aliases={i: 2 * n + i for i in range(2 * n)},
        compiler_params=pltpu.CompilerParams(has_side_effects=EFFECT),
    )(*[pltpu.with_memory_space_constraint(a, pltpu.HBM) for a in list(arrs) + lands], after)
    return outs[:n], outs[n:2 * n], outs[2 * n:3 * n], outs[3 * n:4 * n], outs[4 * n][0, 0]


def _xc_wait(scatter, send_sems, recv_sems, srcs, lands, after, name, peers=ALL_PEERS):
    n = len(srcs)

    def body(*refs):
        s_refs, l_refs = refs[:n], refs[n:2 * n]
        ss, rs = refs[2 * n:3 * n], refs[3 * n:4 * n]
        for cp in _xc_copies(scatter, s_refs, l_refs, ss, rs, peers):
            cp.wait_send()
            cp.wait_recv()
        for cp in _xc_own(scatter, s_refs, l_refs, ss):
            cp.wait()

    outs = pl.pallas_call(
        body, name=name,
        out_shape=[pltpu.HBM(a.shape, a.dtype) for a in list(srcs) + list(lands)],
        in_specs=[HBM] * (2 * n) + [SEM] * (2 * n) + [ANY],
        out_specs=[HBM] * (2 * n),
        input_output_aliases={i: i for i in range(2 * n)},
        compiler_params=pltpu.CompilerParams(has_side_effects=EFFECT),
    )(*srcs, *lands, *send_sems, *recv_sems, after)
    return outs[:n], outs[n:]


def _sib_copies(zones, send_sems, recv_sems):
    x, y, c = lax.axis_index("x"), lax.axis_index("y"), lax.axis_index("c")
    copies = []
    for i in range(len(zones)):
        for q in range(N_DEV // 2):
            slot = zones[i].at[2 * q + c]
            copies.append(pltpu.make_async_remote_copy(
                src_ref=slot, dst_ref=slot, send_sem=send_sems[i].at[q], recv_sem=recv_sems[i].at[q],
                device_id=(x, y, 1 - c), device_id_type=MESH))
    return copies


def _sib_start(zones, name):
    n = len(zones)

    def body(*refs):
        for cp in _sib_copies(refs[:n], refs[n:2 * n], refs[2 * n:3 * n]):
            cp.start()

    outs = pl.pallas_call(
        body, name=name,
        out_shape=[pltpu.SemaphoreType.DMA((N_DEV // 2,))] * (2 * n) + [pltpu.HBM(a.shape, a.dtype) for a in zones],
        in_specs=[HBM] * n,
        out_specs=[SEM] * (2 * n) + [HBM] * n,
        input_output_aliases={i: 2 * n + i for i in range(n)},
        compiler_params=pltpu.CompilerParams(has_side_effects=EFFECT),
    )(*[pltpu.with_memory_space_constraint(a, pltpu.HBM) for a in zones])
    return outs[:n], outs[n:2 * n], outs[2 * n:]


def _sib_wait(send_sems, recv_sems, zones, name):
    n = len(zones)

    def body(*refs):
        for cp in _sib_copies(refs[:n], refs[n:2 * n], refs[2 * n:3 * n]):
            cp.wait_send()
            cp.wait_recv()

    return pl.pallas_call(
        body, name=name,
        out_shape=[pltpu.HBM(a.shape, a.dtype) for a in zones],
        in_specs=[HBM] * n + [SEM] * (2 * n),
        out_specs=[HBM] * n,
        input_output_aliases={i: i for i in range(n)},
        compiler_params=pltpu.CompilerParams(has_side_effects=EFFECT),
    )(*zones, *send_sems, *recv_sems)


def _adamw_math(w, g, m, v):
    m = ADAM_B1 * m + (1.0 - ADAM_B1) * g
    v = ADAM_B2 * v + (1.0 - ADAM_B2) * (g * g)
    m_hat = m / (1.0 - ADAM_B1 ** ADAM_STEP)
    v_hat = v / (1.0 - ADAM_B2 ** ADAM_STEP)
    delta = -ADAM_LR * (m_hat / (jnp.sqrt(v_hat) + ADAM_EPS) + ADAM_WD * w)
    return delta, m, v


def _adamw_sharded(parts, w, m, v, pos, name):
    depth, rows, cols = w.shape
    tr = _tile(rows, 256) if rows % 8 == 0 else rows
    npart = len(parts)

    def body(pos_ref, *refs):
        prefs = refs[:npart]
        w_ref, m_ref, v_ref, g_out, d_out, m_out, v_out = refs[npart:]
        g = prefs[0][...]
        for pr in prefs[1:]:
            g = g + pr[...]
        delta, mn, vn = _adamw_math(w_ref[...], g, m_ref[...], v_ref[...])
        g_out[...] = g
        d_out[...] = delta
        m_out[...] = mn
        v_out[...] = vn

    def part_spec(fn):
        return pl.BlockSpec((1, tr, cols), lambda l, i, p: (fn(p) * depth + l, i, 0))

    blk = pl.BlockSpec((1, tr, cols), lambda l, i, p: (l, i, 0))
    shp = jax.ShapeDtypeStruct((depth, rows, cols), F32)
    return pl.pallas_call(
        body, name=name,
        grid_spec=pltpu.PrefetchScalarGridSpec(
            num_scalar_prefetch=1, grid=(depth, rows // tr),
            in_specs=[part_spec(fn) for _, fn in parts] + [blk, blk, blk],
            out_specs=[blk, blk, blk, blk]),
        out_shape=[shp, shp, shp, shp],
        compiler_params=_cp("parallel", "parallel"),
    )(pos, *[a for a, _ in parts], w, m, v)


def _adamw_layer(parts, w, m, v, pos, layer, prev, name):
    depth, rows, cols = w.shape
    npart = len(parts)
    nprev = 0 if prev is None else 4
    if rows % 16 == 0:
        tr, tc = max(t for t in range(16, 257, 16) if rows % t == 0), cols
    else:
        tr, tc = rows, _tile(cols, 256)
    pick = (lambda i: (i, 0)) if rows % 16 == 0 else (lambda i: (0, i))

    def body(pos_ref, *refs):
        prefs = refs[:npart]
        w_ref, m_ref, v_ref = refs[npart:npart + 3]
        g_out, d_out, m_out, v_out = refs[npart + 3 + nprev:]
        g = prefs[0][...].astype(F32)
        for pr in prefs[1:]:
            g = g + pr[...].astype(F32)
        delta, mn, vn = _adamw_math(w_ref[...], g, m_ref[...], v_ref[...])
        g_out[...] = g
        d_out[...] = delta
        m_out[...] = mn
        v_out[...] = vn

    def part_spec(fn):
        return pl.BlockSpec((1, tr, tc), lambda i, p: (fn(p), *pick(i)))

    blk = pl.BlockSpec((1, tr, tc), lambda i, p: (layer, *pick(i)))
    shp = jax.ShapeDtypeStruct((depth, rows, cols), F32)
    first_prev = 1 + npart + 3
    return pl.pallas_call(
        body, name=name,
        grid_spec=pltpu.PrefetchScalarGridSpec(
            num_scalar_prefetch=1, grid=(rows // tr * (cols // tc),),
            in_specs=[part_spec(fn) for _, fn in parts] + [blk, blk, blk] + [ANY] * nprev,
            out_specs=[blk, blk, blk, blk]),
        out_shape=[shp, shp, shp, shp],
        input_output_aliases={first_prev + j: j for j in range(nprev)},
        compiler_params=_cp("parallel"),
    )(pos, *[a for a, _ in parts], w, m, v, *(prev or ()))


def _adamw_rows_major(parts_by_layer, w, m, v, pos, name):
    rows, depth, cols = w.shape
    tc = _tile(cols, 256)
    npart = len(parts_by_layer[0])

    def body(pos_ref, *refs):
        prefs = refs[:depth * npart]
        w_ref, m_ref, v_ref, g_out, d_out, m_out, v_out = refs[depth * npart:]
        for l in range(depth):
            g = prefs[l * npart][0].astype(F32)
            for pr in prefs[l * npart + 1:(l + 1) * npart]:
                g = g + pr[0].astype(F32)
            delta, mn, vn = _adamw_math(w_ref[:, l, :], g, m_ref[:, l, :], v_ref[:, l, :])
            g_out[:, l, :] = g
            d_out[:, l, :] = delta
            m_out[:, l, :] = mn
            v_out[:, l, :] = vn

    def part_spec(fn):
        return pl.BlockSpec((1, rows, tc), lambda j, p: (fn(p), 0, j))

    blk = pl.BlockSpec((rows, depth, tc), lambda j, p: (0, 0, j))
    shp = jax.ShapeDtypeStruct(w.shape, F32)
    flat = [pf for parts in parts_by_layer for pf in parts]
    return pl.pallas_call(
        body, name=name,
        grid_spec=pltpu.PrefetchScalarGridSpec(
            num_scalar_prefetch=1, grid=(cols // tc,),
            in_specs=[part_spec(fn) for _, fn in flat] + [blk, blk, blk],
            out_specs=[blk, blk, blk, blk]),
        out_shape=[shp, shp, shp, shp],
        compiler_params=_cp("parallel"),
    )(pos, *[a for a, _ in flat], w, m, v)


_P1024 = ["norm1_g", "norm2_g", "ssd_norm_g", "gm_vnorm_g", "gm_out_g"]
_P16 = ["ssd_dt_bias", "ssd_a_log", "ssd_d"]


def _adamw_small(gath, wmv):
    names = list(wmv.keys())
    classes = list(gath.keys())
    flat_in = [gath[k] for k in classes]
    for nme in names:
        flat_in += list(wmv[nme])
    out_shapes = []
    for nme in names:
        out_shapes += [jax.ShapeDtypeStruct(wmv[nme][0].shape, F32)] * 4
    out_shapes += [jax.ShapeDtypeStruct((DEPTH, SSD_CONV, CONV_DIM), F32), jax.ShapeDtypeStruct((DEPTH, FF_CONV, D_FF), F32),
                   jax.ShapeDtypeStruct((1, SSD_HEADS), F32)]
    scratch = [pltpu.VMEM(gath[k].shape[1:], F32) for k in classes]
    ncls = len(classes)

    def body(*refs):
        g_refs = dict(zip(classes, refs[:ncls]))
        pos = ncls
        w_refs = {}
        for nme in names:
            w_refs[nme] = refs[pos:pos + 3]
            pos += 3
        o_refs = {}
        for nme in names:
            o_refs[nme] = refs[pos:pos + 4]
            pos += 4
        scw_out, fcw_out, loss_out = refs[pos], refs[pos + 1], refs[pos + 2]
        s_refs = dict(zip(classes, refs[pos + 3:]))
        for k in classes:
            acc = g_refs[k][0]
            for dev in range(1, N_DEV):
                acc = acc + g_refs[k][dev]
            s_refs[k][...] = acc

        def apply(nme, grad_of):
            w_ref, m_ref, v_ref = w_refs[nme]
            g_out, d_out, m_out, v_out = o_refs[nme]
            shape = w_ref.shape
            if len(shape) == 2:
                idxs = [(slice(l, l + 1),) for l in range(shape[0])]
            elif len(shape) == 3:
                idxs = [(l,) for l in range(shape[0])]
            else:
                idxs = [(l, h) for l in range(shape[0]) for h in range(shape[1])]
            for n_i, ix in enumerate(idxs):
                g = grad_of(n_i)
                delta, mn, vn = _adamw_math(w_ref[ix], g, m_ref[ix], v_ref[ix])
                g_out[ix] = g
                d_out[ix] = delta
                m_out[ix] = mn
                v_out[ix] = vn

        s1024, s1536, s2816, s16, s128, s6144, late1024, late6144 = (s_refs[k] for k in classes)
        s1024[0:1, :] += late1024[...]
        s6144[0:late6144.shape[0], :] += late6144[...]
        for n_i, nme in enumerate(_P1024):
            apply(nme, lambda l, b=2 * n_i: s1024[b + l:b + l + 1, :])
        apply("final_g", lambda l: s1024[10:11, :])
        apply("ssd_conv_b", lambda l: s1536[8 + l:9 + l, :])
        apply("ff_conv_b", lambda l: s2816[6 + l:7 + l, :])
        for n_i, nme in enumerate(_P16):
            apply(nme, lambda l, b=2 * n_i: s16[b + l:b + l + 1, :])
        apply("gm_ws", lambda q: s128[q * CHUNK:(q + 1) * CHUNK, :])
        apply("gm_bs", lambda l: s128[2048 + 8 * l:2048 + 8 * (l + 1), :])
        apply("ada_b", lambda l: s6144[2 * l:2 * l + 1, :] + s6144[2 * l + 1:2 * l + 2, :])
        for l in range(DEPTH):
            scw_out[l] = s1536[SSD_CONV * l:SSD_CONV * (l + 1), :]
            fcw_out[l] = s2816[FF_CONV * l:FF_CONV * (l + 1), :]
        loss_out[...] = s16[2 * len(_P16):2 * len(_P16) + 1, :]

    outs = pl.pallas_call(
        body, name="adamw_small",
        out_shape=out_shapes,
        scratch_shapes=scratch,
        compiler_params=pltpu.CompilerParams(vmem_limit_bytes=VMEM_LIMIT),
    )(*flat_in)
    res = {nme: tuple(outs[4 * i:4 * i + 4]) for i, nme in enumerate(names)}
    return res, outs[-3], outs[-2], outs[-1]


_WEIGHTS = ['ada_w', 'ada_b', 'norm1_g', 'norm2_g', 'w_in', 'ssd_conv_w', 'ssd_conv_b', 'ssd_dt_bias', 'ssd_a_log',
            'ssd_d', 'ssd_norm_g', 'gm_vnorm_g', 'gm_ws', 'gm_bs', 'gm_out_g', 'w_out', 'ff_up', 'ff_conv_w',
            'ff_conv_b', 'ff_down', 'final_g']


_O_XBC, _O_DT, _O_GM = D_SSD, D_SSD + CONV_DIM, D_SSD + CONV_DIM + SSD_HEADS


_TRANSPOSED = ("w_in", "ff_up")


def _full_weight(name, g):
    full = g.reshape(g.shape[0] * g.shape[1], g.shape[2])
    if name != "w_in":
        return full
    zpad = jnp.zeros((N_INP - N_IN, full.shape[1]), full.dtype)
    return jnp.concatenate([full[_O_GM:], full[:_O_XBC], full[_O_XBC:_O_DT], full[_O_DT:_O_GM], zpad], axis=0)


def _by_owner(name, grad):
    if name == "w_in":
        grad = jnp.concatenate([grad[COL_Z:COL_XBC], grad[COL_XBC:COL_DT], grad[COL_DT:COL_DT + SSD_HEADS], grad[:COL_Z]], axis=0)
    return grad.reshape(N_DEV, grad.shape[0] // N_DEV, grad.shape[1])


def kernel(x, c, ada_w, ada_b, norm1_g, norm2_g, w_in, ssd_conv_w, ssd_conv_b, ssd_dt_bias, ssd_a_log, ssd_d, ssd_norm_g, gm_vnorm_g, gm_ws, gm_bs, gm_out_g, w_out, ff_up, ff_conv_w, ff_conv_b, ff_down, final_g, loss_target, m_ada_w, m_ada_b, m_norm1_g, m_norm2_g, m_w_in, m_ssd_conv_w, m_ssd_conv_b, m_ssd_dt_bias, m_ssd_a_log, m_ssd_d, m_ssd_norm_g, m_gm_vnorm_g, m_gm_ws, m_gm_bs, m_gm_out_g, m_w_out, m_ff_up, m_ff_conv_w, m_ff_conv_b, m_ff_down, m_final_g, v_ada_w, v_ada_b, v_norm1_g, v_norm2_g, v_w_in, v_ssd_conv_w, v_ssd_conv_b, v_ssd_dt_bias, v_ssd_a_log, v_ssd_d, v_ssd_norm_g, v_gm_vnorm_g, v_gm_ws, v_gm_bs, v_gm_out_g, v_w_out, v_ff_up, v_ff_conv_w, v_ff_conv_b, v_ff_down, v_final_g):
    given = dict(locals())
    wts = {n: given[n] for n in _WEIGHTS}
    mom = {n: given["m_" + n] for n in _WEIGHTS}
    var = {n: given["v_" + n] for n in _WEIGHTS}
    nseq, seq, d = x.shape
    ix, iy, ic = lax.axis_index("x"), lax.axis_index("y"), lax.axis_index("c")
    me = 4 * ix + 2 * iy + ic
    me_arr = me.astype(jnp.int32).reshape(1)

    for nme, perm in (("ff_up", (0, 2, 1)), ("w_in", (2, 0, 1))):
        wts[nme], mom[nme], var[nme] = (jnp.transpose(a, perm) for a in (wts[nme], mom[nme], var[nme]))

    def shard(l, name):
        return _b(wts[name][:, l, :] if name == "w_in" else wts[name][l])

    g_scw, g_fcw, c_all = _all_gather([ssd_conv_w, ff_conv_w, c], "gather_first")
    scw_f = jnp.transpose(g_scw, (1, 2, 0, 3)).reshape(DEPTH, SSD_CONV, CONV_DIM)
    fcw_f = jnp.transpose(g_fcw, (1, 2, 0, 3)).reshape(DEPTH, FF_CONV, D_FF)
    c_all = c_all.reshape(N_DEV * nseq, d)

    n_ada = ada_w.shape[2]
    ada_b_shard = lax.dynamic_slice_in_dim(ada_b, me * n_ada, n_ada, axis=1).reshape(DEPTH, 1, n_ada)
    mod_part, c_act = _ada_fwd(c_all, ada_w, ada_b_shard)
    first_ssem, first_rsem, first_src, first_land, first_zero = _xc_start(
        False, [mod_part, shard(0, "w_in")], c_act, "ag_first_start", peers=[ALL_PEERS, OTHER_CHIPS])
    _, (mod_g,) = _xc_wait(False, first_ssem[:1], first_rsem[:1], first_src[:1], first_land[:1], c_act,
                           "mod_wait")
    mod_all = jnp.transpose(mod_g, (1, 2, 0, 3)).reshape(DEPTH, N_DEV * nseq, N_MOD * d)
    mod_mine = lax.dynamic_slice_in_dim(mod_all, me * nseq, nseq, axis=1)
    mod_k = jnp.transpose(mod_mine.reshape(DEPTH, nseq, N_MOD, 1, d), (0, 2, 1, 3, 4))
    mods = [[mod_k[l, k] for k in range(N_MOD)] for l in range(DEPTH)]

    later =[(0, "w_out"), (0, "ff_up"), (0, "ff_down"), (1, "w_in"), (1, "w_out"), (1, "ff_up"), (1, "ff_down")]
    ag_groups = {(0, "w_out"): [0], (0, "ff_up"): [1, 2], (1, "w_in"): [3, 4], (1, "ff_up"): [5, 6]}
    big_cache, ag = {}, {}

    def big_w(l, name, after):
        if (l, name) == (0, "w_in") and (l, name) not in big_cache:
            ag["ssem"], ag["rsem"], ag["src"], ag["land"], started = _xc_start(
                False, [shard(l2, n2) for l2, n2 in later], after, "ag_start")
            _, zones = _xc_wait(False, first_ssem[1:], first_rsem[1:], first_src[1:], first_land[1:],
                                jnp.full((8, 128), started, F32), "ag_first_wait", peers=OTHER_CHIPS)
            (zone,) = _sib_wait(*_sib_start(zones, "ag_first_sib_start"), "ag_first_sib_wait")
            big_cache[(l, name)] = _full_weight(name, zone)
        if (l, name) not in big_cache:
            idx = ag_groups[(l, name)]
            pick = lambda seq_: [seq_[i] for i in idx]
            _, lands = _xc_wait(False, pick(ag["ssem"]), pick(ag["rsem"]), pick(ag["src"]), pick(ag["land"]), after,
                                f"ag_wait_{l}_{name}")
            for i, land in zip(idx, lands):
                big_cache[later[i]] = _full_weight(later[i][1], land)
        return big_cache[(l, name)]

    lw = []
    for l in range(DEPTH):
        lw.append(dict(
            norm1_g=norm1_g[l:l + 1] + (first_zero if l == 0 else 0.0), norm2_g=norm2_g[l:l + 1], ssd_conv_w=scw_f[l],
            ssd_conv_b=ssd_conv_b[l:l + 1], ssd_dt_bias=ssd_dt_bias[l:l + 1], ssd_a_log=ssd_a_log[l:l + 1],
            ssd_d=ssd_d[l:l + 1], ssd_norm_g=ssd_norm_g[l:l + 1], gm_vnorm_g=gm_vnorm_g[l:l + 1], gm_ws=gm_ws[l],
            gm_bst=gm_bs[l].T, gm_out_g=gm_out_g[l:l + 1], ff_conv_w=fcw_f[l], ff_conv_b=ff_conv_b[l:l + 1]))

    outs = {}
    pending, win_parts = {}, {}

    def rs_finish(l, group, after):
        names, ssem, rsem, srcs, lands = pending.pop((l, group))
        srcs, lands = _xc_wait(True, ssem, rsem, srcs, lands, after, f"rs_wait_{l}_{group}")
        for nme, own, land in zip(names, srcs, lands):
            parts = [(own, lambda p: p[0])] + [(land, lambda p, k=k: k) for k in range(N_DEV - 1)]
            if nme == "w_in":
                win_parts[l] = parts
                if len(win_parts) == DEPTH:
                    outs[nme] = _adamw_rows_major([win_parts[k] for k in range(DEPTH)], wts[nme], mom[nme], var[nme],
                                                  me_arr, "adamw_w_in")
                continue
            outs[nme] = _adamw_layer(parts, wts[nme], mom[nme], var[nme], me_arr, l, outs.get(nme), f"adamw_{nme}_{l}")
        return land if names[-1] == "w_in" else outs[names[-1]][0]

    def grad_sink(l, group, grads, after):
        names = list(grads)
        ssem, rsem, srcs, lands, zero = _xc_start(True, [_by_owner(n, grads[n]) for n in names], after, f"rs_start_{l}_{group}")
        pending[(l, group)] = (names, ssem, rsem, srcs, lands)
        return zero.reshape(1, 1)

    early_gather = {}

    def small_sink(l, early, small, dmods, dfg, loss_p):
        if l > 0:
            return None
        layers = [dict(early, norm1_g=jnp.zeros((1, d), F32))] + small[1:]
        rows = lambda name: [layers[k][name] for k in range(DEPTH)]
        packed = [
            jnp.concatenate(sum([rows(n) for n in _P1024], []) + [dfg], axis=0),
            jnp.concatenate(rows("ssd_conv_w") + rows("ssd_conv_b"), axis=0),
            jnp.concatenate(rows("ff_conv_w") + rows("ff_conv_b"), axis=0),
            jnp.concatenate(sum([rows(n) for n in _P16], []) + [loss_p[:, :SSD_HEADS]], axis=0),
            jnp.concatenate([layers[k]["gm_ws"].reshape(GM_HEADS * CHUNK, CHUNK) for k in range(DEPTH)] + rows("gm_bs"), axis=0),
            jnp.concatenate([jnp.zeros((nseq, N_MOD * d), F32)] + dmods[1:], axis=0)]
        ssem, rsem, srcs, lands, zero = _xc_start(False, packed, packed[0], "small_start")
        early_gather.update(ssem=ssem, rsem=rsem, srcs=srcs, lands=lands)
        return zero.reshape(1, 1)

    grad_x, small, dmods = _local_step(
        x.reshape(nseq * seq, d), loss_target.reshape(nseq * seq, d), mods, lw, final_g.reshape(1, d), nseq=nseq,
        big_w=big_w, grad_sink=grad_sink, small_sink=small_sink)

    done = grad_x
    for l, grp in ((1, "ffn"), (1, "w_out"), (1, "w_in"), (0, "ffn"), (0, "w_out")):
        done = rs_finish(l, grp, done)
    _, gathered = _xc_wait(False, early_gather["ssem"], early_gather["rsem"], early_gather["srcs"],
                           early_gather["lands"], done, "small_wait")
    gathered = list(gathered)
    gathered += _all_gather([small[0]["norm1_g"], dmods[0]], "gather_late", dep=gathered[0])
    gath = dict(zip(["p1024", "p1536", "p2816", "p16", "p128", "p6144", "late1024", "late6144"], gathered))

    dmod_all = jnp.concatenate([gath["late6144"].reshape(1, N_DEV * nseq, N_MOD * d),
                                jnp.transpose(gath["p6144"].reshape(N_DEV, DEPTH, nseq, N_MOD * d)[:, 1:], (1, 0, 2, 3)).reshape(
                                    DEPTH - 1, N_DEV * nseq, N_MOD * d)], axis=0)
    small_names = _P1024 + ["final_g", "ssd_conv_b", "ff_conv_b"] + _P16 + ["gm_ws", "gm_bs", "ada_b"]
    wmv = {}
    for nme in small_names:
        if nme == "final_g":
            wmv[nme] = tuple(a.reshape(1, d) for a in (wts[nme], mom[nme], var[nme]))
        else:
            wmv[nme] = (wts[nme], mom[nme], var[nme])
    small_out, scw_full, fcw_full, loss_sum = _adamw_small(gath, wmv)
    loss = loss_sum[0, 0]
    rs_finish(0, "w_in", scw_full)
    for nme in small_names:
        outs[nme] = small_out[nme]
    outs["final_g"] = tuple(a.reshape(d) for a in outs["final_g"])

    n_scw, n_fcw = ssd_conv_w.shape[2], ff_conv_w.shape[2]
    g_scw_mine = lax.dynamic_slice_in_dim(scw_full, me * n_scw, n_scw, axis=2)
    g_fcw_mine = lax.dynamic_slice_in_dim(fcw_full, me * n_fcw, n_fcw, axis=2)
    outs["ssd_conv_w"] = _adamw_sharded([(g_scw_mine, lambda p: 0)], ssd_conv_w, m_ssd_conv_w, v_ssd_conv_w, me_arr, "adamw_ssd_conv_w")
    outs["ff_conv_w"] = _adamw_sharded([(g_fcw_mine, lambda p: 0)], ff_conv_w, m_ff_conv_w, v_ff_conv_w, me_arr, "adamw_ff_conv_w")

    dmod_cols = _b(lax.dynamic_slice_in_dim(dmod_all, me * n_ada, n_ada, axis=2))
    g_ada = jnp.stack([_matmul(c_act, dmod_cols[l], ta=True, name=f"mm_ada_dw_{l}") for l in range(DEPTH)])
    outs["ada_w"] = _adamw_sharded([(g_ada, lambda p: 0)], ada_w, m_ada_w, v_ada_w, me_arr, "adamw_ada_w")

    for nme, perm in (("ff_up", (0, 2, 1)), ("w_in", (1, 2, 0))):
        outs[nme] = tuple(jnp.transpose(a, perm) for a in outs[nme])
    result = [loss, grad_x.reshape(nseq, seq, d)]
    for k in range(4):
        result += [outs[n][k] for n in _WEIGHTS]
    return tuple(result)
```

```python
import functools
import math

import jax
import jax.numpy as jnp
from jax import lax
from jax.experimental import pallas as pl
from jax.experimental.pallas import tpu as pltpu

F32 = jnp.float32
BF16 = jnp.bfloat16

N_DEV = 8
D_MODEL = 1024
DEPTH = 2
CHUNK = 128
SSD_HEADS = 16
SSD_HEAD_DIM = 64
SSD_GROUPS = 2
HEADS_PER_GROUP = SSD_HEADS // SSD_GROUPS
GROUP_WIDTH = HEADS_PER_GROUP * SSD_HEAD_DIM
D_STATE = 128
D_SSD = 1024
CONV_DIM = 1536
SSD_CONV = 4
GM_HEADS = 8
GM_HEAD_DIM = 128
D_GM = 1024
D_FF = 2816
FF_CONV = 3
N_IN = 4624
N_MOD = 6
EPS = 1e-6

N_INP = 5120
COL_U, COL_V, COL_Z, COL_XBC, COL_DT = 0, 1024, 2048, 3072, 4608

ADAM_LR = 0.001
ADAM_B1 = 0.9
ADAM_B2 = 0.999
ADAM_EPS = 1e-08
ADAM_WD = 0.01
ADAM_STEP = 10

VMEM_LIMIT = 56 * 1024 * 1024
MESH = pl.DeviceIdType.MESH
ANY = pl.BlockSpec(memory_space=pl.ANY)


def _cp(*sem):
    return pltpu.CompilerParams(dimension_semantics=sem, vmem_limit_bytes=VMEM_LIMIT)


def _tile(n, pref):
    if n <= pref or n % 128:
        return n
    best = 128
    for t in range(128, pref + 1, 128):
        if n % t == 0:
            best = t
    return best


def _per_layer(n):
    return pl.BlockSpec((DEPTH, n), lambda *_: (0, 0))


def _row(ref, layer, cols=slice(None)):
    return ref[layer:layer + 1, cols]


def _silu(x):
    return x * jax.nn.sigmoid(x)


def _gelu(x):
    return 0.5 * x * (1.0 + lax.erf(x * (1.0 / math.sqrt(2.0))))


def _softplus(x):
    return jnp.maximum(x, 0.0) + jnp.log1p(jnp.exp(-jnp.abs(x)))


def _b(x):
    return x.astype(BF16)


_NN = (((1,), (0,)), ((), ()))
_NT = (((1,), (1,)), ((), ()))
_TN = (((0,), (0,)), ((), ()))


def _dg(a, b, dn):
    return lax.dot_general(_b(a), _b(b), dn, preferred_element_type=F32)


@jax.custom_vjp
def _bdot(a, b):
    return _dg(a, b, _NN)


def _bdot_fwd(a, b):
    return _dg(a, b, _NN), (a, b)


def _bdot_bwd(res, ct):
    a, b = res
    return _dg(ct, b, _NT), _dg(a, ct, _TN)


_bdot.defvjp(_bdot_fwd, _bdot_bwd)


@jax.custom_vjp
def _bdot_nt(a, b):
    return _dg(a, b, _NT)


def _bdot_nt_fwd(a, b):
    return _dg(a, b, _NT), (a, b)


def _bdot_nt_bwd(res, ct):
    a, b = res
    return _dg(ct, b, _NN), _dg(ct, a, _TN)


_bdot_nt.defvjp(_bdot_nt_fwd, _bdot_nt_bwd)


@jax.custom_vjp
def _bdot_tn(a, b):
    return _dg(a, b, _TN)


def _bdot_tn_fwd(a, b):
    return _dg(a, b, _TN), (a, b)


def _bdot_tn_bwd(res, ct):
    a, b = res
    return _dg(b, ct, _NT), _dg(a, ct, _NN)


_bdot_tn.defvjp(_bdot_tn_fwd, _bdot_tn_bwd)


def _tri(n, lower):
    r = lax.broadcasted_iota(jnp.int32, (n, n), 0)
    c = lax.broadcasted_iota(jnp.int32, (n, n), 1)
    return ((r >= c) if lower else (r <= c)).astype(F32)


def _eye(n):
    r = lax.broadcasted_iota(jnp.int32, (n, n), 0)
    c = lax.broadcasted_iota(jnp.int32, (n, n), 1)
    return (r == c).astype(F32)


def _hdot(a, b, dn):
    return lax.dot_general(a, b, dn, precision=lax.Precision.HIGHEST, preferred_element_type=F32)


@jax.custom_vjp
def _cumsum_rows(x):
    return _hdot(_tri(x.shape[0], True), x, _NN)


def _cumsum_rows_fwd(x):
    return _cumsum_rows(x), None


def _cumsum_rows_bwd(_, ct):
    return (_hdot(_tri(ct.shape[0], False), ct, _NN),)


_cumsum_rows.defvjp(_cumsum_rows_fwd, _cumsum_rows_bwd)


@jax.custom_vjp
def _transpose(x):
    return _hdot(_eye(x.shape[1]), x, _NT)


def _transpose_fwd(x):
    return _transpose(x), None


def _transpose_bwd(_, ct):
    return (_hdot(_eye(ct.shape[1]), ct, _NT),)


_transpose.defvjp(_transpose_fwd, _transpose_bwd)


MXU_WIDTH = 256
MATMUL_TILE_CAP = 2816
MATMUL_VMEM = 44 * 1024 * 1024


def _mxu_tiles(n):
    if n <= MATMUL_TILE_CAP or n % 128:
        return [n]
    for unit in (MXU_WIDTH, 128):
        opts = [t for t in range(unit, MATMUL_TILE_CAP + 1, unit) if n % t == 0]
        if opts:
            return opts
    return [n]


def _matmul(a, b, *, ta=False, tb=False, name, dep=None, out_dtype=F32):
    pieces = list(a) if isinstance(a, (list, tuple)) else [a]
    npc = len(pieces)
    rows, width = pieces[0].shape
    assert all(p.shape == (rows, width) for p in pieces)
    if ta:
        k_dim, m_dim = rows, width * npc
    else:
        m_dim, k_dim = rows, width * npc
    if tb:
        n_dim, kb = b.shape
    else:
        kb, n_dim = b.shape
    assert kb == k_dim, (pieces[0].shape, npc, b.shape, ta, tb)
    m_unit = width if npc > 1 and ta else m_dim
    k_unit = width if npc > 1 and not ta else k_dim
    tm = _tile(m_unit, 1536)
    tn_opts, tk_opts = _mxu_tiles(n_dim), _mxu_tiles(k_unit)
    tn, tk = tn_opts.pop(), tk_opts.pop()
    while 4 * (tm * tk + tk * tn) + 8 * tm * tn > MATMUL_VMEM:
        if tn >= tk and tn_opts:
            tn = tn_opts.pop()
        else:
            tk = tk_opts.pop()
    ni, nj, nk = m_dim // tm, n_dim // tn, k_dim // tk
    per = width // (tm if ta else tk)
    dn = (((0 if ta else 1,), (1 if tb else 0,)), ((), ()))

    a_bytes, b_bytes = m_dim * k_dim, k_dim * n_dim
    m_outer = nk > 1 or a_bytes + b_bytes * ni <= b_bytes + a_bytes * nj
    if m_outer:
        ij = lambda o, n, k: (o, n)
        grid = (ni, nj, nk)
    else:
        ij = lambda o, n, k: (n, o)
        grid = (nj, ni, nk)

    use_acc = nk > 1 and out_dtype != F32

    def body(*refs):
        a_refs, b_ref = refs[:npc], refs[npc]
        o_ref = refs[-2] if use_acc else refs[-1]
        acc_ref = refs[-1]
        k = pl.program_id(2)
        i = pl.program_id(0 if m_outer else 1)
        along = i if ta else k

        def step(a_ref):
            p = lax.dot_general(a_ref[...], b_ref[...], dn, preferred_element_type=F32)
            if nk == 1:
                o_ref[...] = p.astype(out_dtype)
            else:
                @pl.when(k == 0)
                def _():
                    acc_ref[...] = p

                @pl.when((k > 0) & (k < nk - 1 if use_acc else True))
                def _():
                    acc_ref[...] += p

                if use_acc:
                    @pl.when(k == nk - 1)
                    def _():
                        o_ref[...] = (acc_ref[...] + p).astype(out_dtype)

        if npc == 1:
            step(a_refs[0])
        else:
            for pc in range(npc):
                pl.when((along >= pc * per) & (along < (pc + 1) * per))(functools.partial(step, a_refs[pc]))

    def a_map(pc, o, n, k):
        i, _ = ij(o, n, k)
        along = i if ta else k
        if npc > 1:
            along = jnp.clip(along - pc * per, 0, per - 1)
        return (k, along) if ta else (i, along)

    def b_map(o, n, k):
        _, j = ij(o, n, k)
        return (j, k) if tb else (k, j)

    extra = [] if dep is None else [dep]
    return pl.pallas_call(
        body, name=name,
        grid=grid,
        in_specs=[pl.BlockSpec((tk, tm) if ta else (tm, tk), functools.partial(a_map, pc)) for pc in range(npc)]
        + [pl.BlockSpec((tn, tk) if tb else (tk, tn), b_map)] + [ANY] * len(extra),
        out_specs=pl.BlockSpec((tm, tn), lambda o, n, k: ij(o, n, k)),
        out_shape=jax.ShapeDtypeStruct((m_dim, n_dim), out_dtype),
        scratch_shapes=[pltpu.VMEM((tm, tn), F32)] if use_acc else [],
        compiler_params=_cp("parallel", "parallel", "arbitrary"),
    )(*pieces, b, *extra)


def _ada_fwd(c_all, ada_w, ada_b_shard):
    depth, d, n = ada_w.shape
    nb = c_all.shape[0]

    def body(c_ref, w_ref, b_ref, o_ref, ca_ref):
        ca = _silu(c_ref[...])
        ca_ref[...] = _b(ca)
        o_ref[0] = _dg(ca, w_ref[0], _NN) + b_ref[0]

    return pl.pallas_call(
        body, name="ada_fwd",
        grid=(depth,),
        in_specs=[pl.BlockSpec((nb, d), lambda l: (0, 0)),
                  pl.BlockSpec((1, d, n), lambda l: (l, 0, 0)),
                  pl.BlockSpec((1, 1, n), lambda l: (l, 0, 0))],
        out_specs=[pl.BlockSpec((1, nb, n), lambda l: (l, 0, 0)),
                   pl.BlockSpec((nb, d), lambda l: (0, 0))],
        out_shape=[jax.ShapeDtypeStruct((depth, nb, n), F32), jax.ShapeDtypeStruct((nb, d), BF16)],
        compiler_params=_cp("arbitrary"),
    )(c_all, ada_w, ada_b_shard)


def _fold(acc):
    return jnp.sum(acc, axis=0, keepdims=True)


def _rinv(x):
    return lax.rsqrt(jnp.sum(x * x, axis=-1, keepdims=True) * (1.0 / D_MODEL) + EPS)


def _rms_bwd(a, xhat, rinv):
    return rinv * (a - xhat * (jnp.sum(a * xhat, axis=-1, keepdims=True) * (1.0 / D_MODEL)))


def _row_tile(seq):
    return min(seq, 256)


def _normmod_fwd(x, g, sc, sh, *, nseq, name, layer):
    t, d = x.shape
    seq = t // nseq
    tr = _row_tile(seq)
    nt = seq // tr
    row = pl.BlockSpec((tr, d), lambda s, i: (s * nt + i, 0))
    per_seq = pl.BlockSpec((1, 1, d), lambda s, i: (s, 0, 0))

    def body(x_ref, g_ref, sc_ref, sh_ref, h_ref):
        x_v = x_ref[...]
        h_ref[...] = _b(x_v * _rinv(x_v) * (_row(g_ref, layer) * (1.0 + sc_ref[0])) + sh_ref[0])

    return pl.pallas_call(
        body, name=name, grid=(nseq, nt),
        in_specs=[row, _per_layer(d), per_seq, per_seq],
        out_specs=row,
        out_shape=jax.ShapeDtypeStruct((t, d), BF16),
        compiler_params=_cp("parallel", "parallel"),
    )(x, g, sc, sh)


NORM_TM = 512


def _matmul_normbwd(a, b, dxo, x, delta, gate, g, sc, *, nseq, name, layer, dep=None):
    pieces = list(a) if isinstance(a, (list, tuple)) else [a]
    npc = len(pieces)
    t, width = pieces[0].shape
    k_dim, d = width * npc, b.shape[1]
    assert b.shape[0] == k_dim and all(p.shape == (t, width) for p in pieces)
    seq = t // nseq
    tm = min(NORM_TM, seq)
    per_seq_tiles = seq // tm
    tk = _mxu_tiles(width if npc > 1 else k_dim).pop()
    nk, per = k_dim // tk, width // tk
    has_delta = delta is not None
    extra = [] if dep is None else [dep]

    def body(*refs):
        a_refs, b_ref = refs[:npc], refs[npc]
        dxo_ref, x_ref = refs[npc + 1], refs[npc + 2]
        pos = npc + 3
        if has_delta:
            delta_ref, gate_ref = refs[pos], refs[pos + 1]
            pos += 2
        g_ref, sc_ref = refs[pos], refs[pos + 1]
        pos += 2 + len(extra)
        if has_delta:
            dx_ref, dd_ref, dgate_ref, dg_ref, dsc_ref, dsh_ref = refs[pos:pos + 6]
        else:
            dx_ref, dg_ref, dsc_ref, dsh_ref = refs[pos:pos + 4]
        acc_ref = refs[-1]
        i, k = pl.program_id(0), pl.program_id(1)

        def norm_bwd(dh_v):
            g_v, one_sc = _row(g_ref, layer), 1.0 + sc_ref[0]
            x_v = x_ref[...]
            rinv = _rinv(x_v)
            xhat = x_v * rinv
            dx = dxo_ref[...] + _rms_bwd(dh_v * (g_v * one_sc), xhat, rinv)
            dx_ref[...] = dx

            @pl.when(i == 0)
            def _():
                dg_ref[...] = jnp.zeros_like(dg_ref)

            @pl.when(i % per_seq_tiles == 0)
            def _():
                dsc_ref[...] = jnp.zeros_like(dsc_ref)
                dsh_ref[...] = jnp.zeros_like(dsh_ref)
                if has_delta:
                    dgate_ref[...] = jnp.zeros_like(dgate_ref)

            t_sum = _fold(dh_v * xhat)
            dg_ref[...] += t_sum * one_sc
            dsc_ref[0] += t_sum * g_v
            dsh_ref[0] += _fold(dh_v)
            if has_delta:
                dd_ref[...] = _b(dx * gate_ref[0])
                dgate_ref[0] += _fold(dx * delta_ref[...])

        def step(a_ref):
            p = lax.dot_general(a_ref[...], b_ref[...], _NN, preferred_element_type=F32)
            if nk == 1:
                norm_bwd(p)
            else:
                @pl.when(k == 0)
                def _():
                    acc_ref[...] = p

                @pl.when((k > 0) & (k < nk - 1))
                def _():
                    acc_ref[...] += p

                @pl.when(k == nk - 1)
                def _():
                    norm_bwd(acc_ref[...] + p)

        if npc == 1:
            step(a_refs[0])
        else:
            for pc in range(npc):
                pl.when((k >= pc * per) & (k < (pc + 1) * per))(functools.partial(step, a_refs[pc]))

    def a_map(pc, i, k):
        return (i, jnp.clip(k - pc * per, 0, per - 1) if npc > 1 else k)

    row = pl.BlockSpec((tm, d), lambda i, k: (i, 0))
    per_seq = pl.BlockSpec((1, 1, d), lambda i, k: (i // per_seq_tiles, 0, 0))
    vec = pl.BlockSpec((1, d), lambda i, k: (0, 0))
    shp = lambda *s, dt=F32: jax.ShapeDtypeStruct(s, dt)
    in_specs = [pl.BlockSpec((tm, tk), functools.partial(a_map, pc)) for pc in range(npc)]
    in_specs += [pl.BlockSpec((tk, d), lambda i, k: (k, 0)), row, row]
    operands = [*pieces, b, dxo, x]
    if has_delta:
        in_specs += [row, per_seq]
        operands += [delta, gate]
    in_specs += [_per_layer(d), per_seq] + [ANY] * len(extra)
    operands += [g, sc, *extra]
    if has_delta:
        out_specs = [row, row, per_seq, vec, per_seq, per_seq]
        out_shape = [shp(t, d), shp(t, d, dt=BF16), shp(nseq, 1, d), shp(1, d), shp(nseq, 1, d), shp(nseq, 1, d)]
    else:
        out_specs = [row, vec, per_seq, per_seq]
        out_shape = [shp(t, d), shp(1, d), shp(nseq, 1, d), shp(nseq, 1, d)]
    outs = pl.pallas_call(
        body, name=name, grid=(t // tm, nk),
        in_specs=in_specs, out_specs=out_specs, out_shape=out_shape,
        scratch_shapes=[pltpu.VMEM((tm, d), F32)],
        compiler_params=_cp("arbitrary", "arbitrary"),
    )(*operands)
    if has_delta:
        return tuple(outs)
    dx, dg, dsc, dsh = outs
    return dx, None, None, dg, dsc, dsh


def _matmul_normfwd(a, b, xin, gate, g, sc, sh, *, nseq, name, layer):
    t, k_dim = a.shape
    d = b.shape[1]
    assert b.shape[0] == k_dim and k_dim <= MATMUL_TILE_CAP
    seq = t // nseq
    tm = min(NORM_TM, seq)
    per_seq_tiles = seq // tm

    def body(a_ref, b_ref, xin_ref, gate_ref, g_ref, sc_ref, sh_ref, dl_ref, x_ref, h_ref):
        dl = lax.dot_general(a_ref[...], b_ref[...], _NN, preferred_element_type=F32)
        dl_ref[...] = dl
        x = xin_ref[...] + gate_ref[0] * dl
        x_ref[...] = x
        h_ref[...] = _b(x * _rinv(x) * (_row(g_ref, layer) * (1.0 + sc_ref[0])) + sh_ref[0])

    row = pl.BlockSpec((tm, d), lambda i: (i, 0))
    per_seq = pl.BlockSpec((1, 1, d), lambda i: (i // per_seq_tiles, 0, 0))
    return pl.pallas_call(
        body, name=name, grid=(t // tm,),
        in_specs=[pl.BlockSpec((tm, k_dim), lambda i: (i, 0)), pl.BlockSpec((k_dim, d), lambda i: (0, 0)),
                  row, per_seq, _per_layer(d), per_seq, per_seq],
        out_specs=[row, row, row],
        out_shape=[jax.ShapeDtypeStruct((t, d), F32), jax.ShapeDtypeStruct((t, d), F32), jax.ShapeDtypeStruct((t, d), BF16)],
        compiler_params=_cp("parallel"),
    )(a, b, xin, gate, g, sc, sh)


def _matmul_loss(a, b, xin, gate, fg, target, *, nseq, name):
    t, k_dim = a.shape
    d = b.shape[1]
    assert b.shape[0] == k_dim and k_dim <= MATMUL_TILE_CAP
    seq = t // nseq
    tm = min(NORM_TM, seq)
    per_seq_tiles = seq // tm

    def body(a_ref, b_ref, xin_ref, gate_ref, fg_ref, tgt_ref, dl_ref, loss_ref, dx_ref, dd_ref, dgate_ref, dfg_ref):
        i = pl.program_id(0)
        fg_v, gate_v = fg_ref[...], gate_ref[0]
        dl = lax.dot_general(a_ref[...], b_ref[...], _NN, preferred_element_type=F32)
        dl_ref[...] = dl
        x = xin_ref[...] + gate_v * dl
        rinv = _rinv(x)
        xhat = x * rinv
        err = xhat * fg_v - tgt_ref[...]
        dx = _rms_bwd(err * fg_v * (1.0 / d), xhat, rinv)
        dx_ref[...] = dx
        dd_ref[...] = _b(dx * gate_v)

        @pl.when(i == 0)
        def _():
            loss_ref[...] = jnp.zeros_like(loss_ref)
            dfg_ref[...] = jnp.zeros_like(dfg_ref)

        @pl.when(i % per_seq_tiles == 0)
        def _():
            dgate_ref[...] = jnp.zeros_like(dgate_ref)

        loss_ref[...] += jnp.sum(err * err) * (0.5 / d)
        dfg_ref[...] += _fold(err * xhat) * (1.0 / d)
        dgate_ref[0] += _fold(dx * dl)

    row = pl.BlockSpec((tm, d), lambda i: (i, 0))
    per_seq = pl.BlockSpec((1, 1, d), lambda i: (i // per_seq_tiles, 0, 0))
    vec = pl.BlockSpec((1, d), lambda i: (0, 0))
    return pl.pallas_call(
        body, name=name, grid=(t // tm,),
        in_specs=[pl.BlockSpec((tm, k_dim), lambda i: (i, 0)), pl.BlockSpec((k_dim, d), lambda i: (0, 0)),
                  row, per_seq, vec, row],
        out_specs=[row, pl.BlockSpec((1, 128), lambda i: (0, 0)), row, row, per_seq, vec],
        out_shape=[jax.ShapeDtypeStruct((t, d), F32), jax.ShapeDtypeStruct((1, 128), F32), jax.ShapeDtypeStruct((t, d), F32),
                   jax.ShapeDtypeStruct((t, d), BF16), jax.ShapeDtypeStruct((nseq, 1, d), F32),
                   jax.ShapeDtypeStruct((1, d), F32)],
        compiler_params=_cp("arbitrary"),
    )(a, b, xin, gate, fg, target)


CONV_TC = 256
CONV_LANES = 128
CONV_ROWS = 64
CONV_HALO = 8


def _conv_slabs(seq, fn):
    def step(i, carry):
        r0 = pl.multiple_of(i * CONV_ROWS, CONV_ROWS)
        for h in range(CONV_TC // CONV_LANES):
            fn(r0, slice(h * CONV_LANES, (h + 1) * CONV_LANES))
        return carry

    lax.fori_loop(0, seq // CONV_ROWS, step, 0)


def _slab(ref, r0, cols, seq):
    after = ref[pl.ds(pl.multiple_of(jnp.minimum(r0 + CONV_ROWS, seq - CONV_HALO), CONV_HALO), CONV_HALO), cols]
    return jnp.concatenate([ref[pl.ds(r0, CONV_ROWS), cols], jnp.where(r0 + CONV_ROWS < seq, after, 0.0)], axis=0)


def _conv_block(x, w_ref, b):
    kw = w_ref.shape[0]
    rows = lax.broadcasted_iota(jnp.int32, x.shape, 0)
    y = b + w_ref[kw - 1:kw, :] * x
    for j in range(1, kw):
        y = y + w_ref[kw - 1 - j:kw - j, :] * jnp.where(rows >= j, pltpu.roll(x, j, 0), 0.0)
    return y


def _conv_block_bwd(dy, x, w_ref, dw_ref, db_ref):
    kw = w_ref.shape[0]
    n = x.shape[0]
    rows = lax.broadcasted_iota(jnp.int32, x.shape, 0)
    dx = w_ref[kw - 1:kw, :] * dy
    dw_ref[kw - 1:kw, :] += jnp.sum(dy * x, axis=0, keepdims=True)
    for j in range(1, kw):
        dy_j = jnp.where(rows < n - j, pltpu.roll(dy, n - j, 0), 0.0)
        dx = dx + w_ref[kw - 1 - j:kw - j, :] * dy_j
        dw_ref[kw - 1 - j:kw - j, :] += jnp.sum(dy_j * x, axis=0, keepdims=True)
    db_ref[...] += jnp.sum(dy, axis=0, keepdims=True)
    return dx


def _conv_bwd(dy_ext, x, w_ref, dw_ref, db_ref, cols):
    kw = w_ref.shape[0]
    n = dy_ext.shape[0]
    dy = dy_ext[:CONV_ROWS]
    dx = w_ref[kw - 1:kw, cols] * dy
    dw_ref[kw - 1:kw, cols] += jnp.sum(dy * x, axis=0, keepdims=True)
    for j in range(1, kw):
        dy_j = pltpu.roll(dy_ext, n - j, 0)[:CONV_ROWS]
        dx = dx + w_ref[kw - 1 - j:kw - j, cols] * dy_j
        dw_ref[kw - 1 - j:kw - j, cols] += jnp.sum(dy_j * x, axis=0, keepdims=True)
    db_ref[:, cols] += jnp.sum(dy, axis=0, keepdims=True)
    return dx


def _dsilu(pre):
    sg = jax.nn.sigmoid(pre)
    return pre * sg, sg * (1.0 + pre * (1.0 - sg))


def _conv_specs(kw, layer):
    return [pl.BlockSpec((None, kw, CONV_TC), lambda j, s: (layer, 0, j)),
            pl.BlockSpec((DEPTH, CONV_TC), lambda j, s: (0, j))]


def _ssd_conv_fwd(proj, w, b, *, nseq, layer):
    t = proj.shape[0]
    seq = t // nseq
    nb = CONV_DIM // CONV_TC
    off = COL_XBC // CONV_TC

    def body(x_ref, w_ref, b_ref, o_ref, pre_ref):
        pre = _conv_block(x_ref[...], w_ref, _row(b_ref, layer))
        pre_ref[...] = pre
        o_ref[...] = _silu(pre)

    col = pl.BlockSpec((seq, CONV_TC), lambda j, s: (s, j))
    return pl.pallas_call(
        body, name="ssd_conv_fwd", grid=(nb, nseq),
        in_specs=[pl.BlockSpec((seq, CONV_TC), lambda j, s: (s, off + j)), *_conv_specs(SSD_CONV, layer)],
        out_specs=[col, col],
        out_shape=[jax.ShapeDtypeStruct((t, CONV_DIM), F32)] * 2,
        compiler_params=_cp("parallel", "parallel"),
    )(proj, w, b)


def _ssd_conv_bwd(dact, pre, proj, w, dproj, *, nseq, layer):
    t = proj.shape[0]
    seq = t // nseq
    nb = CONV_DIM // CONV_TC
    off = COL_XBC // CONV_TC

    def body(da_ref, pre_ref, x_ref, w_ref, dproj_ref, dx_ref, dw_ref, db_ref):
        del dproj_ref

        @pl.when(pl.program_id(1) == 0)
        def _():
            dw_ref[...] = jnp.zeros_like(dw_ref)
            db_ref[...] = jnp.zeros_like(db_ref)

        def slab(r0, cols):
            _, dsilu = _dsilu(_slab(pre_ref, r0, cols, seq))
            dpre_ext = _slab(da_ref, r0, cols, seq) * dsilu
            x = x_ref[pl.ds(r0, CONV_ROWS), cols]
            dx_ref[pl.ds(r0, CONV_ROWS), cols] = _b(_conv_bwd(dpre_ext, x, w_ref, dw_ref, db_ref, cols))

        _conv_slabs(seq, slab)

    return pl.pallas_call(
        body, name="ssd_conv_bwd", grid=(nb, nseq),
        in_specs=[pl.BlockSpec((seq, CONV_TC), lambda j, s: (s, j)),
                  pl.BlockSpec((seq, CONV_TC), lambda j, s: (s, j)),
                  pl.BlockSpec((seq, CONV_TC), lambda j, s: (s, off + j)),
                  _conv_specs(SSD_CONV, layer)[0],
                  ANY],
        out_specs=[pl.BlockSpec((seq, CONV_TC), lambda j, s: (s, off + j)),
                   pl.BlockSpec((SSD_CONV, CONV_TC), lambda j, s: (0, j)),
                   pl.BlockSpec((1, CONV_TC), lambda j, s: (0, j))],
        out_shape=[jax.ShapeDtypeStruct(dproj.shape, dproj.dtype), jax.ShapeDtypeStruct((SSD_CONV, CONV_DIM), F32),
                   jax.ShapeDtypeStruct((1, CONV_DIM), F32)],
        input_output_aliases={4: 0},
        compiler_params=_cp("parallel", "arbitrary"),
    )(dact, pre, proj, w, dproj)


def _ffn_act_fwd(up, w, b, *, nseq, layer):
    t = up.shape[0]
    seq = t // nseq
    nb = D_FF // CONV_TC

    def body(g_ref, v_ref, w_ref, b_ref, o_ref):
        pre = _conv_block(g_ref[...].astype(F32), w_ref, _row(b_ref, layer))
        o_ref[...] = _b(_silu(pre) * v_ref[...].astype(F32))

    col = pl.BlockSpec((seq, CONV_TC), lambda j, s: (s, j))
    return pl.pallas_call(
        body, name="ffn_act_fwd", grid=(nb, nseq),
        in_specs=[col,
                  pl.BlockSpec((seq, CONV_TC), lambda j, s: (s, nb + j)),
                  *_conv_specs(FF_CONV, layer)],
        out_specs=col,
        out_shape=jax.ShapeDtypeStruct((t, D_FF), BF16),
        compiler_params=_cp("parallel", "parallel"),
    )(up, up, w, b)


def _ffn_act_bwd(dact, up, w, b, *, nseq, layer):
    t = up.shape[0]
    seq = t // nseq
    nb = D_FF // CONV_TC

    def body(da_ref, g_ref, v_ref, w_ref, b_ref, dg_ref, dv_ref, dw_ref, db_ref):
        @pl.when(pl.program_id(1) == 0)
        def _():
            dw_ref[...] = jnp.zeros_like(dw_ref)
            db_ref[...] = jnp.zeros_like(db_ref)

        gate = g_ref[...].astype(F32)
        silu, dsilu = _dsilu(_conv_block(gate, w_ref, _row(b_ref, layer)))
        da = da_ref[...].astype(F32)
        dv_ref[...] = _b(da * silu)
        dg_ref[...] = _b(_conv_block_bwd(da * v_ref[...].astype(F32) * dsilu, gate, w_ref, dw_ref, db_ref))

    col = pl.BlockSpec((seq, CONV_TC), lambda j, s: (s, j))
    return pl.pallas_call(
        body, name="ffn_act_bwd", grid=(nb, nseq),
        in_specs=[col, col,
                  pl.BlockSpec((seq, CONV_TC), lambda j, s: (s, nb + j)),
                  *_conv_specs(FF_CONV, layer)],
        out_specs=[col, col,
                   pl.BlockSpec((FF_CONV, CONV_TC), lambda j, s: (0, j)),
                   pl.BlockSpec((1, CONV_TC), lambda j, s: (0, j))],
        out_shape=[jax.ShapeDtypeStruct((t, D_FF), BF16), jax.ShapeDtypeStruct((t, D_FF), BF16),
                   jax.ShapeDtypeStruct((FF_CONV, D_FF), F32), jax.ShapeDtypeStruct((1, D_FF), F32)],
        compiler_params=_cp("parallel", "arbitrary"),
    )(dact, up, up, w, b)


SSD_PAIRS = SSD_HEADS // 2
PAIR_W = 2 * SSD_HEAD_DIM
PAIRS_PER_GROUP = SSD_PAIRS // SSD_GROUPS


def _ssd_chunk(xs, bg, cg, dtr, z, hp, dtb, alog, dskip, ng):
    n = dtr.shape[0]
    dt = _softplus(dtr + dtb)
    cs = _cumsum_rows(dt * (-jnp.exp(alog)))
    cs_t = _transpose(cs)
    lane = lax.broadcasted_iota(jnp.int32, (1, SSD_HEADS), 1)
    sub = lax.broadcasted_iota(jnp.int32, (SSD_HEADS, 1), 0)
    row = lax.broadcasted_iota(jnp.int32, (n, 1), 0)
    causal = lax.broadcasted_iota(jnp.int32, (n, n), 0) >= lax.broadcasted_iota(jnp.int32, (n, n), 1)
    future = jnp.where(causal, 0.0, -1e30)
    first = lax.broadcasted_iota(jnp.int32, (1, PAIR_W), 1) < SSD_HEAD_DIM
    first_rows = lax.broadcasted_iota(jnp.int32, (PAIR_W, 1), 0) < SSD_HEAD_DIM
    first_f = first.astype(F32)
    cb = [_bdot_nt(cg[g], bg[g]) for g in range(SSD_GROUPS)]
    ys, hn = [], []
    for p in range(SSD_PAIRS):
        g = p // PAIRS_PER_GROUP
        col, decay, last = [], [], []
        for h in (2 * p, 2 * p + 1):
            oh = (lane == h).astype(F32)
            cs_h = jnp.sum(cs * oh, axis=1, keepdims=True)
            cs_row = jnp.sum(cs_t * (sub == h).astype(F32), axis=0, keepdims=True)
            col.append((jnp.sum(dt * oh, axis=1, keepdims=True), cs_h, jnp.sum(dskip * oh, axis=1, keepdims=True)))
            last.append(jnp.sum(jnp.where(row == n - 1, cs_h, 0.0), axis=0, keepdims=True))
            decay.append(jnp.exp(cs_h - cs_row + future))
        pair = lambda a, b: jnp.where(first, a, b)
        dt_p = pair(col[0][0], col[1][0])
        cs_p = pair(col[0][1], col[1][1])
        last_p = pair(last[0], last[1])
        xc = xs[p] * dt_p
        y = _bdot(cb[g] * decay[0], xc * first_f) + _bdot(cb[g] * decay[1], xc * (1.0 - first_f))
        y = y + _bdot_nt(cg[g], hp[p]) * jnp.exp(cs_p)
        y = y + pair(col[0][2], col[1][2]) * xs[p]
        keep = jnp.where(first_rows, jnp.exp(last[0]), jnp.exp(last[1]))
        hn.append(keep * hp[p] + _bdot_tn(xc * jnp.exp(last_p - cs_p), bg[g]))
        ys.append(y * _silu(z[p]))
    outs = []
    for g in range(SSD_GROUPS):
        ps = range(g * PAIRS_PER_GROUP, (g + 1) * PAIRS_PER_GROUP)
        ms = sum(jnp.sum(ys[p] * ys[p], axis=1, keepdims=True) for p in ps) * (1.0 / GROUP_WIDTH)
        r = lax.rsqrt(ms + EPS)
        outs += [ys[p] * r * ng[p] for p in ps]
    return outs, hn


def _hslices(ref, width, count, base=0, rows=slice(None)):
    return [ref[rows, base + k * width: base + (k + 1) * width] for k in range(count)]


def _ssd_load(xbc_ref, z_ref, dt_ref, ng_ref, layer):
    xs = _hslices(xbc_ref, PAIR_W, SSD_PAIRS)
    bg = _hslices(xbc_ref, D_STATE, SSD_GROUPS, D_SSD)
    cg = _hslices(xbc_ref, D_STATE, SSD_GROUPS, D_SSD + SSD_GROUPS * D_STATE)
    z = _hslices(z_ref, PAIR_W, SSD_PAIRS)
    ng = _hslices(ng_ref, PAIR_W, SSD_PAIRS, rows=slice(layer, layer + 1))
    return xs, bg, cg, dt_ref[:, 0:SSD_HEADS], z, ng


def _ssd_specs(nch):
    rowi = lambda s, c: s * nch + c
    return [pl.BlockSpec((CHUNK, CONV_DIM), lambda s, c: (rowi(s, c), 0)),
            pl.BlockSpec((CHUNK, D_SSD), lambda s, c: (rowi(s, c), COL_Z // D_SSD)),
            pl.BlockSpec((CHUNK, 128), lambda s, c: (rowi(s, c), COL_DT // 128)),
            _per_layer(SSD_HEADS), _per_layer(SSD_HEADS), _per_layer(SSD_HEADS), _per_layer(D_SSD)]


def _ssd_fwd(xbc, proj, dtb, alog, dskip, ng, *, nseq, layer):
    t = proj.shape[0]
    nch = t // nseq // CHUNK
    hd = PAIR_W

    def body(xbc_ref, z_ref, dt_ref, dtb_ref, alog_ref, dsk_ref, ng_ref, y_ref, hp_ref, h_ref):
        @pl.when(pl.program_id(1) == 0)
        def _():
            h_ref[...] = jnp.zeros_like(h_ref)

        xs, bg, cg, dtr, z, ngs = _ssd_load(xbc_ref, z_ref, dt_ref, ng_ref, layer)
        hp_ref[0] = h_ref[...]
        hp = [h_ref[h * hd:(h + 1) * hd, :] for h in range(SSD_PAIRS)]
        outs, hn = _ssd_chunk(xs, bg, cg, dtr, z, hp, _row(dtb_ref, layer), _row(alog_ref, layer), _row(dsk_ref, layer), ngs)
        for h in range(SSD_PAIRS):
            y_ref[:, h * hd:(h + 1) * hd] = _b(outs[h])
            h_ref[h * hd:(h + 1) * hd, :] = hn[h]

    return pl.pallas_call(
        body, name="ssd_fwd", grid=(nseq, nch),
        in_specs=_ssd_specs(nch),
        out_specs=[pl.BlockSpec((CHUNK, D_SSD), lambda s, c: (s * nch + c, 0)),
                   pl.BlockSpec((1, D_SSD, D_STATE), lambda s, c: (s * nch + c, 0, 0))],
        out_shape=[jax.ShapeDtypeStruct((t, D_SSD + D_GM), BF16),
                   jax.ShapeDtypeStruct((t // CHUNK, D_SSD, D_STATE), F32)],
        scratch_shapes=[pltpu.VMEM((D_SSD, D_STATE), F32)],
        compiler_params=_cp("arbitrary", "arbitrary"),
    )(xbc, proj, proj, dtb, alog, dskip, ng)


def _ssd_bwd(dy, xbc, proj, hprev, dtb, alog, dskip, ng, *, nseq, layer):
    t = proj.shape[0]
    nch = t // nseq // CHUNK
    hd = PAIR_W
    rev = lambda s, c: s * nch + (nch - 1 - c)

    def body(dy_ref, xbc_ref, z_ref, dt_ref, hp_ref, dtb_ref, alog_ref, dsk_ref, ng_ref,
             dxbc_ref, dproj_ref, ddtb_ref, dalog_ref, ddsk_ref, dng_ref, dh_ref):
        first = (pl.program_id(0) == 0) & (pl.program_id(1) == 0)

        @pl.when(pl.program_id(1) == 0)
        def _():
            dh_ref[...] = jnp.zeros_like(dh_ref)

        @pl.when(first)
        def _():
            ddtb_ref[...] = jnp.zeros_like(ddtb_ref)
            dalog_ref[...] = jnp.zeros_like(dalog_ref)
            ddsk_ref[...] = jnp.zeros_like(ddsk_ref)
            dng_ref[...] = jnp.zeros_like(dng_ref)

        xs, bg, cg, dtr, z, ngs = _ssd_load(xbc_ref, z_ref, dt_ref, ng_ref, layer)
        hp = [hp_ref[0, h * hd:(h + 1) * hd, :] for h in range(SSD_PAIRS)]
        _, vjp = jax.vjp(_ssd_chunk, xs, bg, cg, dtr, z, hp, _row(dtb_ref, layer), _row(alog_ref, layer), _row(dsk_ref, layer), ngs)
        douts = [dy_ref[:, h * hd:(h + 1) * hd] for h in range(SSD_PAIRS)]
        dhn = [dh_ref[h * hd:(h + 1) * hd, :] for h in range(SSD_PAIRS)]
        dxs, dbg, dcg, ddtr, dz, dhp, ddtb, dalog, ddsk, dngs = vjp((douts, dhn))
        dproj_ref[:, :COL_Z] = jnp.zeros((CHUNK, COL_Z), BF16)
        dproj_ref[:, COL_XBC:] = jnp.zeros((CHUNK, N_INP - COL_XBC), BF16)
        for h in range(SSD_PAIRS):
            dxbc_ref[:, h * hd:(h + 1) * hd] = dxs[h]
            dproj_ref[:, COL_Z + h * hd: COL_Z + (h + 1) * hd] = _b(dz[h])
            dh_ref[h * hd:(h + 1) * hd, :] = dhp[h]
            dng_ref[:, h * hd:(h + 1) * hd] += dngs[h]
        for g in range(SSD_GROUPS):
            dxbc_ref[:, D_SSD + g * D_STATE: D_SSD + (g + 1) * D_STATE] = dbg[g]
            dxbc_ref[:, D_SSD + (SSD_GROUPS + g) * D_STATE: D_SSD + (SSD_GROUPS + g + 1) * D_STATE] = dcg[g]
        dproj_ref[:, COL_DT:COL_DT + SSD_HEADS] = _b(ddtr)
        ddtb_ref[...] += ddtb
        dalog_ref[...] += dalog
        ddsk_ref[...] += ddsk

    small = pl.BlockSpec((1, SSD_HEADS), lambda s, c: (0, 0))
    return pl.pallas_call(
        body, name="ssd_bwd", grid=(nseq, nch),
        in_specs=[pl.BlockSpec((CHUNK, D_SSD), lambda s, c: (rev(s, c), 0)),
                  pl.BlockSpec((CHUNK, CONV_DIM), lambda s, c: (rev(s, c), 0)),
                  pl.BlockSpec((CHUNK, D_SSD), lambda s, c: (rev(s, c), COL_Z // D_SSD)),
                  pl.BlockSpec((CHUNK, 128), lambda s, c: (rev(s, c), COL_DT // 128)),
                  pl.BlockSpec((1, D_SSD, D_STATE), lambda s, c: (rev(s, c), 0, 0)),
                  _per_layer(SSD_HEADS), _per_layer(SSD_HEADS), _per_layer(SSD_HEADS), _per_layer(D_SSD)],
        out_specs=[pl.BlockSpec((CHUNK, CONV_DIM), lambda s, c: (rev(s, c), 0)),
                   pl.BlockSpec((CHUNK, N_INP), lambda s, c: (rev(s, c), 0)),
                   small, small, small,
                   pl.BlockSpec((1, D_SSD), lambda s, c: (0, 0))],
        out_shape=[jax.ShapeDtypeStruct((t, CONV_DIM), F32), jax.ShapeDtypeStruct((t, N_INP), BF16),
                   jax.ShapeDtypeStruct((1, SSD_HEADS), F32), jax.ShapeDtypeStruct((1, SSD_HEADS), F32),
                   jax.ShapeDtypeStruct((1, SSD_HEADS), F32), jax.ShapeDtypeStruct((1, D_SSD), F32)],
        scratch_shapes=[pltpu.VMEM((D_SSD, D_STATE), F32)],
        compiler_params=_cp("arbitrary", "arbitrary"),
    )(dy, xbc, proj, proj, hprev, dtb, alog, dskip, ng)


def _gmlp_chunk(gu, gv, ws, bs_cols, vg, og):
    n = gu[0].shape[0]
    mask = _tri(n, True)
    au = [_gelu(t) for t in gu]
    av = [_gelu(t) for t in gv]
    r = lax.rsqrt(sum(jnp.sum(t * t, axis=1, keepdims=True) for t in av) * (1.0 / D_GM) + EPS)
    p = []
    for h in range(GM_HEADS):
        sv = _bdot(ws[h] * mask, av[h] * r * vg[h]) + bs_cols[h]
        p.append(au[h] * sv)
    r2 = lax.rsqrt(sum(jnp.sum(t * t, axis=1, keepdims=True) for t in p) * (1.0 / D_GM) + EPS)
    return [p[h] * r2 * og[h] for h in range(GM_HEADS)]


def _gmlp_load(u_ref, v_ref, ws_ref, bst_ref, vg_ref, og_ref, layer):
    gu = _hslices(u_ref, GM_HEAD_DIM, GM_HEADS)
    gv = _hslices(v_ref, GM_HEAD_DIM, GM_HEADS)
    ws = [ws_ref[h] for h in range(GM_HEADS)]
    bs_cols = [bst_ref[:, h:h + 1] for h in range(GM_HEADS)]
    mine = slice(layer, layer + 1)
    return (gu, gv, ws, bs_cols, _hslices(vg_ref, GM_HEAD_DIM, GM_HEADS, rows=mine),
            _hslices(og_ref, GM_HEAD_DIM, GM_HEADS, rows=mine))


def _gmlp_specs(layer):
    return [pl.BlockSpec((CHUNK, D_GM), lambda i: (i, COL_U // D_GM)),
            pl.BlockSpec((CHUNK, D_GM), lambda i: (i, COL_V // D_GM)),
            pl.BlockSpec((None, GM_HEADS, CHUNK, CHUNK), lambda i: (layer, 0, 0, 0)),
            pl.BlockSpec((None, CHUNK, GM_HEADS), lambda i: (layer, 0, 0)),
            _per_layer(D_GM), _per_layer(D_GM)]


def _gmlp_fwd(proj, ycat, ws, bst, vg, og, *, layer):
    t = proj.shape[0]

    def body(u_ref, v_ref, ws_ref, bst_ref, vg_ref, og_ref, ycat_ref, o_ref):
        del ycat_ref
        outs = _gmlp_chunk(*_gmlp_load(u_ref, v_ref, ws_ref, bst_ref, vg_ref, og_ref, layer))
        for h in range(GM_HEADS):
            o_ref[:, h * GM_HEAD_DIM:(h + 1) * GM_HEAD_DIM] = _b(outs[h])

    return pl.pallas_call(
        body, name="gmlp_fwd", grid=(t // CHUNK,),
        in_specs=_gmlp_specs(layer) + [ANY],
        out_specs=pl.BlockSpec((CHUNK, D_GM), lambda i: (i, D_SSD // D_GM)),
        out_shape=jax.ShapeDtypeStruct(ycat.shape, ycat.dtype),
        input_output_aliases={6: 0},
        compiler_params=_cp("parallel"),
    )(proj, proj, ws, bst, vg, og, ycat)


def _gmlp_bwd(dy, proj, ws, bst, vg, og, dproj, *, layer):
    t = proj.shape[0]
    w = GM_HEAD_DIM

    def body(dy_ref, u_ref, v_ref, ws_ref, bst_ref, vg_ref, og_ref, dproj_ref,
             dgm_ref, dws_ref, dbst_ref, dvg_ref, dog_ref):
        del dproj_ref

        @pl.when(pl.program_id(0) == 0)
        def _():
            dws_ref[...] = jnp.zeros_like(dws_ref)
            dbst_ref[...] = jnp.zeros_like(dbst_ref)
            dvg_ref[...] = jnp.zeros_like(dvg_ref)
            dog_ref[...] = jnp.zeros_like(dog_ref)

        _, vjp = jax.vjp(_gmlp_chunk, *_gmlp_load(u_ref, v_ref, ws_ref, bst_ref, vg_ref, og_ref, layer))
        dgu, dgv, dws, dbs, dvg, dog = vjp(_hslices(dy_ref, w, GM_HEADS))
        for h in range(GM_HEADS):
            dgm_ref[:, h * w:(h + 1) * w] = _b(dgu[h])
            dgm_ref[:, D_GM + h * w: D_GM + (h + 1) * w] = _b(dgv[h])
            dws_ref[h] += dws[h]
            dbst_ref[:, h:h + 1] += dbs[h]
            dvg_ref[:, h * w:(h + 1) * w] += dvg[h]
            dog_ref[:, h * w:(h + 1) * w] += dog[h]

    return pl.pallas_call(
        body, name="gmlp_bwd", grid=(t // CHUNK,),
        in_specs=[pl.BlockSpec((CHUNK, D_GM), lambda i: (i, 1))] + _gmlp_specs(layer) + [ANY],
        out_specs=[pl.BlockSpec((CHUNK, 2 * D_GM), lambda i: (i, COL_U // (2 * D_GM))),
                   pl.BlockSpec((GM_HEADS, CHUNK, CHUNK), lambda i: (0, 0, 0)),
                   pl.BlockSpec((CHUNK, GM_HEADS), lambda i: (0, 0)),
                   pl.BlockSpec((1, D_GM), lambda i: (0, 0)),
                   pl.BlockSpec((1, D_GM), lambda i: (0, 0))],
        out_shape=[jax.ShapeDtypeStruct(dproj.shape, dproj.dtype), jax.ShapeDtypeStruct((GM_HEADS, CHUNK, CHUNK), F32),
                   jax.ShapeDtypeStruct((CHUNK, GM_HEADS), F32), jax.ShapeDtypeStruct((1, D_GM), F32),
                   jax.ShapeDtypeStruct((1, D_GM), F32)],
        input_output_aliases={7: 0},
        compiler_params=_cp("arbitrary"),
    )(dy, proj, proj, ws, bst, vg, og, dproj)


def _local_step(x, target, mods, w, final_g, *, nseq, big_w, grad_sink, small_sink):
    saved = []
    x0, delta, gate = x, None, None
    h1 = _normmod_fwd(x, w["norm1_g"], mods[0][1], mods[0][0], nseq=nseq, name="norm1_fwd_0", layer=0)
    for l in range(DEPTH):
        sh1, sc1, g1, sh2, sc2, g2 = mods[l]
        w_in = big_w(l, "w_in", h1)
        proj = _matmul(h1, w_in, tb=True, name=f"mm_in_{l}")
        xbc, xbc_pre = _ssd_conv_fwd(proj, w["ssd_conv_w"], w["ssd_conv_b"], nseq=nseq, layer=l)
        ycat, hprev = _ssd_fwd(xbc, proj, w["ssd_dt_bias"], w["ssd_a_log"], w["ssd_d"], w["ssd_norm_g"], nseq=nseq,
                               layer=l)
        ycat = _gmlp_fwd(proj, ycat, w["gm_ws"], w["gm_bst"], w["gm_vnorm_g"], w["gm_out_g"], layer=l)
        w_out = big_w(l, "w_out", ycat)
        mix, x1, h2 = _matmul_normfwd(ycat, w_out, x0, g1, w["norm2_g"], sc2, sh2, nseq=nseq, name=f"mm_out_{l}",
                                      layer=l)
        ff_up = big_w(l, "ff_up", h2)
        up = _matmul(h2, ff_up, tb=True, name=f"mm_up_{l}", out_dtype=BF16)
        act = _ffn_act_fwd(up, w["ff_conv_w"], w["ff_conv_b"], nseq=nseq, layer=l)
        ff_down = big_w(l, "ff_down", act)
        sv = dict(x0=x0, xin_delta=delta, xin_gate=gate, h1=h1, proj=proj, xbc=xbc, xbc_pre=xbc_pre, hprev=hprev,
                  ycat=ycat, mix=mix, x1=x1, h2=h2, up=up, act=act,
                  w_in=w_in, w_out=w_out, ff_up=ff_up, ff_down=ff_down)
        if l + 1 < DEPTH:
            nsh1, nsc1 = mods[l + 1][0], mods[l + 1][1]
            dn, x0, h1 = _matmul_normfwd(act, ff_down, x1, g2, w["norm1_g"], nsc1, nsh1, nseq=nseq,
                                         name=f"mm_down_{l}", layer=l + 1)
        else:
            dn, loss, dx, ddelta, dgate, dfg = _matmul_loss(act, ff_down, x1, g2, final_g, target, nseq=nseq,
                                                            name=f"mm_down_{l}")
        saved.append(dict(sv, dn=dn))
        delta, gate = dn, g2

    small, dmods = [None] * DEPTH, [None] * DEPTH
    for l in reversed(range(DEPTH)):
        sv = saved[l]
        sh1, sc1, g1, sh2, sc2, g2 = mods[l]
        dg2 = dgate
        g_ff_down = _matmul(sv["act"], ddelta, ta=True, name=f"mm_down_dw_{l}", out_dtype=BF16)
        dact = _matmul(ddelta, sv["ff_down"], tb=True, name=f"mm_down_dx_{l}", out_dtype=BF16)
        dgate_ff, dval_ff, dfcw, dfcb = _ffn_act_bwd(dact, sv["up"], w["ff_conv_w"], w["ff_conv_b"], nseq=nseq, layer=l)
        g_ff_up = _matmul([dgate_ff, dval_ff], sv["h2"], ta=True, name=f"mm_up_dw_{l}", out_dtype=BF16)
        dep = grad_sink(l, "ffn", dict(ff_down=g_ff_down, ff_up=g_ff_up), dval_ff)
        dx, dmix, dg1, dn2g, dsc2, dsh2 = _matmul_normbwd([dgate_ff, dval_ff], sv["ff_up"], dx, sv["x1"], sv["mix"], g1,
                                                          w["norm2_g"], sc2, nseq=nseq, name=f"mm_up_dx_{l}", layer=l,
                                                          dep=dep)
        g_w_out = _matmul(sv["ycat"], dmix, ta=True, name=f"mm_out_dw_{l}", out_dtype=BF16)
        dep = grad_sink(l, "w_out", dict(w_out=g_w_out), dmix)
        dycat = _matmul(dmix, sv["w_out"], tb=True, name=f"mm_out_dx_{l}", dep=dep)
        dxbc_act, dproj, ddtb, dalog, ddsk, dng = _ssd_bwd(dycat, sv["xbc"], sv["proj"], sv["hprev"], w["ssd_dt_bias"],
                                                          w["ssd_a_log"], w["ssd_d"], w["ssd_norm_g"], nseq=nseq, layer=l)
        dproj, dscw, dscb = _ssd_conv_bwd(dxbc_act, sv["xbc_pre"], sv["proj"], w["ssd_conv_w"], dproj, nseq=nseq,
                                          layer=l)
        dproj, dws, dbst, dvg, dog = _gmlp_bwd(dycat, sv["proj"], w["gm_ws"], w["gm_bst"], w["gm_vnorm_g"], w["gm_out_g"],
                                               dproj, layer=l)
        early = dict(norm2_g=dn2g, ssd_norm_g=dng, gm_vnorm_g=dvg, gm_out_g=dog,
                     ssd_conv_w=dscw, ssd_conv_b=dscb, ff_conv_w=dfcw, ff_conv_b=dfcb,
                     ssd_dt_bias=ddtb, ssd_a_log=dalog, ssd_d=ddsk, gm_ws=dws, gm_bs=dbst.T)
        dep = small_sink(l, early, small, dmods, dfg, loss)
        g_w_in = _matmul(dproj, sv["h1"], ta=True, name=f"mm_in_dw_{l}", out_dtype=BF16, dep=dep)
        dep = grad_sink(l, "w_in", dict(w_in=g_w_in), dproj)
        dx, ddelta, dgate, dn1g, dsc1, dsh1 = _matmul_normbwd(dproj, sv["w_in"], dx, sv["x0"], sv["xin_delta"],
                                                              sv["xin_gate"], w["norm1_g"], sc1, nseq=nseq,
                                                              name=f"mm_in_dx_{l}", layer=l, dep=dep)
        small[l] = dict(early, norm1_g=dn1g)
        dmods[l] = jnp.concatenate([dsh1, dsc1, dg1, dsh2, dsc2, dg2], axis=-1)[:, 0, :]
    return dx, small, dmods


def _all_gather(arrs, name, dep=None):
    n = len(arrs)
    extra = [] if dep is None else [dep]

    def body(*refs):
        ins, outs = refs[:n], refs[n + len(extra):2 * n + len(extra)]
        send_sems, recv_sems, local_sems = refs[2 * n + len(extra):]
        x, y, c = lax.axis_index("x"), lax.axis_index("y"), lax.axis_index("c")
        me, sibling = (x, y, c), (x, y, 1 - c)
        chips = [(1 - x, y), (x, 1 - y), (1 - x, 1 - y)]

        def copy(i, k, block, to, src=None):
            px, py, pc = block
            dst = outs[i].at[4 * px + 2 * py + pc]
            return pltpu.make_async_remote_copy(
                src_ref=dst if src is None else src, dst_ref=dst,
                send_sem=send_sems.at[7 * i + k], recv_sem=recv_sems.at[7 * i + k],
                device_id=to, device_id_type=MESH)

        mine = [pltpu.make_async_copy(ins[i], outs[i].at[4 * x + 2 * y + c], local_sems.at[i]) for i in range(n)]
        for cp in mine:
            cp.start()
        first = []
        for i in range(n):
            first.append(copy(i, 0, me, sibling, src=ins[i]))
            first += [copy(i, 1 + j, me, (*chip, c), src=ins[i]) for j, chip in enumerate(chips)]
        for cp in first:
            cp.start()
        passed = []
        for j, chip in enumerate(chips):
            for i in range(n):
                copy(i, 1 + j, (*chip, c), me).wait_recv()
                fwd = copy(i, 4 + j, (*chip, c), sibling)
                fwd.start()
                passed.append(fwd)
        for i in range(n):
            copy(i, 0, sibling, me).wait_recv()
            for j, chip in enumerate(chips):
                copy(i, 4 + j, (*chip, 1 - c), me).wait_recv()
        for cp in first + passed:
            cp.wait_send()
        for cp in mine:
            cp.wait()

    return pl.pallas_call(
        body, name=name,
        in_specs=[ANY] * (n + len(extra)), out_specs=[ANY] * n,
        out_shape=[jax.ShapeDtypeStruct((N_DEV,) + a.shape, a.dtype) for a in arrs],
        scratch_shapes=[pltpu.SemaphoreType.DMA((7 * n,)), pltpu.SemaphoreType.DMA((7 * n,)),
                        pltpu.SemaphoreType.DMA((n,))],
    )(*arrs, *extra)


HBM = pl.BlockSpec(memory_space=pltpu.HBM)
SEM = pl.BlockSpec(memory_space=pltpu.SEMAPHORE)
EFFECT = pltpu.SideEffectType.DATAFLOW_SIDE_EFFECTING


def _peer(k):
    x, y, c = lax.axis_index("x"), lax.axis_index("y"), lax.axis_index("c")
    return (1 - x if k & 4 else x, 1 - y if k & 2 else y, 1 - c if k & 1 else c)


ALL_PEERS = tuple(range(1, N_DEV))
OTHER_CHIPS = (2, 4, 6)


def _xc_copies(scatter, srcs, lands, send_sems, recv_sems, peers=ALL_PEERS):
    x, y, c = lax.axis_index("x"), lax.axis_index("y"), lax.axis_index("c")
    copies = []
    for i in range(len(srcs)):
        for k in (peers[i] if isinstance(peers[0], tuple) else peers):
            px, py, pc = _peer(k)
            src = srcs[i].at[4 * px + 2 * py + pc] if scatter else srcs[i]
            dst = lands[i].at[k - 1] if scatter else lands[i].at[4 * x + 2 * y + c]
            copies.append(pltpu.make_async_remote_copy(
                src_ref=src, dst_ref=dst, send_sem=send_sems[i].at[k - 1], recv_sem=recv_sems[i].at[k - 1],
                device_id=(px, py, pc), device_id_type=MESH))
    return copies


def _xc_own(scatter, srcs, lands, send_sems):
    if scatter:
        return []
    me = 4 * lax.axis_index("x") + 2 * lax.axis_index("y") + lax.axis_index("c")
    return [pltpu.make_async_copy(srcs[i], lands[i].at[me], send_sems[i].at[N_DEV - 1]) for i in range(len(srcs))]


def _xc_start(scatter, arrs, after, name, peers=ALL_PEERS):
    n = len(arrs)
    lands = [lax.empty((N_DEV - 1,) + a.shape[1:] if scatter else (N_DEV,) + a.shape, a.dtype) for a in arrs]

    def body(*refs):
        srcs, lnd = refs[:n], refs[n:2 * n]
        send_sems, recv_sems = refs[2 * n + 1:3 * n + 1], refs[3 * n + 1:4 * n + 1]
        token = refs[6 * n + 1]
        for cp in _xc_copies(scatter, srcs, lnd, send_sems, recv_sems, peers) + _xc_own(scatter, srcs, lnd, send_sems):
            cp.start()
        token[...] = jnp.zeros_like(token)

    outs = pl.pallas_call(
        body, name=name,
        out_shape=[pltpu.SemaphoreType.DMA((N_DEV,))] * (2 * n)
        + [pltpu.HBM(a.shape, a.dtype) for a in arrs] + [pltpu.HBM(a.shape, a.dtype) for a in lands]
        + [jax.ShapeDtypeStruct((8, 128), F32)],
        in_specs=[HBM] * (2 * n) + [ANY],
        out_specs=[SEM] * (2 * n) + [HBM] * (2 * n) + [pl.BlockSpec(memory_space=pltpu.VMEM)],
        input_output_aliases={i: 2 * n + i for i in range(2 * n)},
        compiler_params=pltpu.CompilerParams(has_side_effects=EFFECT),
    )(*[pltpu.with_memory_space_constraint(a, pltpu.HBM) for a in list(arrs) + lands], after)
    return outs[:n], outs[n:2 * n], outs[2 * n:3 * n], outs[3 * n:4 * n], outs[4 * n][0, 0]


def _xc_wait(scatter, send_sems, recv_sems, srcs, lands, after, name, peers=ALL_PEERS):
    n = len(srcs)

    def body(*refs):
        s_refs, l_refs = refs[:n], refs[n:2 * n]
        ss, rs = refs[2 * n:3 * n], refs[3 * n:4 * n]
        for cp in _xc_copies(scatter, s_refs, l_refs, ss, rs, peers):
            cp.wait_send()
            cp.wait_recv()
        for cp in _xc_own(scatter, s_refs, l_refs, ss):
            cp.wait()

    outs = pl.pallas_call(
        body, name=name,
        out_shape=[pltpu.HBM(a.shape, a.dtype) for a in list(srcs) + list(lands)],
        in_specs=[HBM] * (2 * n) + [SEM] * (2 * n) + [ANY],
        out_specs=[HBM] * (2 * n),
        input_output_aliases={i: i for i in range(2 * n)},
        compiler_params=pltpu.CompilerParams(has_side_effects=EFFECT),
    )(*srcs, *lands, *send_sems, *recv_sems, after)
    return outs[:n], outs[n:]


def _sib_copies(zones, send_sems, recv_sems):
    x, y, c = lax.axis_index("x"), lax.axis_index("y"), lax.axis_index("c")
    copies = []
    for i in range(len(zones)):
        for q in range(N_DEV // 2):
            slot = zones[i].at[2 * q + c]
            copies.append(pltpu.make_async_remote_copy(
                src_ref=slot, dst_ref=slot, send_sem=send_sems[i].at[q], recv_sem=recv_sems[i].at[q],
                device_id=(x, y, 1 - c), device_id_type=MESH))
    return copies


def _sib_start(zones, name):
    n = len(zones)

    def body(*refs):
        for cp in _sib_copies(refs[:n], refs[n:2 * n], refs[2 * n:3 * n]):
            cp.start()

    outs = pl.pallas_call(
        body, name=name,
        out_shape=[pltpu.SemaphoreType.DMA((N_DEV // 2,))] * (2 * n) + [pltpu.HBM(a.shape, a.dtype) for a in zones],
        in_specs=[HBM] * n,
        out_specs=[SEM] * (2 * n) + [HBM] * n,
        input_output_aliases={i: 2 * n + i for i in range(n)},
        compiler_params=pltpu.CompilerParams(has_side_effects=EFFECT),
    )(*[pltpu.with_memory_space_constraint(a, pltpu.HBM) for a in zones])
    return outs[:n], outs[n:2 * n], outs[2 * n:]


def _sib_wait(send_sems, recv_sems, zones, name):
    n = len(zones)

    def body(*refs):
        for cp in _sib_copies(refs[:n], refs[n:2 * n], refs[2 * n:3 * n]):
            cp.wait_send()
            cp.wait_recv()

    return pl.pallas_call(
        body, name=name,
        out_shape=[pltpu.HBM(a.shape, a.dtype) for a in zones],
        in_specs=[HBM] * n + [SEM] * (2 * n),
        out_specs=[HBM] * n,
        input_output_aliases={i: i for i in range(n)},
        compiler_params=pltpu.CompilerParams(has_side_effects=EFFECT),
    )(*zones, *send_sems, *recv_sems)


def _adamw_math(w, g, m, v):
    m = ADAM_B1 * m + (1.0 - ADAM_B1) * g
    v = ADAM_B2 * v + (1.0 - ADAM_B2) * (g * g)
    m_hat = m / (1.0 - ADAM_B1 ** ADAM_STEP)
    v_hat = v / (1.0 - ADAM_B2 ** ADAM_STEP)
    delta = -ADAM_LR * (m_hat / (jnp.sqrt(v_hat) + ADAM_EPS) + ADAM_WD * w)
    return delta, m, v


def _adamw_sharded(parts, w, m, v, pos, name):
    depth, rows, cols = w.shape
    tr = _tile(rows, 256) if rows % 8 == 0 else rows
    npart = len(parts)

    def body(pos_ref, *refs):
        prefs = refs[:npart]
        w_ref, m_ref, v_ref, g_out, d_out, m_out, v_out = refs[npart:]
        g = prefs[0][...]
        for pr in prefs[1:]:
            g = g + pr[...]
        delta, mn, vn = _adamw_math(w_ref[...], g, m_ref[...], v_ref[...])
        g_out[...] = g
        d_out[...] = delta
        m_out[...] = mn
        v_out[...] = vn

    def part_spec(fn):
        return pl.BlockSpec((1, tr, cols), lambda l, i, p: (fn(p) * depth + l, i, 0))

    blk = pl.BlockSpec((1, tr, cols), lambda l, i, p: (l, i, 0))
    shp = jax.ShapeDtypeStruct((depth, rows, cols), F32)
    return pl.pallas_call(
        body, name=name,
        grid_spec=pltpu.PrefetchScalarGridSpec(
            num_scalar_prefetch=1, grid=(depth, rows // tr),
            in_specs=[part_spec(fn) for _, fn in parts] + [blk, blk, blk],
            out_specs=[blk, blk, blk, blk]),
        out_shape=[shp, shp, shp, shp],
        compiler_params=_cp("parallel", "parallel"),
    )(pos, *[a for a, _ in parts], w, m, v)


def _adamw_layer(parts, w, m, v, pos, layer, prev, name):
    depth, rows, cols = w.shape
    npart = len(parts)
    nprev = 0 if prev is None else 4
    if rows % 16 == 0:
        tr, tc = max(t for t in range(16, 257, 16) if rows % t == 0), cols
    else:
        tr, tc = rows, _tile(cols, 256)
    pick = (lambda i: (i, 0)) if rows % 16 == 0 else (lambda i: (0, i))

    def body(pos_ref, *refs):
        prefs = refs[:npart]
        w_ref, m_ref, v_ref = refs[npart:npart + 3]
        g_out, d_out, m_out, v_out = refs[npart + 3 + nprev:]
        g = prefs[0][...].astype(F32)
        for pr in prefs[1:]:
            g = g + pr[...].astype(F32)
        delta, mn, vn = _adamw_math(w_ref[...], g, m_ref[...], v_ref[...])
        g_out[...] = g
        d_out[...] = delta
        m_out[...] = mn
        v_out[...] = vn

    def part_spec(fn):
        return pl.BlockSpec((1, tr, tc), lambda i, p: (fn(p), *pick(i)))

    blk = pl.BlockSpec((1, tr, tc), lambda i, p: (layer, *pick(i)))
    shp = jax.ShapeDtypeStruct((depth, rows, cols), F32)
    first_prev = 1 + npart + 3
    return pl.pallas_call(
        body, name=name,
        grid_spec=pltpu.PrefetchScalarGridSpec(
            num_scalar_prefetch=1, grid=(rows // tr * (cols // tc),),
            in_specs=[part_spec(fn) for _, fn in parts] + [blk, blk, blk] + [ANY] * nprev,
            out_specs=[blk, blk, blk, blk]),
        out_shape=[shp, shp, shp, shp],
        input_output_aliases={first_prev + j: j for j in range(nprev)},
        compiler_params=_cp("parallel"),
    )(pos, *[a for a, _ in parts], w, m, v, *(prev or ()))


def _adamw_rows_major(parts_by_layer, w, m, v, pos, name):
    rows, depth, cols = w.shape
    tc = _tile(cols, 256)
    npart = len(parts_by_layer[0])

    def body(pos_ref, *refs):
        prefs = refs[:depth * npart]
        w_ref, m_ref, v_ref, g_out, d_out, m_out, v_out = refs[depth * npart:]
        for l in range(depth):
            g = prefs[l * npart][0].astype(F32)
            for pr in prefs[l * npart + 1:(l + 1) * npart]:
                g = g + pr[0].astype(F32)
            delta, mn, vn = _adamw_math(w_ref[:, l, :], g, m_ref[:, l, :], v_ref[:, l, :])
            g_out[:, l, :] = g
            d_out[:, l, :] = delta
            m_out[:, l, :] = mn
            v_out[:, l, :] = vn

    def part_spec(fn):
        return pl.BlockSpec((1, rows, tc), lambda j, p: (fn(p), 0, j))

    blk = pl.BlockSpec((rows, depth, tc), lambda j, p: (0, 0, j))
    shp = jax.ShapeDtypeStruct(w.shape, F32)
    flat = [pf for parts in parts_by_layer for pf in parts]
    return pl.pallas_call(
        body, name=name,
        grid_spec=pltpu.PrefetchScalarGridSpec(
            num_scalar_prefetch=1, grid=(cols // tc,),
            in_specs=[part_spec(fn) for _, fn in flat] + [blk, blk, blk],
            out_specs=[blk, blk, blk, blk]),
        out_shape=[shp, shp, shp, shp],
        compiler_params=_cp("parallel"),
    )(pos, *[a for a, _ in flat], w, m, v)


_P1024 = ["norm1_g", "norm2_g", "ssd_norm_g", "gm_vnorm_g", "gm_out_g"]
_P16 = ["ssd_dt_bias", "ssd_a_log", "ssd_d"]


def _adamw_small(gath, wmv):
    names = list(wmv.keys())
    classes = list(gath.keys())
    flat_in = [gath[k] for k in classes]
    for nme in names:
        flat_in += list(wmv[nme])
    out_shapes = []
    for nme in names:
        out_shapes += [jax.ShapeDtypeStruct(wmv[nme][0].shape, F32)] * 4
    out_shapes += [jax.ShapeDtypeStruct((DEPTH, SSD_CONV, CONV_DIM), F32), jax.ShapeDtypeStruct((DEPTH, FF_CONV, D_FF), F32),
                   jax.ShapeDtypeStruct((1, SSD_HEADS), F32)]
    scratch = [pltpu.VMEM(gath[k].shape[1:], F32) for k in classes]
    ncls = len(classes)

    def body(*refs):
        g_refs = dict(zip(classes, refs[:ncls]))
        pos = ncls
        w_refs = {}
        for nme in names:
            w_refs[nme] = refs[pos:pos + 3]
            pos += 3
        o_refs = {}
        for nme in names:
            o_refs[nme] = refs[pos:pos + 4]
            pos += 4
        scw_out, fcw_out, loss_out = refs[pos], refs[pos + 1], refs[pos + 2]
        s_refs = dict(zip(classes, refs[pos + 3:]))
        for k in classes:
            acc = g_refs[k][0]
            for dev in range(1, N_DEV):
                acc = acc + g_refs[k][dev]
            s_refs[k][...] = acc

        def apply(nme, grad_of):
            w_ref, m_ref, v_ref = w_refs[nme]
            g_out, d_out, m_out, v_out = o_refs[nme]
            shape = w_ref.shape
            if len(shape) == 2:
                idxs = [(slice(l, l + 1),) for l in range(shape[0])]
            elif len(shape) == 3:
                idxs = [(l,) for l in range(shape[0])]
            else:
                idxs = [(l, h) for l in range(shape[0]) for h in range(shape[1])]
            for n_i, ix in enumerate(idxs):
                g = grad_of(n_i)
                delta, mn, vn = _adamw_math(w_ref[ix], g, m_ref[ix], v_ref[ix])
                g_out[ix] = g
                d_out[ix] = delta
                m_out[ix] = mn
                v_out[ix] = vn

        s1024, s1536, s2816, s16, s128, s6144, late1024, late6144 = (s_refs[k] for k in classes)
        s1024[0:1, :] += late1024[...]
        s6144[0:late6144.shape[0], :] += late6144[...]
        for n_i, nme in enumerate(_P1024):
            apply(nme, lambda l, b=2 * n_i: s1024[b + l:b + l + 1, :])
        apply("final_g", lambda l: s1024[10:11, :])
        apply("ssd_conv_b", lambda l: s1536[8 + l:9 + l, :])
        apply("ff_conv_b", lambda l: s2816[6 + l:7 + l, :])
        for n_i, nme in enumerate(_P16):
            apply(nme, lambda l, b=2 * n_i: s16[b + l:b + l + 1, :])
        apply("gm_ws", lambda q: s128[q * CHUNK:(q + 1) * CHUNK, :])
        apply("gm_bs", lambda l: s128[2048 + 8 * l:2048 + 8 * (l + 1), :])
        apply("ada_b", lambda l: s6144[2 * l:2 * l + 1, :] + s6144[2 * l + 1:2 * l + 2, :])
        for l in range(DEPTH):
            scw_out[l] = s1536[SSD_CONV * l:SSD_CONV * (l + 1), :]
            fcw_out[l] = s2816[FF_CONV * l:FF_CONV * (l + 1), :]
        loss_out[...] = s16[2 * len(_P16):2 * len(_P16) + 1, :]

    outs = pl.pallas_call(
        body, name="adamw_small",
        out_shape=out_shapes,
        scratch_shapes=scratch,
        compiler_params=pltpu.CompilerParams(vmem_limit_bytes=VMEM_LIMIT),
    )(*flat_in)
    res = {nme: tuple(outs[4 * i:4 * i + 4]) for i, nme in enumerate(names)}
    return res, outs[-3], outs[-2], outs[-1]


_WEIGHTS = ['ada_w', 'ada_b', 'norm1_g', 'norm2_g', 'w_in', 'ssd_conv_w', 'ssd_conv_b', 'ssd_dt_bias', 'ssd_a_log',
            'ssd_d', 'ssd_norm_g', 'gm_vnorm_g', 'gm_ws', 'gm_bs', 'gm_out_g', 'w_out', 'ff_up', 'ff_conv_w',
            'ff_conv_b', 'ff_down', 'final_g']


_O_XBC, _O_DT, _O_GM = D_SSD, D_SSD + CONV_DIM, D_SSD + CONV_DIM + SSD_HEADS


_TRANSPOSED = ("w_in", "ff_up")


def _full_weight(name, g):
    full = g.reshape(g.shape[0] * g.shape[1], g.shape[2])
    if name != "w_in":
        return full
    zpad = jnp.zeros((N_INP - N_IN, full.shape[1]), full.dtype)
    return jnp.concatenate([full[_O_GM:], full[:_O_XBC], full[_O_XBC:_O_DT], full[_O_DT:_O_GM], zpad], axis=0)


def _by_owner(name, grad):
    if name == "w_in":
        grad = jnp.concatenate([grad[COL_Z:COL_XBC], grad[COL_XBC:COL_DT], grad[COL_DT:COL_DT + SSD_HEADS], grad[:COL_Z]], axis=0)
    return grad.reshape(N_DEV, grad.shape[0] // N_DEV, grad.shape[1])


def kernel(x, c, ada_w, ada_b, norm1_g, norm2_g, w_in, ssd_conv_w, ssd_conv_b, ssd_dt_bias, ssd_a_log, ssd_d, ssd_norm_g, gm_vnorm_g, gm_ws, gm_bs, gm_out_g, w_out, ff_up, ff_conv_w, ff_conv_b, ff_down, final_g, loss_target, m_ada_w, m_ada_b, m_norm1_g, m_norm2_g, m_w_in, m_ssd_conv_w, m_ssd_conv_b, m_ssd_dt_bias, m_ssd_a_log, m_ssd_d, m_ssd_norm_g, m_gm_vnorm_g, m_gm_ws, m_gm_bs, m_gm_out_g, m_w_out, m_ff_up, m_ff_conv_w, m_ff_conv_b, m_ff_down, m_final_g, v_ada_w, v_ada_b, v_norm1_g, v_norm2_g, v_w_in, v_ssd_conv_w, v_ssd_conv_b, v_ssd_dt_bias, v_ssd_a_log, v_ssd_d, v_ssd_norm_g, v_gm_vnorm_g, v_gm_ws, v_gm_bs, v_gm_out_g, v_w_out, v_ff_up, v_ff_conv_w, v_ff_conv_b, v_ff_down, v_final_g):
    given = dict(locals())
    wts = {n: given[n] for n in _WEIGHTS}
    mom = {n: given["m_" + n] for n in _WEIGHTS}
    var = {n: given["v_" + n] for n in _WEIGHTS}
    nseq, seq, d = x.shape
    ix, iy, ic = lax.axis_index("x"), lax.axis_index("y"), lax.axis_index("c")
    me = 4 * ix + 2 * iy + ic
    me_arr = me.astype(jnp.int32).reshape(1)

    for nme, perm in (("ff_up", (0, 2, 1)), ("w_in", (2, 0, 1))):
        wts[nme], mom[nme], var[nme] = (jnp.transpose(a, perm) for a in (wts[nme], mom[nme], var[nme]))

    def shard(l, name):
        return _b(wts[name][:, l, :] if name == "w_in" else wts[name][l])

    g_scw, g_fcw, c_all = _all_gather([ssd_conv_w, ff_conv_w, c], "gather_first")
    scw_f = jnp.transpose(g_scw, (1, 2, 0, 3)).reshape(DEPTH, SSD_CONV, CONV_DIM)
    fcw_f = jnp.transpose(g_fcw, (1, 2, 0, 3)).reshape(DEPTH, FF_CONV, D_FF)
    c_all = c_all.reshape(N_DEV * nseq, d)

    n_ada = ada_w.shape[2]
    ada_b_shard = lax.dynamic_slice_in_dim(ada_b, me * n_ada, n_ada, axis=1).reshape(DEPTH, 1, n_ada)
    mod_part, c_act = _ada_fwd(c_all, ada_w, ada_b_shard)
    first_ssem, first_rsem, first_src, first_land, first_zero = _xc_start(
        False, [mod_part, shard(0, "w_in")], c_act, "ag_first_start", peers=[ALL_PEERS, OTHER_CHIPS])
    _, (mod_g,) = _xc_wait(False, first_ssem[:1], first_rsem[:1], first_src[:1], first_land[:1], c_act,
                           "mod_wait")
    mod_all = jnp.transpose(mod_g, (1, 2, 0, 3)).reshape(DEPTH, N_DEV * nseq, N_MOD * d)
    mod_mine = lax.dynamic_slice_in_dim(mod_all, me * nseq, nseq, axis=1)
    mod_k = jnp.transpose(mod_mine.reshape(DEPTH, nseq, N_MOD, 1, d), (0, 2, 1, 3, 4))
    mods = [[mod_k[l, k] for k in range(N_MOD)] for l in range(DEPTH)]

    later =[(0, "w_out"), (0, "ff_up"), (0, "ff_down"), (1, "w_in"), (1, "w_out"), (1, "ff_up"), (1, "ff_down")]
    ag_groups = {(0, "w_out"): [0], (0, "ff_up"): [1, 2], (1, "w_in"): [3, 4], (1, "ff_up"): [5, 6]}
    big_cache, ag = {}, {}

    def big_w(l, name, after):
        if (l, name) == (0, "w_in") and (l, name) not in big_cache:
            ag["ssem"], ag["rsem"], ag["src"], ag["land"], started = _xc_start(
                False, [shard(l2, n2) for l2, n2 in later], after, "ag_start")
            _, zones = _xc_wait(False, first_ssem[1:], first_rsem[1:], first_src[1:], first_land[1:],
                                jnp.full((8, 128), started, F32), "ag_first_wait", peers=OTHER_CHIPS)
            (zone,) = _sib_wait(*_sib_start(zones, "ag_first_sib_start"), "ag_first_sib_wait")
            big_cache[(l, name)] = _full_weight(name, zone)
        if (l, name) not in big_cache:
            idx = ag_groups[(l, name)]
            pick = lambda seq_: [seq_[i] for i in idx]
            _, lands = _xc_wait(False, pick(ag["ssem"]), pick(ag["rsem"]), pick(ag["src"]), pick(ag["land"]), after,
                                f"ag_wait_{l}_{name}")
            for i, land in zip(idx, lands):
                big_cache[later[i]] = _full_weight(later[i][1], land)
        return big_cache[(l, name)]

    small_w = dict(
        norm1_g=norm1_g + first_zero, norm2_g=norm2_g, ssd_conv_w=scw_f, ssd_conv_b=ssd_conv_b, ssd_dt_bias=ssd_dt_bias,
        ssd_a_log=ssd_a_log, ssd_d=ssd_d, ssd_norm_g=ssd_norm_g, gm_vnorm_g=gm_vnorm_g, gm_ws=gm_ws,
        gm_bst=jnp.transpose(gm_bs, (0, 2, 1)), gm_out_g=gm_out_g, ff_conv_w=fcw_f, ff_conv_b=ff_conv_b)

    outs = {}
    pending, win_parts = {}, {}

    def rs_finish(l, group, after):
        names, ssem, rsem, srcs, lands = pending.pop((l, group))
        srcs, lands = _xc_wait(True, ssem, rsem, srcs, lands, after, f"rs_wait_{l}_{group}")
        for nme, own, land in zip(names, srcs, lands):
            parts = [(own, lambda p: p[0])] + [(land, lambda p, k=k: k) for k in range(N_DEV - 1)]
            if nme == "w_in":
                win_parts[l] = parts
                if len(win_parts) == DEPTH:
                    outs[nme] = _adamw_rows_major([win_parts[k] for k in range(DEPTH)], wts[nme], mom[nme], var[nme],
                                                  me_arr, "adamw_w_in")
                continue
            outs[nme] = _adamw_layer(parts, wts[nme], mom[nme], var[nme], me_arr, l, outs.get(nme), f"adamw_{nme}_{l}")
        return land if names[-1] == "w_in" else outs[names[-1]][0]

    def grad_sink(l, group, grads, after):
        names = list(grads)
        ssem, rsem, srcs, lands, zero = _xc_start(True, [_by_owner(n, grads[n]) for n in names], after, f"rs_start_{l}_{group}")
        pending[(l, group)] = (names, ssem, rsem, srcs, lands)
        return zero.reshape(1, 1)

    early_gather = {}

    def small_sink(l, early, small, dmods, dfg, loss_p):
        if l > 0:
            return None
        layers = [dict(early, norm1_g=jnp.zeros((1, d), F32))] + small[1:]
        rows = lambda name: [layers[k][name] for k in range(DEPTH)]
        packed = [
            jnp.concatenate(sum([rows(n) for n in _P1024], []) + [dfg], axis=0),
            jnp.concatenate(rows("ssd_conv_w") + rows("ssd_conv_b"), axis=0),
            jnp.concatenate(rows("ff_conv_w") + rows("ff_conv_b"), axis=0),
            jnp.concatenate(sum([rows(n) for n in _P16], []) + [loss_p[:, :SSD_HEADS]], axis=0),
            jnp.concatenate([layers[k]["gm_ws"].reshape(GM_HEADS * CHUNK, CHUNK) for k in range(DEPTH)] + rows("gm_bs"), axis=0),
            jnp.concatenate([jnp.zeros((nseq, N_MOD * d), F32)] + dmods[1:], axis=0)]
        ssem, rsem, srcs, lands, zero = _xc_start(False, packed, packed[0], "small_start")
        early_gather.update(ssem=ssem, rsem=rsem, srcs=srcs, lands=lands)
        return zero.reshape(1, 1)

    grad_x, small, dmods = _local_step(
        x.reshape(nseq * seq, d), loss_target.reshape(nseq * seq, d), mods, small_w, final_g.reshape(1, d), nseq=nseq,
        big_w=big_w, grad_sink=grad_sink, small_sink=small_sink)

    done = grad_x
    for l, grp in ((1, "ffn"), (1, "w_out"), (1, "w_in"), (0, "ffn"), (0, "w_out")):
        done = rs_finish(l, grp, done)
    _, gathered = _xc_wait(False, early_gather["ssem"], early_gather["rsem"], early_gather["srcs"],
                           early_gather["lands"], done, "small_wait")
    gathered = list(gathered)
    gathered += _all_gather([small[0]["norm1_g"], dmods[0]], "gather_late", dep=gathered[0])
    gath = dict(zip(["p1024", "p1536", "p2816", "p16", "p128", "p6144", "late1024", "late6144"], gathered))

    dmod_all = jnp.concatenate([gath["late6144"].reshape(1, N_DEV * nseq, N_MOD * d),
                                jnp.transpose(gath["p6144"].reshape(N_DEV, DEPTH, nseq, N_MOD * d)[:, 1:], (1, 0, 2, 3)).reshape(
                                    DEPTH - 1, N_DEV * nseq, N_MOD * d)], axis=0)
    small_names = _P1024 + ["final_g", "ssd_conv_b", "ff_conv_b"] + _P16 + ["gm_ws", "gm_bs", "ada_b"]
    wmv = {}
    for nme in small_names:
        if nme == "final_g":
            wmv[nme] = tuple(a.reshape(1, d) for a in (wts[nme], mom[nme], var[nme]))
        else:
            wmv[nme] = (wts[nme], mom[nme], var[nme])
    small_out, scw_full, fcw_full, loss_sum = _adamw_small(gath, wmv)
    loss = loss_sum[0, 0]
    rs_finish(0, "w_in", scw_full)
    for nme in small_names:
        outs[nme] = small_out[nme]
    outs["final_g"] = tuple(a.reshape(d) for a in outs["final_g"])

    n_scw, n_fcw = ssd_conv_w.shape[2], ff_conv_w.shape[2]
    g_scw_mine = lax.dynamic_slice_in_dim(scw_full, me * n_scw, n_scw, axis=2)
    g_fcw_mine = lax.dynamic_slice_in_dim(fcw_full, me * n_fcw, n_fcw, axis=2)
    outs["ssd_conv_w"] = _adamw_sharded([(g_scw_mine, lambda p: 0)], ssd_conv_w, m_ssd_conv_w, v_ssd_conv_w, me_arr, "adamw_ssd_conv_w")
    outs["ff_conv_w"] = _adamw_sharded([(g_fcw_mine, lambda p: 0)], ff_conv_w, m_ff_conv_w, v_ff_conv_w, me_arr, "adamw_ff_conv_w")

    dmod_cols = _b(lax.dynamic_slice_in_dim(dmod_all, me * n_ada, n_ada, axis=2))
    g_ada = jnp.stack([_matmul(c_act, dmod_cols[l], ta=True, name=f"mm_ada_dw_{l}") for l in range(DEPTH)])
    outs["ada_w"] = _adamw_sharded([(g_ada, lambda p: 0)], ada_w, m_ada_w, v_ada_w, me_arr, "adamw_ada_w")

    for nme, perm in (("ff_up", (0, 2, 1)), ("w_in", (1, 2, 0))):
        outs[nme] = tuple(jnp.transpose(a, perm) for a in outs[nme])
    result = [loss, grad_x.reshape(nseq, seq, d)]
    for k in range(4):
        result += [outs[n][k] for n in _WEIGHTS]
    return tuple(result)
```

```python
import functools
import math

import jax
import jax.numpy as jnp
from jax import lax
from jax.experimental import pallas as pl
from jax.experimental.pallas import tpu as pltpu

F32 = jnp.float32
BF16 = jnp.bfloat16

N_DEV = 8
D_MODEL = 1024
DEPTH = 2
CHUNK = 128
SSD_HEADS = 16
SSD_HEAD_DIM = 64
SSD_GROUPS = 2
HEADS_PER_GROUP = SSD_HEADS // SSD_GROUPS
GROUP_WIDTH = HEADS_PER_GROUP * SSD_HEAD_DIM
D_STATE = 128
D_SSD = 1024
CONV_DIM = 1536
SSD_CONV = 4
GM_HEADS = 8
GM_HEAD_DIM = 128
D_GM = 1024
D_FF = 2816
FF_CONV = 3
N_IN = 4624
N_MOD = 6
EPS = 1e-6

N_INP = 5120
COL_U, COL_V, COL_Z, COL_XBC, COL_DT = 0, 1024, 2048, 3072, 4608

ADAM_LR = 0.001
ADAM_B1 = 0.9
ADAM_B2 = 0.999
ADAM_EPS = 1e-08
ADAM_WD = 0.01
ADAM_STEP = 10

VMEM_LIMIT = 56 * 1024 * 1024
MESH = pl.DeviceIdType.MESH
ANY = pl.BlockSpec(memory_space=pl.ANY)


def _cp(*sem):
    return pltpu.CompilerParams(dimension_semantics=sem, vmem_limit_bytes=VMEM_LIMIT)


def _tile(n, pref):
    if n <= pref or n % 128:
        return n
    best = 128
    for t in range(128, pref + 1, 128):
        if n % t == 0:
            best = t
    return best


def _per_layer(n):
    return pl.BlockSpec((DEPTH, n), lambda *_: (0, 0))


def _row(ref, layer, cols=slice(None)):
    return ref[layer:layer + 1, cols]


def _silu(x):
    return x * jax.nn.sigmoid(x)


def _gelu(x):
    return 0.5 * x * (1.0 + lax.erf(x * (1.0 / math.sqrt(2.0))))


def _softplus(x):
    return jnp.maximum(x, 0.0) + jnp.log1p(jnp.exp(-jnp.abs(x)))


def _b(x):
    return x.astype(BF16)


_NN = (((1,), (0,)), ((), ()))
_NT = (((1,), (1,)), ((), ()))
_TN = (((0,), (0,)), ((), ()))


def _dg(a, b, dn):
    return lax.dot_general(_b(a), _b(b), dn, preferred_element_type=F32)


@jax.custom_vjp
def _bdot(a, b):
    return _dg(a, b, _NN)


def _bdot_fwd(a, b):
    return _dg(a, b, _NN), (a, b)


def _bdot_bwd(res, ct):
    a, b = res
    return _dg(ct, b, _NT), _dg(a, ct, _TN)


_bdot.defvjp(_bdot_fwd, _bdot_bwd)


@jax.custom_vjp
def _bdot_nt(a, b):
    return _dg(a, b, _NT)


def _bdot_nt_fwd(a, b):
    return _dg(a, b, _NT), (a, b)


def _bdot_nt_bwd(res, ct):
    a, b = res
    return _dg(ct, b, _NN), _dg(ct, a, _TN)


_bdot_nt.defvjp(_bdot_nt_fwd, _bdot_nt_bwd)


@jax.custom_vjp
def _bdot_tn(a, b):
    return _dg(a, b, _TN)


def _bdot_tn_fwd(a, b):
    return _dg(a, b, _TN), (a, b)


def _bdot_tn_bwd(res, ct):
    a, b = res
    return _dg(b, ct, _NT), _dg(a, ct, _NN)


_bdot_tn.defvjp(_bdot_tn_fwd, _bdot_tn_bwd)


def _tri(n, lower):
    r = lax.broadcasted_iota(jnp.int32, (n, n), 0)
    c = lax.broadcasted_iota(jnp.int32, (n, n), 1)
    return ((r >= c) if lower else (r <= c)).astype(F32)


def _eye(n):
    r = lax.broadcasted_iota(jnp.int32, (n, n), 0)
    c = lax.broadcasted_iota(jnp.int32, (n, n), 1)
    return (r == c).astype(F32)


def _hdot(a, b, dn):
    return lax.dot_general(a, b, dn, precision=lax.Precision.HIGHEST, preferred_element_type=F32)


@jax.custom_vjp
def _cumsum_rows(x):
    return _hdot(_tri(x.shape[0], True), x, _NN)


def _cumsum_rows_fwd(x):
    return _cumsum_rows(x), None


def _cumsum_rows_bwd(_, ct):
    return (_hdot(_tri(ct.shape[0], False), ct, _NN),)


_cumsum_rows.defvjp(_cumsum_rows_fwd, _cumsum_rows_bwd)


@jax.custom_vjp
def _transpose(x):
    return _hdot(_eye(x.shape[1]), x, _NT)


def _transpose_fwd(x):
    return _transpose(x), None


def _transpose_bwd(_, ct):
    return (_hdot(_eye(ct.shape[1]), ct, _NT),)


_transpose.defvjp(_transpose_fwd, _transpose_bwd)


MXU_WIDTH = 256
MATMUL_TILE_CAP = 2816
MATMUL_VMEM = 44 * 1024 * 1024


def _mxu_tiles(n):
    if n <= MATMUL_TILE_CAP or n % 128:
        return [n]
    for unit in (MXU_WIDTH, 128):
        opts = [t for t in range(unit, MATMUL_TILE_CAP + 1, unit) if n % t == 0]
        if opts:
            return opts
    return [n]


def _matmul(a, b, *, ta=False, tb=False, name, dep=None, out_dtype=F32):
    pieces = list(a) if isinstance(a, (list, tuple)) else [a]
    npc = len(pieces)
    rows, width = pieces[0].shape
    assert all(p.shape == (rows, width) for p in pieces)
    if ta:
        k_dim, m_dim = rows, width * npc
    else:
        m_dim, k_dim = rows, width * npc
    if tb:
        n_dim, kb = b.shape
    else:
        kb, n_dim = b.shape
    assert kb == k_dim, (pieces[0].shape, npc, b.shape, ta, tb)
    m_unit = width if npc > 1 and ta else m_dim
    k_unit = width if npc > 1 and not ta else k_dim
    tm = _tile(m_unit, 1536)
    tn_opts, tk_opts = _mxu_tiles(n_dim), _mxu_tiles(k_unit)
    tn, tk = tn_opts.pop(), tk_opts.pop()
    while 4 * (tm * tk + tk * tn) + 8 * tm * tn > MATMUL_VMEM:
        if tn >= tk and tn_opts:
            tn = tn_opts.pop()
        else:
            tk = tk_opts.pop()
    ni, nj, nk = m_dim // tm, n_dim // tn, k_dim // tk
    per = width // (tm if ta else tk)
    dn = (((0 if ta else 1,), (1 if tb else 0,)), ((), ()))

    a_bytes, b_bytes = m_dim * k_dim, k_dim * n_dim
    m_outer = nk > 1 or a_bytes + b_bytes * ni <= b_bytes + a_bytes * nj
    if m_outer:
        ij = lambda o, n, k: (o, n)
        grid = (ni, nj, nk)
    else:
        ij = lambda o, n, k: (n, o)
        grid = (nj, ni, nk)

    use_acc = nk > 1 and out_dtype != F32

    def body(*refs):
        a_refs, b_ref = refs[:npc], refs[npc]
        o_ref = refs[-2] if use_acc else refs[-1]
        acc_ref = refs[-1]
        k = pl.program_id(2)
        i = pl.program_id(0 if m_outer else 1)
        along = i if ta else k

        def step(a_ref):
            p = lax.dot_general(a_ref[...], b_ref[...], dn, preferred_element_type=F32)
            if nk == 1:
                o_ref[...] = p.astype(out_dtype)
            else:
                @pl.when(k == 0)
                def _():
                    acc_ref[...] = p

                @pl.when((k > 0) & (k < nk - 1 if use_acc else True))
                def _():
                    acc_ref[...] += p

                if use_acc:
                    @pl.when(k == nk - 1)
                    def _():
                        o_ref[...] = (acc_ref[...] + p).astype(out_dtype)

        if npc == 1:
            step(a_refs[0])
        else:
            for pc in range(npc):
                pl.when((along >= pc * per) & (along < (pc + 1) * per))(functools.partial(step, a_refs[pc]))

    def a_map(pc, o, n, k):
        i, _ = ij(o, n, k)
        along = i if ta else k
        if npc > 1:
            along = jnp.clip(along - pc * per, 0, per - 1)
        return (k, along) if ta else (i, along)

    def b_map(o, n, k):
        _, j = ij(o, n, k)
        return (j, k) if tb else (k, j)

    extra = [] if dep is None else [dep]
    return pl.pallas_call(
        body, name=name,
        grid=grid,
        in_specs=[pl.BlockSpec((tk, tm) if ta else (tm, tk), functools.partial(a_map, pc)) for pc in range(npc)]
        + [pl.BlockSpec((tn, tk) if tb else (tk, tn), b_map)] + [ANY] * len(extra),
        out_specs=pl.BlockSpec((tm, tn), lambda o, n, k: ij(o, n, k)),
        out_shape=jax.ShapeDtypeStruct((m_dim, n_dim), out_dtype),
        scratch_shapes=[pltpu.VMEM((tm, tn), F32)] if use_acc else [],
        compiler_params=_cp("parallel", "parallel", "arbitrary"),
    )(*pieces, b, *extra)


def _ada_fwd(c_all, ada_w, ada_b_shard):
    depth, d, n = ada_w.shape
    nb = c_all.shape[0]

    def body(c_ref, w_ref, b_ref, o_ref, ca_ref):
        ca = _silu(c_ref[...])
        ca_ref[...] = _b(ca)
        o_ref[0] = _dg(ca, w_ref[0], _NN) + b_ref[0]

    return pl.pallas_call(
        body, name="ada_fwd",
        grid=(depth,),
        in_specs=[pl.BlockSpec((nb, d), lambda l: (0, 0)),
                  pl.BlockSpec((1, d, n), lambda l: (l, 0, 0)),
                  pl.BlockSpec((1, 1, n), lambda l: (l, 0, 0))],
        out_specs=[pl.BlockSpec((1, nb, n), lambda l: (l, 0, 0)),
                   pl.BlockSpec((nb, d), lambda l: (0, 0))],
        out_shape=[jax.ShapeDtypeStruct((depth, nb, n), F32), jax.ShapeDtypeStruct((nb, d), BF16)],
        compiler_params=_cp("arbitrary"),
    )(c_all, ada_w, ada_b_shard)


def _fold(acc):
    return jnp.sum(acc, axis=0, keepdims=True)


def _rinv(x):
    return lax.rsqrt(jnp.sum(x * x, axis=-1, keepdims=True) * (1.0 / D_MODEL) + EPS)


def _rms_bwd(a, xhat, rinv):
    return rinv * (a - xhat * (jnp.sum(a * xhat, axis=-1, keepdims=True) * (1.0 / D_MODEL)))


def _row_tile(seq):
    return min(seq, 256)


def _normmod_fwd(x, g, sc, sh, *, nseq, name, layer):
    t, d = x.shape
    seq = t // nseq
    tr = _row_tile(seq)
    nt = seq // tr
    row = pl.BlockSpec((tr, d), lambda s, i: (s * nt + i, 0))
    per_seq = pl.BlockSpec((1, 1, d), lambda s, i: (s, 0, 0))

    def body(x_ref, g_ref, sc_ref, sh_ref, h_ref):
        x_v = x_ref[...]
        h_ref[...] = _b(x_v * _rinv(x_v) * (_row(g_ref, layer) * (1.0 + sc_ref[0])) + sh_ref[0])

    return pl.pallas_call(
        body, name=name, grid=(nseq, nt),
        in_specs=[row, _per_layer(d), per_seq, per_seq],
        out_specs=row,
        out_shape=jax.ShapeDtypeStruct((t, d), BF16),
        compiler_params=_cp("parallel", "parallel"),
    )(x, g, sc, sh)


NORM_TM = 512


def _matmul_normbwd(a, b, dxo, x, delta, gate, g, sc, *, nseq, name, layer, dep=None):
    pieces = list(a) if isinstance(a, (list, tuple)) else [a]
    npc = len(pieces)
    t, width = pieces[0].shape
    k_dim, d = width * npc, b.shape[1]
    assert b.shape[0] == k_dim and all(p.shape == (t, width) for p in pieces)
    seq = t // nseq
    tm = min(NORM_TM, seq)
    per_seq_tiles = seq // tm
    tk = _mxu_tiles(width if npc > 1 else k_dim).pop()
    nk, per = k_dim // tk, width // tk
    has_delta = delta is not None
    extra = [] if dep is None else [dep]

    def body(*refs):
        a_refs, b_ref = refs[:npc], refs[npc]
        dxo_ref, x_ref = refs[npc + 1], refs[npc + 2]
        pos = npc + 3
        if has_delta:
            delta_ref, gate_ref = refs[pos], refs[pos + 1]
            pos += 2
        g_ref, sc_ref = refs[pos], refs[pos + 1]
        pos += 2 + len(extra)
        if has_delta:
            dx_ref, dd_ref, dgate_ref, dg_ref, dsc_ref, dsh_ref = refs[pos:pos + 6]
        else:
            dx_ref, dg_ref, dsc_ref, dsh_ref = refs[pos:pos + 4]
        acc_ref = refs[-1]
        i, k = pl.program_id(0), pl.program_id(1)

        def norm_bwd(dh_v):
            g_v, one_sc = _row(g_ref, layer), 1.0 + sc_ref[0]
            x_v = x_ref[...]
            rinv = _rinv(x_v)
            xhat = x_v * rinv
            dx = dxo_ref[...] + _rms_bwd(dh_v * (g_v * one_sc), xhat, rinv)
            dx_ref[...] = dx

            @pl.when(i == 0)
            def _():
                dg_ref[...] = jnp.zeros_like(dg_ref)

            @pl.when(i % per_seq_tiles == 0)
            def _():
                dsc_ref[...] = jnp.zeros_like(dsc_ref)
                dsh_ref[...] = jnp.zeros_like(dsh_ref)
                if has_delta:
                    dgate_ref[...] = jnp.zeros_like(dgate_ref)

            t_sum = _fold(dh_v * xhat)
            dg_ref[...] += t_sum * one_sc
            dsc_ref[0] += t_sum * g_v
            dsh_ref[0] += _fold(dh_v)
            if has_delta:
                dd_ref[...] = _b(dx * gate_ref[0])
                dgate_ref[0] += _fold(dx * delta_ref[...])

        def step(a_ref):
            p = lax.dot_general(a_ref[...], b_ref[...], _NN, preferred_element_type=F32)
            if nk == 1:
                norm_bwd(p)
            else:
                @pl.when(k == 0)
                def _():
                    acc_ref[...] = p

                @pl.when((k > 0) & (k < nk - 1))
                def _():
                    acc_ref[...] += p

                @pl.when(k == nk - 1)
                def _():
                    norm_bwd(acc_ref[...] + p)

        if npc == 1:
            step(a_refs[0])
        else:
            for pc in range(npc):
                pl.when((k >= pc * per) & (k < (pc + 1) * per))(functools.partial(step, a_refs[pc]))

    def a_map(pc, i, k):
        return (i, jnp.clip(k - pc * per, 0, per - 1) if npc > 1 else k)

    row = pl.BlockSpec((tm, d), lambda i, k: (i, 0))
    per_seq = pl.BlockSpec((1, 1, d), lambda i, k: (i // per_seq_tiles, 0, 0))
    vec = pl.BlockSpec((1, d), lambda i, k: (0, 0))
    shp = lambda *s, dt=F32: jax.ShapeDtypeStruct(s, dt)
    in_specs = [pl.BlockSpec((tm, tk), functools.partial(a_map, pc)) for pc in range(npc)]
    in_specs += [pl.BlockSpec((tk, d), lambda i, k: (k, 0)), row, row]
    operands = [*pieces, b, dxo, x]
    if has_delta:
        in_specs += [row, per_seq]
        operands += [delta, gate]
    in_specs += [_per_layer(d), per_seq] + [ANY] * len(extra)
    operands += [g, sc, *extra]
    if has_delta:
        out_specs = [row, row, per_seq, vec, per_seq, per_seq]
        out_shape = [shp(t, d), shp(t, d, dt=BF16), shp(nseq, 1, d), shp(1, d), shp(nseq, 1, d), shp(nseq, 1, d)]
    else:
        out_specs = [row, vec, per_seq, per_seq]
        out_shape = [shp(t, d), shp(1, d), shp(nseq, 1, d), shp(nseq, 1, d)]
    outs = pl.pallas_call(
        body, name=name, grid=(t // tm, nk),
        in_specs=in_specs, out_specs=out_specs, out_shape=out_shape,
        scratch_shapes=[pltpu.VMEM((tm, d), F32)],
        compiler_params=_cp("arbitrary", "arbitrary"),
    )(*operands)
    if has_delta:
        return tuple(outs)
    dx, dg, dsc, dsh = outs
    return dx, None, None, dg, dsc, dsh


def _matmul_normfwd(a, b, xin, gate, g, sc, sh, *, nseq, name, layer):
    t, k_dim = a.shape
    d = b.shape[1]
    assert b.shape[0] == k_dim and k_dim <= MATMUL_TILE_CAP
    seq = t // nseq
    tm = min(NORM_TM, seq)
    per_seq_tiles = seq // tm

    def body(a_ref, b_ref, xin_ref, gate_ref, g_ref, sc_ref, sh_ref, dl_ref, x_ref, h_ref):
        dl = lax.dot_general(a_ref[...], b_ref[...], _NN, preferred_element_type=F32)
        dl_ref[...] = dl
        x = xin_ref[...] + gate_ref[0] * dl
        x_ref[...] = x
        h_ref[...] = _b(x * _rinv(x) * (_row(g_ref, layer) * (1.0 + sc_ref[0])) + sh_ref[0])

    row = pl.BlockSpec((tm, d), lambda i: (i, 0))
    per_seq = pl.BlockSpec((1, 1, d), lambda i: (i // per_seq_tiles, 0, 0))
    return pl.pallas_call(
        body, name=name, grid=(t // tm,),
        in_specs=[pl.BlockSpec((tm, k_dim), lambda i: (i, 0)), pl.BlockSpec((k_dim, d), lambda i: (0, 0)),
                  row, per_seq, _per_layer(d), per_seq, per_seq],
        out_specs=[row, row, row],
        out_shape=[jax.ShapeDtypeStruct((t, d), F32), jax.ShapeDtypeStruct((t, d), F32), jax.ShapeDtypeStruct((t, d), BF16)],
        compiler_params=_cp("parallel"),
    )(a, b, xin, gate, g, sc, sh)


def _matmul_loss(a, b, xin, gate, fg, target, *, nseq, name):
    t, k_dim = a.shape
    d = b.shape[1]
    assert b.shape[0] == k_dim and k_dim <= MATMUL_TILE_CAP
    seq = t // nseq
    tm = min(NORM_TM, seq)
    per_seq_tiles = seq // tm

    def body(a_ref, b_ref, xin_ref, gate_ref, fg_ref, tgt_ref, dl_ref, loss_ref, dx_ref, dd_ref, dgate_ref, dfg_ref):
        i = pl.program_id(0)
        fg_v, gate_v = fg_ref[...], gate_ref[0]
        dl = lax.dot_general(a_ref[...], b_ref[...], _NN, preferred_element_type=F32)
        dl_ref[...] = dl
        x = xin_ref[...] + gate_v * dl
        rinv = _rinv(x)
        xhat = x * rinv
        err = xhat * fg_v - tgt_ref[...]
        dx = _rms_bwd(err * fg_v * (1.0 / d), xhat, rinv)
        dx_ref[...] = dx
        dd_ref[...] = _b(dx * gate_v)

        @pl.when(i == 0)
        def _():
            loss_ref[...] = jnp.zeros_like(loss_ref)
            dfg_ref[...] = jnp.zeros_like(dfg_ref)

        @pl.when(i % per_seq_tiles == 0)
        def _():
            dgate_ref[...] = jnp.zeros_like(dgate_ref)

        loss_ref[...] += jnp.sum(err * err) * (0.5 / d)
        dfg_ref[...] += _fold(err * xhat) * (1.0 / d)
        dgate_ref[0] += _fold(dx * dl)

    row = pl.BlockSpec((tm, d), lambda i: (i, 0))
    per_seq = pl.BlockSpec((1, 1, d), lambda i: (i // per_seq_tiles, 0, 0))
    vec = pl.BlockSpec((1, d), lambda i: (0, 0))
    return pl.pallas_call(
        body, name=name, grid=(t // tm,),
        in_specs=[pl.BlockSpec((tm, k_dim), lambda i: (i, 0)), pl.BlockSpec((k_dim, d), lambda i: (0, 0)),
                  row, per_seq, vec, row],
        out_specs=[row, pl.BlockSpec((1, 128), lambda i: (0, 0)), row, row, per_seq, vec],
        out_shape=[jax.ShapeDtypeStruct((t, d), F32), jax.ShapeDtypeStruct((1, 128), F32), jax.ShapeDtypeStruct((t, d), F32),
                   jax.ShapeDtypeStruct((t, d), BF16), jax.ShapeDtypeStruct((nseq, 1, d), F32),
                   jax.ShapeDtypeStruct((1, d), F32)],
        compiler_params=_cp("arbitrary"),
    )(a, b, xin, gate, fg, target)


CONV_TC = 256
CONV_LANES = 128
CONV_ROWS = 64
CONV_HALO = 8


def _conv_slabs(seq, fn):
    def step(i, carry):
        r0 = pl.multiple_of(i * CONV_ROWS, CONV_ROWS)
        for h in range(CONV_TC // CONV_LANES):
            fn(r0, slice(h * CONV_LANES, (h + 1) * CONV_LANES))
        return carry

    lax.fori_loop(0, seq // CONV_ROWS, step, 0)


def _slab(ref, r0, cols, seq):
    after = ref[pl.ds(pl.multiple_of(jnp.minimum(r0 + CONV_ROWS, seq - CONV_HALO), CONV_HALO), CONV_HALO), cols]
    return jnp.concatenate([ref[pl.ds(r0, CONV_ROWS), cols], jnp.where(r0 + CONV_ROWS < seq, after, 0.0)], axis=0)


def _conv_block(x, w_ref, b):
    kw = w_ref.shape[0]
    rows = lax.broadcasted_iota(jnp.int32, x.shape, 0)
    y = b + w_ref[kw - 1:kw, :] * x
    for j in range(1, kw):
        y = y + w_ref[kw - 1 - j:kw - j, :] * jnp.where(rows >= j, pltpu.roll(x, j, 0), 0.0)
    return y


def _conv_block_bwd(dy, x, w_ref, dw_ref, db_ref):
    kw = w_ref.shape[0]
    n = x.shape[0]
    rows = lax.broadcasted_iota(jnp.int32, x.shape, 0)
    dx = w_ref[kw - 1:kw, :] * dy
    dw_ref[kw - 1:kw, :] += jnp.sum(dy * x, axis=0, keepdims=True)
    for j in range(1, kw):
        dy_j = jnp.where(rows < n - j, pltpu.roll(dy, n - j, 0), 0.0)
        dx = dx + w_ref[kw - 1 - j:kw - j, :] * dy_j
        dw_ref[kw - 1 - j:kw - j, :] += jnp.sum(dy_j * x, axis=0, keepdims=True)
    db_ref[...] += jnp.sum(dy, axis=0, keepdims=True)
    return dx


def _conv_bwd(dy_ext, x, w_ref, dw_ref, db_ref, cols):
    kw = w_ref.shape[0]
    n = dy_ext.shape[0]
    dy = dy_ext[:CONV_ROWS]
    dx = w_ref[kw - 1:kw, cols] * dy
    dw_ref[kw - 1:kw, cols] += jnp.sum(dy * x, axis=0, keepdims=True)
    for j in range(1, kw):
        dy_j = pltpu.roll(dy_ext, n - j, 0)[:CONV_ROWS]
        dx = dx + w_ref[kw - 1 - j:kw - j, cols] * dy_j
        dw_ref[kw - 1 - j:kw - j, cols] += jnp.sum(dy_j * x, axis=0, keepdims=True)
    db_ref[:, cols] += jnp.sum(dy, axis=0, keepdims=True)
    return dx


def _dsilu(pre):
    sg = jax.nn.sigmoid(pre)
    return pre * sg, sg * (1.0 + pre * (1.0 - sg))


def _conv_specs(kw, layer):
    return [pl.BlockSpec((None, kw, CONV_TC), lambda j, s: (layer, 0, j)),
            pl.BlockSpec((DEPTH, CONV_TC), lambda j, s: (0, j))]


def _ssd_conv_fwd(proj, w, b, *, nseq, layer):
    t = proj.shape[0]
    seq = t // nseq
    nb = CONV_DIM // CONV_TC
    off = COL_XBC // CONV_TC

    def body(x_ref, w_ref, b_ref, o_ref, pre_ref):
        pre = _conv_block(x_ref[...], w_ref, _row(b_ref, layer))
        pre_ref[...] = pre
        o_ref[...] = _silu(pre)

    col = pl.BlockSpec((seq, CONV_TC), lambda j, s: (s, j))
    return pl.pallas_call(
        body, name="ssd_conv_fwd", grid=(nb, nseq),
        in_specs=[pl.BlockSpec((seq, CONV_TC), lambda j, s: (s, off + j)), *_conv_specs(SSD_CONV, layer)],
        out_specs=[col, col],
        out_shape=[jax.ShapeDtypeStruct((t, CONV_DIM), F32)] * 2,
        compiler_params=_cp("parallel", "parallel"),
    )(proj, w, b)


def _ssd_conv_bwd(dact, pre, proj, w, dproj, *, nseq, layer):
    t = proj.shape[0]
    seq = t // nseq
    nb = CONV_DIM // CONV_TC
    off = COL_XBC // CONV_TC

    def body(da_ref, pre_ref, x_ref, w_ref, dproj_ref, dx_ref, dw_ref, db_ref):
        del dproj_ref

        @pl.when(pl.program_id(1) == 0)
        def _():
            dw_ref[...] = jnp.zeros_like(dw_ref)
            db_ref[...] = jnp.zeros_like(db_ref)

        def slab(r0, cols):
            _, dsilu = _dsilu(_slab(pre_ref, r0, cols, seq))
            dpre_ext = _slab(da_ref, r0, cols, seq) * dsilu
            x = x_ref[pl.ds(r0, CONV_ROWS), cols]
            dx_ref[pl.ds(r0, CONV_ROWS), cols] = _b(_conv_bwd(dpre_ext, x, w_ref, dw_ref, db_ref, cols))

        _conv_slabs(seq, slab)

    return pl.pallas_call(
        body, name="ssd_conv_bwd", grid=(nb, nseq),
        in_specs=[pl.BlockSpec((seq, CONV_TC), lambda j, s: (s, j)),
                  pl.BlockSpec((seq, CONV_TC), lambda j, s: (s, j)),
                  pl.BlockSpec((seq, CONV_TC), lambda j, s: (s, off + j)),
                  _conv_specs(SSD_CONV, layer)[0],
                  ANY],
        out_specs=[pl.BlockSpec((seq, CONV_TC), lambda j, s: (s, off + j)),
                   pl.BlockSpec((SSD_CONV, CONV_TC), lambda j, s: (0, j)),
                   pl.BlockSpec((1, CONV_TC), lambda j, s: (0, j))],
        out_shape=[jax.ShapeDtypeStruct(dproj.shape, dproj.dtype), jax.ShapeDtypeStruct((SSD_CONV, CONV_DIM), F32),
                   jax.ShapeDtypeStruct((1, CONV_DIM), F32)],
        input_output_aliases={4: 0},
        compiler_params=_cp("parallel", "arbitrary"),
    )(dact, pre, proj, w, dproj)


def _ffn_act_fwd(up, w, b, *, nseq, layer):
    t = up.shape[0]
    seq = t // nseq
    nb = D_FF // CONV_TC

    def body(g_ref, v_ref, w_ref, b_ref, o_ref):
        pre = _conv_block(g_ref[...].astype(F32), w_ref, _row(b_ref, layer))
        o_ref[...] = _b(_silu(pre) * v_ref[...].astype(F32))

    col = pl.BlockSpec((seq, CONV_TC), lambda j, s: (s, j))
    return pl.pallas_call(
        body, name="ffn_act_fwd", grid=(nb, nseq),
        in_specs=[col,
                  pl.BlockSpec((seq, CONV_TC), lambda j, s: (s, nb + j)),
                  *_conv_specs(FF_CONV, layer)],
        out_specs=col,
        out_shape=jax.ShapeDtypeStruct((t, D_FF), BF16),
        compiler_params=_cp("parallel", "parallel"),
    )(up, up, w, b)


def _ffn_act_bwd(dact, up, w, b, *, nseq, layer):
    t = up.shape[0]
    seq = t // nseq
    nb = D_FF // CONV_TC

    def body(da_ref, g_ref, v_ref, w_ref, b_ref, dg_ref, dv_ref, dw_ref, db_ref):
        @pl.when(pl.program_id(1) == 0)
        def _():
            dw_ref[...] = jnp.zeros_like(dw_ref)
            db_ref[...] = jnp.zeros_like(db_ref)

        gate = g_ref[...].astype(F32)
        silu, dsilu = _dsilu(_conv_block(gate, w_ref, _row(b_ref, layer)))
        da = da_ref[...].astype(F32)
        dv_ref[...] = _b(da * silu)
        dg_ref[...] = _b(_conv_block_bwd(da * v_ref[...].astype(F32) * dsilu, gate, w_ref, dw_ref, db_ref))

    col = pl.BlockSpec((seq, CONV_TC), lambda j, s: (s, j))
    return pl.pallas_call(
        body, name="ffn_act_bwd", grid=(nb, nseq),
        in_specs=[col, col,
                  pl.BlockSpec((seq, CONV_TC), lambda j, s: (s, nb + j)),
                  *_conv_specs(FF_CONV, layer)],
        out_specs=[col, col,
                   pl.BlockSpec((FF_CONV, CONV_TC), lambda j, s: (0, j)),
                   pl.BlockSpec((1, CONV_TC), lambda j, s: (0, j))],
        out_shape=[jax.ShapeDtypeStruct((t, D_FF), BF16), jax.ShapeDtypeStruct((t, D_FF), BF16),
                   jax.ShapeDtypeStruct((FF_CONV, D_FF), F32), jax.ShapeDtypeStruct((1, D_FF), F32)],
        compiler_params=_cp("parallel", "arbitrary"),
    )(dact, up, up, w, b)


SSD_PAIRS = SSD_HEADS // 2
PAIR_W = 2 * SSD_HEAD_DIM
PAIRS_PER_GROUP = SSD_PAIRS // SSD_GROUPS


def _ssd_chunk(xs, bg, cg, dtr, z, hp, dtb, alog, dskip, ng):
    n = dtr.shape[0]
    dt = _softplus(dtr + dtb)
    cs = _cumsum_rows(dt * (-jnp.exp(alog)))
    cs_t = _transpose(cs)
    lane = lax.broadcasted_iota(jnp.int32, (1, SSD_HEADS), 1)
    sub = lax.broadcasted_iota(jnp.int32, (SSD_HEADS, 1), 0)
    row = lax.broadcasted_iota(jnp.int32, (n, 1), 0)
    causal = lax.broadcasted_iota(jnp.int32, (n, n), 0) >= lax.broadcasted_iota(jnp.int32, (n, n), 1)
    future = jnp.where(causal, 0.0, -1e30)
    first = lax.broadcasted_iota(jnp.int32, (1, PAIR_W), 1) < SSD_HEAD_DIM
    first_rows = lax.broadcasted_iota(jnp.int32, (PAIR_W, 1), 0) < SSD_HEAD_DIM
    first_f = first.astype(F32)
    cb = [_bdot_nt(cg[g], bg[g]) for g in range(SSD_GROUPS)]
    ys, hn = [], []
    for p in range(SSD_PAIRS):
        g = p // PAIRS_PER_GROUP
        col, decay, last = [], [], []
        for h in (2 * p, 2 * p + 1):
            oh = (lane == h).astype(F32)
            cs_h = jnp.sum(cs * oh, axis=1, keepdims=True)
            cs_row = jnp.sum(cs_t * (sub == h).astype(F32), axis=0, keepdims=True)
            col.append((jnp.sum(dt * oh, axis=1, keepdims=True), cs_h, jnp.sum(dskip * oh, axis=1, keepdims=True)))
            last.append(jnp.sum(jnp.where(row == n - 1, cs_h, 0.0), axis=0, keepdims=True))
            decay.append(jnp.exp(cs_h - cs_row + future))
        pair = lambda a, b: jnp.where(first, a, b)
        dt_p = pair(col[0][0], col[1][0])
        cs_p = pair(col[0][1], col[1][1])
        last_p = pair(last[0], last[1])
        xc = xs[p] * dt_p
        y = _bdot(cb[g] * decay[0], xc * first_f) + _bdot(cb[g] * decay[1], xc * (1.0 - first_f))
        y = y + _bdot_nt(cg[g], hp[p]) * jnp.exp(cs_p)
        y = y + pair(col[0][2], col[1][2]) * xs[p]
        keep = jnp.where(first_rows, jnp.exp(last[0]), jnp.exp(last[1]))
        hn.append(keep * hp[p] + _bdot_tn(xc * jnp.exp(last_p - cs_p), bg[g]))
        ys.append(y * _silu(z[p]))
    outs = []
    for g in range(SSD_GROUPS):
        ps = range(g * PAIRS_PER_GROUP, (g + 1) * PAIRS_PER_GROUP)
        ms = sum(jnp.sum(ys[p] * ys[p], axis=1, keepdims=True) for p in ps) * (1.0 / GROUP_WIDTH)
        r = lax.rsqrt(ms + EPS)
        outs += [ys[p] * r * ng[p] for p in ps]
    return outs, hn


def _hslices(ref, width, count, base=0, rows=slice(None)):
    return [ref[rows, base + k * width: base + (k + 1) * width] for k in range(count)]


def _ssd_load(xbc_ref, z_ref, dt_ref, ng_ref, layer):
    xs = _hslices(xbc_ref, PAIR_W, SSD_PAIRS)
    bg = _hslices(xbc_ref, D_STATE, SSD_GROUPS, D_SSD)
    cg = _hslices(xbc_ref, D_STATE, SSD_GROUPS, D_SSD + SSD_GROUPS * D_STATE)
    z = _hslices(z_ref, PAIR_W, SSD_PAIRS)
    ng = _hslices(ng_ref, PAIR_W, SSD_PAIRS, rows=slice(layer, layer + 1))
    return xs, bg, cg, dt_ref[:, 0:SSD_HEADS], z, ng


def _ssd_specs(nch):
    rowi = lambda s, c: s * nch + c
    return [pl.BlockSpec((CHUNK, CONV_DIM), lambda s, c: (rowi(s, c), 0)),
            pl.BlockSpec((CHUNK, D_SSD), lambda s, c: (rowi(s, c), COL_Z // D_SSD)),
            pl.BlockSpec((CHUNK, 128), lambda s, c: (rowi(s, c), COL_DT // 128)),
            _per_layer(SSD_HEADS), _per_layer(SSD_HEADS), _per_layer(SSD_HEADS), _per_layer(D_SSD)]


def _ssd_fwd(xbc, proj, dtb, alog, dskip, ng, *, nseq, layer):
    t = proj.shape[0]
    nch = t // nseq // CHUNK
    hd = PAIR_W

    def body(xbc_ref, z_ref, dt_ref, dtb_ref, alog_ref, dsk_ref, ng_ref, y_ref, hp_ref, h_ref):
        @pl.when(pl.program_id(1) == 0)
        def _():
            h_ref[...] = jnp.zeros_like(h_ref)

        xs, bg, cg, dtr, z, ngs = _ssd_load(xbc_ref, z_ref, dt_ref, ng_ref, layer)
        hp_ref[0] = h_ref[...]
        hp = [h_ref[h * hd:(h + 1) * hd, :] for h in range(SSD_PAIRS)]
        outs, hn = _ssd_chunk(xs, bg, cg, dtr, z, hp, _row(dtb_ref, layer), _row(alog_ref, layer), _row(dsk_ref, layer), ngs)
        for h in range(SSD_PAIRS):
            y_ref[:, h * hd:(h + 1) * hd] = _b(outs[h])
            h_ref[h * hd:(h + 1) * hd, :] = hn[h]

    return pl.pallas_call(
        body, name="ssd_fwd", grid=(nseq, nch),
        in_specs=_ssd_specs(nch),
        out_specs=[pl.BlockSpec((CHUNK, D_SSD), lambda s, c: (s * nch + c, 0)),
                   pl.BlockSpec((1, D_SSD, D_STATE), lambda s, c: (s * nch + c, 0, 0))],
        out_shape=[jax.ShapeDtypeStruct((t, D_SSD + D_GM), BF16),
                   jax.ShapeDtypeStruct((t // CHUNK, D_SSD, D_STATE), F32)],
        scratch_shapes=[pltpu.VMEM((D_SSD, D_STATE), F32)],
        compiler_params=_cp("arbitrary", "arbitrary"),
    )(xbc, proj, proj, dtb, alog, dskip, ng)


def _ssd_bwd(dy, xbc, proj, hprev, dtb, alog, dskip, ng, *, nseq, layer):
    t = proj.shape[0]
    nch = t // nseq // CHUNK
    hd = PAIR_W
    rev = lambda s, c: s * nch + (nch - 1 - c)

    def body(dy_ref, xbc_ref, z_ref, dt_ref, hp_ref, dtb_ref, alog_ref, dsk_ref, ng_ref,
             dxbc_ref, dproj_ref, ddtb_ref, dalog_ref, ddsk_ref, dng_ref, dh_ref):
        first = (pl.program_id(0) == 0) & (pl.program_id(1) == 0)

        @pl.when(pl.program_id(1) == 0)
        def _():
            dh_ref[...] = jnp.zeros_like(dh_ref)

        @pl.when(first)
        def _():
            ddtb_ref[...] = jnp.zeros_like(ddtb_ref)
            dalog_ref[...] = jnp.zeros_like(dalog_ref)
            ddsk_ref[...] = jnp.zeros_like(ddsk_ref)
            dng_ref[...] = jnp.zeros_like(dng_ref)

        xs, bg, cg, dtr, z, ngs = _ssd_load(xbc_ref, z_ref, dt_ref, ng_ref, layer)
        hp = [hp_ref[0, h * hd:(h + 1) * hd, :] for h in range(SSD_PAIRS)]
        _, vjp = jax.vjp(_ssd_chunk, xs, bg, cg, dtr, z, hp, _row(dtb_ref, layer), _row(alog_ref, layer), _row(dsk_ref, layer), ngs)
        douts = [dy_ref[:, h * hd:(h + 1) * hd] for h in range(SSD_PAIRS)]
        dhn = [dh_ref[h * hd:(h + 1) * hd, :] for h in range(SSD_PAIRS)]
        dxs, dbg, dcg, ddtr, dz, dhp, ddtb, dalog, ddsk, dngs = vjp((douts, dhn))
        dproj_ref[:, :COL_Z] = jnp.zeros((CHUNK, COL_Z), BF16)
        dproj_ref[:, COL_XBC:] = jnp.zeros((CHUNK, N_INP - COL_XBC), BF16)
        for h in range(SSD_PAIRS):
            dxbc_ref[:, h * hd:(h + 1) * hd] = dxs[h]
            dproj_ref[:, COL_Z + h * hd: COL_Z + (h + 1) * hd] = _b(dz[h])
            dh_ref[h * hd:(h + 1) * hd, :] = dhp[h]
            dng_ref[:, h * hd:(h + 1) * hd] += dngs[h]
        for g in range(SSD_GROUPS):
            dxbc_ref[:, D_SSD + g * D_STATE: D_SSD + (g + 1) * D_STATE] = dbg[g]
            dxbc_ref[:, D_SSD + (SSD_GROUPS + g) * D_STATE: D_SSD + (SSD_GROUPS + g + 1) * D_STATE] = dcg[g]
        dproj_ref[:, COL_DT:COL_DT + SSD_HEADS] = _b(ddtr)
        ddtb_ref[...] += ddtb
        dalog_ref[...] += dalog
        ddsk_ref[...] += ddsk

    small = pl.BlockSpec((1, SSD_HEADS), lambda s, c: (0, 0))
    return pl.pallas_call(
        body, name="ssd_bwd", grid=(nseq, nch),
        in_specs=[pl.BlockSpec((CHUNK, D_SSD), lambda s, c: (rev(s, c), 0)),
                  pl.BlockSpec((CHUNK, CONV_DIM), lambda s, c: (rev(s, c), 0)),
                  pl.BlockSpec((CHUNK, D_SSD), lambda s, c: (rev(s, c), COL_Z // D_SSD)),
                  pl.BlockSpec((CHUNK, 128), lambda s, c: (rev(s, c), COL_DT // 128)),
                  pl.BlockSpec((1, D_SSD, D_STATE), lambda s, c: (rev(s, c), 0, 0)),
                  _per_layer(SSD_HEADS), _per_layer(SSD_HEADS), _per_layer(SSD_HEADS), _per_layer(D_SSD)],
        out_specs=[pl.BlockSpec((CHUNK, CONV_DIM), lambda s, c: (rev(s, c), 0)),
                   pl.BlockSpec((CHUNK, N_INP), lambda s, c: (rev(s, c), 0)),
                   small, small, small,
                   pl.BlockSpec((1, D_SSD), lambda s, c: (0, 0))],
        out_shape=[jax.ShapeDtypeStruct((t, CONV_DIM), F32), jax.ShapeDtypeStruct((t, N_INP), BF16),
                   jax.ShapeDtypeStruct((1, SSD_HEADS), F32), jax.ShapeDtypeStruct((1, SSD_HEADS), F32),
                   jax.ShapeDtypeStruct((1, SSD_HEADS), F32), jax.ShapeDtypeStruct((1, D_SSD), F32)],
        scratch_shapes=[pltpu.VMEM((D_SSD, D_STATE), F32)],
        compiler_params=_cp("arbitrary", "arbitrary"),
    )(dy, xbc, proj, proj, hprev, dtb, alog, dskip, ng)


def _gmlp_chunk(gu, gv, ws, bs_cols, vg, og):
    n = gu[0].shape[0]
    mask = _tri(n, True)
    au = [_gelu(t) for t in gu]
    av = [_gelu(t) for t in gv]
    r = lax.rsqrt(sum(jnp.sum(t * t, axis=1, keepdims=True) for t in av) * (1.0 / D_GM) + EPS)
    p = []
    for h in range(GM_HEADS):
        sv = _bdot(ws[h] * mask, av[h] * r * vg[h]) + bs_cols[h]
        p.append(au[h] * sv)
    r2 = lax.rsqrt(sum(jnp.sum(t * t, axis=1, keepdims=True) for t in p) * (1.0 / D_GM) + EPS)
    return [p[h] * r2 * og[h] for h in range(GM_HEADS)]


def _gmlp_load(u_ref, v_ref, ws_ref, bst_ref, vg_ref, og_ref, layer):
    gu = _hslices(u_ref, GM_HEAD_DIM, GM_HEADS)
    gv = _hslices(v_ref, GM_HEAD_DIM, GM_HEADS)
    ws = [ws_ref[h] for h in range(GM_HEADS)]
    bs_cols = [bst_ref[:, h:h + 1] for h in range(GM_HEADS)]
    mine = slice(layer, layer + 1)
    return (gu, gv, ws, bs_cols, _hslices(vg_ref, GM_HEAD_DIM, GM_HEADS, rows=mine),
            _hslices(og_ref, GM_HEAD_DIM, GM_HEADS, rows=mine))


def _gmlp_specs(layer):
    return [pl.BlockSpec((CHUNK, D_GM), lambda i: (i, COL_U // D_GM)),
            pl.BlockSpec((CHUNK, D_GM), lambda i: (i, COL_V // D_GM)),
            pl.BlockSpec((None, GM_HEADS, CHUNK, CHUNK), lambda i: (layer, 0, 0, 0)),
            pl.BlockSpec((None, CHUNK, GM_HEADS), lambda i: (layer, 0, 0)),
            _per_layer(D_GM), _per_layer(D_GM)]


def _gmlp_fwd(proj, ycat, ws, bst, vg, og, *, layer):
    t = proj.shape[0]

    def body(u_ref, v_ref, ws_ref, bst_ref, vg_ref, og_ref, ycat_ref, o_ref):
        del ycat_ref
        outs = _gmlp_chunk(*_gmlp_load(u_ref, v_ref, ws_ref, bst_ref, vg_ref, og_ref, layer))
        for h in range(GM_HEADS):
            o_ref[:, h * GM_HEAD_DIM:(h + 1) * GM_HEAD_DIM] = _b(outs[h])

    return pl.pallas_call(
        body, name="gmlp_fwd", grid=(t // CHUNK,),
        in_specs=_gmlp_specs(layer) + [ANY],
        out_specs=pl.BlockSpec((CHUNK, D_GM), lambda i: (i, D_SSD // D_GM)),
        out_shape=jax.ShapeDtypeStruct(ycat.shape, ycat.dtype),
        input_output_aliases={6: 0},
        compiler_params=_cp("parallel"),
    )(proj, proj, ws, bst, vg, og, ycat)


def _gmlp_bwd(dy, proj, ws, bst, vg, og, dproj, *, layer):
    t = proj.shape[0]
    w = GM_HEAD_DIM

    def body(dy_ref, u_ref, v_ref, ws_ref, bst_ref, vg_ref, og_ref, dproj_ref,
             dgm_ref, dws_ref, dbst_ref, dvg_ref, dog_ref):
        del dproj_ref

        @pl.when(pl.program_id(0) == 0)
        def _():
            dws_ref[...] = jnp.zeros_like(dws_ref)
            dbst_ref[...] = jnp.zeros_like(dbst_ref)
            dvg_ref[...] = jnp.zeros_like(dvg_ref)
            dog_ref[...] = jnp.zeros_like(dog_ref)

        _, vjp = jax.vjp(_gmlp_chunk, *_gmlp_load(u_ref, v_ref, ws_ref, bst_ref, vg_ref, og_ref, layer))
        dgu, dgv, dws, dbs, dvg, dog = vjp(_hslices(dy_ref, w, GM_HEADS))
        for h in range(GM_HEADS):
            dgm_ref[:, h * w:(h + 1) * w] = _b(dgu[h])
            dgm_ref[:, D_GM + h * w: D_GM + (h + 1) * w] = _b(dgv[h])
            dws_ref[h] += dws[h]
            dbst_ref[:, h:h + 1] += dbs[h]
            dvg_ref[:, h * w:(h + 1) * w] += dvg[h]
            dog_ref[:, h * w:(h + 1) * w] += dog[h]

    return pl.pallas_call(
        body, name="gmlp_bwd", grid=(t // CHUNK,),
        in_specs=[pl.BlockSpec((CHUNK, D_GM), lambda i: (i, 1))] + _gmlp_specs(layer) + [ANY],
        out_specs=[pl.BlockSpec((CHUNK, 2 * D_GM), lambda i: (i, COL_U // (2 * D_GM))),
                   pl.BlockSpec((GM_HEADS, CHUNK, CHUNK), lambda i: (0, 0, 0)),
                   pl.BlockSpec((CHUNK, GM_HEADS), lambda i: (0, 0)),
                   pl.BlockSpec((1, D_GM), lambda i: (0, 0)),
                   pl.BlockSpec((1, D_GM), lambda i: (0, 0))],
        out_shape=[jax.ShapeDtypeStruct(dproj.shape, dproj.dtype), jax.ShapeDtypeStruct((GM_HEADS, CHUNK, CHUNK), F32),
                   jax.ShapeDtypeStruct((CHUNK, GM_HEADS), F32), jax.ShapeDtypeStruct((1, D_GM), F32),
                   jax.ShapeDtypeStruct((1, D_GM), F32)],
        input_output_aliases={7: 0},
        compiler_params=_cp("arbitrary"),
    )(dy, proj, proj, ws, bst, vg, og, dproj)


def _local_step(x, target, mods, w, final_g, *, nseq, big_w, grad_sink, small_sink):
    saved = []
    x0, delta, gate = x, None, None
    h1 = _normmod_fwd(x, w["norm1_g"], mods[0][1], mods[0][0], nseq=nseq, name="norm1_fwd_0", layer=0)
    for l in range(DEPTH):
        sh1, sc1, g1, sh2, sc2, g2 = mods[l]
        w_in = big_w(l, "w_in", h1)
        proj = _matmul(h1, w_in, tb=True, name=f"mm_in_{l}")
        xbc, xbc_pre = _ssd_conv_fwd(proj, w["ssd_conv_w"], w["ssd_conv_b"], nseq=nseq, layer=l)
        ycat, hprev = _ssd_fwd(xbc, proj, w["ssd_dt_bias"], w["ssd_a_log"], w["ssd_d"], w["ssd_norm_g"], nseq=nseq,
                               layer=l)
        ycat = _gmlp_fwd(proj, ycat, w["gm_ws"], w["gm_bst"], w["gm_vnorm_g"], w["gm_out_g"], layer=l)
        w_out = big_w(l, "w_out", ycat)
        mix, x1, h2 = _matmul_normfwd(ycat, w_out, x0, g1, w["norm2_g"], sc2, sh2, nseq=nseq, name=f"mm_out_{l}",
                                      layer=l)
        ff_up = big_w(l, "ff_up", h2)
        up = _matmul(h2, ff_up, tb=True, name=f"mm_up_{l}", out_dtype=BF16)
        act = _ffn_act_fwd(up, w["ff_conv_w"], w["ff_conv_b"], nseq=nseq, layer=l)
        ff_down = big_w(l, "ff_down", act)
        sv = dict(x0=x0, xin_delta=delta, xin_gate=gate, h1=h1, proj=proj, xbc=xbc, xbc_pre=xbc_pre, hprev=hprev,
                  ycat=ycat, mix=mix, x1=x1, h2=h2, up=up, act=act,
                  w_in=w_in, w_out=w_out, ff_up=ff_up, ff_down=ff_down)
        if l + 1 < DEPTH:
            nsh1, nsc1 = mods[l + 1][0], mods[l + 1][1]
            dn, x0, h1 = _matmul_normfwd(act, ff_down, x1, g2, w["norm1_g"], nsc1, nsh1, nseq=nseq,
                                         name=f"mm_down_{l}", layer=l + 1)
        else:
            dn, loss, dx, ddelta, dgate, dfg = _matmul_loss(act, ff_down, x1, g2, final_g, target, nseq=nseq,
                                                            name=f"mm_down_{l}")
        saved.append(dict(sv, dn=dn))
        delta, gate = dn, g2

    small, dmods = [None] * DEPTH, [None] * DEPTH
    for l in reversed(range(DEPTH)):
        sv = saved[l]
        sh1, sc1, g1, sh2, sc2, g2 = mods[l]
        dg2 = dgate
        g_ff_down = _matmul(sv["act"], ddelta, ta=True, name=f"mm_down_dw_{l}", out_dtype=BF16)
        dact = _matmul(ddelta, sv["ff_down"], tb=True, name=f"mm_down_dx_{l}", out_dtype=BF16)
        dgate_ff, dval_ff, dfcw, dfcb = _ffn_act_bwd(dact, sv["up"], w["ff_conv_w"], w["ff_conv_b"], nseq=nseq, layer=l)
        g_ff_up = _matmul([dgate_ff, dval_ff], sv["h2"], ta=True, name=f"mm_up_dw_{l}", out_dtype=BF16)
        dep = grad_sink(l, "ffn", dict(ff_down=g_ff_down, ff_up=g_ff_up), dval_ff)
        dx, dmix, dg1, dn2g, dsc2, dsh2 = _matmul_normbwd([dgate_ff, dval_ff], sv["ff_up"], dx, sv["x1"], sv["mix"], g1,
                                                          w["norm2_g"], sc2, nseq=nseq, name=f"mm_up_dx_{l}", layer=l,
                                                          dep=dep)
        g_w_out = _matmul(sv["ycat"], dmix, ta=True, name=f"mm_out_dw_{l}", out_dtype=BF16)
        dep = grad_sink(l, "w_out", dict(w_out=g_w_out), dmix)
        dycat = _matmul(dmix, sv["w_out"], tb=True, name=f"mm_out_dx_{l}", dep=dep)
        dxbc_act, dproj, ddtb, dalog, ddsk, dng = _ssd_bwd(dycat, sv["xbc"], sv["proj"], sv["hprev"], w["ssd_dt_bias"],
                                                          w["ssd_a_log"], w["ssd_d"], w["ssd_norm_g"], nseq=nseq, layer=l)
        dproj, dscw, dscb = _ssd_conv_bwd(dxbc_act, sv["xbc_pre"], sv["proj"], w["ssd_conv_w"], dproj, nseq=nseq,
                                          layer=l)
        dproj, dws, dbst, dvg, dog = _gmlp_bwd(dycat, sv["proj"], w["gm_ws"], w["gm_bst"], w["gm_vnorm_g"], w["gm_out_g"],
                                               dproj, layer=l)
        early = dict(norm2_g=dn2g, ssd_norm_g=dng, gm_vnorm_g=dvg, gm_out_g=dog,
                     ssd_conv_w=dscw, ssd_conv_b=dscb, ff_conv_w=dfcw, ff_conv_b=dfcb,
                     ssd_dt_bias=ddtb, ssd_a_log=dalog, ssd_d=ddsk, gm_ws=dws, gm_bs=dbst.T)
        dep = small_sink(l, early, small, dmods, dfg, loss)
        g_w_in = _matmul(dproj, sv["h1"], ta=True, name=f"mm_in_dw_{l}", out_dtype=BF16, dep=dep)
        dep = grad_sink(l, "w_in", dict(w_in=g_w_in), dproj)
        dx, ddelta, dgate, dn1g, dsc1, dsh1 = _matmul_normbwd(dproj, sv["w_in"], dx, sv["x0"], sv["xin_delta"],
                                                              sv["xin_gate"], w["norm1_g"], sc1, nseq=nseq,
                                                              name=f"mm_in_dx_{l}", layer=l, dep=dep)
        small[l] = dict(early, norm1_g=dn1g)
        dmods[l] = jnp.concatenate([dsh1, dsc1, dg1, dsh2, dsc2, dg2], axis=-1)[:, 0, :]
    return dx, small, dmods


def _all_gather(arrs, name, dep=None):
    n = len(arrs)
    extra = [] if dep is None else [dep]

    def body(*refs):
        ins, outs = refs[:n], refs[n + len(extra):2 * n + len(extra)]
        send_sems, recv_sems, local_sems = refs[2 * n + len(extra):]
        x, y, c = lax.axis_index("x"), lax.axis_index("y"), lax.axis_index("c")
        me, sibling = (x, y, c), (x, y, 1 - c)
        chips = [(1 - x, y), (x, 1 - y), (1 - x, 1 - y)]

        def copy(i, k, block, to, src=None):
            px, py, pc = block
            dst = outs[i].at[4 * px + 2 * py + pc]
            return pltpu.make_async_remote_copy(
                src_ref=dst if src is None else src, dst_ref=dst,
                send_sem=send_sems.at[7 * i + k], recv_sem=recv_sems.at[7 * i + k],
                device_id=to, device_id_type=MESH)

        mine = [pltpu.make_async_copy(ins[i], outs[i].at[4 * x + 2 * y + c], local_sems.at[i]) for i in range(n)]
        for cp in mine:
            cp.start()
        first = []
        for i in range(n):
            first.append(copy(i, 0, me, sibling, src=ins[i]))
            first += [copy(i, 1 + j, me, (*chip, c), src=ins[i]) for j, chip in enumerate(chips)]
        for cp in first:
            cp.start()
        passed = []
        for j, chip in enumerate(chips):
            for i in range(n):
                copy(i, 1 + j, (*chip, c), me).wait_recv()
                fwd = copy(i, 4 + j, (*chip, c), sibling)
                fwd.start()
                passed.append(fwd)
        for i in range(n):
            copy(i, 0, sibling, me).wait_recv()
            for j, chip in enumerate(chips):
                copy(i, 4 + j, (*chip, 1 - c), me).wait_recv()
        for cp in first + passed:
            cp.wait_send()
        for cp in mine:
            cp.wait()

    return pl.pallas_call(
        body, name=name,
        in_specs=[ANY] * (n + len(extra)), out_specs=[ANY] * n,
        out_shape=[jax.ShapeDtypeStruct((N_DEV,) + a.shape, a.dtype) for a in arrs],
        scratch_shapes=[pltpu.SemaphoreType.DMA((7 * n,)), pltpu.SemaphoreType.DMA((7 * n,)),
                        pltpu.SemaphoreType.DMA((n,))],
    )(*arrs, *extra)


HBM = pl.BlockSpec(memory_space=pltpu.HBM)
SEM = pl.BlockSpec(memory_space=pltpu.SEMAPHORE)
EFFECT = pltpu.SideEffectType.DATAFLOW_SIDE_EFFECTING


def _peer(k):
    x, y, c = lax.axis_index("x"), lax.axis_index("y"), lax.axis_index("c")
    return (1 - x if k & 4 else x, 1 - y if k & 2 else y, 1 - c if k & 1 else c)


ALL_PEERS = tuple(range(1, N_DEV))
OTHER_CHIPS = (2, 4, 6)


def _xc_copies(scatter, srcs, lands, send_sems, recv_sems, peers=ALL_PEERS):
    x, y, c = lax.axis_index("x"), lax.axis_index("y"), lax.axis_index("c")
    copies = []
    for i in range(len(srcs)):
        for k in (peers[i] if isinstance(peers[0], tuple) else peers):
            px, py, pc = _peer(k)
            src = srcs[i].at[4 * px + 2 * py + pc] if scatter else srcs[i]
            dst = lands[i].at[k - 1] if scatter else lands[i].at[4 * x + 2 * y + c]
            copies.append(pltpu.make_async_remote_copy(
                src_ref=src, dst_ref=dst, send_sem=send_sems[i].at[k - 1], recv_sem=recv_sems[i].at[k - 1],
                device_id=(px, py, pc), device_id_type=MESH))
    return copies


def _xc_own(scatter, srcs, lands, send_sems):
    if scatter:
        return []
    me = 4 * lax.axis_index("x") + 2 * lax.axis_index("y") + lax.axis_index("c")
    return [pltpu.make_async_copy(srcs[i], lands[i].at[me], send_sems[i].at[N_DEV - 1]) for i in range(len(srcs))]


def _xc_start(scatter, arrs, after, name, peers=ALL_PEERS):
    n = len(arrs)
    lands = [lax.empty((N_DEV - 1,) + a.shape[1:] if scatter else (N_DEV,) + a.shape, a.dtype) for a in arrs]

    def body(*refs):
        srcs, lnd = refs[:n], refs[n:2 * n]
        send_sems, recv_sems = refs[2 * n + 1:3 * n + 1], refs[3 * n + 1:4 * n + 1]
        token = refs[6 * n + 1]
        for cp in _xc_copies(scatter, srcs, lnd, send_sems, recv_sems, peers) + _xc_own(scatter, srcs, lnd, send_sems):
            cp.start()
        token[...] = jnp.zeros_like(token)

    outs = pl.pallas_call(
        body, name=name,
        out_shape=[pltpu.SemaphoreType.DMA((N_DEV,))] * (2 * n)
        + [pltpu.HBM(a.shape, a.dtype) for a in arrs] + [pltpu.HBM(a.shape, a.dtype) for a in lands]
        + [jax.ShapeDtypeStruct((8, 128), F32)],
        in_specs=[HBM] * (2 * n) + [ANY],
        out_specs=[SEM] * (2 * n) + [HBM] * (2 * n) + [pl.BlockSpec(memory_space=pltpu.VMEM)],
        input_output_aliases={i: 2 * n + i for i in range(2 * n)},
        compiler_params=pltpu.CompilerParams(has_side_effects=EFFECT),
    )(*[pltpu.with_memory_space_constraint(a, pltpu.HBM) for a in list(arrs) + lands], after)
    return outs[:n], outs[n:2 * n], outs[2 * n:3 * n], outs[3 * n:4 * n], outs[4 * n][0, 0]


def _xc_wait(scatter, send_sems, recv_sems, srcs, lands, after, name, peers=ALL_PEERS):
    n = len(srcs)

    def body(*refs):
        s_refs, l_refs = refs[:n], refs[n:2 * n]
        ss, rs = refs[2 * n:3 * n], refs[3 * n:4 * n]
        for cp in _xc_copies(scatter, s_refs, l_refs, ss, rs, peers):
            cp.wait_send()
            cp.wait_recv()
        for cp in _xc_own(scatter, s_refs, l_refs, ss):
            cp.wait()

    outs = pl.pallas_call(
        body, name=name,
        out_shape=[pltpu.HBM(a.shape, a.dtype) for a in list(srcs) + list(lands)],
        in_specs=[HBM] * (2 * n) + [SEM] * (2 * n) + [ANY],
        out_specs=[HBM] * (2 * n),
        input_output_aliases={i: i for i in range(2 * n)},
        compiler_params=pltpu.CompilerParams(has_side_effects=EFFECT),
    )(*srcs, *lands, *send_sems, *recv_sems, after)
    return outs[:n], outs[n:]


def _sib_copies(zones, send_sems, recv_sems):
    x, y, c = lax.axis_index("x"), lax.axis_index("y"), lax.axis_index("c")
    copies = []
    for i in range(len(zones)):
        for q in range(N_DEV // 2):
            slot = zones[i].at[2 * q + c]
            copies.append(pltpu.make_async_remote_copy(
                src_ref=slot, dst_ref=slot, send_sem=send_sems[i].at[q], recv_sem=recv_sems[i].at[q],
                device_id=(x, y, 1 - c), device_id_type=MESH))
    return copies


def _sib_start(zones, name):
    n = len(zones)

    def body(*refs):
        for cp in _sib_copies(refs[:n], refs[n:2 * n], refs[2 * n:3 * n]):
            cp.start()

    outs = pl.pallas_call(
        body, name=name,
        out_shape=[pltpu.SemaphoreType.DMA((N_DEV // 2,))] * (2 * n) + [pltpu.HBM(a.shape, a.dtype) for a in zones],
        in_specs=[HBM] * n,
        out_specs=[SEM] * (2 * n) + [HBM] * n,
        input_output_aliases={i: 2 * n + i for i in range(n)},
        compiler_params=pltpu.CompilerParams(has_side_effects=EFFECT),
    )(*[pltpu.with_memory_space_constraint(a, pltpu.HBM) for a in zones])
    return outs[:n], outs[n:2 * n], outs[2 * n:]


def _sib_wait(send_sems, recv_sems, zones, name):
    n = len(zones)

    def body(*refs):
        for cp in _sib_copies(refs[:n], refs[n:2 * n], refs[2 * n:3 * n]):
            cp.wait_send()
            cp.wait_recv()

    return pl.pallas_call(
        body, name=name,
        out_shape=[pltpu.HBM(a.shape, a.dtype) for a in zones],
        in_specs=[HBM] * n + [SEM] * (2 * n),
        out_specs=[HBM] * n,
        input_output_aliases={i: i for i in range(n)},
        compiler_params=pltpu.CompilerParams(has_side_effects=EFFECT),
    )(*zones, *send_sems, *recv_sems)


def _adamw_math(w, g, m, v):
    m = ADAM_B1 * m + (1.0 - ADAM_B1) * g
    v = ADAM_B2 * v + (1.0 - ADAM_B2) * (g * g)
    m_hat = m / (1.0 - ADAM_B1 ** ADAM_STEP)
    v_hat = v / (1.0 - ADAM_B2 ** ADAM_STEP)
    delta = -ADAM_LR * (m_hat / (jnp.sqrt(v_hat) + ADAM_EPS) + ADAM_WD * w)
    return delta, m, v


def _adamw_sharded(parts, w, m, v, pos, name):
    depth, rows, cols = w.shape
    tr = _tile(rows, 256) if rows % 8 == 0 else rows
    npart = len(parts)

    def body(pos_ref, *refs):
        prefs = refs[:npart]
        w_ref, m_ref, v_ref, g_out, d_out, m_out, v_out = refs[npart:]
        g = prefs[0][...]
        for pr in prefs[1:]:
            g = g + pr[...]
        delta, mn, vn = _adamw_math(w_ref[...], g, m_ref[...], v_ref[...])
        g_out[...] = g
        d_out[...] = delta
        m_out[...] = mn
        v_out[...] = vn

    def part_spec(fn):
        return pl.BlockSpec((1, tr, cols), lambda l, i, p: (fn(p) * depth + l, i, 0))

    blk = pl.BlockSpec((1, tr, cols), lambda l, i, p: (l, i, 0))
    shp = jax.ShapeDtypeStruct((depth, rows, cols), F32)
    return pl.pallas_call(
        body, name=name,
        grid_spec=pltpu.PrefetchScalarGridSpec(
            num_scalar_prefetch=1, grid=(depth, rows // tr),
            in_specs=[part_spec(fn) for _, fn in parts] + [blk, blk, blk],
            out_specs=[blk, blk, blk, blk]),
        out_shape=[shp, shp, shp, shp],
        compiler_params=_cp("parallel", "parallel"),
    )(pos, *[a for a, _ in parts], w, m, v)


def _adamw_layer(parts, w, m, v, pos, layer, prev, name):
    depth, rows, cols = w.shape
    npart = len(parts)
    nprev = 0 if prev is None else 4
    if rows % 16 == 0:
        tr, tc = max(t for t in range(16, 257, 16) if rows % t == 0), cols
    else:
        tr, tc = rows, _tile(cols, 256)
    pick = (lambda i: (i, 0)) if rows % 16 == 0 else (lambda i: (0, i))

    def body(pos_ref, *refs):
        prefs = refs[:npart]
        w_ref, m_ref, v_ref = refs[npart:npart + 3]
        g_out, d_out, m_out, v_out = refs[npart + 3 + nprev:]
        g = prefs[0][...].astype(F32)
        for pr in prefs[1:]:
            g = g + pr[...].astype(F32)
        delta, mn, vn = _adamw_math(w_ref[...], g, m_ref[...], v_ref[...])
        g_out[...] = g
        d_out[...] = delta
        m_out[...] = mn
        v_out[...] = vn

    def part_spec(fn):
        return pl.BlockSpec((1, tr, tc), lambda i, p: (fn(p), *pick(i)))

    blk = pl.BlockSpec((1, tr, tc), lambda i, p: (layer, *pick(i)))
    shp = jax.ShapeDtypeStruct((depth, rows, cols), F32)
    first_prev = 1 + npart + 3
    return pl.pallas_call(
        body, name=name,
        grid_spec=pltpu.PrefetchScalarGridSpec(
            num_scalar_prefetch=1, grid=(rows // tr * (cols // tc),),
            in_specs=[part_spec(fn) for _, fn in parts] + [blk, blk, blk] + [ANY] * nprev,
            out_specs=[blk, blk, blk, blk]),
        out_shape=[shp, shp, shp, shp],
        input_output_aliases={first_prev + j: j for j in range(nprev)},
        compiler_params=_cp("parallel"),
    )(pos, *[a for a, _ in parts], w, m, v, *(prev or ()))


def _adamw_rows_major(parts_by_layer, w, m, v, pos, name):
    rows, depth, cols = w.shape
    tc = _tile(cols, 256)
    npart = len(parts_by_layer[0])

    def body(pos_ref, *refs):
        prefs = refs[:depth * npart]
        w_ref, m_ref, v_ref, g_out, d_out, m_out, v_out = refs[depth * npart:]
        for l in range(depth):
            g = prefs[l * npart][0].astype(F32)
            for pr in prefs[l * npart + 1:(l + 1) * npart]:
                g = g + pr[0].astype(F32)
            delta, mn, vn = _adamw_math(w_ref[:, l, :], g, m_ref[:, l, :], v_ref[:, l, :])
            g_out[:, l, :] = g
            d_out[:, l, :] = delta
            m_out[:, l, :] = mn
            v_out[:, l, :] = vn

    def part_spec(fn):
        return pl.BlockSpec((1, rows, tc), lambda j, p: (fn(p), 0, j))

    blk = pl.BlockSpec((rows, depth, tc), lambda j, p: (0, 0, j))
    shp = jax.ShapeDtypeStruct(w.shape, F32)
    flat = [pf for parts in parts_by_layer for pf in parts]
    return pl.pallas_call(
        body, name=name,
        grid_spec=pltpu.PrefetchScalarGridSpec(
            num_scalar_prefetch=1, grid=(cols // tc,),
            in_specs=[part_spec(fn) for _, fn in flat] + [blk, blk, blk],
            out_specs=[blk, blk, blk, blk]),
        out_shape=[shp, shp, shp, shp],
        compiler_params=_cp("parallel"),
    )(pos, *[a for a, _ in flat], w, m, v)


_P1024 = ["norm1_g", "norm2_g", "ssd_norm_g", "gm_vnorm_g", "gm_out_g"]
_P16 = ["ssd_dt_bias", "ssd_a_log", "ssd_d"]


def _adamw_small(gath, wmv):
    names = list(wmv.keys())
    classes = list(gath.keys())
    flat_in = [gath[k] for k in classes]
    for nme in names:
        flat_in += list(wmv[nme])
    out_shapes = []
    for nme in names:
        out_shapes += [jax.ShapeDtypeStruct(wmv[nme][0].shape, F32)] * 4
    out_shapes += [jax.ShapeDtypeStruct((DEPTH, SSD_CONV, CONV_DIM), F32), jax.ShapeDtypeStruct((DEPTH, FF_CONV, D_FF), F32),
                   jax.ShapeDtypeStruct((1, SSD_HEADS), F32)]
    scratch = [pltpu.VMEM(gath[k].shape[1:], F32) for k in classes]
    ncls = len(classes)

    def body(*refs):
        g_refs = dict(zip(classes, refs[:ncls]))
        pos = ncls
        w_refs = {}
        for nme in names:
            w_refs[nme] = refs[pos:pos + 3]
            pos += 3
        o_refs = {}
        for nme in names:
            o_refs[nme] = refs[pos:pos + 4]
            pos += 4
        scw_out, fcw_out, loss_out = refs[pos], refs[pos + 1], refs[pos + 2]
        s_refs = dict(zip(classes, refs[pos + 3:]))
        for k in classes:
            acc = g_refs[k][0]
            for dev in range(1, N_DEV):
                acc = acc + g_refs[k][dev]
            s_refs[k][...] = acc

        def apply(nme, grad_of):
            w_ref, m_ref, v_ref = w_refs[nme]
            g_out, d_out, m_out, v_out = o_refs[nme]
            shape = w_ref.shape
            if len(shape) == 2:
                idxs = [(slice(l, l + 1),) for l in range(shape[0])]
            elif len(shape) == 3:
                idxs = [(l,) for l in range(shape[0])]
            else:
                idxs = [(l, h) for l in range(shape[0]) for h in range(shape[1])]
            for n_i, ix in enumerate(idxs):
                g = grad_of(n_i)
                delta, mn, vn = _adamw_math(w_ref[ix], g, m_ref[ix], v_ref[ix])
                g_out[ix] = g
                d_out[ix] = delta
                m_out[ix] = mn
                v_out[ix] = vn

        s1024, s1536, s2816, s16, s128, s6144, late1024, late6144 = (s_refs[k] for k in classes)
        s1024[0:1, :] += late1024[...]
        s6144[0:late6144.shape[0], :] += late6144[...]
        for n_i, nme in enumerate(_P1024):
            apply(nme, lambda l, b=2 * n_i: s1024[b + l:b + l + 1, :])
        apply("final_g", lambda l: s1024[10:11, :])
        apply("ssd_conv_b", lambda l: s1536[8 + l:9 + l, :])
        apply("ff_conv_b", lambda l: s2816[6 + l:7 + l, :])
        for n_i, nme in enumerate(_P16):
            apply(nme, lambda l, b=2 * n_i: s16[b + l:b + l + 1, :])
        apply("gm_ws", lambda q: s128[q * CHUNK:(q + 1) * CHUNK, :])
        apply("gm_bs", lambda l: s128[2048 + 8 * l:2048 + 8 * (l + 1), :])
        apply("ada_b", lambda l: s6144[2 * l:2 * l + 1, :] + s6144[2 * l + 1:2 * l + 2, :])
        for l in range(DEPTH):
            scw_out[l] = s1536[SSD_CONV * l:SSD_CONV * (l + 1), :]
            fcw_out[l] = s2816[FF_CONV * l:FF_CONV * (l + 1), :]
        loss_out[...] = s16[2 * len(_P16):2 * len(_P16) + 1, :]

    outs = pl.pallas_call(
        body, name="adamw_small",
        out_shape=out_shapes,
        scratch_shapes=scratch,
        compiler_params=pltpu.CompilerParams(vmem_limit_bytes=VMEM_LIMIT),
    )(*flat_in)
    res = {nme: tuple(outs[4 * i:4 * i + 4]) for i, nme in enumerate(names)}
    return res, outs[-3], outs[-2], outs[-1]


_WEIGHTS = ['ada_w', 'ada_b', 'norm1_g', 'norm2_g', 'w_in', 'ssd_conv_w', 'ssd_conv_b', 'ssd_dt_bias', 'ssd_a_log',
            'ssd_d', 'ssd_norm_g', 'gm_vnorm_g', 'gm_ws', 'gm_bs', 'gm_out_g', 'w_out', 'ff_up', 'ff_conv_w',
            'ff_conv_b', 'ff_down', 'final_g']


_O_XBC, _O_DT, _O_GM = D_SSD, D_SSD + CONV_DIM, D_SSD + CONV_DIM + SSD_HEADS


_TRANSPOSED = ("w_in", "ff_up")


def _full_weight(name, g):
    full = g.reshape(g.shape[0] * g.shape[1], g.shape[2])
    if name != "w_in":
        return full
    zpad = jnp.zeros((N_INP - N_IN, full.shape[1]), full.dtype)
    return jnp.concatenate([full[_O_GM:], full[:_O_XBC], full[_O_XBC:_O_DT], full[_O_DT:_O_GM], zpad], axis=0)


def _by_owner(name, grad):
    if name != "w_in":
        return grad.reshape(N_DEV, grad.shape[0] // N_DEV, grad.shape[1])
    rows = N_IN // N_DEV
    k, r = divmod(_O_GM, rows)
    at = lambda j: COL_Z + j * rows if j <= k else j * rows - _O_GM
    blocks = [grad[at(j):at(j) + rows] for j in range(N_DEV)]
    blocks[k] = jnp.concatenate([grad[at(k):at(k) + r], grad[:rows - r]], axis=0)
    return jnp.stack(blocks)


def kernel(x, c, ada_w, ada_b, norm1_g, norm2_g, w_in, ssd_conv_w, ssd_conv_b, ssd_dt_bias, ssd_a_log, ssd_d, ssd_norm_g, gm_vnorm_g, gm_ws, gm_bs, gm_out_g, w_out, ff_up, ff_conv_w, ff_conv_b, ff_down, final_g, loss_target, m_ada_w, m_ada_b, m_norm1_g, m_norm2_g, m_w_in, m_ssd_conv_w, m_ssd_conv_b, m_ssd_dt_bias, m_ssd_a_log, m_ssd_d, m_ssd_norm_g, m_gm_vnorm_g, m_gm_ws, m_gm_bs, m_gm_out_g, m_w_out, m_ff_up, m_ff_conv_w, m_ff_conv_b, m_ff_down, m_final_g, v_ada_w, v_ada_b, v_norm1_g, v_norm2_g, v_w_in, v_ssd_conv_w, v_ssd_conv_b, v_ssd_dt_bias, v_ssd_a_log, v_ssd_d, v_ssd_norm_g, v_gm_vnorm_g, v_gm_ws, v_gm_bs, v_gm_out_g, v_w_out, v_ff_up, v_ff_conv_w, v_ff_conv_b, v_ff_down, v_final_g):
    given = dict(locals())
    wts = {n: given[n] for n in _WEIGHTS}
    mom = {n: given["m_" + n] for n in _WEIGHTS}
    var = {n: given["v_" + n] for n in _WEIGHTS}
    nseq, seq, d = x.shape
    ix, iy, ic = lax.axis_index("x"), lax.axis_index("y"), lax.axis_index("c")
    me = 4 * ix + 2 * iy + ic
    me_arr = me.astype(jnp.int32).reshape(1)

    for nme, perm in (("ff_up", (0, 2, 1)), ("w_in", (2, 0, 1))):
        wts[nme], mom[nme], var[nme] = (jnp.transpose(a, perm) for a in (wts[nme], mom[nme], var[nme]))

    def shard(l, name):
        return _b(wts[name][:, l, :] if name == "w_in" else wts[name][l])

    g_scw, g_fcw, c_all = _all_gather([ssd_conv_w, ff_conv_w, c], "gather_first")
    scw_f = jnp.transpose(g_scw, (1, 2, 0, 3)).reshape(DEPTH, SSD_CONV, CONV_DIM)
    fcw_f = jnp.transpose(g_fcw, (1, 2, 0, 3)).reshape(DEPTH, FF_CONV, D_FF)
    c_all = c_all.reshape(N_DEV * nseq, d)

    n_ada = ada_w.shape[2]
    ada_b_shard = lax.dynamic_slice_in_dim(ada_b, me * n_ada, n_ada, axis=1).reshape(DEPTH, 1, n_ada)
    mod_part, c_act = _ada_fwd(c_all, ada_w, ada_b_shard)
    first_ssem, first_rsem, first_src, first_land, first_zero = _xc_start(
        False, [mod_part, shard(0, "w_in")], c_act, "ag_first_start", peers=[ALL_PEERS, OTHER_CHIPS])
    _, (mod_g,) = _xc_wait(False, first_ssem[:1], first_rsem[:1], first_src[:1], first_land[:1], c_act,
                           "mod_wait")
    mod_all = jnp.transpose(mod_g, (1, 2, 0, 3)).reshape(DEPTH, N_DEV * nseq, N_MOD * d)
    mod_mine = lax.dynamic_slice_in_dim(mod_all, me * nseq, nseq, axis=1)
    mod_k = jnp.transpose(mod_mine.reshape(DEPTH, nseq, N_MOD, 1, d), (0, 2, 1, 3, 4))
    mods = [[mod_k[l, k] for k in range(N_MOD)] for l in range(DEPTH)]

    later =[(0, "w_out"), (0, "ff_up"), (0, "ff_down"), (1, "w_in"), (1, "w_out"), (1, "ff_up"), (1, "ff_down")]
    ag_groups = {(0, "w_out"): [0], (0, "ff_up"): [1, 2], (1, "w_in"): [3, 4], (1, "ff_up"): [5, 6]}
    big_cache, ag = {}, {}

    def big_w(l, name, after):
        if (l, name) == (0, "w_in") and (l, name) not in big_cache:
            ag["ssem"], ag["rsem"], ag["src"], ag["land"], started = _xc_start(
                False, [shard(l2, n2) for l2, n2 in later], after, "ag_start")
            _, zones = _xc_wait(False, first_ssem[1:], first_rsem[1:], first_src[1:], first_land[1:],
                                jnp.full((8, 128), started, F32), "ag_first_wait", peers=OTHER_CHIPS)
            (zone,) = _sib_wait(*_sib_start(zones, "ag_first_sib_start"), "ag_first_sib_wait")
            big_cache[(l, name)] = _full_weight(name, zone)
        if (l, name) not in big_cache:
            idx = ag_groups[(l, name)]
            pick = lambda seq_: [seq_[i] for i in idx]
            _, lands = _xc_wait(False, pick(ag["ssem"]), pick(ag["rsem"]), pick(ag["src"]), pick(ag["land"]), after,
                                f"ag_wait_{l}_{name}")
            for i, land in zip(idx, lands):
                big_cache[later[i]] = _full_weight(later[i][1], land)
        return big_cache[(l, name)]

    small_w = dict(
        norm1_g=norm1_g + first_zero, norm2_g=norm2_g, ssd_conv_w=scw_f, ssd_conv_b=ssd_conv_b, ssd_dt_bias=ssd_dt_bias,
        ssd_a_log=ssd_a_log, ssd_d=ssd_d, ssd_norm_g=ssd_norm_g, gm_vnorm_g=gm_vnorm_g, gm_ws=gm_ws,
        gm_bst=jnp.transpose(gm_bs, (0, 2, 1)), gm_out_g=gm_out_g, ff_conv_w=fcw_f, ff_conv_b=ff_conv_b)

    outs = {}
    pending, win_parts = {}, {}

    def rs_finish(l, group, after):
        names, ssem, rsem, srcs, lands = pending.pop((l, group))
        srcs, lands = _xc_wait(True, ssem, rsem, srcs, lands, after, f"rs_wait_{l}_{group}")
        for nme, own, land in zip(names, srcs, lands):
            parts = [(own, lambda p: p[0])] + [(land, lambda p, k=k: k) for k in range(N_DEV - 1)]
            if nme == "w_in":
                win_parts[l] = parts
                if len(win_parts) == DEPTH:
                    outs[nme] = _adamw_rows_major([win_parts[k] for k in range(DEPTH)], wts[nme], mom[nme], var[nme],
                                                  me_arr, "adamw_w_in")
                continue
            outs[nme] = _adamw_layer(parts, wts[nme], mom[nme], var[nme], me_arr, l, outs.get(nme), f"adamw_{nme}_{l}")
        return land if names[-1] == "w_in" else outs[names[-1]][0]

    def grad_sink(l, group, grads, after):
        names = list(grads)
        ssem, rsem, srcs, lands, zero = _xc_start(True, [_by_owner(n, grads[n]) for n in names], after, f"rs_start_{l}_{group}")
        pending[(l, group)] = (names, ssem, rsem, srcs, lands)
        return zero.reshape(1, 1)

    early_gather = {}

    def small_sink(l, early, small, dmods, dfg, loss_p):
        if l > 0:
            return None
        layers = [dict(early, norm1_g=jnp.zeros((1, d), F32))] + small[1:]
        rows = lambda name: [layers[k][name] for k in range(DEPTH)]
        packed = [
            jnp.concatenate(sum([rows(n) for n in _P1024], []) + [dfg], axis=0),
            jnp.concatenate(rows("ssd_conv_w") + rows("ssd_conv_b"), axis=0),
            jnp.concatenate(rows("ff_conv_w") + rows("ff_conv_b"), axis=0),
            jnp.concatenate(sum([rows(n) for n in _P16], []) + [loss_p[:, :SSD_HEADS]], axis=0),
            jnp.concatenate([layers[k]["gm_ws"].reshape(GM_HEADS * CHUNK, CHUNK) for k in range(DEPTH)] + rows("gm_bs"), axis=0),
            jnp.concatenate([jnp.zeros((nseq, N_MOD * d), F32)] + dmods[1:], axis=0)]
        ssem, rsem, srcs, lands, zero = _xc_start(False, packed, packed[0], "small_start")
        early_gather.update(ssem=ssem, rsem=rsem, srcs=srcs, lands=lands)
        return zero.reshape(1, 1)

    grad_x, small, dmods = _local_step(
        x.reshape(nseq * seq, d), loss_target.reshape(nseq * seq, d), mods, small_w, final_g.reshape(1, d), nseq=nseq,
        big_w=big_w, grad_sink=grad_sink, small_sink=small_sink)

    done = grad_x
    for l, grp in ((1, "ffn"), (1, "w_out"), (1, "w_in"), (0, "ffn"), (0, "w_out")):
        done = rs_finish(l, grp, done)
    _, gathered = _xc_wait(False, early_gather["ssem"], early_gather["rsem"], early_gather["srcs"],
                           early_gather["lands"], done, "small_wait")
    gathered = list(gathered)
    gathered += _all_gather([small[0]["norm1_g"], dmods[0]], "gather_late", dep=gathered[0])
    gath = dict(zip(["p1024", "p1536", "p2816", "p16", "p128", "p6144", "late1024", "late6144"], gathered))

    dmod_all = jnp.concatenate([gath["late6144"].reshape(1, N_DEV * nseq, N_MOD * d),
                                jnp.transpose(gath["p6144"].reshape(N_DEV, DEPTH, nseq, N_MOD * d)[:, 1:], (1, 0, 2, 3)).reshape(
                                    DEPTH - 1, N_DEV * nseq, N_MOD * d)], axis=0)
    small_names = _P1024 + ["final_g", "ssd_conv_b", "ff_conv_b"] + _P16 + ["gm_ws", "gm_bs", "ada_b"]
    wmv = {}
    for nme in small_names:
        if nme == "final_g":
            wmv[nme] = tuple(a.reshape(1, d) for a in (wts[nme], mom[nme], var[nme]))
        else:
            wmv[nme] = (wts[nme], mom[nme], var[nme])
    small_out, scw_full, fcw_full, loss_sum = _adamw_small(gath, wmv)
    loss = loss_sum[0, 0]
    rs_finish(0, "w_in", scw_full)
    for nme in small_names:
        outs[nme] = small_out[nme]
    outs["final_g"] = tuple(a.reshape(d) for a in outs["final_g"])

    n_scw, n_fcw = ssd_conv_w.shape[2], ff_conv_w.shape[2]
    g_scw_mine = lax.dynamic_slice_in_dim(scw_full, me * n_scw, n_scw, axis=2)
    g_fcw_mine = lax.dynamic_slice_in_dim(fcw_full, me * n_fcw, n_fcw, axis=2)
    outs["ssd_conv_w"] = _adamw_sharded([(g_scw_mine, lambda p: 0)], ssd_conv_w, m_ssd_conv_w, v_ssd_conv_w, me_arr, "adamw_ssd_conv_w")
    outs["ff_conv_w"] = _adamw_sharded([(g_fcw_mine, lambda p: 0)], ff_conv_w, m_ff_conv_w, v_ff_conv_w, me_arr, "adamw_ff_conv_w")

    dmod_cols = _b(lax.dynamic_slice_in_dim(dmod_all, me * n_ada, n_ada, axis=2))
    g_ada = jnp.stack([_matmul(c_act, dmod_cols[l], ta=True, name=f"mm_ada_dw_{l}") for l in range(DEPTH)])
    outs["ada_w"] = _adamw_sharded([(g_ada, lambda p: 0)], ada_w, m_ada_w, v_ada_w, me_arr, "adamw_ada_w")

    for nme, perm in (("ff_up", (0, 2, 1)), ("w_in", (1, 2, 0))):
        outs[nme] = tuple(jnp.transpose(a, perm) for a in outs[nme])
    result = [loss, grad_x.reshape(nseq, seq, d)]
    for k in range(4):
        result += [outs[n][k] for n in _WEIGHTS]
    return tuple(result)
```

```python
import functools
import math

import jax
import jax.numpy as jnp
from jax import lax
from jax.experimental import pallas as pl
from jax.experimental.pallas import tpu as pltpu

F32 = jnp.float32
BF16 = jnp.bfloat16

N_DEV = 8
D_MODEL = 1024
DEPTH = 2
CHUNK = 128
SSD_HEADS = 16
SSD_HEAD_DIM = 64
SSD_GROUPS = 2
HEADS_PER_GROUP = SSD_HEADS // SSD_GROUPS
GROUP_WIDTH = HEADS_PER_GROUP * SSD_HEAD_DIM
D_STATE = 128
D_SSD = 1024
CONV_DIM = 1536
SSD_CONV = 4
GM_HEADS = 8
GM_HEAD_DIM = 128
D_GM = 1024
D_FF = 2816
FF_CONV = 3
N_IN = 4624
N_MOD = 6
EPS = 1e-6

N_INP = 5120
COL_U, COL_V, COL_Z, COL_XBC, COL_DT = 0, 1024, 2048, 3072, 4608

ADAM_LR = 0.001
ADAM_B1 = 0.9
ADAM_B2 = 0.999
ADAM_EPS = 1e-08
ADAM_WD = 0.01
ADAM_STEP = 10

VMEM_LIMIT = 56 * 1024 * 1024
MESH = pl.DeviceIdType.MESH
ANY = pl.BlockSpec(memory_space=pl.ANY)


def _cp(*sem):
    return pltpu.CompilerParams(dimension_semantics=sem, vmem_limit_bytes=VMEM_LIMIT)


def _tile(n, pref):
    if n <= pref or n % 128:
        return n
    best = 128
    for t in range(128, pref + 1, 128):
        if n % t == 0:
            best = t
    return best


def _per_layer(n):
    return pl.BlockSpec((DEPTH, n), lambda *_: (0, 0))


def _row(ref, layer, cols=slice(None)):
    return ref[layer:layer + 1, cols]


def _silu(x):
    return x * jax.nn.sigmoid(x)


def _gelu(x):
    return 0.5 * x * (1.0 + lax.erf(x * (1.0 / math.sqrt(2.0))))


def _softplus(x):
    return jnp.maximum(x, 0.0) + jnp.log1p(jnp.exp(-jnp.abs(x)))


def _b(x):
    return x.astype(BF16)


_NN = (((1,), (0,)), ((), ()))
_NT = (((1,), (1,)), ((), ()))
_TN = (((0,), (0,)), ((), ()))


def _dg(a, b, dn):
    return lax.dot_general(_b(a), _b(b), dn, preferred_element_type=F32)


@jax.custom_vjp
def _bdot(a, b):
    return _dg(a, b, _NN)


def _bdot_fwd(a, b):
    return _dg(a, b, _NN), (a, b)


def _bdot_bwd(res, ct):
    a, b = res
    return _dg(ct, b, _NT), _dg(a, ct, _TN)


_bdot.defvjp(_bdot_fwd, _bdot_bwd)


@jax.custom_vjp
def _bdot_nt(a, b):
    return _dg(a, b, _NT)


def _bdot_nt_fwd(a, b):
    return _dg(a, b, _NT), (a, b)


def _bdot_nt_bwd(res, ct):
    a, b = res
    return _dg(ct, b, _NN), _dg(ct, a, _TN)


_bdot_nt.defvjp(_bdot_nt_fwd, _bdot_nt_bwd)


@jax.custom_vjp
def _bdot_tn(a, b):
    return _dg(a, b, _TN)


def _bdot_tn_fwd(a, b):
    return _dg(a, b, _TN), (a, b)


def _bdot_tn_bwd(res, ct):
    a, b = res
    return _dg(b, ct, _NT), _dg(a, ct, _NN)


_bdot_tn.defvjp(_bdot_tn_fwd, _bdot_tn_bwd)


def _tri(n, lower):
    r = lax.broadcasted_iota(jnp.int32, (n, n), 0)
    c = lax.broadcasted_iota(jnp.int32, (n, n), 1)
    return ((r >= c) if lower else (r <= c)).astype(F32)


def _eye(n):
    r = lax.broadcasted_iota(jnp.int32, (n, n), 0)
    c = lax.broadcasted_iota(jnp.int32, (n, n), 1)
    return (r == c).astype(F32)


def _hdot(a, b, dn):
    return lax.dot_general(a, b, dn, precision=lax.Precision.HIGHEST, preferred_element_type=F32)


@jax.custom_vjp
def _cumsum_rows(x):
    return _hdot(_tri(x.shape[0], True), x, _NN)


def _cumsum_rows_fwd(x):
    return _cumsum_rows(x), None


def _cumsum_rows_bwd(_, ct):
    return (_hdot(_tri(ct.shape[0], False), ct, _NN),)


_cumsum_rows.defvjp(_cumsum_rows_fwd, _cumsum_rows_bwd)


@jax.custom_vjp
def _transpose(x):
    return _hdot(_eye(x.shape[1]), x, _NT)


def _transpose_fwd(x):
    return _transpose(x), None


def _transpose_bwd(_, ct):
    return (_hdot(_eye(ct.shape[1]), ct, _NT),)


_transpose.defvjp(_transpose_fwd, _transpose_bwd)


MXU_WIDTH = 256
MATMUL_TILE_CAP = 2816
MATMUL_VMEM = 44 * 1024 * 1024


def _mxu_tiles(n):
    if n <= MATMUL_TILE_CAP or n % 128:
        return [n]
    for unit in (MXU_WIDTH, 128):
        opts = [t for t in range(unit, MATMUL_TILE_CAP + 1, unit) if n % t == 0]
        if opts:
            return opts
    return [n]


def _matmul(a, b, *, ta=False, tb=False, name, dep=None, out_dtype=F32):
    pieces = list(a) if isinstance(a, (list, tuple)) else [a]
    npc = len(pieces)
    rows, width = pieces[0].shape
    assert all(p.shape == (rows, width) for p in pieces)
    if ta:
        k_dim, m_dim = rows, width * npc
    else:
        m_dim, k_dim = rows, width * npc
    if tb:
        n_dim, kb = b.shape
    else:
        kb, n_dim = b.shape
    assert kb == k_dim, (pieces[0].shape, npc, b.shape, ta, tb)
    m_unit = width if npc > 1 and ta else m_dim
    k_unit = width if npc > 1 and not ta else k_dim
    tm = _tile(m_unit, 1536)
    tn_opts, tk_opts = _mxu_tiles(n_dim), _mxu_tiles(k_unit)
    tn, tk = tn_opts.pop(), tk_opts.pop()
    while 4 * (tm * tk + tk * tn) + 8 * tm * tn > MATMUL_VMEM:
        if tn >= tk and tn_opts:
            tn = tn_opts.pop()
        else:
            tk = tk_opts.pop()
    ni, nj, nk = m_dim // tm, n_dim // tn, k_dim // tk
    per = width // (tm if ta else tk)
    dn = (((0 if ta else 1,), (1 if tb else 0,)), ((), ()))

    a_bytes, b_bytes = m_dim * k_dim, k_dim * n_dim
    m_outer = nk > 1 or a_bytes + b_bytes * ni <= b_bytes + a_bytes * nj
    if m_outer:
        ij = lambda o, n, k: (o, n)
        grid = (ni, nj, nk)
    else:
        ij = lambda o, n, k: (n, o)
        grid = (nj, ni, nk)

    use_acc = nk > 1 and out_dtype != F32

    def body(*refs):
        a_refs, b_ref = refs[:npc], refs[npc]
        o_ref = refs[-2] if use_acc else refs[-1]
        acc_ref = refs[-1]
        k = pl.program_id(2)
        i = pl.program_id(0 if m_outer else 1)
        along = i if ta else k

        def step(a_ref):
            p = lax.dot_general(a_ref[...], b_ref[...], dn, preferred_element_type=F32)
            if nk == 1:
                o_ref[...] = p.astype(out_dtype)
            else:
                @pl.when(k == 0)
                def _():
                    acc_ref[...] = p

                @pl.when((k > 0) & (k < nk - 1 if use_acc else True))
                def _():
                    acc_ref[...] += p

                if use_acc:
                    @pl.when(k == nk - 1)
                    def _():
                        o_ref[...] = (acc_ref[...] + p).astype(out_dtype)

        if npc == 1:
            step(a_refs[0])
        else:
            for pc in range(npc):
                pl.when((along >= pc * per) & (along < (pc + 1) * per))(functools.partial(step, a_refs[pc]))

    def a_map(pc, o, n, k):
        i, _ = ij(o, n, k)
        along = i if ta else k
        if npc > 1:
            along = jnp.clip(along - pc * per, 0, per - 1)
        return (k, along) if ta else (i, along)

    def b_map(o, n, k):
        _, j = ij(o, n, k)
        return (j, k) if tb else (k, j)

    extra = [] if dep is None else [dep]
    return pl.pallas_call(
        body, name=name,
        grid=grid,
        in_specs=[pl.BlockSpec((tk, tm) if ta else (tm, tk), functools.partial(a_map, pc)) for pc in range(npc)]
        + [pl.BlockSpec((tn, tk) if tb else (tk, tn), b_map)] + [ANY] * len(extra),
        out_specs=pl.BlockSpec((tm, tn), lambda o, n, k: ij(o, n, k)),
        out_shape=jax.ShapeDtypeStruct((m_dim, n_dim), out_dtype),
        scratch_shapes=[pltpu.VMEM((tm, tn), F32)] if use_acc else [],
        compiler_params=_cp("parallel", "parallel", "arbitrary"),
    )(*pieces, b, *extra)


def _ada_fwd(c_all, ada_w, ada_b_shard):
    depth, d, n = ada_w.shape
    nb = c_all.shape[0]

    def body(c_ref, w_ref, b_ref, o_ref, ca_ref):
        ca = _silu(c_ref[...])
        ca_ref[...] = _b(ca)
        o_ref[0] = _dg(ca, w_ref[0], _NN) + b_ref[0]

    return pl.pallas_call(
        body, name="ada_fwd",
        grid=(depth,),
        in_specs=[pl.BlockSpec((nb, d), lambda l: (0, 0)),
                  pl.BlockSpec((1, d, n), lambda l: (l, 0, 0)),
                  pl.BlockSpec((1, 1, n), lambda l: (l, 0, 0))],
        out_specs=[pl.BlockSpec((1, nb, n), lambda l: (l, 0, 0)),
                   pl.BlockSpec((nb, d), lambda l: (0, 0))],
        out_shape=[jax.ShapeDtypeStruct((depth, nb, n), F32), jax.ShapeDtypeStruct((nb, d), BF16)],
        compiler_params=_cp("arbitrary"),
    )(c_all, ada_w, ada_b_shard)


def _fold(acc):
    return jnp.sum(acc, axis=0, keepdims=True)


def _rinv(x):
    return lax.rsqrt(jnp.sum(x * x, axis=-1, keepdims=True) * (1.0 / D_MODEL) + EPS)


def _rms_bwd(a, xhat, rinv):
    return rinv * (a - xhat * (jnp.sum(a * xhat, axis=-1, keepdims=True) * (1.0 / D_MODEL)))


def _row_tile(seq):
    return min(seq, 256)


def _normmod_fwd(x, g, sc, sh, *, nseq, name, layer):
    t, d = x.shape
    seq = t // nseq
    tr = _row_tile(seq)
    nt = seq // tr
    row = pl.BlockSpec((tr, d), lambda s, i: (s * nt + i, 0))
    per_seq = pl.BlockSpec((1, 1, d), lambda s, i: (s, 0, 0))

    def body(x_ref, g_ref, sc_ref, sh_ref, h_ref):
        x_v = x_ref[...]
        h_ref[...] = _b(x_v * _rinv(x_v) * (_row(g_ref, layer) * (1.0 + sc_ref[0])) + sh_ref[0])

    return pl.pallas_call(
        body, name=name, grid=(nseq, nt),
        in_specs=[row, _per_layer(d), per_seq, per_seq],
        out_specs=row,
        out_shape=jax.ShapeDtypeStruct((t, d), BF16),
        compiler_params=_cp("parallel", "parallel"),
    )(x, g, sc, sh)


NORM_TM = 512


def _matmul_normbwd(a, b, dxo, x, delta, gate, g, sc, *, nseq, name, layer, dep=None):
    pieces = list(a) if isinstance(a, (list, tuple)) else [a]
    npc = len(pieces)
    t, width = pieces[0].shape
    k_dim, d = width * npc, b.shape[1]
    assert b.shape[0] == k_dim and all(p.shape == (t, width) for p in pieces)
    seq = t // nseq
    tm = min(NORM_TM, seq)
    per_seq_tiles = seq // tm
    tk = _mxu_tiles(width if npc > 1 else k_dim).pop()
    nk, per = k_dim // tk, width // tk
    has_delta = delta is not None
    extra = [] if dep is None else [dep]

    def body(*refs):
        a_refs, b_ref = refs[:npc], refs[npc]
        dxo_ref, x_ref = refs[npc + 1], refs[npc + 2]
        pos = npc + 3
        if has_delta:
            delta_ref, gate_ref = refs[pos], refs[pos + 1]
            pos += 2
        g_ref, sc_ref = refs[pos], refs[pos + 1]
        pos += 2 + len(extra)
        if has_delta:
            dx_ref, dd_ref, dgate_ref, dg_ref, dsc_ref, dsh_ref = refs[pos:pos + 6]
        else:
            dx_ref, dg_ref, dsc_ref, dsh_ref = refs[pos:pos + 4]
        acc_ref = refs[-1]
        i, k = pl.program_id(0), pl.program_id(1)

        def norm_bwd(dh_v):
            g_v, one_sc = _row(g_ref, layer), 1.0 + sc_ref[0]
            x_v = x_ref[...]
            rinv = _rinv(x_v)
            xhat = x_v * rinv
            dx = dxo_ref[...] + _rms_bwd(dh_v * (g_v * one_sc), xhat, rinv)
            dx_ref[...] = dx

            @pl.when(i == 0)
            def _():
                dg_ref[...] = jnp.zeros_like(dg_ref)

            @pl.when(i % per_seq_tiles == 0)
            def _():
                dsc_ref[...] = jnp.zeros_like(dsc_ref)
                dsh_ref[...] = jnp.zeros_like(dsh_ref)
                if has_delta:
                    dgate_ref[...] = jnp.zeros_like(dgate_ref)

            t_sum = _fold(dh_v * xhat)
            dg_ref[...] += t_sum * one_sc
            dsc_ref[0] += t_sum * g_v
            dsh_ref[0] += _fold(dh_v)
            if has_delta:
                dd_ref[...] = _b(dx * gate_ref[0])
                dgate_ref[0] += _fold(dx * delta_ref[...])

        def step(a_ref):
            p = lax.dot_general(a_ref[...], b_ref[...], _NN, preferred_element_type=F32)
            if nk == 1:
                norm_bwd(p)
            else:
                @pl.when(k == 0)
                def _():
                    acc_ref[...] = p

                @pl.when((k > 0) & (k < nk - 1))
                def _():
                    acc_ref[...] += p

                @pl.when(k == nk - 1)
                def _():
                    norm_bwd(acc_ref[...] + p)

        if npc == 1:
            step(a_refs[0])
        else:
            for pc in range(npc):
                pl.when((k >= pc * per) & (k < (pc + 1) * per))(functools.partial(step, a_refs[pc]))

    def a_map(pc, i, k):
        return (i, jnp.clip(k - pc * per, 0, per - 1) if npc > 1 else k)

    row = pl.BlockSpec((tm, d), lambda i, k: (i, 0))
    per_seq = pl.BlockSpec((1, 1, d), lambda i, k: (i // per_seq_tiles, 0, 0))
    vec = pl.BlockSpec((1, d), lambda i, k: (0, 0))
    shp = lambda *s, dt=F32: jax.ShapeDtypeStruct(s, dt)
    in_specs = [pl.BlockSpec((tm, tk), functools.partial(a_map, pc)) for pc in range(npc)]
    in_specs += [pl.BlockSpec((tk, d), lambda i, k: (k, 0)), row, row]
    operands = [*pieces, b, dxo, x]
    if has_delta:
        in_specs += [row, per_seq]
        operands += [delta, gate]
    in_specs += [_per_layer(d), per_seq] + [ANY] * len(extra)
    operands += [g, sc, *extra]
    if has_delta:
        out_specs = [row, row, per_seq, vec, per_seq, per_seq]
        out_shape = [shp(t, d), shp(t, d, dt=BF16), shp(nseq, 1, d), shp(1, d), shp(nseq, 1, d), shp(nseq, 1, d)]
    else:
        out_specs = [row, vec, per_seq, per_seq]
        out_shape = [shp(t, d), shp(1, d), shp(nseq, 1, d), shp(nseq, 1, d)]
    outs = pl.pallas_call(
        body, name=name, grid=(t // tm, nk),
        in_specs=in_specs, out_specs=out_specs, out_shape=out_shape,
        scratch_shapes=[pltpu.VMEM((tm, d), F32)],
        compiler_params=_cp("arbitrary", "arbitrary"),
    )(*operands)
    if has_delta:
        return tuple(outs)
    dx, dg, dsc, dsh = outs
    return dx, None, None, dg, dsc, dsh


def _matmul_normfwd(a, b, xin, gate, g, sc, sh, *, nseq, name, layer):
    t, k_dim = a.shape
    d = b.shape[1]
    assert b.shape[0] == k_dim and k_dim <= MATMUL_TILE_CAP
    seq = t // nseq
    tm = min(NORM_TM, seq)
    per_seq_tiles = seq // tm

    def body(a_ref, b_ref, xin_ref, gate_ref, g_ref, sc_ref, sh_ref, dl_ref, x_ref, h_ref):
        dl = lax.dot_general(a_ref[...], b_ref[...], _NN, preferred_element_type=F32)
        dl_ref[...] = dl
        x = xin_ref[...] + gate_ref[0] * dl
        x_ref[...] = x
        h_ref[...] = _b(x * _rinv(x) * (_row(g_ref, layer) * (1.0 + sc_ref[0])) + sh_ref[0])

    row = pl.BlockSpec((tm, d), lambda i: (i, 0))
    per_seq = pl.BlockSpec((1, 1, d), lambda i: (i // per_seq_tiles, 0, 0))
    return pl.pallas_call(
        body, name=name, grid=(t // tm,),
        in_specs=[pl.BlockSpec((tm, k_dim), lambda i: (i, 0)), pl.BlockSpec((k_dim, d), lambda i: (0, 0)),
                  row, per_seq, _per_layer(d), per_seq, per_seq],
        out_specs=[row, row, row],
        out_shape=[jax.ShapeDtypeStruct((t, d), F32), jax.ShapeDtypeStruct((t, d), F32), jax.ShapeDtypeStruct((t, d), BF16)],
        compiler_params=_cp("parallel"),
    )(a, b, xin, gate, g, sc, sh)


def _matmul_loss(a, b, xin, gate, fg, target, *, nseq, name):
    t, k_dim = a.shape
    d = b.shape[1]
    assert b.shape[0] == k_dim and k_dim <= MATMUL_TILE_CAP
    seq = t // nseq
    tm = min(NORM_TM, seq)
    per_seq_tiles = seq // tm

    def body(a_ref, b_ref, xin_ref, gate_ref, fg_ref, tgt_ref, dl_ref, loss_ref, dx_ref, dd_ref, dgate_ref, dfg_ref):
        i = pl.program_id(0)
        fg_v, gate_v = fg_ref[...], gate_ref[0]
        dl = lax.dot_general(a_ref[...], b_ref[...], _NN, preferred_element_type=F32)
        dl_ref[...] = dl
        x = xin_ref[...] + gate_v * dl
        rinv = _rinv(x)
        xhat = x * rinv
        err = xhat * fg_v - tgt_ref[...]
        dx = _rms_bwd(err * fg_v * (1.0 / d), xhat, rinv)
        dx_ref[...] = dx
        dd_ref[...] = _b(dx * gate_v)

        @pl.when(i == 0)
        def _():
            loss_ref[...] = jnp.zeros_like(loss_ref)
            dfg_ref[...] = jnp.zeros_like(dfg_ref)

        @pl.when(i % per_seq_tiles == 0)
        def _():
            dgate_ref[...] = jnp.zeros_like(dgate_ref)

        loss_ref[...] += jnp.sum(err * err) * (0.5 / d)
        dfg_ref[...] += _fold(err * xhat) * (1.0 / d)
        dgate_ref[0] += _fold(dx * dl)

    row = pl.BlockSpec((tm, d), lambda i: (i, 0))
    per_seq = pl.BlockSpec((1, 1, d), lambda i: (i // per_seq_tiles, 0, 0))
    vec = pl.BlockSpec((1, d), lambda i: (0, 0))
    return pl.pallas_call(
        body, name=name, grid=(t // tm,),
        in_specs=[pl.BlockSpec((tm, k_dim), lambda i: (i, 0)), pl.BlockSpec((k_dim, d), lambda i: (0, 0)),
                  row, per_seq, vec, row],
        out_specs=[row, pl.BlockSpec((1, 128), lambda i: (0, 0)), row, row, per_seq, vec],
        out_shape=[jax.ShapeDtypeStruct((t, d), F32), jax.ShapeDtypeStruct((1, 128), F32), jax.ShapeDtypeStruct((t, d), F32),
                   jax.ShapeDtypeStruct((t, d), BF16), jax.ShapeDtypeStruct((nseq, 1, d), F32),
                   jax.ShapeDtypeStruct((1, d), F32)],
        compiler_params=_cp("arbitrary"),
    )(a, b, xin, gate, fg, target)


CONV_TC = 256
CONV_LANES = 128
CONV_ROWS = 64
CONV_HALO = 8


def _conv_slabs(seq, fn):
    def step(i, carry):
        r0 = pl.multiple_of(i * CONV_ROWS, CONV_ROWS)
        for h in range(CONV_TC // CONV_LANES):
            fn(r0, slice(h * CONV_LANES, (h + 1) * CONV_LANES))
        return carry

    lax.fori_loop(0, seq // CONV_ROWS, step, 0)


def _slab(ref, r0, cols, seq):
    after = ref[pl.ds(pl.multiple_of(jnp.minimum(r0 + CONV_ROWS, seq - CONV_HALO), CONV_HALO), CONV_HALO), cols]
    return jnp.concatenate([ref[pl.ds(r0, CONV_ROWS), cols], jnp.where(r0 + CONV_ROWS < seq, after, 0.0)], axis=0)


def _conv_block(x, w_ref, b):
    kw = w_ref.shape[0]
    rows = lax.broadcasted_iota(jnp.int32, x.shape, 0)
    y = b + w_ref[kw - 1:kw, :] * x
    for j in range(1, kw):
        y = y + w_ref[kw - 1 - j:kw - j, :] * jnp.where(rows >= j, pltpu.roll(x, j, 0), 0.0)
    return y


def _conv_block_bwd(dy, x, w_ref, dw_ref, db_ref):
    kw = w_ref.shape[0]
    n = x.shape[0]
    rows = lax.broadcasted_iota(jnp.int32, x.shape, 0)
    dx = w_ref[kw - 1:kw, :] * dy
    dw_ref[kw - 1:kw, :] += jnp.sum(dy * x, axis=0, keepdims=True)
    for j in range(1, kw):
        dy_j = jnp.where(rows < n - j, pltpu.roll(dy, n - j, 0), 0.0)
        dx = dx + w_ref[kw - 1 - j:kw - j, :] * dy_j
        dw_ref[kw - 1 - j:kw - j, :] += jnp.sum(dy_j * x, axis=0, keepdims=True)
    db_ref[...] += jnp.sum(dy, axis=0, keepdims=True)
    return dx


def _conv_bwd(dy_ext, x, w_ref, dw_ref, db_ref, cols):
    kw = w_ref.shape[0]
    n = dy_ext.shape[0]
    dy = dy_ext[:CONV_ROWS]
    dx = w_ref[kw - 1:kw, cols] * dy
    dw_ref[kw - 1:kw, cols] += jnp.sum(dy * x, axis=0, keepdims=True)
    for j in range(1, kw):
        dy_j = pltpu.roll(dy_ext, n - j, 0)[:CONV_ROWS]
        dx = dx + w_ref[kw - 1 - j:kw - j, cols] * dy_j
        dw_ref[kw - 1 - j:kw - j, cols] += jnp.sum(dy_j * x, axis=0, keepdims=True)
    db_ref[:, cols] += jnp.sum(dy, axis=0, keepdims=True)
    return dx


def _dsilu(pre):
    sg = jax.nn.sigmoid(pre)
    return pre * sg, sg * (1.0 + pre * (1.0 - sg))


def _conv_specs(kw, layer):
    return [pl.BlockSpec((None, kw, CONV_TC), lambda j, s: (layer, 0, j)),
            pl.BlockSpec((DEPTH, CONV_TC), lambda j, s: (0, j))]


def _ssd_conv_fwd(proj, w, b, *, nseq, layer):
    t = proj.shape[0]
    seq = t // nseq
    nb = CONV_DIM // CONV_TC
    off = COL_XBC // CONV_TC

    def body(x_ref, w_ref, b_ref, o_ref, pre_ref):
        pre = _conv_block(x_ref[...], w_ref, _row(b_ref, layer))
        pre_ref[...] = pre
        o_ref[...] = _silu(pre)

    col = pl.BlockSpec((seq, CONV_TC), lambda j, s: (s, j))
    return pl.pallas_call(
        body, name="ssd_conv_fwd", grid=(nb, nseq),
        in_specs=[pl.BlockSpec((seq, CONV_TC), lambda j, s: (s, off + j)), *_conv_specs(SSD_CONV, layer)],
        out_specs=[col, col],
        out_shape=[jax.ShapeDtypeStruct((t, CONV_DIM), F32)] * 2,
        compiler_params=_cp("parallel", "parallel"),
    )(proj, w, b)


def _ssd_conv_bwd(dact, pre, proj, w, dproj, *, nseq, layer):
    t = proj.shape[0]
    seq = t // nseq
    nb = CONV_DIM // CONV_TC
    off = COL_XBC // CONV_TC

    def body(da_ref, pre_ref, x_ref, w_ref, dproj_ref, dx_ref, dw_ref, db_ref):
        del dproj_ref

        @pl.when(pl.program_id(1) == 0)
        def _():
            dw_ref[...] = jnp.zeros_like(dw_ref)
            db_ref[...] = jnp.zeros_like(db_ref)

        def slab(r0, cols):
            _, dsilu = _dsilu(_slab(pre_ref, r0, cols, seq))
            dpre_ext = _slab(da_ref, r0, cols, seq) * dsilu
            x = x_ref[pl.ds(r0, CONV_ROWS), cols]
            dx_ref[pl.ds(r0, CONV_ROWS), cols] = _b(_conv_bwd(dpre_ext, x, w_ref, dw_ref, db_ref, cols))

        _conv_slabs(seq, slab)

    return pl.pallas_call(
        body, name="ssd_conv_bwd", grid=(nb, nseq),
        in_specs=[pl.BlockSpec((seq, CONV_TC), lambda j, s: (s, j)),
                  pl.BlockSpec((seq, CONV_TC), lambda j, s: (s, j)),
                  pl.BlockSpec((seq, CONV_TC), lambda j, s: (s, off + j)),
                  _conv_specs(SSD_CONV, layer)[0],
                  ANY],
        out_specs=[pl.BlockSpec((seq, CONV_TC), lambda j, s: (s, off + j)),
                   pl.BlockSpec((SSD_CONV, CONV_TC), lambda j, s: (0, j)),
                   pl.BlockSpec((1, CONV_TC), lambda j, s: (0, j))],
        out_shape=[jax.ShapeDtypeStruct(dproj.shape, dproj.dtype), jax.ShapeDtypeStruct((SSD_CONV, CONV_DIM), F32),
                   jax.ShapeDtypeStruct((1, CONV_DIM), F32)],
        input_output_aliases={4: 0},
        compiler_params=_cp("parallel", "arbitrary"),
    )(dact, pre, proj, w, dproj)


def _ffn_act_fwd(up, w, b, *, nseq, layer):
    t = up.shape[0]
    seq = t // nseq
    nb = D_FF // CONV_TC

    def body(g_ref, v_ref, w_ref, b_ref, o_ref):
        pre = _conv_block(g_ref[...].astype(F32), w_ref, _row(b_ref, layer))
        o_ref[...] = _b(_silu(pre) * v_ref[...].astype(F32))

    col = pl.BlockSpec((seq, CONV_TC), lambda j, s: (s, j))
    return pl.pallas_call(
        body, name="ffn_act_fwd", grid=(nb, nseq),
        in_specs=[col,
                  pl.BlockSpec((seq, CONV_TC), lambda j, s: (s, nb + j)),
                  *_conv_specs(FF_CONV, layer)],
        out_specs=col,
        out_shape=jax.ShapeDtypeStruct((t, D_FF), BF16),
        compiler_params=_cp("parallel", "parallel"),
    )(up, up, w, b)


def _ffn_act_bwd(dact, up, w, b, *, nseq, layer):
    t = up.shape[0]
    seq = t // nseq
    nb = D_FF // CONV_TC

    def body(da_ref, g_ref, v_ref, w_ref, b_ref, dg_ref, dv_ref, dw_ref, db_ref):
        @pl.when(pl.program_id(1) == 0)
        def _():
            dw_ref[...] = jnp.zeros_like(dw_ref)
            db_ref[...] = jnp.zeros_like(db_ref)

        gate = g_ref[...].astype(F32)
        silu, dsilu = _dsilu(_conv_block(gate, w_ref, _row(b_ref, layer)))
        da = da_ref[...].astype(F32)
        dv_ref[...] = _b(da * silu)
        dg_ref[...] = _b(_conv_block_bwd(da * v_ref[...].astype(F32) * dsilu, gate, w_ref, dw_ref, db_ref))

    col = pl.BlockSpec((seq, CONV_TC), lambda j, s: (s, j))
    return pl.pallas_call(
        body, name="ffn_act_bwd", grid=(nb, nseq),
        in_specs=[col, col,
                  pl.BlockSpec((seq, CONV_TC), lambda j, s: (s, nb + j)),
                  *_conv_specs(FF_CONV, layer)],
        out_specs=[col, col,
                   pl.BlockSpec((FF_CONV, CONV_TC), lambda j, s: (0, j)),
                   pl.BlockSpec((1, CONV_TC), lambda j, s: (0, j))],
        out_shape=[jax.ShapeDtypeStruct((t, D_FF), BF16), jax.ShapeDtypeStruct((t, D_FF), BF16),
                   jax.ShapeDtypeStruct((FF_CONV, D_FF), F32), jax.ShapeDtypeStruct((1, D_FF), F32)],
        compiler_params=_cp("parallel", "arbitrary"),
    )(dact, up, up, w, b)


SSD_PAIRS = SSD_HEADS // 2
PAIR_W = 2 * SSD_HEAD_DIM
PAIRS_PER_GROUP = SSD_PAIRS // SSD_GROUPS


def _ssd_chunk(xs, bg, cg, dtr, z, hp, dtb, alog, dskip, ng):
    n = dtr.shape[0]
    dt = _softplus(dtr + dtb)
    cs = _cumsum_rows(dt * (-jnp.exp(alog)))
    cs_t = _transpose(cs)
    lane = lax.broadcasted_iota(jnp.int32, (1, SSD_HEADS), 1)
    sub = lax.broadcasted_iota(jnp.int32, (SSD_HEADS, 1), 0)
    row = lax.broadcasted_iota(jnp.int32, (n, 1), 0)
    causal = lax.broadcasted_iota(jnp.int32, (n, n), 0) >= lax.broadcasted_iota(jnp.int32, (n, n), 1)
    future = jnp.where(causal, 0.0, -1e30)
    first = lax.broadcasted_iota(jnp.int32, (1, PAIR_W), 1) < SSD_HEAD_DIM
    first_rows = lax.broadcasted_iota(jnp.int32, (PAIR_W, 1), 0) < SSD_HEAD_DIM
    first_f = first.astype(F32)
    cb = [_bdot_nt(cg[g], bg[g]) for g in range(SSD_GROUPS)]
    ys, hn = [], []
    for p in range(SSD_PAIRS):
        g = p // PAIRS_PER_GROUP
        col, decay, last = [], [], []
        for h in (2 * p, 2 * p + 1):
            oh = (lane == h).astype(F32)
            cs_h = jnp.sum(cs * oh, axis=1, keepdims=True)
            cs_row = jnp.sum(cs_t * (sub == h).astype(F32), axis=0, keepdims=True)
            col.append((jnp.sum(dt * oh, axis=1, keepdims=True), cs_h, jnp.sum(dskip * oh, axis=1, keepdims=True)))
            last.append(jnp.sum(jnp.where(row == n - 1, cs_h, 0.0), axis=0, keepdims=True))
            decay.append(jnp.exp(cs_h - cs_row + future))
        pair = lambda a, b: jnp.where(first, a, b)
        dt_p = pair(col[0][0], col[1][0])
        cs_p = pair(col[0][1], col[1][1])
        last_p = pair(last[0], last[1])
        xc = xs[p] * dt_p
        y = _bdot(cb[g] * decay[0], xc * first_f) + _bdot(cb[g] * decay[1], xc * (1.0 - first_f))
        y = y + _bdot_nt(cg[g], hp[p]) * jnp.exp(cs_p)
        y = y + pair(col[0][2], col[1][2]) * xs[p]
        keep = jnp.where(first_rows, jnp.exp(last[0]), jnp.exp(last[1]))
        hn.append(keep * hp[p] + _bdot_tn(xc * jnp.exp(last_p - cs_p), bg[g]))
        ys.append(y * _silu(z[p]))
    outs = []
    for g in range(SSD_GROUPS):
        ps = range(g * PAIRS_PER_GROUP, (g + 1) * PAIRS_PER_GROUP)
        ms = sum(jnp.sum(ys[p] * ys[p], axis=1, keepdims=True) for p in ps) * (1.0 / GROUP_WIDTH)
        r = lax.rsqrt(ms + EPS)
        outs += [ys[p] * r * ng[p] for p in ps]
    return outs, hn


def _hslices(ref, width, count, base=0, rows=slice(None)):
    return [ref[rows, base + k * width: base + (k + 1) * width] for k in range(count)]


def _ssd_load(xbc_ref, z_ref, dt_ref, ng_ref, layer):
    xs = _hslices(xbc_ref, PAIR_W, SSD_PAIRS)
    bg = _hslices(xbc_ref, D_STATE, SSD_GROUPS, D_SSD)
    cg = _hslices(xbc_ref, D_STATE, SSD_GROUPS, D_SSD + SSD_GROUPS * D_STATE)
    z = _hslices(z_ref, PAIR_W, SSD_PAIRS)
    ng = _hslices(ng_ref, PAIR_W, SSD_PAIRS, rows=slice(layer, layer + 1))
    return xs, bg, cg, dt_ref[:, 0:SSD_HEADS], z, ng


def _ssd_specs(nch):
    rowi = lambda s, c: s * nch + c
    return [pl.BlockSpec((CHUNK, CONV_DIM), lambda s, c: (rowi(s, c), 0)),
            pl.BlockSpec((CHUNK, D_SSD), lambda s, c: (rowi(s, c), COL_Z // D_SSD)),
            pl.BlockSpec((CHUNK, 128), lambda s, c: (rowi(s, c), COL_DT // 128)),
            _per_layer(SSD_HEADS), _per_layer(SSD_HEADS), _per_layer(SSD_HEADS), _per_layer(D_SSD)]


def _ssd_fwd(xbc, proj, dtb, alog, dskip, ng, *, nseq, layer):
    t = proj.shape[0]
    nch = t // nseq // CHUNK
    hd = PAIR_W

    def body(xbc_ref, z_ref, dt_ref, dtb_ref, alog_ref, dsk_ref, ng_ref, y_ref, hp_ref, h_ref):
        @pl.when(pl.program_id(1) == 0)
        def _():
            h_ref[...] = jnp.zeros_like(h_ref)

        xs, bg, cg, dtr, z, ngs = _ssd_load(xbc_ref, z_ref, dt_ref, ng_ref, layer)
        hp_ref[0] = h_ref[...]
        hp = [h_ref[h * hd:(h + 1) * hd, :] for h in range(SSD_PAIRS)]
        outs, hn = _ssd_chunk(xs, bg, cg, dtr, z, hp, _row(dtb_ref, layer), _row(alog_ref, layer), _row(dsk_ref, layer), ngs)
        for h in range(SSD_PAIRS):
            y_ref[:, h * hd:(h + 1) * hd] = _b(outs[h])
            h_ref[h * hd:(h + 1) * hd, :] = hn[h]

    return pl.pallas_call(
        body, name="ssd_fwd", grid=(nseq, nch),
        in_specs=_ssd_specs(nch),
        out_specs=[pl.BlockSpec((CHUNK, D_SSD), lambda s, c: (s * nch + c, 0)),
                   pl.BlockSpec((1, D_SSD, D_STATE), lambda s, c: (s * nch + c, 0, 0))],
        out_shape=[jax.ShapeDtypeStruct((t, D_SSD + D_GM), BF16),
                   jax.ShapeDtypeStruct((t // CHUNK, D_SSD, D_STATE), F32)],
        scratch_shapes=[pltpu.VMEM((D_SSD, D_STATE), F32)],
        compiler_params=_cp("arbitrary", "arbitrary"),
    )(xbc, proj, proj, dtb, alog, dskip, ng)


def _ssd_bwd(dy, xbc, proj, hprev, dtb, alog, dskip, ng, *, nseq, layer):
    t = proj.shape[0]
    nch = t // nseq // CHUNK
    hd = PAIR_W
    rev = lambda s, c: s * nch + (nch - 1 - c)

    def body(dy_ref, xbc_ref, z_ref, dt_ref, hp_ref, dtb_ref, alog_ref, dsk_ref, ng_ref,
             dxbc_ref, dproj_ref, ddtb_ref, dalog_ref, ddsk_ref, dng_ref, dh_ref):
        first = (pl.program_id(0) == 0) & (pl.program_id(1) == 0)

        @pl.when(pl.program_id(1) == 0)
        def _():
            dh_ref[...] = jnp.zeros_like(dh_ref)

        @pl.when(first)
        def _():
            ddtb_ref[...] = jnp.zeros_like(ddtb_ref)
            dalog_ref[...] = jnp.zeros_like(dalog_ref)
            ddsk_ref[...] = jnp.zeros_like(ddsk_ref)
            dng_ref[...] = jnp.zeros_like(dng_ref)

        xs, bg, cg, dtr, z, ngs = _ssd_load(xbc_ref, z_ref, dt_ref, ng_ref, layer)
        hp = [hp_ref[0, h * hd:(h + 1) * hd, :] for h in range(SSD_PAIRS)]
        _, vjp = jax.vjp(_ssd_chunk, xs, bg, cg, dtr, z, hp, _row(dtb_ref, layer), _row(alog_ref, layer), _row(dsk_ref, layer), ngs)
        douts = [dy_ref[:, h * hd:(h + 1) * hd] for h in range(SSD_PAIRS)]
        dhn = [dh_ref[h * hd:(h + 1) * hd, :] for h in range(SSD_PAIRS)]
        dxs, dbg, dcg, ddtr, dz, dhp, ddtb, dalog, ddsk, dngs = vjp((douts, dhn))
        dproj_ref[:, :COL_Z] = jnp.zeros((CHUNK, COL_Z), BF16)
        dproj_ref[:, COL_XBC:] = jnp.zeros((CHUNK, N_INP - COL_XBC), BF16)
        for h in range(SSD_PAIRS):
            dxbc_ref[:, h * hd:(h + 1) * hd] = dxs[h]
            dproj_ref[:, COL_Z + h * hd: COL_Z + (h + 1) * hd] = _b(dz[h])
            dh_ref[h * hd:(h + 1) * hd, :] = dhp[h]
            dng_ref[:, h * hd:(h + 1) * hd] += dngs[h]
        for g in range(SSD_GROUPS):
            dxbc_ref[:, D_SSD + g * D_STATE: D_SSD + (g + 1) * D_STATE] = dbg[g]
            dxbc_ref[:, D_SSD + (SSD_GROUPS + g) * D_STATE: D_SSD + (SSD_GROUPS + g + 1) * D_STATE] = dcg[g]
        dproj_ref[:, COL_DT:COL_DT + SSD_HEADS] = _b(ddtr)
        ddtb_ref[...] += ddtb
        dalog_ref[...] += dalog
        ddsk_ref[...] += ddsk

    small = pl.BlockSpec((1, SSD_HEADS), lambda s, c: (0, 0))
    return pl.pallas_call(
        body, name="ssd_bwd", grid=(nseq, nch),
        in_specs=[pl.BlockSpec((CHUNK, D_SSD), lambda s, c: (rev(s, c), 0)),
                  pl.BlockSpec((CHUNK, CONV_DIM), lambda s, c: (rev(s, c), 0)),
                  pl.BlockSpec((CHUNK, D_SSD), lambda s, c: (rev(s, c), COL_Z // D_SSD)),
                  pl.BlockSpec((CHUNK, 128), lambda s, c: (rev(s, c), COL_DT // 128)),
                  pl.BlockSpec((1, D_SSD, D_STATE), lambda s, c: (rev(s, c), 0, 0)),
                  _per_layer(SSD_HEADS), _per_layer(SSD_HEADS), _per_layer(SSD_HEADS), _per_layer(D_SSD)],
        out_specs=[pl.BlockSpec((CHUNK, CONV_DIM), lambda s, c: (rev(s, c), 0)),
                   pl.BlockSpec((CHUNK, N_INP), lambda s, c: (rev(s, c), 0)),
                   small, small, small,
                   pl.BlockSpec((1, D_SSD), lambda s, c: (0, 0))],
        out_shape=[jax.ShapeDtypeStruct((t, CONV_DIM), F32), jax.ShapeDtypeStruct((t, N_INP), BF16),
                   jax.ShapeDtypeStruct((1, SSD_HEADS), F32), jax.ShapeDtypeStruct((1, SSD_HEADS), F32),
                   jax.ShapeDtypeStruct((1, SSD_HEADS), F32), jax.ShapeDtypeStruct((1, D_SSD), F32)],
        scratch_shapes=[pltpu.VMEM((D_SSD, D_STATE), F32)],
        compiler_params=_cp("arbitrary", "arbitrary"),
    )(dy, xbc, proj, proj, hprev, dtb, alog, dskip, ng)


def _gmlp_chunk(gu, gv, ws, bs_cols, vg, og):
    n = gu[0].shape[0]
    mask = _tri(n, True)
    au = [_gelu(t) for t in gu]
    av = [_gelu(t) for t in gv]
    r = lax.rsqrt(sum(jnp.sum(t * t, axis=1, keepdims=True) for t in av) * (1.0 / D_GM) + EPS)
    p = []
    for h in range(GM_HEADS):
        sv = _bdot(ws[h] * mask, av[h] * r * vg[h]) + bs_cols[h]
        p.append(au[h] * sv)
    r2 = lax.rsqrt(sum(jnp.sum(t * t, axis=1, keepdims=True) for t in p) * (1.0 / D_GM) + EPS)
    return [p[h] * r2 * og[h] for h in range(GM_HEADS)]


def _gmlp_load(u_ref, v_ref, ws_ref, bst_ref, vg_ref, og_ref, layer):
    gu = _hslices(u_ref, GM_HEAD_DIM, GM_HEADS)
    gv = _hslices(v_ref, GM_HEAD_DIM, GM_HEADS)
    ws = [ws_ref[h] for h in range(GM_HEADS)]
    bs_cols = [bst_ref[:, h:h + 1] for h in range(GM_HEADS)]
    mine = slice(layer, layer + 1)
    return (gu, gv, ws, bs_cols, _hslices(vg_ref, GM_HEAD_DIM, GM_HEADS, rows=mine),
            _hslices(og_ref, GM_HEAD_DIM, GM_HEADS, rows=mine))


def _gmlp_specs(layer):
    return [pl.BlockSpec((CHUNK, D_GM), lambda i: (i, COL_U // D_GM)),
            pl.BlockSpec((CHUNK, D_GM), lambda i: (i, COL_V // D_GM)),
            pl.BlockSpec((None, GM_HEADS, CHUNK, CHUNK), lambda i: (layer, 0, 0, 0)),
            pl.BlockSpec((None, CHUNK, GM_HEADS), lambda i: (layer, 0, 0)),
            _per_layer(D_GM), _per_layer(D_GM)]


def _gmlp_fwd(proj, ycat, ws, bst, vg, og, *, layer):
    t = proj.shape[0]

    def body(u_ref, v_ref, ws_ref, bst_ref, vg_ref, og_ref, ycat_ref, o_ref):
        del ycat_ref
        outs = _gmlp_chunk(*_gmlp_load(u_ref, v_ref, ws_ref, bst_ref, vg_ref, og_ref, layer))
        for h in range(GM_HEADS):
            o_ref[:, h * GM_HEAD_DIM:(h + 1) * GM_HEAD_DIM] = _b(outs[h])

    return pl.pallas_call(
        body, name="gmlp_fwd", grid=(t // CHUNK,),
        in_specs=_gmlp_specs(layer) + [ANY],
        out_specs=pl.BlockSpec((CHUNK, D_GM), lambda i: (i, D_SSD // D_GM)),
        out_shape=jax.ShapeDtypeStruct(ycat.shape, ycat.dtype),
        input_output_aliases={6: 0},
        compiler_params=_cp("parallel"),
    )(proj, proj, ws, bst, vg, og, ycat)


def _gmlp_bwd(dy, proj, ws, bst, vg, og, dproj, *, layer):
    t = proj.shape[0]
    w = GM_HEAD_DIM

    def body(dy_ref, u_ref, v_ref, ws_ref, bst_ref, vg_ref, og_ref, dproj_ref,
             dgm_ref, dws_ref, dbst_ref, dvg_ref, dog_ref):
        del dproj_ref

        @pl.when(pl.program_id(0) == 0)
        def _():
            dws_ref[...] = jnp.zeros_like(dws_ref)
            dbst_ref[...] = jnp.zeros_like(dbst_ref)
            dvg_ref[...] = jnp.zeros_like(dvg_ref)
            dog_ref[...] = jnp.zeros_like(dog_ref)

        _, vjp = jax.vjp(_gmlp_chunk, *_gmlp_load(u_ref, v_ref, ws_ref, bst_ref, vg_ref, og_ref, layer))
        dgu, dgv, dws, dbs, dvg, dog = vjp(_hslices(dy_ref, w, GM_HEADS))
        for h in range(GM_HEADS):
            dgm_ref[:, h * w:(h + 1) * w] = _b(dgu[h])
            dgm_ref[:, D_GM + h * w: D_GM + (h + 1) * w] = _b(dgv[h])
            dws_ref[h] += dws[h]
            dbst_ref[:, h:h + 1] += dbs[h]
            dvg_ref[:, h * w:(h + 1) * w] += dvg[h]
            dog_ref[:, h * w:(h + 1) * w] += dog[h]

    return pl.pallas_call(
        body, name="gmlp_bwd", grid=(t // CHUNK,),
        in_specs=[pl.BlockSpec((CHUNK, D_GM), lambda i: (i, 1))] + _gmlp_specs(layer) + [ANY],
        out_specs=[pl.BlockSpec((CHUNK, 2 * D_GM), lambda i: (i, COL_U // (2 * D_GM))),
                   pl.BlockSpec((GM_HEADS, CHUNK, CHUNK), lambda i: (0, 0, 0)),
                   pl.BlockSpec((CHUNK, GM_HEADS), lambda i: (0, 0)),
                   pl.BlockSpec((1, D_GM), lambda i: (0, 0)),
                   pl.BlockSpec((1, D_GM), lambda i: (0, 0))],
        out_shape=[jax.ShapeDtypeStruct(dproj.shape, dproj.dtype), jax.ShapeDtypeStruct((GM_HEADS, CHUNK, CHUNK), F32),
                   jax.ShapeDtypeStruct((CHUNK, GM_HEADS), F32), jax.ShapeDtypeStruct((1, D_GM), F32),
                   jax.ShapeDtypeStruct((1, D_GM), F32)],
        input_output_aliases={7: 0},
        compiler_params=_cp("arbitrary"),
    )(dy, proj, proj, ws, bst, vg, og, dproj)


def _local_step(x, target, mods, w, final_g, *, nseq, big_w, grad_sink, small_sink):
    saved = []
    x0, delta, gate = x, None, None
    h1 = _normmod_fwd(x, w["norm1_g"], mods[0][1], mods[0][0], nseq=nseq, name="norm1_fwd_0", layer=0)
    for l in range(DEPTH):
        sh1, sc1, g1, sh2, sc2, g2 = mods[l]
        w_in = big_w(l, "w_in", h1)
        proj = _matmul(h1, w_in, tb=True, name=f"mm_in_{l}")
        xbc, xbc_pre = _ssd_conv_fwd(proj, w["ssd_conv_w"], w["ssd_conv_b"], nseq=nseq, layer=l)
        ycat, hprev = _ssd_fwd(xbc, proj, w["ssd_dt_bias"], w["ssd_a_log"], w["ssd_d"], w["ssd_norm_g"], nseq=nseq,
                               layer=l)
        ycat = _gmlp_fwd(proj, ycat, w["gm_ws"], w["gm_bst"], w["gm_vnorm_g"], w["gm_out_g"], layer=l)
        w_out = big_w(l, "w_out", ycat)
        mix, x1, h2 = _matmul_normfwd(ycat, w_out, x0, g1, w["norm2_g"], sc2, sh2, nseq=nseq, name=f"mm_out_{l}",
                                      layer=l)
        ff_up = big_w(l, "ff_up", h2)
        up = _matmul(h2, ff_up, tb=True, name=f"mm_up_{l}", out_dtype=BF16)
        act = _ffn_act_fwd(up, w["ff_conv_w"], w["ff_conv_b"], nseq=nseq, layer=l)
        ff_down = big_w(l, "ff_down", act)
        sv = dict(x0=x0, xin_delta=delta, xin_gate=gate, h1=h1, proj=proj, xbc=xbc, xbc_pre=xbc_pre, hprev=hprev,
                  ycat=ycat, mix=mix, x1=x1, h2=h2, up=up, act=act,
                  w_in=w_in, w_out=w_out, ff_up=ff_up, ff_down=ff_down)
        if l + 1 < DEPTH:
            nsh1, nsc1 = mods[l + 1][0], mods[l + 1][1]
            dn, x0, h1 = _matmul_normfwd(act, ff_down, x1, g2, w["norm1_g"], nsc1, nsh1, nseq=nseq,
                                         name=f"mm_down_{l}", layer=l + 1)
        else:
            dn, loss, dx, ddelta, dgate, dfg = _matmul_loss(act, ff_down, x1, g2, final_g, target, nseq=nseq,
                                                            name=f"mm_down_{l}")
        saved.append(dict(sv, dn=dn))
        delta, gate = dn, g2

    small, dmods = [None] * DEPTH, [None] * DEPTH
    for l in reversed(range(DEPTH)):
        sv = saved[l]
        sh1, sc1, g1, sh2, sc2, g2 = mods[l]
        dg2 = dgate
        g_ff_down = _matmul(sv["act"], ddelta, ta=True, name=f"mm_down_dw_{l}", out_dtype=BF16)
        dact = _matmul(ddelta, sv["ff_down"], tb=True, name=f"mm_down_dx_{l}", out_dtype=BF16)
        dgate_ff, dval_ff, dfcw, dfcb = _ffn_act_bwd(dact, sv["up"], w["ff_conv_w"], w["ff_conv_b"], nseq=nseq, layer=l)
        g_ff_up = _matmul([dgate_ff, dval_ff], sv["h2"], ta=True, name=f"mm_up_dw_{l}", out_dtype=BF16)
        dep = grad_sink(l, "ffn", dict(ff_down=g_ff_down, ff_up=g_ff_up), dval_ff)
        dx, dmix, dg1, dn2g, dsc2, dsh2 = _matmul_normbwd([dgate_ff, dval_ff], sv["ff_up"], dx, sv["x1"], sv["mix"], g1,
                                                          w["norm2_g"], sc2, nseq=nseq, name=f"mm_up_dx_{l}", layer=l,
                                                          dep=dep)
        g_w_out = _matmul(sv["ycat"], dmix, ta=True, name=f"mm_out_dw_{l}", out_dtype=BF16)
        dep = grad_sink(l, "w_out", dict(w_out=g_w_out), dmix)
        dycat = _matmul(dmix, sv["w_out"], tb=True, name=f"mm_out_dx_{l}", dep=dep)
        dxbc_act, dproj, ddtb, dalog, ddsk, dng = _ssd_bwd(dycat, sv["xbc"], sv["proj"], sv["hprev"], w["ssd_dt_bias"],
                                                          w["ssd_a_log"], w["ssd_d"], w["ssd_norm_g"], nseq=nseq, layer=l)
        dproj, dscw, dscb = _ssd_conv_bwd(dxbc_act, sv["xbc_pre"], sv["proj"], w["ssd_conv_w"], dproj, nseq=nseq,
                                          layer=l)
        dproj, dws, dbst, dvg, dog = _gmlp_bwd(dycat, sv["proj"], w["gm_ws"], w["gm_bst"], w["gm_vnorm_g"], w["gm_out_g"],
                                               dproj, layer=l)
        early = dict(norm2_g=dn2g, ssd_norm_g=dng, gm_vnorm_g=dvg, gm_out_g=dog,
                     ssd_conv_w=dscw, ssd_conv_b=dscb, ff_conv_w=dfcw, ff_conv_b=dfcb,
                     ssd_dt_bias=ddtb, ssd_a_log=dalog, ssd_d=ddsk, gm_ws=dws, gm_bs=dbst.T)
        dep = small_sink(l, early, small, dmods, dfg, loss)
        g_w_in = _matmul(dproj, sv["h1"], ta=True, name=f"mm_in_dw_{l}", out_dtype=BF16, dep=dep)
        dep = grad_sink(l, "w_in", dict(w_in=g_w_in), dproj)
        dx, ddelta, dgate, dn1g, dsc1, dsh1 = _matmul_normbwd(dproj, sv["w_in"], dx, sv["x0"], sv["xin_delta"],
                                                              sv["xin_gate"], w["norm1_g"], sc1, nseq=nseq,
                                                              name=f"mm_in_dx_{l}", layer=l, dep=dep)
        small[l] = dict(early, norm1_g=dn1g)
        dmods[l] = jnp.concatenate([dsh1, dsc1, dg1, dsh2, dsc2, dg2], axis=-1)[:, 0, :]
    return dx, small, dmods


def _all_gather(arrs, name, dep=None):
    n = len(arrs)
    extra = [] if dep is None else [dep]

    def body(*refs):
        ins, outs = refs[:n], refs[n + len(extra):2 * n + len(extra)]
        send_sems, recv_sems, local_sems = refs[2 * n + len(extra):]
        x, y, c = lax.axis_index("x"), lax.axis_index("y"), lax.axis_index("c")
        me, sibling = (x, y, c), (x, y, 1 - c)
        chips = [(1 - x, y), (x, 1 - y), (1 - x, 1 - y)]

        def copy(i, k, block, to, src=None):
            px, py, pc = block
            dst = outs[i].at[4 * px + 2 * py + pc]
            return pltpu.make_async_remote_copy(
                src_ref=dst if src is None else src, dst_ref=dst,
                send_sem=send_sems.at[7 * i + k], recv_sem=recv_sems.at[7 * i + k],
                device_id=to, device_id_type=MESH)

        mine = [pltpu.make_async_copy(ins[i], outs[i].at[4 * x + 2 * y + c], local_sems.at[i]) for i in range(n)]
        for cp in mine:
            cp.start()
        first = []
        for i in range(n):
            first.append(copy(i, 0, me, sibling, src=ins[i]))
            first += [copy(i, 1 + j, me, (*chip, c), src=ins[i]) for j, chip in enumerate(chips)]
        for cp in first:
            cp.start()
        passed = []
        for j, chip in enumerate(chips):
            for i in range(n):
                copy(i, 1 + j, (*chip, c), me).wait_recv()
                fwd = copy(i, 4 + j, (*chip, c), sibling)
                fwd.start()
                passed.append(fwd)
        for i in range(n):
            copy(i, 0, sibling, me).wait_recv()
            for j, chip in enumerate(chips):
                copy(i, 4 + j, (*chip, 1 - c), me).wait_recv()
        for cp in first + passed:
            cp.wait_send()
        for cp in mine:
            cp.wait()

    return pl.pallas_call(
        body, name=name,
        in_specs=[ANY] * (n + len(extra)), out_specs=[ANY] * n,
        out_shape=[jax.ShapeDtypeStruct((N_DEV,) + a.shape, a.dtype) for a in arrs],
        scratch_shapes=[pltpu.SemaphoreType.DMA((7 * n,)), pltpu.SemaphoreType.DMA((7 * n,)),
                        pltpu.SemaphoreType.DMA((n,))],
    )(*arrs, *extra)


HBM = pl.BlockSpec(memory_space=pltpu.HBM)
SEM = pl.BlockSpec(memory_space=pltpu.SEMAPHORE)
EFFECT = pltpu.SideEffectType.DATAFLOW_SIDE_EFFECTING


def _peer(k):
    x, y, c = lax.axis_index("x"), lax.axis_index("y"), lax.axis_index("c")
    return (1 - x if k & 4 else x, 1 - y if k & 2 else y, 1 - c if k & 1 else c)


ALL_PEERS = tuple(range(1, N_DEV))
OTHER_CHIPS = (2, 4, 6)


def _xc_copies(scatter, srcs, lands, send_sems, recv_sems, peers=ALL_PEERS):
    x, y, c = lax.axis_index("x"), lax.axis_index("y"), lax.axis_index("c")
    copies = []
    for i in range(len(srcs)):
        for k in (peers[i] if isinstance(peers[0], tuple) else peers):
            px, py, pc = _peer(k)
            src = srcs[i].at[4 * px + 2 * py + pc] if scatter else srcs[i]
            dst = lands[i].at[k - 1] if scatter else lands[i].at[4 * x + 2 * y + c]
            copies.append(pltpu.make_async_remote_copy(
                src_ref=src, dst_ref=dst, send_sem=send_sems[i].at[k - 1], recv_sem=recv_sems[i].at[k - 1],
                device_id=(px, py, pc), device_id_type=MESH))
    return copies


def _xc_own(scatter, srcs, lands, send_sems):
    if scatter:
        return []
    me = 4 * lax.axis_index("x") + 2 * lax.axis_index("y") + lax.axis_index("c")
    return [pltpu.make_async_copy(srcs[i], lands[i].at[me], send_sems[i].at[N_DEV - 1]) for i in range(len(srcs))]


def _xc_start(scatter, arrs, after, name, peers=ALL_PEERS):
    n = len(arrs)
    lands = [lax.empty((N_DEV - 1,) + a.shape[1:] if scatter else (N_DEV,) + a.shape, a.dtype) for a in arrs]

    def body(*refs):
        srcs, lnd = refs[:n], refs[n:2 * n]
        send_sems, recv_sems = refs[2 * n + 1:3 * n + 1], refs[3 * n + 1:4 * n + 1]
        token = refs[6 * n + 1]
        for cp in _xc_copies(scatter, srcs, lnd, send_sems, recv_sems, peers) + _xc_own(scatter, srcs, lnd, send_sems):
            cp.start()
        token[...] = jnp.zeros_like(token)

    outs = pl.pallas_call(
        body, name=name,
        out_shape=[pltpu.SemaphoreType.DMA((N_DEV,))] * (2 * n)
        + [pltpu.HBM(a.shape, a.dtype) for a in arrs] + [pltpu.HBM(a.shape, a.dtype) for a in lands]
        + [jax.ShapeDtypeStruct((8, 128), F32)],
        in_specs=[HBM] * (2 * n) + [ANY],
        out_specs=[SEM] * (2 * n) + [HBM] * (2 * n) + [pl.BlockSpec(memory_space=pltpu.VMEM)],
        input_output_aliases={i: 2 * n + i for i in range(2 * n)},
        compiler_params=pltpu.CompilerParams(has_side_effects=EFFECT),
    )(*[pltpu.with_memory_space_constraint(a, pltpu.HBM) for a in list(arrs) + lands], after)
    return outs[:n], outs[n:2 * n], outs[2 * n:3 * n], outs[3 * n:4 * n], outs[4 * n][0, 0]


def _xc_wait(scatter, send_sems, recv_sems, srcs, lands, after, name, peers=ALL_PEERS):
    n = len(srcs)

    def body(*refs):
        s_refs, l_refs = refs[:n], refs[n:2 * n]
        ss, rs = refs[2 * n:3 * n], refs[3 * n:4 * n]
        for cp in _xc_copies(scatter, s_refs, l_refs, ss, rs, peers):
            cp.wait_send()
            cp.wait_recv()
        for cp in _xc_own(scatter, s_refs, l_refs, ss):
            cp.wait()

    outs = pl.pallas_call(
        body, name=name,
        out_shape=[pltpu.HBM(a.shape, a.dtype) for a in list(srcs) + list(lands)],
        in_specs=[HBM] * (2 * n) + [SEM] * (2 * n) + [ANY],
        out_specs=[HBM] * (2 * n),
        input_output_aliases={i: i for i in range(2 * n)},
        compiler_params=pltpu.CompilerParams(has_side_effects=EFFECT),
    )(*srcs, *lands, *send_sems, *recv_sems, after)
    return outs[:n], outs[n:]


def _sib_copies(zones, send_sems, recv_sems):
    x, y, c = lax.axis_index("x"), lax.axis_index("y"), lax.axis_index("c")
    copies = []
    for i in range(len(zones)):
        for q in range(N_DEV // 2):
            slot = zones[i].at[2 * q + c]
            copies.append(pltpu.make_async_remote_copy(
                src_ref=slot, dst_ref=slot, send_sem=send_sems[i].at[q], recv_sem=recv_sems[i].at[q],
                device_id=(x, y, 1 - c), device_id_type=MESH))
    return copies


def _sib_start(zones, name):
    n = len(zones)

    def body(*refs):
        for cp in _sib_copies(refs[:n], refs[n:2 * n], refs[2 * n:3 * n]):
            cp.start()

    outs = pl.pallas_call(
        body, name=name,
        out_shape=[pltpu.SemaphoreType.DMA((N_DEV // 2,))] * (2 * n) + [pltpu.HBM(a.shape, a.dtype) for a in zones],
        in_specs=[HBM] * n,
        out_specs=[SEM] * (2 * n) + [HBM] * n,
        input_output_aliases={i: 2 * n + i for i in range(n)},
        compiler_params=pltpu.CompilerParams(has_side_effects=EFFECT),
    )(*[pltpu.with_memory_space_constraint(a, pltpu.HBM) for a in zones])
    return outs[:n], outs[n:2 * n], outs[2 * n:]


def _sib_wait(send_sems, recv_sems, zones, name):
    n = len(zones)

    def body(*refs):
        for cp in _sib_copies(refs[:n], refs[n:2 * n], refs[2 * n:3 * n]):
            cp.wait_send()
            cp.wait_recv()

    return pl.pallas_call(
        body, name=name,
        out_shape=[pltpu.HBM(a.shape, a.dtype) for a in zones],
        in_specs=[HBM] * n + [SEM] * (2 * n),
        out_specs=[HBM] * n,
        input_output_aliases={i: i for i in range(n)},
        compiler_params=pltpu.CompilerParams(has_side_effects=EFFECT),
    )(*zones, *send_sems, *recv_sems)


def _adamw_math(w, g, m, v):
    m = ADAM_B1 * m + (1.0 - ADAM_B1) * g
    v = ADAM_B2 * v + (1.0 - ADAM_B2) * (g * g)
    m_hat = m / (1.0 - ADAM_B1 ** ADAM_STEP)
    v_hat = v / (1.0 - ADAM_B2 ** ADAM_STEP)
    delta = -ADAM_LR * (m_hat / (jnp.sqrt(v_hat) + ADAM_EPS) + ADAM_WD * w)
    return delta, m, v


def _adamw_sharded(parts, w, m, v, pos, name):
    depth, rows, cols = w.shape
    tr = _tile(rows, 256) if rows % 8 == 0 else rows
    npart = len(parts)

    def body(pos_ref, *refs):
        prefs = refs[:npart]
        w_ref, m_ref, v_ref, g_out, d_out, m_out, v_out = refs[npart:]
        g = prefs[0][...]
        for pr in prefs[1:]:
            g = g + pr[...]
        delta, mn, vn = _adamw_math(w_ref[...], g, m_ref[...], v_ref[...])
        g_out[...] = g
        d_out[...] = delta
        m_out[...] = mn
        v_out[...] = vn

    def part_spec(fn):
        return pl.BlockSpec((1, tr, cols), lambda l, i, p: (fn(p) * depth + l, i, 0))

    blk = pl.BlockSpec((1, tr, cols), lambda l, i, p: (l, i, 0))
    shp = jax.ShapeDtypeStruct((depth, rows, cols), F32)
    return pl.pallas_call(
        body, name=name,
        grid_spec=pltpu.PrefetchScalarGridSpec(
            num_scalar_prefetch=1, grid=(depth, rows // tr),
            in_specs=[part_spec(fn) for _, fn in parts] + [blk, blk, blk],
            out_specs=[blk, blk, blk, blk]),
        out_shape=[shp, shp, shp, shp],
        compiler_params=_cp("parallel", "parallel"),
    )(pos, *[a for a, _ in parts], w, m, v)


def _adamw_layer(parts, w, m, v, pos, layer, prev, name):
    depth, rows, cols = w.shape
    npart = len(parts)
    nprev = 0 if prev is None else 4
    if rows % 16 == 0:
        tr, tc = max(t for t in range(16, 257, 16) if rows % t == 0), cols
    else:
        tr, tc = rows, _tile(cols, 256)
    pick = (lambda i: (i, 0)) if rows % 16 == 0 else (lambda i: (0, i))

    def body(pos_ref, *refs):
        prefs = refs[:npart]
        w_ref, m_ref, v_ref = refs[npart:npart + 3]
        g_out, d_out, m_out, v_out = refs[npart + 3 + nprev:]
        g = prefs[0][...].astype(F32)
        for pr in prefs[1:]:
            g = g + pr[...].astype(F32)
        delta, mn, vn = _adamw_math(w_ref[...], g, m_ref[...], v_ref[...])
        g_out[...] = g
        d_out[...] = delta
        m_out[...] = mn
        v_out[...] = vn

    def part_spec(fn):
        return pl.BlockSpec((1, tr, tc), lambda i, p: (fn(p), *pick(i)))

    blk = pl.BlockSpec((1, tr, tc), lambda i, p: (layer, *pick(i)))
    shp = jax.ShapeDtypeStruct((depth, rows, cols), F32)
    first_prev = 1 + npart + 3
    return pl.pallas_call(
        body, name=name,
        grid_spec=pltpu.PrefetchScalarGridSpec(
            num_scalar_prefetch=1, grid=(rows // tr * (cols // tc),),
            in_specs=[part_spec(fn) for _, fn in parts] + [blk, blk, blk] + [ANY] * nprev,
            out_specs=[blk, blk, blk, blk]),
        out_shape=[shp, shp, shp, shp],
        input_output_aliases={first_prev + j: j for j in range(nprev)},
        compiler_params=_cp("parallel"),
    )(pos, *[a for a, _ in parts], w, m, v, *(prev or ()))


def _adamw_rows_major(parts_by_layer, w, m, v, pos, name):
    rows, depth, cols = w.shape
    tc = _tile(cols, 256)
    npart = len(parts_by_layer[0])

    def body(pos_ref, *refs):
        prefs = refs[:depth * npart]
        w_ref, m_ref, v_ref, g_out, d_out, m_out, v_out = refs[depth * npart:]
        for l in range(depth):
            g = prefs[l * npart][0].astype(F32)
            for pr in prefs[l * npart + 1:(l + 1) * npart]:
                g = g + pr[0].astype(F32)
            delta, mn, vn = _adamw_math(w_ref[:, l, :], g, m_ref[:, l, :], v_ref[:, l, :])
            g_out[:, l, :] = g
            d_out[:, l, :] = delta
            m_out[:, l, :] = mn
            v_out[:, l, :] = vn

    def part_spec(fn):
        return pl.BlockSpec((1, rows, tc), lambda j, p: (fn(p), 0, j))

    blk = pl.BlockSpec((rows, depth, tc), lambda j, p: (0, 0, j))
    shp = jax.ShapeDtypeStruct(w.shape, F32)
    flat = [pf for parts in parts_by_layer for pf in parts]
    return pl.pallas_call(
        body, name=name,
        grid_spec=pltpu.PrefetchScalarGridSpec(
            num_scalar_prefetch=1, grid=(cols // tc,),
            in_specs=[part_spec(fn) for _, fn in flat] + [blk, blk, blk],
            out_specs=[blk, blk, blk, blk]),
        out_shape=[shp, shp, shp, shp],
        compiler_params=_cp("parallel"),
    )(pos, *[a for a, _ in flat], w, m, v)


_P1024 = ["norm1_g", "norm2_g", "ssd_norm_g", "gm_vnorm_g", "gm_out_g"]
_P16 = ["ssd_dt_bias", "ssd_a_log", "ssd_d"]


def _adamw_small(gath, wmv):
    names = list(wmv.keys())
    classes = list(gath.keys())
    flat_in = [gath[k] for k in classes]
    for nme in names:
        flat_in += list(wmv[nme])
    out_shapes = []
    for nme in names:
        out_shapes += [jax.ShapeDtypeStruct(wmv[nme][0].shape, F32)] * 4
    out_shapes += [jax.ShapeDtypeStruct((DEPTH, SSD_CONV, CONV_DIM), F32), jax.ShapeDtypeStruct((DEPTH, FF_CONV, D_FF), F32),
                   jax.ShapeDtypeStruct((1, SSD_HEADS), F32)]
    scratch = [pltpu.VMEM(gath[k].shape[1:], F32) for k in classes]
    ncls = len(classes)

    def body(*refs):
        g_refs = dict(zip(classes, refs[:ncls]))
        pos = ncls
        w_refs = {}
        for nme in names:
            w_refs[nme] = refs[pos:pos + 3]
            pos += 3
        o_refs = {}
        for nme in names:
            o_refs[nme] = refs[pos:pos + 4]
            pos += 4
        scw_out, fcw_out, loss_out = refs[pos], refs[pos + 1], refs[pos + 2]
        s_refs = dict(zip(classes, refs[pos + 3:]))
        for k in classes:
            acc = g_refs[k][0]
            for dev in range(1, N_DEV):
                acc = acc + g_refs[k][dev]
            s_refs[k][...] = acc

        def apply(nme, grad_of):
            w_ref, m_ref, v_ref = w_refs[nme]
            g_out, d_out, m_out, v_out = o_refs[nme]
            shape = w_ref.shape
            if len(shape) == 2:
                idxs = [(slice(l, l + 1),) for l in range(shape[0])]
            elif len(shape) == 3:
                idxs = [(l,) for l in range(shape[0])]
            else:
                idxs = [(l, h) for l in range(shape[0]) for h in range(shape[1])]
            for n_i, ix in enumerate(idxs):
                g = grad_of(n_i)
                delta, mn, vn = _adamw_math(w_ref[ix], g, m_ref[ix], v_ref[ix])
                g_out[ix] = g
                d_out[ix] = delta
                m_out[ix] = mn
                v_out[ix] = vn

        s1024, s1536, s2816, s16, s128, s6144, late1024, late6144 = (s_refs[k] for k in classes)
        s1024[0:1, :] += late1024[...]
        s6144[0:late6144.shape[0], :] += late6144[...]
        for n_i, nme in enumerate(_P1024):
            apply(nme, lambda l, b=2 * n_i: s1024[b + l:b + l + 1, :])
        apply("final_g", lambda l: s1024[10:11, :])
        apply("ssd_conv_b", lambda l: s1536[8 + l:9 + l, :])
        apply("ff_conv_b", lambda l: s2816[6 + l:7 + l, :])
        for n_i, nme in enumerate(_P16):
            apply(nme, lambda l, b=2 * n_i: s16[b + l:b + l + 1, :])
        apply("gm_ws", lambda q: s128[q * CHUNK:(q + 1) * CHUNK, :])
        apply("gm_bs", lambda l: s128[2048 + 8 * l:2048 + 8 * (l + 1), :])
        apply("ada_b", lambda l: s6144[2 * l:2 * l + 1, :] + s6144[2 * l + 1:2 * l + 2, :])
        for l in range(DEPTH):
            scw_out[l] = s1536[SSD_CONV * l:SSD_CONV * (l + 1), :]
            fcw_out[l] = s2816[FF_CONV * l:FF_CONV * (l + 1), :]
        loss_out[...] = s16[2 * len(_P16):2 * len(_P16) + 1, :]

    outs = pl.pallas_call(
        body, name="adamw_small",
        out_shape=out_shapes,
        scratch_shapes=scratch,
        compiler_params=pltpu.CompilerParams(vmem_limit_bytes=VMEM_LIMIT),
    )(*flat_in)
    res = {nme: tuple(outs[4 * i:4 * i + 4]) for i, nme in enumerate(names)}
    return res, outs[-3], outs[-2], outs[-1]


_WEIGHTS = ['ada_w', 'ada_b', 'norm1_g', 'norm2_g', 'w_in', 'ssd_conv_w', 'ssd_conv_b', 'ssd_dt_bias', 'ssd_a_log',
            'ssd_d', 'ssd_norm_g', 'gm_vnorm_g', 'gm_ws', 'gm_bs', 'gm_out_g', 'w_out', 'ff_up', 'ff_conv_w',
            'ff_conv_b', 'ff_down', 'final_g']


_O_XBC, _O_DT, _O_GM = D_SSD, D_SSD + CONV_DIM, D_SSD + CONV_DIM + SSD_HEADS


_TRANSPOSED = ("w_in", "ff_up")


def _full_weight(name, g):
    if name != "w_in":
        return g.reshape(g.shape[0] * g.shape[1], g.shape[2])
    rows = g.shape[1]
    k, r = divmod(_O_GM, rows)
    at = lambda j: COL_Z + j * rows if j <= k else j * rows - _O_GM
    pieces = [(g[j], at(j)) for j in range(N_DEV) if j != k]
    pieces += [(g[k, :r], at(k)), (g[k, r:], 0), (jnp.zeros((N_INP - N_IN, g.shape[2]), g.dtype), N_IN)]
    low = jnp.array(-jnp.inf, g.dtype)
    placed = [lax.pad(p, low, [(o, N_INP - o - p.shape[0], 0), (0, 0, 0)]) for p, o in pieces]
    return functools.reduce(jnp.maximum, placed)


def _by_owner(name, grad):
    if name != "w_in":
        return grad.reshape(N_DEV, grad.shape[0] // N_DEV, grad.shape[1])
    rows = N_IN // N_DEV
    k, r = divmod(_O_GM, rows)
    at = lambda j: COL_Z + j * rows if j <= k else j * rows - _O_GM
    blocks = [grad[at(j):at(j) + rows] for j in range(N_DEV)]
    blocks[k] = jnp.concatenate([grad[at(k):at(k) + r], grad[:rows - r]], axis=0)
    return jnp.stack(blocks)


def kernel(x, c, ada_w, ada_b, norm1_g, norm2_g, w_in, ssd_conv_w, ssd_conv_b, ssd_dt_bias, ssd_a_log, ssd_d, ssd_norm_g, gm_vnorm_g, gm_ws, gm_bs, gm_out_g, w_out, ff_up, ff_conv_w, ff_conv_b, ff_down, final_g, loss_target, m_ada_w, m_ada_b, m_norm1_g, m_norm2_g, m_w_in, m_ssd_conv_w, m_ssd_conv_b, m_ssd_dt_bias, m_ssd_a_log, m_ssd_d, m_ssd_norm_g, m_gm_vnorm_g, m_gm_ws, m_gm_bs, m_gm_out_g, m_w_out, m_ff_up, m_ff_conv_w, m_ff_conv_b, m_ff_down, m_final_g, v_ada_w, v_ada_b, v_norm1_g, v_norm2_g, v_w_in, v_ssd_conv_w, v_ssd_conv_b, v_ssd_dt_bias, v_ssd_a_log, v_ssd_d, v_ssd_norm_g, v_gm_vnorm_g, v_gm_ws, v_gm_bs, v_gm_out_g, v_w_out, v_ff_up, v_ff_conv_w, v_ff_conv_b, v_ff_down, v_final_g):
    given = dict(locals())
    wts = {n: given[n] for n in _WEIGHTS}
    mom = {n: given["m_" + n] for n in _WEIGHTS}
    var = {n: given["v_" + n] for n in _WEIGHTS}
    nseq, seq, d = x.shape
    ix, iy, ic = lax.axis_index("x"), lax.axis_index("y"), lax.axis_index("c")
    me = 4 * ix + 2 * iy + ic
    me_arr = me.astype(jnp.int32).reshape(1)

    for nme, perm in (("ff_up", (0, 2, 1)), ("w_in", (2, 0, 1))):
        wts[nme], mom[nme], var[nme] = (jnp.transpose(a, perm) for a in (wts[nme], mom[nme], var[nme]))

    def shard(l, name):
        return _b(wts[name][:, l, :] if name == "w_in" else wts[name][l])

    g_scw, g_fcw, c_all = _all_gather([ssd_conv_w, ff_conv_w, c], "gather_first")
    scw_f = jnp.transpose(g_scw, (1, 2, 0, 3)).reshape(DEPTH, SSD_CONV, CONV_DIM)
    fcw_f = jnp.transpose(g_fcw, (1, 2, 0, 3)).reshape(DEPTH, FF_CONV, D_FF)
    c_all = c_all.reshape(N_DEV * nseq, d)

    n_ada = ada_w.shape[2]
    ada_b_shard = lax.dynamic_slice_in_dim(ada_b, me * n_ada, n_ada, axis=1).reshape(DEPTH, 1, n_ada)
    mod_part, c_act = _ada_fwd(c_all, ada_w, ada_b_shard)
    first_ssem, first_rsem, first_src, first_land, first_zero = _xc_start(
        False, [mod_part, shard(0, "w_in")], c_act, "ag_first_start", peers=[ALL_PEERS, OTHER_CHIPS])
    _, (mod_g,) = _xc_wait(False, first_ssem[:1], first_rsem[:1], first_src[:1], first_land[:1], c_act,
                           "mod_wait")
    mod_all = jnp.transpose(mod_g, (1, 2, 0, 3)).reshape(DEPTH, N_DEV * nseq, N_MOD * d)
    mod_mine = lax.dynamic_slice_in_dim(mod_all, me * nseq, nseq, axis=1)
    mod_k = jnp.transpose(mod_mine.reshape(DEPTH, nseq, N_MOD, 1, d), (0, 2, 1, 3, 4))
    mods = [[mod_k[l, k] for k in range(N_MOD)] for l in range(DEPTH)]

    later =[(0, "w_out"), (0, "ff_up"), (0, "ff_down"), (1, "w_in"), (1, "w_out"), (1, "ff_up"), (1, "ff_down")]
    ag_groups = {(0, "w_out"): [0], (0, "ff_up"): [1, 2], (1, "w_in"): [3, 4], (1, "ff_up"): [5, 6]}
    big_cache, ag = {}, {}

    def big_w(l, name, after):
        if (l, name) == (0, "w_in") and (l, name) not in big_cache:
            ag["ssem"], ag["rsem"], ag["src"], ag["land"], started = _xc_start(
                False, [shard(l2, n2) for l2, n2 in later], after, "ag_start")
            _, zones = _xc_wait(False, first_ssem[1:], first_rsem[1:], first_src[1:], first_land[1:],
                                jnp.full((8, 128), started, F32), "ag_first_wait", peers=OTHER_CHIPS)
            (zone,) = _sib_wait(*_sib_start(zones, "ag_first_sib_start"), "ag_first_sib_wait")
            big_cache[(l, name)] = _full_weight(name, zone)
        if (l, name) not in big_cache:
            idx = ag_groups[(l, name)]
            pick = lambda seq_: [seq_[i] for i in idx]
            _, lands = _xc_wait(False, pick(ag["ssem"]), pick(ag["rsem"]), pick(ag["src"]), pick(ag["land"]), after,
                                f"ag_wait_{l}_{name}")
            for i, land in zip(idx, lands):
                big_cache[later[i]] = _full_weight(later[i][1], land)
        return big_cache[(l, name)]

    small_w = dict(
        norm1_g=norm1_g + first_zero, norm2_g=norm2_g, ssd_conv_w=scw_f, ssd_conv_b=ssd_conv_b, ssd_dt_bias=ssd_dt_bias,
        ssd_a_log=ssd_a_log, ssd_d=ssd_d, ssd_norm_g=ssd_norm_g, gm_vnorm_g=gm_vnorm_g, gm_ws=gm_ws,
        gm_bst=jnp.transpose(gm_bs, (0, 2, 1)), gm_out_g=gm_out_g, ff_conv_w=fcw_f, ff_conv_b=ff_conv_b)

    outs = {}
    pending, win_parts = {}, {}

    def rs_finish(l, group, after):
        names, ssem, rsem, srcs, lands = pending.pop((l, group))
        srcs, lands = _xc_wait(True, ssem, rsem, srcs, lands, after, f"rs_wait_{l}_{group}")
        for nme, own, land in zip(names, srcs, lands):
            parts = [(own, lambda p: p[0])] + [(land, lambda p, k=k: k) for k in range(N_DEV - 1)]
            if nme == "w_in":
                win_parts[l] = parts
                if len(win_parts) == DEPTH:
                    outs[nme] = _adamw_rows_major([win_parts[k] for k in range(DEPTH)], wts[nme], mom[nme], var[nme],
                                                  me_arr, "adamw_w_in")
                continue
            outs[nme] = _adamw_layer(parts, wts[nme], mom[nme], var[nme], me_arr, l, outs.get(nme), f"adamw_{nme}_{l}")
        return land if names[-1] == "w_in" else outs[names[-1]][0]

    def grad_sink(l, group, grads, after):
        names = list(grads)
        ssem, rsem, srcs, lands, zero = _xc_start(True, [_by_owner(n, grads[n]) for n in names], after, f"rs_start_{l}_{group}")
        pending[(l, group)] = (names, ssem, rsem, srcs, lands)
        return zero.reshape(1, 1)

    early_gather = {}

    def small_sink(l, early, small, dmods, dfg, loss_p):
        if l > 0:
            return None
        layers = [dict(early, norm1_g=jnp.zeros((1, d), F32))] + small[1:]
        rows = lambda name: [layers[k][name] for k in range(DEPTH)]
        packed = [
            jnp.concatenate(sum([rows(n) for n in _P1024], []) + [dfg], axis=0),
            jnp.concatenate(rows("ssd_conv_w") + rows("ssd_conv_b"), axis=0),
            jnp.concatenate(rows("ff_conv_w") + rows("ff_conv_b"), axis=0),
            jnp.concatenate(sum([rows(n) for n in _P16], []) + [loss_p[:, :SSD_HEADS]], axis=0),
            jnp.concatenate([layers[k]["gm_ws"].reshape(GM_HEADS * CHUNK, CHUNK) for k in range(DEPTH)] + rows("gm_bs"), axis=0),
            jnp.concatenate([jnp.zeros((nseq, N_MOD * d), F32)] + dmods[1:], axis=0)]
        ssem, rsem, srcs, lands, zero = _xc_start(False, packed, packed[0], "small_start")
        early_gather.update(ssem=ssem, rsem=rsem, srcs=srcs, lands=lands)
        return zero.reshape(1, 1)

    grad_x, small, dmods = _local_step(
        x.reshape(nseq * seq, d), loss_target.reshape(nseq * seq, d), mods, small_w, final_g.reshape(1, d), nseq=nseq,
        big_w=big_w, grad_sink=grad_sink, small_sink=small_sink)

    done = grad_x
    for l, grp in ((1, "ffn"), (1, "w_out"), (1, "w_in"), (0, "ffn"), (0, "w_out")):
        done = rs_finish(l, grp, done)
    _, gathered = _xc_wait(False, early_gather["ssem"], early_gather["rsem"], early_gather["srcs"],
                           early_gather["lands"], done, "small_wait")
    gathered = list(gathered)
    gathered += _all_gather([small[0]["norm1_g"], dmods[0]], "gather_late", dep=gathered[0])
    gath = dict(zip(["p1024", "p1536", "p2816", "p16", "p128", "p6144", "late1024", "late6144"], gathered))

    dmod_all = jnp.concatenate([gath["late6144"].reshape(1, N_DEV * nseq, N_MOD * d),
                                jnp.transpose(gath["p6144"].reshape(N_DEV, DEPTH, nseq, N_MOD * d)[:, 1:], (1, 0, 2, 3)).reshape(
                                    DEPTH - 1, N_DEV * nseq, N_MOD * d)], axis=0)
    small_names = _P1024 + ["final_g", "ssd_conv_b", "ff_conv_b"] + _P16 + ["gm_ws", "gm_bs", "ada_b"]
    wmv = {}
    for nme in small_names:
        if nme == "final_g":
            wmv[nme] = tuple(a.reshape(1, d) for a in (wts[nme], mom[nme], var[nme]))
        else:
            wmv[nme] = (wts[nme], mom[nme], var[nme])
    small_out, scw_full, fcw_full, loss_sum = _adamw_small(gath, wmv)
    loss = loss_sum[0, 0]
    rs_finish(0, "w_in", scw_full)
    for nme in small_names:
        outs[nme] = small_out[nme]
    outs["final_g"] = tuple(a.reshape(d) for a in outs["final_g"])

    n_scw, n_fcw = ssd_conv_w.shape[2], ff_conv_w.shape[2]
    g_scw_mine = lax.dynamic_slice_in_dim(scw_full, me * n_scw, n_scw, axis=2)
    g_fcw_mine = lax.dynamic_slice_in_dim(fcw_full, me * n_fcw, n_fcw, axis=2)
    outs["ssd_conv_w"] = _adamw_sharded([(g_scw_mine, lambda p: 0)], ssd_conv_w, m_ssd_conv_w, v_ssd_conv_w, me_arr, "adamw_ssd_conv_w")
    outs["ff_conv_w"] = _adamw_sharded([(g_fcw_mine, lambda p: 0)], ff_conv_w, m_ff_conv_w, v_ff_conv_w, me_arr, "adamw_ff_conv_w")

    dmod_cols = _b(lax.dynamic_slice_in_dim(dmod_all, me * n_ada, n_ada, axis=2))
    g_ada = jnp.stack([_matmul(c_act, dmod_cols[l], ta=True, name=f"mm_ada_dw_{l}") for l in range(DEPTH)])
    outs["ada_w"] = _adamw_sharded([(g_ada, lambda p: 0)], ada_w, m_ada_w, v_ada_w, me_arr, "adamw_ada_w")

    for nme, perm in (("ff_up", (0, 2, 1)), ("w_in", (1, 2, 0))):
        outs[nme] = tuple(jnp.transpose(a, perm) for a in outs[nme])
    result = [loss, grad_x.reshape(nseq, seq, d)]
    for k in range(4):
        result += [outs[n][k] for n in _WEIGHTS]
    return tuple(result)
```

```python
import functools
import math

import jax
import jax.numpy as jnp
from jax import lax
from jax.experimental import pallas as pl
from jax.experimental.pallas import tpu as pltpu

F32 = jnp.float32
BF16 = jnp.bfloat16

N_DEV = 8
D_MODEL = 1024
DEPTH = 2
CHUNK = 128
SSD_HEADS = 16
SSD_HEAD_DIM = 64
SSD_GROUPS = 2
HEADS_PER_GROUP = SSD_HEADS // SSD_GROUPS
GROUP_WIDTH = HEADS_PER_GROUP * SSD_HEAD_DIM
D_STATE = 128
D_SSD = 1024
CONV_DIM = 1536
SSD_CONV = 4
GM_HEADS = 8
GM_HEAD_DIM = 128
D_GM = 1024
D_FF = 2816
FF_CONV = 3
N_IN = 4624
N_MOD = 6
EPS = 1e-6

N_INP = 5120
COL_U, COL_V, COL_Z, COL_XBC, COL_DT = 0, 1024, 2048, 3072, 4608

ADAM_LR = 0.001
ADAM_B1 = 0.9
ADAM_B2 = 0.999
ADAM_EPS = 1e-08
ADAM_WD = 0.01
ADAM_STEP = 10

VMEM_LIMIT = 56 * 1024 * 1024
MESH = pl.DeviceIdType.MESH
ANY = pl.BlockSpec(memory_space=pl.ANY)


def _cp(*sem):
    return pltpu.CompilerParams(dimension_semantics=sem, vmem_limit_bytes=VMEM_LIMIT)


def _tile(n, pref):
    if n <= pref or n % 128:
        return n
    best = 128
    for t in range(128, pref + 1, 128):
        if n % t == 0:
            best = t
    return best


def _per_layer(n):
    return pl.BlockSpec((DEPTH, n), lambda *_: (0, 0))


def _row(ref, layer, cols=slice(None)):
    return ref[layer:layer + 1, cols]


def _silu(x):
    return x * jax.nn.sigmoid(x)


def _gelu(x):
    return 0.5 * x * (1.0 + lax.erf(x * (1.0 / math.sqrt(2.0))))


def _softplus(x):
    return jnp.maximum(x, 0.0) + jnp.log1p(jnp.exp(-jnp.abs(x)))


def _b(x):
    return x.astype(BF16)


_NN = (((1,), (0,)), ((), ()))
_NT = (((1,), (1,)), ((), ()))
_TN = (((0,), (0,)), ((), ()))


def _dg(a, b, dn):
    return lax.dot_general(_b(a), _b(b), dn, preferred_element_type=F32)


@jax.custom_vjp
def _bdot(a, b):
    return _dg(a, b, _NN)


def _bdot_fwd(a, b):
    return _dg(a, b, _NN), (a, b)


def _bdot_bwd(res, ct):
    a, b = res
    return _dg(ct, b, _NT), _dg(a, ct, _TN)


_bdot.defvjp(_bdot_fwd, _bdot_bwd)


@jax.custom_vjp
def _bdot_nt(a, b):
    return _dg(a, b, _NT)


def _bdot_nt_fwd(a, b):
    return _dg(a, b, _NT), (a, b)


def _bdot_nt_bwd(res, ct):
    a, b = res
    return _dg(ct, b, _NN), _dg(ct, a, _TN)


_bdot_nt.defvjp(_bdot_nt_fwd, _bdot_nt_bwd)


@jax.custom_vjp
def _bdot_tn(a, b):
    return _dg(a, b, _TN)


def _bdot_tn_fwd(a, b):
    return _dg(a, b, _TN), (a, b)


def _bdot_tn_bwd(res, ct):
    a, b = res
    return _dg(b, ct, _NT), _dg(a, ct, _NN)


_bdot_tn.defvjp(_bdot_tn_fwd, _bdot_tn_bwd)


def _tri(n, lower):
    r = lax.broadcasted_iota(jnp.int32, (n, n), 0)
    c = lax.broadcasted_iota(jnp.int32, (n, n), 1)
    return ((r >= c) if lower else (r <= c)).astype(F32)


def _eye(n):
    r = lax.broadcasted_iota(jnp.int32, (n, n), 0)
    c = lax.broadcasted_iota(jnp.int32, (n, n), 1)
    return (r == c).astype(F32)


def _hdot(a, b, dn):
    return lax.dot_general(a, b, dn, precision=lax.Precision.HIGHEST, preferred_element_type=F32)


@jax.custom_vjp
def _cumsum_rows(x):
    return _hdot(_tri(x.shape[0], True), x, _NN)


def _cumsum_rows_fwd(x):
    return _cumsum_rows(x), None


def _cumsum_rows_bwd(_, ct):
    return (_hdot(_tri(ct.shape[0], False), ct, _NN),)


_cumsum_rows.defvjp(_cumsum_rows_fwd, _cumsum_rows_bwd)


@jax.custom_vjp
def _transpose(x):
    return _hdot(_eye(x.shape[1]), x, _NT)


def _transpose_fwd(x):
    return _transpose(x), None


def _transpose_bwd(_, ct):
    return (_hdot(_eye(ct.shape[1]), ct, _NT),)


_transpose.defvjp(_transpose_fwd, _transpose_bwd)


MXU_WIDTH = 256
MATMUL_TILE_CAP = 2816
MATMUL_VMEM = 44 * 1024 * 1024


def _mxu_tiles(n):
    if n <= MATMUL_TILE_CAP or n % 128:
        return [n]
    for unit in (MXU_WIDTH, 128):
        opts = [t for t in range(unit, MATMUL_TILE_CAP + 1, unit) if n % t == 0]
        if opts:
            return opts
    return [n]


def _matmul(a, b, *, ta=False, tb=False, name, dep=None, out_dtype=F32):
    pieces = list(a) if isinstance(a, (list, tuple)) else [a]
    npc = len(pieces)
    rows, width = pieces[0].shape
    assert all(p.shape == (rows, width) for p in pieces)
    if ta:
        k_dim, m_dim = rows, width * npc
    else:
        m_dim, k_dim = rows, width * npc
    if tb:
        n_dim, kb = b.shape
    else:
        kb, n_dim = b.shape
    assert kb == k_dim, (pieces[0].shape, npc, b.shape, ta, tb)
    m_unit = width if npc > 1 and ta else m_dim
    k_unit = width if npc > 1 and not ta else k_dim
    tm = _tile(m_unit, 1536)
    tn_opts, tk_opts = _mxu_tiles(n_dim), _mxu_tiles(k_unit)
    tn, tk = tn_opts.pop(), tk_opts.pop()
    while 4 * (tm * tk + tk * tn) + 8 * tm * tn > MATMUL_VMEM:
        if tn >= tk and tn_opts:
            tn = tn_opts.pop()
        else:
            tk = tk_opts.pop()
    ni, nj, nk = m_dim // tm, n_dim // tn, k_dim // tk
    per = width // (tm if ta else tk)
    dn = (((0 if ta else 1,), (1 if tb else 0,)), ((), ()))

    a_bytes, b_bytes = m_dim * k_dim, k_dim * n_dim
    m_outer = nk > 1 or a_bytes + b_bytes * ni <= b_bytes + a_bytes * nj
    if m_outer:
        ij = lambda o, n, k: (o, n)
        grid = (ni, nj, nk)
    else:
        ij = lambda o, n, k: (n, o)
        grid = (nj, ni, nk)

    use_acc = nk > 1 and out_dtype != F32

    def body(*refs):
        a_refs, b_ref = refs[:npc], refs[npc]
        o_ref = refs[-2] if use_acc else refs[-1]
        acc_ref = refs[-1]
        k = pl.program_id(2)
        i = pl.program_id(0 if m_outer else 1)
        along = i if ta else k

        def step(a_ref):
            p = lax.dot_general(a_ref[...], b_ref[...], dn, preferred_element_type=F32)
            if nk == 1:
                o_ref[...] = p.astype(out_dtype)
            else:
                @pl.when(k == 0)
                def _():
                    acc_ref[...] = p

                @pl.when((k > 0) & (k < nk - 1 if use_acc else True))
                def _():
                    acc_ref[...] += p

                if use_acc:
                    @pl.when(k == nk - 1)
                    def _():
                        o_ref[...] = (acc_ref[...] + p).astype(out_dtype)

        if npc == 1:
            step(a_refs[0])
        else:
            for pc in range(npc):
                pl.when((along >= pc * per) & (along < (pc + 1) * per))(functools.partial(step, a_refs[pc]))

    def a_map(pc, o, n, k):
        i, _ = ij(o, n, k)
        along = i if ta else k
        if npc > 1:
            along = jnp.clip(along - pc * per, 0, per - 1)
        return (k, along) if ta else (i, along)

    def b_map(o, n, k):
        _, j = ij(o, n, k)
        return (j, k) if tb else (k, j)

    extra = [] if dep is None else [dep]
    return pl.pallas_call(
        body, name=name,
        grid=grid,
        in_specs=[pl.BlockSpec((tk, tm) if ta else (tm, tk), functools.partial(a_map, pc)) for pc in range(npc)]
        + [pl.BlockSpec((tn, tk) if tb else (tk, tn), b_map)] + [ANY] * len(extra),
        out_specs=pl.BlockSpec((tm, tn), lambda o, n, k: ij(o, n, k)),
        out_shape=jax.ShapeDtypeStruct((m_dim, n_dim), out_dtype),
        scratch_shapes=[pltpu.VMEM((tm, tn), F32)] if use_acc else [],
        compiler_params=_cp("parallel", "parallel", "arbitrary"),
    )(*pieces, b, *extra)


def _ada_fwd(c_all, ada_w, ada_b_shard):
    depth, d, n = ada_w.shape
    nb = c_all.shape[0]

    def body(c_ref, w_ref, b_ref, o_ref, ca_ref):
        ca = _silu(c_ref[...])
        ca_ref[...] = _b(ca)
        o_ref[0] = _dg(ca, w_ref[0], _NN) + b_ref[0]

    return pl.pallas_call(
        body, name="ada_fwd",
        grid=(depth,),
        in_specs=[pl.BlockSpec((nb, d), lambda l: (0, 0)),
                  pl.BlockSpec((1, d, n), lambda l: (l, 0, 0)),
                  pl.BlockSpec((1, 1, n), lambda l: (l, 0, 0))],
        out_specs=[pl.BlockSpec((1, nb, n), lambda l: (l, 0, 0)),
                   pl.BlockSpec((nb, d), lambda l: (0, 0))],
        out_shape=[jax.ShapeDtypeStruct((depth, nb, n), F32), jax.ShapeDtypeStruct((nb, d), BF16)],
        compiler_params=_cp("arbitrary"),
    )(c_all, ada_w, ada_b_shard)


def _fold(acc):
    return jnp.sum(acc, axis=0, keepdims=True)


def _rinv(x):
    return lax.rsqrt(jnp.sum(x * x, axis=-1, keepdims=True) * (1.0 / D_MODEL) + EPS)


def _rms_bwd(a, xhat, rinv):
    return rinv * (a - xhat * (jnp.sum(a * xhat, axis=-1, keepdims=True) * (1.0 / D_MODEL)))


def _row_tile(seq):
    return min(seq, 256)


def _normmod_fwd(x, g, sc, sh, *, nseq, name, layer):
    t, d = x.shape
    seq = t // nseq
    tr = _row_tile(seq)
    nt = seq // tr
    row = pl.BlockSpec((tr, d), lambda s, i: (s * nt + i, 0))
    per_seq = pl.BlockSpec((1, 1, d), lambda s, i: (s, 0, 0))

    def body(x_ref, g_ref, sc_ref, sh_ref, h_ref):
        x_v = x_ref[...]
        h_ref[...] = _b(x_v * _rinv(x_v) * (_row(g_ref, layer) * (1.0 + sc_ref[0])) + sh_ref[0])

    return pl.pallas_call(
        body, name=name, grid=(nseq, nt),
        in_specs=[row, _per_layer(d), per_seq, per_seq],
        out_specs=row,
        out_shape=jax.ShapeDtypeStruct((t, d), BF16),
        compiler_params=_cp("parallel", "parallel"),
    )(x, g, sc, sh)


NORM_TM = 512


def _matmul_normbwd(a, b, dxo, x, delta, gate, g, sc, *, nseq, name, layer, dep=None):
    pieces = list(a) if isinstance(a, (list, tuple)) else [a]
    npc = len(pieces)
    t, width = pieces[0].shape
    k_dim, d = width * npc, b.shape[1]
    assert b.shape[0] == k_dim and all(p.shape == (t, width) for p in pieces)
    seq = t // nseq
    tm = min(NORM_TM, seq)
    per_seq_tiles = seq // tm
    tk = _mxu_tiles(width if npc > 1 else k_dim).pop()
    nk, per = k_dim // tk, width // tk
    has_delta = delta is not None
    extra = [] if dep is None else [dep]

    def body(*refs):
        a_refs, b_ref = refs[:npc], refs[npc]
        dxo_ref, x_ref = refs[npc + 1], refs[npc + 2]
        pos = npc + 3
        if has_delta:
            delta_ref, gate_ref = refs[pos], refs[pos + 1]
            pos += 2
        g_ref, sc_ref = refs[pos], refs[pos + 1]
        pos += 2 + len(extra)
        if has_delta:
            dx_ref, dd_ref, dgate_ref, dg_ref, dsc_ref, dsh_ref = refs[pos:pos + 6]
        else:
            dx_ref, dg_ref, dsc_ref, dsh_ref = refs[pos:pos + 4]
        acc_ref = refs[-1]
        i, k = pl.program_id(0), pl.program_id(1)

        def norm_bwd(dh_v):
            g_v, one_sc = _row(g_ref, layer), 1.0 + sc_ref[0]
            x_v = x_ref[...]
            rinv = _rinv(x_v)
            xhat = x_v * rinv
            dx = dxo_ref[...] + _rms_bwd(dh_v * (g_v * one_sc), xhat, rinv)
            dx_ref[...] = dx

            @pl.when(i == 0)
            def _():
                dg_ref[...] = jnp.zeros_like(dg_ref)

            @pl.when(i % per_seq_tiles == 0)
            def _():
                dsc_ref[...] = jnp.zeros_like(dsc_ref)
                dsh_ref[...] = jnp.zeros_like(dsh_ref)
                if has_delta:
                    dgate_ref[...] = jnp.zeros_like(dgate_ref)

            t_sum = _fold(dh_v * xhat)
            dg_ref[...] += t_sum * one_sc
            dsc_ref[0] += t_sum * g_v
            dsh_ref[0] += _fold(dh_v)
            if has_delta:
                dd_ref[...] = _b(dx * gate_ref[0])
                dgate_ref[0] += _fold(dx * delta_ref[...])

        def step(a_ref):
            p = lax.dot_general(a_ref[...], b_ref[...], _NN, preferred_element_type=F32)
            if nk == 1:
                norm_bwd(p)
            else:
                @pl.when(k == 0)
                def _():
                    acc_ref[...] = p

                @pl.when((k > 0) & (k < nk - 1))
                def _():
                    acc_ref[...] += p

                @pl.when(k == nk - 1)
                def _():
                    norm_bwd(acc_ref[...] + p)

        if npc == 1:
            step(a_refs[0])
        else:
            for pc in range(npc):
                pl.when((k >= pc * per) & (k < (pc + 1) * per))(functools.partial(step, a_refs[pc]))

    def a_map(pc, i, k):
        return (i, jnp.clip(k - pc * per, 0, per - 1) if npc > 1 else k)

    row = pl.BlockSpec((tm, d), lambda i, k: (i, 0))
    per_seq = pl.BlockSpec((1, 1, d), lambda i, k: (i // per_seq_tiles, 0, 0))
    vec = pl.BlockSpec((1, d), lambda i, k: (0, 0))
    shp = lambda *s, dt=F32: jax.ShapeDtypeStruct(s, dt)
    in_specs = [pl.BlockSpec((tm, tk), functools.partial(a_map, pc)) for pc in range(npc)]
    in_specs += [pl.BlockSpec((tk, d), lambda i, k: (k, 0)), row, row]
    operands = [*pieces, b, dxo, x]
    if has_delta:
        in_specs += [row, per_seq]
        operands += [delta, gate]
    in_specs += [_per_layer(d), per_seq] + [ANY] * len(extra)
    operands += [g, sc, *extra]
    if has_delta:
        out_specs = [row, row, per_seq, vec, per_seq, per_seq]
        out_shape = [shp(t, d), shp(t, d, dt=BF16), shp(nseq, 1, d), shp(1, d), shp(nseq, 1, d), shp(nseq, 1, d)]
    else:
        out_specs = [row, vec, per_seq, per_seq]
        out_shape = [shp(t, d), shp(1, d), shp(nseq, 1, d), shp(nseq, 1, d)]
    outs = pl.pallas_call(
        body, name=name, grid=(t // tm, nk),
        in_specs=in_specs, out_specs=out_specs, out_shape=out_shape,
        scratch_shapes=[pltpu.VMEM((tm, d), F32)],
        compiler_params=_cp("arbitrary", "arbitrary"),
    )(*operands)
    if has_delta:
        return tuple(outs)
    dx, dg, dsc, dsh = outs
    return dx, None, None, dg, dsc, dsh


def _matmul_normfwd(a, b, xin, gate, g, sc, sh, *, nseq, name, layer):
    t, k_dim = a.shape
    d = b.shape[1]
    assert b.shape[0] == k_dim and k_dim <= MATMUL_TILE_CAP
    seq = t // nseq
    tm = min(NORM_TM, seq)
    per_seq_tiles = seq // tm

    def body(a_ref, b_ref, xin_ref, gate_ref, g_ref, sc_ref, sh_ref, dl_ref, x_ref, h_ref):
        dl = lax.dot_general(a_ref[...], b_ref[...], _NN, preferred_element_type=F32)
        dl_ref[...] = dl
        x = xin_ref[...] + gate_ref[0] * dl
        x_ref[...] = x
        h_ref[...] = _b(x * _rinv(x) * (_row(g_ref, layer) * (1.0 + sc_ref[0])) + sh_ref[0])

    row = pl.BlockSpec((tm, d), lambda i: (i, 0))
    per_seq = pl.BlockSpec((1, 1, d), lambda i: (i // per_seq_tiles, 0, 0))
    return pl.pallas_call(
        body, name=name, grid=(t // tm,),
        in_specs=[pl.BlockSpec((tm, k_dim), lambda i: (i, 0)), pl.BlockSpec((k_dim, d), lambda i: (0, 0)),
                  row, per_seq, _per_layer(d), per_seq, per_seq],
        out_specs=[row, row, row],
        out_shape=[jax.ShapeDtypeStruct((t, d), F32), jax.ShapeDtypeStruct((t, d), F32), jax.ShapeDtypeStruct((t, d), BF16)],
        compiler_params=_cp("parallel"),
    )(a, b, xin, gate, g, sc, sh)


def _matmul_loss(a, b, xin, gate, fg, target, *, nseq, name):
    t, k_dim = a.shape
    d = b.shape[1]
    assert b.shape[0] == k_dim and k_dim <= MATMUL_TILE_CAP
    seq = t // nseq
    tm = min(NORM_TM, seq)
    per_seq_tiles = seq // tm

    def body(a_ref, b_ref, xin_ref, gate_ref, fg_ref, tgt_ref, dl_ref, loss_ref, dx_ref, dd_ref, dgate_ref, dfg_ref):
        i = pl.program_id(0)
        fg_v, gate_v = fg_ref[...], gate_ref[0]
        dl = lax.dot_general(a_ref[...], b_ref[...], _NN, preferred_element_type=F32)
        dl_ref[...] = dl
        x = xin_ref[...] + gate_v * dl
        rinv = _rinv(x)
        xhat = x * rinv
        err = xhat * fg_v - tgt_ref[...]
        dx = _rms_bwd(err * fg_v * (1.0 / d), xhat, rinv)
        dx_ref[...] = dx
        dd_ref[...] = _b(dx * gate_v)

        @pl.when(i == 0)
        def _():
            loss_ref[...] = jnp.zeros_like(loss_ref)
            dfg_ref[...] = jnp.zeros_like(dfg_ref)

        @pl.when(i % per_seq_tiles == 0)
        def _():
            dgate_ref[...] = jnp.zeros_like(dgate_ref)

        loss_ref[...] += jnp.sum(err * err) * (0.5 / d)
        dfg_ref[...] += _fold(err * xhat) * (1.0 / d)
        dgate_ref[0] += _fold(dx * dl)

    row = pl.BlockSpec((tm, d), lambda i: (i, 0))
    per_seq = pl.BlockSpec((1, 1, d), lambda i: (i // per_seq_tiles, 0, 0))
    vec = pl.BlockSpec((1, d), lambda i: (0, 0))
    return pl.pallas_call(
        body, name=name, grid=(t // tm,),
        in_specs=[pl.BlockSpec((tm, k_dim), lambda i: (i, 0)), pl.BlockSpec((k_dim, d), lambda i: (0, 0)),
                  row, per_seq, vec, row],
        out_specs=[row, pl.BlockSpec((1, 128), lambda i: (0, 0)), row, row, per_seq, vec],
        out_shape=[jax.ShapeDtypeStruct((t, d), F32), jax.ShapeDtypeStruct((1, 128), F32), jax.ShapeDtypeStruct((t, d), F32),
                   jax.ShapeDtypeStruct((t, d), BF16), jax.ShapeDtypeStruct((nseq, 1, d), F32),
                   jax.ShapeDtypeStruct((1, d), F32)],
        compiler_params=_cp("arbitrary"),
    )(a, b, xin, gate, fg, target)


CONV_TC = 256
CONV_LANES = 128
CONV_ROWS = 64
CONV_HALO = 8


def _conv_slabs(seq, fn):
    def step(i, carry):
        r0 = pl.multiple_of(i * CONV_ROWS, CONV_ROWS)
        for h in range(CONV_TC // CONV_LANES):
            fn(r0, slice(h * CONV_LANES, (h + 1) * CONV_LANES))
        return carry

    lax.fori_loop(0, seq // CONV_ROWS, step, 0)


def _slab(ref, r0, cols, seq):
    after = ref[pl.ds(pl.multiple_of(jnp.minimum(r0 + CONV_ROWS, seq - CONV_HALO), CONV_HALO), CONV_HALO), cols]
    return jnp.concatenate([ref[pl.ds(r0, CONV_ROWS), cols], jnp.where(r0 + CONV_ROWS < seq, after, 0.0)], axis=0)


def _conv_block(x, w_ref, b):
    kw = w_ref.shape[0]
    rows = lax.broadcasted_iota(jnp.int32, x.shape, 0)
    y = b + w_ref[kw - 1:kw, :] * x
    for j in range(1, kw):
        y = y + w_ref[kw - 1 - j:kw - j, :] * jnp.where(rows >= j, pltpu.roll(x, j, 0), 0.0)
    return y


def _conv_block_bwd(dy, x, w_ref, dw_ref, db_ref):
    kw = w_ref.shape[0]
    n = x.shape[0]
    rows = lax.broadcasted_iota(jnp.int32, x.shape, 0)
    dx = w_ref[kw - 1:kw, :] * dy
    dw_ref[kw - 1:kw, :] += jnp.sum(dy * x, axis=0, keepdims=True)
    for j in range(1, kw):
        dy_j = jnp.where(rows < n - j, pltpu.roll(dy, n - j, 0), 0.0)
        dx = dx + w_ref[kw - 1 - j:kw - j, :] * dy_j
        dw_ref[kw - 1 - j:kw - j, :] += jnp.sum(dy_j * x, axis=0, keepdims=True)
    db_ref[...] += jnp.sum(dy, axis=0, keepdims=True)
    return dx


def _conv_bwd(dy_ext, x, w_ref, dw_ref, db_ref, cols):
    kw = w_ref.shape[0]
    n = dy_ext.shape[0]
    dy = dy_ext[:CONV_ROWS]
    dx = w_ref[kw - 1:kw, cols] * dy
    dw_ref[kw - 1:kw, cols] += jnp.sum(dy * x, axis=0, keepdims=True)
    for j in range(1, kw):
        dy_j = pltpu.roll(dy_ext, n - j, 0)[:CONV_ROWS]
        dx = dx + w_ref[kw - 1 - j:kw - j, cols] * dy_j
        dw_ref[kw - 1 - j:kw - j, cols] += jnp.sum(dy_j * x, axis=0, keepdims=True)
    db_ref[:, cols] += jnp.sum(dy, axis=0, keepdims=True)
    return dx


def _dsilu(pre):
    sg = jax.nn.sigmoid(pre)
    return pre * sg, sg * (1.0 + pre * (1.0 - sg))


def _conv_specs(kw, layer):
    return [pl.BlockSpec((None, kw, CONV_TC), lambda j, s: (layer, 0, j)),
            pl.BlockSpec((DEPTH, CONV_TC), lambda j, s: (0, j))]


def _ssd_conv_fwd(proj, w, b, *, nseq, layer):
    t = proj.shape[0]
    seq = t // nseq
    nb = CONV_DIM // CONV_TC
    off = COL_XBC // CONV_TC

    def body(x_ref, w_ref, b_ref, o_ref, pre_ref):
        pre = _conv_block(x_ref[...], w_ref, _row(b_ref, layer))
        pre_ref[...] = pre
        o_ref[...] = _silu(pre)

    col = pl.BlockSpec((seq, CONV_TC), lambda j, s: (s, j))
    return pl.pallas_call(
        body, name="ssd_conv_fwd", grid=(nb, nseq),
        in_specs=[pl.BlockSpec((seq, CONV_TC), lambda j, s: (s, off + j)), *_conv_specs(SSD_CONV, layer)],
        out_specs=[col, col],
        out_shape=[jax.ShapeDtypeStruct((t, CONV_DIM), F32)] * 2,
        compiler_params=_cp("parallel", "parallel"),
    )(proj, w, b)


def _ssd_conv_bwd(dact, pre, proj, w, dproj, *, nseq, layer):
    t = proj.shape[0]
    seq = t // nseq
    nb = CONV_DIM // CONV_TC
    off = COL_XBC // CONV_TC

    def body(da_ref, pre_ref, x_ref, w_ref, dproj_ref, dx_ref, dw_ref, db_ref):
        del dproj_ref

        @pl.when(pl.program_id(1) == 0)
        def _():
            dw_ref[...] = jnp.zeros_like(dw_ref)
            db_ref[...] = jnp.zeros_like(db_ref)

        def slab(r0, cols):
            _, dsilu = _dsilu(_slab(pre_ref, r0, cols, seq))
            dpre_ext = _slab(da_ref, r0, cols, seq) * dsilu
            x = x_ref[pl.ds(r0, CONV_ROWS), cols]
            dx_ref[pl.ds(r0, CONV_ROWS), cols] = _b(_conv_bwd(dpre_ext, x, w_ref, dw_ref, db_ref, cols))

        _conv_slabs(seq, slab)

    return pl.pallas_call(
        body, name="ssd_conv_bwd", grid=(nb, nseq),
        in_specs=[pl.BlockSpec((seq, CONV_TC), lambda j, s: (s, j)),
                  pl.BlockSpec((seq, CONV_TC), lambda j, s: (s, j)),
                  pl.BlockSpec((seq, CONV_TC), lambda j, s: (s, off + j)),
                  _conv_specs(SSD_CONV, layer)[0],
                  ANY],
        out_specs=[pl.BlockSpec((seq, CONV_TC), lambda j, s: (s, off + j)),
                   pl.BlockSpec((SSD_CONV, CONV_TC), lambda j, s: (0, j)),
                   pl.BlockSpec((1, CONV_TC), lambda j, s: (0, j))],
        out_shape=[jax.ShapeDtypeStruct(dproj.shape, dproj.dtype), jax.ShapeDtypeStruct((SSD_CONV, CONV_DIM), F32),
                   jax.ShapeDtypeStruct((1, CONV_DIM), F32)],
        input_output_aliases={4: 0},
        compiler_params=_cp("parallel", "arbitrary"),
    )(dact, pre, proj, w, dproj)


def _ffn_act_fwd(up, w, b, *, nseq, layer):
    t = up.shape[0]
    seq = t // nseq
    nb = D_FF // CONV_TC

    def body(g_ref, v_ref, w_ref, b_ref, o_ref):
        pre = _conv_block(g_ref[...].astype(F32), w_ref, _row(b_ref, layer))
        o_ref[...] = _b(_silu(pre) * v_ref[...].astype(F32))

    col = pl.BlockSpec((seq, CONV_TC), lambda j, s: (s, j))
    return pl.pallas_call(
        body, name="ffn_act_fwd", grid=(nb, nseq),
        in_specs=[col,
                  pl.BlockSpec((seq, CONV_TC), lambda j, s: (s, nb + j)),
                  *_conv_specs(FF_CONV, layer)],
        out_specs=col,
        out_shape=jax.ShapeDtypeStruct((t, D_FF), BF16),
        compiler_params=_cp("parallel", "parallel"),
    )(up, up, w, b)


def _ffn_act_bwd(dact, up, w, b, *, nseq, layer):
    t = up.shape[0]
    seq = t // nseq
    nb = D_FF // CONV_TC

    def body(da_ref, g_ref, v_ref, w_ref, b_ref, dg_ref, dv_ref, dw_ref, db_ref):
        @pl.when(pl.program_id(1) == 0)
        def _():
            dw_ref[...] = jnp.zeros_like(dw_ref)
            db_ref[...] = jnp.zeros_like(db_ref)

        gate = g_ref[...].astype(F32)
        silu, dsilu = _dsilu(_conv_block(gate, w_ref, _row(b_ref, layer)))
        da = da_ref[...].astype(F32)
        dv_ref[...] = _b(da * silu)
        dg_ref[...] = _b(_conv_block_bwd(da * v_ref[...].astype(F32) * dsilu, gate, w_ref, dw_ref, db_ref))

    col = pl.BlockSpec((seq, CONV_TC), lambda j, s: (s, j))
    return pl.pallas_call(
        body, name="ffn_act_bwd", grid=(nb, nseq),
        in_specs=[col, col,
                  pl.BlockSpec((seq, CONV_TC), lambda j, s: (s, nb + j)),
                  *_conv_specs(FF_CONV, layer)],
        out_specs=[col, col,
                   pl.BlockSpec((FF_CONV, CONV_TC), lambda j, s: (0, j)),
                   pl.BlockSpec((1, CONV_TC), lambda j, s: (0, j))],
        out_shape=[jax.ShapeDtypeStruct((t, D_FF), BF16), jax.ShapeDtypeStruct((t, D_FF), BF16),
                   jax.ShapeDtypeStruct((FF_CONV, D_FF), F32), jax.ShapeDtypeStruct((1, D_FF), F32)],
        compiler_params=_cp("parallel", "arbitrary"),
    )(dact, up, up, w, b)


SSD_PAIRS = SSD_HEADS // 2
PAIR_W = 2 * SSD_HEAD_DIM
PAIRS_PER_GROUP = SSD_PAIRS // SSD_GROUPS


def _ssd_chunk(xs, bg, cg, dtr, z, hp, dtb, alog, dskip, ng):
    n = dtr.shape[0]
    dt = _softplus(dtr + dtb)
    cs = _cumsum_rows(dt * (-jnp.exp(alog)))
    cs_t = _transpose(cs)
    lane = lax.broadcasted_iota(jnp.int32, (1, SSD_HEADS), 1)
    sub = lax.broadcasted_iota(jnp.int32, (SSD_HEADS, 1), 0)
    row = lax.broadcasted_iota(jnp.int32, (n, 1), 0)
    causal = lax.broadcasted_iota(jnp.int32, (n, n), 0) >= lax.broadcasted_iota(jnp.int32, (n, n), 1)
    future = jnp.where(causal, 0.0, -1e30)
    first = lax.broadcasted_iota(jnp.int32, (1, PAIR_W), 1) < SSD_HEAD_DIM
    first_rows = lax.broadcasted_iota(jnp.int32, (PAIR_W, 1), 0) < SSD_HEAD_DIM
    first_f = first.astype(F32)
    cb = [_bdot_nt(cg[g], bg[g]) for g in range(SSD_GROUPS)]
    ys, hn = [], []
    for p in range(SSD_PAIRS):
        g = p // PAIRS_PER_GROUP
        col, decay, last = [], [], []
        for h in (2 * p, 2 * p + 1):
            oh = (lane == h).astype(F32)
            cs_h = jnp.sum(cs * oh, axis=1, keepdims=True)
            cs_row = jnp.sum(cs_t * (sub == h).astype(F32), axis=0, keepdims=True)
            col.append((jnp.sum(dt * oh, axis=1, keepdims=True), cs_h, jnp.sum(dskip * oh, axis=1, keepdims=True)))
            last.append(jnp.sum(jnp.where(row == n - 1, cs_h, 0.0), axis=0, keepdims=True))
            decay.append(jnp.exp(cs_h - cs_row + future))
        pair = lambda a, b: jnp.where(first, a, b)
        dt_p = pair(col[0][0], col[1][0])
        cs_p = pair(col[0][1], col[1][1])
        last_p = pair(last[0], last[1])
        xc = xs[p] * dt_p
        y = _bdot(cb[g] * decay[0], xc * first_f) + _bdot(cb[g] * decay[1], xc * (1.0 - first_f))
        y = y + _bdot_nt(cg[g], hp[p]) * jnp.exp(cs_p)
        y = y + pair(col[0][2], col[1][2]) * xs[p]
        keep = jnp.where(first_rows, jnp.exp(last[0]), jnp.exp(last[1]))
        hn.append(keep * hp[p] + _bdot_tn(xc * jnp.exp(last_p - cs_p), bg[g]))
        ys.append(y * _silu(z[p]))
    outs = []
    for g in range(SSD_GROUPS):
        ps = range(g * PAIRS_PER_GROUP, (g + 1) * PAIRS_PER_GROUP)
        ms = sum(jnp.sum(ys[p] * ys[p], axis=1, keepdims=True) for p in ps) * (1.0 / GROUP_WIDTH)
        r = lax.rsqrt(ms + EPS)
        outs += [ys[p] * r * ng[p] for p in ps]
    return outs, hn


def _hslices(ref, width, count, base=0, rows=slice(None)):
    return [ref[rows, base + k * width: base + (k + 1) * width] for k in range(count)]


def _ssd_load(xbc_ref, z_ref, dt_ref, ng_ref, layer):
    xs = _hslices(xbc_ref, PAIR_W, SSD_PAIRS)
    bg = _hslices(xbc_ref, D_STATE, SSD_GROUPS, D_SSD)
    cg = _hslices(xbc_ref, D_STATE, SSD_GROUPS, D_SSD + SSD_GROUPS * D_STATE)
    z = _hslices(z_ref, PAIR_W, SSD_PAIRS)
    ng = _hslices(ng_ref, PAIR_W, SSD_PAIRS, rows=slice(layer, layer + 1))
    return xs, bg, cg, dt_ref[:, 0:SSD_HEADS], z, ng


def _ssd_specs(nch):
    rowi = lambda s, c: s * nch + c
    return [pl.BlockSpec((CHUNK, CONV_DIM), lambda s, c: (rowi(s, c), 0)),
            pl.BlockSpec((CHUNK, D_SSD), lambda s, c: (rowi(s, c), COL_Z // D_SSD)),
            pl.BlockSpec((CHUNK, 128), lambda s, c: (rowi(s, c), COL_DT // 128)),
            _per_layer(SSD_HEADS), _per_layer(SSD_HEADS), _per_layer(SSD_HEADS), _per_layer(D_SSD)]


def _ssd_fwd(xbc, proj, dtb, alog, dskip, ng, *, nseq, layer):
    t = proj.shape[0]
    nch = t // nseq // CHUNK
    hd = PAIR_W

    def body(xbc_ref, z_ref, dt_ref, dtb_ref, alog_ref, dsk_ref, ng_ref, y_ref, hp_ref, h_ref):
        @pl.when(pl.program_id(1) == 0)
        def _():
            h_ref[...] = jnp.zeros_like(h_ref)

        xs, bg, cg, dtr, z, ngs = _ssd_load(xbc_ref, z_ref, dt_ref, ng_ref, layer)
        hp_ref[0] = h_ref[...]
        hp = [h_ref[h * hd:(h + 1) * hd, :] for h in range(SSD_PAIRS)]
        outs, hn = _ssd_chunk(xs, bg, cg, dtr, z, hp, _row(dtb_ref, layer), _row(alog_ref, layer), _row(dsk_ref, layer), ngs)
        for h in range(SSD_PAIRS):
            y_ref[:, h * hd:(h + 1) * hd] = _b(outs[h])
            h_ref[h * hd:(h + 1) * hd, :] = hn[h]

    return pl.pallas_call(
        body, name="ssd_fwd", grid=(nseq, nch),
        in_specs=_ssd_specs(nch),
        out_specs=[pl.BlockSpec((CHUNK, D_SSD), lambda s, c: (s * nch + c, 0)),
                   pl.BlockSpec((1, D_SSD, D_STATE), lambda s, c: (s * nch + c, 0, 0))],
        out_shape=[jax.ShapeDtypeStruct((t, D_SSD + D_GM), BF16),
                   jax.ShapeDtypeStruct((t // CHUNK, D_SSD, D_STATE), F32)],
        scratch_shapes=[pltpu.VMEM((D_SSD, D_STATE), F32)],
        compiler_params=_cp("arbitrary", "arbitrary"),
    )(xbc, proj, proj, dtb, alog, dskip, ng)


def _ssd_bwd(dy, xbc, proj, hprev, dtb, alog, dskip, ng, *, nseq, layer):
    t = proj.shape[0]
    nch = t // nseq // CHUNK
    hd = PAIR_W
    rev = lambda s, c: s * nch + (nch - 1 - c)

    def body(dy_ref, xbc_ref, z_ref, dt_ref, hp_ref, dtb_ref, alog_ref, dsk_ref, ng_ref,
             dxbc_ref, dproj_ref, ddtb_ref, dalog_ref, ddsk_ref, dng_ref, dh_ref):
        first = (pl.program_id(0) == 0) & (pl.program_id(1) == 0)

        @pl.when(pl.program_id(1) == 0)
        def _():
            dh_ref[...] = jnp.zeros_like(dh_ref)

        @pl.when(first)
        def _():
            ddtb_ref[...] = jnp.zeros_like(ddtb_ref)
            dalog_ref[...] = jnp.zeros_like(dalog_ref)
            ddsk_ref[...] = jnp.zeros_like(ddsk_ref)
            dng_ref[...] = jnp.zeros_like(dng_ref)

        xs, bg, cg, dtr, z, ngs = _ssd_load(xbc_ref, z_ref, dt_ref, ng_ref, layer)
        hp = [hp_ref[0, h * hd:(h + 1) * hd, :] for h in range(SSD_PAIRS)]
        _, vjp = jax.vjp(_ssd_chunk, xs, bg, cg, dtr, z, hp, _row(dtb_ref, layer), _row(alog_ref, layer), _row(dsk_ref, layer), ngs)
        douts = [dy_ref[:, h * hd:(h + 1) * hd] for h in range(SSD_PAIRS)]
        dhn = [dh_ref[h * hd:(h + 1) * hd, :] for h in range(SSD_PAIRS)]
        dxs, dbg, dcg, ddtr, dz, dhp, ddtb, dalog, ddsk, dngs = vjp((douts, dhn))
        dproj_ref[:, :COL_Z] = jnp.zeros((CHUNK, COL_Z), BF16)
        dproj_ref[:, COL_XBC:] = jnp.zeros((CHUNK, N_INP - COL_XBC), BF16)
        for h in range(SSD_PAIRS):
            dxbc_ref[:, h * hd:(h + 1) * hd] = dxs[h]
            dproj_ref[:, COL_Z + h * hd: COL_Z + (h + 1) * hd] = _b(dz[h])
            dh_ref[h * hd:(h + 1) * hd, :] = dhp[h]
            dng_ref[:, h * hd:(h + 1) * hd] += dngs[h]
        for g in range(SSD_GROUPS):
            dxbc_ref[:, D_SSD + g * D_STATE: D_SSD + (g + 1) * D_STATE] = dbg[g]
            dxbc_ref[:, D_SSD + (SSD_GROUPS + g) * D_STATE: D_SSD + (SSD_GROUPS + g + 1) * D_STATE] = dcg[g]
        dproj_ref[:, COL_DT:COL_DT + SSD_HEADS] = _b(ddtr)
        ddtb_ref[...] += ddtb
        dalog_ref[...] += dalog
        ddsk_ref[...] += ddsk

    small = pl.BlockSpec((1, SSD_HEADS), lambda s, c: (0, 0))
    return pl.pallas_call(
        body, name="ssd_bwd", grid=(nseq, nch),
        in_specs=[pl.BlockSpec((CHUNK, D_SSD), lambda s, c: (rev(s, c), 0)),
                  pl.BlockSpec((CHUNK, CONV_DIM), lambda s, c: (rev(s, c), 0)),
                  pl.BlockSpec((CHUNK, D_SSD), lambda s, c: (rev(s, c), COL_Z // D_SSD)),
                  pl.BlockSpec((CHUNK, 128), lambda s, c: (rev(s, c), COL_DT // 128)),
                  pl.BlockSpec((1, D_SSD, D_STATE), lambda s, c: (rev(s, c), 0, 0)),
                  _per_layer(SSD_HEADS), _per_layer(SSD_HEADS), _per_layer(SSD_HEADS), _per_layer(D_SSD)],
        out_specs=[pl.BlockSpec((CHUNK, CONV_DIM), lambda s, c: (rev(s, c), 0)),
                   pl.BlockSpec((CHUNK, N_INP), lambda s, c: (rev(s, c), 0)),
                   small, small, small,
                   pl.BlockSpec((1, D_SSD), lambda s, c: (0, 0))],
        out_shape=[jax.ShapeDtypeStruct((t, CONV_DIM), F32), jax.ShapeDtypeStruct((t, N_INP), BF16),
                   jax.ShapeDtypeStruct((1, SSD_HEADS), F32), jax.ShapeDtypeStruct((1, SSD_HEADS), F32),
                   jax.ShapeDtypeStruct((1, SSD_HEADS), F32), jax.ShapeDtypeStruct((1, D_SSD), F32)],
        scratch_shapes=[pltpu.VMEM((D_SSD, D_STATE), F32)],
        compiler_params=_cp("arbitrary", "arbitrary"),
    )(dy, xbc, proj, proj, hprev, dtb, alog, dskip, ng)


def _gmlp_chunk(gu, gv, ws, bs_cols, vg, og):
    n = gu[0].shape[0]
    mask = _tri(n, True)
    au = [_gelu(t) for t in gu]
    av = [_gelu(t) for t in gv]
    r = lax.rsqrt(sum(jnp.sum(t * t, axis=1, keepdims=True) for t in av) * (1.0 / D_GM) + EPS)
    p = []
    for h in range(GM_HEADS):
        sv = _bdot(ws[h] * mask, av[h] * r * vg[h]) + bs_cols[h]
        p.append(au[h] * sv)
    r2 = lax.rsqrt(sum(jnp.sum(t * t, axis=1, keepdims=True) for t in p) * (1.0 / D_GM) + EPS)
    return [p[h] * r2 * og[h] for h in range(GM_HEADS)]


def _gmlp_load(u_ref, v_ref, ws_ref, bst_ref, vg_ref, og_ref, layer):
    gu = _hslices(u_ref, GM_HEAD_DIM, GM_HEADS)
    gv = _hslices(v_ref, GM_HEAD_DIM, GM_HEADS)
    ws = [ws_ref[h] for h in range(GM_HEADS)]
    bs_cols = [bst_ref[:, h:h + 1] for h in range(GM_HEADS)]
    mine = slice(layer, layer + 1)
    return (gu, gv, ws, bs_cols, _hslices(vg_ref, GM_HEAD_DIM, GM_HEADS, rows=mine),
            _hslices(og_ref, GM_HEAD_DIM, GM_HEADS, rows=mine))


def _gmlp_specs(layer):
    return [pl.BlockSpec((CHUNK, D_GM), lambda i: (i, COL_U // D_GM)),
            pl.BlockSpec((CHUNK, D_GM), lambda i: (i, COL_V // D_GM)),
            pl.BlockSpec((None, GM_HEADS, CHUNK, CHUNK), lambda i: (layer, 0, 0, 0)),
            pl.BlockSpec((None, CHUNK, GM_HEADS), lambda i: (layer, 0, 0)),
            _per_layer(D_GM), _per_layer(D_GM)]


def _gmlp_fwd(proj, ycat, ws, bst, vg, og, *, layer):
    t = proj.shape[0]

    def body(u_ref, v_ref, ws_ref, bst_ref, vg_ref, og_ref, ycat_ref, o_ref):
        del ycat_ref
        outs = _gmlp_chunk(*_gmlp_load(u_ref, v_ref, ws_ref, bst_ref, vg_ref, og_ref, layer))
        for h in range(GM_HEADS):
            o_ref[:, h * GM_HEAD_DIM:(h + 1) * GM_HEAD_DIM] = _b(outs[h])

    return pl.pallas_call(
        body, name="gmlp_fwd", grid=(t // CHUNK,),
        in_specs=_gmlp_specs(layer) + [ANY],
        out_specs=pl.BlockSpec((CHUNK, D_GM), lambda i: (i, D_SSD // D_GM)),
        out_shape=jax.ShapeDtypeStruct(ycat.shape, ycat.dtype),
        input_output_aliases={6: 0},
        compiler_params=_cp("parallel"),
    )(proj, proj, ws, bst, vg, og, ycat)


def _gmlp_bwd(dy, proj, ws, bst, vg, og, dproj, *, layer):
    t = proj.shape[0]
    w = GM_HEAD_DIM

    def body(dy_ref, u_ref, v_ref, ws_ref, bst_ref, vg_ref, og_ref, dproj_ref,
             dgm_ref, dws_ref, dbst_ref, dvg_ref, dog_ref):
        del dproj_ref

        @pl.when(pl.program_id(0) == 0)
        def _():
            dws_ref[...] = jnp.zeros_like(dws_ref)
            dbst_ref[...] = jnp.zeros_like(dbst_ref)
            dvg_ref[...] = jnp.zeros_like(dvg_ref)
            dog_ref[...] = jnp.zeros_like(dog_ref)

        _, vjp = jax.vjp(_gmlp_chunk, *_gmlp_load(u_ref, v_ref, ws_ref, bst_ref, vg_ref, og_ref, layer))
        dgu, dgv, dws, dbs, dvg, dog = vjp(_hslices(dy_ref, w, GM_HEADS))
        for h in range(GM_HEADS):
            dgm_ref[:, h * w:(h + 1) * w] = _b(dgu[h])
            dgm_ref[:, D_GM + h * w: D_GM + (h + 1) * w] = _b(dgv[h])
            dws_ref[h] += dws[h]
            dbst_ref[:, h:h + 1] += dbs[h]
            dvg_ref[:, h * w:(h + 1) * w] += dvg[h]
            dog_ref[:, h * w:(h + 1) * w] += dog[h]

    return pl.pallas_call(
        body, name="gmlp_bwd", grid=(t // CHUNK,),
        in_specs=[pl.BlockSpec((CHUNK, D_GM), lambda i: (i, 1))] + _gmlp_specs(layer) + [ANY],
        out_specs=[pl.BlockSpec((CHUNK, 2 * D_GM), lambda i: (i, COL_U // (2 * D_GM))),
                   pl.BlockSpec((GM_HEADS, CHUNK, CHUNK), lambda i: (0, 0, 0)),
                   pl.BlockSpec((CHUNK, GM_HEADS), lambda i: (0, 0)),
                   pl.BlockSpec((1, D_GM), lambda i: (0, 0)),
                   pl.BlockSpec((1, D_GM), lambda i: (0, 0))],
        out_shape=[jax.ShapeDtypeStruct(dproj.shape, dproj.dtype), jax.ShapeDtypeStruct((GM_HEADS, CHUNK, CHUNK), F32),
                   jax.ShapeDtypeStruct((CHUNK, GM_HEADS), F32), jax.ShapeDtypeStruct((1, D_GM), F32),
                   jax.ShapeDtypeStruct((1, D_GM), F32)],
        input_output_aliases={7: 0},
        compiler_params=_cp("arbitrary"),
    )(dy, proj, proj, ws, bst, vg, og, dproj)


def _local_step(x, target, mods, w, final_g, *, nseq, big_w, grad_sink, small_sink):
    saved = []
    x0, delta, gate = x, None, None
    h1 = _normmod_fwd(x, w["norm1_g"], mods[0][1], mods[0][0], nseq=nseq, name="norm1_fwd_0", layer=0)
    for l in range(DEPTH):
        sh1, sc1, g1, sh2, sc2, g2 = mods[l]
        w_in = big_w(l, "w_in", h1)
        proj = _matmul(h1, w_in, tb=True, name=f"mm_in_{l}")
        xbc, xbc_pre = _ssd_conv_fwd(proj, w["ssd_conv_w"], w["ssd_conv_b"], nseq=nseq, layer=l)
        ycat, hprev = _ssd_fwd(xbc, proj, w["ssd_dt_bias"], w["ssd_a_log"], w["ssd_d"], w["ssd_norm_g"], nseq=nseq,
                               layer=l)
        ycat = _gmlp_fwd(proj, ycat, w["gm_ws"], w["gm_bst"], w["gm_vnorm_g"], w["gm_out_g"], layer=l)
        w_out = big_w(l, "w_out", ycat)
        mix, x1, h2 = _matmul_normfwd(ycat, w_out, x0, g1, w["norm2_g"], sc2, sh2, nseq=nseq, name=f"mm_out_{l}",
                                      layer=l)
        ff_up = big_w(l, "ff_up", h2)
        up = _matmul(h2, ff_up, tb=True, name=f"mm_up_{l}", out_dtype=BF16)
        act = _ffn_act_fwd(up, w["ff_conv_w"], w["ff_conv_b"], nseq=nseq, layer=l)
        ff_down = big_w(l, "ff_down", act)
        sv = dict(x0=x0, xin_delta=delta, xin_gate=gate, h1=h1, proj=proj, xbc=xbc, xbc_pre=xbc_pre, hprev=hprev,
                  ycat=ycat, mix=mix, x1=x1, h2=h2, up=up, act=act,
                  w_in=w_in, w_out=w_out, ff_up=ff_up, ff_down=ff_down)
        if l + 1 < DEPTH:
            nsh1, nsc1 = mods[l + 1][0], mods[l + 1][1]
            dn, x0, h1 = _matmul_normfwd(act, ff_down, x1, g2, w["norm1_g"], nsc1, nsh1, nseq=nseq,
                                         name=f"mm_down_{l}", layer=l + 1)
        else:
            dn, loss, dx, ddelta, dgate, dfg = _matmul_loss(act, ff_down, x1, g2, final_g, target, nseq=nseq,
                                                            name=f"mm_down_{l}")
        saved.append(dict(sv, dn=dn))
        delta, gate = dn, g2

    small, dmods = [None] * DEPTH, [None] * DEPTH
    for l in reversed(range(DEPTH)):
        sv = saved[l]
        sh1, sc1, g1, sh2, sc2, g2 = mods[l]
        dg2 = dgate
        g_ff_down = _matmul(sv["act"], ddelta, ta=True, name=f"mm_down_dw_{l}", out_dtype=BF16)
        dact = _matmul(ddelta, sv["ff_down"], tb=True, name=f"mm_down_dx_{l}", out_dtype=BF16)
        dgate_ff, dval_ff, dfcw, dfcb = _ffn_act_bwd(dact, sv["up"], w["ff_conv_w"], w["ff_conv_b"], nseq=nseq, layer=l)
        g_ff_up = _matmul([dgate_ff, dval_ff], sv["h2"], ta=True, name=f"mm_up_dw_{l}", out_dtype=BF16)
        dep = grad_sink(l, "ffn", dict(ff_down=g_ff_down, ff_up=g_ff_up), dval_ff)
        dx, dmix, dg1, dn2g, dsc2, dsh2 = _matmul_normbwd([dgate_ff, dval_ff], sv["ff_up"], dx, sv["x1"], sv["mix"], g1,
                                                          w["norm2_g"], sc2, nseq=nseq, name=f"mm_up_dx_{l}", layer=l,
                                                          dep=dep)
        g_w_out = _matmul(sv["ycat"], dmix, ta=True, name=f"mm_out_dw_{l}", out_dtype=BF16)
        dep = grad_sink(l, "w_out", dict(w_out=g_w_out), dmix)
        dycat = _matmul(dmix, sv["w_out"], tb=True, name=f"mm_out_dx_{l}", dep=dep)
        dxbc_act, dproj, ddtb, dalog, ddsk, dng = _ssd_bwd(dycat, sv["xbc"], sv["proj"], sv["hprev"], w["ssd_dt_bias"],
                                                          w["ssd_a_log"], w["ssd_d"], w["ssd_norm_g"], nseq=nseq, layer=l)
        dproj, dscw, dscb = _ssd_conv_bwd(dxbc_act, sv["xbc_pre"], sv["proj"], w["ssd_conv_w"], dproj, nseq=nseq,
                                          layer=l)
        dproj, dws, dbst, dvg, dog = _gmlp_bwd(dycat, sv["proj"], w["gm_ws"], w["gm_bst"], w["gm_vnorm_g"], w["gm_out_g"],
                                               dproj, layer=l)
        early = dict(norm2_g=dn2g, ssd_norm_g=dng, gm_vnorm_g=dvg, gm_out_g=dog,
                     ssd_conv_w=dscw, ssd_conv_b=dscb, ff_conv_w=dfcw, ff_conv_b=dfcb,
                     ssd_dt_bias=ddtb, ssd_a_log=dalog, ssd_d=ddsk, gm_ws=dws, gm_bs=dbst.T)
        dep = small_sink(l, early, small, dmods, dfg, loss)
        g_w_in = _matmul(dproj, sv["h1"], ta=True, name=f"mm_in_dw_{l}", out_dtype=BF16, dep=dep)
        dep = grad_sink(l, "w_in", dict(w_in=g_w_in), dproj)
        dx, ddelta, dgate, dn1g, dsc1, dsh1 = _matmul_normbwd(dproj, sv["w_in"], dx, sv["x0"], sv["xin_delta"],
                                                              sv["xin_gate"], w["norm1_g"], sc1, nseq=nseq,
                                                              name=f"mm_in_dx_{l}", layer=l, dep=dep)
        small[l] = dict(early, norm1_g=dn1g)
        dmods[l] = jnp.concatenate([dsh1, dsc1, dg1, dsh2, dsc2, dg2], axis=-1)[:, 0, :]
    return dx, small, dmods


def _all_gather(arrs, name, dep=None):
    n = len(arrs)
    extra = [] if dep is None else [dep]

    def body(*refs):
        ins, outs = refs[:n], refs[n + len(extra):2 * n + len(extra)]
        send_sems, recv_sems, local_sems = refs[2 * n + len(extra):]
        x, y, c = lax.axis_index("x"), lax.axis_index("y"), lax.axis_index("c")
        me, sibling = (x, y, c), (x, y, 1 - c)
        chips = [(1 - x, y), (x, 1 - y), (1 - x, 1 - y)]

        def copy(i, k, block, to, src=None):
            px, py, pc = block
            dst = outs[i].at[4 * px + 2 * py + pc]
            return pltpu.make_async_remote_copy(
                src_ref=dst if src is None else src, dst_ref=dst,
                send_sem=send_sems.at[7 * i + k], recv_sem=recv_sems.at[7 * i + k],
                device_id=to, device_id_type=MESH)

        mine = [pltpu.make_async_copy(ins[i], outs[i].at[4 * x + 2 * y + c], local_sems.at[i]) for i in range(n)]
        for cp in mine:
            cp.start()
        first = []
        for i in range(n):
            first.append(copy(i, 0, me, sibling, src=ins[i]))
            first += [copy(i, 1 + j, me, (*chip, c), src=ins[i]) for j, chip in enumerate(chips)]
        for cp in first:
            cp.start()
        passed = []
        for j, chip in enumerate(chips):
            for i in range(n):
                copy(i, 1 + j, (*chip, c), me).wait_recv()
                fwd = copy(i, 4 + j, (*chip, c), sibling)
                fwd.start()
                passed.append(fwd)
        for i in range(n):
            copy(i, 0, sibling, me).wait_recv()
            for j, chip in enumerate(chips):
                copy(i, 4 + j, (*chip, 1 - c), me).wait_recv()
        for cp in first + passed:
            cp.wait_send()
        for cp in mine:
            cp.wait()

    return pl.pallas_call(
        body, name=name,
        in_specs=[ANY] * (n + len(extra)), out_specs=[ANY] * n,
        out_shape=[jax.ShapeDtypeStruct((N_DEV,) + a.shape, a.dtype) for a in arrs],
        scratch_shapes=[pltpu.SemaphoreType.DMA((7 * n,)), pltpu.SemaphoreType.DMA((7 * n,)),
                        pltpu.SemaphoreType.DMA((n,))],
    )(*arrs, *extra)


HBM = pl.BlockSpec(memory_space=pltpu.HBM)
SEM = pl.BlockSpec(memory_space=pltpu.SEMAPHORE)
EFFECT = pltpu.SideEffectType.DATAFLOW_SIDE_EFFECTING


def _peer(k):
    x, y, c = lax.axis_index("x"), lax.axis_index("y"), lax.axis_index("c")
    return (1 - x if k & 4 else x, 1 - y if k & 2 else y, 1 - c if k & 1 else c)


ALL_PEERS = tuple(range(1, N_DEV))
OTHER_CHIPS = (2, 4, 6)


def _xc_copies(scatter, srcs, lands, send_sems, recv_sems, peers=ALL_PEERS):
    x, y, c = lax.axis_index("x"), lax.axis_index("y"), lax.axis_index("c")
    copies = []
    for i in range(len(srcs)):
        for k in (peers[i] if isinstance(peers[0], tuple) else peers):
            px, py, pc = _peer(k)
            src = srcs[i].at[4 * px + 2 * py + pc] if scatter else srcs[i]
            dst = lands[i].at[k - 1] if scatter else lands[i].at[4 * x + 2 * y + c]
            copies.append(pltpu.make_async_remote_copy(
                src_ref=src, dst_ref=dst, send_sem=send_sems[i].at[k - 1], recv_sem=recv_sems[i].at[k - 1],
                device_id=(px, py, pc), device_id_type=MESH))
    return copies


def _xc_own(scatter, srcs, lands, send_sems):
    if scatter:
        return []
    me = 4 * lax.axis_index("x") + 2 * lax.axis_index("y") + lax.axis_index("c")
    return [pltpu.make_async_copy(srcs[i], lands[i].at[me], send_sems[i].at[N_DEV - 1]) for i in range(len(srcs))]


def _xc_start(scatter, arrs, after, name, peers=ALL_PEERS):
    n = len(arrs)
    lands = [lax.empty((N_DEV - 1,) + a.shape[1:] if scatter else (N_DEV,) + a.shape, a.dtype) for a in arrs]

    def body(*refs):
        srcs, lnd = refs[:n], refs[n:2 * n]
        send_sems, recv_sems = refs[2 * n + 1:3 * n + 1], refs[3 * n + 1:4 * n + 1]
        token = refs[6 * n + 1]
        for cp in _xc_copies(scatter, srcs, lnd, send_sems, recv_sems, peers) + _xc_own(scatter, srcs, lnd, send_sems):
            cp.start()
        token[...] = jnp.zeros_like(token)

    outs = pl.pallas_call(
        body, name=name,
        out_shape=[pltpu.SemaphoreType.DMA((N_DEV,))] * (2 * n)
        + [pltpu.HBM(a.shape, a.dtype) for a in arrs] + [pltpu.HBM(a.shape, a.dtype) for a in lands]
        + [jax.ShapeDtypeStruct((8, 128), F32)],
        in_specs=[HBM] * (2 * n) + [ANY],
        out_specs=[SEM] * (2 * n) + [HBM] * (2 * n) + [pl.BlockSpec(memory_space=pltpu.VMEM)],
        input_output_aliases={i: 2 * n + i for i in range(2 * n)},
        compiler_params=pltpu.CompilerParams(has_side_effects=EFFECT),
    )(*[pltpu.with_memory_space_constraint(a, pltpu.HBM) for a in list(arrs) + lands], after)
    return outs[:n], outs[n:2 * n], outs[2 * n:3 * n], outs[3 * n:4 * n], outs[4 * n][0, 0]


def _xc_wait(scatter, send_sems, recv_sems, srcs, lands, after, name, peers=ALL_PEERS):
    n = len(srcs)

    def body(*refs):
        s_refs, l_refs = refs[:n], refs[n:2 * n]
        ss, rs = refs[2 * n:3 * n], refs[3 * n:4 * n]
        for cp in _xc_copies(scatter, s_refs, l_refs, ss, rs, peers):
            cp.wait_send()
            cp.wait_recv()
        for cp in _xc_own(scatter, s_refs, l_refs, ss):
            cp.wait()

    outs = pl.pallas_call(
        body, name=name,
        out_shape=[pltpu.HBM(a.shape, a.dtype) for a in list(srcs) + list(lands)],
        in_specs=[HBM] * (2 * n) + [SEM] * (2 * n) + [ANY],
        out_specs=[HBM] * (2 * n),
        input_output_aliases={i: i for i in range(2 * n)},
        compiler_params=pltpu.CompilerParams(has_side_effects=EFFECT),
    )(*srcs, *lands, *send_sems, *recv_sems, after)
    return outs[:n], outs[n:]


def _sib_copies(zones, send_sems, recv_sems):
    x, y, c = lax.axis_index("x"), lax.axis_index("y"), lax.axis_index("c")
    copies = []
    for i in range(len(zones)):
        for q in range(N_DEV // 2):
            slot = zones[i].at[2 * q + c]
            copies.append(pltpu.make_async_remote_copy(
                src_ref=slot, dst_ref=slot, send_sem=send_sems[i].at[q], recv_sem=recv_sems[i].at[q],
                device_id=(x, y, 1 - c), device_id_type=MESH))
    return copies


def _sib_start(zones, name):
    n = len(zones)

    def body(*refs):
        for cp in _sib_copies(refs[:n], refs[n:2 * n], refs[2 * n:3 * n]):
            cp.start()

    outs = pl.pallas_call(
        body, name=name,
        out_shape=[pltpu.SemaphoreType.DMA((N_DEV // 2,))] * (2 * n) + [pltpu.HBM(a.shape, a.dtype) for a in zones],
        in_specs=[HBM] * n,
        out_specs=[SEM] * (2 * n) + [HBM] * n,
        input_output_aliases={i: 2 * n + i for i in range(n)},
        compiler_params=pltpu.CompilerParams(has_side_effects=EFFECT),
    )(*[pltpu.with_memory_space_constraint(a, pltpu.HBM) for a in zones])
    return outs[:n], outs[n:2 * n], outs[2 * n:]


def _sib_wait(send_sems, recv_sems, zones, name):
    n = len(zones)

    def body(*refs):
        for cp in _sib_copies(refs[:n], refs[n:2 * n], refs[2 * n:3 * n]):
            cp.wait_send()
            cp.wait_recv()

    return pl.pallas_call(
        body, name=name,
        out_shape=[pltpu.HBM(a.shape, a.dtype) for a in zones],
        in_specs=[HBM] * n + [SEM] * (2 * n),
        out_specs=[HBM] * n,
        input_output_aliases={i: i for i in range(n)},
        compiler_params=pltpu.CompilerParams(has_side_effects=EFFECT),
    )(*zones, *send_sems, *recv_sems)


def _adamw_math(w, g, m, v):
    m = ADAM_B1 * m + (1.0 - ADAM_B1) * g
    v = ADAM_B2 * v + (1.0 - ADAM_B2) * (g * g)
    m_hat = m / (1.0 - ADAM_B1 ** ADAM_STEP)
    v_hat = v / (1.0 - ADAM_B2 ** ADAM_STEP)
    delta = -ADAM_LR * (m_hat / (jnp.sqrt(v_hat) + ADAM_EPS) + ADAM_WD * w)
    return delta, m, v


def _adamw_sharded(parts, w, m, v, pos, name):
    depth, rows, cols = w.shape
    tr = _tile(rows, 256) if rows % 8 == 0 else rows
    npart = len(parts)

    def body(pos_ref, *refs):
        prefs = refs[:npart]
        w_ref, m_ref, v_ref, g_out, d_out, m_out, v_out = refs[npart:]
        g = prefs[0][...]
        for pr in prefs[1:]:
            g = g + pr[...]
        delta, mn, vn = _adamw_math(w_ref[...], g, m_ref[...], v_ref[...])
        g_out[...] = g
        d_out[...] = delta
        m_out[...] = mn
        v_out[...] = vn

    def part_spec(fn):
        return pl.BlockSpec((1, tr, cols), lambda l, i, p: (fn(p) * depth + l, i, 0))

    blk = pl.BlockSpec((1, tr, cols), lambda l, i, p: (l, i, 0))
    shp = jax.ShapeDtypeStruct((depth, rows, cols), F32)
    return pl.pallas_call(
        body, name=name,
        grid_spec=pltpu.PrefetchScalarGridSpec(
            num_scalar_prefetch=1, grid=(depth, rows // tr),
            in_specs=[part_spec(fn) for _, fn in parts] + [blk, blk, blk],
            out_specs=[blk, blk, blk, blk]),
        out_shape=[shp, shp, shp, shp],
        compiler_params=_cp("parallel", "parallel"),
    )(pos, *[a for a, _ in parts], w, m, v)


def _adamw_layer(parts, w, m, v, pos, layer, prev, name):
    depth, rows, cols = w.shape
    npart = len(parts)
    nprev = 0 if prev is None else 4
    if rows % 16 == 0:
        tr, tc = max(t for t in range(16, 257, 16) if rows % t == 0), cols
    else:
        tr, tc = rows, _tile(cols, 256)
    pick = (lambda i: (i, 0)) if rows % 16 == 0 else (lambda i: (0, i))

    def body(pos_ref, *refs):
        prefs = refs[:npart]
        w_ref, m_ref, v_ref = refs[npart:npart + 3]
        g_out, d_out, m_out, v_out = refs[npart + 3 + nprev:]
        g = prefs[0][...].astype(F32)
        for pr in prefs[1:]:
            g = g + pr[...].astype(F32)
        delta, mn, vn = _adamw_math(w_ref[...], g, m_ref[...], v_ref[...])
        g_out[...] = g
        d_out[...] = delta
        m_out[...] = mn
        v_out[...] = vn

    def part_spec(fn):
        return pl.BlockSpec((1, tr, tc), lambda i, p: (fn(p), *pick(i)))

    blk = pl.BlockSpec((1, tr, tc), lambda i, p: (layer, *pick(i)))
    shp = jax.ShapeDtypeStruct((depth, rows, cols), F32)
    first_prev = 1 + npart + 3
    return pl.pallas_call(
        body, name=name,
        grid_spec=pltpu.PrefetchScalarGridSpec(
            num_scalar_prefetch=1, grid=(rows // tr * (cols // tc),),
            in_specs=[part_spec(fn) for _, fn in parts] + [blk, blk, blk] + [ANY] * nprev,
            out_specs=[blk, blk, blk, blk]),
        out_shape=[shp, shp, shp, shp],
        input_output_aliases={first_prev + j: j for j in range(nprev)},
        compiler_params=_cp("parallel"),
    )(pos, *[a for a, _ in parts], w, m, v, *(prev or ()))


def _adamw_rows_major(parts_by_layer, w, m, v, pos, name):
    rows, depth, cols = w.shape
    tc = _tile(cols, 256)
    npart = len(parts_by_layer[0])

    def body(pos_ref, *refs):
        prefs = refs[:depth * npart]
        w_ref, m_ref, v_ref, g_out, d_out, m_out, v_out = refs[depth * npart:]
        for l in range(depth):
            g = prefs[l * npart][0].astype(F32)
            for pr in prefs[l * npart + 1:(l + 1) * npart]:
                g = g + pr[0].astype(F32)
            delta, mn, vn = _adamw_math(w_ref[:, l, :], g, m_ref[:, l, :], v_ref[:, l, :])
            g_out[:, l, :] = g
            d_out[:, l, :] = delta
            m_out[:, l, :] = mn
            v_out[:, l, :] = vn

    def part_spec(fn):
        return pl.BlockSpec((1, rows, tc), lambda j, p: (fn(p), 0, j))

    blk = pl.BlockSpec((rows, depth, tc), lambda j, p: (0, 0, j))
    shp = jax.ShapeDtypeStruct(w.shape, F32)
    flat = [pf for parts in parts_by_layer for pf in parts]
    return pl.pallas_call(
        body, name=name,
        grid_spec=pltpu.PrefetchScalarGridSpec(
            num_scalar_prefetch=1, grid=(cols // tc,),
            in_specs=[part_spec(fn) for _, fn in flat] + [blk, blk, blk],
            out_specs=[blk, blk, blk, blk]),
        out_shape=[shp, shp, shp, shp],
        compiler_params=_cp("parallel"),
    )(pos, *[a for a, _ in flat], w, m, v)


_P1024 = ["norm1_g", "norm2_g", "ssd_norm_g", "gm_vnorm_g", "gm_out_g"]
_P16 = ["ssd_dt_bias", "ssd_a_log", "ssd_d"]


def _adamw_small(gath, wmv):
    names = list(wmv.keys())
    classes = list(gath.keys())
    flat_in = [gath[k] for k in classes]
    for nme in names:
        flat_in += list(wmv[nme])
    out_shapes = []
    for nme in names:
        out_shapes += [jax.ShapeDtypeStruct(wmv[nme][0].shape, F32)] * 4
    out_shapes += [jax.ShapeDtypeStruct((DEPTH, SSD_CONV, CONV_DIM), F32), jax.ShapeDtypeStruct((DEPTH, FF_CONV, D_FF), F32),
                   jax.ShapeDtypeStruct((1, SSD_HEADS), F32)]
    scratch = [pltpu.VMEM(gath[k].shape[1:], F32) for k in classes]
    ncls = len(classes)

    def body(*refs):
        g_refs = dict(zip(classes, refs[:ncls]))
        pos = ncls
        w_refs = {}
        for nme in names:
            w_refs[nme] = refs[pos:pos + 3]
            pos += 3
        o_refs = {}
        for nme in names:
            o_refs[nme] = refs[pos:pos + 4]
            pos += 4
        scw_out, fcw_out, loss_out = refs[pos], refs[pos + 1], refs[pos + 2]
        s_refs = dict(zip(classes, refs[pos + 3:]))
        for k in classes:
            acc = g_refs[k][0]
            for dev in range(1, N_DEV):
                acc = acc + g_refs[k][dev]
            s_refs[k][...] = acc

        def apply(nme, grad_of):
            w_ref, m_ref, v_ref = w_refs[nme]
            g_out, d_out, m_out, v_out = o_refs[nme]
            shape = w_ref.shape
            if len(shape) == 2:
                idxs = [(slice(l, l + 1),) for l in range(shape[0])]
            elif len(shape) == 3:
                idxs = [(l,) for l in range(shape[0])]
            else:
                idxs = [(l, h) for l in range(shape[0]) for h in range(shape[1])]
            for n_i, ix in enumerate(idxs):
                g = grad_of(n_i)
                delta, mn, vn = _adamw_math(w_ref[ix], g, m_ref[ix], v_ref[ix])
                g_out[ix] = g
                d_out[ix] = delta
                m_out[ix] = mn
                v_out[ix] = vn

        s1024, s1536, s2816, s16, s128, s6144, late1024, late6144 = (s_refs[k] for k in classes)
        s1024[0:1, :] += late1024[...]
        s6144[0:late6144.shape[0], :] += late6144[...]
        for n_i, nme in enumerate(_P1024):
            apply(nme, lambda l, b=2 * n_i: s1024[b + l:b + l + 1, :])
        apply("final_g", lambda l: s1024[10:11, :])
        apply("ssd_conv_b", lambda l: s1536[8 + l:9 + l, :])
        apply("ff_conv_b", lambda l: s2816[6 + l:7 + l, :])
        for n_i, nme in enumerate(_P16):
            apply(nme, lambda l, b=2 * n_i: s16[b + l:b + l + 1, :])
        apply("gm_ws", lambda q: s128[q * CHUNK:(q + 1) * CHUNK, :])
        apply("gm_bs", lambda l: s128[2048 + 8 * l:2048 + 8 * (l + 1), :])
        apply("ada_b", lambda l: s6144[2 * l:2 * l + 1, :] + s6144[2 * l + 1:2 * l + 2, :])
        for l in range(DEPTH):
            scw_out[l] = s1536[SSD_CONV * l:SSD_CONV * (l + 1), :]
            fcw_out[l] = s2816[FF_CONV * l:FF_CONV * (l + 1), :]
        loss_out[...] = s16[2 * len(_P16):2 * len(_P16) + 1, :]

    outs = pl.pallas_call(
        body, name="adamw_small",
        out_shape=out_shapes,
        scratch_shapes=scratch,
        compiler_params=pltpu.CompilerParams(vmem_limit_bytes=VMEM_LIMIT),
    )(*flat_in)
    res = {nme: tuple(outs[4 * i:4 * i + 4]) for i, nme in enumerate(names)}
    return res, outs[-3], outs[-2], outs[-1]


_WEIGHTS = ['ada_w', 'ada_b', 'norm1_g', 'norm2_g', 'w_in', 'ssd_conv_w', 'ssd_conv_b', 'ssd_dt_bias', 'ssd_a_log',
            'ssd_d', 'ssd_norm_g', 'gm_vnorm_g', 'gm_ws', 'gm_bs', 'gm_out_g', 'w_out', 'ff_up', 'ff_conv_w',
            'ff_conv_b', 'ff_down', 'final_g']


_O_XBC, _O_DT, _O_GM = D_SSD, D_SSD + CONV_DIM, D_SSD + CONV_DIM + SSD_HEADS


_TRANSPOSED = ("w_in", "ff_up")


def _full_weight(name, g):
    full = g.reshape(g.shape[0] * g.shape[1], g.shape[2])
    if name != "w_in":
        return full
    zpad = jnp.zeros((N_INP - N_IN, full.shape[1]), full.dtype)
    return jnp.concatenate([full[_O_GM:], full[:_O_XBC], full[_O_XBC:_O_DT], full[_O_DT:_O_GM], zpad], axis=0)


def _by_owner(name, grad):
    if name != "w_in":
        return grad.reshape(N_DEV, grad.shape[0] // N_DEV, grad.shape[1])
    rows = N_IN // N_DEV
    k, r = divmod(_O_GM, rows)
    at = lambda j: COL_Z + j * rows if j <= k else j * rows - _O_GM
    blocks = [grad[at(j):at(j) + rows] for j in range(N_DEV)]
    blocks[k] = jnp.concatenate([grad[at(k):at(k) + r], grad[:rows - r]], axis=0)
    return jnp.stack(blocks)


def kernel(x, c, ada_w, ada_b, norm1_g, norm2_g, w_in, ssd_conv_w, ssd_conv_b, ssd_dt_bias, ssd_a_log, ssd_d, ssd_norm_g, gm_vnorm_g, gm_ws, gm_bs, gm_out_g, w_out, ff_up, ff_conv_w, ff_conv_b, ff_down, final_g, loss_target, m_ada_w, m_ada_b, m_norm1_g, m_norm2_g, m_w_in, m_ssd_conv_w, m_ssd_conv_b, m_ssd_dt_bias, m_ssd_a_log, m_ssd_d, m_ssd_norm_g, m_gm_vnorm_g, m_gm_ws, m_gm_bs, m_gm_out_g, m_w_out, m_ff_up, m_ff_conv_w, m_ff_conv_b, m_ff_down, m_final_g, v_ada_w, v_ada_b, v_norm1_g, v_norm2_g, v_w_in, v_ssd_conv_w, v_ssd_conv_b, v_ssd_dt_bias, v_ssd_a_log, v_ssd_d, v_ssd_norm_g, v_gm_vnorm_g, v_gm_ws, v_gm_bs, v_gm_out_g, v_w_out, v_ff_up, v_ff_conv_w, v_ff_conv_b, v_ff_down, v_final_g):
    given = dict(locals())
    wts = {n: given[n] for n in _WEIGHTS}
    mom = {n: given["m_" + n] for n in _WEIGHTS}
    var = {n: given["v_" + n] for n in _WEIGHTS}
    nseq, seq, d = x.shape
    ix, iy, ic = lax.axis_index("x"), lax.axis_index("y"), lax.axis_index("c")
    me = 4 * ix + 2 * iy + ic
    me_arr = me.astype(jnp.int32).reshape(1)

    for nme, perm in (("ff_up", (0, 2, 1)), ("w_in", (2, 0, 1))):
        wts[nme], mom[nme], var[nme] = (jnp.transpose(a, perm) for a in (wts[nme], mom[nme], var[nme]))

    def shard(l, name):
        return _b(wts[name][:, l, :] if name == "w_in" else wts[name][l])

    g_scw, g_fcw, c_all = _all_gather([ssd_conv_w, ff_conv_w, c], "gather_first")
    scw_f = jnp.transpose(g_scw, (1, 2, 0, 3)).reshape(DEPTH, SSD_CONV, CONV_DIM)
    fcw_f = jnp.transpose(g_fcw, (1, 2, 0, 3)).reshape(DEPTH, FF_CONV, D_FF)
    c_all = c_all.reshape(N_DEV * nseq, d)

    n_ada = ada_w.shape[2]
    ada_b_shard = lax.dynamic_slice_in_dim(ada_b, me * n_ada, n_ada, axis=1).reshape(DEPTH, 1, n_ada)
    mod_part, c_act = _ada_fwd(c_all, ada_w, ada_b_shard)
    first_ssem, first_rsem, first_src, first_land, first_zero = _xc_start(
        False, [mod_part, shard(0, "w_in")], c_act, "ag_first_start", peers=[ALL_PEERS, OTHER_CHIPS])
    _, (mod_g,) = _xc_wait(False, first_ssem[:1], first_rsem[:1], first_src[:1], first_land[:1], c_act,
                           "mod_wait")
    mod_all = jnp.transpose(mod_g, (1, 2, 0, 3)).reshape(DEPTH, N_DEV * nseq, N_MOD * d)
    mod_mine = lax.dynamic_slice_in_dim(mod_all, me * nseq, nseq, axis=1)
    mod_k = jnp.transpose(mod_mine.reshape(DEPTH, nseq, N_MOD, 1, d), (0, 2, 1, 3, 4))
    mods = [[mod_k[l, k] for k in range(N_MOD)] for l in range(DEPTH)]

    later =[(0, "w_out"), (0, "ff_up"), (0, "ff_down"), (1, "w_in"), (1, "w_out"), (1, "ff_up"), (1, "ff_down")]
    ag_groups = {(0, "w_out"): [0], (0, "ff_up"): [1, 2], (1, "w_in"): [3, 4], (1, "ff_up"): [5, 6]}
    big_cache, ag = {}, {}

    def big_w(l, name, after):
        if (l, name) == (0, "w_in") and (l, name) not in big_cache:
            ag["ssem"], ag["rsem"], ag["src"], ag["land"], started = _xc_start(
                False, [shard(l2, n2) for l2, n2 in later], after, "ag_start")
            _, zones = _xc_wait(False, first_ssem[1:], first_rsem[1:], first_src[1:], first_land[1:],
                                jnp.full((8, 128), started, F32), "ag_first_wait", peers=OTHER_CHIPS)
            (zone,) = _sib_wait(*_sib_start(zones, "ag_first_sib_start"), "ag_first_sib_wait")
            big_cache[(l, name)] = _full_weight(name, zone)
        if (l, name) not in big_cache:
            idx = ag_groups[(l, name)]
            pick = lambda seq_: [seq_[i] for i in idx]
            _, lands = _xc_wait(False, pick(ag["ssem"]), pick(ag["rsem"]), pick(ag["src"]), pick(ag["land"]), after,
                                f"ag_wait_{l}_{name}")
            for i, land in zip(idx, lands):
                big_cache[later[i]] = _full_weight(later[i][1], land)
        return big_cache[(l, name)]

    small_w = dict(
        norm1_g=norm1_g + first_zero, norm2_g=norm2_g, ssd_conv_w=scw_f, ssd_conv_b=ssd_conv_b, ssd_dt_bias=ssd_dt_bias,
        ssd_a_log=ssd_a_log, ssd_d=ssd_d, ssd_norm_g=ssd_norm_g, gm_vnorm_g=gm_vnorm_g, gm_ws=gm_ws,
        gm_bst=jnp.transpose(gm_bs, (0, 2, 1)), gm_out_g=gm_out_g, ff_conv_w=fcw_f, ff_conv_b=ff_conv_b)

    outs = {}
    pending, win_parts = {}, {}

    def rs_finish(l, group, after):
        names, ssem, rsem, srcs, lands = pending.pop((l, group))
        srcs, lands = _xc_wait(True, ssem, rsem, srcs, lands, after, f"rs_wait_{l}_{group}")
        for nme, own, land in zip(names, srcs, lands):
            parts = [(own, lambda p: p[0])] + [(land, lambda p, k=k: k) for k in range(N_DEV - 1)]
            if nme == "w_in":
                win_parts[l] = parts
                if len(win_parts) == DEPTH:
                    outs[nme] = _adamw_rows_major([win_parts[k] for k in range(DEPTH)], wts[nme], mom[nme], var[nme],
                                                  me_arr, "adamw_w_in")
                continue
            outs[nme] = _adamw_layer(parts, wts[nme], mom[nme], var[nme], me_arr, l, outs.get(nme), f"adamw_{nme}_{l}")
        return land if names[-1] == "w_in" else outs[names[-1]][0]

    def grad_sink(l, group, grads, after):
        names = list(grads)
        ssem, rsem, srcs, lands, zero = _xc_start(True, [_by_owner(n, grads[n]) for n in names], after, f"rs_start_{l}_{group}")
        pending[(l, group)] = (names, ssem, rsem, srcs, lands)
        return zero.reshape(1, 1)

    early_gather = {}

    def small_sink(l, early, small, dmods, dfg, loss_p):
        if l > 0:
            return None
        layers = [dict(early, norm1_g=jnp.zeros((1, d), F32))] + small[1:]
        rows = lambda name: [layers[k][name] for k in range(DEPTH)]
        packed = [
            jnp.concatenate(sum([rows(n) for n in _P1024], []) + [dfg], axis=0),
            jnp.concatenate(rows("ssd_conv_w") + rows("ssd_conv_b"), axis=0),
            jnp.concatenate(rows("ff_conv_w") + rows("ff_conv_b"), axis=0),
            jnp.concatenate(sum([rows(n) for n in _P16], []) + [loss_p[:, :SSD_HEADS]], axis=0),
            jnp.concatenate([layers[k]["gm_ws"].reshape(GM_HEADS * CHUNK, CHUNK) for k in range(DEPTH)] + rows("gm_bs"), axis=0),
            jnp.concatenate([jnp.zeros((nseq, N_MOD * d), F32)] + dmods[1:], axis=0)]
        ssem, rsem, srcs, lands, zero = _xc_start(False, packed, packed[0], "small_start")
        early_gather.update(ssem=ssem, rsem=rsem, srcs=srcs, lands=lands)
        return zero.reshape(1, 1)

    grad_x, small, dmods = _local_step(
        x.reshape(nseq * seq, d), loss_target.reshape(nseq * seq, d), mods, small_w, final_g.reshape(1, d), nseq=nseq,
        big_w=big_w, grad_sink=grad_sink, small_sink=small_sink)

    late_ssem, late_rsem, late_src, late_land, _ = _xc_start(False, [small[0]["norm1_g"], dmods[0]], grad_x, "late_start")
    done = grad_x
    for l, grp in ((1, "ffn"), (1, "w_out"), (1, "w_in"), (0, "ffn"), (0, "w_out")):
        done = rs_finish(l, grp, done)
    rs_finish(0, "w_in", done)
    _, gathered = _xc_wait(False, early_gather["ssem"], early_gather["rsem"], early_gather["srcs"],
                           early_gather["lands"], outs["w_in"][0], "small_wait")
    _, late = _xc_wait(False, late_ssem, late_rsem, late_src, late_land, gathered[0], "late_wait")
    gathered = list(gathered) + list(late)
    gath = dict(zip(["p1024", "p1536", "p2816", "p16", "p128", "p6144", "late1024", "late6144"], gathered))

    dmod_all = jnp.concatenate([gath["late6144"].reshape(1, N_DEV * nseq, N_MOD * d),
                                jnp.transpose(gath["p6144"].reshape(N_DEV, DEPTH, nseq, N_MOD * d)[:, 1:], (1, 0, 2, 3)).reshape(
                                    DEPTH - 1, N_DEV * nseq, N_MOD * d)], axis=0)
    small_names = _P1024 + ["final_g", "ssd_conv_b", "ff_conv_b"] + _P16 + ["gm_ws", "gm_bs", "ada_b"]
    wmv = {}
    for nme in small_names:
        if nme == "final_g":
            wmv[nme] = tuple(a.reshape(1, d) for a in (wts[nme], mom[nme], var[nme]))
        else:
            wmv[nme] = (wts[nme], mom[nme], var[nme])
    small_out, scw_full, fcw_full, loss_sum = _adamw_small(gath, wmv)
    loss = loss_sum[0, 0]
    for nme in small_names:
        outs[nme] = small_out[nme]
    outs["final_g"] = tuple(a.reshape(d) for a in outs["final_g"])

    n_scw, n_fcw = ssd_conv_w.shape[2], ff_conv_w.shape[2]
    g_scw_mine = lax.dynamic_slice_in_dim(scw_full, me * n_scw, n_scw, axis=2)
    g_fcw_mine = lax.dynamic_slice_in_dim(fcw_full, me * n_fcw, n_fcw, axis=2)
    outs["ssd_conv_w"] = _adamw_sharded([(g_scw_mine, lambda p: 0)], ssd_conv_w, m_ssd_conv_w, v_ssd_conv_w, me_arr, "adamw_ssd_conv_w")
    outs["ff_conv_w"] = _adamw_sharded([(g_fcw_mine, lambda p: 0)], ff_conv_w, m_ff_conv_w, v_ff_conv_w, me_arr, "adamw_ff_conv_w")

    dmod_cols = _b(lax.dynamic_slice_in_dim(dmod_all, me * n_ada, n_ada, axis=2))
    g_ada = jnp.stack([_matmul(c_act, dmod_cols[l], ta=True, name=f"mm_ada_dw_{l}") for l in range(DEPTH)])
    outs["ada_w"] = _adamw_sharded([(g_ada, lambda p: 0)], ada_w, m_ada_w, v_ada_w, me_arr, "adamw_ada_w")

    for nme, perm in (("ff_up", (0, 2, 1)), ("w_in", (1, 2, 0))):
        outs[nme] = tuple(jnp.transpose(a, perm) for a in outs[nme])
    result = [loss, grad_x.reshape(nseq, seq, d)]
    for k in range(4):
        result += [outs[n][k] for n in _WEIGHTS]
    return tuple(result)
```

```python
import functools
import math

import jax
import jax.numpy as jnp
from jax import lax
from jax.experimental import pallas as pl
from jax.experimental.pallas import tpu as pltpu

F32 = jnp.float32
BF16 = jnp.bfloat16

N_DEV = 8
D_MODEL = 1024
DEPTH = 2
CHUNK = 128
SSD_HEADS = 16
SSD_HEAD_DIM = 64
SSD_GROUPS = 2
HEADS_PER_GROUP = SSD_HEADS // SSD_GROUPS
GROUP_WIDTH = HEADS_PER_GROUP * SSD_HEAD_DIM
D_STATE = 128
D_SSD = 1024
CONV_DIM = 1536
SSD_CONV = 4
GM_HEADS = 8
GM_HEAD_DIM = 128
D_GM = 1024
D_FF = 2816
FF_CONV = 3
N_IN = 4624
N_MOD = 6
EPS = 1e-6

N_INP = 5120
COL_U, COL_V, COL_Z, COL_XBC, COL_DT = 0, 1024, 2048, 3072, 4608

ADAM_LR = 0.001
ADAM_B1 = 0.9
ADAM_B2 = 0.999
ADAM_EPS = 1e-08
ADAM_WD = 0.01
ADAM_STEP = 10

VMEM_LIMIT = 56 * 1024 * 1024
MESH = pl.DeviceIdType.MESH
ANY = pl.BlockSpec(memory_space=pl.ANY)


def _cp(*sem):
    return pltpu.CompilerParams(dimension_semantics=sem, vmem_limit_bytes=VMEM_LIMIT)


def _tile(n, pref):
    if n <= pref or n % 128:
        return n
    best = 128
    for t in range(128, pref + 1, 128):
        if n % t == 0:
            best = t
    return best


def _per_layer(n):
    return pl.BlockSpec((DEPTH, n), lambda *_: (0, 0))


def _row(ref, layer, cols=slice(None)):
    return ref[layer:layer + 1, cols]


def _silu(x):
    return x * jax.nn.sigmoid(x)


def _gelu(x):
    return 0.5 * x * (1.0 + lax.erf(x * (1.0 / math.sqrt(2.0))))


def _softplus(x):
    return jnp.maximum(x, 0.0) + jnp.log1p(jnp.exp(-jnp.abs(x)))


def _b(x):
    return x.astype(BF16)


_NN = (((1,), (0,)), ((), ()))
_NT = (((1,), (1,)), ((), ()))
_TN = (((0,), (0,)), ((), ()))


def _dg(a, b, dn):
    return lax.dot_general(_b(a), _b(b), dn, preferred_element_type=F32)


@jax.custom_vjp
def _bdot(a, b):
    return _dg(a, b, _NN)


def _bdot_fwd(a, b):
    return _dg(a, b, _NN), (a, b)


def _bdot_bwd(res, ct):
    a, b = res
    return _dg(ct, b, _NT), _dg(a, ct, _TN)


_bdot.defvjp(_bdot_fwd, _bdot_bwd)


@jax.custom_vjp
def _bdot_nt(a, b):
    return _dg(a, b, _NT)


def _bdot_nt_fwd(a, b):
    return _dg(a, b, _NT), (a, b)


def _bdot_nt_bwd(res, ct):
    a, b = res
    return _dg(ct, b, _NN), _dg(ct, a, _TN)


_bdot_nt.defvjp(_bdot_nt_fwd, _bdot_nt_bwd)


@jax.custom_vjp
def _bdot_tn(a, b):
    return _dg(a, b, _TN)


def _bdot_tn_fwd(a, b):
    return _dg(a, b, _TN), (a, b)


def _bdot_tn_bwd(res, ct):
    a, b = res
    return _dg(b, ct, _NT), _dg(a, ct, _NN)


_bdot_tn.defvjp(_bdot_tn_fwd, _bdot_tn_bwd)


def _tri(n, lower):
    r = lax.broadcasted_iota(jnp.int32, (n, n), 0)
    c = lax.broadcasted_iota(jnp.int32, (n, n), 1)
    return ((r >= c) if lower else (r <= c)).astype(F32)


def _eye(n):
    r = lax.broadcasted_iota(jnp.int32, (n, n), 0)
    c = lax.broadcasted_iota(jnp.int32, (n, n), 1)
    return (r == c).astype(F32)


def _hdot(a, b, dn):
    return lax.dot_general(a, b, dn, precision=lax.Precision.HIGHEST, preferred_element_type=F32)


@jax.custom_vjp
def _cumsum_rows(x):
    return _hdot(_tri(x.shape[0], True), x, _NN)


def _cumsum_rows_fwd(x):
    return _cumsum_rows(x), None


def _cumsum_rows_bwd(_, ct):
    return (_hdot(_tri(ct.shape[0], False), ct, _NN),)


_cumsum_rows.defvjp(_cumsum_rows_fwd, _cumsum_rows_bwd)


@jax.custom_vjp
def _transpose(x):
    return _hdot(_eye(x.shape[1]), x, _NT)


def _transpose_fwd(x):
    return _transpose(x), None


def _transpose_bwd(_, ct):
    return (_hdot(_eye(ct.shape[1]), ct, _NT),)


_transpose.defvjp(_transpose_fwd, _transpose_bwd)


MXU_WIDTH = 256
MATMUL_TILE_CAP = 2816
MATMUL_VMEM = 44 * 1024 * 1024


def _mxu_tiles(n):
    if n <= MATMUL_TILE_CAP or n % 128:
        return [n]
    for unit in (MXU_WIDTH, 128):
        opts = [t for t in range(unit, MATMUL_TILE_CAP + 1, unit) if n % t == 0]
        if opts:
            return opts
    return [n]


def _matmul(a, b, *, ta=False, tb=False, name, dep=None, out_dtype=F32):
    pieces = list(a) if isinstance(a, (list, tuple)) else [a]
    npc = len(pieces)
    rows, width = pieces[0].shape
    assert all(p.shape == (rows, width) for p in pieces)
    if ta:
        k_dim, m_dim = rows, width * npc
    else:
        m_dim, k_dim = rows, width * npc
    if tb:
        n_dim, kb = b.shape
    else:
        kb, n_dim = b.shape
    assert kb == k_dim, (pieces[0].shape, npc, b.shape, ta, tb)
    m_unit = width if npc > 1 and ta else m_dim
    k_unit = width if npc > 1 and not ta else k_dim
    tm = _tile(m_unit, 1536)
    tn_opts, tk_opts = _mxu_tiles(n_dim), _mxu_tiles(k_unit)
    tn, tk = tn_opts.pop(), tk_opts.pop()
    while 4 * (tm * tk + tk * tn) + 8 * tm * tn > MATMUL_VMEM:
        if tn >= tk and tn_opts:
            tn = tn_opts.pop()
        else:
            tk = tk_opts.pop()
    ni, nj, nk = m_dim // tm, n_dim // tn, k_dim // tk
    per = width // (tm if ta else tk)
    dn = (((0 if ta else 1,), (1 if tb else 0,)), ((), ()))

    a_bytes, b_bytes = m_dim * k_dim, k_dim * n_dim
    m_outer = nk > 1 or a_bytes + b_bytes * ni <= b_bytes + a_bytes * nj
    if m_outer:
        ij = lambda o, n, k: (o, n)
        grid = (ni, nj, nk)
    else:
        ij = lambda o, n, k: (n, o)
        grid = (nj, ni, nk)

    use_acc = nk > 1 and out_dtype != F32

    def body(*refs):
        a_refs, b_ref = refs[:npc], refs[npc]
        o_ref = refs[-2] if use_acc else refs[-1]
        acc_ref = refs[-1]
        k = pl.program_id(2)
        i = pl.program_id(0 if m_outer else 1)
        along = i if ta else k

        def step(a_ref):
            p = lax.dot_general(a_ref[...], b_ref[...], dn, preferred_element_type=F32)
            if nk == 1:
                o_ref[...] = p.astype(out_dtype)
            else:
                @pl.when(k == 0)
                def _():
                    acc_ref[...] = p

                @pl.when((k > 0) & (k < nk - 1 if use_acc else True))
                def _():
                    acc_ref[...] += p

                if use_acc:
                    @pl.when(k == nk - 1)
                    def _():
                        o_ref[...] = (acc_ref[...] + p).astype(out_dtype)

        if npc == 1:
            step(a_refs[0])
        else:
            for pc in range(npc):
                pl.when((along >= pc * per) & (along < (pc + 1) * per))(functools.partial(step, a_refs[pc]))

    def a_map(pc, o, n, k):
        i, _ = ij(o, n, k)
        along = i if ta else k
        if npc > 1:
            along = jnp.clip(along - pc * per, 0, per - 1)
        return (k, along) if ta else (i, along)

    def b_map(o, n, k):
        _, j = ij(o, n, k)
        return (j, k) if tb else (k, j)

    extra = [] if dep is None else [dep]
    return pl.pallas_call(
        body, name=name,
        grid=grid,
        in_specs=[pl.BlockSpec((tk, tm) if ta else (tm, tk), functools.partial(a_map, pc)) for pc in range(npc)]
        + [pl.BlockSpec((tn, tk) if tb else (tk, tn), b_map)] + [ANY] * len(extra),
        out_specs=pl.BlockSpec((tm, tn), lambda o, n, k: ij(o, n, k)),
        out_shape=jax.ShapeDtypeStruct((m_dim, n_dim), out_dtype),
        scratch_shapes=[pltpu.VMEM((tm, tn), F32)] if use_acc else [],
        compiler_params=_cp("parallel", "parallel", "arbitrary"),
    )(*pieces, b, *extra)


def _ada_fwd(c_all, ada_w, ada_b_shard):
    depth, d, n = ada_w.shape
    nb = c_all.shape[0]

    def body(c_ref, w_ref, b_ref, o_ref, ca_ref):
        ca = _silu(c_ref[...])
        ca_ref[...] = _b(ca)
        o_ref[0] = _dg(ca, w_ref[0], _NN) + b_ref[0]

    return pl.pallas_call(
        body, name="ada_fwd",
        grid=(depth,),
        in_specs=[pl.BlockSpec((nb, d), lambda l: (0, 0)),
                  pl.BlockSpec((1, d, n), lambda l: (l, 0, 0)),
                  pl.BlockSpec((1, 1, n), lambda l: (l, 0, 0))],
        out_specs=[pl.BlockSpec((1, nb, n), lambda l: (l, 0, 0)),
                   pl.BlockSpec((nb, d), lambda l: (0, 0))],
        out_shape=[jax.ShapeDtypeStruct((depth, nb, n), F32), jax.ShapeDtypeStruct((nb, d), BF16)],
        compiler_params=_cp("arbitrary"),
    )(c_all, ada_w, ada_b_shard)


def _fold(acc):
    return jnp.sum(acc, axis=0, keepdims=True)


def _rinv(x):
    return lax.rsqrt(jnp.sum(x * x, axis=-1, keepdims=True) * (1.0 / D_MODEL) + EPS)


def _rms_bwd(a, xhat, rinv):
    return rinv * (a - xhat * (jnp.sum(a * xhat, axis=-1, keepdims=True) * (1.0 / D_MODEL)))


def _row_tile(seq):
    return min(seq, 256)


def _normmod_fwd(x, g, sc, sh, *, nseq, name, layer):
    t, d = x.shape
    seq = t // nseq
    tr = _row_tile(seq)
    nt = seq // tr
    row = pl.BlockSpec((tr, d), lambda s, i: (s * nt + i, 0))
    per_seq = pl.BlockSpec((1, 1, d), lambda s, i: (s, 0, 0))

    def body(x_ref, g_ref, sc_ref, sh_ref, h_ref):
        x_v = x_ref[...]
        h_ref[...] = _b(x_v * _rinv(x_v) * (_row(g_ref, layer) * (1.0 + sc_ref[0])) + sh_ref[0])

    return pl.pallas_call(
        body, name=name, grid=(nseq, nt),
        in_specs=[row, _per_layer(d), per_seq, per_seq],
        out_specs=row,
        out_shape=jax.ShapeDtypeStruct((t, d), BF16),
        compiler_params=_cp("parallel", "parallel"),
    )(x, g, sc, sh)


NORM_TM = 512


def _matmul_normbwd(a, b, dxo, x, delta, gate, g, sc, *, nseq, name, layer, dep=None):
    pieces = list(a) if isinstance(a, (list, tuple)) else [a]
    npc = len(pieces)
    t, width = pieces[0].shape
    k_dim, d = width * npc, b.shape[1]
    assert b.shape[0] == k_dim and all(p.shape == (t, width) for p in pieces)
    seq = t // nseq
    tm = min(NORM_TM, seq)
    per_seq_tiles = seq // tm
    tk = _mxu_tiles(width if npc > 1 else k_dim).pop()
    nk, per = k_dim // tk, width // tk
    has_delta = delta is not None
    extra = [] if dep is None else [dep]

    def body(*refs):
        a_refs, b_ref = refs[:npc], refs[npc]
        dxo_ref, x_ref = refs[npc + 1], refs[npc + 2]
        pos = npc + 3
        if has_delta:
            delta_ref, gate_ref = refs[pos], refs[pos + 1]
            pos += 2
        g_ref, sc_ref = refs[pos], refs[pos + 1]
        pos += 2 + len(extra)
        if has_delta:
            dx_ref, dd_ref, dgate_ref, dg_ref, dsc_ref, dsh_ref = refs[pos:pos + 6]
        else:
            dx_ref, dg_ref, dsc_ref, dsh_ref = refs[pos:pos + 4]
        acc_ref = refs[-1]
        i, k = pl.program_id(0), pl.program_id(1)

        def norm_bwd(dh_v):
            g_v, one_sc = _row(g_ref, layer), 1.0 + sc_ref[0]
            x_v = x_ref[...]
            rinv = _rinv(x_v)
            xhat = x_v * rinv
            dx = dxo_ref[...] + _rms_bwd(dh_v * (g_v * one_sc), xhat, rinv)
            dx_ref[...] = dx

            @pl.when(i == 0)
            def _():
                dg_ref[...] = jnp.zeros_like(dg_ref)

            @pl.when(i % per_seq_tiles == 0)
            def _():
                dsc_ref[...] = jnp.zeros_like(dsc_ref)
                dsh_ref[...] = jnp.zeros_like(dsh_ref)
                if has_delta:
                    dgate_ref[...] = jnp.zeros_like(dgate_ref)

            t_sum = _fold(dh_v * xhat)
            dg_ref[...] += t_sum * one_sc
            dsc_ref[0] += t_sum * g_v
            dsh_ref[0] += _fold(dh_v)
            if has_delta:
                dd_ref[...] = _b(dx * gate_ref[0])
                dgate_ref[0] += _fold(dx * delta_ref[...])

        def step(a_ref):
            p = lax.dot_general(a_ref[...], b_ref[...], _NN, preferred_element_type=F32)
            if nk == 1:
                norm_bwd(p)
            else:
                @pl.when(k == 0)
                def _():
                    acc_ref[...] = p

                @pl.when((k > 0) & (k < nk - 1))
                def _():
                    acc_ref[...] += p

                @pl.when(k == nk - 1)
                def _():
                    norm_bwd(acc_ref[...] + p)

        if npc == 1:
            step(a_refs[0])
        else:
            for pc in range(npc):
                pl.when((k >= pc * per) & (k < (pc + 1) * per))(functools.partial(step, a_refs[pc]))

    def a_map(pc, i, k):
        return (i, jnp.clip(k - pc * per, 0, per - 1) if npc > 1 else k)

    row = pl.BlockSpec((tm, d), lambda i, k: (i, 0))
    per_seq = pl.BlockSpec((1, 1, d), lambda i, k: (i // per_seq_tiles, 0, 0))
    vec = pl.BlockSpec((1, d), lambda i, k: (0, 0))
    shp = lambda *s, dt=F32: jax.ShapeDtypeStruct(s, dt)
    in_specs = [pl.BlockSpec((tm, tk), functools.partial(a_map, pc)) for pc in range(npc)]
    in_specs += [pl.BlockSpec((tk, d), lambda i, k: (k, 0)), row, row]
    operands = [*pieces, b, dxo, x]
    if has_delta:
        in_specs += [row, per_seq]
        operands += [delta, gate]
    in_specs += [_per_layer(d), per_seq] + [ANY] * len(extra)
    operands += [g, sc, *extra]
    if has_delta:
        out_specs = [row, row, per_seq, vec, per_seq, per_seq]
        out_shape = [shp(t, d), shp(t, d, dt=BF16), shp(nseq, 1, d), shp(1, d), shp(nseq, 1, d), shp(nseq, 1, d)]
    else:
        out_specs = [row, vec, per_seq, per_seq]
        out_shape = [shp(t, d), shp(1, d), shp(nseq, 1, d), shp(nseq, 1, d)]
    outs = pl.pallas_call(
        body, name=name, grid=(t // tm, nk),
        in_specs=in_specs, out_specs=out_specs, out_shape=out_shape,
        scratch_shapes=[pltpu.VMEM((tm, d), F32)],
        compiler_params=_cp("arbitrary", "arbitrary"),
    )(*operands)
    if has_delta:
        return tuple(outs)
    dx, dg, dsc, dsh = outs
    return dx, None, None, dg, dsc, dsh


def _matmul_normfwd(a, b, xin, gate, g, sc, sh, *, nseq, name, layer):
    t, k_dim = a.shape
    d = b.shape[1]
    assert b.shape[0] == k_dim and k_dim <= MATMUL_TILE_CAP
    seq = t // nseq
    tm = min(NORM_TM, seq)
    per_seq_tiles = seq // tm

    def body(a_ref, b_ref, xin_ref, gate_ref, g_ref, sc_ref, sh_ref, dl_ref, x_ref, h_ref):
        dl = lax.dot_general(a_ref[...], b_ref[...], _NN, preferred_element_type=F32)
        dl_ref[...] = dl
        x = xin_ref[...] + gate_ref[0] * dl
        x_ref[...] = x
        h_ref[...] = _b(x * _rinv(x) * (_row(g_ref, layer) * (1.0 + sc_ref[0])) + sh_ref[0])

    row = pl.BlockSpec((tm, d), lambda i: (i, 0))
    per_seq = pl.BlockSpec((1, 1, d), lambda i: (i // per_seq_tiles, 0, 0))
    return pl.pallas_call(
        body, name=name, grid=(t // tm,),
        in_specs=[pl.BlockSpec((tm, k_dim), lambda i: (i, 0)), pl.BlockSpec((k_dim, d), lambda i: (0, 0)),
                  row, per_seq, _per_layer(d), per_seq, per_seq],
        out_specs=[row, row, row],
        out_shape=[jax.ShapeDtypeStruct((t, d), F32), jax.ShapeDtypeStruct((t, d), F32), jax.ShapeDtypeStruct((t, d), BF16)],
        compiler_params=_cp("parallel"),
    )(a, b, xin, gate, g, sc, sh)


def _matmul_loss(a, b, xin, gate, fg, target, *, nseq, name):
    t, k_dim = a.shape
    d = b.shape[1]
    assert b.shape[0] == k_dim and k_dim <= MATMUL_TILE_CAP
    seq = t // nseq
    tm = min(NORM_TM, seq)
    per_seq_tiles = seq // tm

    def body(a_ref, b_ref, xin_ref, gate_ref, fg_ref, tgt_ref, dl_ref, loss_ref, dx_ref, dd_ref, dgate_ref, dfg_ref):
        i = pl.program_id(0)
        fg_v, gate_v = fg_ref[...], gate_ref[0]
        dl = lax.dot_general(a_ref[...], b_ref[...], _NN, preferred_element_type=F32)
        dl_ref[...] = dl
        x = xin_ref[...] + gate_v * dl
        rinv = _rinv(x)
        xhat = x * rinv
        err = xhat * fg_v - tgt_ref[...]
        dx = _rms_bwd(err * fg_v * (1.0 / d), xhat, rinv)
        dx_ref[...] = dx
        dd_ref[...] = _b(dx * gate_v)

        @pl.when(i == 0)
        def _():
            loss_ref[...] = jnp.zeros_like(loss_ref)
            dfg_ref[...] = jnp.zeros_like(dfg_ref)

        @pl.when(i % per_seq_tiles == 0)
        def _():
            dgate_ref[...] = jnp.zeros_like(dgate_ref)

        loss_ref[...] += jnp.sum(err * err) * (0.5 / d)
        dfg_ref[...] += _fold(err * xhat) * (1.0 / d)
        dgate_ref[0] += _fold(dx * dl)

    row = pl.BlockSpec((tm, d), lambda i: (i, 0))
    per_seq = pl.BlockSpec((1, 1, d), lambda i: (i // per_seq_tiles, 0, 0))
    vec = pl.BlockSpec((1, d), lambda i: (0, 0))
    return pl.pallas_call(
        body, name=name, grid=(t // tm,),
        in_specs=[pl.BlockSpec((tm, k_dim), lambda i: (i, 0)), pl.BlockSpec((k_dim, d), lambda i: (0, 0)),
                  row, per_seq, vec, row],
        out_specs=[row, pl.BlockSpec((1, 128), lambda i: (0, 0)), row, row, per_seq, vec],
        out_shape=[jax.ShapeDtypeStruct((t, d), F32), jax.ShapeDtypeStruct((1, 128), F32), jax.ShapeDtypeStruct((t, d), F32),
                   jax.ShapeDtypeStruct((t, d), BF16), jax.ShapeDtypeStruct((nseq, 1, d), F32),
                   jax.ShapeDtypeStruct((1, d), F32)],
        compiler_params=_cp("arbitrary"),
    )(a, b, xin, gate, fg, target)


CONV_TC = 256
CONV_LANES = 128
CONV_ROWS = 64
CONV_HALO = 8


def _conv_slabs(seq, fn):
    def step(i, carry):
        r0 = pl.multiple_of(i * CONV_ROWS, CONV_ROWS)
        for h in range(CONV_TC // CONV_LANES):
            fn(r0, slice(h * CONV_LANES, (h + 1) * CONV_LANES))
        return carry

    lax.fori_loop(0, seq // CONV_ROWS, step, 0)


def _slab(ref, r0, cols, seq):
    after = ref[pl.ds(pl.multiple_of(jnp.minimum(r0 + CONV_ROWS, seq - CONV_HALO), CONV_HALO), CONV_HALO), cols]
    return jnp.concatenate([ref[pl.ds(r0, CONV_ROWS), cols], jnp.where(r0 + CONV_ROWS < seq, after, 0.0)], axis=0)


def _conv_block(x, w_ref, b):
    kw = w_ref.shape[0]
    rows = lax.broadcasted_iota(jnp.int32, x.shape, 0)
    y = b + w_ref[kw - 1:kw, :] * x
    for j in range(1, kw):
        y = y + w_ref[kw - 1 - j:kw - j, :] * jnp.where(rows >= j, pltpu.roll(x, j, 0), 0.0)
    return y


def _conv_block_bwd(dy, x, w_ref, dw_ref, db_ref):
    kw = w_ref.shape[0]
    n = x.shape[0]
    rows = lax.broadcasted_iota(jnp.int32, x.shape, 0)
    dx = w_ref[kw - 1:kw, :] * dy
    dw_ref[kw - 1:kw, :] += jnp.sum(dy * x, axis=0, keepdims=True)
    for j in range(1, kw):
        dy_j = jnp.where(rows < n - j, pltpu.roll(dy, n - j, 0), 0.0)
        dx = dx + w_ref[kw - 1 - j:kw - j, :] * dy_j
        dw_ref[kw - 1 - j:kw - j, :] += jnp.sum(dy_j * x, axis=0, keepdims=True)
    db_ref[...] += jnp.sum(dy, axis=0, keepdims=True)
    return dx


def _conv_bwd(dy_ext, x, w_ref, dw_ref, db_ref, cols):
    kw = w_ref.shape[0]
    n = dy_ext.shape[0]
    dy = dy_ext[:CONV_ROWS]
    dx = w_ref[kw - 1:kw, cols] * dy
    dw_ref[kw - 1:kw, cols] += jnp.sum(dy * x, axis=0, keepdims=True)
    for j in range(1, kw):
        dy_j = pltpu.roll(dy_ext, n - j, 0)[:CONV_ROWS]
        dx = dx + w_ref[kw - 1 - j:kw - j, cols] * dy_j
        dw_ref[kw - 1 - j:kw - j, cols] += jnp.sum(dy_j * x, axis=0, keepdims=True)
    db_ref[:, cols] += jnp.sum(dy, axis=0, keepdims=True)
    return dx


def _dsilu(pre):
    sg = jax.nn.sigmoid(pre)
    return pre * sg, sg * (1.0 + pre * (1.0 - sg))


def _conv_specs(kw, layer):
    return [pl.BlockSpec((None, kw, CONV_TC), lambda j, s: (layer, 0, j)),
            pl.BlockSpec((DEPTH, CONV_TC), lambda j, s: (0, j))]


def _ssd_conv_fwd(proj, w, b, *, nseq, layer):
    t = proj.shape[0]
    seq = t // nseq
    nb = CONV_DIM // CONV_TC
    off = COL_XBC // CONV_TC

    def body(x_ref, w_ref, b_ref, o_ref, pre_ref):
        pre = _conv_block(x_ref[...], w_ref, _row(b_ref, layer))
        pre_ref[...] = pre
        o_ref[...] = _silu(pre)

    col = pl.BlockSpec((seq, CONV_TC), lambda j, s: (s, j))
    return pl.pallas_call(
        body, name="ssd_conv_fwd", grid=(nb, nseq),
        in_specs=[pl.BlockSpec((seq, CONV_TC), lambda j, s: (s, off + j)), *_conv_specs(SSD_CONV, layer)],
        out_specs=[col, col],
        out_shape=[jax.ShapeDtypeStruct((t, CONV_DIM), F32)] * 2,
        compiler_params=_cp("parallel", "parallel"),
    )(proj, w, b)


def _ssd_conv_bwd(dact, pre, proj, w, dproj, *, nseq, layer):
    t = proj.shape[0]
    seq = t // nseq
    nb = CONV_DIM // CONV_TC
    off = COL_XBC // CONV_TC

    def body(da_ref, pre_ref, x_ref, w_ref, dproj_ref, dx_ref, dw_ref, db_ref):
        del dproj_ref

        @pl.when(pl.program_id(1) == 0)
        def _():
            dw_ref[...] = jnp.zeros_like(dw_ref)
            db_ref[...] = jnp.zeros_like(db_ref)

        def slab(r0, cols):
            _, dsilu = _dsilu(_slab(pre_ref, r0, cols, seq))
            dpre_ext = _slab(da_ref, r0, cols, seq) * dsilu
            x = x_ref[pl.ds(r0, CONV_ROWS), cols]
            dx_ref[pl.ds(r0, CONV_ROWS), cols] = _b(_conv_bwd(dpre_ext, x, w_ref, dw_ref, db_ref, cols))

        _conv_slabs(seq, slab)

    return pl.pallas_call(
        body, name="ssd_conv_bwd", grid=(nb, nseq),
        in_specs=[pl.BlockSpec((seq, CONV_TC), lambda j, s: (s, j)),
                  pl.BlockSpec((seq, CONV_TC), lambda j, s: (s, j)),
                  pl.BlockSpec((seq, CONV_TC), lambda j, s: (s, off + j)),
                  _conv_specs(SSD_CONV, layer)[0],
                  ANY],
        out_specs=[pl.BlockSpec((seq, CONV_TC), lambda j, s: (s, off + j)),
                   pl.BlockSpec((SSD_CONV, CONV_TC), lambda j, s: (0, j)),
                   pl.BlockSpec((1, CONV_TC), lambda j, s: (0, j))],
        out_shape=[jax.ShapeDtypeStruct(dproj.shape, dproj.dtype), jax.ShapeDtypeStruct((SSD_CONV, CONV_DIM), F32),
                   jax.ShapeDtypeStruct((1, CONV_DIM), F32)],
        input_output_aliases={4: 0},
        compiler_params=_cp("parallel", "arbitrary"),
    )(dact, pre, proj, w, dproj)


def _ffn_act_fwd(up, w, b, *, nseq, layer):
    t = up.shape[0]
    seq = t // nseq
    nb = D_FF // CONV_TC

    def body(g_ref, v_ref, w_ref, b_ref, o_ref):
        pre = _conv_block(g_ref[...].astype(F32), w_ref, _row(b_ref, layer))
        o_ref[...] = _b(_silu(pre) * v_ref[...].astype(F32))

    col = pl.BlockSpec((seq, CONV_TC), lambda j, s: (s, j))
    return pl.pallas_call(
        body, name="ffn_act_fwd", grid=(nb, nseq),
        in_specs=[col,
                  pl.BlockSpec((seq, CONV_TC), lambda j, s: (s, nb + j)),
                  *_conv_specs(FF_CONV, layer)],
        out_specs=col,
        out_shape=jax.ShapeDtypeStruct((t, D_FF), BF16),
        compiler_params=_cp("parallel", "parallel"),
    )(up, up, w, b)


def _ffn_act_bwd(dact, up, w, b, *, nseq, layer):
    t = up.shape[0]
    seq = t // nseq
    nb = D_FF // CONV_TC

    def body(da_ref, g_ref, v_ref, w_ref, b_ref, dg_ref, dv_ref, dw_ref, db_ref):
        @pl.when(pl.program_id(1) == 0)
        def _():
            dw_ref[...] = jnp.zeros_like(dw_ref)
            db_ref[...] = jnp.zeros_like(db_ref)

        gate = g_ref[...].astype(F32)
        silu, dsilu = _dsilu(_conv_block(gate, w_ref, _row(b_ref, layer)))
        da = da_ref[...].astype(F32)
        dv_ref[...] = _b(da * silu)
        dg_ref[...] = _b(_conv_block_bwd(da * v_ref[...].astype(F32) * dsilu, gate, w_ref, dw_ref, db_ref))

    col = pl.BlockSpec((seq, CONV_TC), lambda j, s: (s, j))
    return pl.pallas_call(
        body, name="ffn_act_bwd", grid=(nb, nseq),
        in_specs=[col, col,
                  pl.BlockSpec((seq, CONV_TC), lambda j, s: (s, nb + j)),
                  *_conv_specs(FF_CONV, layer)],
        out_specs=[col, col,
                   pl.BlockSpec((FF_CONV, CONV_TC), lambda j, s: (0, j)),
                   pl.BlockSpec((1, CONV_TC), lambda j, s: (0, j))],
        out_shape=[jax.ShapeDtypeStruct((t, D_FF), BF16), jax.ShapeDtypeStruct((t, D_FF), BF16),
                   jax.ShapeDtypeStruct((FF_CONV, D_FF), F32), jax.ShapeDtypeStruct((1, D_FF), F32)],
        compiler_params=_cp("parallel", "arbitrary"),
    )(dact, up, up, w, b)


SSD_PAIRS = SSD_HEADS // 2
PAIR_W = 2 * SSD_HEAD_DIM
PAIRS_PER_GROUP = SSD_PAIRS // SSD_GROUPS


def _ssd_chunk(xs, bg, cg, dtr, z, hp, dtb, alog, dskip, ng):
    n = dtr.shape[0]
    dt = _softplus(dtr + dtb)
    cs = _cumsum_rows(dt * (-jnp.exp(alog)))
    cs_t = _transpose(cs)
    lane = lax.broadcasted_iota(jnp.int32, (1, SSD_HEADS), 1)
    sub = lax.broadcasted_iota(jnp.int32, (SSD_HEADS, 1), 0)
    row = lax.broadcasted_iota(jnp.int32, (n, 1), 0)
    causal = lax.broadcasted_iota(jnp.int32, (n, n), 0) >= lax.broadcasted_iota(jnp.int32, (n, n), 1)
    future = jnp.where(causal, 0.0, -1e30)
    first = lax.broadcasted_iota(jnp.int32, (1, PAIR_W), 1) < SSD_HEAD_DIM
    first_rows = lax.broadcasted_iota(jnp.int32, (PAIR_W, 1), 0) < SSD_HEAD_DIM
    first_f = first.astype(F32)
    cb = [_bdot_nt(cg[g], bg[g]) for g in range(SSD_GROUPS)]
    ys, hn = [], []
    for p in range(SSD_PAIRS):
        g = p // PAIRS_PER_GROUP
        col, decay, last = [], [], []
        for h in (2 * p, 2 * p + 1):
            oh = (lane == h).astype(F32)
            cs_h = jnp.sum(cs * oh, axis=1, keepdims=True)
            cs_row = jnp.sum(cs_t * (sub == h).astype(F32), axis=0, keepdims=True)
            col.append((jnp.sum(dt * oh, axis=1, keepdims=True), cs_h, jnp.sum(dskip * oh, axis=1, keepdims=True)))
            last.append(jnp.sum(jnp.where(row == n - 1, cs_h, 0.0), axis=0, keepdims=True))
            decay.append(jnp.exp(cs_h - cs_row + future))
        pair = lambda a, b: jnp.where(first, a, b)
        dt_p = pair(col[0][0], col[1][0])
        cs_p = pair(col[0][1], col[1][1])
        last_p = pair(last[0], last[1])
        xc = xs[p] * dt_p
        y = _bdot(cb[g] * decay[0], xc * first_f) + _bdot(cb[g] * decay[1], xc * (1.0 - first_f))
        y = y + _bdot_nt(cg[g], hp[p]) * jnp.exp(cs_p)
        y = y + pair(col[0][2], col[1][2]) * xs[p]
        keep = jnp.where(first_rows, jnp.exp(last[0]), jnp.exp(last[1]))
        hn.append(keep * hp[p] + _bdot_tn(xc * jnp.exp(last_p - cs_p), bg[g]))
        ys.append(y * _silu(z[p]))
    outs = []
    for g in range(SSD_GROUPS):
        ps = range(g * PAIRS_PER_GROUP, (g + 1) * PAIRS_PER_GROUP)
        ms = sum(jnp.sum(ys[p] * ys[p], axis=1, keepdims=True) for p in ps) * (1.0 / GROUP_WIDTH)
        r = lax.rsqrt(ms + EPS)
        outs += [ys[p] * r * ng[p] for p in ps]
    return outs, hn


def _hslices(ref, width, count, base=0, rows=slice(None)):
    return [ref[rows, base + k * width: base + (k + 1) * width] for k in range(count)]


def _ssd_load(xbc_ref, z_ref, dt_ref, ng_ref, layer):
    xs = _hslices(xbc_ref, PAIR_W, SSD_PAIRS)
    bg = _hslices(xbc_ref, D_STATE, SSD_GROUPS, D_SSD)
    cg = _hslices(xbc_ref, D_STATE, SSD_GROUPS, D_SSD + SSD_GROUPS * D_STATE)
    z = _hslices(z_ref, PAIR_W, SSD_PAIRS)
    ng = _hslices(ng_ref, PAIR_W, SSD_PAIRS, rows=slice(layer, layer + 1))
    return xs, bg, cg, dt_ref[:, 0:SSD_HEADS], z, ng


def _ssd_specs(nch):
    rowi = lambda s, c: s * nch + c
    return [pl.BlockSpec((CHUNK, CONV_DIM), lambda s, c: (rowi(s, c), 0)),
            pl.BlockSpec((CHUNK, D_SSD), lambda s, c: (rowi(s, c), COL_Z // D_SSD)),
            pl.BlockSpec((CHUNK, 128), lambda s, c: (rowi(s, c), COL_DT // 128)),
            _per_layer(SSD_HEADS), _per_layer(SSD_HEADS), _per_layer(SSD_HEADS), _per_layer(D_SSD)]


def _ssd_fwd(xbc, proj, dtb, alog, dskip, ng, *, nseq, layer):
    t = proj.shape[0]
    nch = t // nseq // CHUNK
    hd = PAIR_W

    def body(xbc_ref, z_ref, dt_ref, dtb_ref, alog_ref, dsk_ref, ng_ref, y_ref, hp_ref, h_ref):
        @pl.when(pl.program_id(1) == 0)
        def _():
            h_ref[...] = jnp.zeros_like(h_ref)

        xs, bg, cg, dtr, z, ngs = _ssd_load(xbc_ref, z_ref, dt_ref, ng_ref, layer)
        hp_ref[0] = h_ref[...]
        hp = [h_ref[h * hd:(h + 1) * hd, :] for h in range(SSD_PAIRS)]
        outs, hn = _ssd_chunk(xs, bg, cg, dtr, z, hp, _row(dtb_ref, layer), _row(alog_ref, layer), _row(dsk_ref, layer), ngs)
        for h in range(SSD_PAIRS):
            y_ref[:, h * hd:(h + 1) * hd] = _b(outs[h])
            h_ref[h * hd:(h + 1) * hd, :] = hn[h]

    return pl.pallas_call(
        body, name="ssd_fwd", grid=(nseq, nch),
        in_specs=_ssd_specs(nch),
        out_specs=[pl.BlockSpec((CHUNK, D_SSD), lambda s, c: (s * nch + c, 0)),
                   pl.BlockSpec((1, D_SSD, D_STATE), lambda s, c: (s * nch + c, 0, 0))],
        out_shape=[jax.ShapeDtypeStruct((t, D_SSD + D_GM), BF16),
                   jax.ShapeDtypeStruct((t // CHUNK, D_SSD, D_STATE), F32)],
        scratch_shapes=[pltpu.VMEM((D_SSD, D_STATE), F32)],
        compiler_params=_cp("arbitrary", "arbitrary"),
    )(xbc, proj, proj, dtb, alog, dskip, ng)


def _ssd_bwd(dy, xbc, proj, hprev, dtb, alog, dskip, ng, *, nseq, layer):
    t = proj.shape[0]
    nch = t // nseq // CHUNK
    hd = PAIR_W
    rev = lambda s, c: s * nch + (nch - 1 - c)

    def body(dy_ref, xbc_ref, z_ref, dt_ref, hp_ref, dtb_ref, alog_ref, dsk_ref, ng_ref,
             dxbc_ref, dproj_ref, ddtb_ref, dalog_ref, ddsk_ref, dng_ref, dh_ref):
        first = (pl.program_id(0) == 0) & (pl.program_id(1) == 0)

        @pl.when(pl.program_id(1) == 0)
        def _():
            dh_ref[...] = jnp.zeros_like(dh_ref)

        @pl.when(first)
        def _():
            ddtb_ref[...] = jnp.zeros_like(ddtb_ref)
            dalog_ref[...] = jnp.zeros_like(dalog_ref)
            ddsk_ref[...] = jnp.zeros_like(ddsk_ref)
            dng_ref[...] = jnp.zeros_like(dng_ref)

        xs, bg, cg, dtr, z, ngs = _ssd_load(xbc_ref, z_ref, dt_ref, ng_ref, layer)
        hp = [hp_ref[0, h * hd:(h + 1) * hd, :] for h in range(SSD_PAIRS)]
        _, vjp = jax.vjp(_ssd_chunk, xs, bg, cg, dtr, z, hp, _row(dtb_ref, layer), _row(alog_ref, layer), _row(dsk_ref, layer), ngs)
        douts = [dy_ref[:, h * hd:(h + 1) * hd] for h in range(SSD_PAIRS)]
        dhn = [dh_ref[h * hd:(h + 1) * hd, :] for h in range(SSD_PAIRS)]
        dxs, dbg, dcg, ddtr, dz, dhp, ddtb, dalog, ddsk, dngs = vjp((douts, dhn))
        dproj_ref[:, :COL_Z] = jnp.zeros((CHUNK, COL_Z), BF16)
        dproj_ref[:, COL_XBC:] = jnp.zeros((CHUNK, N_INP - COL_XBC), BF16)
        for h in range(SSD_PAIRS):
            dxbc_ref[:, h * hd:(h + 1) * hd] = dxs[h]
            dproj_ref[:, COL_Z + h * hd: COL_Z + (h + 1) * hd] = _b(dz[h])
            dh_ref[h * hd:(h + 1) * hd, :] = dhp[h]
            dng_ref[:, h * hd:(h + 1) * hd] += dngs[h]
        for g in range(SSD_GROUPS):
            dxbc_ref[:, D_SSD + g * D_STATE: D_SSD + (g + 1) * D_STATE] = dbg[g]
            dxbc_ref[:, D_SSD + (SSD_GROUPS + g) * D_STATE: D_SSD + (SSD_GROUPS + g + 1) * D_STATE] = dcg[g]
        dproj_ref[:, COL_DT:COL_DT + SSD_HEADS] = _b(ddtr)
        ddtb_ref[...] += ddtb
        dalog_ref[...] += dalog
        ddsk_ref[...] += ddsk

    small = pl.BlockSpec((1, SSD_HEADS), lambda s, c: (0, 0))
    return pl.pallas_call(
        body, name="ssd_bwd", grid=(nseq, nch),
        in_specs=[pl.BlockSpec((CHUNK, D_SSD), lambda s, c: (rev(s, c), 0)),
                  pl.BlockSpec((CHUNK, CONV_DIM), lambda s, c: (rev(s, c), 0)),
                  pl.BlockSpec((CHUNK, D_SSD), lambda s, c: (rev(s, c), COL_Z // D_SSD)),
                  pl.BlockSpec((CHUNK, 128), lambda s, c: (rev(s, c), COL_DT // 128)),
                  pl.BlockSpec((1, D_SSD, D_STATE), lambda s, c: (rev(s, c), 0, 0)),
                  _per_layer(SSD_HEADS), _per_layer(SSD_HEADS), _per_layer(SSD_HEADS), _per_layer(D_SSD)],
        out_specs=[pl.BlockSpec((CHUNK, CONV_DIM), lambda s, c: (rev(s, c), 0)),
                   pl.BlockSpec((CHUNK, N_INP), lambda s, c: (rev(s, c), 0)),
                   small, small, small,
                   pl.BlockSpec((1, D_SSD), lambda s, c: (0, 0))],
        out_shape=[jax.ShapeDtypeStruct((t, CONV_DIM), F32), jax.ShapeDtypeStruct((t, N_INP), BF16),
                   jax.ShapeDtypeStruct((1, SSD_HEADS), F32), jax.ShapeDtypeStruct((1, SSD_HEADS), F32),
                   jax.ShapeDtypeStruct((1, SSD_HEADS), F32), jax.ShapeDtypeStruct((1, D_SSD), F32)],
        scratch_shapes=[pltpu.VMEM((D_SSD, D_STATE), F32)],
        compiler_params=_cp("arbitrary", "arbitrary"),
    )(dy, xbc, proj, proj, hprev, dtb, alog, dskip, ng)


def _gmlp_chunk(gu, gv, ws, bs_cols, vg, og):
    n = gu[0].shape[0]
    mask = _tri(n, True)
    au = [_gelu(t) for t in gu]
    av = [_gelu(t) for t in gv]
    r = lax.rsqrt(sum(jnp.sum(t * t, axis=1, keepdims=True) for t in av) * (1.0 / D_GM) + EPS)
    p = []
    for h in range(GM_HEADS):
        sv = _bdot(ws[h] * mask, av[h] * r * vg[h]) + bs_cols[h]
        p.append(au[h] * sv)
    r2 = lax.rsqrt(sum(jnp.sum(t * t, axis=1, keepdims=True) for t in p) * (1.0 / D_GM) + EPS)
    return [p[h] * r2 * og[h] for h in range(GM_HEADS)]


def _gmlp_load(u_ref, v_ref, ws_ref, bst_ref, vg_ref, og_ref, layer):
    gu = _hslices(u_ref, GM_HEAD_DIM, GM_HEADS)
    gv = _hslices(v_ref, GM_HEAD_DIM, GM_HEADS)
    ws = [ws_ref[h] for h in range(GM_HEADS)]
    bs_cols = [bst_ref[:, h:h + 1] for h in range(GM_HEADS)]
    mine = slice(layer, layer + 1)
    return (gu, gv, ws, bs_cols, _hslices(vg_ref, GM_HEAD_DIM, GM_HEADS, rows=mine),
            _hslices(og_ref, GM_HEAD_DIM, GM_HEADS, rows=mine))


def _gmlp_specs(layer):
    return [pl.BlockSpec((CHUNK, D_GM), lambda i: (i, COL_U // D_GM)),
            pl.BlockSpec((CHUNK, D_GM), lambda i: (i, COL_V // D_GM)),
            pl.BlockSpec((None, GM_HEADS, CHUNK, CHUNK), lambda i: (layer, 0, 0, 0)),
            pl.BlockSpec((None, CHUNK, GM_HEADS), lambda i: (layer, 0, 0)),
            _per_layer(D_GM), _per_layer(D_GM)]


def _gmlp_fwd(proj, ycat, ws, bst, vg, og, *, layer):
    t = proj.shape[0]

    def body(u_ref, v_ref, ws_ref, bst_ref, vg_ref, og_ref, ycat_ref, o_ref):
        del ycat_ref
        outs = _gmlp_chunk(*_gmlp_load(u_ref, v_ref, ws_ref, bst_ref, vg_ref, og_ref, layer))
        for h in range(GM_HEADS):
            o_ref[:, h * GM_HEAD_DIM:(h + 1) * GM_HEAD_DIM] = _b(outs[h])

    return pl.pallas_call(
        body, name="gmlp_fwd", grid=(t // CHUNK,),
        in_specs=_gmlp_specs(layer) + [ANY],
        out_specs=pl.BlockSpec((CHUNK, D_GM), lambda i: (i, D_SSD // D_GM)),
        out_shape=jax.ShapeDtypeStruct(ycat.shape, ycat.dtype),
        input_output_aliases={6: 0},
        compiler_params=_cp("parallel"),
    )(proj, proj, ws, bst, vg, og, ycat)


def _gmlp_bwd(dy, proj, ws, bst, vg, og, dproj, *, layer):
    t = proj.shape[0]
    w = GM_HEAD_DIM

    def body(dy_ref, u_ref, v_ref, ws_ref, bst_ref, vg_ref, og_ref, dproj_ref,
             dgm_ref, dws_ref, dbst_ref, dvg_ref, dog_ref):
        del dproj_ref

        @pl.when(pl.program_id(0) == 0)
        def _():
            dws_ref[...] = jnp.zeros_like(dws_ref)
            dbst_ref[...] = jnp.zeros_like(dbst_ref)
            dvg_ref[...] = jnp.zeros_like(dvg_ref)
            dog_ref[...] = jnp.zeros_like(dog_ref)

        _, vjp = jax.vjp(_gmlp_chunk, *_gmlp_load(u_ref, v_ref, ws_ref, bst_ref, vg_ref, og_ref, layer))
        dgu, dgv, dws, dbs, dvg, dog = vjp(_hslices(dy_ref, w, GM_HEADS))
        for h in range(GM_HEADS):
            dgm_ref[:, h * w:(h + 1) * w] = _b(dgu[h])
            dgm_ref[:, D_GM + h * w: D_GM + (h + 1) * w] = _b(dgv[h])
            dws_ref[h] += dws[h]
            dbst_ref[:, h:h + 1] += dbs[h]
            dvg_ref[:, h * w:(h + 1) * w] += dvg[h]
            dog_ref[:, h * w:(h + 1) * w] += dog[h]

    return pl.pallas_call(
        body, name="gmlp_bwd", grid=(t // CHUNK,),
        in_specs=[pl.BlockSpec((CHUNK, D_GM), lambda i: (i, 1))] + _gmlp_specs(layer) + [ANY],
        out_specs=[pl.BlockSpec((CHUNK, 2 * D_GM), lambda i: (i, COL_U // (2 * D_GM))),
                   pl.BlockSpec((GM_HEADS, CHUNK, CHUNK), lambda i: (0, 0, 0)),
                   pl.BlockSpec((CHUNK, GM_HEADS), lambda i: (0, 0)),
                   pl.BlockSpec((1, D_GM), lambda i: (0, 0)),
                   pl.BlockSpec((1, D_GM), lambda i: (0, 0))],
        out_shape=[jax.ShapeDtypeStruct(dproj.shape, dproj.dtype), jax.ShapeDtypeStruct((GM_HEADS, CHUNK, CHUNK), F32),
                   jax.ShapeDtypeStruct((CHUNK, GM_HEADS), F32), jax.ShapeDtypeStruct((1, D_GM), F32),
                   jax.ShapeDtypeStruct((1, D_GM), F32)],
        input_output_aliases={7: 0},
        compiler_params=_cp("arbitrary"),
    )(dy, proj, proj, ws, bst, vg, og, dproj)


def _local_step(x, target, mods, w, final_g, *, nseq, big_w, grad_sink, small_sink):
    saved = []
    x0, delta, gate = x, None, None
    h1 = _normmod_fwd(x, w["norm1_g"], mods[0][1], mods[0][0], nseq=nseq, name="norm1_fwd_0", layer=0)
    for l in range(DEPTH):
        sh1, sc1, g1, sh2, sc2, g2 = mods[l]
        w_in = big_w(l, "w_in", h1)
        proj = _matmul(h1, w_in, tb=True, name=f"mm_in_{l}")
        xbc, xbc_pre = _ssd_conv_fwd(proj, w["ssd_conv_w"], w["ssd_conv_b"], nseq=nseq, layer=l)
        ycat, hprev = _ssd_fwd(xbc, proj, w["ssd_dt_bias"], w["ssd_a_log"], w["ssd_d"], w["ssd_norm_g"], nseq=nseq,
                               layer=l)
        ycat = _gmlp_fwd(proj, ycat, w["gm_ws"], w["gm_bst"], w["gm_vnorm_g"], w["gm_out_g"], layer=l)
        w_out = big_w(l, "w_out", ycat)
        mix, x1, h2 = _matmul_normfwd(ycat, w_out, x0, g1, w["norm2_g"], sc2, sh2, nseq=nseq, name=f"mm_out_{l}",
                                      layer=l)
        ff_up = big_w(l, "ff_up", h2)
        up = _matmul(h2, ff_up, tb=True, name=f"mm_up_{l}", out_dtype=BF16)
        act = _ffn_act_fwd(up, w["ff_conv_w"], w["ff_conv_b"], nseq=nseq, layer=l)
        ff_down = big_w(l, "ff_down", act)
        sv = dict(x0=x0, xin_delta=delta, xin_gate=gate, h1=h1, proj=proj, xbc=xbc, xbc_pre=xbc_pre, hprev=hprev,
                  ycat=ycat, mix=mix, x1=x1, h2=h2, up=up, act=act,
                  w_in=w_in, w_out=w_out, ff_up=ff_up, ff_down=ff_down)
        if l + 1 < DEPTH:
            nsh1, nsc1 = mods[l + 1][0], mods[l + 1][1]
            dn, x0, h1 = _matmul_normfwd(act, ff_down, x1, g2, w["norm1_g"], nsc1, nsh1, nseq=nseq,
                                         name=f"mm_down_{l}", layer=l + 1)
        else:
            dn, loss, dx, ddelta, dgate, dfg = _matmul_loss(act, ff_down, x1, g2, final_g, target, nseq=nseq,
                                                            name=f"mm_down_{l}")
        saved.append(dict(sv, dn=dn))
        delta, gate = dn, g2

    small, dmods = [None] * DEPTH, [None] * DEPTH
    for l in reversed(range(DEPTH)):
        sv = saved[l]
        sh1, sc1, g1, sh2, sc2, g2 = mods[l]
        dg2 = dgate
        g_ff_down = _matmul(sv["act"], ddelta, ta=True, name=f"mm_down_dw_{l}", out_dtype=BF16)
        dact = _matmul(ddelta, sv["ff_down"], tb=True, name=f"mm_down_dx_{l}", out_dtype=BF16)
        dgate_ff, dval_ff, dfcw, dfcb = _ffn_act_bwd(dact, sv["up"], w["ff_conv_w"], w["ff_conv_b"], nseq=nseq, layer=l)
        g_ff_up = _matmul([dgate_ff, dval_ff], sv["h2"], ta=True, name=f"mm_up_dw_{l}", out_dtype=BF16)
        dep = grad_sink(l, "ffn", dict(ff_down=g_ff_down, ff_up=g_ff_up), dval_ff)
        dx, dmix, dg1, dn2g, dsc2, dsh2 = _matmul_normbwd([dgate_ff, dval_ff], sv["ff_up"], dx, sv["x1"], sv["mix"], g1,
                                                          w["norm2_g"], sc2, nseq=nseq, name=f"mm_up_dx_{l}", layer=l,
                                                          dep=dep)
        g_w_out = _matmul(sv["ycat"], dmix, ta=True, name=f"mm_out_dw_{l}", out_dtype=BF16)
        dep = grad_sink(l, "w_out", dict(w_out=g_w_out), dmix)
        dycat = _matmul(dmix, sv["w_out"], tb=True, name=f"mm_out_dx_{l}", dep=dep)
        dxbc_act, dproj, ddtb, dalog, ddsk, dng = _ssd_bwd(dycat, sv["xbc"], sv["proj"], sv["hprev"], w["ssd_dt_bias"],
                                                          w["ssd_a_log"], w["ssd_d"], w["ssd_norm_g"], nseq=nseq, layer=l)
        dproj, dscw, dscb = _ssd_conv_bwd(dxbc_act, sv["xbc_pre"], sv["proj"], w["ssd_conv_w"], dproj, nseq=nseq,
                                          layer=l)
        dproj, dws, dbst, dvg, dog = _gmlp_bwd(dycat, sv["proj"], w["gm_ws"], w["gm_bst"], w["gm_vnorm_g"], w["gm_out_g"],
                                               dproj, layer=l)
        early = dict(norm2_g=dn2g, ssd_norm_g=dng, gm_vnorm_g=dvg, gm_out_g=dog,
                     ssd_conv_w=dscw, ssd_conv_b=dscb, ff_conv_w=dfcw, ff_conv_b=dfcb,
                     ssd_dt_bias=ddtb, ssd_a_log=dalog, ssd_d=ddsk, gm_ws=dws, gm_bs=dbst.T)
        dep = small_sink(l, early, small, dmods, dfg, loss)
        g_w_in = _matmul(dproj, sv["h1"], ta=True, name=f"mm_in_dw_{l}", out_dtype=BF16, dep=dep)
        dep = grad_sink(l, "w_in", dict(w_in=g_w_in), dproj)
        dx, ddelta, dgate, dn1g, dsc1, dsh1 = _matmul_normbwd(dproj, sv["w_in"], dx, sv["x0"], sv["xin_delta"],
                                                              sv["xin_gate"], w["norm1_g"], sc1, nseq=nseq,
                                                              name=f"mm_in_dx_{l}", layer=l, dep=dep)
        small[l] = dict(early, norm1_g=dn1g)
        dmods[l] = jnp.concatenate([dsh1, dsc1, dg1, dsh2, dsc2, dg2], axis=-1)[:, 0, :]
    return dx, small, dmods


def _all_gather(arrs, name, dep=None):
    n = len(arrs)
    extra = [] if dep is None else [dep]

    def body(*refs):
        ins, outs = refs[:n], refs[n + len(extra):2 * n + len(extra)]
        send_sems, recv_sems, local_sems = refs[2 * n + len(extra):]
        x, y, c = lax.axis_index("x"), lax.axis_index("y"), lax.axis_index("c")
        me, sibling = (x, y, c), (x, y, 1 - c)
        chips = [(1 - x, y), (x, 1 - y), (1 - x, 1 - y)]

        def copy(i, k, block, to, src=None):
            px, py, pc = block
            dst = outs[i].at[4 * px + 2 * py + pc]
            return pltpu.make_async_remote_copy(
                src_ref=dst if src is None else src, dst_ref=dst,
                send_sem=send_sems.at[7 * i + k], recv_sem=recv_sems.at[7 * i + k],
                device_id=to, device_id_type=MESH)

        mine = [pltpu.make_async_copy(ins[i], outs[i].at[4 * x + 2 * y + c], local_sems.at[i]) for i in range(n)]
        for cp in mine:
            cp.start()
        first = []
        for i in range(n):
            first.append(copy(i, 0, me, sibling, src=ins[i]))
            first += [copy(i, 1 + j, me, (*chip, c), src=ins[i]) for j, chip in enumerate(chips)]
        for cp in first:
            cp.start()
        passed = []
        for j, chip in enumerate(chips):
            for i in range(n):
                copy(i, 1 + j, (*chip, c), me).wait_recv()
                fwd = copy(i, 4 + j, (*chip, c), sibling)
                fwd.start()
                passed.append(fwd)
        for i in range(n):
            copy(i, 0, sibling, me).wait_recv()
            for j, chip in enumerate(chips):
                copy(i, 4 + j, (*chip, 1 - c), me).wait_recv()
        for cp in first + passed:
            cp.wait_send()
        for cp in mine:
            cp.wait()

    return pl.pallas_call(
        body, name=name,
        in_specs=[ANY] * (n + len(extra)), out_specs=[ANY] * n,
        out_shape=[jax.ShapeDtypeStruct((N_DEV,) + a.shape, a.dtype) for a in arrs],
        scratch_shapes=[pltpu.SemaphoreType.DMA((7 * n,)), pltpu.SemaphoreType.DMA((7 * n,)),
                        pltpu.SemaphoreType.DMA((n,))],
    )(*arrs, *extra)


HBM = pl.BlockSpec(memory_space=pltpu.HBM)
SEM = pl.BlockSpec(memory_space=pltpu.SEMAPHORE)
EFFECT = pltpu.SideEffectType.DATAFLOW_SIDE_EFFECTING


def _peer(k):
    x, y, c = lax.axis_index("x"), lax.axis_index("y"), lax.axis_index("c")
    return (1 - x if k & 4 else x, 1 - y if k & 2 else y, 1 - c if k & 1 else c)


ALL_PEERS = tuple(range(1, N_DEV))
OTHER_CHIPS = (2, 4, 6)


def _xc_copies(scatter, srcs, lands, send_sems, recv_sems, peers=ALL_PEERS):
    x, y, c = lax.axis_index("x"), lax.axis_index("y"), lax.axis_index("c")
    copies = []
    for i in range(len(srcs)):
        for k in (peers[i] if isinstance(peers[0], tuple) else peers):
            px, py, pc = _peer(k)
            src = srcs[i].at[4 * px + 2 * py + pc] if scatter else srcs[i]
            dst = lands[i].at[k - 1] if scatter else lands[i].at[4 * x + 2 * y + c]
            copies.append(pltpu.make_async_remote_copy(
                src_ref=src, dst_ref=dst, send_sem=send_sems[i].at[k - 1], recv_sem=recv_sems[i].at[k - 1],
                device_id=(px, py, pc), device_id_type=MESH))
    return copies


def _xc_own(scatter, srcs, lands, send_sems):
    if scatter:
        return []
    me = 4 * lax.axis_index("x") + 2 * lax.axis_index("y") + lax.axis_index("c")
    return [pltpu.make_async_copy(srcs[i], lands[i].at[me], send_sems[i].at[N_DEV - 1]) for i in range(len(srcs))]


def _xc_start(scatter, arrs, after, name, peers=ALL_PEERS):
    n = len(arrs)
    lands = [lax.empty((N_DEV - 1,) + a.shape[1:] if scatter else (N_DEV,) + a.shape, a.dtype) for a in arrs]

    def body(*refs):
        srcs, lnd = refs[:n], refs[n:2 * n]
        send_sems, recv_sems = refs[2 * n + 1:3 * n + 1], refs[3 * n + 1:4 * n + 1]
        token = refs[6 * n + 1]
        for cp in _xc_copies(scatter, srcs, lnd, send_sems, recv_sems, peers) + _xc_own(scatter, srcs, lnd, send_sems):
            cp.start()
        token[...] = jnp.zeros_like(token)

    outs = pl.pallas_call(
        body, name=name,
        out_shape=[pltpu.SemaphoreType.DMA((N_DEV,))] * (2 * n)
        + [pltpu.HBM(a.shape, a.dtype) for a in arrs] + [pltpu.HBM(a.shape, a.dtype) for a in lands]
        + [jax.ShapeDtypeStruct((8, 128), F32)],
        in_specs=[HBM] * (2 * n) + [ANY],
        out_specs=[SEM] * (2 * n) + [HBM] * (2 * n) + [pl.BlockSpec(memory_space=pltpu.VMEM)],
        input_output_aliases={i: 2 * n + i for i in range(2 * n)},
        compiler_params=pltpu.CompilerParams(has_side_effects=EFFECT),
    )(*[pltpu.with_memory_space_constraint(a, pltpu.HBM) for a in list(arrs) + lands], after)
    return outs[:n], outs[n:2 * n], outs[2 * n:3 * n], outs[3 * n:4 * n], outs[4 * n][0, 0]


def _xc_wait(scatter, send_sems, recv_sems, srcs, lands, after, name, peers=ALL_PEERS):
    n = len(srcs)

    def body(*refs):
        s_refs, l_refs = refs[:n], refs[n:2 * n]
        ss, rs = refs[2 * n:3 * n], refs[3 * n:4 * n]
        for cp in _xc_copies(scatter, s_refs, l_refs, ss, rs, peers):
            cp.wait_send()
            cp.wait_recv()
        for cp in _xc_own(scatter, s_refs, l_refs, ss):
            cp.wait()

    outs = pl.pallas_call(
        body, name=name,
        out_shape=[pltpu.HBM(a.shape, a.dtype) for a in list(srcs) + list(lands)],
        in_specs=[HBM] * (2 * n) + [SEM] * (2 * n) + [ANY],
        out_specs=[HBM] * (2 * n),
        input_output_aliases={i: i for i in range(2 * n)},
        compiler_params=pltpu.CompilerParams(has_side_effects=EFFECT),
    )(*srcs, *lands, *send_sems, *recv_sems, after)
    return outs[:n], outs[n:]


def _sib_copies(zones, send_sems, recv_sems):
    x, y, c = lax.axis_index("x"), lax.axis_index("y"), lax.axis_index("c")
    copies = []
    for i in range(len(zones)):
        for q in range(N_DEV // 2):
            slot = zones[i].at[2 * q + c]
            copies.append(pltpu.make_async_remote_copy(
                src_ref=slot, dst_ref=slot, send_sem=send_sems[i].at[q], recv_sem=recv_sems[i].at[q],
                device_id=(x, y, 1 - c), device_id_type=MESH))
    return copies


def _sib_start(zones, name):
    n = len(zones)

    def body(*refs):
        for cp in _sib_copies(refs[:n], refs[n:2 * n], refs[2 * n:3 * n]):
            cp.start()

    outs = pl.pallas_call(
        body, name=name,
        out_shape=[pltpu.SemaphoreType.DMA((N_DEV // 2,))] * (2 * n) + [pltpu.HBM(a.shape, a.dtype) for a in zones],
        in_specs=[HBM] * n,
        out_specs=[SEM] * (2 * n) + [HBM] * n,
        input_output_aliases={i: 2 * n + i for i in range(n)},
        compiler_params=pltpu.CompilerParams(has_side_effects=EFFECT),
    )(*[pltpu.with_memory_space_constraint(a, pltpu.HBM) for a in zones])
    return outs[:n], outs[n:2 * n], outs[2 * n:]


def _sib_wait(send_sems, recv_sems, zones, name):
    n = len(zones)

    def body(*refs):
        for cp in _sib_copies(refs[:n], refs[n:2 * n], refs[2 * n:3 * n]):
            cp.wait_send()
            cp.wait_recv()

    return pl.pallas_call(
        body, name=name,
        out_shape=[pltpu.HBM(a.shape, a.dtype) for a in zones],
        in_specs=[HBM] * n + [SEM] * (2 * n),
        out_specs=[HBM] * n,
        input_output_aliases={i: i for i in range(n)},
        compiler_params=pltpu.CompilerParams(has_side_effects=EFFECT),
    )(*zones, *send_sems, *recv_sems)


def _adamw_math(w, g, m, v):
    m = ADAM_B1 * m + (1.0 - ADAM_B1) * g
    v = ADAM_B2 * v + (1.0 - ADAM_B2) * (g * g)
    m_hat = m / (1.0 - ADAM_B1 ** ADAM_STEP)
    v_hat = v / (1.0 - ADAM_B2 ** ADAM_STEP)
    delta = -ADAM_LR * (m_hat / (jnp.sqrt(v_hat) + ADAM_EPS) + ADAM_WD * w)
    return delta, m, v


def _adamw_sharded(parts, w, m, v, pos, name):
    depth, rows, cols = w.shape
    tr = _tile(rows, 256) if rows % 8 == 0 else rows
    npart = len(parts)

    def body(pos_ref, *refs):
        prefs = refs[:npart]
        w_ref, m_ref, v_ref, g_out, d_out, m_out, v_out = refs[npart:]
        g = prefs[0][...]
        for pr in prefs[1:]:
            g = g + pr[...]
        delta, mn, vn = _adamw_math(w_ref[...], g, m_ref[...], v_ref[...])
        g_out[...] = g
        d_out[...] = delta
        m_out[...] = mn
        v_out[...] = vn

    def part_spec(fn):
        return pl.BlockSpec((1, tr, cols), lambda l, i, p: (fn(p) * depth + l, i, 0))

    blk = pl.BlockSpec((1, tr, cols), lambda l, i, p: (l, i, 0))
    shp = jax.ShapeDtypeStruct((depth, rows, cols), F32)
    return pl.pallas_call(
        body, name=name,
        grid_spec=pltpu.PrefetchScalarGridSpec(
            num_scalar_prefetch=1, grid=(depth, rows // tr),
            in_specs=[part_spec(fn) for _, fn in parts] + [blk, blk, blk],
            out_specs=[blk, blk, blk, blk]),
        out_shape=[shp, shp, shp, shp],
        compiler_params=_cp("parallel", "parallel"),
    )(pos, *[a for a, _ in parts], w, m, v)


def _adamw_layer(parts, w, m, v, pos, layer, prev, name):
    depth, rows, cols = w.shape
    npart = len(parts)
    nprev = 0 if prev is None else 4
    if rows % 16 == 0:
        tr, tc = max(t for t in range(16, 257, 16) if rows % t == 0), cols
    else:
        tr, tc = rows, _tile(cols, 256)
    pick = (lambda i: (i, 0)) if rows % 16 == 0 else (lambda i: (0, i))

    def body(pos_ref, *refs):
        prefs = refs[:npart]
        w_ref, m_ref, v_ref = refs[npart:npart + 3]
        g_out, d_out, m_out, v_out = refs[npart + 3 + nprev:]
        g = prefs[0][...].astype(F32)
        for pr in prefs[1:]:
            g = g + pr[...].astype(F32)
        delta, mn, vn = _adamw_math(w_ref[...], g, m_ref[...], v_ref[...])
        g_out[...] = g
        d_out[...] = delta
        m_out[...] = mn
        v_out[...] = vn

    def part_spec(fn):
        return pl.BlockSpec((1, tr, tc), lambda i, p: (fn(p), *pick(i)))

    blk = pl.BlockSpec((1, tr, tc), lambda i, p: (layer, *pick(i)))
    shp = jax.ShapeDtypeStruct((depth, rows, cols), F32)
    first_prev = 1 + npart + 3
    return pl.pallas_call(
        body, name=name,
        grid_spec=pltpu.PrefetchScalarGridSpec(
            num_scalar_prefetch=1, grid=(rows // tr * (cols // tc),),
            in_specs=[part_spec(fn) for _, fn in parts] + [blk, blk, blk] + [ANY] * nprev,
            out_specs=[blk, blk, blk, blk]),
        out_shape=[shp, shp, shp, shp],
        input_output_aliases={first_prev + j: j for j in range(nprev)},
        compiler_params=_cp("parallel"),
    )(pos, *[a for a, _ in parts], w, m, v, *(prev or ()))


def _adamw_rows_major(parts_by_layer, w, m, v, pos, name):
    rows, depth, cols = w.shape
    tc = _tile(cols, 256)
    npart = len(parts_by_layer[0])

    def body(pos_ref, *refs):
        prefs = refs[:depth * npart]
        w_ref, m_ref, v_ref, g_out, d_out, m_out, v_out = refs[depth * npart:]
        for l in range(depth):
            g = prefs[l * npart][0].astype(F32)
            for pr in prefs[l * npart + 1:(l + 1) * npart]:
                g = g + pr[0].astype(F32)
            delta, mn, vn = _adamw_math(w_ref[:, l, :], g, m_ref[:, l, :], v_ref[:, l, :])
            g_out[:, l, :] = g
            d_out[:, l, :] = delta
            m_out[:, l, :] = mn
            v_out[:, l, :] = vn

    def part_spec(fn):
        return pl.BlockSpec((1, rows, tc), lambda j, p: (fn(p), 0, j))

    blk = pl.BlockSpec((rows, depth, tc), lambda j, p: (0, 0, j))
    shp = jax.ShapeDtypeStruct(w.shape, F32)
    flat = [pf for parts in parts_by_layer for pf in parts]
    return pl.pallas_call(
        body, name=name,
        grid_spec=pltpu.PrefetchScalarGridSpec(
            num_scalar_prefetch=1, grid=(cols // tc,),
            in_specs=[part_spec(fn) for _, fn in flat] + [blk, blk, blk],
            out_specs=[blk, blk, blk, blk]),
        out_shape=[shp, shp, shp, shp],
        compiler_params=_cp("parallel"),
    )(pos, *[a for a, _ in flat], w, m, v)


_P1024 = ["norm1_g", "norm2_g", "ssd_norm_g", "gm_vnorm_g", "gm_out_g"]
_P16 = ["ssd_dt_bias", "ssd_a_log", "ssd_d"]


def _adamw_small(gath, wmv):
    names = list(wmv.keys())
    classes = list(gath.keys())
    flat_in = [gath[k] for k in classes]
    for nme in names:
        flat_in += list(wmv[nme])
    out_shapes = []
    for nme in names:
        out_shapes += [jax.ShapeDtypeStruct(wmv[nme][0].shape, F32)] * 4
    out_shapes += [jax.ShapeDtypeStruct((DEPTH, SSD_CONV, CONV_DIM), F32), jax.ShapeDtypeStruct((DEPTH, FF_CONV, D_FF), F32),
                   jax.ShapeDtypeStruct((1, SSD_HEADS), F32)]
    scratch = [pltpu.VMEM(gath[k].shape[1:], F32) for k in classes]
    ncls = len(classes)

    def body(*refs):
        g_refs = dict(zip(classes, refs[:ncls]))
        pos = ncls
        w_refs = {}
        for nme in names:
            w_refs[nme] = refs[pos:pos + 3]
            pos += 3
        o_refs = {}
        for nme in names:
            o_refs[nme] = refs[pos:pos + 4]
            pos += 4
        scw_out, fcw_out, loss_out = refs[pos], refs[pos + 1], refs[pos + 2]
        s_refs = dict(zip(classes, refs[pos + 3:]))
        for k in classes:
            acc = g_refs[k][0]
            for dev in range(1, N_DEV):
                acc = acc + g_refs[k][dev]
            s_refs[k][...] = acc

        def apply(nme, grad_of):
            w_ref, m_ref, v_ref = w_refs[nme]
            g_out, d_out, m_out, v_out = o_refs[nme]
            shape = w_ref.shape
            if len(shape) == 2:
                idxs = [(slice(l, l + 1),) for l in range(shape[0])]
            elif len(shape) == 3:
                idxs = [(l,) for l in range(shape[0])]
            else:
                idxs = [(l, h) for l in range(shape[0]) for h in range(shape[1])]
            for n_i, ix in enumerate(idxs):
                g = grad_of(n_i)
                delta, mn, vn = _adamw_math(w_ref[ix], g, m_ref[ix], v_ref[ix])
                g_out[ix] = g
                d_out[ix] = delta
                m_out[ix] = mn
                v_out[ix] = vn

        s1024, s1536, s2816, s16, s128, s6144, late1024, late6144 = (s_refs[k] for k in classes)
        s1024[0:1, :] += late1024[...]
        s6144[0:late6144.shape[0], :] += late6144[...]
        for n_i, nme in enumerate(_P1024):
            apply(nme, lambda l, b=2 * n_i: s1024[b + l:b + l + 1, :])
        apply("final_g", lambda l: s1024[10:11, :])
        apply("ssd_conv_b", lambda l: s1536[8 + l:9 + l, :])
        apply("ff_conv_b", lambda l: s2816[6 + l:7 + l, :])
        for n_i, nme in enumerate(_P16):
            apply(nme, lambda l, b=2 * n_i: s16[b + l:b + l + 1, :])
        apply("gm_ws", lambda q: s128[q * CHUNK:(q + 1) * CHUNK, :])
        apply("gm_bs", lambda l: s128[2048 + 8 * l:2048 + 8 * (l + 1), :])
        apply("ada_b", lambda l: s6144[2 * l:2 * l + 1, :] + s6144[2 * l + 1:2 * l + 2, :])
        for l in range(DEPTH):
            scw_out[l] = s1536[SSD_CONV * l:SSD_CONV * (l + 1), :]
            fcw_out[l] = s2816[FF_CONV * l:FF_CONV * (l + 1), :]
        loss_out[...] = s16[2 * len(_P16):2 * len(_P16) + 1, :]

    outs = pl.pallas_call(
        body, name="adamw_small",
        out_shape=out_shapes,
        scratch_shapes=scratch,
        compiler_params=pltpu.CompilerParams(vmem_limit_bytes=VMEM_LIMIT),
    )(*flat_in)
    res = {nme: tuple(outs[4 * i:4 * i + 4]) for i, nme in enumerate(names)}
    return res, outs[-3], outs[-2], outs[-1]


_WEIGHTS = ['ada_w', 'ada_b', 'norm1_g', 'norm2_g', 'w_in', 'ssd_conv_w', 'ssd_conv_b', 'ssd_dt_bias', 'ssd_a_log',
            'ssd_d', 'ssd_norm_g', 'gm_vnorm_g', 'gm_ws', 'gm_bs', 'gm_out_g', 'w_out', 'ff_up', 'ff_conv_w',
            'ff_conv_b', 'ff_down', 'final_g']


_O_XBC, _O_DT, _O_GM = D_SSD, D_SSD + CONV_DIM, D_SSD + CONV_DIM + SSD_HEADS


_TRANSPOSED = ("w_in", "ff_up")


def _full_weight(name, g):
    if name != "w_in":
        return g.reshape(g.shape[0] * g.shape[1], g.shape[2])
    rows = g.shape[1]
    k, r = divmod(_O_GM, rows)
    at = lambda j: COL_Z + j * rows if j <= k else j * rows - _O_GM

    def body(g_ref, o_ref):
        for j in range(N_DEV):
            if j == k:
                o_ref[at(k):at(k) + r, :] = g_ref[k, :r, :]
                o_ref[0:rows - r, :] = g_ref[k, r:, :]
            else:
                o_ref[at(j):at(j) + rows, :] = g_ref[j]
        o_ref[N_IN:, :] = jnp.zeros((N_INP - N_IN, g.shape[2]), g.dtype)

    return pl.pallas_call(
        body, name="w_in_rows", out_shape=jax.ShapeDtypeStruct((N_INP, g.shape[2]), g.dtype),
        compiler_params=pltpu.CompilerParams(vmem_limit_bytes=VMEM_LIMIT),
    )(g)


def _by_owner(name, grad):
    if name != "w_in":
        return grad.reshape(N_DEV, grad.shape[0] // N_DEV, grad.shape[1])
    rows = N_IN // N_DEV
    k, r = divmod(_O_GM, rows)
    at = lambda j: COL_Z + j * rows if j <= k else j * rows - _O_GM
    blocks = [grad[at(j):at(j) + rows] for j in range(N_DEV)]
    blocks[k] = jnp.concatenate([grad[at(k):at(k) + r], grad[:rows - r]], axis=0)
    return jnp.stack(blocks)


def kernel(x, c, ada_w, ada_b, norm1_g, norm2_g, w_in, ssd_conv_w, ssd_conv_b, ssd_dt_bias, ssd_a_log, ssd_d, ssd_norm_g, gm_vnorm_g, gm_ws, gm_bs, gm_out_g, w_out, ff_up, ff_conv_w, ff_conv_b, ff_down, final_g, loss_target, m_ada_w, m_ada_b, m_norm1_g, m_norm2_g, m_w_in, m_ssd_conv_w, m_ssd_conv_b, m_ssd_dt_bias, m_ssd_a_log, m_ssd_d, m_ssd_norm_g, m_gm_vnorm_g, m_gm_ws, m_gm_bs, m_gm_out_g, m_w_out, m_ff_up, m_ff_conv_w, m_ff_conv_b, m_ff_down, m_final_g, v_ada_w, v_ada_b, v_norm1_g, v_norm2_g, v_w_in, v_ssd_conv_w, v_ssd_conv_b, v_ssd_dt_bias, v_ssd_a_log, v_ssd_d, v_ssd_norm_g, v_gm_vnorm_g, v_gm_ws, v_gm_bs, v_gm_out_g, v_w_out, v_ff_up, v_ff_conv_w, v_ff_conv_b, v_ff_down, v_final_g):
    given = dict(locals())
    wts = {n: given[n] for n in _WEIGHTS}
    mom = {n: given["m_" + n] for n in _WEIGHTS}
    var = {n: given["v_" + n] for n in _WEIGHTS}
    nseq, seq, d = x.shape
    ix, iy, ic = lax.axis_index("x"), lax.axis_index("y"), lax.axis_index("c")
    me = 4 * ix + 2 * iy + ic
    me_arr = me.astype(jnp.int32).reshape(1)

    for nme, perm in (("ff_up", (0, 2, 1)), ("w_in", (2, 0, 1))):
        wts[nme], mom[nme], var[nme] = (jnp.transpose(a, perm) for a in (wts[nme], mom[nme], var[nme]))

    def shard(l, name):
        return _b(wts[name][:, l, :] if name == "w_in" else wts[name][l])

    g_scw, g_fcw, c_all = _all_gather([ssd_conv_w, ff_conv_w, c], "gather_first")
    scw_f = jnp.transpose(g_scw, (1, 2, 0, 3)).reshape(DEPTH, SSD_CONV, CONV_DIM)
    fcw_f = jnp.transpose(g_fcw, (1, 2, 0, 3)).reshape(DEPTH, FF_CONV, D_FF)
    c_all = c_all.reshape(N_DEV * nseq, d)

    n_ada = ada_w.shape[2]
    ada_b_shard = lax.dynamic_slice_in_dim(ada_b, me * n_ada, n_ada, axis=1).reshape(DEPTH, 1, n_ada)
    mod_part, c_act = _ada_fwd(c_all, ada_w, ada_b_shard)
    first_ssem, first_rsem, first_src, first_land, first_zero = _xc_start(
        False, [mod_part, shard(0, "w_in")], c_act, "ag_first_start", peers=[ALL_PEERS, OTHER_CHIPS])
    _, (mod_g,) = _xc_wait(False, first_ssem[:1], first_rsem[:1], first_src[:1], first_land[:1], c_act,
                           "mod_wait")
    mod_all = jnp.transpose(mod_g, (1, 2, 0, 3)).reshape(DEPTH, N_DEV * nseq, N_MOD * d)
    mod_mine = lax.dynamic_slice_in_dim(mod_all, me * nseq, nseq, axis=1)
    mod_k = jnp.transpose(mod_mine.reshape(DEPTH, nseq, N_MOD, 1, d), (0, 2, 1, 3, 4))
    mods = [[mod_k[l, k] for k in range(N_MOD)] for l in range(DEPTH)]

    later =[(0, "w_out"), (0, "ff_up"), (0, "ff_down"), (1, "w_in"), (1, "w_out"), (1, "ff_up"), (1, "ff_down")]
    ag_groups = {(0, "w_out"): [0], (0, "ff_up"): [1, 2], (1, "w_in"): [3, 4], (1, "ff_up"): [5, 6]}
    big_cache, ag = {}, {}

    def big_w(l, name, after):
        if (l, name) == (0, "w_in") and (l, name) not in big_cache:
            ag["ssem"], ag["rsem"], ag["src"], ag["land"], started = _xc_start(
                False, [shard(l2, n2) for l2, n2 in later], after, "ag_start")
            _, zones = _xc_wait(False, first_ssem[1:], first_rsem[1:], first_src[1:], first_land[1:],
                                jnp.full((8, 128), started, F32), "ag_first_wait", peers=OTHER_CHIPS)
            (zone,) = _sib_wait(*_sib_start(zones, "ag_first_sib_start"), "ag_first_sib_wait")
            big_cache[(l, name)] = _full_weight(name, zone)
        if (l, name) not in big_cache:
            idx = ag_groups[(l, name)]
            pick = lambda seq_: [seq_[i] for i in idx]
            _, lands = _xc_wait(False, pick(ag["ssem"]), pick(ag["rsem"]), pick(ag["src"]), pick(ag["land"]), after,
                                f"ag_wait_{l}_{name}")
            for i, land in zip(idx, lands):
                big_cache[later[i]] = _full_weight(later[i][1], land)
        return big_cache[(l, name)]

    small_w = dict(
        norm1_g=norm1_g + first_zero, norm2_g=norm2_g, ssd_conv_w=scw_f, ssd_conv_b=ssd_conv_b, ssd_dt_bias=ssd_dt_bias,
        ssd_a_log=ssd_a_log, ssd_d=ssd_d, ssd_norm_g=ssd_norm_g, gm_vnorm_g=gm_vnorm_g, gm_ws=gm_ws,
        gm_bst=jnp.transpose(gm_bs, (0, 2, 1)), gm_out_g=gm_out_g, ff_conv_w=fcw_f, ff_conv_b=ff_conv_b)

    outs = {}
    pending, win_parts = {}, {}

    def rs_finish(l, group, after):
        names, ssem, rsem, srcs, lands = pending.pop((l, group))
        srcs, lands = _xc_wait(True, ssem, rsem, srcs, lands, after, f"rs_wait_{l}_{group}")
        for nme, own, land in zip(names, srcs, lands):
            parts = [(own, lambda p: p[0])] + [(land, lambda p, k=k: k) for k in range(N_DEV - 1)]
            if nme == "w_in":
                win_parts[l] = parts
                if len(win_parts) == DEPTH:
                    outs[nme] = _adamw_rows_major([win_parts[k] for k in range(DEPTH)], wts[nme], mom[nme], var[nme],
                                                  me_arr, "adamw_w_in")
                continue
            outs[nme] = _adamw_layer(parts, wts[nme], mom[nme], var[nme], me_arr, l, outs.get(nme), f"adamw_{nme}_{l}")
        return land if names[-1] == "w_in" else outs[names[-1]][0]

    def grad_sink(l, group, grads, after):
        names = list(grads)
        ssem, rsem, srcs, lands, zero = _xc_start(True, [_by_owner(n, grads[n]) for n in names], after, f"rs_start_{l}_{group}")
        pending[(l, group)] = (names, ssem, rsem, srcs, lands)
        return zero.reshape(1, 1)

    early_gather = {}

    def small_sink(l, early, small, dmods, dfg, loss_p):
        if l > 0:
            return None
        layers = [dict(early, norm1_g=jnp.zeros((1, d), F32))] + small[1:]
        rows = lambda name: [layers[k][name] for k in range(DEPTH)]
        packed = [
            jnp.concatenate(sum([rows(n) for n in _P1024], []) + [dfg], axis=0),
            jnp.concatenate(rows("ssd_conv_w") + rows("ssd_conv_b"), axis=0),
            jnp.concatenate(rows("ff_conv_w") + rows("ff_conv_b"), axis=0),
            jnp.concatenate(sum([rows(n) for n in _P16], []) + [loss_p[:, :SSD_HEADS]], axis=0),
            jnp.concatenate([layers[k]["gm_ws"].reshape(GM_HEADS * CHUNK, CHUNK) for k in range(DEPTH)] + rows("gm_bs"), axis=0),
            jnp.concatenate([jnp.zeros((nseq, N_MOD * d), F32)] + dmods[1:], axis=0)]
        ssem, rsem, srcs, lands, zero = _xc_start(False, packed, packed[0], "small_start")
        early_gather.update(ssem=ssem, rsem=rsem, srcs=srcs, lands=lands)
        return zero.reshape(1, 1)

    grad_x, small, dmods = _local_step(
        x.reshape(nseq * seq, d), loss_target.reshape(nseq * seq, d), mods, small_w, final_g.reshape(1, d), nseq=nseq,
        big_w=big_w, grad_sink=grad_sink, small_sink=small_sink)

    done = grad_x
    for l, grp in ((1, "ffn"), (1, "w_out"), (1, "w_in"), (0, "ffn"), (0, "w_out")):
        done = rs_finish(l, grp, done)
    _, gathered = _xc_wait(False, early_gather["ssem"], early_gather["rsem"], early_gather["srcs"],
                           early_gather["lands"], done, "small_wait")
    gathered = list(gathered)
    gathered += _all_gather([small[0]["norm1_g"], dmods[0]], "gather_late", dep=gathered[0])
    gath = dict(zip(["p1024", "p1536", "p2816", "p16", "p128", "p6144", "late1024", "late6144"], gathered))

    dmod_all = jnp.concatenate([gath["late6144"].reshape(1, N_DEV * nseq, N_MOD * d),
                                jnp.transpose(gath["p6144"].reshape(N_DEV, DEPTH, nseq, N_MOD * d)[:, 1:], (1, 0, 2, 3)).reshape(
                                    DEPTH - 1, N_DEV * nseq, N_MOD * d)], axis=0)
    small_names = _P1024 + ["final_g", "ssd_conv_b", "ff_conv_b"] + _P16 + ["gm_ws", "gm_bs", "ada_b"]
    wmv = {}
    for nme in small_names:
        if nme == "final_g":
            wmv[nme] = tuple(a.reshape(1, d) for a in (wts[nme], mom[nme], var[nme]))
        else:
            wmv[nme] = (wts[nme], mom[nme], var[nme])
    small_out, scw_full, fcw_full, loss_sum = _adamw_small(gath, wmv)
    loss = loss_sum[0, 0]
    rs_finish(0, "w_in", scw_full)
    for nme in small_names:
        outs[nme] = small_out[nme]
    outs["final_g"] = tuple(a.reshape(d) for a in outs["final_g"])

    n_scw, n_fcw = ssd_conv_w.shape[2], ff_conv_w.shape[2]
    g_scw_mine = lax.dynamic_slice_in_dim(scw_full, me * n_scw, n_scw, axis=2)
    g_fcw_mine = lax.dynamic_slice_in_dim(fcw_full, me * n_fcw, n_fcw, axis=2)
    outs["ssd_conv_w"] = _adamw_sharded([(g_scw_mine, lambda p: 0)], ssd_conv_w, m_ssd_conv_w, v_ssd_conv_w, me_arr, "adamw_ssd_conv_w")
    outs["ff_conv_w"] = _adamw_sharded([(g_fcw_mine, lambda p: 0)], ff_conv_w, m_ff_conv_w, v_ff_conv_w, me_arr, "adamw_ff_conv_w")

    dmod_cols = _b(lax.dynamic_slice_in_dim(dmod_all, me * n_ada, n_ada, axis=2))
    g_ada = jnp.stack([_matmul(c_act, dmod_cols[l], ta=True, name=f"mm_ada_dw_{l}") for l in range(DEPTH)])
    outs["ada_w"] = _adamw_sharded([(g_ada, lambda p: 0)], ada_w, m_ada_w, v_ada_w, me_arr, "adamw_ada_w")

    for nme, perm in (("ff_up", (0, 2, 1)), ("w_in", (1, 2, 0))):
        outs[nme] = tuple(jnp.transpose(a, perm) for a in outs[nme])
    result = [loss, grad_x.reshape(nseq, seq, d)]
    for k in range(4):
        result += [outs[n][k] for n in _WEIGHTS]
    return tuple(result)
```

```python
import functools
import math

import jax
import jax.numpy as jnp
from jax import lax
from jax.experimental import pallas as pl
from jax.experimental.pallas import tpu as pltpu

F32 = jnp.float32
BF16 = jnp.bfloat16

N_DEV = 8
D_MODEL = 1024
DEPTH = 2
CHUNK = 128
SSD_HEADS = 16
SSD_HEAD_DIM = 64
SSD_GROUPS = 2
HEADS_PER_GROUP = SSD_HEADS // SSD_GROUPS
GROUP_WIDTH = HEADS_PER_GROUP * SSD_HEAD_DIM
D_STATE = 128
D_SSD = 1024
CONV_DIM = 1536
SSD_CONV = 4
GM_HEADS = 8
GM_HEAD_DIM = 128
D_GM = 1024
D_FF = 2816
FF_CONV = 3
N_IN = 4624
N_MOD = 6
EPS = 1e-6

N_INP = 5120
COL_U, COL_V, COL_Z, COL_XBC, COL_DT = 0, 1024, 2048, 3072, 4608

ADAM_LR = 0.001
ADAM_B1 = 0.9
ADAM_B2 = 0.999
ADAM_EPS = 1e-08
ADAM_WD = 0.01
ADAM_STEP = 10

VMEM_LIMIT = 56 * 1024 * 1024
MESH = pl.DeviceIdType.MESH
ANY = pl.BlockSpec(memory_space=pl.ANY)


def _cp(*sem):
    return pltpu.CompilerParams(dimension_semantics=sem, vmem_limit_bytes=VMEM_LIMIT)


def _tile(n, pref):
    if n <= pref or n % 128:
        return n
    best = 128
    for t in range(128, pref + 1, 128):
        if n % t == 0:
            best = t
    return best


def _per_layer(n):
    return pl.BlockSpec((DEPTH, n), lambda *_: (0, 0))


def _row(ref, layer, cols=slice(None)):
    return ref[layer:layer + 1, cols]


def _silu(x):
    return x * jax.nn.sigmoid(x)


def _gelu(x):
    return 0.5 * x * (1.0 + lax.erf(x * (1.0 / math.sqrt(2.0))))


def _softplus(x):
    return jnp.maximum(x, 0.0) + jnp.log1p(jnp.exp(-jnp.abs(x)))


def _b(x):
    return x.astype(BF16)


_NN = (((1,), (0,)), ((), ()))
_NT = (((1,), (1,)), ((), ()))
_TN = (((0,), (0,)), ((), ()))


def _dg(a, b, dn):
    return lax.dot_general(_b(a), _b(b), dn, preferred_element_type=F32)


@jax.custom_vjp
def _bdot(a, b):
    return _dg(a, b, _NN)


def _bdot_fwd(a, b):
    return _dg(a, b, _NN), (a, b)


def _bdot_bwd(res, ct):
    a, b = res
    return _dg(ct, b, _NT), _dg(a, ct, _TN)


_bdot.defvjp(_bdot_fwd, _bdot_bwd)


@jax.custom_vjp
def _bdot_nt(a, b):
    return _dg(a, b, _NT)


def _bdot_nt_fwd(a, b):
    return _dg(a, b, _NT), (a, b)


def _bdot_nt_bwd(res, ct):
    a, b = res
    return _dg(ct, b, _NN), _dg(ct, a, _TN)


_bdot_nt.defvjp(_bdot_nt_fwd, _bdot_nt_bwd)


@jax.custom_vjp
def _bdot_tn(a, b):
    return _dg(a, b, _TN)


def _bdot_tn_fwd(a, b):
    return _dg(a, b, _TN), (a, b)


def _bdot_tn_bwd(res, ct):
    a, b = res
    return _dg(b, ct, _NT), _dg(a, ct, _NN)


_bdot_tn.defvjp(_bdot_tn_fwd, _bdot_tn_bwd)


def _tri(n, lower):
    r = lax.broadcasted_iota(jnp.int32, (n, n), 0)
    c = lax.broadcasted_iota(jnp.int32, (n, n), 1)
    return ((r >= c) if lower else (r <= c)).astype(F32)


def _eye(n):
    r = lax.broadcasted_iota(jnp.int32, (n, n), 0)
    c = lax.broadcasted_iota(jnp.int32, (n, n), 1)
    return (r == c).astype(F32)


def _hdot(a, b, dn):
    return lax.dot_general(a, b, dn, precision=lax.Precision.HIGHEST, preferred_element_type=F32)


@jax.custom_vjp
def _cumsum_rows(x):
    return _hdot(_tri(x.shape[0], True), x, _NN)


def _cumsum_rows_fwd(x):
    return _cumsum_rows(x), None


def _cumsum_rows_bwd(_, ct):
    return (_hdot(_tri(ct.shape[0], False), ct, _NN),)


_cumsum_rows.defvjp(_cumsum_rows_fwd, _cumsum_rows_bwd)


@jax.custom_vjp
def _transpose(x):
    return _hdot(_eye(x.shape[1]), x, _NT)


def _transpose_fwd(x):
    return _transpose(x), None


def _transpose_bwd(_, ct):
    return (_hdot(_eye(ct.shape[1]), ct, _NT),)


_transpose.defvjp(_transpose_fwd, _transpose_bwd)


MXU_WIDTH = 256
MATMUL_TILE_CAP = 2816
MATMUL_VMEM = 44 * 1024 * 1024


def _mxu_tiles(n):
    if n <= MATMUL_TILE_CAP or n % 128:
        return [n]
    for unit in (MXU_WIDTH, 128):
        opts = [t for t in range(unit, MATMUL_TILE_CAP + 1, unit) if n % t == 0]
        if opts:
            return opts
    return [n]


def _matmul(a, b, *, ta=False, tb=False, name, dep=None, out_dtype=F32):
    pieces = list(a) if isinstance(a, (list, tuple)) else [a]
    npc = len(pieces)
    rows, width = pieces[0].shape
    assert all(p.shape == (rows, width) for p in pieces)
    if ta:
        k_dim, m_dim = rows, width * npc
    else:
        m_dim, k_dim = rows, width * npc
    if tb:
        n_dim, kb = b.shape
    else:
        kb, n_dim = b.shape
    assert kb == k_dim, (pieces[0].shape, npc, b.shape, ta, tb)
    m_unit = width if npc > 1 and ta else m_dim
    k_unit = width if npc > 1 and not ta else k_dim
    tm = _tile(m_unit, 1536)
    tn_opts, tk_opts = _mxu_tiles(n_dim), _mxu_tiles(k_unit)
    tn, tk = tn_opts.pop(), tk_opts.pop()
    while 4 * (tm * tk + tk * tn) + 8 * tm * tn > MATMUL_VMEM:
        if tn >= tk and tn_opts:
            tn = tn_opts.pop()
        else:
            tk = tk_opts.pop()
    ni, nj, nk = m_dim // tm, n_dim // tn, k_dim // tk
    per = width // (tm if ta else tk)
    dn = (((0 if ta else 1,), (1 if tb else 0,)), ((), ()))

    a_bytes, b_bytes = m_dim * k_dim, k_dim * n_dim
    m_outer = nk > 1 or a_bytes + b_bytes * ni <= b_bytes + a_bytes * nj
    if m_outer:
        ij = lambda o, n, k: (o, n)
        grid = (ni, nj, nk)
    else:
        ij = lambda o, n, k: (n, o)
        grid = (nj, ni, nk)

    use_acc = nk > 1 and out_dtype != F32

    def body(*refs):
        a_refs, b_ref = refs[:npc], refs[npc]
        o_ref = refs[-2] if use_acc else refs[-1]
        acc_ref = refs[-1]
        k = pl.program_id(2)
        i = pl.program_id(0 if m_outer else 1)
        along = i if ta else k

        def step(a_ref):
            p = lax.dot_general(a_ref[...], b_ref[...], dn, preferred_element_type=F32)
            if nk == 1:
                o_ref[...] = p.astype(out_dtype)
            else:
                @pl.when(k == 0)
                def _():
                    acc_ref[...] = p

                @pl.when((k > 0) & (k < nk - 1 if use_acc else True))
                def _():
                    acc_ref[...] += p

                if use_acc:
                    @pl.when(k == nk - 1)
                    def _():
                        o_ref[...] = (acc_ref[...] + p).astype(out_dtype)

        if npc == 1:
            step(a_refs[0])
        else:
            for pc in range(npc):
                pl.when((along >= pc * per) & (along < (pc + 1) * per))(functools.partial(step, a_refs[pc]))

    def a_map(pc, o, n, k):
        i, _ = ij(o, n, k)
        along = i if ta else k
        if npc > 1:
            along = jnp.clip(along - pc * per, 0, per - 1)
        return (k, along) if ta else (i, along)

    def b_map(o, n, k):
        _, j = ij(o, n, k)
        return (j, k) if tb else (k, j)

    extra = [] if dep is None else [dep]
    return pl.pallas_call(
        body, name=name,
        grid=grid,
        in_specs=[pl.BlockSpec((tk, tm) if ta else (tm, tk), functools.partial(a_map, pc)) for pc in range(npc)]
        + [pl.BlockSpec((tn, tk) if tb else (tk, tn), b_map)] + [ANY] * len(extra),
        out_specs=pl.BlockSpec((tm, tn), lambda o, n, k: ij(o, n, k)),
        out_shape=jax.ShapeDtypeStruct((m_dim, n_dim), out_dtype),
        scratch_shapes=[pltpu.VMEM((tm, tn), F32)] if use_acc else [],
        compiler_params=_cp("parallel", "parallel", "arbitrary"),
    )(*pieces, b, *extra)


def _ada_fwd(c_all, ada_w, ada_b_shard):
    depth, d, n = ada_w.shape
    nb = c_all.shape[0]

    def body(c_ref, w_ref, b_ref, o_ref, ca_ref):
        ca = _silu(c_ref[...])
        ca_ref[...] = _b(ca)
        o_ref[0] = _dg(ca, w_ref[0], _NN) + b_ref[0]

    return pl.pallas_call(
        body, name="ada_fwd",
        grid=(depth,),
        in_specs=[pl.BlockSpec((nb, d), lambda l: (0, 0)),
                  pl.BlockSpec((1, d, n), lambda l: (l, 0, 0)),
                  pl.BlockSpec((1, 1, n), lambda l: (l, 0, 0))],
        out_specs=[pl.BlockSpec((1, nb, n), lambda l: (l, 0, 0)),
                   pl.BlockSpec((nb, d), lambda l: (0, 0))],
        out_shape=[jax.ShapeDtypeStruct((depth, nb, n), F32), jax.ShapeDtypeStruct((nb, d), BF16)],
        compiler_params=_cp("arbitrary"),
    )(c_all, ada_w, ada_b_shard)


def _fold(acc):
    return jnp.sum(acc, axis=0, keepdims=True)


def _rinv(x):
    return lax.rsqrt(jnp.sum(x * x, axis=-1, keepdims=True) * (1.0 / D_MODEL) + EPS)


def _rms_bwd(a, xhat, rinv):
    return rinv * (a - xhat * (jnp.sum(a * xhat, axis=-1, keepdims=True) * (1.0 / D_MODEL)))


def _row_tile(seq):
    return min(seq, 256)


def _normmod_fwd(x, g, sc, sh, *, nseq, name, layer):
    t, d = x.shape
    seq = t // nseq
    tr = _row_tile(seq)
    nt = seq // tr
    row = pl.BlockSpec((tr, d), lambda s, i: (s * nt + i, 0))
    per_seq = pl.BlockSpec((1, 1, d), lambda s, i: (s, 0, 0))

    def body(x_ref, g_ref, sc_ref, sh_ref, h_ref):
        x_v = x_ref[...]
        h_ref[...] = _b(x_v * _rinv(x_v) * (_row(g_ref, layer) * (1.0 + sc_ref[0])) + sh_ref[0])

    return pl.pallas_call(
        body, name=name, grid=(nseq, nt),
        in_specs=[row, _per_layer(d), per_seq, per_seq],
        out_specs=row,
        out_shape=jax.ShapeDtypeStruct((t, d), BF16),
        compiler_params=_cp("parallel", "parallel"),
    )(x, g, sc, sh)


NORM_TM = 512


def _matmul_normbwd(a, b, dxo, x, delta, gate, g, sc, *, nseq, name, layer, dep=None):
    pieces = list(a) if isinstance(a, (list, tuple)) else [a]
    npc = len(pieces)
    t, width = pieces[0].shape
    k_dim, d = width * npc, b.shape[1]
    assert b.shape[0] == k_dim and all(p.shape == (t, width) for p in pieces)
    seq = t // nseq
    tm = min(NORM_TM, seq)
    per_seq_tiles = seq // tm
    tk = _mxu_tiles(width if npc > 1 else k_dim).pop()
    nk, per = k_dim // tk, width // tk
    has_delta = delta is not None
    extra = [] if dep is None else [dep]

    def body(*refs):
        a_refs, b_ref = refs[:npc], refs[npc]
        dxo_ref, x_ref = refs[npc + 1], refs[npc + 2]
        pos = npc + 3
        if has_delta:
            delta_ref, gate_ref = refs[pos], refs[pos + 1]
            pos += 2
        g_ref, sc_ref = refs[pos], refs[pos + 1]
        pos += 2 + len(extra)
        if has_delta:
            dx_ref, dd_ref, dgate_ref, dg_ref, dsc_ref, dsh_ref = refs[pos:pos + 6]
        else:
            dx_ref, dg_ref, dsc_ref, dsh_ref = refs[pos:pos + 4]
        acc_ref = refs[-1]
        i, k = pl.program_id(0), pl.program_id(1)

        def norm_bwd(dh_v):
            g_v, one_sc = _row(g_ref, layer), 1.0 + sc_ref[0]
            x_v = x_ref[...]
            rinv = _rinv(x_v)
            xhat = x_v * rinv
            dx = dxo_ref[...] + _rms_bwd(dh_v * (g_v * one_sc), xhat, rinv)
            dx_ref[...] = dx

            @pl.when(i == 0)
            def _():
                dg_ref[...] = jnp.zeros_like(dg_ref)

            @pl.when(i % per_seq_tiles == 0)
            def _():
                dsc_ref[...] = jnp.zeros_like(dsc_ref)
                dsh_ref[...] = jnp.zeros_like(dsh_ref)
                if has_delta:
                    dgate_ref[...] = jnp.zeros_like(dgate_ref)

            t_sum = _fold(dh_v * xhat)
            dg_ref[...] += t_sum * one_sc
            dsc_ref[0] += t_sum * g_v
            dsh_ref[0] += _fold(dh_v)
            if has_delta:
                dd_ref[...] = _b(dx * gate_ref[0])
                dgate_ref[0] += _fold(dx * delta_ref[...])

        def step(a_ref):
            p = lax.dot_general(a_ref[...], b_ref[...], _NN, preferred_element_type=F32)
            if nk == 1:
                norm_bwd(p)
            else:
                @pl.when(k == 0)
                def _():
                    acc_ref[...] = p

                @pl.when((k > 0) & (k < nk - 1))
                def _():
                    acc_ref[...] += p

                @pl.when(k == nk - 1)
                def _():
                    norm_bwd(acc_ref[...] + p)

        if npc == 1:
            step(a_refs[0])
        else:
            for pc in range(npc):
                pl.when((k >= pc * per) & (k < (pc + 1) * per))(functools.partial(step, a_refs[pc]))

    def a_map(pc, i, k):
        return (i, jnp.clip(k - pc * per, 0, per - 1) if npc > 1 else k)

    row = pl.BlockSpec((tm, d), lambda i, k: (i, 0))
    per_seq = pl.BlockSpec((1, 1, d), lambda i, k: (i // per_seq_tiles, 0, 0))
    vec = pl.BlockSpec((1, d), lambda i, k: (0, 0))
    shp = lambda *s, dt=F32: jax.ShapeDtypeStruct(s, dt)
    in_specs = [pl.BlockSpec((tm, tk), functools.partial(a_map, pc)) for pc in range(npc)]
    in_specs += [pl.BlockSpec((tk, d), lambda i, k: (k, 0)), row, row]
    operands = [*pieces, b, dxo, x]
    if has_delta:
        in_specs += [row, per_seq]
        operands += [delta, gate]
    in_specs += [_per_layer(d), per_seq] + [ANY] * len(extra)
    operands += [g, sc, *extra]
    if has_delta:
        out_specs = [row, row, per_seq, vec, per_seq, per_seq]
        out_shape = [shp(t, d), shp(t, d, dt=BF16), shp(nseq, 1, d), shp(1, d), shp(nseq, 1, d), shp(nseq, 1, d)]
    else:
        out_specs = [row, vec, per_seq, per_seq]
        out_shape = [shp(t, d), shp(1, d), shp(nseq, 1, d), shp(nseq, 1, d)]
    outs = pl.pallas_call(
        body, name=name, grid=(t // tm, nk),
        in_specs=in_specs, out_specs=out_specs, out_shape=out_shape,
        scratch_shapes=[pltpu.VMEM((tm, d), F32)],
        compiler_params=_cp("arbitrary", "arbitrary"),
    )(*operands)
    if has_delta:
        return tuple(outs)
    dx, dg, dsc, dsh = outs
    return dx, None, None, dg, dsc, dsh


def _matmul_normfwd(a, b, xin, gate, g, sc, sh, *, nseq, name, layer):
    t, k_dim = a.shape
    d = b.shape[1]
    assert b.shape[0] == k_dim and k_dim <= MATMUL_TILE_CAP
    seq = t // nseq
    tm = min(NORM_TM, seq)
    per_seq_tiles = seq // tm

    def body(a_ref, b_ref, xin_ref, gate_ref, g_ref, sc_ref, sh_ref, dl_ref, x_ref, h_ref):
        dl = lax.dot_general(a_ref[...], b_ref[...], _NN, preferred_element_type=F32)
        dl_ref[...] = dl
        x = xin_ref[...] + gate_ref[0] * dl
        x_ref[...] = x
        h_ref[...] = _b(x * _rinv(x) * (_row(g_ref, layer) * (1.0 + sc_ref[0])) + sh_ref[0])

    row = pl.BlockSpec((tm, d), lambda i: (i, 0))
    per_seq = pl.BlockSpec((1, 1, d), lambda i: (i // per_seq_tiles, 0, 0))
    return pl.pallas_call(
        body, name=name, grid=(t // tm,),
        in_specs=[pl.BlockSpec((tm, k_dim), lambda i: (i, 0)), pl.BlockSpec((k_dim, d), lambda i: (0, 0)),
                  row, per_seq, _per_layer(d), per_seq, per_seq],
        out_specs=[row, row, row],
        out_shape=[jax.ShapeDtypeStruct((t, d), F32), jax.ShapeDtypeStruct((t, d), F32), jax.ShapeDtypeStruct((t, d), BF16)],
        compiler_params=_cp("parallel"),
    )(a, b, xin, gate, g, sc, sh)


def _matmul_loss(a, b, xin, gate, fg, target, *, nseq, name):
    t, k_dim = a.shape
    d = b.shape[1]
    assert b.shape[0] == k_dim and k_dim <= MATMUL_TILE_CAP
    seq = t // nseq
    tm = min(NORM_TM, seq)
    per_seq_tiles = seq // tm

    def body(a_ref, b_ref, xin_ref, gate_ref, fg_ref, tgt_ref, dl_ref, loss_ref, dx_ref, dd_ref, dgate_ref, dfg_ref):
        i = pl.program_id(0)
        fg_v, gate_v = fg_ref[...], gate_ref[0]
        dl = lax.dot_general(a_ref[...], b_ref[...], _NN, preferred_element_type=F32)
        dl_ref[...] = dl
        x = xin_ref[...] + gate_v * dl
        rinv = _rinv(x)
        xhat = x * rinv
        err = xhat * fg_v - tgt_ref[...]
        dx = _rms_bwd(err * fg_v * (1.0 / d), xhat, rinv)
        dx_ref[...] = dx
        dd_ref[...] = _b(dx * gate_v)

        @pl.when(i == 0)
        def _():
            loss_ref[...] = jnp.zeros_like(loss_ref)
            dfg_ref[...] = jnp.zeros_like(dfg_ref)

        @pl.when(i % per_seq_tiles == 0)
        def _():
            dgate_ref[...] = jnp.zeros_like(dgate_ref)

        loss_ref[...] += jnp.sum(err * err) * (0.5 / d)
        dfg_ref[...] += _fold(err * xhat) * (1.0 / d)
        dgate_ref[0] += _fold(dx * dl)

    row = pl.BlockSpec((tm, d), lambda i: (i, 0))
    per_seq = pl.BlockSpec((1, 1, d), lambda i: (i // per_seq_tiles, 0, 0))
    vec = pl.BlockSpec((1, d), lambda i: (0, 0))
    return pl.pallas_call(
        body, name=name, grid=(t // tm,),
        in_specs=[pl.BlockSpec((tm, k_dim), lambda i: (i, 0)), pl.BlockSpec((k_dim, d), lambda i: (0, 0)),
                  row, per_seq, vec, row],
        out_specs=[row, pl.BlockSpec((1, 128), lambda i: (0, 0)), row, row, per_seq, vec],
        out_shape=[jax.ShapeDtypeStruct((t, d), F32), jax.ShapeDtypeStruct((1, 128), F32), jax.ShapeDtypeStruct((t, d), F32),
                   jax.ShapeDtypeStruct((t, d), BF16), jax.ShapeDtypeStruct((nseq, 1, d), F32),
                   jax.ShapeDtypeStruct((1, d), F32)],
        compiler_params=_cp("arbitrary"),
    )(a, b, xin, gate, fg, target)


CONV_TC = 256
CONV_LANES = 128
CONV_ROWS = 64
CONV_HALO = 8


def _conv_slabs(seq, fn):
    def step(i, carry):
        r0 = pl.multiple_of(i * CONV_ROWS, CONV_ROWS)
        for h in range(CONV_TC // CONV_LANES):
            fn(r0, slice(h * CONV_LANES, (h + 1) * CONV_LANES))
        return carry

    lax.fori_loop(0, seq // CONV_ROWS, step, 0)


def _slab(ref, r0, cols, seq):
    after = ref[pl.ds(pl.multiple_of(jnp.minimum(r0 + CONV_ROWS, seq - CONV_HALO), CONV_HALO), CONV_HALO), cols]
    return jnp.concatenate([ref[pl.ds(r0, CONV_ROWS), cols], jnp.where(r0 + CONV_ROWS < seq, after, 0.0)], axis=0)


def _conv_block(x, w_ref, b):
    kw = w_ref.shape[0]
    rows = lax.broadcasted_iota(jnp.int32, x.shape, 0)
    y = b + w_ref[kw - 1:kw, :] * x
    for j in range(1, kw):
        y = y + w_ref[kw - 1 - j:kw - j, :] * jnp.where(rows >= j, pltpu.roll(x, j, 0), 0.0)
    return y


def _conv_block_bwd(dy, x, w_ref, dw_ref, db_ref):
    kw = w_ref.shape[0]
    n = x.shape[0]
    rows = lax.broadcasted_iota(jnp.int32, x.shape, 0)
    dx = w_ref[kw - 1:kw, :] * dy
    dw_ref[kw - 1:kw, :] += jnp.sum(dy * x, axis=0, keepdims=True)
    for j in range(1, kw):
        dy_j = jnp.where(rows < n - j, pltpu.roll(dy, n - j, 0), 0.0)
        dx = dx + w_ref[kw - 1 - j:kw - j, :] * dy_j
        dw_ref[kw - 1 - j:kw - j, :] += jnp.sum(dy_j * x, axis=0, keepdims=True)
    db_ref[...] += jnp.sum(dy, axis=0, keepdims=True)
    return dx


def _conv_bwd(dy_ext, x, w_ref, dw_ref, db_ref, cols):
    kw = w_ref.shape[0]
    n = dy_ext.shape[0]
    dy = dy_ext[:CONV_ROWS]
    dx = w_ref[kw - 1:kw, cols] * dy
    dw_ref[kw - 1:kw, cols] += jnp.sum(dy * x, axis=0, keepdims=True)
    for j in range(1, kw):
        dy_j = pltpu.roll(dy_ext, n - j, 0)[:CONV_ROWS]
        dx = dx + w_ref[kw - 1 - j:kw - j, cols] * dy_j
        dw_ref[kw - 1 - j:kw - j, cols] += jnp.sum(dy_j * x, axis=0, keepdims=True)
    db_ref[:, cols] += jnp.sum(dy, axis=0, keepdims=True)
    return dx


def _dsilu(pre):
    sg = jax.nn.sigmoid(pre)
    return pre * sg, sg * (1.0 + pre * (1.0 - sg))


def _conv_specs(kw, layer):
    return [pl.BlockSpec((None, kw, CONV_TC), lambda j, s: (layer, 0, j)),
            pl.BlockSpec((DEPTH, CONV_TC), lambda j, s: (0, j))]


def _ssd_conv_fwd(proj, w, b, *, nseq, layer):
    t = proj.shape[0]
    seq = t // nseq
    nb = CONV_DIM // CONV_TC
    off = COL_XBC // CONV_TC

    def body(x_ref, w_ref, b_ref, o_ref, pre_ref):
        pre = _conv_block(x_ref[...], w_ref, _row(b_ref, layer))
        pre_ref[...] = pre
        o_ref[...] = _silu(pre)

    col = pl.BlockSpec((seq, CONV_TC), lambda j, s: (s, j))
    return pl.pallas_call(
        body, name="ssd_conv_fwd", grid=(nb, nseq),
        in_specs=[pl.BlockSpec((seq, CONV_TC), lambda j, s: (s, off + j)), *_conv_specs(SSD_CONV, layer)],
        out_specs=[col, col],
        out_shape=[jax.ShapeDtypeStruct((t, CONV_DIM), F32)] * 2,
        compiler_params=_cp("parallel", "parallel"),
    )(proj, w, b)


def _ssd_conv_bwd(dact, pre, proj, w, dproj, *, nseq, layer):
    t = proj.shape[0]
    seq = t // nseq
    nb = CONV_DIM // CONV_TC
    off = COL_XBC // CONV_TC

    def body(da_ref, pre_ref, x_ref, w_ref, dproj_ref, dx_ref, dw_ref, db_ref):
        del dproj_ref

        @pl.when(pl.program_id(1) == 0)
        def _():
            dw_ref[...] = jnp.zeros_like(dw_ref)
            db_ref[...] = jnp.zeros_like(db_ref)

        def slab(r0, cols):
            _, dsilu = _dsilu(_slab(pre_ref, r0, cols, seq))
            dpre_ext = _slab(da_ref, r0, cols, seq) * dsilu
            x = x_ref[pl.ds(r0, CONV_ROWS), cols]
            dx_ref[pl.ds(r0, CONV_ROWS), cols] = _b(_conv_bwd(dpre_ext, x, w_ref, dw_ref, db_ref, cols))

        _conv_slabs(seq, slab)

    return pl.pallas_call(
        body, name="ssd_conv_bwd", grid=(nb, nseq),
        in_specs=[pl.BlockSpec((seq, CONV_TC), lambda j, s: (s, j)),
                  pl.BlockSpec((seq, CONV_TC), lambda j, s: (s, j)),
                  pl.BlockSpec((seq, CONV_TC), lambda j, s: (s, off + j)),
                  _conv_specs(SSD_CONV, layer)[0],
                  ANY],
        out_specs=[pl.BlockSpec((seq, CONV_TC), lambda j, s: (s, off + j)),
                   pl.BlockSpec((SSD_CONV, CONV_TC), lambda j, s: (0, j)),
                   pl.BlockSpec((1, CONV_TC), lambda j, s: (0, j))],
        out_shape=[jax.ShapeDtypeStruct(dproj.shape, dproj.dtype), jax.ShapeDtypeStruct((SSD_CONV, CONV_DIM), F32),
                   jax.ShapeDtypeStruct((1, CONV_DIM), F32)],
        input_output_aliases={4: 0},
        compiler_params=_cp("parallel", "arbitrary"),
    )(dact, pre, proj, w, dproj)


def _ffn_act_fwd(up, w, b, *, nseq, layer):
    t = up.shape[0]
    seq = t // nseq
    nb = D_FF // CONV_TC

    def body(g_ref, v_ref, w_ref, b_ref, o_ref):
        pre = _conv_block(g_ref[...].astype(F32), w_ref, _row(b_ref, layer))
        o_ref[...] = _b(_silu(pre) * v_ref[...].astype(F32))

    col = pl.BlockSpec((seq, CONV_TC), lambda j, s: (s, j))
    return pl.pallas_call(
        body, name="ffn_act_fwd", grid=(nb, nseq),
        in_specs=[col,
                  pl.BlockSpec((seq, CONV_TC), lambda j, s: (s, nb + j)),
                  *_conv_specs(FF_CONV, layer)],
        out_specs=col,
        out_shape=jax.ShapeDtypeStruct((t, D_FF), BF16),
        compiler_params=_cp("parallel", "parallel"),
    )(up, up, w, b)


def _ffn_act_bwd(dact, up, w, b, *, nseq, layer):
    t = up.shape[0]
    seq = t // nseq
    nb = D_FF // CONV_TC

    def body(da_ref, g_ref, v_ref, w_ref, b_ref, dg_ref, dv_ref, dw_ref, db_ref):
        @pl.when(pl.program_id(1) == 0)
        def _():
            dw_ref[...] = jnp.zeros_like(dw_ref)
            db_ref[...] = jnp.zeros_like(db_ref)

        gate = g_ref[...].astype(F32)
        silu, dsilu = _dsilu(_conv_block(gate, w_ref, _row(b_ref, layer)))
        da = da_ref[...].astype(F32)
        dv_ref[...] = _b(da * silu)
        dg_ref[...] = _b(_conv_block_bwd(da * v_ref[...].astype(F32) * dsilu, gate, w_ref, dw_ref, db_ref))

    col = pl.BlockSpec((seq, CONV_TC), lambda j, s: (s, j))
    return pl.pallas_call(
        body, name="ffn_act_bwd", grid=(nb, nseq),
        in_specs=[col, col,
                  pl.BlockSpec((seq, CONV_TC), lambda j, s: (s, nb + j)),
                  *_conv_specs(FF_CONV, layer)],
        out_specs=[col, col,
                   pl.BlockSpec((FF_CONV, CONV_TC), lambda j, s: (0, j)),
                   pl.BlockSpec((1, CONV_TC), lambda j, s: (0, j))],
        out_shape=[jax.ShapeDtypeStruct((t, D_FF), BF16), jax.ShapeDtypeStruct((t, D_FF), BF16),
                   jax.ShapeDtypeStruct((FF_CONV, D_FF), F32), jax.ShapeDtypeStruct((1, D_FF), F32)],
        compiler_params=_cp("parallel", "arbitrary"),
    )(dact, up, up, w, b)


SSD_PAIRS = SSD_HEADS // 2
PAIR_W = 2 * SSD_HEAD_DIM
PAIRS_PER_GROUP = SSD_PAIRS // SSD_GROUPS


def _ssd_chunk(xs, bg, cg, dtr, z, hp, dtb, alog, dskip, ng):
    n = dtr.shape[0]
    dt = _softplus(dtr + dtb)
    cs = _cumsum_rows(dt * (-jnp.exp(alog)))
    cs_t = _transpose(cs)
    lane = lax.broadcasted_iota(jnp.int32, (1, SSD_HEADS), 1)
    sub = lax.broadcasted_iota(jnp.int32, (SSD_HEADS, 1), 0)
    row = lax.broadcasted_iota(jnp.int32, (n, 1), 0)
    causal = lax.broadcasted_iota(jnp.int32, (n, n), 0) >= lax.broadcasted_iota(jnp.int32, (n, n), 1)
    future = jnp.where(causal, 0.0, -1e30)
    first = lax.broadcasted_iota(jnp.int32, (1, PAIR_W), 1) < SSD_HEAD_DIM
    first_rows = lax.broadcasted_iota(jnp.int32, (PAIR_W, 1), 0) < SSD_HEAD_DIM
    first_f = first.astype(F32)
    cb = [_bdot_nt(cg[g], bg[g]) for g in range(SSD_GROUPS)]
    ys, hn = [], []
    for p in range(SSD_PAIRS):
        g = p // PAIRS_PER_GROUP
        col, decay, last = [], [], []
        for h in (2 * p, 2 * p + 1):
            oh = (lane == h).astype(F32)
            cs_h = jnp.sum(cs * oh, axis=1, keepdims=True)
            cs_row = jnp.sum(cs_t * (sub == h).astype(F32), axis=0, keepdims=True)
            col.append((jnp.sum(dt * oh, axis=1, keepdims=True), cs_h, jnp.sum(dskip * oh, axis=1, keepdims=True)))
            last.append(jnp.sum(jnp.where(row == n - 1, cs_h, 0.0), axis=0, keepdims=True))
            decay.append(jnp.exp(cs_h - cs_row + future))
        pair = lambda a, b: jnp.where(first, a, b)
        dt_p = pair(col[0][0], col[1][0])
        cs_p = pair(col[0][1], col[1][1])
        last_p = pair(last[0], last[1])
        xc = xs[p] * dt_p
        y = _bdot(cb[g] * decay[0], xc * first_f) + _bdot(cb[g] * decay[1], xc * (1.0 - first_f))
        y = y + _bdot_nt(cg[g], hp[p]) * jnp.exp(cs_p)
        y = y + pair(col[0][2], col[1][2]) * xs[p]
        keep = jnp.where(first_rows, jnp.exp(last[0]), jnp.exp(last[1]))
        hn.append(keep * hp[p] + _bdot_tn(xc * jnp.exp(last_p - cs_p), bg[g]))
        ys.append(y * _silu(z[p]))
    outs = []
    for g in range(SSD_GROUPS):
        ps = range(g * PAIRS_PER_GROUP, (g + 1) * PAIRS_PER_GROUP)
        ms = sum(jnp.sum(ys[p] * ys[p], axis=1, keepdims=True) for p in ps) * (1.0 / GROUP_WIDTH)
        r = lax.rsqrt(ms + EPS)
        outs += [ys[p] * r * ng[p] for p in ps]
    return outs, hn


def _hslices(ref, width, count, base=0, rows=slice(None)):
    return [ref[rows, base + k * width: base + (k + 1) * width] for k in range(count)]


def _ssd_load(xbc_ref, z_ref, dt_ref, ng_ref, layer):
    xs = _hslices(xbc_ref, PAIR_W, SSD_PAIRS)
    bg = _hslices(xbc_ref, D_STATE, SSD_GROUPS, D_SSD)
    cg = _hslices(xbc_ref, D_STATE, SSD_GROUPS, D_SSD + SSD_GROUPS * D_STATE)
    z = _hslices(z_ref, PAIR_W, SSD_PAIRS)
    ng = _hslices(ng_ref, PAIR_W, SSD_PAIRS, rows=slice(layer, layer + 1))
    return xs, bg, cg, dt_ref[:, 0:SSD_HEADS], z, ng


def _ssd_specs(nch):
    rowi = lambda s, c: s * nch + c
    return [pl.BlockSpec((CHUNK, CONV_DIM), lambda s, c: (rowi(s, c), 0)),
            pl.BlockSpec((CHUNK, D_SSD), lambda s, c: (rowi(s, c), COL_Z // D_SSD)),
            pl.BlockSpec((CHUNK, 128), lambda s, c: (rowi(s, c), COL_DT // 128)),
            _per_layer(SSD_HEADS), _per_layer(SSD_HEADS), _per_layer(SSD_HEADS), _per_layer(D_SSD)]


def _ssd_fwd(xbc, proj, dtb, alog, dskip, ng, *, nseq, layer):
    t = proj.shape[0]
    nch = t // nseq // CHUNK
    hd = PAIR_W

    def body(xbc_ref, z_ref, dt_ref, dtb_ref, alog_ref, dsk_ref, ng_ref, y_ref, hp_ref, h_ref):
        @pl.when(pl.program_id(1) == 0)
        def _():
            h_ref[...] = jnp.zeros_like(h_ref)

        xs, bg, cg, dtr, z, ngs = _ssd_load(xbc_ref, z_ref, dt_ref, ng_ref, layer)
        hp_ref[0] = h_ref[...]
        hp = [h_ref[h * hd:(h + 1) * hd, :] for h in range(SSD_PAIRS)]
        outs, hn = _ssd_chunk(xs, bg, cg, dtr, z, hp, _row(dtb_ref, layer), _row(alog_ref, layer), _row(dsk_ref, layer), ngs)
        for h in range(SSD_PAIRS):
            y_ref[:, h * hd:(h + 1) * hd] = _b(outs[h])
            h_ref[h * hd:(h + 1) * hd, :] = hn[h]

    return pl.pallas_call(
        body, name="ssd_fwd", grid=(nseq, nch),
        in_specs=_ssd_specs(nch),
        out_specs=[pl.BlockSpec((CHUNK, D_SSD), lambda s, c: (s * nch + c, 0)),
                   pl.BlockSpec((1, D_SSD, D_STATE), lambda s, c: (s * nch + c, 0, 0))],
        out_shape=[jax.ShapeDtypeStruct((t, D_SSD + D_GM), BF16),
                   jax.ShapeDtypeStruct((t // CHUNK, D_SSD, D_STATE), F32)],
        scratch_shapes=[pltpu.VMEM((D_SSD, D_STATE), F32)],
        compiler_params=_cp("arbitrary", "arbitrary"),
    )(xbc, proj, proj, dtb, alog, dskip, ng)


def _ssd_bwd(dy, xbc, proj, hprev, dtb, alog, dskip, ng, *, nseq, layer):
    t = proj.shape[0]
    nch = t // nseq // CHUNK
    hd = PAIR_W
    rev = lambda s, c: s * nch + (nch - 1 - c)

    def body(dy_ref, xbc_ref, z_ref, dt_ref, hp_ref, dtb_ref, alog_ref, dsk_ref, ng_ref,
             dxbc_ref, dproj_ref, ddtb_ref, dalog_ref, ddsk_ref, dng_ref, dh_ref):
        first = (pl.program_id(0) == 0) & (pl.program_id(1) == 0)

        @pl.when(pl.program_id(1) == 0)
        def _():
            dh_ref[...] = jnp.zeros_like(dh_ref)

        @pl.when(first)
        def _():
            ddtb_ref[...] = jnp.zeros_like(ddtb_ref)
            dalog_ref[...] = jnp.zeros_like(dalog_ref)
            ddsk_ref[...] = jnp.zeros_like(ddsk_ref)
            dng_ref[...] = jnp.zeros_like(dng_ref)

        xs, bg, cg, dtr, z, ngs = _ssd_load(xbc_ref, z_ref, dt_ref, ng_ref, layer)
        hp = [hp_ref[0, h * hd:(h + 1) * hd, :] for h in range(SSD_PAIRS)]
        _, vjp = jax.vjp(_ssd_chunk, xs, bg, cg, dtr, z, hp, _row(dtb_ref, layer), _row(alog_ref, layer), _row(dsk_ref, layer), ngs)
        douts = [dy_ref[:, h * hd:(h + 1) * hd] for h in range(SSD_PAIRS)]
        dhn = [dh_ref[h * hd:(h + 1) * hd, :] for h in range(SSD_PAIRS)]
        dxs, dbg, dcg, ddtr, dz, dhp, ddtb, dalog, ddsk, dngs = vjp((douts, dhn))
        dproj_ref[:, :COL_Z] = jnp.zeros((CHUNK, COL_Z), BF16)
        dproj_ref[:, COL_XBC:] = jnp.zeros((CHUNK, N_INP - COL_XBC), BF16)
        for h in range(SSD_PAIRS):
            dxbc_ref[:, h * hd:(h + 1) * hd] = dxs[h]
            dproj_ref[:, COL_Z + h * hd: COL_Z + (h + 1) * hd] = _b(dz[h])
            dh_ref[h * hd:(h + 1) * hd, :] = dhp[h]
            dng_ref[:, h * hd:(h + 1) * hd] += dngs[h]
        for g in range(SSD_GROUPS):
            dxbc_ref[:, D_SSD + g * D_STATE: D_SSD + (g + 1) * D_STATE] = dbg[g]
            dxbc_ref[:, D_SSD + (SSD_GROUPS + g) * D_STATE: D_SSD + (SSD_GROUPS + g + 1) * D_STATE] = dcg[g]
        dproj_ref[:, COL_DT:COL_DT + SSD_HEADS] = _b(ddtr)
        ddtb_ref[...] += ddtb
        dalog_ref[...] += dalog
        ddsk_ref[...] += ddsk

    small = pl.BlockSpec((1, SSD_HEADS), lambda s, c: (0, 0))
    return pl.pallas_call(
        body, name="ssd_bwd", grid=(nseq, nch),
        in_specs=[pl.BlockSpec((CHUNK, D_SSD), lambda s, c: (rev(s, c), 0)),
                  pl.BlockSpec((CHUNK, CONV_DIM), lambda s, c: (rev(s, c), 0)),
                  pl.BlockSpec((CHUNK, D_SSD), lambda s, c: (rev(s, c), COL_Z // D_SSD)),
                  pl.BlockSpec((CHUNK, 128), lambda s, c: (rev(s, c), COL_DT // 128)),
                  pl.BlockSpec((1, D_SSD, D_STATE), lambda s, c: (rev(s, c), 0, 0)),
                  _per_layer(SSD_HEADS), _per_layer(SSD_HEADS), _per_layer(SSD_HEADS), _per_layer(D_SSD)],
        out_specs=[pl.BlockSpec((CHUNK, CONV_DIM), lambda s, c: (rev(s, c), 0)),
                   pl.BlockSpec((CHUNK, N_INP), lambda s, c: (rev(s, c), 0)),
                   small, small, small,
                   pl.BlockSpec((1, D_SSD), lambda s, c: (0, 0))],
        out_shape=[jax.ShapeDtypeStruct((t, CONV_DIM), F32), jax.ShapeDtypeStruct((t, N_INP), BF16),
                   jax.ShapeDtypeStruct((1, SSD_HEADS), F32), jax.ShapeDtypeStruct((1, SSD_HEADS), F32),
                   jax.ShapeDtypeStruct((1, SSD_HEADS), F32), jax.ShapeDtypeStruct((1, D_SSD), F32)],
        scratch_shapes=[pltpu.VMEM((D_SSD, D_STATE), F32)],
        compiler_params=_cp("arbitrary", "arbitrary"),
    )(dy, xbc, proj, proj, hprev, dtb, alog, dskip, ng)


def _gmlp_chunk(gu, gv, ws, bs_cols, vg, og):
    n = gu[0].shape[0]
    mask = _tri(n, True)
    au = [_gelu(t) for t in gu]
    av = [_gelu(t) for t in gv]
    r = lax.rsqrt(sum(jnp.sum(t * t, axis=1, keepdims=True) for t in av) * (1.0 / D_GM) + EPS)
    p = []
    for h in range(GM_HEADS):
        sv = _bdot(ws[h] * mask, av[h] * r * vg[h]) + bs_cols[h]
        p.append(au[h] * sv)
    r2 = lax.rsqrt(sum(jnp.sum(t * t, axis=1, keepdims=True) for t in p) * (1.0 / D_GM) + EPS)
    return [p[h] * r2 * og[h] for h in range(GM_HEADS)]


def _gmlp_load(u_ref, v_ref, ws_ref, bst_ref, vg_ref, og_ref, layer):
    gu = _hslices(u_ref, GM_HEAD_DIM, GM_HEADS)
    gv = _hslices(v_ref, GM_HEAD_DIM, GM_HEADS)
    ws = [ws_ref[h] for h in range(GM_HEADS)]
    bs_cols = [bst_ref[:, h:h + 1] for h in range(GM_HEADS)]
    mine = slice(layer, layer + 1)
    return (gu, gv, ws, bs_cols, _hslices(vg_ref, GM_HEAD_DIM, GM_HEADS, rows=mine),
            _hslices(og_ref, GM_HEAD_DIM, GM_HEADS, rows=mine))


def _gmlp_specs(layer):
    return [pl.BlockSpec((CHUNK, D_GM), lambda i: (i, COL_U // D_GM)),
            pl.BlockSpec((CHUNK, D_GM), lambda i: (i, COL_V // D_GM)),
            pl.BlockSpec((None, GM_HEADS, CHUNK, CHUNK), lambda i: (layer, 0, 0, 0)),
            pl.BlockSpec((None, CHUNK, GM_HEADS), lambda i: (layer, 0, 0)),
            _per_layer(D_GM), _per_layer(D_GM)]


def _gmlp_fwd(proj, ycat, ws, bst, vg, og, *, layer):
    t = proj.shape[0]

    def body(u_ref, v_ref, ws_ref, bst_ref, vg_ref, og_ref, ycat_ref, o_ref):
        del ycat_ref
        outs = _gmlp_chunk(*_gmlp_load(u_ref, v_ref, ws_ref, bst_ref, vg_ref, og_ref, layer))
        for h in range(GM_HEADS):
            o_ref[:, h * GM_HEAD_DIM:(h + 1) * GM_HEAD_DIM] = _b(outs[h])

    return pl.pallas_call(
        body, name="gmlp_fwd", grid=(t // CHUNK,),
        in_specs=_gmlp_specs(layer) + [ANY],
        out_specs=pl.BlockSpec((CHUNK, D_GM), lambda i: (i, D_SSD // D_GM)),
        out_shape=jax.ShapeDtypeStruct(ycat.shape, ycat.dtype),
        input_output_aliases={6: 0},
        compiler_params=_cp("parallel"),
    )(proj, proj, ws, bst, vg, og, ycat)


def _gmlp_bwd(dy, proj, ws, bst, vg, og, dproj, *, layer):
    t = proj.shape[0]
    w = GM_HEAD_DIM

    def body(dy_ref, u_ref, v_ref, ws_ref, bst_ref, vg_ref, og_ref, dproj_ref,
             dgm_ref, dws_ref, dbst_ref, dvg_ref, dog_ref):
        del dproj_ref

        @pl.when(pl.program_id(0) == 0)
        def _():
            dws_ref[...] = jnp.zeros_like(dws_ref)
            dbst_ref[...] = jnp.zeros_like(dbst_ref)
            dvg_ref[...] = jnp.zeros_like(dvg_ref)
            dog_ref[...] = jnp.zeros_like(dog_ref)

        _, vjp = jax.vjp(_gmlp_chunk, *_gmlp_load(u_ref, v_ref, ws_ref, bst_ref, vg_ref, og_ref, layer))
        dgu, dgv, dws, dbs, dvg, dog = vjp(_hslices(dy_ref, w, GM_HEADS))
        for h in range(GM_HEADS):
            dgm_ref[:, h * w:(h + 1) * w] = _b(dgu[h])
            dgm_ref[:, D_GM + h * w: D_GM + (h + 1) * w] = _b(dgv[h])
            dws_ref[h] += dws[h]
            dbst_ref[:, h:h + 1] += dbs[h]
            dvg_ref[:, h * w:(h + 1) * w] += dvg[h]
            dog_ref[:, h * w:(h + 1) * w] += dog[h]

    return pl.pallas_call(
        body, name="gmlp_bwd", grid=(t // CHUNK,),
        in_specs=[pl.BlockSpec((CHUNK, D_GM), lambda i: (i, 1))] + _gmlp_specs(layer) + [ANY],
        out_specs=[pl.BlockSpec((CHUNK, 2 * D_GM), lambda i: (i, COL_U // (2 * D_GM))),
                   pl.BlockSpec((GM_HEADS, CHUNK, CHUNK), lambda i: (0, 0, 0)),
                   pl.BlockSpec((CHUNK, GM_HEADS), lambda i: (0, 0)),
                   pl.BlockSpec((1, D_GM), lambda i: (0, 0)),
                   pl.BlockSpec((1, D_GM), lambda i: (0, 0))],
        out_shape=[jax.ShapeDtypeStruct(dproj.shape, dproj.dtype), jax.ShapeDtypeStruct((GM_HEADS, CHUNK, CHUNK), F32),
                   jax.ShapeDtypeStruct((CHUNK, GM_HEADS), F32), jax.ShapeDtypeStruct((1, D_GM), F32),
                   jax.ShapeDtypeStruct((1, D_GM), F32)],
        input_output_aliases={7: 0},
        compiler_params=_cp("arbitrary"),
    )(dy, proj, proj, ws, bst, vg, og, dproj)


def _local_step(x, target, mods, w, final_g, *, nseq, big_w, grad_sink, small_sink):
    saved = []
    x0, delta, gate = x, None, None
    h1 = _normmod_fwd(x, w["norm1_g"], mods[0][1], mods[0][0], nseq=nseq, name="norm1_fwd_0", layer=0)
    for l in range(DEPTH):
        sh1, sc1, g1, sh2, sc2, g2 = mods[l]
        w_in = big_w(l, "w_in", h1)
        proj = _matmul(h1, w_in, tb=True, name=f"mm_in_{l}")
        xbc, xbc_pre = _ssd_conv_fwd(proj, w["ssd_conv_w"], w["ssd_conv_b"], nseq=nseq, layer=l)
        ycat, hprev = _ssd_fwd(xbc, proj, w["ssd_dt_bias"], w["ssd_a_log"], w["ssd_d"], w["ssd_norm_g"], nseq=nseq,
                               layer=l)
        ycat = _gmlp_fwd(proj, ycat, w["gm_ws"], w["gm_bst"], w["gm_vnorm_g"], w["gm_out_g"], layer=l)
        w_out = big_w(l, "w_out", ycat)
        mix, x1, h2 = _matmul_normfwd(ycat, w_out, x0, g1, w["norm2_g"], sc2, sh2, nseq=nseq, name=f"mm_out_{l}",
                                      layer=l)
        ff_up = big_w(l, "ff_up", h2)
        up = _matmul(h2, ff_up, tb=True, name=f"mm_up_{l}", out_dtype=BF16)
        act = _ffn_act_fwd(up, w["ff_conv_w"], w["ff_conv_b"], nseq=nseq, layer=l)
        ff_down = big_w(l, "ff_down", act)
        sv = dict(x0=x0, xin_delta=delta, xin_gate=gate, h1=h1, proj=proj, xbc=xbc, xbc_pre=xbc_pre, hprev=hprev,
                  ycat=ycat, mix=mix, x1=x1, h2=h2, up=up, act=act,
                  w_in=w_in, w_out=w_out, ff_up=ff_up, ff_down=ff_down)
        if l + 1 < DEPTH:
            nsh1, nsc1 = mods[l + 1][0], mods[l + 1][1]
            dn, x0, h1 = _matmul_normfwd(act, ff_down, x1, g2, w["norm1_g"], nsc1, nsh1, nseq=nseq,
                                         name=f"mm_down_{l}", layer=l + 1)
        else:
            dn, loss, dx, ddelta, dgate, dfg = _matmul_loss(act, ff_down, x1, g2, final_g, target, nseq=nseq,
                                                            name=f"mm_down_{l}")
        saved.append(dict(sv, dn=dn))
        delta, gate = dn, g2

    small, dmods = [None] * DEPTH, [None] * DEPTH
    for l in reversed(range(DEPTH)):
        sv = saved[l]
        sh1, sc1, g1, sh2, sc2, g2 = mods[l]
        dg2 = dgate
        g_ff_down = _matmul(sv["act"], ddelta, ta=True, name=f"mm_down_dw_{l}", out_dtype=BF16)
        dact = _matmul(ddelta, sv["ff_down"], tb=True, name=f"mm_down_dx_{l}", out_dtype=BF16)
        dgate_ff, dval_ff, dfcw, dfcb = _ffn_act_bwd(dact, sv["up"], w["ff_conv_w"], w["ff_conv_b"], nseq=nseq, layer=l)
        g_ff_up = _matmul([dgate_ff, dval_ff], sv["h2"], ta=True, name=f"mm_up_dw_{l}", out_dtype=BF16)
        dep = grad_sink(l, "ffn", dict(ff_down=g_ff_down, ff_up=g_ff_up), dval_ff)
        dx, dmix, dg1, dn2g, dsc2, dsh2 = _matmul_normbwd([dgate_ff, dval_ff], sv["ff_up"], dx, sv["x1"], sv["mix"], g1,
                                                          w["norm2_g"], sc2, nseq=nseq, name=f"mm_up_dx_{l}", layer=l,
                                                          dep=dep)
        g_w_out = _matmul(sv["ycat"], dmix, ta=True, name=f"mm_out_dw_{l}", out_dtype=BF16)
        dep = grad_sink(l, "w_out", dict(w_out=g_w_out), dmix)
        dycat = _matmul(dmix, sv["w_out"], tb=True, name=f"mm_out_dx_{l}", dep=dep)
        dxbc_act, dproj, ddtb, dalog, ddsk, dng = _ssd_bwd(dycat, sv["xbc"], sv["proj"], sv["hprev"], w["ssd_dt_bias"],
                                                          w["ssd_a_log"], w["ssd_d"], w["ssd_norm_g"], nseq=nseq, layer=l)
        dproj, dscw, dscb = _ssd_conv_bwd(dxbc_act, sv["xbc_pre"], sv["proj"], w["ssd_conv_w"], dproj, nseq=nseq,
                                          layer=l)
        dproj, dws, dbst, dvg, dog = _gmlp_bwd(dycat, sv["proj"], w["gm_ws"], w["gm_bst"], w["gm_vnorm_g"], w["gm_out_g"],
                                               dproj, layer=l)
        early = dict(norm2_g=dn2g, ssd_norm_g=dng, gm_vnorm_g=dvg, gm_out_g=dog,
                     ssd_conv_w=dscw, ssd_conv_b=dscb, ff_conv_w=dfcw, ff_conv_b=dfcb,
                     ssd_dt_bias=ddtb, ssd_a_log=dalog, ssd_d=ddsk, gm_ws=dws, gm_bs=dbst.T)
        dep = small_sink(l, early, small, dmods, dfg, loss)
        g_w_in = _matmul(dproj, sv["h1"], ta=True, name=f"mm_in_dw_{l}", out_dtype=BF16, dep=dep)
        dep = grad_sink(l, "w_in", dict(w_in=g_w_in), dproj)
        dx, ddelta, dgate, dn1g, dsc1, dsh1 = _matmul_normbwd(dproj, sv["w_in"], dx, sv["x0"], sv["xin_delta"],
                                                              sv["xin_gate"], w["norm1_g"], sc1, nseq=nseq,
                                                              name=f"mm_in_dx_{l}", layer=l, dep=dep)
        small[l] = dict(early, norm1_g=dn1g)
        dmods[l] = jnp.concatenate([dsh1, dsc1, dg1, dsh2, dsc2, dg2], axis=-1)[:, 0, :]
    return dx, small, dmods


def _all_gather(arrs, name, dep=None):
    n = len(arrs)
    extra = [] if dep is None else [dep]

    def body(*refs):
        ins, outs = refs[:n], refs[n + len(extra):2 * n + len(extra)]
        send_sems, recv_sems, local_sems = refs[2 * n + len(extra):]
        x, y, c = lax.axis_index("x"), lax.axis_index("y"), lax.axis_index("c")
        me, sibling = (x, y, c), (x, y, 1 - c)
        chips = [(1 - x, y), (x, 1 - y), (1 - x, 1 - y)]

        def copy(i, k, block, to, src=None):
            px, py, pc = block
            dst = outs[i].at[4 * px + 2 * py + pc]
            return pltpu.make_async_remote_copy(
                src_ref=dst if src is None else src, dst_ref=dst,
                send_sem=send_sems.at[7 * i + k], recv_sem=recv_sems.at[7 * i + k],
                device_id=to, device_id_type=MESH)

        mine = [pltpu.make_async_copy(ins[i], outs[i].at[4 * x + 2 * y + c], local_sems.at[i]) for i in range(n)]
        for cp in mine:
            cp.start()
        first = []
        for i in range(n):
            first.append(copy(i, 0, me, sibling, src=ins[i]))
            first += [copy(i, 1 + j, me, (*chip, c), src=ins[i]) for j, chip in enumerate(chips)]
        for cp in first:
            cp.start()
        passed = []
        for j, chip in enumerate(chips):
            for i in range(n):
                copy(i, 1 + j, (*chip, c), me).wait_recv()
                fwd = copy(i, 4 + j, (*chip, c), sibling)
                fwd.start()
                passed.append(fwd)
        for i in range(n):
            copy(i, 0, sibling, me).wait_recv()
            for j, chip in enumerate(chips):
                copy(i, 4 + j, (*chip, 1 - c), me).wait_recv()
        for cp in first + passed:
            cp.wait_send()
        for cp in mine:
            cp.wait()

    return pl.pallas_call(
        body, name=name,
        in_specs=[ANY] * (n + len(extra)), out_specs=[ANY] * n,
        out_shape=[jax.ShapeDtypeStruct((N_DEV,) + a.shape, a.dtype) for a in arrs],
        scratch_shapes=[pltpu.SemaphoreType.DMA((7 * n,)), pltpu.SemaphoreType.DMA((7 * n,)),
                        pltpu.SemaphoreType.DMA((n,))],
    )(*arrs, *extra)


HBM = pl.BlockSpec(memory_space=pltpu.HBM)
SEM = pl.BlockSpec(memory_space=pltpu.SEMAPHORE)
EFFECT = pltpu.SideEffectType.DATAFLOW_SIDE_EFFECTING


def _peer(k):
    x, y, c = lax.axis_index("x"), lax.axis_index("y"), lax.axis_index("c")
    return (1 - x if k & 4 else x, 1 - y if k & 2 else y, 1 - c if k & 1 else c)


ALL_PEERS = tuple(range(1, N_DEV))
OTHER_CHIPS = (2, 4, 6)


def _xc_copies(scatter, srcs, lands, send_sems, recv_sems, peers=ALL_PEERS):
    x, y, c = lax.axis_index("x"), lax.axis_index("y"), lax.axis_index("c")
    copies = []
    for i in range(len(srcs)):
        for k in (peers[i] if isinstance(peers[0], tuple) else peers):
            px, py, pc = _peer(k)
            src = srcs[i].at[4 * px + 2 * py + pc] if scatter else srcs[i]
            dst = lands[i].at[k - 1] if scatter else lands[i].at[4 * x + 2 * y + c]
            copies.append(pltpu.make_async_remote_copy(
                src_ref=src, dst_ref=dst, send_sem=send_sems[i].at[k - 1], recv_sem=recv_sems[i].at[k - 1],
                device_id=(px, py, pc), device_id_type=MESH))
    return copies


def _xc_own(scatter, srcs, lands, send_sems):
    if scatter:
        return []
    me = 4 * lax.axis_index("x") + 2 * lax.axis_index("y") + lax.axis_index("c")
    return [pltpu.make_async_copy(srcs[i], lands[i].at[me], send_sems[i].at[N_DEV - 1]) for i in range(len(srcs))]


def _xc_start(scatter, arrs, after, name, peers=ALL_PEERS):
    n = len(arrs)
    lands = [lax.empty((N_DEV - 1,) + a.shape[1:] if scatter else (N_DEV,) + a.shape, a.dtype) for a in arrs]

    def body(*refs):
        srcs, lnd = refs[:n], refs[n:2 * n]
        send_sems, recv_sems = refs[2 * n + 1:3 * n + 1], refs[3 * n + 1:4 * n + 1]
        token = refs[6 * n + 1]
        for cp in _xc_copies(scatter, srcs, lnd, send_sems, recv_sems, peers) + _xc_own(scatter, srcs, lnd, send_sems):
            cp.start()
        token[...] = jnp.zeros_like(token)

    outs = pl.pallas_call(
        body, name=name,
        out_shape=[pltpu.SemaphoreType.DMA((N_DEV,))] * (2 * n)
        + [pltpu.HBM(a.shape, a.dtype) for a in arrs] + [pltpu.HBM(a.shape, a.dtype) for a in lands]
        + [jax.ShapeDtypeStruct((8, 128), F32)],
        in_specs=[HBM] * (2 * n) + [ANY],
        out_specs=[SEM] * (2 * n) + [HBM] * (2 * n) + [pl.BlockSpec(memory_space=pltpu.VMEM)],
        input_output_aliases={i: 2 * n + i for i in range(2 * n)},
        compiler_params=pltpu.CompilerParams(has_side_effects=EFFECT),
    )(*[pltpu.with_memory_space_constraint(a, pltpu.HBM) for a in list(arrs) + lands], after)
    return outs[:n], outs[n:2 * n], outs[2 * n:3 * n], outs[3 * n:4 * n], outs[4 * n][0, 0]


def _xc_wait(scatter, send_sems, recv_sems, srcs, lands, after, name, peers=ALL_PEERS):
    n = len(srcs)

    def body(*refs):
        s_refs, l_refs = refs[:n], refs[n:2 * n]
        ss, rs = refs[2 * n:3 * n], refs[3 * n:4 * n]
        for cp in _xc_copies(scatter, s_refs, l_refs, ss, rs, peers):
            cp.wait_send()
            cp.wait_recv()
        for cp in _xc_own(scatter, s_refs, l_refs, ss):
            cp.wait()

    outs = pl.pallas_call(
        body, name=name,
        out_shape=[pltpu.HBM(a.shape, a.dtype) for a in list(srcs) + list(lands)],
        in_specs=[HBM] * (2 * n) + [SEM] * (2 * n) + [ANY],
        out_specs=[HBM] * (2 * n),
        input_output_aliases={i: i for i in range(2 * n)},
        compiler_params=pltpu.CompilerParams(has_side_effects=EFFECT),
    )(*srcs, *lands, *send_sems, *recv_sems, after)
    return outs[:n], outs[n:]


def _sib_copies(zones, send_sems, recv_sems):
    x, y, c = lax.axis_index("x"), lax.axis_index("y"), lax.axis_index("c")
    copies = []
    for i in range(len(zones)):
        for q in range(N_DEV // 2):
            slot = zones[i].at[2 * q + c]
            copies.append(pltpu.make_async_remote_copy(
                src_ref=slot, dst_ref=slot, send_sem=send_sems[i].at[q], recv_sem=recv_sems[i].at[q],
                device_id=(x, y, 1 - c), device_id_type=MESH))
    return copies


def _sib_start(zones, name):
    n = len(zones)

    def body(*refs):
        for cp in _sib_copies(refs[:n], refs[n:2 * n], refs[2 * n:3 * n]):
            cp.start()

    outs = pl.pallas_call(
        body, name=name,
        out_shape=[pltpu.SemaphoreType.DMA((N_DEV // 2,))] * (2 * n) + [pltpu.HBM(a.shape, a.dtype) for a in zones],
        in_specs=[HBM] * n,
        out_specs=[SEM] * (2 * n) + [HBM] * n,
        input_output_aliases={i: 2 * n + i for i in range(n)},
        compiler_params=pltpu.CompilerParams(has_side_effects=EFFECT),
    )(*[pltpu.with_memory_space_constraint(a, pltpu.HBM) for a in zones])
    return outs[:n], outs[n:2 * n], outs[2 * n:]


def _sib_wait(send_sems, recv_sems, zones, name):
    n = len(zones)

    def body(*refs):
        for cp in _sib_copies(refs[:n], refs[n:2 * n], refs[2 * n:3 * n]):
            cp.wait_send()
            cp.wait_recv()

    return pl.pallas_call(
        body, name=name,
        out_shape=[pltpu.HBM(a.shape, a.dtype) for a in zones],
        in_specs=[HBM] * n + [SEM] * (2 * n),
        out_specs=[HBM] * n,
        input_output_aliases={i: i for i in range(n)},
        compiler_params=pltpu.CompilerParams(has_side_effects=EFFECT),
    )(*zones, *send_sems, *recv_sems)


def _adamw_math(w, g, m, v):
    m = ADAM_B1 * m + (1.0 - ADAM_B1) * g
    v = ADAM_B2 * v + (1.0 - ADAM_B2) * (g * g)
    m_hat = m / (1.0 - ADAM_B1 ** ADAM_STEP)
    v_hat = v / (1.0 - ADAM_B2 ** ADAM_STEP)
    delta = -ADAM_LR * (m_hat / (jnp.sqrt(v_hat) + ADAM_EPS) + ADAM_WD * w)
    return delta, m, v


def _adamw_sharded(parts, w, m, v, pos, name):
    depth, rows, cols = w.shape
    tr = _tile(rows, 256) if rows % 8 == 0 else rows
    npart = len(parts)

    def body(pos_ref, *refs):
        prefs = refs[:npart]
        w_ref, m_ref, v_ref, g_out, d_out, m_out, v_out = refs[npart:]
        g = prefs[0][...]
        for pr in prefs[1:]:
            g = g + pr[...]
        delta, mn, vn = _adamw_math(w_ref[...], g, m_ref[...], v_ref[...])
        g_out[...] = g
        d_out[...] = delta
        m_out[...] = mn
        v_out[...] = vn

    def part_spec(fn):
        return pl.BlockSpec((1, tr, cols), lambda l, i, p: (fn(p) * depth + l, i, 0))

    blk = pl.BlockSpec((1, tr, cols), lambda l, i, p: (l, i, 0))
    shp = jax.ShapeDtypeStruct((depth, rows, cols), F32)
    return pl.pallas_call(
        body, name=name,
        grid_spec=pltpu.PrefetchScalarGridSpec(
            num_scalar_prefetch=1, grid=(depth, rows // tr),
            in_specs=[part_spec(fn) for _, fn in parts] + [blk, blk, blk],
            out_specs=[blk, blk, blk, blk]),
        out_shape=[shp, shp, shp, shp],
        compiler_params=_cp("parallel", "parallel"),
    )(pos, *[a for a, _ in parts], w, m, v)


def _adamw_layer(parts, w, m, v, pos, layer, prev, name):
    depth, rows, cols = w.shape
    npart = len(parts)
    nprev = 0 if prev is None else 4
    if rows % 16 == 0:
        tr, tc = max(t for t in range(16, 257, 16) if rows % t == 0), cols
    else:
        tr, tc = rows, _tile(cols, 256)
    pick = (lambda i: (i, 0)) if rows % 16 == 0 else (lambda i: (0, i))

    def body(pos_ref, *refs):
        prefs = refs[:npart]
        w_ref, m_ref, v_ref = refs[npart:npart + 3]
        g_out, d_out, m_out, v_out = refs[npart + 3 + nprev:]
        g = prefs[0][...].astype(F32)
        for pr in prefs[1:]:
            g = g + pr[...].astype(F32)
        delta, mn, vn = _adamw_math(w_ref[...], g, m_ref[...], v_ref[...])
        g_out[...] = g
        d_out[...] = delta
        m_out[...] = mn
        v_out[...] = vn

    def part_spec(fn):
        return pl.BlockSpec((1, tr, tc), lambda i, p: (fn(p), *pick(i)))

    blk = pl.BlockSpec((1, tr, tc), lambda i, p: (layer, *pick(i)))
    shp = jax.ShapeDtypeStruct((depth, rows, cols), F32)
    first_prev = 1 + npart + 3
    return pl.pallas_call(
        body, name=name,
        grid_spec=pltpu.PrefetchScalarGridSpec(
            num_scalar_prefetch=1, grid=(rows // tr * (cols // tc),),
            in_specs=[part_spec(fn) for _, fn in parts] + [blk, blk, blk] + [ANY] * nprev,
            out_specs=[blk, blk, blk, blk]),
        out_shape=[shp, shp, shp, shp],
        input_output_aliases={first_prev + j: j for j in range(nprev)},
        compiler_params=_cp("parallel"),
    )(pos, *[a for a, _ in parts], w, m, v, *(prev or ()))


def _adamw_rows_major(parts_by_layer, w, m, v, pos, name):
    rows, depth, cols = w.shape
    tc = _tile(cols, 256)
    npart = len(parts_by_layer[0])

    def body(pos_ref, *refs):
        prefs = refs[:depth * npart]
        w_ref, m_ref, v_ref, g_out, d_out, m_out, v_out = refs[depth * npart:]
        for l in range(depth):
            g = prefs[l * npart][0].astype(F32)
            for pr in prefs[l * npart + 1:(l + 1) * npart]:
                g = g + pr[0].astype(F32)
            delta, mn, vn = _adamw_math(w_ref[:, l, :], g, m_ref[:, l, :], v_ref[:, l, :])
            g_out[:, l, :] = g
            d_out[:, l, :] = delta
            m_out[:, l, :] = mn
            v_out[:, l, :] = vn

    def part_spec(fn):
        return pl.BlockSpec((1, rows, tc), lambda j, p: (fn(p), 0, j))

    blk = pl.BlockSpec((rows, depth, tc), lambda j, p: (0, 0, j))
    shp = jax.ShapeDtypeStruct(w.shape, F32)
    flat = [pf for parts in parts_by_layer for pf in parts]
    return pl.pallas_call(
        body, name=name,
        grid_spec=pltpu.PrefetchScalarGridSpec(
            num_scalar_prefetch=1, grid=(cols // tc,),
            in_specs=[part_spec(fn) for _, fn in flat] + [blk, blk, blk],
            out_specs=[blk, blk, blk, blk]),
        out_shape=[shp, shp, shp, shp],
        compiler_params=_cp("parallel"),
    )(pos, *[a for a, _ in flat], w, m, v)


_P1024 = ["norm1_g", "norm2_g", "ssd_norm_g", "gm_vnorm_g", "gm_out_g"]
_P16 = ["ssd_dt_bias", "ssd_a_log", "ssd_d"]


def _adamw_small(gath, wmv):
    names = list(wmv.keys())
    classes = list(gath.keys())
    flat_in = [gath[k] for k in classes]
    for nme in names:
        flat_in += list(wmv[nme])
    out_shapes = []
    for nme in names:
        out_shapes += [jax.ShapeDtypeStruct(wmv[nme][0].shape, F32)] * 4
    out_shapes += [jax.ShapeDtypeStruct((DEPTH, SSD_CONV, CONV_DIM), F32), jax.ShapeDtypeStruct((DEPTH, FF_CONV, D_FF), F32),
                   jax.ShapeDtypeStruct((1, SSD_HEADS), F32)]
    scratch = [pltpu.VMEM(gath[k].shape[1:], F32) for k in classes]
    ncls = len(classes)

    def body(*refs):
        g_refs = dict(zip(classes, refs[:ncls]))
        pos = ncls
        w_refs = {}
        for nme in names:
            w_refs[nme] = refs[pos:pos + 3]
            pos += 3
        o_refs = {}
        for nme in names:
            o_refs[nme] = refs[pos:pos + 4]
            pos += 4
        scw_out, fcw_out, loss_out = refs[pos], refs[pos + 1], refs[pos + 2]
        s_refs = dict(zip(classes, refs[pos + 3:]))
        for k in classes:
            acc = g_refs[k][0]
            for dev in range(1, N_DEV):
                acc = acc + g_refs[k][dev]
            s_refs[k][...] = acc

        def apply(nme, grad_of):
            w_ref, m_ref, v_ref = w_refs[nme]
            g_out, d_out, m_out, v_out = o_refs[nme]
            shape = w_ref.shape
            if len(shape) == 2:
                idxs = [(slice(l, l + 1),) for l in range(shape[0])]
            elif len(shape) == 3:
                idxs = [(l,) for l in range(shape[0])]
            else:
                idxs = [(l, h) for l in range(shape[0]) for h in range(shape[1])]
            for n_i, ix in enumerate(idxs):
                g = grad_of(n_i)
                delta, mn, vn = _adamw_math(w_ref[ix], g, m_ref[ix], v_ref[ix])
                g_out[ix] = g
                d_out[ix] = delta
                m_out[ix] = mn
                v_out[ix] = vn

        s1024, s1536, s2816, s16, s128, s6144, late1024, late6144 = (s_refs[k] for k in classes)
        s1024[0:1, :] += late1024[...]
        s6144[0:late6144.shape[0], :] += late6144[...]
        for n_i, nme in enumerate(_P1024):
            apply(nme, lambda l, b=2 * n_i: s1024[b + l:b + l + 1, :])
        apply("final_g", lambda l: s1024[10:11, :])
        apply("ssd_conv_b", lambda l: s1536[8 + l:9 + l, :])
        apply("ff_conv_b", lambda l: s2816[6 + l:7 + l, :])
        for n_i, nme in enumerate(_P16):
            apply(nme, lambda l, b=2 * n_i: s16[b + l:b + l + 1, :])
        apply("gm_ws", lambda q: s128[q * CHUNK:(q + 1) * CHUNK, :])
        apply("gm_bs", lambda l: s128[2048 + 8 * l:2048 + 8 * (l + 1), :])
        apply("ada_b", lambda l: s6144[2 * l:2 * l + 1, :] + s6144[2 * l + 1:2 * l + 2, :])
        for l in range(DEPTH):
            scw_out[l] = s1536[SSD_CONV * l:SSD_CONV * (l + 1), :]
            fcw_out[l] = s2816[FF_CONV * l:FF_CONV * (l + 1), :]
        loss_out[...] = s16[2 * len(_P16):2 * len(_P16) + 1, :]

    outs = pl.pallas_call(
        body, name="adamw_small",
        out_shape=out_shapes,
        scratch_shapes=scratch,
        compiler_params=pltpu.CompilerParams(vmem_limit_bytes=VMEM_LIMIT),
    )(*flat_in)
    res = {nme: tuple(outs[4 * i:4 * i + 4]) for i, nme in enumerate(names)}
    return res, outs[-3], outs[-2], outs[-1]


_WEIGHTS = ['ada_w', 'ada_b', 'norm1_g', 'norm2_g', 'w_in', 'ssd_conv_w', 'ssd_conv_b', 'ssd_dt_bias', 'ssd_a_log',
            'ssd_d', 'ssd_norm_g', 'gm_vnorm_g', 'gm_ws', 'gm_bs', 'gm_out_g', 'w_out', 'ff_up', 'ff_conv_w',
            'ff_conv_b', 'ff_down', 'final_g']


_O_XBC, _O_DT, _O_GM = D_SSD, D_SSD + CONV_DIM, D_SSD + CONV_DIM + SSD_HEADS


_TRANSPOSED = ("w_in", "ff_up")


def _full_weight(name, g):
    if name != "w_in":
        return g.reshape(g.shape[0] * g.shape[1], g.shape[2])
    rows = g.shape[1]
    k, r = divmod(_O_GM, rows)
    at = lambda j: COL_Z + j * rows if j <= k else j * rows - _O_GM

    def body(g_ref, o_ref):
        for j in range(N_DEV):
            if j == k:
                o_ref[at(k):at(k) + r, :] = g_ref[k, :r, :]
                o_ref[0:rows - r, :] = g_ref[k, r:, :]
            else:
                o_ref[at(j):at(j) + rows, :] = g_ref[j]
        o_ref[N_IN:, :] = jnp.zeros((N_INP - N_IN, g.shape[2]), g.dtype)

    return pl.pallas_call(
        body, name="w_in_rows", out_shape=jax.ShapeDtypeStruct((N_INP, g.shape[2]), g.dtype),
        compiler_params=pltpu.CompilerParams(vmem_limit_bytes=VMEM_LIMIT),
    )(g)


def _by_owner(name, grad):
    if name != "w_in":
        return grad.reshape(N_DEV, grad.shape[0] // N_DEV, grad.shape[1])
    rows = N_IN // N_DEV
    k, r = divmod(_O_GM, rows)
    at = lambda j: COL_Z + j * rows if j <= k else j * rows - _O_GM

    def body(g_ref, o_ref):
        for j in range(N_DEV):
            if j == k:
                o_ref[k, :r, :] = g_ref[at(k):at(k) + r, :]
                o_ref[k, r:, :] = g_ref[0:rows - r, :]
            else:
                o_ref[j] = g_ref[at(j):at(j) + rows, :]

    return pl.pallas_call(
        body, name="w_in_grad_blocks", out_shape=jax.ShapeDtypeStruct((N_DEV, rows, grad.shape[1]), grad.dtype),
        compiler_params=pltpu.CompilerParams(vmem_limit_bytes=VMEM_LIMIT),
    )(grad)


def kernel(x, c, ada_w, ada_b, norm1_g, norm2_g, w_in, ssd_conv_w, ssd_conv_b, ssd_dt_bias, ssd_a_log, ssd_d, ssd_norm_g, gm_vnorm_g, gm_ws, gm_bs, gm_out_g, w_out, ff_up, ff_conv_w, ff_conv_b, ff_down, final_g, loss_target, m_ada_w, m_ada_b, m_norm1_g, m_norm2_g, m_w_in, m_ssd_conv_w, m_ssd_conv_b, m_ssd_dt_bias, m_ssd_a_log, m_ssd_d, m_ssd_norm_g, m_gm_vnorm_g, m_gm_ws, m_gm_bs, m_gm_out_g, m_w_out, m_ff_up, m_ff_conv_w, m_ff_conv_b, m_ff_down, m_final_g, v_ada_w, v_ada_b, v_norm1_g, v_norm2_g, v_w_in, v_ssd_conv_w, v_ssd_conv_b, v_ssd_dt_bias, v_ssd_a_log, v_ssd_d, v_ssd_norm_g, v_gm_vnorm_g, v_gm_ws, v_gm_bs, v_gm_out_g, v_w_out, v_ff_up, v_ff_conv_w, v_ff_conv_b, v_ff_down, v_final_g):
    given = dict(locals())
    wts = {n: given[n] for n in _WEIGHTS}
    mom = {n: given["m_" + n] for n in _WEIGHTS}
    var = {n: given["v_" + n] for n in _WEIGHTS}
    nseq, seq, d = x.shape
    ix, iy, ic = lax.axis_index("x"), lax.axis_index("y"), lax.axis_index("c")
    me = 4 * ix + 2 * iy + ic
    me_arr = me.astype(jnp.int32).reshape(1)

    for nme, perm in (("ff_up", (0, 2, 1)), ("w_in", (2, 0, 1))):
        wts[nme], mom[nme], var[nme] = (jnp.transpose(a, perm) for a in (wts[nme], mom[nme], var[nme]))

    def shard(l, name):
        return _b(wts[name][:, l, :] if name == "w_in" else wts[name][l])

    g_scw, g_fcw, c_all = _all_gather([ssd_conv_w, ff_conv_w, c], "gather_first")
    scw_f = jnp.transpose(g_scw, (1, 2, 0, 3)).reshape(DEPTH, SSD_CONV, CONV_DIM)
    fcw_f = jnp.transpose(g_fcw, (1, 2, 0, 3)).reshape(DEPTH, FF_CONV, D_FF)
    c_all = c_all.reshape(N_DEV * nseq, d)

    n_ada = ada_w.shape[2]
    ada_b_shard = lax.dynamic_slice_in_dim(ada_b, me * n_ada, n_ada, axis=1).reshape(DEPTH, 1, n_ada)
    mod_part, c_act = _ada_fwd(c_all, ada_w, ada_b_shard)
    first_ssem, first_rsem, first_src, first_land, first_zero = _xc_start(
        False, [mod_part, shard(0, "w_in")], c_act, "ag_first_start", peers=[ALL_PEERS, OTHER_CHIPS])
    _, (mod_g,) = _xc_wait(False, first_ssem[:1], first_rsem[:1], first_src[:1], first_land[:1], c_act,
                           "mod_wait")
    mod_all = jnp.transpose(mod_g, (1, 2, 0, 3)).reshape(DEPTH, N_DEV * nseq, N_MOD * d)
    mod_mine = lax.dynamic_slice_in_dim(mod_all, me * nseq, nseq, axis=1)
    mod_k = jnp.transpose(mod_mine.reshape(DEPTH, nseq, N_MOD, 1, d), (0, 2, 1, 3, 4))
    mods = [[mod_k[l, k] for k in range(N_MOD)] for l in range(DEPTH)]

    later =[(0, "w_out"), (0, "ff_up"), (0, "ff_down"), (1, "w_in"), (1, "w_out"), (1, "ff_up"), (1, "ff_down")]
    ag_groups = {(0, "w_out"): [0], (0, "ff_up"): [1, 2], (1, "w_in"): [3, 4], (1, "ff_up"): [5, 6]}
    big_cache, ag = {}, {}

    def big_w(l, name, after):
        if (l, name) == (0, "w_in") and (l, name) not in big_cache:
            ag["ssem"], ag["rsem"], ag["src"], ag["land"], started = _xc_start(
                False, [shard(l2, n2) for l2, n2 in later], after, "ag_start")
            _, zones = _xc_wait(False, first_ssem[1:], first_rsem[1:], first_src[1:], first_land[1:],
                                jnp.full((8, 128), started, F32), "ag_first_wait", peers=OTHER_CHIPS)
            (zone,) = _sib_wait(*_sib_start(zones, "ag_first_sib_start"), "ag_first_sib_wait")
            big_cache[(l, name)] = _full_weight(name, zone)
        if (l, name) not in big_cache:
            idx = ag_groups[(l, name)]
            pick = lambda seq_: [seq_[i] for i in idx]
            _, lands = _xc_wait(False, pick(ag["ssem"]), pick(ag["rsem"]), pick(ag["src"]), pick(ag["land"]), after,
                                f"ag_wait_{l}_{name}")
            for i, land in zip(idx, lands):
                big_cache[later[i]] = _full_weight(later[i][1], land)
        return big_cache[(l, name)]

    small_w = dict(
        norm1_g=norm1_g + first_zero, norm2_g=norm2_g, ssd_conv_w=scw_f, ssd_conv_b=ssd_conv_b, ssd_dt_bias=ssd_dt_bias,
        ssd_a_log=ssd_a_log, ssd_d=ssd_d, ssd_norm_g=ssd_norm_g, gm_vnorm_g=gm_vnorm_g, gm_ws=gm_ws,
        gm_bst=jnp.transpose(gm_bs, (0, 2, 1)), gm_out_g=gm_out_g, ff_conv_w=fcw_f, ff_conv_b=ff_conv_b)

    outs = {}
    pending, win_parts = {}, {}

    def rs_finish(l, group, after):
        names, ssem, rsem, srcs, lands = pending.pop((l, group))
        srcs, lands = _xc_wait(True, ssem, rsem, srcs, lands, after, f"rs_wait_{l}_{group}")
        for nme, own, land in zip(names, srcs, lands):
            parts = [(own, lambda p: p[0])] + [(land, lambda p, k=k: k) for k in range(N_DEV - 1)]
            if nme == "w_in":
                win_parts[l] = parts
                if len(win_parts) == DEPTH:
                    outs[nme] = _adamw_rows_major([win_parts[k] for k in range(DEPTH)], wts[nme], mom[nme], var[nme],
                                                  me_arr, "adamw_w_in")
                continue
            outs[nme] = _adamw_layer(parts, wts[nme], mom[nme], var[nme], me_arr, l, outs.get(nme), f"adamw_{nme}_{l}")
        return land if names[-1] == "w_in" else outs[names[-1]][0]

    def grad_sink(l, group, grads, after):
        names = list(grads)
        ssem, rsem, srcs, lands, zero = _xc_start(True, [_by_owner(n, grads[n]) for n in names], after, f"rs_start_{l}_{group}")
        pending[(l, group)] = (names, ssem, rsem, srcs, lands)
        return zero.reshape(1, 1)

    early_gather = {}

    def small_sink(l, early, small, dmods, dfg, loss_p):
        if l > 0:
            return None
        layers = [dict(early, norm1_g=jnp.zeros((1, d), F32))] + small[1:]
        rows = lambda name: [layers[k][name] for k in range(DEPTH)]
        packed = [
            jnp.concatenate(sum([rows(n) for n in _P1024], []) + [dfg], axis=0),
            jnp.concatenate(rows("ssd_conv_w") + rows("ssd_conv_b"), axis=0),
            jnp.concatenate(rows("ff_conv_w") + rows("ff_conv_b"), axis=0),
            jnp.concatenate(sum([rows(n) for n in _P16], []) + [loss_p[:, :SSD_HEADS]], axis=0),
            jnp.concatenate([layers[k]["gm_ws"].reshape(GM_HEADS * CHUNK, CHUNK) for k in range(DEPTH)] + rows("gm_bs"), axis=0),
            jnp.concatenate([jnp.zeros((nseq, N_MOD * d), F32)] + dmods[1:], axis=0)]
        ssem, rsem, srcs, lands, zero = _xc_start(False, packed, packed[0], "small_start")
        early_gather.update(ssem=ssem, rsem=rsem, srcs=srcs, lands=lands)
        return zero.reshape(1, 1)

    grad_x, small, dmods = _local_step(
        x.reshape(nseq * seq, d), loss_target.reshape(nseq * seq, d), mods, small_w, final_g.reshape(1, d), nseq=nseq,
        big_w=big_w, grad_sink=grad_sink, small_sink=small_sink)

    done = grad_x
    for l, grp in ((1, "ffn"), (1, "w_out"), (1, "w_in"), (0, "ffn"), (0, "w_out")):
        done = rs_finish(l, grp, done)
    _, gathered = _xc_wait(False, early_gather["ssem"], early_gather["rsem"], early_gather["srcs"],
                           early_gather["lands"], done, "small_wait")
    gathered = list(gathered)
    gathered += _all_gather([small[0]["norm1_g"], dmods[0]], "gather_late", dep=gathered[0])
    gath = dict(zip(["p1024", "p1536", "p2816", "p16", "p128", "p6144", "late1024", "late6144"], gathered))

    dmod_all = jnp.concatenate([gath["late6144"].reshape(1, N_DEV * nseq, N_MOD * d),
                                jnp.transpose(gath["p6144"].reshape(N_DEV, DEPTH, nseq, N_MOD * d)[:, 1:], (1, 0, 2, 3)).reshape(
                                    DEPTH - 1, N_DEV * nseq, N_MOD * d)], axis=0)
    small_names = _P1024 + ["final_g", "ssd_conv_b", "ff_conv_b"] + _P16 + ["gm_ws", "gm_bs", "ada_b"]
    wmv = {}
    for nme in small_names:
        if nme == "final_g":
            wmv[nme] = tuple(a.reshape(1, d) for a in (wts[nme], mom[nme], var[nme]))
        else:
            wmv[nme] = (wts[nme], mom[nme], var[nme])
    small_out, scw_full, fcw_full, loss_sum = _adamw_small(gath, wmv)
    loss = loss_sum[0, 0]
    rs_finish(0, "w_in", scw_full)
    for nme in small_names:
        outs[nme] = small_out[nme]
    outs["final_g"] = tuple(a.reshape(d) for a in outs["final_g"])

    n_scw, n_fcw = ssd_conv_w.shape[2], ff_conv_w.shape[2]
    g_scw_mine = lax.dynamic_slice_in_dim(scw_full, me * n_scw, n_scw, axis=2)
    g_fcw_mine = lax.dynamic_slice_in_dim(fcw_full, me * n_fcw, n_fcw, axis=2)
    outs["ssd_conv_w"] = _adamw_sharded([(g_scw_mine, lambda p: 0)], ssd_conv_w, m_ssd_conv_w, v_ssd_conv_w, me_arr, "adamw_ssd_conv_w")
    outs["ff_conv_w"] = _adamw_sharded([(g_fcw_mine, lambda p: 0)], ff_conv_w, m_ff_conv_w, v_ff_conv_w, me_arr, "adamw_ff_conv_w")

    dmod_cols = _b(lax.dynamic_slice_in_dim(dmod_all, me * n_ada, n_ada, axis=2))
    g_ada = jnp.stack([_matmul(c_act, dmod_cols[l], ta=True, name=f"mm_ada_dw_{l}") for l in range(DEPTH)])
    outs["ada_w"] = _adamw_sharded([(g_ada, lambda p: 0)], ada_w, m_ada_w, v_ada_w, me_arr, "adamw_ada_w")

    for nme, perm in (("ff_up", (0, 2, 1)), ("w_in", (1, 2, 0))):
        outs[nme] = tuple(jnp.transpose(a, perm) for a in outs[nme])
    result = [loss, grad_x.reshape(nseq, seq, d)]
    for k in range(4):
        result += [outs[n][k] for n in _WEIGHTS]
    return tuple(result)
```

```python
import functools
import math

import jax
import jax.numpy as jnp
from jax import lax
from jax.experimental import pallas as pl
from jax.experimental.pallas import tpu as pltpu

F32 = jnp.float32
BF16 = jnp.bfloat16

N_DEV = 8
D_MODEL = 1024
DEPTH = 2
CHUNK = 128
SSD_HEADS = 16
SSD_HEAD_DIM = 64
SSD_GROUPS = 2
HEADS_PER_GROUP = SSD_HEADS // SSD_GROUPS
GROUP_WIDTH = HEADS_PER_GROUP * SSD_HEAD_DIM
D_STATE = 128
D_SSD = 1024
CONV_DIM = 1536
SSD_CONV = 4
GM_HEADS = 8
GM_HEAD_DIM = 128
D_GM = 1024
D_FF = 2816
FF_CONV = 3
N_IN = 4624
N_MOD = 6
EPS = 1e-6

N_INP = 5120
COL_U, COL_V, COL_Z, COL_XBC, COL_DT = 0, 1024, 2048, 3072, 4608

ADAM_LR = 0.001
ADAM_B1 = 0.9
ADAM_B2 = 0.999
ADAM_EPS = 1e-08
ADAM_WD = 0.01
ADAM_STEP = 10

VMEM_LIMIT = 56 * 1024 * 1024
MESH = pl.DeviceIdType.MESH
ANY = pl.BlockSpec(memory_space=pl.ANY)


def _cp(*sem):
    return pltpu.CompilerParams(dimension_semantics=sem, vmem_limit_bytes=VMEM_LIMIT)


def _tile(n, pref):
    if n <= pref or n % 128:
        return n
    best = 128
    for t in range(128, pref + 1, 128):
        if n % t == 0:
            best = t
    return best


def _per_layer(n):
    return pl.BlockSpec((DEPTH, n), lambda *_: (0, 0))


def _row(ref, layer, cols=slice(None)):
    return ref[layer:layer + 1, cols]


def _silu(x):
    return x * jax.nn.sigmoid(x)


def _gelu(x):
    return 0.5 * x * (1.0 + lax.erf(x * (1.0 / math.sqrt(2.0))))


def _softplus(x):
    return jnp.maximum(x, 0.0) + jnp.log1p(jnp.exp(-jnp.abs(x)))


def _b(x):
    return x.astype(BF16)


_NN = (((1,), (0,)), ((), ()))
_NT = (((1,), (1,)), ((), ()))
_TN = (((0,), (0,)), ((), ()))


def _dg(a, b, dn):
    return lax.dot_general(_b(a), _b(b), dn, preferred_element_type=F32)


@jax.custom_vjp
def _bdot(a, b):
    return _dg(a, b, _NN)


def _bdot_fwd(a, b):
    return _dg(a, b, _NN), (a, b)


def _bdot_bwd(res, ct):
    a, b = res
    return _dg(ct, b, _NT), _dg(a, ct, _TN)


_bdot.defvjp(_bdot_fwd, _bdot_bwd)


@jax.custom_vjp
def _bdot_nt(a, b):
    return _dg(a, b, _NT)


def _bdot_nt_fwd(a, b):
    return _dg(a, b, _NT), (a, b)


def _bdot_nt_bwd(res, ct):
    a, b = res
    return _dg(ct, b, _NN), _dg(ct, a, _TN)


_bdot_nt.defvjp(_bdot_nt_fwd, _bdot_nt_bwd)


@jax.custom_vjp
def _bdot_tn(a, b):
    return _dg(a, b, _TN)


def _bdot_tn_fwd(a, b):
    return _dg(a, b, _TN), (a, b)


def _bdot_tn_bwd(res, ct):
    a, b = res
    return _dg(b, ct, _NT), _dg(a, ct, _NN)


_bdot_tn.defvjp(_bdot_tn_fwd, _bdot_tn_bwd)


def _tri(n, lower):
    r = lax.broadcasted_iota(jnp.int32, (n, n), 0)
    c = lax.broadcasted_iota(jnp.int32, (n, n), 1)
    return ((r >= c) if lower else (r <= c)).astype(F32)


def _eye(n):
    r = lax.broadcasted_iota(jnp.int32, (n, n), 0)
    c = lax.broadcasted_iota(jnp.int32, (n, n), 1)
    return (r == c).astype(F32)


def _hdot(a, b, dn):
    return lax.dot_general(a, b, dn, precision=lax.Precision.HIGHEST, preferred_element_type=F32)


@jax.custom_vjp
def _cumsum_rows(x):
    return _hdot(_tri(x.shape[0], True), x, _NN)


def _cumsum_rows_fwd(x):
    return _cumsum_rows(x), None


def _cumsum_rows_bwd(_, ct):
    return (_hdot(_tri(ct.shape[0], False), ct, _NN),)


_cumsum_rows.defvjp(_cumsum_rows_fwd, _cumsum_rows_bwd)


@jax.custom_vjp
def _transpose(x):
    return _hdot(_eye(x.shape[1]), x, _NT)


def _transpose_fwd(x):
    return _transpose(x), None


def _transpose_bwd(_, ct):
    return (_hdot(_eye(ct.shape[1]), ct, _NT),)


_transpose.defvjp(_transpose_fwd, _transpose_bwd)


MXU_WIDTH = 256
MATMUL_TILE_CAP = 2816
MATMUL_VMEM = 44 * 1024 * 1024


def _mxu_tiles(n):
    if n <= MATMUL_TILE_CAP or n % 128:
        return [n]
    for unit in (MXU_WIDTH, 128):
        opts = [t for t in range(unit, MATMUL_TILE_CAP + 1, unit) if n % t == 0]
        if opts:
            return opts
    return [n]


def _matmul(a, b, *, ta=False, tb=False, name, dep=None, out_dtype=F32):
    pieces = list(a) if isinstance(a, (list, tuple)) else [a]
    npc = len(pieces)
    rows, width = pieces[0].shape
    assert all(p.shape == (rows, width) for p in pieces)
    if ta:
        k_dim, m_dim = rows, width * npc
    else:
        m_dim, k_dim = rows, width * npc
    if tb:
        n_dim, kb = b.shape
    else:
        kb, n_dim = b.shape
    assert kb == k_dim, (pieces[0].shape, npc, b.shape, ta, tb)
    m_unit = width if npc > 1 and ta else m_dim
    k_unit = width if npc > 1 and not ta else k_dim
    tm = _tile(m_unit, 1536)
    tn_opts, tk_opts = _mxu_tiles(n_dim), _mxu_tiles(k_unit)
    tn, tk = tn_opts.pop(), tk_opts.pop()
    while 4 * (tm * tk + tk * tn) + 8 * tm * tn > MATMUL_VMEM:
        if tn >= tk and tn_opts:
            tn = tn_opts.pop()
        else:
            tk = tk_opts.pop()
    ni, nj, nk = m_dim // tm, n_dim // tn, k_dim // tk
    per = width // (tm if ta else tk)
    dn = (((0 if ta else 1,), (1 if tb else 0,)), ((), ()))

    a_bytes, b_bytes = m_dim * k_dim, k_dim * n_dim
    m_outer = nk > 1 or a_bytes + b_bytes * ni <= b_bytes + a_bytes * nj
    if m_outer:
        ij = lambda o, n, k: (o, n)
        grid = (ni, nj, nk)
    else:
        ij = lambda o, n, k: (n, o)
        grid = (nj, ni, nk)

    use_acc = nk > 1 and out_dtype != F32

    def body(*refs):
        a_refs, b_ref = refs[:npc], refs[npc]
        o_ref = refs[-2] if use_acc else refs[-1]
        acc_ref = refs[-1]
        k = pl.program_id(2)
        i = pl.program_id(0 if m_outer else 1)
        along = i if ta else k

        def step(a_ref):
            p = lax.dot_general(a_ref[...], b_ref[...], dn, preferred_element_type=F32)
            if nk == 1:
                o_ref[...] = p.astype(out_dtype)
            else:
                @pl.when(k == 0)
                def _():
                    acc_ref[...] = p

                @pl.when((k > 0) & (k < nk - 1 if use_acc else True))
                def _():
                    acc_ref[...] += p

                if use_acc:
                    @pl.when(k == nk - 1)
                    def _():
                        o_ref[...] = (acc_ref[...] + p).astype(out_dtype)

        if npc == 1:
            step(a_refs[0])
        else:
            for pc in range(npc):
                pl.when((along >= pc * per) & (along < (pc + 1) * per))(functools.partial(step, a_refs[pc]))

    def a_map(pc, o, n, k):
        i, _ = ij(o, n, k)
        along = i if ta else k
        if npc > 1:
            along = jnp.clip(along - pc * per, 0, per - 1)
        return (k, along) if ta else (i, along)

    def b_map(o, n, k):
        _, j = ij(o, n, k)
        return (j, k) if tb else (k, j)

    extra = [] if dep is None else [dep]
    return pl.pallas_call(
        body, name=name,
        grid=grid,
        in_specs=[pl.BlockSpec((tk, tm) if ta else (tm, tk), functools.partial(a_map, pc)) for pc in range(npc)]
        + [pl.BlockSpec((tn, tk) if tb else (tk, tn), b_map)] + [ANY] * len(extra),
        out_specs=pl.BlockSpec((tm, tn), lambda o, n, k: ij(o, n, k)),
        out_shape=jax.ShapeDtypeStruct((m_dim, n_dim), out_dtype),
        scratch_shapes=[pltpu.VMEM((tm, tn), F32)] if use_acc else [],
        compiler_params=_cp("parallel", "parallel", "arbitrary"),
    )(*pieces, b, *extra)


def _ada_fwd(c_all, ada_w, ada_b_shard):
    depth, d, n = ada_w.shape
    nb = c_all.shape[0]

    def body(c_ref, w_ref, b_ref, o_ref, ca_ref):
        ca = _silu(c_ref[...])
        ca_ref[...] = _b(ca)
        o_ref[0] = _dg(ca, w_ref[0], _NN) + b_ref[0]

    return pl.pallas_call(
        body, name="ada_fwd",
        grid=(depth,),
        in_specs=[pl.BlockSpec((nb, d), lambda l: (0, 0)),
                  pl.BlockSpec((1, d, n), lambda l: (l, 0, 0)),
                  pl.BlockSpec((1, 1, n), lambda l: (l, 0, 0))],
        out_specs=[pl.BlockSpec((1, nb, n), lambda l: (l, 0, 0)),
                   pl.BlockSpec((nb, d), lambda l: (0, 0))],
        out_shape=[jax.ShapeDtypeStruct((depth, nb, n), F32), jax.ShapeDtypeStruct((nb, d), BF16)],
        compiler_params=_cp("arbitrary"),
    )(c_all, ada_w, ada_b_shard)


def _fold(acc):
    return jnp.sum(acc, axis=0, keepdims=True)


def _rinv(x):
    return lax.rsqrt(jnp.sum(x * x, axis=-1, keepdims=True) * (1.0 / D_MODEL) + EPS)


def _rms_bwd(a, xhat, rinv):
    return rinv * (a - xhat * (jnp.sum(a * xhat, axis=-1, keepdims=True) * (1.0 / D_MODEL)))


def _row_tile(seq):
    return min(seq, 256)


def _normmod_fwd(x, g, sc, sh, *, nseq, name, layer):
    t, d = x.shape
    seq = t // nseq
    tr = _row_tile(seq)
    nt = seq // tr
    row = pl.BlockSpec((tr, d), lambda s, i: (s * nt + i, 0))
    per_seq = pl.BlockSpec((1, 1, d), lambda s, i: (s, 0, 0))

    def body(x_ref, g_ref, sc_ref, sh_ref, h_ref):
        x_v = x_ref[...]
        h_ref[...] = _b(x_v * _rinv(x_v) * (_row(g_ref, layer) * (1.0 + sc_ref[0])) + sh_ref[0])

    return pl.pallas_call(
        body, name=name, grid=(nseq, nt),
        in_specs=[row, _per_layer(d), per_seq, per_seq],
        out_specs=row,
        out_shape=jax.ShapeDtypeStruct((t, d), BF16),
        compiler_params=_cp("parallel", "parallel"),
    )(x, g, sc, sh)


NORM_TM = 512


def _matmul_normbwd(a, b, dxo, x, delta, gate, g, sc, *, nseq, name, layer, dep=None):
    pieces = list(a) if isinstance(a, (list, tuple)) else [a]
    npc = len(pieces)
    t, width = pieces[0].shape
    k_dim, d = width * npc, b.shape[1]
    assert b.shape[0] == k_dim and all(p.shape == (t, width) for p in pieces)
    seq = t // nseq
    tm = min(NORM_TM, seq)
    per_seq_tiles = seq // tm
    tk = _mxu_tiles(width if npc > 1 else k_dim).pop()
    nk, per = k_dim // tk, width // tk
    has_delta = delta is not None
    extra = [] if dep is None else [dep]

    def body(*refs):
        a_refs, b_ref = refs[:npc], refs[npc]
        dxo_ref, x_ref = refs[npc + 1], refs[npc + 2]
        pos = npc + 3
        if has_delta:
            delta_ref, gate_ref = refs[pos], refs[pos + 1]
            pos += 2
        g_ref, sc_ref = refs[pos], refs[pos + 1]
        pos += 2 + len(extra)
        if has_delta:
            dx_ref, dd_ref, dgate_ref, dg_ref, dsc_ref, dsh_ref = refs[pos:pos + 6]
        else:
            dx_ref, dg_ref, dsc_ref, dsh_ref = refs[pos:pos + 4]
        acc_ref = refs[-1]
        i, k = pl.program_id(0), pl.program_id(1)

        def norm_bwd(dh_v):
            g_v, one_sc = _row(g_ref, layer), 1.0 + sc_ref[0]
            x_v = x_ref[...]
            rinv = _rinv(x_v)
            xhat = x_v * rinv
            dx = dxo_ref[...] + _rms_bwd(dh_v * (g_v * one_sc), xhat, rinv)
            dx_ref[...] = dx

            @pl.when(i == 0)
            def _():
                dg_ref[...] = jnp.zeros_like(dg_ref)

            @pl.when(i % per_seq_tiles == 0)
            def _():
                dsc_ref[...] = jnp.zeros_like(dsc_ref)
                dsh_ref[...] = jnp.zeros_like(dsh_ref)
                if has_delta:
                    dgate_ref[...] = jnp.zeros_like(dgate_ref)

            t_sum = _fold(dh_v * xhat)
            dg_ref[...] += t_sum * one_sc
            dsc_ref[0] += t_sum * g_v
            dsh_ref[0] += _fold(dh_v)
            if has_delta:
                dd_ref[...] = _b(dx * gate_ref[0])
                dgate_ref[0] += _fold(dx * delta_ref[...])

        def step(a_ref):
            p = lax.dot_general(a_ref[...], b_ref[...], _NN, preferred_element_type=F32)
            if nk == 1:
                norm_bwd(p)
            else:
                @pl.when(k == 0)
                def _():
                    acc_ref[...] = p

                @pl.when((k > 0) & (k < nk - 1))
                def _():
                    acc_ref[...] += p

                @pl.when(k == nk - 1)
                def _():
                    norm_bwd(acc_ref[...] + p)

        if npc == 1:
            step(a_refs[0])
        else:
            for pc in range(npc):
                pl.when((k >= pc * per) & (k < (pc + 1) * per))(functools.partial(step, a_refs[pc]))

    def a_map(pc, i, k):
        return (i, jnp.clip(k - pc * per, 0, per - 1) if npc > 1 else k)

    row = pl.BlockSpec((tm, d), lambda i, k: (i, 0))
    per_seq = pl.BlockSpec((1, 1, d), lambda i, k: (i // per_seq_tiles, 0, 0))
    vec = pl.BlockSpec((1, d), lambda i, k: (0, 0))
    shp = lambda *s, dt=F32: jax.ShapeDtypeStruct(s, dt)
    in_specs = [pl.BlockSpec((tm, tk), functools.partial(a_map, pc)) for pc in range(npc)]
    in_specs += [pl.BlockSpec((tk, d), lambda i, k: (k, 0)), row, row]
    operands = [*pieces, b, dxo, x]
    if has_delta:
        in_specs += [row, per_seq]
        operands += [delta, gate]
    in_specs += [_per_layer(d), per_seq] + [ANY] * len(extra)
    operands += [g, sc, *extra]
    if has_delta:
        out_specs = [row, row, per_seq, vec, per_seq, per_seq]
        out_shape = [shp(t, d), shp(t, d, dt=BF16), shp(nseq, 1, d), shp(1, d), shp(nseq, 1, d), shp(nseq, 1, d)]
    else:
        out_specs = [row, vec, per_seq, per_seq]
        out_shape = [shp(t, d), shp(1, d), shp(nseq, 1, d), shp(nseq, 1, d)]
    outs = pl.pallas_call(
        body, name=name, grid=(t // tm, nk),
        in_specs=in_specs, out_specs=out_specs, out_shape=out_shape,
        scratch_shapes=[pltpu.VMEM((tm, d), F32)],
        compiler_params=_cp("arbitrary", "arbitrary"),
    )(*operands)
    if has_delta:
        return tuple(outs)
    dx, dg, dsc, dsh = outs
    return dx, None, None, dg, dsc, dsh


def _matmul_normfwd(a, b, xin, gate, g, sc, sh, *, nseq, name, layer):
    t, k_dim = a.shape
    d = b.shape[1]
    assert b.shape[0] == k_dim and k_dim <= MATMUL_TILE_CAP
    seq = t // nseq
    tm = min(NORM_TM, seq)
    per_seq_tiles = seq // tm

    def body(a_ref, b_ref, xin_ref, gate_ref, g_ref, sc_ref, sh_ref, dl_ref, x_ref, h_ref):
        dl = lax.dot_general(a_ref[...], b_ref[...], _NN, preferred_element_type=F32)
        dl_ref[...] = dl
        x = xin_ref[...] + gate_ref[0] * dl
        x_ref[...] = x
        h_ref[...] = _b(x * _rinv(x) * (_row(g_ref, layer) * (1.0 + sc_ref[0])) + sh_ref[0])

    row = pl.BlockSpec((tm, d), lambda i: (i, 0))
    per_seq = pl.BlockSpec((1, 1, d), lambda i: (i // per_seq_tiles, 0, 0))
    return pl.pallas_call(
        body, name=name, grid=(t // tm,),
        in_specs=[pl.BlockSpec((tm, k_dim), lambda i: (i, 0)), pl.BlockSpec((k_dim, d), lambda i: (0, 0)),
                  row, per_seq, _per_layer(d), per_seq, per_seq],
        out_specs=[row, row, row],
        out_shape=[jax.ShapeDtypeStruct((t, d), F32), jax.ShapeDtypeStruct((t, d), F32), jax.ShapeDtypeStruct((t, d), BF16)],
        compiler_params=_cp("parallel"),
    )(a, b, xin, gate, g, sc, sh)


def _matmul_loss(a, b, xin, gate, fg, target, *, nseq, name):
    t, k_dim = a.shape
    d = b.shape[1]
    assert b.shape[0] == k_dim and k_dim <= MATMUL_TILE_CAP
    seq = t // nseq
    tm = min(NORM_TM, seq)
    per_seq_tiles = seq // tm

    def body(a_ref, b_ref, xin_ref, gate_ref, fg_ref, tgt_ref, dl_ref, loss_ref, dx_ref, dd_ref, dgate_ref, dfg_ref):
        i = pl.program_id(0)
        fg_v, gate_v = fg_ref[...], gate_ref[0]
        dl = lax.dot_general(a_ref[...], b_ref[...], _NN, preferred_element_type=F32)
        dl_ref[...] = dl
        x = xin_ref[...] + gate_v * dl
        rinv = _rinv(x)
        xhat = x * rinv
        err = xhat * fg_v - tgt_ref[...]
        dx = _rms_bwd(err * fg_v * (1.0 / d), xhat, rinv)
        dx_ref[...] = dx
        dd_ref[...] = _b(dx * gate_v)

        @pl.when(i == 0)
        def _():
            loss_ref[...] = jnp.zeros_like(loss_ref)
            dfg_ref[...] = jnp.zeros_like(dfg_ref)

        @pl.when(i % per_seq_tiles == 0)
        def _():
            dgate_ref[...] = jnp.zeros_like(dgate_ref)

        loss_ref[...] += jnp.sum(err * err) * (0.5 / d)
        dfg_ref[...] += _fold(err * xhat) * (1.0 / d)
        dgate_ref[0] += _fold(dx * dl)

    row = pl.BlockSpec((tm, d), lambda i: (i, 0))
    per_seq = pl.BlockSpec((1, 1, d), lambda i: (i // per_seq_tiles, 0, 0))
    vec = pl.BlockSpec((1, d), lambda i: (0, 0))
    return pl.pallas_call(
        body, name=name, grid=(t // tm,),
        in_specs=[pl.BlockSpec((tm, k_dim), lambda i: (i, 0)), pl.BlockSpec((k_dim, d), lambda i: (0, 0)),
                  row, per_seq, vec, row],
        out_specs=[row, pl.BlockSpec((1, 128), lambda i: (0, 0)), row, row, per_seq, vec],
        out_shape=[jax.ShapeDtypeStruct((t, d), F32), jax.ShapeDtypeStruct((1, 128), F32), jax.ShapeDtypeStruct((t, d), F32),
                   jax.ShapeDtypeStruct((t, d), BF16), jax.ShapeDtypeStruct((nseq, 1, d), F32),
                   jax.ShapeDtypeStruct((1, d), F32)],
        compiler_params=_cp("arbitrary"),
    )(a, b, xin, gate, fg, target)


CONV_TC = 256
CONV_LANES = 128
CONV_ROWS = 64
CONV_HALO = 8


def _conv_slabs(seq, fn):
    def step(i, carry):
        r0 = pl.multiple_of(i * CONV_ROWS, CONV_ROWS)
        for h in range(CONV_TC // CONV_LANES):
            fn(r0, slice(h * CONV_LANES, (h + 1) * CONV_LANES))
        return carry

    lax.fori_loop(0, seq // CONV_ROWS, step, 0)


def _slab(ref, r0, cols, seq):
    after = ref[pl.ds(pl.multiple_of(jnp.minimum(r0 + CONV_ROWS, seq - CONV_HALO), CONV_HALO), CONV_HALO), cols]
    return jnp.concatenate([ref[pl.ds(r0, CONV_ROWS), cols], jnp.where(r0 + CONV_ROWS < seq, after, 0.0)], axis=0)


def _conv_block(x, w_ref, b):
    kw = w_ref.shape[0]
    rows = lax.broadcasted_iota(jnp.int32, x.shape, 0)
    y = b + w_ref[kw - 1:kw, :] * x
    for j in range(1, kw):
        y = y + w_ref[kw - 1 - j:kw - j, :] * jnp.where(rows >= j, pltpu.roll(x, j, 0), 0.0)
    return y


def _conv_block_bwd(dy, x, w_ref, dw_ref, db_ref):
    kw = w_ref.shape[0]
    n = x.shape[0]
    rows = lax.broadcasted_iota(jnp.int32, x.shape, 0)
    dx = w_ref[kw - 1:kw, :] * dy
    dw_ref[kw - 1:kw, :] += jnp.sum(dy * x, axis=0, keepdims=True)
    for j in range(1, kw):
        dy_j = jnp.where(rows < n - j, pltpu.roll(dy, n - j, 0), 0.0)
        dx = dx + w_ref[kw - 1 - j:kw - j, :] * dy_j
        dw_ref[kw - 1 - j:kw - j, :] += jnp.sum(dy_j * x, axis=0, keepdims=True)
    db_ref[...] += jnp.sum(dy, axis=0, keepdims=True)
    return dx


def _conv_bwd(dy_ext, x, w_ref, dw_ref, db_ref, cols):
    kw = w_ref.shape[0]
    n = dy_ext.shape[0]
    dy = dy_ext[:CONV_ROWS]
    dx = w_ref[kw - 1:kw, cols] * dy
    dw_ref[kw - 1:kw, cols] += jnp.sum(dy * x, axis=0, keepdims=True)
    for j in range(1, kw):
        dy_j = pltpu.roll(dy_ext, n - j, 0)[:CONV_ROWS]
        dx = dx + w_ref[kw - 1 - j:kw - j, cols] * dy_j
        dw_ref[kw - 1 - j:kw - j, cols] += jnp.sum(dy_j * x, axis=0, keepdims=True)
    db_ref[:, cols] += jnp.sum(dy, axis=0, keepdims=True)
    return dx


def _dsilu(pre):
    sg = jax.nn.sigmoid(pre)
    return pre * sg, sg * (1.0 + pre * (1.0 - sg))


def _conv_specs(kw, layer):
    return [pl.BlockSpec((None, kw, CONV_TC), lambda j, s: (layer, 0, j)),
            pl.BlockSpec((DEPTH, CONV_TC), lambda j, s: (0, j))]


def _ssd_conv_fwd(proj, w, b, *, nseq, layer):
    t = proj.shape[0]
    seq = t // nseq
    nb = CONV_DIM // CONV_TC
    off = COL_XBC // CONV_TC

    def body(x_ref, w_ref, b_ref, o_ref, pre_ref):
        pre = _conv_block(x_ref[...], w_ref, _row(b_ref, layer))
        pre_ref[...] = pre
        o_ref[...] = _silu(pre)

    col = pl.BlockSpec((seq, CONV_TC), lambda j, s: (s, j))
    return pl.pallas_call(
        body, name="ssd_conv_fwd", grid=(nb, nseq),
        in_specs=[pl.BlockSpec((seq, CONV_TC), lambda j, s: (s, off + j)), *_conv_specs(SSD_CONV, layer)],
        out_specs=[col, col],
        out_shape=[jax.ShapeDtypeStruct((t, CONV_DIM), F32)] * 2,
        compiler_params=_cp("parallel", "parallel"),
    )(proj, w, b)


def _ssd_conv_bwd(dact, pre, proj, w, dproj, *, nseq, layer):
    t = proj.shape[0]
    seq = t // nseq
    nb = CONV_DIM // CONV_TC
    off = COL_XBC // CONV_TC

    def body(da_ref, pre_ref, x_ref, w_ref, dproj_ref, dx_ref, dw_ref, db_ref):
        del dproj_ref

        @pl.when(pl.program_id(1) == 0)
        def _():
            dw_ref[...] = jnp.zeros_like(dw_ref)
            db_ref[...] = jnp.zeros_like(db_ref)

        def slab(r0, cols):
            _, dsilu = _dsilu(_slab(pre_ref, r0, cols, seq))
            dpre_ext = _slab(da_ref, r0, cols, seq) * dsilu
            x = x_ref[pl.ds(r0, CONV_ROWS), cols]
            dx_ref[pl.ds(r0, CONV_ROWS), cols] = _b(_conv_bwd(dpre_ext, x, w_ref, dw_ref, db_ref, cols))

        _conv_slabs(seq, slab)

    return pl.pallas_call(
        body, name="ssd_conv_bwd", grid=(nb, nseq),
        in_specs=[pl.BlockSpec((seq, CONV_TC), lambda j, s: (s, j)),
                  pl.BlockSpec((seq, CONV_TC), lambda j, s: (s, j)),
                  pl.BlockSpec((seq, CONV_TC), lambda j, s: (s, off + j)),
                  _conv_specs(SSD_CONV, layer)[0],
                  ANY],
        out_specs=[pl.BlockSpec((seq, CONV_TC), lambda j, s: (s, off + j)),
                   pl.BlockSpec((SSD_CONV, CONV_TC), lambda j, s: (0, j)),
                   pl.BlockSpec((1, CONV_TC), lambda j, s: (0, j))],
        out_shape=[jax.ShapeDtypeStruct(dproj.shape, dproj.dtype), jax.ShapeDtypeStruct((SSD_CONV, CONV_DIM), F32),
                   jax.ShapeDtypeStruct((1, CONV_DIM), F32)],
        input_output_aliases={4: 0},
        compiler_params=_cp("parallel", "arbitrary"),
    )(dact, pre, proj, w, dproj)


def _ffn_act_fwd(up, w, b, *, nseq, layer):
    t = up.shape[0]
    seq = t // nseq
    nb = D_FF // CONV_TC

    def body(g_ref, v_ref, w_ref, b_ref, o_ref):
        pre = _conv_block(g_ref[...].astype(F32), w_ref, _row(b_ref, layer))
        o_ref[...] = _b(_silu(pre) * v_ref[...].astype(F32))

    col = pl.BlockSpec((seq, CONV_TC), lambda j, s: (s, j))
    return pl.pallas_call(
        body, name="ffn_act_fwd", grid=(nb, nseq),
        in_specs=[col,
                  pl.BlockSpec((seq, CONV_TC), lambda j, s: (s, nb + j)),
                  *_conv_specs(FF_CONV, layer)],
        out_specs=col,
        out_shape=jax.ShapeDtypeStruct((t, D_FF), BF16),
        compiler_params=_cp("parallel", "parallel"),
    )(up, up, w, b)


def _ffn_act_bwd(dact, up, w, b, *, nseq, layer):
    t = up.shape[0]
    seq = t // nseq
    nb = D_FF // CONV_TC

    def body(da_ref, g_ref, v_ref, w_ref, b_ref, dg_ref, dv_ref, dw_ref, db_ref):
        @pl.when(pl.program_id(1) == 0)
        def _():
            dw_ref[...] = jnp.zeros_like(dw_ref)
            db_ref[...] = jnp.zeros_like(db_ref)

        gate = g_ref[...].astype(F32)
        silu, dsilu = _dsilu(_conv_block(gate, w_ref, _row(b_ref, layer)))
        da = da_ref[...].astype(F32)
        dv_ref[...] = _b(da * silu)
        dg_ref[...] = _b(_conv_block_bwd(da * v_ref[...].astype(F32) * dsilu, gate, w_ref, dw_ref, db_ref))

    col = pl.BlockSpec((seq, CONV_TC), lambda j, s: (s, j))
    return pl.pallas_call(
        body, name="ffn_act_bwd", grid=(nb, nseq),
        in_specs=[col, col,
                  pl.BlockSpec((seq, CONV_TC), lambda j, s: (s, nb + j)),
                  *_conv_specs(FF_CONV, layer)],
        out_specs=[col, col,
                   pl.BlockSpec((FF_CONV, CONV_TC), lambda j, s: (0, j)),
                   pl.BlockSpec((1, CONV_TC), lambda j, s: (0, j))],
        out_shape=[jax.ShapeDtypeStruct((t, D_FF), BF16), jax.ShapeDtypeStruct((t, D_FF), BF16),
                   jax.ShapeDtypeStruct((FF_CONV, D_FF), F32), jax.ShapeDtypeStruct((1, D_FF), F32)],
        compiler_params=_cp("parallel", "arbitrary"),
    )(dact, up, up, w, b)


SSD_PAIRS = SSD_HEADS // 2
PAIR_W = 2 * SSD_HEAD_DIM
PAIRS_PER_GROUP = SSD_PAIRS // SSD_GROUPS


def _ssd_chunk(xs, bg, cg, dtr, z, hp, dtb, alog, dskip, ng):
    n = dtr.shape[0]
    dt = _softplus(dtr + dtb)
    cs = _cumsum_rows(dt * (-jnp.exp(alog)))
    cs_t = _transpose(cs)
    lane = lax.broadcasted_iota(jnp.int32, (1, SSD_HEADS), 1)
    sub = lax.broadcasted_iota(jnp.int32, (SSD_HEADS, 1), 0)
    row = lax.broadcasted_iota(jnp.int32, (n, 1), 0)
    causal = lax.broadcasted_iota(jnp.int32, (n, n), 0) >= lax.broadcasted_iota(jnp.int32, (n, n), 1)
    future = jnp.where(causal, 0.0, -1e30)
    first = lax.broadcasted_iota(jnp.int32, (1, PAIR_W), 1) < SSD_HEAD_DIM
    first_rows = lax.broadcasted_iota(jnp.int32, (PAIR_W, 1), 0) < SSD_HEAD_DIM
    first_f = first.astype(F32)
    cb = [_bdot_nt(cg[g], bg[g]) for g in range(SSD_GROUPS)]
    ys, hn = [], []
    for p in range(SSD_PAIRS):
        g = p // PAIRS_PER_GROUP
        col, decay, last = [], [], []
        for h in (2 * p, 2 * p + 1):
            oh = (lane == h).astype(F32)
            cs_h = jnp.sum(cs * oh, axis=1, keepdims=True)
            cs_row = jnp.sum(cs_t * (sub == h).astype(F32), axis=0, keepdims=True)
            col.append((jnp.sum(dt * oh, axis=1, keepdims=True), cs_h, jnp.sum(dskip * oh, axis=1, keepdims=True)))
            last.append(jnp.sum(jnp.where(row == n - 1, cs_h, 0.0), axis=0, keepdims=True))
            decay.append(jnp.exp(cs_h - cs_row + future))
        pair = lambda a, b: jnp.where(first, a, b)
        dt_p = pair(col[0][0], col[1][0])
        cs_p = pair(col[0][1], col[1][1])
        last_p = pair(last[0], last[1])
        xc = xs[p] * dt_p
        y = _bdot(cb[g] * decay[0], xc * first_f) + _bdot(cb[g] * decay[1], xc * (1.0 - first_f))
        y = y + _bdot_nt(cg[g], hp[p]) * jnp.exp(cs_p)
        y = y + pair(col[0][2], col[1][2]) * xs[p]
        keep = jnp.where(first_rows, jnp.exp(last[0]), jnp.exp(last[1]))
        hn.append(keep * hp[p] + _bdot_tn(xc * jnp.exp(last_p - cs_p), bg[g]))
        ys.append(y * _silu(z[p]))
    outs = []
    for g in range(SSD_GROUPS):
        ps = range(g * PAIRS_PER_GROUP, (g + 1) * PAIRS_PER_GROUP)
        ms = sum(jnp.sum(ys[p] * ys[p], axis=1, keepdims=True) for p in ps) * (1.0 / GROUP_WIDTH)
        r = lax.rsqrt(ms + EPS)
        outs += [ys[p] * r * ng[p] for p in ps]
    return outs, hn


def _hslices(ref, width, count, base=0, rows=slice(None)):
    return [ref[rows, base + k * width: base + (k + 1) * width] for k in range(count)]


def _ssd_load(xbc_ref, z_ref, dt_ref, ng_ref, layer):
    xs = _hslices(xbc_ref, PAIR_W, SSD_PAIRS)
    bg = _hslices(xbc_ref, D_STATE, SSD_GROUPS, D_SSD)
    cg = _hslices(xbc_ref, D_STATE, SSD_GROUPS, D_SSD + SSD_GROUPS * D_STATE)
    z = _hslices(z_ref, PAIR_W, SSD_PAIRS)
    ng = _hslices(ng_ref, PAIR_W, SSD_PAIRS, rows=slice(layer, layer + 1))
    return xs, bg, cg, dt_ref[:, 0:SSD_HEADS], z, ng


def _ssd_specs(nch):
    rowi = lambda s, c: s * nch + c
    return [pl.BlockSpec((CHUNK, CONV_DIM), lambda s, c: (rowi(s, c), 0)),
            pl.BlockSpec((CHUNK, D_SSD), lambda s, c: (rowi(s, c), COL_Z // D_SSD)),
            pl.BlockSpec((CHUNK, 128), lambda s, c: (rowi(s, c), COL_DT // 128)),
            _per_layer(SSD_HEADS), _per_layer(SSD_HEADS), _per_layer(SSD_HEADS), _per_layer(D_SSD)]


def _ssd_fwd(xbc, proj, dtb, alog, dskip, ng, *, nseq, layer):
    t = proj.shape[0]
    nch = t // nseq // CHUNK
    hd = PAIR_W

    def body(xbc_ref, z_ref, dt_ref, dtb_ref, alog_ref, dsk_ref, ng_ref, y_ref, hp_ref, h_ref):
        @pl.when(pl.program_id(1) == 0)
        def _():
            h_ref[...] = jnp.zeros_like(h_ref)

        xs, bg, cg, dtr, z, ngs = _ssd_load(xbc_ref, z_ref, dt_ref, ng_ref, layer)
        hp_ref[0] = h_ref[...]
        hp = [h_ref[h * hd:(h + 1) * hd, :] for h in range(SSD_PAIRS)]
        outs, hn = _ssd_chunk(xs, bg, cg, dtr, z, hp, _row(dtb_ref, layer), _row(alog_ref, layer), _row(dsk_ref, layer), ngs)
        for h in range(SSD_PAIRS):
            y_ref[:, h * hd:(h + 1) * hd] = _b(outs[h])
            h_ref[h * hd:(h + 1) * hd, :] = hn[h]

    return pl.pallas_call(
        body, name="ssd_fwd", grid=(nseq, nch),
        in_specs=_ssd_specs(nch),
        out_specs=[pl.BlockSpec((CHUNK, D_SSD), lambda s, c: (s * nch + c, 0)),
                   pl.BlockSpec((1, D_SSD, D_STATE), lambda s, c: (s * nch + c, 0, 0))],
        out_shape=[jax.ShapeDtypeStruct((t, D_SSD + D_GM), BF16),
                   jax.ShapeDtypeStruct((t // CHUNK, D_SSD, D_STATE), F32)],
        scratch_shapes=[pltpu.VMEM((D_SSD, D_STATE), F32)],
        compiler_params=_cp("arbitrary", "arbitrary"),
    )(xbc, proj, proj, dtb, alog, dskip, ng)


def _ssd_bwd(dy, xbc, proj, hprev, dtb, alog, dskip, ng, *, nseq, layer):
    t = proj.shape[0]
    nch = t // nseq // CHUNK
    hd = PAIR_W
    rev = lambda s, c: s * nch + (nch - 1 - c)

    def body(dy_ref, xbc_ref, z_ref, dt_ref, hp_ref, dtb_ref, alog_ref, dsk_ref, ng_ref,
             dxbc_ref, dproj_ref, ddtb_ref, dalog_ref, ddsk_ref, dng_ref, dh_ref):
        first = (pl.program_id(0) == 0) & (pl.program_id(1) == 0)

        @pl.when(pl.program_id(1) == 0)
        def _():
            dh_ref[...] = jnp.zeros_like(dh_ref)

        @pl.when(first)
        def _():
            ddtb_ref[...] = jnp.zeros_like(ddtb_ref)
            dalog_ref[...] = jnp.zeros_like(dalog_ref)
            ddsk_ref[...] = jnp.zeros_like(ddsk_ref)
            dng_ref[...] = jnp.zeros_like(dng_ref)

        xs, bg, cg, dtr, z, ngs = _ssd_load(xbc_ref, z_ref, dt_ref, ng_ref, layer)
        hp = [hp_ref[0, h * hd:(h + 1) * hd, :] for h in range(SSD_PAIRS)]
        _, vjp = jax.vjp(_ssd_chunk, xs, bg, cg, dtr, z, hp, _row(dtb_ref, layer), _row(alog_ref, layer), _row(dsk_ref, layer), ngs)
        douts = [dy_ref[:, h * hd:(h + 1) * hd] for h in range(SSD_PAIRS)]
        dhn = [dh_ref[h * hd:(h + 1) * hd, :] for h in range(SSD_PAIRS)]
        dxs, dbg, dcg, ddtr, dz, dhp, ddtb, dalog, ddsk, dngs = vjp((douts, dhn))
        dproj_ref[:, :COL_Z] = jnp.zeros((CHUNK, COL_Z), BF16)
        dproj_ref[:, COL_XBC:] = jnp.zeros((CHUNK, N_INP - COL_XBC), BF16)
        for h in range(SSD_PAIRS):
            dxbc_ref[:, h * hd:(h + 1) * hd] = dxs[h]
            dproj_ref[:, COL_Z + h * hd: COL_Z + (h + 1) * hd] = _b(dz[h])
            dh_ref[h * hd:(h + 1) * hd, :] = dhp[h]
            dng_ref[:, h * hd:(h + 1) * hd] += dngs[h]
        for g in range(SSD_GROUPS):
            dxbc_ref[:, D_SSD + g * D_STATE: D_SSD + (g + 1) * D_STATE] = dbg[g]
            dxbc_ref[:, D_SSD + (SSD_GROUPS + g) * D_STATE: D_SSD + (SSD_GROUPS + g + 1) * D_STATE] = dcg[g]
        dproj_ref[:, COL_DT:COL_DT + SSD_HEADS] = _b(ddtr)
        ddtb_ref[...] += ddtb
        dalog_ref[...] += dalog
        ddsk_ref[...] += ddsk

    small = pl.BlockSpec((1, SSD_HEADS), lambda s, c: (0, 0))
    return pl.pallas_call(
        body, name="ssd_bwd", grid=(nseq, nch),
        in_specs=[pl.BlockSpec((CHUNK, D_SSD), lambda s, c: (rev(s, c), 0)),
                  pl.BlockSpec((CHUNK, CONV_DIM), lambda s, c: (rev(s, c), 0)),
                  pl.BlockSpec((CHUNK, D_SSD), lambda s, c: (rev(s, c), COL_Z // D_SSD)),
                  pl.BlockSpec((CHUNK, 128), lambda s, c: (rev(s, c), COL_DT // 128)),
                  pl.BlockSpec((1, D_SSD, D_STATE), lambda s, c: (rev(s, c), 0, 0)),
                  _per_layer(SSD_HEADS), _per_layer(SSD_HEADS), _per_layer(SSD_HEADS), _per_layer(D_SSD)],
        out_specs=[pl.BlockSpec((CHUNK, CONV_DIM), lambda s, c: (rev(s, c), 0)),
                   pl.BlockSpec((CHUNK, N_INP), lambda s, c: (rev(s, c), 0)),
                   small, small, small,
                   pl.BlockSpec((1, D_SSD), lambda s, c: (0, 0))],
        out_shape=[jax.ShapeDtypeStruct((t, CONV_DIM), F32), jax.ShapeDtypeStruct((t, N_INP), BF16),
                   jax.ShapeDtypeStruct((1, SSD_HEADS), F32), jax.ShapeDtypeStruct((1, SSD_HEADS), F32),
                   jax.ShapeDtypeStruct((1, SSD_HEADS), F32), jax.ShapeDtypeStruct((1, D_SSD), F32)],
        scratch_shapes=[pltpu.VMEM((D_SSD, D_STATE), F32)],
        compiler_params=_cp("arbitrary", "arbitrary"),
    )(dy, xbc, proj, proj, hprev, dtb, alog, dskip, ng)


def _gmlp_chunk(gu, gv, ws, bs_cols, vg, og):
    n = gu[0].shape[0]
    mask = _tri(n, True)
    au = [_gelu(t) for t in gu]
    av = [_gelu(t) for t in gv]
    r = lax.rsqrt(sum(jnp.sum(t * t, axis=1, keepdims=True) for t in av) * (1.0 / D_GM) + EPS)
    p = []
    for h in range(GM_HEADS):
        sv = _bdot(ws[h] * mask, av[h] * r * vg[h]) + bs_cols[h]
        p.append(au[h] * sv)
    r2 = lax.rsqrt(sum(jnp.sum(t * t, axis=1, keepdims=True) for t in p) * (1.0 / D_GM) + EPS)
    return [p[h] * r2 * og[h] for h in range(GM_HEADS)]


def _gmlp_load(u_ref, v_ref, ws_ref, bst_ref, vg_ref, og_ref, layer):
    gu = _hslices(u_ref, GM_HEAD_DIM, GM_HEADS)
    gv = _hslices(v_ref, GM_HEAD_DIM, GM_HEADS)
    ws = [ws_ref[h] for h in range(GM_HEADS)]
    bs_cols = [bst_ref[:, h:h + 1] for h in range(GM_HEADS)]
    mine = slice(layer, layer + 1)
    return (gu, gv, ws, bs_cols, _hslices(vg_ref, GM_HEAD_DIM, GM_HEADS, rows=mine),
            _hslices(og_ref, GM_HEAD_DIM, GM_HEADS, rows=mine))


def _gmlp_specs(layer):
    return [pl.BlockSpec((CHUNK, D_GM), lambda i: (i, COL_U // D_GM)),
            pl.BlockSpec((CHUNK, D_GM), lambda i: (i, COL_V // D_GM)),
            pl.BlockSpec((None, GM_HEADS, CHUNK, CHUNK), lambda i: (layer, 0, 0, 0)),
            pl.BlockSpec((None, CHUNK, GM_HEADS), lambda i: (layer, 0, 0)),
            _per_layer(D_GM), _per_layer(D_GM)]


def _gmlp_fwd(proj, ycat, ws, bst, vg, og, *, layer):
    t = proj.shape[0]

    def body(u_ref, v_ref, ws_ref, bst_ref, vg_ref, og_ref, ycat_ref, o_ref):
        del ycat_ref
        outs = _gmlp_chunk(*_gmlp_load(u_ref, v_ref, ws_ref, bst_ref, vg_ref, og_ref, layer))
        for h in range(GM_HEADS):
            o_ref[:, h * GM_HEAD_DIM:(h + 1) * GM_HEAD_DIM] = _b(outs[h])

    return pl.pallas_call(
        body, name="gmlp_fwd", grid=(t // CHUNK,),
        in_specs=_gmlp_specs(layer) + [ANY],
        out_specs=pl.BlockSpec((CHUNK, D_GM), lambda i: (i, D_SSD // D_GM)),
        out_shape=jax.ShapeDtypeStruct(ycat.shape, ycat.dtype),
        input_output_aliases={6: 0},
        compiler_params=_cp("parallel"),
    )(proj, proj, ws, bst, vg, og, ycat)


def _gmlp_bwd(dy, proj, ws, bst, vg, og, dproj, *, layer):
    t = proj.shape[0]
    w = GM_HEAD_DIM

    def body(dy_ref, u_ref, v_ref, ws_ref, bst_ref, vg_ref, og_ref, dproj_ref,
             dgm_ref, dws_ref, dbst_ref, dvg_ref, dog_ref):
        del dproj_ref

        @pl.when(pl.program_id(0) == 0)
        def _():
            dws_ref[...] = jnp.zeros_like(dws_ref)
            dbst_ref[...] = jnp.zeros_like(dbst_ref)
            dvg_ref[...] = jnp.zeros_like(dvg_ref)
            dog_ref[...] = jnp.zeros_like(dog_ref)

        _, vjp = jax.vjp(_gmlp_chunk, *_gmlp_load(u_ref, v_ref, ws_ref, bst_ref, vg_ref, og_ref, layer))
        dgu, dgv, dws, dbs, dvg, dog = vjp(_hslices(dy_ref, w, GM_HEADS))
        for h in range(GM_HEADS):
            dgm_ref[:, h * w:(h + 1) * w] = _b(dgu[h])
            dgm_ref[:, D_GM + h * w: D_GM + (h + 1) * w] = _b(dgv[h])
            dws_ref[h] += dws[h]
            dbst_ref[:, h:h + 1] += dbs[h]
            dvg_ref[:, h * w:(h + 1) * w] += dvg[h]
            dog_ref[:, h * w:(h + 1) * w] += dog[h]

    return pl.pallas_call(
        body, name="gmlp_bwd", grid=(t // CHUNK,),
        in_specs=[pl.BlockSpec((CHUNK, D_GM), lambda i: (i, 1))] + _gmlp_specs(layer) + [ANY],
        out_specs=[pl.BlockSpec((CHUNK, 2 * D_GM), lambda i: (i, COL_U // (2 * D_GM))),
                   pl.BlockSpec((GM_HEADS, CHUNK, CHUNK), lambda i: (0, 0, 0)),
                   pl.BlockSpec((CHUNK, GM_HEADS), lambda i: (0, 0)),
                   pl.BlockSpec((1, D_GM), lambda i: (0, 0)),
                   pl.BlockSpec((1, D_GM), lambda i: (0, 0))],
        out_shape=[jax.ShapeDtypeStruct(dproj.shape, dproj.dtype), jax.ShapeDtypeStruct((GM_HEADS, CHUNK, CHUNK), F32),
                   jax.ShapeDtypeStruct((CHUNK, GM_HEADS), F32), jax.ShapeDtypeStruct((1, D_GM), F32),
                   jax.ShapeDtypeStruct((1, D_GM), F32)],
        input_output_aliases={7: 0},
        compiler_params=_cp("arbitrary"),
    )(dy, proj, proj, ws, bst, vg, og, dproj)


def _local_step(x, target, mods, w, final_g, *, nseq, big_w, grad_sink, small_sink):
    saved = []
    x0, delta, gate = x, None, None
    h1 = _normmod_fwd(x, w["norm1_g"], mods[0][1], mods[0][0], nseq=nseq, name="norm1_fwd_0", layer=0)
    for l in range(DEPTH):
        sh1, sc1, g1, sh2, sc2, g2 = mods[l]
        w_in = big_w(l, "w_in", h1)
        proj = _matmul(h1, w_in, tb=True, name=f"mm_in_{l}")
        xbc, xbc_pre = _ssd_conv_fwd(proj, w["ssd_conv_w"], w["ssd_conv_b"], nseq=nseq, layer=l)
        ycat, hprev = _ssd_fwd(xbc, proj, w["ssd_dt_bias"], w["ssd_a_log"], w["ssd_d"], w["ssd_norm_g"], nseq=nseq,
                               layer=l)
        ycat = _gmlp_fwd(proj, ycat, w["gm_ws"], w["gm_bst"], w["gm_vnorm_g"], w["gm_out_g"], layer=l)
        w_out = big_w(l, "w_out", ycat)
        mix, x1, h2 = _matmul_normfwd(ycat, w_out, x0, g1, w["norm2_g"], sc2, sh2, nseq=nseq, name=f"mm_out_{l}",
                                      layer=l)
        ff_up = big_w(l, "ff_up", h2)
        up = _matmul(h2, ff_up, tb=True, name=f"mm_up_{l}", out_dtype=BF16)
        act = _ffn_act_fwd(up, w["ff_conv_w"], w["ff_conv_b"], nseq=nseq, layer=l)
        ff_down = big_w(l, "ff_down", act)
        sv = dict(x0=x0, xin_delta=delta, xin_gate=gate, h1=h1, proj=proj, xbc=xbc, xbc_pre=xbc_pre, hprev=hprev,
                  ycat=ycat, mix=mix, x1=x1, h2=h2, up=up, act=act,
                  w_in=w_in, w_out=w_out, ff_up=ff_up, ff_down=ff_down)
        if l + 1 < DEPTH:
            nsh1, nsc1 = mods[l + 1][0], mods[l + 1][1]
            dn, x0, h1 = _matmul_normfwd(act, ff_down, x1, g2, w["norm1_g"], nsc1, nsh1, nseq=nseq,
                                         name=f"mm_down_{l}", layer=l + 1)
        else:
            dn, loss, dx, ddelta, dgate, dfg = _matmul_loss(act, ff_down, x1, g2, final_g, target, nseq=nseq,
                                                            name=f"mm_down_{l}")
        saved.append(dict(sv, dn=dn))
        delta, gate = dn, g2

    small, dmods = [None] * DEPTH, [None] * DEPTH
    for l in reversed(range(DEPTH)):
        sv = saved[l]
        sh1, sc1, g1, sh2, sc2, g2 = mods[l]
        dg2 = dgate
        g_ff_down = _matmul(sv["act"], ddelta, ta=True, name=f"mm_down_dw_{l}", out_dtype=BF16)
        dact = _matmul(ddelta, sv["ff_down"], tb=True, name=f"mm_down_dx_{l}", out_dtype=BF16)
        dgate_ff, dval_ff, dfcw, dfcb = _ffn_act_bwd(dact, sv["up"], w["ff_conv_w"], w["ff_conv_b"], nseq=nseq, layer=l)
        g_ff_up = _matmul([dgate_ff, dval_ff], sv["h2"], ta=True, name=f"mm_up_dw_{l}", out_dtype=BF16)
        dep = grad_sink(l, "ffn", dict(ff_down=g_ff_down, ff_up=g_ff_up), dval_ff)
        dx, dmix, dg1, dn2g, dsc2, dsh2 = _matmul_normbwd([dgate_ff, dval_ff], sv["ff_up"], dx, sv["x1"], sv["mix"], g1,
                                                          w["norm2_g"], sc2, nseq=nseq, name=f"mm_up_dx_{l}", layer=l,
                                                          dep=dep)
        g_w_out = _matmul(sv["ycat"], dmix, ta=True, name=f"mm_out_dw_{l}", out_dtype=BF16)
        dep = grad_sink(l, "w_out", dict(w_out=g_w_out), dmix)
        dycat = _matmul(dmix, sv["w_out"], tb=True, name=f"mm_out_dx_{l}", dep=dep)
        dxbc_act, dproj, ddtb, dalog, ddsk, dng = _ssd_bwd(dycat, sv["xbc"], sv["proj"], sv["hprev"], w["ssd_dt_bias"],
                                                          w["ssd_a_log"], w["ssd_d"], w["ssd_norm_g"], nseq=nseq, layer=l)
        dproj, dscw, dscb = _ssd_conv_bwd(dxbc_act, sv["xbc_pre"], sv["proj"], w["ssd_conv_w"], dproj, nseq=nseq,
                                          layer=l)
        dproj, dws, dbst, dvg, dog = _gmlp_bwd(dycat, sv["proj"], w["gm_ws"], w["gm_bst"], w["gm_vnorm_g"], w["gm_out_g"],
                                               dproj, layer=l)
        early = dict(norm2_g=dn2g, ssd_norm_g=dng, gm_vnorm_g=dvg, gm_out_g=dog,
                     ssd_conv_w=dscw, ssd_conv_b=dscb, ff_conv_w=dfcw, ff_conv_b=dfcb,
                     ssd_dt_bias=ddtb, ssd_a_log=dalog, ssd_d=ddsk, gm_ws=dws, gm_bs=dbst.T)
        dep = small_sink(l, early, small, dmods, dfg, loss)
        g_w_in = _matmul(dproj, sv["h1"], ta=True, name=f"mm_in_dw_{l}", out_dtype=BF16, dep=dep)
        dep = grad_sink(l, "w_in", dict(w_in=g_w_in), dproj)
        dx, ddelta, dgate, dn1g, dsc1, dsh1 = _matmul_normbwd(dproj, sv["w_in"], dx, sv["x0"], sv["xin_delta"],
                                                              sv["xin_gate"], w["norm1_g"], sc1, nseq=nseq,
                                                              name=f"mm_in_dx_{l}", layer=l, dep=dep)
        small[l] = dict(early, norm1_g=dn1g)
        dmods[l] = jnp.concatenate([dsh1, dsc1, dg1, dsh2, dsc2, dg2], axis=-1)[:, 0, :]
    return dx, small, dmods


def _all_gather(arrs, name, dep=None):
    n = len(arrs)
    extra = [] if dep is None else [dep]

    def body(*refs):
        ins, outs = refs[:n], refs[n + len(extra):2 * n + len(extra)]
        send_sems, recv_sems, local_sems = refs[2 * n + len(extra):]
        x, y, c = lax.axis_index("x"), lax.axis_index("y"), lax.axis_index("c")
        me, sibling = (x, y, c), (x, y, 1 - c)
        chips = [(1 - x, y), (x, 1 - y), (1 - x, 1 - y)]

        def copy(i, k, block, to, src=None):
            px, py, pc = block
            dst = outs[i].at[4 * px + 2 * py + pc]
            return pltpu.make_async_remote_copy(
                src_ref=dst if src is None else src, dst_ref=dst,
                send_sem=send_sems.at[7 * i + k], recv_sem=recv_sems.at[7 * i + k],
                device_id=to, device_id_type=MESH)

        mine = [pltpu.make_async_copy(ins[i], outs[i].at[4 * x + 2 * y + c], local_sems.at[i]) for i in range(n)]
        for cp in mine:
            cp.start()
        first = []
        for i in range(n):
            first.append(copy(i, 0, me, sibling, src=ins[i]))
            first += [copy(i, 1 + j, me, (*chip, c), src=ins[i]) for j, chip in enumerate(chips)]
        for cp in first:
            cp.start()
        passed = []
        for j, chip in enumerate(chips):
            for i in range(n):
                copy(i, 1 + j, (*chip, c), me).wait_recv()
                fwd = copy(i, 4 + j, (*chip, c), sibling)
                fwd.start()
                passed.append(fwd)
        for i in range(n):
            copy(i, 0, sibling, me).wait_recv()
            for j, chip in enumerate(chips):
                copy(i, 4 + j, (*chip, 1 - c), me).wait_recv()
        for cp in first + passed:
            cp.wait_send()
        for cp in mine:
            cp.wait()

    return pl.pallas_call(
        body, name=name,
        in_specs=[ANY] * (n + len(extra)), out_specs=[ANY] * n,
        out_shape=[jax.ShapeDtypeStruct((N_DEV,) + a.shape, a.dtype) for a in arrs],
        scratch_shapes=[pltpu.SemaphoreType.DMA((7 * n,)), pltpu.SemaphoreType.DMA((7 * n,)),
                        pltpu.SemaphoreType.DMA((n,))],
    )(*arrs, *extra)


HBM = pl.BlockSpec(memory_space=pltpu.HBM)
SEM = pl.BlockSpec(memory_space=pltpu.SEMAPHORE)
EFFECT = pltpu.SideEffectType.DATAFLOW_SIDE_EFFECTING


def _peer(k):
    x, y, c = lax.axis_index("x"), lax.axis_index("y"), lax.axis_index("c")
    return (1 - x if k & 4 else x, 1 - y if k & 2 else y, 1 - c if k & 1 else c)


ALL_PEERS = tuple(range(1, N_DEV))
OTHER_CHIPS = (2, 4, 6)


def _xc_copies(scatter, srcs, lands, send_sems, recv_sems, peers=ALL_PEERS):
    x, y, c = lax.axis_index("x"), lax.axis_index("y"), lax.axis_index("c")
    copies = []
    for i in range(len(srcs)):
        for k in (peers[i] if isinstance(peers[0], tuple) else peers):
            px, py, pc = _peer(k)
            src = srcs[i].at[4 * px + 2 * py + pc] if scatter else srcs[i]
            dst = lands[i].at[k - 1] if scatter else lands[i].at[4 * x + 2 * y + c]
            copies.append(pltpu.make_async_remote_copy(
                src_ref=src, dst_ref=dst, send_sem=send_sems[i].at[k - 1], recv_sem=recv_sems[i].at[k - 1],
                device_id=(px, py, pc), device_id_type=MESH))
    return copies


def _xc_own(scatter, srcs, lands, send_sems):
    if scatter:
        return []
    me = 4 * lax.axis_index("x") + 2 * lax.axis_index("y") + lax.axis_index("c")
    return [pltpu.make_async_copy(srcs[i], lands[i].at[me], send_sems[i].at[N_DEV - 1]) for i in range(len(srcs))]


def _xc_start(scatter, arrs, after, name, peers=ALL_PEERS):
    n = len(arrs)
    lands = [lax.empty((N_DEV - 1,) + a.shape[1:] if scatter else (N_DEV,) + a.shape, a.dtype) for a in arrs]

    def body(*refs):
        srcs, lnd = refs[:n], refs[n:2 * n]
        send_sems, recv_sems = refs[2 * n + 1:3 * n + 1], refs[3 * n + 1:4 * n + 1]
        token = refs[6 * n + 1]
        for cp in _xc_copies(scatter, srcs, lnd, send_sems, recv_sems, peers) + _xc_own(scatter, srcs, lnd, send_sems):
            cp.start()
        token[...] = jnp.zeros_like(token)

    outs = pl.pallas_call(
        body, name=name,
        out_shape=[pltpu.SemaphoreType.DMA((N_DEV,))] * (2 * n)
        + [pltpu.HBM(a.shape, a.dtype) for a in arrs] + [pltpu.HBM(a.shape, a.dtype) for a in lands]
        + [jax.ShapeDtypeStruct((8, 128), F32)],
        in_specs=[HBM] * (2 * n) + [ANY],
        out_specs=[SEM] * (2 * n) + [HBM] * (2 * n) + [pl.BlockSpec(memory_space=pltpu.VMEM)],
        input_output_aliases={i: 2 * n + i for i in range(2 * n)},
        compiler_params=pltpu.CompilerParams(has_side_effects=EFFECT),
    )(*[pltpu.with_memory_space_constraint(a, pltpu.HBM) for a in list(arrs) + lands], after)
    return outs[:n], outs[n:2 * n], outs[2 * n:3 * n], outs[3 * n:4 * n], outs[4 * n][0, 0]


def _xc_wait(scatter, send_sems, recv_sems, srcs, lands, after, name, peers=ALL_PEERS):
    n = len(srcs)

    def body(*refs):
        s_refs, l_refs = refs[:n], refs[n:2 * n]
        ss, rs = refs[2 * n:3 * n], refs[3 * n:4 * n]
        for cp in _xc_copies(scatter, s_refs, l_refs, ss, rs, peers):
            cp.wait_send()
            cp.wait_recv()
        for cp in _xc_own(scatter, s_refs, l_refs, ss):
            cp.wait()

    outs = pl.pallas_call(
        body, name=name,
        out_shape=[pltpu.HBM(a.shape, a.dtype) for a in list(srcs) + list(lands)],
        in_specs=[HBM] * (2 * n) + [SEM] * (2 * n) + [ANY],
        out_specs=[HBM] * (2 * n),
        input_output_aliases={i: i for i in range(2 * n)},
        compiler_params=pltpu.CompilerParams(has_side_effects=EFFECT),
    )(*srcs, *lands, *send_sems, *recv_sems, after)
    return outs[:n], outs[n:]


def _sib_copies(zones, send_sems, recv_sems):
    x, y, c = lax.axis_index("x"), lax.axis_index("y"), lax.axis_index("c")
    copies = []
    for i in range(len(zones)):
        for q in range(N_DEV // 2):
            slot = zones[i].at[2 * q + c]
            copies.append(pltpu.make_async_remote_copy(
                src_ref=slot, dst_ref=slot, send_sem=send_sems[i].at[q], recv_sem=recv_sems[i].at[q],
                device_id=(x, y, 1 - c), device_id_type=MESH))
    return copies


def _sib_start(zones, name):
    n = len(zones)

    def body(*refs):
        for cp in _sib_copies(refs[:n], refs[n:2 * n], refs[2 * n:3 * n]):
            cp.start()

    outs = pl.pallas_call(
        body, name=name,
        out_shape=[pltpu.SemaphoreType.DMA((N_DEV // 2,))] * (2 * n) + [pltpu.HBM(a.shape, a.dtype) for a in zones],
        in_specs=[HBM] * n,
        out_specs=[SEM] * (2 * n) + [HBM] * n,
        input_output_aliases={i: 2 * n + i for i in range(n)},
        compiler_params=pltpu.CompilerParams(has_side_effects=EFFECT),
    )(*[pltpu.with_memory_space_constraint(a, pltpu.HBM) for a in zones])
    return outs[:n], outs[n:2 * n], outs[2 * n:]


def _sib_wait(send_sems, recv_sems, zones, name):
    n = len(zones)

    def body(*refs):
        for cp in _sib_copies(refs[:n], refs[n:2 * n], refs[2 * n:3 * n]):
            cp.wait_send()
            cp.wait_recv()

    return pl.pallas_call(
        body, name=name,
        out_shape=[pltpu.HBM(a.shape, a.dtype) for a in zones],
        in_specs=[HBM] * n + [SEM] * (2 * n),
        out_specs=[HBM] * n,
        input_output_aliases={i: i for i in range(n)},
        compiler_params=pltpu.CompilerParams(has_side_effects=EFFECT),
    )(*zones, *send_sems, *recv_sems)


def _adamw_math(w, g, m, v):
    m = ADAM_B1 * m + (1.0 - ADAM_B1) * g
    v = ADAM_B2 * v + (1.0 - ADAM_B2) * (g * g)
    m_hat = m / (1.0 - ADAM_B1 ** ADAM_STEP)
    v_hat = v / (1.0 - ADAM_B2 ** ADAM_STEP)
    delta = -ADAM_LR * (m_hat / (jnp.sqrt(v_hat) + ADAM_EPS) + ADAM_WD * w)
    return delta, m, v


def _adamw_sharded(parts, w, m, v, pos, name):
    depth, rows, cols = w.shape
    tr = _tile(rows, 256) if rows % 8 == 0 else rows
    npart = len(parts)

    def body(pos_ref, *refs):
        prefs = refs[:npart]
        w_ref, m_ref, v_ref, g_out, d_out, m_out, v_out = refs[npart:]
        g = prefs[0][...]
        for pr in prefs[1:]:
            g = g + pr[...]
        delta, mn, vn = _adamw_math(w_ref[...], g, m_ref[...], v_ref[...])
        g_out[...] = g
        d_out[...] = delta
        m_out[...] = mn
        v_out[...] = vn

    def part_spec(fn):
        return pl.BlockSpec((1, tr, cols), lambda l, i, p: (fn(p) * depth + l, i, 0))

    blk = pl.BlockSpec((1, tr, cols), lambda l, i, p: (l, i, 0))
    shp = jax.ShapeDtypeStruct((depth, rows, cols), F32)
    return pl.pallas_call(
        body, name=name,
        grid_spec=pltpu.PrefetchScalarGridSpec(
            num_scalar_prefetch=1, grid=(depth, rows // tr),
            in_specs=[part_spec(fn) for _, fn in parts] + [blk, blk, blk],
            out_specs=[blk, blk, blk, blk]),
        out_shape=[shp, shp, shp, shp],
        compiler_params=_cp("parallel", "parallel"),
    )(pos, *[a for a, _ in parts], w, m, v)


def _adamw_layer(parts, w, m, v, pos, layer, prev, name):
    depth, rows, cols = w.shape
    npart = len(parts)
    nprev = 0 if prev is None else 4
    if rows % 16 == 0:
        tr, tc = max(t for t in range(16, 257, 16) if rows % t == 0), cols
    else:
        tr, tc = rows, _tile(cols, 256)
    pick = (lambda i: (i, 0)) if rows % 16 == 0 else (lambda i: (0, i))

    def body(pos_ref, *refs):
        prefs = refs[:npart]
        w_ref, m_ref, v_ref = refs[npart:npart + 3]
        g_out, d_out, m_out, v_out = refs[npart + 3 + nprev:]
        g = prefs[0][...].astype(F32)
        for pr in prefs[1:]:
            g = g + pr[...].astype(F32)
        delta, mn, vn = _adamw_math(w_ref[...], g, m_ref[...], v_ref[...])
        g_out[...] = g
        d_out[...] = delta
        m_out[...] = mn
        v_out[...] = vn

    def part_spec(fn):
        return pl.BlockSpec((1, tr, tc), lambda i, p: (fn(p), *pick(i)))

    blk = pl.BlockSpec((1, tr, tc), lambda i, p: (layer, *pick(i)))
    shp = jax.ShapeDtypeStruct((depth, rows, cols), F32)
    first_prev = 1 + npart + 3
    return pl.pallas_call(
        body, name=name,
        grid_spec=pltpu.PrefetchScalarGridSpec(
            num_scalar_prefetch=1, grid=(rows // tr * (cols // tc),),
            in_specs=[part_spec(fn) for _, fn in parts] + [blk, blk, blk] + [ANY] * nprev,
            out_specs=[blk, blk, blk, blk]),
        out_shape=[shp, shp, shp, shp],
        input_output_aliases={first_prev + j: j for j in range(nprev)},
        compiler_params=_cp("parallel"),
    )(pos, *[a for a, _ in parts], w, m, v, *(prev or ()))


def _adamw_rows_major(parts_by_layer, w, m, v, pos, name):
    rows, depth, cols = w.shape
    tc = _tile(cols, 256)
    npart = len(parts_by_layer[0])

    def body(pos_ref, *refs):
        prefs = refs[:depth * npart]
        w_ref, m_ref, v_ref, g_out, d_out, m_out, v_out = refs[depth * npart:]
        for l in range(depth):
            g = prefs[l * npart][0].astype(F32)
            for pr in prefs[l * npart + 1:(l + 1) * npart]:
                g = g + pr[0].astype(F32)
            delta, mn, vn = _adamw_math(w_ref[:, l, :], g, m_ref[:, l, :], v_ref[:, l, :])
            g_out[:, l, :] = g
            d_out[:, l, :] = delta
            m_out[:, l, :] = mn
            v_out[:, l, :] = vn

    def part_spec(fn):
        return pl.BlockSpec((1, rows, tc), lambda j, p: (fn(p), 0, j))

    blk = pl.BlockSpec((rows, depth, tc), lambda j, p: (0, 0, j))
    shp = jax.ShapeDtypeStruct(w.shape, F32)
    flat = [pf for parts in parts_by_layer for pf in parts]
    return pl.pallas_call(
        body, name=name,
        grid_spec=pltpu.PrefetchScalarGridSpec(
            num_scalar_prefetch=1, grid=(cols // tc,),
            in_specs=[part_spec(fn) for _, fn in flat] + [blk, blk, blk],
            out_specs=[blk, blk, blk, blk]),
        out_shape=[shp, shp, shp, shp],
        compiler_params=_cp("parallel"),
    )(pos, *[a for a, _ in flat], w, m, v)


_P1024 = ["norm1_g", "norm2_g", "ssd_norm_g", "gm_vnorm_g", "gm_out_g"]
_P16 = ["ssd_dt_bias", "ssd_a_log", "ssd_d"]


def _adamw_small(gath, wmv):
    names = list(wmv.keys())
    classes = list(gath.keys())
    flat_in = [gath[k] for k in classes]
    for nme in names:
        flat_in += list(wmv[nme])
    out_shapes = []
    for nme in names:
        out_shapes += [jax.ShapeDtypeStruct(wmv[nme][0].shape, F32)] * 4
    out_shapes += [jax.ShapeDtypeStruct((DEPTH, SSD_CONV, CONV_DIM), F32), jax.ShapeDtypeStruct((DEPTH, FF_CONV, D_FF), F32),
                   jax.ShapeDtypeStruct((1, SSD_HEADS), F32)]
    scratch = [pltpu.VMEM(gath[k].shape[1:], F32) for k in classes]
    ncls = len(classes)

    def body(*refs):
        g_refs = dict(zip(classes, refs[:ncls]))
        pos = ncls
        w_refs = {}
        for nme in names:
            w_refs[nme] = refs[pos:pos + 3]
            pos += 3
        o_refs = {}
        for nme in names:
            o_refs[nme] = refs[pos:pos + 4]
            pos += 4
        scw_out, fcw_out, loss_out = refs[pos], refs[pos + 1], refs[pos + 2]
        s_refs = dict(zip(classes, refs[pos + 3:]))
        for k in classes:
            acc = g_refs[k][0]
            for dev in range(1, N_DEV):
                acc = acc + g_refs[k][dev]
            s_refs[k][...] = acc

        def apply(nme, grad_of):
            w_ref, m_ref, v_ref = w_refs[nme]
            g_out, d_out, m_out, v_out = o_refs[nme]
            shape = w_ref.shape
            if len(shape) == 2:
                idxs = [(slice(l, l + 1),) for l in range(shape[0])]
            elif len(shape) == 3:
                idxs = [(l,) for l in range(shape[0])]
            else:
                idxs = [(l, h) for l in range(shape[0]) for h in range(shape[1])]
            for n_i, ix in enumerate(idxs):
                g = grad_of(n_i)
                delta, mn, vn = _adamw_math(w_ref[ix], g, m_ref[ix], v_ref[ix])
                g_out[ix] = g
                d_out[ix] = delta
                m_out[ix] = mn
                v_out[ix] = vn

        s1024, s1536, s2816, s16, s128, s6144, late1024, late6144 = (s_refs[k] for k in classes)
        s1024[0:1, :] += late1024[...]
        s6144[0:late6144.shape[0], :] += late6144[...]
        for n_i, nme in enumerate(_P1024):
            apply(nme, lambda l, b=2 * n_i: s1024[b + l:b + l + 1, :])
        apply("final_g", lambda l: s1024[10:11, :])
        apply("ssd_conv_b", lambda l: s1536[8 + l:9 + l, :])
        apply("ff_conv_b", lambda l: s2816[6 + l:7 + l, :])
        for n_i, nme in enumerate(_P16):
            apply(nme, lambda l, b=2 * n_i: s16[b + l:b + l + 1, :])
        apply("gm_ws", lambda q: s128[q * CHUNK:(q + 1) * CHUNK, :])
        apply("gm_bs", lambda l: s128[2048 + 8 * l:2048 + 8 * (l + 1), :])
        apply("ada_b", lambda l: s6144[2 * l:2 * l + 1, :] + s6144[2 * l + 1:2 * l + 2, :])
        for l in range(DEPTH):
            scw_out[l] = s1536[SSD_CONV * l:SSD_CONV * (l + 1), :]
            fcw_out[l] = s2816[FF_CONV * l:FF_CONV * (l + 1), :]
        loss_out[...] = s16[2 * len(_P16):2 * len(_P16) + 1, :]

    outs = pl.pallas_call(
        body, name="adamw_small",
        out_shape=out_shapes,
        scratch_shapes=scratch,
        compiler_params=pltpu.CompilerParams(vmem_limit_bytes=VMEM_LIMIT),
    )(*flat_in)
    res = {nme: tuple(outs[4 * i:4 * i + 4]) for i, nme in enumerate(names)}
    return res, outs[-3], outs[-2], outs[-1]


_WEIGHTS = ['ada_w', 'ada_b', 'norm1_g', 'norm2_g', 'w_in', 'ssd_conv_w', 'ssd_conv_b', 'ssd_dt_bias', 'ssd_a_log',
            'ssd_d', 'ssd_norm_g', 'gm_vnorm_g', 'gm_ws', 'gm_bs', 'gm_out_g', 'w_out', 'ff_up', 'ff_conv_w',
            'ff_conv_b', 'ff_down', 'final_g']


_O_XBC, _O_DT, _O_GM = D_SSD, D_SSD + CONV_DIM, D_SSD + CONV_DIM + SSD_HEADS


_TRANSPOSED = ("w_in", "ff_up")


def _full_weight(name, g):
    if name != "w_in":
        return g.reshape(g.shape[0] * g.shape[1], g.shape[2])
    rows = g.shape[1]
    k, r = divmod(_O_GM, rows)
    at = lambda j: COL_Z + j * rows if j <= k else j * rows - _O_GM

    def body(g_hbm, o_ref, g_ref, sems):
        copies = [pltpu.make_async_copy(g_hbm.at[j], g_ref.at[j], sems.at[j]) for j in range(N_DEV)]
        for cp in copies:
            cp.start()
        o_ref[N_IN:, :] = jnp.zeros((N_INP - N_IN, g.shape[2]), g.dtype)
        for j in range(N_DEV):
            copies[j].wait()
            if j == k:
                o_ref[at(k):at(k) + r, :] = g_ref[k, :r, :]
                o_ref[0:rows - r, :] = g_ref[k, r:, :]
            else:
                o_ref[at(j):at(j) + rows, :] = g_ref[j]

    return pl.pallas_call(
        body, name="w_in_rows", in_specs=[ANY],
        out_shape=jax.ShapeDtypeStruct((N_INP, g.shape[2]), g.dtype),
        scratch_shapes=[pltpu.VMEM(g.shape, g.dtype), pltpu.SemaphoreType.DMA((N_DEV,))],
        compiler_params=pltpu.CompilerParams(vmem_limit_bytes=VMEM_LIMIT),
    )(g)


def _by_owner(name, grad):
    if name != "w_in":
        return grad.reshape(N_DEV, grad.shape[0] // N_DEV, grad.shape[1])
    rows = N_IN // N_DEV
    k, r = divmod(_O_GM, rows)
    at = lambda j: COL_Z + j * rows if j <= k else j * rows - _O_GM

    step = 512
    n_pieces = N_INP // step

    def body(g_hbm, o_ref, g_ref, sems):
        copies = [pltpu.make_async_copy(g_hbm.at[pl.ds(i * step, step)], g_ref.at[pl.ds(i * step, step)], sems.at[i])
                  for i in range(n_pieces)]
        for cp in copies:
            cp.start()
        waited = set()

        def need(lo, hi):
            for i in range(lo // step, (hi - 1) // step + 1):
                if i not in waited:
                    copies[i].wait()
                    waited.add(i)

        for j in sorted(range(N_DEV), key=at):
            if j == k:
                need(at(k), at(k) + r)
                need(0, rows - r)
                o_ref[k, :r, :] = g_ref[at(k):at(k) + r, :]
                o_ref[k, r:, :] = g_ref[0:rows - r, :]
            else:
                need(at(j), at(j) + rows)
                o_ref[j] = g_ref[at(j):at(j) + rows, :]
        need(0, N_INP)

    return pl.pallas_call(
        body, name="w_in_grad_blocks", in_specs=[ANY],
        out_shape=jax.ShapeDtypeStruct((N_DEV, rows, grad.shape[1]), grad.dtype),
        scratch_shapes=[pltpu.VMEM(grad.shape, grad.dtype), pltpu.SemaphoreType.DMA((n_pieces,))],
        compiler_params=pltpu.CompilerParams(vmem_limit_bytes=VMEM_LIMIT),
    )(grad)


def kernel(x, c, ada_w, ada_b, norm1_g, norm2_g, w_in, ssd_conv_w, ssd_conv_b, ssd_dt_bias, ssd_a_log, ssd_d, ssd_norm_g, gm_vnorm_g, gm_ws, gm_bs, gm_out_g, w_out, ff_up, ff_conv_w, ff_conv_b, ff_down, final_g, loss_target, m_ada_w, m_ada_b, m_norm1_g, m_norm2_g, m_w_in, m_ssd_conv_w, m_ssd_conv_b, m_ssd_dt_bias, m_ssd_a_log, m_ssd_d, m_ssd_norm_g, m_gm_vnorm_g, m_gm_ws, m_gm_bs, m_gm_out_g, m_w_out, m_ff_up, m_ff_conv_w, m_ff_conv_b, m_ff_down, m_final_g, v_ada_w, v_ada_b, v_norm1_g, v_norm2_g, v_w_in, v_ssd_conv_w, v_ssd_conv_b, v_ssd_dt_bias, v_ssd_a_log, v_ssd_d, v_ssd_norm_g, v_gm_vnorm_g, v_gm_ws, v_gm_bs, v_gm_out_g, v_w_out, v_ff_up, v_ff_conv_w, v_ff_conv_b, v_ff_down, v_final_g):
    given = dict(locals())
    wts = {n: given[n] for n in _WEIGHTS}
    mom = {n: given["m_" + n] for n in _WEIGHTS}
    var = {n: given["v_" + n] for n in _WEIGHTS}
    nseq, seq, d = x.shape
    ix, iy, ic = lax.axis_index("x"), lax.axis_index("y"), lax.axis_index("c")
    me = 4 * ix + 2 * iy + ic
    me_arr = me.astype(jnp.int32).reshape(1)

    for nme, perm in (("ff_up", (0, 2, 1)), ("w_in", (2, 0, 1))):
        wts[nme], mom[nme], var[nme] = (jnp.transpose(a, perm) for a in (wts[nme], mom[nme], var[nme]))

    def shard(l, name):
        return _b(wts[name][:, l, :] if name == "w_in" else wts[name][l])

    g_scw, g_fcw, c_all = _all_gather([ssd_conv_w, ff_conv_w, c], "gather_first")
    scw_f = jnp.transpose(g_scw, (1, 2, 0, 3)).reshape(DEPTH, SSD_CONV, CONV_DIM)
    fcw_f = jnp.transpose(g_fcw, (1, 2, 0, 3)).reshape(DEPTH, FF_CONV, D_FF)
    c_all = c_all.reshape(N_DEV * nseq, d)

    n_ada = ada_w.shape[2]
    ada_b_shard = lax.dynamic_slice_in_dim(ada_b, me * n_ada, n_ada, axis=1).reshape(DEPTH, 1, n_ada)
    mod_part, c_act = _ada_fwd(c_all, ada_w, ada_b_shard)
    first_ssem, first_rsem, first_src, first_land, first_zero = _xc_start(
        False, [mod_part, shard(0, "w_in")], c_act, "ag_first_start", peers=[ALL_PEERS, OTHER_CHIPS])
    _, (mod_g,) = _xc_wait(False, first_ssem[:1], first_rsem[:1], first_src[:1], first_land[:1], c_act,
                           "mod_wait")
    mod_all = jnp.transpose(mod_g, (1, 2, 0, 3)).reshape(DEPTH, N_DEV * nseq, N_MOD * d)
    mod_mine = lax.dynamic_slice_in_dim(mod_all, me * nseq, nseq, axis=1)
    mod_k = jnp.transpose(mod_mine.reshape(DEPTH, nseq, N_MOD, 1, d), (0, 2, 1, 3, 4))
    mods = [[mod_k[l, k] for k in range(N_MOD)] for l in range(DEPTH)]

    later =[(0, "w_out"), (0, "ff_up"), (0, "ff_down"), (1, "w_in"), (1, "w_out"), (1, "ff_up"), (1, "ff_down")]
    ag_groups = {(0, "w_out"): [0], (0, "ff_up"): [1, 2], (1, "w_in"): [3, 4], (1, "ff_up"): [5, 6]}
    big_cache, ag = {}, {}

    def big_w(l, name, after):
        if (l, name) == (0, "w_in") and (l, name) not in big_cache:
            ag["ssem"], ag["rsem"], ag["src"], ag["land"], started = _xc_start(
                False, [shard(l2, n2) for l2, n2 in later], after, "ag_start")
            _, zones = _xc_wait(False, first_ssem[1:], first_rsem[1:], first_src[1:], first_land[1:],
                                jnp.full((8, 128), started, F32), "ag_first_wait", peers=OTHER_CHIPS)
            (zone,) = _sib_wait(*_sib_start(zones, "ag_first_sib_start"), "ag_first_sib_wait")
            big_cache[(l, name)] = _full_weight(name, zone)
        if (l, name) not in big_cache:
            idx = ag_groups[(l, name)]
            pick = lambda seq_: [seq_[i] for i in idx]
            _, lands = _xc_wait(False, pick(ag["ssem"]), pick(ag["rsem"]), pick(ag["src"]), pick(ag["land"]), after,
                                f"ag_wait_{l}_{name}")
            for i, land in zip(idx, lands):
                big_cache[later[i]] = _full_weight(later[i][1], land)
        return big_cache[(l, name)]

    small_w = dict(
        norm1_g=norm1_g + first_zero, norm2_g=norm2_g, ssd_conv_w=scw_f, ssd_conv_b=ssd_conv_b, ssd_dt_bias=ssd_dt_bias,
        ssd_a_log=ssd_a_log, ssd_d=ssd_d, ssd_norm_g=ssd_norm_g, gm_vnorm_g=gm_vnorm_g, gm_ws=gm_ws,
        gm_bst=jnp.transpose(gm_bs, (0, 2, 1)), gm_out_g=gm_out_g, ff_conv_w=fcw_f, ff_conv_b=ff_conv_b)

    outs = {}
    pending, win_parts = {}, {}

    def rs_finish(l, group, after):
        names, ssem, rsem, srcs, lands = pending.pop((l, group))
        srcs, lands = _xc_wait(True, ssem, rsem, srcs, lands, after, f"rs_wait_{l}_{group}")
        for nme, own, land in zip(names, srcs, lands):
            parts = [(own, lambda p: p[0])] + [(land, lambda p, k=k: k) for k in range(N_DEV - 1)]
            if nme == "w_in":
                win_parts[l] = parts
                if len(win_parts) == DEPTH:
                    outs[nme] = _adamw_rows_major([win_parts[k] for k in range(DEPTH)], wts[nme], mom[nme], var[nme],
                                                  me_arr, "adamw_w_in")
                continue
            outs[nme] = _adamw_layer(parts, wts[nme], mom[nme], var[nme], me_arr, l, outs.get(nme), f"adamw_{nme}_{l}")
        return land if names[-1] == "w_in" else outs[names[-1]][0]

    def grad_sink(l, group, grads, after):
        names = list(grads)
        ssem, rsem, srcs, lands, zero = _xc_start(True, [_by_owner(n, grads[n]) for n in names], after, f"rs_start_{l}_{group}")
        pending[(l, group)] = (names, ssem, rsem, srcs, lands)
        return zero.reshape(1, 1)

    early_gather = {}

    def small_sink(l, early, small, dmods, dfg, loss_p):
        if l > 0:
            return None
        layers = [dict(early, norm1_g=jnp.zeros((1, d), F32))] + small[1:]
        rows = lambda name: [layers[k][name] for k in range(DEPTH)]
        packed = [
            jnp.concatenate(sum([rows(n) for n in _P1024], []) + [dfg], axis=0),
            jnp.concatenate(rows("ssd_conv_w") + rows("ssd_conv_b"), axis=0),
            jnp.concatenate(rows("ff_conv_w") + rows("ff_conv_b"), axis=0),
            jnp.concatenate(sum([rows(n) for n in _P16], []) + [loss_p[:, :SSD_HEADS]], axis=0),
            jnp.concatenate([layers[k]["gm_ws"].reshape(GM_HEADS * CHUNK, CHUNK) for k in range(DEPTH)] + rows("gm_bs"), axis=0),
            jnp.concatenate([jnp.zeros((nseq, N_MOD * d), F32)] + dmods[1:], axis=0)]
        ssem, rsem, srcs, lands, zero = _xc_start(False, packed, packed[0], "small_start")
        early_gather.update(ssem=ssem, rsem=rsem, srcs=srcs, lands=lands)
        return zero.reshape(1, 1)

    grad_x, small, dmods = _local_step(
        x.reshape(nseq * seq, d), loss_target.reshape(nseq * seq, d), mods, small_w, final_g.reshape(1, d), nseq=nseq,
        big_w=big_w, grad_sink=grad_sink, small_sink=small_sink)

    done = grad_x
    for l, grp in ((1, "ffn"), (1, "w_out"), (1, "w_in"), (0, "ffn"), (0, "w_out")):
        done = rs_finish(l, grp, done)
    _, gathered = _xc_wait(False, early_gather["ssem"], early_gather["rsem"], early_gather["srcs"],
                           early_gather["lands"], done, "small_wait")
    gathered = list(gathered)
    gathered += _all_gather([small[0]["norm1_g"], dmods[0]], "gather_late", dep=gathered[0])
    gath = dict(zip(["p1024", "p1536", "p2816", "p16", "p128", "p6144", "late1024", "late6144"], gathered))

    dmod_all = jnp.concatenate([gath["late6144"].reshape(1, N_DEV * nseq, N_MOD * d),
                                jnp.transpose(gath["p6144"].reshape(N_DEV, DEPTH, nseq, N_MOD * d)[:, 1:], (1, 0, 2, 3)).reshape(
                                    DEPTH - 1, N_DEV * nseq, N_MOD * d)], axis=0)
    small_names = _P1024 + ["final_g", "ssd_conv_b", "ff_conv_b"] + _P16 + ["gm_ws", "gm_bs", "ada_b"]
    wmv = {}
    for nme in small_names:
        if nme == "final_g":
            wmv[nme] = tuple(a.reshape(1, d) for a in (wts[nme], mom[nme], var[nme]))
        else:
            wmv[nme] = (wts[nme], mom[nme], var[nme])
    small_out, scw_full, fcw_full, loss_sum = _adamw_small(gath, wmv)
    loss = loss_sum[0, 0]
    rs_finish(0, "w_in", scw_full)
    for nme in small_names:
        outs[nme] = small_out[nme]
    outs["final_g"] = tuple(a.reshape(d) for a in outs["final_g"])

    n_scw, n_fcw = ssd_conv_w.shape[2], ff_conv_w.shape[2]
    g_scw_mine = lax.dynamic_slice_in_dim(scw_full, me * n_scw, n_scw, axis=2)
    g_fcw_mine = lax.dynamic_slice_in_dim(fcw_full, me * n_fcw, n_fcw, axis=2)
    outs["ssd_conv_w"] = _adamw_sharded([(g_scw_mine, lambda p: 0)], ssd_conv_w, m_ssd_conv_w, v_ssd_conv_w, me_arr, "adamw_ssd_conv_w")
    outs["ff_conv_w"] = _adamw_sharded([(g_fcw_mine, lambda p: 0)], ff_conv_w, m_ff_conv_w, v_ff_conv_w, me_arr, "adamw_ff_conv_w")

    dmod_cols = _b(lax.dynamic_slice_in_dim(dmod_all, me * n_ada, n_ada, axis=2))
    g_ada = jnp.stack([_matmul(c_act, dmod_cols[l], ta=True, name=f"mm_ada_dw_{l}") for l in range(DEPTH)])
    outs["ada_w"] = _adamw_sharded([(g_ada, lambda p: 0)], ada_w, m_ada_w, v_ada_w, me_arr, "adamw_ada_w")

    for nme, perm in (("ff_up", (0, 2, 1)), ("w_in", (1, 2, 0))):
        outs[nme] = tuple(jnp.transpose(a, perm) for a in outs[nme])
    result = [loss, grad_x.reshape(nseq, seq, d)]
    for k in range(4):
        result += [outs[n][k] for n in _WEIGHTS]
    return tuple(result)
```

```python
import functools
import math

import jax
import jax.numpy as jnp
from jax import lax
from jax.experimental import pallas as pl
from jax.experimental.pallas import tpu as pltpu

F32 = jnp.float32
BF16 = jnp.bfloat16

N_DEV = 8
D_MODEL = 1024
DEPTH = 2
CHUNK = 128
SSD_HEADS = 16
SSD_HEAD_DIM = 64
SSD_GROUPS = 2
HEADS_PER_GROUP = SSD_HEADS // SSD_GROUPS
GROUP_WIDTH = HEADS_PER_GROUP * SSD_HEAD_DIM
D_STATE = 128
D_SSD = 1024
CONV_DIM = 1536
SSD_CONV = 4
GM_HEADS = 8
GM_HEAD_DIM = 128
D_GM = 1024
D_FF = 2816
FF_CONV = 3
N_IN = 4624
N_MOD = 6
EPS = 1e-6

N_INP = 5120
COL_U, COL_V, COL_Z, COL_XBC, COL_DT = 0, 1024, 2048, 3072, 4608

ADAM_LR = 0.001
ADAM_B1 = 0.9
ADAM_B2 = 0.999
ADAM_EPS = 1e-08
ADAM_WD = 0.01
ADAM_STEP = 10

VMEM_LIMIT = 56 * 1024 * 1024
MESH = pl.DeviceIdType.MESH
ANY = pl.BlockSpec(memory_space=pl.ANY)


def _cp(*sem):
    return pltpu.CompilerParams(dimension_semantics=sem, vmem_limit_bytes=VMEM_LIMIT)


def _tile(n, pref):
    if n <= pref or n % 128:
        return n
    best = 128
    for t in range(128, pref + 1, 128):
        if n % t == 0:
            best = t
    return best


def _per_layer(n):
    return pl.BlockSpec((DEPTH, n), lambda *_: (0, 0))


def _row(ref, layer, cols=slice(None)):
    return ref[layer:layer + 1, cols]


def _silu(x):
    return x * jax.nn.sigmoid(x)


def _gelu(x):
    return 0.5 * x * (1.0 + lax.erf(x * (1.0 / math.sqrt(2.0))))


def _softplus(x):
    return jnp.maximum(x, 0.0) + jnp.log1p(jnp.exp(-jnp.abs(x)))


def _b(x):
    return x.astype(BF16)


_NN = (((1,), (0,)), ((), ()))
_NT = (((1,), (1,)), ((), ()))
_TN = (((0,), (0,)), ((), ()))


def _dg(a, b, dn):
    return lax.dot_general(_b(a), _b(b), dn, preferred_element_type=F32)


@jax.custom_vjp
def _bdot(a, b):
    return _dg(a, b, _NN)


def _bdot_fwd(a, b):
    return _dg(a, b, _NN), (a, b)


def _bdot_bwd(res, ct):
    a, b = res
    return _dg(ct, b, _NT), _dg(a, ct, _TN)


_bdot.defvjp(_bdot_fwd, _bdot_bwd)


@jax.custom_vjp
def _bdot_nt(a, b):
    return _dg(a, b, _NT)


def _bdot_nt_fwd(a, b):
    return _dg(a, b, _NT), (a, b)


def _bdot_nt_bwd(res, ct):
    a, b = res
    return _dg(ct, b, _NN), _dg(ct, a, _TN)


_bdot_nt.defvjp(_bdot_nt_fwd, _bdot_nt_bwd)


@jax.custom_vjp
def _bdot_tn(a, b):
    return _dg(a, b, _TN)


def _bdot_tn_fwd(a, b):
    return _dg(a, b, _TN), (a, b)


def _bdot_tn_bwd(res, ct):
    a, b = res
    return _dg(b, ct, _NT), _dg(a, ct, _NN)


_bdot_tn.defvjp(_bdot_tn_fwd, _bdot_tn_bwd)


def _tri(n, lower):
    r = lax.broadcasted_iota(jnp.int32, (n, n), 0)
    c = lax.broadcasted_iota(jnp.int32, (n, n), 1)
    return ((r >= c) if lower else (r <= c)).astype(F32)


def _eye(n):
    r = lax.broadcasted_iota(jnp.int32, (n, n), 0)
    c = lax.broadcasted_iota(jnp.int32, (n, n), 1)
    return (r == c).astype(F32)


def _hdot(a, b, dn):
    return lax.dot_general(a, b, dn, precision=lax.Precision.HIGHEST, preferred_element_type=F32)


@jax.custom_vjp
def _cumsum_rows(x):
    return _hdot(_tri(x.shape[0], True), x, _NN)


def _cumsum_rows_fwd(x):
    return _cumsum_rows(x), None


def _cumsum_rows_bwd(_, ct):
    return (_hdot(_tri(ct.shape[0], False), ct, _NN),)


_cumsum_rows.defvjp(_cumsum_rows_fwd, _cumsum_rows_bwd)


@jax.custom_vjp
def _transpose(x):
    return _hdot(_eye(x.shape[1]), x, _NT)


def _transpose_fwd(x):
    return _transpose(x), None


def _transpose_bwd(_, ct):
    return (_hdot(_eye(ct.shape[1]), ct, _NT),)


_transpose.defvjp(_transpose_fwd, _transpose_bwd)


MXU_WIDTH = 256
MATMUL_TILE_CAP = 2816
MATMUL_VMEM = 44 * 1024 * 1024


def _mxu_tiles(n):
    if n <= MATMUL_TILE_CAP or n % 128:
        return [n]
    for unit in (MXU_WIDTH, 128):
        opts = [t for t in range(unit, MATMUL_TILE_CAP + 1, unit) if n % t == 0]
        if opts:
            return opts
    return [n]


def _matmul(a, b, *, ta=False, tb=False, name, dep=None, out_dtype=F32):
    pieces = list(a) if isinstance(a, (list, tuple)) else [a]
    npc = len(pieces)
    rows, width = pieces[0].shape
    assert all(p.shape == (rows, width) for p in pieces)
    if ta:
        k_dim, m_dim = rows, width * npc
    else:
        m_dim, k_dim = rows, width * npc
    if tb:
        n_dim, kb = b.shape
    else:
        kb, n_dim = b.shape
    assert kb == k_dim, (pieces[0].shape, npc, b.shape, ta, tb)
    m_unit = width if npc > 1 and ta else m_dim
    k_unit = width if npc > 1 and not ta else k_dim
    tm = _tile(m_unit, 1536)
    tn_opts, tk_opts = _mxu_tiles(n_dim), _mxu_tiles(k_unit)
    tn, tk = tn_opts.pop(), tk_opts.pop()
    while 4 * (tm * tk + tk * tn) + 8 * tm * tn > MATMUL_VMEM:
        if tn >= tk and tn_opts:
            tn = tn_opts.pop()
        else:
            tk = tk_opts.pop()
    ni, nj, nk = m_dim // tm, n_dim // tn, k_dim // tk
    per = width // (tm if ta else tk)
    dn = (((0 if ta else 1,), (1 if tb else 0,)), ((), ()))

    a_bytes, b_bytes = m_dim * k_dim, k_dim * n_dim
    m_outer = nk > 1 or a_bytes + b_bytes * ni <= b_bytes + a_bytes * nj
    if m_outer:
        ij = lambda o, n, k: (o, n)
        grid = (ni, nj, nk)
    else:
        ij = lambda o, n, k: (n, o)
        grid = (nj, ni, nk)

    use_acc = nk > 1 and out_dtype != F32

    def body(*refs):
        a_refs, b_ref = refs[:npc], refs[npc]
        o_ref = refs[-2] if use_acc else refs[-1]
        acc_ref = refs[-1]
        k = pl.program_id(2)
        i = pl.program_id(0 if m_outer else 1)
        along = i if ta else k

        def step(a_ref):
            p = lax.dot_general(a_ref[...], b_ref[...], dn, preferred_element_type=F32)
            if nk == 1:
                o_ref[...] = p.astype(out_dtype)
            else:
                @pl.when(k == 0)
                def _():
                    acc_ref[...] = p

                @pl.when((k > 0) & (k < nk - 1 if use_acc else True))
                def _():
                    acc_ref[...] += p

                if use_acc:
                    @pl.when(k == nk - 1)
                    def _():
                        o_ref[...] = (acc_ref[...] + p).astype(out_dtype)

        if npc == 1:
            step(a_refs[0])
        else:
            for pc in range(npc):
                pl.when((along >= pc * per) & (along < (pc + 1) * per))(functools.partial(step, a_refs[pc]))

    def a_map(pc, o, n, k):
        i, _ = ij(o, n, k)
        along = i if ta else k
        if npc > 1:
            along = jnp.clip(along - pc * per, 0, per - 1)
        return (k, along) if ta else (i, along)

    def b_map(o, n, k):
        _, j = ij(o, n, k)
        return (j, k) if tb else (k, j)

    extra = [] if dep is None else [dep]
    return pl.pallas_call(
        body, name=name,
        grid=grid,
        in_specs=[pl.BlockSpec((tk, tm) if ta else (tm, tk), functools.partial(a_map, pc)) for pc in range(npc)]
        + [pl.BlockSpec((tn, tk) if tb else (tk, tn), b_map)] + [ANY] * len(extra),
        out_specs=pl.BlockSpec((tm, tn), lambda o, n, k: ij(o, n, k)),
        out_shape=jax.ShapeDtypeStruct((m_dim, n_dim), out_dtype),
        scratch_shapes=[pltpu.VMEM((tm, tn), F32)] if use_acc else [],
        compiler_params=_cp("parallel", "parallel", "arbitrary"),
    )(*pieces, b, *extra)


def _ada_fwd(c_all, ada_w, ada_b_shard):
    depth, d, n = ada_w.shape
    nb = c_all.shape[0]

    def body(c_ref, w_ref, b_ref, o_ref, ca_ref):
        ca = _silu(c_ref[...])
        ca_ref[...] = _b(ca)
        o_ref[0] = _dg(ca, w_ref[0], _NN) + b_ref[0]

    return pl.pallas_call(
        body, name="ada_fwd",
        grid=(depth,),
        in_specs=[pl.BlockSpec((nb, d), lambda l: (0, 0)),
                  pl.BlockSpec((1, d, n), lambda l: (l, 0, 0)),
                  pl.BlockSpec((1, 1, n), lambda l: (l, 0, 0))],
        out_specs=[pl.BlockSpec((1, nb, n), lambda l: (l, 0, 0)),
                   pl.BlockSpec((nb, d), lambda l: (0, 0))],
        out_shape=[jax.ShapeDtypeStruct((depth, nb, n), F32), jax.ShapeDtypeStruct((nb, d), BF16)],
        compiler_params=_cp("arbitrary"),
    )(c_all, ada_w, ada_b_shard)


def _fold(acc):
    return jnp.sum(acc, axis=0, keepdims=True)


def _rinv(x):
    return lax.rsqrt(jnp.sum(x * x, axis=-1, keepdims=True) * (1.0 / D_MODEL) + EPS)


def _rms_bwd(a, xhat, rinv):
    return rinv * (a - xhat * (jnp.sum(a * xhat, axis=-1, keepdims=True) * (1.0 / D_MODEL)))


def _row_tile(seq):
    return min(seq, 256)


def _normmod_fwd(x, g, sc, sh, *, nseq, name, layer):
    t, d = x.shape
    seq = t // nseq
    tr = _row_tile(seq)
    nt = seq // tr
    row = pl.BlockSpec((tr, d), lambda s, i: (s * nt + i, 0))
    per_seq = pl.BlockSpec((1, 1, d), lambda s, i: (s, 0, 0))

    def body(x_ref, g_ref, sc_ref, sh_ref, h_ref):
        x_v = x_ref[...]
        h_ref[...] = _b(x_v * _rinv(x_v) * (_row(g_ref, layer) * (1.0 + sc_ref[0])) + sh_ref[0])

    return pl.pallas_call(
        body, name=name, grid=(nseq, nt),
        in_specs=[row, _per_layer(d), per_seq, per_seq],
        out_specs=row,
        out_shape=jax.ShapeDtypeStruct((t, d), BF16),
        compiler_params=_cp("parallel", "parallel"),
    )(x, g, sc, sh)


NORM_TM = 512


def _matmul_normbwd(a, b, dxo, x, delta, gate, g, sc, *, nseq, name, layer, dep=None):
    pieces = list(a) if isinstance(a, (list, tuple)) else [a]
    npc = len(pieces)
    t, width = pieces[0].shape
    k_dim, d = width * npc, b.shape[1]
    assert b.shape[0] == k_dim and all(p.shape == (t, width) for p in pieces)
    seq = t // nseq
    tm = min(NORM_TM, seq)
    per_seq_tiles = seq // tm
    tk = _mxu_tiles(width if npc > 1 else k_dim).pop()
    nk, per = k_dim // tk, width // tk
    has_delta = delta is not None
    extra = [] if dep is None else [dep]

    def body(*refs):
        a_refs, b_ref = refs[:npc], refs[npc]
        dxo_ref, x_ref = refs[npc + 1], refs[npc + 2]
        pos = npc + 3
        if has_delta:
            delta_ref, gate_ref = refs[pos], refs[pos + 1]
            pos += 2
        g_ref, sc_ref = refs[pos], refs[pos + 1]
        pos += 2 + len(extra)
        if has_delta:
            dx_ref, dd_ref, dgate_ref, dg_ref, dsc_ref, dsh_ref = refs[pos:pos + 6]
        else:
            dx_ref, dg_ref, dsc_ref, dsh_ref = refs[pos:pos + 4]
        acc_ref = refs[-1]
        i, k = pl.program_id(0), pl.program_id(1)

        def norm_bwd(dh_v):
            g_v, one_sc = _row(g_ref, layer), 1.0 + sc_ref[0]
            x_v = x_ref[...]
            rinv = _rinv(x_v)
            xhat = x_v * rinv
            dx = dxo_ref[...] + _rms_bwd(dh_v * (g_v * one_sc), xhat, rinv)
            dx_ref[...] = dx

            @pl.when(i == 0)
            def _():
                dg_ref[...] = jnp.zeros_like(dg_ref)

            @pl.when(i % per_seq_tiles == 0)
            def _():
                dsc_ref[...] = jnp.zeros_like(dsc_ref)
                dsh_ref[...] = jnp.zeros_like(dsh_ref)
                if has_delta:
                    dgate_ref[...] = jnp.zeros_like(dgate_ref)

            t_sum = _fold(dh_v * xhat)
            dg_ref[...] += t_sum * one_sc
            dsc_ref[0] += t_sum * g_v
            dsh_ref[0] += _fold(dh_v)
            if has_delta:
                dd_ref[...] = _b(dx * gate_ref[0])
                dgate_ref[0] += _fold(dx * delta_ref[...])

        def step(a_ref):
            p = lax.dot_general(a_ref[...], b_ref[...], _NN, preferred_element_type=F32)
            if nk == 1:
                norm_bwd(p)
            else:
                @pl.when(k == 0)
                def _():
                    acc_ref[...] = p

                @pl.when((k > 0) & (k < nk - 1))
                def _():
                    acc_ref[...] += p

                @pl.when(k == nk - 1)
                def _():
                    norm_bwd(acc_ref[...] + p)

        if npc == 1:
            step(a_refs[0])
        else:
            for pc in range(npc):
                pl.when((k >= pc * per) & (k < (pc + 1) * per))(functools.partial(step, a_refs[pc]))

    def a_map(pc, i, k):
        return (i, jnp.clip(k - pc * per, 0, per - 1) if npc > 1 else k)

    row = pl.BlockSpec((tm, d), lambda i, k: (i, 0))
    per_seq = pl.BlockSpec((1, 1, d), lambda i, k: (i // per_seq_tiles, 0, 0))
    vec = pl.BlockSpec((1, d), lambda i, k: (0, 0))
    shp = lambda *s, dt=F32: jax.ShapeDtypeStruct(s, dt)
    in_specs = [pl.BlockSpec((tm, tk), functools.partial(a_map, pc)) for pc in range(npc)]
    in_specs += [pl.BlockSpec((tk, d), lambda i, k: (k, 0)), row, row]
    operands = [*pieces, b, dxo, x]
    if has_delta:
        in_specs += [row, per_seq]
        operands += [delta, gate]
    in_specs += [_per_layer(d), per_seq] + [ANY] * len(extra)
    operands += [g, sc, *extra]
    if has_delta:
        out_specs = [row, row, per_seq, vec, per_seq, per_seq]
        out_shape = [shp(t, d), shp(t, d, dt=BF16), shp(nseq, 1, d), shp(1, d), shp(nseq, 1, d), shp(nseq, 1, d)]
    else:
        out_specs = [row, vec, per_seq, per_seq]
        out_shape = [shp(t, d), shp(1, d), shp(nseq, 1, d), shp(nseq, 1, d)]
    outs = pl.pallas_call(
        body, name=name, grid=(t // tm, nk),
        in_specs=in_specs, out_specs=out_specs, out_shape=out_shape,
        scratch_shapes=[pltpu.VMEM((tm, d), F32)],
        compiler_params=_cp("arbitrary", "arbitrary"),
    )(*operands)
    if has_delta:
        return tuple(outs)
    dx, dg, dsc, dsh = outs
    return dx, None, None, dg, dsc, dsh


def _matmul_normfwd(a, b, xin, gate, g, sc, sh, *, nseq, name, layer):
    t, k_dim = a.shape
    d = b.shape[1]
    assert b.shape[0] == k_dim and k_dim <= MATMUL_TILE_CAP
    seq = t // nseq
    tm = min(NORM_TM, seq)
    per_seq_tiles = seq // tm

    def body(a_ref, b_ref, xin_ref, gate_ref, g_ref, sc_ref, sh_ref, dl_ref, x_ref, h_ref):
        dl = lax.dot_general(a_ref[...], b_ref[...], _NN, preferred_element_type=F32)
        dl_ref[...] = dl
        x = xin_ref[...] + gate_ref[0] * dl
        x_ref[...] = x
        h_ref[...] = _b(x * _rinv(x) * (_row(g_ref, layer) * (1.0 + sc_ref[0])) + sh_ref[0])

    row = pl.BlockSpec((tm, d), lambda i: (i, 0))
    per_seq = pl.BlockSpec((1, 1, d), lambda i: (i // per_seq_tiles, 0, 0))
    return pl.pallas_call(
        body, name=name, grid=(t // tm,),
        in_specs=[pl.BlockSpec((tm, k_dim), lambda i: (i, 0)), pl.BlockSpec((k_dim, d), lambda i: (0, 0)),
                  row, per_seq, _per_layer(d), per_seq, per_seq],
        out_specs=[row, row, row],
        out_shape=[jax.ShapeDtypeStruct((t, d), F32), jax.ShapeDtypeStruct((t, d), F32), jax.ShapeDtypeStruct((t, d), BF16)],
        compiler_params=_cp("parallel"),
    )(a, b, xin, gate, g, sc, sh)


def _matmul_loss(a, b, xin, gate, fg, target, *, nseq, name):
    t, k_dim = a.shape
    d = b.shape[1]
    assert b.shape[0] == k_dim and k_dim <= MATMUL_TILE_CAP
    seq = t // nseq
    tm = min(NORM_TM, seq)
    per_seq_tiles = seq // tm

    def body(a_ref, b_ref, xin_ref, gate_ref, fg_ref, tgt_ref, dl_ref, loss_ref, dx_ref, dd_ref, dgate_ref, dfg_ref):
        i = pl.program_id(0)
        fg_v, gate_v = fg_ref[...], gate_ref[0]
        dl = lax.dot_general(a_ref[...], b_ref[...], _NN, preferred_element_type=F32)
        dl_ref[...] = dl
        x = xin_ref[...] + gate_v * dl
        rinv = _rinv(x)
        xhat = x * rinv
        err = xhat * fg_v - tgt_ref[...]
        dx = _rms_bwd(err * fg_v * (1.0 / d), xhat, rinv)
        dx_ref[...] = dx
        dd_ref[...] = _b(dx * gate_v)

        @pl.when(i == 0)
        def _():
            loss_ref[...] = jnp.zeros_like(loss_ref)
            dfg_ref[...] = jnp.zeros_like(dfg_ref)

        @pl.when(i % per_seq_tiles == 0)
        def _():
            dgate_ref[...] = jnp.zeros_like(dgate_ref)

        loss_ref[...] += jnp.sum(err * err) * (0.5 / d)
        dfg_ref[...] += _fold(err * xhat) * (1.0 / d)
        dgate_ref[0] += _fold(dx * dl)

    row = pl.BlockSpec((tm, d), lambda i: (i, 0))
    per_seq = pl.BlockSpec((1, 1, d), lambda i: (i // per_seq_tiles, 0, 0))
    vec = pl.BlockSpec((1, d), lambda i: (0, 0))
    return pl.pallas_call(
        body, name=name, grid=(t // tm,),
        in_specs=[pl.BlockSpec((tm, k_dim), lambda i: (i, 0)), pl.BlockSpec((k_dim, d), lambda i: (0, 0)),
                  row, per_seq, vec, row],
        out_specs=[row, pl.BlockSpec((1, 128), lambda i: (0, 0)), row, row, per_seq, vec],
        out_shape=[jax.ShapeDtypeStruct((t, d), F32), jax.ShapeDtypeStruct((1, 128), F32), jax.ShapeDtypeStruct((t, d), F32),
                   jax.ShapeDtypeStruct((t, d), BF16), jax.ShapeDtypeStruct((nseq, 1, d), F32),
                   jax.ShapeDtypeStruct((1, d), F32)],
        compiler_params=_cp("arbitrary"),
    )(a, b, xin, gate, fg, target)


CONV_TC = 256
CONV_LANES = 128
CONV_ROWS = 64
CONV_HALO = 8


def _conv_slabs(seq, fn):
    def step(i, carry):
        r0 = pl.multiple_of(i * CONV_ROWS, CONV_ROWS)
        for h in range(CONV_TC // CONV_LANES):
            fn(r0, slice(h * CONV_LANES, (h + 1) * CONV_LANES))
        return carry

    lax.fori_loop(0, seq // CONV_ROWS, step, 0)


def _slab(ref, r0, cols, seq):
    after = ref[pl.ds(pl.multiple_of(jnp.minimum(r0 + CONV_ROWS, seq - CONV_HALO), CONV_HALO), CONV_HALO), cols]
    return jnp.concatenate([ref[pl.ds(r0, CONV_ROWS), cols], jnp.where(r0 + CONV_ROWS < seq, after, 0.0)], axis=0)


def _conv_block(x, w_ref, b):
    kw = w_ref.shape[0]
    rows = lax.broadcasted_iota(jnp.int32, x.shape, 0)
    y = b + w_ref[kw - 1:kw, :] * x
    for j in range(1, kw):
        y = y + w_ref[kw - 1 - j:kw - j, :] * jnp.where(rows >= j, pltpu.roll(x, j, 0), 0.0)
    return y


def _conv_block_bwd(dy, x, w_ref, dw_ref, db_ref):
    kw = w_ref.shape[0]
    n = x.shape[0]
    rows = lax.broadcasted_iota(jnp.int32, x.shape, 0)
    dx = w_ref[kw - 1:kw, :] * dy
    dw_ref[kw - 1:kw, :] += jnp.sum(dy * x, axis=0, keepdims=True)
    for j in range(1, kw):
        dy_j = jnp.where(rows < n - j, pltpu.roll(dy, n - j, 0), 0.0)
        dx = dx + w_ref[kw - 1 - j:kw - j, :] * dy_j
        dw_ref[kw - 1 - j:kw - j, :] += jnp.sum(dy_j * x, axis=0, keepdims=True)
    db_ref[...] += jnp.sum(dy, axis=0, keepdims=True)
    return dx


def _conv_bwd(dy_ext, x, w_ref, dw_ref, db_ref, cols):
    kw = w_ref.shape[0]
    n = dy_ext.shape[0]
    dy = dy_ext[:CONV_ROWS]
    dx = w_ref[kw - 1:kw, cols] * dy
    dw_ref[kw - 1:kw, cols] += jnp.sum(dy * x, axis=0, keepdims=True)
    for j in range(1, kw):
        dy_j = pltpu.roll(dy_ext, n - j, 0)[:CONV_ROWS]
        dx = dx + w_ref[kw - 1 - j:kw - j, cols] * dy_j
        dw_ref[kw - 1 - j:kw - j, cols] += jnp.sum(dy_j * x, axis=0, keepdims=True)
    db_ref[:, cols] += jnp.sum(dy, axis=0, keepdims=True)
    return dx


def _dsilu(pre):
    sg = jax.nn.sigmoid(pre)
    return pre * sg, sg * (1.0 + pre * (1.0 - sg))


def _conv_specs(kw, layer):
    return [pl.BlockSpec((None, kw, CONV_TC), lambda j, s: (layer, 0, j)),
            pl.BlockSpec((DEPTH, CONV_TC), lambda j, s: (0, j))]


def _ssd_conv_fwd(proj, w, b, *, nseq, layer):
    t = proj.shape[0]
    seq = t // nseq
    nb = CONV_DIM // CONV_TC
    off = COL_XBC // CONV_TC

    def body(x_ref, w_ref, b_ref, o_ref, pre_ref):
        pre = _conv_block(x_ref[...], w_ref, _row(b_ref, layer))
        pre_ref[...] = pre
        o_ref[...] = _silu(pre)

    col = pl.BlockSpec((seq, CONV_TC), lambda j, s: (s, j))
    return pl.pallas_call(
        body, name="ssd_conv_fwd", grid=(nb, nseq),
        in_specs=[pl.BlockSpec((seq, CONV_TC), lambda j, s: (s, off + j)), *_conv_specs(SSD_CONV, layer)],
        out_specs=[col, col],
        out_shape=[jax.ShapeDtypeStruct((t, CONV_DIM), F32)] * 2,
        compiler_params=_cp("parallel", "parallel"),
    )(proj, w, b)


def _ssd_conv_bwd(dact, pre, proj, w, dproj, *, nseq, layer):
    t = proj.shape[0]
    seq = t // nseq
    nb = CONV_DIM // CONV_TC
    off = COL_XBC // CONV_TC

    def body(da_ref, pre_ref, x_ref, w_ref, dproj_ref, dx_ref, dw_ref, db_ref):
        del dproj_ref

        @pl.when(pl.program_id(1) == 0)
        def _():
            dw_ref[...] = jnp.zeros_like(dw_ref)
            db_ref[...] = jnp.zeros_like(db_ref)

        def slab(r0, cols):
            _, dsilu = _dsilu(_slab(pre_ref, r0, cols, seq))
            dpre_ext = _slab(da_ref, r0, cols, seq) * dsilu
            x = x_ref[pl.ds(r0, CONV_ROWS), cols]
            dx_ref[pl.ds(r0, CONV_ROWS), cols] = _b(_conv_bwd(dpre_ext, x, w_ref, dw_ref, db_ref, cols))

        _conv_slabs(seq, slab)

    return pl.pallas_call(
        body, name="ssd_conv_bwd", grid=(nb, nseq),
        in_specs=[pl.BlockSpec((seq, CONV_TC), lambda j, s: (s, j)),
                  pl.BlockSpec((seq, CONV_TC), lambda j, s: (s, j)),
                  pl.BlockSpec((seq, CONV_TC), lambda j, s: (s, off + j)),
                  _conv_specs(SSD_CONV, layer)[0],
                  ANY],
        out_specs=[pl.BlockSpec((seq, CONV_TC), lambda j, s: (s, off + j)),
                   pl.BlockSpec((SSD_CONV, CONV_TC), lambda j, s: (0, j)),
                   pl.BlockSpec((1, CONV_TC), lambda j, s: (0, j))],
        out_shape=[jax.ShapeDtypeStruct(dproj.shape, dproj.dtype), jax.ShapeDtypeStruct((SSD_CONV, CONV_DIM), F32),
                   jax.ShapeDtypeStruct((1, CONV_DIM), F32)],
        input_output_aliases={4: 0},
        compiler_params=_cp("parallel", "arbitrary"),
    )(dact, pre, proj, w, dproj)


def _ffn_act_fwd(up, w, b, *, nseq, layer):
    t = up.shape[0]
    seq = t // nseq
    nb = D_FF // CONV_TC

    def body(g_ref, v_ref, w_ref, b_ref, o_ref):
        pre = _conv_block(g_ref[...].astype(F32), w_ref, _row(b_ref, layer))
        o_ref[...] = _b(_silu(pre) * v_ref[...].astype(F32))

    col = pl.BlockSpec((seq, CONV_TC), lambda j, s: (s, j))
    return pl.pallas_call(
        body, name="ffn_act_fwd", grid=(nb, nseq),
        in_specs=[col,
                  pl.BlockSpec((seq, CONV_TC), lambda j, s: (s, nb + j)),
                  *_conv_specs(FF_CONV, layer)],
        out_specs=col,
        out_shape=jax.ShapeDtypeStruct((t, D_FF), BF16),
        compiler_params=_cp("parallel", "parallel"),
    )(up, up, w, b)


def _ffn_act_bwd(dact, up, w, b, *, nseq, layer):
    t = up.shape[0]
    seq = t // nseq
    nb = D_FF // CONV_TC

    def body(da_ref, g_ref, v_ref, w_ref, b_ref, dg_ref, dv_ref, dw_ref, db_ref):
        @pl.when(pl.program_id(1) == 0)
        def _():
            dw_ref[...] = jnp.zeros_like(dw_ref)
            db_ref[...] = jnp.zeros_like(db_ref)

        gate = g_ref[...].astype(F32)
        silu, dsilu = _dsilu(_conv_block(gate, w_ref, _row(b_ref, layer)))
        da = da_ref[...].astype(F32)
        dv_ref[...] = _b(da * silu)
        dg_ref[...] = _b(_conv_block_bwd(da * v_ref[...].astype(F32) * dsilu, gate, w_ref, dw_ref, db_ref))

    col = pl.BlockSpec((seq, CONV_TC), lambda j, s: (s, j))
    return pl.pallas_call(
        body, name="ffn_act_bwd", grid=(nb, nseq),
        in_specs=[col, col,
                  pl.BlockSpec((seq, CONV_TC), lambda j, s: (s, nb + j)),
                  *_conv_specs(FF_CONV, layer)],
        out_specs=[col, col,
                   pl.BlockSpec((FF_CONV, CONV_TC), lambda j, s: (0, j)),
                   pl.BlockSpec((1, CONV_TC), lambda j, s: (0, j))],
        out_shape=[jax.ShapeDtypeStruct((t, D_FF), BF16), jax.ShapeDtypeStruct((t, D_FF), BF16),
                   jax.ShapeDtypeStruct((FF_CONV, D_FF), F32), jax.ShapeDtypeStruct((1, D_FF), F32)],
        compiler_params=_cp("parallel", "arbitrary"),
    )(dact, up, up, w, b)


SSD_PAIRS = SSD_HEADS // 2
PAIR_W = 2 * SSD_HEAD_DIM
PAIRS_PER_GROUP = SSD_PAIRS // SSD_GROUPS


def _ssd_chunk(xs, bg, cg, dtr, z, hp, dtb, alog, dskip, ng):
    n = dtr.shape[0]
    dt = _softplus(dtr + dtb)
    cs = _cumsum_rows(dt * (-jnp.exp(alog)))
    cs_t = _transpose(cs)
    lane = lax.broadcasted_iota(jnp.int32, (1, SSD_HEADS), 1)
    sub = lax.broadcasted_iota(jnp.int32, (SSD_HEADS, 1), 0)
    row = lax.broadcasted_iota(jnp.int32, (n, 1), 0)
    causal = lax.broadcasted_iota(jnp.int32, (n, n), 0) >= lax.broadcasted_iota(jnp.int32, (n, n), 1)
    future = jnp.where(causal, 0.0, -1e30)
    first = lax.broadcasted_iota(jnp.int32, (1, PAIR_W), 1) < SSD_HEAD_DIM
    first_rows = lax.broadcasted_iota(jnp.int32, (PAIR_W, 1), 0) < SSD_HEAD_DIM
    first_f = first.astype(F32)
    cb = [_bdot_nt(cg[g], bg[g]) for g in range(SSD_GROUPS)]
    ys, hn = [], []
    for p in range(SSD_PAIRS):
        g = p // PAIRS_PER_GROUP
        col, decay, last = [], [], []
        for h in (2 * p, 2 * p + 1):
            oh = (lane == h).astype(F32)
            cs_h = jnp.sum(cs * oh, axis=1, keepdims=True)
            cs_row = jnp.sum(cs_t * (sub == h).astype(F32), axis=0, keepdims=True)
            col.append((jnp.sum(dt * oh, axis=1, keepdims=True), cs_h, jnp.sum(dskip * oh, axis=1, keepdims=True)))
            last.append(jnp.sum(jnp.where(row == n - 1, cs_h, 0.0), axis=0, keepdims=True))
            decay.append(jnp.exp(cs_h - cs_row + future))
        pair = lambda a, b: jnp.where(first, a, b)
        dt_p = pair(col[0][0], col[1][0])
        cs_p = pair(col[0][1], col[1][1])
        last_p = pair(last[0], last[1])
        xc = xs[p] * dt_p
        y = _bdot(cb[g] * decay[0], xc * first_f) + _bdot(cb[g] * decay[1], xc * (1.0 - first_f))
        y = y + _bdot_nt(cg[g], hp[p]) * jnp.exp(cs_p)
        y = y + pair(col[0][2], col[1][2]) * xs[p]
        keep = jnp.where(first_rows, jnp.exp(last[0]), jnp.exp(last[1]))
        hn.append(keep * hp[p] + _bdot_tn(xc * jnp.exp(last_p - cs_p), bg[g]))
        ys.append(y * _silu(z[p]))
    outs = []
    for g in range(SSD_GROUPS):
        ps = range(g * PAIRS_PER_GROUP, (g + 1) * PAIRS_PER_GROUP)
        ms = sum(jnp.sum(ys[p] * ys[p], axis=1, keepdims=True) for p in ps) * (1.0 / GROUP_WIDTH)
        r = lax.rsqrt(ms + EPS)
        outs += [ys[p] * r * ng[p] for p in ps]
    return outs, hn


def _hslices(ref, width, count, base=0, rows=slice(None)):
    return [ref[rows, base + k * width: base + (k + 1) * width] for k in range(count)]


def _ssd_load(xbc_ref, z_ref, dt_ref, ng_ref, layer):
    xs = _hslices(xbc_ref, PAIR_W, SSD_PAIRS)
    bg = _hslices(xbc_ref, D_STATE, SSD_GROUPS, D_SSD)
    cg = _hslices(xbc_ref, D_STATE, SSD_GROUPS, D_SSD + SSD_GROUPS * D_STATE)
    z = _hslices(z_ref, PAIR_W, SSD_PAIRS)
    ng = _hslices(ng_ref, PAIR_W, SSD_PAIRS, rows=slice(layer, layer + 1))
    return xs, bg, cg, dt_ref[:, 0:SSD_HEADS], z, ng


def _ssd_specs(nch):
    rowi = lambda s, c: s * nch + c
    return [pl.BlockSpec((CHUNK, CONV_DIM), lambda s, c: (rowi(s, c), 0)),
            pl.BlockSpec((CHUNK, D_SSD), lambda s, c: (rowi(s, c), COL_Z // D_SSD)),
            pl.BlockSpec((CHUNK, 128), lambda s, c: (rowi(s, c), COL_DT // 128)),
            _per_layer(SSD_HEADS), _per_layer(SSD_HEADS), _per_layer(SSD_HEADS), _per_layer(D_SSD)]


def _ssd_fwd(xbc, proj, dtb, alog, dskip, ng, *, nseq, layer):
    t = proj.shape[0]
    nch = t // nseq // CHUNK
    hd = PAIR_W

    def body(xbc_ref, z_ref, dt_ref, dtb_ref, alog_ref, dsk_ref, ng_ref, y_ref, hp_ref, h_ref):
        @pl.when(pl.program_id(1) == 0)
        def _():
            h_ref[...] = jnp.zeros_like(h_ref)

        xs, bg, cg, dtr, z, ngs = _ssd_load(xbc_ref, z_ref, dt_ref, ng_ref, layer)
        hp_ref[0] = h_ref[...]
        hp = [h_ref[h * hd:(h + 1) * hd, :] for h in range(SSD_PAIRS)]
        outs, hn = _ssd_chunk(xs, bg, cg, dtr, z, hp, _row(dtb_ref, layer), _row(alog_ref, layer), _row(dsk_ref, layer), ngs)
        for h in range(SSD_PAIRS):
            y_ref[:, h * hd:(h + 1) * hd] = _b(outs[h])
            h_ref[h * hd:(h + 1) * hd, :] = hn[h]

    return pl.pallas_call(
        body, name="ssd_fwd", grid=(nseq, nch),
        in_specs=_ssd_specs(nch),
        out_specs=[pl.BlockSpec((CHUNK, D_SSD), lambda s, c: (s * nch + c, 0)),
                   pl.BlockSpec((1, D_SSD, D_STATE), lambda s, c: (s * nch + c, 0, 0))],
        out_shape=[jax.ShapeDtypeStruct((t, D_SSD + D_GM), BF16),
                   jax.ShapeDtypeStruct((t // CHUNK, D_SSD, D_STATE), F32)],
        scratch_shapes=[pltpu.VMEM((D_SSD, D_STATE), F32)],
        compiler_params=_cp("arbitrary", "arbitrary"),
    )(xbc, proj, proj, dtb, alog, dskip, ng)


def _ssd_bwd(dy, xbc, proj, hprev, dtb, alog, dskip, ng, *, nseq, layer):
    t = proj.shape[0]
    nch = t // nseq // CHUNK
    hd = PAIR_W
    rev = lambda s, c: s * nch + (nch - 1 - c)

    def body(dy_ref, xbc_ref, z_ref, dt_ref, hp_ref, dtb_ref, alog_ref, dsk_ref, ng_ref,
             dxbc_ref, dproj_ref, ddtb_ref, dalog_ref, ddsk_ref, dng_ref, dh_ref):
        first = (pl.program_id(0) == 0) & (pl.program_id(1) == 0)

        @pl.when(pl.program_id(1) == 0)
        def _():
            dh_ref[...] = jnp.zeros_like(dh_ref)

        @pl.when(first)
        def _():
            ddtb_ref[...] = jnp.zeros_like(ddtb_ref)
            dalog_ref[...] = jnp.zeros_like(dalog_ref)
            ddsk_ref[...] = jnp.zeros_like(ddsk_ref)
            dng_ref[...] = jnp.zeros_like(dng_ref)

        xs, bg, cg, dtr, z, ngs = _ssd_load(xbc_ref, z_ref, dt_ref, ng_ref, layer)
        hp = [hp_ref[0, h * hd:(h + 1) * hd, :] for h in range(SSD_PAIRS)]
        _, vjp = jax.vjp(_ssd_chunk, xs, bg, cg, dtr, z, hp, _row(dtb_ref, layer), _row(alog_ref, layer), _row(dsk_ref, layer), ngs)
        douts = [dy_ref[:, h * hd:(h + 1) * hd] for h in range(SSD_PAIRS)]
        dhn = [dh_ref[h * hd:(h + 1) * hd, :] for h in range(SSD_PAIRS)]
        dxs, dbg, dcg, ddtr, dz, dhp, ddtb, dalog, ddsk, dngs = vjp((douts, dhn))
        dproj_ref[:, :COL_Z] = jnp.zeros((CHUNK, COL_Z), BF16)
        dproj_ref[:, COL_XBC:] = jnp.zeros((CHUNK, N_INP - COL_XBC), BF16)
        for h in range(SSD_PAIRS):
            dxbc_ref[:, h * hd:(h + 1) * hd] = dxs[h]
            dproj_ref[:, COL_Z + h * hd: COL_Z + (h + 1) * hd] = _b(dz[h])
            dh_ref[h * hd:(h + 1) * hd, :] = dhp[h]
            dng_ref[:, h * hd:(h + 1) * hd] += dngs[h]
        for g in range(SSD_GROUPS):
            dxbc_ref[:, D_SSD + g * D_STATE: D_SSD + (g + 1) * D_STATE] = dbg[g]
            dxbc_ref[:, D_SSD + (SSD_GROUPS + g) * D_STATE: D_SSD + (SSD_GROUPS + g + 1) * D_STATE] = dcg[g]
        dproj_ref[:, COL_DT:COL_DT + SSD_HEADS] = _b(ddtr)
        ddtb_ref[...] += ddtb
        dalog_ref[...] += dalog
        ddsk_ref[...] += ddsk

    small = pl.BlockSpec((1, SSD_HEADS), lambda s, c: (0, 0))
    return pl.pallas_call(
        body, name="ssd_bwd", grid=(nseq, nch),
        in_specs=[pl.BlockSpec((CHUNK, D_SSD), lambda s, c: (rev(s, c), 0)),
                  pl.BlockSpec((CHUNK, CONV_DIM), lambda s, c: (rev(s, c), 0)),
                  pl.BlockSpec((CHUNK, D_SSD), lambda s, c: (rev(s, c), COL_Z // D_SSD)),
                  pl.BlockSpec((CHUNK, 128), lambda s, c: (rev(s, c), COL_DT // 128)),
                  pl.BlockSpec((1, D_SSD, D_STATE), lambda s, c: (rev(s, c), 0, 0)),
                  _per_layer(SSD_HEADS), _per_layer(SSD_HEADS), _per_layer(SSD_HEADS), _per_layer(D_SSD)],
        out_specs=[pl.BlockSpec((CHUNK, CONV_DIM), lambda s, c: (rev(s, c), 0)),
                   pl.BlockSpec((CHUNK, N_INP), lambda s, c: (rev(s, c), 0)),
                   small, small, small,
                   pl.BlockSpec((1, D_SSD), lambda s, c: (0, 0))],
        out_shape=[jax.ShapeDtypeStruct((t, CONV_DIM), F32), jax.ShapeDtypeStruct((t, N_INP), BF16),
                   jax.ShapeDtypeStruct((1, SSD_HEADS), F32), jax.ShapeDtypeStruct((1, SSD_HEADS), F32),
                   jax.ShapeDtypeStruct((1, SSD_HEADS), F32), jax.ShapeDtypeStruct((1, D_SSD), F32)],
        scratch_shapes=[pltpu.VMEM((D_SSD, D_STATE), F32)],
        compiler_params=_cp("arbitrary", "arbitrary"),
    )(dy, xbc, proj, proj, hprev, dtb, alog, dskip, ng)


def _gmlp_chunk(gu, gv, ws, bs_cols, vg, og):
    n = gu[0].shape[0]
    mask = _tri(n, True)
    au = [_gelu(t) for t in gu]
    av = [_gelu(t) for t in gv]
    r = lax.rsqrt(sum(jnp.sum(t * t, axis=1, keepdims=True) for t in av) * (1.0 / D_GM) + EPS)
    p = []
    for h in range(GM_HEADS):
        sv = _bdot(ws[h] * mask, av[h] * r * vg[h]) + bs_cols[h]
        p.append(au[h] * sv)
    r2 = lax.rsqrt(sum(jnp.sum(t * t, axis=1, keepdims=True) for t in p) * (1.0 / D_GM) + EPS)
    return [p[h] * r2 * og[h] for h in range(GM_HEADS)]


def _gmlp_load(u_ref, v_ref, ws_ref, bst_ref, vg_ref, og_ref, layer):
    gu = _hslices(u_ref, GM_HEAD_DIM, GM_HEADS)
    gv = _hslices(v_ref, GM_HEAD_DIM, GM_HEADS)
    ws = [ws_ref[h] for h in range(GM_HEADS)]
    bs_cols = [bst_ref[:, h:h + 1] for h in range(GM_HEADS)]
    mine = slice(layer, layer + 1)
    return (gu, gv, ws, bs_cols, _hslices(vg_ref, GM_HEAD_DIM, GM_HEADS, rows=mine),
            _hslices(og_ref, GM_HEAD_DIM, GM_HEADS, rows=mine))


def _gmlp_specs(layer):
    return [pl.BlockSpec((CHUNK, D_GM), lambda i: (i, COL_U // D_GM)),
            pl.BlockSpec((CHUNK, D_GM), lambda i: (i, COL_V // D_GM)),
            pl.BlockSpec((None, GM_HEADS, CHUNK, CHUNK), lambda i: (layer, 0, 0, 0)),
            pl.BlockSpec((None, CHUNK, GM_HEADS), lambda i: (layer, 0, 0)),
            _per_layer(D_GM), _per_layer(D_GM)]


def _gmlp_fwd(proj, ycat, ws, bst, vg, og, *, layer):
    t = proj.shape[0]

    def body(u_ref, v_ref, ws_ref, bst_ref, vg_ref, og_ref, ycat_ref, o_ref):
        del ycat_ref
        outs = _gmlp_chunk(*_gmlp_load(u_ref, v_ref, ws_ref, bst_ref, vg_ref, og_ref, layer))
        for h in range(GM_HEADS):
            o_ref[:, h * GM_HEAD_DIM:(h + 1) * GM_HEAD_DIM] = _b(outs[h])

    return pl.pallas_call(
        body, name="gmlp_fwd", grid=(t // CHUNK,),
        in_specs=_gmlp_specs(layer) + [ANY],
        out_specs=pl.BlockSpec((CHUNK, D_GM), lambda i: (i, D_SSD // D_GM)),
        out_shape=jax.ShapeDtypeStruct(ycat.shape, ycat.dtype),
        input_output_aliases={6: 0},
        compiler_params=_cp("parallel"),
    )(proj, proj, ws, bst, vg, og, ycat)


def _gmlp_bwd(dy, proj, ws, bst, vg, og, dproj, *, layer):
    t = proj.shape[0]
    w = GM_HEAD_DIM

    def body(dy_ref, u_ref, v_ref, ws_ref, bst_ref, vg_ref, og_ref, dproj_ref,
             dgm_ref, dws_ref, dbst_ref, dvg_ref, dog_ref):
        del dproj_ref

        @pl.when(pl.program_id(0) == 0)
        def _():
            dws_ref[...] = jnp.zeros_like(dws_ref)
            dbst_ref[...] = jnp.zeros_like(dbst_ref)
            dvg_ref[...] = jnp.zeros_like(dvg_ref)
            dog_ref[...] = jnp.zeros_like(dog_ref)

        _, vjp = jax.vjp(_gmlp_chunk, *_gmlp_load(u_ref, v_ref, ws_ref, bst_ref, vg_ref, og_ref, layer))
        dgu, dgv, dws, dbs, dvg, dog = vjp(_hslices(dy_ref, w, GM_HEADS))
        for h in range(GM_HEADS):
            dgm_ref[:, h * w:(h + 1) * w] = _b(dgu[h])
            dgm_ref[:, D_GM + h * w: D_GM + (h + 1) * w] = _b(dgv[h])
            dws_ref[h] += dws[h]
            dbst_ref[:, h:h + 1] += dbs[h]
            dvg_ref[:, h * w:(h + 1) * w] += dvg[h]
            dog_ref[:, h * w:(h + 1) * w] += dog[h]

    return pl.pallas_call(
        body, name="gmlp_bwd", grid=(t // CHUNK,),
        in_specs=[pl.BlockSpec((CHUNK, D_GM), lambda i: (i, 1))] + _gmlp_specs(layer) + [ANY],
        out_specs=[pl.BlockSpec((CHUNK, 2 * D_GM), lambda i: (i, COL_U // (2 * D_GM))),
                   pl.BlockSpec((GM_HEADS, CHUNK, CHUNK), lambda i: (0, 0, 0)),
                   pl.BlockSpec((CHUNK, GM_HEADS), lambda i: (0, 0)),
                   pl.BlockSpec((1, D_GM), lambda i: (0, 0)),
                   pl.BlockSpec((1, D_GM), lambda i: (0, 0))],
        out_shape=[jax.ShapeDtypeStruct(dproj.shape, dproj.dtype), jax.ShapeDtypeStruct((GM_HEADS, CHUNK, CHUNK), F32),
                   jax.ShapeDtypeStruct((CHUNK, GM_HEADS), F32), jax.ShapeDtypeStruct((1, D_GM), F32),
                   jax.ShapeDtypeStruct((1, D_GM), F32)],
        input_output_aliases={7: 0},
        compiler_params=_cp("arbitrary"),
    )(dy, proj, proj, ws, bst, vg, og, dproj)


def _local_step(x, target, mods, w, final_g, *, nseq, big_w, grad_sink, small_sink):
    saved = []
    x0, delta, gate = x, None, None
    h1 = _normmod_fwd(x, w["norm1_g"], mods[0][1], mods[0][0], nseq=nseq, name="norm1_fwd_0", layer=0)
    for l in range(DEPTH):
        sh1, sc1, g1, sh2, sc2, g2 = mods[l]
        w_in = big_w(l, "w_in", h1)
        proj = _matmul(h1, w_in, tb=True, name=f"mm_in_{l}")
        xbc, xbc_pre = _ssd_conv_fwd(proj, w["ssd_conv_w"], w["ssd_conv_b"], nseq=nseq, layer=l)
        ycat, hprev = _ssd_fwd(xbc, proj, w["ssd_dt_bias"], w["ssd_a_log"], w["ssd_d"], w["ssd_norm_g"], nseq=nseq,
                               layer=l)
        ycat = _gmlp_fwd(proj, ycat, w["gm_ws"], w["gm_bst"], w["gm_vnorm_g"], w["gm_out_g"], layer=l)
        w_out = big_w(l, "w_out", ycat)
        mix, x1, h2 = _matmul_normfwd(ycat, w_out, x0, g1, w["norm2_g"], sc2, sh2, nseq=nseq, name=f"mm_out_{l}",
                                      layer=l)
        ff_up = big_w(l, "ff_up", h2)
        up = _matmul(h2, ff_up, tb=True, name=f"mm_up_{l}", out_dtype=BF16)
        act = _ffn_act_fwd(up, w["ff_conv_w"], w["ff_conv_b"], nseq=nseq, layer=l)
        ff_down = big_w(l, "ff_down", act)
        sv = dict(x0=x0, xin_delta=delta, xin_gate=gate, h1=h1, proj=proj, xbc=xbc, xbc_pre=xbc_pre, hprev=hprev,
                  ycat=ycat, mix=mix, x1=x1, h2=h2, up=up, act=act,
                  w_in=w_in, w_out=w_out, ff_up=ff_up, ff_down=ff_down)
        if l + 1 < DEPTH:
            nsh1, nsc1 = mods[l + 1][0], mods[l + 1][1]
            dn, x0, h1 = _matmul_normfwd(act, ff_down, x1, g2, w["norm1_g"], nsc1, nsh1, nseq=nseq,
                                         name=f"mm_down_{l}", layer=l + 1)
        else:
            dn, loss, dx, ddelta, dgate, dfg = _matmul_loss(act, ff_down, x1, g2, final_g, target, nseq=nseq,
                                                            name=f"mm_down_{l}")
        saved.append(dict(sv, dn=dn))
        delta, gate = dn, g2

    small, dmods = [None] * DEPTH, [None] * DEPTH
    for l in reversed(range(DEPTH)):
        sv = saved[l]
        sh1, sc1, g1, sh2, sc2, g2 = mods[l]
        dg2 = dgate
        g_ff_down = _matmul(sv["act"], ddelta, ta=True, name=f"mm_down_dw_{l}", out_dtype=BF16)
        dact = _matmul(ddelta, sv["ff_down"], tb=True, name=f"mm_down_dx_{l}", out_dtype=BF16)
        dgate_ff, dval_ff, dfcw, dfcb = _ffn_act_bwd(dact, sv["up"], w["ff_conv_w"], w["ff_conv_b"], nseq=nseq, layer=l)
        g_ff_up = _matmul([dgate_ff, dval_ff], sv["h2"], ta=True, name=f"mm_up_dw_{l}", out_dtype=BF16)
        dep = grad_sink(l, "ffn", dict(ff_down=g_ff_down, ff_up=g_ff_up), dval_ff)
        dx, dmix, dg1, dn2g, dsc2, dsh2 = _matmul_normbwd([dgate_ff, dval_ff], sv["ff_up"], dx, sv["x1"], sv["mix"], g1,
                                                          w["norm2_g"], sc2, nseq=nseq, name=f"mm_up_dx_{l}", layer=l,
                                                          dep=dep)
        g_w_out = _matmul(sv["ycat"], dmix, ta=True, name=f"mm_out_dw_{l}", out_dtype=BF16)
        dep = grad_sink(l, "w_out", dict(w_out=g_w_out), dmix)
        dycat = _matmul(dmix, sv["w_out"], tb=True, name=f"mm_out_dx_{l}", dep=dep)
        dxbc_act, dproj, ddtb, dalog, ddsk, dng = _ssd_bwd(dycat, sv["xbc"], sv["proj"], sv["hprev"], w["ssd_dt_bias"],
                                                          w["ssd_a_log"], w["ssd_d"], w["ssd_norm_g"], nseq=nseq, layer=l)
        dproj, dscw, dscb = _ssd_conv_bwd(dxbc_act, sv["xbc_pre"], sv["proj"], w["ssd_conv_w"], dproj, nseq=nseq,
                                          layer=l)
        dproj, dws, dbst, dvg, dog = _gmlp_bwd(dycat, sv["proj"], w["gm_ws"], w["gm_bst"], w["gm_vnorm_g"], w["gm_out_g"],
                                               dproj, layer=l)
        early = dict(norm2_g=dn2g, ssd_norm_g=dng, gm_vnorm_g=dvg, gm_out_g=dog,
                     ssd_conv_w=dscw, ssd_conv_b=dscb, ff_conv_w=dfcw, ff_conv_b=dfcb,
                     ssd_dt_bias=ddtb, ssd_a_log=dalog, ssd_d=ddsk, gm_ws=dws, gm_bs=dbst.T)
        dep = small_sink(l, early, small, dmods, dfg, loss)
        g_w_in = _matmul(dproj, sv["h1"], ta=True, name=f"mm_in_dw_{l}", out_dtype=BF16, dep=dep)
        dep = grad_sink(l, "w_in", dict(w_in=g_w_in), dproj)
        dx, ddelta, dgate, dn1g, dsc1, dsh1 = _matmul_normbwd(dproj, sv["w_in"], dx, sv["x0"], sv["xin_delta"],
                                                              sv["xin_gate"], w["norm1_g"], sc1, nseq=nseq,
                                                              name=f"mm_in_dx_{l}", layer=l, dep=dep)
        small[l] = dict(early, norm1_g=dn1g)
        dmods[l] = jnp.concatenate([dsh1, dsc1, dg1, dsh2, dsc2, dg2], axis=-1)[:, 0, :]
    return dx, small, dmods


def _all_gather(arrs, name, dep=None):
    n = len(arrs)
    extra = [] if dep is None else [dep]

    def body(*refs):
        ins, outs = refs[:n], refs[n + len(extra):2 * n + len(extra)]
        send_sems, recv_sems, local_sems = refs[2 * n + len(extra):]
        x, y, c = lax.axis_index("x"), lax.axis_index("y"), lax.axis_index("c")
        me, sibling = (x, y, c), (x, y, 1 - c)
        chips = [(1 - x, y), (x, 1 - y), (1 - x, 1 - y)]

        def copy(i, k, block, to, src=None):
            px, py, pc = block
            dst = outs[i].at[4 * px + 2 * py + pc]
            return pltpu.make_async_remote_copy(
                src_ref=dst if src is None else src, dst_ref=dst,
                send_sem=send_sems.at[7 * i + k], recv_sem=recv_sems.at[7 * i + k],
                device_id=to, device_id_type=MESH)

        mine = [pltpu.make_async_copy(ins[i], outs[i].at[4 * x + 2 * y + c], local_sems.at[i]) for i in range(n)]
        for cp in mine:
            cp.start()
        first = []
        for i in range(n):
            first.append(copy(i, 0, me, sibling, src=ins[i]))
            first += [copy(i, 1 + j, me, (*chip, c), src=ins[i]) for j, chip in enumerate(chips)]
        for cp in first:
            cp.start()
        passed = []
        for j, chip in enumerate(chips):
            for i in range(n):
                copy(i, 1 + j, (*chip, c), me).wait_recv()
                fwd = copy(i, 4 + j, (*chip, c), sibling)
                fwd.start()
                passed.append(fwd)
        for i in range(n):
            copy(i, 0, sibling, me).wait_recv()
            for j, chip in enumerate(chips):
                copy(i, 4 + j, (*chip, 1 - c), me).wait_recv()
        for cp in first + passed:
            cp.wait_send()
        for cp in mine:
            cp.wait()

    return pl.pallas_call(
        body, name=name,
        in_specs=[ANY] * (n + len(extra)), out_specs=[ANY] * n,
        out_shape=[jax.ShapeDtypeStruct((N_DEV,) + a.shape, a.dtype) for a in arrs],
        scratch_shapes=[pltpu.SemaphoreType.DMA((7 * n,)), pltpu.SemaphoreType.DMA((7 * n,)),
                        pltpu.SemaphoreType.DMA((n,))],
    )(*arrs, *extra)


HBM = pl.BlockSpec(memory_space=pltpu.HBM)
SEM = pl.BlockSpec(memory_space=pltpu.SEMAPHORE)
EFFECT = pltpu.SideEffectType.DATAFLOW_SIDE_EFFECTING


def _peer(k):
    x, y, c = lax.axis_index("x"), lax.axis_index("y"), lax.axis_index("c")
    return (1 - x if k & 4 else x, 1 - y if k & 2 else y, 1 - c if k & 1 else c)


ALL_PEERS = tuple(range(1, N_DEV))
OTHER_CHIPS = (2, 4, 6)


def _xc_copies(scatter, srcs, lands, send_sems, recv_sems, peers=ALL_PEERS):
    x, y, c = lax.axis_index("x"), lax.axis_index("y"), lax.axis_index("c")
    copies = []
    for i in range(len(srcs)):
        for k in (peers[i] if isinstance(peers[0], tuple) else peers):
            px, py, pc = _peer(k)
            src = srcs[i].at[4 * px + 2 * py + pc] if scatter else srcs[i]
            dst = lands[i].at[k - 1] if scatter else lands[i].at[4 * x + 2 * y + c]
            copies.append(pltpu.make_async_remote_copy(
                src_ref=src, dst_ref=dst, send_sem=send_sems[i].at[k - 1], recv_sem=recv_sems[i].at[k - 1],
                device_id=(px, py, pc), device_id_type=MESH))
    return copies


def _xc_own(scatter, srcs, lands, send_sems):
    if scatter:
        return []
    me = 4 * lax.axis_index("x") + 2 * lax.axis_index("y") + lax.axis_index("c")
    return [pltpu.make_async_copy(srcs[i], lands[i].at[me], send_sems[i].at[N_DEV - 1]) for i in range(len(srcs))]


def _xc_start(scatter, arrs, after, name, peers=ALL_PEERS):
    n = len(arrs)
    lands = [lax.empty((N_DEV - 1,) + a.shape[1:] if scatter else (N_DEV,) + a.shape, a.dtype) for a in arrs]

    def body(*refs):
        srcs, lnd = refs[:n], refs[n:2 * n]
        send_sems, recv_sems = refs[2 * n + 1:3 * n + 1], refs[3 * n + 1:4 * n + 1]
        token = refs[6 * n + 1]
        for cp in _xc_copies(scatter, srcs, lnd, send_sems, recv_sems, peers) + _xc_own(scatter, srcs, lnd, send_sems):
            cp.start()
        token[...] = jnp.zeros_like(token)

    outs = pl.pallas_call(
        body, name=name,
        out_shape=[pltpu.SemaphoreType.DMA((N_DEV,))] * (2 * n)
        + [pltpu.HBM(a.shape, a.dtype) for a in arrs] + [pltpu.HBM(a.shape, a.dtype) for a in lands]
        + [jax.ShapeDtypeStruct((8, 128), F32)],
        in_specs=[HBM] * (2 * n) + [ANY],
        out_specs=[SEM] * (2 * n) + [HBM] * (2 * n) + [pl.BlockSpec(memory_space=pltpu.VMEM)],
        input_output_aliases={i: 2 * n + i for i in range(2 * n)},
        compiler_params=pltpu.CompilerParams(has_side_effects=EFFECT),
    )(*[pltpu.with_memory_space_constraint(a, pltpu.HBM) for a in list(arrs) + lands], after)
    return outs[:n], outs[n:2 * n], outs[2 * n:3 * n], outs[3 * n:4 * n], outs[4 * n][0, 0]


def _xc_wait(scatter, send_sems, recv_sems, srcs, lands, after, name, peers=ALL_PEERS):
    n = len(srcs)

    def body(*refs):
        s_refs, l_refs = refs[:n], refs[n:2 * n]
        ss, rs = refs[2 * n:3 * n], refs[3 * n:4 * n]
        for cp in _xc_copies(scatter, s_refs, l_refs, ss, rs, peers):
            cp.wait_send()
            cp.wait_recv()
        for cp in _xc_own(scatter, s_refs, l_refs, ss):
            cp.wait()

    outs = pl.pallas_call(
        body, name=name,
        out_shape=[pltpu.HBM(a.shape, a.dtype) for a in list(srcs) + list(lands)],
        in_specs=[HBM] * (2 * n) + [SEM] * (2 * n) + [ANY],
        out_specs=[HBM] * (2 * n),
        input_output_aliases={i: i for i in range(2 * n)},
        compiler_params=pltpu.CompilerParams(has_side_effects=EFFECT),
    )(*srcs, *lands, *send_sems, *recv_sems, after)
    return outs[:n], outs[n:]


def _sib_copies(zones, send_sems, recv_sems):
    x, y, c = lax.axis_index("x"), lax.axis_index("y"), lax.axis_index("c")
    copies = []
    for i in range(len(zones)):
        for q in range(N_DEV // 2):
            slot = zones[i].at[2 * q + c]
            copies.append(pltpu.make_async_remote_copy(
                src_ref=slot, dst_ref=slot, send_sem=send_sems[i].at[q], recv_sem=recv_sems[i].at[q],
                device_id=(x, y, 1 - c), device_id_type=MESH))
    return copies


def _sib_start(zones, name):
    n = len(zones)

    def body(*refs):
        for cp in _sib_copies(refs[:n], refs[n:2 * n], refs[2 * n:3 * n]):
            cp.start()

    outs = pl.pallas_call(
        body, name=name,
        out_shape=[pltpu.SemaphoreType.DMA((N_DEV // 2,))] * (2 * n) + [pltpu.HBM(a.shape, a.dtype) for a in zones],
        in_specs=[HBM] * n,
        out_specs=[SEM] * (2 * n) + [HBM] * n,
        input_output_aliases={i: 2 * n + i for i in range(n)},
        compiler_params=pltpu.CompilerParams(has_side_effects=EFFECT),
    )(*[pltpu.with_memory_space_constraint(a, pltpu.HBM) for a in zones])
    return outs[:n], outs[n:2 * n], outs[2 * n:]


def _sib_wait(send_sems, recv_sems, zones, name):
    n = len(zones)

    def body(*refs):
        for cp in _sib_copies(refs[:n], refs[n:2 * n], refs[2 * n:3 * n]):
            cp.wait_send()
            cp.wait_recv()

    return pl.pallas_call(
        body, name=name,
        out_shape=[pltpu.HBM(a.shape, a.dtype) for a in zones],
        in_specs=[HBM] * n + [SEM] * (2 * n),
        out_specs=[HBM] * n,
        input_output_aliases={i: i for i in range(n)},
        compiler_params=pltpu.CompilerParams(has_side_effects=EFFECT),
    )(*zones, *send_sems, *recv_sems)


def _adamw_math(w, g, m, v):
    m = ADAM_B1 * m + (1.0 - ADAM_B1) * g
    v = ADAM_B2 * v + (1.0 - ADAM_B2) * (g * g)
    m_hat = m / (1.0 - ADAM_B1 ** ADAM_STEP)
    v_hat = v / (1.0 - ADAM_B2 ** ADAM_STEP)
    delta = -ADAM_LR * (m_hat / (jnp.sqrt(v_hat) + ADAM_EPS) + ADAM_WD * w)
    return delta, m, v


def _adamw_sharded(parts, w, m, v, pos, name):
    depth, rows, cols = w.shape
    tr = _tile(rows, 256) if rows % 8 == 0 else rows
    npart = len(parts)

    def body(pos_ref, *refs):
        prefs = refs[:npart]
        w_ref, m_ref, v_ref, g_out, d_out, m_out, v_out = refs[npart:]
        g = prefs[0][...]
        for pr in prefs[1:]:
            g = g + pr[...]
        delta, mn, vn = _adamw_math(w_ref[...], g, m_ref[...], v_ref[...])
        g_out[...] = g
        d_out[...] = delta
        m_out[...] = mn
        v_out[...] = vn

    def part_spec(fn):
        return pl.BlockSpec((1, tr, cols), lambda l, i, p: (fn(p) * depth + l, i, 0))

    blk = pl.BlockSpec((1, tr, cols), lambda l, i, p: (l, i, 0))
    shp = jax.ShapeDtypeStruct((depth, rows, cols), F32)
    return pl.pallas_call(
        body, name=name,
        grid_spec=pltpu.PrefetchScalarGridSpec(
            num_scalar_prefetch=1, grid=(depth, rows // tr),
            in_specs=[part_spec(fn) for _, fn in parts] + [blk, blk, blk],
            out_specs=[blk, blk, blk, blk]),
        out_shape=[shp, shp, shp, shp],
        compiler_params=_cp("parallel", "parallel"),
    )(pos, *[a for a, _ in parts], w, m, v)


def _adamw_layer(parts, w, m, v, pos, layer, prev, name):
    depth, rows, cols = w.shape
    npart = len(parts)
    nprev = 0 if prev is None else 4
    if rows % 16 == 0:
        tr, tc = max(t for t in range(16, 257, 16) if rows % t == 0), cols
    else:
        tr, tc = rows, _tile(cols, 256)
    pick = (lambda i: (i, 0)) if rows % 16 == 0 else (lambda i: (0, i))

    def body(pos_ref, *refs):
        prefs = refs[:npart]
        w_ref, m_ref, v_ref = refs[npart:npart + 3]
        g_out, d_out, m_out, v_out = refs[npart + 3 + nprev:]
        g = prefs[0][...].astype(F32)
        for pr in prefs[1:]:
            g = g + pr[...].astype(F32)
        delta, mn, vn = _adamw_math(w_ref[...], g, m_ref[...], v_ref[...])
        g_out[...] = g
        d_out[...] = delta
        m_out[...] = mn
        v_out[...] = vn

    def part_spec(fn):
        return pl.BlockSpec((1, tr, tc), lambda i, p: (fn(p), *pick(i)))

    blk = pl.BlockSpec((1, tr, tc), lambda i, p: (layer, *pick(i)))
    shp = jax.ShapeDtypeStruct((depth, rows, cols), F32)
    first_prev = 1 + npart + 3
    return pl.pallas_call(
        body, name=name,
        grid_spec=pltpu.PrefetchScalarGridSpec(
            num_scalar_prefetch=1, grid=(rows // tr * (cols // tc),),
            in_specs=[part_spec(fn) for _, fn in parts] + [blk, blk, blk] + [ANY] * nprev,
            out_specs=[blk, blk, blk, blk]),
        out_shape=[shp, shp, shp, shp],
        input_output_aliases={first_prev + j: j for j in range(nprev)},
        compiler_params=_cp("parallel"),
    )(pos, *[a for a, _ in parts], w, m, v, *(prev or ()))


def _adamw_rows_major(parts_by_layer, w, m, v, pos, name):
    rows, depth, cols = w.shape
    tc = _tile(cols, 256)
    npart = len(parts_by_layer[0])

    def body(pos_ref, *refs):
        prefs = refs[:depth * npart]
        w_ref, m_ref, v_ref, g_out, d_out, m_out, v_out = refs[depth * npart:]
        for l in range(depth):
            g = prefs[l * npart][0].astype(F32)
            for pr in prefs[l * npart + 1:(l + 1) * npart]:
                g = g + pr[0].astype(F32)
            delta, mn, vn = _adamw_math(w_ref[:, l, :], g, m_ref[:, l, :], v_ref[:, l, :])
            g_out[:, l, :] = g
            d_out[:, l, :] = delta
            m_out[:, l, :] = mn
            v_out[:, l, :] = vn

    def part_spec(fn):
        return pl.BlockSpec((1, rows, tc), lambda j, p: (fn(p), 0, j))

    blk = pl.BlockSpec((rows, depth, tc), lambda j, p: (0, 0, j))
    shp = jax.ShapeDtypeStruct(w.shape, F32)
    flat = [pf for parts in parts_by_layer for pf in parts]
    return pl.pallas_call(
        body, name=name,
        grid_spec=pltpu.PrefetchScalarGridSpec(
            num_scalar_prefetch=1, grid=(cols // tc,),
            in_specs=[part_spec(fn) for _, fn in flat] + [blk, blk, blk],
            out_specs=[blk, blk, blk, blk]),
        out_shape=[shp, shp, shp, shp],
        compiler_params=_cp("parallel"),
    )(pos, *[a for a, _ in flat], w, m, v)


_P1024 = ["norm1_g", "norm2_g", "ssd_norm_g", "gm_vnorm_g", "gm_out_g"]
_P16 = ["ssd_dt_bias", "ssd_a_log", "ssd_d"]


def _adamw_small(gath, wmv):
    names = list(wmv.keys())
    classes = list(gath.keys())
    flat_in = [gath[k] for k in classes]
    for nme in names:
        flat_in += list(wmv[nme])
    out_shapes = []
    for nme in names:
        out_shapes += [jax.ShapeDtypeStruct(wmv[nme][0].shape, F32)] * 4
    out_shapes += [jax.ShapeDtypeStruct((DEPTH, SSD_CONV, CONV_DIM), F32), jax.ShapeDtypeStruct((DEPTH, FF_CONV, D_FF), F32),
                   jax.ShapeDtypeStruct((1, SSD_HEADS), F32)]
    scratch = [pltpu.VMEM(gath[k].shape[1:], F32) for k in classes]
    ncls = len(classes)

    def body(*refs):
        g_refs = dict(zip(classes, refs[:ncls]))
        pos = ncls
        w_refs = {}
        for nme in names:
            w_refs[nme] = refs[pos:pos + 3]
            pos += 3
        o_refs = {}
        for nme in names:
            o_refs[nme] = refs[pos:pos + 4]
            pos += 4
        scw_out, fcw_out, loss_out = refs[pos], refs[pos + 1], refs[pos + 2]
        s_refs = dict(zip(classes, refs[pos + 3:]))
        for k in classes:
            acc = g_refs[k][0]
            for dev in range(1, N_DEV):
                acc = acc + g_refs[k][dev]
            s_refs[k][...] = acc

        def apply(nme, grad_of):
            w_ref, m_ref, v_ref = w_refs[nme]
            g_out, d_out, m_out, v_out = o_refs[nme]
            shape = w_ref.shape
            if len(shape) == 2:
                idxs = [(slice(l, l + 1),) for l in range(shape[0])]
            elif len(shape) == 3:
                idxs = [(l,) for l in range(shape[0])]
            else:
                idxs = [(l, h) for l in range(shape[0]) for h in range(shape[1])]
            for n_i, ix in enumerate(idxs):
                g = grad_of(n_i)
                delta, mn, vn = _adamw_math(w_ref[ix], g, m_ref[ix], v_ref[ix])
                g_out[ix] = g
                d_out[ix] = delta
                m_out[ix] = mn
                v_out[ix] = vn

        s1024, s1536, s2816, s16, s128, s6144, late1024, late6144 = (s_refs[k] for k in classes)
        s1024[0:1, :] += late1024[...]
        s6144[0:late6144.shape[0], :] += late6144[...]
        for n_i, nme in enumerate(_P1024):
            apply(nme, lambda l, b=2 * n_i: s1024[b + l:b + l + 1, :])
        apply("final_g", lambda l: s1024[10:11, :])
        apply("ssd_conv_b", lambda l: s1536[8 + l:9 + l, :])
        apply("ff_conv_b", lambda l: s2816[6 + l:7 + l, :])
        for n_i, nme in enumerate(_P16):
            apply(nme, lambda l, b=2 * n_i: s16[b + l:b + l + 1, :])
        apply("gm_ws", lambda q: s128[q * CHUNK:(q + 1) * CHUNK, :])
        apply("gm_bs", lambda l: s128[2048 + 8 * l:2048 + 8 * (l + 1), :])
        apply("ada_b", lambda l: s6144[2 * l:2 * l + 1, :] + s6144[2 * l + 1:2 * l + 2, :])
        for l in range(DEPTH):
            scw_out[l] = s1536[SSD_CONV * l:SSD_CONV * (l + 1), :]
            fcw_out[l] = s2816[FF_CONV * l:FF_CONV * (l + 1), :]
        loss_out[...] = s16[2 * len(_P16):2 * len(_P16) + 1, :]

    outs = pl.pallas_call(
        body, name="adamw_small",
        out_shape=out_shapes,
        scratch_shapes=scratch,
        compiler_params=pltpu.CompilerParams(vmem_limit_bytes=VMEM_LIMIT),
    )(*flat_in)
    res = {nme: tuple(outs[4 * i:4 * i + 4]) for i, nme in enumerate(names)}
    return res, outs[-3], outs[-2], outs[-1]


_WEIGHTS = ['ada_w', 'ada_b', 'norm1_g', 'norm2_g', 'w_in', 'ssd_conv_w', 'ssd_conv_b', 'ssd_dt_bias', 'ssd_a_log',
            'ssd_d', 'ssd_norm_g', 'gm_vnorm_g', 'gm_ws', 'gm_bs', 'gm_out_g', 'w_out', 'ff_up', 'ff_conv_w',
            'ff_conv_b', 'ff_down', 'final_g']


_O_XBC, _O_DT, _O_GM = D_SSD, D_SSD + CONV_DIM, D_SSD + CONV_DIM + SSD_HEADS


_TRANSPOSED = ("w_in", "ff_up")


def _full_weight(name, g):
    if name != "w_in":
        return g.reshape(g.shape[0] * g.shape[1], g.shape[2])
    rows = g.shape[1]
    k, r = divmod(_O_GM, rows)
    at = lambda j: COL_Z + j * rows if j <= k else j * rows - _O_GM

    step = 512

    def body(g_hbm, o_hbm, g_ref, o_ref, sems, out_sems):
        copies = [pltpu.make_async_copy(g_hbm.at[j], g_ref.at[j], sems.at[j]) for j in range(N_DEV)]
        for cp in copies:
            cp.start()
        outs = []

        def flush(upto):
            while (len(outs) + 1) * step <= upto:
                rows_i = pl.ds(len(outs) * step, step)
                outs.append(pltpu.make_async_copy(o_ref.at[rows_i], o_hbm.at[rows_i], out_sems.at[len(outs)]))
                outs[-1].start()

        copies[k].wait()
        o_ref[0:rows - r, :] = g_ref[k, r:, :]
        flush(rows - r)
        for j in list(range(k + 1, N_DEV)) + list(range(k)):
            copies[j].wait()
            o_ref[at(j):at(j) + rows, :] = g_ref[j]
            flush(at(j) + rows)
        o_ref[at(k):at(k) + r, :] = g_ref[k, :r, :]
        o_ref[N_IN:, :] = jnp.zeros((N_INP - N_IN, g.shape[2]), g.dtype)
        flush(N_INP)
        for cp in outs:
            cp.wait()

    return pl.pallas_call(
        body, name="w_in_rows", in_specs=[ANY], out_specs=ANY,
        out_shape=jax.ShapeDtypeStruct((N_INP, g.shape[2]), g.dtype),
        scratch_shapes=[pltpu.VMEM(g.shape, g.dtype), pltpu.VMEM((N_INP, g.shape[2]), g.dtype),
                        pltpu.SemaphoreType.DMA((N_DEV,)), pltpu.SemaphoreType.DMA((N_INP // step,))],
        compiler_params=pltpu.CompilerParams(vmem_limit_bytes=VMEM_LIMIT),
    )(g)


def _by_owner(name, grad):
    if name != "w_in":
        return grad.reshape(N_DEV, grad.shape[0] // N_DEV, grad.shape[1])
    rows = N_IN // N_DEV
    k, r = divmod(_O_GM, rows)
    at = lambda j: COL_Z + j * rows if j <= k else j * rows - _O_GM

    step = 512
    n_pieces = N_INP // step

    def body(g_hbm, o_hbm, g_ref, o_ref, sems, out_sems):
        copies = [pltpu.make_async_copy(g_hbm.at[pl.ds(i * step, step)], g_ref.at[pl.ds(i * step, step)], sems.at[i])
                  for i in range(n_pieces)]
        for cp in copies:
            cp.start()
        waited = set()
        outs = [pltpu.make_async_copy(o_ref.at[j], o_hbm.at[j], out_sems.at[j]) for j in range(N_DEV)]

        def need(lo, hi):
            for i in range(lo // step, (hi - 1) // step + 1):
                if i not in waited:
                    copies[i].wait()
                    waited.add(i)

        for j in sorted(range(N_DEV), key=at):
            if j == k:
                need(at(k), at(k) + r)
                need(0, rows - r)
                o_ref[k, :r, :] = g_ref[at(k):at(k) + r, :]
                o_ref[k, r:, :] = g_ref[0:rows - r, :]
            else:
                need(at(j), at(j) + rows)
                o_ref[j] = g_ref[at(j):at(j) + rows, :]
            outs[j].start()
        need(0, N_INP)
        for cp in outs:
            cp.wait()

    return pl.pallas_call(
        body, name="w_in_grad_blocks", in_specs=[ANY], out_specs=ANY,
        out_shape=jax.ShapeDtypeStruct((N_DEV, rows, grad.shape[1]), grad.dtype),
        scratch_shapes=[pltpu.VMEM(grad.shape, grad.dtype), pltpu.VMEM((N_DEV, rows, grad.shape[1]), grad.dtype),
                        pltpu.SemaphoreType.DMA((n_pieces,)), pltpu.SemaphoreType.DMA((N_DEV,))],
        compiler_params=pltpu.CompilerParams(vmem_limit_bytes=VMEM_LIMIT),
    )(grad)


def kernel(x, c, ada_w, ada_b, norm1_g, norm2_g, w_in, ssd_conv_w, ssd_conv_b, ssd_dt_bias, ssd_a_log, ssd_d, ssd_norm_g, gm_vnorm_g, gm_ws, gm_bs, gm_out_g, w_out, ff_up, ff_conv_w, ff_conv_b, ff_down, final_g, loss_target, m_ada_w, m_ada_b, m_norm1_g, m_norm2_g, m_w_in, m_ssd_conv_w, m_ssd_conv_b, m_ssd_dt_bias, m_ssd_a_log, m_ssd_d, m_ssd_norm_g, m_gm_vnorm_g, m_gm_ws, m_gm_bs, m_gm_out_g, m_w_out, m_ff_up, m_ff_conv_w, m_ff_conv_b, m_ff_down, m_final_g, v_ada_w, v_ada_b, v_norm1_g, v_norm2_g, v_w_in, v_ssd_conv_w, v_ssd_conv_b, v_ssd_dt_bias, v_ssd_a_log, v_ssd_d, v_ssd_norm_g, v_gm_vnorm_g, v_gm_ws, v_gm_bs, v_gm_out_g, v_w_out, v_ff_up, v_ff_conv_w, v_ff_conv_b, v_ff_down, v_final_g):
    given = dict(locals())
    wts = {n: given[n] for n in _WEIGHTS}
    mom = {n: given["m_" + n] for n in _WEIGHTS}
    var = {n: given["v_" + n] for n in _WEIGHTS}
    nseq, seq, d = x.shape
    ix, iy, ic = lax.axis_index("x"), lax.axis_index("y"), lax.axis_index("c")
    me = 4 * ix + 2 * iy + ic
    me_arr = me.astype(jnp.int32).reshape(1)

    for nme, perm in (("ff_up", (0, 2, 1)), ("w_in", (2, 0, 1))):
        wts[nme], mom[nme], var[nme] = (jnp.transpose(a, perm) for a in (wts[nme], mom[nme], var[nme]))

    def shard(l, name):
        return _b(wts[name][:, l, :] if name == "w_in" else wts[name][l])

    g_scw, g_fcw, c_all = _all_gather([ssd_conv_w, ff_conv_w, c], "gather_first")
    scw_f = jnp.transpose(g_scw, (1, 2, 0, 3)).reshape(DEPTH, SSD_CONV, CONV_DIM)
    fcw_f = jnp.transpose(g_fcw, (1, 2, 0, 3)).reshape(DEPTH, FF_CONV, D_FF)
    c_all = c_all.reshape(N_DEV * nseq, d)

    n_ada = ada_w.shape[2]
    ada_b_shard = lax.dynamic_slice_in_dim(ada_b, me * n_ada, n_ada, axis=1).reshape(DEPTH, 1, n_ada)
    mod_part, c_act = _ada_fwd(c_all, ada_w, ada_b_shard)
    first_ssem, first_rsem, first_src, first_land, first_zero = _xc_start(
        False, [mod_part, shard(0, "w_in")], c_act, "ag_first_start", peers=[ALL_PEERS, OTHER_CHIPS])
    _, (mod_g,) = _xc_wait(False, first_ssem[:1], first_rsem[:1], first_src[:1], first_land[:1], c_act,
                           "mod_wait")
    mod_all = jnp.transpose(mod_g, (1, 2, 0, 3)).reshape(DEPTH, N_DEV * nseq, N_MOD * d)
    mod_mine = lax.dynamic_slice_in_dim(mod_all, me * nseq, nseq, axis=1)
    mod_k = jnp.transpose(mod_mine.reshape(DEPTH, nseq, N_MOD, 1, d), (0, 2, 1, 3, 4))
    mods = [[mod_k[l, k] for k in range(N_MOD)] for l in range(DEPTH)]

    later =[(0, "w_out"), (0, "ff_up"), (0, "ff_down"), (1, "w_in"), (1, "w_out"), (1, "ff_up"), (1, "ff_down")]
    ag_groups = {(0, "w_out"): [0], (0, "ff_up"): [1, 2], (1, "w_in"): [3, 4], (1, "ff_up"): [5, 6]}
    big_cache, ag = {}, {}

    def big_w(l, name, after):
        if (l, name) == (0, "w_in") and (l, name) not in big_cache:
            ag["ssem"], ag["rsem"], ag["src"], ag["land"], started = _xc_start(
                False, [shard(l2, n2) for l2, n2 in later], after, "ag_start")
            _, zones = _xc_wait(False, first_ssem[1:], first_rsem[1:], first_src[1:], first_land[1:],
                                jnp.full((8, 128), started, F32), "ag_first_wait", peers=OTHER_CHIPS)
            (zone,) = _sib_wait(*_sib_start(zones, "ag_first_sib_start"), "ag_first_sib_wait")
            big_cache[(l, name)] = _full_weight(name, zone)
        if (l, name) not in big_cache:
            idx = ag_groups[(l, name)]
            pick = lambda seq_: [seq_[i] for i in idx]
            _, lands = _xc_wait(False, pick(ag["ssem"]), pick(ag["rsem"]), pick(ag["src"]), pick(ag["land"]), after,
                                f"ag_wait_{l}_{name}")
            for i, land in zip(idx, lands):
                big_cache[later[i]] = _full_weight(later[i][1], land)
        return big_cache[(l, name)]

    small_w = dict(
        norm1_g=norm1_g + first_zero, norm2_g=norm2_g, ssd_conv_w=scw_f, ssd_conv_b=ssd_conv_b, ssd_dt_bias=ssd_dt_bias,
        ssd_a_log=ssd_a_log, ssd_d=ssd_d, ssd_norm_g=ssd_norm_g, gm_vnorm_g=gm_vnorm_g, gm_ws=gm_ws,
        gm_bst=jnp.transpose(gm_bs, (0, 2, 1)), gm_out_g=gm_out_g, ff_conv_w=fcw_f, ff_conv_b=ff_conv_b)

    outs = {}
    pending, win_parts = {}, {}

    def rs_finish(l, group, after):
        names, ssem, rsem, srcs, lands = pending.pop((l, group))
        srcs, lands = _xc_wait(True, ssem, rsem, srcs, lands, after, f"rs_wait_{l}_{group}")
        for nme, own, land in zip(names, srcs, lands):
            parts = [(own, lambda p: p[0])] + [(land, lambda p, k=k: k) for k in range(N_DEV - 1)]
            if nme == "w_in":
                win_parts[l] = parts
                if len(win_parts) == DEPTH:
                    outs[nme] = _adamw_rows_major([win_parts[k] for k in range(DEPTH)], wts[nme], mom[nme], var[nme],
                                                  me_arr, "adamw_w_in")
                continue
            outs[nme] = _adamw_layer(parts, wts[nme], mom[nme], var[nme], me_arr, l, outs.get(nme), f"adamw_{nme}_{l}")
        return land if names[-1] == "w_in" else outs[names[-1]][0]

    def grad_sink(l, group, grads, after):
        names = list(grads)
        ssem, rsem, srcs, lands, zero = _xc_start(True, [_by_owner(n, grads[n]) for n in names], after, f"rs_start_{l}_{group}")
        pending[(l, group)] = (names, ssem, rsem, srcs, lands)
        return zero.reshape(1, 1)

    early_gather = {}

    def small_sink(l, early, small, dmods, dfg, loss_p):
        if l > 0:
            return None
        layers = [dict(early, norm1_g=jnp.zeros((1, d), F32))] + small[1:]
        rows = lambda name: [layers[k][name] for k in range(DEPTH)]
        packed = [
            jnp.concatenate(sum([rows(n) for n in _P1024], []) + [dfg], axis=0),
            jnp.concatenate(rows("ssd_conv_w") + rows("ssd_conv_b"), axis=0),
            jnp.concatenate(rows("ff_conv_w") + rows("ff_conv_b"), axis=0),
            jnp.concatenate(sum([rows(n) for n in _P16], []) + [loss_p[:, :SSD_HEADS]], axis=0),
            jnp.concatenate([layers[k]["gm_ws"].reshape(GM_HEADS * CHUNK, CHUNK) for k in range(DEPTH)] + rows("gm_bs"), axis=0),
            jnp.concatenate([jnp.zeros((nseq, N_MOD * d), F32)] + dmods[1:], axis=0)]
        ssem, rsem, srcs, lands, zero = _xc_start(False, packed, packed[0], "small_start")
        early_gather.update(ssem=ssem, rsem=rsem, srcs=srcs, lands=lands)
        return zero.reshape(1, 1)

    grad_x, small, dmods = _local_step(
        x.reshape(nseq * seq, d), loss_target.reshape(nseq * seq, d), mods, small_w, final_g.reshape(1, d), nseq=nseq,
        big_w=big_w, grad_sink=grad_sink, small_sink=small_sink)

    done = grad_x
    for l, grp in ((1, "ffn"), (1, "w_out"), (1, "w_in"), (0, "ffn"), (0, "w_out")):
        done = rs_finish(l, grp, done)
    _, gathered = _xc_wait(False, early_gather["ssem"], early_gather["rsem"], early_gather["srcs"],
                           early_gather["lands"], done, "small_wait")
    gathered = list(gathered)
    gathered += _all_gather([small[0]["norm1_g"], dmods[0]], "gather_late", dep=gathered[0])
    gath = dict(zip(["p1024", "p1536", "p2816", "p16", "p128", "p6144", "late1024", "late6144"], gathered))

    dmod_all = jnp.concatenate([gath["late6144"].reshape(1, N_DEV * nseq, N_MOD * d),
                                jnp.transpose(gath["p6144"].reshape(N_DEV, DEPTH, nseq, N_MOD * d)[:, 1:], (1, 0, 2, 3)).reshape(
                                    DEPTH - 1, N_DEV * nseq, N_MOD * d)], axis=0)
    small_names = _P1024 + ["final_g", "ssd_conv_b", "ff_conv_b"] + _P16 + ["gm_ws", "gm_bs", "ada_b"]
    wmv = {}
    for nme in small_names:
        if nme == "final_g":
            wmv[nme] = tuple(a.reshape(1, d) for a in (wts[nme], mom[nme], var[nme]))
        else:
            wmv[nme] = (wts[nme], mom[nme], var[nme])
    small_out, scw_full, fcw_full, loss_sum = _adamw_small(gath, wmv)
    loss = loss_sum[0, 0]
    rs_finish(0, "w_in", scw_full)
    for nme in small_names:
        outs[nme] = small_out[nme]
    outs["final_g"] = tuple(a.reshape(d) for a in outs["final_g"])

    n_scw, n_fcw = ssd_conv_w.shape[2], ff_conv_w.shape[2]
    g_scw_mine = lax.dynamic_slice_in_dim(scw_full, me * n_scw, n_scw, axis=2)
    g_fcw_mine = lax.dynamic_slice_in_dim(fcw_full, me * n_fcw, n_fcw, axis=2)
    outs["ssd_conv_w"] = _adamw_sharded([(g_scw_mine, lambda p: 0)], ssd_conv_w, m_ssd_conv_w, v_ssd_conv_w, me_arr, "adamw_ssd_conv_w")
    outs["ff_conv_w"] = _adamw_sharded([(g_fcw_mine, lambda p: 0)], ff_conv_w, m_ff_conv_w, v_ff_conv_w, me_arr, "adamw_ff_conv_w")

    dmod_cols = _b(lax.dynamic_slice_in_dim(dmod_all, me * n_ada, n_ada, axis=2))
    g_ada = jnp.stack([_matmul(c_act, dmod_cols[l], ta=True, name=f"mm_ada_dw_{l}") for l in range(DEPTH)])
    outs["ada_w"] = _adamw_sharded([(g_ada, lambda p: 0)], ada_w, m_ada_w, v_ada_w, me_arr, "adamw_ada_w")

    for nme, perm in (("ff_up", (0, 2, 1)), ("w_in", (1, 2, 0))):
        outs[nme] = tuple(jnp.transpose(a, perm) for a in outs[nme])
    result = [loss, grad_x.reshape(nseq, seq, d)]
    for k in range(4):
        result += [outs[n][k] for n in _WEIGHTS]
    return tuple(result)
```
